```python
import jax, jax.numpy as jnp
from jax import lax
import numpy as np

D_MODEL = 1024
BATCH = 8
SEQ = 4096
DEPTH = 2

N_MIXERS = 2
N_A_LAYERS = (DEPTH + 1) // 2
N_B_LAYERS = DEPTH // 2
CHUNK = 128
A_WIDTH = D_MODEL
A_GROUPS = 16
A_GROUP_DIM = A_WIDTH // A_GROUPS
B_HEADS = 16
B_HEAD_DIM = D_MODEL // B_HEADS
B_PATTERNS = ((128, 1), (512, 4), (2048, 16))
N_PAT = len(B_PATTERNS)
D_FF = 4 * D_MODEL
ALPHA = (2 * DEPTH) ** 0.25
BETA = (8 * DEPTH) ** -0.25
LN_EPS = 1e-5
ADA_SCALE = 0.1
NEG = -1e30

kernel_name = "hybrid_gmlp_dilated_attn_deepnorm_adaln"


def layer_norm(x, g, b):
    xf = x.astype(jnp.float32)
    mu = jnp.mean(xf, axis=-1, keepdims=True)
    var = jnp.mean(jnp.square(xf - mu), axis=-1, keepdims=True)
    y = (xf - mu) * lax.rsqrt(var + LN_EPS) * g.astype(jnp.float32) + b.astype(jnp.float32)
    return y.astype(x.dtype)


def ada_mod(c, w, b):
    m = jax.nn.silu(c) @ w + b
    shift, scale, gate = jnp.split(m, 3, axis=-1)
    return shift[:, None, :], scale[:, None, :], 1.0 + gate[:, None, :]


def alibi_slopes(n_heads):
    h = jnp.arange(1, n_heads + 1, dtype=jnp.float32)
    return jnp.exp2(-8.0 * h / n_heads)


def mixer_a(h, w_in, b_in, vn_g, vn_b, w_s, b_s, w_out):
    B, S, _ = h.shape
    uv = jax.nn.gelu(h @ w_in + b_in)
    u, v = jnp.split(uv, 2, axis=-1)
    v = layer_norm(v, vn_g, vn_b)
    v = v.reshape(B, S // CHUNK, CHUNK, A_GROUPS, A_GROUP_DIM)
    causal = jnp.tril(jnp.ones((CHUNK, CHUNK), dtype=bool))
    w_causal = jnp.where(causal, w_s, 0.0)
    z = jnp.einsum('gts,bnsgc->bntgc', w_causal, v) + b_s.T[:, :, None]
    z = z.reshape(B, S, A_WIDTH)
    return (u * z) @ w_out


def dilated_branch(q, k, v, window, dilation, slopes):
    B, S, H, E = q.shape
    span = window // dilation
    seg = span * dilation
    S_pad = -(-S // seg) * seg
    nb = S_pad // seg
    pad = ((0, 0), (0, S_pad - S), (0, 0), (0, 0))

    def to_blocks(t):
        return jnp.pad(t, pad).reshape(B, nb, span, dilation, H, E)

    def with_prev(t):
        prev = jnp.concatenate([jnp.zeros_like(t[:, :1]), t[:, :-1]], axis=1)
        return jnp.concatenate([prev, t], axis=2)

    qb = to_blocks(q)
    kb = with_prev(to_blocks(k))
    vb = with_prev(to_blocks(v))
    s = jnp.einsum('bnqrhe,bnkrhe->bnrhqk', qb, kb,
                   preferred_element_type=jnp.float32) * (E ** -0.5)
    qi = jnp.arange(span)[:, None]
    ki = jnp.arange(2 * span)[None, :]
    diff = span + qi - ki
    blk = jnp.arange(nb)[:, None, None]
    valid = (diff >= 0) & (diff <= span) & (blk * span + ki - span >= 0)
    bias = -slopes[:, None, None] * (dilation * diff).astype(jnp.float32)
    s = s + bias
    s = jnp.where(valid[None, :, None, None], s, NEG)
    m = jnp.max(s, axis=-1, keepdims=True)
    p = jnp.exp(s - m)
    l = jnp.sum(p, axis=-1, keepdims=True)
    o = jnp.einsum('bnrhqk,bnkrhe->bnqrhe', p / l, vb.astype(jnp.float32))
    lse = (m + jnp.log(l))[..., 0].transpose(0, 1, 4, 2, 3)
    o = o.reshape(B, S_pad, H, E)[:, :S]
    lse = lse.reshape(B, S_pad, H)[:, :S]
    return o, lse


def mixer_b(h, w_qkv, w_out, slopes):
    B, S, _ = h.shape
    qkv = (h @ w_qkv).reshape(B, S, N_PAT, 3, B_HEADS, B_HEAD_DIM)
    outs, lses = [], []
    for g, (window, dilation) in enumerate(B_PATTERNS):
        o, lse = dilated_branch(qkv[:, :, g, 0], qkv[:, :, g, 1], qkv[:, :, g, 2],
                                window, dilation, slopes)
        outs.append(o)
        lses.append(lse)
    wts = jax.nn.softmax(jnp.stack(lses, axis=0), axis=0)[..., None]
    o = jnp.sum(wts * jnp.stack(outs, axis=0), axis=0)
    return o.reshape(B, S, B_HEADS * B_HEAD_DIM).astype(h.dtype) @ w_out


def squared_relu_mlp(h, w_up, w_down):
    return jnp.square(jax.nn.relu(h @ w_up)) @ w_down


def _fwd_setup_inputs(seed: int = 0) -> dict:
    key = jax.random.key(seed)
    ks = jax.random.split(key, 18)
    f32 = jnp.float32
    nrm = lambda k, shape: jax.random.normal(k, shape, dtype=f32)
    return {
        "x": nrm(ks[0], (BATCH, SEQ, D_MODEL)),
        "c": nrm(ks[1], (BATCH, D_MODEL)),
        "ada_w": nrm(ks[2], (DEPTH, 2, D_MODEL, 3 * D_MODEL)) * (D_MODEL ** -0.5) * ADA_SCALE,
        "ada_b": nrm(ks[3], (DEPTH, 2, 3 * D_MODEL)) * 0.01,
        "ln_g": 1.0 + 0.02 * nrm(ks[4], (DEPTH, 2, D_MODEL)),
        "ln_b": 0.02 * nrm(ks[5], (DEPTH, 2, D_MODEL)),
        "a_w_in": nrm(ks[6], (N_A_LAYERS, D_MODEL, 2 * A_WIDTH)) * (D_MODEL ** -0.5),
        "a_b_in": 0.02 * nrm(ks[7], (N_A_LAYERS, 2 * A_WIDTH)),
        "a_vn_g": 1.0 + 0.02 * nrm(ks[8], (N_A_LAYERS, A_WIDTH)),
        "a_vn_b": 0.02 * nrm(ks[9], (N_A_LAYERS, A_WIDTH)),
        "a_w_s": nrm(ks[10], (N_A_LAYERS, A_GROUPS, CHUNK, CHUNK)) * (CHUNK ** -0.5),
        "a_b_s": 1.0 + 0.02 * nrm(ks[11], (N_A_LAYERS, A_GROUPS, CHUNK)),
        "a_w_out": nrm(ks[12], (N_A_LAYERS, A_WIDTH, D_MODEL)) * (A_WIDTH ** -0.5) * BETA,
        "b_w_qkv": nrm(ks[13], (N_B_LAYERS, D_MODEL, N_PAT * 3 * B_HEADS * B_HEAD_DIM)) * (D_MODEL ** -0.5),
        "b_w_out": nrm(ks[14], (N_B_LAYERS, B_HEADS * B_HEAD_DIM, D_MODEL)) * ((B_HEADS * B_HEAD_DIM) ** -0.5) * BETA,
        "mlp_w_up": nrm(ks[15], (DEPTH, D_MODEL, D_FF)) * (D_MODEL ** -0.5),
        "mlp_w_down": nrm(ks[16], (DEPTH, D_FF, D_MODEL)) * (D_FF ** -0.5) * BETA,
    }


def _fwd_reference(x, c, ada_w, ada_b, ln_g, ln_b, a_w_in, a_b_in, a_vn_g, a_vn_b, a_w_s, a_b_s,
              a_w_out, b_w_qkv, b_w_out, mlp_w_up, mlp_w_down):
    slopes = alibi_slopes(B_HEADS)
    for i in range(DEPTH):
        j = i // N_MIXERS
        shift, scale, gate = ada_mod(c, ada_w[i, 0], ada_b[i, 0])
        h = x * (1.0 + scale) + shift
        if i % N_MIXERS == 0:
            y = mixer_a(h, a_w_in[j], a_b_in[j], a_vn_g[j], a_vn_b[j], a_w_s[j], a_b_s[j], a_w_out[j])
        else:
            y = mixer_b(h, b_w_qkv[j], b_w_out[j], slopes)
        x = layer_norm(ALPHA * x + gate * y, ln_g[i, 0], ln_b[i, 0])
        shift, scale, gate = ada_mod(c, ada_w[i, 1], ada_b[i, 1])
        h = x * (1.0 + scale) + shift
        y = squared_relu_mlp(h, mlp_w_up[i], mlp_w_down[i])
        x = layer_norm(ALPHA * x + gate * y, ln_g[i, 1], ln_b[i, 1])
    return x


import jax as _jax
import jax.numpy as _jnp

TWIN_FORMAT = 'train_step'
FWD_PARAMS = ['x', 'c', 'ada_w', 'ada_b', 'ln_g', 'ln_b', 'a_w_in', 'a_b_in', 'a_vn_g', 'a_vn_b', 'a_w_s', 'a_b_s', 'a_w_out', 'b_w_qkv', 'b_w_out', 'mlp_w_up', 'mlp_w_down']
TWIN_WEIGHTS = ['ada_w', 'ada_b', 'ln_g', 'ln_b', 'a_w_in', 'a_b_in', 'a_vn_g', 'a_vn_b', 'a_w_s', 'a_b_s', 'a_w_out', 'b_w_qkv', 'b_w_out', 'mlp_w_up', 'mlp_w_down']
TWIN_DIFF_INPUT = 'x'
TWIN_INPUTS = ['x', 'c', 'ada_w', 'ada_b', 'ln_g', 'ln_b', 'a_w_in', 'a_b_in', 'a_vn_g', 'a_vn_b', 'a_w_s', 'a_b_s', 'a_w_out', 'b_w_qkv', 'b_w_out', 'mlp_w_up', 'mlp_w_down', 'loss_target', 'm_ada_w', 'm_ada_b', 'm_ln_g', 'm_ln_b', 'm_a_w_in', 'm_a_b_in', 'm_a_vn_g', 'm_a_vn_b', 'm_a_w_s', 'm_a_b_s', 'm_a_w_out', 'm_b_w_qkv', 'm_b_w_out', 'm_mlp_w_up', 'm_mlp_w_down', 'v_ada_w', 'v_ada_b', 'v_ln_g', 'v_ln_b', 'v_a_w_in', 'v_a_b_in', 'v_a_vn_g', 'v_a_vn_b', 'v_a_w_s', 'v_a_b_s', 'v_a_w_out', 'v_b_w_qkv', 'v_b_w_out', 'v_mlp_w_up', 'v_mlp_w_down']
TWIN_OUTPUTS = ['loss', 'grad_x', 'grad_ada_w', 'grad_ada_b', 'grad_ln_g', 'grad_ln_b', 'grad_a_w_in', 'grad_a_b_in', 'grad_a_vn_g', 'grad_a_vn_b', 'grad_a_w_s', 'grad_a_b_s', 'grad_a_w_out', 'grad_b_w_qkv', 'grad_b_w_out', 'grad_mlp_w_up', 'grad_mlp_w_down', 'delta_ada_w', 'delta_ada_b', 'delta_ln_g', 'delta_ln_b', 'delta_a_w_in', 'delta_a_b_in', 'delta_a_vn_g', 'delta_a_vn_b', 'delta_a_w_s', 'delta_a_b_s', 'delta_a_w_out', 'delta_b_w_qkv', 'delta_b_w_out', 'delta_mlp_w_up', 'delta_mlp_w_down', 'new_m_ada_w', 'new_m_ada_b', 'new_m_ln_g', 'new_m_ln_b', 'new_m_a_w_in', 'new_m_a_b_in', 'new_m_a_vn_g', 'new_m_a_vn_b', 'new_m_a_w_s', 'new_m_a_b_s', 'new_m_a_w_out', 'new_m_b_w_qkv', 'new_m_b_w_out', 'new_m_mlp_w_up', 'new_m_mlp_w_down', 'new_v_ada_w', 'new_v_ada_b', 'new_v_ln_g', 'new_v_ln_b', 'new_v_a_w_in', 'new_v_a_b_in', 'new_v_a_vn_g', 'new_v_a_vn_b', 'new_v_a_w_s', 'new_v_a_b_s', 'new_v_a_w_out', 'new_v_b_w_qkv', 'new_v_b_w_out', 'new_v_mlp_w_up', 'new_v_mlp_w_down']
TWIN_LEAF_KINDS = {'loss': 'loss', 'grad_x': 'grad_x', 'grad_ada_w': 'grad_w', 'grad_ada_b': 'grad_w', 'grad_ln_g': 'grad_w', 'grad_ln_b': 'grad_w', 'grad_a_w_in': 'grad_w', 'grad_a_b_in': 'grad_w', 'grad_a_vn_g': 'grad_w', 'grad_a_vn_b': 'grad_w', 'grad_a_w_s': 'grad_w', 'grad_a_b_s': 'grad_w', 'grad_a_w_out': 'grad_w', 'grad_b_w_qkv': 'grad_w', 'grad_b_w_out': 'grad_w', 'grad_mlp_w_up': 'grad_w', 'grad_mlp_w_down': 'grad_w', 'delta_ada_w': 'delta_w', 'delta_ada_b': 'delta_w', 'delta_ln_g': 'delta_w', 'delta_ln_b': 'delta_w', 'delta_a_w_in': 'delta_w', 'delta_a_b_in': 'delta_w', 'delta_a_vn_g': 'delta_w', 'delta_a_vn_b': 'delta_w', 'delta_a_w_s': 'delta_w', 'delta_a_b_s': 'delta_w', 'delta_a_w_out': 'delta_w', 'delta_b_w_qkv': 'delta_w', 'delta_b_w_out': 'delta_w', 'delta_mlp_w_up': 'delta_w', 'delta_mlp_w_down': 'delta_w', 'new_m_ada_w': 'new_m', 'new_m_ada_b': 'new_m', 'new_m_ln_g': 'new_m', 'new_m_ln_b': 'new_m', 'new_m_a_w_in': 'new_m', 'new_m_a_b_in': 'new_m', 'new_m_a_vn_g': 'new_m', 'new_m_a_vn_b': 'new_m', 'new_m_a_w_s': 'new_m', 'new_m_a_b_s': 'new_m', 'new_m_a_w_out': 'new_m', 'new_m_b_w_qkv': 'new_m', 'new_m_b_w_out': 'new_m', 'new_m_mlp_w_up': 'new_m', 'new_m_mlp_w_down': 'new_m', 'new_v_ada_w': 'new_v', 'new_v_ada_b': 'new_v', 'new_v_ln_g': 'new_v', 'new_v_ln_b': 'new_v', 'new_v_a_w_in': 'new_v', 'new_v_a_b_in': 'new_v', 'new_v_a_vn_g': 'new_v', 'new_v_a_vn_b': 'new_v', 'new_v_a_w_s': 'new_v', 'new_v_a_b_s': 'new_v', 'new_v_a_w_out': 'new_v', 'new_v_b_w_qkv': 'new_v', 'new_v_b_w_out': 'new_v', 'new_v_mlp_w_up': 'new_v', 'new_v_mlp_w_down': 'new_v'}


def _forward(args):
    return _fwd_reference(*[args[k] for k in FWD_PARAMS])


def _output_shape():
    def fwd():
        inp = _fwd_setup_inputs(0)
        return _fwd_reference(*[inp[k] for k in FWD_PARAMS])
    out = _jax.eval_shape(fwd)
    return out.shape, out.dtype

N_MICROBATCH = 1
ADAM_LR = 0.001
ADAM_B1 = 0.9
ADAM_B2 = 0.999
ADAM_EPS = 1e-08
ADAM_WD = 0.01
ADAM_STEP = 10
PER_EXAMPLE_BATCH_AXIS = {'x': 0, 'c': 0, 'loss_target': 0}
SHARED_INPUTS = []
_WEIGHT_DTYPES = {'ada_w': _jnp.float32, 'ada_b': _jnp.float32, 'ln_g': _jnp.float32, 'ln_b': _jnp.float32, 'a_w_in': _jnp.float32, 'a_b_in': _jnp.float32, 'a_vn_g': _jnp.float32, 'a_vn_b': _jnp.float32, 'a_w_s': _jnp.float32, 'a_b_s': _jnp.float32, 'a_w_out': _jnp.float32, 'b_w_qkv': _jnp.float32, 'b_w_out': _jnp.float32, 'mlp_w_up': _jnp.float32, 'mlp_w_down': _jnp.float32}
MOMENT_SCALE = {'ada_w': 4.862873e-02, 'ada_b': 1.254709e-01, 'ln_g': 1.609588e+01, 'ln_b': 3.995375e+00, 'a_w_in': 4.463522e-02, 'a_b_in': 7.763670e-02, 'a_vn_g': 3.091485e-02, 'a_vn_b': 3.072188e-02, 'a_w_s': 2.085958e-02, 'a_b_s': 2.964704e-02, 'a_w_out': 1.399472e-01, 'b_w_qkv': 1.245381e-02, 'b_w_out': 6.292154e-02, 'mlp_w_up': 4.325845e-02, 'mlp_w_down': 2.327316e-01}


def _to_microbatches(a, axis):
    t = _jnp.moveaxis(a, axis, 0)
    t = t.reshape((N_MICROBATCH, t.shape[0] // N_MICROBATCH) + t.shape[1:])
    return _jnp.moveaxis(t, 1, axis + 1)


def setup_inputs(seed: int = 0) -> dict:
    inp = _fwd_setup_inputs(seed)
    key = _jax.random.fold_in(_jax.random.key(seed), 7919)
    shape, _ = _output_shape()
    out = dict(inp)
    out["loss_target"] = _jax.random.normal(_jax.random.fold_in(key, 0), shape, _jnp.float32)
    for i, name in enumerate(TWIN_WEIGHTS):
        w = inp[name].astype(_jnp.float32)
        if MOMENT_SCALE is None:
            s = _jnp.sqrt(_jnp.mean(_jnp.square(w)) + 1e-30)
        else:
            s = MOMENT_SCALE[name]
        km, kv = _jax.random.split(_jax.random.fold_in(key, i + 1))
        out[name] = w
        out["m_" + name] = s * _jax.random.normal(km, w.shape, _jnp.float32)
        out["v_" + name] = (s * s) * _jax.random.uniform(kv, w.shape, _jnp.float32, 0.5, 1.5)
    if N_MICROBATCH > 1:
        for name, axis in PER_EXAMPLE_BATCH_AXIS.items():
            out[name] = _to_microbatches(out[name], axis)
    return {'x': out['x'], 'c': out['c'], 'ada_w': out['ada_w'], 'ada_b': out['ada_b'], 'ln_g': out['ln_g'], 'ln_b': out['ln_b'], 'a_w_in': out['a_w_in'], 'a_b_in': out['a_b_in'], 'a_vn_g': out['a_vn_g'], 'a_vn_b': out['a_vn_b'], 'a_w_s': out['a_w_s'], 'a_b_s': out['a_b_s'], 'a_w_out': out['a_w_out'], 'b_w_qkv': out['b_w_qkv'], 'b_w_out': out['b_w_out'], 'mlp_w_up': out['mlp_w_up'], 'mlp_w_down': out['mlp_w_down'], 'loss_target': out['loss_target'], 'm_ada_w': out['m_ada_w'], 'm_ada_b': out['m_ada_b'], 'm_ln_g': out['m_ln_g'], 'm_ln_b': out['m_ln_b'], 'm_a_w_in': out['m_a_w_in'], 'm_a_b_in': out['m_a_b_in'], 'm_a_vn_g': out['m_a_vn_g'], 'm_a_vn_b': out['m_a_vn_b'], 'm_a_w_s': out['m_a_w_s'], 'm_a_b_s': out['m_a_b_s'], 'm_a_w_out': out['m_a_w_out'], 'm_b_w_qkv': out['m_b_w_qkv'], 'm_b_w_out': out['m_b_w_out'], 'm_mlp_w_up': out['m_mlp_w_up'], 'm_mlp_w_down': out['m_mlp_w_down'], 'v_ada_w': out['v_ada_w'], 'v_ada_b': out['v_ada_b'], 'v_ln_g': out['v_ln_g'], 'v_ln_b': out['v_ln_b'], 'v_a_w_in': out['v_a_w_in'], 'v_a_b_in': out['v_a_b_in'], 'v_a_vn_g': out['v_a_vn_g'], 'v_a_vn_b': out['v_a_vn_b'], 'v_a_w_s': out['v_a_w_s'], 'v_a_b_s': out['v_a_b_s'], 'v_a_w_out': out['v_a_w_out'], 'v_b_w_qkv': out['v_b_w_qkv'], 'v_b_w_out': out['v_b_w_out'], 'v_mlp_w_up': out['v_mlp_w_up'], 'v_mlp_w_down': out['v_mlp_w_down']}


def _loss(weights, diff, rest, loss_target):
    with _jax.named_scope("forward"):
        args = {**rest, TWIN_DIFF_INPUT: diff, **{k: w.astype(_WEIGHT_DTYPES[k]) for k, w in weights.items()}}
        y = _forward(args)
    with _jax.named_scope("loss_head"):
        err = _jnp.square(y.astype(_jnp.float32) - loss_target)
        return 0.5 * _jnp.sum(_jnp.mean(err, axis=-1)) if err.ndim else 0.5 * err


def _adamw(w, g, m, v):
    m = ADAM_B1 * m + (1.0 - ADAM_B1) * g
    v = ADAM_B2 * v + (1.0 - ADAM_B2) * _jnp.square(g)
    m_hat = m / (1.0 - ADAM_B1 ** ADAM_STEP)
    v_hat = v / (1.0 - ADAM_B2 ** ADAM_STEP)
    delta = -ADAM_LR * (m_hat / (_jnp.sqrt(v_hat) + ADAM_EPS) + ADAM_WD * w)
    return delta, m, v


def reference(x, c, ada_w, ada_b, ln_g, ln_b, a_w_in, a_b_in, a_vn_g, a_vn_b, a_w_s, a_b_s, a_w_out, b_w_qkv, b_w_out, mlp_w_up, mlp_w_down, loss_target, m_ada_w, m_ada_b, m_ln_g, m_ln_b, m_a_w_in, m_a_b_in, m_a_vn_g, m_a_vn_b, m_a_w_s, m_a_b_s, m_a_w_out, m_b_w_qkv, m_b_w_out, m_mlp_w_up, m_mlp_w_down, v_ada_w, v_ada_b, v_ln_g, v_ln_b, v_a_w_in, v_a_b_in, v_a_vn_g, v_a_vn_b, v_a_w_s, v_a_b_s, v_a_w_out, v_b_w_qkv, v_b_w_out, v_mlp_w_up, v_mlp_w_down):
    given = dict(x=x, c=c, ada_w=ada_w, ada_b=ada_b, ln_g=ln_g, ln_b=ln_b, a_w_in=a_w_in, a_b_in=a_b_in, a_vn_g=a_vn_g, a_vn_b=a_vn_b, a_w_s=a_w_s, a_b_s=a_b_s, a_w_out=a_w_out, b_w_qkv=b_w_qkv, b_w_out=b_w_out, mlp_w_up=mlp_w_up, mlp_w_down=mlp_w_down, loss_target=loss_target, m_ada_w=m_ada_w, m_ada_b=m_ada_b, m_ln_g=m_ln_g, m_ln_b=m_ln_b, m_a_w_in=m_a_w_in, m_a_b_in=m_a_b_in, m_a_vn_g=m_a_vn_g, m_a_vn_b=m_a_vn_b, m_a_w_s=m_a_w_s, m_a_b_s=m_a_b_s, m_a_w_out=m_a_w_out, m_b_w_qkv=m_b_w_qkv, m_b_w_out=m_b_w_out, m_mlp_w_up=m_mlp_w_up, m_mlp_w_down=m_mlp_w_down, v_ada_w=v_ada_w, v_ada_b=v_ada_b, v_ln_g=v_ln_g, v_ln_b=v_ln_b, v_a_w_in=v_a_w_in, v_a_b_in=v_a_b_in, v_a_vn_g=v_a_vn_g, v_a_vn_b=v_a_vn_b, v_a_w_s=v_a_w_s, v_a_b_s=v_a_b_s, v_a_w_out=v_a_w_out, v_b_w_qkv=v_b_w_qkv, v_b_w_out=v_b_w_out, v_mlp_w_up=v_mlp_w_up, v_mlp_w_down=v_mlp_w_down)
    weights = {n: given[n] for n in TWIN_WEIGHTS}
    shared = {n: given[n] for n in SHARED_INPUTS}
    per_example = {n: given[n] for n in ['x', 'c']}
    grad_fn = _jax.value_and_grad(_loss, argnums=(0, 1))

    def one_microbatch(ex, loss_target):
        ex = dict(ex)
        diff = ex.pop(TWIN_DIFF_INPUT)
        return grad_fn(weights, diff, {**shared, **ex}, loss_target)

    if N_MICROBATCH == 1:
        loss, (grad_w, grad_x) = one_microbatch(per_example, given["loss_target"])
    else:
        def body(carry, xs):
            loss_sum, grad_sum = carry
            l_k, (gw_k, gx_k) = one_microbatch(xs[0], xs[1])
            with _jax.named_scope("update"):
                return (loss_sum + l_k, _jax.tree.map(_jnp.add, grad_sum, gw_k)), gx_k

        init = (_jnp.zeros((), _jnp.float32), _jax.tree.map(_jnp.zeros_like, weights))
        (loss, grad_w), grad_x = _jax.lax.scan(body, init, (per_example, given["loss_target"]))
    with _jax.named_scope("update"):
        delta_w, new_m, new_v = {}, {}, {}
        for n in TWIN_WEIGHTS:
            delta_w[n], new_m[n], new_v[n] = _adamw(weights[n], grad_w[n], given["m_" + n], given["v_" + n])
    return (loss, grad_x, *[grad_w[n] for n in TWIN_WEIGHTS], *[delta_w[n] for n in TWIN_WEIGHTS],
            *[new_m[n] for n in TWIN_WEIGHTS], *[new_v[n] for n in TWIN_WEIGHTS])
```

```python
import math

import jax
import jax.numpy as jnp
from jax import lax
from jax.experimental import pallas as pl
from jax.experimental.pallas import tpu as pltpu

F32 = jnp.float32
BF16 = jnp.bfloat16
MESH = pl.DeviceIdType.MESH
ANY = pl.BlockSpec(memory_space=pl.ANY)
VMEM = pl.BlockSpec(memory_space=pltpu.VMEM)

N_DEV = 8
D_MODEL = 1024
SEQ = 4096
DEPTH = 2
CHUNK = 128
A_GROUPS = 16
A_GROUP_DIM = D_MODEL // A_GROUPS
B_HEADS = 16
B_HEAD_DIM = 64
B_PATTERNS = ((128, 1), (512, 4), (2048, 16))
N_PAT = len(B_PATTERNS)
SPAN = 128
D_FF = 4 * D_MODEL
D_QKV = N_PAT * 3 * D_MODEL
ALPHA = (2 * DEPTH) ** 0.25
LN_EPS = 1e-5
NEG = -1e30
ADAM_LR = 0.001
ADAM_B1 = 0.9
ADAM_B2 = 0.999
ADAM_EPS = 1e-08
ADAM_WD = 0.01
ADAM_STEP = 10
GELU_C = math.sqrt(2.0 / math.pi)
GELU_A = 0.044715

VMEM_LIMIT_BYTES = 56 * 1024 * 1024
LANES = 128
SUBLANES = 8
ROW_TILE = 512
N_SUB = 2 * DEPTH


def _params(sem):
    return pltpu.CompilerParams(dimension_semantics=sem, vmem_limit_bytes=VMEM_LIMIT_BYTES)


def _lane(shape):
    return lax.broadcasted_iota(jnp.int32, shape, len(shape) - 1)


def _split_bf16(x):
    hi = x.astype(BF16)
    lo = (x - hi.astype(F32)).astype(BF16)
    return hi, lo


def _group_expand_matrix(groups_padded, width):
    per = width // A_GROUPS
    r = lax.broadcasted_iota(jnp.int32, (groups_padded, width), 0)
    c = lax.broadcasted_iota(jnp.int32, (groups_padded, width), 1)
    return (c // per == r).astype(BF16)


def _group_reduce_matrix(width, groups_padded):
    per = width // A_GROUPS
    r = lax.broadcasted_iota(jnp.int32, (width, groups_padded), 0)
    c = lax.broadcasted_iota(jnp.int32, (width, groups_padded), 1)
    return (r // per == c).astype(BF16)


def _expand_groups(w):
    e = _group_expand_matrix(LANES, D_MODEL)
    hi, lo = _split_bf16(w)
    return jnp.dot(hi, e, preferred_element_type=F32) + jnp.dot(lo, e, preferred_element_type=F32)


def _reduce_groups(x):
    e = _group_reduce_matrix(D_MODEL, LANES)
    hi, lo = _split_bf16(x)
    return jnp.dot(hi, e, preferred_element_type=F32) + jnp.dot(lo, e, preferred_element_type=F32)


def mm_nn(a, b3, *, name, tm, tn, epilogue, extras=(), outs):
    m, k = a.shape
    p, _, n = b3.shape
    q = n // tn
    grid = (m // tm, p * q)

    def body(a_ref, b_ref, *rest):
        ex = rest[: len(extras)]
        out_refs = rest[len(extras) :]
        acc = jnp.dot(a_ref[...], b_ref[...], preferred_element_type=F32)
        res = epilogue(acc, *[e[...] for e in ex])
        for o_ref, r in zip(out_refs, res, strict=True):
            o_ref[...] = r.astype(o_ref.dtype)

    ex_specs = []
    for _, kind in extras:
        if kind == "row":
            ex_specs.append(pl.BlockSpec((1, tn), lambda i, j: (0, j)))
        else:
            ex_specs.append(pl.BlockSpec((tm, tn), lambda i, j: (i, j)))
    return pl.pallas_call(
        body,
        name=name,
        grid=grid,
        in_specs=[
            pl.BlockSpec((tm, k), lambda i, j: (i, 0)),
            pl.BlockSpec((None, k, tn), lambda i, j: (j // q, 0, j % q)),
            *ex_specs,
        ],
        out_specs=[pl.BlockSpec((tm, tn), lambda i, j: (i, j)) for _ in outs],
        out_shape=[jax.ShapeDtypeStruct((m, p * n), dt) for dt in outs],
        compiler_params=_params(("parallel", "parallel")),
    )(a, b3, *[arr for arr, _ in extras])


def mm_nt(g, b3, *, name, tm, tko, ps, epilogue, extras=(), outs):
    m = g.shape[0]
    p, k, n = b3.shape
    nc = p // ps
    grid = (m // tm, k // tko, nc)

    def body(g_ref, b_ref, *rest):
        ex = rest[: len(extras)]
        out_refs = rest[len(extras) : len(extras) + len(outs)]
        acc_ref = rest[-1]
        c = pl.program_id(2)
        part = None
        for s in range(ps):
            d = lax.dot_general(
                g_ref[:, s * n : (s + 1) * n], b_ref[s], (((1,), (1,)), ((), ())), preferred_element_type=F32
            )
            part = d if part is None else part + d

        @pl.when(c == 0)
        def _():
            acc_ref[...] = part

        @pl.when(c > 0)
        def _():
            acc_ref[...] += part

        @pl.when(c == nc - 1)
        def _():
            res = epilogue(acc_ref[...], *[e[...] for e in ex])
            for o_ref, r in zip(out_refs, res, strict=True):
                o_ref[...] = r.astype(o_ref.dtype)

    ex_specs = []
    for _, kind in extras:
        if kind == "row":
            ex_specs.append(pl.BlockSpec((1, tko), lambda i, j, c: (0, j)))
        else:
            ex_specs.append(pl.BlockSpec((tm, tko), lambda i, j, c: (i, j)))
    return pl.pallas_call(
        body,
        name=name,
        grid=grid,
        in_specs=[
            pl.BlockSpec((tm, ps * n), lambda i, j, c: (i, c)),
            pl.BlockSpec((ps, tko, n), lambda i, j, c: (c, j, 0)),
            *ex_specs,
        ],
        out_specs=[pl.BlockSpec((tm, tko), lambda i, j, c: (i, j)) for _ in outs],
        out_shape=[jax.ShapeDtypeStruct((m, k), dt) for dt in outs],
        scratch_shapes=[pltpu.VMEM((tm, tko), F32)],
        compiler_params=_params(("parallel", "parallel", "arbitrary")),
    )(g, b3, *[arr for arr, _ in extras])


def mm_tn(a, g, *, name, p, tk, tn, tmc, out_dtype):
    m, k = a.shape
    n = g.shape[1] // p
    q = n // tn
    nc = m // tmc
    grid = (k // tk, p * q, nc)

    def body(a_ref, g_ref, o_ref, acc_ref):
        c = pl.program_id(2)
        part = lax.dot_general(a_ref[...], g_ref[...], (((0,), (0,)), ((), ())), preferred_element_type=F32)

        @pl.when(c == 0)
        def _():
            acc_ref[...] = part

        @pl.when(c > 0)
        def _():
            acc_ref[...] += part

        @pl.when(c == nc - 1)
        def _():
            o_ref[...] = acc_ref[...].astype(o_ref.dtype)

    return pl.pallas_call(
        body,
        name=name,
        grid=grid,
        in_specs=[
            pl.BlockSpec((tmc, tk), lambda i, j, c: (c, i)),
            pl.BlockSpec((tmc, tn), lambda i, j, c: (c, j)),
        ],
        out_specs=pl.BlockSpec((None, tk, tn), lambda i, j, c: (j // q, i, j % q)),
        out_shape=jax.ShapeDtypeStruct((p, k, n), out_dtype),
        scratch_shapes=[pltpu.VMEM((tk, tn), F32)],
        compiler_params=_params(("parallel", "parallel", "arbitrary")),
    )(a, g)


def _rows(cols):
    return pl.BlockSpec((ROW_TILE, cols), lambda i: (i, 0))


def _vec(cols, rows=1):
    return pl.BlockSpec((rows, cols), lambda i: (0, 0))


def _layer_norm_hat(z):
    mu = jnp.mean(z, axis=-1, keepdims=True)
    zc = z - mu
    var = jnp.mean(zc * zc, axis=-1, keepdims=True)
    rstd = lax.rsqrt(var + LN_EPS)
    return zc * rstd, rstd


def modulate(x, scale, shift, *, name):
    s, d = x.shape

    def body(x_ref, sc_ref, sh_ref, h_ref):
        h_ref[...] = (x_ref[...] * (1.0 + sc_ref[...]) + sh_ref[...]).astype(BF16)

    return pl.pallas_call(
        body,
        name=name,
        grid=(s // ROW_TILE,),
        in_specs=[_rows(d), _vec(d), _vec(d)],
        out_specs=_rows(d),
        out_shape=jax.ShapeDtypeStruct((s, d), BF16),
        compiler_params=_params(("parallel",)),
    )(x, scale, shift)


def residual_ln(x, y, gate1, g, b, nscale, nshift, *, name):
    s, d = x.shape

    def body(x_ref, y_ref, gt_ref, g_ref, b_ref, sc_ref, sh_ref, xn_ref, hn_ref):
        z = ALPHA * x_ref[...] + gt_ref[...] * y_ref[...]
        xhat, _ = _layer_norm_hat(z)
        xn = xhat * g_ref[...] + b_ref[...]
        xn_ref[...] = xn
        hn_ref[...] = (xn * (1.0 + sc_ref[...]) + sh_ref[...]).astype(BF16)

    return pl.pallas_call(
        body,
        name=name,
        grid=(s // ROW_TILE,),
        in_specs=[_rows(d), _rows(d), _vec(d), _vec(d), _vec(d), _vec(d), _vec(d)],
        out_specs=[_rows(d), _rows(d)],
        out_shape=[jax.ShapeDtypeStruct((s, d), F32), jax.ShapeDtypeStruct((s, d), BF16)],
        compiler_params=_params(("parallel",)),
    )(x, y, gate1, g, b, nscale, nshift)


def residual_ln_loss(x, y, gate1, g, b, target, *, name):
    s, d = x.shape

    def body(x_ref, y_ref, gt_ref, g_ref, b_ref, t_ref, dx_ref, loss_ref):
        z = ALPHA * x_ref[...] + gt_ref[...] * y_ref[...]
        xhat, _ = _layer_norm_hat(z)
        err = xhat * g_ref[...] + b_ref[...] - t_ref[...]
        dx_ref[...] = err * (1.0 / d)
        part = jnp.sum(jnp.sum(err * err, axis=1, keepdims=True), axis=0, keepdims=True) * (0.5 / d)

        @pl.when(pl.program_id(0) == 0)
        def _():
            loss_ref[...] = part

        @pl.when(pl.program_id(0) > 0)
        def _():
            loss_ref[...] += part

    return pl.pallas_call(
        body,
        name=name,
        grid=(s // ROW_TILE,),
        in_specs=[_rows(d), _rows(d), _vec(d), _vec(d), _vec(d), _rows(d)],
        out_specs=[_rows(d), pl.BlockSpec((1, 1), lambda i: (0, 0))],
        out_shape=[jax.ShapeDtypeStruct((s, d), F32), jax.ShapeDtypeStruct((1, 1), F32)],
        compiler_params=_params(("arbitrary",)),
    )(x, y, gate1, g, b, target)


ST_DSCALE, ST_DSHIFT, ST_DG, ST_DB, ST_DGATE = 0, 1, 2, 3, 4


def residual_ln_bwd(*, name, later=None, dxo=None, this=None):
    lead = later[0] if later is not None else dxo
    s, d = lead.shape
    has_later, has_ln = later is not None, this is not None

    def body(*refs):
        refs = list(refs)
        if has_later:
            dh_ref, dzl_ref, xm_ref, scl_ref = refs[:4]
            refs = refs[4:]
        else:
            dxo_ref = refs.pop(0)
        if has_ln:
            x_ref, y_ref, gt_ref, g_ref = refs[:4]
            refs = refs[4:]
            dz_ref, dy_ref, st_ref = refs
        else:
            dx_ref, st_ref = refs

        @pl.when(pl.program_id(0) == 0)
        def _():
            st_ref[...] = jnp.zeros_like(st_ref)

        def acc(row, val):
            st_ref[row : row + 1, :] += jnp.sum(val, axis=0, keepdims=True)

        if has_later:
            dh = dh_ref[...]
            g_out = ALPHA * dzl_ref[...] + dh * (1.0 + scl_ref[...])
            acc(ST_DSCALE, dh * xm_ref[...])
            acc(ST_DSHIFT, dh)
        else:
            g_out = dxo_ref[...]
        if not has_ln:
            dx_ref[...] = g_out
            return
        y = y_ref[...]
        gate1 = gt_ref[...]
        z = ALPHA * x_ref[...] + gate1 * y
        xhat, rstd = _layer_norm_hat(z)
        acc(ST_DG, g_out * xhat)
        acc(ST_DB, g_out)
        dxh = g_out * g_ref[...]
        m1 = jnp.mean(dxh, axis=-1, keepdims=True)
        m2 = jnp.mean(dxh * xhat, axis=-1, keepdims=True)
        dz = rstd * (dxh - m1 - xhat * m2)
        acc(ST_DGATE, dz * y)
        dz_ref[...] = dz
        dy_ref[...] = (dz * gate1).astype(BF16)

    ins, specs = [], []
    if has_later:
        ins += list(later)
        specs += [_rows(d), _rows(d), _rows(d), _vec(d)]
    else:
        ins += [dxo]
        specs += [_rows(d)]
    if has_ln:
        ins += list(this)
        specs += [_rows(d), _rows(d), _vec(d), _vec(d)]
        out_specs = [_rows(d), _rows(d), _vec(d, SUBLANES)]
        out_shape = [
            jax.ShapeDtypeStruct((s, d), F32),
            jax.ShapeDtypeStruct((s, d), BF16),
            jax.ShapeDtypeStruct((SUBLANES, d), F32),
        ]
    else:
        out_specs = [_rows(d), _vec(d, SUBLANES)]
        out_shape = [jax.ShapeDtypeStruct((s, d), F32), jax.ShapeDtypeStruct((SUBLANES, d), F32)]
    return pl.pallas_call(
        body,
        name=name,
        grid=(s // ROW_TILE,),
        in_specs=specs,
        out_specs=out_specs,
        out_shape=out_shape,
        compiler_params=_params(("arbitrary",)),
    )(*ins)


GATE_CHUNKS = 4


def _gelu(x):
    return 0.5 * x * (1.0 + jnp.tanh(GELU_C * (x + GELU_A * x * x * x)))


def _gelu_grad(x):
    t = jnp.tanh(GELU_C * (x + GELU_A * x * x * x))
    return 0.5 * (1.0 + t) + 0.5 * x * (1.0 - t * t) * (GELU_C * (1.0 + 3.0 * GELU_A * x * x))


def _causal_weights(w_ref, transpose):
    t = lax.broadcasted_iota(jnp.int32, (CHUNK, CHUNK), 0)
    s = lax.broadcasted_iota(jnp.int32, (CHUNK, CHUNK), 1)
    out = []
    for g in range(A_GROUPS):
        w = jnp.where(t >= s, w_ref[g], 0.0)
        out.append((w.T if transpose else w).astype(BF16))
    return out


def _spatial(ws, vn, lo_mask):
    rows = vn.shape[0]
    out_rows = []
    for r in range(rows // CHUNK):
        cols = []
        for j in range(A_GROUPS // 2):
            blk = vn[r * CHUNK : (r + 1) * CHUNK, j * LANES : (j + 1) * LANES]
            za = jnp.dot(ws[2 * j], blk, preferred_element_type=F32)
            zb = jnp.dot(ws[2 * j + 1], blk, preferred_element_type=F32)
            cols.append(jnp.where(lo_mask, za, zb))
        out_rows.append(jnp.concatenate(cols, axis=1))
    return jnp.concatenate(out_rows, axis=0)


def _gate_forward(a, vg, vb, ws, bias, lo_mask):
    u = _gelu(a[:, :D_MODEL])
    v = _gelu(a[:, D_MODEL:])
    vhat, rstd = _layer_norm_hat(v)
    vn = (vhat * vg + vb).astype(BF16)
    z = _spatial(ws, vn, lo_mask) + jnp.concatenate([bias] * (a.shape[0] // CHUNK), axis=0)
    return u, vhat, rstd, vn, z


def gate_fwd(a_pre, vn_g, vn_b, w_s, bias_full, *, name):
    s = a_pre.shape[0]
    tr = GATE_CHUNKS * CHUNK

    def body(a_ref, vg_ref, vb_ref, w_ref, bias_ref, p_ref):
        lo_mask = _lane((CHUNK, LANES)) < A_GROUP_DIM
        ws = _causal_weights(w_ref, transpose=False)
        u, _, _, _, z = _gate_forward(a_ref[...], vg_ref[...], vb_ref[...], ws, bias_ref[...], lo_mask)
        p_ref[...] = (u * z).astype(BF16)

    return pl.pallas_call(
        body,
        name=name,
        grid=(s // tr,),
        in_specs=[
            pl.BlockSpec((tr, 2 * D_MODEL), lambda i: (i, 0)),
            _vec(D_MODEL),
            _vec(D_MODEL),
            pl.BlockSpec((A_GROUPS, CHUNK, CHUNK), lambda i: (0, 0, 0)),
            _vec(D_MODEL, CHUNK),
        ],
        out_specs=pl.BlockSpec((tr, D_MODEL), lambda i: (i, 0)),
        out_shape=jax.ShapeDtypeStruct((s, D_MODEL), BF16),
        compiler_params=_params(("parallel",)),
    )(a_pre, vn_g, vn_b, w_s, bias_full)


def gate_bwd(a_pre, dp, vn_g, vn_b, w_s, bias_full, *, name):
    s = a_pre.shape[0]
    tr = GATE_CHUNKS * CHUNK
    nsteps = s // tr

    def body(a_ref, dp_ref, vg_ref, vb_ref, w_ref, bias_ref, da_ref, dw_ref, dbs_ref, rows_ref, dbias_acc):
        step = pl.program_id(0)
        lo_mask = _lane((CHUNK, LANES)) < A_GROUP_DIM

        @pl.when(step == 0)
        def _():
            dw_ref[...] = jnp.zeros_like(dw_ref)
            rows_ref[...] = jnp.zeros_like(rows_ref)
            dbias_acc[...] = jnp.zeros_like(dbias_acc)

        a = a_ref[...]
        vg = vg_ref[...]
        ws = _causal_weights(w_ref, transpose=False)
        wts = _causal_weights(w_ref, transpose=True)
        u, vhat, rstd, vn, z = _gate_forward(a, vg, vb_ref[...], ws, bias_ref[...], lo_mask)
        dp = dp_ref[...]
        du = dp * z
        dzz = dp * u
        dzz_b = dzz.astype(BF16)
        dvn = _spatial(wts, dzz_b, lo_mask)
        dbias = None
        for r in range(GATE_CHUNKS):
            rs = slice(r * CHUNK, (r + 1) * CHUNK)
            dbias = dzz[rs] if dbias is None else dbias + dzz[rs]
            for j in range(A_GROUPS // 2):
                cs = slice(j * LANES, (j + 1) * LANES)
                dblk = dzz[rs, cs]
                vblk = vn[rs, cs]
                for half in range(2):
                    keep = lo_mask if half == 0 else jnp.logical_not(lo_mask)
                    dm = jnp.where(keep, dblk, 0.0).astype(BF16)
                    dw_ref[2 * j + half] += lax.dot_general(
                        dm, vblk, (((1,), (1,)), ((), ())), preferred_element_type=F32
                    )
        dbias_acc[...] += dbias
        rows_ref[1:2, :D_MODEL] += jnp.sum(dvn * vhat, axis=0, keepdims=True)
        rows_ref[1:2, D_MODEL:] += jnp.sum(dvn, axis=0, keepdims=True)
        dvh = dvn * vg
        m1 = jnp.mean(dvh, axis=-1, keepdims=True)
        m2 = jnp.mean(dvh * vhat, axis=-1, keepdims=True)
        dv = rstd * (dvh - m1 - vhat * m2)
        da_u = du * _gelu_grad(a[:, :D_MODEL])
        da_v = dv * _gelu_grad(a[:, D_MODEL:])
        da_ref[:, :D_MODEL] = da_u.astype(BF16)
        da_ref[:, D_MODEL:] = da_v.astype(BF16)
        rows_ref[0:1, :D_MODEL] += jnp.sum(da_u, axis=0, keepdims=True)
        rows_ref[0:1, D_MODEL:] += jnp.sum(da_v, axis=0, keepdims=True)

        @pl.when(step == nsteps - 1)
        def _():
            t = lax.broadcasted_iota(jnp.int32, (CHUNK, CHUNK), 0)
            sx = lax.broadcasted_iota(jnp.int32, (CHUNK, CHUNK), 1)
            for g in range(A_GROUPS):
                dw_ref[g] = jnp.where(t >= sx, dw_ref[g], 0.0)
            dbs_ref[...] = _reduce_groups(dbias_acc[...])

    return pl.pallas_call(
        body,
        name=name,
        grid=(nsteps,),
        in_specs=[
            pl.BlockSpec((tr, 2 * D_MODEL), lambda i: (i, 0)),
            pl.BlockSpec((tr, D_MODEL), lambda i: (i, 0)),
            _vec(D_MODEL),
            _vec(D_MODEL),
            pl.BlockSpec((A_GROUPS, CHUNK, CHUNK), lambda i: (0, 0, 0)),
            _vec(D_MODEL, CHUNK),
        ],
        out_specs=[
            pl.BlockSpec((tr, 2 * D_MODEL), lambda i: (i, 0)),
            pl.BlockSpec((A_GROUPS, CHUNK, CHUNK), lambda i: (0, 0, 0)),
            _vec(LANES, CHUNK),
            _vec(2 * D_MODEL, SUBLANES),
        ],
        out_shape=[
            jax.ShapeDtypeStruct((s, 2 * D_MODEL), BF16),
            jax.ShapeDtypeStruct((A_GROUPS, CHUNK, CHUNK), F32),
            jax.ShapeDtypeStruct((CHUNK, LANES), F32),
            jax.ShapeDtypeStruct((SUBLANES, 2 * D_MODEL), F32),
        ],
        scratch_shapes=[pltpu.VMEM((CHUNK, D_MODEL), F32)],
        compiler_params=_params(("arbitrary",)),
    )(a_pre, dp, vn_g, vn_b, w_s, bias_full)


def alibi_tables(dilation):
    qi = jnp.arange(SPAN)[:, None]
    ki = jnp.arange(2 * SPAN)[None, :]
    diff = SPAN + qi - ki
    valid = (diff >= 0) & (diff <= SPAN)
    heads = jnp.arange(1, B_HEADS + 1, dtype=F32)
    slopes = jnp.exp2(-8.0 * heads / B_HEADS)
    bias = -slopes[:, None, None] * (dilation * diff).astype(F32)
    return jnp.where(valid[None], bias, NEG)


def _qkv_specs(pat, dilation, grid_rank_fn):
    cols = D_QKV // D_MODEL
    base = 3 * pat

    def spec(which, prev):
        def index(r, n):
            blk = jnp.maximum(n - 1, 0) if prev else n
            return (grid_rank_fn(blk), r * cols + base + which)

        return pl.BlockSpec((SPAN, D_MODEL), index)

    return [spec(0, False), spec(1, True), spec(1, False), spec(2, True), spec(2, False)]


def attn_fwd(qkv, pat, *, name):
    _, dilation = B_PATTERNS[pat]
    rows = SEQ // dilation
    nb = rows // SPAN
    qkv2 = qkv.reshape(rows, dilation * D_QKV)
    bias = alibi_tables(dilation)

    def body(q_ref, kp_ref, kc_ref, vp_ref, vc_ref, bias_ref, o_ref, lse_ref):
        n = pl.program_id(1)
        lane = _lane((SPAN, LANES))
        lo_mask = lane < B_HEAD_DIM
        first_prev = jnp.logical_and(n == 0, _lane((SPAN, 2 * SPAN)) < SPAN)
        q = q_ref[...] * jnp.asarray(B_HEAD_DIM**-0.5, BF16)
        kk = jnp.concatenate([kp_ref[...], kc_ref[...]], axis=0)
        vv = jnp.concatenate([vp_ref[...], vc_ref[...]], axis=0)
        halves = (lo_mask.astype(BF16), jnp.logical_not(lo_mask).astype(BF16))
        stats = jnp.zeros((SPAN, LANES), F32)
        for j in range(B_HEADS // 2):
            cs = slice(j * LANES, (j + 1) * LANES)
            qp, kp, vp = q[:, cs], kk[:, cs], vv[:, cs]
            outs = []
            for half in range(2):
                h = 2 * j + half
                sc = lax.dot_general(qp * halves[half], kp, (((1,), (1,)), ((), ())), preferred_element_type=F32)
                sc = jnp.where(first_prev, NEG, sc + bias_ref[h])
                m = jnp.max(sc, axis=1, keepdims=True)
                p = jnp.exp(sc - m)
                l = jnp.sum(p, axis=1, keepdims=True)
                acc = jnp.dot(p.astype(BF16), vp, preferred_element_type=F32)
                outs.append(acc * (1.0 / l))
                stats = jnp.where(lane == h, m + jnp.log(l), stats)
            o_ref[:, cs] = jnp.where(lo_mask, outs[0], outs[1])
        lse_ref[...] = stats

    return pl.pallas_call(
        body,
        name=name,
        grid=(dilation, nb),
        in_specs=[
            *_qkv_specs(pat, dilation, lambda blk: blk),
            pl.BlockSpec((B_HEADS, SPAN, 2 * SPAN), lambda r, n: (0, 0, 0)),
        ],
        out_specs=[
            pl.BlockSpec((SPAN, D_MODEL), lambda r, n: (n, r)),
            pl.BlockSpec((SPAN, LANES), lambda r, n: (n, r)),
        ],
        out_shape=[
            jax.ShapeDtypeStruct((rows, dilation * D_MODEL), F32),
            jax.ShapeDtypeStruct((rows, dilation * LANES), F32),
        ],
        compiler_params=_params(("parallel", "arbitrary")),
    )(qkv2, qkv2, qkv2, qkv2, qkv2, bias)


def attn_combine(outs, lses, *, name):
    def body(o0, o1, o2, l0, l1, l2, ob_ref, of_ref, lse_ref):
        ls = [l0[...], l1[...], l2[...]]
        m = jnp.maximum(jnp.maximum(ls[0], ls[1]), ls[2])
        tot = jnp.log(jnp.exp(ls[0] - m) + jnp.exp(ls[1] - m) + jnp.exp(ls[2] - m)) + m
        o = None
        for o_ref, l in zip((o0, o1, o2), ls, strict=True):
            term = _expand_groups(jnp.exp(l - tot)) * o_ref[...]
            o = term if o is None else o + term
        ob_ref[...] = o.astype(BF16)
        of_ref[...] = o
        lse_ref[...] = tot

    return pl.pallas_call(
        body,
        name=name,
        grid=(SEQ // ROW_TILE,),
        in_specs=[_rows(D_MODEL)] * 3 + [_rows(LANES)] * 3,
        out_specs=[_rows(D_MODEL), _rows(D_MODEL), _rows(LANES)],
        out_shape=[
            jax.ShapeDtypeStruct((SEQ, D_MODEL), BF16),
            jax.ShapeDtypeStruct((SEQ, D_MODEL), F32),
            jax.ShapeDtypeStruct((SEQ, LANES), F32),
        ],
        compiler_params=_params(("parallel",)),
    )(*outs, *lses)


def attn_delta(do, o, *, name):
    def body(do_ref, o_ref, dob_ref, dl_ref):
        do_v = do_ref[...]
        dob_ref[...] = do_v.astype(BF16)
        dl_ref[...] = _reduce_groups(do_v * o_ref[...])

    return pl.pallas_call(
        body,
        name=name,
        grid=(SEQ // ROW_TILE,),
        in_specs=[_rows(D_MODEL), _rows(D_MODEL)],
        out_specs=[_rows(D_MODEL), _rows(LANES)],
        out_shape=[jax.ShapeDtypeStruct((SEQ, D_MODEL), BF16), jax.ShapeDtypeStruct((SEQ, LANES), F32)],
        compiler_params=_params(("parallel",)),
    )(do, o)


def attn_bwd(qkv, do_b, lse, delta, pat, *, name):
    _, dilation = B_PATTERNS[pat]
    rows = SEQ // dilation
    nb = rows // SPAN
    qkv2 = qkv.reshape(rows, dilation * D_QKV)
    do2 = do_b.reshape(rows, dilation * D_MODEL)
    lse2 = lse.reshape(rows, dilation * LANES)
    delta2 = delta.reshape(rows, dilation * LANES)
    bias = alibi_tables(dilation)
    last = nb - 1

    def body(q_ref, kp_ref, kc_ref, vp_ref, vc_ref, do_ref, lse_ref, dl_ref, bias_ref, dq_ref, dk_ref, dv_ref, ck_ref, cv_ref):
        n = pl.program_id(1)

        @pl.when(n == nb)
        def _():
            dk_ref[...] = ck_ref[...].astype(BF16)
            dv_ref[...] = cv_ref[...].astype(BF16)

        @pl.when(n == 0)
        def _():
            ck_ref[...] = jnp.zeros_like(ck_ref)
            cv_ref[...] = jnp.zeros_like(cv_ref)

        @pl.when(n < nb)
        def _():
            lane = _lane((SPAN, LANES))
            lo_mask = lane < B_HEAD_DIM
            first_prev = jnp.logical_and(n == 0, _lane((SPAN, 2 * SPAN)) < SPAN)
            q = q_ref[...] * jnp.asarray(B_HEAD_DIM**-0.5, BF16)
            kk = jnp.concatenate([kp_ref[...], kc_ref[...]], axis=0)
            vv = jnp.concatenate([vp_ref[...], vc_ref[...]], axis=0)
            do_v = do_ref[...]
            lse_v = lse_ref[...]
            dl_v = dl_ref[...]
            halves = (lo_mask.astype(BF16), jnp.logical_not(lo_mask).astype(BF16))
            for j in range(B_HEADS // 2):
                cs = slice(j * LANES, (j + 1) * LANES)
                qp, kp, vp, dop = q[:, cs], kk[:, cs], vv[:, cs], do_v[:, cs]
                dqs = []
                dk_pair = None
                dv_pair = None
                for half in range(2):
                    h = 2 * j + half
                    qm = qp * halves[half]
                    dom = dop * halves[half]
                    lse_h = jnp.sum(jnp.where(lane == h, lse_v, 0.0), axis=1, keepdims=True)
                    dl_h = jnp.sum(jnp.where(lane == h, dl_v, 0.0), axis=1, keepdims=True)
                    sc = lax.dot_general(qm, kp, (((1,), (1,)), ((), ())), preferred_element_type=F32)
                    sc = jnp.where(first_prev, NEG, sc + bias_ref[h])
                    p = jnp.exp(sc - lse_h)
                    dp = lax.dot_general(dom, vp, (((1,), (1,)), ((), ())), preferred_element_type=F32)
                    ds = (p * (dp - dl_h)).astype(BF16)
                    dqs.append(jnp.dot(ds, kp, preferred_element_type=F32))
                    dk_h = lax.dot_general(ds, qm, (((0,), (0,)), ((), ())), preferred_element_type=F32)
                    dv_h = lax.dot_general(
                        p.astype(BF16), dom, (((0,), (0,)), ((), ())), preferred_element_type=F32
                    )
                    dk_pair = dk_h if dk_pair is None else dk_pair + dk_h
                    dv_pair = dv_h if dv_pair is None else dv_pair + dv_h
                dq_ref[:, cs] = (jnp.where(lo_mask, dqs[0], dqs[1]) * (B_HEAD_DIM**-0.5)).astype(BF16)
                dk_ref[:, cs] = (ck_ref[:, cs] + dk_pair[:SPAN]).astype(BF16)
                dv_ref[:, cs] = (cv_ref[:, cs] + dv_pair[:SPAN]).astype(BF16)
                ck_ref[:, cs] = dk_pair[SPAN:]
                cv_ref[:, cs] = dv_pair[SPAN:]

    def clamp(n):
        return jnp.minimum(n, last)

    def row_spec(width):
        return pl.BlockSpec((SPAN, width), lambda r, n: (clamp(n), r))

    lagged = pl.BlockSpec((SPAN, D_MODEL), lambda r, n: (jnp.maximum(n - 1, 0), r))
    outs = pl.pallas_call(
        body,
        name=name,
        grid=(dilation, nb + 1),
        in_specs=[
            *_qkv_specs(pat, dilation, clamp),
            row_spec(D_MODEL),
            row_spec(LANES),
            row_spec(LANES),
            pl.BlockSpec((B_HEADS, SPAN, 2 * SPAN), lambda r, n: (0, 0, 0)),
        ],
        out_specs=[row_spec(D_MODEL), lagged, lagged],
        out_shape=[jax.ShapeDtypeStruct((rows, dilation * D_MODEL), BF16)] * 3,
        scratch_shapes=[pltpu.VMEM((SPAN, D_MODEL), F32), pltpu.VMEM((SPAN, D_MODEL), F32)],
        compiler_params=_params(("arbitrary", "arbitrary")),
    )(qkv2, qkv2, qkv2, qkv2, qkv2, do2, lse2, delta2, bias)
    return [o.reshape(SEQ, D_MODEL) for o in outs]


def _position():
    x, y, c = lax.axis_index("x"), lax.axis_index("y"), lax.axis_index("c")
    return x, y, c, 4 * x + 2 * y + c


def _peer(k, x, y, c):
    px = 1 - x if k & 4 else x
    py = 1 - y if k & 2 else y
    pc = 1 - c if k & 1 else c
    return (px, py, pc), 4 * px + 2 * py + pc


def _remote(src, dst, send_sem, recv_sem, device):
    return pltpu.make_async_remote_copy(
        src_ref=src, dst_ref=dst, send_sem=send_sem, recv_sem=recv_sem, device_id=device, device_id_type=MESH
    )


def _silu_bf16(cf):
    return (cf * (1.0 / (1.0 + jnp.exp(-cf)))).astype(BF16)


def ada_exchange(c8, w4, b4, ln8):
    nt, _, ncol = w4.shape

    def body(c8_ref, w_ref, b_ref, ln_ref, cg_ref, lng_ref, mrecv_ref, mloc_ref, send_sems, recv_sems):
        x, y, c, me = _position()
        cg_ref[me] = c8_ref[...]
        lng_ref[me] = ln_ref[...]
        first = []
        for k in range(1, N_DEV):
            dev, _ = _peer(k, x, y, c)
            first.append(_remote(c8_ref, cg_ref.at[me], send_sems.at[0, k], recv_sems.at[0, k], dev))
            first.append(_remote(ln_ref, lng_ref.at[me], send_sems.at[1, k], recv_sems.at[1, k], dev))
        for cp in first:
            cp.start()
        for k in range(1, N_DEV):
            dev, pid = _peer(k, x, y, c)
            _remote(c8_ref, cg_ref.at[pid], send_sems.at[0, k], recv_sems.at[0, k], dev).wait_recv()
            _remote(ln_ref, lng_ref.at[pid], send_sems.at[1, k], recv_sems.at[1, k], dev).wait_recv()
        sc = _silu_bf16(cg_ref[...].reshape(N_DEV * SUBLANES, D_MODEL))
        for t in range(nt):
            mloc_ref[t] = jnp.dot(sc, w_ref[t].astype(BF16), preferred_element_type=F32) + b_ref[t : t + 1, :]

        def group(dev_id):
            return pl.ds(pl.multiple_of(dev_id * SUBLANES, SUBLANES), SUBLANES)

        mrecv_ref[me] = mloc_ref[:, group(me), :]
        second = []
        for k in range(1, N_DEV):
            dev, pid = _peer(k, x, y, c)
            second.append(
                _remote(mloc_ref.at[:, group(pid), :], mrecv_ref.at[me], send_sems.at[2, k], recv_sems.at[2, k], dev)
            )
        for cp in second:
            cp.start()
        for k in range(1, N_DEV):
            dev, pid = _peer(k, x, y, c)
            _remote(
                mloc_ref.at[:, group(pid), :], mrecv_ref.at[pid], send_sems.at[2, k], recv_sems.at[2, k], dev
            ).wait_recv()
        for cp in first + second:
            cp.wait_send()

    return pl.pallas_call(
        body,
        name="ada_exchange",
        in_specs=[VMEM, VMEM, VMEM, VMEM],
        out_specs=[VMEM, VMEM, VMEM],
        out_shape=[
            jax.ShapeDtypeStruct((N_DEV, SUBLANES, D_MODEL), F32),
            jax.ShapeDtypeStruct((N_DEV, SUBLANES, LANES), F32),
            jax.ShapeDtypeStruct((N_DEV, nt, SUBLANES, ncol), F32),
        ],
        scratch_shapes=[
            pltpu.VMEM((nt, N_DEV * SUBLANES, ncol), F32),
            pltpu.SemaphoreType.DMA((3, N_DEV)),
            pltpu.SemaphoreType.DMA((3, N_DEV)),
        ],
        compiler_params=pltpu.CompilerParams(vmem_limit_bytes=VMEM_LIMIT_BYTES),
    )(c8, w4, b4, ln8)


def small_exchange(dmx, flat):
    def body(dmx_ref, flat_ref, dmrecv_ref, red_ref, land_ref, send_sems, recv_sems):
        x, y, c, me = _position()
        dmrecv_ref[me] = dmx_ref[me]
        land_ref[me] = flat_ref[me]
        first = []
        for k in range(1, N_DEV):
            dev, pid = _peer(k, x, y, c)
            first.append(_remote(dmx_ref.at[pid], dmrecv_ref.at[me], send_sems.at[0, k], recv_sems.at[0, k], dev))
            first.append(_remote(flat_ref.at[pid], land_ref.at[me], send_sems.at[1, k], recv_sems.at[1, k], dev))
        for cp in first:
            cp.start()
        for k in range(1, N_DEV):
            dev, pid = _peer(k, x, y, c)
            _remote(dmx_ref.at[pid], dmrecv_ref.at[pid], send_sems.at[0, k], recv_sems.at[0, k], dev).wait_recv()
            _remote(flat_ref.at[pid], land_ref.at[pid], send_sems.at[1, k], recv_sems.at[1, k], dev).wait_recv()
        total = land_ref[0]
        for s in range(1, N_DEV):
            total = total + land_ref[s]
        red_ref[me] = total
        second = []
        for k in range(1, N_DEV):
            dev, _ = _peer(k, x, y, c)
            second.append(_remote(red_ref.at[me], red_ref.at[me], send_sems.at[2, k], recv_sems.at[2, k], dev))
        for cp in second:
            cp.start()
        for k in range(1, N_DEV):
            dev, pid = _peer(k, x, y, c)
            _remote(red_ref.at[pid], red_ref.at[pid], send_sems.at[2, k], recv_sems.at[2, k], dev).wait_recv()
        for cp in first + second:
            cp.wait_send()

    return pl.pallas_call(
        body,
        name="small_exchange",
        in_specs=[VMEM, VMEM],
        out_specs=[VMEM, VMEM],
        out_shape=[jax.ShapeDtypeStruct(dmx.shape, F32), jax.ShapeDtypeStruct(flat.shape, F32)],
        scratch_shapes=[
            pltpu.VMEM(flat.shape, F32),
            pltpu.SemaphoreType.DMA((3, N_DEV)),
            pltpu.SemaphoreType.DMA((3, N_DEV)),
        ],
        compiler_params=pltpu.CompilerParams(vmem_limit_bytes=VMEM_LIMIT_BYTES),
    )(dmx, flat)


def all_gather_shards(shards, *, name):
    nt = len(shards)

    def body(*refs):
        ins, outs = refs[:nt], refs[nt : 2 * nt]
        send_sems, recv_sems, local_sems = refs[2 * nt :]
        x, y, c, me = _position()
        sibling = (x, y, 1 - c)
        chips = [(1 - x, y), (x, 1 - y), (1 - x, 1 - y)]

        def block(t, px, py, pc):
            return outs[t].at[4 * px + 2 * py + pc]

        mine, first, passed = [], [], []
        for t in range(nt):
            cp = pltpu.make_async_copy(ins[t], block(t, x, y, c), local_sems.at[t])
            cp.start()
            mine.append(cp)
            sends = [_remote(ins[t], block(t, x, y, c), send_sems.at[t, 0], recv_sems.at[t, 0], sibling)]
            sends += [
                _remote(ins[t], block(t, x, y, c), send_sems.at[t, 1 + j], recv_sems.at[t, 1 + j], (*chip, c))
                for j, chip in enumerate(chips)
            ]
            for cp in sends:
                cp.start()
            first += sends
        for t in range(nt):
            for j, chip in enumerate(chips):
                blk = block(t, *chip, c)
                _remote(blk, blk, send_sems.at[t, 1 + j], recv_sems.at[t, 1 + j], (*chip, c)).wait_recv()
                fwd = _remote(blk, blk, send_sems.at[t, 4 + j], recv_sems.at[t, 4 + j], sibling)
                fwd.start()
                passed.append(fwd)
        for t in range(nt):
            blk = block(t, x, y, 1 - c)
            _remote(blk, blk, send_sems.at[t, 0], recv_sems.at[t, 0], sibling).wait_recv()
            for j, chip in enumerate(chips):
                blk = block(t, *chip, 1 - c)
                _remote(blk, blk, send_sems.at[t, 4 + j], recv_sems.at[t, 4 + j], sibling).wait_recv()
        for cp in first + passed:
            cp.wait_send()
        for cp in mine:
            cp.wait()

    return pl.pallas_call(
        body,
        name=name,
        in_specs=[ANY] * nt,
        out_specs=[ANY] * nt,
        out_shape=[jax.ShapeDtypeStruct((N_DEV, *s.shape), s.dtype) for s in shards],
        scratch_shapes=[
            pltpu.SemaphoreType.DMA((nt, N_DEV - 1)),
            pltpu.SemaphoreType.DMA((nt, N_DEV - 1)),
            pltpu.SemaphoreType.DMA((nt,)),
        ],
    )(*shards)


def reduce_scatter_parts(parts, *, name):
    nt = len(parts)

    def body(*refs):
        ins, outs = refs[:nt], refs[nt : 2 * nt]
        send_sems, recv_sems, local_sems = refs[2 * nt :]
        x, y, c, me = _position()
        mine, sends = [], []
        for t in range(nt):
            cp = pltpu.make_async_copy(ins[t].at[me], outs[t].at[me], local_sems.at[t])
            cp.start()
            mine.append(cp)
            for k in range(1, N_DEV):
                dev, pid = _peer(k, x, y, c)
                cp = _remote(ins[t].at[pid], outs[t].at[me], send_sems.at[t, k], recv_sems.at[t, k], dev)
                cp.start()
                sends.append(cp)
        for t in range(nt):
            for k in range(1, N_DEV):
                dev, pid = _peer(k, x, y, c)
                _remote(ins[t].at[pid], outs[t].at[pid], send_sems.at[t, k], recv_sems.at[t, k], dev).wait_recv()
        for cp in sends:
            cp.wait_send()
        for cp in mine:
            cp.wait()

    return pl.pallas_call(
        body,
        name=name,
        in_specs=[ANY] * nt,
        out_specs=[ANY] * nt,
        out_shape=[jax.ShapeDtypeStruct(p.shape, p.dtype) for p in parts],
        scratch_shapes=[
            pltpu.SemaphoreType.DMA((nt, N_DEV)),
            pltpu.SemaphoreType.DMA((nt, N_DEV)),
            pltpu.SemaphoreType.DMA((nt,)),
        ],
    )(*parts)


def _adam_update(g, w, m, v):
    m2 = ADAM_B1 * m + (1.0 - ADAM_B1) * g
    v2 = ADAM_B2 * v + (1.0 - ADAM_B2) * jnp.square(g)
    m_hat = m2 / (1.0 - ADAM_B1**ADAM_STEP)
    v_hat = v2 / (1.0 - ADAM_B2**ADAM_STEP)
    delta = -ADAM_LR * (m_hat / (jnp.sqrt(v_hat) + ADAM_EPS) + ADAM_WD * w)
    return delta, m2, v2


def adamw(gparts, w, m, v, *, name):
    p, r, c = gparts.shape
    tr = r if r <= 256 else (256 if c <= D_MODEL else 128)

    def body(g_ref, w_ref, m_ref, v_ref, go_ref, d_ref, mo_ref, vo_ref):
        g = g_ref[0].astype(F32)
        for i in range(1, p):
            g = g + g_ref[i].astype(F32)
        delta, m2, v2 = _adam_update(g, w_ref[...], m_ref[...], v_ref[...])
        go_ref[...] = g
        d_ref[...] = delta
        mo_ref[...] = m2
        vo_ref[...] = v2

    blk = pl.BlockSpec((tr, c), lambda i: (i, 0))
    return pl.pallas_call(
        body,
        name=name,
        grid=(r // tr,),
        in_specs=[pl.BlockSpec((p, tr, c), lambda i: (0, i, 0)), blk, blk, blk],
        out_specs=[blk] * 4,
        out_shape=[jax.ShapeDtypeStruct((r, c), F32)] * 4,
        compiler_params=_params(("parallel",)),
    )(gparts, w, m, v)


def ada_grad_adamw(cg, dmrecv, w4, m4, v4, *, name):
    nt, k, ncol = w4.shape

    def body(cg_ref, dm_ref, w_ref, m_ref, v_ref, go_ref, d_ref, mo_ref, vo_ref, gb_ref):
        sc = _silu_bf16(cg_ref[...].reshape(N_DEV * SUBLANES, k))
        dm = dm_ref[...].reshape(N_DEV * SUBLANES, ncol)
        g = lax.dot_general(sc, dm.astype(BF16), (((0,), (0,)), ((), ())), preferred_element_type=F32)
        delta, m2, v2 = _adam_update(g, w_ref[...], m_ref[...], v_ref[...])
        go_ref[...] = g
        d_ref[...] = delta
        mo_ref[...] = m2
        vo_ref[...] = v2
        gb_ref[...] = jnp.broadcast_to(jnp.sum(dm, axis=0, keepdims=True), (SUBLANES, ncol))

    wblk = pl.BlockSpec((None, k, ncol), lambda t: (t, 0, 0))
    return pl.pallas_call(
        body,
        name=name,
        grid=(nt,),
        in_specs=[
            pl.BlockSpec((N_DEV, SUBLANES, k), lambda t: (0, 0, 0)),
            pl.BlockSpec((N_DEV, None, SUBLANES, ncol), lambda t: (0, t, 0, 0)),
            wblk,
            wblk,
            wblk,
        ],
        out_specs=[wblk] * 4 + [pl.BlockSpec((None, SUBLANES, ncol), lambda t: (t, 0, 0))],
        out_shape=[jax.ShapeDtypeStruct((nt, k, ncol), F32)] * 4 + [jax.ShapeDtypeStruct((nt, SUBLANES, ncol), F32)],
        compiler_params=_params(("parallel",)),
    )(cg, dmrecv, w4, m4, v4)


def kernel(x, c, ada_w, ada_b, ln_g, ln_b, a_w_in, a_b_in, a_vn_g, a_vn_b, a_w_s, a_b_s, a_w_out, b_w_qkv, b_w_out, mlp_w_up, mlp_w_down, loss_target, m_ada_w, m_ada_b, m_ln_g, m_ln_b, m_a_w_in, m_a_b_in, m_a_vn_g, m_a_vn_b, m_a_w_s, m_a_b_s, m_a_w_out, m_b_w_qkv, m_b_w_out, m_mlp_w_up, m_mlp_w_down, v_ada_w, v_ada_b, v_ln_g, v_ln_b, v_a_w_in, v_a_b_in, v_a_vn_g, v_a_vn_b, v_a_w_s, v_a_b_s, v_a_w_out, v_b_w_qkv, v_b_w_out, v_mlp_w_up, v_mlp_w_down):
    x0 = x[0]
    target = loss_target[0]

    ada_w4 = ada_w.reshape(N_SUB, D_MODEL, -1)
    ada_b4 = ada_b.reshape(N_SUB, -1)
    ln8 = jnp.concatenate([ln_g.reshape(N_SUB, -1), ln_b.reshape(N_SUB, -1)], axis=0)
    c8 = jnp.broadcast_to(c, (SUBLANES, D_MODEL))
    cg, lng, mrecv = ada_exchange(c8, ada_w4, ada_b4, ln8)
    modv = mrecv[:, :, 0, :].transpose(1, 0, 2).reshape(N_SUB, 3 * D_MODEL)
    shift = [modv[t : t + 1, :D_MODEL] for t in range(N_SUB)]
    scale = [modv[t : t + 1, D_MODEL : 2 * D_MODEL] for t in range(N_SUB)]
    gate1 = [1.0 + modv[t : t + 1, 2 * D_MODEL :] for t in range(N_SUB)]
    lng_full = [lng[:, t, :].reshape(1, D_MODEL) for t in range(N_SUB)]
    lnb_full = [lng[:, N_SUB + t, :].reshape(1, D_MODEL) for t in range(N_SUB)]

    shards = [
        a_w_in[0].astype(BF16),
        a_w_out[0].astype(BF16),
        b_w_qkv[0].astype(BF16),
        b_w_out[0].astype(BF16),
        mlp_w_up[0].astype(BF16),
        mlp_w_up[1].astype(BF16),
        mlp_w_down[0].astype(BF16),
        mlp_w_down[1].astype(BF16),
    ]
    gathered = all_gather_shards(shards, name="gather_weights")
    w_in3 = gathered[0]
    w_aout3 = gathered[1].reshape(1, D_MODEL, D_MODEL)
    w_qkv3 = gathered[2]
    w_bout3 = gathered[3].reshape(1, D_MODEL, D_MODEL)
    w_up3 = [gathered[4], gathered[5]]
    w_dn3 = [gathered[6].reshape(1, D_FF, D_MODEL), gathered[7].reshape(1, D_FF, D_MODEL)]

    ident = lambda acc: (acc,)
    relu2 = lambda acc: (acc, jnp.square(jnp.maximum(acc, 0.0)))
    vn_g, vn_b, w_s = a_vn_g, a_vn_b, a_w_s[0]
    bias_full = jnp.repeat(a_b_s[0].T, A_GROUP_DIM, axis=1)

    def mlp_forward(i, h):
        a, r = mm_nn(h, w_up3[i], name=f"mlp{i}_up", tm=1024, tn=512, epilogue=relu2, outs=(BF16, BF16))
        (y,) = mm_nn(r, w_dn3[i], name=f"mlp{i}_down", tm=512, tn=512, epilogue=ident, outs=(F32,))
        return a, r, y

    h0 = modulate(x0, scale[0], shift[0], name="modulate0")
    (a_pre,) = mm_nn(
        h0, w_in3, name="a_in", tm=1024, tn=256, epilogue=lambda acc, b: (acc + b,), extras=[(a_b_in, "row")], outs=(F32,)
    )
    p_gate = gate_fwd(a_pre, vn_g, vn_b, w_s, bias_full, name="gate_fwd")
    (y0,) = mm_nn(p_gate, w_aout3, name="a_out", tm=1024, tn=512, epilogue=ident, outs=(F32,))
    x1, h1 = residual_ln(x0, y0, gate1[0], lng_full[0], lnb_full[0], scale[1], shift[1], name="res_ln0")
    a1, r1, y1 = mlp_forward(0, h1)
    x2, h2 = residual_ln(x1, y1, gate1[1], lng_full[1], lnb_full[1], scale[2], shift[2], name="res_ln1")
    (qkv,) = mm_nn(h2, w_qkv3, name="b_qkv", tm=1024, tn=1152, epilogue=ident, outs=(BF16,))
    pat_o, pat_lse = [], []
    for pat in range(N_PAT):
        o_p, lse_p = attn_fwd(qkv, pat, name=f"attn_fwd{pat}")
        pat_o.append(o_p.reshape(SEQ, D_MODEL))
        pat_lse.append(lse_p.reshape(SEQ, LANES))
    o_b, o_f, lse = attn_combine(pat_o, pat_lse, name="attn_combine")
    (y2,) = mm_nn(o_b, w_bout3, name="b_out", tm=1024, tn=512, epilogue=ident, outs=(F32,))
    x3, h3 = residual_ln(x2, y2, gate1[2], lng_full[2], lnb_full[2], scale[3], shift[3], name="res_ln2")
    a3, r3, y3 = mlp_forward(1, h3)
    dxo, loss_local = residual_ln_loss(x3, y3, gate1[3], lng_full[3], lnb_full[3], target, name="res_ln3_loss")
    loss = lax.psum(loss_local[0, 0], ("x", "y", "c"))

    def mlp_backward(i, h, a, r, dy):
        dw_dn = mm_tn(r, dy, name=f"mlp{i}_dw_down", p=1, tk=1024, tn=512, tmc=1024, out_dtype=BF16)
        (da,) = mm_nt(
            dy,
            w_dn3[i],
            name=f"mlp{i}_da",
            tm=1024,
            tko=1024,
            ps=1,
            epilogue=lambda acc, act: (acc * (2.0 * jnp.maximum(act.astype(F32), 0.0)),),
            extras=[(a, "full")],
            outs=(BF16,),
        )
        dw_up = mm_tn(h, da, name=f"mlp{i}_dw_up", p=N_DEV, tk=1024, tn=512, tmc=1024, out_dtype=BF16)
        (dh,) = mm_nt(da, w_up3[i], name=f"mlp{i}_dh", tm=1024, tko=1024, ps=2, epilogue=ident, outs=(F32,))
        return dw_dn, dw_up, dh

    dz3, dy3, st3 = residual_ln_bwd(name="res_bwd3", dxo=dxo, this=(x3, y3, gate1[3], lng_full[3]))
    dw_dn1, dw_up1, dh3 = mlp_backward(1, h3, a3, r3, dy3)
    dz2, dy2, st2 = residual_ln_bwd(
        name="res_bwd2", later=(dh3, dz3, x3, scale[3]), this=(x2, y2, gate1[2], lng_full[2])
    )
    dw_bout = mm_tn(o_b, dy2, name="b_dw_out", p=1, tk=1024, tn=512, tmc=1024, out_dtype=BF16)
    (d_o,) = mm_nt(dy2, w_bout3, name="b_do", tm=1024, tko=1024, ps=1, epilogue=ident, outs=(F32,))
    do_b, delta = attn_delta(d_o, o_f, name="attn_delta")
    dqkv_parts = []
    for pat in range(N_PAT):
        dqkv_parts += attn_bwd(qkv, do_b, lse, delta, pat, name=f"attn_bwd{pat}")
    dqkv = jnp.concatenate(dqkv_parts, axis=1)
    dw_qkv = mm_tn(h2, dqkv, name="b_dw_qkv", p=N_DEV, tk=512, tn=1152, tmc=1024, out_dtype=BF16)
    (dh2,) = mm_nt(dqkv, w_qkv3, name="b_dh", tm=1024, tko=1024, ps=1, epilogue=ident, outs=(F32,))
    dz1, dy1, st1 = residual_ln_bwd(
        name="res_bwd1", later=(dh2, dz2, x2, scale[2]), this=(x1, y1, gate1[1], lng_full[1])
    )
    dw_dn0, dw_up0, dh1 = mlp_backward(0, h1, a1, r1, dy1)
    dz0, dy0, st0 = residual_ln_bwd(
        name="res_bwd0", later=(dh1, dz1, x1, scale[1]), this=(x0, y0, gate1[0], lng_full[0])
    )
    dw_aout = mm_tn(p_gate, dy0, name="a_dw_out", p=1, tk=1024, tn=512, tmc=1024, out_dtype=BF16)
    (dp_gate,) = mm_nt(dy0, w_aout3, name="a_dp", tm=1024, tko=1024, ps=1, epilogue=ident, outs=(F32,))
    da0, d_ws, d_bs, gate_rows = gate_bwd(a_pre, dp_gate, vn_g, vn_b, w_s, bias_full, name="gate_bwd")
    dw_in = mm_tn(h0, da0, name="a_dw_in", p=N_DEV, tk=1024, tn=256, tmc=1024, out_dtype=BF16)
    (dh0,) = mm_nt(da0, w_in3, name="a_dh", tm=1024, tko=1024, ps=4, epilogue=ident, outs=(F32,))
    grad_x, stf = residual_ln_bwd(name="res_bwd_in", later=(dh0, dz0, x0, scale[0]))

    stats_after = [stf, st0, st1, st2]
    stats_own = [st0, st1, st2, st3]
    dm = jnp.stack(
        [
            jnp.concatenate(
                [stats_after[t][ST_DSHIFT], stats_after[t][ST_DSCALE], stats_own[t][ST_DGATE]], axis=0
            )
            for t in range(N_SUB)
        ]
    )
    ncol = 3 * D_MODEL // N_DEV
    dmx = jnp.pad(
        dm.reshape(N_SUB, N_DEV, ncol).transpose(1, 0, 2)[:, :, None, :], ((0, 0), (0, 0), (0, SUBLANES - 1), (0, 0))
    )
    small = [
        gate_rows[0],
        gate_rows[1],
        d_ws.reshape(-1),
        d_bs[:, :A_GROUPS].T.reshape(-1),
        *[stats_own[t][ST_DG] for t in range(N_SUB)],
        *[stats_own[t][ST_DB] for t in range(N_SUB)],
    ]
    n_small = sum(s.size for s in small)
    part_rows = -(-n_small // (N_DEV * LANES * SUBLANES)) * SUBLANES
    flat = jnp.concatenate(small + [jnp.zeros((N_DEV * part_rows * LANES - n_small,), F32)])
    dmrecv, reduced = small_exchange(dmx, flat.reshape(N_DEV, part_rows, LANES))
    reduced = reduced.reshape(-1)
    sizes = [2 * D_MODEL, D_MODEL, D_MODEL, A_GROUPS * CHUNK * CHUNK, A_GROUPS * CHUNK, N_SUB * D_MODEL, N_SUB * D_MODEL]
    offs = [sum(sizes[:i]) for i in range(len(sizes) + 1)]
    g_b_in, g_vn_g, g_vn_b, g_ws, g_bs, g_lng, g_lnb = [reduced[offs[i] : offs[i + 1]] for i in range(len(sizes))]

    parts = [dw_in, dw_aout.reshape(N_DEV, -1, D_MODEL), dw_qkv, dw_bout.reshape(N_DEV, -1, D_MODEL), dw_up0, dw_up1,
             dw_dn0.reshape(N_DEV, -1, D_MODEL), dw_dn1.reshape(N_DEV, -1, D_MODEL)]
    recv = reduce_scatter_parts(parts, name="scatter_grads")

    results = {}

    def update(wname, gparts, w, m, v):
        shape = w.shape
        w2 = w.reshape(-1, shape[-1])
        outs = adamw(gparts.reshape(gparts.shape[0], *w2.shape), w2, m.reshape(w2.shape), v.reshape(w2.shape), name=f"adamw_{wname}")
        results[wname] = [o.reshape(shape) for o in outs]

    ada_outs = ada_grad_adamw(cg, dmrecv, ada_w4, m_ada_w.reshape(ada_w4.shape), v_ada_w.reshape(ada_w4.shape), name="ada_grad_adamw")
    results["ada_w"] = [o.reshape(ada_w.shape) for o in ada_outs[:4]]
    update("ada_b", ada_outs[4][:, 0, :][None], ada_b, m_ada_b, v_ada_b)
    me = 4 * lax.axis_index("x") + 2 * lax.axis_index("y") + lax.axis_index("c")
    ln_cols = D_MODEL // N_DEV
    my_ln = lambda gfull: lax.dynamic_slice_in_dim(gfull.reshape(N_SUB, N_DEV, ln_cols), me, 1, axis=1)
    update("ln_g", my_ln(g_lng).reshape(1, N_SUB, ln_cols), ln_g, m_ln_g, v_ln_g)
    update("ln_b", my_ln(g_lnb).reshape(1, N_SUB, ln_cols), ln_b, m_ln_b, v_ln_b)
    update("a_w_in", recv[0], a_w_in, m_a_w_in, v_a_w_in)
    update("a_b_in", g_b_in[None], a_b_in, m_a_b_in, v_a_b_in)
    update("a_vn_g", g_vn_g[None], a_vn_g, m_a_vn_g, v_a_vn_g)
    update("a_vn_b", g_vn_b[None], a_vn_b, m_a_vn_b, v_a_vn_b)
    update("a_w_s", g_ws[None], a_w_s, m_a_w_s, v_a_w_s)
    update("a_b_s", g_bs[None], a_b_s, m_a_b_s, v_a_b_s)
    update("a_w_out", recv[1], a_w_out, m_a_w_out, v_a_w_out)
    update("b_w_qkv", recv[2], b_w_qkv, m_b_w_qkv, v_b_w_qkv)
    update("b_w_out", recv[3], b_w_out, m_b_w_out, v_b_w_out)
    update("mlp_w_up", jnp.concatenate([recv[4], recv[5]], axis=1), mlp_w_up, m_mlp_w_up, v_mlp_w_up)
    update("mlp_w_down", jnp.concatenate([recv[6], recv[7]], axis=1), mlp_w_down, m_mlp_w_down, v_mlp_w_down)

    order = ["ada_w", "ada_b", "ln_g", "ln_b", "a_w_in", "a_b_in", "a_vn_g", "a_vn_b", "a_w_s", "a_b_s", "a_w_out", "b_w_qkv", "b_w_out", "mlp_w_up", "mlp_w_down"]
    return (loss, grad_x[None], *[results[n][0] for n in order], *[results[n][1] for n in order],
            *[results[n][2] for n in order], *[results[n][3] for n in order])
```

```python
import math

import jax
import jax.numpy as jnp
from jax import lax
from jax.experimental import pallas as pl
from jax.experimental.pallas import tpu as pltpu

F32 = jnp.float32
BF16 = jnp.bfloat16
MESH = pl.DeviceIdType.MESH
ANY = pl.BlockSpec(memory_space=pl.ANY)
VMEM = pl.BlockSpec(memory_space=pltpu.VMEM)

N_DEV = 8
D_MODEL = 1024
SEQ = 4096
DEPTH = 2
CHUNK = 128
A_GROUPS = 16
A_GROUP_DIM = D_MODEL // A_GROUPS
B_HEADS = 16
B_HEAD_DIM = 64
B_PATTERNS = ((128, 1), (512, 4), (2048, 16))
N_PAT = len(B_PATTERNS)
SPAN = 128
D_FF = 4 * D_MODEL
D_QKV = N_PAT * 3 * D_MODEL
ALPHA = (2 * DEPTH) ** 0.25
LN_EPS = 1e-5
NEG = -1e30
ADAM_LR = 0.001
ADAM_B1 = 0.9
ADAM_B2 = 0.999
ADAM_EPS = 1e-08
ADAM_WD = 0.01
ADAM_STEP = 10
GELU_C = math.sqrt(2.0 / math.pi)
GELU_A = 0.044715

VMEM_LIMIT_BYTES = 56 * 1024 * 1024
LANES = 128
SUBLANES = 8
ROW_TILE = 512
N_SUB = 2 * DEPTH


def _params(sem):
    return pltpu.CompilerParams(dimension_semantics=sem, vmem_limit_bytes=VMEM_LIMIT_BYTES)


def _lane(shape):
    return lax.broadcasted_iota(jnp.int32, shape, len(shape) - 1)


def _split_bf16(x):
    hi = x.astype(BF16)
    lo = (x - hi.astype(F32)).astype(BF16)
    return hi, lo


def _group_expand_matrix(groups_padded, width):
    per = width // A_GROUPS
    r = lax.broadcasted_iota(jnp.int32, (groups_padded, width), 0)
    c = lax.broadcasted_iota(jnp.int32, (groups_padded, width), 1)
    return (c // per == r).astype(BF16)


def _group_reduce_matrix(width, groups_padded):
    per = width // A_GROUPS
    r = lax.broadcasted_iota(jnp.int32, (width, groups_padded), 0)
    c = lax.broadcasted_iota(jnp.int32, (width, groups_padded), 1)
    return (r // per == c).astype(BF16)


def _expand_groups(w):
    e = _group_expand_matrix(LANES, D_MODEL)
    hi, lo = _split_bf16(w)
    return jnp.dot(hi, e, preferred_element_type=F32) + jnp.dot(lo, e, preferred_element_type=F32)


def _reduce_groups(x):
    e = _group_reduce_matrix(D_MODEL, LANES)
    hi, lo = _split_bf16(x)
    return jnp.dot(hi, e, preferred_element_type=F32) + jnp.dot(lo, e, preferred_element_type=F32)


def mm_nn(a, b3, *, name, tm, tn, epilogue, extras=(), outs):
    m, k = a.shape
    p, _, n = b3.shape
    q = n // tn
    grid = (m // tm, p * q)

    def body(a_ref, b_ref, *rest):
        ex = rest[: len(extras)]
        out_refs = rest[len(extras) :]
        acc = jnp.dot(a_ref[...], b_ref[...], preferred_element_type=F32)
        res = epilogue(acc, *[e[...] for e in ex])
        for o_ref, r in zip(out_refs, res, strict=True):
            o_ref[...] = r.astype(o_ref.dtype)

    ex_specs = []
    for _, kind in extras:
        if kind == "row":
            ex_specs.append(pl.BlockSpec((1, tn), lambda i, j: (0, j)))
        else:
            ex_specs.append(pl.BlockSpec((tm, tn), lambda i, j: (i, j)))
    return pl.pallas_call(
        body,
        name=name,
        grid=grid,
        in_specs=[
            pl.BlockSpec((tm, k), lambda i, j: (i, 0)),
            pl.BlockSpec((None, k, tn), lambda i, j: (j // q, 0, j % q)),
            *ex_specs,
        ],
        out_specs=[pl.BlockSpec((tm, tn), lambda i, j: (i, j)) for _ in outs],
        out_shape=[jax.ShapeDtypeStruct((m, p * n), dt) for dt in outs],
        compiler_params=_params(("parallel", "parallel")),
    )(a, b3, *[arr for arr, _ in extras])


def mm_nt(g, b3, *, name, tm, tko, ps, epilogue, extras=(), outs):
    m = g.shape[0]
    p, k, n = b3.shape
    nc = p // ps
    grid = (m // tm, k // tko, nc)

    def body(g_ref, b_ref, *rest):
        ex = rest[: len(extras)]
        out_refs = rest[len(extras) : len(extras) + len(outs)]
        acc_ref = rest[-1]
        c = pl.program_id(2)
        part = None
        for s in range(ps):
            d = lax.dot_general(
                g_ref[:, s * n : (s + 1) * n], b_ref[s], (((1,), (1,)), ((), ())), preferred_element_type=F32
            )
            part = d if part is None else part + d

        @pl.when(c == 0)
        def _():
            acc_ref[...] = part

        @pl.when(c > 0)
        def _():
            acc_ref[...] += part

        @pl.when(c == nc - 1)
        def _():
            res = epilogue(acc_ref[...], *[e[...] for e in ex])
            for o_ref, r in zip(out_refs, res, strict=True):
                o_ref[...] = r.astype(o_ref.dtype)

    ex_specs = []
    for _, kind in extras:
        if kind == "row":
            ex_specs.append(pl.BlockSpec((1, tko), lambda i, j, c: (0, j)))
        else:
            ex_specs.append(pl.BlockSpec((tm, tko), lambda i, j, c: (i, j)))
    return pl.pallas_call(
        body,
        name=name,
        grid=grid,
        in_specs=[
            pl.BlockSpec((tm, ps * n), lambda i, j, c: (i, c)),
            pl.BlockSpec((ps, tko, n), lambda i, j, c: (c, j, 0)),
            *ex_specs,
        ],
        out_specs=[pl.BlockSpec((tm, tko), lambda i, j, c: (i, j)) for _ in outs],
        out_shape=[jax.ShapeDtypeStruct((m, k), dt) for dt in outs],
        scratch_shapes=[pltpu.VMEM((tm, tko), F32)],
        compiler_params=_params(("parallel", "parallel", "arbitrary")),
    )(g, b3, *[arr for arr, _ in extras])


def mm_tn(a, g, *, name, p, tk, tn, tmc, out_dtype):
    m, k = a.shape
    n = g.shape[1] // p
    q = n // tn
    nc = m // tmc
    grid = (k // tk, p * q, nc)

    def body(a_ref, g_ref, o_ref, acc_ref):
        c = pl.program_id(2)
        part = lax.dot_general(a_ref[...], g_ref[...], (((0,), (0,)), ((), ())), preferred_element_type=F32)

        @pl.when(c == 0)
        def _():
            acc_ref[...] = part

        @pl.when(c > 0)
        def _():
            acc_ref[...] += part

        @pl.when(c == nc - 1)
        def _():
            o_ref[...] = acc_ref[...].astype(o_ref.dtype)

    return pl.pallas_call(
        body,
        name=name,
        grid=grid,
        in_specs=[
            pl.BlockSpec((tmc, tk), lambda i, j, c: (c, i)),
            pl.BlockSpec((tmc, tn), lambda i, j, c: (c, j)),
        ],
        out_specs=pl.BlockSpec((None, tk, tn), lambda i, j, c: (j // q, i, j % q)),
        out_shape=jax.ShapeDtypeStruct((p, k, n), out_dtype),
        scratch_shapes=[pltpu.VMEM((tk, tn), F32)],
        compiler_params=_params(("parallel", "parallel", "arbitrary")),
    )(a, g)


def _rows(cols):
    return pl.BlockSpec((ROW_TILE, cols), lambda i: (i, 0))


def _vec(cols, rows=1):
    return pl.BlockSpec((rows, cols), lambda i: (0, 0))


def _layer_norm_hat(z):
    mu = jnp.mean(z, axis=-1, keepdims=True)
    zc = z - mu
    var = jnp.mean(zc * zc, axis=-1, keepdims=True)
    rstd = lax.rsqrt(var + LN_EPS)
    return zc * rstd, rstd


def modulate(x, scale, shift, *, name):
    s, d = x.shape

    def body(x_ref, sc_ref, sh_ref, h_ref):
        h_ref[...] = (x_ref[...] * (1.0 + sc_ref[...]) + sh_ref[...]).astype(BF16)

    return pl.pallas_call(
        body,
        name=name,
        grid=(s // ROW_TILE,),
        in_specs=[_rows(d), _vec(d), _vec(d)],
        out_specs=_rows(d),
        out_shape=jax.ShapeDtypeStruct((s, d), BF16),
        compiler_params=_params(("parallel",)),
    )(x, scale, shift)


def residual_ln(x, y, gate1, g, b, nscale, nshift, *, name):
    s, d = x.shape

    def body(x_ref, y_ref, gt_ref, g_ref, b_ref, sc_ref, sh_ref, xn_ref, hn_ref):
        z = ALPHA * x_ref[...] + gt_ref[...] * y_ref[...]
        xhat, _ = _layer_norm_hat(z)
        xn = xhat * g_ref[...] + b_ref[...]
        xn_ref[...] = xn
        hn_ref[...] = (xn * (1.0 + sc_ref[...]) + sh_ref[...]).astype(BF16)

    return pl.pallas_call(
        body,
        name=name,
        grid=(s // ROW_TILE,),
        in_specs=[_rows(d), _rows(d), _vec(d), _vec(d), _vec(d), _vec(d), _vec(d)],
        out_specs=[_rows(d), _rows(d)],
        out_shape=[jax.ShapeDtypeStruct((s, d), F32), jax.ShapeDtypeStruct((s, d), BF16)],
        compiler_params=_params(("parallel",)),
    )(x, y, gate1, g, b, nscale, nshift)


def residual_ln_loss(x, y, gate1, g, b, target, *, name):
    s, d = x.shape

    def body(x_ref, y_ref, gt_ref, g_ref, b_ref, t_ref, dx_ref, loss_ref):
        z = ALPHA * x_ref[...] + gt_ref[...] * y_ref[...]
        xhat, _ = _layer_norm_hat(z)
        err = xhat * g_ref[...] + b_ref[...] - t_ref[...]
        dx_ref[...] = err * (1.0 / d)
        part = jnp.sum(jnp.sum(err * err, axis=1, keepdims=True), axis=0, keepdims=True) * (0.5 / d)

        @pl.when(pl.program_id(0) == 0)
        def _():
            loss_ref[...] = part

        @pl.when(pl.program_id(0) > 0)
        def _():
            loss_ref[...] += part

    return pl.pallas_call(
        body,
        name=name,
        grid=(s // ROW_TILE,),
        in_specs=[_rows(d), _rows(d), _vec(d), _vec(d), _vec(d), _rows(d)],
        out_specs=[_rows(d), pl.BlockSpec((1, 1), lambda i: (0, 0))],
        out_shape=[jax.ShapeDtypeStruct((s, d), F32), jax.ShapeDtypeStruct((1, 1), F32)],
        compiler_params=_params(("arbitrary",)),
    )(x, y, gate1, g, b, target)


ST_DSCALE, ST_DSHIFT, ST_DG, ST_DB, ST_DGATE = 0, 1, 2, 3, 4


def residual_ln_bwd(*, name, later=None, dxo=None, this=None):
    lead = later[0] if later is not None else dxo
    s, d = lead.shape
    has_later, has_ln = later is not None, this is not None

    def body(*refs):
        refs = list(refs)
        if has_later:
            dh_ref, dzl_ref, xm_ref, scl_ref = refs[:4]
            refs = refs[4:]
        else:
            dxo_ref = refs.pop(0)
        if has_ln:
            x_ref, y_ref, gt_ref, g_ref = refs[:4]
            refs = refs[4:]
            dz_ref, dy_ref, st_ref = refs
        else:
            dx_ref, st_ref = refs

        @pl.when(pl.program_id(0) == 0)
        def _():
            st_ref[...] = jnp.zeros_like(st_ref)

        def acc(row, val):
            st_ref[row : row + 1, :] += jnp.sum(val, axis=0, keepdims=True)

        if has_later:
            dh = dh_ref[...]
            g_out = ALPHA * dzl_ref[...] + dh * (1.0 + scl_ref[...])
            acc(ST_DSCALE, dh * xm_ref[...])
            acc(ST_DSHIFT, dh)
        else:
            g_out = dxo_ref[...]
        if not has_ln:
            dx_ref[...] = g_out
            return
        y = y_ref[...]
        gate1 = gt_ref[...]
        z = ALPHA * x_ref[...] + gate1 * y
        xhat, rstd = _layer_norm_hat(z)
        acc(ST_DG, g_out * xhat)
        acc(ST_DB, g_out)
        dxh = g_out * g_ref[...]
        m1 = jnp.mean(dxh, axis=-1, keepdims=True)
        m2 = jnp.mean(dxh * xhat, axis=-1, keepdims=True)
        dz = rstd * (dxh - m1 - xhat * m2)
        acc(ST_DGATE, dz * y)
        dz_ref[...] = dz
        dy_ref[...] = (dz * gate1).astype(BF16)

    ins, specs = [], []
    if has_later:
        ins += list(later)
        specs += [_rows(d), _rows(d), _rows(d), _vec(d)]
    else:
        ins += [dxo]
        specs += [_rows(d)]
    if has_ln:
        ins += list(this)
        specs += [_rows(d), _rows(d), _vec(d), _vec(d)]
        out_specs = [_rows(d), _rows(d), _vec(d, SUBLANES)]
        out_shape = [
            jax.ShapeDtypeStruct((s, d), F32),
            jax.ShapeDtypeStruct((s, d), BF16),
            jax.ShapeDtypeStruct((SUBLANES, d), F32),
        ]
    else:
        out_specs = [_rows(d), _vec(d, SUBLANES)]
        out_shape = [jax.ShapeDtypeStruct((s, d), F32), jax.ShapeDtypeStruct((SUBLANES, d), F32)]
    return pl.pallas_call(
        body,
        name=name,
        grid=(s // ROW_TILE,),
        in_specs=specs,
        out_specs=out_specs,
        out_shape=out_shape,
        compiler_params=_params(("arbitrary",)),
    )(*ins)


GATE_CHUNKS = 4


def _gelu(x):
    return 0.5 * x * (1.0 + jnp.tanh(GELU_C * (x + GELU_A * x * x * x)))


def _gelu_grad(x):
    t = jnp.tanh(GELU_C * (x + GELU_A * x * x * x))
    return 0.5 * (1.0 + t) + 0.5 * x * (1.0 - t * t) * (GELU_C * (1.0 + 3.0 * GELU_A * x * x))


def _causal_weights(w_ref, transpose):
    t = lax.broadcasted_iota(jnp.int32, (CHUNK, CHUNK), 0)
    s = lax.broadcasted_iota(jnp.int32, (CHUNK, CHUNK), 1)
    out = []
    for g in range(A_GROUPS):
        w = jnp.where(t >= s, w_ref[g], 0.0)
        out.append((w.T if transpose else w).astype(BF16))
    return out


def _spatial(ws, vn, lo_mask):
    rows = vn.shape[0]
    out_rows = []
    for r in range(rows // CHUNK):
        cols = []
        for j in range(A_GROUPS // 2):
            blk = vn[r * CHUNK : (r + 1) * CHUNK, j * LANES : (j + 1) * LANES]
            za = jnp.dot(ws[2 * j], blk, preferred_element_type=F32)
            zb = jnp.dot(ws[2 * j + 1], blk, preferred_element_type=F32)
            cols.append(jnp.where(lo_mask, za, zb))
        out_rows.append(jnp.concatenate(cols, axis=1))
    return jnp.concatenate(out_rows, axis=0)


def _gate_forward(a, vg, vb, ws, bias, lo_mask):
    u = _gelu(a[:, :D_MODEL])
    v = _gelu(a[:, D_MODEL:])
    vhat, rstd = _layer_norm_hat(v)
    vn = (vhat * vg + vb).astype(BF16)
    z = _spatial(ws, vn, lo_mask) + jnp.concatenate([bias] * (a.shape[0] // CHUNK), axis=0)
    return u, vhat, rstd, vn, z


def gate_fwd(a_pre, vn_g, vn_b, w_s, bias_full, *, name):
    s = a_pre.shape[0]
    tr = GATE_CHUNKS * CHUNK

    def body(a_ref, vg_ref, vb_ref, w_ref, bias_ref, p_ref):
        lo_mask = _lane((CHUNK, LANES)) < A_GROUP_DIM
        ws = _causal_weights(w_ref, transpose=False)
        u, _, _, _, z = _gate_forward(a_ref[...], vg_ref[...], vb_ref[...], ws, bias_ref[...], lo_mask)
        p_ref[...] = (u * z).astype(BF16)

    return pl.pallas_call(
        body,
        name=name,
        grid=(s // tr,),
        in_specs=[
            pl.BlockSpec((tr, 2 * D_MODEL), lambda i: (i, 0)),
            _vec(D_MODEL),
            _vec(D_MODEL),
            pl.BlockSpec((A_GROUPS, CHUNK, CHUNK), lambda i: (0, 0, 0)),
            _vec(D_MODEL, CHUNK),
        ],
        out_specs=pl.BlockSpec((tr, D_MODEL), lambda i: (i, 0)),
        out_shape=jax.ShapeDtypeStruct((s, D_MODEL), BF16),
        compiler_params=_params(("parallel",)),
    )(a_pre, vn_g, vn_b, w_s, bias_full)


def gate_bwd(a_pre, dp, vn_g, vn_b, w_s, bias_full, *, name):
    s = a_pre.shape[0]
    tr = GATE_CHUNKS * CHUNK
    nsteps = s // tr

    def body(a_ref, dp_ref, vg_ref, vb_ref, w_ref, bias_ref, da_ref, dw_ref, dbs_ref, rows_ref, dbias_acc):
        step = pl.program_id(0)
        lo_mask = _lane((CHUNK, LANES)) < A_GROUP_DIM

        @pl.when(step == 0)
        def _():
            dw_ref[...] = jnp.zeros_like(dw_ref)
            rows_ref[...] = jnp.zeros_like(rows_ref)
            dbias_acc[...] = jnp.zeros_like(dbias_acc)

        a = a_ref[...]
        vg = vg_ref[...]
        ws = _causal_weights(w_ref, transpose=False)
        wts = _causal_weights(w_ref, transpose=True)
        u, vhat, rstd, vn, z = _gate_forward(a, vg, vb_ref[...], ws, bias_ref[...], lo_mask)
        dp = dp_ref[...]
        du = dp * z
        dzz = dp * u
        dzz_b = dzz.astype(BF16)
        dvn = _spatial(wts, dzz_b, lo_mask)
        dbias = None
        for r in range(GATE_CHUNKS):
            rs = slice(r * CHUNK, (r + 1) * CHUNK)
            dbias = dzz[rs] if dbias is None else dbias + dzz[rs]
            for j in range(A_GROUPS // 2):
                cs = slice(j * LANES, (j + 1) * LANES)
                dblk = dzz[rs, cs]
                vblk = vn[rs, cs]
                for half in range(2):
                    keep = lo_mask if half == 0 else jnp.logical_not(lo_mask)
                    dm = jnp.where(keep, dblk, 0.0).astype(BF16)
                    dw_ref[2 * j + half] += lax.dot_general(
                        dm, vblk, (((1,), (1,)), ((), ())), preferred_element_type=F32
                    )
        dbias_acc[...] += dbias
        rows_ref[1:2, :D_MODEL] += jnp.sum(dvn * vhat, axis=0, keepdims=True)
        rows_ref[1:2, D_MODEL:] += jnp.sum(dvn, axis=0, keepdims=True)
        dvh = dvn * vg
        m1 = jnp.mean(dvh, axis=-1, keepdims=True)
        m2 = jnp.mean(dvh * vhat, axis=-1, keepdims=True)
        dv = rstd * (dvh - m1 - vhat * m2)
        da_u = du * _gelu_grad(a[:, :D_MODEL])
        da_v = dv * _gelu_grad(a[:, D_MODEL:])
        da_ref[:, :D_MODEL] = da_u.astype(BF16)
        da_ref[:, D_MODEL:] = da_v.astype(BF16)
        rows_ref[0:1, :D_MODEL] += jnp.sum(da_u, axis=0, keepdims=True)
        rows_ref[0:1, D_MODEL:] += jnp.sum(da_v, axis=0, keepdims=True)

        @pl.when(step == nsteps - 1)
        def _():
            t = lax.broadcasted_iota(jnp.int32, (CHUNK, CHUNK), 0)
            sx = lax.broadcasted_iota(jnp.int32, (CHUNK, CHUNK), 1)
            for g in range(A_GROUPS):
                dw_ref[g] = jnp.where(t >= sx, dw_ref[g], 0.0)
            dbs_ref[...] = _reduce_groups(dbias_acc[...])

    return pl.pallas_call(
        body,
        name=name,
        grid=(nsteps,),
        in_specs=[
            pl.BlockSpec((tr, 2 * D_MODEL), lambda i: (i, 0)),
            pl.BlockSpec((tr, D_MODEL), lambda i: (i, 0)),
            _vec(D_MODEL),
            _vec(D_MODEL),
            pl.BlockSpec((A_GROUPS, CHUNK, CHUNK), lambda i: (0, 0, 0)),
            _vec(D_MODEL, CHUNK),
        ],
        out_specs=[
            pl.BlockSpec((tr, 2 * D_MODEL), lambda i: (i, 0)),
            pl.BlockSpec((A_GROUPS, CHUNK, CHUNK), lambda i: (0, 0, 0)),
            _vec(LANES, CHUNK),
            _vec(2 * D_MODEL, SUBLANES),
        ],
        out_shape=[
            jax.ShapeDtypeStruct((s, 2 * D_MODEL), BF16),
            jax.ShapeDtypeStruct((A_GROUPS, CHUNK, CHUNK), F32),
            jax.ShapeDtypeStruct((CHUNK, LANES), F32),
            jax.ShapeDtypeStruct((SUBLANES, 2 * D_MODEL), F32),
        ],
        scratch_shapes=[pltpu.VMEM((CHUNK, D_MODEL), F32)],
        compiler_params=_params(("arbitrary",)),
    )(a_pre, dp, vn_g, vn_b, w_s, bias_full)


def alibi_tables(dilation):
    qi = jnp.arange(SPAN)[:, None]
    ki = jnp.arange(2 * SPAN)[None, :]
    diff = SPAN + qi - ki
    valid = (diff >= 0) & (diff <= SPAN)
    heads = jnp.arange(1, B_HEADS + 1, dtype=F32)
    slopes = jnp.exp2(-8.0 * heads / B_HEADS)
    bias = -slopes[:, None, None] * (dilation * diff).astype(F32)
    return jnp.where(valid[None], bias, NEG)


def _qkv_specs(pat, dilation, grid_rank_fn):
    cols = D_QKV // D_MODEL
    base = 3 * pat

    def spec(which, prev):
        def index(r, n):
            blk = jnp.maximum(n - 1, 0) if prev else n
            return (grid_rank_fn(blk), r * cols + base + which)

        return pl.BlockSpec((SPAN, D_MODEL), index)

    return [spec(0, False), spec(1, True), spec(1, False), spec(2, True), spec(2, False)]


def attn_fwd(qkv, pat, *, name):
    _, dilation = B_PATTERNS[pat]
    rows = SEQ // dilation
    nb = rows // SPAN
    qkv2 = qkv.reshape(rows, dilation * D_QKV)
    bias = alibi_tables(dilation)

    def body(q_ref, kp_ref, kc_ref, vp_ref, vc_ref, bias_ref, o_ref, lse_ref):
        n = pl.program_id(1)
        lane = _lane((SPAN, LANES))
        lo_mask = lane < B_HEAD_DIM
        first_prev = jnp.logical_and(n == 0, _lane((SPAN, 2 * SPAN)) < SPAN)
        q = q_ref[...] * jnp.asarray(B_HEAD_DIM**-0.5, BF16)
        kk = jnp.concatenate([kp_ref[...], kc_ref[...]], axis=0)
        vv = jnp.concatenate([vp_ref[...], vc_ref[...]], axis=0)
        halves = (lo_mask.astype(BF16), jnp.logical_not(lo_mask).astype(BF16))
        stats = jnp.zeros((SPAN, LANES), F32)
        for j in range(B_HEADS // 2):
            cs = slice(j * LANES, (j + 1) * LANES)
            qp, kp, vp = q[:, cs], kk[:, cs], vv[:, cs]
            outs = []
            for half in range(2):
                h = 2 * j + half
                sc = lax.dot_general(qp * halves[half], kp, (((1,), (1,)), ((), ())), preferred_element_type=F32)
                sc = jnp.where(first_prev, NEG, sc + bias_ref[h])
                m = jnp.max(sc, axis=1, keepdims=True)
                p = jnp.exp(sc - m)
                l = jnp.sum(p, axis=1, keepdims=True)
                acc = jnp.dot(p.astype(BF16), vp, preferred_element_type=F32)
                outs.append(acc * (1.0 / l))
                stats = jnp.where(lane == h, m + jnp.log(l), stats)
            o_ref[:, cs] = jnp.where(lo_mask, outs[0], outs[1])
        lse_ref[...] = stats

    return pl.pallas_call(
        body,
        name=name,
        grid=(dilation, nb),
        in_specs=[
            *_qkv_specs(pat, dilation, lambda blk: blk),
            pl.BlockSpec((B_HEADS, SPAN, 2 * SPAN), lambda r, n: (0, 0, 0)),
        ],
        out_specs=[
            pl.BlockSpec((SPAN, D_MODEL), lambda r, n: (n, r)),
            pl.BlockSpec((SPAN, LANES), lambda r, n: (n, r)),
        ],
        out_shape=[
            jax.ShapeDtypeStruct((rows, dilation * D_MODEL), F32),
            jax.ShapeDtypeStruct((rows, dilation * LANES), F32),
        ],
        compiler_params=_params(("parallel", "arbitrary")),
    )(qkv2, qkv2, qkv2, qkv2, qkv2, bias)


def attn_combine(outs, lses, *, name):
    def body(o0, o1, o2, l0, l1, l2, ob_ref, of_ref, lse_ref):
        ls = [l0[...], l1[...], l2[...]]
        m = jnp.maximum(jnp.maximum(ls[0], ls[1]), ls[2])
        tot = jnp.log(jnp.exp(ls[0] - m) + jnp.exp(ls[1] - m) + jnp.exp(ls[2] - m)) + m
        o = None
        for o_ref, l in zip((o0, o1, o2), ls, strict=True):
            term = _expand_groups(jnp.exp(l - tot)) * o_ref[...]
            o = term if o is None else o + term
        ob_ref[...] = o.astype(BF16)
        of_ref[...] = o
        lse_ref[...] = tot

    return pl.pallas_call(
        body,
        name=name,
        grid=(SEQ // ROW_TILE,),
        in_specs=[_rows(D_MODEL)] * 3 + [_rows(LANES)] * 3,
        out_specs=[_rows(D_MODEL), _rows(D_MODEL), _rows(LANES)],
        out_shape=[
            jax.ShapeDtypeStruct((SEQ, D_MODEL), BF16),
            jax.ShapeDtypeStruct((SEQ, D_MODEL), F32),
            jax.ShapeDtypeStruct((SEQ, LANES), F32),
        ],
        compiler_params=_params(("parallel",)),
    )(*outs, *lses)


def attn_delta(do, o, *, name):
    def body(do_ref, o_ref, dob_ref, dl_ref):
        do_v = do_ref[...]
        dob_ref[...] = do_v.astype(BF16)
        dl_ref[...] = _reduce_groups(do_v * o_ref[...])

    return pl.pallas_call(
        body,
        name=name,
        grid=(SEQ // ROW_TILE,),
        in_specs=[_rows(D_MODEL), _rows(D_MODEL)],
        out_specs=[_rows(D_MODEL), _rows(LANES)],
        out_shape=[jax.ShapeDtypeStruct((SEQ, D_MODEL), BF16), jax.ShapeDtypeStruct((SEQ, LANES), F32)],
        compiler_params=_params(("parallel",)),
    )(do, o)


def attn_bwd(qkv, do_b, lse, delta, pat, *, name):
    _, dilation = B_PATTERNS[pat]
    rows = SEQ // dilation
    nb = rows // SPAN
    qkv2 = qkv.reshape(rows, dilation * D_QKV)
    do2 = do_b.reshape(rows, dilation * D_MODEL)
    lse2 = lse.reshape(rows, dilation * LANES)
    delta2 = delta.reshape(rows, dilation * LANES)
    bias = alibi_tables(dilation)
    last = nb - 1

    def body(q_ref, kp_ref, kc_ref, vp_ref, vc_ref, do_ref, lse_ref, dl_ref, bias_ref, dq_ref, dk_ref, dv_ref, ck_ref, cv_ref):
        n = pl.program_id(1)

        @pl.when(n == nb)
        def _():
            dk_ref[...] = ck_ref[...].astype(BF16)
            dv_ref[...] = cv_ref[...].astype(BF16)

        @pl.when(n == 0)
        def _():
            ck_ref[...] = jnp.zeros_like(ck_ref)
            cv_ref[...] = jnp.zeros_like(cv_ref)

        @pl.when(n < nb)
        def _():
            lane = _lane((SPAN, LANES))
            lo_mask = lane < B_HEAD_DIM
            first_prev = jnp.logical_and(n == 0, _lane((SPAN, 2 * SPAN)) < SPAN)
            q = q_ref[...] * jnp.asarray(B_HEAD_DIM**-0.5, BF16)
            kk = jnp.concatenate([kp_ref[...], kc_ref[...]], axis=0)
            vv = jnp.concatenate([vp_ref[...], vc_ref[...]], axis=0)
            do_v = do_ref[...]
            lse_v = lse_ref[...]
            dl_v = dl_ref[...]
            halves = (lo_mask.astype(BF16), jnp.logical_not(lo_mask).astype(BF16))
            for j in range(B_HEADS // 2):
                cs = slice(j * LANES, (j + 1) * LANES)
                qp, kp, vp, dop = q[:, cs], kk[:, cs], vv[:, cs], do_v[:, cs]
                dqs = []
                dk_pair = None
                dv_pair = None
                for half in range(2):
                    h = 2 * j + half
                    qm = qp * halves[half]
                    dom = dop * halves[half]
                    lse_h = jnp.sum(jnp.where(lane == h, lse_v, 0.0), axis=1, keepdims=True)
                    dl_h = jnp.sum(jnp.where(lane == h, dl_v, 0.0), axis=1, keepdims=True)
                    sc = lax.dot_general(qm, kp, (((1,), (1,)), ((), ())), preferred_element_type=F32)
                    sc = jnp.where(first_prev, NEG, sc + bias_ref[h])
                    p = jnp.exp(sc - lse_h)
                    dp = lax.dot_general(dom, vp, (((1,), (1,)), ((), ())), preferred_element_type=F32)
                    ds = (p * (dp - dl_h)).astype(BF16)
                    dqs.append(jnp.dot(ds, kp, preferred_element_type=F32))
                    dk_h = lax.dot_general(ds, qm, (((0,), (0,)), ((), ())), preferred_element_type=F32)
                    dv_h = lax.dot_general(
                        p.astype(BF16), dom, (((0,), (0,)), ((), ())), preferred_element_type=F32
                    )
                    dk_pair = dk_h if dk_pair is None else dk_pair + dk_h
                    dv_pair = dv_h if dv_pair is None else dv_pair + dv_h
                dq_ref[:, cs] = (jnp.where(lo_mask, dqs[0], dqs[1]) * (B_HEAD_DIM**-0.5)).astype(BF16)
                dk_ref[:, cs] = (ck_ref[:, cs] + dk_pair[:SPAN]).astype(BF16)
                dv_ref[:, cs] = (cv_ref[:, cs] + dv_pair[:SPAN]).astype(BF16)
                ck_ref[:, cs] = dk_pair[SPAN:]
                cv_ref[:, cs] = dv_pair[SPAN:]

    def clamp(n):
        return jnp.minimum(n, last)

    def row_spec(width):
        return pl.BlockSpec((SPAN, width), lambda r, n: (clamp(n), r))

    lagged = pl.BlockSpec((SPAN, D_MODEL), lambda r, n: (jnp.maximum(n - 1, 0), r))
    outs = pl.pallas_call(
        body,
        name=name,
        grid=(dilation, nb + 1),
        in_specs=[
            *_qkv_specs(pat, dilation, clamp),
            row_spec(D_MODEL),
            row_spec(LANES),
            row_spec(LANES),
            pl.BlockSpec((B_HEADS, SPAN, 2 * SPAN), lambda r, n: (0, 0, 0)),
        ],
        out_specs=[row_spec(D_MODEL), lagged, lagged],
        out_shape=[jax.ShapeDtypeStruct((rows, dilation * D_MODEL), BF16)] * 3,
        scratch_shapes=[pltpu.VMEM((SPAN, D_MODEL), F32), pltpu.VMEM((SPAN, D_MODEL), F32)],
        compiler_params=_params(("arbitrary", "arbitrary")),
    )(qkv2, qkv2, qkv2, qkv2, qkv2, do2, lse2, delta2, bias)
    return [o.reshape(SEQ, D_MODEL) for o in outs]


def _position():
    x, y, c = lax.axis_index("x"), lax.axis_index("y"), lax.axis_index("c")
    return x, y, c, 4 * x + 2 * y + c


def _peer(k, x, y, c):
    px = 1 - x if k & 4 else x
    py = 1 - y if k & 2 else y
    pc = 1 - c if k & 1 else c
    return (px, py, pc), 4 * px + 2 * py + pc


def _remote(src, dst, send_sem, recv_sem, device):
    return pltpu.make_async_remote_copy(
        src_ref=src, dst_ref=dst, send_sem=send_sem, recv_sem=recv_sem, device_id=device, device_id_type=MESH
    )


def _silu_bf16(cf):
    return (cf * (1.0 / (1.0 + jnp.exp(-cf)))).astype(BF16)


def ada_exchange(c8, w4, b4, ln8):
    nt, _, ncol = w4.shape

    def body(c8_ref, w_ref, b_ref, ln_ref, cg_ref, lng_ref, mrecv_ref, mloc_ref, send_sems, recv_sems):
        x, y, c, me = _position()
        cg_ref[me] = c8_ref[...]
        lng_ref[me] = ln_ref[...]
        first = []
        for k in range(1, N_DEV):
            dev, _ = _peer(k, x, y, c)
            first.append(_remote(c8_ref, cg_ref.at[me], send_sems.at[0, k], recv_sems.at[0, k], dev))
            first.append(_remote(ln_ref, lng_ref.at[me], send_sems.at[1, k], recv_sems.at[1, k], dev))
        for cp in first:
            cp.start()
        for k in range(1, N_DEV):
            dev, pid = _peer(k, x, y, c)
            _remote(c8_ref, cg_ref.at[pid], send_sems.at[0, k], recv_sems.at[0, k], dev).wait_recv()
            _remote(ln_ref, lng_ref.at[pid], send_sems.at[1, k], recv_sems.at[1, k], dev).wait_recv()
        sc = _silu_bf16(cg_ref[...].reshape(N_DEV * SUBLANES, D_MODEL))
        for t in range(nt):
            mloc_ref[t] = jnp.dot(sc, w_ref[t].astype(BF16), preferred_element_type=F32) + b_ref[t : t + 1, :]

        def group(dev_id):
            return pl.ds(pl.multiple_of(dev_id * SUBLANES, SUBLANES), SUBLANES)

        mrecv_ref[me] = mloc_ref[:, group(me), :]
        second = []
        for k in range(1, N_DEV):
            dev, pid = _peer(k, x, y, c)
            second.append(
                _remote(mloc_ref.at[:, group(pid), :], mrecv_ref.at[me], send_sems.at[2, k], recv_sems.at[2, k], dev)
            )
        for cp in second:
            cp.start()
        for k in range(1, N_DEV):
            dev, pid = _peer(k, x, y, c)
            _remote(
                mloc_ref.at[:, group(pid), :], mrecv_ref.at[pid], send_sems.at[2, k], recv_sems.at[2, k], dev
            ).wait_recv()
        for cp in first + second:
            cp.wait_send()

    return pl.pallas_call(
        body,
        name="ada_exchange",
        in_specs=[VMEM, VMEM, VMEM, VMEM],
        out_specs=[VMEM, VMEM, VMEM],
        out_shape=[
            jax.ShapeDtypeStruct((N_DEV, SUBLANES, D_MODEL), F32),
            jax.ShapeDtypeStruct((N_DEV, SUBLANES, LANES), F32),
            jax.ShapeDtypeStruct((N_DEV, nt, SUBLANES, ncol), F32),
        ],
        scratch_shapes=[
            pltpu.VMEM((nt, N_DEV * SUBLANES, ncol), F32),
            pltpu.SemaphoreType.DMA((3, N_DEV)),
            pltpu.SemaphoreType.DMA((3, N_DEV)),
        ],
        compiler_params=pltpu.CompilerParams(vmem_limit_bytes=VMEM_LIMIT_BYTES),
    )(c8, w4, b4, ln8)


def small_exchange(dmx, flat):
    def body(dmx_ref, flat_ref, dmrecv_ref, red_ref, land_ref, send_sems, recv_sems):
        x, y, c, me = _position()
        dmrecv_ref[me] = dmx_ref[me]
        land_ref[me] = flat_ref[me]
        first = []
        for k in range(1, N_DEV):
            dev, pid = _peer(k, x, y, c)
            first.append(_remote(dmx_ref.at[pid], dmrecv_ref.at[me], send_sems.at[0, k], recv_sems.at[0, k], dev))
            first.append(_remote(flat_ref.at[pid], land_ref.at[me], send_sems.at[1, k], recv_sems.at[1, k], dev))
        for cp in first:
            cp.start()
        for k in range(1, N_DEV):
            dev, pid = _peer(k, x, y, c)
            _remote(dmx_ref.at[pid], dmrecv_ref.at[pid], send_sems.at[0, k], recv_sems.at[0, k], dev).wait_recv()
            _remote(flat_ref.at[pid], land_ref.at[pid], send_sems.at[1, k], recv_sems.at[1, k], dev).wait_recv()
        total = land_ref[0]
        for s in range(1, N_DEV):
            total = total + land_ref[s]
        red_ref[me] = total
        second = []
        for k in range(1, N_DEV):
            dev, _ = _peer(k, x, y, c)
            second.append(_remote(red_ref.at[me], red_ref.at[me], send_sems.at[2, k], recv_sems.at[2, k], dev))
        for cp in second:
            cp.start()
        for k in range(1, N_DEV):
            dev, pid = _peer(k, x, y, c)
            _remote(red_ref.at[pid], red_ref.at[pid], send_sems.at[2, k], recv_sems.at[2, k], dev).wait_recv()
        for cp in first + second:
            cp.wait_send()

    return pl.pallas_call(
        body,
        name="small_exchange",
        in_specs=[VMEM, VMEM],
        out_specs=[VMEM, VMEM],
        out_shape=[jax.ShapeDtypeStruct(dmx.shape, F32), jax.ShapeDtypeStruct(flat.shape, F32)],
        scratch_shapes=[
            pltpu.VMEM(flat.shape, F32),
            pltpu.SemaphoreType.DMA((3, N_DEV)),
            pltpu.SemaphoreType.DMA((3, N_DEV)),
        ],
        compiler_params=pltpu.CompilerParams(vmem_limit_bytes=VMEM_LIMIT_BYTES),
    )(dmx, flat)


HBM = pl.BlockSpec(memory_space=pltpu.HBM)
SEM = pl.BlockSpec(memory_space=pltpu.SEMAPHORE)
EFFECT = pltpu.SideEffectType.DATAFLOW_SIDE_EFFECTING


def _own_slot(me, block):
    land = lax.empty((N_DEV, *block.shape), block.dtype)
    return lax.dynamic_update_slice_in_dim(land, block[None], me, axis=0)


class Exchange:
    def __init__(self, srcs, lands, *, scatter, name):
        self.scatter = scatter
        self.name = name
        nt = self.nt = len(srcs)
        peers = N_DEV - 1

        def body(*refs):
            src_refs, land_refs = refs[:nt], refs[nt : 2 * nt]
            send_sems, recv_sems = refs[2 * nt : 3 * nt], refs[3 * nt : 4 * nt]
            token = refs[-1]
            x, y, c, me = _position()
            for t in range(nt):
                for k in range(1, N_DEV):
                    dev, pid = _peer(k, x, y, c)
                    src = src_refs[t].at[pid] if scatter else src_refs[t]
                    _remote(src, land_refs[t].at[me], send_sems[t].at[k - 1], recv_sems[t].at[k - 1], dev).start()
            token[...] = jnp.zeros_like(token)

        outs = pl.pallas_call(
            body,
            name=name + "_start",
            in_specs=[HBM] * (2 * nt),
            out_specs=[SEM] * (2 * nt) + [HBM] * (2 * nt) + [VMEM],
            out_shape=[pltpu.SemaphoreType.DMA((peers,))] * (2 * nt)
            + [pltpu.HBM(a.shape, a.dtype) for a in (*srcs, *lands)]
            + [jax.ShapeDtypeStruct((SUBLANES, LANES), F32)],
            input_output_aliases={i: 2 * nt + i for i in range(2 * nt)},
            compiler_params=pltpu.CompilerParams(has_side_effects=EFFECT),
        )(*[pltpu.with_memory_space_constraint(a, pltpu.HBM) for a in (*srcs, *lands)])
        self.send_sems, self.recv_sems = outs[:nt], outs[nt : 2 * nt]
        self.srcs, self.lands = outs[2 * nt : 3 * nt], outs[3 * nt : 4 * nt]
        self.token = outs[-1]

    def zero(self):
        return self.token[0, 0]

    def wait(self, which, after, *, name):
        scatter = self.scatter
        n = len(which)

        def body(*refs):
            src_refs, land_refs = refs[:n], refs[n : 2 * n]
            send_sems, recv_sems = refs[2 * n : 3 * n], refs[3 * n : 4 * n]
            x, y, c, _ = _position()
            for t in range(n):
                for k in range(1, N_DEV):
                    dev, pid = _peer(k, x, y, c)
                    src = src_refs[t].at[pid] if scatter else src_refs[t]
                    cp = _remote(src, land_refs[t].at[pid], send_sems[t].at[k - 1], recv_sems[t].at[k - 1], dev)
                    cp.wait_send()
                    cp.wait_recv()

        srcs = [self.srcs[t] for t in which]
        lands = [self.lands[t] for t in which]
        outs = pl.pallas_call(
            body,
            name=name,
            in_specs=[HBM] * (2 * n) + [SEM] * (2 * n) + [ANY],
            out_specs=[HBM] * (2 * n),
            out_shape=[pltpu.HBM(a.shape, a.dtype) for a in (*srcs, *lands)],
            input_output_aliases={i: i for i in range(2 * n)},
            compiler_params=pltpu.CompilerParams(has_side_effects=EFFECT),
        )(*srcs, *lands, *[self.send_sems[t] for t in which], *[self.recv_sems[t] for t in which], after)
        return outs[n:]


def _adam_update(g, w, m, v):
    m2 = ADAM_B1 * m + (1.0 - ADAM_B1) * g
    v2 = ADAM_B2 * v + (1.0 - ADAM_B2) * jnp.square(g)
    m_hat = m2 / (1.0 - ADAM_B1**ADAM_STEP)
    v_hat = v2 / (1.0 - ADAM_B2**ADAM_STEP)
    delta = -ADAM_LR * (m_hat / (jnp.sqrt(v_hat) + ADAM_EPS) + ADAM_WD * w)
    return delta, m2, v2


def adamw(gparts, w, m, v, *, name):
    p, r, c = gparts.shape
    tr = r if r <= 256 else (256 if c <= D_MODEL else 128)

    def body(g_ref, w_ref, m_ref, v_ref, go_ref, d_ref, mo_ref, vo_ref):
        g = g_ref[0].astype(F32)
        for i in range(1, p):
            g = g + g_ref[i].astype(F32)
        delta, m2, v2 = _adam_update(g, w_ref[...], m_ref[...], v_ref[...])
        go_ref[...] = g
        d_ref[...] = delta
        mo_ref[...] = m2
        vo_ref[...] = v2

    blk = pl.BlockSpec((tr, c), lambda i: (i, 0))
    return pl.pallas_call(
        body,
        name=name,
        grid=(r // tr,),
        in_specs=[pl.BlockSpec((p, tr, c), lambda i: (0, i, 0)), blk, blk, blk],
        out_specs=[blk] * 4,
        out_shape=[jax.ShapeDtypeStruct((r, c), F32)] * 4,
        compiler_params=_params(("parallel",)),
    )(gparts, w, m, v)


def ada_grad_adamw(cg, dmrecv, w4, m4, v4, *, name):
    nt, k, ncol = w4.shape

    def body(cg_ref, dm_ref, w_ref, m_ref, v_ref, go_ref, d_ref, mo_ref, vo_ref, gb_ref):
        sc = _silu_bf16(cg_ref[...].reshape(N_DEV * SUBLANES, k))
        dm = dm_ref[...].reshape(N_DEV * SUBLANES, ncol)
        g = lax.dot_general(sc, dm.astype(BF16), (((0,), (0,)), ((), ())), preferred_element_type=F32)
        delta, m2, v2 = _adam_update(g, w_ref[...], m_ref[...], v_ref[...])
        go_ref[...] = g
        d_ref[...] = delta
        mo_ref[...] = m2
        vo_ref[...] = v2
        gb_ref[...] = jnp.broadcast_to(jnp.sum(dm, axis=0, keepdims=True), (SUBLANES, ncol))

    wblk = pl.BlockSpec((None, k, ncol), lambda t: (t, 0, 0))
    return pl.pallas_call(
        body,
        name=name,
        grid=(nt,),
        in_specs=[
            pl.BlockSpec((N_DEV, SUBLANES, k), lambda t: (0, 0, 0)),
            pl.BlockSpec((N_DEV, None, SUBLANES, ncol), lambda t: (0, t, 0, 0)),
            wblk,
            wblk,
            wblk,
        ],
        out_specs=[wblk] * 4 + [pl.BlockSpec((None, SUBLANES, ncol), lambda t: (t, 0, 0))],
        out_shape=[jax.ShapeDtypeStruct((nt, k, ncol), F32)] * 4 + [jax.ShapeDtypeStruct((nt, SUBLANES, ncol), F32)],
        compiler_params=_params(("parallel",)),
    )(cg, dmrecv, w4, m4, v4)


def kernel(x, c, ada_w, ada_b, ln_g, ln_b, a_w_in, a_b_in, a_vn_g, a_vn_b, a_w_s, a_b_s, a_w_out, b_w_qkv, b_w_out, mlp_w_up, mlp_w_down, loss_target, m_ada_w, m_ada_b, m_ln_g, m_ln_b, m_a_w_in, m_a_b_in, m_a_vn_g, m_a_vn_b, m_a_w_s, m_a_b_s, m_a_w_out, m_b_w_qkv, m_b_w_out, m_mlp_w_up, m_mlp_w_down, v_ada_w, v_ada_b, v_ln_g, v_ln_b, v_a_w_in, v_a_b_in, v_a_vn_g, v_a_vn_b, v_a_w_s, v_a_b_s, v_a_w_out, v_b_w_qkv, v_b_w_out, v_mlp_w_up, v_mlp_w_down):
    x0 = x[0]
    target = loss_target[0]
    me = 4 * lax.axis_index("x") + 2 * lax.axis_index("y") + lax.axis_index("c")

    W_IN, W_AOUT, W_UP0, W_DN0, W_QKV, W_BOUT, W_UP1, W_DN1 = range(8)
    shards = [
        a_w_in[0].astype(BF16),
        a_w_out[0].astype(BF16),
        mlp_w_up[0].astype(BF16),
        mlp_w_down[0].astype(BF16),
        b_w_qkv[0].astype(BF16),
        b_w_out[0].astype(BF16),
        mlp_w_up[1].astype(BF16),
        mlp_w_down[1].astype(BF16),
    ]
    gather = Exchange(shards, [_own_slot(me, s) for s in shards], scatter=False, name="gather")

    ada_w4 = ada_w.reshape(N_SUB, D_MODEL, -1)
    ada_b4 = ada_b.reshape(N_SUB, -1)
    ln8 = jnp.concatenate([ln_g.reshape(N_SUB, -1), ln_b.reshape(N_SUB, -1)], axis=0)
    c8 = jnp.broadcast_to(c, (SUBLANES, D_MODEL)) + gather.zero()
    cg, lng, mrecv = ada_exchange(c8, ada_w4, ada_b4, ln8)
    modv = mrecv[:, :, 0, :].transpose(1, 0, 2).reshape(N_SUB, 3 * D_MODEL)
    shift = [modv[t : t + 1, :D_MODEL] for t in range(N_SUB)]
    scale = [modv[t : t + 1, D_MODEL : 2 * D_MODEL] for t in range(N_SUB)]
    gate1 = [1.0 + modv[t : t + 1, 2 * D_MODEL :] for t in range(N_SUB)]
    lng_full = [lng[:, t, :].reshape(1, D_MODEL) for t in range(N_SUB)]
    lnb_full = [lng[:, N_SUB + t, :].reshape(1, D_MODEL) for t in range(N_SUB)]

    ident = lambda acc: (acc,)
    relu2 = lambda acc: (acc, jnp.square(jnp.maximum(acc, 0.0)))
    vn_g, vn_b, w_s = a_vn_g, a_vn_b, a_w_s[0]
    bias_full = jnp.repeat(a_b_s[0].T, A_GROUP_DIM, axis=1)
    w_up3, w_dn3 = [None, None], [None, None]

    def mlp_forward(i, h, after):
        up, dn = gather.wait([W_UP0, W_DN0] if i == 0 else [W_UP1, W_DN1], after, name=f"gather_wait_mlp{i}")
        w_up3[i], w_dn3[i] = up, dn.reshape(1, D_FF, D_MODEL)
        a, r = mm_nn(h, w_up3[i], name=f"mlp{i}_up", tm=1024, tn=512, epilogue=relu2, outs=(BF16, BF16))
        (y,) = mm_nn(r, w_dn3[i], name=f"mlp{i}_down", tm=512, tn=512, epilogue=ident, outs=(F32,))
        return a, r, y

    h0 = modulate(x0, scale[0], shift[0], name="modulate0")
    w_in3, w_aout3 = gather.wait([W_IN, W_AOUT], h0, name="gather_wait_a")
    w_aout3 = w_aout3.reshape(1, D_MODEL, D_MODEL)
    (a_pre,) = mm_nn(
        h0, w_in3, name="a_in", tm=1024, tn=256, epilogue=lambda acc, b: (acc + b,), extras=[(a_b_in, "row")], outs=(F32,)
    )
    p_gate = gate_fwd(a_pre, vn_g, vn_b, w_s, bias_full, name="gate_fwd")
    (y0,) = mm_nn(p_gate, w_aout3, name="a_out", tm=1024, tn=512, epilogue=ident, outs=(F32,))
    x1, h1 = residual_ln(x0, y0, gate1[0], lng_full[0], lnb_full[0], scale[1], shift[1], name="res_ln0")
    a1, r1, y1 = mlp_forward(0, h1, y0)
    x2, h2 = residual_ln(x1, y1, gate1[1], lng_full[1], lnb_full[1], scale[2], shift[2], name="res_ln1")
    w_qkv3, w_bout3 = gather.wait([W_QKV, W_BOUT], y1, name="gather_wait_b")
    w_bout3 = w_bout3.reshape(1, D_MODEL, D_MODEL)
    (qkv,) = mm_nn(h2, w_qkv3, name="b_qkv", tm=1024, tn=1152, epilogue=ident, outs=(BF16,))
    pat_o, pat_lse = [], []
    for pat in range(N_PAT):
        o_p, lse_p = attn_fwd(qkv, pat, name=f"attn_fwd{pat}")
        pat_o.append(o_p.reshape(SEQ, D_MODEL))
        pat_lse.append(lse_p.reshape(SEQ, LANES))
    o_b, o_f, lse = attn_combine(pat_o, pat_lse, name="attn_combine")
    (y2,) = mm_nn(o_b, w_bout3, name="b_out", tm=1024, tn=512, epilogue=ident, outs=(F32,))
    x3, h3 = residual_ln(x2, y2, gate1[2], lng_full[2], lnb_full[2], scale[3], shift[3], name="res_ln2")
    a3, r3, y3 = mlp_forward(1, h3, y2)
    dxo, loss_local = residual_ln_loss(x3, y3, gate1[3], lng_full[3], lnb_full[3], target, name="res_ln3_loss")
    loss = lax.psum(loss_local[0, 0], ("x", "y", "c"))

    def scatter(parts, name):
        parts = [p.reshape(N_DEV, -1, p.shape[-1]) for p in parts]
        lands = [_own_slot(me, lax.dynamic_index_in_dim(p, me, 0, keepdims=False)) for p in parts]
        return Exchange(parts, lands, scatter=True, name=name)

    def mlp_backward(i, h, a, r, dy):
        (da,) = mm_nt(
            dy,
            w_dn3[i],
            name=f"mlp{i}_da",
            tm=1024,
            tko=1024,
            ps=1,
            epilogue=lambda acc, act: (acc * (2.0 * jnp.maximum(act.astype(F32), 0.0)),),
            extras=[(a, "full")],
            outs=(BF16,),
        )
        (dh,) = mm_nt(da, w_up3[i], name=f"mlp{i}_dh", tm=1024, tko=1024, ps=2, epilogue=ident, outs=(F32,))
        dw_dn = mm_tn(r, dy, name=f"mlp{i}_dw_down", p=1, tk=1024, tn=512, tmc=1024, out_dtype=BF16)
        dw_up = mm_tn(h, da, name=f"mlp{i}_dw_up", p=N_DEV, tk=1024, tn=512, tmc=1024, out_dtype=BF16)
        return scatter([dw_up, dw_dn], f"scatter_mlp{i}"), dh

    dz3, dy3, st3 = residual_ln_bwd(name="res_bwd3", dxo=dxo, this=(x3, y3, gate1[3], lng_full[3]))
    rs_mlp1, dh3 = mlp_backward(1, h3, a3, r3, dy3)
    dz2, dy2, st2 = residual_ln_bwd(
        name="res_bwd2", later=(dh3, dz3, x3, scale[3] + rs_mlp1.zero()), this=(x2, y2, gate1[2], lng_full[2])
    )
    (d_o,) = mm_nt(dy2, w_bout3, name="b_do", tm=1024, tko=1024, ps=1, epilogue=ident, outs=(F32,))
    do_b, delta = attn_delta(d_o, o_f, name="attn_delta")
    dqkv_parts = []
    for pat in range(N_PAT):
        dqkv_parts += attn_bwd(qkv, do_b, lse, delta, pat, name=f"attn_bwd{pat}")
    dqkv = jnp.concatenate(dqkv_parts, axis=1)
    (dh2,) = mm_nt(dqkv, w_qkv3, name="b_dh", tm=1024, tko=1024, ps=1, epilogue=ident, outs=(F32,))
    dw_bout = mm_tn(o_b, dy2, name="b_dw_out", p=1, tk=1024, tn=512, tmc=1024, out_dtype=BF16)
    dw_qkv = mm_tn(h2, dqkv, name="b_dw_qkv", p=N_DEV, tk=512, tn=1152, tmc=1024, out_dtype=BF16)
    rs_b = scatter([dw_qkv, dw_bout], "scatter_b")
    dz1, dy1, st1 = residual_ln_bwd(
        name="res_bwd1", later=(dh2, dz2, x2, scale[2] + rs_b.zero()), this=(x1, y1, gate1[1], lng_full[1])
    )
    rs_mlp0, dh1 = mlp_backward(0, h1, a1, r1, dy1)
    dz0, dy0, st0 = residual_ln_bwd(
        name="res_bwd0", later=(dh1, dz1, x1, scale[1] + rs_mlp0.zero()), this=(x0, y0, gate1[0], lng_full[0])
    )
    (dp_gate,) = mm_nt(dy0, w_aout3, name="a_dp", tm=1024, tko=1024, ps=1, epilogue=ident, outs=(F32,))
    da0, d_ws, d_bs, gate_rows = gate_bwd(a_pre, dp_gate, vn_g, vn_b, w_s, bias_full, name="gate_bwd")
    (dh0,) = mm_nt(da0, w_in3, name="a_dh", tm=1024, tko=1024, ps=4, epilogue=ident, outs=(F32,))
    dw_aout = mm_tn(p_gate, dy0, name="a_dw_out", p=1, tk=1024, tn=512, tmc=1024, out_dtype=BF16)
    dw_in = mm_tn(h0, da0, name="a_dw_in", p=N_DEV, tk=1024, tn=256, tmc=1024, out_dtype=BF16)
    rs_a = scatter([dw_in, dw_aout], "scatter_a")
    grad_x, stf = residual_ln_bwd(name="res_bwd_in", later=(dh0, dz0, x0, scale[0] + rs_a.zero()))

    stats_after = [stf, st0, st1, st2]
    stats_own = [st0, st1, st2, st3]
    dm = jnp.stack(
        [
            jnp.concatenate(
                [stats_after[t][ST_DSHIFT], stats_after[t][ST_DSCALE], stats_own[t][ST_DGATE]], axis=0
            )
            for t in range(N_SUB)
        ]
    )
    ncol = 3 * D_MODEL // N_DEV
    dmx = jnp.pad(
        dm.reshape(N_SUB, N_DEV, ncol).transpose(1, 0, 2)[:, :, None, :], ((0, 0), (0, 0), (0, SUBLANES - 1), (0, 0))
    )
    small = [
        gate_rows[0],
        gate_rows[1],
        d_ws.reshape(-1),
        d_bs[:, :A_GROUPS].T.reshape(-1),
        *[stats_own[t][ST_DG] for t in range(N_SUB)],
        *[stats_own[t][ST_DB] for t in range(N_SUB)],
    ]
    n_small = sum(s.size for s in small)
    part_rows = -(-n_small // (N_DEV * LANES * SUBLANES)) * SUBLANES
    flat = jnp.concatenate(small + [jnp.zeros((N_DEV * part_rows * LANES - n_small,), F32)])
    dmrecv, reduced = small_exchange(dmx, flat.reshape(N_DEV, part_rows, LANES))
    reduced = reduced.reshape(-1)
    sizes = [2 * D_MODEL, D_MODEL, D_MODEL, A_GROUPS * CHUNK * CHUNK, A_GROUPS * CHUNK, N_SUB * D_MODEL, N_SUB * D_MODEL]
    offs = [sum(sizes[:i]) for i in range(len(sizes) + 1)]
    g_b_in, g_vn_g, g_vn_b, g_ws, g_bs, g_lng, g_lnb = [reduced[offs[i] : offs[i + 1]] for i in range(len(sizes))]

    results = {}

    def update(wname, gparts, w, m, v):
        shape = w.shape
        w2 = w.reshape(-1, shape[-1])
        outs = adamw(gparts.reshape(gparts.shape[0], *w2.shape), w2, m.reshape(w2.shape), v.reshape(w2.shape), name=f"adamw_{wname}")
        results[wname] = [o.reshape(shape) for o in outs]

    ada_outs = ada_grad_adamw(cg, dmrecv, ada_w4, m_ada_w.reshape(ada_w4.shape), v_ada_w.reshape(ada_w4.shape), name="ada_grad_adamw")
    results["ada_w"] = [o.reshape(ada_w.shape) for o in ada_outs[:4]]
    update("ada_b", ada_outs[4][:, 0, :][None], ada_b, m_ada_b, v_ada_b)
    ln_cols = D_MODEL // N_DEV
    my_ln = lambda gfull: lax.dynamic_slice_in_dim(gfull.reshape(N_SUB, N_DEV, ln_cols), me, 1, axis=1)
    update("ln_g", my_ln(g_lng).reshape(1, N_SUB, ln_cols), ln_g, m_ln_g, v_ln_g)
    update("ln_b", my_ln(g_lnb).reshape(1, N_SUB, ln_cols), ln_b, m_ln_b, v_ln_b)
    update("a_b_in", g_b_in[None], a_b_in, m_a_b_in, v_a_b_in)
    update("a_vn_g", g_vn_g[None], a_vn_g, m_a_vn_g, v_a_vn_g)
    update("a_vn_b", g_vn_b[None], a_vn_b, m_a_vn_b, v_a_vn_b)
    update("a_w_s", g_ws[None], a_w_s, m_a_w_s, v_a_w_s)
    update("a_b_s", g_bs[None], a_b_s, m_a_b_s, v_a_b_s)
    g_up1, g_dn1 = rs_mlp1.wait([0, 1], grad_x, name="scatter_wait_mlp1")
    g_qkv, g_bout = rs_b.wait([0, 1], grad_x, name="scatter_wait_b")
    update("b_w_qkv", g_qkv, b_w_qkv, m_b_w_qkv, v_b_w_qkv)
    update("b_w_out", g_bout, b_w_out, m_b_w_out, v_b_w_out)
    g_up0, g_dn0 = rs_mlp0.wait([0, 1], grad_x, name="scatter_wait_mlp0")
    update("mlp_w_up", jnp.concatenate([g_up0, g_up1], axis=1), mlp_w_up, m_mlp_w_up, v_mlp_w_up)
    update("mlp_w_down", jnp.concatenate([g_dn0, g_dn1], axis=1), mlp_w_down, m_mlp_w_down, v_mlp_w_down)
    g_in, g_aout = rs_a.wait([0, 1], grad_x, name="scatter_wait_a")
    update("a_w_in", g_in, a_w_in, m_a_w_in, v_a_w_in)
    update("a_w_out", g_aout, a_w_out, m_a_w_out, v_a_w_out)

    order = ["ada_w", "ada_b", "ln_g", "ln_b", "a_w_in", "a_b_in", "a_vn_g", "a_vn_b", "a_w_s", "a_b_s", "a_w_out", "b_w_qkv", "b_w_out", "mlp_w_up", "mlp_w_down"]
    return (loss, grad_x[None], *[results[n][0] for n in order], *[results[n][1] for n in order],
            *[results[n][2] for n in order], *[results[n][3] for n in order])
```

```python
import math

import jax
import jax.numpy as jnp
from jax import lax
from jax.experimental import pallas as pl
from jax.experimental.pallas import tpu as pltpu

F32 = jnp.float32
BF16 = jnp.bfloat16
MESH = pl.DeviceIdType.MESH
ANY = pl.BlockSpec(memory_space=pl.ANY)
VMEM = pl.BlockSpec(memory_space=pltpu.VMEM)

N_DEV = 8
D_MODEL = 1024
SEQ = 4096
DEPTH = 2
CHUNK = 128
A_GROUPS = 16
A_GROUP_DIM = D_MODEL // A_GROUPS
B_HEADS = 16
B_HEAD_DIM = 64
B_PATTERNS = ((128, 1), (512, 4), (2048, 16))
N_PAT = len(B_PATTERNS)
SPAN = 128
D_FF = 4 * D_MODEL
D_QKV = N_PAT * 3 * D_MODEL
ALPHA = (2 * DEPTH) ** 0.25
LN_EPS = 1e-5
NEG = -1e30
ADAM_LR = 0.001
ADAM_B1 = 0.9
ADAM_B2 = 0.999
ADAM_EPS = 1e-08
ADAM_WD = 0.01
ADAM_STEP = 10
GELU_C = math.sqrt(2.0 / math.pi)
GELU_A = 0.044715

VMEM_LIMIT_BYTES = 56 * 1024 * 1024
LANES = 128
SUBLANES = 8
ROW_TILE = 512
N_SUB = 2 * DEPTH


def _params(sem):
    return pltpu.CompilerParams(dimension_semantics=sem, vmem_limit_bytes=VMEM_LIMIT_BYTES)


def _lane(shape):
    return lax.broadcasted_iota(jnp.int32, shape, len(shape) - 1)


def _split_bf16(x):
    hi = x.astype(BF16)
    lo = (x - hi.astype(F32)).astype(BF16)
    return hi, lo


def _group_expand_matrix(groups_padded, width):
    per = width // A_GROUPS
    r = lax.broadcasted_iota(jnp.int32, (groups_padded, width), 0)
    c = lax.broadcasted_iota(jnp.int32, (groups_padded, width), 1)
    return (c // per == r).astype(BF16)


def _group_reduce_matrix(width, groups_padded):
    per = width // A_GROUPS
    r = lax.broadcasted_iota(jnp.int32, (width, groups_padded), 0)
    c = lax.broadcasted_iota(jnp.int32, (width, groups_padded), 1)
    return (r // per == c).astype(BF16)


def _expand_groups(w):
    e = _group_expand_matrix(LANES, D_MODEL)
    hi, lo = _split_bf16(w)
    return jnp.dot(hi, e, preferred_element_type=F32) + jnp.dot(lo, e, preferred_element_type=F32)


def _reduce_groups(x):
    e = _group_reduce_matrix(D_MODEL, LANES)
    hi, lo = _split_bf16(x)
    return jnp.dot(hi, e, preferred_element_type=F32) + jnp.dot(lo, e, preferred_element_type=F32)


def mm_nn(a, b3, *, name, tm, tn, epilogue, extras=(), outs):
    m, k = a.shape
    p, _, n = b3.shape
    q = n // tn
    grid = (m // tm, p * q)

    def body(a_ref, b_ref, *rest):
        ex = rest[: len(extras)]
        out_refs = rest[len(extras) :]
        acc = jnp.dot(a_ref[...], b_ref[...], preferred_element_type=F32)
        res = epilogue(acc, *[e[...] for e in ex])
        for o_ref, r in zip(out_refs, res, strict=True):
            o_ref[...] = r.astype(o_ref.dtype)

    ex_specs = []
    for _, kind in extras:
        if kind == "row":
            ex_specs.append(pl.BlockSpec((1, tn), lambda i, j: (0, j)))
        else:
            ex_specs.append(pl.BlockSpec((tm, tn), lambda i, j: (i, j)))
    return pl.pallas_call(
        body,
        name=name,
        grid=grid,
        in_specs=[
            pl.BlockSpec((tm, k), lambda i, j: (i, 0)),
            pl.BlockSpec((None, k, tn), lambda i, j: (j // q, 0, j % q)),
            *ex_specs,
        ],
        out_specs=[pl.BlockSpec((tm, tn), lambda i, j: (i, j)) for _ in outs],
        out_shape=[jax.ShapeDtypeStruct((m, p * n), dt) for dt in outs],
        compiler_params=_params(("parallel", "parallel")),
    )(a, b3, *[arr for arr, _ in extras])


def mm_nt(g, b3, *, name, tm, tko, ps, epilogue, extras=(), outs):
    m = g.shape[0]
    p, k, n = b3.shape
    nc = p // ps
    grid = (m // tm, k // tko, nc)

    def body(g_ref, b_ref, *rest):
        ex = rest[: len(extras)]
        out_refs = rest[len(extras) : len(extras) + len(outs)]
        acc_ref = rest[-1]
        c = pl.program_id(2)
        part = None
        for s in range(ps):
            d = lax.dot_general(
                g_ref[:, s * n : (s + 1) * n], b_ref[s], (((1,), (1,)), ((), ())), preferred_element_type=F32
            )
            part = d if part is None else part + d

        @pl.when(c == 0)
        def _():
            acc_ref[...] = part

        @pl.when(c > 0)
        def _():
            acc_ref[...] += part

        @pl.when(c == nc - 1)
        def _():
            res = epilogue(acc_ref[...], *[e[...] for e in ex])
            for o_ref, r in zip(out_refs, res, strict=True):
                o_ref[...] = r.astype(o_ref.dtype)

    ex_specs = []
    for _, kind in extras:
        if kind == "row":
            ex_specs.append(pl.BlockSpec((1, tko), lambda i, j, c: (0, j)))
        else:
            ex_specs.append(pl.BlockSpec((tm, tko), lambda i, j, c: (i, j)))
    return pl.pallas_call(
        body,
        name=name,
        grid=grid,
        in_specs=[
            pl.BlockSpec((tm, ps * n), lambda i, j, c: (i, c)),
            pl.BlockSpec((ps, tko, n), lambda i, j, c: (c, j, 0)),
            *ex_specs,
        ],
        out_specs=[pl.BlockSpec((tm, tko), lambda i, j, c: (i, j)) for _ in outs],
        out_shape=[jax.ShapeDtypeStruct((m, k), dt) for dt in outs],
        scratch_shapes=[pltpu.VMEM((tm, tko), F32)],
        compiler_params=_params(("parallel", "parallel", "arbitrary")),
    )(g, b3, *[arr for arr, _ in extras])


def mm_tn(a, g, *, name, p, tk, tn, tmc, out_dtype):
    m, k = a.shape
    n = g.shape[1] // p
    q = n // tn
    nc = m // tmc
    grid = (k // tk, p * q, nc)

    def body(a_ref, g_ref, o_ref, acc_ref):
        c = pl.program_id(2)
        part = lax.dot_general(a_ref[...], g_ref[...], (((0,), (0,)), ((), ())), preferred_element_type=F32)

        @pl.when(c == 0)
        def _():
            acc_ref[...] = part

        @pl.when(c > 0)
        def _():
            acc_ref[...] += part

        @pl.when(c == nc - 1)
        def _():
            o_ref[...] = acc_ref[...].astype(o_ref.dtype)

    return pl.pallas_call(
        body,
        name=name,
        grid=grid,
        in_specs=[
            pl.BlockSpec((tmc, tk), lambda i, j, c: (c, i)),
            pl.BlockSpec((tmc, tn), lambda i, j, c: (c, j)),
        ],
        out_specs=pl.BlockSpec((None, tk, tn), lambda i, j, c: (j // q, i, j % q)),
        out_shape=jax.ShapeDtypeStruct((p, k, n), out_dtype),
        scratch_shapes=[pltpu.VMEM((tk, tn), F32)],
        compiler_params=_params(("parallel", "parallel", "arbitrary")),
    )(a, g)


def _rows(cols):
    return pl.BlockSpec((ROW_TILE, cols), lambda i: (i, 0))


def _vec(cols, rows=1):
    return pl.BlockSpec((rows, cols), lambda i: (0, 0))


def _layer_norm_hat(z):
    mu = jnp.mean(z, axis=-1, keepdims=True)
    zc = z - mu
    var = jnp.mean(zc * zc, axis=-1, keepdims=True)
    rstd = lax.rsqrt(var + LN_EPS)
    return zc * rstd, rstd


def modulate(x, scale, shift, *, name):
    s, d = x.shape

    def body(x_ref, sc_ref, sh_ref, h_ref):
        h_ref[...] = (x_ref[...] * (1.0 + sc_ref[...]) + sh_ref[...]).astype(BF16)

    return pl.pallas_call(
        body,
        name=name,
        grid=(s // ROW_TILE,),
        in_specs=[_rows(d), _vec(d), _vec(d)],
        out_specs=_rows(d),
        out_shape=jax.ShapeDtypeStruct((s, d), BF16),
        compiler_params=_params(("parallel",)),
    )(x, scale, shift)


def residual_ln(x, y, gate1, g, b, nscale, nshift, *, name):
    s, d = x.shape

    def body(x_ref, y_ref, gt_ref, g_ref, b_ref, sc_ref, sh_ref, xn_ref, hn_ref):
        z = ALPHA * x_ref[...] + gt_ref[...] * y_ref[...]
        xhat, _ = _layer_norm_hat(z)
        xn = xhat * g_ref[...] + b_ref[...]
        xn_ref[...] = xn
        hn_ref[...] = (xn * (1.0 + sc_ref[...]) + sh_ref[...]).astype(BF16)

    return pl.pallas_call(
        body,
        name=name,
        grid=(s // ROW_TILE,),
        in_specs=[_rows(d), _rows(d), _vec(d), _vec(d), _vec(d), _vec(d), _vec(d)],
        out_specs=[_rows(d), _rows(d)],
        out_shape=[jax.ShapeDtypeStruct((s, d), F32), jax.ShapeDtypeStruct((s, d), BF16)],
        compiler_params=_params(("parallel",)),
    )(x, y, gate1, g, b, nscale, nshift)


def residual_ln_loss(x, y, gate1, g, b, target, *, name):
    s, d = x.shape

    def body(x_ref, y_ref, gt_ref, g_ref, b_ref, t_ref, dx_ref, loss_ref):
        z = ALPHA * x_ref[...] + gt_ref[...] * y_ref[...]
        xhat, _ = _layer_norm_hat(z)
        err = xhat * g_ref[...] + b_ref[...] - t_ref[...]
        dx_ref[...] = err * (1.0 / d)
        part = jnp.sum(jnp.sum(err * err, axis=1, keepdims=True), axis=0, keepdims=True) * (0.5 / d)

        @pl.when(pl.program_id(0) == 0)
        def _():
            loss_ref[...] = part

        @pl.when(pl.program_id(0) > 0)
        def _():
            loss_ref[...] += part

    return pl.pallas_call(
        body,
        name=name,
        grid=(s // ROW_TILE,),
        in_specs=[_rows(d), _rows(d), _vec(d), _vec(d), _vec(d), _rows(d)],
        out_specs=[_rows(d), pl.BlockSpec((1, 1), lambda i: (0, 0))],
        out_shape=[jax.ShapeDtypeStruct((s, d), F32), jax.ShapeDtypeStruct((1, 1), F32)],
        compiler_params=_params(("arbitrary",)),
    )(x, y, gate1, g, b, target)


ST_DSCALE, ST_DSHIFT, ST_DG, ST_DB, ST_DGATE = 0, 1, 2, 3, 4


def residual_ln_bwd(*, name, later=None, dxo=None, this=None):
    lead = later[0] if later is not None else dxo
    s, d = lead.shape
    has_later, has_ln = later is not None, this is not None

    def body(*refs):
        refs = list(refs)
        if has_later:
            dh_ref, dzl_ref, xm_ref, scl_ref = refs[:4]
            refs = refs[4:]
        else:
            dxo_ref = refs.pop(0)
        if has_ln:
            x_ref, y_ref, gt_ref, g_ref = refs[:4]
            refs = refs[4:]
            dz_ref, dy_ref, st_ref = refs
        else:
            dx_ref, st_ref = refs

        @pl.when(pl.program_id(0) == 0)
        def _():
            st_ref[...] = jnp.zeros_like(st_ref)

        def acc(row, val):
            st_ref[row : row + 1, :] += jnp.sum(val, axis=0, keepdims=True)

        if has_later:
            dh = dh_ref[...]
            g_out = ALPHA * dzl_ref[...] + dh * (1.0 + scl_ref[...])
            acc(ST_DSCALE, dh * xm_ref[...])
            acc(ST_DSHIFT, dh)
        else:
            g_out = dxo_ref[...]
        if not has_ln:
            dx_ref[...] = g_out
            return
        y = y_ref[...]
        gate1 = gt_ref[...]
        z = ALPHA * x_ref[...] + gate1 * y
        xhat, rstd = _layer_norm_hat(z)
        acc(ST_DG, g_out * xhat)
        acc(ST_DB, g_out)
        dxh = g_out * g_ref[...]
        m1 = jnp.mean(dxh, axis=-1, keepdims=True)
        m2 = jnp.mean(dxh * xhat, axis=-1, keepdims=True)
        dz = rstd * (dxh - m1 - xhat * m2)
        acc(ST_DGATE, dz * y)
        dz_ref[...] = dz
        dy_ref[...] = (dz * gate1).astype(BF16)

    ins, specs = [], []
    if has_later:
        ins += list(later)
        specs += [_rows(d), _rows(d), _rows(d), _vec(d)]
    else:
        ins += [dxo]
        specs += [_rows(d)]
    if has_ln:
        ins += list(this)
        specs += [_rows(d), _rows(d), _vec(d), _vec(d)]
        out_specs = [_rows(d), _rows(d), _vec(d, SUBLANES)]
        out_shape = [
            jax.ShapeDtypeStruct((s, d), F32),
            jax.ShapeDtypeStruct((s, d), BF16),
            jax.ShapeDtypeStruct((SUBLANES, d), F32),
        ]
    else:
        out_specs = [_rows(d), _vec(d, SUBLANES)]
        out_shape = [jax.ShapeDtypeStruct((s, d), F32), jax.ShapeDtypeStruct((SUBLANES, d), F32)]
    return pl.pallas_call(
        body,
        name=name,
        grid=(s // ROW_TILE,),
        in_specs=specs,
        out_specs=out_specs,
        out_shape=out_shape,
        compiler_params=_params(("arbitrary",)),
    )(*ins)


GATE_CHUNKS = 4


def _gelu(x):
    return 0.5 * x * (1.0 + jnp.tanh(GELU_C * (x + GELU_A * x * x * x)))


def _gelu_grad(x):
    t = jnp.tanh(GELU_C * (x + GELU_A * x * x * x))
    return 0.5 * (1.0 + t) + 0.5 * x * (1.0 - t * t) * (GELU_C * (1.0 + 3.0 * GELU_A * x * x))


def _causal_weights(w_ref, transpose):
    t = lax.broadcasted_iota(jnp.int32, (CHUNK, CHUNK), 0)
    s = lax.broadcasted_iota(jnp.int32, (CHUNK, CHUNK), 1)
    out = []
    for g in range(A_GROUPS):
        w = jnp.where(t >= s, w_ref[g], 0.0)
        out.append((w.T if transpose else w).astype(BF16))
    return out


def _spatial(ws, vn, lo_mask):
    rows = vn.shape[0]
    out_rows = []
    for r in range(rows // CHUNK):
        cols = []
        for j in range(A_GROUPS // 2):
            blk = vn[r * CHUNK : (r + 1) * CHUNK, j * LANES : (j + 1) * LANES]
            za = jnp.dot(ws[2 * j], blk, preferred_element_type=F32)
            zb = jnp.dot(ws[2 * j + 1], blk, preferred_element_type=F32)
            cols.append(jnp.where(lo_mask, za, zb))
        out_rows.append(jnp.concatenate(cols, axis=1))
    return jnp.concatenate(out_rows, axis=0)


def _gate_forward(a, vg, vb, ws, bias, lo_mask):
    u = _gelu(a[:, :D_MODEL])
    v = _gelu(a[:, D_MODEL:])
    vhat, rstd = _layer_norm_hat(v)
    vn = (vhat * vg + vb).astype(BF16)
    z = _spatial(ws, vn, lo_mask) + jnp.concatenate([bias] * (a.shape[0] // CHUNK), axis=0)
    return u, vhat, rstd, vn, z


def gate_fwd(a_pre, vn_g, vn_b, w_s, bias_full, *, name):
    s = a_pre.shape[0]
    tr = GATE_CHUNKS * CHUNK

    def body(a_ref, vg_ref, vb_ref, w_ref, bias_ref, p_ref):
        lo_mask = _lane((CHUNK, LANES)) < A_GROUP_DIM
        ws = _causal_weights(w_ref, transpose=False)
        u, _, _, _, z = _gate_forward(a_ref[...], vg_ref[...], vb_ref[...], ws, bias_ref[...], lo_mask)
        p_ref[...] = (u * z).astype(BF16)

    return pl.pallas_call(
        body,
        name=name,
        grid=(s // tr,),
        in_specs=[
            pl.BlockSpec((tr, 2 * D_MODEL), lambda i: (i, 0)),
            _vec(D_MODEL),
            _vec(D_MODEL),
            pl.BlockSpec((A_GROUPS, CHUNK, CHUNK), lambda i: (0, 0, 0)),
            _vec(D_MODEL, CHUNK),
        ],
        out_specs=pl.BlockSpec((tr, D_MODEL), lambda i: (i, 0)),
        out_shape=jax.ShapeDtypeStruct((s, D_MODEL), BF16),
        compiler_params=_params(("parallel",)),
    )(a_pre, vn_g, vn_b, w_s, bias_full)


def gate_bwd(a_pre, dp, vn_g, vn_b, w_s, bias_full, *, name):
    s = a_pre.shape[0]
    tr = GATE_CHUNKS * CHUNK
    nsteps = s // tr

    def body(a_ref, dp_ref, vg_ref, vb_ref, w_ref, bias_ref, da_ref, dw_ref, dbs_ref, rows_ref, dbias_acc):
        step = pl.program_id(0)
        lo_mask = _lane((CHUNK, LANES)) < A_GROUP_DIM

        @pl.when(step == 0)
        def _():
            dw_ref[...] = jnp.zeros_like(dw_ref)
            rows_ref[...] = jnp.zeros_like(rows_ref)
            dbias_acc[...] = jnp.zeros_like(dbias_acc)

        a = a_ref[...]
        vg = vg_ref[...]
        ws = _causal_weights(w_ref, transpose=False)
        wts = _causal_weights(w_ref, transpose=True)
        u, vhat, rstd, vn, z = _gate_forward(a, vg, vb_ref[...], ws, bias_ref[...], lo_mask)
        dp = dp_ref[...]
        du = dp * z
        dzz = dp * u
        dzz_b = dzz.astype(BF16)
        dvn = _spatial(wts, dzz_b, lo_mask)
        dbias = None
        for r in range(GATE_CHUNKS):
            rs = slice(r * CHUNK, (r + 1) * CHUNK)
            dbias = dzz[rs] if dbias is None else dbias + dzz[rs]
            for j in range(A_GROUPS // 2):
                cs = slice(j * LANES, (j + 1) * LANES)
                dblk = dzz[rs, cs]
                vblk = vn[rs, cs]
                for half in range(2):
                    keep = lo_mask if half == 0 else jnp.logical_not(lo_mask)
                    dm = jnp.where(keep, dblk, 0.0).astype(BF16)
                    dw_ref[2 * j + half] += lax.dot_general(
                        dm, vblk, (((1,), (1,)), ((), ())), preferred_element_type=F32
                    )
        dbias_acc[...] += dbias
        rows_ref[1:2, :D_MODEL] += jnp.sum(dvn * vhat, axis=0, keepdims=True)
        rows_ref[1:2, D_MODEL:] += jnp.sum(dvn, axis=0, keepdims=True)
        dvh = dvn * vg
        m1 = jnp.mean(dvh, axis=-1, keepdims=True)
        m2 = jnp.mean(dvh * vhat, axis=-1, keepdims=True)
        dv = rstd * (dvh - m1 - vhat * m2)
        da_u = du * _gelu_grad(a[:, :D_MODEL])
        da_v = dv * _gelu_grad(a[:, D_MODEL:])
        da_ref[:, :D_MODEL] = da_u.astype(BF16)
        da_ref[:, D_MODEL:] = da_v.astype(BF16)
        rows_ref[0:1, :D_MODEL] += jnp.sum(da_u, axis=0, keepdims=True)
        rows_ref[0:1, D_MODEL:] += jnp.sum(da_v, axis=0, keepdims=True)

        @pl.when(step == nsteps - 1)
        def _():
            t = lax.broadcasted_iota(jnp.int32, (CHUNK, CHUNK), 0)
            sx = lax.broadcasted_iota(jnp.int32, (CHUNK, CHUNK), 1)
            for g in range(A_GROUPS):
                dw_ref[g] = jnp.where(t >= sx, dw_ref[g], 0.0)
            dbs_ref[...] = _reduce_groups(dbias_acc[...])

    return pl.pallas_call(
        body,
        name=name,
        grid=(nsteps,),
        in_specs=[
            pl.BlockSpec((tr, 2 * D_MODEL), lambda i: (i, 0)),
            pl.BlockSpec((tr, D_MODEL), lambda i: (i, 0)),
            _vec(D_MODEL),
            _vec(D_MODEL),
            pl.BlockSpec((A_GROUPS, CHUNK, CHUNK), lambda i: (0, 0, 0)),
            _vec(D_MODEL, CHUNK),
        ],
        out_specs=[
            pl.BlockSpec((tr, 2 * D_MODEL), lambda i: (i, 0)),
            pl.BlockSpec((A_GROUPS, CHUNK, CHUNK), lambda i: (0, 0, 0)),
            _vec(LANES, CHUNK),
            _vec(2 * D_MODEL, SUBLANES),
        ],
        out_shape=[
            jax.ShapeDtypeStruct((s, 2 * D_MODEL), BF16),
            jax.ShapeDtypeStruct((A_GROUPS, CHUNK, CHUNK), F32),
            jax.ShapeDtypeStruct((CHUNK, LANES), F32),
            jax.ShapeDtypeStruct((SUBLANES, 2 * D_MODEL), F32),
        ],
        scratch_shapes=[pltpu.VMEM((CHUNK, D_MODEL), F32)],
        compiler_params=_params(("arbitrary",)),
    )(a_pre, dp, vn_g, vn_b, w_s, bias_full)


def alibi_tables(dilation):
    qi = jnp.arange(SPAN)[:, None]
    ki = jnp.arange(2 * SPAN)[None, :]
    diff = SPAN + qi - ki
    valid = (diff >= 0) & (diff <= SPAN)
    heads = jnp.arange(1, B_HEADS + 1, dtype=F32)
    slopes = jnp.exp2(-8.0 * heads / B_HEADS)
    bias = -slopes[:, None, None] * (dilation * diff).astype(F32)
    return jnp.where(valid[None], bias, NEG)


def _qkv_specs(pat, dilation, grid_rank_fn):
    cols = D_QKV // D_MODEL
    base = 3 * pat

    def spec(which, prev):
        def index(r, n):
            blk = jnp.maximum(n - 1, 0) if prev else n
            return (grid_rank_fn(blk), r * cols + base + which)

        return pl.BlockSpec((SPAN, D_MODEL), index)

    return [spec(0, False), spec(1, True), spec(1, False), spec(2, True), spec(2, False)]


def attn_fwd(qkv, pat, *, name):
    _, dilation = B_PATTERNS[pat]
    rows = SEQ // dilation
    nb = rows // SPAN
    qkv2 = qkv.reshape(rows, dilation * D_QKV)
    bias = alibi_tables(dilation)

    def body(q_ref, kp_ref, kc_ref, vp_ref, vc_ref, bias_ref, o_ref, lse_ref):
        n = pl.program_id(1)
        lane = _lane((SPAN, LANES))
        lo_mask = lane < B_HEAD_DIM
        first_prev = jnp.logical_and(n == 0, _lane((SPAN, 2 * SPAN)) < SPAN)
        q = q_ref[...] * jnp.asarray(B_HEAD_DIM**-0.5, BF16)
        kk = jnp.concatenate([kp_ref[...], kc_ref[...]], axis=0)
        vv = jnp.concatenate([vp_ref[...], vc_ref[...]], axis=0)
        halves = (lo_mask.astype(BF16), jnp.logical_not(lo_mask).astype(BF16))
        stats = jnp.zeros((SPAN, LANES), F32)
        for j in range(B_HEADS // 2):
            cs = slice(j * LANES, (j + 1) * LANES)
            qp, kp, vp = q[:, cs], kk[:, cs], vv[:, cs]
            outs = []
            for half in range(2):
                h = 2 * j + half
                sc = lax.dot_general(qp * halves[half], kp, (((1,), (1,)), ((), ())), preferred_element_type=F32)
                sc = jnp.where(first_prev, NEG, sc + bias_ref[h])
                m = jnp.max(sc, axis=1, keepdims=True)
                p = jnp.exp(sc - m)
                l = jnp.sum(p, axis=1, keepdims=True)
                acc = jnp.dot(p.astype(BF16), vp, preferred_element_type=F32)
                outs.append(acc * (1.0 / l))
                stats = jnp.where(lane == h, m + jnp.log(l), stats)
            o_ref[:, cs] = jnp.where(lo_mask, outs[0], outs[1])
        lse_ref[...] = stats

    return pl.pallas_call(
        body,
        name=name,
        grid=(dilation, nb),
        in_specs=[
            *_qkv_specs(pat, dilation, lambda blk: blk),
            pl.BlockSpec((B_HEADS, SPAN, 2 * SPAN), lambda r, n: (0, 0, 0)),
        ],
        out_specs=[
            pl.BlockSpec((SPAN, D_MODEL), lambda r, n: (n, r)),
            pl.BlockSpec((SPAN, LANES), lambda r, n: (n, r)),
        ],
        out_shape=[
            jax.ShapeDtypeStruct((rows, dilation * D_MODEL), F32),
            jax.ShapeDtypeStruct((rows, dilation * LANES), F32),
        ],
        compiler_params=_params(("parallel", "arbitrary")),
    )(qkv2, qkv2, qkv2, qkv2, qkv2, bias)


def attn_combine(outs, lses, *, name):
    def body(o0, o1, o2, l0, l1, l2, ob_ref, of_ref, lse_ref):
        ls = [l0[...], l1[...], l2[...]]
        m = jnp.maximum(jnp.maximum(ls[0], ls[1]), ls[2])
        tot = jnp.log(jnp.exp(ls[0] - m) + jnp.exp(ls[1] - m) + jnp.exp(ls[2] - m)) + m
        o = None
        for o_ref, l in zip((o0, o1, o2), ls, strict=True):
            term = _expand_groups(jnp.exp(l - tot)) * o_ref[...]
            o = term if o is None else o + term
        ob_ref[...] = o.astype(BF16)
        of_ref[...] = o
        lse_ref[...] = tot

    return pl.pallas_call(
        body,
        name=name,
        grid=(SEQ // ROW_TILE,),
        in_specs=[_rows(D_MODEL)] * 3 + [_rows(LANES)] * 3,
        out_specs=[_rows(D_MODEL), _rows(D_MODEL), _rows(LANES)],
        out_shape=[
            jax.ShapeDtypeStruct((SEQ, D_MODEL), BF16),
            jax.ShapeDtypeStruct((SEQ, D_MODEL), F32),
            jax.ShapeDtypeStruct((SEQ, LANES), F32),
        ],
        compiler_params=_params(("parallel",)),
    )(*outs, *lses)


def attn_delta(do, o, *, name):
    def body(do_ref, o_ref, dob_ref, dl_ref):
        do_v = do_ref[...]
        dob_ref[...] = do_v.astype(BF16)
        dl_ref[...] = _reduce_groups(do_v * o_ref[...])

    return pl.pallas_call(
        body,
        name=name,
        grid=(SEQ // ROW_TILE,),
        in_specs=[_rows(D_MODEL), _rows(D_MODEL)],
        out_specs=[_rows(D_MODEL), _rows(LANES)],
        out_shape=[jax.ShapeDtypeStruct((SEQ, D_MODEL), BF16), jax.ShapeDtypeStruct((SEQ, LANES), F32)],
        compiler_params=_params(("parallel",)),
    )(do, o)


def attn_bwd(qkv, do_b, lse, delta, pat, *, name):
    _, dilation = B_PATTERNS[pat]
    rows = SEQ // dilation
    nb = rows // SPAN
    qkv2 = qkv.reshape(rows, dilation * D_QKV)
    do2 = do_b.reshape(rows, dilation * D_MODEL)
    lse2 = lse.reshape(rows, dilation * LANES)
    delta2 = delta.reshape(rows, dilation * LANES)
    bias = alibi_tables(dilation)
    last = nb - 1

    def body(q_ref, kp_ref, kc_ref, vp_ref, vc_ref, do_ref, lse_ref, dl_ref, bias_ref, dq_ref, dk_ref, dv_ref, ck_ref, cv_ref):
        n = pl.program_id(1)

        @pl.when(n == nb)
        def _():
            dk_ref[...] = ck_ref[...].astype(BF16)
            dv_ref[...] = cv_ref[...].astype(BF16)

        @pl.when(n == 0)
        def _():
            ck_ref[...] = jnp.zeros_like(ck_ref)
            cv_ref[...] = jnp.zeros_like(cv_ref)

        @pl.when(n < nb)
        def _():
            lane = _lane((SPAN, LANES))
            lo_mask = lane < B_HEAD_DIM
            first_prev = jnp.logical_and(n == 0, _lane((SPAN, 2 * SPAN)) < SPAN)
            q = q_ref[...] * jnp.asarray(B_HEAD_DIM**-0.5, BF16)
            kk = jnp.concatenate([kp_ref[...], kc_ref[...]], axis=0)
            vv = jnp.concatenate([vp_ref[...], vc_ref[...]], axis=0)
            do_v = do_ref[...]
            lse_v = lse_ref[...]
            dl_v = dl_ref[...]
            halves = (lo_mask.astype(BF16), jnp.logical_not(lo_mask).astype(BF16))
            for j in range(B_HEADS // 2):
                cs = slice(j * LANES, (j + 1) * LANES)
                qp, kp, vp, dop = q[:, cs], kk[:, cs], vv[:, cs], do_v[:, cs]
                dqs = []
                dk_pair = None
                dv_pair = None
                for half in range(2):
                    h = 2 * j + half
                    qm = qp * halves[half]
                    dom = dop * halves[half]
                    lse_h = jnp.sum(jnp.where(lane == h, lse_v, 0.0), axis=1, keepdims=True)
                    dl_h = jnp.sum(jnp.where(lane == h, dl_v, 0.0), axis=1, keepdims=True)
                    sc = lax.dot_general(qm, kp, (((1,), (1,)), ((), ())), preferred_element_type=F32)
                    sc = jnp.where(first_prev, NEG, sc + bias_ref[h])
                    p = jnp.exp(sc - lse_h)
                    dp = lax.dot_general(dom, vp, (((1,), (1,)), ((), ())), preferred_element_type=F32)
                    ds = (p * (dp - dl_h)).astype(BF16)
                    dqs.append(jnp.dot(ds, kp, preferred_element_type=F32))
                    dk_h = lax.dot_general(ds, qm, (((0,), (0,)), ((), ())), preferred_element_type=F32)
                    dv_h = lax.dot_general(
                        p.astype(BF16), dom, (((0,), (0,)), ((), ())), preferred_element_type=F32
                    )
                    dk_pair = dk_h if dk_pair is None else dk_pair + dk_h
                    dv_pair = dv_h if dv_pair is None else dv_pair + dv_h
                dq_ref[:, cs] = (jnp.where(lo_mask, dqs[0], dqs[1]) * (B_HEAD_DIM**-0.5)).astype(BF16)
                dk_ref[:, cs] = (ck_ref[:, cs] + dk_pair[:SPAN]).astype(BF16)
                dv_ref[:, cs] = (cv_ref[:, cs] + dv_pair[:SPAN]).astype(BF16)
                ck_ref[:, cs] = dk_pair[SPAN:]
                cv_ref[:, cs] = dv_pair[SPAN:]

    def clamp(n):
        return jnp.minimum(n, last)

    def row_spec(width):
        return pl.BlockSpec((SPAN, width), lambda r, n: (clamp(n), r))

    lagged = pl.BlockSpec((SPAN, D_MODEL), lambda r, n: (jnp.maximum(n - 1, 0), r))
    outs = pl.pallas_call(
        body,
        name=name,
        grid=(dilation, nb + 1),
        in_specs=[
            *_qkv_specs(pat, dilation, clamp),
            row_spec(D_MODEL),
            row_spec(LANES),
            row_spec(LANES),
            pl.BlockSpec((B_HEADS, SPAN, 2 * SPAN), lambda r, n: (0, 0, 0)),
        ],
        out_specs=[row_spec(D_MODEL), lagged, lagged],
        out_shape=[jax.ShapeDtypeStruct((rows, dilation * D_MODEL), BF16)] * 3,
        scratch_shapes=[pltpu.VMEM((SPAN, D_MODEL), F32), pltpu.VMEM((SPAN, D_MODEL), F32)],
        compiler_params=_params(("arbitrary", "arbitrary")),
    )(qkv2, qkv2, qkv2, qkv2, qkv2, do2, lse2, delta2, bias)
    return [o.reshape(SEQ, D_MODEL) for o in outs]


def _position():
    x, y, c = lax.axis_index("x"), lax.axis_index("y"), lax.axis_index("c")
    return x, y, c, 4 * x + 2 * y + c


def _peer(k, x, y, c):
    px = 1 - x if k & 4 else x
    py = 1 - y if k & 2 else y
    pc = 1 - c if k & 1 else c
    return (px, py, pc), 4 * px + 2 * py + pc


def _remote(src, dst, send_sem, recv_sem, device):
    return pltpu.make_async_remote_copy(
        src_ref=src, dst_ref=dst, send_sem=send_sem, recv_sem=recv_sem, device_id=device, device_id_type=MESH
    )


def _silu_bf16(cf):
    return (cf * (1.0 / (1.0 + jnp.exp(-cf)))).astype(BF16)


def ada_exchange(c8, w4, b4, ln8):
    nt, _, ncol = w4.shape

    def body(c8_ref, w_ref, b_ref, ln_ref, cg_ref, lng_ref, mrecv_ref, mloc_ref, send_sems, recv_sems):
        x, y, c, me = _position()
        cg_ref[me] = c8_ref[...]
        lng_ref[me] = ln_ref[...]
        first = []
        for k in range(1, N_DEV):
            dev, _ = _peer(k, x, y, c)
            first.append(_remote(c8_ref, cg_ref.at[me], send_sems.at[0, k], recv_sems.at[0, k], dev))
            first.append(_remote(ln_ref, lng_ref.at[me], send_sems.at[1, k], recv_sems.at[1, k], dev))
        for cp in first:
            cp.start()
        for k in range(1, N_DEV):
            dev, pid = _peer(k, x, y, c)
            _remote(c8_ref, cg_ref.at[pid], send_sems.at[0, k], recv_sems.at[0, k], dev).wait_recv()
            _remote(ln_ref, lng_ref.at[pid], send_sems.at[1, k], recv_sems.at[1, k], dev).wait_recv()
        sc = _silu_bf16(cg_ref[...].reshape(N_DEV * SUBLANES, D_MODEL))
        for t in range(nt):
            mloc_ref[t] = jnp.dot(sc, w_ref[t].astype(BF16), preferred_element_type=F32) + b_ref[t : t + 1, :]

        def group(dev_id):
            return pl.ds(pl.multiple_of(dev_id * SUBLANES, SUBLANES), SUBLANES)

        mrecv_ref[me] = mloc_ref[:, group(me), :]
        second = []
        for k in range(1, N_DEV):
            dev, pid = _peer(k, x, y, c)
            second.append(
                _remote(mloc_ref.at[:, group(pid), :], mrecv_ref.at[me], send_sems.at[2, k], recv_sems.at[2, k], dev)
            )
        for cp in second:
            cp.start()
        for k in range(1, N_DEV):
            dev, pid = _peer(k, x, y, c)
            _remote(
                mloc_ref.at[:, group(pid), :], mrecv_ref.at[pid], send_sems.at[2, k], recv_sems.at[2, k], dev
            ).wait_recv()
        for cp in first + second:
            cp.wait_send()

    return pl.pallas_call(
        body,
        name="ada_exchange",
        in_specs=[VMEM, VMEM, VMEM, VMEM],
        out_specs=[VMEM, VMEM, VMEM],
        out_shape=[
            jax.ShapeDtypeStruct((N_DEV, SUBLANES, D_MODEL), F32),
            jax.ShapeDtypeStruct((N_DEV, SUBLANES, LANES), F32),
            jax.ShapeDtypeStruct((N_DEV, nt, SUBLANES, ncol), F32),
        ],
        scratch_shapes=[
            pltpu.VMEM((nt, N_DEV * SUBLANES, ncol), F32),
            pltpu.SemaphoreType.DMA((3, N_DEV)),
            pltpu.SemaphoreType.DMA((3, N_DEV)),
        ],
        compiler_params=pltpu.CompilerParams(vmem_limit_bytes=VMEM_LIMIT_BYTES),
    )(c8, w4, b4, ln8)


def small_exchange(dmx, flat):
    def body(dmx_ref, flat_ref, dmrecv_ref, red_ref, land_ref, send_sems, recv_sems):
        x, y, c, me = _position()
        dmrecv_ref[me] = dmx_ref[me]
        land_ref[me] = flat_ref[me]
        first = []
        for k in range(1, N_DEV):
            dev, pid = _peer(k, x, y, c)
            first.append(_remote(dmx_ref.at[pid], dmrecv_ref.at[me], send_sems.at[0, k], recv_sems.at[0, k], dev))
            first.append(_remote(flat_ref.at[pid], land_ref.at[me], send_sems.at[1, k], recv_sems.at[1, k], dev))
        for cp in first:
            cp.start()
        for k in range(1, N_DEV):
            dev, pid = _peer(k, x, y, c)
            _remote(dmx_ref.at[pid], dmrecv_ref.at[pid], send_sems.at[0, k], recv_sems.at[0, k], dev).wait_recv()
            _remote(flat_ref.at[pid], land_ref.at[pid], send_sems.at[1, k], recv_sems.at[1, k], dev).wait_recv()
        total = land_ref[0]
        for s in range(1, N_DEV):
            total = total + land_ref[s]
        red_ref[me] = total
        second = []
        for k in range(1, N_DEV):
            dev, _ = _peer(k, x, y, c)
            second.append(_remote(red_ref.at[me], red_ref.at[me], send_sems.at[2, k], recv_sems.at[2, k], dev))
        for cp in second:
            cp.start()
        for k in range(1, N_DEV):
            dev, pid = _peer(k, x, y, c)
            _remote(red_ref.at[pid], red_ref.at[pid], send_sems.at[2, k], recv_sems.at[2, k], dev).wait_recv()
        for cp in first + second:
            cp.wait_send()

    return pl.pallas_call(
        body,
        name="small_exchange",
        in_specs=[VMEM, VMEM],
        out_specs=[VMEM, VMEM],
        out_shape=[jax.ShapeDtypeStruct(dmx.shape, F32), jax.ShapeDtypeStruct(flat.shape, F32)],
        scratch_shapes=[
            pltpu.VMEM(flat.shape, F32),
            pltpu.SemaphoreType.DMA((3, N_DEV)),
            pltpu.SemaphoreType.DMA((3, N_DEV)),
        ],
        compiler_params=pltpu.CompilerParams(vmem_limit_bytes=VMEM_LIMIT_BYTES),
    )(dmx, flat)


HBM = pl.BlockSpec(memory_space=pltpu.HBM)
SEM = pl.BlockSpec(memory_space=pltpu.SEMAPHORE)
EFFECT = pltpu.SideEffectType.DATAFLOW_SIDE_EFFECTING


def _own_slot(me, block):
    land = lax.empty((N_DEV, *block.shape), block.dtype)
    return lax.dynamic_update_slice_in_dim(land, block[None], me, axis=0)


class Exchange:
    def __init__(self, srcs, lands, after, *, scatter, name):
        self.scatter = scatter
        self.name = name
        nt = self.nt = len(srcs)
        peers = N_DEV - 1

        def body(*refs):
            src_refs, land_refs = refs[:nt], refs[nt : 2 * nt]
            send_sems, recv_sems = refs[2 * nt + 1 : 3 * nt + 1], refs[3 * nt + 1 : 4 * nt + 1]
            token = refs[-1]
            x, y, c, me = _position()
            for t in range(nt):
                for k in range(1, N_DEV):
                    dev, pid = _peer(k, x, y, c)
                    src = src_refs[t].at[pid] if scatter else src_refs[t]
                    _remote(src, land_refs[t].at[me], send_sems[t].at[k - 1], recv_sems[t].at[k - 1], dev).start()
            token[...] = jnp.zeros_like(token)

        outs = pl.pallas_call(
            body,
            name=name + "_start",
            in_specs=[HBM] * (2 * nt) + [ANY],
            out_specs=[SEM] * (2 * nt) + [HBM] * (2 * nt) + [VMEM],
            out_shape=[pltpu.SemaphoreType.DMA((peers,))] * (2 * nt)
            + [pltpu.HBM(a.shape, a.dtype) for a in (*srcs, *lands)]
            + [jax.ShapeDtypeStruct((SUBLANES, LANES), F32)],
            input_output_aliases={i: 2 * nt + i for i in range(2 * nt)},
            compiler_params=pltpu.CompilerParams(has_side_effects=EFFECT),
        )(*[pltpu.with_memory_space_constraint(a, pltpu.HBM) for a in (*srcs, *lands)], after)
        self.send_sems, self.recv_sems = outs[:nt], outs[nt : 2 * nt]
        self.srcs, self.lands = outs[2 * nt : 3 * nt], outs[3 * nt : 4 * nt]
        self.token = outs[-1]

    def zero(self):
        return self.token[0, 0]

    def wait(self, which, after, *, name):
        scatter = self.scatter
        n = len(which)

        def body(*refs):
            src_refs, land_refs = refs[:n], refs[n : 2 * n]
            send_sems, recv_sems = refs[2 * n : 3 * n], refs[3 * n : 4 * n]
            x, y, c, _ = _position()
            for t in range(n):
                for k in range(1, N_DEV):
                    dev, pid = _peer(k, x, y, c)
                    src = src_refs[t].at[pid] if scatter else src_refs[t]
                    cp = _remote(src, land_refs[t].at[pid], send_sems[t].at[k - 1], recv_sems[t].at[k - 1], dev)
                    cp.wait_send()
                    cp.wait_recv()

        srcs = [self.srcs[t] for t in which]
        lands = [self.lands[t] for t in which]
        outs = pl.pallas_call(
            body,
            name=name,
            in_specs=[HBM] * (2 * n) + [SEM] * (2 * n) + [ANY],
            out_specs=[HBM] * (2 * n),
            out_shape=[pltpu.HBM(a.shape, a.dtype) for a in (*srcs, *lands)],
            input_output_aliases={i: i for i in range(2 * n)},
            compiler_params=pltpu.CompilerParams(has_side_effects=EFFECT),
        )(*srcs, *lands, *[self.send_sems[t] for t in which], *[self.recv_sems[t] for t in which], after)
        return outs[n:]


def _adam_update(g, w, m, v):
    m2 = ADAM_B1 * m + (1.0 - ADAM_B1) * g
    v2 = ADAM_B2 * v + (1.0 - ADAM_B2) * jnp.square(g)
    m_hat = m2 / (1.0 - ADAM_B1**ADAM_STEP)
    v_hat = v2 / (1.0 - ADAM_B2**ADAM_STEP)
    delta = -ADAM_LR * (m_hat / (jnp.sqrt(v_hat) + ADAM_EPS) + ADAM_WD * w)
    return delta, m2, v2


def adamw(gparts, w, m, v, *, name):
    p, r, c = gparts.shape
    tr = r if r <= 256 else (256 if c <= D_MODEL else 128)

    def body(g_ref, w_ref, m_ref, v_ref, go_ref, d_ref, mo_ref, vo_ref):
        g = g_ref[0].astype(F32)
        for i in range(1, p):
            g = g + g_ref[i].astype(F32)
        delta, m2, v2 = _adam_update(g, w_ref[...], m_ref[...], v_ref[...])
        go_ref[...] = g
        d_ref[...] = delta
        mo_ref[...] = m2
        vo_ref[...] = v2

    blk = pl.BlockSpec((tr, c), lambda i: (i, 0))
    return pl.pallas_call(
        body,
        name=name,
        grid=(r // tr,),
        in_specs=[pl.BlockSpec((p, tr, c), lambda i: (0, i, 0)), blk, blk, blk],
        out_specs=[blk] * 4,
        out_shape=[jax.ShapeDtypeStruct((r, c), F32)] * 4,
        compiler_params=_params(("parallel",)),
    )(gparts, w, m, v)


def ada_grad_adamw(cg, dmrecv, w4, m4, v4, *, name):
    nt, k, ncol = w4.shape

    def body(cg_ref, dm_ref, w_ref, m_ref, v_ref, go_ref, d_ref, mo_ref, vo_ref, gb_ref):
        sc = _silu_bf16(cg_ref[...].reshape(N_DEV * SUBLANES, k))
        dm = dm_ref[...].reshape(N_DEV * SUBLANES, ncol)
        g = lax.dot_general(sc, dm.astype(BF16), (((0,), (0,)), ((), ())), preferred_element_type=F32)
        delta, m2, v2 = _adam_update(g, w_ref[...], m_ref[...], v_ref[...])
        go_ref[...] = g
        d_ref[...] = delta
        mo_ref[...] = m2
        vo_ref[...] = v2
        gb_ref[...] = jnp.broadcast_to(jnp.sum(dm, axis=0, keepdims=True), (SUBLANES, ncol))

    wblk = pl.BlockSpec((None, k, ncol), lambda t: (t, 0, 0))
    return pl.pallas_call(
        body,
        name=name,
        grid=(nt,),
        in_specs=[
            pl.BlockSpec((N_DEV, SUBLANES, k), lambda t: (0, 0, 0)),
            pl.BlockSpec((N_DEV, None, SUBLANES, ncol), lambda t: (0, t, 0, 0)),
            wblk,
            wblk,
            wblk,
        ],
        out_specs=[wblk] * 4 + [pl.BlockSpec((None, SUBLANES, ncol), lambda t: (t, 0, 0))],
        out_shape=[jax.ShapeDtypeStruct((nt, k, ncol), F32)] * 4 + [jax.ShapeDtypeStruct((nt, SUBLANES, ncol), F32)],
        compiler_params=_params(("parallel",)),
    )(cg, dmrecv, w4, m4, v4)


def kernel(x, c, ada_w, ada_b, ln_g, ln_b, a_w_in, a_b_in, a_vn_g, a_vn_b, a_w_s, a_b_s, a_w_out, b_w_qkv, b_w_out, mlp_w_up, mlp_w_down, loss_target, m_ada_w, m_ada_b, m_ln_g, m_ln_b, m_a_w_in, m_a_b_in, m_a_vn_g, m_a_vn_b, m_a_w_s, m_a_b_s, m_a_w_out, m_b_w_qkv, m_b_w_out, m_mlp_w_up, m_mlp_w_down, v_ada_w, v_ada_b, v_ln_g, v_ln_b, v_a_w_in, v_a_b_in, v_a_vn_g, v_a_vn_b, v_a_w_s, v_a_b_s, v_a_w_out, v_b_w_qkv, v_b_w_out, v_mlp_w_up, v_mlp_w_down):
    x0 = x[0]
    target = loss_target[0]
    me = 4 * lax.axis_index("x") + 2 * lax.axis_index("y") + lax.axis_index("c")

    ada_w4 = ada_w.reshape(N_SUB, D_MODEL, -1)
    ada_b4 = ada_b.reshape(N_SUB, -1)
    ln8 = jnp.concatenate([ln_g.reshape(N_SUB, -1), ln_b.reshape(N_SUB, -1)], axis=0)
    c8 = jnp.broadcast_to(c, (SUBLANES, D_MODEL))
    cg, lng, mrecv = ada_exchange(c8, ada_w4, ada_b4, ln8)

    W_IN, W_AOUT, W_UP0, W_DN0, W_QKV, W_BOUT, W_UP1, W_DN1 = range(8)
    shards = [
        a_w_in[0].astype(BF16),
        a_w_out[0].astype(BF16),
        mlp_w_up[0].astype(BF16),
        mlp_w_down[0].astype(BF16),
        b_w_qkv[0].astype(BF16),
        b_w_out[0].astype(BF16),
        mlp_w_up[1].astype(BF16),
        mlp_w_down[1].astype(BF16),
    ]
    gather = Exchange(shards, [_own_slot(me, s) for s in shards], mrecv, scatter=False, name="gather")

    modv = mrecv[:, :, 0, :].transpose(1, 0, 2).reshape(N_SUB, 3 * D_MODEL) + gather.zero()
    shift = [modv[t : t + 1, :D_MODEL] for t in range(N_SUB)]
    scale = [modv[t : t + 1, D_MODEL : 2 * D_MODEL] for t in range(N_SUB)]
    gate1 = [1.0 + modv[t : t + 1, 2 * D_MODEL :] for t in range(N_SUB)]
    lng_full = [lng[:, t, :].reshape(1, D_MODEL) for t in range(N_SUB)]
    lnb_full = [lng[:, N_SUB + t, :].reshape(1, D_MODEL) for t in range(N_SUB)]

    ident = lambda acc: (acc,)
    relu2 = lambda acc: (acc, jnp.square(jnp.maximum(acc, 0.0)))
    vn_g, vn_b, w_s = a_vn_g, a_vn_b, a_w_s[0]
    bias_full = jnp.repeat(a_b_s[0].T, A_GROUP_DIM, axis=1)
    w_up3, w_dn3 = [None, None], [None, None]

    def mlp_forward(i, h, after):
        up, dn = gather.wait([W_UP0, W_DN0] if i == 0 else [W_UP1, W_DN1], after, name=f"gather_wait_mlp{i}")
        w_up3[i], w_dn3[i] = up, dn.reshape(1, D_FF, D_MODEL)
        a, r = mm_nn(h, w_up3[i], name=f"mlp{i}_up", tm=1024, tn=512, epilogue=relu2, outs=(BF16, BF16))
        (y,) = mm_nn(r, w_dn3[i], name=f"mlp{i}_down", tm=512, tn=512, epilogue=ident, outs=(F32,))
        return a, r, y

    h0 = modulate(x0, scale[0], shift[0], name="modulate0")
    w_in3, w_aout3 = gather.wait([W_IN, W_AOUT], h0, name="gather_wait_a")
    w_aout3 = w_aout3.reshape(1, D_MODEL, D_MODEL)
    (a_pre,) = mm_nn(
        h0, w_in3, name="a_in", tm=1024, tn=256, epilogue=lambda acc, b: (acc + b,), extras=[(a_b_in, "row")], outs=(F32,)
    )
    p_gate = gate_fwd(a_pre, vn_g, vn_b, w_s, bias_full, name="gate_fwd")
    (y0,) = mm_nn(p_gate, w_aout3, name="a_out", tm=1024, tn=512, epilogue=ident, outs=(F32,))
    x1, h1 = residual_ln(x0, y0, gate1[0], lng_full[0], lnb_full[0], scale[1], shift[1], name="res_ln0")
    a1, r1, y1 = mlp_forward(0, h1, y0)
    x2, h2 = residual_ln(x1, y1, gate1[1], lng_full[1], lnb_full[1], scale[2], shift[2], name="res_ln1")
    w_qkv3, w_bout3 = gather.wait([W_QKV, W_BOUT], y1, name="gather_wait_b")
    w_bout3 = w_bout3.reshape(1, D_MODEL, D_MODEL)
    (qkv,) = mm_nn(h2, w_qkv3, name="b_qkv", tm=1024, tn=1152, epilogue=ident, outs=(BF16,))
    pat_o, pat_lse = [], []
    for pat in range(N_PAT):
        o_p, lse_p = attn_fwd(qkv, pat, name=f"attn_fwd{pat}")
        pat_o.append(o_p.reshape(SEQ, D_MODEL))
        pat_lse.append(lse_p.reshape(SEQ, LANES))
    o_b, o_f, lse = attn_combine(pat_o, pat_lse, name="attn_combine")
    (y2,) = mm_nn(o_b, w_bout3, name="b_out", tm=1024, tn=512, epilogue=ident, outs=(F32,))
    x3, h3 = residual_ln(x2, y2, gate1[2], lng_full[2], lnb_full[2], scale[3], shift[3], name="res_ln2")
    a3, r3, y3 = mlp_forward(1, h3, y2)
    dxo, loss_local = residual_ln_loss(x3, y3, gate1[3], lng_full[3], lnb_full[3], target, name="res_ln3_loss")
    loss = lax.psum(loss_local[0, 0], ("x", "y", "c"))

    def scatter(parts, after, name):
        parts = [p.reshape(N_DEV, -1, p.shape[-1]) for p in parts]
        lands = [_own_slot(me, lax.dynamic_index_in_dim(p, me, 0, keepdims=False)) for p in parts]
        return Exchange(parts, lands, after, scatter=True, name=name)

    def mlp_backward(i, h, a, r, dy):
        (da,) = mm_nt(
            dy,
            w_dn3[i],
            name=f"mlp{i}_da",
            tm=1024,
            tko=1024,
            ps=1,
            epilogue=lambda acc, act: (acc * (2.0 * jnp.maximum(act.astype(F32), 0.0)),),
            extras=[(a, "full")],
            outs=(BF16,),
        )
        (dh,) = mm_nt(da, w_up3[i], name=f"mlp{i}_dh", tm=1024, tko=1024, ps=2, epilogue=ident, outs=(F32,))
        dw_dn = mm_tn(r, dy, name=f"mlp{i}_dw_down", p=1, tk=1024, tn=512, tmc=1024, out_dtype=BF16)
        dw_up = mm_tn(h, da, name=f"mlp{i}_dw_up", p=N_DEV, tk=1024, tn=512, tmc=1024, out_dtype=BF16)
        return scatter([dw_up, dw_dn], dh, f"scatter_mlp{i}"), dh

    dz3, dy3, st3 = residual_ln_bwd(name="res_bwd3", dxo=dxo, this=(x3, y3, gate1[3], lng_full[3]))
    rs_mlp1, dh3 = mlp_backward(1, h3, a3, r3, dy3)
    dz2, dy2, st2 = residual_ln_bwd(
        name="res_bwd2", later=(dh3, dz3, x3, scale[3] + rs_mlp1.zero()), this=(x2, y2, gate1[2], lng_full[2])
    )
    (d_o,) = mm_nt(dy2, w_bout3, name="b_do", tm=1024, tko=1024, ps=1, epilogue=ident, outs=(F32,))
    do_b, delta = attn_delta(d_o, o_f, name="attn_delta")
    dqkv_parts = []
    for pat in range(N_PAT):
        dqkv_parts += attn_bwd(qkv, do_b, lse, delta, pat, name=f"attn_bwd{pat}")
    dqkv = jnp.concatenate(dqkv_parts, axis=1)
    (dh2,) = mm_nt(dqkv, w_qkv3, name="b_dh", tm=1024, tko=1024, ps=1, epilogue=ident, outs=(F32,))
    dw_bout = mm_tn(o_b, dy2, name="b_dw_out", p=1, tk=1024, tn=512, tmc=1024, out_dtype=BF16)
    dw_qkv = mm_tn(h2, dqkv, name="b_dw_qkv", p=N_DEV, tk=512, tn=1152, tmc=1024, out_dtype=BF16)
    rs_b = scatter([dw_qkv, dw_bout], dh2, "scatter_b")
    dz1, dy1, st1 = residual_ln_bwd(
        name="res_bwd1", later=(dh2, dz2, x2, scale[2] + rs_b.zero()), this=(x1, y1, gate1[1], lng_full[1])
    )
    rs_mlp0, dh1 = mlp_backward(0, h1, a1, r1, dy1)
    dz0, dy0, st0 = residual_ln_bwd(
        name="res_bwd0", later=(dh1, dz1, x1, scale[1] + rs_mlp0.zero()), this=(x0, y0, gate1[0], lng_full[0])
    )
    (dp_gate,) = mm_nt(dy0, w_aout3, name="a_dp", tm=1024, tko=1024, ps=1, epilogue=ident, outs=(F32,))
    da0, d_ws, d_bs, gate_rows = gate_bwd(a_pre, dp_gate, vn_g, vn_b, w_s, bias_full, name="gate_bwd")
    (dh0,) = mm_nt(da0, w_in3, name="a_dh", tm=1024, tko=1024, ps=4, epilogue=ident, outs=(F32,))
    dw_aout = mm_tn(p_gate, dy0, name="a_dw_out", p=1, tk=1024, tn=512, tmc=1024, out_dtype=BF16)
    dw_in = mm_tn(h0, da0, name="a_dw_in", p=N_DEV, tk=1024, tn=256, tmc=1024, out_dtype=BF16)
    grad_x, stf = residual_ln_bwd(name="res_bwd_in", later=(dh0, dz0, x0, scale[0]))

    stats_after = [stf, st0, st1, st2]
    stats_own = [st0, st1, st2, st3]
    dm = jnp.stack(
        [
            jnp.concatenate(
                [stats_after[t][ST_DSHIFT], stats_after[t][ST_DSCALE], stats_own[t][ST_DGATE]], axis=0
            )
            for t in range(N_SUB)
        ]
    )
    ncol = 3 * D_MODEL // N_DEV
    dmx = jnp.pad(
        dm.reshape(N_SUB, N_DEV, ncol).transpose(1, 0, 2)[:, :, None, :], ((0, 0), (0, 0), (0, SUBLANES - 1), (0, 0))
    )
    small = [
        gate_rows[0],
        gate_rows[1],
        d_ws.reshape(-1),
        d_bs[:, :A_GROUPS].T.reshape(-1),
        *[stats_own[t][ST_DG] for t in range(N_SUB)],
        *[stats_own[t][ST_DB] for t in range(N_SUB)],
    ]
    n_small = sum(s.size for s in small)
    part_rows = -(-n_small // (N_DEV * LANES * SUBLANES)) * SUBLANES
    flat = jnp.concatenate(small + [jnp.zeros((N_DEV * part_rows * LANES - n_small,), F32)])
    dmrecv, reduced = small_exchange(dmx, flat.reshape(N_DEV, part_rows, LANES))
    reduced = reduced.reshape(-1)
    sizes = [2 * D_MODEL, D_MODEL, D_MODEL, A_GROUPS * CHUNK * CHUNK, A_GROUPS * CHUNK, N_SUB * D_MODEL, N_SUB * D_MODEL]
    offs = [sum(sizes[:i]) for i in range(len(sizes) + 1)]
    g_b_in, g_vn_g, g_vn_b, g_ws, g_bs, g_lng, g_lnb = [reduced[offs[i] : offs[i + 1]] for i in range(len(sizes))]

    results = {}

    def update(wname, gparts, w, m, v):
        shape = w.shape
        w2 = w.reshape(-1, shape[-1])
        outs = adamw(gparts.reshape(gparts.shape[0], *w2.shape), w2, m.reshape(w2.shape), v.reshape(w2.shape), name=f"adamw_{wname}")
        results[wname] = [o.reshape(shape) for o in outs]

    rs_a = scatter([dw_in, dw_aout], reduced, "scatter_a")
    ada_outs = ada_grad_adamw(cg + rs_a.zero(), dmrecv, ada_w4, m_ada_w.reshape(ada_w4.shape), v_ada_w.reshape(ada_w4.shape), name="ada_grad_adamw")
    results["ada_w"] = [o.reshape(ada_w.shape) for o in ada_outs[:4]]
    update("ada_b", ada_outs[4][:, 0, :][None], ada_b, m_ada_b, v_ada_b)
    ln_cols = D_MODEL // N_DEV
    my_ln = lambda gfull: lax.dynamic_slice_in_dim(gfull.reshape(N_SUB, N_DEV, ln_cols), me, 1, axis=1)
    update("ln_g", my_ln(g_lng).reshape(1, N_SUB, ln_cols), ln_g, m_ln_g, v_ln_g)
    update("ln_b", my_ln(g_lnb).reshape(1, N_SUB, ln_cols), ln_b, m_ln_b, v_ln_b)
    update("a_b_in", g_b_in[None], a_b_in, m_a_b_in, v_a_b_in)
    update("a_vn_g", g_vn_g[None], a_vn_g, m_a_vn_g, v_a_vn_g)
    update("a_vn_b", g_vn_b[None], a_vn_b, m_a_vn_b, v_a_vn_b)
    update("a_w_s", g_ws[None], a_w_s, m_a_w_s, v_a_w_s)
    update("a_b_s", g_bs[None], a_b_s, m_a_b_s, v_a_b_s)
    g_up1, g_dn1 = rs_mlp1.wait([0, 1], grad_x, name="scatter_wait_mlp1")
    g_qkv, g_bout = rs_b.wait([0, 1], grad_x, name="scatter_wait_b")
    update("b_w_qkv", g_qkv, b_w_qkv, m_b_w_qkv, v_b_w_qkv)
    update("b_w_out", g_bout, b_w_out, m_b_w_out, v_b_w_out)
    g_up0, g_dn0 = rs_mlp0.wait([0, 1], grad_x, name="scatter_wait_mlp0")
    update("mlp_w_up", jnp.concatenate([g_up0, g_up1], axis=1), mlp_w_up, m_mlp_w_up, v_mlp_w_up)
    update("mlp_w_down", jnp.concatenate([g_dn0, g_dn1], axis=1), mlp_w_down, m_mlp_w_down, v_mlp_w_down)
    g_in, g_aout = rs_a.wait([0, 1], grad_x, name="scatter_wait_a")
    update("a_w_in", g_in, a_w_in, m_a_w_in, v_a_w_in)
    update("a_w_out", g_aout, a_w_out, m_a_w_out, v_a_w_out)

    order = ["ada_w", "ada_b", "ln_g", "ln_b", "a_w_in", "a_b_in", "a_vn_g", "a_vn_b", "a_w_s", "a_b_s", "a_w_out", "b_w_qkv", "b_w_out", "mlp_w_up", "mlp_w_down"]
    return (loss, grad_x[None], *[results[n][0] for n in order], *[results[n][1] for n in order],
            *[results[n][2] for n in order], *[results[n][3] for n in order])
```

```python
import math

import jax
import jax.numpy as jnp
from jax import lax
from jax.experimental import pallas as pl
from jax.experimental.pallas import tpu as pltpu

F32 = jnp.float32
BF16 = jnp.bfloat16
MESH = pl.DeviceIdType.MESH
ANY = pl.BlockSpec(memory_space=pl.ANY)
VMEM = pl.BlockSpec(memory_space=pltpu.VMEM)

N_DEV = 8
D_MODEL = 1024
SEQ = 4096
DEPTH = 2
CHUNK = 128
A_GROUPS = 16
A_GROUP_DIM = D_MODEL // A_GROUPS
B_HEADS = 16
B_HEAD_DIM = 64
B_PATTERNS = ((128, 1), (512, 4), (2048, 16))
N_PAT = len(B_PATTERNS)
SPAN = 128
D_FF = 4 * D_MODEL
D_QKV = N_PAT * 3 * D_MODEL
ALPHA = (2 * DEPTH) ** 0.25
LN_EPS = 1e-5
NEG = -1e30
ADAM_LR = 0.001
ADAM_B1 = 0.9
ADAM_B2 = 0.999
ADAM_EPS = 1e-08
ADAM_WD = 0.01
ADAM_STEP = 10
GELU_C = math.sqrt(2.0 / math.pi)
GELU_A = 0.044715

VMEM_LIMIT_BYTES = 56 * 1024 * 1024
LANES = 128
SUBLANES = 8
ROW_TILE = 512
N_SUB = 2 * DEPTH


def _params(sem):
    return pltpu.CompilerParams(dimension_semantics=sem, vmem_limit_bytes=VMEM_LIMIT_BYTES)


def _lane(shape):
    return lax.broadcasted_iota(jnp.int32, shape, len(shape) - 1)


def _split_bf16(x):
    hi = x.astype(BF16)
    lo = (x - hi.astype(F32)).astype(BF16)
    return hi, lo


def _group_expand_matrix(groups_padded, width):
    per = width // A_GROUPS
    r = lax.broadcasted_iota(jnp.int32, (groups_padded, width), 0)
    c = lax.broadcasted_iota(jnp.int32, (groups_padded, width), 1)
    return (c // per == r).astype(BF16)


def _group_reduce_matrix(width, groups_padded):
    per = width // A_GROUPS
    r = lax.broadcasted_iota(jnp.int32, (width, groups_padded), 0)
    c = lax.broadcasted_iota(jnp.int32, (width, groups_padded), 1)
    return (r // per == c).astype(BF16)


def _expand_groups(w):
    e = _group_expand_matrix(LANES, D_MODEL)
    hi, lo = _split_bf16(w)
    return jnp.dot(hi, e, preferred_element_type=F32) + jnp.dot(lo, e, preferred_element_type=F32)


def _reduce_groups(x):
    e = _group_reduce_matrix(D_MODEL, LANES)
    hi, lo = _split_bf16(x)
    return jnp.dot(hi, e, preferred_element_type=F32) + jnp.dot(lo, e, preferred_element_type=F32)


def mm_nn(a, b3, *, name, tm, tn, epilogue, extras=(), outs):
    m, k = a.shape
    p, _, n = b3.shape
    q = n // tn
    grid = (m // tm, p * q)

    def body(a_ref, b_ref, *rest):
        ex = rest[: len(extras)]
        out_refs = rest[len(extras) :]
        acc = jnp.dot(a_ref[...], b_ref[...], preferred_element_type=F32)
        res = epilogue(acc, *[e[...] for e in ex])
        for o_ref, r in zip(out_refs, res, strict=True):
            o_ref[...] = r.astype(o_ref.dtype)

    ex_specs = []
    for _, kind in extras:
        if kind == "row":
            ex_specs.append(pl.BlockSpec((1, tn), lambda i, j: (0, j)))
        else:
            ex_specs.append(pl.BlockSpec((tm, tn), lambda i, j: (i, j)))
    return pl.pallas_call(
        body,
        name=name,
        grid=grid,
        in_specs=[
            pl.BlockSpec((tm, k), lambda i, j: (i, 0)),
            pl.BlockSpec((None, k, tn), lambda i, j: (j // q, 0, j % q)),
            *ex_specs,
        ],
        out_specs=[pl.BlockSpec((tm, tn), lambda i, j: (i, j)) for _ in outs],
        out_shape=[jax.ShapeDtypeStruct((m, p * n), dt) for dt in outs],
        compiler_params=_params(("parallel", "parallel")),
    )(a, b3, *[arr for arr, _ in extras])


def mm_nt(g, b3, *, name, tm, tko, ps, epilogue, extras=(), outs):
    m = g.shape[0]
    p, k, n = b3.shape
    nc = p // ps
    grid = (m // tm, k // tko, nc)

    def body(g_ref, b_ref, *rest):
        ex = rest[: len(extras)]
        out_refs = rest[len(extras) : len(extras) + len(outs)]
        acc_ref = rest[-1]
        c = pl.program_id(2)
        part = None
        for s in range(ps):
            d = lax.dot_general(
                g_ref[:, s * n : (s + 1) * n], b_ref[s], (((1,), (1,)), ((), ())), preferred_element_type=F32
            )
            part = d if part is None else part + d

        @pl.when(c == 0)
        def _():
            acc_ref[...] = part

        @pl.when(c > 0)
        def _():
            acc_ref[...] += part

        @pl.when(c == nc - 1)
        def _():
            res = epilogue(acc_ref[...], *[e[...] for e in ex])
            for o_ref, r in zip(out_refs, res, strict=True):
                o_ref[...] = r.astype(o_ref.dtype)

    ex_specs = []
    for _, kind in extras:
        if kind == "row":
            ex_specs.append(pl.BlockSpec((1, tko), lambda i, j, c: (0, j)))
        else:
            ex_specs.append(pl.BlockSpec((tm, tko), lambda i, j, c: (i, j)))
    return pl.pallas_call(
        body,
        name=name,
        grid=grid,
        in_specs=[
            pl.BlockSpec((tm, ps * n), lambda i, j, c: (i, c)),
            pl.BlockSpec((ps, tko, n), lambda i, j, c: (c, j, 0)),
            *ex_specs,
        ],
        out_specs=[pl.BlockSpec((tm, tko), lambda i, j, c: (i, j)) for _ in outs],
        out_shape=[jax.ShapeDtypeStruct((m, k), dt) for dt in outs],
        scratch_shapes=[pltpu.VMEM((tm, tko), F32)],
        compiler_params=_params(("parallel", "parallel", "arbitrary")),
    )(g, b3, *[arr for arr, _ in extras])


def mm_tn(a, g, *, name, p, tk, tn, tmc, out_dtype):
    m, k = a.shape
    n = g.shape[1] // p
    q = n // tn
    nc = m // tmc
    grid = (k // tk, p * q, nc)

    def body(a_ref, g_ref, o_ref, acc_ref):
        c = pl.program_id(2)
        part = lax.dot_general(a_ref[...], g_ref[...], (((0,), (0,)), ((), ())), preferred_element_type=F32)

        @pl.when(c == 0)
        def _():
            acc_ref[...] = part

        @pl.when(c > 0)
        def _():
            acc_ref[...] += part

        @pl.when(c == nc - 1)
        def _():
            o_ref[...] = acc_ref[...].astype(o_ref.dtype)

    return pl.pallas_call(
        body,
        name=name,
        grid=grid,
        in_specs=[
            pl.BlockSpec((tmc, tk), lambda i, j, c: (c, i)),
            pl.BlockSpec((tmc, tn), lambda i, j, c: (c, j)),
        ],
        out_specs=pl.BlockSpec((None, tk, tn), lambda i, j, c: (j // q, i, j % q)),
        out_shape=jax.ShapeDtypeStruct((p, k, n), out_dtype),
        scratch_shapes=[pltpu.VMEM((tk, tn), F32)],
        compiler_params=_params(("parallel", "parallel", "arbitrary")),
    )(a, g)


def _rows(cols):
    return pl.BlockSpec((ROW_TILE, cols), lambda i: (i, 0))


def _vec(cols, rows=1):
    return pl.BlockSpec((rows, cols), lambda i: (0, 0))


def _layer_norm_hat(z):
    mu = jnp.mean(z, axis=-1, keepdims=True)
    zc = z - mu
    var = jnp.mean(zc * zc, axis=-1, keepdims=True)
    rstd = lax.rsqrt(var + LN_EPS)
    return zc * rstd, rstd


def modulate(x, scale, shift, *, name):
    s, d = x.shape

    def body(x_ref, sc_ref, sh_ref, h_ref):
        h_ref[...] = (x_ref[...] * (1.0 + sc_ref[...]) + sh_ref[...]).astype(BF16)

    return pl.pallas_call(
        body,
        name=name,
        grid=(s // ROW_TILE,),
        in_specs=[_rows(d), _vec(d), _vec(d)],
        out_specs=_rows(d),
        out_shape=jax.ShapeDtypeStruct((s, d), BF16),
        compiler_params=_params(("parallel",)),
    )(x, scale, shift)


def residual_ln(x, y, gate1, g, b, nscale, nshift, *, name):
    s, d = x.shape

    def body(x_ref, y_ref, gt_ref, g_ref, b_ref, sc_ref, sh_ref, xn_ref, hn_ref):
        z = ALPHA * x_ref[...] + gt_ref[...] * y_ref[...]
        xhat, _ = _layer_norm_hat(z)
        xn = xhat * g_ref[...] + b_ref[...]
        xn_ref[...] = xn
        hn_ref[...] = (xn * (1.0 + sc_ref[...]) + sh_ref[...]).astype(BF16)

    return pl.pallas_call(
        body,
        name=name,
        grid=(s // ROW_TILE,),
        in_specs=[_rows(d), _rows(d), _vec(d), _vec(d), _vec(d), _vec(d), _vec(d)],
        out_specs=[_rows(d), _rows(d)],
        out_shape=[jax.ShapeDtypeStruct((s, d), F32), jax.ShapeDtypeStruct((s, d), BF16)],
        compiler_params=_params(("parallel",)),
    )(x, y, gate1, g, b, nscale, nshift)


def residual_ln_loss(x, y, gate1, g, b, target, *, name):
    s, d = x.shape

    def body(x_ref, y_ref, gt_ref, g_ref, b_ref, t_ref, dx_ref, loss_ref):
        z = ALPHA * x_ref[...] + gt_ref[...] * y_ref[...]
        xhat, _ = _layer_norm_hat(z)
        err = xhat * g_ref[...] + b_ref[...] - t_ref[...]
        dx_ref[...] = err * (1.0 / d)
        part = jnp.sum(jnp.sum(err * err, axis=1, keepdims=True), axis=0, keepdims=True) * (0.5 / d)

        @pl.when(pl.program_id(0) == 0)
        def _():
            loss_ref[...] = part

        @pl.when(pl.program_id(0) > 0)
        def _():
            loss_ref[...] += part

    return pl.pallas_call(
        body,
        name=name,
        grid=(s // ROW_TILE,),
        in_specs=[_rows(d), _rows(d), _vec(d), _vec(d), _vec(d), _rows(d)],
        out_specs=[_rows(d), pl.BlockSpec((1, 1), lambda i: (0, 0))],
        out_shape=[jax.ShapeDtypeStruct((s, d), F32), jax.ShapeDtypeStruct((1, 1), F32)],
        compiler_params=_params(("arbitrary",)),
    )(x, y, gate1, g, b, target)


ST_DSCALE, ST_DSHIFT, ST_DG, ST_DB, ST_DGATE = 0, 1, 2, 3, 4


def residual_ln_bwd(*, name, later=None, dxo=None, this=None):
    lead = later[0] if later is not None else dxo
    s, d = lead.shape
    has_later, has_ln = later is not None, this is not None

    def body(*refs):
        refs = list(refs)
        if has_later:
            dh_ref, dzl_ref, xm_ref, scl_ref = refs[:4]
            refs = refs[4:]
        else:
            dxo_ref = refs.pop(0)
        if has_ln:
            x_ref, y_ref, gt_ref, g_ref = refs[:4]
            refs = refs[4:]
            dz_ref, dy_ref, st_ref = refs
        else:
            dx_ref, st_ref = refs

        @pl.when(pl.program_id(0) == 0)
        def _():
            st_ref[...] = jnp.zeros_like(st_ref)

        def acc(row, val):
            st_ref[row : row + 1, :] += jnp.sum(val, axis=0, keepdims=True)

        if has_later:
            dh = dh_ref[...]
            g_out = ALPHA * dzl_ref[...] + dh * (1.0 + scl_ref[...])
            acc(ST_DSCALE, dh * xm_ref[...])
            acc(ST_DSHIFT, dh)
        else:
            g_out = dxo_ref[...]
        if not has_ln:
            dx_ref[...] = g_out
            return
        y = y_ref[...]
        gate1 = gt_ref[...]
        z = ALPHA * x_ref[...] + gate1 * y
        xhat, rstd = _layer_norm_hat(z)
        acc(ST_DG, g_out * xhat)
        acc(ST_DB, g_out)
        dxh = g_out * g_ref[...]
        m1 = jnp.mean(dxh, axis=-1, keepdims=True)
        m2 = jnp.mean(dxh * xhat, axis=-1, keepdims=True)
        dz = rstd * (dxh - m1 - xhat * m2)
        acc(ST_DGATE, dz * y)
        dz_ref[...] = dz
        dy_ref[...] = (dz * gate1).astype(BF16)

    ins, specs = [], []
    if has_later:
        ins += list(later)
        specs += [_rows(d), _rows(d), _rows(d), _vec(d)]
    else:
        ins += [dxo]
        specs += [_rows(d)]
    if has_ln:
        ins += list(this)
        specs += [_rows(d), _rows(d), _vec(d), _vec(d)]
        out_specs = [_rows(d), _rows(d), _vec(d, SUBLANES)]
        out_shape = [
            jax.ShapeDtypeStruct((s, d), F32),
            jax.ShapeDtypeStruct((s, d), BF16),
            jax.ShapeDtypeStruct((SUBLANES, d), F32),
        ]
    else:
        out_specs = [_rows(d), _vec(d, SUBLANES)]
        out_shape = [jax.ShapeDtypeStruct((s, d), F32), jax.ShapeDtypeStruct((SUBLANES, d), F32)]
    return pl.pallas_call(
        body,
        name=name,
        grid=(s // ROW_TILE,),
        in_specs=specs,
        out_specs=out_specs,
        out_shape=out_shape,
        compiler_params=_params(("arbitrary",)),
    )(*ins)


GATE_CHUNKS = 4


def _gelu(x):
    return 0.5 * x * (1.0 + jnp.tanh(GELU_C * (x + GELU_A * x * x * x)))


def _gelu_grad(x):
    t = jnp.tanh(GELU_C * (x + GELU_A * x * x * x))
    return 0.5 * (1.0 + t) + 0.5 * x * (1.0 - t * t) * (GELU_C * (1.0 + 3.0 * GELU_A * x * x))


def _causal_weights(w_ref, transpose):
    t = lax.broadcasted_iota(jnp.int32, (CHUNK, CHUNK), 0)
    s = lax.broadcasted_iota(jnp.int32, (CHUNK, CHUNK), 1)
    out = []
    for g in range(A_GROUPS):
        w = jnp.where(t >= s, w_ref[g], 0.0)
        out.append((w.T if transpose else w).astype(BF16))
    return out


def _spatial(ws, vn, lo_mask):
    rows = vn.shape[0]
    out_rows = []
    for r in range(rows // CHUNK):
        cols = []
        for j in range(A_GROUPS // 2):
            blk = vn[r * CHUNK : (r + 1) * CHUNK, j * LANES : (j + 1) * LANES]
            za = jnp.dot(ws[2 * j], blk, preferred_element_type=F32)
            zb = jnp.dot(ws[2 * j + 1], blk, preferred_element_type=F32)
            cols.append(jnp.where(lo_mask, za, zb))
        out_rows.append(jnp.concatenate(cols, axis=1))
    return jnp.concatenate(out_rows, axis=0)


def _gate_forward(a, vg, vb, ws, bias, lo_mask):
    u = _gelu(a[:, :D_MODEL])
    v = _gelu(a[:, D_MODEL:])
    vhat, rstd = _layer_norm_hat(v)
    vn = (vhat * vg + vb).astype(BF16)
    z = _spatial(ws, vn, lo_mask) + jnp.concatenate([bias] * (a.shape[0] // CHUNK), axis=0)
    return u, vhat, rstd, vn, z


def gate_fwd(a_pre, vn_g, vn_b, w_s, bias_full, *, name):
    s = a_pre.shape[0]
    tr = GATE_CHUNKS * CHUNK

    def body(a_ref, vg_ref, vb_ref, w_ref, bias_ref, p_ref):
        lo_mask = _lane((CHUNK, LANES)) < A_GROUP_DIM
        ws = _causal_weights(w_ref, transpose=False)
        u, _, _, _, z = _gate_forward(a_ref[...], vg_ref[...], vb_ref[...], ws, bias_ref[...], lo_mask)
        p_ref[...] = (u * z).astype(BF16)

    return pl.pallas_call(
        body,
        name=name,
        grid=(s // tr,),
        in_specs=[
            pl.BlockSpec((tr, 2 * D_MODEL), lambda i: (i, 0)),
            _vec(D_MODEL),
            _vec(D_MODEL),
            pl.BlockSpec((A_GROUPS, CHUNK, CHUNK), lambda i: (0, 0, 0)),
            _vec(D_MODEL, CHUNK),
        ],
        out_specs=pl.BlockSpec((tr, D_MODEL), lambda i: (i, 0)),
        out_shape=jax.ShapeDtypeStruct((s, D_MODEL), BF16),
        compiler_params=_params(("parallel",)),
    )(a_pre, vn_g, vn_b, w_s, bias_full)


def gate_bwd(a_pre, dp, vn_g, vn_b, w_s, bias_full, *, name):
    s = a_pre.shape[0]
    tr = GATE_CHUNKS * CHUNK
    nsteps = s // tr

    def body(a_ref, dp_ref, vg_ref, vb_ref, w_ref, bias_ref, da_ref, dw_ref, dbs_ref, rows_ref, dbias_acc):
        step = pl.program_id(0)
        lo_mask = _lane((CHUNK, LANES)) < A_GROUP_DIM

        @pl.when(step == 0)
        def _():
            dw_ref[...] = jnp.zeros_like(dw_ref)
            rows_ref[...] = jnp.zeros_like(rows_ref)
            dbias_acc[...] = jnp.zeros_like(dbias_acc)

        a = a_ref[...]
        vg = vg_ref[...]
        ws = _causal_weights(w_ref, transpose=False)
        wts = _causal_weights(w_ref, transpose=True)
        u, vhat, rstd, vn, z = _gate_forward(a, vg, vb_ref[...], ws, bias_ref[...], lo_mask)
        dp = dp_ref[...]
        du = dp * z
        dzz = dp * u
        dzz_b = dzz.astype(BF16)
        dvn = _spatial(wts, dzz_b, lo_mask)
        dbias = None
        for r in range(GATE_CHUNKS):
            rs = slice(r * CHUNK, (r + 1) * CHUNK)
            dbias = dzz[rs] if dbias is None else dbias + dzz[rs]
            for j in range(A_GROUPS // 2):
                cs = slice(j * LANES, (j + 1) * LANES)
                dblk = dzz[rs, cs]
                vblk = vn[rs, cs]
                for half in range(2):
                    keep = lo_mask if half == 0 else jnp.logical_not(lo_mask)
                    dm = jnp.where(keep, dblk, 0.0).astype(BF16)
                    dw_ref[2 * j + half] += lax.dot_general(
                        dm, vblk, (((1,), (1,)), ((), ())), preferred_element_type=F32
                    )
        dbias_acc[...] += dbias
        rows_ref[1:2, :D_MODEL] += jnp.sum(dvn * vhat, axis=0, keepdims=True)
        rows_ref[1:2, D_MODEL:] += jnp.sum(dvn, axis=0, keepdims=True)
        dvh = dvn * vg
        m1 = jnp.mean(dvh, axis=-1, keepdims=True)
        m2 = jnp.mean(dvh * vhat, axis=-1, keepdims=True)
        dv = rstd * (dvh - m1 - vhat * m2)
        da_u = du * _gelu_grad(a[:, :D_MODEL])
        da_v = dv * _gelu_grad(a[:, D_MODEL:])
        da_ref[:, :D_MODEL] = da_u.astype(BF16)
        da_ref[:, D_MODEL:] = da_v.astype(BF16)
        rows_ref[0:1, :D_MODEL] += jnp.sum(da_u, axis=0, keepdims=True)
        rows_ref[0:1, D_MODEL:] += jnp.sum(da_v, axis=0, keepdims=True)

        @pl.when(step == nsteps - 1)
        def _():
            t = lax.broadcasted_iota(jnp.int32, (CHUNK, CHUNK), 0)
            sx = lax.broadcasted_iota(jnp.int32, (CHUNK, CHUNK), 1)
            for g in range(A_GROUPS):
                dw_ref[g] = jnp.where(t >= sx, dw_ref[g], 0.0)
            dbs_ref[...] = _reduce_groups(dbias_acc[...])

    return pl.pallas_call(
        body,
        name=name,
        grid=(nsteps,),
        in_specs=[
            pl.BlockSpec((tr, 2 * D_MODEL), lambda i: (i, 0)),
            pl.BlockSpec((tr, D_MODEL), lambda i: (i, 0)),
            _vec(D_MODEL),
            _vec(D_MODEL),
            pl.BlockSpec((A_GROUPS, CHUNK, CHUNK), lambda i: (0, 0, 0)),
            _vec(D_MODEL, CHUNK),
        ],
        out_specs=[
            pl.BlockSpec((tr, 2 * D_MODEL), lambda i: (i, 0)),
            pl.BlockSpec((A_GROUPS, CHUNK, CHUNK), lambda i: (0, 0, 0)),
            _vec(LANES, CHUNK),
            _vec(2 * D_MODEL, SUBLANES),
        ],
        out_shape=[
            jax.ShapeDtypeStruct((s, 2 * D_MODEL), BF16),
            jax.ShapeDtypeStruct((A_GROUPS, CHUNK, CHUNK), F32),
            jax.ShapeDtypeStruct((CHUNK, LANES), F32),
            jax.ShapeDtypeStruct((SUBLANES, 2 * D_MODEL), F32),
        ],
        scratch_shapes=[pltpu.VMEM((CHUNK, D_MODEL), F32)],
        compiler_params=_params(("arbitrary",)),
    )(a_pre, dp, vn_g, vn_b, w_s, bias_full)


def alibi_tables(dilation):
    qi = jnp.arange(SPAN)[:, None]
    ki = jnp.arange(2 * SPAN)[None, :]
    diff = SPAN + qi - ki
    valid = (diff >= 0) & (diff <= SPAN)
    heads = jnp.arange(1, B_HEADS + 1, dtype=F32)
    slopes = jnp.exp2(-8.0 * heads / B_HEADS)
    bias = -slopes[:, None, None] * (dilation * diff).astype(F32)
    return jnp.where(valid[None], bias, NEG)


def permute_rows(x, dilation, *, inverse, name, add=None):
    s, w = x.shape
    tile = SPAN * dilation
    nat = pl.BlockSpec((tile, w), lambda i: (i, 0))
    streams = pl.BlockSpec((dilation, SPAN, w), lambda i: (0, i, 0))
    x3 = x.reshape(dilation, s // dilation, w) if inverse else x

    def body(*refs):
        if not inverse:
            x_ref, o_ref = refs
            o_ref[...] = jnp.swapaxes(x_ref[...].reshape(SPAN, dilation, w), 0, 1)
            return
        val = jnp.swapaxes(refs[0][...], 0, 1).reshape(tile, w)
        if add is not None:
            val = val + refs[1][...]
        refs[-1][...] = val

    out = pl.pallas_call(
        body,
        name=name,
        grid=(s // tile,),
        in_specs=([streams] + ([nat] if add is not None else [])) if inverse else [nat],
        out_specs=nat if inverse else streams,
        out_shape=jax.ShapeDtypeStruct((s, w) if inverse else (dilation, s // dilation, w), x.dtype),
        compiler_params=_params(("parallel",)),
    )(*([x3] + ([add] if add is not None else [])))
    return out.reshape(s, w)


def _qkv_specs(nb, clamp):
    def spec(which, prev):
        def index(r, n):
            blk = jnp.maximum(n - 1, 0) if prev else n
            return (r * nb + clamp(blk), which)

        return pl.BlockSpec((SPAN, D_MODEL), index)

    return [spec(0, False), spec(1, True), spec(1, False), spec(2, True), spec(2, False)]


def attn_fwd(qkv_p, pat, *, name):
    _, dilation = B_PATTERNS[pat]
    nb = SEQ // dilation // SPAN
    bias = alibi_tables(dilation)

    def body(q_ref, kp_ref, kc_ref, vp_ref, vc_ref, bias_ref, o_ref, lse_ref):
        n = pl.program_id(1)
        lane = _lane((SPAN, LANES))
        lo_mask = lane < B_HEAD_DIM
        first_prev = jnp.logical_and(n == 0, _lane((SPAN, 2 * SPAN)) < SPAN)
        q = q_ref[...] * jnp.asarray(B_HEAD_DIM**-0.5, BF16)
        kk = jnp.concatenate([kp_ref[...], kc_ref[...]], axis=0)
        vv = jnp.concatenate([vp_ref[...], vc_ref[...]], axis=0)
        halves = (lo_mask.astype(BF16), jnp.logical_not(lo_mask).astype(BF16))
        stats = jnp.zeros((SPAN, LANES), F32)
        for j in range(B_HEADS // 2):
            cs = slice(j * LANES, (j + 1) * LANES)
            qp, kp, vp = q[:, cs], kk[:, cs], vv[:, cs]
            outs = []
            for half in range(2):
                h = 2 * j + half
                sc = lax.dot_general(qp * halves[half], kp, (((1,), (1,)), ((), ())), preferred_element_type=F32)
                sc = jnp.where(first_prev, NEG, sc + bias_ref[h])
                m = jnp.max(sc, axis=1, keepdims=True)
                p = jnp.exp(sc - m)
                l = jnp.sum(p, axis=1, keepdims=True)
                acc = jnp.dot(p.astype(BF16), vp, preferred_element_type=F32)
                outs.append(acc * (1.0 / l))
                stats = jnp.where(lane == h, m + jnp.log(l), stats)
            o_ref[:, cs] = jnp.where(lo_mask, outs[0], outs[1])
        lse_ref[...] = stats

    return pl.pallas_call(
        body,
        name=name,
        grid=(dilation, nb),
        in_specs=[
            *_qkv_specs(nb, lambda blk: blk),
            pl.BlockSpec((B_HEADS, SPAN, 2 * SPAN), lambda r, n: (0, 0, 0)),
        ],
        out_specs=[
            pl.BlockSpec((SPAN, D_MODEL), lambda r, n: (r * nb + n, 0)),
            pl.BlockSpec((SPAN, LANES), lambda r, n: (r * nb + n, 0)),
        ],
        out_shape=[jax.ShapeDtypeStruct((SEQ, D_MODEL), F32), jax.ShapeDtypeStruct((SEQ, LANES), F32)],
        compiler_params=_params(("parallel", "arbitrary")),
    )(qkv_p, qkv_p, qkv_p, qkv_p, qkv_p, bias)


def attn_combine(outs, lses, *, name):
    def body(o0, o1, o2, l0, l1, l2, ob_ref, of_ref, lse_ref):
        ls = [l0[...], l1[...], l2[...]]
        m = jnp.maximum(jnp.maximum(ls[0], ls[1]), ls[2])
        tot = jnp.log(jnp.exp(ls[0] - m) + jnp.exp(ls[1] - m) + jnp.exp(ls[2] - m)) + m
        o = None
        for o_ref, l in zip((o0, o1, o2), ls, strict=True):
            term = _expand_groups(jnp.exp(l - tot)) * o_ref[...]
            o = term if o is None else o + term
        ob_ref[...] = o.astype(BF16)
        of_ref[...] = o
        lse_ref[...] = tot

    return pl.pallas_call(
        body,
        name=name,
        grid=(SEQ // ROW_TILE,),
        in_specs=[_rows(D_MODEL)] * 3 + [_rows(LANES)] * 3,
        out_specs=[_rows(D_MODEL), _rows(D_MODEL), _rows(LANES)],
        out_shape=[
            jax.ShapeDtypeStruct((SEQ, D_MODEL), BF16),
            jax.ShapeDtypeStruct((SEQ, D_MODEL), F32),
            jax.ShapeDtypeStruct((SEQ, LANES), F32),
        ],
        compiler_params=_params(("parallel",)),
    )(*outs, *lses)


def attn_delta(do, o, *, name):
    def body(do_ref, o_ref, dob_ref, dl_ref):
        do_v = do_ref[...]
        dob_ref[...] = do_v.astype(BF16)
        dl_ref[...] = _reduce_groups(do_v * o_ref[...])

    return pl.pallas_call(
        body,
        name=name,
        grid=(SEQ // ROW_TILE,),
        in_specs=[_rows(D_MODEL), _rows(D_MODEL)],
        out_specs=[_rows(D_MODEL), _rows(LANES)],
        out_shape=[jax.ShapeDtypeStruct((SEQ, D_MODEL), BF16), jax.ShapeDtypeStruct((SEQ, LANES), F32)],
        compiler_params=_params(("parallel",)),
    )(do, o)


def attn_bwd(qkv_p, do_p, lse_p, delta_p, pat, *, name):
    _, dilation = B_PATTERNS[pat]
    nb = SEQ // dilation // SPAN
    bias = alibi_tables(dilation)
    last = nb - 1
    q_cols, k_cols, v_cols = (slice(i * D_MODEL, (i + 1) * D_MODEL) for i in range(3))

    def body(q_ref, kp_ref, kc_ref, vp_ref, vc_ref, do_ref, lse_ref, dl_ref, bias_ref, out_ref, cq_ref, ck_ref, cv_ref):
        n = pl.program_id(1)

        @pl.when(n == nb)
        def _():
            out_ref[:, q_cols] = cq_ref[...].astype(BF16)
            out_ref[:, k_cols] = ck_ref[...].astype(BF16)
            out_ref[:, v_cols] = cv_ref[...].astype(BF16)

        @pl.when(n == 0)
        def _():
            cq_ref[...] = jnp.zeros_like(cq_ref)
            ck_ref[...] = jnp.zeros_like(ck_ref)
            cv_ref[...] = jnp.zeros_like(cv_ref)

        @pl.when(n < nb)
        def _():
            lane = _lane((SPAN, LANES))
            lo_mask = lane < B_HEAD_DIM
            first_prev = jnp.logical_and(n == 0, _lane((SPAN, 2 * SPAN)) < SPAN)
            q = q_ref[...] * jnp.asarray(B_HEAD_DIM**-0.5, BF16)
            kk = jnp.concatenate([kp_ref[...], kc_ref[...]], axis=0)
            vv = jnp.concatenate([vp_ref[...], vc_ref[...]], axis=0)
            do_v = do_ref[...]
            lse_v = lse_ref[...]
            dl_v = dl_ref[...]
            halves = (lo_mask.astype(BF16), jnp.logical_not(lo_mask).astype(BF16))
            for j in range(B_HEADS // 2):
                cs = slice(j * LANES, (j + 1) * LANES)
                qp, kp, vp, dop = q[:, cs], kk[:, cs], vv[:, cs], do_v[:, cs]
                dqs = []
                dk_pair = None
                dv_pair = None
                for half in range(2):
                    h = 2 * j + half
                    qm = qp * halves[half]
                    dom = dop * halves[half]
                    lse_h = jnp.sum(jnp.where(lane == h, lse_v, 0.0), axis=1, keepdims=True)
                    dl_h = jnp.sum(jnp.where(lane == h, dl_v, 0.0), axis=1, keepdims=True)
                    sc = lax.dot_general(qm, kp, (((1,), (1,)), ((), ())), preferred_element_type=F32)
                    sc = jnp.where(first_prev, NEG, sc + bias_ref[h])
                    p = jnp.exp(sc - lse_h)
                    dp = lax.dot_general(dom, vp, (((1,), (1,)), ((), ())), preferred_element_type=F32)
                    ds = (p * (dp - dl_h)).astype(BF16)
                    dqs.append(jnp.dot(ds, kp, preferred_element_type=F32))
                    dk_h = lax.dot_general(ds, qm, (((0,), (0,)), ((), ())), preferred_element_type=F32)
                    dv_h = lax.dot_general(
                        p.astype(BF16), dom, (((0,), (0,)), ((), ())), preferred_element_type=F32
                    )
                    dk_pair = dk_h if dk_pair is None else dk_pair + dk_h
                    dv_pair = dv_h if dv_pair is None else dv_pair + dv_h
                oq = slice(j * LANES, (j + 1) * LANES)
                ok = slice(D_MODEL + j * LANES, D_MODEL + (j + 1) * LANES)
                ov = slice(2 * D_MODEL + j * LANES, 2 * D_MODEL + (j + 1) * LANES)
                out_ref[:, oq] = cq_ref[:, cs].astype(BF16)
                out_ref[:, ok] = (ck_ref[:, cs] + dk_pair[:SPAN]).astype(BF16)
                out_ref[:, ov] = (cv_ref[:, cs] + dv_pair[:SPAN]).astype(BF16)
                cq_ref[:, cs] = jnp.where(lo_mask, dqs[0], dqs[1]) * (B_HEAD_DIM**-0.5)
                ck_ref[:, cs] = dk_pair[SPAN:]
                cv_ref[:, cs] = dv_pair[SPAN:]

    def clamp(n):
        return jnp.minimum(n, last)

    def row_spec(width):
        return pl.BlockSpec((SPAN, width), lambda r, n: (r * nb + clamp(n), 0))

    return pl.pallas_call(
        body,
        name=name,
        grid=(dilation, nb + 1),
        in_specs=[
            *_qkv_specs(nb, clamp),
            row_spec(D_MODEL),
            row_spec(LANES),
            row_spec(LANES),
            pl.BlockSpec((B_HEADS, SPAN, 2 * SPAN), lambda r, n: (0, 0, 0)),
        ],
        out_specs=pl.BlockSpec((SPAN, 3 * D_MODEL), lambda r, n: (r * nb + jnp.maximum(n - 1, 0), 0)),
        out_shape=jax.ShapeDtypeStruct((SEQ, 3 * D_MODEL), BF16),
        scratch_shapes=[pltpu.VMEM((SPAN, D_MODEL), F32)] * 3,
        compiler_params=_params(("arbitrary", "arbitrary")),
    )(qkv_p, qkv_p, qkv_p, qkv_p, qkv_p, do_p, lse_p, delta_p, bias)


def _position():
    x, y, c = lax.axis_index("x"), lax.axis_index("y"), lax.axis_index("c")
    return x, y, c, 4 * x + 2 * y + c


def _peer(k, x, y, c):
    px = 1 - x if k & 4 else x
    py = 1 - y if k & 2 else y
    pc = 1 - c if k & 1 else c
    return (px, py, pc), 4 * px + 2 * py + pc


def _remote(src, dst, send_sem, recv_sem, device):
    return pltpu.make_async_remote_copy(
        src_ref=src, dst_ref=dst, send_sem=send_sem, recv_sem=recv_sem, device_id=device, device_id_type=MESH
    )


def _silu_bf16(cf):
    return (cf * (1.0 / (1.0 + jnp.exp(-cf)))).astype(BF16)


def ada_exchange(c8, w4, b4, ln8):
    nt, _, ncol = w4.shape

    def body(c8_ref, w_ref, b_ref, ln_ref, cg_ref, lng_ref, mrecv_ref, mloc_ref, send_sems, recv_sems):
        x, y, c, me = _position()
        cg_ref[me] = c8_ref[...]
        lng_ref[me] = ln_ref[...]
        first = []
        for k in range(1, N_DEV):
            dev, _ = _peer(k, x, y, c)
            first.append(_remote(c8_ref, cg_ref.at[me], send_sems.at[0, k], recv_sems.at[0, k], dev))
            first.append(_remote(ln_ref, lng_ref.at[me], send_sems.at[1, k], recv_sems.at[1, k], dev))
        for cp in first:
            cp.start()
        for k in range(1, N_DEV):
            dev, pid = _peer(k, x, y, c)
            _remote(c8_ref, cg_ref.at[pid], send_sems.at[0, k], recv_sems.at[0, k], dev).wait_recv()
            _remote(ln_ref, lng_ref.at[pid], send_sems.at[1, k], recv_sems.at[1, k], dev).wait_recv()
        sc = _silu_bf16(cg_ref[...].reshape(N_DEV * SUBLANES, D_MODEL))
        for t in range(nt):
            mloc_ref[t] = jnp.dot(sc, w_ref[t].astype(BF16), preferred_element_type=F32) + b_ref[t : t + 1, :]

        def group(dev_id):
            return pl.ds(pl.multiple_of(dev_id * SUBLANES, SUBLANES), SUBLANES)

        mrecv_ref[me] = mloc_ref[:, group(me), :]
        second = []
        for k in range(1, N_DEV):
            dev, pid = _peer(k, x, y, c)
            second.append(
                _remote(mloc_ref.at[:, group(pid), :], mrecv_ref.at[me], send_sems.at[2, k], recv_sems.at[2, k], dev)
            )
        for cp in second:
            cp.start()
        for k in range(1, N_DEV):
            dev, pid = _peer(k, x, y, c)
            _remote(
                mloc_ref.at[:, group(pid), :], mrecv_ref.at[pid], send_sems.at[2, k], recv_sems.at[2, k], dev
            ).wait_recv()
        for cp in first + second:
            cp.wait_send()

    return pl.pallas_call(
        body,
        name="ada_exchange",
        in_specs=[VMEM, VMEM, VMEM, VMEM],
        out_specs=[VMEM, VMEM, VMEM],
        out_shape=[
            jax.ShapeDtypeStruct((N_DEV, SUBLANES, D_MODEL), F32),
            jax.ShapeDtypeStruct((N_DEV, SUBLANES, LANES), F32),
            jax.ShapeDtypeStruct((N_DEV, nt, SUBLANES, ncol), F32),
        ],
        scratch_shapes=[
            pltpu.VMEM((nt, N_DEV * SUBLANES, ncol), F32),
            pltpu.SemaphoreType.DMA((3, N_DEV)),
            pltpu.SemaphoreType.DMA((3, N_DEV)),
        ],
        compiler_params=pltpu.CompilerParams(vmem_limit_bytes=VMEM_LIMIT_BYTES),
    )(c8, w4, b4, ln8)


def small_exchange(dmx, flat):
    def body(dmx_ref, flat_ref, dmrecv_ref, red_ref, land_ref, send_sems, recv_sems):
        x, y, c, me = _position()
        dmrecv_ref[me] = dmx_ref[me]
        land_ref[me] = flat_ref[me]
        first = []
        for k in range(1, N_DEV):
            dev, pid = _peer(k, x, y, c)
            first.append(_remote(dmx_ref.at[pid], dmrecv_ref.at[me], send_sems.at[0, k], recv_sems.at[0, k], dev))
            first.append(_remote(flat_ref.at[pid], land_ref.at[me], send_sems.at[1, k], recv_sems.at[1, k], dev))
        for cp in first:
            cp.start()
        for k in range(1, N_DEV):
            dev, pid = _peer(k, x, y, c)
            _remote(dmx_ref.at[pid], dmrecv_ref.at[pid], send_sems.at[0, k], recv_sems.at[0, k], dev).wait_recv()
            _remote(flat_ref.at[pid], land_ref.at[pid], send_sems.at[1, k], recv_sems.at[1, k], dev).wait_recv()
        total = land_ref[0]
        for s in range(1, N_DEV):
            total = total + land_ref[s]
        red_ref[me] = total
        second = []
        for k in range(1, N_DEV):
            dev, _ = _peer(k, x, y, c)
            second.append(_remote(red_ref.at[me], red_ref.at[me], send_sems.at[2, k], recv_sems.at[2, k], dev))
        for cp in second:
            cp.start()
        for k in range(1, N_DEV):
            dev, pid = _peer(k, x, y, c)
            _remote(red_ref.at[pid], red_ref.at[pid], send_sems.at[2, k], recv_sems.at[2, k], dev).wait_recv()
        for cp in first + second:
            cp.wait_send()

    return pl.pallas_call(
        body,
        name="small_exchange",
        in_specs=[VMEM, VMEM],
        out_specs=[VMEM, VMEM],
        out_shape=[jax.ShapeDtypeStruct(dmx.shape, F32), jax.ShapeDtypeStruct(flat.shape, F32)],
        scratch_shapes=[
            pltpu.VMEM(flat.shape, F32),
            pltpu.SemaphoreType.DMA((3, N_DEV)),
            pltpu.SemaphoreType.DMA((3, N_DEV)),
        ],
        compiler_params=pltpu.CompilerParams(vmem_limit_bytes=VMEM_LIMIT_BYTES),
    )(dmx, flat)


HBM = pl.BlockSpec(memory_space=pltpu.HBM)
SEM = pl.BlockSpec(memory_space=pltpu.SEMAPHORE)
EFFECT = pltpu.SideEffectType.DATAFLOW_SIDE_EFFECTING


def _own_slot(me, block):
    land = lax.empty((N_DEV, *block.shape), block.dtype)
    return lax.dynamic_update_slice_in_dim(land, block[None], me, axis=0)


class Exchange:
    def __init__(self, srcs, lands, after, *, scatter, name):
        self.scatter = scatter
        self.name = name
        nt = self.nt = len(srcs)
        peers = N_DEV - 1

        def body(*refs):
            src_refs, land_refs = refs[:nt], refs[nt : 2 * nt]
            send_sems, recv_sems = refs[2 * nt + 1 : 3 * nt + 1], refs[3 * nt + 1 : 4 * nt + 1]
            token = refs[-1]
            x, y, c, me = _position()
            for t in range(nt):
                for k in range(1, N_DEV):
                    dev, pid = _peer(k, x, y, c)
                    src = src_refs[t].at[pid] if scatter else src_refs[t]
                    _remote(src, land_refs[t].at[me], send_sems[t].at[k - 1], recv_sems[t].at[k - 1], dev).start()
            token[...] = jnp.zeros_like(token)

        outs = pl.pallas_call(
            body,
            name=name + "_start",
            in_specs=[HBM] * (2 * nt) + [ANY],
            out_specs=[SEM] * (2 * nt) + [HBM] * (2 * nt) + [VMEM],
            out_shape=[pltpu.SemaphoreType.DMA((peers,))] * (2 * nt)
            + [pltpu.HBM(a.shape, a.dtype) for a in (*srcs, *lands)]
            + [jax.ShapeDtypeStruct((SUBLANES, LANES), F32)],
            input_output_aliases={i: 2 * nt + i for i in range(2 * nt)},
            compiler_params=pltpu.CompilerParams(has_side_effects=EFFECT),
        )(*[pltpu.with_memory_space_constraint(a, pltpu.HBM) for a in (*srcs, *lands)], after)
        self.send_sems, self.recv_sems = outs[:nt], outs[nt : 2 * nt]
        self.srcs, self.lands = outs[2 * nt : 3 * nt], outs[3 * nt : 4 * nt]
        self.token = outs[-1]

    def zero(self):
        return self.token[0, 0]

    def wait(self, which, after, *, name):
        scatter = self.scatter
        n = len(which)

        def body(*refs):
            src_refs, land_refs = refs[:n], refs[n : 2 * n]
            send_sems, recv_sems = refs[2 * n : 3 * n], refs[3 * n : 4 * n]
            x, y, c, _ = _position()
            for t in range(n):
                for k in range(1, N_DEV):
                    dev, pid = _peer(k, x, y, c)
                    src = src_refs[t].at[pid] if scatter else src_refs[t]
                    cp = _remote(src, land_refs[t].at[pid], send_sems[t].at[k - 1], recv_sems[t].at[k - 1], dev)
                    cp.wait_send()
                    cp.wait_recv()

        srcs = [self.srcs[t] for t in which]
        lands = [self.lands[t] for t in which]
        outs = pl.pallas_call(
            body,
            name=name,
            in_specs=[HBM] * (2 * n) + [SEM] * (2 * n) + [ANY],
            out_specs=[HBM] * (2 * n),
            out_shape=[pltpu.HBM(a.shape, a.dtype) for a in (*srcs, *lands)],
            input_output_aliases={i: i for i in range(2 * n)},
            compiler_params=pltpu.CompilerParams(has_side_effects=EFFECT),
        )(*srcs, *lands, *[self.send_sems[t] for t in which], *[self.recv_sems[t] for t in which], after)
        return outs[n:]


def _adam_update(g, w, m, v):
    m2 = ADAM_B1 * m + (1.0 - ADAM_B1) * g
    v2 = ADAM_B2 * v + (1.0 - ADAM_B2) * jnp.square(g)
    m_hat = m2 / (1.0 - ADAM_B1**ADAM_STEP)
    v_hat = v2 / (1.0 - ADAM_B2**ADAM_STEP)
    delta = -ADAM_LR * (m_hat / (jnp.sqrt(v_hat) + ADAM_EPS) + ADAM_WD * w)
    return delta, m2, v2


def adamw(gparts, w, m, v, *, name):
    p, r, c = gparts.shape
    tr = r if r <= 256 else (256 if c <= D_MODEL else 128)

    def body(g_ref, w_ref, m_ref, v_ref, go_ref, d_ref, mo_ref, vo_ref):
        g = g_ref[0].astype(F32)
        for i in range(1, p):
            g = g + g_ref[i].astype(F32)
        delta, m2, v2 = _adam_update(g, w_ref[...], m_ref[...], v_ref[...])
        go_ref[...] = g
        d_ref[...] = delta
        mo_ref[...] = m2
        vo_ref[...] = v2

    blk = pl.BlockSpec((tr, c), lambda i: (i, 0))
    return pl.pallas_call(
        body,
        name=name,
        grid=(r // tr,),
        in_specs=[pl.BlockSpec((p, tr, c), lambda i: (0, i, 0)), blk, blk, blk],
        out_specs=[blk] * 4,
        out_shape=[jax.ShapeDtypeStruct((r, c), F32)] * 4,
        compiler_params=_params(("parallel",)),
    )(gparts, w, m, v)


def ada_grad_adamw(cg, dmrecv, w4, m4, v4, *, name):
    nt, k, ncol = w4.shape

    def body(cg_ref, dm_ref, w_ref, m_ref, v_ref, go_ref, d_ref, mo_ref, vo_ref, gb_ref):
        sc = _silu_bf16(cg_ref[...].reshape(N_DEV * SUBLANES, k))
        dm = dm_ref[...].reshape(N_DEV * SUBLANES, ncol)
        g = lax.dot_general(sc, dm.astype(BF16), (((0,), (0,)), ((), ())), preferred_element_type=F32)
        delta, m2, v2 = _adam_update(g, w_ref[...], m_ref[...], v_ref[...])
        go_ref[...] = g
        d_ref[...] = delta
        mo_ref[...] = m2
        vo_ref[...] = v2
        gb_ref[...] = jnp.broadcast_to(jnp.sum(dm, axis=0, keepdims=True), (SUBLANES, ncol))

    wblk = pl.BlockSpec((None, k, ncol), lambda t: (t, 0, 0))
    return pl.pallas_call(
        body,
        name=name,
        grid=(nt,),
        in_specs=[
            pl.BlockSpec((N_DEV, SUBLANES, k), lambda t: (0, 0, 0)),
            pl.BlockSpec((N_DEV, None, SUBLANES, ncol), lambda t: (0, t, 0, 0)),
            wblk,
            wblk,
            wblk,
        ],
        out_specs=[wblk] * 4 + [pl.BlockSpec((None, SUBLANES, ncol), lambda t: (t, 0, 0))],
        out_shape=[jax.ShapeDtypeStruct((nt, k, ncol), F32)] * 4 + [jax.ShapeDtypeStruct((nt, SUBLANES, ncol), F32)],
        compiler_params=_params(("parallel",)),
    )(cg, dmrecv, w4, m4, v4)


def kernel(x, c, ada_w, ada_b, ln_g, ln_b, a_w_in, a_b_in, a_vn_g, a_vn_b, a_w_s, a_b_s, a_w_out, b_w_qkv, b_w_out, mlp_w_up, mlp_w_down, loss_target, m_ada_w, m_ada_b, m_ln_g, m_ln_b, m_a_w_in, m_a_b_in, m_a_vn_g, m_a_vn_b, m_a_w_s, m_a_b_s, m_a_w_out, m_b_w_qkv, m_b_w_out, m_mlp_w_up, m_mlp_w_down, v_ada_w, v_ada_b, v_ln_g, v_ln_b, v_a_w_in, v_a_b_in, v_a_vn_g, v_a_vn_b, v_a_w_s, v_a_b_s, v_a_w_out, v_b_w_qkv, v_b_w_out, v_mlp_w_up, v_mlp_w_down):
    x0 = x[0]
    target = loss_target[0]
    me = 4 * lax.axis_index("x") + 2 * lax.axis_index("y") + lax.axis_index("c")

    ada_w4 = ada_w.reshape(N_SUB, D_MODEL, -1)
    ada_b4 = ada_b.reshape(N_SUB, -1)
    ln8 = jnp.concatenate([ln_g.reshape(N_SUB, -1), ln_b.reshape(N_SUB, -1)], axis=0)
    c8 = jnp.broadcast_to(c, (SUBLANES, D_MODEL))
    cg, lng, mrecv = ada_exchange(c8, ada_w4, ada_b4, ln8)

    W_IN, W_AOUT, W_UP0, W_DN0, W_QKV, W_BOUT, W_UP1, W_DN1 = range(8)
    shards = [
        a_w_in[0].astype(BF16),
        a_w_out[0].astype(BF16),
        mlp_w_up[0].astype(BF16),
        mlp_w_down[0].astype(BF16),
        b_w_qkv[0].astype(BF16),
        b_w_out[0].astype(BF16),
        mlp_w_up[1].astype(BF16),
        mlp_w_down[1].astype(BF16),
    ]
    gather = Exchange(shards, [_own_slot(me, s) for s in shards], mrecv, scatter=False, name="gather")

    modv = mrecv[:, :, 0, :].transpose(1, 0, 2).reshape(N_SUB, 3 * D_MODEL) + gather.zero()
    shift = [modv[t : t + 1, :D_MODEL] for t in range(N_SUB)]
    scale = [modv[t : t + 1, D_MODEL : 2 * D_MODEL] for t in range(N_SUB)]
    gate1 = [1.0 + modv[t : t + 1, 2 * D_MODEL :] for t in range(N_SUB)]
    lng_full = [lng[:, t, :].reshape(1, D_MODEL) for t in range(N_SUB)]
    lnb_full = [lng[:, N_SUB + t, :].reshape(1, D_MODEL) for t in range(N_SUB)]

    ident = lambda acc: (acc,)
    relu2 = lambda acc: (acc, jnp.square(jnp.maximum(acc, 0.0)))
    vn_g, vn_b, w_s = a_vn_g, a_vn_b, a_w_s[0]
    bias_full = jnp.repeat(a_b_s[0].T, A_GROUP_DIM, axis=1)
    w_up3, w_dn3 = [None, None], [None, None]

    def mlp_forward(i, h, after):
        up, dn = gather.wait([W_UP0, W_DN0] if i == 0 else [W_UP1, W_DN1], after, name=f"gather_wait_mlp{i}")
        w_up3[i], w_dn3[i] = up, dn.reshape(1, D_FF, D_MODEL)
        a, r = mm_nn(h, w_up3[i], name=f"mlp{i}_up", tm=1024, tn=512, epilogue=relu2, outs=(BF16, BF16))
        (y,) = mm_nn(r, w_dn3[i], name=f"mlp{i}_down", tm=512, tn=512, epilogue=ident, outs=(F32,))
        return a, r, y

    h0 = modulate(x0, scale[0], shift[0], name="modulate0")
    w_in3, w_aout3 = gather.wait([W_IN, W_AOUT], h0, name="gather_wait_a")
    w_aout3 = w_aout3.reshape(1, D_MODEL, D_MODEL)
    (a_pre,) = mm_nn(
        h0, w_in3, name="a_in", tm=1024, tn=256, epilogue=lambda acc, b: (acc + b,), extras=[(a_b_in, "row")], outs=(F32,)
    )
    p_gate = gate_fwd(a_pre, vn_g, vn_b, w_s, bias_full, name="gate_fwd")
    (y0,) = mm_nn(p_gate, w_aout3, name="a_out", tm=1024, tn=512, epilogue=ident, outs=(F32,))
    x1, h1 = residual_ln(x0, y0, gate1[0], lng_full[0], lnb_full[0], scale[1], shift[1], name="res_ln0")
    a1, r1, y1 = mlp_forward(0, h1, y0)
    x2, h2 = residual_ln(x1, y1, gate1[1], lng_full[1], lnb_full[1], scale[2], shift[2], name="res_ln1")
    w_qkv_shards, w_bout3 = gather.wait([W_QKV, W_BOUT], y1, name="gather_wait_b")
    w_bout3 = w_bout3.reshape(1, D_MODEL, D_MODEL)
    w_qkv_cols = w_qkv_shards.transpose(1, 0, 2).reshape(D_MODEL, D_QKV)
    w_pat = [w_qkv_cols[None, :, 3 * D_MODEL * g : 3 * D_MODEL * (g + 1)] for g in range(N_PAT)]
    dil = [d for _, d in B_PATTERNS]
    h2_p = [h2] + [permute_rows(h2, dil[g], inverse=False, name=f"perm_h{g}") for g in range(1, N_PAT)]
    qkv_p, pat_o, pat_lse = [], [], []
    for g in range(N_PAT):
        (qkv_g,) = mm_nn(h2_p[g], w_pat[g], name=f"b_qkv{g}", tm=1024, tn=1024, epilogue=ident, outs=(BF16,))
        o_g, lse_g = attn_fwd(qkv_g, g, name=f"attn_fwd{g}")
        if g > 0:
            o_g = permute_rows(o_g, dil[g], inverse=True, name=f"unperm_o{g}")
            lse_g = permute_rows(lse_g, dil[g], inverse=True, name=f"unperm_lse{g}")
        qkv_p.append(qkv_g)
        pat_o.append(o_g)
        pat_lse.append(lse_g)
    o_b, o_f, lse = attn_combine(pat_o, pat_lse, name="attn_combine")
    (y2,) = mm_nn(o_b, w_bout3, name="b_out", tm=1024, tn=512, epilogue=ident, outs=(F32,))
    x3, h3 = residual_ln(x2, y2, gate1[2], lng_full[2], lnb_full[2], scale[3], shift[3], name="res_ln2")
    a3, r3, y3 = mlp_forward(1, h3, y2)
    dxo, loss_local = residual_ln_loss(x3, y3, gate1[3], lng_full[3], lnb_full[3], target, name="res_ln3_loss")
    loss = lax.psum(loss_local[0, 0], ("x", "y", "c"))

    def scatter(parts, after, name):
        parts = [p.reshape(N_DEV, -1, p.shape[-1]) for p in parts]
        lands = [_own_slot(me, lax.dynamic_index_in_dim(p, me, 0, keepdims=False)) for p in parts]
        return Exchange(parts, lands, after, scatter=True, name=name)

    def mlp_backward(i, h, a, r, dy):
        (da,) = mm_nt(
            dy,
            w_dn3[i],
            name=f"mlp{i}_da",
            tm=1024,
            tko=1024,
            ps=1,
            epilogue=lambda acc, act: (acc * (2.0 * jnp.maximum(act.astype(F32), 0.0)),),
            extras=[(a, "full")],
            outs=(BF16,),
        )
        (dh,) = mm_nt(da, w_up3[i], name=f"mlp{i}_dh", tm=1024, tko=1024, ps=2, epilogue=ident, outs=(F32,))
        dw_dn = mm_tn(r, dy, name=f"mlp{i}_dw_down", p=1, tk=1024, tn=512, tmc=1024, out_dtype=BF16)
        dw_up = mm_tn(h, da, name=f"mlp{i}_dw_up", p=N_DEV, tk=1024, tn=512, tmc=1024, out_dtype=BF16)
        return scatter([dw_up, dw_dn], dh, f"scatter_mlp{i}"), dh

    dz3, dy3, st3 = residual_ln_bwd(name="res_bwd3", dxo=dxo, this=(x3, y3, gate1[3], lng_full[3]))
    rs_mlp1, dh3 = mlp_backward(1, h3, a3, r3, dy3)
    dz2, dy2, st2 = residual_ln_bwd(
        name="res_bwd2", later=(dh3, dz3, x3, scale[3] + rs_mlp1.zero()), this=(x2, y2, gate1[2], lng_full[2])
    )
    (d_o,) = mm_nt(dy2, w_bout3, name="b_do", tm=1024, tko=1024, ps=1, epilogue=ident, outs=(F32,))
    do_b, delta = attn_delta(d_o, o_f, name="attn_delta")
    dh2, dw_pat = None, []
    for g in range(N_PAT):
        do_g, lse_g, delta_g = do_b, lse, delta
        if g > 0:
            do_g = permute_rows(do_b, dil[g], inverse=False, name=f"perm_do{g}")
            lse_g = permute_rows(lse, dil[g], inverse=False, name=f"perm_lse{g}")
            delta_g = permute_rows(delta, dil[g], inverse=False, name=f"perm_delta{g}")
        dqkv_g = attn_bwd(qkv_p[g], do_g, lse_g, delta_g, g, name=f"attn_bwd{g}")
        (dh_g,) = mm_nt(dqkv_g, w_pat[g], name=f"b_dh{g}", tm=1024, tko=1024, ps=1, epilogue=ident, outs=(F32,))
        dh2 = dh_g if g == 0 else permute_rows(dh_g, dil[g], inverse=True, add=dh2, name=f"unperm_dh{g}")
        dw_pat.append(mm_tn(h2_p[g], dqkv_g, name=f"b_dw_qkv{g}", p=1, tk=1024, tn=1024, tmc=1024, out_dtype=BF16)[0])
    dw_bout = mm_tn(o_b, dy2, name="b_dw_out", p=1, tk=1024, tn=512, tmc=1024, out_dtype=BF16)
    dw_qkv = jnp.concatenate(dw_pat, axis=1).reshape(D_MODEL, N_DEV, -1).transpose(1, 0, 2)
    rs_b = scatter([dw_qkv, dw_bout], dh2, "scatter_b")
    dz1, dy1, st1 = residual_ln_bwd(
        name="res_bwd1", later=(dh2, dz2, x2, scale[2] + rs_b.zero()), this=(x1, y1, gate1[1], lng_full[1])
    )
    rs_mlp0, dh1 = mlp_backward(0, h1, a1, r1, dy1)
    dz0, dy0, st0 = residual_ln_bwd(
        name="res_bwd0", later=(dh1, dz1, x1, scale[1] + rs_mlp0.zero()), this=(x0, y0, gate1[0], lng_full[0])
    )
    (dp_gate,) = mm_nt(dy0, w_aout3, name="a_dp", tm=1024, tko=1024, ps=1, epilogue=ident, outs=(F32,))
    da0, d_ws, d_bs, gate_rows = gate_bwd(a_pre, dp_gate, vn_g, vn_b, w_s, bias_full, name="gate_bwd")
    (dh0,) = mm_nt(da0, w_in3, name="a_dh", tm=1024, tko=1024, ps=4, epilogue=ident, outs=(F32,))
    dw_aout = mm_tn(p_gate, dy0, name="a_dw_out", p=1, tk=1024, tn=512, tmc=1024, out_dtype=BF16)
    dw_in = mm_tn(h0, da0, name="a_dw_in", p=N_DEV, tk=1024, tn=256, tmc=1024, out_dtype=BF16)
    grad_x, stf = residual_ln_bwd(name="res_bwd_in", later=(dh0, dz0, x0, scale[0]))

    stats_after = [stf, st0, st1, st2]
    stats_own = [st0, st1, st2, st3]
    dm = jnp.stack(
        [
            jnp.concatenate(
                [stats_after[t][ST_DSHIFT], stats_after[t][ST_DSCALE], stats_own[t][ST_DGATE]], axis=0
            )
            for t in range(N_SUB)
        ]
    )
    ncol = 3 * D_MODEL // N_DEV
    dmx = jnp.pad(
        dm.reshape(N_SUB, N_DEV, ncol).transpose(1, 0, 2)[:, :, None, :], ((0, 0), (0, 0), (0, SUBLANES - 1), (0, 0))
    )
    small = [
        gate_rows[0],
        gate_rows[1],
        d_ws.reshape(-1),
        d_bs[:, :A_GROUPS].T.reshape(-1),
        *[stats_own[t][ST_DG] for t in range(N_SUB)],
        *[stats_own[t][ST_DB] for t in range(N_SUB)],
    ]
    n_small = sum(s.size for s in small)
    part_rows = -(-n_small // (N_DEV * LANES * SUBLANES)) * SUBLANES
    flat = jnp.concatenate(small + [jnp.zeros((N_DEV * part_rows * LANES - n_small,), F32)])
    dmrecv, reduced = small_exchange(dmx, flat.reshape(N_DEV, part_rows, LANES))
    reduced = reduced.reshape(-1)
    sizes = [2 * D_MODEL, D_MODEL, D_MODEL, A_GROUPS * CHUNK * CHUNK, A_GROUPS * CHUNK, N_SUB * D_MODEL, N_SUB * D_MODEL]
    offs = [sum(sizes[:i]) for i in range(len(sizes) + 1)]
    g_b_in, g_vn_g, g_vn_b, g_ws, g_bs, g_lng, g_lnb = [reduced[offs[i] : offs[i + 1]] for i in range(len(sizes))]

    results = {}

    def update(wname, gparts, w, m, v):
        shape = w.shape
        w2 = w.reshape(-1, shape[-1])
        outs = adamw(gparts.reshape(gparts.shape[0], *w2.shape), w2, m.reshape(w2.shape), v.reshape(w2.shape), name=f"adamw_{wname}")
        results[wname] = [o.reshape(shape) for o in outs]

    rs_a = scatter([dw_in, dw_aout], reduced, "scatter_a")
    ada_outs = ada_grad_adamw(cg + rs_a.zero(), dmrecv, ada_w4, m_ada_w.reshape(ada_w4.shape), v_ada_w.reshape(ada_w4.shape), name="ada_grad_adamw")
    results["ada_w"] = [o.reshape(ada_w.shape) for o in ada_outs[:4]]
    update("ada_b", ada_outs[4][:, 0, :][None], ada_b, m_ada_b, v_ada_b)
    ln_cols = D_MODEL // N_DEV
    my_ln = lambda gfull: lax.dynamic_slice_in_dim(gfull.reshape(N_SUB, N_DEV, ln_cols), me, 1, axis=1)
    update("ln_g", my_ln(g_lng).reshape(1, N_SUB, ln_cols), ln_g, m_ln_g, v_ln_g)
    update("ln_b", my_ln(g_lnb).reshape(1, N_SUB, ln_cols), ln_b, m_ln_b, v_ln_b)
    update("a_b_in", g_b_in[None], a_b_in, m_a_b_in, v_a_b_in)
    update("a_vn_g", g_vn_g[None], a_vn_g, m_a_vn_g, v_a_vn_g)
    update("a_vn_b", g_vn_b[None], a_vn_b, m_a_vn_b, v_a_vn_b)
    update("a_w_s", g_ws[None], a_w_s, m_a_w_s, v_a_w_s)
    update("a_b_s", g_bs[None], a_b_s, m_a_b_s, v_a_b_s)
    g_up1, g_dn1 = rs_mlp1.wait([0, 1], grad_x, name="scatter_wait_mlp1")
    g_qkv, g_bout = rs_b.wait([0, 1], grad_x, name="scatter_wait_b")
    update("b_w_qkv", g_qkv, b_w_qkv, m_b_w_qkv, v_b_w_qkv)
    update("b_w_out", g_bout, b_w_out, m_b_w_out, v_b_w_out)
    g_up0, g_dn0 = rs_mlp0.wait([0, 1], grad_x, name="scatter_wait_mlp0")
    update("mlp_w_up", jnp.concatenate([g_up0, g_up1], axis=1), mlp_w_up, m_mlp_w_up, v_mlp_w_up)
    update("mlp_w_down", jnp.concatenate([g_dn0, g_dn1], axis=1), mlp_w_down, m_mlp_w_down, v_mlp_w_down)
    g_in, g_aout = rs_a.wait([0, 1], grad_x, name="scatter_wait_a")
    update("a_w_in", g_in, a_w_in, m_a_w_in, v_a_w_in)
    update("a_w_out", g_aout, a_w_out, m_a_w_out, v_a_w_out)

    order = ["ada_w", "ada_b", "ln_g", "ln_b", "a_w_in", "a_b_in", "a_vn_g", "a_vn_b", "a_w_s", "a_b_s", "a_w_out", "b_w_qkv", "b_w_out", "mlp_w_up", "mlp_w_down"]
    return (loss, grad_x[None], *[results[n][0] for n in order], *[results[n][1] for n in order],
            *[results[n][2] for n in order], *[results[n][3] for n in order])
```

```python
import math

import jax
import jax.numpy as jnp
from jax import lax
from jax.experimental import pallas as pl
from jax.experimental.pallas import tpu as pltpu

F32 = jnp.float32
BF16 = jnp.bfloat16
MESH = pl.DeviceIdType.MESH
ANY = pl.BlockSpec(memory_space=pl.ANY)
VMEM = pl.BlockSpec(memory_space=pltpu.VMEM)

N_DEV = 8
D_MODEL = 1024
SEQ = 4096
DEPTH = 2
CHUNK = 128
A_GROUPS = 16
A_GROUP_DIM = D_MODEL // A_GROUPS
B_HEADS = 16
B_HEAD_DIM = 64
B_PATTERNS = ((128, 1), (512, 4), (2048, 16))
N_PAT = len(B_PATTERNS)
SPAN = 128
D_FF = 4 * D_MODEL
D_QKV = N_PAT * 3 * D_MODEL
ALPHA = (2 * DEPTH) ** 0.25
LN_EPS = 1e-5
NEG = -1e30
ADAM_LR = 0.001
ADAM_B1 = 0.9
ADAM_B2 = 0.999
ADAM_EPS = 1e-08
ADAM_WD = 0.01
ADAM_STEP = 10
GELU_C = math.sqrt(2.0 / math.pi)
GELU_A = 0.044715

VMEM_LIMIT_BYTES = 56 * 1024 * 1024
LANES = 128
SUBLANES = 8
ROW_TILE = 512
N_SUB = 2 * DEPTH


def _params(sem):
    return pltpu.CompilerParams(dimension_semantics=sem, vmem_limit_bytes=VMEM_LIMIT_BYTES)


def _lane(shape):
    return lax.broadcasted_iota(jnp.int32, shape, len(shape) - 1)


def _split_bf16(x):
    hi = x.astype(BF16)
    lo = (x - hi.astype(F32)).astype(BF16)
    return hi, lo


def _group_expand_matrix(groups_padded, width):
    per = width // A_GROUPS
    r = lax.broadcasted_iota(jnp.int32, (groups_padded, width), 0)
    c = lax.broadcasted_iota(jnp.int32, (groups_padded, width), 1)
    return (c // per == r).astype(BF16)


def _group_reduce_matrix(width, groups_padded):
    per = width // A_GROUPS
    r = lax.broadcasted_iota(jnp.int32, (width, groups_padded), 0)
    c = lax.broadcasted_iota(jnp.int32, (width, groups_padded), 1)
    return (r // per == c).astype(BF16)


def _expand_groups(w):
    e = _group_expand_matrix(LANES, D_MODEL)
    hi, lo = _split_bf16(w)
    return jnp.dot(hi, e, preferred_element_type=F32) + jnp.dot(lo, e, preferred_element_type=F32)


def _reduce_groups(x):
    e = _group_reduce_matrix(D_MODEL, LANES)
    hi, lo = _split_bf16(x)
    return jnp.dot(hi, e, preferred_element_type=F32) + jnp.dot(lo, e, preferred_element_type=F32)


def _column_tiles(p, n, ps, tn):
    assert (ps == 1 or tn == n) and p % ps == 0 and n % tn == 0
    q = n // tn
    return (p // ps) * q, q


def _extra_specs(extras, tm, width):
    specs = []
    for _, kind in extras:
        if kind == "row":
            specs.append(pl.BlockSpec((1, width), lambda i, j, c: (0, j)))
        else:
            specs.append(pl.BlockSpec((tm, width), lambda i, j, c: (i, j)))
    return specs


def mm_nn(a, b3, *, name, tm, ps, tn, tk, epilogue, extras=(), outs, prologue=None):
    m, k = a.shape
    p, _, n = b3.shape
    nj, q = _column_tiles(p, n, ps, tn)
    nk = k // tk
    width = ps * tn

    def body(a_ref, b_ref, *rest):
        ex = rest[: len(extras)]
        out_refs = rest[len(extras) : len(extras) + len(outs)]
        kk = pl.program_id(2)
        av = a_ref[...] if prologue is None else prologue(a_ref[...])

        def finish(cs, acc):
            res = epilogue(acc, *[e[:, cs] for e in ex])
            for o_ref, r in zip(out_refs, res, strict=True):
                o_ref[:, cs] = r.astype(o_ref.dtype)

        for s in range(ps):
            cs = slice(s * tn, (s + 1) * tn)
            part = jnp.dot(av, b_ref[s], preferred_element_type=F32)
            if nk == 1:
                finish(cs, part)
                continue
            acc_ref = rest[-1]

            @pl.when(kk == 0)
            def _(part=part, cs=cs):
                acc_ref[:, cs] = part

            @pl.when(kk > 0)
            def _(part=part, cs=cs):
                acc_ref[:, cs] += part

        if nk > 1:

            @pl.when(kk == nk - 1)
            def _():
                for s in range(ps):
                    cs = slice(s * tn, (s + 1) * tn)
                    finish(cs, rest[-1][:, cs])

    return pl.pallas_call(
        body,
        name=name,
        grid=(m // tm, nj, nk),
        in_specs=[
            pl.BlockSpec((tm, tk), lambda i, j, kk: (i, kk)),
            pl.BlockSpec((ps, tk, tn), lambda i, j, kk: (j // q, kk, j % q)),
            *_extra_specs(extras, tm, width),
        ],
        out_specs=[pl.BlockSpec((tm, width), lambda i, j, kk: (i, j)) for _ in outs],
        out_shape=[jax.ShapeDtypeStruct((m, p * n), dt) for dt in outs],
        scratch_shapes=[pltpu.VMEM((tm, width), F32)] if nk > 1 else [],
        compiler_params=_params(("parallel", "parallel", "arbitrary")),
    )(a, b3, *[arr for arr, _ in extras])


def mm_nt(g, b3, *, name, tm, tko, ps, tc, epilogue, extras=(), outs):
    m = g.shape[0]
    p, k, n = b3.shape
    nc, q = _column_tiles(p, n, ps, tc)

    def body(g_ref, b_ref, *rest):
        ex = rest[: len(extras)]
        out_refs = rest[len(extras) : len(extras) + len(outs)]
        c = pl.program_id(2)
        part = None
        for s in range(ps):
            d = lax.dot_general(
                g_ref[:, s * tc : (s + 1) * tc], b_ref[s], (((1,), (1,)), ((), ())), preferred_element_type=F32
            )
            part = d if part is None else part + d

        def finish(acc):
            res = epilogue(acc, *[e[...] for e in ex])
            for o_ref, r in zip(out_refs, res, strict=True):
                o_ref[...] = r.astype(o_ref.dtype)

        if nc == 1:
            finish(part)
            return
        acc_ref = rest[-1]

        @pl.when(c == 0)
        def _():
            acc_ref[...] = part

        @pl.when(c > 0)
        def _():
            acc_ref[...] += part

        @pl.when(c == nc - 1)
        def _():
            finish(acc_ref[...])

    return pl.pallas_call(
        body,
        name=name,
        grid=(m // tm, k // tko, nc),
        in_specs=[
            pl.BlockSpec((tm, ps * tc), lambda i, j, c: (i, c)),
            pl.BlockSpec((ps, tko, tc), lambda i, j, c: (c // q, j, c % q)),
            *_extra_specs(extras, tm, tko),
        ],
        out_specs=[pl.BlockSpec((tm, tko), lambda i, j, c: (i, j)) for _ in outs],
        out_shape=[jax.ShapeDtypeStruct((m, k), dt) for dt in outs],
        scratch_shapes=[pltpu.VMEM((tm, tko), F32)] if nc > 1 else [],
        compiler_params=_params(("parallel", "parallel", "arbitrary")),
    )(g, b3, *[arr for arr, _ in extras])


def mm_tn(a, g, *, name, p, tk, ps, tn, tmc, out_dtype, prologue=None):
    m, k = a.shape
    n = g.shape[1] // p
    nj, q = _column_tiles(p, n, ps, tn)
    nc = m // tmc

    def body(a_ref, g_ref, o_ref, acc_ref):
        c = pl.program_id(2)
        av = a_ref[...] if prologue is None else prologue(a_ref[...])
        for s in range(ps):
            part = lax.dot_general(
                av, g_ref[:, s * tn : (s + 1) * tn], (((0,), (0,)), ((), ())), preferred_element_type=F32
            )

            @pl.when(c == 0)
            def _(part=part, s=s):
                acc_ref[s] = part

            @pl.when(c > 0)
            def _(part=part, s=s):
                acc_ref[s] += part

        @pl.when(c == nc - 1)
        def _():
            o_ref[...] = acc_ref[...].astype(o_ref.dtype)

    return pl.pallas_call(
        body,
        name=name,
        grid=(k // tk, nj, nc),
        in_specs=[
            pl.BlockSpec((tmc, tk), lambda i, j, c: (c, i)),
            pl.BlockSpec((tmc, ps * tn), lambda i, j, c: (c, j)),
        ],
        out_specs=pl.BlockSpec((ps, tk, tn), lambda i, j, c: (j // q, i, j % q)),
        out_shape=jax.ShapeDtypeStruct((p, k, n), out_dtype),
        scratch_shapes=[pltpu.VMEM((ps, tk, tn), F32)],
        compiler_params=_params(("parallel", "parallel", "arbitrary")),
    )(a, g)


def _rows(cols):
    return pl.BlockSpec((ROW_TILE, cols), lambda i: (i, 0))


def _vec(cols, rows=1):
    return pl.BlockSpec((rows, cols), lambda i: (0, 0))


def _layer_norm_hat(z):
    mu = jnp.mean(z, axis=-1, keepdims=True)
    zc = z - mu
    var = jnp.mean(zc * zc, axis=-1, keepdims=True)
    rstd = lax.rsqrt(var + LN_EPS)
    return zc * rstd, rstd


def modulate(x, scale, shift, *, name):
    s, d = x.shape

    def body(x_ref, sc_ref, sh_ref, h_ref):
        h_ref[...] = (x_ref[...] * (1.0 + sc_ref[...]) + sh_ref[...]).astype(BF16)

    return pl.pallas_call(
        body,
        name=name,
        grid=(s // ROW_TILE,),
        in_specs=[_rows(d), _vec(d), _vec(d)],
        out_specs=_rows(d),
        out_shape=jax.ShapeDtypeStruct((s, d), BF16),
        compiler_params=_params(("parallel",)),
    )(x, scale, shift)


def residual_ln(x, y, gate1, g, b, nscale, nshift, *, name):
    s, d = x.shape

    def body(x_ref, y_ref, gt_ref, g_ref, b_ref, sc_ref, sh_ref, xn_ref, hn_ref):
        z = ALPHA * x_ref[...] + gt_ref[...] * y_ref[...]
        xhat, _ = _layer_norm_hat(z)
        xn = xhat * g_ref[...] + b_ref[...]
        xn_ref[...] = xn
        hn_ref[...] = (xn * (1.0 + sc_ref[...]) + sh_ref[...]).astype(BF16)

    return pl.pallas_call(
        body,
        name=name,
        grid=(s // ROW_TILE,),
        in_specs=[_rows(d), _rows(d), _vec(d), _vec(d), _vec(d), _vec(d), _vec(d)],
        out_specs=[_rows(d), _rows(d)],
        out_shape=[jax.ShapeDtypeStruct((s, d), F32), jax.ShapeDtypeStruct((s, d), BF16)],
        compiler_params=_params(("parallel",)),
    )(x, y, gate1, g, b, nscale, nshift)


def residual_ln_loss(x, y, gate1, g, b, target, *, name):
    s, d = x.shape

    def body(x_ref, y_ref, gt_ref, g_ref, b_ref, t_ref, dx_ref, loss_ref):
        z = ALPHA * x_ref[...] + gt_ref[...] * y_ref[...]
        xhat, _ = _layer_norm_hat(z)
        err = xhat * g_ref[...] + b_ref[...] - t_ref[...]
        dx_ref[...] = err * (1.0 / d)
        part = jnp.sum(jnp.sum(err * err, axis=1, keepdims=True), axis=0, keepdims=True) * (0.5 / d)

        @pl.when(pl.program_id(0) == 0)
        def _():
            loss_ref[...] = part

        @pl.when(pl.program_id(0) > 0)
        def _():
            loss_ref[...] += part

    return pl.pallas_call(
        body,
        name=name,
        grid=(s // ROW_TILE,),
        in_specs=[_rows(d), _rows(d), _vec(d), _vec(d), _vec(d), _rows(d)],
        out_specs=[_rows(d), pl.BlockSpec((1, 1), lambda i: (0, 0))],
        out_shape=[jax.ShapeDtypeStruct((s, d), F32), jax.ShapeDtypeStruct((1, 1), F32)],
        compiler_params=_params(("arbitrary",)),
    )(x, y, gate1, g, b, target)


ST_DSCALE, ST_DSHIFT, ST_DG, ST_DB, ST_DGATE = 0, 1, 2, 3, 4


def residual_ln_bwd(*, name, later=None, dxo=None, this=None):
    lead = later[0] if later is not None else dxo
    s, d = lead.shape
    has_later, has_ln = later is not None, this is not None

    def body(*refs):
        refs = list(refs)
        if has_later:
            dh_ref, dzl_ref, xm_ref, scl_ref = refs[:4]
            refs = refs[4:]
        else:
            dxo_ref = refs.pop(0)
        if has_ln:
            x_ref, y_ref, gt_ref, g_ref = refs[:4]
            refs = refs[4:]
            dz_ref, dy_ref, st_ref = refs
        else:
            dx_ref, st_ref = refs

        @pl.when(pl.program_id(0) == 0)
        def _():
            st_ref[...] = jnp.zeros_like(st_ref)

        def acc(row, val):
            st_ref[row : row + 1, :] += jnp.sum(val, axis=0, keepdims=True)

        if has_later:
            dh = dh_ref[...]
            g_out = ALPHA * dzl_ref[...] + dh * (1.0 + scl_ref[...])
            acc(ST_DSCALE, dh * xm_ref[...])
            acc(ST_DSHIFT, dh)
        else:
            g_out = dxo_ref[...]
        if not has_ln:
            dx_ref[...] = g_out
            return
        y = y_ref[...]
        gate1 = gt_ref[...]
        z = ALPHA * x_ref[...] + gate1 * y
        xhat, rstd = _layer_norm_hat(z)
        acc(ST_DG, g_out * xhat)
        acc(ST_DB, g_out)
        dxh = g_out * g_ref[...]
        m1 = jnp.mean(dxh, axis=-1, keepdims=True)
        m2 = jnp.mean(dxh * xhat, axis=-1, keepdims=True)
        dz = rstd * (dxh - m1 - xhat * m2)
        acc(ST_DGATE, dz * y)
        dz_ref[...] = dz
        dy_ref[...] = (dz * gate1).astype(BF16)

    ins, specs = [], []
    if has_later:
        ins += list(later)
        specs += [_rows(d), _rows(d), _rows(d), _vec(d)]
    else:
        ins += [dxo]
        specs += [_rows(d)]
    if has_ln:
        ins += list(this)
        specs += [_rows(d), _rows(d), _vec(d), _vec(d)]
        out_specs = [_rows(d), _rows(d), _vec(d, SUBLANES)]
        out_shape = [
            jax.ShapeDtypeStruct((s, d), F32),
            jax.ShapeDtypeStruct((s, d), BF16),
            jax.ShapeDtypeStruct((SUBLANES, d), F32),
        ]
    else:
        out_specs = [_rows(d), _vec(d, SUBLANES)]
        out_shape = [jax.ShapeDtypeStruct((s, d), F32), jax.ShapeDtypeStruct((SUBLANES, d), F32)]
    return pl.pallas_call(
        body,
        name=name,
        grid=(s // ROW_TILE,),
        in_specs=specs,
        out_specs=out_specs,
        out_shape=out_shape,
        compiler_params=_params(("arbitrary",)),
    )(*ins)


GATE_CHUNKS = 4


def _gelu(x):
    return 0.5 * x * (1.0 + jnp.tanh(GELU_C * (x + GELU_A * x * x * x)))


def _gelu_grad(x):
    t = jnp.tanh(GELU_C * (x + GELU_A * x * x * x))
    return 0.5 * (1.0 + t) + 0.5 * x * (1.0 - t * t) * (GELU_C * (1.0 + 3.0 * GELU_A * x * x))


def _causal_weights(w_ref, transpose):
    t = lax.broadcasted_iota(jnp.int32, (CHUNK, CHUNK), 0)
    s = lax.broadcasted_iota(jnp.int32, (CHUNK, CHUNK), 1)
    out = []
    for g in range(A_GROUPS):
        w = jnp.where(t >= s, w_ref[g], 0.0)
        out.append((w.T if transpose else w).astype(BF16))
    return out


def _spatial(ws, vn, lo_mask):
    rows = vn.shape[0]
    out_rows = []
    for r in range(rows // CHUNK):
        cols = []
        for j in range(A_GROUPS // 2):
            blk = vn[r * CHUNK : (r + 1) * CHUNK, j * LANES : (j + 1) * LANES]
            za = jnp.dot(ws[2 * j], blk, preferred_element_type=F32)
            zb = jnp.dot(ws[2 * j + 1], blk, preferred_element_type=F32)
            cols.append(jnp.where(lo_mask, za, zb))
        out_rows.append(jnp.concatenate(cols, axis=1))
    return jnp.concatenate(out_rows, axis=0)


def _gate_forward(a, vg, vb, ws, bias, lo_mask):
    u = _gelu(a[:, :D_MODEL])
    v = _gelu(a[:, D_MODEL:])
    vhat, rstd = _layer_norm_hat(v)
    vn = (vhat * vg + vb).astype(BF16)
    z = _spatial(ws, vn, lo_mask) + jnp.concatenate([bias] * (a.shape[0] // CHUNK), axis=0)
    return u, vhat, rstd, vn, z


def gate_fwd(a_pre, vn_g, vn_b, w_s, bias_full, *, name):
    s = a_pre.shape[0]
    tr = GATE_CHUNKS * CHUNK

    def body(a_ref, vg_ref, vb_ref, w_ref, bias_ref, p_ref):
        lo_mask = _lane((CHUNK, LANES)) < A_GROUP_DIM
        ws = _causal_weights(w_ref, transpose=False)
        u, _, _, _, z = _gate_forward(a_ref[...].astype(F32), vg_ref[...], vb_ref[...], ws, bias_ref[...], lo_mask)
        p_ref[...] = (u * z).astype(BF16)

    return pl.pallas_call(
        body,
        name=name,
        grid=(s // tr,),
        in_specs=[
            pl.BlockSpec((tr, 2 * D_MODEL), lambda i: (i, 0)),
            _vec(D_MODEL),
            _vec(D_MODEL),
            pl.BlockSpec((A_GROUPS, CHUNK, CHUNK), lambda i: (0, 0, 0)),
            _vec(D_MODEL, CHUNK),
        ],
        out_specs=pl.BlockSpec((tr, D_MODEL), lambda i: (i, 0)),
        out_shape=jax.ShapeDtypeStruct((s, D_MODEL), BF16),
        compiler_params=_params(("parallel",)),
    )(a_pre, vn_g, vn_b, w_s, bias_full)


def gate_bwd(a_pre, dp, vn_g, vn_b, w_s, bias_full, *, name):
    s = a_pre.shape[0]
    tr = GATE_CHUNKS * CHUNK
    nsteps = s // tr

    def body(a_ref, dp_ref, vg_ref, vb_ref, w_ref, bias_ref, da_ref, dw_ref, dbs_ref, rows_ref, dbias_acc):
        step = pl.program_id(0)
        lo_mask = _lane((CHUNK, LANES)) < A_GROUP_DIM

        @pl.when(step == 0)
        def _():
            dw_ref[...] = jnp.zeros_like(dw_ref)
            rows_ref[...] = jnp.zeros_like(rows_ref)
            dbias_acc[...] = jnp.zeros_like(dbias_acc)

        a = a_ref[...].astype(F32)
        vg = vg_ref[...]
        ws = _causal_weights(w_ref, transpose=False)
        wts = _causal_weights(w_ref, transpose=True)
        u, vhat, rstd, vn, z = _gate_forward(a, vg, vb_ref[...], ws, bias_ref[...], lo_mask)
        dp = dp_ref[...]
        du = dp * z
        dzz = dp * u
        dzz_b = dzz.astype(BF16)
        dvn = _spatial(wts, dzz_b, lo_mask)
        dbias = None
        for r in range(GATE_CHUNKS):
            rs = slice(r * CHUNK, (r + 1) * CHUNK)
            dbias = dzz[rs] if dbias is None else dbias + dzz[rs]
            for j in range(A_GROUPS // 2):
                cs = slice(j * LANES, (j + 1) * LANES)
                dblk = dzz[rs, cs]
                vblk = vn[rs, cs]
                for half in range(2):
                    keep = lo_mask if half == 0 else jnp.logical_not(lo_mask)
                    dm = jnp.where(keep, dblk, 0.0).astype(BF16)
                    dw_ref[2 * j + half] += lax.dot_general(
                        dm, vblk, (((1,), (1,)), ((), ())), preferred_element_type=F32
                    )
        dbias_acc[...] += dbias
        rows_ref[1:2, :D_MODEL] += jnp.sum(dvn * vhat, axis=0, keepdims=True)
        rows_ref[1:2, D_MODEL:] += jnp.sum(dvn, axis=0, keepdims=True)
        dvh = dvn * vg
        m1 = jnp.mean(dvh, axis=-1, keepdims=True)
        m2 = jnp.mean(dvh * vhat, axis=-1, keepdims=True)
        dv = rstd * (dvh - m1 - vhat * m2)
        da_u = du * _gelu_grad(a[:, :D_MODEL])
        da_v = dv * _gelu_grad(a[:, D_MODEL:])
        da_ref[:, :D_MODEL] = da_u.astype(BF16)
        da_ref[:, D_MODEL:] = da_v.astype(BF16)
        rows_ref[0:1, :D_MODEL] += jnp.sum(da_u, axis=0, keepdims=True)
        rows_ref[0:1, D_MODEL:] += jnp.sum(da_v, axis=0, keepdims=True)

        @pl.when(step == nsteps - 1)
        def _():
            t = lax.broadcasted_iota(jnp.int32, (CHUNK, CHUNK), 0)
            sx = lax.broadcasted_iota(jnp.int32, (CHUNK, CHUNK), 1)
            for g in range(A_GROUPS):
                dw_ref[g] = jnp.where(t >= sx, dw_ref[g], 0.0)
            dbs_ref[...] = _reduce_groups(dbias_acc[...])

    return pl.pallas_call(
        body,
        name=name,
        grid=(nsteps,),
        in_specs=[
            pl.BlockSpec((tr, 2 * D_MODEL), lambda i: (i, 0)),
            pl.BlockSpec((tr, D_MODEL), lambda i: (i, 0)),
            _vec(D_MODEL),
            _vec(D_MODEL),
            pl.BlockSpec((A_GROUPS, CHUNK, CHUNK), lambda i: (0, 0, 0)),
            _vec(D_MODEL, CHUNK),
        ],
        out_specs=[
            pl.BlockSpec((tr, 2 * D_MODEL), lambda i: (i, 0)),
            pl.BlockSpec((A_GROUPS, CHUNK, CHUNK), lambda i: (0, 0, 0)),
            _vec(LANES, CHUNK),
            _vec(2 * D_MODEL, SUBLANES),
        ],
        out_shape=[
            jax.ShapeDtypeStruct((s, 2 * D_MODEL), BF16),
            jax.ShapeDtypeStruct((A_GROUPS, CHUNK, CHUNK), F32),
            jax.ShapeDtypeStruct((CHUNK, LANES), F32),
            jax.ShapeDtypeStruct((SUBLANES, 2 * D_MODEL), F32),
        ],
        scratch_shapes=[pltpu.VMEM((CHUNK, D_MODEL), F32)],
        compiler_params=_params(("arbitrary",)),
    )(a_pre, dp, vn_g, vn_b, w_s, bias_full)


def alibi_tables(dilation):
    qi = jnp.arange(SPAN)[:, None]
    ki = jnp.arange(2 * SPAN)[None, :]
    diff = SPAN + qi - ki
    valid = (diff >= 0) & (diff <= SPAN)
    heads = jnp.arange(1, B_HEADS + 1, dtype=F32)
    slopes = jnp.exp2(-8.0 * heads / B_HEADS)
    bias = -slopes[:, None, None] * (dilation * diff).astype(F32)
    return jnp.where(valid[None], bias, NEG)


def permute_rows(x, dilation, *, inverse, name, add=None):
    s, w = x.shape
    tile = SPAN * dilation
    nat = pl.BlockSpec((tile, w), lambda i: (i, 0))
    streams = pl.BlockSpec((dilation, SPAN, w), lambda i: (0, i, 0))
    x3 = x.reshape(dilation, s // dilation, w) if inverse else x

    def body(*refs):
        if not inverse:
            x_ref, o_ref = refs
            o_ref[...] = jnp.swapaxes(x_ref[...].reshape(SPAN, dilation, w), 0, 1)
            return
        val = jnp.swapaxes(refs[0][...], 0, 1).reshape(tile, w)
        if add is not None:
            val = val + refs[1][...]
        refs[-1][...] = val

    out = pl.pallas_call(
        body,
        name=name,
        grid=(s // tile,),
        in_specs=([streams] + ([nat] if add is not None else [])) if inverse else [nat],
        out_specs=nat if inverse else streams,
        out_shape=jax.ShapeDtypeStruct((s, w) if inverse else (dilation, s // dilation, w), x.dtype),
        compiler_params=_params(("parallel",)),
    )(*([x3] + ([add] if add is not None else [])))
    return out.reshape(s, w)


def _qkv_specs(nb, clamp):
    def spec(which, prev):
        def index(r, n):
            blk = jnp.maximum(n - 1, 0) if prev else n
            return (r * nb + clamp(blk), which)

        return pl.BlockSpec((SPAN, D_MODEL), index)

    return [spec(0, False), spec(1, True), spec(1, False), spec(2, True), spec(2, False)]


def attn_fwd(qkv_p, pat, *, name):
    _, dilation = B_PATTERNS[pat]
    nb = SEQ // dilation // SPAN
    bias = alibi_tables(dilation)

    def body(q_ref, kp_ref, kc_ref, vp_ref, vc_ref, bias_ref, o_ref, lse_ref):
        n = pl.program_id(1)
        lane = _lane((SPAN, LANES))
        lo_mask = lane < B_HEAD_DIM
        first_prev = jnp.logical_and(n == 0, _lane((SPAN, 2 * SPAN)) < SPAN)
        q = q_ref[...] * jnp.asarray(B_HEAD_DIM**-0.5, BF16)
        kk = jnp.concatenate([kp_ref[...], kc_ref[...]], axis=0)
        vv = jnp.concatenate([vp_ref[...], vc_ref[...]], axis=0)
        halves = (lo_mask.astype(BF16), jnp.logical_not(lo_mask).astype(BF16))
        stats = jnp.zeros((SPAN, LANES), F32)
        for j in range(B_HEADS // 2):
            cs = slice(j * LANES, (j + 1) * LANES)
            qp, kp, vp = q[:, cs], kk[:, cs], vv[:, cs]
            outs = []
            for half in range(2):
                h = 2 * j + half
                sc = lax.dot_general(qp * halves[half], kp, (((1,), (1,)), ((), ())), preferred_element_type=F32)
                sc = jnp.where(first_prev, NEG, sc + bias_ref[h])
                m = jnp.max(sc, axis=1, keepdims=True)
                p = jnp.exp(sc - m)
                l = jnp.sum(p, axis=1, keepdims=True)
                acc = jnp.dot(p.astype(BF16), vp, preferred_element_type=F32)
                outs.append(acc * (1.0 / l))
                stats = jnp.where(lane == h, m + jnp.log(l), stats)
            o_ref[:, cs] = jnp.where(lo_mask, outs[0], outs[1])
        lse_ref[...] = stats

    return pl.pallas_call(
        body,
        name=name,
        grid=(dilation, nb),
        in_specs=[
            *_qkv_specs(nb, lambda blk: blk),
            pl.BlockSpec((B_HEADS, SPAN, 2 * SPAN), lambda r, n: (0, 0, 0)),
        ],
        out_specs=[
            pl.BlockSpec((SPAN, D_MODEL), lambda r, n: (r * nb + n, 0)),
            pl.BlockSpec((SPAN, LANES), lambda r, n: (r * nb + n, 0)),
        ],
        out_shape=[jax.ShapeDtypeStruct((SEQ, D_MODEL), F32), jax.ShapeDtypeStruct((SEQ, LANES), F32)],
        compiler_params=_params(("parallel", "arbitrary")),
    )(qkv_p, qkv_p, qkv_p, qkv_p, qkv_p, bias)


def attn_combine(outs, lses, *, name):
    def body(o0, o1, o2, l0, l1, l2, ob_ref, of_ref, lse_ref):
        ls = [l0[...], l1[...], l2[...]]
        m = jnp.maximum(jnp.maximum(ls[0], ls[1]), ls[2])
        tot = jnp.log(jnp.exp(ls[0] - m) + jnp.exp(ls[1] - m) + jnp.exp(ls[2] - m)) + m
        o = None
        for o_ref, l in zip((o0, o1, o2), ls, strict=True):
            term = _expand_groups(jnp.exp(l - tot)) * o_ref[...]
            o = term if o is None else o + term
        ob_ref[...] = o.astype(BF16)
        of_ref[...] = o
        lse_ref[...] = tot

    return pl.pallas_call(
        body,
        name=name,
        grid=(SEQ // ROW_TILE,),
        in_specs=[_rows(D_MODEL)] * 3 + [_rows(LANES)] * 3,
        out_specs=[_rows(D_MODEL), _rows(D_MODEL), _rows(LANES)],
        out_shape=[
            jax.ShapeDtypeStruct((SEQ, D_MODEL), BF16),
            jax.ShapeDtypeStruct((SEQ, D_MODEL), F32),
            jax.ShapeDtypeStruct((SEQ, LANES), F32),
        ],
        compiler_params=_params(("parallel",)),
    )(*outs, *lses)


def attn_delta(do, o, *, name):
    def body(do_ref, o_ref, dob_ref, dl_ref):
        do_v = do_ref[...]
        dob_ref[...] = do_v.astype(BF16)
        dl_ref[...] = _reduce_groups(do_v * o_ref[...])

    return pl.pallas_call(
        body,
        name=name,
        grid=(SEQ // ROW_TILE,),
        in_specs=[_rows(D_MODEL), _rows(D_MODEL)],
        out_specs=[_rows(D_MODEL), _rows(LANES)],
        out_shape=[jax.ShapeDtypeStruct((SEQ, D_MODEL), BF16), jax.ShapeDtypeStruct((SEQ, LANES), F32)],
        compiler_params=_params(("parallel",)),
    )(do, o)


def attn_bwd(qkv_p, do_p, lse_p, delta_p, pat, *, name):
    _, dilation = B_PATTERNS[pat]
    nb = SEQ // dilation // SPAN
    bias = alibi_tables(dilation)
    last = nb - 1
    q_cols, k_cols, v_cols = (slice(i * D_MODEL, (i + 1) * D_MODEL) for i in range(3))

    def body(q_ref, kp_ref, kc_ref, vp_ref, vc_ref, do_ref, lse_ref, dl_ref, bias_ref, out_ref, cq_ref, ck_ref, cv_ref):
        n = pl.program_id(1)

        @pl.when(n == nb)
        def _():
            out_ref[:, q_cols] = cq_ref[...].astype(BF16)
            out_ref[:, k_cols] = ck_ref[...].astype(BF16)
            out_ref[:, v_cols] = cv_ref[...].astype(BF16)

        @pl.when(n == 0)
        def _():
            cq_ref[...] = jnp.zeros_like(cq_ref)
            ck_ref[...] = jnp.zeros_like(ck_ref)
            cv_ref[...] = jnp.zeros_like(cv_ref)

        @pl.when(n < nb)
        def _():
            lane = _lane((SPAN, LANES))
            lo_mask = lane < B_HEAD_DIM
            first_prev = jnp.logical_and(n == 0, _lane((SPAN, 2 * SPAN)) < SPAN)
            q = q_ref[...] * jnp.asarray(B_HEAD_DIM**-0.5, BF16)
            kk = jnp.concatenate([kp_ref[...], kc_ref[...]], axis=0)
            vv = jnp.concatenate([vp_ref[...], vc_ref[...]], axis=0)
            do_v = do_ref[...]
            lse_v = lse_ref[...]
            dl_v = dl_ref[...]
            halves = (lo_mask.astype(BF16), jnp.logical_not(lo_mask).astype(BF16))
            for j in range(B_HEADS // 2):
                cs = slice(j * LANES, (j + 1) * LANES)
                qp, kp, vp, dop = q[:, cs], kk[:, cs], vv[:, cs], do_v[:, cs]
                dqs = []
                dk_pair = None
                dv_pair = None
                for half in range(2):
                    h = 2 * j + half
                    qm = qp * halves[half]
                    dom = dop * halves[half]
                    lse_h = jnp.sum(jnp.where(lane == h, lse_v, 0.0), axis=1, keepdims=True)
                    dl_h = jnp.sum(jnp.where(lane == h, dl_v, 0.0), axis=1, keepdims=True)
                    sc = lax.dot_general(qm, kp, (((1,), (1,)), ((), ())), preferred_element_type=F32)
                    sc = jnp.where(first_prev, NEG, sc + bias_ref[h])
                    p = jnp.exp(sc - lse_h)
                    dp = lax.dot_general(dom, vp, (((1,), (1,)), ((), ())), preferred_element_type=F32)
                    ds = (p * (dp - dl_h)).astype(BF16)
                    dqs.append(jnp.dot(ds, kp, preferred_element_type=F32))
                    dk_h = lax.dot_general(ds, qm, (((0,), (0,)), ((), ())), preferred_element_type=F32)
                    dv_h = lax.dot_general(
                        p.astype(BF16), dom, (((0,), (0,)), ((), ())), preferred_element_type=F32
                    )
                    dk_pair = dk_h if dk_pair is None else dk_pair + dk_h
                    dv_pair = dv_h if dv_pair is None else dv_pair + dv_h
                oq = slice(j * LANES, (j + 1) * LANES)
                ok = slice(D_MODEL + j * LANES, D_MODEL + (j + 1) * LANES)
                ov = slice(2 * D_MODEL + j * LANES, 2 * D_MODEL + (j + 1) * LANES)
                out_ref[:, oq] = cq_ref[:, cs].astype(BF16)
                out_ref[:, ok] = (ck_ref[:, cs] + dk_pair[:SPAN]).astype(BF16)
                out_ref[:, ov] = (cv_ref[:, cs] + dv_pair[:SPAN]).astype(BF16)
                cq_ref[:, cs] = jnp.where(lo_mask, dqs[0], dqs[1]) * (B_HEAD_DIM**-0.5)
                ck_ref[:, cs] = dk_pair[SPAN:]
                cv_ref[:, cs] = dv_pair[SPAN:]

    def clamp(n):
        return jnp.minimum(n, last)

    def row_spec(width):
        return pl.BlockSpec((SPAN, width), lambda r, n: (r * nb + clamp(n), 0))

    return pl.pallas_call(
        body,
        name=name,
        grid=(dilation, nb + 1),
        in_specs=[
            *_qkv_specs(nb, clamp),
            row_spec(D_MODEL),
            row_spec(LANES),
            row_spec(LANES),
            pl.BlockSpec((B_HEADS, SPAN, 2 * SPAN), lambda r, n: (0, 0, 0)),
        ],
        out_specs=pl.BlockSpec((SPAN, 3 * D_MODEL), lambda r, n: (r * nb + jnp.maximum(n - 1, 0), 0)),
        out_shape=jax.ShapeDtypeStruct((SEQ, 3 * D_MODEL), BF16),
        scratch_shapes=[pltpu.VMEM((SPAN, D_MODEL), F32)] * 3,
        compiler_params=_params(("arbitrary", "arbitrary")),
    )(qkv_p, qkv_p, qkv_p, qkv_p, qkv_p, do_p, lse_p, delta_p, bias)


def _position():
    x, y, c = lax.axis_index("x"), lax.axis_index("y"), lax.axis_index("c")
    return x, y, c, 4 * x + 2 * y + c


def _peer(k, x, y, c):
    px = 1 - x if k & 4 else x
    py = 1 - y if k & 2 else y
    pc = 1 - c if k & 1 else c
    return (px, py, pc), 4 * px + 2 * py + pc


def _remote(src, dst, send_sem, recv_sem, device):
    return pltpu.make_async_remote_copy(
        src_ref=src, dst_ref=dst, send_sem=send_sem, recv_sem=recv_sem, device_id=device, device_id_type=MESH
    )


def _silu_bf16(cf):
    return (cf * (1.0 / (1.0 + jnp.exp(-cf)))).astype(BF16)


def ada_exchange(c8, w4, b4, ln8):
    nt, _, ncol = w4.shape

    def body(c8_ref, w_ref, b_ref, ln_ref, cg_ref, lng_ref, mrecv_ref, mloc_ref, send_sems, recv_sems):
        x, y, c, me = _position()
        cg_ref[me] = c8_ref[...]
        lng_ref[me] = ln_ref[...]
        first = []
        for k in range(1, N_DEV):
            dev, _ = _peer(k, x, y, c)
            first.append(_remote(c8_ref, cg_ref.at[me], send_sems.at[0, k], recv_sems.at[0, k], dev))
            first.append(_remote(ln_ref, lng_ref.at[me], send_sems.at[1, k], recv_sems.at[1, k], dev))
        for cp in first:
            cp.start()
        for k in range(1, N_DEV):
            dev, pid = _peer(k, x, y, c)
            _remote(c8_ref, cg_ref.at[pid], send_sems.at[0, k], recv_sems.at[0, k], dev).wait_recv()
            _remote(ln_ref, lng_ref.at[pid], send_sems.at[1, k], recv_sems.at[1, k], dev).wait_recv()
        sc = _silu_bf16(cg_ref[...].reshape(N_DEV * SUBLANES, D_MODEL))
        for t in range(nt):
            mloc_ref[t] = jnp.dot(sc, w_ref[t].astype(BF16), preferred_element_type=F32) + b_ref[t : t + 1, :]

        def group(dev_id):
            return pl.ds(pl.multiple_of(dev_id * SUBLANES, SUBLANES), SUBLANES)

        mrecv_ref[me] = mloc_ref[:, group(me), :]
        second = []
        for k in range(1, N_DEV):
            dev, pid = _peer(k, x, y, c)
            second.append(
                _remote(mloc_ref.at[:, group(pid), :], mrecv_ref.at[me], send_sems.at[2, k], recv_sems.at[2, k], dev)
            )
        for cp in second:
            cp.start()
        for k in range(1, N_DEV):
            dev, pid = _peer(k, x, y, c)
            _remote(
                mloc_ref.at[:, group(pid), :], mrecv_ref.at[pid], send_sems.at[2, k], recv_sems.at[2, k], dev
            ).wait_recv()
        for cp in first + second:
            cp.wait_send()

    return pl.pallas_call(
        body,
        name="ada_exchange",
        in_specs=[VMEM, VMEM, VMEM, VMEM],
        out_specs=[VMEM, VMEM, VMEM],
        out_shape=[
            jax.ShapeDtypeStruct((N_DEV, SUBLANES, D_MODEL), F32),
            jax.ShapeDtypeStruct((N_DEV, SUBLANES, LANES), F32),
            jax.ShapeDtypeStruct((N_DEV, nt, SUBLANES, ncol), F32),
        ],
        scratch_shapes=[
            pltpu.VMEM((nt, N_DEV * SUBLANES, ncol), F32),
            pltpu.SemaphoreType.DMA((3, N_DEV)),
            pltpu.SemaphoreType.DMA((3, N_DEV)),
        ],
        compiler_params=pltpu.CompilerParams(vmem_limit_bytes=VMEM_LIMIT_BYTES),
    )(c8, w4, b4, ln8)


def small_exchange(dmx, flat):
    def body(dmx_ref, flat_ref, dmrecv_ref, red_ref, land_ref, send_sems, recv_sems):
        x, y, c, me = _position()
        dmrecv_ref[me] = dmx_ref[me]
        land_ref[me] = flat_ref[me]
        first = []
        for k in range(1, N_DEV):
            dev, pid = _peer(k, x, y, c)
            first.append(_remote(dmx_ref.at[pid], dmrecv_ref.at[me], send_sems.at[0, k], recv_sems.at[0, k], dev))
            first.append(_remote(flat_ref.at[pid], land_ref.at[me], send_sems.at[1, k], recv_sems.at[1, k], dev))
        for cp in first:
            cp.start()
        for k in range(1, N_DEV):
            dev, pid = _peer(k, x, y, c)
            _remote(dmx_ref.at[pid], dmrecv_ref.at[pid], send_sems.at[0, k], recv_sems.at[0, k], dev).wait_recv()
            _remote(flat_ref.at[pid], land_ref.at[pid], send_sems.at[1, k], recv_sems.at[1, k], dev).wait_recv()
        total = land_ref[0]
        for s in range(1, N_DEV):
            total = total + land_ref[s]
        red_ref[me] = total
        second = []
        for k in range(1, N_DEV):
            dev, _ = _peer(k, x, y, c)
            second.append(_remote(red_ref.at[me], red_ref.at[me], send_sems.at[2, k], recv_sems.at[2, k], dev))
        for cp in second:
            cp.start()
        for k in range(1, N_DEV):
            dev, pid = _peer(k, x, y, c)
            _remote(red_ref.at[pid], red_ref.at[pid], send_sems.at[2, k], recv_sems.at[2, k], dev).wait_recv()
        for cp in first + second:
            cp.wait_send()

    return pl.pallas_call(
        body,
        name="small_exchange",
        in_specs=[VMEM, VMEM],
        out_specs=[VMEM, VMEM],
        out_shape=[jax.ShapeDtypeStruct(dmx.shape, F32), jax.ShapeDtypeStruct(flat.shape, F32)],
        scratch_shapes=[
            pltpu.VMEM(flat.shape, F32),
            pltpu.SemaphoreType.DMA((3, N_DEV)),
            pltpu.SemaphoreType.DMA((3, N_DEV)),
        ],
        compiler_params=pltpu.CompilerParams(vmem_limit_bytes=VMEM_LIMIT_BYTES),
    )(dmx, flat)


HBM = pl.BlockSpec(memory_space=pltpu.HBM)
SEM = pl.BlockSpec(memory_space=pltpu.SEMAPHORE)
EFFECT = pltpu.SideEffectType.DATAFLOW_SIDE_EFFECTING


def _own_slot(me, block):
    land = lax.empty((N_DEV, *block.shape), block.dtype)
    return lax.dynamic_update_slice_in_dim(land, block[None], me, axis=0)


class Exchange:
    def __init__(self, srcs, lands, after, *, scatter, name):
        self.scatter = scatter
        self.name = name
        nt = self.nt = len(srcs)
        peers = N_DEV - 1

        def body(*refs):
            src_refs, land_refs = refs[:nt], refs[nt : 2 * nt]
            send_sems, recv_sems = refs[2 * nt + 1 : 3 * nt + 1], refs[3 * nt + 1 : 4 * nt + 1]
            token = refs[-1]
            x, y, c, me = _position()
            for t in range(nt):
                for k in range(1, N_DEV):
                    dev, pid = _peer(k, x, y, c)
                    src = src_refs[t].at[pid] if scatter else src_refs[t]
                    _remote(src, land_refs[t].at[me], send_sems[t].at[k - 1], recv_sems[t].at[k - 1], dev).start()
            token[...] = jnp.zeros_like(token)

        outs = pl.pallas_call(
            body,
            name=name + "_start",
            in_specs=[HBM] * (2 * nt) + [ANY],
            out_specs=[SEM] * (2 * nt) + [HBM] * (2 * nt) + [VMEM],
            out_shape=[pltpu.SemaphoreType.DMA((peers,))] * (2 * nt)
            + [pltpu.HBM(a.shape, a.dtype) for a in (*srcs, *lands)]
            + [jax.ShapeDtypeStruct((SUBLANES, LANES), F32)],
            input_output_aliases={i: 2 * nt + i for i in range(2 * nt)},
            compiler_params=pltpu.CompilerParams(has_side_effects=EFFECT),
        )(*[pltpu.with_memory_space_constraint(a, pltpu.HBM) for a in (*srcs, *lands)], after)
        self.send_sems, self.recv_sems = outs[:nt], outs[nt : 2 * nt]
        self.srcs, self.lands = outs[2 * nt : 3 * nt], outs[3 * nt : 4 * nt]
        self.token = outs[-1]

    def zero(self):
        return self.token[0, 0]

    def wait(self, which, after, *, name):
        scatter = self.scatter
        n = len(which)

        def body(*refs):
            src_refs, land_refs = refs[:n], refs[n : 2 * n]
            send_sems, recv_sems = refs[2 * n : 3 * n], refs[3 * n : 4 * n]
            x, y, c, _ = _position()
            for t in range(n):
                for k in range(1, N_DEV):
                    dev, pid = _peer(k, x, y, c)
                    src = src_refs[t].at[pid] if scatter else src_refs[t]
                    cp = _remote(src, land_refs[t].at[pid], send_sems[t].at[k - 1], recv_sems[t].at[k - 1], dev)
                    cp.wait_send()
                    cp.wait_recv()

        srcs = [self.srcs[t] for t in which]
        lands = [self.lands[t] for t in which]
        outs = pl.pallas_call(
            body,
            name=name,
            in_specs=[HBM] * (2 * n) + [SEM] * (2 * n) + [ANY],
            out_specs=[HBM] * (2 * n),
            out_shape=[pltpu.HBM(a.shape, a.dtype) for a in (*srcs, *lands)],
            input_output_aliases={i: i for i in range(2 * n)},
            compiler_params=pltpu.CompilerParams(has_side_effects=EFFECT),
        )(*srcs, *lands, *[self.send_sems[t] for t in which], *[self.recv_sems[t] for t in which], after)
        return outs[n:]


def _adam_update(g, w, m, v):
    m2 = ADAM_B1 * m + (1.0 - ADAM_B1) * g
    v2 = ADAM_B2 * v + (1.0 - ADAM_B2) * jnp.square(g)
    m_hat = m2 / (1.0 - ADAM_B1**ADAM_STEP)
    v_hat = v2 / (1.0 - ADAM_B2**ADAM_STEP)
    delta = -ADAM_LR * (m_hat / (jnp.sqrt(v_hat) + ADAM_EPS) + ADAM_WD * w)
    return delta, m2, v2


def adamw(gparts, w, m, v, *, name):
    p, r, c = gparts.shape
    tr = r if r <= 256 else (256 if c <= D_MODEL else 128)

    def body(g_ref, w_ref, m_ref, v_ref, go_ref, d_ref, mo_ref, vo_ref):
        g = g_ref[0].astype(F32)
        for i in range(1, p):
            g = g + g_ref[i].astype(F32)
        delta, m2, v2 = _adam_update(g, w_ref[...], m_ref[...], v_ref[...])
        go_ref[...] = g
        d_ref[...] = delta
        mo_ref[...] = m2
        vo_ref[...] = v2

    blk = pl.BlockSpec((tr, c), lambda i: (i, 0))
    return pl.pallas_call(
        body,
        name=name,
        grid=(r // tr,),
        in_specs=[pl.BlockSpec((p, tr, c), lambda i: (0, i, 0)), blk, blk, blk],
        out_specs=[blk] * 4,
        out_shape=[jax.ShapeDtypeStruct((r, c), F32)] * 4,
        compiler_params=_params(("parallel",)),
    )(gparts, w, m, v)


def ada_grad_adamw(cg, dmrecv, w4, m4, v4, *, name):
    nt, k, ncol = w4.shape

    def body(cg_ref, dm_ref, w_ref, m_ref, v_ref, go_ref, d_ref, mo_ref, vo_ref, gb_ref):
        sc = _silu_bf16(cg_ref[...].reshape(N_DEV * SUBLANES, k))
        dm = dm_ref[...].reshape(N_DEV * SUBLANES, ncol)
        g = lax.dot_general(sc, dm.astype(BF16), (((0,), (0,)), ((), ())), preferred_element_type=F32)
        delta, m2, v2 = _adam_update(g, w_ref[...], m_ref[...], v_ref[...])
        go_ref[...] = g
        d_ref[...] = delta
        mo_ref[...] = m2
        vo_ref[...] = v2
        gb_ref[...] = jnp.broadcast_to(jnp.sum(dm, axis=0, keepdims=True), (SUBLANES, ncol))

    wblk = pl.BlockSpec((None, k, ncol), lambda t: (t, 0, 0))
    return pl.pallas_call(
        body,
        name=name,
        grid=(nt,),
        in_specs=[
            pl.BlockSpec((N_DEV, SUBLANES, k), lambda t: (0, 0, 0)),
            pl.BlockSpec((N_DEV, None, SUBLANES, ncol), lambda t: (0, t, 0, 0)),
            wblk,
            wblk,
            wblk,
        ],
        out_specs=[wblk] * 4 + [pl.BlockSpec((None, SUBLANES, ncol), lambda t: (t, 0, 0))],
        out_shape=[jax.ShapeDtypeStruct((nt, k, ncol), F32)] * 4 + [jax.ShapeDtypeStruct((nt, SUBLANES, ncol), F32)],
        compiler_params=_params(("parallel",)),
    )(cg, dmrecv, w4, m4, v4)


def kernel(x, c, ada_w, ada_b, ln_g, ln_b, a_w_in, a_b_in, a_vn_g, a_vn_b, a_w_s, a_b_s, a_w_out, b_w_qkv, b_w_out, mlp_w_up, mlp_w_down, loss_target, m_ada_w, m_ada_b, m_ln_g, m_ln_b, m_a_w_in, m_a_b_in, m_a_vn_g, m_a_vn_b, m_a_w_s, m_a_b_s, m_a_w_out, m_b_w_qkv, m_b_w_out, m_mlp_w_up, m_mlp_w_down, v_ada_w, v_ada_b, v_ln_g, v_ln_b, v_a_w_in, v_a_b_in, v_a_vn_g, v_a_vn_b, v_a_w_s, v_a_b_s, v_a_w_out, v_b_w_qkv, v_b_w_out, v_mlp_w_up, v_mlp_w_down):
    x0 = x[0]
    target = loss_target[0]
    me = 4 * lax.axis_index("x") + 2 * lax.axis_index("y") + lax.axis_index("c")

    ada_w4 = ada_w.reshape(N_SUB, D_MODEL, -1)
    ada_b4 = ada_b.reshape(N_SUB, -1)
    ln8 = jnp.concatenate([ln_g.reshape(N_SUB, -1), ln_b.reshape(N_SUB, -1)], axis=0)
    c8 = jnp.broadcast_to(c, (SUBLANES, D_MODEL))
    cg, lng, mrecv = ada_exchange(c8, ada_w4, ada_b4, ln8)

    W_IN, W_AOUT, W_UP0, W_DN0, W_QKV, W_BOUT, W_UP1, W_DN1 = range(8)
    shards = [
        a_w_in[0].astype(BF16),
        a_w_out[0].astype(BF16),
        mlp_w_up[0].astype(BF16),
        mlp_w_down[0].astype(BF16),
        b_w_qkv[0].astype(BF16),
        b_w_out[0].astype(BF16),
        mlp_w_up[1].astype(BF16),
        mlp_w_down[1].astype(BF16),
    ]
    gather = Exchange(shards, [_own_slot(me, s) for s in shards], mrecv, scatter=False, name="gather")

    modv = mrecv[:, :, 0, :].transpose(1, 0, 2).reshape(N_SUB, 3 * D_MODEL) + gather.zero()
    shift = [modv[t : t + 1, :D_MODEL] for t in range(N_SUB)]
    scale = [modv[t : t + 1, D_MODEL : 2 * D_MODEL] for t in range(N_SUB)]
    gate1 = [1.0 + modv[t : t + 1, 2 * D_MODEL :] for t in range(N_SUB)]
    lng_full = [lng[:, t, :].reshape(1, D_MODEL) for t in range(N_SUB)]
    lnb_full = [lng[:, N_SUB + t, :].reshape(1, D_MODEL) for t in range(N_SUB)]

    ident = lambda acc: (acc,)
    relu2 = lambda a: jnp.square(jnp.maximum(a.astype(F32), 0.0)).astype(BF16)
    vn_g, vn_b, w_s = a_vn_g, a_vn_b, a_w_s[0]
    bias_full = jnp.repeat(a_b_s[0].T, A_GROUP_DIM, axis=1)
    w_up3, w_dn3 = [None, None], [None, None]

    def mlp_forward(i, h, after):
        up, dn = gather.wait([W_UP0, W_DN0] if i == 0 else [W_UP1, W_DN1], after, name=f"gather_wait_mlp{i}")
        w_up3[i], w_dn3[i] = up, dn.reshape(1, D_FF, D_MODEL)
        (a,) = mm_nn(h, w_up3[i], name=f"mlp{i}_up", tm=2048, ps=2, tn=512, tk=D_MODEL, epilogue=ident, outs=(BF16,))
        (y,) = mm_nn(
            a, w_dn3[i], name=f"mlp{i}_down", tm=2048, ps=1, tn=D_MODEL, tk=1024, prologue=relu2, epilogue=ident, outs=(F32,)
        )
        return a, y

    h0 = modulate(x0, scale[0], shift[0], name="modulate0")
    w_in3, w_aout3 = gather.wait([W_IN, W_AOUT], h0, name="gather_wait_a")
    w_aout3 = w_aout3.reshape(1, D_MODEL, D_MODEL)
    (a_pre,) = mm_nn(
        h0, w_in3, name="a_in", tm=2048, ps=4, tn=256, tk=D_MODEL, epilogue=lambda acc, b: (acc + b,),
        extras=[(a_b_in, "row")], outs=(BF16,),
    )
    p_gate = gate_fwd(a_pre, vn_g, vn_b, w_s, bias_full, name="gate_fwd")
    (y0,) = mm_nn(p_gate, w_aout3, name="a_out", tm=2048, ps=1, tn=D_MODEL, tk=D_MODEL, epilogue=ident, outs=(F32,))
    x1, h1 = residual_ln(x0, y0, gate1[0], lng_full[0], lnb_full[0], scale[1], shift[1], name="res_ln0")
    a1, y1 = mlp_forward(0, h1, y0)
    x2, h2 = residual_ln(x1, y1, gate1[1], lng_full[1], lnb_full[1], scale[2], shift[2], name="res_ln1")
    w_qkv_shards, w_bout3 = gather.wait([W_QKV, W_BOUT], y1, name="gather_wait_b")
    w_bout3 = w_bout3.reshape(1, D_MODEL, D_MODEL)
    w_qkv_cols = w_qkv_shards.transpose(1, 0, 2).reshape(D_MODEL, D_QKV)
    w_pat = [w_qkv_cols[None, :, 3 * D_MODEL * g : 3 * D_MODEL * (g + 1)] for g in range(N_PAT)]
    dil = [d for _, d in B_PATTERNS]
    h2_p = [h2] + [permute_rows(h2, dil[g], inverse=False, name=f"perm_h{g}") for g in range(1, N_PAT)]
    qkv_p, pat_o, pat_lse = [], [], []
    for g in range(N_PAT):
        (qkv_g,) = mm_nn(
            h2_p[g], w_pat[g], name=f"b_qkv{g}", tm=2048, ps=1, tn=D_MODEL, tk=D_MODEL, epilogue=ident, outs=(BF16,)
        )
        o_g, lse_g = attn_fwd(qkv_g, g, name=f"attn_fwd{g}")
        if g > 0:
            o_g = permute_rows(o_g, dil[g], inverse=True, name=f"unperm_o{g}")
            lse_g = permute_rows(lse_g, dil[g], inverse=True, name=f"unperm_lse{g}")
        qkv_p.append(qkv_g)
        pat_o.append(o_g)
        pat_lse.append(lse_g)
    o_b, o_f, lse = attn_combine(pat_o, pat_lse, name="attn_combine")
    (y2,) = mm_nn(o_b, w_bout3, name="b_out", tm=2048, ps=1, tn=D_MODEL, tk=D_MODEL, epilogue=ident, outs=(F32,))
    x3, h3 = residual_ln(x2, y2, gate1[2], lng_full[2], lnb_full[2], scale[3], shift[3], name="res_ln2")
    a3, y3 = mlp_forward(1, h3, y2)
    dxo, loss_local = residual_ln_loss(x3, y3, gate1[3], lng_full[3], lnb_full[3], target, name="res_ln3_loss")
    loss = lax.psum(loss_local[0, 0], ("x", "y", "c"))

    def scatter(parts, after, name):
        parts = [p.reshape(N_DEV, -1, p.shape[-1]) for p in parts]
        lands = [_own_slot(me, lax.dynamic_index_in_dim(p, me, 0, keepdims=False)) for p in parts]
        return Exchange(parts, lands, after, scatter=True, name=name)

    def mlp_backward(i, h, a, dy):
        (da,) = mm_nt(
            dy,
            w_dn3[i],
            name=f"mlp{i}_da",
            tm=2048,
            tko=1024,
            ps=1,
            tc=D_MODEL,
            epilogue=lambda acc, act: (acc * (2.0 * jnp.maximum(act.astype(F32), 0.0)),),
            extras=[(a, "full")],
            outs=(BF16,),
        )
        (dh,) = mm_nt(da, w_up3[i], name=f"mlp{i}_dh", tm=2048, tko=1024, ps=2, tc=512, epilogue=ident, outs=(F32,))
        dw_dn = mm_tn(
            a, dy, name=f"mlp{i}_dw_down", p=1, tk=1024, ps=1, tn=D_MODEL, tmc=2048, prologue=relu2, out_dtype=BF16
        )
        dw_up = mm_tn(h, da, name=f"mlp{i}_dw_up", p=N_DEV, tk=1024, ps=2, tn=512, tmc=2048, out_dtype=BF16)
        return scatter([dw_up, dw_dn], dh, f"scatter_mlp{i}"), dh

    dz3, dy3, st3 = residual_ln_bwd(name="res_bwd3", dxo=dxo, this=(x3, y3, gate1[3], lng_full[3]))
    rs_mlp1, dh3 = mlp_backward(1, h3, a3, dy3)
    dz2, dy2, st2 = residual_ln_bwd(
        name="res_bwd2", later=(dh3, dz3, x3, scale[3] + rs_mlp1.zero()), this=(x2, y2, gate1[2], lng_full[2])
    )
    (d_o,) = mm_nt(dy2, w_bout3, name="b_do", tm=2048, tko=1024, ps=1, tc=D_MODEL, epilogue=ident, outs=(F32,))
    do_b, delta = attn_delta(d_o, o_f, name="attn_delta")
    dh2, dw_pat = None, []
    for g in range(N_PAT):
        do_g, lse_g, delta_g = do_b, lse, delta
        if g > 0:
            do_g = permute_rows(do_b, dil[g], inverse=False, name=f"perm_do{g}")
            lse_g = permute_rows(lse, dil[g], inverse=False, name=f"perm_lse{g}")
            delta_g = permute_rows(delta, dil[g], inverse=False, name=f"perm_delta{g}")
        dqkv_g = attn_bwd(qkv_p[g], do_g, lse_g, delta_g, g, name=f"attn_bwd{g}")
        (dh_g,) = mm_nt(
            dqkv_g, w_pat[g], name=f"b_dh{g}", tm=2048, tko=1024, ps=1, tc=D_MODEL, epilogue=ident, outs=(F32,)
        )
        dh2 = dh_g if g == 0 else permute_rows(dh_g, dil[g], inverse=True, add=dh2, name=f"unperm_dh{g}")
        dw_pat.append(
            mm_tn(h2_p[g], dqkv_g, name=f"b_dw_qkv{g}", p=1, tk=1024, ps=1, tn=D_MODEL, tmc=2048, out_dtype=BF16)[0]
        )
    dw_bout = mm_tn(o_b, dy2, name="b_dw_out", p=1, tk=1024, ps=1, tn=D_MODEL, tmc=2048, out_dtype=BF16)
    dw_qkv = jnp.concatenate(dw_pat, axis=1).reshape(D_MODEL, N_DEV, -1).transpose(1, 0, 2)
    rs_b = scatter([dw_qkv, dw_bout], dh2, "scatter_b")
    dz1, dy1, st1 = residual_ln_bwd(
        name="res_bwd1", later=(dh2, dz2, x2, scale[2] + rs_b.zero()), this=(x1, y1, gate1[1], lng_full[1])
    )
    rs_mlp0, dh1 = mlp_backward(0, h1, a1, dy1)
    dz0, dy0, st0 = residual_ln_bwd(
        name="res_bwd0", later=(dh1, dz1, x1, scale[1] + rs_mlp0.zero()), this=(x0, y0, gate1[0], lng_full[0])
    )
    (dp_gate,) = mm_nt(dy0, w_aout3, name="a_dp", tm=2048, tko=1024, ps=1, tc=D_MODEL, epilogue=ident, outs=(F32,))
    da0, d_ws, d_bs, gate_rows = gate_bwd(a_pre, dp_gate, vn_g, vn_b, w_s, bias_full, name="gate_bwd")
    (dh0,) = mm_nt(da0, w_in3, name="a_dh", tm=2048, tko=1024, ps=4, tc=256, epilogue=ident, outs=(F32,))
    dw_aout = mm_tn(p_gate, dy0, name="a_dw_out", p=1, tk=1024, ps=1, tn=D_MODEL, tmc=2048, out_dtype=BF16)
    dw_in = mm_tn(h0, da0, name="a_dw_in", p=N_DEV, tk=1024, ps=4, tn=256, tmc=2048, out_dtype=BF16)
    grad_x, stf = residual_ln_bwd(name="res_bwd_in", later=(dh0, dz0, x0, scale[0]))

    stats_after = [stf, st0, st1, st2]
    stats_own = [st0, st1, st2, st3]
    dm = jnp.stack(
        [
            jnp.concatenate(
                [stats_after[t][ST_DSHIFT], stats_after[t][ST_DSCALE], stats_own[t][ST_DGATE]], axis=0
            )
            for t in range(N_SUB)
        ]
    )
    ncol = 3 * D_MODEL // N_DEV
    dmx = jnp.pad(
        dm.reshape(N_SUB, N_DEV, ncol).transpose(1, 0, 2)[:, :, None, :], ((0, 0), (0, 0), (0, SUBLANES - 1), (0, 0))
    )
    small = [
        gate_rows[0],
        gate_rows[1],
        d_ws.reshape(-1),
        d_bs[:, :A_GROUPS].T.reshape(-1),
        *[stats_own[t][ST_DG] for t in range(N_SUB)],
        *[stats_own[t][ST_DB] for t in range(N_SUB)],
    ]
    n_small = sum(s.size for s in small)
    part_rows = -(-n_small // (N_DEV * LANES * SUBLANES)) * SUBLANES
    flat = jnp.concatenate(small + [jnp.zeros((N_DEV * part_rows * LANES - n_small,), F32)])
    dmrecv, reduced = small_exchange(dmx, flat.reshape(N_DEV, part_rows, LANES))
    reduced = reduced.reshape(-1)
    sizes = [2 * D_MODEL, D_MODEL, D_MODEL, A_GROUPS * CHUNK * CHUNK, A_GROUPS * CHUNK, N_SUB * D_MODEL, N_SUB * D_MODEL]
    offs = [sum(sizes[:i]) for i in range(len(sizes) + 1)]
    g_b_in, g_vn_g, g_vn_b, g_ws, g_bs, g_lng, g_lnb = [reduced[offs[i] : offs[i + 1]] for i in range(len(sizes))]

    results = {}

    def update(wname, gparts, w, m, v):
        shape = w.shape
        w2 = w.reshape(-1, shape[-1])
        outs = adamw(gparts.reshape(gparts.shape[0], *w2.shape), w2, m.reshape(w2.shape), v.reshape(w2.shape), name=f"adamw_{wname}")
        results[wname] = [o.reshape(shape) for o in outs]

    rs_a = scatter([dw_in, dw_aout], reduced, "scatter_a")
    ada_outs = ada_grad_adamw(cg + rs_a.zero(), dmrecv, ada_w4, m_ada_w.reshape(ada_w4.shape), v_ada_w.reshape(ada_w4.shape), name="ada_grad_adamw")
    results["ada_w"] = [o.reshape(ada_w.shape) for o in ada_outs[:4]]
    update("ada_b", ada_outs[4][:, 0, :][None], ada_b, m_ada_b, v_ada_b)
    ln_cols = D_MODEL // N_DEV
    my_ln = lambda gfull: lax.dynamic_slice_in_dim(gfull.reshape(N_SUB, N_DEV, ln_cols), me, 1, axis=1)
    update("ln_g", my_ln(g_lng).reshape(1, N_SUB, ln_cols), ln_g, m_ln_g, v_ln_g)
    update("ln_b", my_ln(g_lnb).reshape(1, N_SUB, ln_cols), ln_b, m_ln_b, v_ln_b)
    update("a_b_in", g_b_in[None], a_b_in, m_a_b_in, v_a_b_in)
    update("a_vn_g", g_vn_g[None], a_vn_g, m_a_vn_g, v_a_vn_g)
    update("a_vn_b", g_vn_b[None], a_vn_b, m_a_vn_b, v_a_vn_b)
    update("a_w_s", g_ws[None], a_w_s, m_a_w_s, v_a_w_s)
    update("a_b_s", g_bs[None], a_b_s, m_a_b_s, v_a_b_s)
    g_up1, g_dn1 = rs_mlp1.wait([0, 1], grad_x, name="scatter_wait_mlp1")
    g_qkv, g_bout = rs_b.wait([0, 1], grad_x, name="scatter_wait_b")
    update("b_w_qkv", g_qkv, b_w_qkv, m_b_w_qkv, v_b_w_qkv)
    update("b_w_out", g_bout, b_w_out, m_b_w_out, v_b_w_out)
    g_up0, g_dn0 = rs_mlp0.wait([0, 1], grad_x, name="scatter_wait_mlp0")
    update("mlp_w_up", jnp.concatenate([g_up0, g_up1], axis=1), mlp_w_up, m_mlp_w_up, v_mlp_w_up)
    update("mlp_w_down", jnp.concatenate([g_dn0, g_dn1], axis=1), mlp_w_down, m_mlp_w_down, v_mlp_w_down)
    g_in, g_aout = rs_a.wait([0, 1], grad_x, name="scatter_wait_a")
    update("a_w_in", g_in, a_w_in, m_a_w_in, v_a_w_in)
    update("a_w_out", g_aout, a_w_out, m_a_w_out, v_a_w_out)

    order = ["ada_w", "ada_b", "ln_g", "ln_b", "a_w_in", "a_b_in", "a_vn_g", "a_vn_b", "a_w_s", "a_b_s", "a_w_out", "b_w_qkv", "b_w_out", "mlp_w_up", "mlp_w_down"]
    return (loss, grad_x[None], *[results[n][0] for n in order], *[results[n][1] for n in order],
            *[results[n][2] for n in order], *[results[n][3] for n in order])
```

```python
import math

import jax
import jax.numpy as jnp
from jax import lax
from jax.experimental import pallas as pl
from jax.experimental.pallas import tpu as pltpu

F32 = jnp.float32
BF16 = jnp.bfloat16
MESH = pl.DeviceIdType.MESH
ANY = pl.BlockSpec(memory_space=pl.ANY)
VMEM = pl.BlockSpec(memory_space=pltpu.VMEM)

N_DEV = 8
D_MODEL = 1024
SEQ = 4096
DEPTH = 2
CHUNK = 128
A_GROUPS = 16
A_GROUP_DIM = D_MODEL // A_GROUPS
B_HEADS = 16
B_HEAD_DIM = 64
B_PATTERNS = ((128, 1), (512, 4), (2048, 16))
N_PAT = len(B_PATTERNS)
SPAN = 128
D_FF = 4 * D_MODEL
D_QKV = N_PAT * 3 * D_MODEL
ALPHA = (2 * DEPTH) ** 0.25
LN_EPS = 1e-5
NEG = -1e30
ADAM_LR = 0.001
ADAM_B1 = 0.9
ADAM_B2 = 0.999
ADAM_EPS = 1e-08
ADAM_WD = 0.01
ADAM_STEP = 10
GELU_C = math.sqrt(2.0 / math.pi)
GELU_A = 0.044715

VMEM_LIMIT_BYTES = 56 * 1024 * 1024
LANES = 128
SUBLANES = 8
ROW_TILE = 512
N_SUB = 2 * DEPTH


def _params(sem):
    return pltpu.CompilerParams(dimension_semantics=sem, vmem_limit_bytes=VMEM_LIMIT_BYTES)


def _lane(shape):
    return lax.broadcasted_iota(jnp.int32, shape, len(shape) - 1)


def _split_bf16(x):
    hi = x.astype(BF16)
    lo = (x - hi.astype(F32)).astype(BF16)
    return hi, lo


def _group_expand_matrix(groups_padded, width):
    per = width // A_GROUPS
    r = lax.broadcasted_iota(jnp.int32, (groups_padded, width), 0)
    c = lax.broadcasted_iota(jnp.int32, (groups_padded, width), 1)
    return (c // per == r).astype(BF16)


def _group_reduce_matrix(width, groups_padded):
    per = width // A_GROUPS
    r = lax.broadcasted_iota(jnp.int32, (width, groups_padded), 0)
    c = lax.broadcasted_iota(jnp.int32, (width, groups_padded), 1)
    return (r // per == c).astype(BF16)


def _expand_groups(w):
    e = _group_expand_matrix(LANES, D_MODEL)
    hi, lo = _split_bf16(w)
    return jnp.dot(hi, e, preferred_element_type=F32) + jnp.dot(lo, e, preferred_element_type=F32)


def _reduce_groups(x):
    e = _group_reduce_matrix(D_MODEL, LANES)
    hi, lo = _split_bf16(x)
    return jnp.dot(hi, e, preferred_element_type=F32) + jnp.dot(lo, e, preferred_element_type=F32)


def _column_tiles(p, n, ps, tn):
    assert (ps == 1 or tn == n) and p % ps == 0 and n % tn == 0
    q = n // tn
    return (p // ps) * q, q


def _extra_specs(extras, tm, width):
    specs = []
    for _, kind in extras:
        if kind == "row":
            specs.append(pl.BlockSpec((1, width), lambda i, j, c: (0, j)))
        else:
            specs.append(pl.BlockSpec((tm, width), lambda i, j, c: (i, j)))
    return specs


def mm_nn(a, b3, *, name, tm, ps, tn, tk, epilogue, extras=(), outs, prologue=None):
    m, k = a.shape
    p, _, n = b3.shape
    nj, q = _column_tiles(p, n, ps, tn)
    nk = k // tk
    width = ps * tn

    def body(a_ref, b_ref, *rest):
        ex = rest[: len(extras)]
        out_refs = rest[len(extras) : len(extras) + len(outs)]
        kk = pl.program_id(2)
        av = a_ref[...] if prologue is None else prologue(a_ref[...])

        def finish(cs, acc):
            res = epilogue(acc, *[e[:, cs] for e in ex])
            for o_ref, r in zip(out_refs, res, strict=True):
                o_ref[:, cs] = r.astype(o_ref.dtype)

        for s in range(ps):
            cs = slice(s * tn, (s + 1) * tn)
            part = jnp.dot(av, b_ref[s], preferred_element_type=F32)
            if nk == 1:
                finish(cs, part)
                continue
            acc_ref = rest[-1]

            @pl.when(kk == 0)
            def _(part=part, cs=cs):
                acc_ref[:, cs] = part

            @pl.when(kk > 0)
            def _(part=part, cs=cs):
                acc_ref[:, cs] += part

        if nk > 1:

            @pl.when(kk == nk - 1)
            def _():
                for s in range(ps):
                    cs = slice(s * tn, (s + 1) * tn)
                    finish(cs, rest[-1][:, cs])

    return pl.pallas_call(
        body,
        name=name,
        grid=(m // tm, nj, nk),
        in_specs=[
            pl.BlockSpec((tm, tk), lambda i, j, kk: (i, kk)),
            pl.BlockSpec((ps, tk, tn), lambda i, j, kk: (j // q, kk, j % q)),
            *_extra_specs(extras, tm, width),
        ],
        out_specs=[pl.BlockSpec((tm, width), lambda i, j, kk: (i, j)) for _ in outs],
        out_shape=[jax.ShapeDtypeStruct((m, p * n), dt) for dt in outs],
        scratch_shapes=[pltpu.VMEM((tm, width), F32)] if nk > 1 else [],
        compiler_params=_params(("parallel", "parallel", "arbitrary")),
    )(a, b3, *[arr for arr, _ in extras])


def mm_nt(g, b3, *, name, tm, tko, ps, tc, epilogue, extras=(), outs):
    m = g.shape[0]
    p, k, n = b3.shape
    nc, q = _column_tiles(p, n, ps, tc)

    def body(g_ref, b_ref, *rest):
        ex = rest[: len(extras)]
        out_refs = rest[len(extras) : len(extras) + len(outs)]
        c = pl.program_id(2)
        part = None
        for s in range(ps):
            d = lax.dot_general(
                g_ref[:, s * tc : (s + 1) * tc], b_ref[s], (((1,), (1,)), ((), ())), preferred_element_type=F32
            )
            part = d if part is None else part + d

        def finish(acc):
            res = epilogue(acc, *[e[...] for e in ex])
            for o_ref, r in zip(out_refs, res, strict=True):
                o_ref[...] = r.astype(o_ref.dtype)

        if nc == 1:
            finish(part)
            return
        acc_ref = rest[-1]

        @pl.when(c == 0)
        def _():
            acc_ref[...] = part

        @pl.when(c > 0)
        def _():
            acc_ref[...] += part

        @pl.when(c == nc - 1)
        def _():
            finish(acc_ref[...])

    return pl.pallas_call(
        body,
        name=name,
        grid=(m // tm, k // tko, nc),
        in_specs=[
            pl.BlockSpec((tm, ps * tc), lambda i, j, c: (i, c)),
            pl.BlockSpec((ps, tko, tc), lambda i, j, c: (c // q, j, c % q)),
            *_extra_specs(extras, tm, tko),
        ],
        out_specs=[pl.BlockSpec((tm, tko), lambda i, j, c: (i, j)) for _ in outs],
        out_shape=[jax.ShapeDtypeStruct((m, k), dt) for dt in outs],
        scratch_shapes=[pltpu.VMEM((tm, tko), F32)] if nc > 1 else [],
        compiler_params=_params(("parallel", "parallel", "arbitrary")),
    )(g, b3, *[arr for arr, _ in extras])


def mm_tn(a, g, *, name, p, tk, ps, tn, tmc, out_dtype, prologue=None):
    m, k = a.shape
    n = g.shape[1] // p
    nj, q = _column_tiles(p, n, ps, tn)
    nc = m // tmc

    def body(a_ref, g_ref, o_ref, acc_ref):
        c = pl.program_id(2)
        av = a_ref[...] if prologue is None else prologue(a_ref[...])
        for s in range(ps):
            part = lax.dot_general(
                av, g_ref[:, s * tn : (s + 1) * tn], (((0,), (0,)), ((), ())), preferred_element_type=F32
            )

            @pl.when(c == 0)
            def _(part=part, s=s):
                acc_ref[s] = part

            @pl.when(c > 0)
            def _(part=part, s=s):
                acc_ref[s] += part

        @pl.when(c == nc - 1)
        def _():
            o_ref[...] = acc_ref[...].astype(o_ref.dtype)

    return pl.pallas_call(
        body,
        name=name,
        grid=(k // tk, nj, nc),
        in_specs=[
            pl.BlockSpec((tmc, tk), lambda i, j, c: (c, i)),
            pl.BlockSpec((tmc, ps * tn), lambda i, j, c: (c, j)),
        ],
        out_specs=pl.BlockSpec((ps, tk, tn), lambda i, j, c: (j // q, i, j % q)),
        out_shape=jax.ShapeDtypeStruct((p, k, n), out_dtype),
        scratch_shapes=[pltpu.VMEM((ps, tk, tn), F32)],
        compiler_params=_params(("parallel", "parallel", "arbitrary")),
    )(a, g)


def _rows(cols):
    return pl.BlockSpec((ROW_TILE, cols), lambda i: (i, 0))


def _vec(cols, rows=1):
    return pl.BlockSpec((rows, cols), lambda i: (0, 0))


def _layer_norm_hat(z):
    mu = jnp.mean(z, axis=-1, keepdims=True)
    zc = z - mu
    var = jnp.mean(zc * zc, axis=-1, keepdims=True)
    rstd = lax.rsqrt(var + LN_EPS)
    return zc * rstd, rstd


def modulate(x, scale, shift, *, name):
    s, d = x.shape

    def body(x_ref, sc_ref, sh_ref, h_ref):
        h_ref[...] = (x_ref[...] * (1.0 + sc_ref[...]) + sh_ref[...]).astype(BF16)

    return pl.pallas_call(
        body,
        name=name,
        grid=(s // ROW_TILE,),
        in_specs=[_rows(d), _vec(d), _vec(d)],
        out_specs=_rows(d),
        out_shape=jax.ShapeDtypeStruct((s, d), BF16),
        compiler_params=_params(("parallel",)),
    )(x, scale, shift)


def residual_ln(x, y, gate1, g, b, nscale, nshift, *, name):
    s, d = x.shape

    def body(x_ref, y_ref, gt_ref, g_ref, b_ref, sc_ref, sh_ref, xn_ref, hn_ref):
        z = ALPHA * x_ref[...] + gt_ref[...] * y_ref[...]
        xhat, _ = _layer_norm_hat(z)
        xn = xhat * g_ref[...] + b_ref[...]
        xn_ref[...] = xn
        hn_ref[...] = (xn * (1.0 + sc_ref[...]) + sh_ref[...]).astype(BF16)

    return pl.pallas_call(
        body,
        name=name,
        grid=(s // ROW_TILE,),
        in_specs=[_rows(d), _rows(d), _vec(d), _vec(d), _vec(d), _vec(d), _vec(d)],
        out_specs=[_rows(d), _rows(d)],
        out_shape=[jax.ShapeDtypeStruct((s, d), F32), jax.ShapeDtypeStruct((s, d), BF16)],
        compiler_params=_params(("parallel",)),
    )(x, y, gate1, g, b, nscale, nshift)


def residual_ln_loss(x, y, gate1, g, b, target, *, name):
    s, d = x.shape

    def body(x_ref, y_ref, gt_ref, g_ref, b_ref, t_ref, dx_ref, loss_ref):
        z = ALPHA * x_ref[...] + gt_ref[...] * y_ref[...]
        xhat, _ = _layer_norm_hat(z)
        err = xhat * g_ref[...] + b_ref[...] - t_ref[...]
        dx_ref[...] = err * (1.0 / d)
        part = jnp.sum(jnp.sum(err * err, axis=1, keepdims=True), axis=0, keepdims=True) * (0.5 / d)

        @pl.when(pl.program_id(0) == 0)
        def _():
            loss_ref[...] = part

        @pl.when(pl.program_id(0) > 0)
        def _():
            loss_ref[...] += part

    return pl.pallas_call(
        body,
        name=name,
        grid=(s // ROW_TILE,),
        in_specs=[_rows(d), _rows(d), _vec(d), _vec(d), _vec(d), _rows(d)],
        out_specs=[_rows(d), pl.BlockSpec((1, 1), lambda i: (0, 0))],
        out_shape=[jax.ShapeDtypeStruct((s, d), F32), jax.ShapeDtypeStruct((1, 1), F32)],
        compiler_params=_params(("arbitrary",)),
    )(x, y, gate1, g, b, target)


ST_DSCALE, ST_DSHIFT, ST_DG, ST_DB, ST_DGATE = 0, 1, 2, 3, 4


def residual_ln_bwd(*, name, later=None, dxo=None, this=None):
    lead = later[0] if later is not None else dxo
    s, d = lead.shape
    has_later, has_ln = later is not None, this is not None

    def body(*refs):
        refs = list(refs)
        if has_later:
            dh_ref, dzl_ref, xm_ref, scl_ref = refs[:4]
            refs = refs[4:]
        else:
            dxo_ref = refs.pop(0)
        if has_ln:
            x_ref, y_ref, gt_ref, g_ref = refs[:4]
            refs = refs[4:]
            dz_ref, dy_ref, st_ref = refs
        else:
            dx_ref, st_ref = refs

        @pl.when(pl.program_id(0) == 0)
        def _():
            st_ref[...] = jnp.zeros_like(st_ref)

        def acc(row, val):
            st_ref[row : row + 1, :] += jnp.sum(val, axis=0, keepdims=True)

        if has_later:
            dh = dh_ref[...]
            g_out = ALPHA * dzl_ref[...] + dh * (1.0 + scl_ref[...])
            acc(ST_DSCALE, dh * xm_ref[...])
            acc(ST_DSHIFT, dh)
        else:
            g_out = dxo_ref[...]
        if not has_ln:
            dx_ref[...] = g_out
            return
        y = y_ref[...]
        gate1 = gt_ref[...]
        z = ALPHA * x_ref[...] + gate1 * y
        xhat, rstd = _layer_norm_hat(z)
        acc(ST_DG, g_out * xhat)
        acc(ST_DB, g_out)
        dxh = g_out * g_ref[...]
        m1 = jnp.mean(dxh, axis=-1, keepdims=True)
        m2 = jnp.mean(dxh * xhat, axis=-1, keepdims=True)
        dz = rstd * (dxh - m1 - xhat * m2)
        acc(ST_DGATE, dz * y)
        dz_ref[...] = dz
        dy_ref[...] = (dz * gate1).astype(BF16)

    ins, specs = [], []
    if has_later:
        ins += list(later)
        specs += [_rows(d), _rows(d), _rows(d), _vec(d)]
    else:
        ins += [dxo]
        specs += [_rows(d)]
    if has_ln:
        ins += list(this)
        specs += [_rows(d), _rows(d), _vec(d), _vec(d)]
        out_specs = [_rows(d), _rows(d), _vec(d, SUBLANES)]
        out_shape = [
            jax.ShapeDtypeStruct((s, d), F32),
            jax.ShapeDtypeStruct((s, d), BF16),
            jax.ShapeDtypeStruct((SUBLANES, d), F32),
        ]
    else:
        out_specs = [_rows(d), _vec(d, SUBLANES)]
        out_shape = [jax.ShapeDtypeStruct((s, d), F32), jax.ShapeDtypeStruct((SUBLANES, d), F32)]
    return pl.pallas_call(
        body,
        name=name,
        grid=(s // ROW_TILE,),
        in_specs=specs,
        out_specs=out_specs,
        out_shape=out_shape,
        compiler_params=_params(("arbitrary",)),
    )(*ins)


GATE_CHUNKS = 4


def _gelu(x):
    return 0.5 * x * (1.0 + jnp.tanh(GELU_C * (x + GELU_A * x * x * x)))


def _gelu_grad(x):
    t = jnp.tanh(GELU_C * (x + GELU_A * x * x * x))
    return 0.5 * (1.0 + t) + 0.5 * x * (1.0 - t * t) * (GELU_C * (1.0 + 3.0 * GELU_A * x * x))


def _causal_weights(w_ref, transpose):
    t = lax.broadcasted_iota(jnp.int32, (CHUNK, CHUNK), 0)
    s = lax.broadcasted_iota(jnp.int32, (CHUNK, CHUNK), 1)
    out = []
    for g in range(A_GROUPS):
        w = jnp.where(t >= s, w_ref[g], 0.0)
        out.append((w.T if transpose else w).astype(BF16))
    return out


def _spatial(ws, vn, lo_mask):
    rows = vn.shape[0]
    out_rows = []
    for r in range(rows // CHUNK):
        cols = []
        for j in range(A_GROUPS // 2):
            blk = vn[r * CHUNK : (r + 1) * CHUNK, j * LANES : (j + 1) * LANES]
            za = jnp.dot(ws[2 * j], blk, preferred_element_type=F32)
            zb = jnp.dot(ws[2 * j + 1], blk, preferred_element_type=F32)
            cols.append(jnp.where(lo_mask, za, zb))
        out_rows.append(jnp.concatenate(cols, axis=1))
    return jnp.concatenate(out_rows, axis=0)


def _gate_forward(a, vg, vb, ws, bias, lo_mask):
    u = _gelu(a[:, :D_MODEL])
    v = _gelu(a[:, D_MODEL:])
    vhat, rstd = _layer_norm_hat(v)
    vn = (vhat * vg + vb).astype(BF16)
    z = _spatial(ws, vn, lo_mask) + jnp.concatenate([bias] * (a.shape[0] // CHUNK), axis=0)
    return u, vhat, rstd, vn, z


def gate_fwd(a_pre, vn_g, vn_b, w_s, bias_full, *, name):
    s = a_pre.shape[0]
    tr = GATE_CHUNKS * CHUNK

    def body(a_ref, vg_ref, vb_ref, w_ref, bias_ref, p_ref):
        lo_mask = _lane((CHUNK, LANES)) < A_GROUP_DIM
        ws = _causal_weights(w_ref, transpose=False)
        u, _, _, _, z = _gate_forward(a_ref[...].astype(F32), vg_ref[...], vb_ref[...], ws, bias_ref[...], lo_mask)
        p_ref[...] = (u * z).astype(BF16)

    return pl.pallas_call(
        body,
        name=name,
        grid=(s // tr,),
        in_specs=[
            pl.BlockSpec((tr, 2 * D_MODEL), lambda i: (i, 0)),
            _vec(D_MODEL),
            _vec(D_MODEL),
            pl.BlockSpec((A_GROUPS, CHUNK, CHUNK), lambda i: (0, 0, 0)),
            _vec(D_MODEL, CHUNK),
        ],
        out_specs=pl.BlockSpec((tr, D_MODEL), lambda i: (i, 0)),
        out_shape=jax.ShapeDtypeStruct((s, D_MODEL), BF16),
        compiler_params=_params(("parallel",)),
    )(a_pre, vn_g, vn_b, w_s, bias_full)


def gate_bwd(a_pre, dp, vn_g, vn_b, w_s, bias_full, *, name):
    s = a_pre.shape[0]
    tr = GATE_CHUNKS * CHUNK
    nsteps = s // tr

    def body(a_ref, dp_ref, vg_ref, vb_ref, w_ref, bias_ref, da_ref, dw_ref, dbs_ref, rows_ref, dbias_acc):
        step = pl.program_id(0)
        lo_mask = _lane((CHUNK, LANES)) < A_GROUP_DIM

        @pl.when(step == 0)
        def _():
            dw_ref[...] = jnp.zeros_like(dw_ref)
            rows_ref[...] = jnp.zeros_like(rows_ref)
            dbias_acc[...] = jnp.zeros_like(dbias_acc)

        a = a_ref[...].astype(F32)
        vg = vg_ref[...]
        ws = _causal_weights(w_ref, transpose=False)
        wts = _causal_weights(w_ref, transpose=True)
        u, vhat, rstd, vn, z = _gate_forward(a, vg, vb_ref[...], ws, bias_ref[...], lo_mask)
        dp = dp_ref[...]
        du = dp * z
        dzz = dp * u
        dzz_b = dzz.astype(BF16)
        dvn = _spatial(wts, dzz_b, lo_mask)
        dbias = None
        for r in range(GATE_CHUNKS):
            rs = slice(r * CHUNK, (r + 1) * CHUNK)
            dbias = dzz[rs] if dbias is None else dbias + dzz[rs]
            for j in range(A_GROUPS // 2):
                cs = slice(j * LANES, (j + 1) * LANES)
                dblk = dzz[rs, cs]
                vblk = vn[rs, cs]
                for half in range(2):
                    keep = lo_mask if half == 0 else jnp.logical_not(lo_mask)
                    dm = jnp.where(keep, dblk, 0.0).astype(BF16)
                    dw_ref[2 * j + half] += lax.dot_general(
                        dm, vblk, (((1,), (1,)), ((), ())), preferred_element_type=F32
                    )
        dbias_acc[...] += dbias
        rows_ref[1:2, :D_MODEL] += jnp.sum(dvn * vhat, axis=0, keepdims=True)
        rows_ref[1:2, D_MODEL:] += jnp.sum(dvn, axis=0, keepdims=True)
        dvh = dvn * vg
        m1 = jnp.mean(dvh, axis=-1, keepdims=True)
        m2 = jnp.mean(dvh * vhat, axis=-1, keepdims=True)
        dv = rstd * (dvh - m1 - vhat * m2)
        da_u = du * _gelu_grad(a[:, :D_MODEL])
        da_v = dv * _gelu_grad(a[:, D_MODEL:])
        da_ref[:, :D_MODEL] = da_u.astype(BF16)
        da_ref[:, D_MODEL:] = da_v.astype(BF16)
        rows_ref[0:1, :D_MODEL] += jnp.sum(da_u, axis=0, keepdims=True)
        rows_ref[0:1, D_MODEL:] += jnp.sum(da_v, axis=0, keepdims=True)

        @pl.when(step == nsteps - 1)
        def _():
            t = lax.broadcasted_iota(jnp.int32, (CHUNK, CHUNK), 0)
            sx = lax.broadcasted_iota(jnp.int32, (CHUNK, CHUNK), 1)
            for g in range(A_GROUPS):
                dw_ref[g] = jnp.where(t >= sx, dw_ref[g], 0.0)
            dbs_ref[...] = _reduce_groups(dbias_acc[...])

    return pl.pallas_call(
        body,
        name=name,
        grid=(nsteps,),
        in_specs=[
            pl.BlockSpec((tr, 2 * D_MODEL), lambda i: (i, 0)),
            pl.BlockSpec((tr, D_MODEL), lambda i: (i, 0)),
            _vec(D_MODEL),
            _vec(D_MODEL),
            pl.BlockSpec((A_GROUPS, CHUNK, CHUNK), lambda i: (0, 0, 0)),
            _vec(D_MODEL, CHUNK),
        ],
        out_specs=[
            pl.BlockSpec((tr, 2 * D_MODEL), lambda i: (i, 0)),
            pl.BlockSpec((A_GROUPS, CHUNK, CHUNK), lambda i: (0, 0, 0)),
            _vec(LANES, CHUNK),
            _vec(2 * D_MODEL, SUBLANES),
        ],
        out_shape=[
            jax.ShapeDtypeStruct((s, 2 * D_MODEL), BF16),
            jax.ShapeDtypeStruct((A_GROUPS, CHUNK, CHUNK), F32),
            jax.ShapeDtypeStruct((CHUNK, LANES), F32),
            jax.ShapeDtypeStruct((SUBLANES, 2 * D_MODEL), F32),
        ],
        scratch_shapes=[pltpu.VMEM((CHUNK, D_MODEL), F32)],
        compiler_params=_params(("arbitrary",)),
    )(a_pre, dp, vn_g, vn_b, w_s, bias_full)


def alibi_tables(dilation):
    qi = jnp.arange(SPAN)[:, None]
    ki = jnp.arange(2 * SPAN)[None, :]
    diff = SPAN + qi - ki
    valid = (diff >= 0) & (diff <= SPAN)
    heads = jnp.arange(1, B_HEADS + 1, dtype=F32)
    slopes = jnp.exp2(-8.0 * heads / B_HEADS)
    bias = -slopes[:, None, None] * (dilation * diff).astype(F32)
    return jnp.where(valid[None], bias, NEG)


def permute_rows(x, dilation, *, inverse, name, add=None):
    s, w = x.shape
    tile = SPAN * dilation
    nat = pl.BlockSpec((tile, w), lambda i: (i, 0))
    streams = pl.BlockSpec((dilation, SPAN, w), lambda i: (0, i, 0))
    x3 = x.reshape(dilation, s // dilation, w) if inverse else x

    def body(*refs):
        if not inverse:
            x_ref, o_ref = refs
            o_ref[...] = jnp.swapaxes(x_ref[...].reshape(SPAN, dilation, w), 0, 1)
            return
        val = jnp.swapaxes(refs[0][...], 0, 1).reshape(tile, w)
        if add is not None:
            val = val + refs[1][...]
        refs[-1][...] = val

    out = pl.pallas_call(
        body,
        name=name,
        grid=(s // tile,),
        in_specs=([streams] + ([nat] if add is not None else [])) if inverse else [nat],
        out_specs=nat if inverse else streams,
        out_shape=jax.ShapeDtypeStruct((s, w) if inverse else (dilation, s // dilation, w), x.dtype),
        compiler_params=_params(("parallel",)),
    )(*([x3] + ([add] if add is not None else [])))
    return out.reshape(s, w)


def _qkv_specs(nb, clamp):
    def spec(which, prev):
        def index(r, n):
            blk = jnp.maximum(n - 1, 0) if prev else n
            return (r * nb + clamp(blk), which)

        return pl.BlockSpec((SPAN, D_MODEL), index)

    return [spec(0, False), spec(1, True), spec(1, False), spec(2, True), spec(2, False)]


def attn_fwd(qkv_p, pat, *, name):
    _, dilation = B_PATTERNS[pat]
    nb = SEQ // dilation // SPAN
    bias = alibi_tables(dilation)

    def body(q_ref, kp_ref, kc_ref, vp_ref, vc_ref, bias_ref, o_ref, lse_ref):
        n = pl.program_id(1)
        lane = _lane((SPAN, LANES))
        lo_mask = lane < B_HEAD_DIM
        first_prev = jnp.logical_and(n == 0, _lane((SPAN, 2 * SPAN)) < SPAN)
        q = q_ref[...] * jnp.asarray(B_HEAD_DIM**-0.5, BF16)
        kk = jnp.concatenate([kp_ref[...], kc_ref[...]], axis=0)
        vv = jnp.concatenate([vp_ref[...], vc_ref[...]], axis=0)
        halves = (lo_mask.astype(BF16), jnp.logical_not(lo_mask).astype(BF16))
        stats = jnp.zeros((SPAN, LANES), F32)
        for j in range(B_HEADS // 2):
            cs = slice(j * LANES, (j + 1) * LANES)
            qp, kp, vp = q[:, cs], kk[:, cs], vv[:, cs]
            outs = []
            for half in range(2):
                h = 2 * j + half
                sc = lax.dot_general(qp * halves[half], kp, (((1,), (1,)), ((), ())), preferred_element_type=F32)
                sc = jnp.where(first_prev, NEG, sc + bias_ref[h])
                m = jnp.max(sc, axis=1, keepdims=True)
                p = jnp.exp(sc - m)
                l = jnp.sum(p, axis=1, keepdims=True)
                acc = jnp.dot(p.astype(BF16), vp, preferred_element_type=F32)
                outs.append(acc * (1.0 / l))
                stats = jnp.where(lane == h, m + jnp.log(l), stats)
            o_ref[:, cs] = jnp.where(lo_mask, outs[0], outs[1])
        lse_ref[...] = stats

    return pl.pallas_call(
        body,
        name=name,
        grid=(dilation, nb),
        in_specs=[
            *_qkv_specs(nb, lambda blk: blk),
            pl.BlockSpec((B_HEADS, SPAN, 2 * SPAN), lambda r, n: (0, 0, 0)),
        ],
        out_specs=[
            pl.BlockSpec((SPAN, D_MODEL), lambda r, n: (r * nb + n, 0)),
            pl.BlockSpec((SPAN, LANES), lambda r, n: (r * nb + n, 0)),
        ],
        out_shape=[jax.ShapeDtypeStruct((SEQ, D_MODEL), F32), jax.ShapeDtypeStruct((SEQ, LANES), F32)],
        compiler_params=_params(("parallel", "arbitrary")),
    )(qkv_p, qkv_p, qkv_p, qkv_p, qkv_p, bias)


def attn_combine(outs, lses, *, name):
    def body(o0, o1, o2, l0, l1, l2, ob_ref, of_ref, lse_ref):
        ls = [l0[...], l1[...], l2[...]]
        m = jnp.maximum(jnp.maximum(ls[0], ls[1]), ls[2])
        tot = jnp.log(jnp.exp(ls[0] - m) + jnp.exp(ls[1] - m) + jnp.exp(ls[2] - m)) + m
        o = None
        for o_ref, l in zip((o0, o1, o2), ls, strict=True):
            term = _expand_groups(jnp.exp(l - tot)) * o_ref[...]
            o = term if o is None else o + term
        ob_ref[...] = o.astype(BF16)
        of_ref[...] = o
        lse_ref[...] = tot

    return pl.pallas_call(
        body,
        name=name,
        grid=(SEQ // ROW_TILE,),
        in_specs=[_rows(D_MODEL)] * 3 + [_rows(LANES)] * 3,
        out_specs=[_rows(D_MODEL), _rows(D_MODEL), _rows(LANES)],
        out_shape=[
            jax.ShapeDtypeStruct((SEQ, D_MODEL), BF16),
            jax.ShapeDtypeStruct((SEQ, D_MODEL), F32),
            jax.ShapeDtypeStruct((SEQ, LANES), F32),
        ],
        compiler_params=_params(("parallel",)),
    )(*outs, *lses)


def attn_delta(do, o, *, name):
    def body(do_ref, o_ref, dob_ref, dl_ref):
        do_v = do_ref[...]
        dob_ref[...] = do_v.astype(BF16)
        dl_ref[...] = _reduce_groups(do_v * o_ref[...])

    return pl.pallas_call(
        body,
        name=name,
        grid=(SEQ // ROW_TILE,),
        in_specs=[_rows(D_MODEL), _rows(D_MODEL)],
        out_specs=[_rows(D_MODEL), _rows(LANES)],
        out_shape=[jax.ShapeDtypeStruct((SEQ, D_MODEL), BF16), jax.ShapeDtypeStruct((SEQ, LANES), F32)],
        compiler_params=_params(("parallel",)),
    )(do, o)


def attn_bwd(qkv_p, do_p, lse_p, delta_p, pat, *, name):
    _, dilation = B_PATTERNS[pat]
    nb = SEQ // dilation // SPAN
    bias = alibi_tables(dilation)
    last = nb - 1
    q_cols, k_cols, v_cols = (slice(i * D_MODEL, (i + 1) * D_MODEL) for i in range(3))

    def body(q_ref, kp_ref, kc_ref, vp_ref, vc_ref, do_ref, lse_ref, dl_ref, bias_ref, out_ref, cq_ref, ck_ref, cv_ref):
        n = pl.program_id(1)

        @pl.when(n == nb)
        def _():
            out_ref[:, q_cols] = cq_ref[...].astype(BF16)
            out_ref[:, k_cols] = ck_ref[...].astype(BF16)
            out_ref[:, v_cols] = cv_ref[...].astype(BF16)

        @pl.when(n == 0)
        def _():
            cq_ref[...] = jnp.zeros_like(cq_ref)
            ck_ref[...] = jnp.zeros_like(ck_ref)
            cv_ref[...] = jnp.zeros_like(cv_ref)

        @pl.when(n < nb)
        def _():
            lane = _lane((SPAN, LANES))
            lo_mask = lane < B_HEAD_DIM
            first_prev = jnp.logical_and(n == 0, _lane((SPAN, 2 * SPAN)) < SPAN)
            q = q_ref[...] * jnp.asarray(B_HEAD_DIM**-0.5, BF16)
            kk = jnp.concatenate([kp_ref[...], kc_ref[...]], axis=0)
            vv = jnp.concatenate([vp_ref[...], vc_ref[...]], axis=0)
            do_v = do_ref[...]
            lse_v = lse_ref[...]
            dl_v = dl_ref[...]
            halves = (lo_mask.astype(BF16), jnp.logical_not(lo_mask).astype(BF16))
            for j in range(B_HEADS // 2):
                cs = slice(j * LANES, (j + 1) * LANES)
                qp, kp, vp, dop = q[:, cs], kk[:, cs], vv[:, cs], do_v[:, cs]
                dqs = []
                dk_pair = None
                dv_pair = None
                for half in range(2):
                    h = 2 * j + half
                    qm = qp * halves[half]
                    dom = dop * halves[half]
                    lse_h = jnp.sum(jnp.where(lane == h, lse_v, 0.0), axis=1, keepdims=True)
                    dl_h = jnp.sum(jnp.where(lane == h, dl_v, 0.0), axis=1, keepdims=True)
                    sc = lax.dot_general(qm, kp, (((1,), (1,)), ((), ())), preferred_element_type=F32)
                    sc = jnp.where(first_prev, NEG, sc + bias_ref[h])
                    p = jnp.exp(sc - lse_h)
                    dp = lax.dot_general(dom, vp, (((1,), (1,)), ((), ())), preferred_element_type=F32)
                    ds = (p * (dp - dl_h)).astype(BF16)
                    dqs.append(jnp.dot(ds, kp, preferred_element_type=F32))
                    dk_h = lax.dot_general(ds, qm, (((0,), (0,)), ((), ())), preferred_element_type=F32)
                    dv_h = lax.dot_general(
                        p.astype(BF16), dom, (((0,), (0,)), ((), ())), preferred_element_type=F32
                    )
                    dk_pair = dk_h if dk_pair is None else dk_pair + dk_h
                    dv_pair = dv_h if dv_pair is None else dv_pair + dv_h
                oq = slice(j * LANES, (j + 1) * LANES)
                ok = slice(D_MODEL + j * LANES, D_MODEL + (j + 1) * LANES)
                ov = slice(2 * D_MODEL + j * LANES, 2 * D_MODEL + (j + 1) * LANES)
                out_ref[:, oq] = cq_ref[:, cs].astype(BF16)
                out_ref[:, ok] = (ck_ref[:, cs] + dk_pair[:SPAN]).astype(BF16)
                out_ref[:, ov] = (cv_ref[:, cs] + dv_pair[:SPAN]).astype(BF16)
                cq_ref[:, cs] = jnp.where(lo_mask, dqs[0], dqs[1]) * (B_HEAD_DIM**-0.5)
                ck_ref[:, cs] = dk_pair[SPAN:]
                cv_ref[:, cs] = dv_pair[SPAN:]

    def clamp(n):
        return jnp.minimum(n, last)

    def row_spec(width):
        return pl.BlockSpec((SPAN, width), lambda r, n: (r * nb + clamp(n), 0))

    return pl.pallas_call(
        body,
        name=name,
        grid=(dilation, nb + 1),
        in_specs=[
            *_qkv_specs(nb, clamp),
            row_spec(D_MODEL),
            row_spec(LANES),
            row_spec(LANES),
            pl.BlockSpec((B_HEADS, SPAN, 2 * SPAN), lambda r, n: (0, 0, 0)),
        ],
        out_specs=pl.BlockSpec((SPAN, 3 * D_MODEL), lambda r, n: (r * nb + jnp.maximum(n - 1, 0), 0)),
        out_shape=jax.ShapeDtypeStruct((SEQ, 3 * D_MODEL), BF16),
        scratch_shapes=[pltpu.VMEM((SPAN, D_MODEL), F32)] * 3,
        compiler_params=_params(("arbitrary", "arbitrary")),
    )(qkv_p, qkv_p, qkv_p, qkv_p, qkv_p, do_p, lse_p, delta_p, bias)


def _position():
    x, y, c = lax.axis_index("x"), lax.axis_index("y"), lax.axis_index("c")
    return x, y, c, 4 * x + 2 * y + c


def _peer(k, x, y, c):
    px = 1 - x if k & 4 else x
    py = 1 - y if k & 2 else y
    pc = 1 - c if k & 1 else c
    return (px, py, pc), 4 * px + 2 * py + pc


def _remote(src, dst, send_sem, recv_sem, device):
    return pltpu.make_async_remote_copy(
        src_ref=src, dst_ref=dst, send_sem=send_sem, recv_sem=recv_sem, device_id=device, device_id_type=MESH
    )


def _silu_bf16(cf):
    return (cf * (1.0 / (1.0 + jnp.exp(-cf)))).astype(BF16)


def ada_exchange(c8, w4, b4, ln8):
    nt, _, ncol = w4.shape

    def body(c8_ref, w_ref, b_ref, ln_ref, cg_ref, lng_ref, mrecv_ref, mloc_ref, send_sems, recv_sems):
        x, y, c, me = _position()
        cg_ref[me] = c8_ref[...]
        lng_ref[me] = ln_ref[...]
        first = []
        for k in range(1, N_DEV):
            dev, _ = _peer(k, x, y, c)
            first.append(_remote(c8_ref, cg_ref.at[me], send_sems.at[0, k], recv_sems.at[0, k], dev))
            first.append(_remote(ln_ref, lng_ref.at[me], send_sems.at[1, k], recv_sems.at[1, k], dev))
        for cp in first:
            cp.start()
        for k in range(1, N_DEV):
            dev, pid = _peer(k, x, y, c)
            _remote(c8_ref, cg_ref.at[pid], send_sems.at[0, k], recv_sems.at[0, k], dev).wait_recv()
            _remote(ln_ref, lng_ref.at[pid], send_sems.at[1, k], recv_sems.at[1, k], dev).wait_recv()
        sc = _silu_bf16(cg_ref[...].reshape(N_DEV * SUBLANES, D_MODEL))
        for t in range(nt):
            mloc_ref[t] = jnp.dot(sc, w_ref[t].astype(BF16), preferred_element_type=F32) + b_ref[t : t + 1, :]

        def group(dev_id):
            return pl.ds(pl.multiple_of(dev_id * SUBLANES, SUBLANES), SUBLANES)

        mrecv_ref[me] = mloc_ref[:, group(me), :]
        second = []
        for k in range(1, N_DEV):
            dev, pid = _peer(k, x, y, c)
            second.append(
                _remote(mloc_ref.at[:, group(pid), :], mrecv_ref.at[me], send_sems.at[2, k], recv_sems.at[2, k], dev)
            )
        for cp in second:
            cp.start()
        for k in range(1, N_DEV):
            dev, pid = _peer(k, x, y, c)
            _remote(
                mloc_ref.at[:, group(pid), :], mrecv_ref.at[pid], send_sems.at[2, k], recv_sems.at[2, k], dev
            ).wait_recv()
        for cp in first + second:
            cp.wait_send()

    return pl.pallas_call(
        body,
        name="ada_exchange",
        in_specs=[VMEM, VMEM, VMEM, VMEM],
        out_specs=[VMEM, VMEM, VMEM],
        out_shape=[
            jax.ShapeDtypeStruct((N_DEV, SUBLANES, D_MODEL), F32),
            jax.ShapeDtypeStruct((N_DEV, SUBLANES, LANES), F32),
            jax.ShapeDtypeStruct((N_DEV, nt, SUBLANES, ncol), F32),
        ],
        scratch_shapes=[
            pltpu.VMEM((nt, N_DEV * SUBLANES, ncol), F32),
            pltpu.SemaphoreType.DMA((3, N_DEV)),
            pltpu.SemaphoreType.DMA((3, N_DEV)),
        ],
        compiler_params=pltpu.CompilerParams(vmem_limit_bytes=VMEM_LIMIT_BYTES),
    )(c8, w4, b4, ln8)


def small_exchange(dmx, flat):
    def body(dmx_ref, flat_ref, dmrecv_ref, red_ref, land_ref, send_sems, recv_sems):
        x, y, c, me = _position()
        dmrecv_ref[me] = dmx_ref[me]
        land_ref[me] = flat_ref[me]
        first = []
        for k in range(1, N_DEV):
            dev, pid = _peer(k, x, y, c)
            first.append(_remote(dmx_ref.at[pid], dmrecv_ref.at[me], send_sems.at[0, k], recv_sems.at[0, k], dev))
            first.append(_remote(flat_ref.at[pid], land_ref.at[me], send_sems.at[1, k], recv_sems.at[1, k], dev))
        for cp in first:
            cp.start()
        for k in range(1, N_DEV):
            dev, pid = _peer(k, x, y, c)
            _remote(dmx_ref.at[pid], dmrecv_ref.at[pid], send_sems.at[0, k], recv_sems.at[0, k], dev).wait_recv()
            _remote(flat_ref.at[pid], land_ref.at[pid], send_sems.at[1, k], recv_sems.at[1, k], dev).wait_recv()
        total = land_ref[0]
        for s in range(1, N_DEV):
            total = total + land_ref[s]
        red_ref[me] = total
        second = []
        for k in range(1, N_DEV):
            dev, _ = _peer(k, x, y, c)
            second.append(_remote(red_ref.at[me], red_ref.at[me], send_sems.at[2, k], recv_sems.at[2, k], dev))
        for cp in second:
            cp.start()
        for k in range(1, N_DEV):
            dev, pid = _peer(k, x, y, c)
            _remote(red_ref.at[pid], red_ref.at[pid], send_sems.at[2, k], recv_sems.at[2, k], dev).wait_recv()
        for cp in first + second:
            cp.wait_send()

    return pl.pallas_call(
        body,
        name="small_exchange",
        in_specs=[VMEM, VMEM],
        out_specs=[VMEM, VMEM],
        out_shape=[jax.ShapeDtypeStruct(dmx.shape, F32), jax.ShapeDtypeStruct(flat.shape, F32)],
        scratch_shapes=[
            pltpu.VMEM(flat.shape, F32),
            pltpu.SemaphoreType.DMA((3, N_DEV)),
            pltpu.SemaphoreType.DMA((3, N_DEV)),
        ],
        compiler_params=pltpu.CompilerParams(vmem_limit_bytes=VMEM_LIMIT_BYTES),
    )(dmx, flat)


HBM = pl.BlockSpec(memory_space=pltpu.HBM)
SEM = pl.BlockSpec(memory_space=pltpu.SEMAPHORE)
EFFECT = pltpu.SideEffectType.DATAFLOW_SIDE_EFFECTING


def _own_slot(me, block):
    land = lax.empty((N_DEV, *block.shape), block.dtype)
    return lax.dynamic_update_slice_in_dim(land, block[None], me, axis=0)


N_CHIP_PEERS = 3


class Gather:
    def __init__(self, shards, lands, after, *, name):
        nt = len(shards)
        self.name = name

        def body(*refs):
            src_refs, land_refs = refs[:nt], refs[nt : 2 * nt]
            send_sems, recv_sems = refs[2 * nt + 1 : 3 * nt + 1], refs[3 * nt + 1 : 4 * nt + 1]
            token = refs[-1]
            x, y, c, me = _position()
            for t in range(nt):
                for k, dev in enumerate(self._targets(x, y, c)):
                    _remote(src_refs[t], land_refs[t].at[me], send_sems[t].at[k], recv_sems[t].at[k], dev).start()
            token[...] = jnp.zeros_like(token)

        outs = pl.pallas_call(
            body,
            name=name + "_start",
            in_specs=[HBM] * (2 * nt) + [ANY],
            out_specs=[SEM] * (2 * nt) + [HBM] * (2 * nt) + [VMEM],
            out_shape=[pltpu.SemaphoreType.DMA((1 + N_CHIP_PEERS,))] * (2 * nt)
            + [pltpu.HBM(a.shape, a.dtype) for a in (*shards, *lands)]
            + [jax.ShapeDtypeStruct((SUBLANES, LANES), F32)],
            input_output_aliases={i: 2 * nt + i for i in range(2 * nt)},
            compiler_params=pltpu.CompilerParams(has_side_effects=EFFECT),
        )(*[pltpu.with_memory_space_constraint(a, pltpu.HBM) for a in (*shards, *lands)], after)
        self.send_sems, self.recv_sems = list(outs[:nt]), list(outs[nt : 2 * nt])
        self.srcs, self.lands = list(outs[2 * nt : 3 * nt]), list(outs[3 * nt : 4 * nt])
        self.token = outs[-1]

    @staticmethod
    def _chips(x, y):
        return [(1 - x, y), (x, 1 - y), (1 - x, 1 - y)]

    @classmethod
    def _targets(cls, x, y, c):
        return [(x, y, 1 - c)] + [(*chip, c) for chip in cls._chips(x, y)]

    def zero(self):
        return self.token[0, 0]

    def wait(self, which, after, *, name):
        n = len(which)

        def slot(px, py, pc):
            return 4 * px + 2 * py + pc

        def pass_body(*refs):
            land_refs, recv_sems = refs[:n], refs[n : 2 * n]
            fwd_send, fwd_recv = refs[3 * n + 1 : 4 * n + 1], refs[4 * n + 1 : 5 * n + 1]
            x, y, c, _ = _position()
            for t in range(n):
                for j, chip in enumerate(self._chips(x, y)):
                    blk = land_refs[t].at[slot(*chip, c)]
                    _remote(blk, blk, fwd_send[t].at[j], recv_sems[t].at[1 + j], (*chip, c)).wait_recv()
                    _remote(blk, blk, fwd_send[t].at[j], fwd_recv[t].at[j], (x, y, 1 - c)).start()

        lands = [self.lands[t] for t in which]
        outs = pl.pallas_call(
            pass_body,
            name=name + "_pass",
            in_specs=[HBM] * n + [SEM] * n + [ANY],
            out_specs=[HBM] * n + [SEM] * (2 * n),
            out_shape=[pltpu.HBM(a.shape, a.dtype) for a in lands] + [pltpu.SemaphoreType.DMA((N_CHIP_PEERS,))] * (2 * n),
            input_output_aliases={i: i for i in range(n)},
            compiler_params=pltpu.CompilerParams(has_side_effects=EFFECT),
        )(*lands, *[self.recv_sems[t] for t in which], after)
        lands, fwd_send, fwd_recv = outs[:n], outs[n : 2 * n], outs[2 * n :]

        def wait_body(*refs):
            src_refs, land_refs = refs[:n], refs[n : 2 * n]
            send_sems, recv_sems = refs[2 * n : 3 * n], refs[3 * n : 4 * n]
            fwd_send, fwd_recv = refs[4 * n : 5 * n], refs[5 * n : 6 * n]
            x, y, c, me = _position()
            sibling = (x, y, 1 - c)
            for t in range(n):
                for k, dev in enumerate(self._targets(x, y, c)):
                    _remote(src_refs[t], land_refs[t].at[me], send_sems[t].at[k], recv_sems[t].at[k], dev).wait_send()
                blk = land_refs[t].at[slot(x, y, 1 - c)]
                _remote(blk, blk, send_sems[t].at[0], recv_sems[t].at[0], sibling).wait_recv()
                for j, chip in enumerate(self._chips(x, y)):
                    sent = land_refs[t].at[slot(*chip, c)]
                    _remote(sent, sent, fwd_send[t].at[j], fwd_recv[t].at[j], sibling).wait_send()
                    got = land_refs[t].at[slot(*chip, 1 - c)]
                    _remote(got, got, fwd_send[t].at[j], fwd_recv[t].at[j], sibling).wait_recv()

        srcs = [self.srcs[t] for t in which]
        outs = pl.pallas_call(
            wait_body,
            name=name,
            in_specs=[HBM] * (2 * n) + [SEM] * (4 * n),
            out_specs=[HBM] * (2 * n),
            out_shape=[pltpu.HBM(a.shape, a.dtype) for a in (*srcs, *lands)],
            input_output_aliases={i: i for i in range(2 * n)},
            compiler_params=pltpu.CompilerParams(has_side_effects=EFFECT),
        )(*srcs, *lands, *[self.send_sems[t] for t in which], *[self.recv_sems[t] for t in which], *fwd_send, *fwd_recv)
        return outs[n:]


class Scatter:
    def __init__(self, srcs, lands, after, *, name):
        self.name = name
        nt = self.nt = len(srcs)
        peers = N_DEV - 1

        def body(*refs):
            src_refs, land_refs = refs[:nt], refs[nt : 2 * nt]
            send_sems, recv_sems = refs[2 * nt + 1 : 3 * nt + 1], refs[3 * nt + 1 : 4 * nt + 1]
            token = refs[-1]
            x, y, c, me = _position()
            for t in range(nt):
                for k in range(1, N_DEV):
                    dev, pid = _peer(k, x, y, c)
                    src = src_refs[t].at[pid]
                    _remote(src, land_refs[t].at[me], send_sems[t].at[k - 1], recv_sems[t].at[k - 1], dev).start()
            token[...] = jnp.zeros_like(token)

        outs = pl.pallas_call(
            body,
            name=name + "_start",
            in_specs=[HBM] * (2 * nt) + [ANY],
            out_specs=[SEM] * (2 * nt) + [HBM] * (2 * nt) + [VMEM],
            out_shape=[pltpu.SemaphoreType.DMA((peers,))] * (2 * nt)
            + [pltpu.HBM(a.shape, a.dtype) for a in (*srcs, *lands)]
            + [jax.ShapeDtypeStruct((SUBLANES, LANES), F32)],
            input_output_aliases={i: 2 * nt + i for i in range(2 * nt)},
            compiler_params=pltpu.CompilerParams(has_side_effects=EFFECT),
        )(*[pltpu.with_memory_space_constraint(a, pltpu.HBM) for a in (*srcs, *lands)], after)
        self.send_sems, self.recv_sems = outs[:nt], outs[nt : 2 * nt]
        self.srcs, self.lands = outs[2 * nt : 3 * nt], outs[3 * nt : 4 * nt]
        self.token = outs[-1]

    def zero(self):
        return self.token[0, 0]

    def wait(self, which, after, *, name):
        n = len(which)

        def body(*refs):
            src_refs, land_refs = refs[:n], refs[n : 2 * n]
            send_sems, recv_sems = refs[2 * n : 3 * n], refs[3 * n : 4 * n]
            x, y, c, _ = _position()
            for t in range(n):
                for k in range(1, N_DEV):
                    dev, pid = _peer(k, x, y, c)
                    src = src_refs[t].at[pid]
                    cp = _remote(src, land_refs[t].at[pid], send_sems[t].at[k - 1], recv_sems[t].at[k - 1], dev)
                    cp.wait_send()
                    cp.wait_recv()

        srcs = [self.srcs[t] for t in which]
        lands = [self.lands[t] for t in which]
        outs = pl.pallas_call(
            body,
            name=name,
            in_specs=[HBM] * (2 * n) + [SEM] * (2 * n) + [ANY],
            out_specs=[HBM] * (2 * n),
            out_shape=[pltpu.HBM(a.shape, a.dtype) for a in (*srcs, *lands)],
            input_output_aliases={i: i for i in range(2 * n)},
            compiler_params=pltpu.CompilerParams(has_side_effects=EFFECT),
        )(*srcs, *lands, *[self.send_sems[t] for t in which], *[self.recv_sems[t] for t in which], after)
        return outs[n:]


def _adam_update(g, w, m, v):
    m2 = ADAM_B1 * m + (1.0 - ADAM_B1) * g
    v2 = ADAM_B2 * v + (1.0 - ADAM_B2) * jnp.square(g)
    m_hat = m2 / (1.0 - ADAM_B1**ADAM_STEP)
    v_hat = v2 / (1.0 - ADAM_B2**ADAM_STEP)
    delta = -ADAM_LR * (m_hat / (jnp.sqrt(v_hat) + ADAM_EPS) + ADAM_WD * w)
    return delta, m2, v2


def adamw(gparts, w, m, v, *, name):
    p, r, c = gparts.shape
    tr = r if r <= 256 else (256 if c <= D_MODEL else 128)

    def body(g_ref, w_ref, m_ref, v_ref, go_ref, d_ref, mo_ref, vo_ref):
        g = g_ref[0].astype(F32)
        for i in range(1, p):
            g = g + g_ref[i].astype(F32)
        delta, m2, v2 = _adam_update(g, w_ref[...], m_ref[...], v_ref[...])
        go_ref[...] = g
        d_ref[...] = delta
        mo_ref[...] = m2
        vo_ref[...] = v2

    blk = pl.BlockSpec((tr, c), lambda i: (i, 0))
    return pl.pallas_call(
        body,
        name=name,
        grid=(r // tr,),
        in_specs=[pl.BlockSpec((p, tr, c), lambda i: (0, i, 0)), blk, blk, blk],
        out_specs=[blk] * 4,
        out_shape=[jax.ShapeDtypeStruct((r, c), F32)] * 4,
        compiler_params=_params(("parallel",)),
    )(gparts, w, m, v)


def ada_grad_adamw(cg, dmrecv, w4, m4, v4, *, name):
    nt, k, ncol = w4.shape

    def body(cg_ref, dm_ref, w_ref, m_ref, v_ref, go_ref, d_ref, mo_ref, vo_ref, gb_ref):
        sc = _silu_bf16(cg_ref[...].reshape(N_DEV * SUBLANES, k))
        dm = dm_ref[...].reshape(N_DEV * SUBLANES, ncol)
        g = lax.dot_general(sc, dm.astype(BF16), (((0,), (0,)), ((), ())), preferred_element_type=F32)
        delta, m2, v2 = _adam_update(g, w_ref[...], m_ref[...], v_ref[...])
        go_ref[...] = g
        d_ref[...] = delta
        mo_ref[...] = m2
        vo_ref[...] = v2
        gb_ref[...] = jnp.broadcast_to(jnp.sum(dm, axis=0, keepdims=True), (SUBLANES, ncol))

    wblk = pl.BlockSpec((None, k, ncol), lambda t: (t, 0, 0))
    return pl.pallas_call(
        body,
        name=name,
        grid=(nt,),
        in_specs=[
            pl.BlockSpec((N_DEV, SUBLANES, k), lambda t: (0, 0, 0)),
            pl.BlockSpec((N_DEV, None, SUBLANES, ncol), lambda t: (0, t, 0, 0)),
            wblk,
            wblk,
            wblk,
        ],
        out_specs=[wblk] * 4 + [pl.BlockSpec((None, SUBLANES, ncol), lambda t: (t, 0, 0))],
        out_shape=[jax.ShapeDtypeStruct((nt, k, ncol), F32)] * 4 + [jax.ShapeDtypeStruct((nt, SUBLANES, ncol), F32)],
        compiler_params=_params(("parallel",)),
    )(cg, dmrecv, w4, m4, v4)


def kernel(x, c, ada_w, ada_b, ln_g, ln_b, a_w_in, a_b_in, a_vn_g, a_vn_b, a_w_s, a_b_s, a_w_out, b_w_qkv, b_w_out, mlp_w_up, mlp_w_down, loss_target, m_ada_w, m_ada_b, m_ln_g, m_ln_b, m_a_w_in, m_a_b_in, m_a_vn_g, m_a_vn_b, m_a_w_s, m_a_b_s, m_a_w_out, m_b_w_qkv, m_b_w_out, m_mlp_w_up, m_mlp_w_down, v_ada_w, v_ada_b, v_ln_g, v_ln_b, v_a_w_in, v_a_b_in, v_a_vn_g, v_a_vn_b, v_a_w_s, v_a_b_s, v_a_w_out, v_b_w_qkv, v_b_w_out, v_mlp_w_up, v_mlp_w_down):
    x0 = x[0]
    target = loss_target[0]
    me = 4 * lax.axis_index("x") + 2 * lax.axis_index("y") + lax.axis_index("c")

    ada_w4 = ada_w.reshape(N_SUB, D_MODEL, -1)
    ada_b4 = ada_b.reshape(N_SUB, -1)
    ln8 = jnp.concatenate([ln_g.reshape(N_SUB, -1), ln_b.reshape(N_SUB, -1)], axis=0)
    c8 = jnp.broadcast_to(c, (SUBLANES, D_MODEL))
    cg, lng, mrecv = ada_exchange(c8, ada_w4, ada_b4, ln8)

    W_IN, W_AOUT, W_UP0, W_DN0, W_QKV, W_BOUT, W_UP1, W_DN1 = range(8)
    shards = [
        a_w_in[0].astype(BF16),
        a_w_out[0].astype(BF16),
        mlp_w_up[0].astype(BF16),
        mlp_w_down[0].astype(BF16),
        b_w_qkv[0].astype(BF16),
        b_w_out[0].astype(BF16),
        mlp_w_up[1].astype(BF16),
        mlp_w_down[1].astype(BF16),
    ]
    gather = Gather(shards, [_own_slot(me, s) for s in shards], mrecv, name="gather")

    modv = mrecv[:, :, 0, :].transpose(1, 0, 2).reshape(N_SUB, 3 * D_MODEL) + gather.zero()
    shift = [modv[t : t + 1, :D_MODEL] for t in range(N_SUB)]
    scale = [modv[t : t + 1, D_MODEL : 2 * D_MODEL] for t in range(N_SUB)]
    gate1 = [1.0 + modv[t : t + 1, 2 * D_MODEL :] for t in range(N_SUB)]
    lng_full = [lng[:, t, :].reshape(1, D_MODEL) for t in range(N_SUB)]
    lnb_full = [lng[:, N_SUB + t, :].reshape(1, D_MODEL) for t in range(N_SUB)]

    ident = lambda acc: (acc,)
    relu2 = lambda a: jnp.square(jnp.maximum(a.astype(F32), 0.0)).astype(BF16)
    vn_g, vn_b, w_s = a_vn_g, a_vn_b, a_w_s[0]
    bias_full = jnp.repeat(a_b_s[0].T, A_GROUP_DIM, axis=1)
    w_up3, w_dn3 = [None, None], [None, None]

    def mlp_forward(i, h, after):
        up, dn = gather.wait([W_UP0, W_DN0] if i == 0 else [W_UP1, W_DN1], after, name=f"gather_wait_mlp{i}")
        w_up3[i], w_dn3[i] = up, dn.reshape(1, D_FF, D_MODEL)
        (a,) = mm_nn(h, w_up3[i], name=f"mlp{i}_up", tm=2048, ps=2, tn=512, tk=D_MODEL, epilogue=ident, outs=(BF16,))
        (y,) = mm_nn(
            a, w_dn3[i], name=f"mlp{i}_down", tm=2048, ps=1, tn=D_MODEL, tk=1024, prologue=relu2, epilogue=ident, outs=(F32,)
        )
        return a, y

    h0 = modulate(x0, scale[0], shift[0], name="modulate0")
    w_in3, w_aout3 = gather.wait([W_IN, W_AOUT], h0, name="gather_wait_a")
    w_aout3 = w_aout3.reshape(1, D_MODEL, D_MODEL)
    (a_pre,) = mm_nn(
        h0, w_in3, name="a_in", tm=2048, ps=4, tn=256, tk=D_MODEL, epilogue=lambda acc, b: (acc + b,),
        extras=[(a_b_in, "row")], outs=(BF16,),
    )
    p_gate = gate_fwd(a_pre, vn_g, vn_b, w_s, bias_full, name="gate_fwd")
    (y0,) = mm_nn(p_gate, w_aout3, name="a_out", tm=2048, ps=1, tn=D_MODEL, tk=D_MODEL, epilogue=ident, outs=(F32,))
    x1, h1 = residual_ln(x0, y0, gate1[0], lng_full[0], lnb_full[0], scale[1], shift[1], name="res_ln0")
    a1, y1 = mlp_forward(0, h1, y0)
    x2, h2 = residual_ln(x1, y1, gate1[1], lng_full[1], lnb_full[1], scale[2], shift[2], name="res_ln1")
    w_qkv_shards, w_bout3 = gather.wait([W_QKV, W_BOUT], y1, name="gather_wait_b")
    w_bout3 = w_bout3.reshape(1, D_MODEL, D_MODEL)
    w_qkv_cols = w_qkv_shards.transpose(1, 0, 2).reshape(D_MODEL, D_QKV)
    w_pat = [w_qkv_cols[None, :, 3 * D_MODEL * g : 3 * D_MODEL * (g + 1)] for g in range(N_PAT)]
    dil = [d for _, d in B_PATTERNS]
    h2_p = [h2] + [permute_rows(h2, dil[g], inverse=False, name=f"perm_h{g}") for g in range(1, N_PAT)]
    qkv_p, pat_o, pat_lse = [], [], []
    for g in range(N_PAT):
        (qkv_g,) = mm_nn(
            h2_p[g], w_pat[g], name=f"b_qkv{g}", tm=2048, ps=1, tn=D_MODEL, tk=D_MODEL, epilogue=ident, outs=(BF16,)
        )
        o_g, lse_g = attn_fwd(qkv_g, g, name=f"attn_fwd{g}")
        if g > 0:
            o_g = permute_rows(o_g, dil[g], inverse=True, name=f"unperm_o{g}")
            lse_g = permute_rows(lse_g, dil[g], inverse=True, name=f"unperm_lse{g}")
        qkv_p.append(qkv_g)
        pat_o.append(o_g)
        pat_lse.append(lse_g)
    o_b, o_f, lse = attn_combine(pat_o, pat_lse, name="attn_combine")
    (y2,) = mm_nn(o_b, w_bout3, name="b_out", tm=2048, ps=1, tn=D_MODEL, tk=D_MODEL, epilogue=ident, outs=(F32,))
    x3, h3 = residual_ln(x2, y2, gate1[2], lng_full[2], lnb_full[2], scale[3], shift[3], name="res_ln2")
    a3, y3 = mlp_forward(1, h3, y2)
    dxo, loss_local = residual_ln_loss(x3, y3, gate1[3], lng_full[3], lnb_full[3], target, name="res_ln3_loss")
    loss = lax.psum(loss_local[0, 0], ("x", "y", "c"))

    def scatter(parts, after, name):
        parts = [p.reshape(N_DEV, -1, p.shape[-1]) for p in parts]
        lands = [_own_slot(me, lax.dynamic_index_in_dim(p, me, 0, keepdims=False)) for p in parts]
        return Scatter(parts, lands, after, name=name)

    def mlp_backward(i, h, a, dy):
        (da,) = mm_nt(
            dy,
            w_dn3[i],
            name=f"mlp{i}_da",
            tm=2048,
            tko=1024,
            ps=1,
            tc=D_MODEL,
            epilogue=lambda acc, act: (acc * (2.0 * jnp.maximum(act.astype(F32), 0.0)),),
            extras=[(a, "full")],
            outs=(BF16,),
        )
        (dh,) = mm_nt(da, w_up3[i], name=f"mlp{i}_dh", tm=2048, tko=1024, ps=2, tc=512, epilogue=ident, outs=(F32,))
        dw_dn = mm_tn(
            a, dy, name=f"mlp{i}_dw_down", p=1, tk=1024, ps=1, tn=D_MODEL, tmc=2048, prologue=relu2, out_dtype=BF16
        )
        dw_up = mm_tn(h, da, name=f"mlp{i}_dw_up", p=N_DEV, tk=1024, ps=2, tn=512, tmc=2048, out_dtype=BF16)
        return scatter([dw_up, dw_dn], dh, f"scatter_mlp{i}"), dh

    dz3, dy3, st3 = residual_ln_bwd(name="res_bwd3", dxo=dxo, this=(x3, y3, gate1[3], lng_full[3]))
    rs_mlp1, dh3 = mlp_backward(1, h3, a3, dy3)
    dz2, dy2, st2 = residual_ln_bwd(
        name="res_bwd2", later=(dh3, dz3, x3, scale[3] + rs_mlp1.zero()), this=(x2, y2, gate1[2], lng_full[2])
    )
    (d_o,) = mm_nt(dy2, w_bout3, name="b_do", tm=2048, tko=1024, ps=1, tc=D_MODEL, epilogue=ident, outs=(F32,))
    do_b, delta = attn_delta(d_o, o_f, name="attn_delta")
    dh2, dw_pat = None, []
    for g in range(N_PAT):
        do_g, lse_g, delta_g = do_b, lse, delta
        if g > 0:
            do_g = permute_rows(do_b, dil[g], inverse=False, name=f"perm_do{g}")
            lse_g = permute_rows(lse, dil[g], inverse=False, name=f"perm_lse{g}")
            delta_g = permute_rows(delta, dil[g], inverse=False, name=f"perm_delta{g}")
        dqkv_g = attn_bwd(qkv_p[g], do_g, lse_g, delta_g, g, name=f"attn_bwd{g}")
        (dh_g,) = mm_nt(
            dqkv_g, w_pat[g], name=f"b_dh{g}", tm=2048, tko=1024, ps=1, tc=D_MODEL, epilogue=ident, outs=(F32,)
        )
        dh2 = dh_g if g == 0 else permute_rows(dh_g, dil[g], inverse=True, add=dh2, name=f"unperm_dh{g}")
        dw_pat.append(
            mm_tn(h2_p[g], dqkv_g, name=f"b_dw_qkv{g}", p=1, tk=1024, ps=1, tn=D_MODEL, tmc=2048, out_dtype=BF16)[0]
        )
    dw_bout = mm_tn(o_b, dy2, name="b_dw_out", p=1, tk=1024, ps=1, tn=D_MODEL, tmc=2048, out_dtype=BF16)
    dw_qkv = jnp.concatenate(dw_pat, axis=1).reshape(D_MODEL, N_DEV, -1).transpose(1, 0, 2)
    rs_b = scatter([dw_qkv, dw_bout], dh2, "scatter_b")
    dz1, dy1, st1 = residual_ln_bwd(
        name="res_bwd1", later=(dh2, dz2, x2, scale[2] + rs_b.zero()), this=(x1, y1, gate1[1], lng_full[1])
    )
    rs_mlp0, dh1 = mlp_backward(0, h1, a1, dy1)
    dz0, dy0, st0 = residual_ln_bwd(
        name="res_bwd0", later=(dh1, dz1, x1, scale[1] + rs_mlp0.zero()), this=(x0, y0, gate1[0], lng_full[0])
    )
    (dp_gate,) = mm_nt(dy0, w_aout3, name="a_dp", tm=2048, tko=1024, ps=1, tc=D_MODEL, epilogue=ident, outs=(F32,))
    da0, d_ws, d_bs, gate_rows = gate_bwd(a_pre, dp_gate, vn_g, vn_b, w_s, bias_full, name="gate_bwd")
    (dh0,) = mm_nt(da0, w_in3, name="a_dh", tm=2048, tko=1024, ps=4, tc=256, epilogue=ident, outs=(F32,))
    dw_aout = mm_tn(p_gate, dy0, name="a_dw_out", p=1, tk=1024, ps=1, tn=D_MODEL, tmc=2048, out_dtype=BF16)
    dw_in = mm_tn(h0, da0, name="a_dw_in", p=N_DEV, tk=1024, ps=4, tn=256, tmc=2048, out_dtype=BF16)
    grad_x, stf = residual_ln_bwd(name="res_bwd_in", later=(dh0, dz0, x0, scale[0]))

    stats_after = [stf, st0, st1, st2]
    stats_own = [st0, st1, st2, st3]
    dm = jnp.stack(
        [
            jnp.concatenate(
                [stats_after[t][ST_DSHIFT], stats_after[t][ST_DSCALE], stats_own[t][ST_DGATE]], axis=0
            )
            for t in range(N_SUB)
        ]
    )
    ncol = 3 * D_MODEL // N_DEV
    dmx = jnp.pad(
        dm.reshape(N_SUB, N_DEV, ncol).transpose(1, 0, 2)[:, :, None, :], ((0, 0), (0, 0), (0, SUBLANES - 1), (0, 0))
    )
    small = [
        gate_rows[0],
        gate_rows[1],
        d_ws.reshape(-1),
        d_bs[:, :A_GROUPS].T.reshape(-1),
        *[stats_own[t][ST_DG] for t in range(N_SUB)],
        *[stats_own[t][ST_DB] for t in range(N_SUB)],
    ]
    n_small = sum(s.size for s in small)
    part_rows = -(-n_small // (N_DEV * LANES * SUBLANES)) * SUBLANES
    flat = jnp.concatenate(small + [jnp.zeros((N_DEV * part_rows * LANES - n_small,), F32)])
    dmrecv, reduced = small_exchange(dmx, flat.reshape(N_DEV, part_rows, LANES))
    reduced = reduced.reshape(-1)
    sizes = [2 * D_MODEL, D_MODEL, D_MODEL, A_GROUPS * CHUNK * CHUNK, A_GROUPS * CHUNK, N_SUB * D_MODEL, N_SUB * D_MODEL]
    offs = [sum(sizes[:i]) for i in range(len(sizes) + 1)]
    g_b_in, g_vn_g, g_vn_b, g_ws, g_bs, g_lng, g_lnb = [reduced[offs[i] : offs[i + 1]] for i in range(len(sizes))]

    results = {}

    def update(wname, gparts, w, m, v):
        shape = w.shape
        w2 = w.reshape(-1, shape[-1])
        outs = adamw(gparts.reshape(gparts.shape[0], *w2.shape), w2, m.reshape(w2.shape), v.reshape(w2.shape), name=f"adamw_{wname}")
        results[wname] = [o.reshape(shape) for o in outs]

    rs_a = scatter([dw_in, dw_aout], reduced, "scatter_a")
    ada_outs = ada_grad_adamw(cg + rs_a.zero(), dmrecv, ada_w4, m_ada_w.reshape(ada_w4.shape), v_ada_w.reshape(ada_w4.shape), name="ada_grad_adamw")
    results["ada_w"] = [o.reshape(ada_w.shape) for o in ada_outs[:4]]
    update("ada_b", ada_outs[4][:, 0, :][None], ada_b, m_ada_b, v_ada_b)
    ln_cols = D_MODEL // N_DEV
    my_ln = lambda gfull: lax.dynamic_slice_in_dim(gfull.reshape(N_SUB, N_DEV, ln_cols), me, 1, axis=1)
    update("ln_g", my_ln(g_lng).reshape(1, N_SUB, ln_cols), ln_g, m_ln_g, v_ln_g)
    update("ln_b", my_ln(g_lnb).reshape(1, N_SUB, ln_cols), ln_b, m_ln_b, v_ln_b)
    update("a_b_in", g_b_in[None], a_b_in, m_a_b_in, v_a_b_in)
    update("a_vn_g", g_vn_g[None], a_vn_g, m_a_vn_g, v_a_vn_g)
    update("a_vn_b", g_vn_b[None], a_vn_b, m_a_vn_b, v_a_vn_b)
    update("a_w_s", g_ws[None], a_w_s, m_a_w_s, v_a_w_s)
    update("a_b_s", g_bs[None], a_b_s, m_a_b_s, v_a_b_s)
    g_up1, g_dn1 = rs_mlp1.wait([0, 1], grad_x, name="scatter_wait_mlp1")
    g_qkv, g_bout = rs_b.wait([0, 1], grad_x, name="scatter_wait_b")
    update("b_w_qkv", g_qkv, b_w_qkv, m_b_w_qkv, v_b_w_qkv)
    update("b_w_out", g_bout, b_w_out, m_b_w_out, v_b_w_out)
    g_up0, g_dn0 = rs_mlp0.wait([0, 1], grad_x, name="scatter_wait_mlp0")
    update("mlp_w_up", jnp.concatenate([g_up0, g_up1], axis=1), mlp_w_up, m_mlp_w_up, v_mlp_w_up)
    update("mlp_w_down", jnp.concatenate([g_dn0, g_dn1], axis=1), mlp_w_down, m_mlp_w_down, v_mlp_w_down)
    g_in, g_aout = rs_a.wait([0, 1], grad_x, name="scatter_wait_a")
    update("a_w_in", g_in, a_w_in, m_a_w_in, v_a_w_in)
    update("a_w_out", g_aout, a_w_out, m_a_w_out, v_a_w_out)

    order = ["ada_w", "ada_b", "ln_g", "ln_b", "a_w_in", "a_b_in", "a_vn_g", "a_vn_b", "a_w_s", "a_b_s", "a_w_out", "b_w_qkv", "b_w_out", "mlp_w_up", "mlp_w_down"]
    return (loss, grad_x[None], *[results[n][0] for n in order], *[results[n][1] for n in order],
            *[results[n][2] for n in order], *[results[n][3] for n in order])
```

```python
import math

import jax
import jax.numpy as jnp
from jax import lax
from jax.experimental import pallas as pl
from jax.experimental.pallas import tpu as pltpu

F32 = jnp.float32
BF16 = jnp.bfloat16
MESH = pl.DeviceIdType.MESH
ANY = pl.BlockSpec(memory_space=pl.ANY)
VMEM = pl.BlockSpec(memory_space=pltpu.VMEM)

N_DEV = 8
D_MODEL = 1024
SEQ = 4096
DEPTH = 2
CHUNK = 128
A_GROUPS = 16
A_GROUP_DIM = D_MODEL // A_GROUPS
B_HEADS = 16
B_HEAD_DIM = 64
B_PATTERNS = ((128, 1), (512, 4), (2048, 16))
N_PAT = len(B_PATTERNS)
SPAN = 128
D_FF = 4 * D_MODEL
D_QKV = N_PAT * 3 * D_MODEL
ALPHA = (2 * DEPTH) ** 0.25
LN_EPS = 1e-5
NEG = -1e30
ADAM_LR = 0.001
ADAM_B1 = 0.9
ADAM_B2 = 0.999
ADAM_EPS = 1e-08
ADAM_WD = 0.01
ADAM_STEP = 10
GELU_C = math.sqrt(2.0 / math.pi)
GELU_A = 0.044715

VMEM_LIMIT_BYTES = 56 * 1024 * 1024
LANES = 128
SUBLANES = 8
ROW_TILE = 512
N_SUB = 2 * DEPTH


def _params(sem):
    return pltpu.CompilerParams(dimension_semantics=sem, vmem_limit_bytes=VMEM_LIMIT_BYTES)


def _lane(shape):
    return lax.broadcasted_iota(jnp.int32, shape, len(shape) - 1)


def _split_bf16(x):
    hi = x.astype(BF16)
    lo = (x - hi.astype(F32)).astype(BF16)
    return hi, lo


def _group_expand_matrix(groups_padded, width):
    per = width // A_GROUPS
    r = lax.broadcasted_iota(jnp.int32, (groups_padded, width), 0)
    c = lax.broadcasted_iota(jnp.int32, (groups_padded, width), 1)
    return (c // per == r).astype(BF16)


def _group_reduce_matrix(width, groups_padded):
    per = width // A_GROUPS
    r = lax.broadcasted_iota(jnp.int32, (width, groups_padded), 0)
    c = lax.broadcasted_iota(jnp.int32, (width, groups_padded), 1)
    return (r // per == c).astype(BF16)


def _expand_groups(w):
    e = _group_expand_matrix(LANES, D_MODEL)
    hi, lo = _split_bf16(w)
    return jnp.dot(hi, e, preferred_element_type=F32) + jnp.dot(lo, e, preferred_element_type=F32)


def _reduce_groups(x):
    e = _group_reduce_matrix(D_MODEL, LANES)
    hi, lo = _split_bf16(x)
    return jnp.dot(hi, e, preferred_element_type=F32) + jnp.dot(lo, e, preferred_element_type=F32)


def _column_tiles(p, n, ps, tn):
    assert (ps == 1 or tn == n) and p % ps == 0 and n % tn == 0
    q = n // tn
    return (p // ps) * q, q


def _extra_specs(extras, tm, width):
    specs = []
    for _, kind in extras:
        if kind == "row":
            specs.append(pl.BlockSpec((1, width), lambda i, j, c: (0, j)))
        else:
            specs.append(pl.BlockSpec((tm, width), lambda i, j, c: (i, j)))
    return specs


def mm_nn(a, b3, *, name, tm, ps, tn, tk, epilogue, extras=(), outs, prologue=None):
    m, k = a.shape
    p, _, n = b3.shape
    nj, q = _column_tiles(p, n, ps, tn)
    nk = k // tk
    width = ps * tn

    def body(a_ref, b_ref, *rest):
        ex = rest[: len(extras)]
        out_refs = rest[len(extras) : len(extras) + len(outs)]
        kk = pl.program_id(2)
        av = a_ref[...] if prologue is None else prologue(a_ref[...])

        def finish(cs, acc):
            res = epilogue(acc, *[e[:, cs] for e in ex])
            for o_ref, r in zip(out_refs, res, strict=True):
                o_ref[:, cs] = r.astype(o_ref.dtype)

        for s in range(ps):
            cs = slice(s * tn, (s + 1) * tn)
            part = jnp.dot(av, b_ref[s], preferred_element_type=F32)
            if nk == 1:
                finish(cs, part)
                continue
            acc_ref = rest[-1]

            @pl.when(kk == 0)
            def _(part=part, cs=cs):
                acc_ref[:, cs] = part

            @pl.when(kk > 0)
            def _(part=part, cs=cs):
                acc_ref[:, cs] += part

        if nk > 1:

            @pl.when(kk == nk - 1)
            def _():
                for s in range(ps):
                    cs = slice(s * tn, (s + 1) * tn)
                    finish(cs, rest[-1][:, cs])

    return pl.pallas_call(
        body,
        name=name,
        grid=(m // tm, nj, nk),
        in_specs=[
            pl.BlockSpec((tm, tk), lambda i, j, kk: (i, kk)),
            pl.BlockSpec((ps, tk, tn), lambda i, j, kk: (j // q, kk, j % q)),
            *_extra_specs(extras, tm, width),
        ],
        out_specs=[pl.BlockSpec((tm, width), lambda i, j, kk: (i, j)) for _ in outs],
        out_shape=[jax.ShapeDtypeStruct((m, p * n), dt) for dt in outs],
        scratch_shapes=[pltpu.VMEM((tm, width), F32)] if nk > 1 else [],
        compiler_params=_params(("parallel", "parallel", "arbitrary")),
    )(a, b3, *[arr for arr, _ in extras])


def mm_nt(g, b3, *, name, tm, tko, ps, tc, epilogue, extras=(), outs):
    m = g.shape[0]
    p, k, n = b3.shape
    nc, q = _column_tiles(p, n, ps, tc)

    def body(g_ref, b_ref, *rest):
        ex = rest[: len(extras)]
        out_refs = rest[len(extras) : len(extras) + len(outs)]
        c = pl.program_id(2)
        part = None
        for s in range(ps):
            d = lax.dot_general(
                g_ref[:, s * tc : (s + 1) * tc], b_ref[s], (((1,), (1,)), ((), ())), preferred_element_type=F32
            )
            part = d if part is None else part + d

        def finish(acc):
            res = epilogue(acc, *[e[...] for e in ex])
            for o_ref, r in zip(out_refs, res, strict=True):
                o_ref[...] = r.astype(o_ref.dtype)

        if nc == 1:
            finish(part)
            return
        acc_ref = rest[-1]

        @pl.when(c == 0)
        def _():
            acc_ref[...] = part

        @pl.when(c > 0)
        def _():
            acc_ref[...] += part

        @pl.when(c == nc - 1)
        def _():
            finish(acc_ref[...])

    return pl.pallas_call(
        body,
        name=name,
        grid=(m // tm, k // tko, nc),
        in_specs=[
            pl.BlockSpec((tm, ps * tc), lambda i, j, c: (i, c)),
            pl.BlockSpec((ps, tko, tc), lambda i, j, c: (c // q, j, c % q)),
            *_extra_specs(extras, tm, tko),
        ],
        out_specs=[pl.BlockSpec((tm, tko), lambda i, j, c: (i, j)) for _ in outs],
        out_shape=[jax.ShapeDtypeStruct((m, k), dt) for dt in outs],
        scratch_shapes=[pltpu.VMEM((tm, tko), F32)] if nc > 1 else [],
        compiler_params=_params(("parallel", "parallel", "arbitrary")),
    )(g, b3, *[arr for arr, _ in extras])


def mm_tn(a, g, *, name, p, tk, ps, tn, tmc, out_dtype, prologue=None):
    m, k = a.shape
    n = g.shape[1] // p
    nj, q = _column_tiles(p, n, ps, tn)
    nc = m // tmc

    def body(a_ref, g_ref, o_ref, acc_ref):
        c = pl.program_id(2)
        av = a_ref[...] if prologue is None else prologue(a_ref[...])
        for s in range(ps):
            part = lax.dot_general(
                av, g_ref[:, s * tn : (s + 1) * tn], (((0,), (0,)), ((), ())), preferred_element_type=F32
            )

            @pl.when(c == 0)
            def _(part=part, s=s):
                acc_ref[s] = part

            @pl.when(c > 0)
            def _(part=part, s=s):
                acc_ref[s] += part

        @pl.when(c == nc - 1)
        def _():
            o_ref[...] = acc_ref[...].astype(o_ref.dtype)

    return pl.pallas_call(
        body,
        name=name,
        grid=(k // tk, nj, nc),
        in_specs=[
            pl.BlockSpec((tmc, tk), lambda i, j, c: (c, i)),
            pl.BlockSpec((tmc, ps * tn), lambda i, j, c: (c, j)),
        ],
        out_specs=pl.BlockSpec((ps, tk, tn), lambda i, j, c: (j // q, i, j % q)),
        out_shape=jax.ShapeDtypeStruct((p, k, n), out_dtype),
        scratch_shapes=[pltpu.VMEM((ps, tk, tn), F32)],
        compiler_params=_params(("parallel", "parallel", "arbitrary")),
    )(a, g)


def _rows(cols):
    return pl.BlockSpec((ROW_TILE, cols), lambda i: (i, 0))


def _vec(cols, rows=1):
    return pl.BlockSpec((rows, cols), lambda i: (0, 0))


def _layer_norm_hat(z):
    mu = jnp.mean(z, axis=-1, keepdims=True)
    zc = z - mu
    var = jnp.mean(zc * zc, axis=-1, keepdims=True)
    rstd = lax.rsqrt(var + LN_EPS)
    return zc * rstd, rstd


def modulate(x, scale, shift, *, name):
    s, d = x.shape

    def body(x_ref, sc_ref, sh_ref, h_ref):
        h_ref[...] = (x_ref[...] * (1.0 + sc_ref[...]) + sh_ref[...]).astype(BF16)

    return pl.pallas_call(
        body,
        name=name,
        grid=(s // ROW_TILE,),
        in_specs=[_rows(d), _vec(d), _vec(d)],
        out_specs=_rows(d),
        out_shape=jax.ShapeDtypeStruct((s, d), BF16),
        compiler_params=_params(("parallel",)),
    )(x, scale, shift)


def residual_ln(x, y, gate1, g, b, nscale, nshift, *, name):
    s, d = x.shape

    def body(x_ref, y_ref, gt_ref, g_ref, b_ref, sc_ref, sh_ref, xn_ref, hn_ref):
        z = ALPHA * x_ref[...] + gt_ref[...] * y_ref[...]
        xhat, _ = _layer_norm_hat(z)
        xn = xhat * g_ref[...] + b_ref[...]
        xn_ref[...] = xn
        hn_ref[...] = (xn * (1.0 + sc_ref[...]) + sh_ref[...]).astype(BF16)

    return pl.pallas_call(
        body,
        name=name,
        grid=(s // ROW_TILE,),
        in_specs=[_rows(d), _rows(d), _vec(d), _vec(d), _vec(d), _vec(d), _vec(d)],
        out_specs=[_rows(d), _rows(d)],
        out_shape=[jax.ShapeDtypeStruct((s, d), F32), jax.ShapeDtypeStruct((s, d), BF16)],
        compiler_params=_params(("parallel",)),
    )(x, y, gate1, g, b, nscale, nshift)


def residual_ln_loss(x, y, gate1, g, b, target, *, name):
    s, d = x.shape

    def body(x_ref, y_ref, gt_ref, g_ref, b_ref, t_ref, dx_ref, loss_ref):
        z = ALPHA * x_ref[...] + gt_ref[...] * y_ref[...]
        xhat, _ = _layer_norm_hat(z)
        err = xhat * g_ref[...] + b_ref[...] - t_ref[...]
        dx_ref[...] = err * (1.0 / d)
        part = jnp.sum(jnp.sum(err * err, axis=1, keepdims=True), axis=0, keepdims=True) * (0.5 / d)

        @pl.when(pl.program_id(0) == 0)
        def _():
            loss_ref[...] = part

        @pl.when(pl.program_id(0) > 0)
        def _():
            loss_ref[...] += part

    return pl.pallas_call(
        body,
        name=name,
        grid=(s // ROW_TILE,),
        in_specs=[_rows(d), _rows(d), _vec(d), _vec(d), _vec(d), _rows(d)],
        out_specs=[_rows(d), pl.BlockSpec((1, 1), lambda i: (0, 0))],
        out_shape=[jax.ShapeDtypeStruct((s, d), F32), jax.ShapeDtypeStruct((1, 1), F32)],
        compiler_params=_params(("arbitrary",)),
    )(x, y, gate1, g, b, target)


ST_DSCALE, ST_DSHIFT, ST_DG, ST_DB, ST_DGATE = 0, 1, 2, 3, 4


def residual_ln_bwd(*, name, later=None, dxo=None, this=None):
    lead = later[0] if later is not None else dxo
    s, d = lead.shape
    has_later, has_ln = later is not None, this is not None

    def body(*refs):
        refs = list(refs)
        if has_later:
            dh_ref, dzl_ref, xm_ref, scl_ref = refs[:4]
            refs = refs[4:]
        else:
            dxo_ref = refs.pop(0)
        if has_ln:
            x_ref, y_ref, gt_ref, g_ref = refs[:4]
            refs = refs[4:]
            dz_ref, dy_ref, st_ref = refs
        else:
            dx_ref, st_ref = refs

        @pl.when(pl.program_id(0) == 0)
        def _():
            st_ref[...] = jnp.zeros_like(st_ref)

        def acc(row, val):
            st_ref[row : row + 1, :] += jnp.sum(val, axis=0, keepdims=True)

        if has_later:
            dh = dh_ref[...]
            g_out = ALPHA * dzl_ref[...] + dh * (1.0 + scl_ref[...])
            acc(ST_DSCALE, dh * xm_ref[...])
            acc(ST_DSHIFT, dh)
        else:
            g_out = dxo_ref[...]
        if not has_ln:
            dx_ref[...] = g_out
            return
        y = y_ref[...]
        gate1 = gt_ref[...]
        z = ALPHA * x_ref[...] + gate1 * y
        xhat, rstd = _layer_norm_hat(z)
        acc(ST_DG, g_out * xhat)
        acc(ST_DB, g_out)
        dxh = g_out * g_ref[...]
        m1 = jnp.mean(dxh, axis=-1, keepdims=True)
        m2 = jnp.mean(dxh * xhat, axis=-1, keepdims=True)
        dz = rstd * (dxh - m1 - xhat * m2)
        acc(ST_DGATE, dz * y)
        dz_ref[...] = dz
        dy_ref[...] = (dz * gate1).astype(BF16)

    ins, specs = [], []
    if has_later:
        ins += list(later)
        specs += [_rows(d), _rows(d), _rows(d), _vec(d)]
    else:
        ins += [dxo]
        specs += [_rows(d)]
    if has_ln:
        ins += list(this)
        specs += [_rows(d), _rows(d), _vec(d), _vec(d)]
        out_specs = [_rows(d), _rows(d), _vec(d, SUBLANES)]
        out_shape = [
            jax.ShapeDtypeStruct((s, d), F32),
            jax.ShapeDtypeStruct((s, d), BF16),
            jax.ShapeDtypeStruct((SUBLANES, d), F32),
        ]
    else:
        out_specs = [_rows(d), _vec(d, SUBLANES)]
        out_shape = [jax.ShapeDtypeStruct((s, d), F32), jax.ShapeDtypeStruct((SUBLANES, d), F32)]
    return pl.pallas_call(
        body,
        name=name,
        grid=(s // ROW_TILE,),
        in_specs=specs,
        out_specs=out_specs,
        out_shape=out_shape,
        compiler_params=_params(("arbitrary",)),
    )(*ins)


GATE_CHUNKS = 4


def _gelu(x):
    return 0.5 * x * (1.0 + jnp.tanh(GELU_C * (x + GELU_A * x * x * x)))


def _gelu_grad(x):
    t = jnp.tanh(GELU_C * (x + GELU_A * x * x * x))
    return 0.5 * (1.0 + t) + 0.5 * x * (1.0 - t * t) * (GELU_C * (1.0 + 3.0 * GELU_A * x * x))


def _causal_weights(w_ref, transpose):
    t = lax.broadcasted_iota(jnp.int32, (CHUNK, CHUNK), 0)
    s = lax.broadcasted_iota(jnp.int32, (CHUNK, CHUNK), 1)
    out = []
    for g in range(A_GROUPS):
        w = jnp.where(t >= s, w_ref[g], 0.0)
        out.append((w.T if transpose else w).astype(BF16))
    return out


def _spatial(ws, vn, lo_mask):
    rows = vn.shape[0]
    out_rows = []
    for r in range(rows // CHUNK):
        cols = []
        for j in range(A_GROUPS // 2):
            blk = vn[r * CHUNK : (r + 1) * CHUNK, j * LANES : (j + 1) * LANES]
            za = jnp.dot(ws[2 * j], blk, preferred_element_type=F32)
            zb = jnp.dot(ws[2 * j + 1], blk, preferred_element_type=F32)
            cols.append(jnp.where(lo_mask, za, zb))
        out_rows.append(jnp.concatenate(cols, axis=1))
    return jnp.concatenate(out_rows, axis=0)


def _gate_forward(a, vg, vb, ws, bias, lo_mask):
    u = _gelu(a[:, :D_MODEL])
    v = _gelu(a[:, D_MODEL:])
    vhat, rstd = _layer_norm_hat(v)
    vn = (vhat * vg + vb).astype(BF16)
    z = _spatial(ws, vn, lo_mask) + jnp.concatenate([bias] * (a.shape[0] // CHUNK), axis=0)
    return u, vhat, rstd, vn, z


def gate_fwd(a_pre, vn_g, vn_b, w_s, bias_full, *, name):
    s = a_pre.shape[0]
    tr = GATE_CHUNKS * CHUNK

    def body(a_ref, vg_ref, vb_ref, w_ref, bias_ref, p_ref):
        lo_mask = _lane((CHUNK, LANES)) < A_GROUP_DIM
        ws = _causal_weights(w_ref, transpose=False)
        u, _, _, _, z = _gate_forward(a_ref[...].astype(F32), vg_ref[...], vb_ref[...], ws, bias_ref[...], lo_mask)
        p_ref[...] = (u * z).astype(BF16)

    return pl.pallas_call(
        body,
        name=name,
        grid=(s // tr,),
        in_specs=[
            pl.BlockSpec((tr, 2 * D_MODEL), lambda i: (i, 0)),
            _vec(D_MODEL),
            _vec(D_MODEL),
            pl.BlockSpec((A_GROUPS, CHUNK, CHUNK), lambda i: (0, 0, 0)),
            _vec(D_MODEL, CHUNK),
        ],
        out_specs=pl.BlockSpec((tr, D_MODEL), lambda i: (i, 0)),
        out_shape=jax.ShapeDtypeStruct((s, D_MODEL), BF16),
        compiler_params=_params(("parallel",)),
    )(a_pre, vn_g, vn_b, w_s, bias_full)


def gate_bwd(a_pre, dp, vn_g, vn_b, w_s, bias_full, *, name):
    s = a_pre.shape[0]
    tr = GATE_CHUNKS * CHUNK
    nsteps = s // tr

    def body(a_ref, dp_ref, vg_ref, vb_ref, w_ref, bias_ref, da_ref, dw_ref, dbs_ref, rows_ref, dbias_acc):
        step = pl.program_id(0)
        lo_mask = _lane((CHUNK, LANES)) < A_GROUP_DIM

        @pl.when(step == 0)
        def _():
            dw_ref[...] = jnp.zeros_like(dw_ref)
            rows_ref[...] = jnp.zeros_like(rows_ref)
            dbias_acc[...] = jnp.zeros_like(dbias_acc)

        a = a_ref[...].astype(F32)
        vg = vg_ref[...]
        ws = _causal_weights(w_ref, transpose=False)
        wts = _causal_weights(w_ref, transpose=True)
        u, vhat, rstd, vn, z = _gate_forward(a, vg, vb_ref[...], ws, bias_ref[...], lo_mask)
        dp = dp_ref[...]
        du = dp * z
        dzz = dp * u
        dzz_b = dzz.astype(BF16)
        dvn = _spatial(wts, dzz_b, lo_mask)
        dbias = None
        for r in range(GATE_CHUNKS):
            rs = slice(r * CHUNK, (r + 1) * CHUNK)
            dbias = dzz[rs] if dbias is None else dbias + dzz[rs]
            for j in range(A_GROUPS // 2):
                cs = slice(j * LANES, (j + 1) * LANES)
                dblk = dzz[rs, cs]
                vblk = vn[rs, cs]
                for half in range(2):
                    keep = lo_mask if half == 0 else jnp.logical_not(lo_mask)
                    dm = jnp.where(keep, dblk, 0.0).astype(BF16)
                    dw_ref[2 * j + half] += lax.dot_general(
                        dm, vblk, (((1,), (1,)), ((), ())), preferred_element_type=F32
                    )
        dbias_acc[...] += dbias
        rows_ref[1:2, :D_MODEL] += jnp.sum(dvn * vhat, axis=0, keepdims=True)
        rows_ref[1:2, D_MODEL:] += jnp.sum(dvn, axis=0, keepdims=True)
        dvh = dvn * vg
        m1 = jnp.mean(dvh, axis=-1, keepdims=True)
        m2 = jnp.mean(dvh * vhat, axis=-1, keepdims=True)
        dv = rstd * (dvh - m1 - vhat * m2)
        da_u = du * _gelu_grad(a[:, :D_MODEL])
        da_v = dv * _gelu_grad(a[:, D_MODEL:])
        da_ref[:, :D_MODEL] = da_u.astype(BF16)
        da_ref[:, D_MODEL:] = da_v.astype(BF16)
        rows_ref[0:1, :D_MODEL] += jnp.sum(da_u, axis=0, keepdims=True)
        rows_ref[0:1, D_MODEL:] += jnp.sum(da_v, axis=0, keepdims=True)

        @pl.when(step == nsteps - 1)
        def _():
            t = lax.broadcasted_iota(jnp.int32, (CHUNK, CHUNK), 0)
            sx = lax.broadcasted_iota(jnp.int32, (CHUNK, CHUNK), 1)
            for g in range(A_GROUPS):
                dw_ref[g] = jnp.where(t >= sx, dw_ref[g], 0.0)
            dbs_ref[...] = _reduce_groups(dbias_acc[...])

    return pl.pallas_call(
        body,
        name=name,
        grid=(nsteps,),
        in_specs=[
            pl.BlockSpec((tr, 2 * D_MODEL), lambda i: (i, 0)),
            pl.BlockSpec((tr, D_MODEL), lambda i: (i, 0)),
            _vec(D_MODEL),
            _vec(D_MODEL),
            pl.BlockSpec((A_GROUPS, CHUNK, CHUNK), lambda i: (0, 0, 0)),
            _vec(D_MODEL, CHUNK),
        ],
        out_specs=[
            pl.BlockSpec((tr, 2 * D_MODEL), lambda i: (i, 0)),
            pl.BlockSpec((A_GROUPS, CHUNK, CHUNK), lambda i: (0, 0, 0)),
            _vec(LANES, CHUNK),
            _vec(2 * D_MODEL, SUBLANES),
        ],
        out_shape=[
            jax.ShapeDtypeStruct((s, 2 * D_MODEL), BF16),
            jax.ShapeDtypeStruct((A_GROUPS, CHUNK, CHUNK), F32),
            jax.ShapeDtypeStruct((CHUNK, LANES), F32),
            jax.ShapeDtypeStruct((SUBLANES, 2 * D_MODEL), F32),
        ],
        scratch_shapes=[pltpu.VMEM((CHUNK, D_MODEL), F32)],
        compiler_params=_params(("arbitrary",)),
    )(a_pre, dp, vn_g, vn_b, w_s, bias_full)


def alibi_tables(dilation):
    qi = jnp.arange(SPAN)[:, None]
    ki = jnp.arange(2 * SPAN)[None, :]
    diff = SPAN + qi - ki
    valid = (diff >= 0) & (diff <= SPAN)
    heads = jnp.arange(1, B_HEADS + 1, dtype=F32)
    slopes = jnp.exp2(-8.0 * heads / B_HEADS)
    bias = -slopes[:, None, None] * (dilation * diff).astype(F32)
    bias = jnp.where(valid[None], bias, NEG).reshape(B_HEADS // 2, 2 * SPAN, 2 * SPAN)
    return bias, bias.transpose(0, 2, 1)


def _pair_rows(x, halves):
    return jnp.concatenate([x * halves[0], x * halves[1]], axis=0)


def _pair_column(v, lane, j):
    pick = lambda h: jnp.sum(jnp.where(lane == h, v, 0.0), axis=1, keepdims=True)
    return jnp.concatenate([pick(2 * j), pick(2 * j + 1)], axis=0)


_NT = (((1,), (1,)), ((), ()))


def permute_rows(x, dilation, *, inverse, name, add=None):
    s, w = x.shape
    tile = SPAN * dilation
    nat = pl.BlockSpec((tile, w), lambda i: (i, 0))
    streams = pl.BlockSpec((dilation, SPAN, w), lambda i: (0, i, 0))
    x3 = x.reshape(dilation, s // dilation, w) if inverse else x

    def body(*refs):
        if not inverse:
            x_ref, o_ref = refs
            o_ref[...] = jnp.swapaxes(x_ref[...].reshape(SPAN, dilation, w), 0, 1)
            return
        val = jnp.swapaxes(refs[0][...], 0, 1).reshape(tile, w)
        if add is not None:
            val = val + refs[1][...]
        refs[-1][...] = val

    out = pl.pallas_call(
        body,
        name=name,
        grid=(s // tile,),
        in_specs=([streams] + ([nat] if add is not None else [])) if inverse else [nat],
        out_specs=nat if inverse else streams,
        out_shape=jax.ShapeDtypeStruct((s, w) if inverse else (dilation, s // dilation, w), x.dtype),
        compiler_params=_params(("parallel",)),
    )(*([x3] + ([add] if add is not None else [])))
    return out.reshape(s, w)


def _qkv_specs(nb, clamp):
    def spec(which, prev):
        def index(r, n):
            blk = jnp.maximum(n - 1, 0) if prev else n
            return (r * nb + clamp(blk), which)

        return pl.BlockSpec((SPAN, D_MODEL), index)

    return [spec(0, False), spec(1, True), spec(1, False), spec(2, True), spec(2, False)]


def attn_fwd(qkv_p, pat, *, name):
    _, dilation = B_PATTERNS[pat]
    nb = SEQ // dilation // SPAN
    bias, _ = alibi_tables(dilation)

    def body(q_ref, kp_ref, kc_ref, vp_ref, vc_ref, bias_ref, o_ref, lse_ref):
        n = pl.program_id(1)
        lane = _lane((SPAN, LANES))
        lo_mask = lane < B_HEAD_DIM
        first_prev = jnp.logical_and(n == 0, _lane((2 * SPAN, 2 * SPAN)) < SPAN)
        q = q_ref[...] * jnp.asarray(B_HEAD_DIM**-0.5, BF16)
        kk = jnp.concatenate([kp_ref[...], kc_ref[...]], axis=0)
        vv = jnp.concatenate([vp_ref[...], vc_ref[...]], axis=0)
        halves = (lo_mask.astype(BF16), jnp.logical_not(lo_mask).astype(BF16))
        stats = jnp.zeros((SPAN, LANES), F32)
        for j in range(B_HEADS // 2):
            cs = slice(j * LANES, (j + 1) * LANES)
            sc = lax.dot_general(_pair_rows(q[:, cs], halves), kk[:, cs], _NT, preferred_element_type=F32)
            sc = jnp.where(first_prev, NEG, sc + bias_ref[j])
            m = jnp.max(sc, axis=1, keepdims=True)
            p = jnp.exp(sc - m)
            l = jnp.sum(p, axis=1, keepdims=True)
            acc = jnp.dot(p.astype(BF16), vv[:, cs], preferred_element_type=F32) * (1.0 / l)
            lse_pair = m + jnp.log(l)
            o_ref[:, cs] = jnp.where(lo_mask, acc[:SPAN], acc[SPAN:])
            stats = jnp.where(lane == 2 * j, lse_pair[:SPAN], stats)
            stats = jnp.where(lane == 2 * j + 1, lse_pair[SPAN:], stats)
        lse_ref[...] = stats

    return pl.pallas_call(
        body,
        name=name,
        grid=(dilation, nb),
        in_specs=[
            *_qkv_specs(nb, lambda blk: blk),
            pl.BlockSpec((B_HEADS // 2, 2 * SPAN, 2 * SPAN), lambda r, n: (0, 0, 0)),
        ],
        out_specs=[
            pl.BlockSpec((SPAN, D_MODEL), lambda r, n: (r * nb + n, 0)),
            pl.BlockSpec((SPAN, LANES), lambda r, n: (r * nb + n, 0)),
        ],
        out_shape=[jax.ShapeDtypeStruct((SEQ, D_MODEL), F32), jax.ShapeDtypeStruct((SEQ, LANES), F32)],
        compiler_params=_params(("parallel", "arbitrary")),
    )(qkv_p, qkv_p, qkv_p, qkv_p, qkv_p, bias)


def attn_combine(outs, lses, *, name):
    def body(o0, o1, o2, l0, l1, l2, ob_ref, of_ref, lse_ref):
        ls = [l0[...], l1[...], l2[...]]
        m = jnp.maximum(jnp.maximum(ls[0], ls[1]), ls[2])
        tot = jnp.log(jnp.exp(ls[0] - m) + jnp.exp(ls[1] - m) + jnp.exp(ls[2] - m)) + m
        o = None
        for o_ref, l in zip((o0, o1, o2), ls, strict=True):
            term = _expand_groups(jnp.exp(l - tot)) * o_ref[...]
            o = term if o is None else o + term
        ob_ref[...] = o.astype(BF16)
        of_ref[...] = o
        lse_ref[...] = tot

    return pl.pallas_call(
        body,
        name=name,
        grid=(SEQ // ROW_TILE,),
        in_specs=[_rows(D_MODEL)] * 3 + [_rows(LANES)] * 3,
        out_specs=[_rows(D_MODEL), _rows(D_MODEL), _rows(LANES)],
        out_shape=[
            jax.ShapeDtypeStruct((SEQ, D_MODEL), BF16),
            jax.ShapeDtypeStruct((SEQ, D_MODEL), F32),
            jax.ShapeDtypeStruct((SEQ, LANES), F32),
        ],
        compiler_params=_params(("parallel",)),
    )(*outs, *lses)


def attn_delta(do, o, *, name):
    def body(do_ref, o_ref, dob_ref, dl_ref):
        do_v = do_ref[...]
        dob_ref[...] = do_v.astype(BF16)
        dl_ref[...] = _reduce_groups(do_v * o_ref[...])

    return pl.pallas_call(
        body,
        name=name,
        grid=(SEQ // ROW_TILE,),
        in_specs=[_rows(D_MODEL), _rows(D_MODEL)],
        out_specs=[_rows(D_MODEL), _rows(LANES)],
        out_shape=[jax.ShapeDtypeStruct((SEQ, D_MODEL), BF16), jax.ShapeDtypeStruct((SEQ, LANES), F32)],
        compiler_params=_params(("parallel",)),
    )(do, o)


def attn_bwd(qkv_p, do_p, lse_p, delta_p, pat, *, name):
    _, dilation = B_PATTERNS[pat]
    nb = SEQ // dilation // SPAN
    bias, bias_t = alibi_tables(dilation)
    last = nb - 1
    q_cols, k_cols, v_cols = (slice(i * D_MODEL, (i + 1) * D_MODEL) for i in range(3))

    def body(q_ref, kp_ref, kc_ref, vp_ref, vc_ref, do_ref, lse_ref, dl_ref, bias_ref, biast_ref, out_ref, cq_ref, ck_ref, cv_ref):
        n = pl.program_id(1)

        @pl.when(n == nb)
        def _():
            out_ref[:, q_cols] = cq_ref[...].astype(BF16)
            out_ref[:, k_cols] = ck_ref[...].astype(BF16)
            out_ref[:, v_cols] = cv_ref[...].astype(BF16)

        @pl.when(n == 0)
        def _():
            cq_ref[...] = jnp.zeros_like(cq_ref)
            ck_ref[...] = jnp.zeros_like(ck_ref)
            cv_ref[...] = jnp.zeros_like(cv_ref)

        @pl.when(n < nb)
        def _():
            lane = _lane((SPAN, LANES))
            lo_mask = lane < B_HEAD_DIM
            pair = (2 * SPAN, 2 * SPAN)
            first = n == 0
            prev_key_cols = jnp.logical_and(first, _lane(pair) < SPAN)
            prev_key_rows = jnp.logical_and(first, lax.broadcasted_iota(jnp.int32, pair, 0) < SPAN)
            q = q_ref[...] * jnp.asarray(B_HEAD_DIM**-0.5, BF16)
            kk = jnp.concatenate([kp_ref[...], kc_ref[...]], axis=0)
            vv = jnp.concatenate([vp_ref[...], vc_ref[...]], axis=0)
            do_v = do_ref[...]
            lse_v = lse_ref[...]
            dl_v = dl_ref[...]
            lse_t = lse_v.T
            dl_t = dl_v.T
            halves = (lo_mask.astype(BF16), jnp.logical_not(lo_mask).astype(BF16))
            for j in range(B_HEADS // 2):
                cs = slice(j * LANES, (j + 1) * LANES)
                kp, vp = kk[:, cs], vv[:, cs]
                q2 = _pair_rows(q[:, cs], halves)
                do2 = _pair_rows(do_v[:, cs], halves)
                lse_c, dl_c = _pair_column(lse_v, lane, j), _pair_column(dl_v, lane, j)
                lse_r = jnp.concatenate([lse_t[2 * j : 2 * j + 1], lse_t[2 * j + 1 : 2 * j + 2]], axis=1)
                dl_r = jnp.concatenate([dl_t[2 * j : 2 * j + 1], dl_t[2 * j + 1 : 2 * j + 2]], axis=1)
                sc = lax.dot_general(q2, kp, _NT, preferred_element_type=F32)
                p = jnp.exp(jnp.where(prev_key_cols, NEG, sc + bias_ref[j]) - lse_c)
                dp = lax.dot_general(do2, vp, _NT, preferred_element_type=F32)
                ds = (p * (dp - dl_c)).astype(BF16)
                dq2 = jnp.dot(ds, kp, preferred_element_type=F32)
                sc_t = lax.dot_general(kp, q2, _NT, preferred_element_type=F32)
                p_t = jnp.exp(jnp.where(prev_key_rows, NEG, sc_t + biast_ref[j]) - lse_r)
                dp_t = lax.dot_general(vp, do2, _NT, preferred_element_type=F32)
                ds_t = (p_t * (dp_t - dl_r)).astype(BF16)
                dk_pair = jnp.dot(ds_t, q2, preferred_element_type=F32)
                dv_pair = jnp.dot(p_t.astype(BF16), do2, preferred_element_type=F32)
                oq = slice(j * LANES, (j + 1) * LANES)
                ok = slice(D_MODEL + j * LANES, D_MODEL + (j + 1) * LANES)
                ov = slice(2 * D_MODEL + j * LANES, 2 * D_MODEL + (j + 1) * LANES)
                out_ref[:, oq] = cq_ref[:, cs].astype(BF16)
                out_ref[:, ok] = (ck_ref[:, cs] + dk_pair[:SPAN]).astype(BF16)
                out_ref[:, ov] = (cv_ref[:, cs] + dv_pair[:SPAN]).astype(BF16)
                cq_ref[:, cs] = jnp.where(lo_mask, dq2[:SPAN], dq2[SPAN:]) * (B_HEAD_DIM**-0.5)
                ck_ref[:, cs] = dk_pair[SPAN:]
                cv_ref[:, cs] = dv_pair[SPAN:]

    def clamp(n):
        return jnp.minimum(n, last)

    def row_spec(width):
        return pl.BlockSpec((SPAN, width), lambda r, n: (r * nb + clamp(n), 0))

    return pl.pallas_call(
        body,
        name=name,
        grid=(dilation, nb + 1),
        in_specs=[
            *_qkv_specs(nb, clamp),
            row_spec(D_MODEL),
            row_spec(LANES),
            row_spec(LANES),
            pl.BlockSpec((B_HEADS // 2, 2 * SPAN, 2 * SPAN), lambda r, n: (0, 0, 0)),
            pl.BlockSpec((B_HEADS // 2, 2 * SPAN, 2 * SPAN), lambda r, n: (0, 0, 0)),
        ],
        out_specs=pl.BlockSpec((SPAN, 3 * D_MODEL), lambda r, n: (r * nb + jnp.maximum(n - 1, 0), 0)),
        out_shape=jax.ShapeDtypeStruct((SEQ, 3 * D_MODEL), BF16),
        scratch_shapes=[pltpu.VMEM((SPAN, D_MODEL), F32)] * 3,
        compiler_params=_params(("arbitrary", "arbitrary")),
    )(qkv_p, qkv_p, qkv_p, qkv_p, qkv_p, do_p, lse_p, delta_p, bias, bias_t)


def _position():
    x, y, c = lax.axis_index("x"), lax.axis_index("y"), lax.axis_index("c")
    return x, y, c, 4 * x + 2 * y + c


def _peer(k, x, y, c):
    px = 1 - x if k & 4 else x
    py = 1 - y if k & 2 else y
    pc = 1 - c if k & 1 else c
    return (px, py, pc), 4 * px + 2 * py + pc


def _remote(src, dst, send_sem, recv_sem, device):
    return pltpu.make_async_remote_copy(
        src_ref=src, dst_ref=dst, send_sem=send_sem, recv_sem=recv_sem, device_id=device, device_id_type=MESH
    )


def _silu_bf16(cf):
    return (cf * (1.0 / (1.0 + jnp.exp(-cf)))).astype(BF16)


def ada_exchange(c8, w4, b4, ln8):
    nt, _, ncol = w4.shape

    def body(c8_ref, w_ref, b_ref, ln_ref, cg_ref, lng_ref, mrecv_ref, mloc_ref, send_sems, recv_sems):
        x, y, c, me = _position()
        cg_ref[me] = c8_ref[...]
        lng_ref[me] = ln_ref[...]
        first = []
        for k in range(1, N_DEV):
            dev, _ = _peer(k, x, y, c)
            first.append(_remote(c8_ref, cg_ref.at[me], send_sems.at[0, k], recv_sems.at[0, k], dev))
            first.append(_remote(ln_ref, lng_ref.at[me], send_sems.at[1, k], recv_sems.at[1, k], dev))
        for cp in first:
            cp.start()
        for k in range(1, N_DEV):
            dev, pid = _peer(k, x, y, c)
            _remote(c8_ref, cg_ref.at[pid], send_sems.at[0, k], recv_sems.at[0, k], dev).wait_recv()
            _remote(ln_ref, lng_ref.at[pid], send_sems.at[1, k], recv_sems.at[1, k], dev).wait_recv()
        sc = _silu_bf16(cg_ref[...].reshape(N_DEV * SUBLANES, D_MODEL))
        for t in range(nt):
            mloc_ref[t] = jnp.dot(sc, w_ref[t].astype(BF16), preferred_element_type=F32) + b_ref[t : t + 1, :]

        def group(dev_id):
            return pl.ds(pl.multiple_of(dev_id * SUBLANES, SUBLANES), SUBLANES)

        mrecv_ref[me] = mloc_ref[:, group(me), :]
        second = []
        for k in range(1, N_DEV):
            dev, pid = _peer(k, x, y, c)
            second.append(
                _remote(mloc_ref.at[:, group(pid), :], mrecv_ref.at[me], send_sems.at[2, k], recv_sems.at[2, k], dev)
            )
        for cp in second:
            cp.start()
        for k in range(1, N_DEV):
            dev, pid = _peer(k, x, y, c)
            _remote(
                mloc_ref.at[:, group(pid), :], mrecv_ref.at[pid], send_sems.at[2, k], recv_sems.at[2, k], dev
            ).wait_recv()
        for cp in first + second:
            cp.wait_send()

    return pl.pallas_call(
        body,
        name="ada_exchange",
        in_specs=[VMEM, VMEM, VMEM, VMEM],
        out_specs=[VMEM, VMEM, VMEM],
        out_shape=[
            jax.ShapeDtypeStruct((N_DEV, SUBLANES, D_MODEL), F32),
            jax.ShapeDtypeStruct((N_DEV, SUBLANES, LANES), F32),
            jax.ShapeDtypeStruct((N_DEV, nt, SUBLANES, ncol), F32),
        ],
        scratch_shapes=[
            pltpu.VMEM((nt, N_DEV * SUBLANES, ncol), F32),
            pltpu.SemaphoreType.DMA((3, N_DEV)),
            pltpu.SemaphoreType.DMA((3, N_DEV)),
        ],
        compiler_params=pltpu.CompilerParams(vmem_limit_bytes=VMEM_LIMIT_BYTES),
    )(c8, w4, b4, ln8)


def small_exchange(dmx, flat):
    def body(dmx_ref, flat_ref, dmrecv_ref, red_ref, land_ref, send_sems, recv_sems):
        x, y, c, me = _position()
        dmrecv_ref[me] = dmx_ref[me]
        land_ref[me] = flat_ref[me]
        first = []
        for k in range(1, N_DEV):
            dev, pid = _peer(k, x, y, c)
            first.append(_remote(dmx_ref.at[pid], dmrecv_ref.at[me], send_sems.at[0, k], recv_sems.at[0, k], dev))
            first.append(_remote(flat_ref.at[pid], land_ref.at[me], send_sems.at[1, k], recv_sems.at[1, k], dev))
        for cp in first:
            cp.start()
        for k in range(1, N_DEV):
            dev, pid = _peer(k, x, y, c)
            _remote(dmx_ref.at[pid], dmrecv_ref.at[pid], send_sems.at[0, k], recv_sems.at[0, k], dev).wait_recv()
            _remote(flat_ref.at[pid], land_ref.at[pid], send_sems.at[1, k], recv_sems.at[1, k], dev).wait_recv()
        total = land_ref[0]
        for s in range(1, N_DEV):
            total = total + land_ref[s]
        red_ref[me] = total
        second = []
        for k in range(1, N_DEV):
            dev, _ = _peer(k, x, y, c)
            second.append(_remote(red_ref.at[me], red_ref.at[me], send_sems.at[2, k], recv_sems.at[2, k], dev))
        for cp in second:
            cp.start()
        for k in range(1, N_DEV):
            dev, pid = _peer(k, x, y, c)
            _remote(red_ref.at[pid], red_ref.at[pid], send_sems.at[2, k], recv_sems.at[2, k], dev).wait_recv()
        for cp in first + second:
            cp.wait_send()

    return pl.pallas_call(
        body,
        name="small_exchange",
        in_specs=[VMEM, VMEM],
        out_specs=[VMEM, VMEM],
        out_shape=[jax.ShapeDtypeStruct(dmx.shape, F32), jax.ShapeDtypeStruct(flat.shape, F32)],
        scratch_shapes=[
            pltpu.VMEM(flat.shape, F32),
            pltpu.SemaphoreType.DMA((3, N_DEV)),
            pltpu.SemaphoreType.DMA((3, N_DEV)),
        ],
        compiler_params=pltpu.CompilerParams(vmem_limit_bytes=VMEM_LIMIT_BYTES),
    )(dmx, flat)


HBM = pl.BlockSpec(memory_space=pltpu.HBM)
SEM = pl.BlockSpec(memory_space=pltpu.SEMAPHORE)
EFFECT = pltpu.SideEffectType.DATAFLOW_SIDE_EFFECTING


def _own_slot(me, block):
    land = lax.empty((N_DEV, *block.shape), block.dtype)
    return lax.dynamic_update_slice_in_dim(land, block[None], me, axis=0)


N_CHIP_PEERS = 3


class Gather:
    def __init__(self, shards, lands, after, *, name):
        nt = len(shards)
        self.name = name

        def body(*refs):
            src_refs, land_refs = refs[:nt], refs[nt : 2 * nt]
            send_sems, recv_sems = refs[2 * nt + 1 : 3 * nt + 1], refs[3 * nt + 1 : 4 * nt + 1]
            token = refs[-1]
            x, y, c, me = _position()
            for t in range(nt):
                for k, dev in enumerate(self._targets(x, y, c)):
                    _remote(src_refs[t], land_refs[t].at[me], send_sems[t].at[k], recv_sems[t].at[k], dev).start()
            token[...] = jnp.zeros_like(token)

        outs = pl.pallas_call(
            body,
            name=name + "_start",
            in_specs=[HBM] * (2 * nt) + [ANY],
            out_specs=[SEM] * (2 * nt) + [HBM] * (2 * nt) + [VMEM],
            out_shape=[pltpu.SemaphoreType.DMA((1 + N_CHIP_PEERS,))] * (2 * nt)
            + [pltpu.HBM(a.shape, a.dtype) for a in (*shards, *lands)]
            + [jax.ShapeDtypeStruct((SUBLANES, LANES), F32)],
            input_output_aliases={i: 2 * nt + i for i in range(2 * nt)},
            compiler_params=pltpu.CompilerParams(has_side_effects=EFFECT),
        )(*[pltpu.with_memory_space_constraint(a, pltpu.HBM) for a in (*shards, *lands)], after)
        self.send_sems, self.recv_sems = list(outs[:nt]), list(outs[nt : 2 * nt])
        self.srcs, self.lands = list(outs[2 * nt : 3 * nt]), list(outs[3 * nt : 4 * nt])
        self.token = outs[-1]

    @staticmethod
    def _chips(x, y):
        return [(1 - x, y), (x, 1 - y), (1 - x, 1 - y)]

    @classmethod
    def _targets(cls, x, y, c):
        return [(x, y, 1 - c)] + [(*chip, c) for chip in cls._chips(x, y)]

    def zero(self):
        return self.token[0, 0]

    def wait(self, which, after, *, name):
        n = len(which)

        def slot(px, py, pc):
            return 4 * px + 2 * py + pc

        def pass_body(*refs):
            land_refs, recv_sems = refs[:n], refs[n : 2 * n]
            fwd_send, fwd_recv = refs[3 * n + 1 : 4 * n + 1], refs[4 * n + 1 : 5 * n + 1]
            x, y, c, _ = _position()
            for t in range(n):
                for j, chip in enumerate(self._chips(x, y)):
                    blk = land_refs[t].at[slot(*chip, c)]
                    _remote(blk, blk, fwd_send[t].at[j], recv_sems[t].at[1 + j], (*chip, c)).wait_recv()
                    _remote(blk, blk, fwd_send[t].at[j], fwd_recv[t].at[j], (x, y, 1 - c)).start()

        lands = [self.lands[t] for t in which]
        outs = pl.pallas_call(
            pass_body,
            name=name + "_pass",
            in_specs=[HBM] * n + [SEM] * n + [ANY],
            out_specs=[HBM] * n + [SEM] * (2 * n),
            out_shape=[pltpu.HBM(a.shape, a.dtype) for a in lands] + [pltpu.SemaphoreType.DMA((N_CHIP_PEERS,))] * (2 * n),
            input_output_aliases={i: i for i in range(n)},
            compiler_params=pltpu.CompilerParams(has_side_effects=EFFECT),
        )(*lands, *[self.recv_sems[t] for t in which], after)
        lands, fwd_send, fwd_recv = outs[:n], outs[n : 2 * n], outs[2 * n :]

        def wait_body(*refs):
            src_refs, land_refs = refs[:n], refs[n : 2 * n]
            send_sems, recv_sems = refs[2 * n : 3 * n], refs[3 * n : 4 * n]
            fwd_send, fwd_recv = refs[4 * n : 5 * n], refs[5 * n : 6 * n]
            x, y, c, me = _position()
            sibling = (x, y, 1 - c)
            for t in range(n):
                for k, dev in enumerate(self._targets(x, y, c)):
                    _remote(src_refs[t], land_refs[t].at[me], send_sems[t].at[k], recv_sems[t].at[k], dev).wait_send()
                blk = land_refs[t].at[slot(x, y, 1 - c)]
                _remote(blk, blk, send_sems[t].at[0], recv_sems[t].at[0], sibling).wait_recv()
                for j, chip in enumerate(self._chips(x, y)):
                    sent = land_refs[t].at[slot(*chip, c)]
                    _remote(sent, sent, fwd_send[t].at[j], fwd_recv[t].at[j], sibling).wait_send()
                    got = land_refs[t].at[slot(*chip, 1 - c)]
                    _remote(got, got, fwd_send[t].at[j], fwd_recv[t].at[j], sibling).wait_recv()

        srcs = [self.srcs[t] for t in which]
        outs = pl.pallas_call(
            wait_body,
            name=name,
            in_specs=[HBM] * (2 * n) + [SEM] * (4 * n),
            out_specs=[HBM] * (2 * n),
            out_shape=[pltpu.HBM(a.shape, a.dtype) for a in (*srcs, *lands)],
            input_output_aliases={i: i for i in range(2 * n)},
            compiler_params=pltpu.CompilerParams(has_side_effects=EFFECT),
        )(*srcs, *lands, *[self.send_sems[t] for t in which], *[self.recv_sems[t] for t in which], *fwd_send, *fwd_recv)
        return outs[n:]


class Scatter:
    def __init__(self, srcs, lands, after, *, name):
        self.name = name
        nt = self.nt = len(srcs)
        peers = N_DEV - 1

        def body(*refs):
            src_refs, land_refs = refs[:nt], refs[nt : 2 * nt]
            send_sems, recv_sems = refs[2 * nt + 1 : 3 * nt + 1], refs[3 * nt + 1 : 4 * nt + 1]
            token = refs[-1]
            x, y, c, me = _position()
            for t in range(nt):
                for k in range(1, N_DEV):
                    dev, pid = _peer(k, x, y, c)
                    src = src_refs[t].at[pid]
                    _remote(src, land_refs[t].at[me], send_sems[t].at[k - 1], recv_sems[t].at[k - 1], dev).start()
            token[...] = jnp.zeros_like(token)

        outs = pl.pallas_call(
            body,
            name=name + "_start",
            in_specs=[HBM] * (2 * nt) + [ANY],
            out_specs=[SEM] * (2 * nt) + [HBM] * (2 * nt) + [VMEM],
            out_shape=[pltpu.SemaphoreType.DMA((peers,))] * (2 * nt)
            + [pltpu.HBM(a.shape, a.dtype) for a in (*srcs, *lands)]
            + [jax.ShapeDtypeStruct((SUBLANES, LANES), F32)],
            input_output_aliases={i: 2 * nt + i for i in range(2 * nt)},
            compiler_params=pltpu.CompilerParams(has_side_effects=EFFECT),
        )(*[pltpu.with_memory_space_constraint(a, pltpu.HBM) for a in (*srcs, *lands)], after)
        self.send_sems, self.recv_sems = outs[:nt], outs[nt : 2 * nt]
        self.srcs, self.lands = outs[2 * nt : 3 * nt], outs[3 * nt : 4 * nt]
        self.token = outs[-1]

    def zero(self):
        return self.token[0, 0]

    def wait(self, which, after, *, name):
        n = len(which)

        def body(*refs):
            src_refs, land_refs = refs[:n], refs[n : 2 * n]
            send_sems, recv_sems = refs[2 * n : 3 * n], refs[3 * n : 4 * n]
            x, y, c, _ = _position()
            for t in range(n):
                for k in range(1, N_DEV):
                    dev, pid = _peer(k, x, y, c)
                    src = src_refs[t].at[pid]
                    cp = _remote(src, land_refs[t].at[pid], send_sems[t].at[k - 1], recv_sems[t].at[k - 1], dev)
                    cp.wait_send()
                    cp.wait_recv()

        srcs = [self.srcs[t] for t in which]
        lands = [self.lands[t] for t in which]
        outs = pl.pallas_call(
            body,
            name=name,
            in_specs=[HBM] * (2 * n) + [SEM] * (2 * n) + [ANY],
            out_specs=[HBM] * (2 * n),
            out_shape=[pltpu.HBM(a.shape, a.dtype) for a in (*srcs, *lands)],
            input_output_aliases={i: i for i in range(2 * n)},
            compiler_params=pltpu.CompilerParams(has_side_effects=EFFECT),
        )(*srcs, *lands, *[self.send_sems[t] for t in which], *[self.recv_sems[t] for t in which], after)
        return outs[n:]


def _adam_update(g, w, m, v):
    m2 = ADAM_B1 * m + (1.0 - ADAM_B1) * g
    v2 = ADAM_B2 * v + (1.0 - ADAM_B2) * jnp.square(g)
    m_hat = m2 / (1.0 - ADAM_B1**ADAM_STEP)
    v_hat = v2 / (1.0 - ADAM_B2**ADAM_STEP)
    delta = -ADAM_LR * (m_hat / (jnp.sqrt(v_hat) + ADAM_EPS) + ADAM_WD * w)
    return delta, m2, v2


def adamw(gparts, w, m, v, *, name):
    p, r, c = gparts.shape
    tr = r if r <= 256 else (256 if c <= D_MODEL else 128)

    def body(g_ref, w_ref, m_ref, v_ref, go_ref, d_ref, mo_ref, vo_ref):
        g = g_ref[0].astype(F32)
        for i in range(1, p):
            g = g + g_ref[i].astype(F32)
        delta, m2, v2 = _adam_update(g, w_ref[...], m_ref[...], v_ref[...])
        go_ref[...] = g
        d_ref[...] = delta
        mo_ref[...] = m2
        vo_ref[...] = v2

    blk = pl.BlockSpec((tr, c), lambda i: (i, 0))
    return pl.pallas_call(
        body,
        name=name,
        grid=(r // tr,),
        in_specs=[pl.BlockSpec((p, tr, c), lambda i: (0, i, 0)), blk, blk, blk],
        out_specs=[blk] * 4,
        out_shape=[jax.ShapeDtypeStruct((r, c), F32)] * 4,
        compiler_params=_params(("parallel",)),
    )(gparts, w, m, v)


def ada_grad_adamw(cg, dmrecv, w4, m4, v4, *, name):
    nt, k, ncol = w4.shape

    def body(cg_ref, dm_ref, w_ref, m_ref, v_ref, go_ref, d_ref, mo_ref, vo_ref, gb_ref):
        sc = _silu_bf16(cg_ref[...].reshape(N_DEV * SUBLANES, k))
        dm = dm_ref[...].reshape(N_DEV * SUBLANES, ncol)
        g = lax.dot_general(sc, dm.astype(BF16), (((0,), (0,)), ((), ())), preferred_element_type=F32)
        delta, m2, v2 = _adam_update(g, w_ref[...], m_ref[...], v_ref[...])
        go_ref[...] = g
        d_ref[...] = delta
        mo_ref[...] = m2
        vo_ref[...] = v2
        gb_ref[...] = jnp.broadcast_to(jnp.sum(dm, axis=0, keepdims=True), (SUBLANES, ncol))

    wblk = pl.BlockSpec((None, k, ncol), lambda t: (t, 0, 0))
    return pl.pallas_call(
        body,
        name=name,
        grid=(nt,),
        in_specs=[
            pl.BlockSpec((N_DEV, SUBLANES, k), lambda t: (0, 0, 0)),
            pl.BlockSpec((N_DEV, None, SUBLANES, ncol), lambda t: (0, t, 0, 0)),
            wblk,
            wblk,
            wblk,
        ],
        out_specs=[wblk] * 4 + [pl.BlockSpec((None, SUBLANES, ncol), lambda t: (t, 0, 0))],
        out_shape=[jax.ShapeDtypeStruct((nt, k, ncol), F32)] * 4 + [jax.ShapeDtypeStruct((nt, SUBLANES, ncol), F32)],
        compiler_params=_params(("parallel",)),
    )(cg, dmrecv, w4, m4, v4)


def kernel(x, c, ada_w, ada_b, ln_g, ln_b, a_w_in, a_b_in, a_vn_g, a_vn_b, a_w_s, a_b_s, a_w_out, b_w_qkv, b_w_out, mlp_w_up, mlp_w_down, loss_target, m_ada_w, m_ada_b, m_ln_g, m_ln_b, m_a_w_in, m_a_b_in, m_a_vn_g, m_a_vn_b, m_a_w_s, m_a_b_s, m_a_w_out, m_b_w_qkv, m_b_w_out, m_mlp_w_up, m_mlp_w_down, v_ada_w, v_ada_b, v_ln_g, v_ln_b, v_a_w_in, v_a_b_in, v_a_vn_g, v_a_vn_b, v_a_w_s, v_a_b_s, v_a_w_out, v_b_w_qkv, v_b_w_out, v_mlp_w_up, v_mlp_w_down):
    x0 = x[0]
    target = loss_target[0]
    me = 4 * lax.axis_index("x") + 2 * lax.axis_index("y") + lax.axis_index("c")

    ada_w4 = ada_w.reshape(N_SUB, D_MODEL, -1)
    ada_b4 = ada_b.reshape(N_SUB, -1)
    ln8 = jnp.concatenate([ln_g.reshape(N_SUB, -1), ln_b.reshape(N_SUB, -1)], axis=0)
    c8 = jnp.broadcast_to(c, (SUBLANES, D_MODEL))
    cg, lng, mrecv = ada_exchange(c8, ada_w4, ada_b4, ln8)

    W_IN, W_AOUT, W_UP0, W_DN0, W_QKV, W_BOUT, W_UP1, W_DN1 = range(8)
    shards = [
        a_w_in[0].astype(BF16),
        a_w_out[0].astype(BF16),
        mlp_w_up[0].astype(BF16),
        mlp_w_down[0].astype(BF16),
        b_w_qkv[0].astype(BF16),
        b_w_out[0].astype(BF16),
        mlp_w_up[1].astype(BF16),
        mlp_w_down[1].astype(BF16),
    ]
    gather = Gather(shards, [_own_slot(me, s) for s in shards], mrecv, name="gather")

    modv = mrecv[:, :, 0, :].transpose(1, 0, 2).reshape(N_SUB, 3 * D_MODEL) + gather.zero()
    shift = [modv[t : t + 1, :D_MODEL] for t in range(N_SUB)]
    scale = [modv[t : t + 1, D_MODEL : 2 * D_MODEL] for t in range(N_SUB)]
    gate1 = [1.0 + modv[t : t + 1, 2 * D_MODEL :] for t in range(N_SUB)]
    lng_full = [lng[:, t, :].reshape(1, D_MODEL) for t in range(N_SUB)]
    lnb_full = [lng[:, N_SUB + t, :].reshape(1, D_MODEL) for t in range(N_SUB)]

    ident = lambda acc: (acc,)
    relu2 = lambda a: jnp.square(jnp.maximum(a.astype(F32), 0.0)).astype(BF16)
    vn_g, vn_b, w_s = a_vn_g, a_vn_b, a_w_s[0]
    bias_full = jnp.repeat(a_b_s[0].T, A_GROUP_DIM, axis=1)
    w_up3, w_dn3 = [None, None], [None, None]

    def mlp_forward(i, h, after):
        up, dn = gather.wait([W_UP0, W_DN0] if i == 0 else [W_UP1, W_DN1], after, name=f"gather_wait_mlp{i}")
        w_up3[i], w_dn3[i] = up, dn.reshape(1, D_FF, D_MODEL)
        (a,) = mm_nn(h, w_up3[i], name=f"mlp{i}_up", tm=2048, ps=2, tn=512, tk=D_MODEL, epilogue=ident, outs=(BF16,))
        (y,) = mm_nn(
            a, w_dn3[i], name=f"mlp{i}_down", tm=2048, ps=1, tn=D_MODEL, tk=1024, prologue=relu2, epilogue=ident, outs=(F32,)
        )
        return a, y

    h0 = modulate(x0, scale[0], shift[0], name="modulate0")
    w_in3, w_aout3 = gather.wait([W_IN, W_AOUT], h0, name="gather_wait_a")
    w_aout3 = w_aout3.reshape(1, D_MODEL, D_MODEL)
    (a_pre,) = mm_nn(
        h0, w_in3, name="a_in", tm=2048, ps=4, tn=256, tk=D_MODEL, epilogue=lambda acc, b: (acc + b,),
        extras=[(a_b_in, "row")], outs=(BF16,),
    )
    p_gate = gate_fwd(a_pre, vn_g, vn_b, w_s, bias_full, name="gate_fwd")
    (y0,) = mm_nn(p_gate, w_aout3, name="a_out", tm=2048, ps=1, tn=D_MODEL, tk=D_MODEL, epilogue=ident, outs=(F32,))
    x1, h1 = residual_ln(x0, y0, gate1[0], lng_full[0], lnb_full[0], scale[1], shift[1], name="res_ln0")
    a1, y1 = mlp_forward(0, h1, y0)
    x2, h2 = residual_ln(x1, y1, gate1[1], lng_full[1], lnb_full[1], scale[2], shift[2], name="res_ln1")
    w_qkv_shards, w_bout3 = gather.wait([W_QKV, W_BOUT], y1, name="gather_wait_b")
    w_bout3 = w_bout3.reshape(1, D_MODEL, D_MODEL)
    w_qkv_cols = w_qkv_shards.transpose(1, 0, 2).reshape(D_MODEL, D_QKV)
    w_pat = [w_qkv_cols[None, :, 3 * D_MODEL * g : 3 * D_MODEL * (g + 1)] for g in range(N_PAT)]
    dil = [d for _, d in B_PATTERNS]
    h2_p = [h2] + [permute_rows(h2, dil[g], inverse=False, name=f"perm_h{g}") for g in range(1, N_PAT)]
    qkv_p, pat_o, pat_lse = [], [], []
    for g in range(N_PAT):
        (qkv_g,) = mm_nn(
            h2_p[g], w_pat[g], name=f"b_qkv{g}", tm=2048, ps=1, tn=D_MODEL, tk=D_MODEL, epilogue=ident, outs=(BF16,)
        )
        o_g, lse_g = attn_fwd(qkv_g, g, name=f"attn_fwd{g}")
        if g > 0:
            o_g = permute_rows(o_g, dil[g], inverse=True, name=f"unperm_o{g}")
            lse_g = permute_rows(lse_g, dil[g], inverse=True, name=f"unperm_lse{g}")
        qkv_p.append(qkv_g)
        pat_o.append(o_g)
        pat_lse.append(lse_g)
    o_b, o_f, lse = attn_combine(pat_o, pat_lse, name="attn_combine")
    (y2,) = mm_nn(o_b, w_bout3, name="b_out", tm=2048, ps=1, tn=D_MODEL, tk=D_MODEL, epilogue=ident, outs=(F32,))
    x3, h3 = residual_ln(x2, y2, gate1[2], lng_full[2], lnb_full[2], scale[3], shift[3], name="res_ln2")
    a3, y3 = mlp_forward(1, h3, y2)
    dxo, loss_local = residual_ln_loss(x3, y3, gate1[3], lng_full[3], lnb_full[3], target, name="res_ln3_loss")
    loss = lax.psum(loss_local[0, 0], ("x", "y", "c"))

    def scatter(parts, after, name):
        parts = [p.reshape(N_DEV, -1, p.shape[-1]) for p in parts]
        lands = [_own_slot(me, lax.dynamic_index_in_dim(p, me, 0, keepdims=False)) for p in parts]
        return Scatter(parts, lands, after, name=name)

    def mlp_backward(i, h, a, dy):
        (da,) = mm_nt(
            dy,
            w_dn3[i],
            name=f"mlp{i}_da",
            tm=2048,
            tko=1024,
            ps=1,
            tc=D_MODEL,
            epilogue=lambda acc, act: (acc * (2.0 * jnp.maximum(act.astype(F32), 0.0)),),
            extras=[(a, "full")],
            outs=(BF16,),
        )
        (dh,) = mm_nt(da, w_up3[i], name=f"mlp{i}_dh", tm=2048, tko=1024, ps=2, tc=512, epilogue=ident, outs=(F32,))
        dw_dn = mm_tn(
            a, dy, name=f"mlp{i}_dw_down", p=1, tk=1024, ps=1, tn=D_MODEL, tmc=2048, prologue=relu2, out_dtype=BF16
        )
        dw_up = mm_tn(h, da, name=f"mlp{i}_dw_up", p=N_DEV, tk=1024, ps=2, tn=512, tmc=2048, out_dtype=BF16)
        return scatter([dw_up, dw_dn], dh, f"scatter_mlp{i}"), dh

    dz3, dy3, st3 = residual_ln_bwd(name="res_bwd3", dxo=dxo, this=(x3, y3, gate1[3], lng_full[3]))
    rs_mlp1, dh3 = mlp_backward(1, h3, a3, dy3)
    dz2, dy2, st2 = residual_ln_bwd(
        name="res_bwd2", later=(dh3, dz3, x3, scale[3] + rs_mlp1.zero()), this=(x2, y2, gate1[2], lng_full[2])
    )
    (d_o,) = mm_nt(dy2, w_bout3, name="b_do", tm=2048, tko=1024, ps=1, tc=D_MODEL, epilogue=ident, outs=(F32,))
    do_b, delta = attn_delta(d_o, o_f, name="attn_delta")
    dh2, dw_pat = None, []
    for g in range(N_PAT):
        do_g, lse_g, delta_g = do_b, lse, delta
        if g > 0:
            do_g = permute_rows(do_b, dil[g], inverse=False, name=f"perm_do{g}")
            lse_g = permute_rows(lse, dil[g], inverse=False, name=f"perm_lse{g}")
            delta_g = permute_rows(delta, dil[g], inverse=False, name=f"perm_delta{g}")
        dqkv_g = attn_bwd(qkv_p[g], do_g, lse_g, delta_g, g, name=f"attn_bwd{g}")
        (dh_g,) = mm_nt(
            dqkv_g, w_pat[g], name=f"b_dh{g}", tm=2048, tko=1024, ps=1, tc=D_MODEL, epilogue=ident, outs=(F32,)
        )
        dh2 = dh_g if g == 0 else permute_rows(dh_g, dil[g], inverse=True, add=dh2, name=f"unperm_dh{g}")
        dw_pat.append(
            mm_tn(h2_p[g], dqkv_g, name=f"b_dw_qkv{g}", p=1, tk=1024, ps=1, tn=D_MODEL, tmc=2048, out_dtype=BF16)[0]
        )
    dw_bout = mm_tn(o_b, dy2, name="b_dw_out", p=1, tk=1024, ps=1, tn=D_MODEL, tmc=2048, out_dtype=BF16)
    dw_qkv = jnp.concatenate(dw_pat, axis=1).reshape(D_MODEL, N_DEV, -1).transpose(1, 0, 2)
    rs_b = scatter([dw_qkv, dw_bout], dh2, "scatter_b")
    dz1, dy1, st1 = residual_ln_bwd(
        name="res_bwd1", later=(dh2, dz2, x2, scale[2] + rs_b.zero()), this=(x1, y1, gate1[1], lng_full[1])
    )
    rs_mlp0, dh1 = mlp_backward(0, h1, a1, dy1)
    dz0, dy0, st0 = residual_ln_bwd(
        name="res_bwd0", later=(dh1, dz1, x1, scale[1] + rs_mlp0.zero()), this=(x0, y0, gate1[0], lng_full[0])
    )
    (dp_gate,) = mm_nt(dy0, w_aout3, name="a_dp", tm=2048, tko=1024, ps=1, tc=D_MODEL, epilogue=ident, outs=(F32,))
    da0, d_ws, d_bs, gate_rows = gate_bwd(a_pre, dp_gate, vn_g, vn_b, w_s, bias_full, name="gate_bwd")
    (dh0,) = mm_nt(da0, w_in3, name="a_dh", tm=2048, tko=1024, ps=4, tc=256, epilogue=ident, outs=(F32,))
    dw_aout = mm_tn(p_gate, dy0, name="a_dw_out", p=1, tk=1024, ps=1, tn=D_MODEL, tmc=2048, out_dtype=BF16)
    dw_in = mm_tn(h0, da0, name="a_dw_in", p=N_DEV, tk=1024, ps=4, tn=256, tmc=2048, out_dtype=BF16)
    grad_x, stf = residual_ln_bwd(name="res_bwd_in", later=(dh0, dz0, x0, scale[0]))

    stats_after = [stf, st0, st1, st2]
    stats_own = [st0, st1, st2, st3]
    dm = jnp.stack(
        [
            jnp.concatenate(
                [stats_after[t][ST_DSHIFT], stats_after[t][ST_DSCALE], stats_own[t][ST_DGATE]], axis=0
            )
            for t in range(N_SUB)
        ]
    )
    ncol = 3 * D_MODEL // N_DEV
    dmx = jnp.pad(
        dm.reshape(N_SUB, N_DEV, ncol).transpose(1, 0, 2)[:, :, None, :], ((0, 0), (0, 0), (0, SUBLANES - 1), (0, 0))
    )
    small = [
        gate_rows[0],
        gate_rows[1],
        d_ws.reshape(-1),
        d_bs[:, :A_GROUPS].T.reshape(-1),
        *[stats_own[t][ST_DG] for t in range(N_SUB)],
        *[stats_own[t][ST_DB] for t in range(N_SUB)],
    ]
    n_small = sum(s.size for s in small)
    part_rows = -(-n_small // (N_DEV * LANES * SUBLANES)) * SUBLANES
    flat = jnp.concatenate(small + [jnp.zeros((N_DEV * part_rows * LANES - n_small,), F32)])
    dmrecv, reduced = small_exchange(dmx, flat.reshape(N_DEV, part_rows, LANES))
    reduced = reduced.reshape(-1)
    sizes = [2 * D_MODEL, D_MODEL, D_MODEL, A_GROUPS * CHUNK * CHUNK, A_GROUPS * CHUNK, N_SUB * D_MODEL, N_SUB * D_MODEL]
    offs = [sum(sizes[:i]) for i in range(len(sizes) + 1)]
    g_b_in, g_vn_g, g_vn_b, g_ws, g_bs, g_lng, g_lnb = [reduced[offs[i] : offs[i + 1]] for i in range(len(sizes))]

    results = {}

    def update(wname, gparts, w, m, v):
        shape = w.shape
        w2 = w.reshape(-1, shape[-1])
        outs = adamw(gparts.reshape(gparts.shape[0], *w2.shape), w2, m.reshape(w2.shape), v.reshape(w2.shape), name=f"adamw_{wname}")
        results[wname] = [o.reshape(shape) for o in outs]

    rs_a = scatter([dw_in, dw_aout], reduced, "scatter_a")
    ada_outs = ada_grad_adamw(cg + rs_a.zero(), dmrecv, ada_w4, m_ada_w.reshape(ada_w4.shape), v_ada_w.reshape(ada_w4.shape), name="ada_grad_adamw")
    results["ada_w"] = [o.reshape(ada_w.shape) for o in ada_outs[:4]]
    update("ada_b", ada_outs[4][:, 0, :][None], ada_b, m_ada_b, v_ada_b)
    ln_cols = D_MODEL // N_DEV
    my_ln = lambda gfull: lax.dynamic_slice_in_dim(gfull.reshape(N_SUB, N_DEV, ln_cols), me, 1, axis=1)
    update("ln_g", my_ln(g_lng).reshape(1, N_SUB, ln_cols), ln_g, m_ln_g, v_ln_g)
    update("ln_b", my_ln(g_lnb).reshape(1, N_SUB, ln_cols), ln_b, m_ln_b, v_ln_b)
    update("a_b_in", g_b_in[None], a_b_in, m_a_b_in, v_a_b_in)
    update("a_vn_g", g_vn_g[None], a_vn_g, m_a_vn_g, v_a_vn_g)
    update("a_vn_b", g_vn_b[None], a_vn_b, m_a_vn_b, v_a_vn_b)
    update("a_w_s", g_ws[None], a_w_s, m_a_w_s, v_a_w_s)
    update("a_b_s", g_bs[None], a_b_s, m_a_b_s, v_a_b_s)
    g_up1, g_dn1 = rs_mlp1.wait([0, 1], grad_x, name="scatter_wait_mlp1")
    g_qkv, g_bout = rs_b.wait([0, 1], grad_x, name="scatter_wait_b")
    update("b_w_qkv", g_qkv, b_w_qkv, m_b_w_qkv, v_b_w_qkv)
    update("b_w_out", g_bout, b_w_out, m_b_w_out, v_b_w_out)
    g_up0, g_dn0 = rs_mlp0.wait([0, 1], grad_x, name="scatter_wait_mlp0")
    update("mlp_w_up", jnp.concatenate([g_up0, g_up1], axis=1), mlp_w_up, m_mlp_w_up, v_mlp_w_up)
    update("mlp_w_down", jnp.concatenate([g_dn0, g_dn1], axis=1), mlp_w_down, m_mlp_w_down, v_mlp_w_down)
    g_in, g_aout = rs_a.wait([0, 1], grad_x, name="scatter_wait_a")
    update("a_w_in", g_in, a_w_in, m_a_w_in, v_a_w_in)
    update("a_w_out", g_aout, a_w_out, m_a_w_out, v_a_w_out)

    order = ["ada_w", "ada_b", "ln_g", "ln_b", "a_w_in", "a_b_in", "a_vn_g", "a_vn_b", "a_w_s", "a_b_s", "a_w_out", "b_w_qkv", "b_w_out", "mlp_w_up", "mlp_w_down"]
    return (loss, grad_x[None], *[results[n][0] for n in order], *[results[n][1] for n in order],
            *[results[n][2] for n in order], *[results[n][3] for n in order])
```

```python
import math

import jax
import jax.numpy as jnp
from jax import lax
from jax.experimental import pallas as pl
from jax.experimental.pallas import tpu as pltpu

F32 = jnp.float32
BF16 = jnp.bfloat16
MESH = pl.DeviceIdType.MESH
ANY = pl.BlockSpec(memory_space=pl.ANY)
VMEM = pl.BlockSpec(memory_space=pltpu.VMEM)

N_DEV = 8
D_MODEL = 1024
SEQ = 4096
DEPTH = 2
CHUNK = 128
A_GROUPS = 16
A_GROUP_DIM = D_MODEL // A_GROUPS
B_HEADS = 16
B_HEAD_DIM = 64
B_PATTERNS = ((128, 1), (512, 4), (2048, 16))
N_PAT = len(B_PATTERNS)
SPAN = 128
D_FF = 4 * D_MODEL
D_QKV = N_PAT * 3 * D_MODEL
ALPHA = (2 * DEPTH) ** 0.25
LN_EPS = 1e-5
NEG = -1e30
ADAM_LR = 0.001
ADAM_B1 = 0.9
ADAM_B2 = 0.999
ADAM_EPS = 1e-08
ADAM_WD = 0.01
ADAM_STEP = 10
GELU_C = math.sqrt(2.0 / math.pi)
GELU_A = 0.044715

VMEM_LIMIT_BYTES = 56 * 1024 * 1024
LANES = 128
SUBLANES = 8
ROW_TILE = 512
N_SUB = 2 * DEPTH


def _params(sem):
    return pltpu.CompilerParams(dimension_semantics=sem, vmem_limit_bytes=VMEM_LIMIT_BYTES)


def _lane(shape):
    return lax.broadcasted_iota(jnp.int32, shape, len(shape) - 1)


def _split_bf16(x):
    hi = x.astype(BF16)
    lo = (x - hi.astype(F32)).astype(BF16)
    return hi, lo


def _group_expand_matrix(groups_padded, width):
    per = width // A_GROUPS
    r = lax.broadcasted_iota(jnp.int32, (groups_padded, width), 0)
    c = lax.broadcasted_iota(jnp.int32, (groups_padded, width), 1)
    return (c // per == r).astype(BF16)


def _group_reduce_matrix(width, groups_padded):
    per = width // A_GROUPS
    r = lax.broadcasted_iota(jnp.int32, (width, groups_padded), 0)
    c = lax.broadcasted_iota(jnp.int32, (width, groups_padded), 1)
    return (r // per == c).astype(BF16)


def _expand_groups(w):
    e = _group_expand_matrix(LANES, D_MODEL)
    hi, lo = _split_bf16(w)
    return jnp.dot(hi, e, preferred_element_type=F32) + jnp.dot(lo, e, preferred_element_type=F32)


def _reduce_groups(x):
    e = _group_reduce_matrix(D_MODEL, LANES)
    hi, lo = _split_bf16(x)
    return jnp.dot(hi, e, preferred_element_type=F32) + jnp.dot(lo, e, preferred_element_type=F32)


def _column_tiles(p, n, ps, tn):
    assert (ps == 1 or tn == n) and p % ps == 0 and n % tn == 0
    q = n // tn
    return (p // ps) * q, q


def _extra_specs(extras, tm, width):
    specs = []
    for _, kind in extras:
        if kind == "row":
            specs.append(pl.BlockSpec((1, width), lambda i, j, c: (0, j)))
        else:
            specs.append(pl.BlockSpec((tm, width), lambda i, j, c: (i, j)))
    return specs


def mm_nn(a, b3, *, name, tm, ps, tn, tk, epilogue, extras=(), outs, prologue=None):
    m, k = a.shape
    p, _, n = b3.shape
    nj, q = _column_tiles(p, n, ps, tn)
    nk = k // tk
    width = ps * tn

    def body(a_ref, b_ref, *rest):
        ex = rest[: len(extras)]
        out_refs = rest[len(extras) : len(extras) + len(outs)]
        kk = pl.program_id(2)
        av = a_ref[...] if prologue is None else prologue(a_ref[...])

        def finish(cs, acc):
            res = epilogue(acc, *[e[:, cs] for e in ex])
            for o_ref, r in zip(out_refs, res, strict=True):
                o_ref[:, cs] = r.astype(o_ref.dtype)

        for s in range(ps):
            cs = slice(s * tn, (s + 1) * tn)
            part = jnp.dot(av, b_ref[s], preferred_element_type=F32)
            if nk == 1:
                finish(cs, part)
                continue
            acc_ref = rest[-1]

            @pl.when(kk == 0)
            def _(part=part, cs=cs):
                acc_ref[:, cs] = part

            @pl.when(kk > 0)
            def _(part=part, cs=cs):
                acc_ref[:, cs] += part

        if nk > 1:

            @pl.when(kk == nk - 1)
            def _():
                for s in range(ps):
                    cs = slice(s * tn, (s + 1) * tn)
                    finish(cs, rest[-1][:, cs])

    return pl.pallas_call(
        body,
        name=name,
        grid=(m // tm, nj, nk),
        in_specs=[
            pl.BlockSpec((tm, tk), lambda i, j, kk: (i, kk)),
            pl.BlockSpec((ps, tk, tn), lambda i, j, kk: (j // q, kk, j % q)),
            *_extra_specs(extras, tm, width),
        ],
        out_specs=[pl.BlockSpec((tm, width), lambda i, j, kk: (i, j)) for _ in outs],
        out_shape=[jax.ShapeDtypeStruct((m, p * n), dt) for dt in outs],
        scratch_shapes=[pltpu.VMEM((tm, width), F32)] if nk > 1 else [],
        compiler_params=_params(("parallel", "parallel", "arbitrary")),
    )(a, b3, *[arr for arr, _ in extras])


def mm_nt(g, b3, *, name, tm, tko, ps, tc, epilogue, extras=(), outs):
    m = g.shape[0]
    p, k, n = b3.shape
    nc, q = _column_tiles(p, n, ps, tc)

    def body(g_ref, b_ref, *rest):
        ex = rest[: len(extras)]
        out_refs = rest[len(extras) : len(extras) + len(outs)]
        c = pl.program_id(2)
        part = None
        for s in range(ps):
            d = lax.dot_general(
                g_ref[:, s * tc : (s + 1) * tc], b_ref[s], (((1,), (1,)), ((), ())), preferred_element_type=F32
            )
            part = d if part is None else part + d

        def finish(acc):
            res = epilogue(acc, *[e[...] for e in ex])
            for o_ref, r in zip(out_refs, res, strict=True):
                o_ref[...] = r.astype(o_ref.dtype)

        if nc == 1:
            finish(part)
            return
        acc_ref = rest[-1]

        @pl.when(c == 0)
        def _():
            acc_ref[...] = part

        @pl.when(c > 0)
        def _():
            acc_ref[...] += part

        @pl.when(c == nc - 1)
        def _():
            finish(acc_ref[...])

    return pl.pallas_call(
        body,
        name=name,
        grid=(m // tm, k // tko, nc),
        in_specs=[
            pl.BlockSpec((tm, ps * tc), lambda i, j, c: (i, c)),
            pl.BlockSpec((ps, tko, tc), lambda i, j, c: (c // q, j, c % q)),
            *_extra_specs(extras, tm, tko),
        ],
        out_specs=[pl.BlockSpec((tm, tko), lambda i, j, c: (i, j)) for _ in outs],
        out_shape=[jax.ShapeDtypeStruct((m, k), dt) for dt in outs],
        scratch_shapes=[pltpu.VMEM((tm, tko), F32)] if nc > 1 else [],
        compiler_params=_params(("parallel", "parallel", "arbitrary")),
    )(g, b3, *[arr for arr, _ in extras])


def mm_tn(a, g, *, name, p, tk, ps, tn, tmc, out_dtype, prologue=None):
    m, k = a.shape
    n = g.shape[1] // p
    nj, q = _column_tiles(p, n, ps, tn)
    nc = m // tmc

    def body(a_ref, g_ref, o_ref, acc_ref):
        c = pl.program_id(2)
        av = a_ref[...] if prologue is None else prologue(a_ref[...])
        for s in range(ps):
            part = lax.dot_general(
                av, g_ref[:, s * tn : (s + 1) * tn], (((0,), (0,)), ((), ())), preferred_element_type=F32
            )

            @pl.when(c == 0)
            def _(part=part, s=s):
                acc_ref[s] = part

            @pl.when(c > 0)
            def _(part=part, s=s):
                acc_ref[s] += part

        @pl.when(c == nc - 1)
        def _():
            o_ref[...] = acc_ref[...].astype(o_ref.dtype)

    return pl.pallas_call(
        body,
        name=name,
        grid=(k // tk, nj, nc),
        in_specs=[
            pl.BlockSpec((tmc, tk), lambda i, j, c: (c, i)),
            pl.BlockSpec((tmc, ps * tn), lambda i, j, c: (c, j)),
        ],
        out_specs=pl.BlockSpec((ps, tk, tn), lambda i, j, c: (j // q, i, j % q)),
        out_shape=jax.ShapeDtypeStruct((p, k, n), out_dtype),
        scratch_shapes=[pltpu.VMEM((ps, tk, tn), F32)],
        compiler_params=_params(("parallel", "parallel", "arbitrary")),
    )(a, g)


def _rows(cols):
    return pl.BlockSpec((ROW_TILE, cols), lambda i: (i, 0))


def _vec(cols, rows=1):
    return pl.BlockSpec((rows, cols), lambda i: (0, 0))


def _layer_norm_hat(z):
    mu = jnp.mean(z, axis=-1, keepdims=True)
    zc = z - mu
    var = jnp.mean(zc * zc, axis=-1, keepdims=True)
    rstd = lax.rsqrt(var + LN_EPS)
    return zc * rstd, rstd


def modulate(x, scale, shift, *, name):
    s, d = x.shape

    def body(x_ref, sc_ref, sh_ref, h_ref):
        h_ref[...] = (x_ref[...] * (1.0 + sc_ref[...]) + sh_ref[...]).astype(BF16)

    return pl.pallas_call(
        body,
        name=name,
        grid=(s // ROW_TILE,),
        in_specs=[_rows(d), _vec(d), _vec(d)],
        out_specs=_rows(d),
        out_shape=jax.ShapeDtypeStruct((s, d), BF16),
        compiler_params=_params(("parallel",)),
    )(x, scale, shift)


def residual_ln(x, y, gate1, g, b, nscale, nshift, *, name):
    s, d = x.shape

    def body(x_ref, y_ref, gt_ref, g_ref, b_ref, sc_ref, sh_ref, xn_ref, hn_ref):
        z = ALPHA * x_ref[...] + gt_ref[...] * y_ref[...].astype(F32)
        xhat, _ = _layer_norm_hat(z)
        xn = xhat * g_ref[...] + b_ref[...]
        xn_ref[...] = xn
        hn_ref[...] = (xn * (1.0 + sc_ref[...]) + sh_ref[...]).astype(BF16)

    return pl.pallas_call(
        body,
        name=name,
        grid=(s // ROW_TILE,),
        in_specs=[_rows(d), _rows(d), _vec(d), _vec(d), _vec(d), _vec(d), _vec(d)],
        out_specs=[_rows(d), _rows(d)],
        out_shape=[jax.ShapeDtypeStruct((s, d), F32), jax.ShapeDtypeStruct((s, d), BF16)],
        compiler_params=_params(("parallel",)),
    )(x, y, gate1, g, b, nscale, nshift)


def residual_ln_loss(x, y, gate1, g, b, target, *, name):
    s, d = x.shape

    def body(x_ref, y_ref, gt_ref, g_ref, b_ref, t_ref, dx_ref, loss_ref):
        z = ALPHA * x_ref[...] + gt_ref[...] * y_ref[...].astype(F32)
        xhat, _ = _layer_norm_hat(z)
        err = xhat * g_ref[...] + b_ref[...] - t_ref[...]
        dx_ref[...] = err * (1.0 / d)
        part = jnp.sum(jnp.sum(err * err, axis=1, keepdims=True), axis=0, keepdims=True) * (0.5 / d)

        @pl.when(pl.program_id(0) == 0)
        def _():
            loss_ref[...] = part

        @pl.when(pl.program_id(0) > 0)
        def _():
            loss_ref[...] += part

    return pl.pallas_call(
        body,
        name=name,
        grid=(s // ROW_TILE,),
        in_specs=[_rows(d), _rows(d), _vec(d), _vec(d), _vec(d), _rows(d)],
        out_specs=[_rows(d), pl.BlockSpec((1, 1), lambda i: (0, 0))],
        out_shape=[jax.ShapeDtypeStruct((s, d), F32), jax.ShapeDtypeStruct((1, 1), F32)],
        compiler_params=_params(("arbitrary",)),
    )(x, y, gate1, g, b, target)


ST_DSCALE, ST_DSHIFT, ST_DG, ST_DB, ST_DGATE = 0, 1, 2, 3, 4


def residual_ln_bwd(*, name, later=None, dxo=None, this=None):
    lead = later[0] if later is not None else dxo
    s, d = lead.shape
    has_later, has_ln = later is not None, this is not None

    def body(*refs):
        refs = list(refs)
        if has_later:
            dh_ref, dzl_ref, xm_ref, scl_ref = refs[:4]
            refs = refs[4:]
        else:
            dxo_ref = refs.pop(0)
        if has_ln:
            x_ref, y_ref, gt_ref, g_ref = refs[:4]
            refs = refs[4:]
            dz_ref, dy_ref, st_ref = refs
        else:
            dx_ref, st_ref = refs

        @pl.when(pl.program_id(0) == 0)
        def _():
            st_ref[...] = jnp.zeros_like(st_ref)

        def acc(row, val):
            st_ref[row : row + 1, :] += jnp.sum(val, axis=0, keepdims=True)

        if has_later:
            dh = dh_ref[...]
            g_out = ALPHA * dzl_ref[...] + dh * (1.0 + scl_ref[...])
            acc(ST_DSCALE, dh * xm_ref[...])
            acc(ST_DSHIFT, dh)
        else:
            g_out = dxo_ref[...]
        if not has_ln:
            dx_ref[...] = g_out
            return
        y = y_ref[...].astype(F32)
        gate1 = gt_ref[...]
        z = ALPHA * x_ref[...] + gate1 * y
        xhat, rstd = _layer_norm_hat(z)
        acc(ST_DG, g_out * xhat)
        acc(ST_DB, g_out)
        dxh = g_out * g_ref[...]
        m1 = jnp.mean(dxh, axis=-1, keepdims=True)
        m2 = jnp.mean(dxh * xhat, axis=-1, keepdims=True)
        dz = rstd * (dxh - m1 - xhat * m2)
        acc(ST_DGATE, dz * y)
        dz_ref[...] = dz
        dy_ref[...] = (dz * gate1).astype(BF16)

    ins, specs = [], []
    if has_later:
        ins += list(later)
        specs += [_rows(d), _rows(d), _rows(d), _vec(d)]
    else:
        ins += [dxo]
        specs += [_rows(d)]
    if has_ln:
        ins += list(this)
        specs += [_rows(d), _rows(d), _vec(d), _vec(d)]
        out_specs = [_rows(d), _rows(d), _vec(d, SUBLANES)]
        out_shape = [
            jax.ShapeDtypeStruct((s, d), F32),
            jax.ShapeDtypeStruct((s, d), BF16),
            jax.ShapeDtypeStruct((SUBLANES, d), F32),
        ]
    else:
        out_specs = [_rows(d), _vec(d, SUBLANES)]
        out_shape = [jax.ShapeDtypeStruct((s, d), F32), jax.ShapeDtypeStruct((SUBLANES, d), F32)]
    return pl.pallas_call(
        body,
        name=name,
        grid=(s // ROW_TILE,),
        in_specs=specs,
        out_specs=out_specs,
        out_shape=out_shape,
        compiler_params=_params(("arbitrary",)),
    )(*ins)


GATE_CHUNKS = 4


def _gelu(x):
    return 0.5 * x * (1.0 + jnp.tanh(GELU_C * (x + GELU_A * x * x * x)))


def _gelu_grad(x):
    t = jnp.tanh(GELU_C * (x + GELU_A * x * x * x))
    return 0.5 * (1.0 + t) + 0.5 * x * (1.0 - t * t) * (GELU_C * (1.0 + 3.0 * GELU_A * x * x))


def _causal_weights(w_ref, transpose):
    t = lax.broadcasted_iota(jnp.int32, (CHUNK, CHUNK), 0)
    s = lax.broadcasted_iota(jnp.int32, (CHUNK, CHUNK), 1)
    out = []
    for g in range(A_GROUPS):
        w = jnp.where(t >= s, w_ref[g], 0.0)
        out.append((w.T if transpose else w).astype(BF16))
    return out


def _spatial(ws, vn, lo_mask):
    rows = vn.shape[0]
    out_rows = []
    for r in range(rows // CHUNK):
        cols = []
        for j in range(A_GROUPS // 2):
            blk = vn[r * CHUNK : (r + 1) * CHUNK, j * LANES : (j + 1) * LANES]
            za = jnp.dot(ws[2 * j], blk, preferred_element_type=F32)
            zb = jnp.dot(ws[2 * j + 1], blk, preferred_element_type=F32)
            cols.append(jnp.where(lo_mask, za, zb))
        out_rows.append(jnp.concatenate(cols, axis=1))
    return jnp.concatenate(out_rows, axis=0)


def _gate_forward(a, vg, vb, ws, bias, lo_mask):
    u = _gelu(a[:, :D_MODEL])
    v = _gelu(a[:, D_MODEL:])
    vhat, rstd = _layer_norm_hat(v)
    vn = (vhat * vg + vb).astype(BF16)
    z = _spatial(ws, vn, lo_mask) + jnp.concatenate([bias] * (a.shape[0] // CHUNK), axis=0)
    return u, vhat, rstd, vn, z


def gate_fwd(a_pre, vn_g, vn_b, w_s, bias_full, *, name):
    s = a_pre.shape[0]
    tr = GATE_CHUNKS * CHUNK

    def body(a_ref, vg_ref, vb_ref, w_ref, bias_ref, p_ref):
        lo_mask = _lane((CHUNK, LANES)) < A_GROUP_DIM
        ws = _causal_weights(w_ref, transpose=False)
        u, _, _, _, z = _gate_forward(a_ref[...].astype(F32), vg_ref[...], vb_ref[...], ws, bias_ref[...], lo_mask)
        p_ref[...] = (u * z).astype(BF16)

    return pl.pallas_call(
        body,
        name=name,
        grid=(s // tr,),
        in_specs=[
            pl.BlockSpec((tr, 2 * D_MODEL), lambda i: (i, 0)),
            _vec(D_MODEL),
            _vec(D_MODEL),
            pl.BlockSpec((A_GROUPS, CHUNK, CHUNK), lambda i: (0, 0, 0)),
            _vec(D_MODEL, CHUNK),
        ],
        out_specs=pl.BlockSpec((tr, D_MODEL), lambda i: (i, 0)),
        out_shape=jax.ShapeDtypeStruct((s, D_MODEL), BF16),
        compiler_params=_params(("parallel",)),
    )(a_pre, vn_g, vn_b, w_s, bias_full)


def gate_bwd(a_pre, dp, vn_g, vn_b, w_s, bias_full, *, name):
    s = a_pre.shape[0]
    tr = GATE_CHUNKS * CHUNK
    nsteps = s // tr

    def body(a_ref, dp_ref, vg_ref, vb_ref, w_ref, bias_ref, da_ref, dw_ref, dbs_ref, rows_ref, dbias_acc):
        step = pl.program_id(0)
        lo_mask = _lane((CHUNK, LANES)) < A_GROUP_DIM

        @pl.when(step == 0)
        def _():
            dw_ref[...] = jnp.zeros_like(dw_ref)
            rows_ref[...] = jnp.zeros_like(rows_ref)
            dbias_acc[...] = jnp.zeros_like(dbias_acc)

        a = a_ref[...].astype(F32)
        vg = vg_ref[...]
        ws = _causal_weights(w_ref, transpose=False)
        wts = _causal_weights(w_ref, transpose=True)
        u, vhat, rstd, vn, z = _gate_forward(a, vg, vb_ref[...], ws, bias_ref[...], lo_mask)
        dp = dp_ref[...]
        du = dp * z
        dzz = dp * u
        dzz_b = dzz.astype(BF16)
        dvn = _spatial(wts, dzz_b, lo_mask)
        dbias = None
        for r in range(GATE_CHUNKS):
            rs = slice(r * CHUNK, (r + 1) * CHUNK)
            dbias = dzz[rs] if dbias is None else dbias + dzz[rs]
            for j in range(A_GROUPS // 2):
                cs = slice(j * LANES, (j + 1) * LANES)
                dblk = dzz[rs, cs]
                vblk = vn[rs, cs]
                for half in range(2):
                    keep = lo_mask if half == 0 else jnp.logical_not(lo_mask)
                    dm = jnp.where(keep, dblk, 0.0).astype(BF16)
                    dw_ref[2 * j + half] += lax.dot_general(
                        dm, vblk, (((1,), (1,)), ((), ())), preferred_element_type=F32
                    )
        dbias_acc[...] += dbias
        rows_ref[1:2, :D_MODEL] += jnp.sum(dvn * vhat, axis=0, keepdims=True)
        rows_ref[1:2, D_MODEL:] += jnp.sum(dvn, axis=0, keepdims=True)
        dvh = dvn * vg
        m1 = jnp.mean(dvh, axis=-1, keepdims=True)
        m2 = jnp.mean(dvh * vhat, axis=-1, keepdims=True)
        dv = rstd * (dvh - m1 - vhat * m2)
        da_u = du * _gelu_grad(a[:, :D_MODEL])
        da_v = dv * _gelu_grad(a[:, D_MODEL:])
        da_ref[:, :D_MODEL] = da_u.astype(BF16)
        da_ref[:, D_MODEL:] = da_v.astype(BF16)
        rows_ref[0:1, :D_MODEL] += jnp.sum(da_u, axis=0, keepdims=True)
        rows_ref[0:1, D_MODEL:] += jnp.sum(da_v, axis=0, keepdims=True)

        @pl.when(step == nsteps - 1)
        def _():
            t = lax.broadcasted_iota(jnp.int32, (CHUNK, CHUNK), 0)
            sx = lax.broadcasted_iota(jnp.int32, (CHUNK, CHUNK), 1)
            for g in range(A_GROUPS):
                dw_ref[g] = jnp.where(t >= sx, dw_ref[g], 0.0)
            dbs_ref[...] = _reduce_groups(dbias_acc[...])

    return pl.pallas_call(
        body,
        name=name,
        grid=(nsteps,),
        in_specs=[
            pl.BlockSpec((tr, 2 * D_MODEL), lambda i: (i, 0)),
            pl.BlockSpec((tr, D_MODEL), lambda i: (i, 0)),
            _vec(D_MODEL),
            _vec(D_MODEL),
            pl.BlockSpec((A_GROUPS, CHUNK, CHUNK), lambda i: (0, 0, 0)),
            _vec(D_MODEL, CHUNK),
        ],
        out_specs=[
            pl.BlockSpec((tr, 2 * D_MODEL), lambda i: (i, 0)),
            pl.BlockSpec((A_GROUPS, CHUNK, CHUNK), lambda i: (0, 0, 0)),
            _vec(LANES, CHUNK),
            _vec(2 * D_MODEL, SUBLANES),
        ],
        out_shape=[
            jax.ShapeDtypeStruct((s, 2 * D_MODEL), BF16),
            jax.ShapeDtypeStruct((A_GROUPS, CHUNK, CHUNK), F32),
            jax.ShapeDtypeStruct((CHUNK, LANES), F32),
            jax.ShapeDtypeStruct((SUBLANES, 2 * D_MODEL), F32),
        ],
        scratch_shapes=[pltpu.VMEM((CHUNK, D_MODEL), F32)],
        compiler_params=_params(("arbitrary",)),
    )(a_pre, dp, vn_g, vn_b, w_s, bias_full)


def alibi_tables(dilation):
    qi = jnp.arange(SPAN)[:, None]
    ki = jnp.arange(2 * SPAN)[None, :]
    diff = SPAN + qi - ki
    valid = (diff >= 0) & (diff <= SPAN)
    heads = jnp.arange(1, B_HEADS + 1, dtype=F32)
    slopes = jnp.exp2(-8.0 * heads / B_HEADS)
    bias = -slopes[:, None, None] * (dilation * diff).astype(F32)
    bias = jnp.where(valid[None], bias, NEG).reshape(B_HEADS // 2, 2 * SPAN, 2 * SPAN)
    return bias, bias.transpose(0, 2, 1)


def _pair_rows(x, halves):
    return jnp.concatenate([x * halves[0], x * halves[1]], axis=0)


def _pair_column(v, lane, j):
    pick = lambda h: jnp.sum(jnp.where(lane == h, v, 0.0), axis=1, keepdims=True)
    return jnp.concatenate([pick(2 * j), pick(2 * j + 1)], axis=0)


_NT = (((1,), (1,)), ((), ()))


def permute_rows(x, dilation, *, inverse, name, add=None):
    s, w = x.shape
    tile = SPAN * dilation
    nat = pl.BlockSpec((tile, w), lambda i: (i, 0))
    streams = pl.BlockSpec((dilation, SPAN, w), lambda i: (0, i, 0))
    x3 = x.reshape(dilation, s // dilation, w) if inverse else x

    def body(*refs):
        if not inverse:
            x_ref, o_ref = refs
            o_ref[...] = jnp.swapaxes(x_ref[...].reshape(SPAN, dilation, w), 0, 1)
            return
        val = jnp.swapaxes(refs[0][...], 0, 1).reshape(tile, w)
        if add is not None:
            val = val + refs[1][...]
        refs[-1][...] = val

    out = pl.pallas_call(
        body,
        name=name,
        grid=(s // tile,),
        in_specs=([streams] + ([nat] if add is not None else [])) if inverse else [nat],
        out_specs=nat if inverse else streams,
        out_shape=jax.ShapeDtypeStruct((s, w) if inverse else (dilation, s // dilation, w), x.dtype),
        compiler_params=_params(("parallel",)),
    )(*([x3] + ([add] if add is not None else [])))
    return out.reshape(s, w)


def _qkv_specs(block_of):
    def spec(which, prev):
        def index(*grid):
            blk = block_of(*grid)
            return (jnp.maximum(blk - 1, 0) if prev else blk, which)

        return pl.BlockSpec((SPAN, D_MODEL), index)

    return [spec(0, False), spec(1, True), spec(1, False), spec(2, True), spec(2, False)]


def attn_fwd(qkv_p, pat, *, name):
    _, dilation = B_PATTERNS[pat]
    nb = SEQ // dilation // SPAN
    bias, _ = alibi_tables(dilation)

    def body(q_ref, kp_ref, kc_ref, vp_ref, vc_ref, bias_ref, o_ref, lse_ref):
        n = pl.program_id(1)
        lane = _lane((SPAN, LANES))
        lo_mask = lane < B_HEAD_DIM
        first_prev = jnp.logical_and(n == 0, _lane((2 * SPAN, 2 * SPAN)) < SPAN)
        q = q_ref[...] * jnp.asarray(B_HEAD_DIM**-0.5, BF16)
        kk = jnp.concatenate([kp_ref[...], kc_ref[...]], axis=0)
        vv = jnp.concatenate([vp_ref[...], vc_ref[...]], axis=0)
        halves = (lo_mask.astype(BF16), jnp.logical_not(lo_mask).astype(BF16))
        stats = jnp.zeros((SPAN, LANES), F32)
        for j in range(B_HEADS // 2):
            cs = slice(j * LANES, (j + 1) * LANES)
            sc = lax.dot_general(_pair_rows(q[:, cs], halves), kk[:, cs], _NT, preferred_element_type=F32)
            sc = jnp.where(first_prev, NEG, sc + bias_ref[j])
            m = jnp.max(sc, axis=1, keepdims=True)
            p = jnp.exp(sc - m)
            l = jnp.sum(p, axis=1, keepdims=True)
            acc = jnp.dot(p.astype(BF16), vv[:, cs], preferred_element_type=F32) * (1.0 / l)
            lse_pair = m + jnp.log(l)
            o_ref[:, cs] = jnp.where(lo_mask, acc[:SPAN], acc[SPAN:]).astype(BF16)
            stats = jnp.where(lane == 2 * j, lse_pair[:SPAN], stats)
            stats = jnp.where(lane == 2 * j + 1, lse_pair[SPAN:], stats)
        lse_ref[...] = stats

    return pl.pallas_call(
        body,
        name=name,
        grid=(dilation, nb),
        in_specs=[
            *_qkv_specs(lambda r, n: r * nb + n),
            pl.BlockSpec((B_HEADS // 2, 2 * SPAN, 2 * SPAN), lambda r, n: (0, 0, 0)),
        ],
        out_specs=[
            pl.BlockSpec((SPAN, D_MODEL), lambda r, n: (r * nb + n, 0)),
            pl.BlockSpec((SPAN, LANES), lambda r, n: (r * nb + n, 0)),
        ],
        out_shape=[jax.ShapeDtypeStruct((SEQ, D_MODEL), BF16), jax.ShapeDtypeStruct((SEQ, LANES), F32)],
        compiler_params=_params(("parallel", "arbitrary")),
    )(qkv_p, qkv_p, qkv_p, qkv_p, qkv_p, bias)


def attn_combine(outs, lses, *, name):
    def body(o0, o1, o2, l0, l1, l2, ob_ref, of_ref, lse_ref):
        ls = [l0[...], l1[...], l2[...]]
        m = jnp.maximum(jnp.maximum(ls[0], ls[1]), ls[2])
        tot = jnp.log(jnp.exp(ls[0] - m) + jnp.exp(ls[1] - m) + jnp.exp(ls[2] - m)) + m
        o = None
        for o_ref, l in zip((o0, o1, o2), ls, strict=True):
            term = _expand_groups(jnp.exp(l - tot)) * o_ref[...]
            o = term if o is None else o + term
        ob_ref[...] = o.astype(BF16)
        of_ref[...] = o
        lse_ref[...] = tot

    return pl.pallas_call(
        body,
        name=name,
        grid=(SEQ // ROW_TILE,),
        in_specs=[_rows(D_MODEL)] * 3 + [_rows(LANES)] * 3,
        out_specs=[_rows(D_MODEL), _rows(D_MODEL), _rows(LANES)],
        out_shape=[
            jax.ShapeDtypeStruct((SEQ, D_MODEL), BF16),
            jax.ShapeDtypeStruct((SEQ, D_MODEL), F32),
            jax.ShapeDtypeStruct((SEQ, LANES), F32),
        ],
        compiler_params=_params(("parallel",)),
    )(*outs, *lses)


def attn_delta(do, o, *, name):
    def body(do_ref, o_ref, dob_ref, dl_ref):
        do_v = do_ref[...]
        dob_ref[...] = do_v.astype(BF16)
        dl_ref[...] = _reduce_groups(do_v * o_ref[...])

    return pl.pallas_call(
        body,
        name=name,
        grid=(SEQ // ROW_TILE,),
        in_specs=[_rows(D_MODEL), _rows(D_MODEL)],
        out_specs=[_rows(D_MODEL), _rows(LANES)],
        out_shape=[jax.ShapeDtypeStruct((SEQ, D_MODEL), BF16), jax.ShapeDtypeStruct((SEQ, LANES), F32)],
        compiler_params=_params(("parallel",)),
    )(do, o)


def attn_bwd(qkv_p, do_p, lse_p, delta_p, pat, *, name):
    _, dilation = B_PATTERNS[pat]
    nb = SEQ // dilation // SPAN
    n_blocks = SEQ // SPAN
    bias, bias_t = alibi_tables(dilation)
    last = n_blocks - 1
    q_cols, k_cols, v_cols = (slice(i * D_MODEL, (i + 1) * D_MODEL) for i in range(3))

    def body(q_ref, kp_ref, kc_ref, vp_ref, vc_ref, do_ref, lse_ref, dl_ref, bias_ref, biast_ref, out_ref, cq_ref, ck_ref, cv_ref):
        g = pl.program_id(0)

        @pl.when(g == n_blocks)
        def _():
            out_ref[:, q_cols] = cq_ref[...].astype(BF16)
            out_ref[:, k_cols] = ck_ref[...].astype(BF16)
            out_ref[:, v_cols] = cv_ref[...].astype(BF16)

        @pl.when(g == 0)
        def _():
            cq_ref[...] = jnp.zeros_like(cq_ref)
            ck_ref[...] = jnp.zeros_like(ck_ref)
            cv_ref[...] = jnp.zeros_like(cv_ref)

        @pl.when(g < n_blocks)
        def _():
            lane = _lane((SPAN, LANES))
            lo_mask = lane < B_HEAD_DIM
            pair = (2 * SPAN, 2 * SPAN)
            first = lax.rem(g, nb) == 0
            prev_key_cols = jnp.logical_and(first, _lane(pair) < SPAN)
            prev_key_rows = jnp.logical_and(first, lax.broadcasted_iota(jnp.int32, pair, 0) < SPAN)
            q = q_ref[...] * jnp.asarray(B_HEAD_DIM**-0.5, BF16)
            kk = jnp.concatenate([kp_ref[...], kc_ref[...]], axis=0)
            vv = jnp.concatenate([vp_ref[...], vc_ref[...]], axis=0)
            do_v = do_ref[...]
            lse_v = lse_ref[...]
            dl_v = dl_ref[...]
            lse_t = lse_v.T
            dl_t = dl_v.T
            halves = (lo_mask.astype(BF16), jnp.logical_not(lo_mask).astype(BF16))
            for j in range(B_HEADS // 2):
                cs = slice(j * LANES, (j + 1) * LANES)
                kp, vp = kk[:, cs], vv[:, cs]
                q2 = _pair_rows(q[:, cs], halves)
                do2 = _pair_rows(do_v[:, cs], halves)
                lse_c, dl_c = _pair_column(lse_v, lane, j), _pair_column(dl_v, lane, j)
                lse_r = jnp.concatenate([lse_t[2 * j : 2 * j + 1], lse_t[2 * j + 1 : 2 * j + 2]], axis=1)
                dl_r = jnp.concatenate([dl_t[2 * j : 2 * j + 1], dl_t[2 * j + 1 : 2 * j + 2]], axis=1)
                sc = lax.dot_general(q2, kp, _NT, preferred_element_type=F32)
                p = jnp.exp(jnp.where(prev_key_cols, NEG, sc + bias_ref[j]) - lse_c)
                dp = lax.dot_general(do2, vp, _NT, preferred_element_type=F32)
                ds = (p * (dp - dl_c)).astype(BF16)
                dq2 = jnp.dot(ds, kp, preferred_element_type=F32)
                sc_t = lax.dot_general(kp, q2, _NT, preferred_element_type=F32)
                p_t = jnp.exp(jnp.where(prev_key_rows, NEG, sc_t + biast_ref[j]) - lse_r)
                dp_t = lax.dot_general(vp, do2, _NT, preferred_element_type=F32)
                ds_t = (p_t * (dp_t - dl_r)).astype(BF16)
                dk_pair = jnp.dot(ds_t, q2, preferred_element_type=F32)
                dv_pair = jnp.dot(p_t.astype(BF16), do2, preferred_element_type=F32)
                oq = slice(j * LANES, (j + 1) * LANES)
                ok = slice(D_MODEL + j * LANES, D_MODEL + (j + 1) * LANES)
                ov = slice(2 * D_MODEL + j * LANES, 2 * D_MODEL + (j + 1) * LANES)
                out_ref[:, oq] = cq_ref[:, cs].astype(BF16)
                out_ref[:, ok] = (ck_ref[:, cs] + dk_pair[:SPAN]).astype(BF16)
                out_ref[:, ov] = (cv_ref[:, cs] + dv_pair[:SPAN]).astype(BF16)
                cq_ref[:, cs] = jnp.where(lo_mask, dq2[:SPAN], dq2[SPAN:]) * (B_HEAD_DIM**-0.5)
                ck_ref[:, cs] = dk_pair[SPAN:]
                cv_ref[:, cs] = dv_pair[SPAN:]

    def block_of(g):
        return jnp.minimum(g, last)

    def row_spec(width):
        return pl.BlockSpec((SPAN, width), lambda g: (block_of(g), 0))

    return pl.pallas_call(
        body,
        name=name,
        grid=(n_blocks + 1,),
        in_specs=[
            *_qkv_specs(block_of),
            row_spec(D_MODEL),
            row_spec(LANES),
            row_spec(LANES),
            pl.BlockSpec((B_HEADS // 2, 2 * SPAN, 2 * SPAN), lambda g: (0, 0, 0)),
            pl.BlockSpec((B_HEADS // 2, 2 * SPAN, 2 * SPAN), lambda g: (0, 0, 0)),
        ],
        out_specs=pl.BlockSpec((SPAN, 3 * D_MODEL), lambda g: (jnp.maximum(g - 1, 0), 0)),
        out_shape=jax.ShapeDtypeStruct((SEQ, 3 * D_MODEL), BF16),
        scratch_shapes=[pltpu.VMEM((SPAN, D_MODEL), F32)] * 3,
        compiler_params=_params(("arbitrary",)),
    )(qkv_p, qkv_p, qkv_p, qkv_p, qkv_p, do_p, lse_p, delta_p, bias, bias_t)


def _position():
    x, y, c = lax.axis_index("x"), lax.axis_index("y"), lax.axis_index("c")
    return x, y, c, 4 * x + 2 * y + c


def _peer(k, x, y, c):
    px = 1 - x if k & 4 else x
    py = 1 - y if k & 2 else y
    pc = 1 - c if k & 1 else c
    return (px, py, pc), 4 * px + 2 * py + pc


def _remote(src, dst, send_sem, recv_sem, device):
    return pltpu.make_async_remote_copy(
        src_ref=src, dst_ref=dst, send_sem=send_sem, recv_sem=recv_sem, device_id=device, device_id_type=MESH
    )


def _silu_bf16(cf):
    return (cf * (1.0 / (1.0 + jnp.exp(-cf)))).astype(BF16)


def ada_exchange(c8, w4, b4, ln8):
    nt, _, ncol = w4.shape

    def body(c8_ref, w_ref, b_ref, ln_ref, cg_ref, lng_ref, mrecv_ref, mloc_ref, send_sems, recv_sems):
        x, y, c, me = _position()
        cg_ref[me] = c8_ref[...]
        lng_ref[me] = ln_ref[...]
        first = []
        for k in range(1, N_DEV):
            dev, _ = _peer(k, x, y, c)
            first.append(_remote(c8_ref, cg_ref.at[me], send_sems.at[0, k], recv_sems.at[0, k], dev))
            first.append(_remote(ln_ref, lng_ref.at[me], send_sems.at[1, k], recv_sems.at[1, k], dev))
        for cp in first:
            cp.start()
        for k in range(1, N_DEV):
            dev, pid = _peer(k, x, y, c)
            _remote(c8_ref, cg_ref.at[pid], send_sems.at[0, k], recv_sems.at[0, k], dev).wait_recv()
            _remote(ln_ref, lng_ref.at[pid], send_sems.at[1, k], recv_sems.at[1, k], dev).wait_recv()
        sc = _silu_bf16(cg_ref[...].reshape(N_DEV * SUBLANES, D_MODEL))
        for t in range(nt):
            mloc_ref[t] = jnp.dot(sc, w_ref[t].astype(BF16), preferred_element_type=F32) + b_ref[t : t + 1, :]

        def group(dev_id):
            return pl.ds(pl.multiple_of(dev_id * SUBLANES, SUBLANES), SUBLANES)

        mrecv_ref[me] = mloc_ref[:, group(me), :]
        second = []
        for k in range(1, N_DEV):
            dev, pid = _peer(k, x, y, c)
            second.append(
                _remote(mloc_ref.at[:, group(pid), :], mrecv_ref.at[me], send_sems.at[2, k], recv_sems.at[2, k], dev)
            )
        for cp in second:
            cp.start()
        for k in range(1, N_DEV):
            dev, pid = _peer(k, x, y, c)
            _remote(
                mloc_ref.at[:, group(pid), :], mrecv_ref.at[pid], send_sems.at[2, k], recv_sems.at[2, k], dev
            ).wait_recv()
        for cp in first + second:
            cp.wait_send()

    return pl.pallas_call(
        body,
        name="ada_exchange",
        in_specs=[VMEM, VMEM, VMEM, VMEM],
        out_specs=[VMEM, VMEM, VMEM],
        out_shape=[
            jax.ShapeDtypeStruct((N_DEV, SUBLANES, D_MODEL), F32),
            jax.ShapeDtypeStruct((N_DEV, SUBLANES, LANES), F32),
            jax.ShapeDtypeStruct((N_DEV, nt, SUBLANES, ncol), F32),
        ],
        scratch_shapes=[
            pltpu.VMEM((nt, N_DEV * SUBLANES, ncol), F32),
            pltpu.SemaphoreType.DMA((3, N_DEV)),
            pltpu.SemaphoreType.DMA((3, N_DEV)),
        ],
        compiler_params=pltpu.CompilerParams(vmem_limit_bytes=VMEM_LIMIT_BYTES),
    )(c8, w4, b4, ln8)


def small_exchange(dmx, flat):
    def body(dmx_ref, flat_ref, dmrecv_ref, red_ref, land_ref, send_sems, recv_sems):
        x, y, c, me = _position()
        dmrecv_ref[me] = dmx_ref[me]
        land_ref[me] = flat_ref[me]
        first = []
        for k in range(1, N_DEV):
            dev, pid = _peer(k, x, y, c)
            first.append(_remote(dmx_ref.at[pid], dmrecv_ref.at[me], send_sems.at[0, k], recv_sems.at[0, k], dev))
            first.append(_remote(flat_ref.at[pid], land_ref.at[me], send_sems.at[1, k], recv_sems.at[1, k], dev))
        for cp in first:
            cp.start()
        for k in range(1, N_DEV):
            dev, pid = _peer(k, x, y, c)
            _remote(dmx_ref.at[pid], dmrecv_ref.at[pid], send_sems.at[0, k], recv_sems.at[0, k], dev).wait_recv()
            _remote(flat_ref.at[pid], land_ref.at[pid], send_sems.at[1, k], recv_sems.at[1, k], dev).wait_recv()
        total = land_ref[0]
        for s in range(1, N_DEV):
            total = total + land_ref[s]
        red_ref[me] = total
        second = []
        for k in range(1, N_DEV):
            dev, _ = _peer(k, x, y, c)
            second.append(_remote(red_ref.at[me], red_ref.at[me], send_sems.at[2, k], recv_sems.at[2, k], dev))
        for cp in second:
            cp.start()
        for k in range(1, N_DEV):
            dev, pid = _peer(k, x, y, c)
            _remote(red_ref.at[pid], red_ref.at[pid], send_sems.at[2, k], recv_sems.at[2, k], dev).wait_recv()
        for cp in first + second:
            cp.wait_send()

    return pl.pallas_call(
        body,
        name="small_exchange",
        in_specs=[VMEM, VMEM],
        out_specs=[VMEM, VMEM],
        out_shape=[jax.ShapeDtypeStruct(dmx.shape, F32), jax.ShapeDtypeStruct(flat.shape, F32)],
        scratch_shapes=[
            pltpu.VMEM(flat.shape, F32),
            pltpu.SemaphoreType.DMA((3, N_DEV)),
            pltpu.SemaphoreType.DMA((3, N_DEV)),
        ],
        compiler_params=pltpu.CompilerParams(vmem_limit_bytes=VMEM_LIMIT_BYTES),
    )(dmx, flat)


HBM = pl.BlockSpec(memory_space=pltpu.HBM)
SEM = pl.BlockSpec(memory_space=pltpu.SEMAPHORE)
EFFECT = pltpu.SideEffectType.DATAFLOW_SIDE_EFFECTING


def _own_slot(me, block):
    land = lax.empty((N_DEV, *block.shape), block.dtype)
    return lax.dynamic_update_slice_in_dim(land, block[None], me, axis=0)


N_CHIP_PEERS = 3


class Gather:
    def __init__(self, shards, lands, after, *, name):
        nt = len(shards)
        self.name = name

        def body(*refs):
            src_refs, land_refs = refs[:nt], refs[nt : 2 * nt]
            send_sems, recv_sems = refs[2 * nt + 1 : 3 * nt + 1], refs[3 * nt + 1 : 4 * nt + 1]
            token = refs[-1]
            x, y, c, me = _position()
            for t in range(nt):
                for k, dev in enumerate(self._targets(x, y, c)):
                    _remote(src_refs[t], land_refs[t].at[me], send_sems[t].at[k], recv_sems[t].at[k], dev).start()
            token[...] = jnp.zeros_like(token)

        outs = pl.pallas_call(
            body,
            name=name + "_start",
            in_specs=[HBM] * (2 * nt) + [ANY],
            out_specs=[SEM] * (2 * nt) + [HBM] * (2 * nt) + [VMEM],
            out_shape=[pltpu.SemaphoreType.DMA((1 + N_CHIP_PEERS,))] * (2 * nt)
            + [pltpu.HBM(a.shape, a.dtype) for a in (*shards, *lands)]
            + [jax.ShapeDtypeStruct((SUBLANES, LANES), F32)],
            input_output_aliases={i: 2 * nt + i for i in range(2 * nt)},
            compiler_params=pltpu.CompilerParams(has_side_effects=EFFECT),
        )(*[pltpu.with_memory_space_constraint(a, pltpu.HBM) for a in (*shards, *lands)], after)
        self.send_sems, self.recv_sems = list(outs[:nt]), list(outs[nt : 2 * nt])
        self.srcs, self.lands = list(outs[2 * nt : 3 * nt]), list(outs[3 * nt : 4 * nt])
        self.token = outs[-1]

    @staticmethod
    def _chips(x, y):
        return [(1 - x, y), (x, 1 - y), (1 - x, 1 - y)]

    @classmethod
    def _targets(cls, x, y, c):
        return [(x, y, 1 - c)] + [(*chip, c) for chip in cls._chips(x, y)]

    def zero(self):
        return self.token[0, 0]

    def wait(self, which, after, *, name):
        n = len(which)

        def slot(px, py, pc):
            return 4 * px + 2 * py + pc

        def pass_body(*refs):
            land_refs, recv_sems = refs[:n], refs[n : 2 * n]
            fwd_send, fwd_recv = refs[3 * n + 1 : 4 * n + 1], refs[4 * n + 1 : 5 * n + 1]
            x, y, c, _ = _position()
            for t in range(n):
                for j, chip in enumerate(self._chips(x, y)):
                    blk = land_refs[t].at[slot(*chip, c)]
                    _remote(blk, blk, fwd_send[t].at[j], recv_sems[t].at[1 + j], (*chip, c)).wait_recv()
                    _remote(blk, blk, fwd_send[t].at[j], fwd_recv[t].at[j], (x, y, 1 - c)).start()

        lands = [self.lands[t] for t in which]
        outs = pl.pallas_call(
            pass_body,
            name=name + "_pass",
            in_specs=[HBM] * n + [SEM] * n + [ANY],
            out_specs=[HBM] * n + [SEM] * (2 * n),
            out_shape=[pltpu.HBM(a.shape, a.dtype) for a in lands] + [pltpu.SemaphoreType.DMA((N_CHIP_PEERS,))] * (2 * n),
            input_output_aliases={i: i for i in range(n)},
            compiler_params=pltpu.CompilerParams(has_side_effects=EFFECT),
        )(*lands, *[self.recv_sems[t] for t in which], after)
        lands, fwd_send, fwd_recv = outs[:n], outs[n : 2 * n], outs[2 * n :]

        def wait_body(*refs):
            src_refs, land_refs = refs[:n], refs[n : 2 * n]
            send_sems, recv_sems = refs[2 * n : 3 * n], refs[3 * n : 4 * n]
            fwd_send, fwd_recv = refs[4 * n : 5 * n], refs[5 * n : 6 * n]
            x, y, c, me = _position()
            sibling = (x, y, 1 - c)
            for t in range(n):
                for k, dev in enumerate(self._targets(x, y, c)):
                    _remote(src_refs[t], land_refs[t].at[me], send_sems[t].at[k], recv_sems[t].at[k], dev).wait_send()
                blk = land_refs[t].at[slot(x, y, 1 - c)]
                _remote(blk, blk, send_sems[t].at[0], recv_sems[t].at[0], sibling).wait_recv()
                for j, chip in enumerate(self._chips(x, y)):
                    sent = land_refs[t].at[slot(*chip, c)]
                    _remote(sent, sent, fwd_send[t].at[j], fwd_recv[t].at[j], sibling).wait_send()
                    got = land_refs[t].at[slot(*chip, 1 - c)]
                    _remote(got, got, fwd_send[t].at[j], fwd_recv[t].at[j], sibling).wait_recv()

        srcs = [self.srcs[t] for t in which]
        outs = pl.pallas_call(
            wait_body,
            name=name,
            in_specs=[HBM] * (2 * n) + [SEM] * (4 * n),
            out_specs=[HBM] * (2 * n),
            out_shape=[pltpu.HBM(a.shape, a.dtype) for a in (*srcs, *lands)],
            input_output_aliases={i: i for i in range(2 * n)},
            compiler_params=pltpu.CompilerParams(has_side_effects=EFFECT),
        )(*srcs, *lands, *[self.send_sems[t] for t in which], *[self.recv_sems[t] for t in which], *fwd_send, *fwd_recv)
        return outs[n:]


class Scatter:
    def __init__(self, srcs, lands, after, *, name):
        self.name = name
        nt = self.nt = len(srcs)
        peers = N_DEV - 1

        def body(*refs):
            src_refs, land_refs = refs[:nt], refs[nt : 2 * nt]
            send_sems, recv_sems = refs[2 * nt + 1 : 3 * nt + 1], refs[3 * nt + 1 : 4 * nt + 1]
            token = refs[-1]
            x, y, c, me = _position()
            for t in range(nt):
                for k in range(1, N_DEV):
                    dev, pid = _peer(k, x, y, c)
                    src = src_refs[t].at[pid]
                    _remote(src, land_refs[t].at[me], send_sems[t].at[k - 1], recv_sems[t].at[k - 1], dev).start()
            token[...] = jnp.zeros_like(token)

        outs = pl.pallas_call(
            body,
            name=name + "_start",
            in_specs=[HBM] * (2 * nt) + [ANY],
            out_specs=[SEM] * (2 * nt) + [HBM] * (2 * nt) + [VMEM],
            out_shape=[pltpu.SemaphoreType.DMA((peers,))] * (2 * nt)
            + [pltpu.HBM(a.shape, a.dtype) for a in (*srcs, *lands)]
            + [jax.ShapeDtypeStruct((SUBLANES, LANES), F32)],
            input_output_aliases={i: 2 * nt + i for i in range(2 * nt)},
            compiler_params=pltpu.CompilerParams(has_side_effects=EFFECT),
        )(*[pltpu.with_memory_space_constraint(a, pltpu.HBM) for a in (*srcs, *lands)], after)
        self.send_sems, self.recv_sems = outs[:nt], outs[nt : 2 * nt]
        self.srcs, self.lands = outs[2 * nt : 3 * nt], outs[3 * nt : 4 * nt]
        self.token = outs[-1]

    def zero(self):
        return self.token[0, 0]

    def wait(self, which, after, *, name):
        n = len(which)

        def body(*refs):
            src_refs, land_refs = refs[:n], refs[n : 2 * n]
            send_sems, recv_sems = refs[2 * n : 3 * n], refs[3 * n : 4 * n]
            x, y, c, _ = _position()
            for t in range(n):
                for k in range(1, N_DEV):
                    dev, pid = _peer(k, x, y, c)
                    src = src_refs[t].at[pid]
                    cp = _remote(src, land_refs[t].at[pid], send_sems[t].at[k - 1], recv_sems[t].at[k - 1], dev)
                    cp.wait_send()
                    cp.wait_recv()

        srcs = [self.srcs[t] for t in which]
        lands = [self.lands[t] for t in which]
        outs = pl.pallas_call(
            body,
            name=name,
            in_specs=[HBM] * (2 * n) + [SEM] * (2 * n) + [ANY],
            out_specs=[HBM] * (2 * n),
            out_shape=[pltpu.HBM(a.shape, a.dtype) for a in (*srcs, *lands)],
            input_output_aliases={i: i for i in range(2 * n)},
            compiler_params=pltpu.CompilerParams(has_side_effects=EFFECT),
        )(*srcs, *lands, *[self.send_sems[t] for t in which], *[self.recv_sems[t] for t in which], after)
        return outs[n:]


def _adam_update(g, w, m, v):
    m2 = ADAM_B1 * m + (1.0 - ADAM_B1) * g
    v2 = ADAM_B2 * v + (1.0 - ADAM_B2) * jnp.square(g)
    m_hat = m2 / (1.0 - ADAM_B1**ADAM_STEP)
    v_hat = v2 / (1.0 - ADAM_B2**ADAM_STEP)
    delta = -ADAM_LR * (m_hat / (jnp.sqrt(v_hat) + ADAM_EPS) + ADAM_WD * w)
    return delta, m2, v2


def adamw(gparts, w, m, v, *, name):
    p, r, c = gparts.shape
    tr = r if r <= 256 else (256 if c <= D_MODEL else 128)

    def body(g_ref, w_ref, m_ref, v_ref, go_ref, d_ref, mo_ref, vo_ref):
        g = g_ref[0].astype(F32)
        for i in range(1, p):
            g = g + g_ref[i].astype(F32)
        delta, m2, v2 = _adam_update(g, w_ref[...], m_ref[...], v_ref[...])
        go_ref[...] = g
        d_ref[...] = delta
        mo_ref[...] = m2
        vo_ref[...] = v2

    blk = pl.BlockSpec((tr, c), lambda i: (i, 0))
    return pl.pallas_call(
        body,
        name=name,
        grid=(r // tr,),
        in_specs=[pl.BlockSpec((p, tr, c), lambda i: (0, i, 0)), blk, blk, blk],
        out_specs=[blk] * 4,
        out_shape=[jax.ShapeDtypeStruct((r, c), F32)] * 4,
        compiler_params=_params(("parallel",)),
    )(gparts, w, m, v)


def ada_grad_adamw(cg, dmrecv, w4, m4, v4, *, name):
    nt, k, ncol = w4.shape

    def body(cg_ref, dm_ref, w_ref, m_ref, v_ref, go_ref, d_ref, mo_ref, vo_ref, gb_ref):
        sc = _silu_bf16(cg_ref[...].reshape(N_DEV * SUBLANES, k))
        dm = dm_ref[...].reshape(N_DEV * SUBLANES, ncol)
        g = lax.dot_general(sc, dm.astype(BF16), (((0,), (0,)), ((), ())), preferred_element_type=F32)
        delta, m2, v2 = _adam_update(g, w_ref[...], m_ref[...], v_ref[...])
        go_ref[...] = g
        d_ref[...] = delta
        mo_ref[...] = m2
        vo_ref[...] = v2
        gb_ref[...] = jnp.broadcast_to(jnp.sum(dm, axis=0, keepdims=True), (SUBLANES, ncol))

    wblk = pl.BlockSpec((None, k, ncol), lambda t: (t, 0, 0))
    return pl.pallas_call(
        body,
        name=name,
        grid=(nt,),
        in_specs=[
            pl.BlockSpec((N_DEV, SUBLANES, k), lambda t: (0, 0, 0)),
            pl.BlockSpec((N_DEV, None, SUBLANES, ncol), lambda t: (0, t, 0, 0)),
            wblk,
            wblk,
            wblk,
        ],
        out_specs=[wblk] * 4 + [pl.BlockSpec((None, SUBLANES, ncol), lambda t: (t, 0, 0))],
        out_shape=[jax.ShapeDtypeStruct((nt, k, ncol), F32)] * 4 + [jax.ShapeDtypeStruct((nt, SUBLANES, ncol), F32)],
        compiler_params=_params(("parallel",)),
    )(cg, dmrecv, w4, m4, v4)


def kernel(x, c, ada_w, ada_b, ln_g, ln_b, a_w_in, a_b_in, a_vn_g, a_vn_b, a_w_s, a_b_s, a_w_out, b_w_qkv, b_w_out, mlp_w_up, mlp_w_down, loss_target, m_ada_w, m_ada_b, m_ln_g, m_ln_b, m_a_w_in, m_a_b_in, m_a_vn_g, m_a_vn_b, m_a_w_s, m_a_b_s, m_a_w_out, m_b_w_qkv, m_b_w_out, m_mlp_w_up, m_mlp_w_down, v_ada_w, v_ada_b, v_ln_g, v_ln_b, v_a_w_in, v_a_b_in, v_a_vn_g, v_a_vn_b, v_a_w_s, v_a_b_s, v_a_w_out, v_b_w_qkv, v_b_w_out, v_mlp_w_up, v_mlp_w_down):
    x0 = x[0]
    target = loss_target[0]
    me = 4 * lax.axis_index("x") + 2 * lax.axis_index("y") + lax.axis_index("c")

    ada_w4 = ada_w.reshape(N_SUB, D_MODEL, -1)
    ada_b4 = ada_b.reshape(N_SUB, -1)
    ln8 = jnp.concatenate([ln_g.reshape(N_SUB, -1), ln_b.reshape(N_SUB, -1)], axis=0)
    c8 = jnp.broadcast_to(c, (SUBLANES, D_MODEL))
    cg, lng, mrecv = ada_exchange(c8, ada_w4, ada_b4, ln8)

    W_IN, W_AOUT, W_UP0, W_DN0, W_QKV, W_BOUT, W_UP1, W_DN1 = range(8)
    shards = [
        a_w_in[0].astype(BF16),
        a_w_out[0].astype(BF16),
        mlp_w_up[0].astype(BF16),
        mlp_w_down[0].astype(BF16),
        b_w_qkv[0].astype(BF16),
        b_w_out[0].astype(BF16),
        mlp_w_up[1].astype(BF16),
        mlp_w_down[1].astype(BF16),
    ]
    gather = Gather(shards, [_own_slot(me, s) for s in shards], mrecv, name="gather")

    modv = mrecv[:, :, 0, :].transpose(1, 0, 2).reshape(N_SUB, 3 * D_MODEL) + gather.zero()
    shift = [modv[t : t + 1, :D_MODEL] for t in range(N_SUB)]
    scale = [modv[t : t + 1, D_MODEL : 2 * D_MODEL] for t in range(N_SUB)]
    gate1 = [1.0 + modv[t : t + 1, 2 * D_MODEL :] for t in range(N_SUB)]
    lng_full = [lng[:, t, :].reshape(1, D_MODEL) for t in range(N_SUB)]
    lnb_full = [lng[:, N_SUB + t, :].reshape(1, D_MODEL) for t in range(N_SUB)]

    ident = lambda acc: (acc,)
    relu2 = lambda a: jnp.square(jnp.maximum(a.astype(F32), 0.0)).astype(BF16)
    vn_g, vn_b, w_s = a_vn_g, a_vn_b, a_w_s[0]
    bias_full = jnp.repeat(a_b_s[0].T, A_GROUP_DIM, axis=1)
    w_up3, w_dn3 = [None, None], [None, None]

    def mlp_forward(i, h, after):
        up, dn = gather.wait([W_UP0, W_DN0] if i == 0 else [W_UP1, W_DN1], after, name=f"gather_wait_mlp{i}")
        w_up3[i], w_dn3[i] = up, dn.reshape(1, D_FF, D_MODEL)
        (a,) = mm_nn(h, w_up3[i], name=f"mlp{i}_up", tm=2048, ps=2, tn=512, tk=D_MODEL, epilogue=ident, outs=(BF16,))
        (y,) = mm_nn(
            a, w_dn3[i], name=f"mlp{i}_down", tm=2048, ps=1, tn=D_MODEL, tk=1024, prologue=relu2, epilogue=ident, outs=(BF16,)
        )
        return a, y

    h0 = modulate(x0, scale[0], shift[0], name="modulate0")
    w_in3, w_aout3 = gather.wait([W_IN, W_AOUT], h0, name="gather_wait_a")
    w_aout3 = w_aout3.reshape(1, D_MODEL, D_MODEL)
    (a_pre,) = mm_nn(
        h0, w_in3, name="a_in", tm=2048, ps=4, tn=256, tk=D_MODEL, epilogue=lambda acc, b: (acc + b,),
        extras=[(a_b_in, "row")], outs=(BF16,),
    )
    p_gate = gate_fwd(a_pre, vn_g, vn_b, w_s, bias_full, name="gate_fwd")
    (y0,) = mm_nn(p_gate, w_aout3, name="a_out", tm=2048, ps=1, tn=D_MODEL, tk=D_MODEL, epilogue=ident, outs=(BF16,))
    x1, h1 = residual_ln(x0, y0, gate1[0], lng_full[0], lnb_full[0], scale[1], shift[1], name="res_ln0")
    a1, y1 = mlp_forward(0, h1, y0)
    x2, h2 = residual_ln(x1, y1, gate1[1], lng_full[1], lnb_full[1], scale[2], shift[2], name="res_ln1")
    w_qkv_shards, w_bout3 = gather.wait([W_QKV, W_BOUT], y1, name="gather_wait_b")
    w_bout3 = w_bout3.reshape(1, D_MODEL, D_MODEL)
    w_qkv_cols = w_qkv_shards.transpose(1, 0, 2).reshape(D_MODEL, D_QKV)
    w_pat = [w_qkv_cols[None, :, 3 * D_MODEL * g : 3 * D_MODEL * (g + 1)] for g in range(N_PAT)]
    dil = [d for _, d in B_PATTERNS]
    h2_p = [h2] + [permute_rows(h2, dil[g], inverse=False, name=f"perm_h{g}") for g in range(1, N_PAT)]
    qkv_p, pat_o, pat_lse = [], [], []
    for g in range(N_PAT):
        (qkv_g,) = mm_nn(
            h2_p[g], w_pat[g], name=f"b_qkv{g}", tm=2048, ps=1, tn=D_MODEL, tk=D_MODEL, epilogue=ident, outs=(BF16,)
        )
        o_g, lse_g = attn_fwd(qkv_g, g, name=f"attn_fwd{g}")
        if g > 0:
            o_g = permute_rows(o_g, dil[g], inverse=True, name=f"unperm_o{g}")
            lse_g = permute_rows(lse_g, dil[g], inverse=True, name=f"unperm_lse{g}")
        qkv_p.append(qkv_g)
        pat_o.append(o_g)
        pat_lse.append(lse_g)
    o_b, o_f, lse = attn_combine(pat_o, pat_lse, name="attn_combine")
    (y2,) = mm_nn(o_b, w_bout3, name="b_out", tm=2048, ps=1, tn=D_MODEL, tk=D_MODEL, epilogue=ident, outs=(BF16,))
    x3, h3 = residual_ln(x2, y2, gate1[2], lng_full[2], lnb_full[2], scale[3], shift[3], name="res_ln2")
    a3, y3 = mlp_forward(1, h3, y2)
    dxo, loss_local = residual_ln_loss(x3, y3, gate1[3], lng_full[3], lnb_full[3], target, name="res_ln3_loss")
    loss = lax.psum(loss_local[0, 0], ("x", "y", "c"))

    def scatter(parts, after, name):
        parts = [p.reshape(N_DEV, -1, p.shape[-1]) for p in parts]
        lands = [_own_slot(me, lax.dynamic_index_in_dim(p, me, 0, keepdims=False)) for p in parts]
        return Scatter(parts, lands, after, name=name)

    def mlp_backward(i, h, a, dy):
        (da,) = mm_nt(
            dy,
            w_dn3[i],
            name=f"mlp{i}_da",
            tm=2048,
            tko=1024,
            ps=1,
            tc=D_MODEL,
            epilogue=lambda acc, act: (acc * (2.0 * jnp.maximum(act.astype(F32), 0.0)),),
            extras=[(a, "full")],
            outs=(BF16,),
        )
        (dh,) = mm_nt(da, w_up3[i], name=f"mlp{i}_dh", tm=2048, tko=1024, ps=2, tc=512, epilogue=ident, outs=(F32,))
        dw_dn = mm_tn(
            a, dy, name=f"mlp{i}_dw_down", p=1, tk=1024, ps=1, tn=D_MODEL, tmc=2048, prologue=relu2, out_dtype=BF16
        )
        dw_up = mm_tn(h, da, name=f"mlp{i}_dw_up", p=N_DEV, tk=1024, ps=2, tn=512, tmc=2048, out_dtype=BF16)
        return scatter([dw_up, dw_dn], dh, f"scatter_mlp{i}"), dh

    dz3, dy3, st3 = residual_ln_bwd(name="res_bwd3", dxo=dxo, this=(x3, y3, gate1[3], lng_full[3]))
    rs_mlp1, dh3 = mlp_backward(1, h3, a3, dy3)
    dz2, dy2, st2 = residual_ln_bwd(
        name="res_bwd2", later=(dh3, dz3, x3, scale[3] + rs_mlp1.zero()), this=(x2, y2, gate1[2], lng_full[2])
    )
    (d_o,) = mm_nt(dy2, w_bout3, name="b_do", tm=2048, tko=1024, ps=1, tc=D_MODEL, epilogue=ident, outs=(F32,))
    do_b, delta = attn_delta(d_o, o_f, name="attn_delta")
    dh2, dw_pat = None, []
    for g in range(N_PAT):
        do_g, lse_g, delta_g = do_b, lse, delta
        if g > 0:
            do_g = permute_rows(do_b, dil[g], inverse=False, name=f"perm_do{g}")
            lse_g = permute_rows(lse, dil[g], inverse=False, name=f"perm_lse{g}")
            delta_g = permute_rows(delta, dil[g], inverse=False, name=f"perm_delta{g}")
        dqkv_g = attn_bwd(qkv_p[g], do_g, lse_g, delta_g, g, name=f"attn_bwd{g}")
        (dh_g,) = mm_nt(
            dqkv_g, w_pat[g], name=f"b_dh{g}", tm=2048, tko=1024, ps=1, tc=D_MODEL, epilogue=ident, outs=(F32,)
        )
        dh2 = dh_g if g == 0 else permute_rows(dh_g, dil[g], inverse=True, add=dh2, name=f"unperm_dh{g}")
        dw_pat.append(
            mm_tn(h2_p[g], dqkv_g, name=f"b_dw_qkv{g}", p=1, tk=1024, ps=1, tn=D_MODEL, tmc=2048, out_dtype=BF16)[0]
        )
    dw_bout = mm_tn(o_b, dy2, name="b_dw_out", p=1, tk=1024, ps=1, tn=D_MODEL, tmc=2048, out_dtype=BF16)
    dw_qkv = jnp.concatenate(dw_pat, axis=1).reshape(D_MODEL, N_DEV, -1).transpose(1, 0, 2)
    rs_b = scatter([dw_qkv, dw_bout], dh2, "scatter_b")
    dz1, dy1, st1 = residual_ln_bwd(
        name="res_bwd1", later=(dh2, dz2, x2, scale[2] + rs_b.zero()), this=(x1, y1, gate1[1], lng_full[1])
    )
    rs_mlp0, dh1 = mlp_backward(0, h1, a1, dy1)
    dz0, dy0, st0 = residual_ln_bwd(
        name="res_bwd0", later=(dh1, dz1, x1, scale[1] + rs_mlp0.zero()), this=(x0, y0, gate1[0], lng_full[0])
    )
    (dp_gate,) = mm_nt(dy0, w_aout3, name="a_dp", tm=2048, tko=1024, ps=1, tc=D_MODEL, epilogue=ident, outs=(F32,))
    da0, d_ws, d_bs, gate_rows = gate_bwd(a_pre, dp_gate, vn_g, vn_b, w_s, bias_full, name="gate_bwd")
    (dh0,) = mm_nt(da0, w_in3, name="a_dh", tm=2048, tko=1024, ps=4, tc=256, epilogue=ident, outs=(F32,))
    dw_aout = mm_tn(p_gate, dy0, name="a_dw_out", p=1, tk=1024, ps=1, tn=D_MODEL, tmc=2048, out_dtype=BF16)
    dw_in = mm_tn(h0, da0, name="a_dw_in", p=N_DEV, tk=1024, ps=4, tn=256, tmc=2048, out_dtype=BF16)
    grad_x, stf = residual_ln_bwd(name="res_bwd_in", later=(dh0, dz0, x0, scale[0]))

    stats_after = [stf, st0, st1, st2]
    stats_own = [st0, st1, st2, st3]
    dm = jnp.stack(
        [
            jnp.concatenate(
                [stats_after[t][ST_DSHIFT], stats_after[t][ST_DSCALE], stats_own[t][ST_DGATE]], axis=0
            )
            for t in range(N_SUB)
        ]
    )
    ncol = 3 * D_MODEL // N_DEV
    dmx = jnp.pad(
        dm.reshape(N_SUB, N_DEV, ncol).transpose(1, 0, 2)[:, :, None, :], ((0, 0), (0, 0), (0, SUBLANES - 1), (0, 0))
    )
    small = [
        gate_rows[0],
        gate_rows[1],
        d_ws.reshape(-1),
        d_bs[:, :A_GROUPS].T.reshape(-1),
        *[stats_own[t][ST_DG] for t in range(N_SUB)],
        *[stats_own[t][ST_DB] for t in range(N_SUB)],
    ]
    n_small = sum(s.size for s in small)
    part_rows = -(-n_small // (N_DEV * LANES * SUBLANES)) * SUBLANES
    flat = jnp.concatenate(small + [jnp.zeros((N_DEV * part_rows * LANES - n_small,), F32)])
    dmrecv, reduced = small_exchange(dmx, flat.reshape(N_DEV, part_rows, LANES))
    reduced = reduced.reshape(-1)
    sizes = [2 * D_MODEL, D_MODEL, D_MODEL, A_GROUPS * CHUNK * CHUNK, A_GROUPS * CHUNK, N_SUB * D_MODEL, N_SUB * D_MODEL]
    offs = [sum(sizes[:i]) for i in range(len(sizes) + 1)]
    g_b_in, g_vn_g, g_vn_b, g_ws, g_bs, g_lng, g_lnb = [reduced[offs[i] : offs[i + 1]] for i in range(len(sizes))]

    results = {}

    def update(wname, gparts, w, m, v):
        shape = w.shape
        w2 = w.reshape(-1, shape[-1])
        outs = adamw(gparts.reshape(gparts.shape[0], *w2.shape), w2, m.reshape(w2.shape), v.reshape(w2.shape), name=f"adamw_{wname}")
        results[wname] = [o.reshape(shape) for o in outs]

    rs_a = scatter([dw_in, dw_aout], reduced, "scatter_a")
    ada_outs = ada_grad_adamw(cg + rs_a.zero(), dmrecv, ada_w4, m_ada_w.reshape(ada_w4.shape), v_ada_w.reshape(ada_w4.shape), name="ada_grad_adamw")
    results["ada_w"] = [o.reshape(ada_w.shape) for o in ada_outs[:4]]
    update("ada_b", ada_outs[4][:, 0, :][None], ada_b, m_ada_b, v_ada_b)
    ln_cols = D_MODEL // N_DEV
    my_ln = lambda gfull: lax.dynamic_slice_in_dim(gfull.reshape(N_SUB, N_DEV, ln_cols), me, 1, axis=1)
    update("ln_g", my_ln(g_lng).reshape(1, N_SUB, ln_cols), ln_g, m_ln_g, v_ln_g)
    update("ln_b", my_ln(g_lnb).reshape(1, N_SUB, ln_cols), ln_b, m_ln_b, v_ln_b)
    update("a_b_in", g_b_in[None], a_b_in, m_a_b_in, v_a_b_in)
    update("a_vn_g", g_vn_g[None], a_vn_g, m_a_vn_g, v_a_vn_g)
    update("a_vn_b", g_vn_b[None], a_vn_b, m_a_vn_b, v_a_vn_b)
    update("a_w_s", g_ws[None], a_w_s, m_a_w_s, v_a_w_s)
    update("a_b_s", g_bs[None], a_b_s, m_a_b_s, v_a_b_s)
    g_up1, g_dn1 = rs_mlp1.wait([0, 1], grad_x, name="scatter_wait_mlp1")
    g_qkv, g_bout = rs_b.wait([0, 1], grad_x, name="scatter_wait_b")
    update("b_w_qkv", g_qkv, b_w_qkv, m_b_w_qkv, v_b_w_qkv)
    update("b_w_out", g_bout, b_w_out, m_b_w_out, v_b_w_out)
    g_up0, g_dn0 = rs_mlp0.wait([0, 1], grad_x, name="scatter_wait_mlp0")
    update("mlp_w_up", jnp.concatenate([g_up0, g_up1], axis=1), mlp_w_up, m_mlp_w_up, v_mlp_w_up)
    update("mlp_w_down", jnp.concatenate([g_dn0, g_dn1], axis=1), mlp_w_down, m_mlp_w_down, v_mlp_w_down)
    g_in, g_aout = rs_a.wait([0, 1], grad_x, name="scatter_wait_a")
    update("a_w_in", g_in, a_w_in, m_a_w_in, v_a_w_in)
    update("a_w_out", g_aout, a_w_out, m_a_w_out, v_a_w_out)

    order = ["ada_w", "ada_b", "ln_g", "ln_b", "a_w_in", "a_b_in", "a_vn_g", "a_vn_b", "a_w_s", "a_b_s", "a_w_out", "b_w_qkv", "b_w_out", "mlp_w_up", "mlp_w_down"]
    return (loss, grad_x[None], *[results[n][0] for n in order], *[results[n][1] for n in order],
            *[results[n][2] for n in order], *[results[n][3] for n in order])
```

```python
import math

import jax
import jax.numpy as jnp
from jax import lax
from jax.experimental import pallas as pl
from jax.experimental.pallas import tpu as pltpu

F32 = jnp.float32
BF16 = jnp.bfloat16
MESH = pl.DeviceIdType.MESH
ANY = pl.BlockSpec(memory_space=pl.ANY)
VMEM = pl.BlockSpec(memory_space=pltpu.VMEM)

N_DEV = 8
D_MODEL = 1024
SEQ = 4096
DEPTH = 2
CHUNK = 128
A_GROUPS = 16
A_GROUP_DIM = D_MODEL // A_GROUPS
B_HEADS = 16
B_HEAD_DIM = 64
B_PATTERNS = ((128, 1), (512, 4), (2048, 16))
N_PAT = len(B_PATTERNS)
SPAN = 128
D_FF = 4 * D_MODEL
D_QKV = N_PAT * 3 * D_MODEL
ALPHA = (2 * DEPTH) ** 0.25
LN_EPS = 1e-5
NEG = -1e30
ADAM_LR = 0.001
ADAM_B1 = 0.9
ADAM_B2 = 0.999
ADAM_EPS = 1e-08
ADAM_WD = 0.01
ADAM_STEP = 10
GELU_C = math.sqrt(2.0 / math.pi)
GELU_A = 0.044715

VMEM_LIMIT_BYTES = 56 * 1024 * 1024
LANES = 128
SUBLANES = 8
ROW_TILE = 512
N_SUB = 2 * DEPTH


def _params(sem):
    return pltpu.CompilerParams(dimension_semantics=sem, vmem_limit_bytes=VMEM_LIMIT_BYTES)


def _lane(shape):
    return lax.broadcasted_iota(jnp.int32, shape, len(shape) - 1)


def _split_bf16(x):
    hi = x.astype(BF16)
    lo = (x - hi.astype(F32)).astype(BF16)
    return hi, lo


def _group_expand_matrix(groups_padded, width):
    per = width // A_GROUPS
    r = lax.broadcasted_iota(jnp.int32, (groups_padded, width), 0)
    c = lax.broadcasted_iota(jnp.int32, (groups_padded, width), 1)
    return (c // per == r).astype(BF16)


def _group_reduce_matrix(width, groups_padded):
    per = width // A_GROUPS
    r = lax.broadcasted_iota(jnp.int32, (width, groups_padded), 0)
    c = lax.broadcasted_iota(jnp.int32, (width, groups_padded), 1)
    return (r // per == c).astype(BF16)


def _expand_groups(w):
    e = _group_expand_matrix(LANES, D_MODEL)
    hi, lo = _split_bf16(w)
    return jnp.dot(hi, e, preferred_element_type=F32) + jnp.dot(lo, e, preferred_element_type=F32)


def _reduce_groups(x):
    e = _group_reduce_matrix(D_MODEL, LANES)
    hi, lo = _split_bf16(x)
    return jnp.dot(hi, e, preferred_element_type=F32) + jnp.dot(lo, e, preferred_element_type=F32)


def _column_tiles(p, n, ps, tn):
    assert (ps == 1 or tn == n) and p % ps == 0 and n % tn == 0
    q = n // tn
    return (p // ps) * q, q


def _extra_specs(extras, tm, width):
    specs = []
    for _, kind in extras:
        if kind == "row":
            specs.append(pl.BlockSpec((1, width), lambda i, j, c: (0, j)))
        else:
            specs.append(pl.BlockSpec((tm, width), lambda i, j, c: (i, j)))
    return specs


def mm_nn(a, b3, *, name, tm, ps, tn, tk, epilogue, extras=(), outs, prologue=None):
    m, k = a.shape
    p, _, n = b3.shape
    nj, q = _column_tiles(p, n, ps, tn)
    nk = k // tk
    width = ps * tn

    def body(a_ref, b_ref, *rest):
        ex = rest[: len(extras)]
        out_refs = rest[len(extras) : len(extras) + len(outs)]
        kk = pl.program_id(2)
        av = a_ref[...] if prologue is None else prologue(a_ref[...])

        def finish(cs, acc):
            res = epilogue(acc, *[e[:, cs] for e in ex])
            for o_ref, r in zip(out_refs, res, strict=True):
                o_ref[:, cs] = r.astype(o_ref.dtype)

        for s in range(ps):
            cs = slice(s * tn, (s + 1) * tn)
            part = jnp.dot(av, b_ref[s], preferred_element_type=F32)
            if nk == 1:
                finish(cs, part)
                continue
            acc_ref = rest[-1]

            @pl.when(kk == 0)
            def _(part=part, cs=cs):
                acc_ref[:, cs] = part

            @pl.when(kk > 0)
            def _(part=part, cs=cs):
                acc_ref[:, cs] += part

        if nk > 1:

            @pl.when(kk == nk - 1)
            def _():
                for s in range(ps):
                    cs = slice(s * tn, (s + 1) * tn)
                    finish(cs, rest[-1][:, cs])

    return pl.pallas_call(
        body,
        name=name,
        grid=(m // tm, nj, nk),
        in_specs=[
            pl.BlockSpec((tm, tk), lambda i, j, kk: (i, kk)),
            pl.BlockSpec((ps, tk, tn), lambda i, j, kk: (j // q, kk, j % q)),
            *_extra_specs(extras, tm, width),
        ],
        out_specs=[pl.BlockSpec((tm, width), lambda i, j, kk: (i, j)) for _ in outs],
        out_shape=[jax.ShapeDtypeStruct((m, p * n), dt) for dt in outs],
        scratch_shapes=[pltpu.VMEM((tm, width), F32)] if nk > 1 else [],
        compiler_params=_params(("parallel", "parallel", "arbitrary")),
    )(a, b3, *[arr for arr, _ in extras])


def mm_nt(g, b3, *, name, tm, tko, ps, tc, epilogue, extras=(), outs):
    m = g.shape[0]
    p, k, n = b3.shape
    nc, q = _column_tiles(p, n, ps, tc)

    def body(g_ref, b_ref, *rest):
        ex = rest[: len(extras)]
        out_refs = rest[len(extras) : len(extras) + len(outs)]
        c = pl.program_id(2)
        part = None
        for s in range(ps):
            d = lax.dot_general(
                g_ref[:, s * tc : (s + 1) * tc], b_ref[s], (((1,), (1,)), ((), ())), preferred_element_type=F32
            )
            part = d if part is None else part + d

        def finish(acc):
            res = epilogue(acc, *[e[...] for e in ex])
            for o_ref, r in zip(out_refs, res, strict=True):
                o_ref[...] = r.astype(o_ref.dtype)

        if nc == 1:
            finish(part)
            return
        acc_ref = rest[-1]

        @pl.when(c == 0)
        def _():
            acc_ref[...] = part

        @pl.when(c > 0)
        def _():
            acc_ref[...] += part

        @pl.when(c == nc - 1)
        def _():
            finish(acc_ref[...])

    return pl.pallas_call(
        body,
        name=name,
        grid=(m // tm, k // tko, nc),
        in_specs=[
            pl.BlockSpec((tm, ps * tc), lambda i, j, c: (i, c)),
            pl.BlockSpec((ps, tko, tc), lambda i, j, c: (c // q, j, c % q)),
            *_extra_specs(extras, tm, tko),
        ],
        out_specs=[pl.BlockSpec((tm, tko), lambda i, j, c: (i, j)) for _ in outs],
        out_shape=[jax.ShapeDtypeStruct((m, k), dt) for dt in outs],
        scratch_shapes=[pltpu.VMEM((tm, tko), F32)] if nc > 1 else [],
        compiler_params=_params(("parallel", "parallel", "arbitrary")),
    )(g, b3, *[arr for arr, _ in extras])


def mm_tn(a, g, *, name, p, tk, ps, tn, tmc, out_dtype, prologue=None):
    m, k = a.shape
    n = g.shape[1] // p
    nj, q = _column_tiles(p, n, ps, tn)
    nc = m // tmc

    def body(a_ref, g_ref, o_ref, acc_ref):
        c = pl.program_id(2)
        av = a_ref[...] if prologue is None else prologue(a_ref[...])
        part = lax.dot_general(av, g_ref[...], (((0,), (0,)), ((), ())), preferred_element_type=F32)

        @pl.when(c == 0)
        def _():
            acc_ref[...] = part

        @pl.when(c > 0)
        def _():
            acc_ref[...] += part

        @pl.when(c == nc - 1)
        def _():
            for s in range(ps):
                o_ref[s] = acc_ref[:, s * tn : (s + 1) * tn].astype(o_ref.dtype)

    return pl.pallas_call(
        body,
        name=name,
        grid=(k // tk, nj, nc),
        in_specs=[
            pl.BlockSpec((tmc, tk), lambda i, j, c: (c, i)),
            pl.BlockSpec((tmc, ps * tn), lambda i, j, c: (c, j)),
        ],
        out_specs=pl.BlockSpec((ps, tk, tn), lambda i, j, c: (j // q, i, j % q)),
        out_shape=jax.ShapeDtypeStruct((p, k, n), out_dtype),
        scratch_shapes=[pltpu.VMEM((tk, ps * tn), F32)],
        compiler_params=_params(("parallel", "parallel", "arbitrary")),
    )(a, g)


def _rows(cols):
    return pl.BlockSpec((ROW_TILE, cols), lambda i: (i, 0))


def _vec(cols, rows=1):
    return pl.BlockSpec((rows, cols), lambda i: (0, 0))


def _layer_norm_hat(z):
    mu = jnp.mean(z, axis=-1, keepdims=True)
    zc = z - mu
    var = jnp.mean(zc * zc, axis=-1, keepdims=True)
    rstd = lax.rsqrt(var + LN_EPS)
    return zc * rstd, rstd


def modulate(x, scale, shift, *, name):
    s, d = x.shape

    def body(x_ref, sc_ref, sh_ref, h_ref):
        h_ref[...] = (x_ref[...] * (1.0 + sc_ref[...]) + sh_ref[...]).astype(BF16)

    return pl.pallas_call(
        body,
        name=name,
        grid=(s // ROW_TILE,),
        in_specs=[_rows(d), _vec(d), _vec(d)],
        out_specs=_rows(d),
        out_shape=jax.ShapeDtypeStruct((s, d), BF16),
        compiler_params=_params(("parallel",)),
    )(x, scale, shift)


def residual_ln(x, y, gate1, g, b, nscale, nshift, *, name):
    s, d = x.shape

    def body(x_ref, y_ref, gt_ref, g_ref, b_ref, sc_ref, sh_ref, xn_ref, hn_ref):
        z = ALPHA * x_ref[...] + gt_ref[...] * y_ref[...].astype(F32)
        xhat, _ = _layer_norm_hat(z)
        xn = xhat * g_ref[...] + b_ref[...]
        xn_ref[...] = xn
        hn_ref[...] = (xn * (1.0 + sc_ref[...]) + sh_ref[...]).astype(BF16)

    return pl.pallas_call(
        body,
        name=name,
        grid=(s // ROW_TILE,),
        in_specs=[_rows(d), _rows(d), _vec(d), _vec(d), _vec(d), _vec(d), _vec(d)],
        out_specs=[_rows(d), _rows(d)],
        out_shape=[jax.ShapeDtypeStruct((s, d), F32), jax.ShapeDtypeStruct((s, d), BF16)],
        compiler_params=_params(("parallel",)),
    )(x, y, gate1, g, b, nscale, nshift)


def residual_ln_loss(x, y, gate1, g, b, target, *, name):
    s, d = x.shape

    def body(x_ref, y_ref, gt_ref, g_ref, b_ref, t_ref, dx_ref, loss_ref):
        z = ALPHA * x_ref[...] + gt_ref[...] * y_ref[...].astype(F32)
        xhat, _ = _layer_norm_hat(z)
        err = xhat * g_ref[...] + b_ref[...] - t_ref[...]
        dx_ref[...] = err * (1.0 / d)
        part = jnp.sum(jnp.sum(err * err, axis=1, keepdims=True), axis=0, keepdims=True) * (0.5 / d)

        @pl.when(pl.program_id(0) == 0)
        def _():
            loss_ref[...] = part

        @pl.when(pl.program_id(0) > 0)
        def _():
            loss_ref[...] += part

    return pl.pallas_call(
        body,
        name=name,
        grid=(s // ROW_TILE,),
        in_specs=[_rows(d), _rows(d), _vec(d), _vec(d), _vec(d), _rows(d)],
        out_specs=[_rows(d), pl.BlockSpec((1, 1), lambda i: (0, 0))],
        out_shape=[jax.ShapeDtypeStruct((s, d), F32), jax.ShapeDtypeStruct((1, 1), F32)],
        compiler_params=_params(("arbitrary",)),
    )(x, y, gate1, g, b, target)


ST_DSCALE, ST_DSHIFT, ST_DG, ST_DB, ST_DGATE = 0, 1, 2, 3, 4


def residual_ln_bwd(*, name, later=None, dxo=None, x_out=None, this=None):
    lead = later[0] if later is not None else dxo
    s, d = lead.shape
    has_later, has_ln = later is not None, this is not None

    def body(*refs):
        refs = list(refs)
        if has_later:
            dh_ref, dzl_ref, scl_ref = refs[:3]
            refs = refs[3:]
        else:
            dxo_ref = refs.pop(0)
        if has_ln:
            x_ref, y_ref, gt_ref, g_ref, b_ref = refs[:5]
            refs = refs[5:]
            dz_ref, dy_ref, st_ref = refs
        else:
            xo_ref, dx_ref, st_ref = refs

        @pl.when(pl.program_id(0) == 0)
        def _():
            st_ref[...] = jnp.zeros_like(st_ref)

        def acc(row, val):
            st_ref[row : row + 1, :] += jnp.sum(val, axis=0, keepdims=True)

        if has_ln:
            y = y_ref[...].astype(F32)
            gate1 = gt_ref[...]
            xhat, rstd = _layer_norm_hat(ALPHA * x_ref[...] + gate1 * y)
            x_out_v = xhat * g_ref[...] + b_ref[...]
        else:
            x_out_v = xo_ref[...]
        if has_later:
            dh = dh_ref[...]
            g_out = ALPHA * dzl_ref[...] + dh * (1.0 + scl_ref[...])
            acc(ST_DSCALE, dh * x_out_v)
            acc(ST_DSHIFT, dh)
        else:
            g_out = dxo_ref[...]
        if not has_ln:
            dx_ref[...] = g_out
            return
        acc(ST_DG, g_out * xhat)
        acc(ST_DB, g_out)
        dxh = g_out * g_ref[...]
        m1 = jnp.mean(dxh, axis=-1, keepdims=True)
        m2 = jnp.mean(dxh * xhat, axis=-1, keepdims=True)
        dz = rstd * (dxh - m1 - xhat * m2)
        acc(ST_DGATE, dz * y)
        dz_ref[...] = dz
        dy_ref[...] = (dz * gate1).astype(BF16)

    ins, specs = [], []
    if has_later:
        ins += list(later)
        specs += [_rows(d), _rows(d), _vec(d)]
    else:
        ins += [dxo]
        specs += [_rows(d)]
    if not has_ln:
        ins += [x_out]
        specs += [_rows(d)]
    if has_ln:
        ins += list(this)
        specs += [_rows(d), _rows(d), _vec(d), _vec(d), _vec(d)]
        out_specs = [_rows(d), _rows(d), _vec(d, SUBLANES)]
        out_shape = [
            jax.ShapeDtypeStruct((s, d), F32),
            jax.ShapeDtypeStruct((s, d), BF16),
            jax.ShapeDtypeStruct((SUBLANES, d), F32),
        ]
    else:
        out_specs = [_rows(d), _vec(d, SUBLANES)]
        out_shape = [jax.ShapeDtypeStruct((s, d), F32), jax.ShapeDtypeStruct((SUBLANES, d), F32)]
    return pl.pallas_call(
        body,
        name=name,
        grid=(s // ROW_TILE,),
        in_specs=specs,
        out_specs=out_specs,
        out_shape=out_shape,
        compiler_params=_params(("arbitrary",)),
    )(*ins)


GATE_CHUNKS = 4


def _gelu(x):
    return 0.5 * x * (1.0 + jnp.tanh(GELU_C * (x + GELU_A * x * x * x)))


def _gelu_grad(x):
    t = jnp.tanh(GELU_C * (x + GELU_A * x * x * x))
    return 0.5 * (1.0 + t) + 0.5 * x * (1.0 - t * t) * (GELU_C * (1.0 + 3.0 * GELU_A * x * x))


def _causal_weights(w_ref, transpose):
    t = lax.broadcasted_iota(jnp.int32, (CHUNK, CHUNK), 0)
    s = lax.broadcasted_iota(jnp.int32, (CHUNK, CHUNK), 1)
    out = []
    for g in range(A_GROUPS):
        w = jnp.where(t >= s, w_ref[g], 0.0)
        out.append((w.T if transpose else w).astype(BF16))
    return out


def _spatial(ws, vn, lo_mask):
    rows = vn.shape[0]
    out_rows = []
    for r in range(rows // CHUNK):
        cols = []
        for j in range(A_GROUPS // 2):
            blk = vn[r * CHUNK : (r + 1) * CHUNK, j * LANES : (j + 1) * LANES]
            za = jnp.dot(ws[2 * j], blk, preferred_element_type=F32)
            zb = jnp.dot(ws[2 * j + 1], blk, preferred_element_type=F32)
            cols.append(jnp.where(lo_mask, za, zb))
        out_rows.append(jnp.concatenate(cols, axis=1))
    return jnp.concatenate(out_rows, axis=0)


def _gate_forward(a, vg, vb, ws, bias, lo_mask):
    u = _gelu(a[:, :D_MODEL])
    v = _gelu(a[:, D_MODEL:])
    vhat, rstd = _layer_norm_hat(v)
    vn = (vhat * vg + vb).astype(BF16)
    z = _spatial(ws, vn, lo_mask) + jnp.concatenate([bias] * (a.shape[0] // CHUNK), axis=0)
    return u, vhat, rstd, vn, z


def gate_fwd(a_pre, vn_g, vn_b, w_s, bias_full, *, name):
    s = a_pre.shape[0]
    tr = GATE_CHUNKS * CHUNK

    def body(a_ref, vg_ref, vb_ref, w_ref, bias_ref, p_ref):
        lo_mask = _lane((CHUNK, LANES)) < A_GROUP_DIM
        ws = _causal_weights(w_ref, transpose=False)
        u, _, _, _, z = _gate_forward(a_ref[...].astype(F32), vg_ref[...], vb_ref[...], ws, bias_ref[...], lo_mask)
        p_ref[...] = (u * z).astype(BF16)

    return pl.pallas_call(
        body,
        name=name,
        grid=(s // tr,),
        in_specs=[
            pl.BlockSpec((tr, 2 * D_MODEL), lambda i: (i, 0)),
            _vec(D_MODEL),
            _vec(D_MODEL),
            pl.BlockSpec((A_GROUPS, CHUNK, CHUNK), lambda i: (0, 0, 0)),
            _vec(D_MODEL, CHUNK),
        ],
        out_specs=pl.BlockSpec((tr, D_MODEL), lambda i: (i, 0)),
        out_shape=jax.ShapeDtypeStruct((s, D_MODEL), BF16),
        compiler_params=_params(("parallel",)),
    )(a_pre, vn_g, vn_b, w_s, bias_full)


def gate_bwd(a_pre, dp, vn_g, vn_b, w_s, bias_full, *, name):
    s = a_pre.shape[0]
    tr = GATE_CHUNKS * CHUNK
    nsteps = s // tr

    def body(a_ref, dp_ref, vg_ref, vb_ref, w_ref, bias_ref, da_ref, dw_ref, dbs_ref, rows_ref, dbias_acc):
        step = pl.program_id(0)
        lo_mask = _lane((CHUNK, LANES)) < A_GROUP_DIM

        @pl.when(step == 0)
        def _():
            dw_ref[...] = jnp.zeros_like(dw_ref)
            rows_ref[...] = jnp.zeros_like(rows_ref)
            dbias_acc[...] = jnp.zeros_like(dbias_acc)

        a = a_ref[...].astype(F32)
        vg = vg_ref[...]
        ws = _causal_weights(w_ref, transpose=False)
        wts = _causal_weights(w_ref, transpose=True)
        u, vhat, rstd, vn, z = _gate_forward(a, vg, vb_ref[...], ws, bias_ref[...], lo_mask)
        dp = dp_ref[...]
        du = dp * z
        dzz = dp * u
        dzz_b = dzz.astype(BF16)
        dvn = _spatial(wts, dzz_b, lo_mask)
        dbias = None
        for r in range(GATE_CHUNKS):
            rs = slice(r * CHUNK, (r + 1) * CHUNK)
            dbias = dzz[rs] if dbias is None else dbias + dzz[rs]
            for j in range(A_GROUPS // 2):
                cs = slice(j * LANES, (j + 1) * LANES)
                dblk = dzz[rs, cs]
                vblk = vn[rs, cs]
                for half in range(2):
                    keep = lo_mask if half == 0 else jnp.logical_not(lo_mask)
                    dm = jnp.where(keep, dblk, 0.0).astype(BF16)
                    dw_ref[2 * j + half] += lax.dot_general(
                        dm, vblk, (((1,), (1,)), ((), ())), preferred_element_type=F32
                    )
        dbias_acc[...] += dbias
        rows_ref[1:2, :D_MODEL] += jnp.sum(dvn * vhat, axis=0, keepdims=True)
        rows_ref[1:2, D_MODEL:] += jnp.sum(dvn, axis=0, keepdims=True)
        dvh = dvn * vg
        m1 = jnp.mean(dvh, axis=-1, keepdims=True)
        m2 = jnp.mean(dvh * vhat, axis=-1, keepdims=True)
        dv = rstd * (dvh - m1 - vhat * m2)
        da_u = du * _gelu_grad(a[:, :D_MODEL])
        da_v = dv * _gelu_grad(a[:, D_MODEL:])
        da_ref[:, :D_MODEL] = da_u.astype(BF16)
        da_ref[:, D_MODEL:] = da_v.astype(BF16)
        rows_ref[0:1, :D_MODEL] += jnp.sum(da_u, axis=0, keepdims=True)
        rows_ref[0:1, D_MODEL:] += jnp.sum(da_v, axis=0, keepdims=True)

        @pl.when(step == nsteps - 1)
        def _():
            t = lax.broadcasted_iota(jnp.int32, (CHUNK, CHUNK), 0)
            sx = lax.broadcasted_iota(jnp.int32, (CHUNK, CHUNK), 1)
            for g in range(A_GROUPS):
                dw_ref[g] = jnp.where(t >= sx, dw_ref[g], 0.0)
            dbs_ref[...] = _reduce_groups(dbias_acc[...])

    return pl.pallas_call(
        body,
        name=name,
        grid=(nsteps,),
        in_specs=[
            pl.BlockSpec((tr, 2 * D_MODEL), lambda i: (i, 0)),
            pl.BlockSpec((tr, D_MODEL), lambda i: (i, 0)),
            _vec(D_MODEL),
            _vec(D_MODEL),
            pl.BlockSpec((A_GROUPS, CHUNK, CHUNK), lambda i: (0, 0, 0)),
            _vec(D_MODEL, CHUNK),
        ],
        out_specs=[
            pl.BlockSpec((tr, 2 * D_MODEL), lambda i: (i, 0)),
            pl.BlockSpec((A_GROUPS, CHUNK, CHUNK), lambda i: (0, 0, 0)),
            _vec(LANES, CHUNK),
            _vec(2 * D_MODEL, SUBLANES),
        ],
        out_shape=[
            jax.ShapeDtypeStruct((s, 2 * D_MODEL), BF16),
            jax.ShapeDtypeStruct((A_GROUPS, CHUNK, CHUNK), F32),
            jax.ShapeDtypeStruct((CHUNK, LANES), F32),
            jax.ShapeDtypeStruct((SUBLANES, 2 * D_MODEL), F32),
        ],
        scratch_shapes=[pltpu.VMEM((CHUNK, D_MODEL), F32)],
        compiler_params=_params(("arbitrary",)),
    )(a_pre, dp, vn_g, vn_b, w_s, bias_full)


def alibi_tables(dilation):
    qi = jnp.arange(SPAN)[:, None]
    ki = jnp.arange(2 * SPAN)[None, :]
    diff = SPAN + qi - ki
    valid = (diff >= 0) & (diff <= SPAN)
    heads = jnp.arange(1, B_HEADS + 1, dtype=F32)
    slopes = jnp.exp2(-8.0 * heads / B_HEADS)
    bias = -slopes[:, None, None] * (dilation * diff).astype(F32)
    bias = jnp.where(valid[None], bias, NEG).reshape(B_HEADS // 2, 2 * SPAN, 2 * SPAN)
    return bias, bias.transpose(0, 2, 1)


def _pair_rows(x, halves):
    return jnp.concatenate([x * halves[0], x * halves[1]], axis=0)


def _pair_column(v, lane, j):
    pick = lambda h: jnp.sum(jnp.where(lane == h, v, 0.0), axis=1, keepdims=True)
    return jnp.concatenate([pick(2 * j), pick(2 * j + 1)], axis=0)


_NT = (((1,), (1,)), ((), ()))


def permute_rows(x, dilation, *, inverse, name, add=None):
    s, w = x.shape
    tile = SPAN * dilation
    nat = pl.BlockSpec((tile, w), lambda i: (i, 0))
    streams = pl.BlockSpec((dilation, SPAN, w), lambda i: (0, i, 0))
    x3 = x.reshape(dilation, s // dilation, w) if inverse else x

    def body(*refs):
        if not inverse:
            x_ref, o_ref = refs
            o_ref[...] = jnp.swapaxes(x_ref[...].reshape(SPAN, dilation, w), 0, 1)
            return
        val = jnp.swapaxes(refs[0][...], 0, 1).reshape(tile, w)
        if add is not None:
            val = val + refs[1][...]
        refs[-1][...] = val

    out = pl.pallas_call(
        body,
        name=name,
        grid=(s // tile,),
        in_specs=([streams] + ([nat] if add is not None else [])) if inverse else [nat],
        out_specs=nat if inverse else streams,
        out_shape=jax.ShapeDtypeStruct((s, w) if inverse else (dilation, s // dilation, w), x.dtype),
        compiler_params=_params(("parallel",)),
    )(*([x3] + ([add] if add is not None else [])))
    return out.reshape(s, w)


def _qkv_specs(block_of):
    def spec(which, prev):
        def index(*grid):
            blk = block_of(*grid)
            return (jnp.maximum(blk - 1, 0) if prev else blk, which)

        return pl.BlockSpec((SPAN, D_MODEL), index)

    return [spec(0, False), spec(1, True), spec(1, False), spec(2, True), spec(2, False)]


def attn_fwd(qkv_p, pat, *, name):
    _, dilation = B_PATTERNS[pat]
    nb = SEQ // dilation // SPAN
    bias, _ = alibi_tables(dilation)

    def body(q_ref, kp_ref, kc_ref, vp_ref, vc_ref, bias_ref, o_ref, lse_ref):
        n = pl.program_id(1)
        lane = _lane((SPAN, LANES))
        lo_mask = lane < B_HEAD_DIM
        first_prev = jnp.logical_and(n == 0, _lane((2 * SPAN, 2 * SPAN)) < SPAN)
        q = q_ref[...] * jnp.asarray(B_HEAD_DIM**-0.5, BF16)
        kk = jnp.concatenate([kp_ref[...], kc_ref[...]], axis=0)
        vv = jnp.concatenate([vp_ref[...], vc_ref[...]], axis=0)
        halves = (lo_mask.astype(BF16), jnp.logical_not(lo_mask).astype(BF16))
        stats = jnp.zeros((SPAN, LANES), F32)
        for j in range(B_HEADS // 2):
            cs = slice(j * LANES, (j + 1) * LANES)
            sc = lax.dot_general(_pair_rows(q[:, cs], halves), kk[:, cs], _NT, preferred_element_type=F32)
            sc = jnp.where(first_prev, NEG, sc + bias_ref[j])
            m = jnp.max(sc, axis=1, keepdims=True)
            p = jnp.exp(sc - m)
            l = jnp.sum(p, axis=1, keepdims=True)
            acc = jnp.dot(p.astype(BF16), vv[:, cs], preferred_element_type=F32) * (1.0 / l)
            lse_pair = m + jnp.log(l)
            o_ref[:, cs] = jnp.where(lo_mask, acc[:SPAN], acc[SPAN:]).astype(BF16)
            stats = jnp.where(lane == 2 * j, lse_pair[:SPAN], stats)
            stats = jnp.where(lane == 2 * j + 1, lse_pair[SPAN:], stats)
        lse_ref[...] = stats

    return pl.pallas_call(
        body,
        name=name,
        grid=(dilation, nb),
        in_specs=[
            *_qkv_specs(lambda r, n: r * nb + n),
            pl.BlockSpec((B_HEADS // 2, 2 * SPAN, 2 * SPAN), lambda r, n: (0, 0, 0)),
        ],
        out_specs=[
            pl.BlockSpec((SPAN, D_MODEL), lambda r, n: (r * nb + n, 0)),
            pl.BlockSpec((SPAN, LANES), lambda r, n: (r * nb + n, 0)),
        ],
        out_shape=[jax.ShapeDtypeStruct((SEQ, D_MODEL), BF16), jax.ShapeDtypeStruct((SEQ, LANES), F32)],
        compiler_params=_params(("parallel", "arbitrary")),
    )(qkv_p, qkv_p, qkv_p, qkv_p, qkv_p, bias)


def attn_combine(outs, lses, *, name):
    def body(o0, o1, o2, l0, l1, l2, ob_ref, of_ref, lse_ref):
        ls = [l0[...], l1[...], l2[...]]
        m = jnp.maximum(jnp.maximum(ls[0], ls[1]), ls[2])
        tot = jnp.log(jnp.exp(ls[0] - m) + jnp.exp(ls[1] - m) + jnp.exp(ls[2] - m)) + m
        o = None
        for o_ref, l in zip((o0, o1, o2), ls, strict=True):
            term = _expand_groups(jnp.exp(l - tot)) * o_ref[...]
            o = term if o is None else o + term
        ob_ref[...] = o.astype(BF16)
        of_ref[...] = o
        lse_ref[...] = tot

    return pl.pallas_call(
        body,
        name=name,
        grid=(SEQ // ROW_TILE,),
        in_specs=[_rows(D_MODEL)] * 3 + [_rows(LANES)] * 3,
        out_specs=[_rows(D_MODEL), _rows(D_MODEL), _rows(LANES)],
        out_shape=[
            jax.ShapeDtypeStruct((SEQ, D_MODEL), BF16),
            jax.ShapeDtypeStruct((SEQ, D_MODEL), F32),
            jax.ShapeDtypeStruct((SEQ, LANES), F32),
        ],
        compiler_params=_params(("parallel",)),
    )(*outs, *lses)


def attn_delta(do, o, *, name):
    def body(do_ref, o_ref, dob_ref, dl_ref):
        do_v = do_ref[...]
        dob_ref[...] = do_v.astype(BF16)
        dl_ref[...] = _reduce_groups(do_v * o_ref[...])

    return pl.pallas_call(
        body,
        name=name,
        grid=(SEQ // ROW_TILE,),
        in_specs=[_rows(D_MODEL), _rows(D_MODEL)],
        out_specs=[_rows(D_MODEL), _rows(LANES)],
        out_shape=[jax.ShapeDtypeStruct((SEQ, D_MODEL), BF16), jax.ShapeDtypeStruct((SEQ, LANES), F32)],
        compiler_params=_params(("parallel",)),
    )(do, o)


def attn_bwd(qkv_p, do_p, lse_p, delta_p, pat, *, name):
    _, dilation = B_PATTERNS[pat]
    nb = SEQ // dilation // SPAN
    n_blocks = SEQ // SPAN
    bias, bias_t = alibi_tables(dilation)
    last = n_blocks - 1
    q_cols, k_cols, v_cols = (slice(i * D_MODEL, (i + 1) * D_MODEL) for i in range(3))

    def body(q_ref, kp_ref, kc_ref, vp_ref, vc_ref, do_ref, lse_ref, dl_ref, bias_ref, biast_ref, out_ref, cq_ref, ck_ref, cv_ref):
        g = pl.program_id(0)

        @pl.when(g == n_blocks)
        def _():
            out_ref[:, q_cols] = cq_ref[...].astype(BF16)
            out_ref[:, k_cols] = ck_ref[...].astype(BF16)
            out_ref[:, v_cols] = cv_ref[...].astype(BF16)

        @pl.when(g == 0)
        def _():
            cq_ref[...] = jnp.zeros_like(cq_ref)
            ck_ref[...] = jnp.zeros_like(ck_ref)
            cv_ref[...] = jnp.zeros_like(cv_ref)

        @pl.when(g < n_blocks)
        def _():
            lane = _lane((SPAN, LANES))
            lo_mask = lane < B_HEAD_DIM
            pair = (2 * SPAN, 2 * SPAN)
            first = lax.rem(g, nb) == 0
            prev_key_cols = jnp.logical_and(first, _lane(pair) < SPAN)
            prev_key_rows = jnp.logical_and(first, lax.broadcasted_iota(jnp.int32, pair, 0) < SPAN)
            q = q_ref[...] * jnp.asarray(B_HEAD_DIM**-0.5, BF16)
            kk = jnp.concatenate([kp_ref[...], kc_ref[...]], axis=0)
            vv = jnp.concatenate([vp_ref[...], vc_ref[...]], axis=0)
            do_v = do_ref[...]
            lse_v = lse_ref[...]
            dl_v = dl_ref[...]
            lse_t = lse_v.T
            dl_t = dl_v.T
            halves = (lo_mask.astype(BF16), jnp.logical_not(lo_mask).astype(BF16))
            for j in range(B_HEADS // 2):
                cs = slice(j * LANES, (j + 1) * LANES)
                kp, vp = kk[:, cs], vv[:, cs]
                q2 = _pair_rows(q[:, cs], halves)
                do2 = _pair_rows(do_v[:, cs], halves)
                lse_c, dl_c = _pair_column(lse_v, lane, j), _pair_column(dl_v, lane, j)
                lse_r = jnp.concatenate([lse_t[2 * j : 2 * j + 1], lse_t[2 * j + 1 : 2 * j + 2]], axis=1)
                dl_r = jnp.concatenate([dl_t[2 * j : 2 * j + 1], dl_t[2 * j + 1 : 2 * j + 2]], axis=1)
                sc = lax.dot_general(q2, kp, _NT, preferred_element_type=F32)
                p = jnp.exp(jnp.where(prev_key_cols, NEG, sc + bias_ref[j]) - lse_c)
                dp = lax.dot_general(do2, vp, _NT, preferred_element_type=F32)
                ds = (p * (dp - dl_c)).astype(BF16)
                dq2 = jnp.dot(ds, kp, preferred_element_type=F32)
                sc_t = lax.dot_general(kp, q2, _NT, preferred_element_type=F32)
                p_t = jnp.exp(jnp.where(prev_key_rows, NEG, sc_t + biast_ref[j]) - lse_r)
                dp_t = lax.dot_general(vp, do2, _NT, preferred_element_type=F32)
                ds_t = (p_t * (dp_t - dl_r)).astype(BF16)
                dk_pair = jnp.dot(ds_t, q2, preferred_element_type=F32)
                dv_pair = jnp.dot(p_t.astype(BF16), do2, preferred_element_type=F32)
                oq = slice(j * LANES, (j + 1) * LANES)
                ok = slice(D_MODEL + j * LANES, D_MODEL + (j + 1) * LANES)
                ov = slice(2 * D_MODEL + j * LANES, 2 * D_MODEL + (j + 1) * LANES)
                out_ref[:, oq] = cq_ref[:, cs].astype(BF16)
                out_ref[:, ok] = (ck_ref[:, cs] + dk_pair[:SPAN]).astype(BF16)
                out_ref[:, ov] = (cv_ref[:, cs] + dv_pair[:SPAN]).astype(BF16)
                cq_ref[:, cs] = jnp.where(lo_mask, dq2[:SPAN], dq2[SPAN:]) * (B_HEAD_DIM**-0.5)
                ck_ref[:, cs] = dk_pair[SPAN:]
                cv_ref[:, cs] = dv_pair[SPAN:]

    def block_of(g):
        return jnp.minimum(g, last)

    def row_spec(width):
        return pl.BlockSpec((SPAN, width), lambda g: (block_of(g), 0))

    return pl.pallas_call(
        body,
        name=name,
        grid=(n_blocks + 1,),
        in_specs=[
            *_qkv_specs(block_of),
            row_spec(D_MODEL),
            row_spec(LANES),
            row_spec(LANES),
            pl.BlockSpec((B_HEADS // 2, 2 * SPAN, 2 * SPAN), lambda g: (0, 0, 0)),
            pl.BlockSpec((B_HEADS // 2, 2 * SPAN, 2 * SPAN), lambda g: (0, 0, 0)),
        ],
        out_specs=pl.BlockSpec((SPAN, 3 * D_MODEL), lambda g: (jnp.maximum(g - 1, 0), 0)),
        out_shape=jax.ShapeDtypeStruct((SEQ, 3 * D_MODEL), BF16),
        scratch_shapes=[pltpu.VMEM((SPAN, D_MODEL), F32)] * 3,
        compiler_params=_params(("arbitrary",)),
    )(qkv_p, qkv_p, qkv_p, qkv_p, qkv_p, do_p, lse_p, delta_p, bias, bias_t)


def _position():
    x, y, c = lax.axis_index("x"), lax.axis_index("y"), lax.axis_index("c")
    return x, y, c, 4 * x + 2 * y + c


def _peer(k, x, y, c):
    px = 1 - x if k & 4 else x
    py = 1 - y if k & 2 else y
    pc = 1 - c if k & 1 else c
    return (px, py, pc), 4 * px + 2 * py + pc


def _remote(src, dst, send_sem, recv_sem, device):
    return pltpu.make_async_remote_copy(
        src_ref=src, dst_ref=dst, send_sem=send_sem, recv_sem=recv_sem, device_id=device, device_id_type=MESH
    )


def _silu_bf16(cf):
    return (cf * (1.0 / (1.0 + jnp.exp(-cf)))).astype(BF16)


def ada_exchange(c8, w4, b4, ln8):
    nt, _, ncol = w4.shape

    def body(c8_ref, w_ref, b_ref, ln_ref, cg_ref, lng_ref, mrecv_ref, mloc_ref, send_sems, recv_sems):
        x, y, c, me = _position()
        cg_ref[me] = c8_ref[...]
        lng_ref[me] = ln_ref[...]
        first = []
        for k in range(1, N_DEV):
            dev, _ = _peer(k, x, y, c)
            first.append(_remote(c8_ref, cg_ref.at[me], send_sems.at[0, k], recv_sems.at[0, k], dev))
            first.append(_remote(ln_ref, lng_ref.at[me], send_sems.at[1, k], recv_sems.at[1, k], dev))
        for cp in first:
            cp.start()
        for k in range(1, N_DEV):
            dev, pid = _peer(k, x, y, c)
            _remote(c8_ref, cg_ref.at[pid], send_sems.at[0, k], recv_sems.at[0, k], dev).wait_recv()
            _remote(ln_ref, lng_ref.at[pid], send_sems.at[1, k], recv_sems.at[1, k], dev).wait_recv()
        sc = _silu_bf16(cg_ref[...].reshape(N_DEV * SUBLANES, D_MODEL))
        for t in range(nt):
            mloc_ref[t] = jnp.dot(sc, w_ref[t].astype(BF16), preferred_element_type=F32) + b_ref[t : t + 1, :]

        def group(dev_id):
            return pl.ds(pl.multiple_of(dev_id * SUBLANES, SUBLANES), SUBLANES)

        mrecv_ref[me] = mloc_ref[:, group(me), :]
        second = []
        for k in range(1, N_DEV):
            dev, pid = _peer(k, x, y, c)
            second.append(
                _remote(mloc_ref.at[:, group(pid), :], mrecv_ref.at[me], send_sems.at[2, k], recv_sems.at[2, k], dev)
            )
        for cp in second:
            cp.start()
        for k in range(1, N_DEV):
            dev, pid = _peer(k, x, y, c)
            _remote(
                mloc_ref.at[:, group(pid), :], mrecv_ref.at[pid], send_sems.at[2, k], recv_sems.at[2, k], dev
            ).wait_recv()
        for cp in first + second:
            cp.wait_send()

    return pl.pallas_call(
        body,
        name="ada_exchange",
        in_specs=[VMEM, VMEM, VMEM, VMEM],
        out_specs=[VMEM, VMEM, VMEM],
        out_shape=[
            jax.ShapeDtypeStruct((N_DEV, SUBLANES, D_MODEL), F32),
            jax.ShapeDtypeStruct((N_DEV, SUBLANES, LANES), F32),
            jax.ShapeDtypeStruct((N_DEV, nt, SUBLANES, ncol), F32),
        ],
        scratch_shapes=[
            pltpu.VMEM((nt, N_DEV * SUBLANES, ncol), F32),
            pltpu.SemaphoreType.DMA((3, N_DEV)),
            pltpu.SemaphoreType.DMA((3, N_DEV)),
        ],
        compiler_params=pltpu.CompilerParams(vmem_limit_bytes=VMEM_LIMIT_BYTES),
    )(c8, w4, b4, ln8)


def small_exchange(dmx, flat):
    def body(dmx_ref, flat_ref, dmrecv_ref, red_ref, land_ref, send_sems, recv_sems):
        x, y, c, me = _position()
        dmrecv_ref[me] = dmx_ref[me]
        land_ref[me] = flat_ref[me]
        first = []
        for k in range(1, N_DEV):
            dev, pid = _peer(k, x, y, c)
            first.append(_remote(dmx_ref.at[pid], dmrecv_ref.at[me], send_sems.at[0, k], recv_sems.at[0, k], dev))
            first.append(_remote(flat_ref.at[pid], land_ref.at[me], send_sems.at[1, k], recv_sems.at[1, k], dev))
        for cp in first:
            cp.start()
        for k in range(1, N_DEV):
            dev, pid = _peer(k, x, y, c)
            _remote(dmx_ref.at[pid], dmrecv_ref.at[pid], send_sems.at[0, k], recv_sems.at[0, k], dev).wait_recv()
            _remote(flat_ref.at[pid], land_ref.at[pid], send_sems.at[1, k], recv_sems.at[1, k], dev).wait_recv()
        total = land_ref[0]
        for s in range(1, N_DEV):
            total = total + land_ref[s]
        red_ref[me] = total
        second = []
        for k in range(1, N_DEV):
            dev, _ = _peer(k, x, y, c)
            second.append(_remote(red_ref.at[me], red_ref.at[me], send_sems.at[2, k], recv_sems.at[2, k], dev))
        for cp in second:
            cp.start()
        for k in range(1, N_DEV):
            dev, pid = _peer(k, x, y, c)
            _remote(red_ref.at[pid], red_ref.at[pid], send_sems.at[2, k], recv_sems.at[2, k], dev).wait_recv()
        for cp in first + second:
            cp.wait_send()

    return pl.pallas_call(
        body,
        name="small_exchange",
        in_specs=[VMEM, VMEM],
        out_specs=[VMEM, VMEM],
        out_shape=[jax.ShapeDtypeStruct(dmx.shape, F32), jax.ShapeDtypeStruct(flat.shape, F32)],
        scratch_shapes=[
            pltpu.VMEM(flat.shape, F32),
            pltpu.SemaphoreType.DMA((3, N_DEV)),
            pltpu.SemaphoreType.DMA((3, N_DEV)),
        ],
        compiler_params=pltpu.CompilerParams(vmem_limit_bytes=VMEM_LIMIT_BYTES),
    )(dmx, flat)


HBM = pl.BlockSpec(memory_space=pltpu.HBM)
SEM = pl.BlockSpec(memory_space=pltpu.SEMAPHORE)
EFFECT = pltpu.SideEffectType.DATAFLOW_SIDE_EFFECTING


def _own_slot(me, block):
    land = lax.empty((N_DEV, *block.shape), block.dtype)
    return lax.dynamic_update_slice_in_dim(land, block[None], me, axis=0)


N_CHIP_PEERS = 3


class Gather:
    def __init__(self, shards, lands, after, *, name):
        nt = len(shards)
        self.name = name

        def body(*refs):
            src_refs, land_refs = refs[:nt], refs[nt : 2 * nt]
            send_sems, recv_sems = refs[2 * nt + 1 : 3 * nt + 1], refs[3 * nt + 1 : 4 * nt + 1]
            token = refs[-1]
            x, y, c, me = _position()
            for t in range(nt):
                for k, dev in enumerate(self._targets(x, y, c)):
                    _remote(src_refs[t], land_refs[t].at[me], send_sems[t].at[k], recv_sems[t].at[k], dev).start()
            token[...] = jnp.zeros_like(token)

        outs = pl.pallas_call(
            body,
            name=name + "_start",
            in_specs=[HBM] * (2 * nt) + [ANY],
            out_specs=[SEM] * (2 * nt) + [HBM] * (2 * nt) + [VMEM],
            out_shape=[pltpu.SemaphoreType.DMA((1 + N_CHIP_PEERS,))] * (2 * nt)
            + [pltpu.HBM(a.shape, a.dtype) for a in (*shards, *lands)]
            + [jax.ShapeDtypeStruct((SUBLANES, LANES), F32)],
            input_output_aliases={i: 2 * nt + i for i in range(2 * nt)},
            compiler_params=pltpu.CompilerParams(has_side_effects=EFFECT),
        )(*[pltpu.with_memory_space_constraint(a, pltpu.HBM) for a in (*shards, *lands)], after)
        self.send_sems, self.recv_sems = list(outs[:nt]), list(outs[nt : 2 * nt])
        self.srcs, self.lands = list(outs[2 * nt : 3 * nt]), list(outs[3 * nt : 4 * nt])
        self.token = outs[-1]

    @staticmethod
    def _chips(x, y):
        return [(1 - x, y), (x, 1 - y), (1 - x, 1 - y)]

    @classmethod
    def _targets(cls, x, y, c):
        return [(x, y, 1 - c)] + [(*chip, c) for chip in cls._chips(x, y)]

    def zero(self):
        return self.token[0, 0]

    def wait(self, which, after, *, name):
        n = len(which)

        def slot(px, py, pc):
            return 4 * px + 2 * py + pc

        def pass_body(*refs):
            land_refs, recv_sems = refs[:n], refs[n : 2 * n]
            fwd_send, fwd_recv = refs[3 * n + 1 : 4 * n + 1], refs[4 * n + 1 : 5 * n + 1]
            x, y, c, _ = _position()
            for t in range(n):
                for j, chip in enumerate(self._chips(x, y)):
                    blk = land_refs[t].at[slot(*chip, c)]
                    _remote(blk, blk, fwd_send[t].at[j], recv_sems[t].at[1 + j], (*chip, c)).wait_recv()
                    _remote(blk, blk, fwd_send[t].at[j], fwd_recv[t].at[j], (x, y, 1 - c)).start()

        lands = [self.lands[t] for t in which]
        outs = pl.pallas_call(
            pass_body,
            name=name + "_pass",
            in_specs=[HBM] * n + [SEM] * n + [ANY],
            out_specs=[HBM] * n + [SEM] * (2 * n),
            out_shape=[pltpu.HBM(a.shape, a.dtype) for a in lands] + [pltpu.SemaphoreType.DMA((N_CHIP_PEERS,))] * (2 * n),
            input_output_aliases={i: i for i in range(n)},
            compiler_params=pltpu.CompilerParams(has_side_effects=EFFECT),
        )(*lands, *[self.recv_sems[t] for t in which], after)
        lands, fwd_send, fwd_recv = outs[:n], outs[n : 2 * n], outs[2 * n :]

        def wait_body(*refs):
            src_refs, land_refs = refs[:n], refs[n : 2 * n]
            send_sems, recv_sems = refs[2 * n : 3 * n], refs[3 * n : 4 * n]
            fwd_send, fwd_recv = refs[4 * n : 5 * n], refs[5 * n : 6 * n]
            x, y, c, me = _position()
            sibling = (x, y, 1 - c)
            for t in range(n):
                for k, dev in enumerate(self._targets(x, y, c)):
                    _remote(src_refs[t], land_refs[t].at[me], send_sems[t].at[k], recv_sems[t].at[k], dev).wait_send()
                blk = land_refs[t].at[slot(x, y, 1 - c)]
                _remote(blk, blk, send_sems[t].at[0], recv_sems[t].at[0], sibling).wait_recv()
                for j, chip in enumerate(self._chips(x, y)):
                    sent = land_refs[t].at[slot(*chip, c)]
                    _remote(sent, sent, fwd_send[t].at[j], fwd_recv[t].at[j], sibling).wait_send()
                    got = land_refs[t].at[slot(*chip, 1 - c)]
                    _remote(got, got, fwd_send[t].at[j], fwd_recv[t].at[j], sibling).wait_recv()

        srcs = [self.srcs[t] for t in which]
        outs = pl.pallas_call(
            wait_body,
            name=name,
            in_specs=[HBM] * (2 * n) + [SEM] * (4 * n),
            out_specs=[HBM] * (2 * n),
            out_shape=[pltpu.HBM(a.shape, a.dtype) for a in (*srcs, *lands)],
            input_output_aliases={i: i for i in range(2 * n)},
            compiler_params=pltpu.CompilerParams(has_side_effects=EFFECT),
        )(*srcs, *lands, *[self.send_sems[t] for t in which], *[self.recv_sems[t] for t in which], *fwd_send, *fwd_recv)
        return outs[n:]


class Scatter:
    def __init__(self, srcs, lands, after, *, name):
        self.name = name
        nt = self.nt = len(srcs)
        peers = N_DEV - 1

        def body(*refs):
            src_refs, land_refs = refs[:nt], refs[nt : 2 * nt]
            send_sems, recv_sems = refs[2 * nt + 1 : 3 * nt + 1], refs[3 * nt + 1 : 4 * nt + 1]
            token = refs[-1]
            x, y, c, me = _position()
            for t in range(nt):
                for k in range(1, N_DEV):
                    dev, pid = _peer(k, x, y, c)
                    src = src_refs[t].at[pid]
                    _remote(src, land_refs[t].at[me], send_sems[t].at[k - 1], recv_sems[t].at[k - 1], dev).start()
            token[...] = jnp.zeros_like(token)

        outs = pl.pallas_call(
            body,
            name=name + "_start",
            in_specs=[HBM] * (2 * nt) + [ANY],
            out_specs=[SEM] * (2 * nt) + [HBM] * (2 * nt) + [VMEM],
            out_shape=[pltpu.SemaphoreType.DMA((peers,))] * (2 * nt)
            + [pltpu.HBM(a.shape, a.dtype) for a in (*srcs, *lands)]
            + [jax.ShapeDtypeStruct((SUBLANES, LANES), F32)],
            input_output_aliases={i: 2 * nt + i for i in range(2 * nt)},
            compiler_params=pltpu.CompilerParams(has_side_effects=EFFECT),
        )(*[pltpu.with_memory_space_constraint(a, pltpu.HBM) for a in (*srcs, *lands)], after)
        self.send_sems, self.recv_sems = outs[:nt], outs[nt : 2 * nt]
        self.srcs, self.lands = outs[2 * nt : 3 * nt], outs[3 * nt : 4 * nt]
        self.token = outs[-1]

    def zero(self):
        return self.token[0, 0]

    def wait(self, which, after, *, name):
        n = len(which)

        def body(*refs):
            src_refs, land_refs = refs[:n], refs[n : 2 * n]
            send_sems, recv_sems = refs[2 * n : 3 * n], refs[3 * n : 4 * n]
            x, y, c, _ = _position()
            for t in range(n):
                for k in range(1, N_DEV):
                    dev, pid = _peer(k, x, y, c)
                    src = src_refs[t].at[pid]
                    cp = _remote(src, land_refs[t].at[pid], send_sems[t].at[k - 1], recv_sems[t].at[k - 1], dev)
                    cp.wait_send()
                    cp.wait_recv()

        srcs = [self.srcs[t] for t in which]
        lands = [self.lands[t] for t in which]
        outs = pl.pallas_call(
            body,
            name=name,
            in_specs=[HBM] * (2 * n) + [SEM] * (2 * n) + [ANY],
            out_specs=[HBM] * (2 * n),
            out_shape=[pltpu.HBM(a.shape, a.dtype) for a in (*srcs, *lands)],
            input_output_aliases={i: i for i in range(2 * n)},
            compiler_params=pltpu.CompilerParams(has_side_effects=EFFECT),
        )(*srcs, *lands, *[self.send_sems[t] for t in which], *[self.recv_sems[t] for t in which], after)
        return outs[n:]


def _adam_update(g, w, m, v):
    m2 = ADAM_B1 * m + (1.0 - ADAM_B1) * g
    v2 = ADAM_B2 * v + (1.0 - ADAM_B2) * jnp.square(g)
    m_hat = m2 / (1.0 - ADAM_B1**ADAM_STEP)
    v_hat = v2 / (1.0 - ADAM_B2**ADAM_STEP)
    delta = -ADAM_LR * (m_hat / (jnp.sqrt(v_hat) + ADAM_EPS) + ADAM_WD * w)
    return delta, m2, v2


def adamw(gparts, w, m, v, *, name):
    p, r, c = gparts.shape
    tr = r if r <= 256 else (256 if c <= D_MODEL else 128)

    def body(g_ref, w_ref, m_ref, v_ref, go_ref, d_ref, mo_ref, vo_ref):
        g = g_ref[0].astype(F32)
        for i in range(1, p):
            g = g + g_ref[i].astype(F32)
        delta, m2, v2 = _adam_update(g, w_ref[...], m_ref[...], v_ref[...])
        go_ref[...] = g
        d_ref[...] = delta
        mo_ref[...] = m2
        vo_ref[...] = v2

    blk = pl.BlockSpec((tr, c), lambda i: (i, 0))
    return pl.pallas_call(
        body,
        name=name,
        grid=(r // tr,),
        in_specs=[pl.BlockSpec((p, tr, c), lambda i: (0, i, 0)), blk, blk, blk],
        out_specs=[blk] * 4,
        out_shape=[jax.ShapeDtypeStruct((r, c), F32)] * 4,
        compiler_params=_params(("parallel",)),
    )(gparts, w, m, v)


def ada_grad_adamw(cg, dmrecv, w4, m4, v4, *, name):
    nt, k, ncol = w4.shape

    def body(cg_ref, dm_ref, w_ref, m_ref, v_ref, go_ref, d_ref, mo_ref, vo_ref, gb_ref):
        sc = _silu_bf16(cg_ref[...].reshape(N_DEV * SUBLANES, k))
        dm = dm_ref[...].reshape(N_DEV * SUBLANES, ncol)
        g = lax.dot_general(sc, dm.astype(BF16), (((0,), (0,)), ((), ())), preferred_element_type=F32)
        delta, m2, v2 = _adam_update(g, w_ref[...], m_ref[...], v_ref[...])
        go_ref[...] = g
        d_ref[...] = delta
        mo_ref[...] = m2
        vo_ref[...] = v2
        gb_ref[...] = jnp.broadcast_to(jnp.sum(dm, axis=0, keepdims=True), (SUBLANES, ncol))

    wblk = pl.BlockSpec((None, k, ncol), lambda t: (t, 0, 0))
    return pl.pallas_call(
        body,
        name=name,
        grid=(nt,),
        in_specs=[
            pl.BlockSpec((N_DEV, SUBLANES, k), lambda t: (0, 0, 0)),
            pl.BlockSpec((N_DEV, None, SUBLANES, ncol), lambda t: (0, t, 0, 0)),
            wblk,
            wblk,
            wblk,
        ],
        out_specs=[wblk] * 4 + [pl.BlockSpec((None, SUBLANES, ncol), lambda t: (t, 0, 0))],
        out_shape=[jax.ShapeDtypeStruct((nt, k, ncol), F32)] * 4 + [jax.ShapeDtypeStruct((nt, SUBLANES, ncol), F32)],
        compiler_params=_params(("parallel",)),
    )(cg, dmrecv, w4, m4, v4)


def kernel(x, c, ada_w, ada_b, ln_g, ln_b, a_w_in, a_b_in, a_vn_g, a_vn_b, a_w_s, a_b_s, a_w_out, b_w_qkv, b_w_out, mlp_w_up, mlp_w_down, loss_target, m_ada_w, m_ada_b, m_ln_g, m_ln_b, m_a_w_in, m_a_b_in, m_a_vn_g, m_a_vn_b, m_a_w_s, m_a_b_s, m_a_w_out, m_b_w_qkv, m_b_w_out, m_mlp_w_up, m_mlp_w_down, v_ada_w, v_ada_b, v_ln_g, v_ln_b, v_a_w_in, v_a_b_in, v_a_vn_g, v_a_vn_b, v_a_w_s, v_a_b_s, v_a_w_out, v_b_w_qkv, v_b_w_out, v_mlp_w_up, v_mlp_w_down):
    x0 = x[0]
    target = loss_target[0]
    me = 4 * lax.axis_index("x") + 2 * lax.axis_index("y") + lax.axis_index("c")

    ada_w4 = ada_w.reshape(N_SUB, D_MODEL, -1)
    ada_b4 = ada_b.reshape(N_SUB, -1)
    ln8 = jnp.concatenate([ln_g.reshape(N_SUB, -1), ln_b.reshape(N_SUB, -1)], axis=0)
    c8 = jnp.broadcast_to(c, (SUBLANES, D_MODEL))
    cg, lng, mrecv = ada_exchange(c8, ada_w4, ada_b4, ln8)

    W_IN, W_AOUT, W_UP0, W_DN0, W_QKV, W_BOUT, W_UP1, W_DN1 = range(8)
    shards = [
        a_w_in[0].astype(BF16),
        a_w_out[0].astype(BF16),
        mlp_w_up[0].astype(BF16),
        mlp_w_down[0].astype(BF16),
        b_w_qkv[0].astype(BF16),
        b_w_out[0].astype(BF16),
        mlp_w_up[1].astype(BF16),
        mlp_w_down[1].astype(BF16),
    ]
    gather = Gather(shards, [_own_slot(me, s) for s in shards], mrecv, name="gather")

    modv = mrecv[:, :, 0, :].transpose(1, 0, 2).reshape(N_SUB, 3 * D_MODEL) + gather.zero()
    shift = [modv[t : t + 1, :D_MODEL] for t in range(N_SUB)]
    scale = [modv[t : t + 1, D_MODEL : 2 * D_MODEL] for t in range(N_SUB)]
    gate1 = [1.0 + modv[t : t + 1, 2 * D_MODEL :] for t in range(N_SUB)]
    lng_full = [lng[:, t, :].reshape(1, D_MODEL) for t in range(N_SUB)]
    lnb_full = [lng[:, N_SUB + t, :].reshape(1, D_MODEL) for t in range(N_SUB)]

    ident = lambda acc: (acc,)
    relu2 = lambda a: jnp.square(jnp.maximum(a.astype(F32), 0.0)).astype(BF16)
    vn_g, vn_b, w_s = a_vn_g, a_vn_b, a_w_s[0]
    bias_full = jnp.repeat(a_b_s[0].T, A_GROUP_DIM, axis=1)
    w_up3, w_dn3 = [None, None], [None, None]

    def mlp_forward(i, h, after):
        up, dn = gather.wait([W_UP0, W_DN0] if i == 0 else [W_UP1, W_DN1], after, name=f"gather_wait_mlp{i}")
        w_up3[i], w_dn3[i] = up, dn.reshape(1, D_FF, D_MODEL)
        (a,) = mm_nn(h, w_up3[i], name=f"mlp{i}_up", tm=2048, ps=2, tn=512, tk=D_MODEL, epilogue=ident, outs=(BF16,))
        (y,) = mm_nn(
            a, w_dn3[i], name=f"mlp{i}_down", tm=2048, ps=1, tn=D_MODEL, tk=1024, prologue=relu2, epilogue=ident, outs=(BF16,)
        )
        return a, y

    h0 = modulate(x0, scale[0], shift[0], name="modulate0")
    w_in3, w_aout3 = gather.wait([W_IN, W_AOUT], h0, name="gather_wait_a")
    w_aout3 = w_aout3.reshape(1, D_MODEL, D_MODEL)
    (a_pre,) = mm_nn(
        h0, w_in3, name="a_in", tm=2048, ps=4, tn=256, tk=D_MODEL, epilogue=lambda acc, b: (acc + b,),
        extras=[(a_b_in, "row")], outs=(BF16,),
    )
    p_gate = gate_fwd(a_pre, vn_g, vn_b, w_s, bias_full, name="gate_fwd")
    (y0,) = mm_nn(p_gate, w_aout3, name="a_out", tm=2048, ps=1, tn=D_MODEL, tk=D_MODEL, epilogue=ident, outs=(BF16,))
    x1, h1 = residual_ln(x0, y0, gate1[0], lng_full[0], lnb_full[0], scale[1], shift[1], name="res_ln0")
    a1, y1 = mlp_forward(0, h1, y0)
    x2, h2 = residual_ln(x1, y1, gate1[1], lng_full[1], lnb_full[1], scale[2], shift[2], name="res_ln1")
    w_qkv_shards, w_bout3 = gather.wait([W_QKV, W_BOUT], y1, name="gather_wait_b")
    w_bout3 = w_bout3.reshape(1, D_MODEL, D_MODEL)
    w_qkv_cols = w_qkv_shards.transpose(1, 0, 2).reshape(D_MODEL, D_QKV)
    w_pat = [w_qkv_cols[None, :, 3 * D_MODEL * g : 3 * D_MODEL * (g + 1)] for g in range(N_PAT)]
    dil = [d for _, d in B_PATTERNS]
    h2_p = [h2] + [permute_rows(h2, dil[g], inverse=False, name=f"perm_h{g}") for g in range(1, N_PAT)]
    qkv_p, pat_o, pat_lse = [], [], []
    for g in range(N_PAT):
        (qkv_g,) = mm_nn(
            h2_p[g], w_pat[g], name=f"b_qkv{g}", tm=2048, ps=1, tn=D_MODEL, tk=D_MODEL, epilogue=ident, outs=(BF16,)
        )
        o_g, lse_g = attn_fwd(qkv_g, g, name=f"attn_fwd{g}")
        if g > 0:
            o_g = permute_rows(o_g, dil[g], inverse=True, name=f"unperm_o{g}")
            lse_g = permute_rows(lse_g, dil[g], inverse=True, name=f"unperm_lse{g}")
        qkv_p.append(qkv_g)
        pat_o.append(o_g)
        pat_lse.append(lse_g)
    o_b, o_f, lse = attn_combine(pat_o, pat_lse, name="attn_combine")
    (y2,) = mm_nn(o_b, w_bout3, name="b_out", tm=2048, ps=1, tn=D_MODEL, tk=D_MODEL, epilogue=ident, outs=(BF16,))
    x3, h3 = residual_ln(x2, y2, gate1[2], lng_full[2], lnb_full[2], scale[3], shift[3], name="res_ln2")
    a3, y3 = mlp_forward(1, h3, y2)
    dxo, loss_local = residual_ln_loss(x3, y3, gate1[3], lng_full[3], lnb_full[3], target, name="res_ln3_loss")
    loss = lax.psum(loss_local[0, 0], ("x", "y", "c"))

    def scatter(parts, after, name):
        parts = [p.reshape(N_DEV, -1, p.shape[-1]) for p in parts]
        lands = [_own_slot(me, lax.dynamic_index_in_dim(p, me, 0, keepdims=False)) for p in parts]
        return Scatter(parts, lands, after, name=name)

    def mlp_backward(i, h, a, dy):
        (da,) = mm_nt(
            dy,
            w_dn3[i],
            name=f"mlp{i}_da",
            tm=2048,
            tko=1024,
            ps=1,
            tc=D_MODEL,
            epilogue=lambda acc, act: (acc * (2.0 * jnp.maximum(act.astype(F32), 0.0)),),
            extras=[(a, "full")],
            outs=(BF16,),
        )
        (dh,) = mm_nt(da, w_up3[i], name=f"mlp{i}_dh", tm=2048, tko=1024, ps=2, tc=512, epilogue=ident, outs=(F32,))
        dw_dn = mm_tn(
            a, dy, name=f"mlp{i}_dw_down", p=1, tk=1024, ps=1, tn=D_MODEL, tmc=2048, prologue=relu2, out_dtype=BF16
        )
        dw_up = mm_tn(h, da, name=f"mlp{i}_dw_up", p=N_DEV, tk=1024, ps=2, tn=512, tmc=2048, out_dtype=BF16)
        return scatter([dw_up, dw_dn], dh, f"scatter_mlp{i}"), dh

    dz3, dy3, st3 = residual_ln_bwd(name="res_bwd3", dxo=dxo, this=(x3, y3, gate1[3], lng_full[3], lnb_full[3]))
    rs_mlp1, dh3 = mlp_backward(1, h3, a3, dy3)
    dz2, dy2, st2 = residual_ln_bwd(
        name="res_bwd2", later=(dh3, dz3, scale[3] + rs_mlp1.zero()), this=(x2, y2, gate1[2], lng_full[2], lnb_full[2])
    )
    (d_o,) = mm_nt(dy2, w_bout3, name="b_do", tm=2048, tko=1024, ps=1, tc=D_MODEL, epilogue=ident, outs=(F32,))
    do_b, delta = attn_delta(d_o, o_f, name="attn_delta")
    dh2, dw_pat = None, []
    for g in range(N_PAT):
        do_g, lse_g, delta_g = do_b, lse, delta
        if g > 0:
            do_g = permute_rows(do_b, dil[g], inverse=False, name=f"perm_do{g}")
            lse_g = permute_rows(lse, dil[g], inverse=False, name=f"perm_lse{g}")
            delta_g = permute_rows(delta, dil[g], inverse=False, name=f"perm_delta{g}")
        dqkv_g = attn_bwd(qkv_p[g], do_g, lse_g, delta_g, g, name=f"attn_bwd{g}")
        (dh_g,) = mm_nt(
            dqkv_g, w_pat[g], name=f"b_dh{g}", tm=2048, tko=1024, ps=1, tc=D_MODEL, epilogue=ident, outs=(F32,)
        )
        dh2 = dh_g if g == 0 else permute_rows(dh_g, dil[g], inverse=True, add=dh2, name=f"unperm_dh{g}")
        dw_pat.append(
            mm_tn(h2_p[g], dqkv_g, name=f"b_dw_qkv{g}", p=1, tk=1024, ps=1, tn=D_MODEL, tmc=2048, out_dtype=BF16)[0]
        )
    dw_bout = mm_tn(o_b, dy2, name="b_dw_out", p=1, tk=1024, ps=1, tn=D_MODEL, tmc=2048, out_dtype=BF16)
    dw_qkv = jnp.concatenate(dw_pat, axis=1).reshape(D_MODEL, N_DEV, -1).transpose(1, 0, 2)
    rs_b = scatter([dw_qkv, dw_bout], dh2, "scatter_b")
    dz1, dy1, st1 = residual_ln_bwd(
        name="res_bwd1", later=(dh2, dz2, scale[2] + rs_b.zero()), this=(x1, y1, gate1[1], lng_full[1], lnb_full[1])
    )
    rs_mlp0, dh1 = mlp_backward(0, h1, a1, dy1)
    dz0, dy0, st0 = residual_ln_bwd(
        name="res_bwd0", later=(dh1, dz1, scale[1] + rs_mlp0.zero()), this=(x0, y0, gate1[0], lng_full[0], lnb_full[0])
    )
    (dp_gate,) = mm_nt(dy0, w_aout3, name="a_dp", tm=2048, tko=1024, ps=1, tc=D_MODEL, epilogue=ident, outs=(F32,))
    da0, d_ws, d_bs, gate_rows = gate_bwd(a_pre, dp_gate, vn_g, vn_b, w_s, bias_full, name="gate_bwd")
    (dh0,) = mm_nt(da0, w_in3, name="a_dh", tm=2048, tko=1024, ps=4, tc=256, epilogue=ident, outs=(F32,))
    dw_aout = mm_tn(p_gate, dy0, name="a_dw_out", p=1, tk=1024, ps=1, tn=D_MODEL, tmc=2048, out_dtype=BF16)
    dw_in = mm_tn(h0, da0, name="a_dw_in", p=N_DEV, tk=1024, ps=4, tn=256, tmc=2048, out_dtype=BF16)
    grad_x, stf = residual_ln_bwd(name="res_bwd_in", later=(dh0, dz0, scale[0]), x_out=x0)

    stats_after = [stf, st0, st1, st2]
    stats_own = [st0, st1, st2, st3]
    dm = jnp.stack(
        [
            jnp.concatenate(
                [stats_after[t][ST_DSHIFT], stats_after[t][ST_DSCALE], stats_own[t][ST_DGATE]], axis=0
            )
            for t in range(N_SUB)
        ]
    )
    ncol = 3 * D_MODEL // N_DEV
    dmx = jnp.pad(
        dm.reshape(N_SUB, N_DEV, ncol).transpose(1, 0, 2)[:, :, None, :], ((0, 0), (0, 0), (0, SUBLANES - 1), (0, 0))
    )
    small = [
        gate_rows[0],
        gate_rows[1],
        d_ws.reshape(-1),
        d_bs[:, :A_GROUPS].T.reshape(-1),
        *[stats_own[t][ST_DG] for t in range(N_SUB)],
        *[stats_own[t][ST_DB] for t in range(N_SUB)],
    ]
    n_small = sum(s.size for s in small)
    part_rows = -(-n_small // (N_DEV * LANES * SUBLANES)) * SUBLANES
    flat = jnp.concatenate(small + [jnp.zeros((N_DEV * part_rows * LANES - n_small,), F32)])
    dmrecv, reduced = small_exchange(dmx, flat.reshape(N_DEV, part_rows, LANES))
    reduced = reduced.reshape(-1)
    sizes = [2 * D_MODEL, D_MODEL, D_MODEL, A_GROUPS * CHUNK * CHUNK, A_GROUPS * CHUNK, N_SUB * D_MODEL, N_SUB * D_MODEL]
    offs = [sum(sizes[:i]) for i in range(len(sizes) + 1)]
    g_b_in, g_vn_g, g_vn_b, g_ws, g_bs, g_lng, g_lnb = [reduced[offs[i] : offs[i + 1]] for i in range(len(sizes))]

    results = {}

    def update(wname, gparts, w, m, v):
        shape = w.shape
        w2 = w.reshape(-1, shape[-1])
        outs = adamw(gparts.reshape(gparts.shape[0], *w2.shape), w2, m.reshape(w2.shape), v.reshape(w2.shape), name=f"adamw_{wname}")
        results[wname] = [o.reshape(shape) for o in outs]

    rs_a = scatter([dw_in, dw_aout], reduced, "scatter_a")
    ada_outs = ada_grad_adamw(cg + rs_a.zero(), dmrecv, ada_w4, m_ada_w.reshape(ada_w4.shape), v_ada_w.reshape(ada_w4.shape), name="ada_grad_adamw")
    results["ada_w"] = [o.reshape(ada_w.shape) for o in ada_outs[:4]]
    update("ada_b", ada_outs[4][:, 0, :][None], ada_b, m_ada_b, v_ada_b)
    ln_cols = D_MODEL // N_DEV
    my_ln = lambda gfull: lax.dynamic_slice_in_dim(gfull.reshape(N_SUB, N_DEV, ln_cols), me, 1, axis=1)
    update("ln_g", my_ln(g_lng).reshape(1, N_SUB, ln_cols), ln_g, m_ln_g, v_ln_g)
    update("ln_b", my_ln(g_lnb).reshape(1, N_SUB, ln_cols), ln_b, m_ln_b, v_ln_b)
    update("a_b_in", g_b_in[None], a_b_in, m_a_b_in, v_a_b_in)
    update("a_vn_g", g_vn_g[None], a_vn_g, m_a_vn_g, v_a_vn_g)
    update("a_vn_b", g_vn_b[None], a_vn_b, m_a_vn_b, v_a_vn_b)
    update("a_w_s", g_ws[None], a_w_s, m_a_w_s, v_a_w_s)
    update("a_b_s", g_bs[None], a_b_s, m_a_b_s, v_a_b_s)
    g_up1, g_dn1 = rs_mlp1.wait([0, 1], grad_x, name="scatter_wait_mlp1")
    g_qkv, g_bout = rs_b.wait([0, 1], grad_x, name="scatter_wait_b")
    update("b_w_qkv", g_qkv, b_w_qkv, m_b_w_qkv, v_b_w_qkv)
    update("b_w_out", g_bout, b_w_out, m_b_w_out, v_b_w_out)
    g_up0, g_dn0 = rs_mlp0.wait([0, 1], grad_x, name="scatter_wait_mlp0")
    update("mlp_w_up", jnp.concatenate([g_up0, g_up1], axis=1), mlp_w_up, m_mlp_w_up, v_mlp_w_up)
    update("mlp_w_down", jnp.concatenate([g_dn0, g_dn1], axis=1), mlp_w_down, m_mlp_w_down, v_mlp_w_down)
    g_in, g_aout = rs_a.wait([0, 1], grad_x, name="scatter_wait_a")
    update("a_w_in", g_in, a_w_in, m_a_w_in, v_a_w_in)
    update("a_w_out", g_aout, a_w_out, m_a_w_out, v_a_w_out)

    order = ["ada_w", "ada_b", "ln_g", "ln_b", "a_w_in", "a_b_in", "a_vn_g", "a_vn_b", "a_w_s", "a_b_s", "a_w_out", "b_w_qkv", "b_w_out", "mlp_w_up", "mlp_w_down"]
    return (loss, grad_x[None], *[results[n][0] for n in order], *[results[n][1] for n in order],
            *[results[n][2] for n in order], *[results[n][3] for n in order])
```

```python
import math

import jax
import jax.numpy as jnp
from jax import lax
from jax.experimental import pallas as pl
from jax.experimental.pallas import tpu as pltpu

F32 = jnp.float32
BF16 = jnp.bfloat16
MESH = pl.DeviceIdType.MESH
ANY = pl.BlockSpec(memory_space=pl.ANY)
VMEM = pl.BlockSpec(memory_space=pltpu.VMEM)

N_DEV = 8
D_MODEL = 1024
SEQ = 4096
DEPTH = 2
CHUNK = 128
A_GROUPS = 16
A_GROUP_DIM = D_MODEL // A_GROUPS
B_HEADS = 16
B_HEAD_DIM = 64
B_PATTERNS = ((128, 1), (512, 4), (2048, 16))
N_PAT = len(B_PATTERNS)
SPAN = 128
D_FF = 4 * D_MODEL
D_QKV = N_PAT * 3 * D_MODEL
ALPHA = (2 * DEPTH) ** 0.25
LN_EPS = 1e-5
NEG = -1e30
ADAM_LR = 0.001
ADAM_B1 = 0.9
ADAM_B2 = 0.999
ADAM_EPS = 1e-08
ADAM_WD = 0.01
ADAM_STEP = 10
GELU_C = math.sqrt(2.0 / math.pi)
GELU_A = 0.044715

VMEM_LIMIT_BYTES = 56 * 1024 * 1024
LANES = 128
SUBLANES = 8
ROW_TILE = 512
N_SUB = 2 * DEPTH


def _params(sem):
    return pltpu.CompilerParams(dimension_semantics=sem, vmem_limit_bytes=VMEM_LIMIT_BYTES)


def _lane(shape):
    return lax.broadcasted_iota(jnp.int32, shape, len(shape) - 1)


def _split_bf16(x):
    hi = x.astype(BF16)
    lo = (x - hi.astype(F32)).astype(BF16)
    return hi, lo


def _group_expand_matrix(groups_padded, width):
    per = width // A_GROUPS
    r = lax.broadcasted_iota(jnp.int32, (groups_padded, width), 0)
    c = lax.broadcasted_iota(jnp.int32, (groups_padded, width), 1)
    return (c // per == r).astype(BF16)


def _group_reduce_matrix(width, groups_padded):
    per = width // A_GROUPS
    r = lax.broadcasted_iota(jnp.int32, (width, groups_padded), 0)
    c = lax.broadcasted_iota(jnp.int32, (width, groups_padded), 1)
    return (r // per == c).astype(BF16)


def _expand_groups(w):
    e = _group_expand_matrix(LANES, D_MODEL)
    hi, lo = _split_bf16(w)
    return jnp.dot(hi, e, preferred_element_type=F32) + jnp.dot(lo, e, preferred_element_type=F32)


def _reduce_groups(x):
    e = _group_reduce_matrix(D_MODEL, LANES)
    hi, lo = _split_bf16(x)
    return jnp.dot(hi, e, preferred_element_type=F32) + jnp.dot(lo, e, preferred_element_type=F32)


def _column_tiles(p, n, ps, tn):
    assert (ps == 1 or tn == n) and p % ps == 0 and n % tn == 0
    q = n // tn
    return (p // ps) * q, q


def _extra_specs(extras, tm, width):
    specs = []
    for _, kind in extras:
        if kind == "row":
            specs.append(pl.BlockSpec((1, width), lambda i, j, c: (0, j)))
        else:
            specs.append(pl.BlockSpec((tm, width), lambda i, j, c: (i, j)))
    return specs


def mm_nn(a, b3, *, name, tm, ps, tn, tk, epilogue, extras=(), outs, prologue=None, b_tile0=0, b_tiles=None):
    m, k = a.shape
    p, _, n = b3.shape
    nj, q = _column_tiles(p, n, ps, tn)
    nj = nj if b_tiles is None else b_tiles
    nk = k // tk
    width = ps * tn

    def body(a_ref, b_ref, *rest):
        ex = rest[: len(extras)]
        out_refs = rest[len(extras) : len(extras) + len(outs)]
        kk = pl.program_id(2)
        av = a_ref[...] if prologue is None else prologue(a_ref[...])

        def finish(cs, acc):
            res = epilogue(acc, *[e[:, cs] for e in ex])
            for o_ref, r in zip(out_refs, res, strict=True):
                o_ref[:, cs] = r.astype(o_ref.dtype)

        for s in range(ps):
            cs = slice(s * tn, (s + 1) * tn)
            part = jnp.dot(av, b_ref[s], preferred_element_type=F32)
            if nk == 1:
                finish(cs, part)
                continue
            acc_ref = rest[-1]

            @pl.when(kk == 0)
            def _(part=part, cs=cs):
                acc_ref[:, cs] = part

            @pl.when(kk > 0)
            def _(part=part, cs=cs):
                acc_ref[:, cs] += part

        if nk > 1:

            @pl.when(kk == nk - 1)
            def _():
                for s in range(ps):
                    cs = slice(s * tn, (s + 1) * tn)
                    finish(cs, rest[-1][:, cs])

    return pl.pallas_call(
        body,
        name=name,
        grid=(m // tm, nj, nk),
        in_specs=[
            pl.BlockSpec((tm, tk), lambda i, j, kk: (i, kk)),
            pl.BlockSpec((ps, tk, tn), lambda i, j, kk: ((j + b_tile0) // q, kk, (j + b_tile0) % q)),
            *_extra_specs(extras, tm, width),
        ],
        out_specs=[pl.BlockSpec((tm, width), lambda i, j, kk: (i, j)) for _ in outs],
        out_shape=[jax.ShapeDtypeStruct((m, nj * width), dt) for dt in outs],
        scratch_shapes=[pltpu.VMEM((tm, width), F32)] if nk > 1 else [],
        compiler_params=_params(("parallel", "parallel", "arbitrary")),
    )(a, b3, *[arr for arr, _ in extras])


def mm_nt(g, b3, *, name, tm, tko, ps, tc, epilogue, extras=(), outs, b_tile0=0):
    m = g.shape[0]
    p, k, n = b3.shape
    _, q = _column_tiles(p, n, ps, tc)
    nc = g.shape[1] // (ps * tc)

    def body(g_ref, b_ref, *rest):
        ex = rest[: len(extras)]
        out_refs = rest[len(extras) : len(extras) + len(outs)]
        c = pl.program_id(2)
        part = None
        for s in range(ps):
            d = lax.dot_general(
                g_ref[:, s * tc : (s + 1) * tc], b_ref[s], (((1,), (1,)), ((), ())), preferred_element_type=F32
            )
            part = d if part is None else part + d

        def finish(acc):
            res = epilogue(acc, *[e[...] for e in ex])
            for o_ref, r in zip(out_refs, res, strict=True):
                o_ref[...] = r.astype(o_ref.dtype)

        if nc == 1:
            finish(part)
            return
        acc_ref = rest[-1]

        @pl.when(c == 0)
        def _():
            acc_ref[...] = part

        @pl.when(c > 0)
        def _():
            acc_ref[...] += part

        @pl.when(c == nc - 1)
        def _():
            finish(acc_ref[...])

    return pl.pallas_call(
        body,
        name=name,
        grid=(m // tm, k // tko, nc),
        in_specs=[
            pl.BlockSpec((tm, ps * tc), lambda i, j, c: (i, c)),
            pl.BlockSpec((ps, tko, tc), lambda i, j, c: ((c + b_tile0) // q, j, (c + b_tile0) % q)),
            *_extra_specs(extras, tm, tko),
        ],
        out_specs=[pl.BlockSpec((tm, tko), lambda i, j, c: (i, j)) for _ in outs],
        out_shape=[jax.ShapeDtypeStruct((m, k), dt) for dt in outs],
        scratch_shapes=[pltpu.VMEM((tm, tko), F32)] if nc > 1 else [],
        compiler_params=_params(("parallel", "parallel", "arbitrary")),
    )(g, b3, *[arr for arr, _ in extras])


def mm_tn(a, g, *, name, p, tk, ps, tn, tmc, out_dtype, prologue=None):
    m, k = a.shape
    n = g.shape[1] // p
    nj, q = _column_tiles(p, n, ps, tn)
    nc = m // tmc

    def body(a_ref, g_ref, o_ref, acc_ref):
        c = pl.program_id(2)
        av = a_ref[...] if prologue is None else prologue(a_ref[...])
        part = lax.dot_general(av, g_ref[...], (((0,), (0,)), ((), ())), preferred_element_type=F32)

        @pl.when(c == 0)
        def _():
            acc_ref[...] = part

        @pl.when(c > 0)
        def _():
            acc_ref[...] += part

        @pl.when(c == nc - 1)
        def _():
            for s in range(ps):
                o_ref[s] = acc_ref[:, s * tn : (s + 1) * tn].astype(o_ref.dtype)

    return pl.pallas_call(
        body,
        name=name,
        grid=(k // tk, nj, nc),
        in_specs=[
            pl.BlockSpec((tmc, tk), lambda i, j, c: (c, i)),
            pl.BlockSpec((tmc, ps * tn), lambda i, j, c: (c, j)),
        ],
        out_specs=pl.BlockSpec((ps, tk, tn), lambda i, j, c: (j // q, i, j % q)),
        out_shape=jax.ShapeDtypeStruct((p, k, n), out_dtype),
        scratch_shapes=[pltpu.VMEM((tk, ps * tn), F32)],
        compiler_params=_params(("parallel", "parallel", "arbitrary")),
    )(a, g)


def _rows(cols):
    return pl.BlockSpec((ROW_TILE, cols), lambda i: (i, 0))


def _vec(cols, rows=1):
    return pl.BlockSpec((rows, cols), lambda i: (0, 0))


def _layer_norm_hat(z):
    mu = jnp.mean(z, axis=-1, keepdims=True)
    zc = z - mu
    var = jnp.mean(zc * zc, axis=-1, keepdims=True)
    rstd = lax.rsqrt(var + LN_EPS)
    return zc * rstd, rstd


def modulate(x, scale, shift, *, name):
    s, d = x.shape

    def body(x_ref, sc_ref, sh_ref, h_ref):
        h_ref[...] = (x_ref[...] * (1.0 + sc_ref[...]) + sh_ref[...]).astype(BF16)

    return pl.pallas_call(
        body,
        name=name,
        grid=(s // ROW_TILE,),
        in_specs=[_rows(d), _vec(d), _vec(d)],
        out_specs=_rows(d),
        out_shape=jax.ShapeDtypeStruct((s, d), BF16),
        compiler_params=_params(("parallel",)),
    )(x, scale, shift)


def residual_ln(x, y, gate1, g, b, nscale, nshift, *, name):
    s, d = x.shape

    def body(x_ref, y_ref, gt_ref, g_ref, b_ref, sc_ref, sh_ref, xn_ref, hn_ref):
        z = ALPHA * x_ref[...] + gt_ref[...] * y_ref[...].astype(F32)
        xhat, _ = _layer_norm_hat(z)
        xn = xhat * g_ref[...] + b_ref[...]
        xn_ref[...] = xn
        hn_ref[...] = (xn * (1.0 + sc_ref[...]) + sh_ref[...]).astype(BF16)

    return pl.pallas_call(
        body,
        name=name,
        grid=(s // ROW_TILE,),
        in_specs=[_rows(d), _rows(d), _vec(d), _vec(d), _vec(d), _vec(d), _vec(d)],
        out_specs=[_rows(d), _rows(d)],
        out_shape=[jax.ShapeDtypeStruct((s, d), F32), jax.ShapeDtypeStruct((s, d), BF16)],
        compiler_params=_params(("parallel",)),
    )(x, y, gate1, g, b, nscale, nshift)


def residual_ln_loss(x, y, gate1, g, b, target, *, name):
    s, d = x.shape

    def body(x_ref, y_ref, gt_ref, g_ref, b_ref, t_ref, dx_ref, loss_ref):
        z = ALPHA * x_ref[...] + gt_ref[...] * y_ref[...].astype(F32)
        xhat, _ = _layer_norm_hat(z)
        err = xhat * g_ref[...] + b_ref[...] - t_ref[...]
        dx_ref[...] = err * (1.0 / d)
        part = jnp.sum(jnp.sum(err * err, axis=1, keepdims=True), axis=0, keepdims=True) * (0.5 / d)

        @pl.when(pl.program_id(0) == 0)
        def _():
            loss_ref[...] = part

        @pl.when(pl.program_id(0) > 0)
        def _():
            loss_ref[...] += part

    return pl.pallas_call(
        body,
        name=name,
        grid=(s // ROW_TILE,),
        in_specs=[_rows(d), _rows(d), _vec(d), _vec(d), _vec(d), _rows(d)],
        out_specs=[_rows(d), pl.BlockSpec((1, 1), lambda i: (0, 0))],
        out_shape=[jax.ShapeDtypeStruct((s, d), F32), jax.ShapeDtypeStruct((1, 1), F32)],
        compiler_params=_params(("arbitrary",)),
    )(x, y, gate1, g, b, target)


ST_DSCALE, ST_DSHIFT, ST_DG, ST_DB, ST_DGATE = 0, 1, 2, 3, 4


def residual_ln_bwd(*, name, later=None, dxo=None, x_out=None, this=None):
    lead = later[0] if later is not None else dxo
    s, d = lead.shape
    has_later, has_ln = later is not None, this is not None

    def body(*refs):
        refs = list(refs)
        if has_later:
            dh_ref, dzl_ref, scl_ref = refs[:3]
            refs = refs[3:]
        else:
            dxo_ref = refs.pop(0)
        if has_ln:
            x_ref, y_ref, gt_ref, g_ref, b_ref = refs[:5]
            refs = refs[5:]
            dz_ref, dy_ref, st_ref = refs
        else:
            xo_ref, dx_ref, st_ref = refs

        @pl.when(pl.program_id(0) == 0)
        def _():
            st_ref[...] = jnp.zeros_like(st_ref)

        def acc(row, val):
            st_ref[row : row + 1, :] += jnp.sum(val, axis=0, keepdims=True)

        if has_ln:
            y = y_ref[...].astype(F32)
            gate1 = gt_ref[...]
            xhat, rstd = _layer_norm_hat(ALPHA * x_ref[...] + gate1 * y)
            x_out_v = xhat * g_ref[...] + b_ref[...]
        else:
            x_out_v = xo_ref[...]
        if has_later:
            dh = dh_ref[...]
            g_out = ALPHA * dzl_ref[...] + dh * (1.0 + scl_ref[...])
            acc(ST_DSCALE, dh * x_out_v)
            acc(ST_DSHIFT, dh)
        else:
            g_out = dxo_ref[...]
        if not has_ln:
            dx_ref[...] = g_out
            return
        acc(ST_DG, g_out * xhat)
        acc(ST_DB, g_out)
        dxh = g_out * g_ref[...]
        m1 = jnp.mean(dxh, axis=-1, keepdims=True)
        m2 = jnp.mean(dxh * xhat, axis=-1, keepdims=True)
        dz = rstd * (dxh - m1 - xhat * m2)
        acc(ST_DGATE, dz * y)
        dz_ref[...] = dz
        dy_ref[...] = (dz * gate1).astype(BF16)

    ins, specs = [], []
    if has_later:
        ins += list(later)
        specs += [_rows(d), _rows(d), _vec(d)]
    else:
        ins += [dxo]
        specs += [_rows(d)]
    if not has_ln:
        ins += [x_out]
        specs += [_rows(d)]
    if has_ln:
        ins += list(this)
        specs += [_rows(d), _rows(d), _vec(d), _vec(d), _vec(d)]
        out_specs = [_rows(d), _rows(d), _vec(d, SUBLANES)]
        out_shape = [
            jax.ShapeDtypeStruct((s, d), F32),
            jax.ShapeDtypeStruct((s, d), BF16),
            jax.ShapeDtypeStruct((SUBLANES, d), F32),
        ]
    else:
        out_specs = [_rows(d), _vec(d, SUBLANES)]
        out_shape = [jax.ShapeDtypeStruct((s, d), F32), jax.ShapeDtypeStruct((SUBLANES, d), F32)]
    return pl.pallas_call(
        body,
        name=name,
        grid=(s // ROW_TILE,),
        in_specs=specs,
        out_specs=out_specs,
        out_shape=out_shape,
        compiler_params=_params(("arbitrary",)),
    )(*ins)


GATE_CHUNKS = 4


def _gelu(x):
    return 0.5 * x * (1.0 + jnp.tanh(GELU_C * (x + GELU_A * x * x * x)))


def _gelu_grad(x):
    t = jnp.tanh(GELU_C * (x + GELU_A * x * x * x))
    return 0.5 * (1.0 + t) + 0.5 * x * (1.0 - t * t) * (GELU_C * (1.0 + 3.0 * GELU_A * x * x))


def _causal_weights(w_ref, transpose):
    t = lax.broadcasted_iota(jnp.int32, (CHUNK, CHUNK), 0)
    s = lax.broadcasted_iota(jnp.int32, (CHUNK, CHUNK), 1)
    out = []
    for g in range(A_GROUPS):
        w = jnp.where(t >= s, w_ref[g], 0.0)
        out.append((w.T if transpose else w).astype(BF16))
    return out


def _spatial(ws, vn, lo_mask):
    rows = vn.shape[0]
    out_rows = []
    for r in range(rows // CHUNK):
        cols = []
        for j in range(A_GROUPS // 2):
            blk = vn[r * CHUNK : (r + 1) * CHUNK, j * LANES : (j + 1) * LANES]
            za = jnp.dot(ws[2 * j], blk, preferred_element_type=F32)
            zb = jnp.dot(ws[2 * j + 1], blk, preferred_element_type=F32)
            cols.append(jnp.where(lo_mask, za, zb))
        out_rows.append(jnp.concatenate(cols, axis=1))
    return jnp.concatenate(out_rows, axis=0)


def _gate_forward(a, vg, vb, ws, bias, lo_mask):
    u = _gelu(a[:, :D_MODEL])
    v = _gelu(a[:, D_MODEL:])
    vhat, rstd = _layer_norm_hat(v)
    vn = (vhat * vg + vb).astype(BF16)
    z = _spatial(ws, vn, lo_mask) + jnp.concatenate([bias] * (a.shape[0] // CHUNK), axis=0)
    return u, vhat, rstd, vn, z


def gate_fwd(a_pre, vn_g, vn_b, w_s, bias_full, *, name):
    s = a_pre.shape[0]
    tr = GATE_CHUNKS * CHUNK

    def body(a_ref, vg_ref, vb_ref, w_ref, bias_ref, p_ref):
        lo_mask = _lane((CHUNK, LANES)) < A_GROUP_DIM
        ws = _causal_weights(w_ref, transpose=False)
        u, _, _, _, z = _gate_forward(a_ref[...].astype(F32), vg_ref[...], vb_ref[...], ws, bias_ref[...], lo_mask)
        p_ref[...] = (u * z).astype(BF16)

    return pl.pallas_call(
        body,
        name=name,
        grid=(s // tr,),
        in_specs=[
            pl.BlockSpec((tr, 2 * D_MODEL), lambda i: (i, 0)),
            _vec(D_MODEL),
            _vec(D_MODEL),
            pl.BlockSpec((A_GROUPS, CHUNK, CHUNK), lambda i: (0, 0, 0)),
            _vec(D_MODEL, CHUNK),
        ],
        out_specs=pl.BlockSpec((tr, D_MODEL), lambda i: (i, 0)),
        out_shape=jax.ShapeDtypeStruct((s, D_MODEL), BF16),
        compiler_params=_params(("parallel",)),
    )(a_pre, vn_g, vn_b, w_s, bias_full)


def gate_bwd(a_pre, dp, vn_g, vn_b, w_s, bias_full, *, name):
    s = a_pre.shape[0]
    tr = GATE_CHUNKS * CHUNK
    nsteps = s // tr

    def body(a_ref, dp_ref, vg_ref, vb_ref, w_ref, bias_ref, da_ref, dw_ref, dbs_ref, rows_ref, dbias_acc):
        step = pl.program_id(0)
        lo_mask = _lane((CHUNK, LANES)) < A_GROUP_DIM

        @pl.when(step == 0)
        def _():
            dw_ref[...] = jnp.zeros_like(dw_ref)
            rows_ref[...] = jnp.zeros_like(rows_ref)
            dbias_acc[...] = jnp.zeros_like(dbias_acc)

        a = a_ref[...].astype(F32)
        vg = vg_ref[...]
        ws = _causal_weights(w_ref, transpose=False)
        wts = _causal_weights(w_ref, transpose=True)
        u, vhat, rstd, vn, z = _gate_forward(a, vg, vb_ref[...], ws, bias_ref[...], lo_mask)
        dp = dp_ref[...]
        du = dp * z
        dzz = dp * u
        dzz_b = dzz.astype(BF16)
        dvn = _spatial(wts, dzz_b, lo_mask)
        dbias = None
        for r in range(GATE_CHUNKS):
            rs = slice(r * CHUNK, (r + 1) * CHUNK)
            dbias = dzz[rs] if dbias is None else dbias + dzz[rs]
            for j in range(A_GROUPS // 2):
                cs = slice(j * LANES, (j + 1) * LANES)
                dblk = dzz[rs, cs]
                vblk = vn[rs, cs]
                for half in range(2):
                    keep = lo_mask if half == 0 else jnp.logical_not(lo_mask)
                    dm = jnp.where(keep, dblk, 0.0).astype(BF16)
                    dw_ref[2 * j + half] += lax.dot_general(
                        dm, vblk, (((1,), (1,)), ((), ())), preferred_element_type=F32
                    )
        dbias_acc[...] += dbias
        rows_ref[1:2, :D_MODEL] += jnp.sum(dvn * vhat, axis=0, keepdims=True)
        rows_ref[1:2, D_MODEL:] += jnp.sum(dvn, axis=0, keepdims=True)
        dvh = dvn * vg
        m1 = jnp.mean(dvh, axis=-1, keepdims=True)
        m2 = jnp.mean(dvh * vhat, axis=-1, keepdims=True)
        dv = rstd * (dvh - m1 - vhat * m2)
        da_u = du * _gelu_grad(a[:, :D_MODEL])
        da_v = dv * _gelu_grad(a[:, D_MODEL:])
        da_ref[:, :D_MODEL] = da_u.astype(BF16)
        da_ref[:, D_MODEL:] = da_v.astype(BF16)
        rows_ref[0:1, :D_MODEL] += jnp.sum(da_u, axis=0, keepdims=True)
        rows_ref[0:1, D_MODEL:] += jnp.sum(da_v, axis=0, keepdims=True)

        @pl.when(step == nsteps - 1)
        def _():
            t = lax.broadcasted_iota(jnp.int32, (CHUNK, CHUNK), 0)
            sx = lax.broadcasted_iota(jnp.int32, (CHUNK, CHUNK), 1)
            for g in range(A_GROUPS):
                dw_ref[g] = jnp.where(t >= sx, dw_ref[g], 0.0)
            dbs_ref[...] = _reduce_groups(dbias_acc[...])

    return pl.pallas_call(
        body,
        name=name,
        grid=(nsteps,),
        in_specs=[
            pl.BlockSpec((tr, 2 * D_MODEL), lambda i: (i, 0)),
            pl.BlockSpec((tr, D_MODEL), lambda i: (i, 0)),
            _vec(D_MODEL),
            _vec(D_MODEL),
            pl.BlockSpec((A_GROUPS, CHUNK, CHUNK), lambda i: (0, 0, 0)),
            _vec(D_MODEL, CHUNK),
        ],
        out_specs=[
            pl.BlockSpec((tr, 2 * D_MODEL), lambda i: (i, 0)),
            pl.BlockSpec((A_GROUPS, CHUNK, CHUNK), lambda i: (0, 0, 0)),
            _vec(LANES, CHUNK),
            _vec(2 * D_MODEL, SUBLANES),
        ],
        out_shape=[
            jax.ShapeDtypeStruct((s, 2 * D_MODEL), BF16),
            jax.ShapeDtypeStruct((A_GROUPS, CHUNK, CHUNK), F32),
            jax.ShapeDtypeStruct((CHUNK, LANES), F32),
            jax.ShapeDtypeStruct((SUBLANES, 2 * D_MODEL), F32),
        ],
        scratch_shapes=[pltpu.VMEM((CHUNK, D_MODEL), F32)],
        compiler_params=_params(("arbitrary",)),
    )(a_pre, dp, vn_g, vn_b, w_s, bias_full)


def alibi_tables(dilation):
    qi = jnp.arange(SPAN)[:, None]
    ki = jnp.arange(2 * SPAN)[None, :]
    diff = SPAN + qi - ki
    valid = (diff >= 0) & (diff <= SPAN)
    heads = jnp.arange(1, B_HEADS + 1, dtype=F32)
    slopes = jnp.exp2(-8.0 * heads / B_HEADS)
    bias = -slopes[:, None, None] * (dilation * diff).astype(F32)
    bias = jnp.where(valid[None], bias, NEG).reshape(B_HEADS // 2, 2 * SPAN, 2 * SPAN)
    return bias, bias.transpose(0, 2, 1)


def _pair_rows(x, halves):
    return jnp.concatenate([x * halves[0], x * halves[1]], axis=0)


def _pair_column(v, lane, j):
    pick = lambda h: jnp.sum(jnp.where(lane == h, v, 0.0), axis=1, keepdims=True)
    return jnp.concatenate([pick(2 * j), pick(2 * j + 1)], axis=0)


_NT = (((1,), (1,)), ((), ()))


def permute_rows(x, dilation, *, inverse, name, add=None):
    s, w = x.shape
    tile = SPAN * dilation
    nat = pl.BlockSpec((tile, w), lambda i: (i, 0))
    streams = pl.BlockSpec((dilation, SPAN, w), lambda i: (0, i, 0))
    x3 = x.reshape(dilation, s // dilation, w) if inverse else x

    def body(*refs):
        if not inverse:
            x_ref, o_ref = refs
            o_ref[...] = jnp.swapaxes(x_ref[...].reshape(SPAN, dilation, w), 0, 1)
            return
        val = jnp.swapaxes(refs[0][...], 0, 1).reshape(tile, w)
        if add is not None:
            val = val + refs[1][...]
        refs[-1][...] = val

    out = pl.pallas_call(
        body,
        name=name,
        grid=(s // tile,),
        in_specs=([streams] + ([nat] if add is not None else [])) if inverse else [nat],
        out_specs=nat if inverse else streams,
        out_shape=jax.ShapeDtypeStruct((s, w) if inverse else (dilation, s // dilation, w), x.dtype),
        compiler_params=_params(("parallel",)),
    )(*([x3] + ([add] if add is not None else [])))
    return out.reshape(s, w)


def _qkv_specs(block_of):
    def spec(which, prev):
        def index(*grid):
            blk = block_of(*grid)
            return (jnp.maximum(blk - 1, 0) if prev else blk, which)

        return pl.BlockSpec((SPAN, D_MODEL), index)

    return [spec(0, False), spec(1, True), spec(1, False), spec(2, True), spec(2, False)]


def attn_fwd(qkv_p, pat, *, name):
    _, dilation = B_PATTERNS[pat]
    nb = SEQ // dilation // SPAN
    bias, _ = alibi_tables(dilation)

    def body(q_ref, kp_ref, kc_ref, vp_ref, vc_ref, bias_ref, o_ref, lse_ref):
        n = pl.program_id(1)
        lane = _lane((SPAN, LANES))
        lo_mask = lane < B_HEAD_DIM
        first_prev = jnp.logical_and(n == 0, _lane((2 * SPAN, 2 * SPAN)) < SPAN)
        q = q_ref[...] * jnp.asarray(B_HEAD_DIM**-0.5, BF16)
        kk = jnp.concatenate([kp_ref[...], kc_ref[...]], axis=0)
        vv = jnp.concatenate([vp_ref[...], vc_ref[...]], axis=0)
        halves = (lo_mask.astype(BF16), jnp.logical_not(lo_mask).astype(BF16))
        stats = jnp.zeros((SPAN, LANES), F32)
        for j in range(B_HEADS // 2):
            cs = slice(j * LANES, (j + 1) * LANES)
            sc = lax.dot_general(_pair_rows(q[:, cs], halves), kk[:, cs], _NT, preferred_element_type=F32)
            sc = jnp.where(first_prev, NEG, sc + bias_ref[j])
            m = jnp.max(sc, axis=1, keepdims=True)
            p = jnp.exp(sc - m)
            l = jnp.sum(p, axis=1, keepdims=True)
            acc = jnp.dot(p.astype(BF16), vv[:, cs], preferred_element_type=F32) * (1.0 / l)
            lse_pair = m + jnp.log(l)
            o_ref[:, cs] = jnp.where(lo_mask, acc[:SPAN], acc[SPAN:]).astype(BF16)
            stats = jnp.where(lane == 2 * j, lse_pair[:SPAN], stats)
            stats = jnp.where(lane == 2 * j + 1, lse_pair[SPAN:], stats)
        lse_ref[...] = stats

    return pl.pallas_call(
        body,
        name=name,
        grid=(dilation, nb),
        in_specs=[
            *_qkv_specs(lambda r, n: r * nb + n),
            pl.BlockSpec((B_HEADS // 2, 2 * SPAN, 2 * SPAN), lambda r, n: (0, 0, 0)),
        ],
        out_specs=[
            pl.BlockSpec((SPAN, D_MODEL), lambda r, n: (r * nb + n, 0)),
            pl.BlockSpec((SPAN, LANES), lambda r, n: (r * nb + n, 0)),
        ],
        out_shape=[jax.ShapeDtypeStruct((SEQ, D_MODEL), BF16), jax.ShapeDtypeStruct((SEQ, LANES), F32)],
        compiler_params=_params(("parallel", "arbitrary")),
    )(qkv_p, qkv_p, qkv_p, qkv_p, qkv_p, bias)


def attn_combine(outs, lses, *, name):
    def body(o0, o1, o2, l0, l1, l2, ob_ref, of_ref, lse_ref):
        ls = [l0[...], l1[...], l2[...]]
        m = jnp.maximum(jnp.maximum(ls[0], ls[1]), ls[2])
        tot = jnp.log(jnp.exp(ls[0] - m) + jnp.exp(ls[1] - m) + jnp.exp(ls[2] - m)) + m
        o = None
        for o_ref, l in zip((o0, o1, o2), ls, strict=True):
            term = _expand_groups(jnp.exp(l - tot)) * o_ref[...]
            o = term if o is None else o + term
        ob_ref[...] = o.astype(BF16)
        of_ref[...] = o
        lse_ref[...] = tot

    return pl.pallas_call(
        body,
        name=name,
        grid=(SEQ // ROW_TILE,),
        in_specs=[_rows(D_MODEL)] * 3 + [_rows(LANES)] * 3,
        out_specs=[_rows(D_MODEL), _rows(D_MODEL), _rows(LANES)],
        out_shape=[
            jax.ShapeDtypeStruct((SEQ, D_MODEL), BF16),
            jax.ShapeDtypeStruct((SEQ, D_MODEL), F32),
            jax.ShapeDtypeStruct((SEQ, LANES), F32),
        ],
        compiler_params=_params(("parallel",)),
    )(*outs, *lses)


def attn_delta(do, o, *, name):
    def body(do_ref, o_ref, dob_ref, dl_ref):
        do_v = do_ref[...]
        dob_ref[...] = do_v.astype(BF16)
        dl_ref[...] = _reduce_groups(do_v * o_ref[...])

    return pl.pallas_call(
        body,
        name=name,
        grid=(SEQ // ROW_TILE,),
        in_specs=[_rows(D_MODEL), _rows(D_MODEL)],
        out_specs=[_rows(D_MODEL), _rows(LANES)],
        out_shape=[jax.ShapeDtypeStruct((SEQ, D_MODEL), BF16), jax.ShapeDtypeStruct((SEQ, LANES), F32)],
        compiler_params=_params(("parallel",)),
    )(do, o)


def attn_bwd(qkv_p, do_p, lse_p, delta_p, pat, *, name):
    _, dilation = B_PATTERNS[pat]
    nb = SEQ // dilation // SPAN
    n_blocks = SEQ // SPAN
    bias, bias_t = alibi_tables(dilation)
    last = n_blocks - 1
    q_cols, k_cols, v_cols = (slice(i * D_MODEL, (i + 1) * D_MODEL) for i in range(3))

    def body(q_ref, kp_ref, kc_ref, vp_ref, vc_ref, do_ref, lse_ref, dl_ref, bias_ref, biast_ref, out_ref, cq_ref, ck_ref, cv_ref):
        g = pl.program_id(0)

        @pl.when(g == n_blocks)
        def _():
            out_ref[:, q_cols] = cq_ref[...].astype(BF16)
            out_ref[:, k_cols] = ck_ref[...].astype(BF16)
            out_ref[:, v_cols] = cv_ref[...].astype(BF16)

        @pl.when(g == 0)
        def _():
            cq_ref[...] = jnp.zeros_like(cq_ref)
            ck_ref[...] = jnp.zeros_like(ck_ref)
            cv_ref[...] = jnp.zeros_like(cv_ref)

        @pl.when(g < n_blocks)
        def _():
            lane = _lane((SPAN, LANES))
            lo_mask = lane < B_HEAD_DIM
            pair = (2 * SPAN, 2 * SPAN)
            first = lax.rem(g, nb) == 0
            prev_key_cols = jnp.logical_and(first, _lane(pair) < SPAN)
            prev_key_rows = jnp.logical_and(first, lax.broadcasted_iota(jnp.int32, pair, 0) < SPAN)
            q = q_ref[...] * jnp.asarray(B_HEAD_DIM**-0.5, BF16)
            kk = jnp.concatenate([kp_ref[...], kc_ref[...]], axis=0)
            vv = jnp.concatenate([vp_ref[...], vc_ref[...]], axis=0)
            do_v = do_ref[...]
            lse_v = lse_ref[...]
            dl_v = dl_ref[...]
            lse_t = lse_v.T
            dl_t = dl_v.T
            halves = (lo_mask.astype(BF16), jnp.logical_not(lo_mask).astype(BF16))
            for j in range(B_HEADS // 2):
                cs = slice(j * LANES, (j + 1) * LANES)
                kp, vp = kk[:, cs], vv[:, cs]
                q2 = _pair_rows(q[:, cs], halves)
                do2 = _pair_rows(do_v[:, cs], halves)
                lse_c, dl_c = _pair_column(lse_v, lane, j), _pair_column(dl_v, lane, j)
                lse_r = jnp.concatenate([lse_t[2 * j : 2 * j + 1], lse_t[2 * j + 1 : 2 * j + 2]], axis=1)
                dl_r = jnp.concatenate([dl_t[2 * j : 2 * j + 1], dl_t[2 * j + 1 : 2 * j + 2]], axis=1)
                sc = lax.dot_general(q2, kp, _NT, preferred_element_type=F32)
                p = jnp.exp(jnp.where(prev_key_cols, NEG, sc + bias_ref[j]) - lse_c)
                dp = lax.dot_general(do2, vp, _NT, preferred_element_type=F32)
                ds = (p * (dp - dl_c)).astype(BF16)
                dq2 = jnp.dot(ds, kp, preferred_element_type=F32)
                sc_t = lax.dot_general(kp, q2, _NT, preferred_element_type=F32)
                p_t = jnp.exp(jnp.where(prev_key_rows, NEG, sc_t + biast_ref[j]) - lse_r)
                dp_t = lax.dot_general(vp, do2, _NT, preferred_element_type=F32)
                ds_t = (p_t * (dp_t - dl_r)).astype(BF16)
                dk_pair = jnp.dot(ds_t, q2, preferred_element_type=F32)
                dv_pair = jnp.dot(p_t.astype(BF16), do2, preferred_element_type=F32)
                oq = slice(j * LANES, (j + 1) * LANES)
                ok = slice(D_MODEL + j * LANES, D_MODEL + (j + 1) * LANES)
                ov = slice(2 * D_MODEL + j * LANES, 2 * D_MODEL + (j + 1) * LANES)
                out_ref[:, oq] = cq_ref[:, cs].astype(BF16)
                out_ref[:, ok] = (ck_ref[:, cs] + dk_pair[:SPAN]).astype(BF16)
                out_ref[:, ov] = (cv_ref[:, cs] + dv_pair[:SPAN]).astype(BF16)
                cq_ref[:, cs] = jnp.where(lo_mask, dq2[:SPAN], dq2[SPAN:]) * (B_HEAD_DIM**-0.5)
                ck_ref[:, cs] = dk_pair[SPAN:]
                cv_ref[:, cs] = dv_pair[SPAN:]

    def block_of(g):
        return jnp.minimum(g, last)

    def row_spec(width):
        return pl.BlockSpec((SPAN, width), lambda g: (block_of(g), 0))

    return pl.pallas_call(
        body,
        name=name,
        grid=(n_blocks + 1,),
        in_specs=[
            *_qkv_specs(block_of),
            row_spec(D_MODEL),
            row_spec(LANES),
            row_spec(LANES),
            pl.BlockSpec((B_HEADS // 2, 2 * SPAN, 2 * SPAN), lambda g: (0, 0, 0)),
            pl.BlockSpec((B_HEADS // 2, 2 * SPAN, 2 * SPAN), lambda g: (0, 0, 0)),
        ],
        out_specs=pl.BlockSpec((SPAN, 3 * D_MODEL), lambda g: (jnp.maximum(g - 1, 0), 0)),
        out_shape=jax.ShapeDtypeStruct((SEQ, 3 * D_MODEL), BF16),
        scratch_shapes=[pltpu.VMEM((SPAN, D_MODEL), F32)] * 3,
        compiler_params=_params(("arbitrary",)),
    )(qkv_p, qkv_p, qkv_p, qkv_p, qkv_p, do_p, lse_p, delta_p, bias, bias_t)


def _position():
    x, y, c = lax.axis_index("x"), lax.axis_index("y"), lax.axis_index("c")
    return x, y, c, 4 * x + 2 * y + c


def _peer(k, x, y, c):
    px = 1 - x if k & 4 else x
    py = 1 - y if k & 2 else y
    pc = 1 - c if k & 1 else c
    return (px, py, pc), 4 * px + 2 * py + pc


def _remote(src, dst, send_sem, recv_sem, device):
    return pltpu.make_async_remote_copy(
        src_ref=src, dst_ref=dst, send_sem=send_sem, recv_sem=recv_sem, device_id=device, device_id_type=MESH
    )


def _silu_bf16(cf):
    return (cf * (1.0 / (1.0 + jnp.exp(-cf)))).astype(BF16)


def ada_exchange(c8, w4, b4, ln8):
    nt, _, ncol = w4.shape

    def body(c8_ref, w_ref, b_ref, ln_ref, cg_ref, lng_ref, mrecv_ref, mloc_ref, send_sems, recv_sems):
        x, y, c, me = _position()
        cg_ref[me] = c8_ref[...]
        lng_ref[me] = ln_ref[...]
        first = []
        for k in range(1, N_DEV):
            dev, _ = _peer(k, x, y, c)
            first.append(_remote(c8_ref, cg_ref.at[me], send_sems.at[0, k], recv_sems.at[0, k], dev))
            first.append(_remote(ln_ref, lng_ref.at[me], send_sems.at[1, k], recv_sems.at[1, k], dev))
        for cp in first:
            cp.start()
        for k in range(1, N_DEV):
            dev, pid = _peer(k, x, y, c)
            _remote(c8_ref, cg_ref.at[pid], send_sems.at[0, k], recv_sems.at[0, k], dev).wait_recv()
            _remote(ln_ref, lng_ref.at[pid], send_sems.at[1, k], recv_sems.at[1, k], dev).wait_recv()
        sc = _silu_bf16(cg_ref[...].reshape(N_DEV * SUBLANES, D_MODEL))
        for t in range(nt):
            mloc_ref[t] = jnp.dot(sc, w_ref[t].astype(BF16), preferred_element_type=F32) + b_ref[t : t + 1, :]

        def group(dev_id):
            return pl.ds(pl.multiple_of(dev_id * SUBLANES, SUBLANES), SUBLANES)

        mrecv_ref[me] = mloc_ref[:, group(me), :]
        second = []
        for k in range(1, N_DEV):
            dev, pid = _peer(k, x, y, c)
            second.append(
                _remote(mloc_ref.at[:, group(pid), :], mrecv_ref.at[me], send_sems.at[2, k], recv_sems.at[2, k], dev)
            )
        for cp in second:
            cp.start()
        for k in range(1, N_DEV):
            dev, pid = _peer(k, x, y, c)
            _remote(
                mloc_ref.at[:, group(pid), :], mrecv_ref.at[pid], send_sems.at[2, k], recv_sems.at[2, k], dev
            ).wait_recv()
        for cp in first + second:
            cp.wait_send()

    return pl.pallas_call(
        body,
        name="ada_exchange",
        in_specs=[VMEM, VMEM, VMEM, VMEM],
        out_specs=[VMEM, VMEM, VMEM],
        out_shape=[
            jax.ShapeDtypeStruct((N_DEV, SUBLANES, D_MODEL), F32),
            jax.ShapeDtypeStruct((N_DEV, SUBLANES, LANES), F32),
            jax.ShapeDtypeStruct((N_DEV, nt, SUBLANES, ncol), F32),
        ],
        scratch_shapes=[
            pltpu.VMEM((nt, N_DEV * SUBLANES, ncol), F32),
            pltpu.SemaphoreType.DMA((3, N_DEV)),
            pltpu.SemaphoreType.DMA((3, N_DEV)),
        ],
        compiler_params=pltpu.CompilerParams(vmem_limit_bytes=VMEM_LIMIT_BYTES),
    )(c8, w4, b4, ln8)


def small_exchange(dmx, flat):
    def body(dmx_ref, flat_ref, dmrecv_ref, red_ref, land_ref, send_sems, recv_sems):
        x, y, c, me = _position()
        dmrecv_ref[me] = dmx_ref[me]
        land_ref[me] = flat_ref[me]
        first = []
        for k in range(1, N_DEV):
            dev, pid = _peer(k, x, y, c)
            first.append(_remote(dmx_ref.at[pid], dmrecv_ref.at[me], send_sems.at[0, k], recv_sems.at[0, k], dev))
            first.append(_remote(flat_ref.at[pid], land_ref.at[me], send_sems.at[1, k], recv_sems.at[1, k], dev))
        for cp in first:
            cp.start()
        for k in range(1, N_DEV):
            dev, pid = _peer(k, x, y, c)
            _remote(dmx_ref.at[pid], dmrecv_ref.at[pid], send_sems.at[0, k], recv_sems.at[0, k], dev).wait_recv()
            _remote(flat_ref.at[pid], land_ref.at[pid], send_sems.at[1, k], recv_sems.at[1, k], dev).wait_recv()
        total = land_ref[0]
        for s in range(1, N_DEV):
            total = total + land_ref[s]
        red_ref[me] = total
        second = []
        for k in range(1, N_DEV):
            dev, _ = _peer(k, x, y, c)
            second.append(_remote(red_ref.at[me], red_ref.at[me], send_sems.at[2, k], recv_sems.at[2, k], dev))
        for cp in second:
            cp.start()
        for k in range(1, N_DEV):
            dev, pid = _peer(k, x, y, c)
            _remote(red_ref.at[pid], red_ref.at[pid], send_sems.at[2, k], recv_sems.at[2, k], dev).wait_recv()
        for cp in first + second:
            cp.wait_send()

    return pl.pallas_call(
        body,
        name="small_exchange",
        in_specs=[VMEM, VMEM],
        out_specs=[VMEM, VMEM],
        out_shape=[jax.ShapeDtypeStruct(dmx.shape, F32), jax.ShapeDtypeStruct(flat.shape, F32)],
        scratch_shapes=[
            pltpu.VMEM(flat.shape, F32),
            pltpu.SemaphoreType.DMA((3, N_DEV)),
            pltpu.SemaphoreType.DMA((3, N_DEV)),
        ],
        compiler_params=pltpu.CompilerParams(vmem_limit_bytes=VMEM_LIMIT_BYTES),
    )(dmx, flat)


HBM = pl.BlockSpec(memory_space=pltpu.HBM)
SEM = pl.BlockSpec(memory_space=pltpu.SEMAPHORE)
EFFECT = pltpu.SideEffectType.DATAFLOW_SIDE_EFFECTING


def _own_slot(me, block):
    land = lax.empty((N_DEV, *block.shape), block.dtype)
    return lax.dynamic_update_slice_in_dim(land, block[None], me, axis=0)


N_CHIP_PEERS = 3


class Gather:
    def __init__(self, shards, lands, after, *, name):
        nt = len(shards)
        self.name = name

        def body(*refs):
            src_refs, land_refs = refs[:nt], refs[nt : 2 * nt]
            send_sems, recv_sems = refs[2 * nt + 1 : 3 * nt + 1], refs[3 * nt + 1 : 4 * nt + 1]
            token = refs[-1]
            x, y, c, me = _position()
            for t in range(nt):
                for k, dev in enumerate(self._targets(x, y, c)):
                    _remote(src_refs[t], land_refs[t].at[me], send_sems[t].at[k], recv_sems[t].at[k], dev).start()
            token[...] = jnp.zeros_like(token)

        outs = pl.pallas_call(
            body,
            name=name + "_start",
            in_specs=[HBM] * (2 * nt) + [ANY],
            out_specs=[SEM] * (2 * nt) + [HBM] * (2 * nt) + [VMEM],
            out_shape=[pltpu.SemaphoreType.DMA((1 + N_CHIP_PEERS,))] * (2 * nt)
            + [pltpu.HBM(a.shape, a.dtype) for a in (*shards, *lands)]
            + [jax.ShapeDtypeStruct((SUBLANES, LANES), F32)],
            input_output_aliases={i: 2 * nt + i for i in range(2 * nt)},
            compiler_params=pltpu.CompilerParams(has_side_effects=EFFECT),
        )(*[pltpu.with_memory_space_constraint(a, pltpu.HBM) for a in (*shards, *lands)], after)
        self.send_sems, self.recv_sems = list(outs[:nt]), list(outs[nt : 2 * nt])
        self.srcs, self.lands = list(outs[2 * nt : 3 * nt]), list(outs[3 * nt : 4 * nt])
        self.token = outs[-1]

    @staticmethod
    def _chips(x, y):
        return [(1 - x, y), (x, 1 - y), (1 - x, 1 - y)]

    @classmethod
    def _targets(cls, x, y, c):
        return [(x, y, 1 - c)] + [(*chip, c) for chip in cls._chips(x, y)]

    def zero(self):
        return self.token[0, 0]

    def wait(self, which, after, *, name):
        n = len(which)

        def slot(px, py, pc):
            return 4 * px + 2 * py + pc

        def pass_body(*refs):
            land_refs, recv_sems = refs[:n], refs[n : 2 * n]
            fwd_send, fwd_recv = refs[3 * n + 1 : 4 * n + 1], refs[4 * n + 1 : 5 * n + 1]
            x, y, c, _ = _position()
            for t in range(n):
                for j, chip in enumerate(self._chips(x, y)):
                    blk = land_refs[t].at[slot(*chip, c)]
                    _remote(blk, blk, fwd_send[t].at[j], recv_sems[t].at[1 + j], (*chip, c)).wait_recv()
                    _remote(blk, blk, fwd_send[t].at[j], fwd_recv[t].at[j], (x, y, 1 - c)).start()

        lands = [self.lands[t] for t in which]
        outs = pl.pallas_call(
            pass_body,
            name=name + "_pass",
            in_specs=[HBM] * n + [SEM] * n + [ANY],
            out_specs=[HBM] * n + [SEM] * (2 * n),
            out_shape=[pltpu.HBM(a.shape, a.dtype) for a in lands] + [pltpu.SemaphoreType.DMA((N_CHIP_PEERS,))] * (2 * n),
            input_output_aliases={i: i for i in range(n)},
            compiler_params=pltpu.CompilerParams(has_side_effects=EFFECT),
        )(*lands, *[self.recv_sems[t] for t in which], after)
        lands, fwd_send, fwd_recv = outs[:n], outs[n : 2 * n], outs[2 * n :]

        def wait_body(*refs):
            src_refs, land_refs = refs[:n], refs[n : 2 * n]
            send_sems, recv_sems = refs[2 * n : 3 * n], refs[3 * n : 4 * n]
            fwd_send, fwd_recv = refs[4 * n : 5 * n], refs[5 * n : 6 * n]
            x, y, c, me = _position()
            sibling = (x, y, 1 - c)
            for t in range(n):
                for k, dev in enumerate(self._targets(x, y, c)):
                    _remote(src_refs[t], land_refs[t].at[me], send_sems[t].at[k], recv_sems[t].at[k], dev).wait_send()
                blk = land_refs[t].at[slot(x, y, 1 - c)]
                _remote(blk, blk, send_sems[t].at[0], recv_sems[t].at[0], sibling).wait_recv()
                for j, chip in enumerate(self._chips(x, y)):
                    sent = land_refs[t].at[slot(*chip, c)]
                    _remote(sent, sent, fwd_send[t].at[j], fwd_recv[t].at[j], sibling).wait_send()
                    got = land_refs[t].at[slot(*chip, 1 - c)]
                    _remote(got, got, fwd_send[t].at[j], fwd_recv[t].at[j], sibling).wait_recv()

        srcs = [self.srcs[t] for t in which]
        outs = pl.pallas_call(
            wait_body,
            name=name,
            in_specs=[HBM] * (2 * n) + [SEM] * (4 * n),
            out_specs=[HBM] * (2 * n),
            out_shape=[pltpu.HBM(a.shape, a.dtype) for a in (*srcs, *lands)],
            input_output_aliases={i: i for i in range(2 * n)},
            compiler_params=pltpu.CompilerParams(has_side_effects=EFFECT),
        )(*srcs, *lands, *[self.send_sems[t] for t in which], *[self.recv_sems[t] for t in which], *fwd_send, *fwd_recv)
        return outs[n:]


class Scatter:
    def __init__(self, srcs, lands, after, *, name):
        self.name = name
        nt = self.nt = len(srcs)
        peers = N_DEV - 1

        def body(*refs):
            src_refs, land_refs = refs[:nt], refs[nt : 2 * nt]
            send_sems, recv_sems = refs[2 * nt + 1 : 3 * nt + 1], refs[3 * nt + 1 : 4 * nt + 1]
            token = refs[-1]
            x, y, c, me = _position()
            for t in range(nt):
                for k in range(1, N_DEV):
                    dev, pid = _peer(k, x, y, c)
                    src = src_refs[t].at[pid]
                    _remote(src, land_refs[t].at[me], send_sems[t].at[k - 1], recv_sems[t].at[k - 1], dev).start()
            token[...] = jnp.zeros_like(token)

        outs = pl.pallas_call(
            body,
            name=name + "_start",
            in_specs=[HBM] * (2 * nt) + [ANY],
            out_specs=[SEM] * (2 * nt) + [HBM] * (2 * nt) + [VMEM],
            out_shape=[pltpu.SemaphoreType.DMA((peers,))] * (2 * nt)
            + [pltpu.HBM(a.shape, a.dtype) for a in (*srcs, *lands)]
            + [jax.ShapeDtypeStruct((SUBLANES, LANES), F32)],
            input_output_aliases={i: 2 * nt + i for i in range(2 * nt)},
            compiler_params=pltpu.CompilerParams(has_side_effects=EFFECT),
        )(*[pltpu.with_memory_space_constraint(a, pltpu.HBM) for a in (*srcs, *lands)], after)
        self.send_sems, self.recv_sems = outs[:nt], outs[nt : 2 * nt]
        self.srcs, self.lands = outs[2 * nt : 3 * nt], outs[3 * nt : 4 * nt]
        self.token = outs[-1]

    def zero(self):
        return self.token[0, 0]

    def wait(self, which, after, *, name):
        n = len(which)

        def body(*refs):
            src_refs, land_refs = refs[:n], refs[n : 2 * n]
            send_sems, recv_sems = refs[2 * n : 3 * n], refs[3 * n : 4 * n]
            x, y, c, _ = _position()
            for t in range(n):
                for k in range(1, N_DEV):
                    dev, pid = _peer(k, x, y, c)
                    src = src_refs[t].at[pid]
                    cp = _remote(src, land_refs[t].at[pid], send_sems[t].at[k - 1], recv_sems[t].at[k - 1], dev)
                    cp.wait_send()
                    cp.wait_recv()

        srcs = [self.srcs[t] for t in which]
        lands = [self.lands[t] for t in which]
        outs = pl.pallas_call(
            body,
            name=name,
            in_specs=[HBM] * (2 * n) + [SEM] * (2 * n) + [ANY],
            out_specs=[HBM] * (2 * n),
            out_shape=[pltpu.HBM(a.shape, a.dtype) for a in (*srcs, *lands)],
            input_output_aliases={i: i for i in range(2 * n)},
            compiler_params=pltpu.CompilerParams(has_side_effects=EFFECT),
        )(*srcs, *lands, *[self.send_sems[t] for t in which], *[self.recv_sems[t] for t in which], after)
        return outs[n:]


def _adam_update(g, w, m, v):
    m2 = ADAM_B1 * m + (1.0 - ADAM_B1) * g
    v2 = ADAM_B2 * v + (1.0 - ADAM_B2) * jnp.square(g)
    m_hat = m2 / (1.0 - ADAM_B1**ADAM_STEP)
    v_hat = v2 / (1.0 - ADAM_B2**ADAM_STEP)
    delta = -ADAM_LR * (m_hat / (jnp.sqrt(v_hat) + ADAM_EPS) + ADAM_WD * w)
    return delta, m2, v2


def adamw(gparts, w, m, v, *, name):
    nl, r, c = w.shape
    p = gparts[0].shape[0]
    tr = r if r <= 256 else (256 if c <= D_MODEL else 128)
    ni = r // tr

    def body(*refs):
        g_refs = refs[:nl]
        w_ref, m_ref, v_ref, go_ref, d_ref, mo_ref, vo_ref = refs[nl:]
        for layer in range(nl):

            @pl.when(pl.program_id(0) == layer)
            def _(g_ref=g_refs[layer]):
                g = g_ref[0].astype(F32)
                for i in range(1, p):
                    g = g + g_ref[i].astype(F32)
                delta, m2, v2 = _adam_update(g, w_ref[...], m_ref[...], v_ref[...])
                go_ref[...] = g
                d_ref[...] = delta
                mo_ref[...] = m2
                vo_ref[...] = v2

    def parts_spec(layer):
        def index(l, i):
            return (0, jnp.where(l == layer, i, jnp.where(l < layer, 0, ni - 1)), 0)

        return pl.BlockSpec((p, tr, c), index)

    blk = pl.BlockSpec((None, tr, c), lambda l, i: (l, i, 0))
    return pl.pallas_call(
        body,
        name=name,
        grid=(nl, ni),
        in_specs=[*[parts_spec(layer) for layer in range(nl)], blk, blk, blk],
        out_specs=[blk] * 4,
        out_shape=[jax.ShapeDtypeStruct((nl, r, c), F32)] * 4,
        compiler_params=_params(("arbitrary", "arbitrary")),
    )(*gparts, w, m, v)


def ada_grad_adamw(cg, dmrecv, w4, m4, v4, *, name):
    nt, k, ncol = w4.shape

    def body(cg_ref, dm_ref, w_ref, m_ref, v_ref, go_ref, d_ref, mo_ref, vo_ref, gb_ref):
        sc = _silu_bf16(cg_ref[...].reshape(N_DEV * SUBLANES, k))
        dm = dm_ref[...].reshape(N_DEV * SUBLANES, ncol)
        g = lax.dot_general(sc, dm.astype(BF16), (((0,), (0,)), ((), ())), preferred_element_type=F32)
        delta, m2, v2 = _adam_update(g, w_ref[...], m_ref[...], v_ref[...])
        go_ref[...] = g
        d_ref[...] = delta
        mo_ref[...] = m2
        vo_ref[...] = v2
        gb_ref[...] = jnp.broadcast_to(jnp.sum(dm, axis=0, keepdims=True), (SUBLANES, ncol))

    wblk = pl.BlockSpec((None, k, ncol), lambda t: (t, 0, 0))
    return pl.pallas_call(
        body,
        name=name,
        grid=(nt,),
        in_specs=[
            pl.BlockSpec((N_DEV, SUBLANES, k), lambda t: (0, 0, 0)),
            pl.BlockSpec((N_DEV, None, SUBLANES, ncol), lambda t: (0, t, 0, 0)),
            wblk,
            wblk,
            wblk,
        ],
        out_specs=[wblk] * 4 + [pl.BlockSpec((None, SUBLANES, ncol), lambda t: (t, 0, 0))],
        out_shape=[jax.ShapeDtypeStruct((nt, k, ncol), F32)] * 4 + [jax.ShapeDtypeStruct((nt, SUBLANES, ncol), F32)],
        compiler_params=_params(("parallel",)),
    )(cg, dmrecv, w4, m4, v4)


def kernel(x, c, ada_w, ada_b, ln_g, ln_b, a_w_in, a_b_in, a_vn_g, a_vn_b, a_w_s, a_b_s, a_w_out, b_w_qkv, b_w_out, mlp_w_up, mlp_w_down, loss_target, m_ada_w, m_ada_b, m_ln_g, m_ln_b, m_a_w_in, m_a_b_in, m_a_vn_g, m_a_vn_b, m_a_w_s, m_a_b_s, m_a_w_out, m_b_w_qkv, m_b_w_out, m_mlp_w_up, m_mlp_w_down, v_ada_w, v_ada_b, v_ln_g, v_ln_b, v_a_w_in, v_a_b_in, v_a_vn_g, v_a_vn_b, v_a_w_s, v_a_b_s, v_a_w_out, v_b_w_qkv, v_b_w_out, v_mlp_w_up, v_mlp_w_down):
    x0 = x[0]
    target = loss_target[0]
    me = 4 * lax.axis_index("x") + 2 * lax.axis_index("y") + lax.axis_index("c")

    ada_w4 = ada_w.reshape(N_SUB, D_MODEL, -1)
    ada_b4 = ada_b.reshape(N_SUB, -1)
    ln8 = jnp.concatenate([ln_g.reshape(N_SUB, -1), ln_b.reshape(N_SUB, -1)], axis=0)
    c8 = jnp.broadcast_to(c, (SUBLANES, D_MODEL))
    cg, lng, mrecv = ada_exchange(c8, ada_w4, ada_b4, ln8)

    W_IN, W_AOUT, W_UP0, W_DN0, W_QKV, W_BOUT, W_UP1, W_DN1 = range(8)
    shards = [
        a_w_in[0].astype(BF16),
        a_w_out[0].astype(BF16),
        mlp_w_up[0].astype(BF16),
        mlp_w_down[0].astype(BF16),
        b_w_qkv[0].astype(BF16),
        b_w_out[0].astype(BF16),
        mlp_w_up[1].astype(BF16),
        mlp_w_down[1].astype(BF16),
    ]
    gather = Gather(shards, [_own_slot(me, s) for s in shards], mrecv, name="gather")

    modv = mrecv[:, :, 0, :].transpose(1, 0, 2).reshape(N_SUB, 3 * D_MODEL) + gather.zero()
    shift = [modv[t : t + 1, :D_MODEL] for t in range(N_SUB)]
    scale = [modv[t : t + 1, D_MODEL : 2 * D_MODEL] for t in range(N_SUB)]
    gate1 = [1.0 + modv[t : t + 1, 2 * D_MODEL :] for t in range(N_SUB)]
    lng_full = [lng[:, t, :].reshape(1, D_MODEL) for t in range(N_SUB)]
    lnb_full = [lng[:, N_SUB + t, :].reshape(1, D_MODEL) for t in range(N_SUB)]

    ident = lambda acc: (acc,)
    relu2 = lambda a: jnp.square(jnp.maximum(a.astype(F32), 0.0)).astype(BF16)
    vn_g, vn_b, w_s = a_vn_g, a_vn_b, a_w_s[0]
    bias_full = jnp.repeat(a_b_s[0].T, A_GROUP_DIM, axis=1)
    w_up3, w_dn3 = [None, None], [None, None]

    def mlp_forward(i, h, after):
        up, dn = gather.wait([W_UP0, W_DN0] if i == 0 else [W_UP1, W_DN1], after, name=f"gather_wait_mlp{i}")
        w_up3[i], w_dn3[i] = up, dn.reshape(1, D_FF, D_MODEL)
        (a,) = mm_nn(h, w_up3[i], name=f"mlp{i}_up", tm=2048, ps=2, tn=512, tk=D_MODEL, epilogue=ident, outs=(BF16,))
        (y,) = mm_nn(
            a, w_dn3[i], name=f"mlp{i}_down", tm=2048, ps=1, tn=D_MODEL, tk=1024, prologue=relu2, epilogue=ident, outs=(BF16,)
        )
        return a, y

    h0 = modulate(x0, scale[0], shift[0], name="modulate0")
    w_in3, w_aout3 = gather.wait([W_IN, W_AOUT], h0, name="gather_wait_a")
    w_aout3 = w_aout3.reshape(1, D_MODEL, D_MODEL)
    (a_pre,) = mm_nn(
        h0, w_in3, name="a_in", tm=2048, ps=4, tn=256, tk=D_MODEL, epilogue=lambda acc, b: (acc + b,),
        extras=[(a_b_in, "row")], outs=(BF16,),
    )
    p_gate = gate_fwd(a_pre, vn_g, vn_b, w_s, bias_full, name="gate_fwd")
    (y0,) = mm_nn(p_gate, w_aout3, name="a_out", tm=2048, ps=1, tn=D_MODEL, tk=D_MODEL, epilogue=ident, outs=(BF16,))
    x1, h1 = residual_ln(x0, y0, gate1[0], lng_full[0], lnb_full[0], scale[1], shift[1], name="res_ln0")
    a1, y1 = mlp_forward(0, h1, y0)
    x2, h2 = residual_ln(x1, y1, gate1[1], lng_full[1], lnb_full[1], scale[2], shift[2], name="res_ln1")
    w_qkv_shards, w_bout3 = gather.wait([W_QKV, W_BOUT], y1, name="gather_wait_b")
    w_bout3 = w_bout3.reshape(1, D_MODEL, D_MODEL)
    w_qkv3 = w_qkv_shards.transpose(1, 0, 2).reshape(1, D_MODEL, D_QKV)
    pat_tiles = 3
    dil = [d for _, d in B_PATTERNS]
    h2_p = [h2] + [permute_rows(h2, dil[g], inverse=False, name=f"perm_h{g}") for g in range(1, N_PAT)]
    qkv_p, pat_o, pat_lse = [], [], []
    for g in range(N_PAT):
        (qkv_g,) = mm_nn(
            h2_p[g], w_qkv3, name=f"b_qkv{g}", tm=2048, ps=1, tn=D_MODEL, tk=D_MODEL, epilogue=ident, outs=(BF16,),
            b_tile0=pat_tiles * g, b_tiles=pat_tiles,
        )
        o_g, lse_g = attn_fwd(qkv_g, g, name=f"attn_fwd{g}")
        if g > 0:
            o_g = permute_rows(o_g, dil[g], inverse=True, name=f"unperm_o{g}")
            lse_g = permute_rows(lse_g, dil[g], inverse=True, name=f"unperm_lse{g}")
        qkv_p.append(qkv_g)
        pat_o.append(o_g)
        pat_lse.append(lse_g)
    o_b, o_f, lse = attn_combine(pat_o, pat_lse, name="attn_combine")
    (y2,) = mm_nn(o_b, w_bout3, name="b_out", tm=2048, ps=1, tn=D_MODEL, tk=D_MODEL, epilogue=ident, outs=(BF16,))
    x3, h3 = residual_ln(x2, y2, gate1[2], lng_full[2], lnb_full[2], scale[3], shift[3], name="res_ln2")
    a3, y3 = mlp_forward(1, h3, y2)
    dxo, loss_local = residual_ln_loss(x3, y3, gate1[3], lng_full[3], lnb_full[3], target, name="res_ln3_loss")

    def scatter(parts, after, name):
        parts = [p.reshape(N_DEV, -1, p.shape[-1]) for p in parts]
        lands = [_own_slot(me, lax.dynamic_index_in_dim(p, me, 0, keepdims=False)) for p in parts]
        return Scatter(parts, lands, after, name=name)

    def mlp_backward(i, h, a, dy):
        (da,) = mm_nt(
            dy,
            w_dn3[i],
            name=f"mlp{i}_da",
            tm=2048,
            tko=1024,
            ps=1,
            tc=D_MODEL,
            epilogue=lambda acc, act: (acc * (2.0 * jnp.maximum(act.astype(F32), 0.0)),),
            extras=[(a, "full")],
            outs=(BF16,),
        )
        (dh,) = mm_nt(da, w_up3[i], name=f"mlp{i}_dh", tm=2048, tko=1024, ps=2, tc=512, epilogue=ident, outs=(F32,))
        dw_dn = mm_tn(
            a, dy, name=f"mlp{i}_dw_down", p=1, tk=1024, ps=1, tn=D_MODEL, tmc=2048, prologue=relu2, out_dtype=BF16
        )
        dw_up = mm_tn(h, da, name=f"mlp{i}_dw_up", p=N_DEV, tk=1024, ps=2, tn=512, tmc=2048, out_dtype=BF16)
        return scatter([dw_up, dw_dn], dh, f"scatter_mlp{i}"), dh

    dz3, dy3, st3 = residual_ln_bwd(name="res_bwd3", dxo=dxo, this=(x3, y3, gate1[3], lng_full[3], lnb_full[3]))
    rs_mlp1, dh3 = mlp_backward(1, h3, a3, dy3)
    dz2, dy2, st2 = residual_ln_bwd(
        name="res_bwd2", later=(dh3, dz3, scale[3] + rs_mlp1.zero()), this=(x2, y2, gate1[2], lng_full[2], lnb_full[2])
    )
    (d_o,) = mm_nt(dy2, w_bout3, name="b_do", tm=2048, tko=1024, ps=1, tc=D_MODEL, epilogue=ident, outs=(F32,))
    do_b, delta = attn_delta(d_o, o_f, name="attn_delta")
    dh2, dw_pat = None, []
    for g in range(N_PAT):
        do_g, lse_g, delta_g = do_b, lse, delta
        if g > 0:
            do_g = permute_rows(do_b, dil[g], inverse=False, name=f"perm_do{g}")
            lse_g = permute_rows(lse, dil[g], inverse=False, name=f"perm_lse{g}")
            delta_g = permute_rows(delta, dil[g], inverse=False, name=f"perm_delta{g}")
        dqkv_g = attn_bwd(qkv_p[g], do_g, lse_g, delta_g, g, name=f"attn_bwd{g}")
        (dh_g,) = mm_nt(
            dqkv_g, w_qkv3, name=f"b_dh{g}", tm=2048, tko=1024, ps=1, tc=D_MODEL, epilogue=ident, outs=(F32,),
            b_tile0=pat_tiles * g,
        )
        dh2 = dh_g if g == 0 else permute_rows(dh_g, dil[g], inverse=True, add=dh2, name=f"unperm_dh{g}")
        dw_pat.append(
            mm_tn(h2_p[g], dqkv_g, name=f"b_dw_qkv{g}", p=1, tk=1024, ps=1, tn=D_MODEL, tmc=2048, out_dtype=BF16)[0]
        )
    dw_bout = mm_tn(o_b, dy2, name="b_dw_out", p=1, tk=1024, ps=1, tn=D_MODEL, tmc=2048, out_dtype=BF16)
    dw_qkv = jnp.concatenate(dw_pat, axis=1).reshape(D_MODEL, N_DEV, -1).transpose(1, 0, 2)
    rs_b = scatter([dw_qkv, dw_bout], dh2, "scatter_b")
    dz1, dy1, st1 = residual_ln_bwd(
        name="res_bwd1", later=(dh2, dz2, scale[2] + rs_b.zero()), this=(x1, y1, gate1[1], lng_full[1], lnb_full[1])
    )
    rs_mlp0, dh1 = mlp_backward(0, h1, a1, dy1)
    dz0, dy0, st0 = residual_ln_bwd(
        name="res_bwd0", later=(dh1, dz1, scale[1] + rs_mlp0.zero()), this=(x0, y0, gate1[0], lng_full[0], lnb_full[0])
    )
    (dp_gate,) = mm_nt(dy0, w_aout3, name="a_dp", tm=2048, tko=1024, ps=1, tc=D_MODEL, epilogue=ident, outs=(F32,))
    da0, d_ws, d_bs, gate_rows = gate_bwd(a_pre, dp_gate, vn_g, vn_b, w_s, bias_full, name="gate_bwd")
    (dh0,) = mm_nt(da0, w_in3, name="a_dh", tm=2048, tko=1024, ps=4, tc=256, epilogue=ident, outs=(F32,))
    dw_aout = mm_tn(p_gate, dy0, name="a_dw_out", p=1, tk=1024, ps=1, tn=D_MODEL, tmc=2048, out_dtype=BF16)
    dw_in = mm_tn(h0, da0, name="a_dw_in", p=N_DEV, tk=1024, ps=4, tn=256, tmc=2048, out_dtype=BF16)
    grad_x, stf = residual_ln_bwd(name="res_bwd_in", later=(dh0, dz0, scale[0]), x_out=x0)

    stats_after = [stf, st0, st1, st2]
    stats_own = [st0, st1, st2, st3]
    dm = jnp.stack(
        [
            jnp.concatenate(
                [stats_after[t][ST_DSHIFT], stats_after[t][ST_DSCALE], stats_own[t][ST_DGATE]], axis=0
            )
            for t in range(N_SUB)
        ]
    )
    ncol = 3 * D_MODEL // N_DEV
    dmx = jnp.pad(
        dm.reshape(N_SUB, N_DEV, ncol).transpose(1, 0, 2)[:, :, None, :], ((0, 0), (0, 0), (0, SUBLANES - 1), (0, 0))
    )
    small = [
        gate_rows[0],
        gate_rows[1],
        d_ws.reshape(-1),
        d_bs[:, :A_GROUPS].T.reshape(-1),
        *[stats_own[t][ST_DG] for t in range(N_SUB)],
        *[stats_own[t][ST_DB] for t in range(N_SUB)],
        jnp.pad(loss_local.reshape(1), (0, LANES - 1)),
    ]
    n_small = sum(s.size for s in small)
    part_rows = -(-n_small // (N_DEV * LANES * SUBLANES)) * SUBLANES
    flat = jnp.concatenate(small + [jnp.zeros((N_DEV * part_rows * LANES - n_small,), F32)])
    dmrecv, reduced = small_exchange(dmx, flat.reshape(N_DEV, part_rows, LANES))
    reduced = reduced.reshape(-1)
    sizes = [2 * D_MODEL, D_MODEL, D_MODEL, A_GROUPS * CHUNK * CHUNK, A_GROUPS * CHUNK, N_SUB * D_MODEL, N_SUB * D_MODEL]
    offs = [sum(sizes[:i]) for i in range(len(sizes) + 1)]
    g_b_in, g_vn_g, g_vn_b, g_ws, g_bs, g_lng, g_lnb = [reduced[offs[i] : offs[i + 1]] for i in range(len(sizes))]
    loss = reduced[offs[-1]]

    results = {}

    def update(wname, gparts, w, m, v):
        shape = w.shape
        layers = len(gparts) if isinstance(gparts, list) else 1
        w3 = w.reshape(layers, -1, shape[-1])
        parts = [g.reshape(g.shape[0], *w3.shape[1:]) for g in (gparts if layers > 1 else [gparts])]
        outs = adamw(parts, w3, m.reshape(w3.shape), v.reshape(w3.shape), name=f"adamw_{wname}")
        results[wname] = [o.reshape(shape) for o in outs]

    rs_a = scatter([dw_in, dw_aout], reduced, "scatter_a")
    ada_outs = ada_grad_adamw(cg + rs_a.zero(), dmrecv, ada_w4, m_ada_w.reshape(ada_w4.shape), v_ada_w.reshape(ada_w4.shape), name="ada_grad_adamw")
    results["ada_w"] = [o.reshape(ada_w.shape) for o in ada_outs[:4]]
    update("ada_b", ada_outs[4][:, 0, :][None], ada_b, m_ada_b, v_ada_b)
    ln_cols = D_MODEL // N_DEV
    my_ln = lambda gfull: lax.dynamic_slice_in_dim(gfull.reshape(N_SUB, N_DEV, ln_cols), me, 1, axis=1)
    update("ln_g", my_ln(g_lng).reshape(1, N_SUB, ln_cols), ln_g, m_ln_g, v_ln_g)
    update("ln_b", my_ln(g_lnb).reshape(1, N_SUB, ln_cols), ln_b, m_ln_b, v_ln_b)
    update("a_b_in", g_b_in[None], a_b_in, m_a_b_in, v_a_b_in)
    update("a_vn_g", g_vn_g[None], a_vn_g, m_a_vn_g, v_a_vn_g)
    update("a_vn_b", g_vn_b[None], a_vn_b, m_a_vn_b, v_a_vn_b)
    update("a_w_s", g_ws[None], a_w_s, m_a_w_s, v_a_w_s)
    update("a_b_s", g_bs[None], a_b_s, m_a_b_s, v_a_b_s)
    g_up1, g_dn1 = rs_mlp1.wait([0, 1], grad_x, name="scatter_wait_mlp1")
    g_qkv, g_bout = rs_b.wait([0, 1], grad_x, name="scatter_wait_b")
    update("b_w_qkv", g_qkv, b_w_qkv, m_b_w_qkv, v_b_w_qkv)
    update("b_w_out", g_bout, b_w_out, m_b_w_out, v_b_w_out)
    g_up0, g_dn0 = rs_mlp0.wait([0, 1], grad_x, name="scatter_wait_mlp0")
    update("mlp_w_up", [g_up0, g_up1], mlp_w_up, m_mlp_w_up, v_mlp_w_up)
    update("mlp_w_down", [g_dn0, g_dn1], mlp_w_down, m_mlp_w_down, v_mlp_w_down)
    g_in, g_aout = rs_a.wait([0, 1], grad_x, name="scatter_wait_a")
    update("a_w_in", g_in, a_w_in, m_a_w_in, v_a_w_in)
    update("a_w_out", g_aout, a_w_out, m_a_w_out, v_a_w_out)

    order = ["ada_w", "ada_b", "ln_g", "ln_b", "a_w_in", "a_b_in", "a_vn_g", "a_vn_b", "a_w_s", "a_b_s", "a_w_out", "b_w_qkv", "b_w_out", "mlp_w_up", "mlp_w_down"]
    return (loss, grad_x[None], *[results[n][0] for n in order], *[results[n][1] for n in order],
            *[results[n][2] for n in order], *[results[n][3] for n in order])
```

```python
import math

import jax
import jax.numpy as jnp
from jax import lax
from jax.experimental import pallas as pl
from jax.experimental.pallas import tpu as pltpu

F32 = jnp.float32
BF16 = jnp.bfloat16
MESH = pl.DeviceIdType.MESH
ANY = pl.BlockSpec(memory_space=pl.ANY)
VMEM = pl.BlockSpec(memory_space=pltpu.VMEM)

N_DEV = 8
D_MODEL = 1024
SEQ = 4096
DEPTH = 2
CHUNK = 128
A_GROUPS = 16
A_GROUP_DIM = D_MODEL // A_GROUPS
B_HEADS = 16
B_HEAD_DIM = 64
B_PATTERNS = ((128, 1), (512, 4), (2048, 16))
N_PAT = len(B_PATTERNS)
SPAN = 128
D_FF = 4 * D_MODEL
D_QKV = N_PAT * 3 * D_MODEL
ALPHA = (2 * DEPTH) ** 0.25
LN_EPS = 1e-5
NEG = -1e30
ADAM_LR = 0.001
ADAM_B1 = 0.9
ADAM_B2 = 0.999
ADAM_EPS = 1e-08
ADAM_WD = 0.01
ADAM_STEP = 10
GELU_C = math.sqrt(2.0 / math.pi)
GELU_A = 0.044715

VMEM_LIMIT_BYTES = 56 * 1024 * 1024
LANES = 128
SUBLANES = 8
ROW_TILE = 512
N_SUB = 2 * DEPTH


def _params(sem):
    return pltpu.CompilerParams(dimension_semantics=sem, vmem_limit_bytes=VMEM_LIMIT_BYTES)


def _lane(shape):
    return lax.broadcasted_iota(jnp.int32, shape, len(shape) - 1)


def _split_bf16(x):
    hi = x.astype(BF16)
    lo = (x - hi.astype(F32)).astype(BF16)
    return hi, lo


def _group_expand_matrix(groups_padded, width):
    per = width // A_GROUPS
    r = lax.broadcasted_iota(jnp.int32, (groups_padded, width), 0)
    c = lax.broadcasted_iota(jnp.int32, (groups_padded, width), 1)
    return (c // per == r).astype(BF16)


def _group_reduce_matrix(width, groups_padded):
    per = width // A_GROUPS
    r = lax.broadcasted_iota(jnp.int32, (width, groups_padded), 0)
    c = lax.broadcasted_iota(jnp.int32, (width, groups_padded), 1)
    return (r // per == c).astype(BF16)


def _expand_groups(w):
    e = _group_expand_matrix(LANES, D_MODEL)
    hi, lo = _split_bf16(w)
    return jnp.dot(hi, e, preferred_element_type=F32) + jnp.dot(lo, e, preferred_element_type=F32)


def _reduce_groups(x):
    e = _group_reduce_matrix(D_MODEL, LANES)
    hi, lo = _split_bf16(x)
    return jnp.dot(hi, e, preferred_element_type=F32) + jnp.dot(lo, e, preferred_element_type=F32)


def _column_tiles(p, n, ps, tn):
    assert (ps == 1 or tn == n) and p % ps == 0 and n % tn == 0
    q = n // tn
    return (p // ps) * q, q


def _extra_specs(extras, tm, width):
    specs = []
    for _, kind in extras:
        if kind == "row":
            specs.append(pl.BlockSpec((1, width), lambda i, j, c: (0, j)))
        else:
            specs.append(pl.BlockSpec((tm, width), lambda i, j, c: (i, j)))
    return specs


def mm_nn(a, b3, *, name, tm, ps, tn, tk, epilogue, extras=(), outs, prologue=None, b_tile0=0, b_tiles=None):
    m, k = a.shape
    p, _, n = b3.shape
    nj, q = _column_tiles(p, n, ps, tn)
    nj = nj if b_tiles is None else b_tiles
    nk = k // tk
    width = ps * tn

    def body(a_ref, b_ref, *rest):
        ex = rest[: len(extras)]
        out_refs = rest[len(extras) : len(extras) + len(outs)]
        kk = pl.program_id(2)
        av = a_ref[...] if prologue is None else prologue(a_ref[...])

        def finish(cs, acc):
            res = epilogue(acc, *[e[:, cs] for e in ex])
            for o_ref, r in zip(out_refs, res, strict=True):
                o_ref[:, cs] = r.astype(o_ref.dtype)

        for s in range(ps):
            cs = slice(s * tn, (s + 1) * tn)
            part = jnp.dot(av, b_ref[s], preferred_element_type=F32)
            if nk == 1:
                finish(cs, part)
                continue
            acc_ref = rest[-1]

            @pl.when(kk == 0)
            def _(part=part, cs=cs):
                acc_ref[:, cs] = part

            @pl.when(kk > 0)
            def _(part=part, cs=cs):
                acc_ref[:, cs] += part

        if nk > 1:

            @pl.when(kk == nk - 1)
            def _():
                for s in range(ps):
                    cs = slice(s * tn, (s + 1) * tn)
                    finish(cs, rest[-1][:, cs])

    return pl.pallas_call(
        body,
        name=name,
        grid=(m // tm, nj, nk),
        in_specs=[
            pl.BlockSpec((tm, tk), lambda i, j, kk: (i, kk)),
            pl.BlockSpec((ps, tk, tn), lambda i, j, kk: ((j + b_tile0) // q, kk, (j + b_tile0) % q)),
            *_extra_specs(extras, tm, width),
        ],
        out_specs=[pl.BlockSpec((tm, width), lambda i, j, kk: (i, j)) for _ in outs],
        out_shape=[jax.ShapeDtypeStruct((m, nj * width), dt) for dt in outs],
        scratch_shapes=[pltpu.VMEM((tm, width), F32)] if nk > 1 else [],
        compiler_params=_params(("parallel", "parallel", "arbitrary")),
    )(a, b3, *[arr for arr, _ in extras])


def mm_nt(g, b3, *, name, tm, tko, ps, tc, epilogue, extras=(), outs, b_tile0=0):
    m = g.shape[0]
    p, k, n = b3.shape
    _, q = _column_tiles(p, n, ps, tc)
    nc = g.shape[1] // (ps * tc)

    def body(g_ref, b_ref, *rest):
        ex = rest[: len(extras)]
        out_refs = rest[len(extras) : len(extras) + len(outs)]
        c = pl.program_id(2)
        part = None
        for s in range(ps):
            d = lax.dot_general(
                g_ref[:, s * tc : (s + 1) * tc], b_ref[s], (((1,), (1,)), ((), ())), preferred_element_type=F32
            )
            part = d if part is None else part + d

        def finish(acc):
            res = epilogue(acc, *[e[...] for e in ex])
            for o_ref, r in zip(out_refs, res, strict=True):
                o_ref[...] = r.astype(o_ref.dtype)

        if nc == 1:
            finish(part)
            return
        acc_ref = rest[-1]

        @pl.when(c == 0)
        def _():
            acc_ref[...] = part

        @pl.when(c > 0)
        def _():
            acc_ref[...] += part

        @pl.when(c == nc - 1)
        def _():
            finish(acc_ref[...])

    return pl.pallas_call(
        body,
        name=name,
        grid=(m // tm, k // tko, nc),
        in_specs=[
            pl.BlockSpec((tm, ps * tc), lambda i, j, c: (i, c)),
            pl.BlockSpec((ps, tko, tc), lambda i, j, c: ((c + b_tile0) // q, j, (c + b_tile0) % q)),
            *_extra_specs(extras, tm, tko),
        ],
        out_specs=[pl.BlockSpec((tm, tko), lambda i, j, c: (i, j)) for _ in outs],
        out_shape=[jax.ShapeDtypeStruct((m, k), dt) for dt in outs],
        scratch_shapes=[pltpu.VMEM((tm, tko), F32)] if nc > 1 else [],
        compiler_params=_params(("parallel", "parallel", "arbitrary")),
    )(g, b3, *[arr for arr, _ in extras])


def mm_tn(a, g, *, name, p, tk, ps, tn, tmc, out_dtype, prologue=None):
    m, k = a.shape
    n = g.shape[1] // p
    nj, q = _column_tiles(p, n, ps, tn)
    nc = m // tmc

    def body(a_ref, g_ref, o_ref, acc_ref):
        c = pl.program_id(2)
        av = a_ref[...] if prologue is None else prologue(a_ref[...])
        part = lax.dot_general(av, g_ref[...], (((0,), (0,)), ((), ())), preferred_element_type=F32)

        @pl.when(c == 0)
        def _():
            acc_ref[...] = part

        @pl.when(c > 0)
        def _():
            acc_ref[...] += part

        @pl.when(c == nc - 1)
        def _():
            for s in range(ps):
                o_ref[s] = acc_ref[:, s * tn : (s + 1) * tn].astype(o_ref.dtype)

    return pl.pallas_call(
        body,
        name=name,
        grid=(k // tk, nj, nc),
        in_specs=[
            pl.BlockSpec((tmc, tk), lambda i, j, c: (c, i)),
            pl.BlockSpec((tmc, ps * tn), lambda i, j, c: (c, j)),
        ],
        out_specs=pl.BlockSpec((ps, tk, tn), lambda i, j, c: (j // q, i, j % q)),
        out_shape=jax.ShapeDtypeStruct((p, k, n), out_dtype),
        scratch_shapes=[pltpu.VMEM((tk, ps * tn), F32)],
        compiler_params=_params(("parallel", "parallel", "arbitrary")),
    )(a, g)


def _rows(cols):
    return pl.BlockSpec((ROW_TILE, cols), lambda i: (i, 0))


def _vec(cols, rows=1):
    return pl.BlockSpec((rows, cols), lambda i: (0, 0))


def _layer_norm_hat(z):
    mu = jnp.mean(z, axis=-1, keepdims=True)
    zc = z - mu
    var = jnp.mean(zc * zc, axis=-1, keepdims=True)
    rstd = lax.rsqrt(var + LN_EPS)
    return zc * rstd, rstd


def modulate(x, scale, shift, *, name):
    s, d = x.shape

    def body(x_ref, sc_ref, sh_ref, h_ref):
        h_ref[...] = (x_ref[...] * (1.0 + sc_ref[...]) + sh_ref[...]).astype(BF16)

    return pl.pallas_call(
        body,
        name=name,
        grid=(s // ROW_TILE,),
        in_specs=[_rows(d), _vec(d), _vec(d)],
        out_specs=_rows(d),
        out_shape=jax.ShapeDtypeStruct((s, d), BF16),
        compiler_params=_params(("parallel",)),
    )(x, scale, shift)


def residual_ln(x, y, gate1, g, b, nscale, nshift, *, name):
    s, d = x.shape

    def body(x_ref, y_ref, gt_ref, g_ref, b_ref, sc_ref, sh_ref, xn_ref, hn_ref):
        z = ALPHA * x_ref[...] + gt_ref[...] * y_ref[...].astype(F32)
        xhat, _ = _layer_norm_hat(z)
        xn = xhat * g_ref[...] + b_ref[...]
        xn_ref[...] = xn
        hn_ref[...] = (xn * (1.0 + sc_ref[...]) + sh_ref[...]).astype(BF16)

    return pl.pallas_call(
        body,
        name=name,
        grid=(s // ROW_TILE,),
        in_specs=[_rows(d), _rows(d), _vec(d), _vec(d), _vec(d), _vec(d), _vec(d)],
        out_specs=[_rows(d), _rows(d)],
        out_shape=[jax.ShapeDtypeStruct((s, d), F32), jax.ShapeDtypeStruct((s, d), BF16)],
        compiler_params=_params(("parallel",)),
    )(x, y, gate1, g, b, nscale, nshift)


def residual_ln_loss(x, y, gate1, g, b, target, *, name):
    s, d = x.shape

    def body(x_ref, y_ref, gt_ref, g_ref, b_ref, t_ref, dx_ref, loss_ref):
        z = ALPHA * x_ref[...] + gt_ref[...] * y_ref[...].astype(F32)
        xhat, _ = _layer_norm_hat(z)
        err = xhat * g_ref[...] + b_ref[...] - t_ref[...]
        dx_ref[...] = err * (1.0 / d)
        part = jnp.sum(jnp.sum(err * err, axis=1, keepdims=True), axis=0, keepdims=True) * (0.5 / d)

        @pl.when(pl.program_id(0) == 0)
        def _():
            loss_ref[...] = part

        @pl.when(pl.program_id(0) > 0)
        def _():
            loss_ref[...] += part

    return pl.pallas_call(
        body,
        name=name,
        grid=(s // ROW_TILE,),
        in_specs=[_rows(d), _rows(d), _vec(d), _vec(d), _vec(d), _rows(d)],
        out_specs=[_rows(d), pl.BlockSpec((1, 1), lambda i: (0, 0))],
        out_shape=[jax.ShapeDtypeStruct((s, d), F32), jax.ShapeDtypeStruct((1, 1), F32)],
        compiler_params=_params(("arbitrary",)),
    )(x, y, gate1, g, b, target)


ST_DSCALE, ST_DSHIFT, ST_DG, ST_DB, ST_DGATE = 0, 1, 2, 3, 4


def residual_ln_bwd(*, name, later=None, dxo=None, x_out=None, this=None):
    lead = later[0] if later is not None else dxo
    s, d = lead.shape
    has_later, has_ln = later is not None, this is not None

    def body(*refs):
        refs = list(refs)
        if has_later:
            dh_ref, dzl_ref, scl_ref = refs[:3]
            refs = refs[3:]
        else:
            dxo_ref = refs.pop(0)
        if has_ln:
            x_ref, y_ref, gt_ref, g_ref, b_ref = refs[:5]
            refs = refs[5:]
            dz_ref, dy_ref, st_ref = refs
        else:
            xo_ref, dx_ref, st_ref = refs

        @pl.when(pl.program_id(0) == 0)
        def _():
            st_ref[...] = jnp.zeros_like(st_ref)

        def acc(row, val):
            st_ref[row : row + 1, :] += jnp.sum(val, axis=0, keepdims=True)

        if has_ln:
            y = y_ref[...].astype(F32)
            gate1 = gt_ref[...]
            xhat, rstd = _layer_norm_hat(ALPHA * x_ref[...] + gate1 * y)
            x_out_v = xhat * g_ref[...] + b_ref[...]
        else:
            x_out_v = xo_ref[...]
        if has_later:
            dh = dh_ref[...]
            g_out = ALPHA * dzl_ref[...] + dh * (1.0 + scl_ref[...])
            acc(ST_DSCALE, dh * x_out_v)
            acc(ST_DSHIFT, dh)
        else:
            g_out = dxo_ref[...]
        if not has_ln:
            dx_ref[...] = g_out
            return
        acc(ST_DG, g_out * xhat)
        acc(ST_DB, g_out)
        dxh = g_out * g_ref[...]
        m1 = jnp.mean(dxh, axis=-1, keepdims=True)
        m2 = jnp.mean(dxh * xhat, axis=-1, keepdims=True)
        dz = rstd * (dxh - m1 - xhat * m2)
        acc(ST_DGATE, dz * y)
        dz_ref[...] = dz
        dy_ref[...] = (dz * gate1).astype(BF16)

    ins, specs = [], []
    if has_later:
        ins += list(later)
        specs += [_rows(d), _rows(d), _vec(d)]
    else:
        ins += [dxo]
        specs += [_rows(d)]
    if not has_ln:
        ins += [x_out]
        specs += [_rows(d)]
    if has_ln:
        ins += list(this)
        specs += [_rows(d), _rows(d), _vec(d), _vec(d), _vec(d)]
        out_specs = [_rows(d), _rows(d), _vec(d, SUBLANES)]
        out_shape = [
            jax.ShapeDtypeStruct((s, d), F32),
            jax.ShapeDtypeStruct((s, d), BF16),
            jax.ShapeDtypeStruct((SUBLANES, d), F32),
        ]
    else:
        out_specs = [_rows(d), _vec(d, SUBLANES)]
        out_shape = [jax.ShapeDtypeStruct((s, d), F32), jax.ShapeDtypeStruct((SUBLANES, d), F32)]
    return pl.pallas_call(
        body,
        name=name,
        grid=(s // ROW_TILE,),
        in_specs=specs,
        out_specs=out_specs,
        out_shape=out_shape,
        compiler_params=_params(("arbitrary",)),
    )(*ins)


GATE_CHUNKS = 4


def _gelu(x):
    return 0.5 * x * (1.0 + jnp.tanh(GELU_C * (x + GELU_A * x * x * x)))


def _gelu_grad(x):
    t = jnp.tanh(GELU_C * (x + GELU_A * x * x * x))
    return 0.5 * (1.0 + t) + 0.5 * x * (1.0 - t * t) * (GELU_C * (1.0 + 3.0 * GELU_A * x * x))


def _causal_weights(w_ref, transpose):
    t = lax.broadcasted_iota(jnp.int32, (CHUNK, CHUNK), 0)
    s = lax.broadcasted_iota(jnp.int32, (CHUNK, CHUNK), 1)
    out = []
    for g in range(A_GROUPS):
        w = jnp.where(t >= s, w_ref[g], 0.0)
        out.append((w.T if transpose else w).astype(BF16))
    return out


def _spatial(ws, vn, lo_mask):
    rows = vn.shape[0]
    out_rows = []
    for r in range(rows // CHUNK):
        cols = []
        for j in range(A_GROUPS // 2):
            blk = vn[r * CHUNK : (r + 1) * CHUNK, j * LANES : (j + 1) * LANES]
            za = jnp.dot(ws[2 * j], blk, preferred_element_type=F32)
            zb = jnp.dot(ws[2 * j + 1], blk, preferred_element_type=F32)
            cols.append(jnp.where(lo_mask, za, zb))
        out_rows.append(jnp.concatenate(cols, axis=1))
    return jnp.concatenate(out_rows, axis=0)


def _gate_forward(a, vg, vb, ws, bias, lo_mask):
    u = _gelu(a[:, :D_MODEL])
    v = _gelu(a[:, D_MODEL:])
    vhat, rstd = _layer_norm_hat(v)
    vn = (vhat * vg + vb).astype(BF16)
    z = _spatial(ws, vn, lo_mask) + jnp.concatenate([bias] * (a.shape[0] // CHUNK), axis=0)
    return u, vhat, rstd, vn, z


def gate_fwd(a_pre, vn_g, vn_b, w_s, bias_full, *, name):
    s = a_pre.shape[0]
    tr = GATE_CHUNKS * CHUNK

    def body(a_ref, vg_ref, vb_ref, w_ref, bias_ref, p_ref):
        lo_mask = _lane((CHUNK, LANES)) < A_GROUP_DIM
        ws = _causal_weights(w_ref, transpose=False)
        u, _, _, _, z = _gate_forward(a_ref[...].astype(F32), vg_ref[...], vb_ref[...], ws, bias_ref[...], lo_mask)
        p_ref[...] = (u * z).astype(BF16)

    return pl.pallas_call(
        body,
        name=name,
        grid=(s // tr,),
        in_specs=[
            pl.BlockSpec((tr, 2 * D_MODEL), lambda i: (i, 0)),
            _vec(D_MODEL),
            _vec(D_MODEL),
            pl.BlockSpec((A_GROUPS, CHUNK, CHUNK), lambda i: (0, 0, 0)),
            _vec(D_MODEL, CHUNK),
        ],
        out_specs=pl.BlockSpec((tr, D_MODEL), lambda i: (i, 0)),
        out_shape=jax.ShapeDtypeStruct((s, D_MODEL), BF16),
        compiler_params=_params(("parallel",)),
    )(a_pre, vn_g, vn_b, w_s, bias_full)


def gate_bwd(a_pre, dp, vn_g, vn_b, w_s, bias_full, *, name):
    s = a_pre.shape[0]
    tr = GATE_CHUNKS * CHUNK
    nsteps = s // tr

    def body(a_ref, dp_ref, vg_ref, vb_ref, w_ref, bias_ref, da_ref, dw_ref, dbs_ref, rows_ref, dbias_acc):
        step = pl.program_id(0)
        lo_mask = _lane((CHUNK, LANES)) < A_GROUP_DIM

        @pl.when(step == 0)
        def _():
            dw_ref[...] = jnp.zeros_like(dw_ref)
            rows_ref[...] = jnp.zeros_like(rows_ref)
            dbias_acc[...] = jnp.zeros_like(dbias_acc)

        a = a_ref[...].astype(F32)
        vg = vg_ref[...]
        ws = _causal_weights(w_ref, transpose=False)
        wts = _causal_weights(w_ref, transpose=True)
        u, vhat, rstd, vn, z = _gate_forward(a, vg, vb_ref[...], ws, bias_ref[...], lo_mask)
        dp = dp_ref[...]
        du = dp * z
        dzz = dp * u
        dzz_b = dzz.astype(BF16)
        dvn = _spatial(wts, dzz_b, lo_mask)
        dbias = None
        for r in range(GATE_CHUNKS):
            rs = slice(r * CHUNK, (r + 1) * CHUNK)
            dbias = dzz[rs] if dbias is None else dbias + dzz[rs]
            for j in range(A_GROUPS // 2):
                cs = slice(j * LANES, (j + 1) * LANES)
                dblk = dzz[rs, cs]
                vblk = vn[rs, cs]
                for half in range(2):
                    keep = lo_mask if half == 0 else jnp.logical_not(lo_mask)
                    dm = jnp.where(keep, dblk, 0.0).astype(BF16)
                    dw_ref[2 * j + half] += lax.dot_general(
                        dm, vblk, (((1,), (1,)), ((), ())), preferred_element_type=F32
                    )
        dbias_acc[...] += dbias
        rows_ref[1:2, :D_MODEL] += jnp.sum(dvn * vhat, axis=0, keepdims=True)
        rows_ref[1:2, D_MODEL:] += jnp.sum(dvn, axis=0, keepdims=True)
        dvh = dvn * vg
        m1 = jnp.mean(dvh, axis=-1, keepdims=True)
        m2 = jnp.mean(dvh * vhat, axis=-1, keepdims=True)
        dv = rstd * (dvh - m1 - vhat * m2)
        da_u = du * _gelu_grad(a[:, :D_MODEL])
        da_v = dv * _gelu_grad(a[:, D_MODEL:])
        da_ref[:, :D_MODEL] = da_u.astype(BF16)
        da_ref[:, D_MODEL:] = da_v.astype(BF16)
        rows_ref[0:1, :D_MODEL] += jnp.sum(da_u, axis=0, keepdims=True)
        rows_ref[0:1, D_MODEL:] += jnp.sum(da_v, axis=0, keepdims=True)

        @pl.when(step == nsteps - 1)
        def _():
            t = lax.broadcasted_iota(jnp.int32, (CHUNK, CHUNK), 0)
            sx = lax.broadcasted_iota(jnp.int32, (CHUNK, CHUNK), 1)
            for g in range(A_GROUPS):
                dw_ref[g] = jnp.where(t >= sx, dw_ref[g], 0.0)
            dbs_ref[...] = _reduce_groups(dbias_acc[...])

    return pl.pallas_call(
        body,
        name=name,
        grid=(nsteps,),
        in_specs=[
            pl.BlockSpec((tr, 2 * D_MODEL), lambda i: (i, 0)),
            pl.BlockSpec((tr, D_MODEL), lambda i: (i, 0)),
            _vec(D_MODEL),
            _vec(D_MODEL),
            pl.BlockSpec((A_GROUPS, CHUNK, CHUNK), lambda i: (0, 0, 0)),
            _vec(D_MODEL, CHUNK),
        ],
        out_specs=[
            pl.BlockSpec((tr, 2 * D_MODEL), lambda i: (i, 0)),
            pl.BlockSpec((A_GROUPS, CHUNK, CHUNK), lambda i: (0, 0, 0)),
            _vec(LANES, CHUNK),
            _vec(2 * D_MODEL, SUBLANES),
        ],
        out_shape=[
            jax.ShapeDtypeStruct((s, 2 * D_MODEL), BF16),
            jax.ShapeDtypeStruct((A_GROUPS, CHUNK, CHUNK), F32),
            jax.ShapeDtypeStruct((CHUNK, LANES), F32),
            jax.ShapeDtypeStruct((SUBLANES, 2 * D_MODEL), F32),
        ],
        scratch_shapes=[pltpu.VMEM((CHUNK, D_MODEL), F32)],
        compiler_params=_params(("arbitrary",)),
    )(a_pre, dp, vn_g, vn_b, w_s, bias_full)


def alibi_tables(dilation):
    qi = jnp.arange(SPAN)[:, None]
    ki = jnp.arange(2 * SPAN)[None, :]
    diff = SPAN + qi - ki
    valid = (diff >= 0) & (diff <= SPAN)
    heads = jnp.arange(1, B_HEADS + 1, dtype=F32)
    slopes = jnp.exp2(-8.0 * heads / B_HEADS)
    bias = -slopes[:, None, None] * (dilation * diff).astype(F32)
    bias = jnp.where(valid[None], bias, NEG).reshape(B_HEADS // 2, 2 * SPAN, 2 * SPAN)
    return bias, bias.transpose(0, 2, 1)


def _pair_rows(x, halves):
    return jnp.concatenate([x * halves[0], x * halves[1]], axis=0)


def _pair_column(v, lane, j):
    pick = lambda h: jnp.sum(jnp.where(lane == h, v, 0.0), axis=1, keepdims=True)
    return jnp.concatenate([pick(2 * j), pick(2 * j + 1)], axis=0)


_NT = (((1,), (1,)), ((), ()))


def permute_rows(x, dilation, *, inverse, name, add=None):
    s, w = x.shape
    tile = SPAN * dilation
    nat = pl.BlockSpec((tile, w), lambda i: (i, 0))
    streams = pl.BlockSpec((dilation, SPAN, w), lambda i: (0, i, 0))
    x3 = x.reshape(dilation, s // dilation, w) if inverse else x

    def body(*refs):
        if not inverse:
            x_ref, o_ref = refs
            o_ref[...] = jnp.swapaxes(x_ref[...].reshape(SPAN, dilation, w), 0, 1)
            return
        val = jnp.swapaxes(refs[0][...], 0, 1).reshape(tile, w)
        if add is not None:
            val = val + refs[1][...]
        refs[-1][...] = val

    out = pl.pallas_call(
        body,
        name=name,
        grid=(s // tile,),
        in_specs=([streams] + ([nat] if add is not None else [])) if inverse else [nat],
        out_specs=nat if inverse else streams,
        out_shape=jax.ShapeDtypeStruct((s, w) if inverse else (dilation, s // dilation, w), x.dtype),
        compiler_params=_params(("parallel",)),
    )(*([x3] + ([add] if add is not None else [])))
    return out.reshape(s, w)


def _qkv_specs(block_of):
    def spec(which, prev):
        def index(*grid):
            blk = block_of(*grid)
            return (jnp.maximum(blk - 1, 0) if prev else blk, which)

        return pl.BlockSpec((SPAN, D_MODEL), index)

    return [spec(0, False), spec(1, True), spec(1, False), spec(2, True), spec(2, False)]


def attn_fwd(qkv_p, pat, *, name):
    _, dilation = B_PATTERNS[pat]
    nb = SEQ // dilation // SPAN
    bias, _ = alibi_tables(dilation)

    def body(q_ref, kp_ref, kc_ref, vp_ref, vc_ref, bias_ref, o_ref, lse_ref):
        n = pl.program_id(1)
        lane = _lane((SPAN, LANES))
        lo_mask = lane < B_HEAD_DIM
        first_prev = jnp.logical_and(n == 0, _lane((2 * SPAN, 2 * SPAN)) < SPAN)
        q = q_ref[...] * jnp.asarray(B_HEAD_DIM**-0.5, BF16)
        kk = jnp.concatenate([kp_ref[...], kc_ref[...]], axis=0)
        vv = jnp.concatenate([vp_ref[...], vc_ref[...]], axis=0)
        halves = (lo_mask.astype(BF16), jnp.logical_not(lo_mask).astype(BF16))
        stats = jnp.zeros((SPAN, LANES), F32)
        for j in range(B_HEADS // 2):
            cs = slice(j * LANES, (j + 1) * LANES)
            sc = lax.dot_general(_pair_rows(q[:, cs], halves), kk[:, cs], _NT, preferred_element_type=F32)
            sc = jnp.where(first_prev, NEG, sc + bias_ref[j])
            m = jnp.max(sc, axis=1, keepdims=True)
            p = jnp.exp(sc - m)
            l = jnp.sum(p, axis=1, keepdims=True)
            acc = jnp.dot(p.astype(BF16), vv[:, cs], preferred_element_type=F32) * (1.0 / l)
            lse_pair = m + jnp.log(l)
            o_ref[:, cs] = jnp.where(lo_mask, acc[:SPAN], acc[SPAN:]).astype(BF16)
            stats = jnp.where(lane == 2 * j, lse_pair[:SPAN], stats)
            stats = jnp.where(lane == 2 * j + 1, lse_pair[SPAN:], stats)
        lse_ref[...] = stats

    return pl.pallas_call(
        body,
        name=name,
        grid=(dilation, nb),
        in_specs=[
            *_qkv_specs(lambda r, n: r * nb + n),
            pl.BlockSpec((B_HEADS // 2, 2 * SPAN, 2 * SPAN), lambda r, n: (0, 0, 0)),
        ],
        out_specs=[
            pl.BlockSpec((SPAN, D_MODEL), lambda r, n: (r * nb + n, 0)),
            pl.BlockSpec((SPAN, LANES), lambda r, n: (r * nb + n, 0)),
        ],
        out_shape=[jax.ShapeDtypeStruct((SEQ, D_MODEL), BF16), jax.ShapeDtypeStruct((SEQ, LANES), F32)],
        compiler_params=_params(("parallel", "arbitrary")),
    )(qkv_p, qkv_p, qkv_p, qkv_p, qkv_p, bias)


def attn_combine(outs, lses, *, name):
    def body(o0, o1, o2, l0, l1, l2, ob_ref, of_ref, lse_ref):
        ls = [l0[...], l1[...], l2[...]]
        m = jnp.maximum(jnp.maximum(ls[0], ls[1]), ls[2])
        tot = jnp.log(jnp.exp(ls[0] - m) + jnp.exp(ls[1] - m) + jnp.exp(ls[2] - m)) + m
        o = None
        for o_ref, l in zip((o0, o1, o2), ls, strict=True):
            term = _expand_groups(jnp.exp(l - tot)) * o_ref[...]
            o = term if o is None else o + term
        ob_ref[...] = o.astype(BF16)
        of_ref[...] = o
        lse_ref[...] = tot

    return pl.pallas_call(
        body,
        name=name,
        grid=(SEQ // ROW_TILE,),
        in_specs=[_rows(D_MODEL)] * 3 + [_rows(LANES)] * 3,
        out_specs=[_rows(D_MODEL), _rows(D_MODEL), _rows(LANES)],
        out_shape=[
            jax.ShapeDtypeStruct((SEQ, D_MODEL), BF16),
            jax.ShapeDtypeStruct((SEQ, D_MODEL), F32),
            jax.ShapeDtypeStruct((SEQ, LANES), F32),
        ],
        compiler_params=_params(("parallel",)),
    )(*outs, *lses)


def attn_delta(do, o, *, name):
    def body(do_ref, o_ref, dob_ref, dl_ref):
        do_v = do_ref[...]
        dob_ref[...] = do_v.astype(BF16)
        dl_ref[...] = _reduce_groups(do_v * o_ref[...])

    return pl.pallas_call(
        body,
        name=name,
        grid=(SEQ // ROW_TILE,),
        in_specs=[_rows(D_MODEL), _rows(D_MODEL)],
        out_specs=[_rows(D_MODEL), _rows(LANES)],
        out_shape=[jax.ShapeDtypeStruct((SEQ, D_MODEL), BF16), jax.ShapeDtypeStruct((SEQ, LANES), F32)],
        compiler_params=_params(("parallel",)),
    )(do, o)


def attn_bwd(qkv_p, do_p, lse_p, delta_p, pat, *, name):
    _, dilation = B_PATTERNS[pat]
    nb = SEQ // dilation // SPAN
    n_blocks = SEQ // SPAN
    bias, bias_t = alibi_tables(dilation)
    last = n_blocks - 1
    q_cols, k_cols, v_cols = (slice(i * D_MODEL, (i + 1) * D_MODEL) for i in range(3))

    def body(q_ref, kp_ref, kc_ref, vp_ref, vc_ref, do_ref, lse_ref, dl_ref, bias_ref, biast_ref, out_ref, cq_ref, ck_ref, cv_ref):
        g = pl.program_id(0)

        @pl.when(g == n_blocks)
        def _():
            out_ref[:, q_cols] = cq_ref[...].astype(BF16)
            out_ref[:, k_cols] = ck_ref[...].astype(BF16)
            out_ref[:, v_cols] = cv_ref[...].astype(BF16)

        @pl.when(g == 0)
        def _():
            cq_ref[...] = jnp.zeros_like(cq_ref)
            ck_ref[...] = jnp.zeros_like(ck_ref)
            cv_ref[...] = jnp.zeros_like(cv_ref)

        @pl.when(g < n_blocks)
        def _():
            lane = _lane((SPAN, LANES))
            lo_mask = lane < B_HEAD_DIM
            pair = (2 * SPAN, 2 * SPAN)
            first = lax.rem(g, nb) == 0
            prev_key_cols = jnp.logical_and(first, _lane(pair) < SPAN)
            prev_key_rows = jnp.logical_and(first, lax.broadcasted_iota(jnp.int32, pair, 0) < SPAN)
            q = q_ref[...] * jnp.asarray(B_HEAD_DIM**-0.5, BF16)
            kk = jnp.concatenate([kp_ref[...], kc_ref[...]], axis=0)
            vv = jnp.concatenate([vp_ref[...], vc_ref[...]], axis=0)
            do_v = do_ref[...]
            lse_v = lse_ref[...]
            dl_v = dl_ref[...]
            lse_t = lse_v.T
            dl_t = dl_v.T
            halves = (lo_mask.astype(BF16), jnp.logical_not(lo_mask).astype(BF16))
            for j in range(B_HEADS // 2):
                cs = slice(j * LANES, (j + 1) * LANES)
                kp, vp = kk[:, cs], vv[:, cs]
                q2 = _pair_rows(q[:, cs], halves)
                do2 = _pair_rows(do_v[:, cs], halves)
                lse_c, dl_c = _pair_column(lse_v, lane, j), _pair_column(dl_v, lane, j)
                lse_r = jnp.concatenate([lse_t[2 * j : 2 * j + 1], lse_t[2 * j + 1 : 2 * j + 2]], axis=1)
                dl_r = jnp.concatenate([dl_t[2 * j : 2 * j + 1], dl_t[2 * j + 1 : 2 * j + 2]], axis=1)
                sc = lax.dot_general(q2, kp, _NT, preferred_element_type=F32)
                p = jnp.exp(jnp.where(prev_key_cols, NEG, sc + bias_ref[j]) - lse_c)
                dp = lax.dot_general(do2, vp, _NT, preferred_element_type=F32)
                ds = (p * (dp - dl_c)).astype(BF16)
                dq2 = jnp.dot(ds, kp, preferred_element_type=F32)
                sc_t = lax.dot_general(kp, q2, _NT, preferred_element_type=F32)
                p_t = jnp.exp(jnp.where(prev_key_rows, NEG, sc_t + biast_ref[j]) - lse_r)
                dp_t = lax.dot_general(vp, do2, _NT, preferred_element_type=F32)
                ds_t = (p_t * (dp_t - dl_r)).astype(BF16)
                dk_pair = jnp.dot(ds_t, q2, preferred_element_type=F32)
                dv_pair = jnp.dot(p_t.astype(BF16), do2, preferred_element_type=F32)
                oq = slice(j * LANES, (j + 1) * LANES)
                ok = slice(D_MODEL + j * LANES, D_MODEL + (j + 1) * LANES)
                ov = slice(2 * D_MODEL + j * LANES, 2 * D_MODEL + (j + 1) * LANES)
                out_ref[:, oq] = cq_ref[:, cs].astype(BF16)
                out_ref[:, ok] = (ck_ref[:, cs] + dk_pair[:SPAN]).astype(BF16)
                out_ref[:, ov] = (cv_ref[:, cs] + dv_pair[:SPAN]).astype(BF16)
                cq_ref[:, cs] = jnp.where(lo_mask, dq2[:SPAN], dq2[SPAN:]) * (B_HEAD_DIM**-0.5)
                ck_ref[:, cs] = dk_pair[SPAN:]
                cv_ref[:, cs] = dv_pair[SPAN:]

    def block_of(g):
        return jnp.minimum(g, last)

    def row_spec(width):
        return pl.BlockSpec((SPAN, width), lambda g: (block_of(g), 0))

    return pl.pallas_call(
        body,
        name=name,
        grid=(n_blocks + 1,),
        in_specs=[
            *_qkv_specs(block_of),
            row_spec(D_MODEL),
            row_spec(LANES),
            row_spec(LANES),
            pl.BlockSpec((B_HEADS // 2, 2 * SPAN, 2 * SPAN), lambda g: (0, 0, 0)),
            pl.BlockSpec((B_HEADS // 2, 2 * SPAN, 2 * SPAN), lambda g: (0, 0, 0)),
        ],
        out_specs=pl.BlockSpec((SPAN, 3 * D_MODEL), lambda g: (jnp.maximum(g - 1, 0), 0)),
        out_shape=jax.ShapeDtypeStruct((SEQ, 3 * D_MODEL), BF16),
        scratch_shapes=[pltpu.VMEM((SPAN, D_MODEL), F32)] * 3,
        compiler_params=_params(("arbitrary",)),
    )(qkv_p, qkv_p, qkv_p, qkv_p, qkv_p, do_p, lse_p, delta_p, bias, bias_t)


def _position():
    x, y, c = lax.axis_index("x"), lax.axis_index("y"), lax.axis_index("c")
    return x, y, c, 4 * x + 2 * y + c


def _peer(k, x, y, c):
    px = 1 - x if k & 4 else x
    py = 1 - y if k & 2 else y
    pc = 1 - c if k & 1 else c
    return (px, py, pc), 4 * px + 2 * py + pc


def _remote(src, dst, send_sem, recv_sem, device):
    return pltpu.make_async_remote_copy(
        src_ref=src, dst_ref=dst, send_sem=send_sem, recv_sem=recv_sem, device_id=device, device_id_type=MESH
    )


def _silu_bf16(cf):
    return (cf * (1.0 / (1.0 + jnp.exp(-cf)))).astype(BF16)


def ada_exchange(c8, w4, b4, ln8):
    nt, _, ncol = w4.shape

    def body(c8_ref, w_ref, b_ref, ln_ref, cg_ref, lng_ref, mrecv_ref, mloc_ref, send_sems, recv_sems):
        x, y, c, me = _position()
        cg_ref[me] = c8_ref[...]
        lng_ref[me] = ln_ref[...]
        first = []
        for k in range(1, N_DEV):
            dev, _ = _peer(k, x, y, c)
            first.append(_remote(c8_ref, cg_ref.at[me], send_sems.at[0, k], recv_sems.at[0, k], dev))
            first.append(_remote(ln_ref, lng_ref.at[me], send_sems.at[1, k], recv_sems.at[1, k], dev))
        for cp in first:
            cp.start()
        for k in range(1, N_DEV):
            dev, pid = _peer(k, x, y, c)
            _remote(c8_ref, cg_ref.at[pid], send_sems.at[0, k], recv_sems.at[0, k], dev).wait_recv()
            _remote(ln_ref, lng_ref.at[pid], send_sems.at[1, k], recv_sems.at[1, k], dev).wait_recv()
        sc = _silu_bf16(cg_ref[...].reshape(N_DEV * SUBLANES, D_MODEL))
        for t in range(nt):
            mloc_ref[t] = jnp.dot(sc, w_ref[t].astype(BF16), preferred_element_type=F32) + b_ref[t : t + 1, :]

        def group(dev_id):
            return pl.ds(pl.multiple_of(dev_id * SUBLANES, SUBLANES), SUBLANES)

        mrecv_ref[me] = mloc_ref[:, group(me), :]
        second = []
        for k in range(1, N_DEV):
            dev, pid = _peer(k, x, y, c)
            second.append(
                _remote(mloc_ref.at[:, group(pid), :], mrecv_ref.at[me], send_sems.at[2, k], recv_sems.at[2, k], dev)
            )
        for cp in second:
            cp.start()
        for k in range(1, N_DEV):
            dev, pid = _peer(k, x, y, c)
            _remote(
                mloc_ref.at[:, group(pid), :], mrecv_ref.at[pid], send_sems.at[2, k], recv_sems.at[2, k], dev
            ).wait_recv()
        for cp in first + second:
            cp.wait_send()

    return pl.pallas_call(
        body,
        name="ada_exchange",
        in_specs=[VMEM, VMEM, VMEM, VMEM],
        out_specs=[VMEM, VMEM, VMEM],
        out_shape=[
            jax.ShapeDtypeStruct((N_DEV, SUBLANES, D_MODEL), F32),
            jax.ShapeDtypeStruct((N_DEV, SUBLANES, LANES), F32),
            jax.ShapeDtypeStruct((N_DEV, nt, SUBLANES, ncol), F32),
        ],
        scratch_shapes=[
            pltpu.VMEM((nt, N_DEV * SUBLANES, ncol), F32),
            pltpu.SemaphoreType.DMA((3, N_DEV)),
            pltpu.SemaphoreType.DMA((3, N_DEV)),
        ],
        compiler_params=pltpu.CompilerParams(vmem_limit_bytes=VMEM_LIMIT_BYTES),
    )(c8, w4, b4, ln8)


def small_exchange(dmx, flat):
    def body(dmx_ref, flat_ref, dmrecv_ref, red_ref, land_ref, send_sems, recv_sems):
        x, y, c, me = _position()
        dmrecv_ref[me] = dmx_ref[me]
        land_ref[me] = flat_ref[me]
        first = []
        for k in range(1, N_DEV):
            dev, pid = _peer(k, x, y, c)
            first.append(_remote(dmx_ref.at[pid], dmrecv_ref.at[me], send_sems.at[0, k], recv_sems.at[0, k], dev))
            first.append(_remote(flat_ref.at[pid], land_ref.at[me], send_sems.at[1, k], recv_sems.at[1, k], dev))
        for cp in first:
            cp.start()
        for k in range(1, N_DEV):
            dev, pid = _peer(k, x, y, c)
            _remote(dmx_ref.at[pid], dmrecv_ref.at[pid], send_sems.at[0, k], recv_sems.at[0, k], dev).wait_recv()
            _remote(flat_ref.at[pid], land_ref.at[pid], send_sems.at[1, k], recv_sems.at[1, k], dev).wait_recv()
        total = land_ref[0]
        for s in range(1, N_DEV):
            total = total + land_ref[s]
        red_ref[me] = total
        second = []
        for k in range(1, N_DEV):
            dev, _ = _peer(k, x, y, c)
            second.append(_remote(red_ref.at[me], red_ref.at[me], send_sems.at[2, k], recv_sems.at[2, k], dev))
        for cp in second:
            cp.start()
        for k in range(1, N_DEV):
            dev, pid = _peer(k, x, y, c)
            _remote(red_ref.at[pid], red_ref.at[pid], send_sems.at[2, k], recv_sems.at[2, k], dev).wait_recv()
        for cp in first + second:
            cp.wait_send()

    return pl.pallas_call(
        body,
        name="small_exchange",
        in_specs=[VMEM, VMEM],
        out_specs=[VMEM, VMEM],
        out_shape=[jax.ShapeDtypeStruct(dmx.shape, F32), jax.ShapeDtypeStruct(flat.shape, F32)],
        scratch_shapes=[
            pltpu.VMEM(flat.shape, F32),
            pltpu.SemaphoreType.DMA((3, N_DEV)),
            pltpu.SemaphoreType.DMA((3, N_DEV)),
        ],
        compiler_params=pltpu.CompilerParams(vmem_limit_bytes=VMEM_LIMIT_BYTES),
    )(dmx, flat)


HBM = pl.BlockSpec(memory_space=pltpu.HBM)
SEM = pl.BlockSpec(memory_space=pltpu.SEMAPHORE)
EFFECT = pltpu.SideEffectType.DATAFLOW_SIDE_EFFECTING


def shards_to_columns(x, *, name):
    p, k, n = x.shape

    def body(x_ref, o_ref, sems):
        copies = [pltpu.make_async_copy(x_ref.at[s], o_ref.at[:, pl.ds(s * n, n)], sems.at[s]) for s in range(p)]
        for cp in copies:
            cp.start()
        for cp in copies:
            cp.wait()

    return pl.pallas_call(
        body,
        name=name,
        in_specs=[ANY],
        out_specs=ANY,
        out_shape=jax.ShapeDtypeStruct((k, p * n), x.dtype),
        scratch_shapes=[pltpu.SemaphoreType.DMA((p,))],
    )(x)


def columns_to_shards(xs, *, name):
    k = xs[0].shape[0]
    widths = [x.shape[1] for x in xs]
    n = sum(widths) // N_DEV
    pieces = []
    for s in range(N_DEV):
        start = 0
        for i, w in enumerate(widths):
            lo, hi = max(start, s * n), min(start + w, (s + 1) * n)
            if lo < hi:
                pieces.append((i, lo - start, s, lo - s * n, hi - lo))
            start += w

    def body(*refs):
        x_refs, o_ref, sems = refs[: len(xs)], refs[len(xs)], refs[-1]
        copies = [
            pltpu.make_async_copy(x_refs[i].at[:, pl.ds(c0, w)], o_ref.at[s, :, pl.ds(d0, w)], sems.at[j])
            for j, (i, c0, s, d0, w) in enumerate(pieces)
        ]
        for cp in copies:
            cp.start()
        for cp in copies:
            cp.wait()

    return pl.pallas_call(
        body,
        name=name,
        in_specs=[ANY] * len(xs),
        out_specs=ANY,
        out_shape=jax.ShapeDtypeStruct((N_DEV, k, n), xs[0].dtype),
        scratch_shapes=[pltpu.SemaphoreType.DMA((len(pieces),))],
    )(*xs)


def _own_slot(me, block):
    land = lax.empty((N_DEV, *block.shape), block.dtype)
    return lax.dynamic_update_slice_in_dim(land, block[None], me, axis=0)


N_CHIP_PEERS = 3


class Gather:
    def __init__(self, shards, lands, after, *, name):
        nt = len(shards)
        self.name = name

        def body(*refs):
            src_refs, land_refs = refs[:nt], refs[nt : 2 * nt]
            send_sems, recv_sems = refs[2 * nt + 1 : 3 * nt + 1], refs[3 * nt + 1 : 4 * nt + 1]
            token = refs[-1]
            x, y, c, me = _position()
            for t in range(nt):
                for k, dev in enumerate(self._targets(x, y, c)):
                    _remote(src_refs[t], land_refs[t].at[me], send_sems[t].at[k], recv_sems[t].at[k], dev).start()
            token[...] = jnp.zeros_like(token)

        outs = pl.pallas_call(
            body,
            name=name + "_start",
            in_specs=[HBM] * (2 * nt) + [ANY],
            out_specs=[SEM] * (2 * nt) + [HBM] * (2 * nt) + [VMEM],
            out_shape=[pltpu.SemaphoreType.DMA((1 + N_CHIP_PEERS,))] * (2 * nt)
            + [pltpu.HBM(a.shape, a.dtype) for a in (*shards, *lands)]
            + [jax.ShapeDtypeStruct((SUBLANES, LANES), F32)],
            input_output_aliases={i: 2 * nt + i for i in range(2 * nt)},
            compiler_params=pltpu.CompilerParams(has_side_effects=EFFECT),
        )(*[pltpu.with_memory_space_constraint(a, pltpu.HBM) for a in (*shards, *lands)], after)
        self.send_sems, self.recv_sems = list(outs[:nt]), list(outs[nt : 2 * nt])
        self.srcs, self.lands = list(outs[2 * nt : 3 * nt]), list(outs[3 * nt : 4 * nt])
        self.token = outs[-1]

    @staticmethod
    def _chips(x, y):
        return [(1 - x, y), (x, 1 - y), (1 - x, 1 - y)]

    @classmethod
    def _targets(cls, x, y, c):
        return [(x, y, 1 - c)] + [(*chip, c) for chip in cls._chips(x, y)]

    def zero(self):
        return self.token[0, 0]

    def wait(self, which, after, *, name):
        n = len(which)

        def slot(px, py, pc):
            return 4 * px + 2 * py + pc

        def pass_body(*refs):
            land_refs, recv_sems = refs[:n], refs[n : 2 * n]
            fwd_send, fwd_recv = refs[3 * n + 1 : 4 * n + 1], refs[4 * n + 1 : 5 * n + 1]
            x, y, c, _ = _position()
            for t in range(n):
                for j, chip in enumerate(self._chips(x, y)):
                    blk = land_refs[t].at[slot(*chip, c)]
                    _remote(blk, blk, fwd_send[t].at[j], recv_sems[t].at[1 + j], (*chip, c)).wait_recv()
                    _remote(blk, blk, fwd_send[t].at[j], fwd_recv[t].at[j], (x, y, 1 - c)).start()

        lands = [self.lands[t] for t in which]
        outs = pl.pallas_call(
            pass_body,
            name=name + "_pass",
            in_specs=[HBM] * n + [SEM] * n + [ANY],
            out_specs=[HBM] * n + [SEM] * (2 * n),
            out_shape=[pltpu.HBM(a.shape, a.dtype) for a in lands] + [pltpu.SemaphoreType.DMA((N_CHIP_PEERS,))] * (2 * n),
            input_output_aliases={i: i for i in range(n)},
            compiler_params=pltpu.CompilerParams(has_side_effects=EFFECT),
        )(*lands, *[self.recv_sems[t] for t in which], after)
        lands, fwd_send, fwd_recv = outs[:n], outs[n : 2 * n], outs[2 * n :]

        def wait_body(*refs):
            src_refs, land_refs = refs[:n], refs[n : 2 * n]
            send_sems, recv_sems = refs[2 * n : 3 * n], refs[3 * n : 4 * n]
            fwd_send, fwd_recv = refs[4 * n : 5 * n], refs[5 * n : 6 * n]
            x, y, c, me = _position()
            sibling = (x, y, 1 - c)
            for t in range(n):
                for k, dev in enumerate(self._targets(x, y, c)):
                    _remote(src_refs[t], land_refs[t].at[me], send_sems[t].at[k], recv_sems[t].at[k], dev).wait_send()
                blk = land_refs[t].at[slot(x, y, 1 - c)]
                _remote(blk, blk, send_sems[t].at[0], recv_sems[t].at[0], sibling).wait_recv()
                for j, chip in enumerate(self._chips(x, y)):
                    sent = land_refs[t].at[slot(*chip, c)]
                    _remote(sent, sent, fwd_send[t].at[j], fwd_recv[t].at[j], sibling).wait_send()
                    got = land_refs[t].at[slot(*chip, 1 - c)]
                    _remote(got, got, fwd_send[t].at[j], fwd_recv[t].at[j], sibling).wait_recv()

        srcs = [self.srcs[t] for t in which]
        outs = pl.pallas_call(
            wait_body,
            name=name,
            in_specs=[HBM] * (2 * n) + [SEM] * (4 * n),
            out_specs=[HBM] * (2 * n),
            out_shape=[pltpu.HBM(a.shape, a.dtype) for a in (*srcs, *lands)],
            input_output_aliases={i: i for i in range(2 * n)},
            compiler_params=pltpu.CompilerParams(has_side_effects=EFFECT),
        )(*srcs, *lands, *[self.send_sems[t] for t in which], *[self.recv_sems[t] for t in which], *fwd_send, *fwd_recv)
        return outs[n:]


class Scatter:
    def __init__(self, srcs, lands, after, *, name):
        self.name = name
        nt = self.nt = len(srcs)
        peers = N_DEV - 1

        def body(*refs):
            src_refs, land_refs = refs[:nt], refs[nt : 2 * nt]
            send_sems, recv_sems = refs[2 * nt + 1 : 3 * nt + 1], refs[3 * nt + 1 : 4 * nt + 1]
            token = refs[-1]
            x, y, c, me = _position()
            for t in range(nt):
                for k in range(1, N_DEV):
                    dev, pid = _peer(k, x, y, c)
                    src = src_refs[t].at[pid]
                    _remote(src, land_refs[t].at[me], send_sems[t].at[k - 1], recv_sems[t].at[k - 1], dev).start()
            token[...] = jnp.zeros_like(token)

        outs = pl.pallas_call(
            body,
            name=name + "_start",
            in_specs=[HBM] * (2 * nt) + [ANY],
            out_specs=[SEM] * (2 * nt) + [HBM] * (2 * nt) + [VMEM],
            out_shape=[pltpu.SemaphoreType.DMA((peers,))] * (2 * nt)
            + [pltpu.HBM(a.shape, a.dtype) for a in (*srcs, *lands)]
            + [jax.ShapeDtypeStruct((SUBLANES, LANES), F32)],
            input_output_aliases={i: 2 * nt + i for i in range(2 * nt)},
            compiler_params=pltpu.CompilerParams(has_side_effects=EFFECT),
        )(*[pltpu.with_memory_space_constraint(a, pltpu.HBM) for a in (*srcs, *lands)], after)
        self.send_sems, self.recv_sems = outs[:nt], outs[nt : 2 * nt]
        self.srcs, self.lands = outs[2 * nt : 3 * nt], outs[3 * nt : 4 * nt]
        self.token = outs[-1]

    def zero(self):
        return self.token[0, 0]

    def wait(self, which, after, *, name):
        n = len(which)

        def body(*refs):
            src_refs, land_refs = refs[:n], refs[n : 2 * n]
            send_sems, recv_sems = refs[2 * n : 3 * n], refs[3 * n : 4 * n]
            x, y, c, _ = _position()
            for t in range(n):
                for k in range(1, N_DEV):
                    dev, pid = _peer(k, x, y, c)
                    src = src_refs[t].at[pid]
                    cp = _remote(src, land_refs[t].at[pid], send_sems[t].at[k - 1], recv_sems[t].at[k - 1], dev)
                    cp.wait_send()
                    cp.wait_recv()

        srcs = [self.srcs[t] for t in which]
        lands = [self.lands[t] for t in which]
        outs = pl.pallas_call(
            body,
            name=name,
            in_specs=[HBM] * (2 * n) + [SEM] * (2 * n) + [ANY],
            out_specs=[HBM] * (2 * n),
            out_shape=[pltpu.HBM(a.shape, a.dtype) for a in (*srcs, *lands)],
            input_output_aliases={i: i for i in range(2 * n)},
            compiler_params=pltpu.CompilerParams(has_side_effects=EFFECT),
        )(*srcs, *lands, *[self.send_sems[t] for t in which], *[self.recv_sems[t] for t in which], after)
        return outs[n:]


def _adam_update(g, w, m, v):
    m2 = ADAM_B1 * m + (1.0 - ADAM_B1) * g
    v2 = ADAM_B2 * v + (1.0 - ADAM_B2) * jnp.square(g)
    m_hat = m2 / (1.0 - ADAM_B1**ADAM_STEP)
    v_hat = v2 / (1.0 - ADAM_B2**ADAM_STEP)
    delta = -ADAM_LR * (m_hat / (jnp.sqrt(v_hat) + ADAM_EPS) + ADAM_WD * w)
    return delta, m2, v2


def adamw(gparts, w, m, v, *, name):
    nl, r, c = w.shape
    p = gparts[0].shape[0]
    tr = r if r <= 256 else (256 if c <= D_MODEL else 128)
    ni = r // tr

    def body(*refs):
        g_refs = refs[:nl]
        w_ref, m_ref, v_ref, go_ref, d_ref, mo_ref, vo_ref = refs[nl:]
        for layer in range(nl):

            @pl.when(pl.program_id(0) == layer)
            def _(g_ref=g_refs[layer]):
                g = g_ref[0].astype(F32)
                for i in range(1, p):
                    g = g + g_ref[i].astype(F32)
                delta, m2, v2 = _adam_update(g, w_ref[...], m_ref[...], v_ref[...])
                go_ref[...] = g
                d_ref[...] = delta
                mo_ref[...] = m2
                vo_ref[...] = v2

    def parts_spec(layer):
        def index(l, i):
            return (0, jnp.where(l == layer, i, jnp.where(l < layer, 0, ni - 1)), 0)

        return pl.BlockSpec((p, tr, c), index)

    blk = pl.BlockSpec((None, tr, c), lambda l, i: (l, i, 0))
    return pl.pallas_call(
        body,
        name=name,
        grid=(nl, ni),
        in_specs=[*[parts_spec(layer) for layer in range(nl)], blk, blk, blk],
        out_specs=[blk] * 4,
        out_shape=[jax.ShapeDtypeStruct((nl, r, c), F32)] * 4,
        compiler_params=_params(("arbitrary", "arbitrary")),
    )(*gparts, w, m, v)


def ada_grad_adamw(cg, dmrecv, w4, m4, v4, *, name):
    nt, k, ncol = w4.shape

    def body(cg_ref, dm_ref, w_ref, m_ref, v_ref, go_ref, d_ref, mo_ref, vo_ref, gb_ref):
        sc = _silu_bf16(cg_ref[...].reshape(N_DEV * SUBLANES, k))
        dm = dm_ref[...].reshape(N_DEV * SUBLANES, ncol)
        g = lax.dot_general(sc, dm.astype(BF16), (((0,), (0,)), ((), ())), preferred_element_type=F32)
        delta, m2, v2 = _adam_update(g, w_ref[...], m_ref[...], v_ref[...])
        go_ref[...] = g
        d_ref[...] = delta
        mo_ref[...] = m2
        vo_ref[...] = v2
        gb_ref[...] = jnp.broadcast_to(jnp.sum(dm, axis=0, keepdims=True), (SUBLANES, ncol))

    wblk = pl.BlockSpec((None, k, ncol), lambda t: (t, 0, 0))
    return pl.pallas_call(
        body,
        name=name,
        grid=(nt,),
        in_specs=[
            pl.BlockSpec((N_DEV, SUBLANES, k), lambda t: (0, 0, 0)),
            pl.BlockSpec((N_DEV, None, SUBLANES, ncol), lambda t: (0, t, 0, 0)),
            wblk,
            wblk,
            wblk,
        ],
        out_specs=[wblk] * 4 + [pl.BlockSpec((None, SUBLANES, ncol), lambda t: (t, 0, 0))],
        out_shape=[jax.ShapeDtypeStruct((nt, k, ncol), F32)] * 4 + [jax.ShapeDtypeStruct((nt, SUBLANES, ncol), F32)],
        compiler_params=_params(("parallel",)),
    )(cg, dmrecv, w4, m4, v4)


def kernel(x, c, ada_w, ada_b, ln_g, ln_b, a_w_in, a_b_in, a_vn_g, a_vn_b, a_w_s, a_b_s, a_w_out, b_w_qkv, b_w_out, mlp_w_up, mlp_w_down, loss_target, m_ada_w, m_ada_b, m_ln_g, m_ln_b, m_a_w_in, m_a_b_in, m_a_vn_g, m_a_vn_b, m_a_w_s, m_a_b_s, m_a_w_out, m_b_w_qkv, m_b_w_out, m_mlp_w_up, m_mlp_w_down, v_ada_w, v_ada_b, v_ln_g, v_ln_b, v_a_w_in, v_a_b_in, v_a_vn_g, v_a_vn_b, v_a_w_s, v_a_b_s, v_a_w_out, v_b_w_qkv, v_b_w_out, v_mlp_w_up, v_mlp_w_down):
    x0 = x[0]
    target = loss_target[0]
    me = 4 * lax.axis_index("x") + 2 * lax.axis_index("y") + lax.axis_index("c")

    ada_w4 = ada_w.reshape(N_SUB, D_MODEL, -1)
    ada_b4 = ada_b.reshape(N_SUB, -1)
    ln8 = jnp.concatenate([ln_g.reshape(N_SUB, -1), ln_b.reshape(N_SUB, -1)], axis=0)
    c8 = jnp.broadcast_to(c, (SUBLANES, D_MODEL))
    cg, lng, mrecv = ada_exchange(c8, ada_w4, ada_b4, ln8)

    W_IN, W_AOUT, W_UP0, W_DN0, W_QKV, W_BOUT, W_UP1, W_DN1 = range(8)
    shards = [
        a_w_in[0].astype(BF16),
        a_w_out[0].astype(BF16),
        mlp_w_up[0].astype(BF16),
        mlp_w_down[0].astype(BF16),
        b_w_qkv[0].astype(BF16),
        b_w_out[0].astype(BF16),
        mlp_w_up[1].astype(BF16),
        mlp_w_down[1].astype(BF16),
    ]
    gather = Gather(shards, [_own_slot(me, s) for s in shards], mrecv, name="gather")

    modv = mrecv[:, :, 0, :].transpose(1, 0, 2).reshape(N_SUB, 3 * D_MODEL) + gather.zero()
    shift = [modv[t : t + 1, :D_MODEL] for t in range(N_SUB)]
    scale = [modv[t : t + 1, D_MODEL : 2 * D_MODEL] for t in range(N_SUB)]
    gate1 = [1.0 + modv[t : t + 1, 2 * D_MODEL :] for t in range(N_SUB)]
    lng_full = [lng[:, t, :].reshape(1, D_MODEL) for t in range(N_SUB)]
    lnb_full = [lng[:, N_SUB + t, :].reshape(1, D_MODEL) for t in range(N_SUB)]

    ident = lambda acc: (acc,)
    relu2 = lambda a: jnp.square(jnp.maximum(a.astype(F32), 0.0)).astype(BF16)
    vn_g, vn_b, w_s = a_vn_g, a_vn_b, a_w_s[0]
    bias_full = jnp.repeat(a_b_s[0].T, A_GROUP_DIM, axis=1)
    w_up3, w_dn3 = [None, None], [None, None]

    def mlp_forward(i, h, after):
        up, dn = gather.wait([W_UP0, W_DN0] if i == 0 else [W_UP1, W_DN1], after, name=f"gather_wait_mlp{i}")
        w_up3[i], w_dn3[i] = up, dn.reshape(1, D_FF, D_MODEL)
        (a,) = mm_nn(h, w_up3[i], name=f"mlp{i}_up", tm=2048, ps=2, tn=512, tk=D_MODEL, epilogue=ident, outs=(BF16,))
        (y,) = mm_nn(
            a, w_dn3[i], name=f"mlp{i}_down", tm=2048, ps=1, tn=D_MODEL, tk=1024, prologue=relu2, epilogue=ident, outs=(BF16,)
        )
        return a, y

    h0 = modulate(x0, scale[0], shift[0], name="modulate0")
    w_in3, w_aout3 = gather.wait([W_IN, W_AOUT], h0, name="gather_wait_a")
    w_aout3 = w_aout3.reshape(1, D_MODEL, D_MODEL)
    (a_pre,) = mm_nn(
        h0, w_in3, name="a_in", tm=2048, ps=4, tn=256, tk=D_MODEL, epilogue=lambda acc, b: (acc + b,),
        extras=[(a_b_in, "row")], outs=(BF16,),
    )
    p_gate = gate_fwd(a_pre, vn_g, vn_b, w_s, bias_full, name="gate_fwd")
    (y0,) = mm_nn(p_gate, w_aout3, name="a_out", tm=2048, ps=1, tn=D_MODEL, tk=D_MODEL, epilogue=ident, outs=(BF16,))
    x1, h1 = residual_ln(x0, y0, gate1[0], lng_full[0], lnb_full[0], scale[1], shift[1], name="res_ln0")
    a1, y1 = mlp_forward(0, h1, y0)
    x2, h2 = residual_ln(x1, y1, gate1[1], lng_full[1], lnb_full[1], scale[2], shift[2], name="res_ln1")
    w_qkv_shards, w_bout3 = gather.wait([W_QKV, W_BOUT], y1, name="gather_wait_b")
    w_bout3 = w_bout3.reshape(1, D_MODEL, D_MODEL)
    w_qkv3 = shards_to_columns(w_qkv_shards, name="w_qkv_columns")[None]
    pat_tiles = 3
    dil = [d for _, d in B_PATTERNS]
    h2_p = [h2] + [permute_rows(h2, dil[g], inverse=False, name=f"perm_h{g}") for g in range(1, N_PAT)]
    qkv_p, pat_o, pat_lse = [], [], []
    for g in range(N_PAT):
        (qkv_g,) = mm_nn(
            h2_p[g], w_qkv3, name=f"b_qkv{g}", tm=2048, ps=1, tn=D_MODEL, tk=D_MODEL, epilogue=ident, outs=(BF16,),
            b_tile0=pat_tiles * g, b_tiles=pat_tiles,
        )
        o_g, lse_g = attn_fwd(qkv_g, g, name=f"attn_fwd{g}")
        if g > 0:
            o_g = permute_rows(o_g, dil[g], inverse=True, name=f"unperm_o{g}")
            lse_g = permute_rows(lse_g, dil[g], inverse=True, name=f"unperm_lse{g}")
        qkv_p.append(qkv_g)
        pat_o.append(o_g)
        pat_lse.append(lse_g)
    o_b, o_f, lse = attn_combine(pat_o, pat_lse, name="attn_combine")
    (y2,) = mm_nn(o_b, w_bout3, name="b_out", tm=2048, ps=1, tn=D_MODEL, tk=D_MODEL, epilogue=ident, outs=(BF16,))
    x3, h3 = residual_ln(x2, y2, gate1[2], lng_full[2], lnb_full[2], scale[3], shift[3], name="res_ln2")
    a3, y3 = mlp_forward(1, h3, y2)
    dxo, loss_local = residual_ln_loss(x3, y3, gate1[3], lng_full[3], lnb_full[3], target, name="res_ln3_loss")

    def scatter(parts, after, name):
        parts = [p.reshape(N_DEV, -1, p.shape[-1]) for p in parts]
        lands = [_own_slot(me, lax.dynamic_index_in_dim(p, me, 0, keepdims=False)) for p in parts]
        return Scatter(parts, lands, after, name=name)

    def mlp_backward(i, h, a, dy):
        (da,) = mm_nt(
            dy,
            w_dn3[i],
            name=f"mlp{i}_da",
            tm=2048,
            tko=1024,
            ps=1,
            tc=D_MODEL,
            epilogue=lambda acc, act: (acc * (2.0 * jnp.maximum(act.astype(F32), 0.0)),),
            extras=[(a, "full")],
            outs=(BF16,),
        )
        (dh,) = mm_nt(da, w_up3[i], name=f"mlp{i}_dh", tm=2048, tko=1024, ps=2, tc=512, epilogue=ident, outs=(F32,))
        dw_dn = mm_tn(
            a, dy, name=f"mlp{i}_dw_down", p=1, tk=1024, ps=1, tn=D_MODEL, tmc=2048, prologue=relu2, out_dtype=BF16
        )
        dw_up = mm_tn(h, da, name=f"mlp{i}_dw_up", p=N_DEV, tk=1024, ps=2, tn=512, tmc=2048, out_dtype=BF16)
        return scatter([dw_up, dw_dn], dh, f"scatter_mlp{i}"), dh

    dz3, dy3, st3 = residual_ln_bwd(name="res_bwd3", dxo=dxo, this=(x3, y3, gate1[3], lng_full[3], lnb_full[3]))
    rs_mlp1, dh3 = mlp_backward(1, h3, a3, dy3)
    dz2, dy2, st2 = residual_ln_bwd(
        name="res_bwd2", later=(dh3, dz3, scale[3] + rs_mlp1.zero()), this=(x2, y2, gate1[2], lng_full[2], lnb_full[2])
    )
    (d_o,) = mm_nt(dy2, w_bout3, name="b_do", tm=2048, tko=1024, ps=1, tc=D_MODEL, epilogue=ident, outs=(F32,))
    do_b, delta = attn_delta(d_o, o_f, name="attn_delta")
    dh2, dw_pat = None, []
    for g in range(N_PAT):
        do_g, lse_g, delta_g = do_b, lse, delta
        if g > 0:
            do_g = permute_rows(do_b, dil[g], inverse=False, name=f"perm_do{g}")
            lse_g = permute_rows(lse, dil[g], inverse=False, name=f"perm_lse{g}")
            delta_g = permute_rows(delta, dil[g], inverse=False, name=f"perm_delta{g}")
        dqkv_g = attn_bwd(qkv_p[g], do_g, lse_g, delta_g, g, name=f"attn_bwd{g}")
        (dh_g,) = mm_nt(
            dqkv_g, w_qkv3, name=f"b_dh{g}", tm=2048, tko=1024, ps=1, tc=D_MODEL, epilogue=ident, outs=(F32,),
            b_tile0=pat_tiles * g,
        )
        dh2 = dh_g if g == 0 else permute_rows(dh_g, dil[g], inverse=True, add=dh2, name=f"unperm_dh{g}")
        dw_pat.append(
            mm_tn(h2_p[g], dqkv_g, name=f"b_dw_qkv{g}", p=1, tk=1024, ps=1, tn=D_MODEL, tmc=2048, out_dtype=BF16)[0]
        )
    dw_bout = mm_tn(o_b, dy2, name="b_dw_out", p=1, tk=1024, ps=1, tn=D_MODEL, tmc=2048, out_dtype=BF16)
    dw_qkv = columns_to_shards(dw_pat, name="dw_qkv_shards")
    rs_b = scatter([dw_qkv, dw_bout], dh2, "scatter_b")
    dz1, dy1, st1 = residual_ln_bwd(
        name="res_bwd1", later=(dh2, dz2, scale[2] + rs_b.zero()), this=(x1, y1, gate1[1], lng_full[1], lnb_full[1])
    )
    rs_mlp0, dh1 = mlp_backward(0, h1, a1, dy1)
    dz0, dy0, st0 = residual_ln_bwd(
        name="res_bwd0", later=(dh1, dz1, scale[1] + rs_mlp0.zero()), this=(x0, y0, gate1[0], lng_full[0], lnb_full[0])
    )
    (dp_gate,) = mm_nt(dy0, w_aout3, name="a_dp", tm=2048, tko=1024, ps=1, tc=D_MODEL, epilogue=ident, outs=(F32,))
    dw_aout = mm_tn(p_gate, dy0, name="a_dw_out", p=1, tk=1024, ps=1, tn=D_MODEL, tmc=2048, out_dtype=BF16)
    rs_aout = scatter([dw_aout], dp_gate, "scatter_a_out")
    da0, d_ws, d_bs, gate_rows = gate_bwd(a_pre, dp_gate, vn_g + rs_aout.zero(), vn_b, w_s, bias_full, name="gate_bwd")
    dw_in = mm_tn(h0, da0, name="a_dw_in", p=N_DEV, tk=1024, ps=4, tn=256, tmc=2048, out_dtype=BF16)
    rs_in = scatter([dw_in], d_ws, "scatter_a_in")
    (dh0,) = mm_nt(da0, w_in3, name="a_dh", tm=2048, tko=1024, ps=4, tc=256, epilogue=ident, outs=(F32,))
    grad_x, stf = residual_ln_bwd(name="res_bwd_in", later=(dh0, dz0, scale[0] + rs_in.zero()), x_out=x0)

    stats_after = [stf, st0, st1, st2]
    stats_own = [st0, st1, st2, st3]
    dm = jnp.stack(
        [
            jnp.concatenate(
                [stats_after[t][ST_DSHIFT], stats_after[t][ST_DSCALE], stats_own[t][ST_DGATE]], axis=0
            )
            for t in range(N_SUB)
        ]
    )
    ncol = 3 * D_MODEL // N_DEV
    dmx = jnp.pad(
        dm.reshape(N_SUB, N_DEV, ncol).transpose(1, 0, 2)[:, :, None, :], ((0, 0), (0, 0), (0, SUBLANES - 1), (0, 0))
    )
    small = [
        gate_rows[0],
        gate_rows[1],
        d_ws.reshape(-1),
        d_bs[:, :A_GROUPS].T.reshape(-1),
        *[stats_own[t][ST_DG] for t in range(N_SUB)],
        *[stats_own[t][ST_DB] for t in range(N_SUB)],
        jnp.pad(loss_local.reshape(1), (0, LANES - 1)),
    ]
    n_small = sum(s.size for s in small)
    part_rows = -(-n_small // (N_DEV * LANES * SUBLANES)) * SUBLANES
    flat = jnp.concatenate(small + [jnp.zeros((N_DEV * part_rows * LANES - n_small,), F32)])
    dmrecv, reduced = small_exchange(dmx, flat.reshape(N_DEV, part_rows, LANES))
    reduced = reduced.reshape(-1)
    sizes = [2 * D_MODEL, D_MODEL, D_MODEL, A_GROUPS * CHUNK * CHUNK, A_GROUPS * CHUNK, N_SUB * D_MODEL, N_SUB * D_MODEL]
    offs = [sum(sizes[:i]) for i in range(len(sizes) + 1)]
    g_b_in, g_vn_g, g_vn_b, g_ws, g_bs, g_lng, g_lnb = [reduced[offs[i] : offs[i + 1]] for i in range(len(sizes))]
    loss = reduced[offs[-1]]

    results = {}

    def update(wname, gparts, w, m, v):
        shape = w.shape
        layers = len(gparts) if isinstance(gparts, list) else 1
        w3 = w.reshape(layers, -1, shape[-1])
        parts = [g.reshape(g.shape[0], *w3.shape[1:]) for g in (gparts if layers > 1 else [gparts])]
        outs = adamw(parts, w3, m.reshape(w3.shape), v.reshape(w3.shape), name=f"adamw_{wname}")
        results[wname] = [o.reshape(shape) for o in outs]

    ada_outs = ada_grad_adamw(cg, dmrecv, ada_w4, m_ada_w.reshape(ada_w4.shape), v_ada_w.reshape(ada_w4.shape), name="ada_grad_adamw")
    results["ada_w"] = [o.reshape(ada_w.shape) for o in ada_outs[:4]]
    update("ada_b", ada_outs[4][:, 0, :][None], ada_b, m_ada_b, v_ada_b)
    ln_cols = D_MODEL // N_DEV
    my_ln = lambda gfull: lax.dynamic_slice_in_dim(gfull.reshape(N_SUB, N_DEV, ln_cols), me, 1, axis=1)
    update("ln_g", my_ln(g_lng).reshape(1, N_SUB, ln_cols), ln_g, m_ln_g, v_ln_g)
    update("ln_b", my_ln(g_lnb).reshape(1, N_SUB, ln_cols), ln_b, m_ln_b, v_ln_b)
    update("a_b_in", g_b_in[None], a_b_in, m_a_b_in, v_a_b_in)
    update("a_vn_g", g_vn_g[None], a_vn_g, m_a_vn_g, v_a_vn_g)
    update("a_vn_b", g_vn_b[None], a_vn_b, m_a_vn_b, v_a_vn_b)
    update("a_w_s", g_ws[None], a_w_s, m_a_w_s, v_a_w_s)
    update("a_b_s", g_bs[None], a_b_s, m_a_b_s, v_a_b_s)
    g_up1, g_dn1 = rs_mlp1.wait([0, 1], grad_x, name="scatter_wait_mlp1")
    g_qkv, g_bout = rs_b.wait([0, 1], grad_x, name="scatter_wait_b")
    update("b_w_qkv", g_qkv, b_w_qkv, m_b_w_qkv, v_b_w_qkv)
    update("b_w_out", g_bout, b_w_out, m_b_w_out, v_b_w_out)
    g_up0, g_dn0 = rs_mlp0.wait([0, 1], grad_x, name="scatter_wait_mlp0")
    update("mlp_w_up", [g_up0, g_up1], mlp_w_up, m_mlp_w_up, v_mlp_w_up)
    update("mlp_w_down", [g_dn0, g_dn1], mlp_w_down, m_mlp_w_down, v_mlp_w_down)
    (g_aout,) = rs_aout.wait([0], grad_x, name="scatter_wait_a_out")
    (g_in,) = rs_in.wait([0], grad_x, name="scatter_wait_a_in")
    update("a_w_in", g_in, a_w_in, m_a_w_in, v_a_w_in)
    update("a_w_out", g_aout, a_w_out, m_a_w_out, v_a_w_out)

    order = ["ada_w", "ada_b", "ln_g", "ln_b", "a_w_in", "a_b_in", "a_vn_g", "a_vn_b", "a_w_s", "a_b_s", "a_w_out", "b_w_qkv", "b_w_out", "mlp_w_up", "mlp_w_down"]
    return (loss, grad_x[None], *[results[n][0] for n in order], *[results[n][1] for n in order],
            *[results[n][2] for n in order], *[results[n][3] for n in order])
```

```python
import math

import jax
import jax.numpy as jnp
from jax import lax
from jax.experimental import pallas as pl
from jax.experimental.pallas import tpu as pltpu

F32 = jnp.float32
BF16 = jnp.bfloat16
MESH = pl.DeviceIdType.MESH
ANY = pl.BlockSpec(memory_space=pl.ANY)
VMEM = pl.BlockSpec(memory_space=pltpu.VMEM)

N_DEV = 8
D_MODEL = 1024
SEQ = 4096
DEPTH = 2
CHUNK = 128
A_GROUPS = 16
A_GROUP_DIM = D_MODEL // A_GROUPS
B_HEADS = 16
B_HEAD_DIM = 64
B_PATTERNS = ((128, 1), (512, 4), (2048, 16))
N_PAT = len(B_PATTERNS)
SPAN = 128
D_FF = 4 * D_MODEL
D_QKV = N_PAT * 3 * D_MODEL
ALPHA = (2 * DEPTH) ** 0.25
LN_EPS = 1e-5
NEG = -1e30
ADAM_LR = 0.001
ADAM_B1 = 0.9
ADAM_B2 = 0.999
ADAM_EPS = 1e-08
ADAM_WD = 0.01
ADAM_STEP = 10
GELU_C = math.sqrt(2.0 / math.pi)
GELU_A = 0.044715

VMEM_LIMIT_BYTES = 56 * 1024 * 1024
LANES = 128
SUBLANES = 8
ROW_TILE = 512
N_SUB = 2 * DEPTH


def _params(sem):
    return pltpu.CompilerParams(dimension_semantics=sem, vmem_limit_bytes=VMEM_LIMIT_BYTES)


def _lane(shape):
    return lax.broadcasted_iota(jnp.int32, shape, len(shape) - 1)


def _split_bf16(x):
    hi = x.astype(BF16)
    lo = (x - hi.astype(F32)).astype(BF16)
    return hi, lo


def _group_expand_matrix(groups_padded, width):
    per = width // A_GROUPS
    r = lax.broadcasted_iota(jnp.int32, (groups_padded, width), 0)
    c = lax.broadcasted_iota(jnp.int32, (groups_padded, width), 1)
    return (c // per == r).astype(BF16)


def _group_reduce_matrix(width, groups_padded):
    per = width // A_GROUPS
    r = lax.broadcasted_iota(jnp.int32, (width, groups_padded), 0)
    c = lax.broadcasted_iota(jnp.int32, (width, groups_padded), 1)
    return (r // per == c).astype(BF16)


def _expand_groups(w):
    e = _group_expand_matrix(LANES, D_MODEL)
    hi, lo = _split_bf16(w)
    return jnp.dot(hi, e, preferred_element_type=F32) + jnp.dot(lo, e, preferred_element_type=F32)


def _reduce_groups(x):
    e = _group_reduce_matrix(D_MODEL, LANES)
    hi, lo = _split_bf16(x)
    return jnp.dot(hi, e, preferred_element_type=F32) + jnp.dot(lo, e, preferred_element_type=F32)


def _column_tiles(p, n, ps, tn):
    assert (ps == 1 or tn == n) and p % ps == 0 and n % tn == 0
    q = n // tn
    return (p // ps) * q, q


def _extra_specs(extras, tm, width):
    specs = []
    for _, kind in extras:
        if kind == "row":
            specs.append(pl.BlockSpec((1, width), lambda i, j, c: (0, j)))
        else:
            specs.append(pl.BlockSpec((tm, width), lambda i, j, c: (i, j)))
    return specs


def mm_nn(a, b3, *, name, tm, ps, tn, tk, epilogue, extras=(), outs, prologue=None, b_tile0=0, b_tiles=None):
    m, k = a.shape
    p, _, n = b3.shape
    nj, q = _column_tiles(p, n, ps, tn)
    nj = nj if b_tiles is None else b_tiles
    nk = k // tk
    width = ps * tn

    def body(a_ref, b_ref, *rest):
        ex = rest[: len(extras)]
        out_refs = rest[len(extras) : len(extras) + len(outs)]
        kk = pl.program_id(2)
        av = a_ref[...] if prologue is None else prologue(a_ref[...])

        def finish(cs, acc):
            res = epilogue(acc, *[e[:, cs] for e in ex])
            for o_ref, r in zip(out_refs, res, strict=True):
                o_ref[:, cs] = r.astype(o_ref.dtype)

        for s in range(ps):
            cs = slice(s * tn, (s + 1) * tn)
            part = jnp.dot(av, b_ref[s], preferred_element_type=F32)
            if nk == 1:
                finish(cs, part)
                continue
            acc_ref = rest[-1]

            @pl.when(kk == 0)
            def _(part=part, cs=cs):
                acc_ref[:, cs] = part

            @pl.when(kk > 0)
            def _(part=part, cs=cs):
                acc_ref[:, cs] += part

        if nk > 1:

            @pl.when(kk == nk - 1)
            def _():
                for s in range(ps):
                    cs = slice(s * tn, (s + 1) * tn)
                    finish(cs, rest[-1][:, cs])

    return pl.pallas_call(
        body,
        name=name,
        grid=(m // tm, nj, nk),
        in_specs=[
            pl.BlockSpec((tm, tk), lambda i, j, kk: (i, kk)),
            pl.BlockSpec((ps, tk, tn), lambda i, j, kk: ((j + b_tile0) // q, kk, (j + b_tile0) % q)),
            *_extra_specs(extras, tm, width),
        ],
        out_specs=[pl.BlockSpec((tm, width), lambda i, j, kk: (i, j)) for _ in outs],
        out_shape=[jax.ShapeDtypeStruct((m, nj * width), dt) for dt in outs],
        scratch_shapes=[pltpu.VMEM((tm, width), F32)] if nk > 1 else [],
        compiler_params=_params(("parallel", "parallel", "arbitrary")),
    )(a, b3, *[arr for arr, _ in extras])


def mm_nt(g, b3, *, name, tm, tko, ps, tc, epilogue, extras=(), outs, b_tile0=0):
    m = g.shape[0]
    p, k, n = b3.shape
    _, q = _column_tiles(p, n, ps, tc)
    nc = g.shape[1] // (ps * tc)

    def body(g_ref, b_ref, *rest):
        ex = rest[: len(extras)]
        out_refs = rest[len(extras) : len(extras) + len(outs)]
        c = pl.program_id(2)
        part = None
        for s in range(ps):
            d = lax.dot_general(
                g_ref[:, s * tc : (s + 1) * tc], b_ref[s], (((1,), (1,)), ((), ())), preferred_element_type=F32
            )
            part = d if part is None else part + d

        def finish(acc):
            res = epilogue(acc, *[e[...] for e in ex])
            for o_ref, r in zip(out_refs, res, strict=True):
                o_ref[...] = r.astype(o_ref.dtype)

        if nc == 1:
            finish(part)
            return
        acc_ref = rest[-1]

        @pl.when(c == 0)
        def _():
            acc_ref[...] = part

        @pl.when(c > 0)
        def _():
            acc_ref[...] += part

        @pl.when(c == nc - 1)
        def _():
            finish(acc_ref[...])

    return pl.pallas_call(
        body,
        name=name,
        grid=(m // tm, k // tko, nc),
        in_specs=[
            pl.BlockSpec((tm, ps * tc), lambda i, j, c: (i, c)),
            pl.BlockSpec((ps, tko, tc), lambda i, j, c: ((c + b_tile0) // q, j, (c + b_tile0) % q)),
            *_extra_specs(extras, tm, tko),
        ],
        out_specs=[pl.BlockSpec((tm, tko), lambda i, j, c: (i, j)) for _ in outs],
        out_shape=[jax.ShapeDtypeStruct((m, k), dt) for dt in outs],
        scratch_shapes=[pltpu.VMEM((tm, tko), F32)] if nc > 1 else [],
        compiler_params=_params(("parallel", "parallel", "arbitrary")),
    )(g, b3, *[arr for arr, _ in extras])


def mm_tn(a, g, *, name, p, tk, ps, tn, tmc, out_dtype, prologue=None):
    m, k = a.shape
    n = g.shape[1] // p
    nj, q = _column_tiles(p, n, ps, tn)
    nc = m // tmc

    def body(a_ref, g_ref, o_ref, acc_ref):
        c = pl.program_id(2)
        av = a_ref[...] if prologue is None else prologue(a_ref[...])
        part = lax.dot_general(av, g_ref[...], (((0,), (0,)), ((), ())), preferred_element_type=F32)

        @pl.when(c == 0)
        def _():
            acc_ref[...] = part

        @pl.when(c > 0)
        def _():
            acc_ref[...] += part

        @pl.when(c == nc - 1)
        def _():
            for s in range(ps):
                o_ref[s] = acc_ref[:, s * tn : (s + 1) * tn].astype(o_ref.dtype)

    return pl.pallas_call(
        body,
        name=name,
        grid=(k // tk, nj, nc),
        in_specs=[
            pl.BlockSpec((tmc, tk), lambda i, j, c: (c, i)),
            pl.BlockSpec((tmc, ps * tn), lambda i, j, c: (c, j)),
        ],
        out_specs=pl.BlockSpec((ps, tk, tn), lambda i, j, c: (j // q, i, j % q)),
        out_shape=jax.ShapeDtypeStruct((p, k, n), out_dtype),
        scratch_shapes=[pltpu.VMEM((tk, ps * tn), F32)],
        compiler_params=_params(("parallel", "parallel", "arbitrary")),
    )(a, g)


def _rows(cols):
    return pl.BlockSpec((ROW_TILE, cols), lambda i: (i, 0))


def _vec(cols, rows=1):
    return pl.BlockSpec((rows, cols), lambda i: (0, 0))


def _layer_norm_hat(z):
    mu = jnp.mean(z, axis=-1, keepdims=True)
    zc = z - mu
    var = jnp.mean(zc * zc, axis=-1, keepdims=True)
    rstd = lax.rsqrt(var + LN_EPS)
    return zc * rstd, rstd


def modulate(x, scale, shift, *, name):
    s, d = x.shape

    def body(x_ref, sc_ref, sh_ref, h_ref):
        h_ref[...] = (x_ref[...] * (1.0 + sc_ref[...]) + sh_ref[...]).astype(BF16)

    return pl.pallas_call(
        body,
        name=name,
        grid=(s // ROW_TILE,),
        in_specs=[_rows(d), _vec(d), _vec(d)],
        out_specs=_rows(d),
        out_shape=jax.ShapeDtypeStruct((s, d), BF16),
        compiler_params=_params(("parallel",)),
    )(x, scale, shift)


def residual_ln(x, y, gate1, g, b, nscale, nshift, *, name):
    s, d = x.shape

    def body(x_ref, y_ref, gt_ref, g_ref, b_ref, sc_ref, sh_ref, xn_ref, hn_ref):
        z = ALPHA * x_ref[...] + gt_ref[...] * y_ref[...].astype(F32)
        xhat, _ = _layer_norm_hat(z)
        xn = xhat * g_ref[...] + b_ref[...]
        xn_ref[...] = xn
        hn_ref[...] = (xn * (1.0 + sc_ref[...]) + sh_ref[...]).astype(BF16)

    return pl.pallas_call(
        body,
        name=name,
        grid=(s // ROW_TILE,),
        in_specs=[_rows(d), _rows(d), _vec(d), _vec(d), _vec(d), _vec(d), _vec(d)],
        out_specs=[_rows(d), _rows(d)],
        out_shape=[jax.ShapeDtypeStruct((s, d), F32), jax.ShapeDtypeStruct((s, d), BF16)],
        compiler_params=_params(("parallel",)),
    )(x, y, gate1, g, b, nscale, nshift)


def residual_ln_loss(x, y, gate1, g, b, target, *, name):
    s, d = x.shape

    def body(x_ref, y_ref, gt_ref, g_ref, b_ref, t_ref, dx_ref, loss_ref):
        z = ALPHA * x_ref[...] + gt_ref[...] * y_ref[...].astype(F32)
        xhat, _ = _layer_norm_hat(z)
        err = xhat * g_ref[...] + b_ref[...] - t_ref[...]
        dx_ref[...] = err * (1.0 / d)
        part = jnp.sum(jnp.sum(err * err, axis=1, keepdims=True), axis=0, keepdims=True) * (0.5 / d)

        @pl.when(pl.program_id(0) == 0)
        def _():
            loss_ref[...] = part

        @pl.when(pl.program_id(0) > 0)
        def _():
            loss_ref[...] += part

    return pl.pallas_call(
        body,
        name=name,
        grid=(s // ROW_TILE,),
        in_specs=[_rows(d), _rows(d), _vec(d), _vec(d), _vec(d), _rows(d)],
        out_specs=[_rows(d), pl.BlockSpec((1, 1), lambda i: (0, 0))],
        out_shape=[jax.ShapeDtypeStruct((s, d), F32), jax.ShapeDtypeStruct((1, 1), F32)],
        compiler_params=_params(("arbitrary",)),
    )(x, y, gate1, g, b, target)


ST_DSCALE, ST_DSHIFT, ST_DG, ST_DB, ST_DGATE = 0, 1, 2, 3, 4


def residual_ln_bwd(*, name, later=None, dxo=None, x_out=None, this=None):
    lead = later[0] if later is not None else dxo
    s, d = lead.shape
    has_later, has_ln = later is not None, this is not None

    def body(*refs):
        refs = list(refs)
        if has_later:
            dh_ref, dzl_ref, scl_ref = refs[:3]
            refs = refs[3:]
        else:
            dxo_ref = refs.pop(0)
        if has_ln:
            x_ref, y_ref, gt_ref, g_ref, b_ref = refs[:5]
            refs = refs[5:]
            dz_ref, dy_ref, st_ref = refs
        else:
            xo_ref, dx_ref, st_ref = refs

        @pl.when(pl.program_id(0) == 0)
        def _():
            st_ref[...] = jnp.zeros_like(st_ref)

        def acc(row, val):
            st_ref[row : row + 1, :] += jnp.sum(val, axis=0, keepdims=True)

        if has_ln:
            y = y_ref[...].astype(F32)
            gate1 = gt_ref[...]
            xhat, rstd = _layer_norm_hat(ALPHA * x_ref[...] + gate1 * y)
            x_out_v = xhat * g_ref[...] + b_ref[...]
        else:
            x_out_v = xo_ref[...]
        if has_later:
            dh = dh_ref[...]
            g_out = ALPHA * dzl_ref[...] + dh * (1.0 + scl_ref[...])
            acc(ST_DSCALE, dh * x_out_v)
            acc(ST_DSHIFT, dh)
        else:
            g_out = dxo_ref[...]
        if not has_ln:
            dx_ref[...] = g_out
            return
        acc(ST_DG, g_out * xhat)
        acc(ST_DB, g_out)
        dxh = g_out * g_ref[...]
        m1 = jnp.mean(dxh, axis=-1, keepdims=True)
        m2 = jnp.mean(dxh * xhat, axis=-1, keepdims=True)
        dz = rstd * (dxh - m1 - xhat * m2)
        acc(ST_DGATE, dz * y)
        dz_ref[...] = dz
        dy_ref[...] = (dz * gate1).astype(BF16)

    ins, specs = [], []
    if has_later:
        ins += list(later)
        specs += [_rows(d), _rows(d), _vec(d)]
    else:
        ins += [dxo]
        specs += [_rows(d)]
    if not has_ln:
        ins += [x_out]
        specs += [_rows(d)]
    if has_ln:
        ins += list(this)
        specs += [_rows(d), _rows(d), _vec(d), _vec(d), _vec(d)]
        out_specs = [_rows(d), _rows(d), _vec(d, SUBLANES)]
        out_shape = [
            jax.ShapeDtypeStruct((s, d), F32),
            jax.ShapeDtypeStruct((s, d), BF16),
            jax.ShapeDtypeStruct((SUBLANES, d), F32),
        ]
    else:
        out_specs = [_rows(d), _vec(d, SUBLANES)]
        out_shape = [jax.ShapeDtypeStruct((s, d), F32), jax.ShapeDtypeStruct((SUBLANES, d), F32)]
    return pl.pallas_call(
        body,
        name=name,
        grid=(s // ROW_TILE,),
        in_specs=specs,
        out_specs=out_specs,
        out_shape=out_shape,
        compiler_params=_params(("arbitrary",)),
    )(*ins)


GATE_CHUNKS = 4


def _gelu(x):
    return 0.5 * x * (1.0 + jnp.tanh(GELU_C * (x + GELU_A * x * x * x)))


def _gelu_grad(x):
    t = jnp.tanh(GELU_C * (x + GELU_A * x * x * x))
    return 0.5 * (1.0 + t) + 0.5 * x * (1.0 - t * t) * (GELU_C * (1.0 + 3.0 * GELU_A * x * x))


def _causal_weights(w_ref, transpose):
    t = lax.broadcasted_iota(jnp.int32, (CHUNK, CHUNK), 0)
    s = lax.broadcasted_iota(jnp.int32, (CHUNK, CHUNK), 1)
    out = []
    for g in range(A_GROUPS):
        w = jnp.where(t >= s, w_ref[g], 0.0)
        out.append((w.T if transpose else w).astype(BF16))
    return out


def _spatial(ws, vn, lo_mask):
    rows = vn.shape[0]
    out_rows = []
    for r in range(rows // CHUNK):
        cols = []
        for j in range(A_GROUPS // 2):
            blk = vn[r * CHUNK : (r + 1) * CHUNK, j * LANES : (j + 1) * LANES]
            za = jnp.dot(ws[2 * j], blk, preferred_element_type=F32)
            zb = jnp.dot(ws[2 * j + 1], blk, preferred_element_type=F32)
            cols.append(jnp.where(lo_mask, za, zb))
        out_rows.append(jnp.concatenate(cols, axis=1))
    return jnp.concatenate(out_rows, axis=0)


def _gate_forward(a, vg, vb, ws, bias, lo_mask):
    u = _gelu(a[:, :D_MODEL])
    v = _gelu(a[:, D_MODEL:])
    vhat, rstd = _layer_norm_hat(v)
    vn = (vhat * vg + vb).astype(BF16)
    z = _spatial(ws, vn, lo_mask) + jnp.concatenate([bias] * (a.shape[0] // CHUNK), axis=0)
    return u, vhat, rstd, vn, z


def gate_fwd(a_pre, vn_g, vn_b, w_s, bias_full, *, name):
    s = a_pre.shape[0]
    tr = GATE_CHUNKS * CHUNK

    def body(a_ref, vg_ref, vb_ref, w_ref, bias_ref, p_ref):
        lo_mask = _lane((CHUNK, LANES)) < A_GROUP_DIM
        ws = _causal_weights(w_ref, transpose=False)
        u, _, _, _, z = _gate_forward(a_ref[...].astype(F32), vg_ref[...], vb_ref[...], ws, bias_ref[...], lo_mask)
        p_ref[...] = (u * z).astype(BF16)

    return pl.pallas_call(
        body,
        name=name,
        grid=(s // tr,),
        in_specs=[
            pl.BlockSpec((tr, 2 * D_MODEL), lambda i: (i, 0)),
            _vec(D_MODEL),
            _vec(D_MODEL),
            pl.BlockSpec((A_GROUPS, CHUNK, CHUNK), lambda i: (0, 0, 0)),
            _vec(D_MODEL, CHUNK),
        ],
        out_specs=pl.BlockSpec((tr, D_MODEL), lambda i: (i, 0)),
        out_shape=jax.ShapeDtypeStruct((s, D_MODEL), BF16),
        compiler_params=_params(("parallel",)),
    )(a_pre, vn_g, vn_b, w_s, bias_full)


def gate_bwd(a_pre, dp, vn_g, vn_b, w_s, bias_full, *, name):
    s = a_pre.shape[0]
    tr = GATE_CHUNKS * CHUNK
    nsteps = s // tr

    def body(a_ref, dp_ref, vg_ref, vb_ref, w_ref, bias_ref, da_ref, dw_ref, dbs_ref, rows_ref, dbias_acc):
        step = pl.program_id(0)
        lo_mask = _lane((CHUNK, LANES)) < A_GROUP_DIM

        @pl.when(step == 0)
        def _():
            dw_ref[...] = jnp.zeros_like(dw_ref)
            rows_ref[...] = jnp.zeros_like(rows_ref)
            dbias_acc[...] = jnp.zeros_like(dbias_acc)

        a = a_ref[...].astype(F32)
        vg = vg_ref[...]
        ws = _causal_weights(w_ref, transpose=False)
        wts = _causal_weights(w_ref, transpose=True)
        u, vhat, rstd, vn, z = _gate_forward(a, vg, vb_ref[...], ws, bias_ref[...], lo_mask)
        dp = dp_ref[...]
        du = dp * z
        dzz = dp * u
        dzz_b = dzz.astype(BF16)
        dvn = _spatial(wts, dzz_b, lo_mask)
        dbias = None
        for r in range(GATE_CHUNKS):
            rs = slice(r * CHUNK, (r + 1) * CHUNK)
            dbias = dzz[rs] if dbias is None else dbias + dzz[rs]
            for j in range(A_GROUPS // 2):
                cs = slice(j * LANES, (j + 1) * LANES)
                dblk = dzz[rs, cs]
                vblk = vn[rs, cs]
                for half in range(2):
                    keep = lo_mask if half == 0 else jnp.logical_not(lo_mask)
                    dm = jnp.where(keep, dblk, 0.0).astype(BF16)
                    dw_ref[2 * j + half] += lax.dot_general(
                        dm, vblk, (((1,), (1,)), ((), ())), preferred_element_type=F32
                    )
        dbias_acc[...] += dbias
        rows_ref[1:2, :D_MODEL] += jnp.sum(dvn * vhat, axis=0, keepdims=True)
        rows_ref[1:2, D_MODEL:] += jnp.sum(dvn, axis=0, keepdims=True)
        dvh = dvn * vg
        m1 = jnp.mean(dvh, axis=-1, keepdims=True)
        m2 = jnp.mean(dvh * vhat, axis=-1, keepdims=True)
        dv = rstd * (dvh - m1 - vhat * m2)
        da_u = du * _gelu_grad(a[:, :D_MODEL])
        da_v = dv * _gelu_grad(a[:, D_MODEL:])
        da_ref[:, :D_MODEL] = da_u.astype(BF16)
        da_ref[:, D_MODEL:] = da_v.astype(BF16)
        rows_ref[0:1, :D_MODEL] += jnp.sum(da_u, axis=0, keepdims=True)
        rows_ref[0:1, D_MODEL:] += jnp.sum(da_v, axis=0, keepdims=True)

        @pl.when(step == nsteps - 1)
        def _():
            t = lax.broadcasted_iota(jnp.int32, (CHUNK, CHUNK), 0)
            sx = lax.broadcasted_iota(jnp.int32, (CHUNK, CHUNK), 1)
            for g in range(A_GROUPS):
                dw_ref[g] = jnp.where(t >= sx, dw_ref[g], 0.0)
            dbs_ref[...] = _reduce_groups(dbias_acc[...])

    return pl.pallas_call(
        body,
        name=name,
        grid=(nsteps,),
        in_specs=[
            pl.BlockSpec((tr, 2 * D_MODEL), lambda i: (i, 0)),
            pl.BlockSpec((tr, D_MODEL), lambda i: (i, 0)),
            _vec(D_MODEL),
            _vec(D_MODEL),
            pl.BlockSpec((A_GROUPS, CHUNK, CHUNK), lambda i: (0, 0, 0)),
            _vec(D_MODEL, CHUNK),
        ],
        out_specs=[
            pl.BlockSpec((tr, 2 * D_MODEL), lambda i: (i, 0)),
            pl.BlockSpec((A_GROUPS, CHUNK, CHUNK), lambda i: (0, 0, 0)),
            _vec(LANES, CHUNK),
            _vec(2 * D_MODEL, SUBLANES),
        ],
        out_shape=[
            jax.ShapeDtypeStruct((s, 2 * D_MODEL), BF16),
            jax.ShapeDtypeStruct((A_GROUPS, CHUNK, CHUNK), F32),
            jax.ShapeDtypeStruct((CHUNK, LANES), F32),
            jax.ShapeDtypeStruct((SUBLANES, 2 * D_MODEL), F32),
        ],
        scratch_shapes=[pltpu.VMEM((CHUNK, D_MODEL), F32)],
        compiler_params=_params(("arbitrary",)),
    )(a_pre, dp, vn_g, vn_b, w_s, bias_full)


def alibi_tables(dilation):
    qi = jnp.arange(SPAN)[:, None]
    ki = jnp.arange(2 * SPAN)[None, :]
    diff = SPAN + qi - ki
    valid = (diff >= 0) & (diff <= SPAN)
    heads = jnp.arange(1, B_HEADS + 1, dtype=F32)
    slopes = jnp.exp2(-8.0 * heads / B_HEADS)
    bias = -slopes[:, None, None] * (dilation * diff).astype(F32)
    bias = jnp.where(valid[None], bias, NEG).reshape(B_HEADS // 2, 2 * SPAN, 2 * SPAN)
    return bias, bias.transpose(0, 2, 1)


def _pair_rows(x, halves):
    return jnp.concatenate([x * halves[0], x * halves[1]], axis=0)


def _pair_column(v, lane, j):
    pick = lambda h: jnp.sum(jnp.where(lane == h, v, 0.0), axis=1, keepdims=True)
    return jnp.concatenate([pick(2 * j), pick(2 * j + 1)], axis=0)


_NT = (((1,), (1,)), ((), ()))


def permute_rows(x, dilation, *, inverse, name, add=None):
    s, w = x.shape
    tile = SPAN * dilation
    nat = pl.BlockSpec((tile, w), lambda i: (i, 0))
    streams = pl.BlockSpec((dilation, SPAN, w), lambda i: (0, i, 0))
    x3 = x.reshape(dilation, s // dilation, w) if inverse else x

    def body(*refs):
        if not inverse:
            x_ref, o_ref = refs
            o_ref[...] = jnp.swapaxes(x_ref[...].reshape(SPAN, dilation, w), 0, 1)
            return
        val = jnp.swapaxes(refs[0][...], 0, 1).reshape(tile, w)
        if add is not None:
            val = val + refs[1][...]
        refs[-1][...] = val

    out = pl.pallas_call(
        body,
        name=name,
        grid=(s // tile,),
        in_specs=([streams] + ([nat] if add is not None else [])) if inverse else [nat],
        out_specs=nat if inverse else streams,
        out_shape=jax.ShapeDtypeStruct((s, w) if inverse else (dilation, s // dilation, w), x.dtype),
        compiler_params=_params(("parallel",)),
    )(*([x3] + ([add] if add is not None else [])))
    return out.reshape(s, w)


def _qkv_specs(block_of):
    def spec(which, prev):
        def index(*grid):
            blk = block_of(*grid)
            return (jnp.maximum(blk - 1, 0) if prev else blk, which)

        return pl.BlockSpec((SPAN, D_MODEL), index)

    return [spec(0, False), spec(1, True), spec(1, False), spec(2, True), spec(2, False)]


def attn_fwd(qkv_p, pat, *, name):
    _, dilation = B_PATTERNS[pat]
    nb = SEQ // dilation // SPAN
    bias, _ = alibi_tables(dilation)

    def body(q_ref, kp_ref, kc_ref, vp_ref, vc_ref, bias_ref, o_ref, lse_ref):
        n = pl.program_id(1)
        lane = _lane((SPAN, LANES))
        lo_mask = lane < B_HEAD_DIM
        first_prev = jnp.logical_and(n == 0, _lane((2 * SPAN, 2 * SPAN)) < SPAN)
        q = q_ref[...] * jnp.asarray(B_HEAD_DIM**-0.5, BF16)
        kk = jnp.concatenate([kp_ref[...], kc_ref[...]], axis=0)
        vv = jnp.concatenate([vp_ref[...], vc_ref[...]], axis=0)
        halves = (lo_mask.astype(BF16), jnp.logical_not(lo_mask).astype(BF16))
        stats = jnp.zeros((SPAN, LANES), F32)
        for j in range(B_HEADS // 2):
            cs = slice(j * LANES, (j + 1) * LANES)
            sc = lax.dot_general(_pair_rows(q[:, cs], halves), kk[:, cs], _NT, preferred_element_type=F32)
            sc = jnp.where(first_prev, NEG, sc + bias_ref[j])
            m = jnp.max(sc, axis=1, keepdims=True)
            p = jnp.exp(sc - m)
            l = jnp.sum(p, axis=1, keepdims=True)
            acc = jnp.dot(p.astype(BF16), vv[:, cs], preferred_element_type=F32) * (1.0 / l)
            lse_pair = m + jnp.log(l)
            o_ref[:, cs] = jnp.where(lo_mask, acc[:SPAN], acc[SPAN:]).astype(BF16)
            stats = jnp.where(lane == 2 * j, lse_pair[:SPAN], stats)
            stats = jnp.where(lane == 2 * j + 1, lse_pair[SPAN:], stats)
        lse_ref[...] = stats

    return pl.pallas_call(
        body,
        name=name,
        grid=(dilation, nb),
        in_specs=[
            *_qkv_specs(lambda r, n: r * nb + n),
            pl.BlockSpec((B_HEADS // 2, 2 * SPAN, 2 * SPAN), lambda r, n: (0, 0, 0)),
        ],
        out_specs=[
            pl.BlockSpec((SPAN, D_MODEL), lambda r, n: (r * nb + n, 0)),
            pl.BlockSpec((SPAN, LANES), lambda r, n: (r * nb + n, 0)),
        ],
        out_shape=[jax.ShapeDtypeStruct((SEQ, D_MODEL), BF16), jax.ShapeDtypeStruct((SEQ, LANES), F32)],
        compiler_params=_params(("parallel", "arbitrary")),
    )(qkv_p, qkv_p, qkv_p, qkv_p, qkv_p, bias)


def attn_combine(outs, lses, *, name):
    def body(o0, o1, o2, l0, l1, l2, ob_ref, of_ref, lse_ref):
        ls = [l0[...], l1[...], l2[...]]
        m = jnp.maximum(jnp.maximum(ls[0], ls[1]), ls[2])
        tot = jnp.log(jnp.exp(ls[0] - m) + jnp.exp(ls[1] - m) + jnp.exp(ls[2] - m)) + m
        o = None
        for o_ref, l in zip((o0, o1, o2), ls, strict=True):
            term = _expand_groups(jnp.exp(l - tot)) * o_ref[...]
            o = term if o is None else o + term
        ob_ref[...] = o.astype(BF16)
        of_ref[...] = o
        lse_ref[...] = tot

    return pl.pallas_call(
        body,
        name=name,
        grid=(SEQ // ROW_TILE,),
        in_specs=[_rows(D_MODEL)] * 3 + [_rows(LANES)] * 3,
        out_specs=[_rows(D_MODEL), _rows(D_MODEL), _rows(LANES)],
        out_shape=[
            jax.ShapeDtypeStruct((SEQ, D_MODEL), BF16),
            jax.ShapeDtypeStruct((SEQ, D_MODEL), F32),
            jax.ShapeDtypeStruct((SEQ, LANES), F32),
        ],
        compiler_params=_params(("parallel",)),
    )(*outs, *lses)


def attn_delta(do, o, *, name):
    def body(do_ref, o_ref, dob_ref, dl_ref):
        do_v = do_ref[...]
        dob_ref[...] = do_v.astype(BF16)
        dl_ref[...] = _reduce_groups(do_v * o_ref[...])

    return pl.pallas_call(
        body,
        name=name,
        grid=(SEQ // ROW_TILE,),
        in_specs=[_rows(D_MODEL), _rows(D_MODEL)],
        out_specs=[_rows(D_MODEL), _rows(LANES)],
        out_shape=[jax.ShapeDtypeStruct((SEQ, D_MODEL), BF16), jax.ShapeDtypeStruct((SEQ, LANES), F32)],
        compiler_params=_params(("parallel",)),
    )(do, o)


def attn_bwd(qkv_p, do_p, lse_p, delta_p, pat, *, name):
    _, dilation = B_PATTERNS[pat]
    nb = SEQ // dilation // SPAN
    n_blocks = SEQ // SPAN
    bias, bias_t = alibi_tables(dilation)
    last = n_blocks - 1
    q_cols, k_cols, v_cols = (slice(i * D_MODEL, (i + 1) * D_MODEL) for i in range(3))

    def body(q_ref, kp_ref, kc_ref, vp_ref, vc_ref, do_ref, lse_ref, dl_ref, bias_ref, biast_ref, out_ref, cq_ref, ck_ref, cv_ref):
        g = pl.program_id(0)

        @pl.when(g == n_blocks)
        def _():
            out_ref[:, q_cols] = cq_ref[...].astype(BF16)
            out_ref[:, k_cols] = ck_ref[...].astype(BF16)
            out_ref[:, v_cols] = cv_ref[...].astype(BF16)

        @pl.when(g == 0)
        def _():
            cq_ref[...] = jnp.zeros_like(cq_ref)
            ck_ref[...] = jnp.zeros_like(ck_ref)
            cv_ref[...] = jnp.zeros_like(cv_ref)

        @pl.when(g < n_blocks)
        def _():
            lane = _lane((SPAN, LANES))
            lo_mask = lane < B_HEAD_DIM
            pair = (2 * SPAN, 2 * SPAN)
            first = lax.rem(g, nb) == 0
            prev_key_cols = jnp.logical_and(first, _lane(pair) < SPAN)
            prev_key_rows = jnp.logical_and(first, lax.broadcasted_iota(jnp.int32, pair, 0) < SPAN)
            q = q_ref[...] * jnp.asarray(B_HEAD_DIM**-0.5, BF16)
            kk = jnp.concatenate([kp_ref[...], kc_ref[...]], axis=0)
            vv = jnp.concatenate([vp_ref[...], vc_ref[...]], axis=0)
            do_v = do_ref[...]
            lse_v = lse_ref[...]
            dl_v = dl_ref[...]
            lse_t = lse_v.T
            dl_t = dl_v.T
            halves = (lo_mask.astype(BF16), jnp.logical_not(lo_mask).astype(BF16))
            for j in range(B_HEADS // 2):
                cs = slice(j * LANES, (j + 1) * LANES)
                kp, vp = kk[:, cs], vv[:, cs]
                q2 = _pair_rows(q[:, cs], halves)
                do2 = _pair_rows(do_v[:, cs], halves)
                lse_c, dl_c = _pair_column(lse_v, lane, j), _pair_column(dl_v, lane, j)
                lse_r = jnp.concatenate([lse_t[2 * j : 2 * j + 1], lse_t[2 * j + 1 : 2 * j + 2]], axis=1)
                dl_r = jnp.concatenate([dl_t[2 * j : 2 * j + 1], dl_t[2 * j + 1 : 2 * j + 2]], axis=1)
                sc = lax.dot_general(q2, kp, _NT, preferred_element_type=F32)
                p = jnp.exp(jnp.where(prev_key_cols, NEG, sc + bias_ref[j]) - lse_c)
                dp = lax.dot_general(do2, vp, _NT, preferred_element_type=F32)
                ds = (p * (dp - dl_c)).astype(BF16)
                dq2 = jnp.dot(ds, kp, preferred_element_type=F32)
                sc_t = lax.dot_general(kp, q2, _NT, preferred_element_type=F32)
                p_t = jnp.exp(jnp.where(prev_key_rows, NEG, sc_t + biast_ref[j]) - lse_r)
                dp_t = lax.dot_general(vp, do2, _NT, preferred_element_type=F32)
                ds_t = (p_t * (dp_t - dl_r)).astype(BF16)
                dk_pair = jnp.dot(ds_t, q2, preferred_element_type=F32)
                dv_pair = jnp.dot(p_t.astype(BF16), do2, preferred_element_type=F32)
                oq = slice(j * LANES, (j + 1) * LANES)
                ok = slice(D_MODEL + j * LANES, D_MODEL + (j + 1) * LANES)
                ov = slice(2 * D_MODEL + j * LANES, 2 * D_MODEL + (j + 1) * LANES)
                out_ref[:, oq] = cq_ref[:, cs].astype(BF16)
                out_ref[:, ok] = (ck_ref[:, cs] + dk_pair[:SPAN]).astype(BF16)
                out_ref[:, ov] = (cv_ref[:, cs] + dv_pair[:SPAN]).astype(BF16)
                cq_ref[:, cs] = jnp.where(lo_mask, dq2[:SPAN], dq2[SPAN:]) * (B_HEAD_DIM**-0.5)
                ck_ref[:, cs] = dk_pair[SPAN:]
                cv_ref[:, cs] = dv_pair[SPAN:]

    def block_of(g):
        return jnp.minimum(g, last)

    def row_spec(width):
        return pl.BlockSpec((SPAN, width), lambda g: (block_of(g), 0))

    return pl.pallas_call(
        body,
        name=name,
        grid=(n_blocks + 1,),
        in_specs=[
            *_qkv_specs(block_of),
            row_spec(D_MODEL),
            row_spec(LANES),
            row_spec(LANES),
            pl.BlockSpec((B_HEADS // 2, 2 * SPAN, 2 * SPAN), lambda g: (0, 0, 0)),
            pl.BlockSpec((B_HEADS // 2, 2 * SPAN, 2 * SPAN), lambda g: (0, 0, 0)),
        ],
        out_specs=pl.BlockSpec((SPAN, 3 * D_MODEL), lambda g: (jnp.maximum(g - 1, 0), 0)),
        out_shape=jax.ShapeDtypeStruct((SEQ, 3 * D_MODEL), BF16),
        scratch_shapes=[pltpu.VMEM((SPAN, D_MODEL), F32)] * 3,
        compiler_params=_params(("arbitrary",)),
    )(qkv_p, qkv_p, qkv_p, qkv_p, qkv_p, do_p, lse_p, delta_p, bias, bias_t)


def _position():
    x, y, c = lax.axis_index("x"), lax.axis_index("y"), lax.axis_index("c")
    return x, y, c, 4 * x + 2 * y + c


def _peer(k, x, y, c):
    px = 1 - x if k & 4 else x
    py = 1 - y if k & 2 else y
    pc = 1 - c if k & 1 else c
    return (px, py, pc), 4 * px + 2 * py + pc


def _remote(src, dst, send_sem, recv_sem, device):
    return pltpu.make_async_remote_copy(
        src_ref=src, dst_ref=dst, send_sem=send_sem, recv_sem=recv_sem, device_id=device, device_id_type=MESH
    )


def _silu_bf16(cf):
    return (cf * (1.0 / (1.0 + jnp.exp(-cf)))).astype(BF16)


def ada_exchange(c8, w4, b4, ln8):
    nt, _, ncol = w4.shape

    def body(c8_ref, w_ref, b_ref, ln_ref, cg_ref, lng_ref, mrecv_ref, mloc_ref, send_sems, recv_sems):
        x, y, c, me = _position()
        cg_ref[me] = c8_ref[...]
        lng_ref[me] = ln_ref[...]
        first = []
        for k in range(1, N_DEV):
            dev, _ = _peer(k, x, y, c)
            first.append(_remote(c8_ref, cg_ref.at[me], send_sems.at[0, k], recv_sems.at[0, k], dev))
            first.append(_remote(ln_ref, lng_ref.at[me], send_sems.at[1, k], recv_sems.at[1, k], dev))
        for cp in first:
            cp.start()
        for k in range(1, N_DEV):
            dev, pid = _peer(k, x, y, c)
            _remote(c8_ref, cg_ref.at[pid], send_sems.at[0, k], recv_sems.at[0, k], dev).wait_recv()
            _remote(ln_ref, lng_ref.at[pid], send_sems.at[1, k], recv_sems.at[1, k], dev).wait_recv()
        sc = _silu_bf16(cg_ref[...].reshape(N_DEV * SUBLANES, D_MODEL))
        for t in range(nt):
            mloc_ref[t] = jnp.dot(sc, w_ref[t].astype(BF16), preferred_element_type=F32) + b_ref[t : t + 1, :]

        def group(dev_id):
            return pl.ds(pl.multiple_of(dev_id * SUBLANES, SUBLANES), SUBLANES)

        mrecv_ref[me] = mloc_ref[:, group(me), :]
        second = []
        for k in range(1, N_DEV):
            dev, pid = _peer(k, x, y, c)
            second.append(
                _remote(mloc_ref.at[:, group(pid), :], mrecv_ref.at[me], send_sems.at[2, k], recv_sems.at[2, k], dev)
            )
        for cp in second:
            cp.start()
        for k in range(1, N_DEV):
            dev, pid = _peer(k, x, y, c)
            _remote(
                mloc_ref.at[:, group(pid), :], mrecv_ref.at[pid], send_sems.at[2, k], recv_sems.at[2, k], dev
            ).wait_recv()
        for cp in first + second:
            cp.wait_send()

    return pl.pallas_call(
        body,
        name="ada_exchange",
        in_specs=[VMEM, VMEM, VMEM, VMEM],
        out_specs=[VMEM, VMEM, VMEM],
        out_shape=[
            jax.ShapeDtypeStruct((N_DEV, SUBLANES, D_MODEL), F32),
            jax.ShapeDtypeStruct((N_DEV, SUBLANES, LANES), F32),
            jax.ShapeDtypeStruct((N_DEV, nt, SUBLANES, ncol), F32),
        ],
        scratch_shapes=[
            pltpu.VMEM((nt, N_DEV * SUBLANES, ncol), F32),
            pltpu.SemaphoreType.DMA((3, N_DEV)),
            pltpu.SemaphoreType.DMA((3, N_DEV)),
        ],
        compiler_params=pltpu.CompilerParams(vmem_limit_bytes=VMEM_LIMIT_BYTES),
    )(c8, w4, b4, ln8)


def small_exchange(dmx, flat):
    def body(dmx_ref, flat_ref, dmrecv_ref, red_ref, land_ref, send_sems, recv_sems):
        x, y, c, me = _position()
        dmrecv_ref[me] = dmx_ref[me]
        land_ref[me] = flat_ref[me]
        first = []
        for k in range(1, N_DEV):
            dev, pid = _peer(k, x, y, c)
            first.append(_remote(dmx_ref.at[pid], dmrecv_ref.at[me], send_sems.at[0, k], recv_sems.at[0, k], dev))
            first.append(_remote(flat_ref.at[pid], land_ref.at[me], send_sems.at[1, k], recv_sems.at[1, k], dev))
        for cp in first:
            cp.start()
        for k in range(1, N_DEV):
            dev, pid = _peer(k, x, y, c)
            _remote(dmx_ref.at[pid], dmrecv_ref.at[pid], send_sems.at[0, k], recv_sems.at[0, k], dev).wait_recv()
            _remote(flat_ref.at[pid], land_ref.at[pid], send_sems.at[1, k], recv_sems.at[1, k], dev).wait_recv()
        total = land_ref[0]
        for s in range(1, N_DEV):
            total = total + land_ref[s]
        red_ref[me] = total
        second = []
        for k in range(1, N_DEV):
            dev, _ = _peer(k, x, y, c)
            second.append(_remote(red_ref.at[me], red_ref.at[me], send_sems.at[2, k], recv_sems.at[2, k], dev))
        for cp in second:
            cp.start()
        for k in range(1, N_DEV):
            dev, pid = _peer(k, x, y, c)
            _remote(red_ref.at[pid], red_ref.at[pid], send_sems.at[2, k], recv_sems.at[2, k], dev).wait_recv()
        for cp in first + second:
            cp.wait_send()

    return pl.pallas_call(
        body,
        name="small_exchange",
        in_specs=[VMEM, VMEM],
        out_specs=[VMEM, VMEM],
        out_shape=[jax.ShapeDtypeStruct(dmx.shape, F32), jax.ShapeDtypeStruct(flat.shape, F32)],
        scratch_shapes=[
            pltpu.VMEM(flat.shape, F32),
            pltpu.SemaphoreType.DMA((3, N_DEV)),
            pltpu.SemaphoreType.DMA((3, N_DEV)),
        ],
        compiler_params=pltpu.CompilerParams(vmem_limit_bytes=VMEM_LIMIT_BYTES),
    )(dmx, flat)


HBM = pl.BlockSpec(memory_space=pltpu.HBM)
SEM = pl.BlockSpec(memory_space=pltpu.SEMAPHORE)
EFFECT = pltpu.SideEffectType.DATAFLOW_SIDE_EFFECTING


REGROUP_ROWS = 256


def shards_to_columns(x, *, name):
    p, k, n = x.shape

    def body(x_ref, o_ref):
        for s in range(p):
            o_ref[:, s * n : (s + 1) * n] = x_ref[s]

    return pl.pallas_call(
        body,
        name=name,
        grid=(k // REGROUP_ROWS,),
        in_specs=[pl.BlockSpec((p, REGROUP_ROWS, n), lambda i: (0, i, 0))],
        out_specs=pl.BlockSpec((REGROUP_ROWS, p * n), lambda i: (i, 0)),
        out_shape=jax.ShapeDtypeStruct((k, p * n), x.dtype),
        compiler_params=_params(("parallel",)),
    )(x)


def columns_to_shards(xs, *, name):
    k = xs[0].shape[0]
    widths = [x.shape[1] for x in xs]
    n = sum(widths) // N_DEV
    pieces = []
    for s in range(N_DEV):
        start = 0
        for i, w in enumerate(widths):
            lo, hi = max(start, s * n), min(start + w, (s + 1) * n)
            if lo < hi:
                pieces.append((i, lo - start, s, lo - s * n, hi - lo))
            start += w

    def body(*refs):
        x_refs, o_ref = refs[: len(xs)], refs[-1]
        for i, c0, s, d0, w in pieces:
            o_ref[s, :, d0 : d0 + w] = x_refs[i][:, c0 : c0 + w]

    return pl.pallas_call(
        body,
        name=name,
        grid=(k // REGROUP_ROWS,),
        in_specs=[pl.BlockSpec((REGROUP_ROWS, w), lambda i: (i, 0)) for w in widths],
        out_specs=pl.BlockSpec((N_DEV, REGROUP_ROWS, n), lambda i: (0, i, 0)),
        out_shape=jax.ShapeDtypeStruct((N_DEV, k, n), xs[0].dtype),
        compiler_params=_params(("parallel",)),
    )(*xs)


def _own_slot(me, block):
    land = lax.empty((N_DEV, *block.shape), block.dtype)
    return lax.dynamic_update_slice_in_dim(land, block[None], me, axis=0)


N_CHIP_PEERS = 3


class Gather:
    def __init__(self, shards, lands, after, *, name):
        nt = len(shards)
        self.name = name

        def body(*refs):
            src_refs, land_refs = refs[:nt], refs[nt : 2 * nt]
            send_sems, recv_sems = refs[2 * nt + 1 : 3 * nt + 1], refs[3 * nt + 1 : 4 * nt + 1]
            token = refs[-1]
            x, y, c, me = _position()
            for t in range(nt):
                for k, dev in enumerate(self._targets(x, y, c)):
                    _remote(src_refs[t], land_refs[t].at[me], send_sems[t].at[k], recv_sems[t].at[k], dev).start()
            token[...] = jnp.zeros_like(token)

        outs = pl.pallas_call(
            body,
            name=name + "_start",
            in_specs=[HBM] * (2 * nt) + [ANY],
            out_specs=[SEM] * (2 * nt) + [HBM] * (2 * nt) + [VMEM],
            out_shape=[pltpu.SemaphoreType.DMA((1 + N_CHIP_PEERS,))] * (2 * nt)
            + [pltpu.HBM(a.shape, a.dtype) for a in (*shards, *lands)]
            + [jax.ShapeDtypeStruct((SUBLANES, LANES), F32)],
            input_output_aliases={i: 2 * nt + i for i in range(2 * nt)},
            compiler_params=pltpu.CompilerParams(has_side_effects=EFFECT),
        )(*[pltpu.with_memory_space_constraint(a, pltpu.HBM) for a in (*shards, *lands)], after)
        self.send_sems, self.recv_sems = list(outs[:nt]), list(outs[nt : 2 * nt])
        self.srcs, self.lands = list(outs[2 * nt : 3 * nt]), list(outs[3 * nt : 4 * nt])
        self.token = outs[-1]

    @staticmethod
    def _chips(x, y):
        return [(1 - x, y), (x, 1 - y), (1 - x, 1 - y)]

    @classmethod
    def _targets(cls, x, y, c):
        return [(x, y, 1 - c)] + [(*chip, c) for chip in cls._chips(x, y)]

    def zero(self):
        return self.token[0, 0]

    def wait(self, which, after, *, name):
        n = len(which)

        def slot(px, py, pc):
            return 4 * px + 2 * py + pc

        def pass_body(*refs):
            land_refs, recv_sems = refs[:n], refs[n : 2 * n]
            fwd_send, fwd_recv = refs[3 * n + 1 : 4 * n + 1], refs[4 * n + 1 : 5 * n + 1]
            x, y, c, _ = _position()
            for t in range(n):
                for j, chip in enumerate(self._chips(x, y)):
                    blk = land_refs[t].at[slot(*chip, c)]
                    _remote(blk, blk, fwd_send[t].at[j], recv_sems[t].at[1 + j], (*chip, c)).wait_recv()
                    _remote(blk, blk, fwd_send[t].at[j], fwd_recv[t].at[j], (x, y, 1 - c)).start()

        lands = [self.lands[t] for t in which]
        outs = pl.pallas_call(
            pass_body,
            name=name + "_pass",
            in_specs=[HBM] * n + [SEM] * n + [ANY],
            out_specs=[HBM] * n + [SEM] * (2 * n),
            out_shape=[pltpu.HBM(a.shape, a.dtype) for a in lands] + [pltpu.SemaphoreType.DMA((N_CHIP_PEERS,))] * (2 * n),
            input_output_aliases={i: i for i in range(n)},
            compiler_params=pltpu.CompilerParams(has_side_effects=EFFECT),
        )(*lands, *[self.recv_sems[t] for t in which], after)
        lands, fwd_send, fwd_recv = outs[:n], outs[n : 2 * n], outs[2 * n :]

        def wait_body(*refs):
            src_refs, land_refs = refs[:n], refs[n : 2 * n]
            send_sems, recv_sems = refs[2 * n : 3 * n], refs[3 * n : 4 * n]
            fwd_send, fwd_recv = refs[4 * n : 5 * n], refs[5 * n : 6 * n]
            x, y, c, me = _position()
            sibling = (x, y, 1 - c)
            for t in range(n):
                for k, dev in enumerate(self._targets(x, y, c)):
                    _remote(src_refs[t], land_refs[t].at[me], send_sems[t].at[k], recv_sems[t].at[k], dev).wait_send()
                blk = land_refs[t].at[slot(x, y, 1 - c)]
                _remote(blk, blk, send_sems[t].at[0], recv_sems[t].at[0], sibling).wait_recv()
                for j, chip in enumerate(self._chips(x, y)):
                    sent = land_refs[t].at[slot(*chip, c)]
                    _remote(sent, sent, fwd_send[t].at[j], fwd_recv[t].at[j], sibling).wait_send()
                    got = land_refs[t].at[slot(*chip, 1 - c)]
                    _remote(got, got, fwd_send[t].at[j], fwd_recv[t].at[j], sibling).wait_recv()

        srcs = [self.srcs[t] for t in which]
        outs = pl.pallas_call(
            wait_body,
            name=name,
            in_specs=[HBM] * (2 * n) + [SEM] * (4 * n),
            out_specs=[HBM] * (2 * n),
            out_shape=[pltpu.HBM(a.shape, a.dtype) for a in (*srcs, *lands)],
            input_output_aliases={i: i for i in range(2 * n)},
            compiler_params=pltpu.CompilerParams(has_side_effects=EFFECT),
        )(*srcs, *lands, *[self.send_sems[t] for t in which], *[self.recv_sems[t] for t in which], *fwd_send, *fwd_recv)
        return outs[n:]


class Scatter:
    def __init__(self, srcs, lands, after, *, name):
        self.name = name
        nt = self.nt = len(srcs)
        peers = N_DEV - 1

        def body(*refs):
            src_refs, land_refs = refs[:nt], refs[nt : 2 * nt]
            send_sems, recv_sems = refs[2 * nt + 1 : 3 * nt + 1], refs[3 * nt + 1 : 4 * nt + 1]
            token = refs[-1]
            x, y, c, me = _position()
            for t in range(nt):
                for k in range(1, N_DEV):
                    dev, pid = _peer(k, x, y, c)
                    src = src_refs[t].at[pid]
                    _remote(src, land_refs[t].at[me], send_sems[t].at[k - 1], recv_sems[t].at[k - 1], dev).start()
            token[...] = jnp.zeros_like(token)

        outs = pl.pallas_call(
            body,
            name=name + "_start",
            in_specs=[HBM] * (2 * nt) + [ANY],
            out_specs=[SEM] * (2 * nt) + [HBM] * (2 * nt) + [VMEM],
            out_shape=[pltpu.SemaphoreType.DMA((peers,))] * (2 * nt)
            + [pltpu.HBM(a.shape, a.dtype) for a in (*srcs, *lands)]
            + [jax.ShapeDtypeStruct((SUBLANES, LANES), F32)],
            input_output_aliases={i: 2 * nt + i for i in range(2 * nt)},
            compiler_params=pltpu.CompilerParams(has_side_effects=EFFECT),
        )(*[pltpu.with_memory_space_constraint(a, pltpu.HBM) for a in (*srcs, *lands)], after)
        self.send_sems, self.recv_sems = outs[:nt], outs[nt : 2 * nt]
        self.srcs, self.lands = outs[2 * nt : 3 * nt], outs[3 * nt : 4 * nt]
        self.token = outs[-1]

    def zero(self):
        return self.token[0, 0]

    def wait(self, which, after, *, name):
        n = len(which)

        def body(*refs):
            src_refs, land_refs = refs[:n], refs[n : 2 * n]
            send_sems, recv_sems = refs[2 * n : 3 * n], refs[3 * n : 4 * n]
            x, y, c, _ = _position()
            for t in range(n):
                for k in range(1, N_DEV):
                    dev, pid = _peer(k, x, y, c)
                    src = src_refs[t].at[pid]
                    cp = _remote(src, land_refs[t].at[pid], send_sems[t].at[k - 1], recv_sems[t].at[k - 1], dev)
                    cp.wait_send()
                    cp.wait_recv()

        srcs = [self.srcs[t] for t in which]
        lands = [self.lands[t] for t in which]
        outs = pl.pallas_call(
            body,
            name=name,
            in_specs=[HBM] * (2 * n) + [SEM] * (2 * n) + [ANY],
            out_specs=[HBM] * (2 * n),
            out_shape=[pltpu.HBM(a.shape, a.dtype) for a in (*srcs, *lands)],
            input_output_aliases={i: i for i in range(2 * n)},
            compiler_params=pltpu.CompilerParams(has_side_effects=EFFECT),
        )(*srcs, *lands, *[self.send_sems[t] for t in which], *[self.recv_sems[t] for t in which], after)
        return outs[n:]


def _adam_update(g, w, m, v):
    m2 = ADAM_B1 * m + (1.0 - ADAM_B1) * g
    v2 = ADAM_B2 * v + (1.0 - ADAM_B2) * jnp.square(g)
    m_hat = m2 / (1.0 - ADAM_B1**ADAM_STEP)
    v_hat = v2 / (1.0 - ADAM_B2**ADAM_STEP)
    delta = -ADAM_LR * (m_hat / (jnp.sqrt(v_hat) + ADAM_EPS) + ADAM_WD * w)
    return delta, m2, v2


def adamw(gparts, w, m, v, *, name):
    nl, r, c = w.shape
    p = gparts[0].shape[0]
    tr = r if r <= 256 else (256 if c <= D_MODEL else 128)
    ni = r // tr

    def body(*refs):
        g_refs = refs[:nl]
        w_ref, m_ref, v_ref, go_ref, d_ref, mo_ref, vo_ref = refs[nl:]
        for layer in range(nl):

            @pl.when(pl.program_id(0) == layer)
            def _(g_ref=g_refs[layer]):
                g = g_ref[0].astype(F32)
                for i in range(1, p):
                    g = g + g_ref[i].astype(F32)
                delta, m2, v2 = _adam_update(g, w_ref[...], m_ref[...], v_ref[...])
                go_ref[...] = g
                d_ref[...] = delta
                mo_ref[...] = m2
                vo_ref[...] = v2

    def parts_spec(layer):
        def index(l, i):
            return (0, jnp.where(l == layer, i, jnp.where(l < layer, 0, ni - 1)), 0)

        return pl.BlockSpec((p, tr, c), index)

    blk = pl.BlockSpec((None, tr, c), lambda l, i: (l, i, 0))
    return pl.pallas_call(
        body,
        name=name,
        grid=(nl, ni),
        in_specs=[*[parts_spec(layer) for layer in range(nl)], blk, blk, blk],
        out_specs=[blk] * 4,
        out_shape=[jax.ShapeDtypeStruct((nl, r, c), F32)] * 4,
        compiler_params=_params(("arbitrary", "arbitrary")),
    )(*gparts, w, m, v)


def ada_grad_adamw(cg, dmrecv, w4, m4, v4, *, name):
    nt, k, ncol = w4.shape

    def body(cg_ref, dm_ref, w_ref, m_ref, v_ref, go_ref, d_ref, mo_ref, vo_ref, gb_ref):
        sc = _silu_bf16(cg_ref[...].reshape(N_DEV * SUBLANES, k))
        dm = dm_ref[...].reshape(N_DEV * SUBLANES, ncol)
        g = lax.dot_general(sc, dm.astype(BF16), (((0,), (0,)), ((), ())), preferred_element_type=F32)
        delta, m2, v2 = _adam_update(g, w_ref[...], m_ref[...], v_ref[...])
        go_ref[...] = g
        d_ref[...] = delta
        mo_ref[...] = m2
        vo_ref[...] = v2
        gb_ref[...] = jnp.broadcast_to(jnp.sum(dm, axis=0, keepdims=True), (SUBLANES, ncol))

    wblk = pl.BlockSpec((None, k, ncol), lambda t: (t, 0, 0))
    return pl.pallas_call(
        body,
        name=name,
        grid=(nt,),
        in_specs=[
            pl.BlockSpec((N_DEV, SUBLANES, k), lambda t: (0, 0, 0)),
            pl.BlockSpec((N_DEV, None, SUBLANES, ncol), lambda t: (0, t, 0, 0)),
            wblk,
            wblk,
            wblk,
        ],
        out_specs=[wblk] * 4 + [pl.BlockSpec((None, SUBLANES, ncol), lambda t: (t, 0, 0))],
        out_shape=[jax.ShapeDtypeStruct((nt, k, ncol), F32)] * 4 + [jax.ShapeDtypeStruct((nt, SUBLANES, ncol), F32)],
        compiler_params=_params(("parallel",)),
    )(cg, dmrecv, w4, m4, v4)


def kernel(x, c, ada_w, ada_b, ln_g, ln_b, a_w_in, a_b_in, a_vn_g, a_vn_b, a_w_s, a_b_s, a_w_out, b_w_qkv, b_w_out, mlp_w_up, mlp_w_down, loss_target, m_ada_w, m_ada_b, m_ln_g, m_ln_b, m_a_w_in, m_a_b_in, m_a_vn_g, m_a_vn_b, m_a_w_s, m_a_b_s, m_a_w_out, m_b_w_qkv, m_b_w_out, m_mlp_w_up, m_mlp_w_down, v_ada_w, v_ada_b, v_ln_g, v_ln_b, v_a_w_in, v_a_b_in, v_a_vn_g, v_a_vn_b, v_a_w_s, v_a_b_s, v_a_w_out, v_b_w_qkv, v_b_w_out, v_mlp_w_up, v_mlp_w_down):
    x0 = x[0]
    target = loss_target[0]
    me = 4 * lax.axis_index("x") + 2 * lax.axis_index("y") + lax.axis_index("c")

    ada_w4 = ada_w.reshape(N_SUB, D_MODEL, -1)
    ada_b4 = ada_b.reshape(N_SUB, -1)
    ln8 = jnp.concatenate([ln_g.reshape(N_SUB, -1), ln_b.reshape(N_SUB, -1)], axis=0)
    c8 = jnp.broadcast_to(c, (SUBLANES, D_MODEL))
    cg, lng, mrecv = ada_exchange(c8, ada_w4, ada_b4, ln8)

    W_IN, W_AOUT, W_UP0, W_DN0, W_QKV, W_BOUT, W_UP1, W_DN1 = range(8)
    shards = [
        a_w_in[0].astype(BF16),
        a_w_out[0].astype(BF16),
        mlp_w_up[0].astype(BF16),
        mlp_w_down[0].astype(BF16),
        b_w_qkv[0].astype(BF16),
        b_w_out[0].astype(BF16),
        mlp_w_up[1].astype(BF16),
        mlp_w_down[1].astype(BF16),
    ]
    gather = Gather(shards, [_own_slot(me, s) for s in shards], mrecv, name="gather")

    modv = mrecv[:, :, 0, :].transpose(1, 0, 2).reshape(N_SUB, 3 * D_MODEL) + gather.zero()
    shift = [modv[t : t + 1, :D_MODEL] for t in range(N_SUB)]
    scale = [modv[t : t + 1, D_MODEL : 2 * D_MODEL] for t in range(N_SUB)]
    gate1 = [1.0 + modv[t : t + 1, 2 * D_MODEL :] for t in range(N_SUB)]
    lng_full = [lng[:, t, :].reshape(1, D_MODEL) for t in range(N_SUB)]
    lnb_full = [lng[:, N_SUB + t, :].reshape(1, D_MODEL) for t in range(N_SUB)]

    ident = lambda acc: (acc,)
    relu2 = lambda a: jnp.square(jnp.maximum(a.astype(F32), 0.0)).astype(BF16)
    vn_g, vn_b, w_s = a_vn_g, a_vn_b, a_w_s[0]
    bias_full = jnp.repeat(a_b_s[0].T, A_GROUP_DIM, axis=1)
    w_up3, w_dn3 = [None, None], [None, None]

    def mlp_forward(i, h, after):
        up, dn = gather.wait([W_UP0, W_DN0] if i == 0 else [W_UP1, W_DN1], after, name=f"gather_wait_mlp{i}")
        w_up3[i], w_dn3[i] = up, dn.reshape(1, D_FF, D_MODEL)
        (a,) = mm_nn(h, w_up3[i], name=f"mlp{i}_up", tm=2048, ps=2, tn=512, tk=D_MODEL, epilogue=ident, outs=(BF16,))
        (y,) = mm_nn(
            a, w_dn3[i], name=f"mlp{i}_down", tm=2048, ps=1, tn=D_MODEL, tk=1024, prologue=relu2, epilogue=ident, outs=(BF16,)
        )
        return a, y

    h0 = modulate(x0, scale[0], shift[0], name="modulate0")
    w_in3, w_aout3 = gather.wait([W_IN, W_AOUT], h0, name="gather_wait_a")
    w_aout3 = w_aout3.reshape(1, D_MODEL, D_MODEL)
    (a_pre,) = mm_nn(
        h0, w_in3, name="a_in", tm=2048, ps=4, tn=256, tk=D_MODEL, epilogue=lambda acc, b: (acc + b,),
        extras=[(a_b_in, "row")], outs=(BF16,),
    )
    p_gate = gate_fwd(a_pre, vn_g, vn_b, w_s, bias_full, name="gate_fwd")
    (y0,) = mm_nn(p_gate, w_aout3, name="a_out", tm=2048, ps=1, tn=D_MODEL, tk=D_MODEL, epilogue=ident, outs=(BF16,))
    x1, h1 = residual_ln(x0, y0, gate1[0], lng_full[0], lnb_full[0], scale[1], shift[1], name="res_ln0")
    a1, y1 = mlp_forward(0, h1, y0)
    x2, h2 = residual_ln(x1, y1, gate1[1], lng_full[1], lnb_full[1], scale[2], shift[2], name="res_ln1")
    w_qkv_shards, w_bout3 = gather.wait([W_QKV, W_BOUT], y1, name="gather_wait_b")
    w_bout3 = w_bout3.reshape(1, D_MODEL, D_MODEL)
    w_qkv3 = shards_to_columns(w_qkv_shards, name="w_qkv_columns")[None]
    pat_tiles = 3
    dil = [d for _, d in B_PATTERNS]
    h2_p = [h2] + [permute_rows(h2, dil[g], inverse=False, name=f"perm_h{g}") for g in range(1, N_PAT)]
    qkv_p, pat_o, pat_lse = [], [], []
    for g in range(N_PAT):
        (qkv_g,) = mm_nn(
            h2_p[g], w_qkv3, name=f"b_qkv{g}", tm=2048, ps=1, tn=D_MODEL, tk=D_MODEL, epilogue=ident, outs=(BF16,),
            b_tile0=pat_tiles * g, b_tiles=pat_tiles,
        )
        o_g, lse_g = attn_fwd(qkv_g, g, name=f"attn_fwd{g}")
        if g > 0:
            o_g = permute_rows(o_g, dil[g], inverse=True, name=f"unperm_o{g}")
            lse_g = permute_rows(lse_g, dil[g], inverse=True, name=f"unperm_lse{g}")
        qkv_p.append(qkv_g)
        pat_o.append(o_g)
        pat_lse.append(lse_g)
    o_b, o_f, lse = attn_combine(pat_o, pat_lse, name="attn_combine")
    (y2,) = mm_nn(o_b, w_bout3, name="b_out", tm=2048, ps=1, tn=D_MODEL, tk=D_MODEL, epilogue=ident, outs=(BF16,))
    x3, h3 = residual_ln(x2, y2, gate1[2], lng_full[2], lnb_full[2], scale[3], shift[3], name="res_ln2")
    a3, y3 = mlp_forward(1, h3, y2)
    dxo, loss_local = residual_ln_loss(x3, y3, gate1[3], lng_full[3], lnb_full[3], target, name="res_ln3_loss")

    def scatter(parts, after, name):
        parts = [p.reshape(N_DEV, -1, p.shape[-1]) for p in parts]
        lands = [_own_slot(me, lax.dynamic_index_in_dim(p, me, 0, keepdims=False)) for p in parts]
        return Scatter(parts, lands, after, name=name)

    def mlp_backward(i, h, a, dy):
        (da,) = mm_nt(
            dy,
            w_dn3[i],
            name=f"mlp{i}_da",
            tm=2048,
            tko=1024,
            ps=1,
            tc=D_MODEL,
            epilogue=lambda acc, act: (acc * (2.0 * jnp.maximum(act.astype(F32), 0.0)),),
            extras=[(a, "full")],
            outs=(BF16,),
        )
        (dh,) = mm_nt(da, w_up3[i], name=f"mlp{i}_dh", tm=2048, tko=1024, ps=2, tc=512, epilogue=ident, outs=(F32,))
        dw_dn = mm_tn(
            a, dy, name=f"mlp{i}_dw_down", p=1, tk=1024, ps=1, tn=D_MODEL, tmc=2048, prologue=relu2, out_dtype=BF16
        )
        dw_up = mm_tn(h, da, name=f"mlp{i}_dw_up", p=N_DEV, tk=1024, ps=2, tn=512, tmc=2048, out_dtype=BF16)
        return scatter([dw_up, dw_dn], dh, f"scatter_mlp{i}"), dh

    dz3, dy3, st3 = residual_ln_bwd(name="res_bwd3", dxo=dxo, this=(x3, y3, gate1[3], lng_full[3], lnb_full[3]))
    rs_mlp1, dh3 = mlp_backward(1, h3, a3, dy3)
    dz2, dy2, st2 = residual_ln_bwd(
        name="res_bwd2", later=(dh3, dz3, scale[3] + rs_mlp1.zero()), this=(x2, y2, gate1[2], lng_full[2], lnb_full[2])
    )
    (d_o,) = mm_nt(dy2, w_bout3, name="b_do", tm=2048, tko=1024, ps=1, tc=D_MODEL, epilogue=ident, outs=(F32,))
    do_b, delta = attn_delta(d_o, o_f, name="attn_delta")
    dh2, dw_pat = None, []
    for g in range(N_PAT):
        do_g, lse_g, delta_g = do_b, lse, delta
        if g > 0:
            do_g = permute_rows(do_b, dil[g], inverse=False, name=f"perm_do{g}")
            lse_g = permute_rows(lse, dil[g], inverse=False, name=f"perm_lse{g}")
            delta_g = permute_rows(delta, dil[g], inverse=False, name=f"perm_delta{g}")
        dqkv_g = attn_bwd(qkv_p[g], do_g, lse_g, delta_g, g, name=f"attn_bwd{g}")
        (dh_g,) = mm_nt(
            dqkv_g, w_qkv3, name=f"b_dh{g}", tm=2048, tko=1024, ps=1, tc=D_MODEL, epilogue=ident, outs=(F32,),
            b_tile0=pat_tiles * g,
        )
        dh2 = dh_g if g == 0 else permute_rows(dh_g, dil[g], inverse=True, add=dh2, name=f"unperm_dh{g}")
        dw_pat.append(
            mm_tn(h2_p[g], dqkv_g, name=f"b_dw_qkv{g}", p=1, tk=1024, ps=1, tn=D_MODEL, tmc=2048, out_dtype=BF16)[0]
        )
    dw_bout = mm_tn(o_b, dy2, name="b_dw_out", p=1, tk=1024, ps=1, tn=D_MODEL, tmc=2048, out_dtype=BF16)
    dw_qkv = columns_to_shards(dw_pat, name="dw_qkv_shards")
    rs_b = scatter([dw_qkv, dw_bout], dh2, "scatter_b")
    dz1, dy1, st1 = residual_ln_bwd(
        name="res_bwd1", later=(dh2, dz2, scale[2] + rs_b.zero()), this=(x1, y1, gate1[1], lng_full[1], lnb_full[1])
    )
    rs_mlp0, dh1 = mlp_backward(0, h1, a1, dy1)
    dz0, dy0, st0 = residual_ln_bwd(
        name="res_bwd0", later=(dh1, dz1, scale[1] + rs_mlp0.zero()), this=(x0, y0, gate1[0], lng_full[0], lnb_full[0])
    )
    (dp_gate,) = mm_nt(dy0, w_aout3, name="a_dp", tm=2048, tko=1024, ps=1, tc=D_MODEL, epilogue=ident, outs=(F32,))
    dw_aout = mm_tn(p_gate, dy0, name="a_dw_out", p=1, tk=1024, ps=1, tn=D_MODEL, tmc=2048, out_dtype=BF16)
    rs_aout = scatter([dw_aout], dp_gate, "scatter_a_out")
    da0, d_ws, d_bs, gate_rows = gate_bwd(a_pre, dp_gate, vn_g + rs_aout.zero(), vn_b, w_s, bias_full, name="gate_bwd")
    dw_in = mm_tn(h0, da0, name="a_dw_in", p=N_DEV, tk=1024, ps=4, tn=256, tmc=2048, out_dtype=BF16)
    rs_in = scatter([dw_in], d_ws, "scatter_a_in")
    (dh0,) = mm_nt(da0, w_in3, name="a_dh", tm=2048, tko=1024, ps=4, tc=256, epilogue=ident, outs=(F32,))
    grad_x, stf = residual_ln_bwd(name="res_bwd_in", later=(dh0, dz0, scale[0] + rs_in.zero()), x_out=x0)

    stats_after = [stf, st0, st1, st2]
    stats_own = [st0, st1, st2, st3]
    dm = jnp.stack(
        [
            jnp.concatenate(
                [stats_after[t][ST_DSHIFT], stats_after[t][ST_DSCALE], stats_own[t][ST_DGATE]], axis=0
            )
            for t in range(N_SUB)
        ]
    )
    ncol = 3 * D_MODEL // N_DEV
    dmx = jnp.pad(
        dm.reshape(N_SUB, N_DEV, ncol).transpose(1, 0, 2)[:, :, None, :], ((0, 0), (0, 0), (0, SUBLANES - 1), (0, 0))
    )
    small = [
        gate_rows[0],
        gate_rows[1],
        d_ws.reshape(-1),
        d_bs[:, :A_GROUPS].T.reshape(-1),
        *[stats_own[t][ST_DG] for t in range(N_SUB)],
        *[stats_own[t][ST_DB] for t in range(N_SUB)],
        jnp.pad(loss_local.reshape(1), (0, LANES - 1)),
    ]
    n_small = sum(s.size for s in small)
    part_rows = -(-n_small // (N_DEV * LANES * SUBLANES)) * SUBLANES
    flat = jnp.concatenate(small + [jnp.zeros((N_DEV * part_rows * LANES - n_small,), F32)])
    dmrecv, reduced = small_exchange(dmx, flat.reshape(N_DEV, part_rows, LANES))
    reduced = reduced.reshape(-1)
    sizes = [2 * D_MODEL, D_MODEL, D_MODEL, A_GROUPS * CHUNK * CHUNK, A_GROUPS * CHUNK, N_SUB * D_MODEL, N_SUB * D_MODEL]
    offs = [sum(sizes[:i]) for i in range(len(sizes) + 1)]
    g_b_in, g_vn_g, g_vn_b, g_ws, g_bs, g_lng, g_lnb = [reduced[offs[i] : offs[i + 1]] for i in range(len(sizes))]
    loss = reduced[offs[-1]]

    results = {}

    def update(wname, gparts, w, m, v):
        shape = w.shape
        layers = len(gparts) if isinstance(gparts, list) else 1
        w3 = w.reshape(layers, -1, shape[-1])
        parts = [g.reshape(g.shape[0], *w3.shape[1:]) for g in (gparts if layers > 1 else [gparts])]
        outs = adamw(parts, w3, m.reshape(w3.shape), v.reshape(w3.shape), name=f"adamw_{wname}")
        results[wname] = [o.reshape(shape) for o in outs]

    ada_outs = ada_grad_adamw(cg, dmrecv, ada_w4, m_ada_w.reshape(ada_w4.shape), v_ada_w.reshape(ada_w4.shape), name="ada_grad_adamw")
    results["ada_w"] = [o.reshape(ada_w.shape) for o in ada_outs[:4]]
    update("ada_b", ada_outs[4][:, 0, :][None], ada_b, m_ada_b, v_ada_b)
    ln_cols = D_MODEL // N_DEV
    my_ln = lambda gfull: lax.dynamic_slice_in_dim(gfull.reshape(N_SUB, N_DEV, ln_cols), me, 1, axis=1)
    update("ln_g", my_ln(g_lng).reshape(1, N_SUB, ln_cols), ln_g, m_ln_g, v_ln_g)
    update("ln_b", my_ln(g_lnb).reshape(1, N_SUB, ln_cols), ln_b, m_ln_b, v_ln_b)
    update("a_b_in", g_b_in[None], a_b_in, m_a_b_in, v_a_b_in)
    update("a_vn_g", g_vn_g[None], a_vn_g, m_a_vn_g, v_a_vn_g)
    update("a_vn_b", g_vn_b[None], a_vn_b, m_a_vn_b, v_a_vn_b)
    update("a_w_s", g_ws[None], a_w_s, m_a_w_s, v_a_w_s)
    update("a_b_s", g_bs[None], a_b_s, m_a_b_s, v_a_b_s)
    g_up1, g_dn1 = rs_mlp1.wait([0, 1], grad_x, name="scatter_wait_mlp1")
    g_qkv, g_bout = rs_b.wait([0, 1], grad_x, name="scatter_wait_b")
    update("b_w_qkv", g_qkv, b_w_qkv, m_b_w_qkv, v_b_w_qkv)
    update("b_w_out", g_bout, b_w_out, m_b_w_out, v_b_w_out)
    g_up0, g_dn0 = rs_mlp0.wait([0, 1], grad_x, name="scatter_wait_mlp0")
    update("mlp_w_up", [g_up0, g_up1], mlp_w_up, m_mlp_w_up, v_mlp_w_up)
    update("mlp_w_down", [g_dn0, g_dn1], mlp_w_down, m_mlp_w_down, v_mlp_w_down)
    (g_aout,) = rs_aout.wait([0], grad_x, name="scatter_wait_a_out")
    (g_in,) = rs_in.wait([0], grad_x, name="scatter_wait_a_in")
    update("a_w_in", g_in, a_w_in, m_a_w_in, v_a_w_in)
    update("a_w_out", g_aout, a_w_out, m_a_w_out, v_a_w_out)

    order = ["ada_w", "ada_b", "ln_g", "ln_b", "a_w_in", "a_b_in", "a_vn_g", "a_vn_b", "a_w_s", "a_b_s", "a_w_out", "b_w_qkv", "b_w_out", "mlp_w_up", "mlp_w_down"]
    return (loss, grad_x[None], *[results[n][0] for n in order], *[results[n][1] for n in order],
            *[results[n][2] for n in order], *[results[n][3] for n in order])
```

```python
import math

import jax
import jax.numpy as jnp
from jax import lax
from jax.experimental import pallas as pl
from jax.experimental.pallas import tpu as pltpu

F32 = jnp.float32
BF16 = jnp.bfloat16
MESH = pl.DeviceIdType.MESH
ANY = pl.BlockSpec(memory_space=pl.ANY)
VMEM = pl.BlockSpec(memory_space=pltpu.VMEM)

N_DEV = 8
D_MODEL = 1024
SEQ = 4096
DEPTH = 2
CHUNK = 128
A_GROUPS = 16
A_GROUP_DIM = D_MODEL // A_GROUPS
B_HEADS = 16
B_HEAD_DIM = 64
B_PATTERNS = ((128, 1), (512, 4), (2048, 16))
N_PAT = len(B_PATTERNS)
SPAN = 128
D_FF = 4 * D_MODEL
D_QKV = N_PAT * 3 * D_MODEL
ALPHA = (2 * DEPTH) ** 0.25
LN_EPS = 1e-5
NEG = -1e30
ADAM_LR = 0.001
ADAM_B1 = 0.9
ADAM_B2 = 0.999
ADAM_EPS = 1e-08
ADAM_WD = 0.01
ADAM_STEP = 10
GELU_C = math.sqrt(2.0 / math.pi)
GELU_A = 0.044715

VMEM_LIMIT_BYTES = 56 * 1024 * 1024
LANES = 128
SUBLANES = 8
ROW_TILE = 512
N_SUB = 2 * DEPTH


def _params(sem):
    return pltpu.CompilerParams(dimension_semantics=sem, vmem_limit_bytes=VMEM_LIMIT_BYTES)


def _lane(shape):
    return lax.broadcasted_iota(jnp.int32, shape, len(shape) - 1)


def _split_bf16(x):
    hi = x.astype(BF16)
    lo = (x - hi.astype(F32)).astype(BF16)
    return hi, lo


def _group_expand_matrix(groups_padded, width):
    per = width // A_GROUPS
    r = lax.broadcasted_iota(jnp.int32, (groups_padded, width), 0)
    c = lax.broadcasted_iota(jnp.int32, (groups_padded, width), 1)
    return (c // per == r).astype(BF16)


def _group_reduce_matrix(width, groups_padded):
    per = width // A_GROUPS
    r = lax.broadcasted_iota(jnp.int32, (width, groups_padded), 0)
    c = lax.broadcasted_iota(jnp.int32, (width, groups_padded), 1)
    return (r // per == c).astype(BF16)


def _expand_groups(w):
    e = _group_expand_matrix(LANES, D_MODEL)
    hi, lo = _split_bf16(w)
    return jnp.dot(hi, e, preferred_element_type=F32) + jnp.dot(lo, e, preferred_element_type=F32)


def _reduce_groups(x):
    e = _group_reduce_matrix(D_MODEL, LANES)
    hi, lo = _split_bf16(x)
    return jnp.dot(hi, e, preferred_element_type=F32) + jnp.dot(lo, e, preferred_element_type=F32)


def _to_streams(x, d):
    rows, w = x.shape
    return jnp.swapaxes(x.reshape(rows // d, d, w), 0, 1).reshape(rows, w)


def _from_streams(x, d):
    rows, w = x.shape
    return jnp.swapaxes(x.reshape(d, rows // d, w), 0, 1).reshape(rows, w)


def _column_tiles(p, n, ps, tn):
    assert (ps == 1 or tn == n) and p % ps == 0 and n % tn == 0
    q = n // tn
    return (p // ps) * q, q


def _extra_specs(extras, tm, width):
    specs = []
    for _, kind in extras:
        if kind == "row":
            specs.append(pl.BlockSpec((1, width), lambda i, j, c: (0, j)))
        else:
            specs.append(pl.BlockSpec((tm, width), lambda i, j, c: (i, j)))
    return specs


def mm_nn(a, b3, *, name, tm, ps, tn, tk, epilogue, extras=(), outs, prologue=None, b_tile0=0, b_tiles=None, out_streams=1):
    m, k = a.shape
    p, _, n = b3.shape
    nj, q = _column_tiles(p, n, ps, tn)
    nj = nj if b_tiles is None else b_tiles
    nk = k // tk
    width = ps * tn
    d = out_streams
    assert d == 1 or not extras

    def body(a_ref, b_ref, *rest):
        ex = rest[: len(extras)]
        out_refs = rest[len(extras) : len(extras) + len(outs)]
        kk = pl.program_id(2)
        av = a_ref[...] if prologue is None else prologue(a_ref[...])
        if d > 1:
            av = _to_streams(av, d)

        def finish(cs, acc):
            res = epilogue(acc, *[e[:, cs] for e in ex])
            for o_ref, r in zip(out_refs, res, strict=True):
                if d > 1:
                    o_ref[:, :, cs] = r.astype(o_ref.dtype).reshape(d, tm // d, tn)
                else:
                    o_ref[:, cs] = r.astype(o_ref.dtype)

        for s in range(ps):
            cs = slice(s * tn, (s + 1) * tn)
            part = jnp.dot(av, b_ref[s], preferred_element_type=F32)
            if nk == 1:
                finish(cs, part)
                continue
            acc_ref = rest[-1]

            @pl.when(kk == 0)
            def _(part=part, cs=cs):
                acc_ref[:, cs] = part

            @pl.when(kk > 0)
            def _(part=part, cs=cs):
                acc_ref[:, cs] += part

        if nk > 1:

            @pl.when(kk == nk - 1)
            def _():
                for s in range(ps):
                    cs = slice(s * tn, (s + 1) * tn)
                    finish(cs, rest[-1][:, cs])

    if d > 1:
        out_spec = pl.BlockSpec((d, tm // d, width), lambda i, j, kk: (0, i, j))
        out_shape = (d, m // d, nj * width)
    else:
        out_spec = pl.BlockSpec((tm, width), lambda i, j, kk: (i, j))
        out_shape = (m, nj * width)
    res = pl.pallas_call(
        body,
        name=name,
        grid=(m // tm, nj, nk),
        in_specs=[
            pl.BlockSpec((tm, tk), lambda i, j, kk: (i, kk)),
            pl.BlockSpec((ps, tk, tn), lambda i, j, kk: ((j + b_tile0) // q, kk, (j + b_tile0) % q)),
            *_extra_specs(extras, tm, width),
        ],
        out_specs=[out_spec for _ in outs],
        out_shape=[jax.ShapeDtypeStruct(out_shape, dt) for dt in outs],
        scratch_shapes=[pltpu.VMEM((tm, width), F32)] if nk > 1 else [],
        compiler_params=_params(("parallel", "parallel", "arbitrary")),
    )(a, b3, *[arr for arr, _ in extras])
    return [r.reshape(m, nj * width) for r in res]


def mm_nt(g, b3, *, name, tm, tko, ps, tc, epilogue, extras=(), outs, b_tile0=0, g_streams=1):
    m, width = g.shape
    p, k, n = b3.shape
    _, q = _column_tiles(p, n, ps, tc)
    nc = width // (ps * tc)
    ds = g_streams

    def body(g_ref, b_ref, *rest):
        ex = rest[: len(extras)]
        out_refs = rest[len(extras) : len(extras) + len(outs)]
        c = pl.program_id(2)
        gv = g_ref[...].reshape(tm, ps * tc) if ds > 1 else g_ref[...]
        part = None
        for s in range(ps):
            d = lax.dot_general(gv[:, s * tc : (s + 1) * tc], b_ref[s], _NT, preferred_element_type=F32)
            part = d if part is None else part + d

        def finish(acc):
            res = epilogue(acc, *[e[...] for e in ex])
            for o_ref, r in zip(out_refs, res, strict=True):
                o_ref[...] = r.astype(o_ref.dtype)

        if nc == 1:
            finish(part)
            return
        acc_ref = rest[-1]

        @pl.when(c == 0)
        def _():
            acc_ref[...] = part

        @pl.when(c > 0)
        def _():
            acc_ref[...] += part

        @pl.when(c == nc - 1)
        def _():
            finish(acc_ref[...])

    return pl.pallas_call(
        body,
        name=name,
        grid=(m // tm, k // tko, nc),
        in_specs=[
            pl.BlockSpec((ds, tm // ds, ps * tc), lambda i, j, c: (0, i, c))
            if ds > 1
            else pl.BlockSpec((tm, ps * tc), lambda i, j, c: (i, c)),
            pl.BlockSpec((ps, tko, tc), lambda i, j, c: ((c + b_tile0) // q, j, (c + b_tile0) % q)),
            *_extra_specs(extras, tm, tko),
        ],
        out_specs=[pl.BlockSpec((tm, tko), lambda i, j, c: (i, j)) for _ in outs],
        out_shape=[jax.ShapeDtypeStruct((m, k), dt) for dt in outs],
        scratch_shapes=[pltpu.VMEM((tm, tko), F32)] if nc > 1 else [],
        compiler_params=_params(("parallel", "parallel", "arbitrary")),
    )(g.reshape(ds, m // ds, width) if ds > 1 else g, b3, *[arr for arr, _ in extras])


def mm_tn(a, g, *, name, p, tk, ps, tn, tmc, out_dtype, prologue=None, g_streams=1):
    m, k = a.shape
    width = g.shape[1]
    n = width // p
    nj, q = _column_tiles(p, n, ps, tn)
    nc = m // tmc
    ds = g_streams

    def body(a_ref, g_ref, o_ref, acc_ref):
        c = pl.program_id(2)
        av = a_ref[...] if prologue is None else prologue(a_ref[...])
        gv = g_ref[...]
        if ds > 1:
            av, gv = _to_streams(av, ds), gv.reshape(tmc, ps * tn)
        part = lax.dot_general(av, gv, (((0,), (0,)), ((), ())), preferred_element_type=F32)

        @pl.when(c == 0)
        def _():
            acc_ref[...] = part

        @pl.when(c > 0)
        def _():
            acc_ref[...] += part

        @pl.when(c == nc - 1)
        def _():
            for s in range(ps):
                o_ref[s] = acc_ref[:, s * tn : (s + 1) * tn].astype(o_ref.dtype)

    return pl.pallas_call(
        body,
        name=name,
        grid=(k // tk, nj, nc),
        in_specs=[
            pl.BlockSpec((tmc, tk), lambda i, j, c: (c, i)),
            pl.BlockSpec((ds, tmc // ds, ps * tn), lambda i, j, c: (0, c, j))
            if ds > 1
            else pl.BlockSpec((tmc, ps * tn), lambda i, j, c: (c, j)),
        ],
        out_specs=pl.BlockSpec((ps, tk, tn), lambda i, j, c: (j // q, i, j % q)),
        out_shape=jax.ShapeDtypeStruct((p, k, n), out_dtype),
        scratch_shapes=[pltpu.VMEM((tk, ps * tn), F32)],
        compiler_params=_params(("parallel", "parallel", "arbitrary")),
    )(a, g.reshape(ds, m // ds, width) if ds > 1 else g)


def _rows(cols):
    return pl.BlockSpec((ROW_TILE, cols), lambda i: (i, 0))


def _vec(cols, rows=1):
    return pl.BlockSpec((rows, cols), lambda i: (0, 0))


def _layer_norm_hat(z):
    mu = jnp.mean(z, axis=-1, keepdims=True)
    zc = z - mu
    var = jnp.mean(zc * zc, axis=-1, keepdims=True)
    rstd = lax.rsqrt(var + LN_EPS)
    return zc * rstd, rstd


def modulate(x, scale, shift, *, name):
    s, d = x.shape

    def body(x_ref, sc_ref, sh_ref, h_ref):
        h_ref[...] = (x_ref[...] * (1.0 + sc_ref[...]) + sh_ref[...]).astype(BF16)

    return pl.pallas_call(
        body,
        name=name,
        grid=(s // ROW_TILE,),
        in_specs=[_rows(d), _vec(d), _vec(d)],
        out_specs=_rows(d),
        out_shape=jax.ShapeDtypeStruct((s, d), BF16),
        compiler_params=_params(("parallel",)),
    )(x, scale, shift)


def residual_ln(x, y, gate1, g, b, nscale, nshift, *, name):
    s, d = x.shape

    def body(x_ref, y_ref, gt_ref, g_ref, b_ref, sc_ref, sh_ref, xn_ref, hn_ref):
        z = ALPHA * x_ref[...] + gt_ref[...] * y_ref[...].astype(F32)
        xhat, _ = _layer_norm_hat(z)
        xn = xhat * g_ref[...] + b_ref[...]
        xn_ref[...] = xn
        hn_ref[...] = (xn * (1.0 + sc_ref[...]) + sh_ref[...]).astype(BF16)

    return pl.pallas_call(
        body,
        name=name,
        grid=(s // ROW_TILE,),
        in_specs=[_rows(d), _rows(d), _vec(d), _vec(d), _vec(d), _vec(d), _vec(d)],
        out_specs=[_rows(d), _rows(d)],
        out_shape=[jax.ShapeDtypeStruct((s, d), F32), jax.ShapeDtypeStruct((s, d), BF16)],
        compiler_params=_params(("parallel",)),
    )(x, y, gate1, g, b, nscale, nshift)


def residual_ln_loss(x, y, gate1, g, b, target, *, name):
    s, d = x.shape

    def body(x_ref, y_ref, gt_ref, g_ref, b_ref, t_ref, dx_ref, loss_ref):
        z = ALPHA * x_ref[...] + gt_ref[...] * y_ref[...].astype(F32)
        xhat, _ = _layer_norm_hat(z)
        err = xhat * g_ref[...] + b_ref[...] - t_ref[...]
        dx_ref[...] = err * (1.0 / d)
        part = jnp.sum(jnp.sum(err * err, axis=1, keepdims=True), axis=0, keepdims=True) * (0.5 / d)

        @pl.when(pl.program_id(0) == 0)
        def _():
            loss_ref[...] = part

        @pl.when(pl.program_id(0) > 0)
        def _():
            loss_ref[...] += part

    return pl.pallas_call(
        body,
        name=name,
        grid=(s // ROW_TILE,),
        in_specs=[_rows(d), _rows(d), _vec(d), _vec(d), _vec(d), _rows(d)],
        out_specs=[_rows(d), pl.BlockSpec((1, 1), lambda i: (0, 0))],
        out_shape=[jax.ShapeDtypeStruct((s, d), F32), jax.ShapeDtypeStruct((1, 1), F32)],
        compiler_params=_params(("arbitrary",)),
    )(x, y, gate1, g, b, target)


ST_DSCALE, ST_DSHIFT, ST_DG, ST_DB, ST_DGATE = 0, 1, 2, 3, 4


def residual_ln_bwd(*, name, later=None, dxo=None, x_out=None, this=None):
    lead = later[0] if later is not None else dxo
    s, d = lead.shape
    has_later, has_ln = later is not None, this is not None

    def body(*refs):
        refs = list(refs)
        if has_later:
            dh_ref, dzl_ref, scl_ref = refs[:3]
            refs = refs[3:]
        else:
            dxo_ref = refs.pop(0)
        if has_ln:
            x_ref, y_ref, gt_ref, g_ref, b_ref = refs[:5]
            refs = refs[5:]
            dz_ref, dy_ref, st_ref = refs
        else:
            xo_ref, dx_ref, st_ref = refs

        @pl.when(pl.program_id(0) == 0)
        def _():
            st_ref[...] = jnp.zeros_like(st_ref)

        def acc(row, val):
            st_ref[row : row + 1, :] += jnp.sum(val, axis=0, keepdims=True)

        if has_ln:
            y = y_ref[...].astype(F32)
            gate1 = gt_ref[...]
            xhat, rstd = _layer_norm_hat(ALPHA * x_ref[...] + gate1 * y)
            x_out_v = xhat * g_ref[...] + b_ref[...]
        else:
            x_out_v = xo_ref[...]
        if has_later:
            dh = dh_ref[...]
            g_out = ALPHA * dzl_ref[...] + dh * (1.0 + scl_ref[...])
            acc(ST_DSCALE, dh * x_out_v)
            acc(ST_DSHIFT, dh)
        else:
            g_out = dxo_ref[...]
        if not has_ln:
            dx_ref[...] = g_out
            return
        acc(ST_DG, g_out * xhat)
        acc(ST_DB, g_out)
        dxh = g_out * g_ref[...]
        m1 = jnp.mean(dxh, axis=-1, keepdims=True)
        m2 = jnp.mean(dxh * xhat, axis=-1, keepdims=True)
        dz = rstd * (dxh - m1 - xhat * m2)
        acc(ST_DGATE, dz * y)
        dz_ref[...] = dz
        dy_ref[...] = (dz * gate1).astype(BF16)

    ins, specs = [], []
    if has_later:
        ins += list(later)
        specs += [_rows(d), _rows(d), _vec(d)]
    else:
        ins += [dxo]
        specs += [_rows(d)]
    if not has_ln:
        ins += [x_out]
        specs += [_rows(d)]
    if has_ln:
        ins += list(this)
        specs += [_rows(d), _rows(d), _vec(d), _vec(d), _vec(d)]
        out_specs = [_rows(d), _rows(d), _vec(d, SUBLANES)]
        out_shape = [
            jax.ShapeDtypeStruct((s, d), F32),
            jax.ShapeDtypeStruct((s, d), BF16),
            jax.ShapeDtypeStruct((SUBLANES, d), F32),
        ]
    else:
        out_specs = [_rows(d), _vec(d, SUBLANES)]
        out_shape = [jax.ShapeDtypeStruct((s, d), F32), jax.ShapeDtypeStruct((SUBLANES, d), F32)]
    return pl.pallas_call(
        body,
        name=name,
        grid=(s // ROW_TILE,),
        in_specs=specs,
        out_specs=out_specs,
        out_shape=out_shape,
        compiler_params=_params(("arbitrary",)),
    )(*ins)


GATE_CHUNKS = 4


def _gelu(x):
    return 0.5 * x * (1.0 + jnp.tanh(GELU_C * (x + GELU_A * x * x * x)))


def _gelu_grad(x):
    t = jnp.tanh(GELU_C * (x + GELU_A * x * x * x))
    return 0.5 * (1.0 + t) + 0.5 * x * (1.0 - t * t) * (GELU_C * (1.0 + 3.0 * GELU_A * x * x))


def _causal_weights(w_ref, transpose):
    t = lax.broadcasted_iota(jnp.int32, (CHUNK, CHUNK), 0)
    s = lax.broadcasted_iota(jnp.int32, (CHUNK, CHUNK), 1)
    out = []
    for g in range(A_GROUPS):
        w = jnp.where(t >= s, w_ref[g], 0.0)
        out.append((w.T if transpose else w).astype(BF16))
    return out


def _spatial(ws, vn, lo_mask):
    rows = vn.shape[0]
    out_rows = []
    for r in range(rows // CHUNK):
        cols = []
        for j in range(A_GROUPS // 2):
            blk = vn[r * CHUNK : (r + 1) * CHUNK, j * LANES : (j + 1) * LANES]
            za = jnp.dot(ws[2 * j], blk, preferred_element_type=F32)
            zb = jnp.dot(ws[2 * j + 1], blk, preferred_element_type=F32)
            cols.append(jnp.where(lo_mask, za, zb))
        out_rows.append(jnp.concatenate(cols, axis=1))
    return jnp.concatenate(out_rows, axis=0)


def _gate_forward(a, vg, vb, ws, bias, lo_mask):
    u = _gelu(a[:, :D_MODEL])
    v = _gelu(a[:, D_MODEL:])
    vhat, rstd = _layer_norm_hat(v)
    vn = (vhat * vg + vb).astype(BF16)
    z = _spatial(ws, vn, lo_mask) + jnp.concatenate([bias] * (a.shape[0] // CHUNK), axis=0)
    return u, vhat, rstd, vn, z


def gate_fwd(a_pre, vn_g, vn_b, w_s, bias_full, *, name):
    s = a_pre.shape[0]
    tr = GATE_CHUNKS * CHUNK

    def body(a_ref, vg_ref, vb_ref, w_ref, bias_ref, p_ref):
        lo_mask = _lane((CHUNK, LANES)) < A_GROUP_DIM
        ws = _causal_weights(w_ref, transpose=False)
        u, _, _, _, z = _gate_forward(a_ref[...].astype(F32), vg_ref[...], vb_ref[...], ws, bias_ref[...], lo_mask)
        p_ref[...] = (u * z).astype(BF16)

    return pl.pallas_call(
        body,
        name=name,
        grid=(s // tr,),
        in_specs=[
            pl.BlockSpec((tr, 2 * D_MODEL), lambda i: (i, 0)),
            _vec(D_MODEL),
            _vec(D_MODEL),
            pl.BlockSpec((A_GROUPS, CHUNK, CHUNK), lambda i: (0, 0, 0)),
            _vec(D_MODEL, CHUNK),
        ],
        out_specs=pl.BlockSpec((tr, D_MODEL), lambda i: (i, 0)),
        out_shape=jax.ShapeDtypeStruct((s, D_MODEL), BF16),
        compiler_params=_params(("parallel",)),
    )(a_pre, vn_g, vn_b, w_s, bias_full)


def gate_bwd(a_pre, dp, vn_g, vn_b, w_s, bias_full, *, name):
    s = a_pre.shape[0]
    tr = GATE_CHUNKS * CHUNK
    nsteps = s // tr

    def body(a_ref, dp_ref, vg_ref, vb_ref, w_ref, bias_ref, da_ref, dw_ref, dbs_ref, rows_ref, dbias_acc):
        step = pl.program_id(0)
        lo_mask = _lane((CHUNK, LANES)) < A_GROUP_DIM

        @pl.when(step == 0)
        def _():
            dw_ref[...] = jnp.zeros_like(dw_ref)
            rows_ref[...] = jnp.zeros_like(rows_ref)
            dbias_acc[...] = jnp.zeros_like(dbias_acc)

        a = a_ref[...].astype(F32)
        vg = vg_ref[...]
        ws = _causal_weights(w_ref, transpose=False)
        wts = _causal_weights(w_ref, transpose=True)
        u, vhat, rstd, vn, z = _gate_forward(a, vg, vb_ref[...], ws, bias_ref[...], lo_mask)
        dp = dp_ref[...]
        du = dp * z
        dzz = dp * u
        dzz_b = dzz.astype(BF16)
        dvn = _spatial(wts, dzz_b, lo_mask)
        dbias = None
        for r in range(GATE_CHUNKS):
            rs = slice(r * CHUNK, (r + 1) * CHUNK)
            dbias = dzz[rs] if dbias is None else dbias + dzz[rs]
            for j in range(A_GROUPS // 2):
                cs = slice(j * LANES, (j + 1) * LANES)
                dblk = dzz[rs, cs]
                vblk = vn[rs, cs]
                for half in range(2):
                    keep = lo_mask if half == 0 else jnp.logical_not(lo_mask)
                    dm = jnp.where(keep, dblk, 0.0).astype(BF16)
                    dw_ref[2 * j + half] += lax.dot_general(
                        dm, vblk, (((1,), (1,)), ((), ())), preferred_element_type=F32
                    )
        dbias_acc[...] += dbias
        rows_ref[1:2, :D_MODEL] += jnp.sum(dvn * vhat, axis=0, keepdims=True)
        rows_ref[1:2, D_MODEL:] += jnp.sum(dvn, axis=0, keepdims=True)
        dvh = dvn * vg
        m1 = jnp.mean(dvh, axis=-1, keepdims=True)
        m2 = jnp.mean(dvh * vhat, axis=-1, keepdims=True)
        dv = rstd * (dvh - m1 - vhat * m2)
        da_u = du * _gelu_grad(a[:, :D_MODEL])
        da_v = dv * _gelu_grad(a[:, D_MODEL:])
        da_ref[:, :D_MODEL] = da_u.astype(BF16)
        da_ref[:, D_MODEL:] = da_v.astype(BF16)
        rows_ref[0:1, :D_MODEL] += jnp.sum(da_u, axis=0, keepdims=True)
        rows_ref[0:1, D_MODEL:] += jnp.sum(da_v, axis=0, keepdims=True)

        @pl.when(step == nsteps - 1)
        def _():
            t = lax.broadcasted_iota(jnp.int32, (CHUNK, CHUNK), 0)
            sx = lax.broadcasted_iota(jnp.int32, (CHUNK, CHUNK), 1)
            for g in range(A_GROUPS):
                dw_ref[g] = jnp.where(t >= sx, dw_ref[g], 0.0)
            dbs_ref[...] = _reduce_groups(dbias_acc[...])

    return pl.pallas_call(
        body,
        name=name,
        grid=(nsteps,),
        in_specs=[
            pl.BlockSpec((tr, 2 * D_MODEL), lambda i: (i, 0)),
            pl.BlockSpec((tr, D_MODEL), lambda i: (i, 0)),
            _vec(D_MODEL),
            _vec(D_MODEL),
            pl.BlockSpec((A_GROUPS, CHUNK, CHUNK), lambda i: (0, 0, 0)),
            _vec(D_MODEL, CHUNK),
        ],
        out_specs=[
            pl.BlockSpec((tr, 2 * D_MODEL), lambda i: (i, 0)),
            pl.BlockSpec((A_GROUPS, CHUNK, CHUNK), lambda i: (0, 0, 0)),
            _vec(LANES, CHUNK),
            _vec(2 * D_MODEL, SUBLANES),
        ],
        out_shape=[
            jax.ShapeDtypeStruct((s, 2 * D_MODEL), BF16),
            jax.ShapeDtypeStruct((A_GROUPS, CHUNK, CHUNK), F32),
            jax.ShapeDtypeStruct((CHUNK, LANES), F32),
            jax.ShapeDtypeStruct((SUBLANES, 2 * D_MODEL), F32),
        ],
        scratch_shapes=[pltpu.VMEM((CHUNK, D_MODEL), F32)],
        compiler_params=_params(("arbitrary",)),
    )(a_pre, dp, vn_g, vn_b, w_s, bias_full)


def alibi_tables(dilation):
    qi = jnp.arange(SPAN)[:, None]
    ki = jnp.arange(2 * SPAN)[None, :]
    diff = SPAN + qi - ki
    valid = (diff >= 0) & (diff <= SPAN)
    heads = jnp.arange(1, B_HEADS + 1, dtype=F32)
    slopes = jnp.exp2(-8.0 * heads / B_HEADS)
    bias = -slopes[:, None, None] * (dilation * diff).astype(F32)
    bias = jnp.where(valid[None], bias, NEG).reshape(B_HEADS // 2, 2 * SPAN, 2 * SPAN)
    return bias, bias.transpose(0, 2, 1)


def _pair_rows(x, halves):
    return jnp.concatenate([x * halves[0], x * halves[1]], axis=0)


def _pair_column(v, lane, j):
    pick = lambda h: jnp.sum(jnp.where(lane == h, v, 0.0), axis=1, keepdims=True)
    return jnp.concatenate([pick(2 * j), pick(2 * j + 1)], axis=0)


_NT = (((1,), (1,)), ((), ()))


def permute_rows(x, dilation, *, inverse, name, add=None):
    s, w = x.shape
    tile = SPAN * dilation
    nat = pl.BlockSpec((tile, w), lambda i: (i, 0))
    streams = pl.BlockSpec((dilation, SPAN, w), lambda i: (0, i, 0))
    x3 = x.reshape(dilation, s // dilation, w) if inverse else x

    def body(*refs):
        if not inverse:
            x_ref, o_ref = refs
            o_ref[...] = jnp.swapaxes(x_ref[...].reshape(SPAN, dilation, w), 0, 1)
            return
        val = jnp.swapaxes(refs[0][...], 0, 1).reshape(tile, w)
        if add is not None:
            val = val + refs[1][...]
        refs[-1][...] = val

    out = pl.pallas_call(
        body,
        name=name,
        grid=(s // tile,),
        in_specs=([streams] + ([nat] if add is not None else [])) if inverse else [nat],
        out_specs=nat if inverse else streams,
        out_shape=jax.ShapeDtypeStruct((s, w) if inverse else (dilation, s // dilation, w), x.dtype),
        compiler_params=_params(("parallel",)),
    )(*([x3] + ([add] if add is not None else [])))
    return out.reshape(s, w)


def _qkv_specs(block_of):
    def spec(which, prev):
        def index(*grid):
            blk = block_of(*grid)
            return (jnp.maximum(blk - 1, 0) if prev else blk, which)

        return pl.BlockSpec((SPAN, D_MODEL), index)

    return [spec(0, False), spec(1, True), spec(1, False), spec(2, True), spec(2, False)]


def attn_fwd(qkv_p, pat, *, name):
    _, dilation = B_PATTERNS[pat]
    nb = SEQ // dilation // SPAN
    bias, _ = alibi_tables(dilation)

    def body(q_ref, kp_ref, kc_ref, vp_ref, vc_ref, bias_ref, o_ref, lse_ref):
        n = pl.program_id(1)
        lane = _lane((SPAN, LANES))
        lo_mask = lane < B_HEAD_DIM
        first_prev = jnp.logical_and(n == 0, _lane((2 * SPAN, 2 * SPAN)) < SPAN)
        q = q_ref[...] * jnp.asarray(B_HEAD_DIM**-0.5, BF16)
        kk = jnp.concatenate([kp_ref[...], kc_ref[...]], axis=0)
        vv = jnp.concatenate([vp_ref[...], vc_ref[...]], axis=0)
        halves = (lo_mask.astype(BF16), jnp.logical_not(lo_mask).astype(BF16))
        stats = jnp.zeros((SPAN, LANES), F32)
        for j in range(B_HEADS // 2):
            cs = slice(j * LANES, (j + 1) * LANES)
            sc = lax.dot_general(_pair_rows(q[:, cs], halves), kk[:, cs], _NT, preferred_element_type=F32)
            sc = jnp.where(first_prev, NEG, sc + bias_ref[j])
            m = jnp.max(sc, axis=1, keepdims=True)
            p = jnp.exp(sc - m)
            l = jnp.sum(p, axis=1, keepdims=True)
            acc = jnp.dot(p.astype(BF16), vv[:, cs], preferred_element_type=F32) * (1.0 / l)
            lse_pair = m + jnp.log(l)
            o_ref[:, cs] = jnp.where(lo_mask, acc[:SPAN], acc[SPAN:]).astype(BF16)
            stats = jnp.where(lane == 2 * j, lse_pair[:SPAN], stats)
            stats = jnp.where(lane == 2 * j + 1, lse_pair[SPAN:], stats)
        lse_ref[...] = stats

    return pl.pallas_call(
        body,
        name=name,
        grid=(dilation, nb),
        in_specs=[
            *_qkv_specs(lambda r, n: r * nb + n),
            pl.BlockSpec((B_HEADS // 2, 2 * SPAN, 2 * SPAN), lambda r, n: (0, 0, 0)),
        ],
        out_specs=[
            pl.BlockSpec((SPAN, D_MODEL), lambda r, n: (r * nb + n, 0)),
            pl.BlockSpec((SPAN, LANES), lambda r, n: (r * nb + n, 0)),
        ],
        out_shape=[jax.ShapeDtypeStruct((SEQ, D_MODEL), BF16), jax.ShapeDtypeStruct((SEQ, LANES), F32)],
        compiler_params=_params(("parallel", "arbitrary")),
    )(qkv_p, qkv_p, qkv_p, qkv_p, qkv_p, bias)


def attn_combine(outs, lses, *, name):
    def body(o0, o1, o2, l0, l1, l2, ob_ref, of_ref, lse_ref):
        ls = [l0[...], l1[...], l2[...]]
        m = jnp.maximum(jnp.maximum(ls[0], ls[1]), ls[2])
        tot = jnp.log(jnp.exp(ls[0] - m) + jnp.exp(ls[1] - m) + jnp.exp(ls[2] - m)) + m
        o = None
        for o_ref, l in zip((o0, o1, o2), ls, strict=True):
            term = _expand_groups(jnp.exp(l - tot)) * o_ref[...]
            o = term if o is None else o + term
        ob_ref[...] = o.astype(BF16)
        of_ref[...] = o
        lse_ref[...] = tot

    return pl.pallas_call(
        body,
        name=name,
        grid=(SEQ // ROW_TILE,),
        in_specs=[_rows(D_MODEL)] * 3 + [_rows(LANES)] * 3,
        out_specs=[_rows(D_MODEL), _rows(D_MODEL), _rows(LANES)],
        out_shape=[
            jax.ShapeDtypeStruct((SEQ, D_MODEL), BF16),
            jax.ShapeDtypeStruct((SEQ, D_MODEL), F32),
            jax.ShapeDtypeStruct((SEQ, LANES), F32),
        ],
        compiler_params=_params(("parallel",)),
    )(*outs, *lses)


def attn_delta(do, o, *, name):
    def body(do_ref, o_ref, dob_ref, dl_ref):
        do_v = do_ref[...]
        dob_ref[...] = do_v.astype(BF16)
        dl_ref[...] = _reduce_groups(do_v * o_ref[...])

    return pl.pallas_call(
        body,
        name=name,
        grid=(SEQ // ROW_TILE,),
        in_specs=[_rows(D_MODEL), _rows(D_MODEL)],
        out_specs=[_rows(D_MODEL), _rows(LANES)],
        out_shape=[jax.ShapeDtypeStruct((SEQ, D_MODEL), BF16), jax.ShapeDtypeStruct((SEQ, LANES), F32)],
        compiler_params=_params(("parallel",)),
    )(do, o)


def attn_bwd(qkv_p, do_p, lse_p, delta_p, pat, *, name):
    _, dilation = B_PATTERNS[pat]
    nb = SEQ // dilation // SPAN
    n_blocks = SEQ // SPAN
    bias, bias_t = alibi_tables(dilation)
    last = n_blocks - 1
    q_cols, k_cols, v_cols = (slice(i * D_MODEL, (i + 1) * D_MODEL) for i in range(3))

    def body(q_ref, kp_ref, kc_ref, vp_ref, vc_ref, do_ref, lse_ref, dl_ref, bias_ref, biast_ref, out_ref, cq_ref, ck_ref, cv_ref):
        g = pl.program_id(0)

        @pl.when(g == n_blocks)
        def _():
            out_ref[:, q_cols] = cq_ref[...].astype(BF16)
            out_ref[:, k_cols] = ck_ref[...].astype(BF16)
            out_ref[:, v_cols] = cv_ref[...].astype(BF16)

        @pl.when(g == 0)
        def _():
            cq_ref[...] = jnp.zeros_like(cq_ref)
            ck_ref[...] = jnp.zeros_like(ck_ref)
            cv_ref[...] = jnp.zeros_like(cv_ref)

        @pl.when(g < n_blocks)
        def _():
            lane = _lane((SPAN, LANES))
            lo_mask = lane < B_HEAD_DIM
            pair = (2 * SPAN, 2 * SPAN)
            first = lax.rem(g, nb) == 0
            prev_key_cols = jnp.logical_and(first, _lane(pair) < SPAN)
            prev_key_rows = jnp.logical_and(first, lax.broadcasted_iota(jnp.int32, pair, 0) < SPAN)
            q = q_ref[...] * jnp.asarray(B_HEAD_DIM**-0.5, BF16)
            kk = jnp.concatenate([kp_ref[...], kc_ref[...]], axis=0)
            vv = jnp.concatenate([vp_ref[...], vc_ref[...]], axis=0)
            do_v = do_ref[...]
            lse_v = lse_ref[...]
            dl_v = dl_ref[...]
            lse_t = lse_v.T
            dl_t = dl_v.T
            halves = (lo_mask.astype(BF16), jnp.logical_not(lo_mask).astype(BF16))
            for j in range(B_HEADS // 2):
                cs = slice(j * LANES, (j + 1) * LANES)
                kp, vp = kk[:, cs], vv[:, cs]
                q2 = _pair_rows(q[:, cs], halves)
                do2 = _pair_rows(do_v[:, cs], halves)
                lse_c, dl_c = _pair_column(lse_v, lane, j), _pair_column(dl_v, lane, j)
                lse_r = jnp.concatenate([lse_t[2 * j : 2 * j + 1], lse_t[2 * j + 1 : 2 * j + 2]], axis=1)
                dl_r = jnp.concatenate([dl_t[2 * j : 2 * j + 1], dl_t[2 * j + 1 : 2 * j + 2]], axis=1)
                sc = lax.dot_general(q2, kp, _NT, preferred_element_type=F32)
                p = jnp.exp(jnp.where(prev_key_cols, NEG, sc + bias_ref[j]) - lse_c)
                dp = lax.dot_general(do2, vp, _NT, preferred_element_type=F32)
                ds = (p * (dp - dl_c)).astype(BF16)
                dq2 = jnp.dot(ds, kp, preferred_element_type=F32)
                sc_t = lax.dot_general(kp, q2, _NT, preferred_element_type=F32)
                p_t = jnp.exp(jnp.where(prev_key_rows, NEG, sc_t + biast_ref[j]) - lse_r)
                dp_t = lax.dot_general(vp, do2, _NT, preferred_element_type=F32)
                ds_t = (p_t * (dp_t - dl_r)).astype(BF16)
                dk_pair = jnp.dot(ds_t, q2, preferred_element_type=F32)
                dv_pair = jnp.dot(p_t.astype(BF16), do2, preferred_element_type=F32)
                oq = slice(j * LANES, (j + 1) * LANES)
                ok = slice(D_MODEL + j * LANES, D_MODEL + (j + 1) * LANES)
                ov = slice(2 * D_MODEL + j * LANES, 2 * D_MODEL + (j + 1) * LANES)
                out_ref[:, oq] = cq_ref[:, cs].astype(BF16)
                out_ref[:, ok] = (ck_ref[:, cs] + dk_pair[:SPAN]).astype(BF16)
                out_ref[:, ov] = (cv_ref[:, cs] + dv_pair[:SPAN]).astype(BF16)
                cq_ref[:, cs] = jnp.where(lo_mask, dq2[:SPAN], dq2[SPAN:]) * (B_HEAD_DIM**-0.5)
                ck_ref[:, cs] = dk_pair[SPAN:]
                cv_ref[:, cs] = dv_pair[SPAN:]

    def block_of(g):
        return jnp.minimum(g, last)

    def row_spec(width):
        return pl.BlockSpec((SPAN, width), lambda g: (block_of(g), 0))

    return pl.pallas_call(
        body,
        name=name,
        grid=(n_blocks + 1,),
        in_specs=[
            *_qkv_specs(block_of),
            row_spec(D_MODEL),
            row_spec(LANES),
            row_spec(LANES),
            pl.BlockSpec((B_HEADS // 2, 2 * SPAN, 2 * SPAN), lambda g: (0, 0, 0)),
            pl.BlockSpec((B_HEADS // 2, 2 * SPAN, 2 * SPAN), lambda g: (0, 0, 0)),
        ],
        out_specs=pl.BlockSpec((SPAN, 3 * D_MODEL), lambda g: (jnp.maximum(g - 1, 0), 0)),
        out_shape=jax.ShapeDtypeStruct((SEQ, 3 * D_MODEL), BF16),
        scratch_shapes=[pltpu.VMEM((SPAN, D_MODEL), F32)] * 3,
        compiler_params=_params(("arbitrary",)),
    )(qkv_p, qkv_p, qkv_p, qkv_p, qkv_p, do_p, lse_p, delta_p, bias, bias_t)


def _position():
    x, y, c = lax.axis_index("x"), lax.axis_index("y"), lax.axis_index("c")
    return x, y, c, 4 * x + 2 * y + c


def _peer(k, x, y, c):
    px = 1 - x if k & 4 else x
    py = 1 - y if k & 2 else y
    pc = 1 - c if k & 1 else c
    return (px, py, pc), 4 * px + 2 * py + pc


def _remote(src, dst, send_sem, recv_sem, device):
    return pltpu.make_async_remote_copy(
        src_ref=src, dst_ref=dst, send_sem=send_sem, recv_sem=recv_sem, device_id=device, device_id_type=MESH
    )


def _silu_bf16(cf):
    return (cf * (1.0 / (1.0 + jnp.exp(-cf)))).astype(BF16)


def ada_exchange(c8, w4, b4, ln8):
    nt, _, ncol = w4.shape

    def body(c8_ref, w_ref, b_ref, ln_ref, cg_ref, lng_ref, mrecv_ref, mloc_ref, send_sems, recv_sems):
        x, y, c, me = _position()
        cg_ref[me] = c8_ref[...]
        lng_ref[me] = ln_ref[...]
        first = []
        for k in range(1, N_DEV):
            dev, _ = _peer(k, x, y, c)
            first.append(_remote(c8_ref, cg_ref.at[me], send_sems.at[0, k], recv_sems.at[0, k], dev))
            first.append(_remote(ln_ref, lng_ref.at[me], send_sems.at[1, k], recv_sems.at[1, k], dev))
        for cp in first:
            cp.start()
        for k in range(1, N_DEV):
            dev, pid = _peer(k, x, y, c)
            _remote(c8_ref, cg_ref.at[pid], send_sems.at[0, k], recv_sems.at[0, k], dev).wait_recv()
            _remote(ln_ref, lng_ref.at[pid], send_sems.at[1, k], recv_sems.at[1, k], dev).wait_recv()
        sc = _silu_bf16(cg_ref[...].reshape(N_DEV * SUBLANES, D_MODEL))
        for t in range(nt):
            mloc_ref[t] = jnp.dot(sc, w_ref[t].astype(BF16), preferred_element_type=F32) + b_ref[t : t + 1, :]

        def group(dev_id):
            return pl.ds(pl.multiple_of(dev_id * SUBLANES, SUBLANES), SUBLANES)

        mrecv_ref[me] = mloc_ref[:, group(me), :]
        second = []
        for k in range(1, N_DEV):
            dev, pid = _peer(k, x, y, c)
            second.append(
                _remote(mloc_ref.at[:, group(pid), :], mrecv_ref.at[me], send_sems.at[2, k], recv_sems.at[2, k], dev)
            )
        for cp in second:
            cp.start()
        for k in range(1, N_DEV):
            dev, pid = _peer(k, x, y, c)
            _remote(
                mloc_ref.at[:, group(pid), :], mrecv_ref.at[pid], send_sems.at[2, k], recv_sems.at[2, k], dev
            ).wait_recv()
        for cp in first + second:
            cp.wait_send()

    return pl.pallas_call(
        body,
        name="ada_exchange",
        in_specs=[VMEM, VMEM, VMEM, VMEM],
        out_specs=[VMEM, VMEM, VMEM],
        out_shape=[
            jax.ShapeDtypeStruct((N_DEV, SUBLANES, D_MODEL), F32),
            jax.ShapeDtypeStruct((N_DEV, SUBLANES, LANES), F32),
            jax.ShapeDtypeStruct((N_DEV, nt, SUBLANES, ncol), F32),
        ],
        scratch_shapes=[
            pltpu.VMEM((nt, N_DEV * SUBLANES, ncol), F32),
            pltpu.SemaphoreType.DMA((3, N_DEV)),
            pltpu.SemaphoreType.DMA((3, N_DEV)),
        ],
        compiler_params=pltpu.CompilerParams(vmem_limit_bytes=VMEM_LIMIT_BYTES),
    )(c8, w4, b4, ln8)


def small_exchange(dmx, flat):
    def body(dmx_ref, flat_ref, dmrecv_ref, red_ref, land_ref, send_sems, recv_sems):
        x, y, c, me = _position()
        dmrecv_ref[me] = dmx_ref[me]
        land_ref[me] = flat_ref[me]
        first = []
        for k in range(1, N_DEV):
            dev, pid = _peer(k, x, y, c)
            first.append(_remote(dmx_ref.at[pid], dmrecv_ref.at[me], send_sems.at[0, k], recv_sems.at[0, k], dev))
            first.append(_remote(flat_ref.at[pid], land_ref.at[me], send_sems.at[1, k], recv_sems.at[1, k], dev))
        for cp in first:
            cp.start()
        for k in range(1, N_DEV):
            dev, pid = _peer(k, x, y, c)
            _remote(dmx_ref.at[pid], dmrecv_ref.at[pid], send_sems.at[0, k], recv_sems.at[0, k], dev).wait_recv()
            _remote(flat_ref.at[pid], land_ref.at[pid], send_sems.at[1, k], recv_sems.at[1, k], dev).wait_recv()
        total = land_ref[0]
        for s in range(1, N_DEV):
            total = total + land_ref[s]
        red_ref[me] = total
        second = []
        for k in range(1, N_DEV):
            dev, _ = _peer(k, x, y, c)
            second.append(_remote(red_ref.at[me], red_ref.at[me], send_sems.at[2, k], recv_sems.at[2, k], dev))
        for cp in second:
            cp.start()
        for k in range(1, N_DEV):
            dev, pid = _peer(k, x, y, c)
            _remote(red_ref.at[pid], red_ref.at[pid], send_sems.at[2, k], recv_sems.at[2, k], dev).wait_recv()
        for cp in first + second:
            cp.wait_send()

    return pl.pallas_call(
        body,
        name="small_exchange",
        in_specs=[VMEM, VMEM],
        out_specs=[VMEM, VMEM],
        out_shape=[jax.ShapeDtypeStruct(dmx.shape, F32), jax.ShapeDtypeStruct(flat.shape, F32)],
        scratch_shapes=[
            pltpu.VMEM(flat.shape, F32),
            pltpu.SemaphoreType.DMA((3, N_DEV)),
            pltpu.SemaphoreType.DMA((3, N_DEV)),
        ],
        compiler_params=pltpu.CompilerParams(vmem_limit_bytes=VMEM_LIMIT_BYTES),
    )(dmx, flat)


HBM = pl.BlockSpec(memory_space=pltpu.HBM)
SEM = pl.BlockSpec(memory_space=pltpu.SEMAPHORE)
EFFECT = pltpu.SideEffectType.DATAFLOW_SIDE_EFFECTING


REGROUP_ROWS = 256


def shards_to_columns(x, *, name):
    p, k, n = x.shape

    def body(x_ref, o_ref):
        for s in range(p):
            o_ref[:, s * n : (s + 1) * n] = x_ref[s]

    return pl.pallas_call(
        body,
        name=name,
        grid=(k // REGROUP_ROWS,),
        in_specs=[pl.BlockSpec((p, REGROUP_ROWS, n), lambda i: (0, i, 0))],
        out_specs=pl.BlockSpec((REGROUP_ROWS, p * n), lambda i: (i, 0)),
        out_shape=jax.ShapeDtypeStruct((k, p * n), x.dtype),
        compiler_params=_params(("parallel",)),
    )(x)


def columns_to_shards(xs, *, name):
    k = xs[0].shape[0]
    widths = [x.shape[1] for x in xs]
    n = sum(widths) // N_DEV
    pieces = []
    for s in range(N_DEV):
        start = 0
        for i, w in enumerate(widths):
            lo, hi = max(start, s * n), min(start + w, (s + 1) * n)
            if lo < hi:
                pieces.append((i, lo - start, s, lo - s * n, hi - lo))
            start += w

    def body(*refs):
        x_refs, o_ref = refs[: len(xs)], refs[-1]
        for i, c0, s, d0, w in pieces:
            o_ref[s, :, d0 : d0 + w] = x_refs[i][:, c0 : c0 + w]

    return pl.pallas_call(
        body,
        name=name,
        grid=(k // REGROUP_ROWS,),
        in_specs=[pl.BlockSpec((REGROUP_ROWS, w), lambda i: (i, 0)) for w in widths],
        out_specs=pl.BlockSpec((N_DEV, REGROUP_ROWS, n), lambda i: (0, i, 0)),
        out_shape=jax.ShapeDtypeStruct((N_DEV, k, n), xs[0].dtype),
        compiler_params=_params(("parallel",)),
    )(*xs)


def _own_slot(me, block):
    land = lax.empty((N_DEV, *block.shape), block.dtype)
    return lax.dynamic_update_slice_in_dim(land, block[None], me, axis=0)


N_CHIP_PEERS = 3


class Gather:
    def __init__(self, shards, lands, after, *, name):
        nt = len(shards)
        self.name = name

        def body(*refs):
            src_refs, land_refs = refs[:nt], refs[nt : 2 * nt]
            send_sems, recv_sems = refs[2 * nt + 1 : 3 * nt + 1], refs[3 * nt + 1 : 4 * nt + 1]
            token = refs[-1]
            x, y, c, me = _position()
            for t in range(nt):
                for k, dev in enumerate(self._targets(x, y, c)):
                    _remote(src_refs[t], land_refs[t].at[me], send_sems[t].at[k], recv_sems[t].at[k], dev).start()
            token[...] = jnp.zeros_like(token)

        outs = pl.pallas_call(
            body,
            name=name + "_start",
            in_specs=[HBM] * (2 * nt) + [ANY],
            out_specs=[SEM] * (2 * nt) + [HBM] * (2 * nt) + [VMEM],
            out_shape=[pltpu.SemaphoreType.DMA((1 + N_CHIP_PEERS,))] * (2 * nt)
            + [pltpu.HBM(a.shape, a.dtype) for a in (*shards, *lands)]
            + [jax.ShapeDtypeStruct((SUBLANES, LANES), F32)],
            input_output_aliases={i: 2 * nt + i for i in range(2 * nt)},
            compiler_params=pltpu.CompilerParams(has_side_effects=EFFECT),
        )(*[pltpu.with_memory_space_constraint(a, pltpu.HBM) for a in (*shards, *lands)], after)
        self.send_sems, self.recv_sems = list(outs[:nt]), list(outs[nt : 2 * nt])
        self.srcs, self.lands = list(outs[2 * nt : 3 * nt]), list(outs[3 * nt : 4 * nt])
        self.token = outs[-1]

    @staticmethod
    def _chips(x, y):
        return [(1 - x, y), (x, 1 - y), (1 - x, 1 - y)]

    @classmethod
    def _targets(cls, x, y, c):
        return [(x, y, 1 - c)] + [(*chip, c) for chip in cls._chips(x, y)]

    def zero(self):
        return self.token[0, 0]

    def wait(self, which, after, *, name):
        n = len(which)

        def slot(px, py, pc):
            return 4 * px + 2 * py + pc

        def pass_body(*refs):
            land_refs, recv_sems = refs[:n], refs[n : 2 * n]
            fwd_send, fwd_recv = refs[3 * n + 1 : 4 * n + 1], refs[4 * n + 1 : 5 * n + 1]
            x, y, c, _ = _position()
            for t in range(n):
                for j, chip in enumerate(self._chips(x, y)):
                    blk = land_refs[t].at[slot(*chip, c)]
                    _remote(blk, blk, fwd_send[t].at[j], recv_sems[t].at[1 + j], (*chip, c)).wait_recv()
                    _remote(blk, blk, fwd_send[t].at[j], fwd_recv[t].at[j], (x, y, 1 - c)).start()

        lands = [self.lands[t] for t in which]
        outs = pl.pallas_call(
            pass_body,
            name=name + "_pass",
            in_specs=[HBM] * n + [SEM] * n + [ANY],
            out_specs=[HBM] * n + [SEM] * (2 * n),
            out_shape=[pltpu.HBM(a.shape, a.dtype) for a in lands] + [pltpu.SemaphoreType.DMA((N_CHIP_PEERS,))] * (2 * n),
            input_output_aliases={i: i for i in range(n)},
            compiler_params=pltpu.CompilerParams(has_side_effects=EFFECT),
        )(*lands, *[self.recv_sems[t] for t in which], after)
        lands, fwd_send, fwd_recv = outs[:n], outs[n : 2 * n], outs[2 * n :]

        def wait_body(*refs):
            src_refs, land_refs = refs[:n], refs[n : 2 * n]
            send_sems, recv_sems = refs[2 * n : 3 * n], refs[3 * n : 4 * n]
            fwd_send, fwd_recv = refs[4 * n : 5 * n], refs[5 * n : 6 * n]
            x, y, c, me = _position()
            sibling = (x, y, 1 - c)
            for t in range(n):
                for k, dev in enumerate(self._targets(x, y, c)):
                    _remote(src_refs[t], land_refs[t].at[me], send_sems[t].at[k], recv_sems[t].at[k], dev).wait_send()
                blk = land_refs[t].at[slot(x, y, 1 - c)]
                _remote(blk, blk, send_sems[t].at[0], recv_sems[t].at[0], sibling).wait_recv()
                for j, chip in enumerate(self._chips(x, y)):
                    sent = land_refs[t].at[slot(*chip, c)]
                    _remote(sent, sent, fwd_send[t].at[j], fwd_recv[t].at[j], sibling).wait_send()
                    got = land_refs[t].at[slot(*chip, 1 - c)]
                    _remote(got, got, fwd_send[t].at[j], fwd_recv[t].at[j], sibling).wait_recv()

        srcs = [self.srcs[t] for t in which]
        outs = pl.pallas_call(
            wait_body,
            name=name,
            in_specs=[HBM] * (2 * n) + [SEM] * (4 * n),
            out_specs=[HBM] * (2 * n),
            out_shape=[pltpu.HBM(a.shape, a.dtype) for a in (*srcs, *lands)],
            input_output_aliases={i: i for i in range(2 * n)},
            compiler_params=pltpu.CompilerParams(has_side_effects=EFFECT),
        )(*srcs, *lands, *[self.send_sems[t] for t in which], *[self.recv_sems[t] for t in which], *fwd_send, *fwd_recv)
        return outs[n:]


class Scatter:
    def __init__(self, srcs, lands, after, *, name):
        self.name = name
        nt = self.nt = len(srcs)
        peers = N_DEV - 1

        def body(*refs):
            src_refs, land_refs = refs[:nt], refs[nt : 2 * nt]
            send_sems, recv_sems = refs[2 * nt + 1 : 3 * nt + 1], refs[3 * nt + 1 : 4 * nt + 1]
            token = refs[-1]
            x, y, c, me = _position()
            for t in range(nt):
                for k in range(1, N_DEV):
                    dev, pid = _peer(k, x, y, c)
                    src = src_refs[t].at[pid]
                    _remote(src, land_refs[t].at[me], send_sems[t].at[k - 1], recv_sems[t].at[k - 1], dev).start()
            token[...] = jnp.zeros_like(token)

        outs = pl.pallas_call(
            body,
            name=name + "_start",
            in_specs=[HBM] * (2 * nt) + [ANY],
            out_specs=[SEM] * (2 * nt) + [HBM] * (2 * nt) + [VMEM],
            out_shape=[pltpu.SemaphoreType.DMA((peers,))] * (2 * nt)
            + [pltpu.HBM(a.shape, a.dtype) for a in (*srcs, *lands)]
            + [jax.ShapeDtypeStruct((SUBLANES, LANES), F32)],
            input_output_aliases={i: 2 * nt + i for i in range(2 * nt)},
            compiler_params=pltpu.CompilerParams(has_side_effects=EFFECT),
        )(*[pltpu.with_memory_space_constraint(a, pltpu.HBM) for a in (*srcs, *lands)], after)
        self.send_sems, self.recv_sems = outs[:nt], outs[nt : 2 * nt]
        self.srcs, self.lands = outs[2 * nt : 3 * nt], outs[3 * nt : 4 * nt]
        self.token = outs[-1]

    def zero(self):
        return self.token[0, 0]

    def wait(self, which, after, *, name):
        n = len(which)

        def body(*refs):
            src_refs, land_refs = refs[:n], refs[n : 2 * n]
            send_sems, recv_sems = refs[2 * n : 3 * n], refs[3 * n : 4 * n]
            x, y, c, _ = _position()
            for t in range(n):
                for k in range(1, N_DEV):
                    dev, pid = _peer(k, x, y, c)
                    src = src_refs[t].at[pid]
                    cp = _remote(src, land_refs[t].at[pid], send_sems[t].at[k - 1], recv_sems[t].at[k - 1], dev)
                    cp.wait_send()
                    cp.wait_recv()

        srcs = [self.srcs[t] for t in which]
        lands = [self.lands[t] for t in which]
        outs = pl.pallas_call(
            body,
            name=name,
            in_specs=[HBM] * (2 * n) + [SEM] * (2 * n) + [ANY],
            out_specs=[HBM] * (2 * n),
            out_shape=[pltpu.HBM(a.shape, a.dtype) for a in (*srcs, *lands)],
            input_output_aliases={i: i for i in range(2 * n)},
            compiler_params=pltpu.CompilerParams(has_side_effects=EFFECT),
        )(*srcs, *lands, *[self.send_sems[t] for t in which], *[self.recv_sems[t] for t in which], after)
        return outs[n:]


def _adam_update(g, w, m, v):
    m2 = ADAM_B1 * m + (1.0 - ADAM_B1) * g
    v2 = ADAM_B2 * v + (1.0 - ADAM_B2) * jnp.square(g)
    m_hat = m2 / (1.0 - ADAM_B1**ADAM_STEP)
    v_hat = v2 / (1.0 - ADAM_B2**ADAM_STEP)
    delta = -ADAM_LR * (m_hat / (jnp.sqrt(v_hat) + ADAM_EPS) + ADAM_WD * w)
    return delta, m2, v2


def adamw(gparts, w, m, v, *, name):
    nl, r, c = w.shape
    p = gparts[0].shape[0]
    tr = r if r <= 256 else (256 if c <= D_MODEL else 128)
    ni = r // tr

    def body(*refs):
        g_refs = refs[:nl]
        w_ref, m_ref, v_ref, go_ref, d_ref, mo_ref, vo_ref = refs[nl:]
        for layer in range(nl):

            @pl.when(pl.program_id(0) == layer)
            def _(g_ref=g_refs[layer]):
                g = g_ref[0].astype(F32)
                for i in range(1, p):
                    g = g + g_ref[i].astype(F32)
                delta, m2, v2 = _adam_update(g, w_ref[...], m_ref[...], v_ref[...])
                go_ref[...] = g
                d_ref[...] = delta
                mo_ref[...] = m2
                vo_ref[...] = v2

    def parts_spec(layer):
        def index(l, i):
            return (0, jnp.where(l == layer, i, jnp.where(l < layer, 0, ni - 1)), 0)

        return pl.BlockSpec((p, tr, c), index)

    blk = pl.BlockSpec((None, tr, c), lambda l, i: (l, i, 0))
    return pl.pallas_call(
        body,
        name=name,
        grid=(nl, ni),
        in_specs=[*[parts_spec(layer) for layer in range(nl)], blk, blk, blk],
        out_specs=[blk] * 4,
        out_shape=[jax.ShapeDtypeStruct((nl, r, c), F32)] * 4,
        compiler_params=_params(("arbitrary", "arbitrary")),
    )(*gparts, w, m, v)


def ada_grad_adamw(cg, dmrecv, w4, m4, v4, *, name):
    nt, k, ncol = w4.shape

    def body(cg_ref, dm_ref, w_ref, m_ref, v_ref, go_ref, d_ref, mo_ref, vo_ref, gb_ref):
        sc = _silu_bf16(cg_ref[...].reshape(N_DEV * SUBLANES, k))
        dm = dm_ref[...].reshape(N_DEV * SUBLANES, ncol)
        g = lax.dot_general(sc, dm.astype(BF16), (((0,), (0,)), ((), ())), preferred_element_type=F32)
        delta, m2, v2 = _adam_update(g, w_ref[...], m_ref[...], v_ref[...])
        go_ref[...] = g
        d_ref[...] = delta
        mo_ref[...] = m2
        vo_ref[...] = v2
        gb_ref[...] = jnp.broadcast_to(jnp.sum(dm, axis=0, keepdims=True), (SUBLANES, ncol))

    wblk = pl.BlockSpec((None, k, ncol), lambda t: (t, 0, 0))
    return pl.pallas_call(
        body,
        name=name,
        grid=(nt,),
        in_specs=[
            pl.BlockSpec((N_DEV, SUBLANES, k), lambda t: (0, 0, 0)),
            pl.BlockSpec((N_DEV, None, SUBLANES, ncol), lambda t: (0, t, 0, 0)),
            wblk,
            wblk,
            wblk,
        ],
        out_specs=[wblk] * 4 + [pl.BlockSpec((None, SUBLANES, ncol), lambda t: (t, 0, 0))],
        out_shape=[jax.ShapeDtypeStruct((nt, k, ncol), F32)] * 4 + [jax.ShapeDtypeStruct((nt, SUBLANES, ncol), F32)],
        compiler_params=_params(("parallel",)),
    )(cg, dmrecv, w4, m4, v4)


def kernel(x, c, ada_w, ada_b, ln_g, ln_b, a_w_in, a_b_in, a_vn_g, a_vn_b, a_w_s, a_b_s, a_w_out, b_w_qkv, b_w_out, mlp_w_up, mlp_w_down, loss_target, m_ada_w, m_ada_b, m_ln_g, m_ln_b, m_a_w_in, m_a_b_in, m_a_vn_g, m_a_vn_b, m_a_w_s, m_a_b_s, m_a_w_out, m_b_w_qkv, m_b_w_out, m_mlp_w_up, m_mlp_w_down, v_ada_w, v_ada_b, v_ln_g, v_ln_b, v_a_w_in, v_a_b_in, v_a_vn_g, v_a_vn_b, v_a_w_s, v_a_b_s, v_a_w_out, v_b_w_qkv, v_b_w_out, v_mlp_w_up, v_mlp_w_down):
    x0 = x[0]
    target = loss_target[0]
    me = 4 * lax.axis_index("x") + 2 * lax.axis_index("y") + lax.axis_index("c")

    ada_w4 = ada_w.reshape(N_SUB, D_MODEL, -1)
    ada_b4 = ada_b.reshape(N_SUB, -1)
    ln8 = jnp.concatenate([ln_g.reshape(N_SUB, -1), ln_b.reshape(N_SUB, -1)], axis=0)
    c8 = jnp.broadcast_to(c, (SUBLANES, D_MODEL))
    cg, lng, mrecv = ada_exchange(c8, ada_w4, ada_b4, ln8)

    W_IN, W_AOUT, W_UP0, W_DN0, W_QKV, W_BOUT, W_UP1, W_DN1 = range(8)
    shards = [
        a_w_in[0].astype(BF16),
        a_w_out[0].astype(BF16),
        mlp_w_up[0].astype(BF16),
        mlp_w_down[0].astype(BF16),
        b_w_qkv[0].astype(BF16),
        b_w_out[0].astype(BF16),
        mlp_w_up[1].astype(BF16),
        mlp_w_down[1].astype(BF16),
    ]
    gather = Gather(shards, [_own_slot(me, s) for s in shards], mrecv, name="gather")

    modv = mrecv[:, :, 0, :].transpose(1, 0, 2).reshape(N_SUB, 3 * D_MODEL) + gather.zero()
    shift = [modv[t : t + 1, :D_MODEL] for t in range(N_SUB)]
    scale = [modv[t : t + 1, D_MODEL : 2 * D_MODEL] for t in range(N_SUB)]
    gate1 = [1.0 + modv[t : t + 1, 2 * D_MODEL :] for t in range(N_SUB)]
    lng_full = [lng[:, t, :].reshape(1, D_MODEL) for t in range(N_SUB)]
    lnb_full = [lng[:, N_SUB + t, :].reshape(1, D_MODEL) for t in range(N_SUB)]

    ident = lambda acc: (acc,)
    relu2 = lambda a: jnp.square(jnp.maximum(a.astype(F32), 0.0)).astype(BF16)
    vn_g, vn_b, w_s = a_vn_g, a_vn_b, a_w_s[0]
    bias_full = jnp.repeat(a_b_s[0].T, A_GROUP_DIM, axis=1)
    w_up3, w_dn3 = [None, None], [None, None]

    def mlp_forward(i, h, after):
        up, dn = gather.wait([W_UP0, W_DN0] if i == 0 else [W_UP1, W_DN1], after, name=f"gather_wait_mlp{i}")
        w_up3[i], w_dn3[i] = up, dn.reshape(1, D_FF, D_MODEL)
        (a,) = mm_nn(h, w_up3[i], name=f"mlp{i}_up", tm=2048, ps=2, tn=512, tk=D_MODEL, epilogue=ident, outs=(BF16,))
        (y,) = mm_nn(
            a, w_dn3[i], name=f"mlp{i}_down", tm=2048, ps=1, tn=D_MODEL, tk=1024, prologue=relu2, epilogue=ident, outs=(BF16,)
        )
        return a, y

    h0 = modulate(x0, scale[0], shift[0], name="modulate0")
    w_in3, w_aout3 = gather.wait([W_IN, W_AOUT], h0, name="gather_wait_a")
    w_aout3 = w_aout3.reshape(1, D_MODEL, D_MODEL)
    (a_pre,) = mm_nn(
        h0, w_in3, name="a_in", tm=2048, ps=4, tn=256, tk=D_MODEL, epilogue=lambda acc, b: (acc + b,),
        extras=[(a_b_in, "row")], outs=(BF16,),
    )
    p_gate = gate_fwd(a_pre, vn_g, vn_b, w_s, bias_full, name="gate_fwd")
    (y0,) = mm_nn(p_gate, w_aout3, name="a_out", tm=2048, ps=1, tn=D_MODEL, tk=D_MODEL, epilogue=ident, outs=(BF16,))
    x1, h1 = residual_ln(x0, y0, gate1[0], lng_full[0], lnb_full[0], scale[1], shift[1], name="res_ln0")
    a1, y1 = mlp_forward(0, h1, y0)
    x2, h2 = residual_ln(x1, y1, gate1[1], lng_full[1], lnb_full[1], scale[2], shift[2], name="res_ln1")
    w_qkv_shards, w_bout3 = gather.wait([W_QKV, W_BOUT], y1, name="gather_wait_b")
    w_bout3 = w_bout3.reshape(1, D_MODEL, D_MODEL)
    w_qkv3 = shards_to_columns(w_qkv_shards, name="w_qkv_columns")[None]
    pat_tiles = 3
    dil = [d for _, d in B_PATTERNS]
    qkv_p, pat_o, pat_lse = [], [], []
    for g in range(N_PAT):
        (qkv_g,) = mm_nn(
            h2, w_qkv3, name=f"b_qkv{g}", tm=2048, ps=1, tn=D_MODEL, tk=D_MODEL, epilogue=ident, outs=(BF16,),
            b_tile0=pat_tiles * g, b_tiles=pat_tiles, out_streams=dil[g],
        )
        o_g, lse_g = attn_fwd(qkv_g, g, name=f"attn_fwd{g}")
        if g > 0:
            o_g = permute_rows(o_g, dil[g], inverse=True, name=f"unperm_o{g}")
            lse_g = permute_rows(lse_g, dil[g], inverse=True, name=f"unperm_lse{g}")
        qkv_p.append(qkv_g)
        pat_o.append(o_g)
        pat_lse.append(lse_g)
    o_b, o_f, lse = attn_combine(pat_o, pat_lse, name="attn_combine")
    (y2,) = mm_nn(o_b, w_bout3, name="b_out", tm=2048, ps=1, tn=D_MODEL, tk=D_MODEL, epilogue=ident, outs=(BF16,))
    x3, h3 = residual_ln(x2, y2, gate1[2], lng_full[2], lnb_full[2], scale[3], shift[3], name="res_ln2")
    a3, y3 = mlp_forward(1, h3, y2)
    dxo, loss_local = residual_ln_loss(x3, y3, gate1[3], lng_full[3], lnb_full[3], target, name="res_ln3_loss")

    def scatter(parts, after, name):
        parts = [p.reshape(N_DEV, -1, p.shape[-1]) for p in parts]
        lands = [_own_slot(me, lax.dynamic_index_in_dim(p, me, 0, keepdims=False)) for p in parts]
        return Scatter(parts, lands, after, name=name)

    def mlp_backward(i, h, a, dy):
        (da,) = mm_nt(
            dy,
            w_dn3[i],
            name=f"mlp{i}_da",
            tm=2048,
            tko=1024,
            ps=1,
            tc=D_MODEL,
            epilogue=lambda acc, act: (acc * (2.0 * jnp.maximum(act.astype(F32), 0.0)),),
            extras=[(a, "full")],
            outs=(BF16,),
        )
        (dh,) = mm_nt(da, w_up3[i], name=f"mlp{i}_dh", tm=2048, tko=1024, ps=2, tc=512, epilogue=ident, outs=(F32,))
        dw_dn = mm_tn(
            a, dy, name=f"mlp{i}_dw_down", p=1, tk=1024, ps=1, tn=D_MODEL, tmc=2048, prologue=relu2, out_dtype=BF16
        )
        dw_up = mm_tn(h, da, name=f"mlp{i}_dw_up", p=N_DEV, tk=1024, ps=2, tn=512, tmc=2048, out_dtype=BF16)
        return scatter([dw_up, dw_dn], dh, f"scatter_mlp{i}"), dh

    dz3, dy3, st3 = residual_ln_bwd(name="res_bwd3", dxo=dxo, this=(x3, y3, gate1[3], lng_full[3], lnb_full[3]))
    rs_mlp1, dh3 = mlp_backward(1, h3, a3, dy3)
    dz2, dy2, st2 = residual_ln_bwd(
        name="res_bwd2", later=(dh3, dz3, scale[3] + rs_mlp1.zero()), this=(x2, y2, gate1[2], lng_full[2], lnb_full[2])
    )
    (d_o,) = mm_nt(dy2, w_bout3, name="b_do", tm=2048, tko=1024, ps=1, tc=D_MODEL, epilogue=ident, outs=(F32,))
    do_b, delta = attn_delta(d_o, o_f, name="attn_delta")
    dh2, dw_pat = None, []
    for g in range(N_PAT):
        do_g, lse_g, delta_g = do_b, lse, delta
        if g > 0:
            do_g = permute_rows(do_b, dil[g], inverse=False, name=f"perm_do{g}")
            lse_g = permute_rows(lse, dil[g], inverse=False, name=f"perm_lse{g}")
            delta_g = permute_rows(delta, dil[g], inverse=False, name=f"perm_delta{g}")
        dqkv_g = attn_bwd(qkv_p[g], do_g, lse_g, delta_g, g, name=f"attn_bwd{g}")
        (dh2,) = mm_nt(
            dqkv_g, w_qkv3, name=f"b_dh{g}", tm=2048, tko=512, ps=1, tc=D_MODEL, outs=(F32,),
            b_tile0=pat_tiles * g, g_streams=dil[g],
            epilogue=ident if g == 0 else (lambda acc, prev, d=dil[g]: (_from_streams(acc, d) + prev,)),
            extras=[] if g == 0 else [(dh2, "full")],
        )
        dw_pat.append(
            mm_tn(
                h2, dqkv_g, name=f"b_dw_qkv{g}", p=1, tk=1024, ps=1, tn=D_MODEL, tmc=2048, out_dtype=BF16,
                g_streams=dil[g],
            )[0]
        )
    dw_bout = mm_tn(o_b, dy2, name="b_dw_out", p=1, tk=1024, ps=1, tn=D_MODEL, tmc=2048, out_dtype=BF16)
    dw_qkv = columns_to_shards(dw_pat, name="dw_qkv_shards")
    rs_b = scatter([dw_qkv, dw_bout], dh2, "scatter_b")
    dz1, dy1, st1 = residual_ln_bwd(
        name="res_bwd1", later=(dh2, dz2, scale[2] + rs_b.zero()), this=(x1, y1, gate1[1], lng_full[1], lnb_full[1])
    )
    rs_mlp0, dh1 = mlp_backward(0, h1, a1, dy1)
    dz0, dy0, st0 = residual_ln_bwd(
        name="res_bwd0", later=(dh1, dz1, scale[1] + rs_mlp0.zero()), this=(x0, y0, gate1[0], lng_full[0], lnb_full[0])
    )
    (dp_gate,) = mm_nt(dy0, w_aout3, name="a_dp", tm=2048, tko=1024, ps=1, tc=D_MODEL, epilogue=ident, outs=(F32,))
    dw_aout = mm_tn(p_gate, dy0, name="a_dw_out", p=1, tk=1024, ps=1, tn=D_MODEL, tmc=2048, out_dtype=BF16)
    rs_aout = scatter([dw_aout], dp_gate, "scatter_a_out")
    da0, d_ws, d_bs, gate_rows = gate_bwd(a_pre, dp_gate, vn_g + rs_aout.zero(), vn_b, w_s, bias_full, name="gate_bwd")
    dw_in = mm_tn(h0, da0, name="a_dw_in", p=N_DEV, tk=1024, ps=4, tn=256, tmc=2048, out_dtype=BF16)
    rs_in = scatter([dw_in], d_ws, "scatter_a_in")
    (dh0,) = mm_nt(da0, w_in3, name="a_dh", tm=2048, tko=1024, ps=4, tc=256, epilogue=ident, outs=(F32,))
    grad_x, stf = residual_ln_bwd(name="res_bwd_in", later=(dh0, dz0, scale[0] + rs_in.zero()), x_out=x0)

    stats_after = [stf, st0, st1, st2]
    stats_own = [st0, st1, st2, st3]
    dm = jnp.stack(
        [
            jnp.concatenate(
                [stats_after[t][ST_DSHIFT], stats_after[t][ST_DSCALE], stats_own[t][ST_DGATE]], axis=0
            )
            for t in range(N_SUB)
        ]
    )
    ncol = 3 * D_MODEL // N_DEV
    dmx = jnp.pad(
        dm.reshape(N_SUB, N_DEV, ncol).transpose(1, 0, 2)[:, :, None, :], ((0, 0), (0, 0), (0, SUBLANES - 1), (0, 0))
    )
    small = [
        gate_rows[0],
        gate_rows[1],
        d_ws.reshape(-1),
        d_bs[:, :A_GROUPS].T.reshape(-1),
        *[stats_own[t][ST_DG] for t in range(N_SUB)],
        *[stats_own[t][ST_DB] for t in range(N_SUB)],
        jnp.pad(loss_local.reshape(1), (0, LANES - 1)),
    ]
    n_small = sum(s.size for s in small)
    part_rows = -(-n_small // (N_DEV * LANES * SUBLANES)) * SUBLANES
    flat = jnp.concatenate(small + [jnp.zeros((N_DEV * part_rows * LANES - n_small,), F32)])
    dmrecv, reduced = small_exchange(dmx, flat.reshape(N_DEV, part_rows, LANES))
    reduced = reduced.reshape(-1)
    sizes = [2 * D_MODEL, D_MODEL, D_MODEL, A_GROUPS * CHUNK * CHUNK, A_GROUPS * CHUNK, N_SUB * D_MODEL, N_SUB * D_MODEL]
    offs = [sum(sizes[:i]) for i in range(len(sizes) + 1)]
    g_b_in, g_vn_g, g_vn_b, g_ws, g_bs, g_lng, g_lnb = [reduced[offs[i] : offs[i + 1]] for i in range(len(sizes))]
    loss = reduced[offs[-1]]

    results = {}

    def update(wname, gparts, w, m, v):
        shape = w.shape
        layers = len(gparts) if isinstance(gparts, list) else 1
        w3 = w.reshape(layers, -1, shape[-1])
        parts = [g.reshape(g.shape[0], *w3.shape[1:]) for g in (gparts if layers > 1 else [gparts])]
        outs = adamw(parts, w3, m.reshape(w3.shape), v.reshape(w3.shape), name=f"adamw_{wname}")
        results[wname] = [o.reshape(shape) for o in outs]

    ada_outs = ada_grad_adamw(cg, dmrecv, ada_w4, m_ada_w.reshape(ada_w4.shape), v_ada_w.reshape(ada_w4.shape), name="ada_grad_adamw")
    results["ada_w"] = [o.reshape(ada_w.shape) for o in ada_outs[:4]]
    update("ada_b", ada_outs[4][:, 0, :][None], ada_b, m_ada_b, v_ada_b)
    ln_cols = D_MODEL // N_DEV
    my_ln = lambda gfull: lax.dynamic_slice_in_dim(gfull.reshape(N_SUB, N_DEV, ln_cols), me, 1, axis=1)
    update("ln_g", my_ln(g_lng).reshape(1, N_SUB, ln_cols), ln_g, m_ln_g, v_ln_g)
    update("ln_b", my_ln(g_lnb).reshape(1, N_SUB, ln_cols), ln_b, m_ln_b, v_ln_b)
    update("a_b_in", g_b_in[None], a_b_in, m_a_b_in, v_a_b_in)
    update("a_vn_g", g_vn_g[None], a_vn_g, m_a_vn_g, v_a_vn_g)
    update("a_vn_b", g_vn_b[None], a_vn_b, m_a_vn_b, v_a_vn_b)
    update("a_w_s", g_ws[None], a_w_s, m_a_w_s, v_a_w_s)
    update("a_b_s", g_bs[None], a_b_s, m_a_b_s, v_a_b_s)
    g_up1, g_dn1 = rs_mlp1.wait([0, 1], grad_x, name="scatter_wait_mlp1")
    g_qkv, g_bout = rs_b.wait([0, 1], grad_x, name="scatter_wait_b")
    update("b_w_qkv", g_qkv, b_w_qkv, m_b_w_qkv, v_b_w_qkv)
    update("b_w_out", g_bout, b_w_out, m_b_w_out, v_b_w_out)
    g_up0, g_dn0 = rs_mlp0.wait([0, 1], grad_x, name="scatter_wait_mlp0")
    update("mlp_w_up", [g_up0, g_up1], mlp_w_up, m_mlp_w_up, v_mlp_w_up)
    update("mlp_w_down", [g_dn0, g_dn1], mlp_w_down, m_mlp_w_down, v_mlp_w_down)
    (g_aout,) = rs_aout.wait([0], grad_x, name="scatter_wait_a_out")
    (g_in,) = rs_in.wait([0], grad_x, name="scatter_wait_a_in")
    update("a_w_in", g_in, a_w_in, m_a_w_in, v_a_w_in)
    update("a_w_out", g_aout, a_w_out, m_a_w_out, v_a_w_out)

    order = ["ada_w", "ada_b", "ln_g", "ln_b", "a_w_in", "a_b_in", "a_vn_g", "a_vn_b", "a_w_s", "a_b_s", "a_w_out", "b_w_qkv", "b_w_out", "mlp_w_up", "mlp_w_down"]
    return (loss, grad_x[None], *[results[n][0] for n in order], *[results[n][1] for n in order],
            *[results[n][2] for n in order], *[results[n][3] for n in order])
```

```python
import math

import jax
import jax.numpy as jnp
from jax import lax
from jax.experimental import pallas as pl
from jax.experimental.pallas import tpu as pltpu

F32 = jnp.float32
BF16 = jnp.bfloat16
MESH = pl.DeviceIdType.MESH
ANY = pl.BlockSpec(memory_space=pl.ANY)
VMEM = pl.BlockSpec(memory_space=pltpu.VMEM)

N_DEV = 8
D_MODEL = 1024
SEQ = 4096
DEPTH = 2
CHUNK = 128
A_GROUPS = 16
A_GROUP_DIM = D_MODEL // A_GROUPS
B_HEADS = 16
B_HEAD_DIM = 64
B_PATTERNS = ((128, 1), (512, 4), (2048, 16))
N_PAT = len(B_PATTERNS)
SPAN = 128
D_FF = 4 * D_MODEL
D_QKV = N_PAT * 3 * D_MODEL
ALPHA = (2 * DEPTH) ** 0.25
LN_EPS = 1e-5
NEG = -1e30
ADAM_LR = 0.001
ADAM_B1 = 0.9
ADAM_B2 = 0.999
ADAM_EPS = 1e-08
ADAM_WD = 0.01
ADAM_STEP = 10
GELU_C = math.sqrt(2.0 / math.pi)
GELU_A = 0.044715

VMEM_LIMIT_BYTES = 56 * 1024 * 1024
LANES = 128
SUBLANES = 8
ROW_TILE = 512
N_SUB = 2 * DEPTH


def _params(sem):
    return pltpu.CompilerParams(dimension_semantics=sem, vmem_limit_bytes=VMEM_LIMIT_BYTES)


def _lane(shape):
    return lax.broadcasted_iota(jnp.int32, shape, len(shape) - 1)


def _split_bf16(x):
    hi = x.astype(BF16)
    lo = (x - hi.astype(F32)).astype(BF16)
    return hi, lo


def _group_expand_matrix(groups_padded, width):
    per = width // A_GROUPS
    r = lax.broadcasted_iota(jnp.int32, (groups_padded, width), 0)
    c = lax.broadcasted_iota(jnp.int32, (groups_padded, width), 1)
    return (c // per == r).astype(BF16)


def _group_reduce_matrix(width, groups_padded):
    per = width // A_GROUPS
    r = lax.broadcasted_iota(jnp.int32, (width, groups_padded), 0)
    c = lax.broadcasted_iota(jnp.int32, (width, groups_padded), 1)
    return (r // per == c).astype(BF16)


def _expand_groups(w):
    e = _group_expand_matrix(LANES, D_MODEL)
    hi, lo = _split_bf16(w)
    return jnp.dot(hi, e, preferred_element_type=F32) + jnp.dot(lo, e, preferred_element_type=F32)


def _reduce_groups(x):
    e = _group_reduce_matrix(D_MODEL, LANES)
    hi, lo = _split_bf16(x)
    return jnp.dot(hi, e, preferred_element_type=F32) + jnp.dot(lo, e, preferred_element_type=F32)


def _to_streams(x, d):
    rows, w = x.shape
    return jnp.swapaxes(x.reshape(rows // d, d, w), 0, 1).reshape(rows, w)


def _from_streams(x, d):
    rows, w = x.shape
    return jnp.swapaxes(x.reshape(d, rows // d, w), 0, 1).reshape(rows, w)


def _column_tiles(p, n, ps, tn):
    assert (ps == 1 or tn == n) and p % ps == 0 and n % tn == 0
    q = n // tn
    return (p // ps) * q, q


def _extra_specs(extras, tm, width):
    specs = []
    for _, kind in extras:
        if kind == "row":
            specs.append(pl.BlockSpec((1, width), lambda i, j, c: (0, j)))
        else:
            specs.append(pl.BlockSpec((tm, width), lambda i, j, c: (i, j)))
    return specs


def mm_nn(a, b3, *, name, tm, ps, tn, tk, epilogue, extras=(), outs, prologue=None, b_tile0=0, b_tiles=None, out_streams=1):
    m, k = a.shape
    p, _, n = b3.shape
    nj, q = _column_tiles(p, n, ps, tn)
    nj = nj if b_tiles is None else b_tiles
    nk = k // tk
    width = ps * tn
    d = out_streams
    assert d == 1 or not extras

    def body(a_ref, b_ref, *rest):
        ex = rest[: len(extras)]
        out_refs = rest[len(extras) : len(extras) + len(outs)]
        kk = pl.program_id(2)
        av = a_ref[...] if prologue is None else prologue(a_ref[...])
        if d > 1:
            av = _to_streams(av, d)

        def finish(cs, acc):
            res = epilogue(acc, *[e[:, cs] for e in ex])
            for o_ref, r in zip(out_refs, res, strict=True):
                if d > 1:
                    o_ref[:, :, cs] = r.astype(o_ref.dtype).reshape(d, tm // d, tn)
                else:
                    o_ref[:, cs] = r.astype(o_ref.dtype)

        for s in range(ps):
            cs = slice(s * tn, (s + 1) * tn)
            part = jnp.dot(av, b_ref[s], preferred_element_type=F32)
            if nk == 1:
                finish(cs, part)
                continue
            acc_ref = rest[-1]

            @pl.when(kk == 0)
            def _(part=part, cs=cs):
                acc_ref[:, cs] = part

            @pl.when(kk > 0)
            def _(part=part, cs=cs):
                acc_ref[:, cs] += part

        if nk > 1:

            @pl.when(kk == nk - 1)
            def _():
                for s in range(ps):
                    cs = slice(s * tn, (s + 1) * tn)
                    finish(cs, rest[-1][:, cs])

    if d > 1:
        out_spec = pl.BlockSpec((d, tm // d, width), lambda i, j, kk: (0, i, j))
        out_shape = (d, m // d, nj * width)
    else:
        out_spec = pl.BlockSpec((tm, width), lambda i, j, kk: (i, j))
        out_shape = (m, nj * width)
    res = pl.pallas_call(
        body,
        name=name,
        grid=(m // tm, nj, nk),
        in_specs=[
            pl.BlockSpec((tm, tk), lambda i, j, kk: (i, kk)),
            pl.BlockSpec((ps, tk, tn), lambda i, j, kk: ((j + b_tile0) // q, kk, (j + b_tile0) % q)),
            *_extra_specs(extras, tm, width),
        ],
        out_specs=[out_spec for _ in outs],
        out_shape=[jax.ShapeDtypeStruct(out_shape, dt) for dt in outs],
        scratch_shapes=[pltpu.VMEM((tm, width), F32)] if nk > 1 else [],
        compiler_params=_params(("parallel", "parallel", "arbitrary")),
    )(a, b3, *[arr for arr, _ in extras])
    return [r.reshape(m, nj * width) for r in res]


def mm_nt(g, b3, *, name, tm, tko, ps, tc, epilogue, extras=(), outs, b_tile0=0, g_streams=1):
    m, width = g.shape
    p, k, n = b3.shape
    _, q = _column_tiles(p, n, ps, tc)
    nc = width // (ps * tc)
    ds = g_streams

    def body(g_ref, b_ref, *rest):
        ex = rest[: len(extras)]
        out_refs = rest[len(extras) : len(extras) + len(outs)]
        c = pl.program_id(2)
        gv = g_ref[...].reshape(tm, ps * tc) if ds > 1 else g_ref[...]
        part = None
        for s in range(ps):
            d = lax.dot_general(gv[:, s * tc : (s + 1) * tc], b_ref[s], _NT, preferred_element_type=F32)
            part = d if part is None else part + d

        def finish(acc):
            res = epilogue(acc, *[e[...] for e in ex])
            for o_ref, r in zip(out_refs, res, strict=True):
                o_ref[...] = r.astype(o_ref.dtype)

        if nc == 1:
            finish(part)
            return
        acc_ref = rest[-1]

        @pl.when(c == 0)
        def _():
            acc_ref[...] = part

        @pl.when(c > 0)
        def _():
            acc_ref[...] += part

        @pl.when(c == nc - 1)
        def _():
            finish(acc_ref[...])

    return pl.pallas_call(
        body,
        name=name,
        grid=(m // tm, k // tko, nc),
        in_specs=[
            pl.BlockSpec((ds, tm // ds, ps * tc), lambda i, j, c: (0, i, c))
            if ds > 1
            else pl.BlockSpec((tm, ps * tc), lambda i, j, c: (i, c)),
            pl.BlockSpec((ps, tko, tc), lambda i, j, c: ((c + b_tile0) // q, j, (c + b_tile0) % q)),
            *_extra_specs(extras, tm, tko),
        ],
        out_specs=[pl.BlockSpec((tm, tko), lambda i, j, c: (i, j)) for _ in outs],
        out_shape=[jax.ShapeDtypeStruct((m, k), dt) for dt in outs],
        scratch_shapes=[pltpu.VMEM((tm, tko), F32)] if nc > 1 else [],
        compiler_params=_params(("parallel", "parallel", "arbitrary")),
    )(g.reshape(ds, m // ds, width) if ds > 1 else g, b3, *[arr for arr, _ in extras])


def mm_tn(a, g, *, name, p, tk, ps, tn, tmc, out_dtype, prologue=None, g_streams=1):
    m, k = a.shape
    width = g.shape[1]
    n = width // p
    nj, q = _column_tiles(p, n, ps, tn)
    nc = m // tmc
    ds = g_streams

    def body(a_ref, g_ref, o_ref, acc_ref):
        c = pl.program_id(2)
        av = a_ref[...] if prologue is None else prologue(a_ref[...])
        gv = g_ref[...]
        if ds > 1:
            av, gv = _to_streams(av, ds), gv.reshape(tmc, ps * tn)
        part = lax.dot_general(av, gv, (((0,), (0,)), ((), ())), preferred_element_type=F32)

        @pl.when(c == 0)
        def _():
            acc_ref[...] = part

        @pl.when(c > 0)
        def _():
            acc_ref[...] += part

        @pl.when(c == nc - 1)
        def _():
            for s in range(ps):
                o_ref[s] = acc_ref[:, s * tn : (s + 1) * tn].astype(o_ref.dtype)

    return pl.pallas_call(
        body,
        name=name,
        grid=(k // tk, nj, nc),
        in_specs=[
            pl.BlockSpec((tmc, tk), lambda i, j, c: (c, i)),
            pl.BlockSpec((ds, tmc // ds, ps * tn), lambda i, j, c: (0, c, j))
            if ds > 1
            else pl.BlockSpec((tmc, ps * tn), lambda i, j, c: (c, j)),
        ],
        out_specs=pl.BlockSpec((ps, tk, tn), lambda i, j, c: (j // q, i, j % q)),
        out_shape=jax.ShapeDtypeStruct((p, k, n), out_dtype),
        scratch_shapes=[pltpu.VMEM((tk, ps * tn), F32)],
        compiler_params=_params(("parallel", "parallel", "arbitrary")),
    )(a, g.reshape(ds, m // ds, width) if ds > 1 else g)


def _rows(cols):
    return pl.BlockSpec((ROW_TILE, cols), lambda i: (i, 0))


def _vec(cols, rows=1):
    return pl.BlockSpec((rows, cols), lambda i: (0, 0))


def _layer_norm_hat(z):
    mu = jnp.mean(z, axis=-1, keepdims=True)
    zc = z - mu
    var = jnp.mean(zc * zc, axis=-1, keepdims=True)
    rstd = lax.rsqrt(var + LN_EPS)
    return zc * rstd, rstd


def modulate(x, scale, shift, *, name):
    s, d = x.shape

    def body(x_ref, sc_ref, sh_ref, h_ref):
        h_ref[...] = (x_ref[...] * (1.0 + sc_ref[...]) + sh_ref[...]).astype(BF16)

    return pl.pallas_call(
        body,
        name=name,
        grid=(s // ROW_TILE,),
        in_specs=[_rows(d), _vec(d), _vec(d)],
        out_specs=_rows(d),
        out_shape=jax.ShapeDtypeStruct((s, d), BF16),
        compiler_params=_params(("parallel",)),
    )(x, scale, shift)


def residual_ln(x, y, gate1, g, b, nscale, nshift, *, name):
    s, d = x.shape

    def body(x_ref, y_ref, gt_ref, g_ref, b_ref, sc_ref, sh_ref, xn_ref, hn_ref):
        z = ALPHA * x_ref[...] + gt_ref[...] * y_ref[...].astype(F32)
        xhat, _ = _layer_norm_hat(z)
        xn = xhat * g_ref[...] + b_ref[...]
        xn_ref[...] = xn
        hn_ref[...] = (xn * (1.0 + sc_ref[...]) + sh_ref[...]).astype(BF16)

    return pl.pallas_call(
        body,
        name=name,
        grid=(s // ROW_TILE,),
        in_specs=[_rows(d), _rows(d), _vec(d), _vec(d), _vec(d), _vec(d), _vec(d)],
        out_specs=[_rows(d), _rows(d)],
        out_shape=[jax.ShapeDtypeStruct((s, d), F32), jax.ShapeDtypeStruct((s, d), BF16)],
        compiler_params=_params(("parallel",)),
    )(x, y, gate1, g, b, nscale, nshift)


def residual_ln_loss(x, y, gate1, g, b, target, *, name):
    s, d = x.shape

    def body(x_ref, y_ref, gt_ref, g_ref, b_ref, t_ref, dx_ref, loss_ref):
        z = ALPHA * x_ref[...] + gt_ref[...] * y_ref[...].astype(F32)
        xhat, _ = _layer_norm_hat(z)
        err = xhat * g_ref[...] + b_ref[...] - t_ref[...]
        dx_ref[...] = err * (1.0 / d)
        part = jnp.sum(jnp.sum(err * err, axis=1, keepdims=True), axis=0, keepdims=True) * (0.5 / d)

        @pl.when(pl.program_id(0) == 0)
        def _():
            loss_ref[...] = part

        @pl.when(pl.program_id(0) > 0)
        def _():
            loss_ref[...] += part

    return pl.pallas_call(
        body,
        name=name,
        grid=(s // ROW_TILE,),
        in_specs=[_rows(d), _rows(d), _vec(d), _vec(d), _vec(d), _rows(d)],
        out_specs=[_rows(d), pl.BlockSpec((1, 1), lambda i: (0, 0))],
        out_shape=[jax.ShapeDtypeStruct((s, d), F32), jax.ShapeDtypeStruct((1, 1), F32)],
        compiler_params=_params(("arbitrary",)),
    )(x, y, gate1, g, b, target)


ST_DSCALE, ST_DSHIFT, ST_DG, ST_DB, ST_DGATE = 0, 1, 2, 3, 4


def residual_ln_bwd(*, name, later=None, dxo=None, x_out=None, this=None):
    lead = later[0] if later is not None else dxo
    s, d = lead.shape
    has_later, has_ln = later is not None, this is not None

    def body(*refs):
        refs = list(refs)
        if has_later:
            dh_ref, dzl_ref, scl_ref = refs[:3]
            refs = refs[3:]
        else:
            dxo_ref = refs.pop(0)
        if has_ln:
            x_ref, y_ref, gt_ref, g_ref, b_ref = refs[:5]
            refs = refs[5:]
            dz_ref, dy_ref, st_ref = refs
        else:
            xo_ref, dx_ref, st_ref = refs

        @pl.when(pl.program_id(0) == 0)
        def _():
            st_ref[...] = jnp.zeros_like(st_ref)

        def acc(row, val):
            st_ref[row : row + 1, :] += jnp.sum(val, axis=0, keepdims=True)

        if has_ln:
            y = y_ref[...].astype(F32)
            gate1 = gt_ref[...]
            xhat, rstd = _layer_norm_hat(ALPHA * x_ref[...] + gate1 * y)
            x_out_v = xhat * g_ref[...] + b_ref[...]
        else:
            x_out_v = xo_ref[...]
        if has_later:
            dh = dh_ref[...]
            g_out = ALPHA * dzl_ref[...] + dh * (1.0 + scl_ref[...])
            acc(ST_DSCALE, dh * x_out_v)
            acc(ST_DSHIFT, dh)
        else:
            g_out = dxo_ref[...]
        if not has_ln:
            dx_ref[...] = g_out
            return
        acc(ST_DG, g_out * xhat)
        acc(ST_DB, g_out)
        dxh = g_out * g_ref[...]
        m1 = jnp.mean(dxh, axis=-1, keepdims=True)
        m2 = jnp.mean(dxh * xhat, axis=-1, keepdims=True)
        dz = rstd * (dxh - m1 - xhat * m2)
        acc(ST_DGATE, dz * y)
        dz_ref[...] = dz
        dy_ref[...] = (dz * gate1).astype(BF16)

    ins, specs = [], []
    if has_later:
        ins += list(later)
        specs += [_rows(d), _rows(d), _vec(d)]
    else:
        ins += [dxo]
        specs += [_rows(d)]
    if not has_ln:
        ins += [x_out]
        specs += [_rows(d)]
    if has_ln:
        ins += list(this)
        specs += [_rows(d), _rows(d), _vec(d), _vec(d), _vec(d)]
        out_specs = [_rows(d), _rows(d), _vec(d, SUBLANES)]
        out_shape = [
            jax.ShapeDtypeStruct((s, d), F32),
            jax.ShapeDtypeStruct((s, d), BF16),
            jax.ShapeDtypeStruct((SUBLANES, d), F32),
        ]
    else:
        out_specs = [_rows(d), _vec(d, SUBLANES)]
        out_shape = [jax.ShapeDtypeStruct((s, d), F32), jax.ShapeDtypeStruct((SUBLANES, d), F32)]
    return pl.pallas_call(
        body,
        name=name,
        grid=(s // ROW_TILE,),
        in_specs=specs,
        out_specs=out_specs,
        out_shape=out_shape,
        compiler_params=_params(("arbitrary",)),
    )(*ins)


GATE_CHUNKS = 4


def _gelu(x):
    return 0.5 * x * (1.0 + jnp.tanh(GELU_C * (x + GELU_A * x * x * x)))


def _gelu_grad(x):
    t = jnp.tanh(GELU_C * (x + GELU_A * x * x * x))
    return 0.5 * (1.0 + t) + 0.5 * x * (1.0 - t * t) * (GELU_C * (1.0 + 3.0 * GELU_A * x * x))


def _causal_weights(w_ref, transpose):
    t = lax.broadcasted_iota(jnp.int32, (CHUNK, CHUNK), 0)
    s = lax.broadcasted_iota(jnp.int32, (CHUNK, CHUNK), 1)
    out = []
    for g in range(A_GROUPS):
        w = jnp.where(t >= s, w_ref[g], 0.0)
        out.append((w.T if transpose else w).astype(BF16))
    return out


def _spatial(ws, vn, lo_mask):
    rows = vn.shape[0]
    out_rows = []
    for r in range(rows // CHUNK):
        cols = []
        for j in range(A_GROUPS // 2):
            blk = vn[r * CHUNK : (r + 1) * CHUNK, j * LANES : (j + 1) * LANES]
            za = jnp.dot(ws[2 * j], blk, preferred_element_type=F32)
            zb = jnp.dot(ws[2 * j + 1], blk, preferred_element_type=F32)
            cols.append(jnp.where(lo_mask, za, zb))
        out_rows.append(jnp.concatenate(cols, axis=1))
    return jnp.concatenate(out_rows, axis=0)


def _gate_forward(a, vg, vb, ws, bias, lo_mask):
    u = _gelu(a[:, :D_MODEL])
    v = _gelu(a[:, D_MODEL:])
    vhat, rstd = _layer_norm_hat(v)
    vn = (vhat * vg + vb).astype(BF16)
    z = _spatial(ws, vn, lo_mask) + jnp.concatenate([bias] * (a.shape[0] // CHUNK), axis=0)
    return u, vhat, rstd, vn, z


def gate_fwd(a_pre, vn_g, vn_b, w_s, bias_full, *, name):
    s = a_pre.shape[0]
    tr = GATE_CHUNKS * CHUNK

    def body(a_ref, vg_ref, vb_ref, w_ref, bias_ref, p_ref):
        lo_mask = _lane((CHUNK, LANES)) < A_GROUP_DIM
        ws = _causal_weights(w_ref, transpose=False)
        u, _, _, _, z = _gate_forward(a_ref[...].astype(F32), vg_ref[...], vb_ref[...], ws, bias_ref[...], lo_mask)
        p_ref[...] = (u * z).astype(BF16)

    return pl.pallas_call(
        body,
        name=name,
        grid=(s // tr,),
        in_specs=[
            pl.BlockSpec((tr, 2 * D_MODEL), lambda i: (i, 0)),
            _vec(D_MODEL),
            _vec(D_MODEL),
            pl.BlockSpec((A_GROUPS, CHUNK, CHUNK), lambda i: (0, 0, 0)),
            _vec(D_MODEL, CHUNK),
        ],
        out_specs=pl.BlockSpec((tr, D_MODEL), lambda i: (i, 0)),
        out_shape=jax.ShapeDtypeStruct((s, D_MODEL), BF16),
        compiler_params=_params(("parallel",)),
    )(a_pre, vn_g, vn_b, w_s, bias_full)


def gate_bwd(a_pre, dp, vn_g, vn_b, w_s, bias_full, *, name):
    s = a_pre.shape[0]
    tr = GATE_CHUNKS * CHUNK
    nsteps = s // tr

    def body(a_ref, dp_ref, vg_ref, vb_ref, w_ref, bias_ref, da_ref, dw_ref, dbs_ref, rows_ref, dbias_acc):
        step = pl.program_id(0)
        lo_mask = _lane((CHUNK, LANES)) < A_GROUP_DIM

        @pl.when(step == 0)
        def _():
            dw_ref[...] = jnp.zeros_like(dw_ref)
            rows_ref[...] = jnp.zeros_like(rows_ref)
            dbias_acc[...] = jnp.zeros_like(dbias_acc)

        a = a_ref[...].astype(F32)
        vg = vg_ref[...]
        ws = _causal_weights(w_ref, transpose=False)
        wts = _causal_weights(w_ref, transpose=True)
        u, vhat, rstd, vn, z = _gate_forward(a, vg, vb_ref[...], ws, bias_ref[...], lo_mask)
        dp = dp_ref[...]
        du = dp * z
        dzz = dp * u
        dzz_b = dzz.astype(BF16)
        dvn = _spatial(wts, dzz_b, lo_mask)
        dbias = None
        for r in range(GATE_CHUNKS):
            rs = slice(r * CHUNK, (r + 1) * CHUNK)
            dbias = dzz[rs] if dbias is None else dbias + dzz[rs]
            for j in range(A_GROUPS // 2):
                cs = slice(j * LANES, (j + 1) * LANES)
                dblk = dzz[rs, cs]
                vblk = vn[rs, cs]
                for half in range(2):
                    keep = lo_mask if half == 0 else jnp.logical_not(lo_mask)
                    dm = jnp.where(keep, dblk, 0.0).astype(BF16)
                    dw_ref[2 * j + half] += lax.dot_general(
                        dm, vblk, (((1,), (1,)), ((), ())), preferred_element_type=F32
                    )
        dbias_acc[...] += dbias
        rows_ref[1:2, :D_MODEL] += jnp.sum(dvn * vhat, axis=0, keepdims=True)
        rows_ref[1:2, D_MODEL:] += jnp.sum(dvn, axis=0, keepdims=True)
        dvh = dvn * vg
        m1 = jnp.mean(dvh, axis=-1, keepdims=True)
        m2 = jnp.mean(dvh * vhat, axis=-1, keepdims=True)
        dv = rstd * (dvh - m1 - vhat * m2)
        da_u = du * _gelu_grad(a[:, :D_MODEL])
        da_v = dv * _gelu_grad(a[:, D_MODEL:])
        da_ref[:, :D_MODEL] = da_u.astype(BF16)
        da_ref[:, D_MODEL:] = da_v.astype(BF16)
        rows_ref[0:1, :D_MODEL] += jnp.sum(da_u, axis=0, keepdims=True)
        rows_ref[0:1, D_MODEL:] += jnp.sum(da_v, axis=0, keepdims=True)

        @pl.when(step == nsteps - 1)
        def _():
            t = lax.broadcasted_iota(jnp.int32, (CHUNK, CHUNK), 0)
            sx = lax.broadcasted_iota(jnp.int32, (CHUNK, CHUNK), 1)
            for g in range(A_GROUPS):
                dw_ref[g] = jnp.where(t >= sx, dw_ref[g], 0.0)
            dbs_ref[...] = _reduce_groups(dbias_acc[...])

    return pl.pallas_call(
        body,
        name=name,
        grid=(nsteps,),
        in_specs=[
            pl.BlockSpec((tr, 2 * D_MODEL), lambda i: (i, 0)),
            pl.BlockSpec((tr, D_MODEL), lambda i: (i, 0)),
            _vec(D_MODEL),
            _vec(D_MODEL),
            pl.BlockSpec((A_GROUPS, CHUNK, CHUNK), lambda i: (0, 0, 0)),
            _vec(D_MODEL, CHUNK),
        ],
        out_specs=[
            pl.BlockSpec((tr, 2 * D_MODEL), lambda i: (i, 0)),
            pl.BlockSpec((A_GROUPS, CHUNK, CHUNK), lambda i: (0, 0, 0)),
            _vec(LANES, CHUNK),
            _vec(2 * D_MODEL, SUBLANES),
        ],
        out_shape=[
            jax.ShapeDtypeStruct((s, 2 * D_MODEL), BF16),
            jax.ShapeDtypeStruct((A_GROUPS, CHUNK, CHUNK), F32),
            jax.ShapeDtypeStruct((CHUNK, LANES), F32),
            jax.ShapeDtypeStruct((SUBLANES, 2 * D_MODEL), F32),
        ],
        scratch_shapes=[pltpu.VMEM((CHUNK, D_MODEL), F32)],
        compiler_params=_params(("arbitrary",)),
    )(a_pre, dp, vn_g, vn_b, w_s, bias_full)


def alibi_tables(dilation):
    qi = jnp.arange(SPAN)[:, None]
    ki = jnp.arange(2 * SPAN)[None, :]
    diff = SPAN + qi - ki
    valid = (diff >= 0) & (diff <= SPAN)
    heads = jnp.arange(1, B_HEADS + 1, dtype=F32)
    slopes = jnp.exp2(-8.0 * heads / B_HEADS)
    bias = -slopes[:, None, None] * (dilation * diff).astype(F32)
    bias = jnp.where(valid[None], bias, NEG).reshape(B_HEADS // 2, 2 * SPAN, 2 * SPAN)
    return bias, bias.transpose(0, 2, 1)


def _pair_rows(x, halves):
    return jnp.concatenate([x * halves[0], x * halves[1]], axis=0)


def _pair_column(v, lane, j):
    pick = lambda h: jnp.sum(jnp.where(lane == h, v, 0.0), axis=1, keepdims=True)
    return jnp.concatenate([pick(2 * j), pick(2 * j + 1)], axis=0)


_NT = (((1,), (1,)), ((), ()))


def permute_rows(x, dilation, *, inverse, name, add=None):
    s, w = x.shape
    tile = SPAN * dilation
    nat = pl.BlockSpec((tile, w), lambda i: (i, 0))
    streams = pl.BlockSpec((dilation, SPAN, w), lambda i: (0, i, 0))
    x3 = x.reshape(dilation, s // dilation, w) if inverse else x

    def body(*refs):
        if not inverse:
            x_ref, o_ref = refs
            o_ref[...] = jnp.swapaxes(x_ref[...].reshape(SPAN, dilation, w), 0, 1)
            return
        val = jnp.swapaxes(refs[0][...], 0, 1).reshape(tile, w)
        if add is not None:
            val = val + refs[1][...]
        refs[-1][...] = val

    out = pl.pallas_call(
        body,
        name=name,
        grid=(s // tile,),
        in_specs=([streams] + ([nat] if add is not None else [])) if inverse else [nat],
        out_specs=nat if inverse else streams,
        out_shape=jax.ShapeDtypeStruct((s, w) if inverse else (dilation, s // dilation, w), x.dtype),
        compiler_params=_params(("parallel",)),
    )(*([x3] + ([add] if add is not None else [])))
    return out.reshape(s, w)


def _qkv_specs(block_of):
    def spec(which, prev):
        def index(*grid):
            blk = block_of(*grid)
            return (jnp.maximum(blk - 1, 0) if prev else blk, which)

        return pl.BlockSpec((SPAN, D_MODEL), index)

    return [spec(0, False), spec(1, True), spec(1, False), spec(2, True), spec(2, False)]


def attn_fwd(qkv_p, pat, *, name):
    _, dilation = B_PATTERNS[pat]
    nb = SEQ // dilation // SPAN
    bias, _ = alibi_tables(dilation)

    def body(q_ref, kp_ref, kc_ref, vp_ref, vc_ref, bias_ref, o_ref, lse_ref):
        n = pl.program_id(1)
        lane = _lane((SPAN, LANES))
        lo_mask = lane < B_HEAD_DIM
        first_prev = jnp.logical_and(n == 0, _lane((2 * SPAN, 2 * SPAN)) < SPAN)
        q = q_ref[...] * jnp.asarray(B_HEAD_DIM**-0.5, BF16)
        kk = jnp.concatenate([kp_ref[...], kc_ref[...]], axis=0)
        vv = jnp.concatenate([vp_ref[...], vc_ref[...]], axis=0)
        halves = (lo_mask.astype(BF16), jnp.logical_not(lo_mask).astype(BF16))
        stats = jnp.zeros((SPAN, LANES), F32)
        for j in range(B_HEADS // 2):
            cs = slice(j * LANES, (j + 1) * LANES)
            sc = lax.dot_general(_pair_rows(q[:, cs], halves), kk[:, cs], _NT, preferred_element_type=F32)
            sc = jnp.where(first_prev, NEG, sc + bias_ref[j])
            m = jnp.max(sc, axis=1, keepdims=True)
            p = jnp.exp(sc - m)
            l = jnp.sum(p, axis=1, keepdims=True)
            acc = jnp.dot(p.astype(BF16), vv[:, cs], preferred_element_type=F32) * (1.0 / l)
            lse_pair = m + jnp.log(l)
            o_ref[:, cs] = jnp.where(lo_mask, acc[:SPAN], acc[SPAN:]).astype(BF16)
            stats = jnp.where(lane == 2 * j, lse_pair[:SPAN], stats)
            stats = jnp.where(lane == 2 * j + 1, lse_pair[SPAN:], stats)
        lse_ref[...] = stats

    return pl.pallas_call(
        body,
        name=name,
        grid=(dilation, nb),
        in_specs=[
            *_qkv_specs(lambda r, n: r * nb + n),
            pl.BlockSpec((B_HEADS // 2, 2 * SPAN, 2 * SPAN), lambda r, n: (0, 0, 0)),
        ],
        out_specs=[
            pl.BlockSpec((SPAN, D_MODEL), lambda r, n: (r * nb + n, 0)),
            pl.BlockSpec((SPAN, LANES), lambda r, n: (r * nb + n, 0)),
        ],
        out_shape=[jax.ShapeDtypeStruct((SEQ, D_MODEL), BF16), jax.ShapeDtypeStruct((SEQ, LANES), F32)],
        compiler_params=_params(("parallel", "arbitrary")),
    )(qkv_p, qkv_p, qkv_p, qkv_p, qkv_p, bias)


def attn_combine(outs, lses, *, name):
    def body(o0, o1, o2, l0, l1, l2, ob_ref, of_ref, lse_ref):
        ls = [l0[...], l1[...], l2[...]]
        m = jnp.maximum(jnp.maximum(ls[0], ls[1]), ls[2])
        tot = jnp.log(jnp.exp(ls[0] - m) + jnp.exp(ls[1] - m) + jnp.exp(ls[2] - m)) + m
        o = None
        for o_ref, l in zip((o0, o1, o2), ls, strict=True):
            term = _expand_groups(jnp.exp(l - tot)) * o_ref[...]
            o = term if o is None else o + term
        ob_ref[...] = o.astype(BF16)
        of_ref[...] = o
        lse_ref[...] = tot

    return pl.pallas_call(
        body,
        name=name,
        grid=(SEQ // ROW_TILE,),
        in_specs=[_rows(D_MODEL)] * 3 + [_rows(LANES)] * 3,
        out_specs=[_rows(D_MODEL), _rows(D_MODEL), _rows(LANES)],
        out_shape=[
            jax.ShapeDtypeStruct((SEQ, D_MODEL), BF16),
            jax.ShapeDtypeStruct((SEQ, D_MODEL), F32),
            jax.ShapeDtypeStruct((SEQ, LANES), F32),
        ],
        compiler_params=_params(("parallel",)),
    )(*outs, *lses)


def attn_delta(do, o, *, name):
    def body(do_ref, o_ref, dob_ref, dl_ref):
        do_v = do_ref[...]
        dob_ref[...] = do_v.astype(BF16)
        dl_ref[...] = _reduce_groups(do_v * o_ref[...])

    return pl.pallas_call(
        body,
        name=name,
        grid=(SEQ // ROW_TILE,),
        in_specs=[_rows(D_MODEL), _rows(D_MODEL)],
        out_specs=[_rows(D_MODEL), _rows(LANES)],
        out_shape=[jax.ShapeDtypeStruct((SEQ, D_MODEL), BF16), jax.ShapeDtypeStruct((SEQ, LANES), F32)],
        compiler_params=_params(("parallel",)),
    )(do, o)


def attn_bwd(qkv_p, do_p, lse_p, delta_p, pat, *, name):
    _, dilation = B_PATTERNS[pat]
    nb = SEQ // dilation // SPAN
    n_blocks = SEQ // SPAN
    bias, bias_t = alibi_tables(dilation)
    last = n_blocks - 1
    q_cols, k_cols, v_cols = (slice(i * D_MODEL, (i + 1) * D_MODEL) for i in range(3))

    def body(q_ref, kp_ref, kc_ref, vp_ref, vc_ref, do_ref, lse_ref, dl_ref, bias_ref, biast_ref, out_ref, cq_ref, ck_ref, cv_ref):
        g = pl.program_id(0)

        @pl.when(g == n_blocks)
        def _():
            out_ref[:, q_cols] = cq_ref[...].astype(BF16)
            out_ref[:, k_cols] = ck_ref[...].astype(BF16)
            out_ref[:, v_cols] = cv_ref[...].astype(BF16)

        @pl.when(g == 0)
        def _():
            cq_ref[...] = jnp.zeros_like(cq_ref)
            ck_ref[...] = jnp.zeros_like(ck_ref)
            cv_ref[...] = jnp.zeros_like(cv_ref)

        @pl.when(g < n_blocks)
        def _():
            lane = _lane((SPAN, LANES))
            lo_mask = lane < B_HEAD_DIM
            pair = (2 * SPAN, 2 * SPAN)
            first = lax.rem(g, nb) == 0
            prev_key_cols = jnp.logical_and(first, _lane(pair) < SPAN)
            prev_key_rows = jnp.logical_and(first, lax.broadcasted_iota(jnp.int32, pair, 0) < SPAN)
            q = q_ref[...] * jnp.asarray(B_HEAD_DIM**-0.5, BF16)
            kk = jnp.concatenate([kp_ref[...], kc_ref[...]], axis=0)
            vv = jnp.concatenate([vp_ref[...], vc_ref[...]], axis=0)
            do_v = do_ref[...]
            lse_v = lse_ref[...]
            dl_v = dl_ref[...]
            lse_t = lse_v.T
            dl_t = dl_v.T
            halves = (lo_mask.astype(BF16), jnp.logical_not(lo_mask).astype(BF16))
            for j in range(B_HEADS // 2):
                cs = slice(j * LANES, (j + 1) * LANES)
                kp, vp = kk[:, cs], vv[:, cs]
                q2 = _pair_rows(q[:, cs], halves)
                do2 = _pair_rows(do_v[:, cs], halves)
                lse_c, dl_c = _pair_column(lse_v, lane, j), _pair_column(dl_v, lane, j)
                lse_r = jnp.concatenate([lse_t[2 * j : 2 * j + 1], lse_t[2 * j + 1 : 2 * j + 2]], axis=1)
                dl_r = jnp.concatenate([dl_t[2 * j : 2 * j + 1], dl_t[2 * j + 1 : 2 * j + 2]], axis=1)
                sc = lax.dot_general(q2, kp, _NT, preferred_element_type=F32)
                p = jnp.exp(jnp.where(prev_key_cols, NEG, sc + bias_ref[j]) - lse_c)
                dp = lax.dot_general(do2, vp, _NT, preferred_element_type=F32)
                ds = (p * (dp - dl_c)).astype(BF16)
                dq2 = jnp.dot(ds, kp, preferred_element_type=F32)
                sc_t = lax.dot_general(kp, q2, _NT, preferred_element_type=F32)
                p_t = jnp.exp(jnp.where(prev_key_rows, NEG, sc_t + biast_ref[j]) - lse_r)
                dp_t = lax.dot_general(vp, do2, _NT, preferred_element_type=F32)
                ds_t = (p_t * (dp_t - dl_r)).astype(BF16)
                dk_pair = jnp.dot(ds_t, q2, preferred_element_type=F32)
                dv_pair = jnp.dot(p_t.astype(BF16), do2, preferred_element_type=F32)
                oq = slice(j * LANES, (j + 1) * LANES)
                ok = slice(D_MODEL + j * LANES, D_MODEL + (j + 1) * LANES)
                ov = slice(2 * D_MODEL + j * LANES, 2 * D_MODEL + (j + 1) * LANES)
                out_ref[:, oq] = cq_ref[:, cs].astype(BF16)
                out_ref[:, ok] = (ck_ref[:, cs] + dk_pair[:SPAN]).astype(BF16)
                out_ref[:, ov] = (cv_ref[:, cs] + dv_pair[:SPAN]).astype(BF16)
                cq_ref[:, cs] = jnp.where(lo_mask, dq2[:SPAN], dq2[SPAN:]) * (B_HEAD_DIM**-0.5)
                ck_ref[:, cs] = dk_pair[SPAN:]
                cv_ref[:, cs] = dv_pair[SPAN:]

    def block_of(g):
        return jnp.minimum(g, last)

    def row_spec(width):
        return pl.BlockSpec((SPAN, width), lambda g: (block_of(g), 0))

    return pl.pallas_call(
        body,
        name=name,
        grid=(n_blocks + 1,),
        in_specs=[
            *_qkv_specs(block_of),
            row_spec(D_MODEL),
            row_spec(LANES),
            row_spec(LANES),
            pl.BlockSpec((B_HEADS // 2, 2 * SPAN, 2 * SPAN), lambda g: (0, 0, 0)),
            pl.BlockSpec((B_HEADS // 2, 2 * SPAN, 2 * SPAN), lambda g: (0, 0, 0)),
        ],
        out_specs=pl.BlockSpec((SPAN, 3 * D_MODEL), lambda g: (jnp.maximum(g - 1, 0), 0)),
        out_shape=jax.ShapeDtypeStruct((SEQ, 3 * D_MODEL), BF16),
        scratch_shapes=[pltpu.VMEM((SPAN, D_MODEL), F32)] * 3,
        compiler_params=_params(("arbitrary",)),
    )(qkv_p, qkv_p, qkv_p, qkv_p, qkv_p, do_p, lse_p, delta_p, bias, bias_t)


def _position():
    x, y, c = lax.axis_index("x"), lax.axis_index("y"), lax.axis_index("c")
    return x, y, c, 4 * x + 2 * y + c


def _peer(k, x, y, c):
    px = 1 - x if k & 4 else x
    py = 1 - y if k & 2 else y
    pc = 1 - c if k & 1 else c
    return (px, py, pc), 4 * px + 2 * py + pc


def _remote(src, dst, send_sem, recv_sem, device):
    return pltpu.make_async_remote_copy(
        src_ref=src, dst_ref=dst, send_sem=send_sem, recv_sem=recv_sem, device_id=device, device_id_type=MESH
    )


def _silu_bf16(cf):
    return (cf * (1.0 / (1.0 + jnp.exp(-cf)))).astype(BF16)


def ada_exchange(c8, w4, b4, ln8):
    nt, _, ncol = w4.shape

    def body(c8_ref, w_ref, b_ref, ln_ref, cg_ref, lng_ref, mrecv_ref, mloc_ref, send_sems, recv_sems):
        x, y, c, me = _position()
        cg_ref[me] = c8_ref[...]
        lng_ref[me] = ln_ref[...]
        first = []
        for k in range(1, N_DEV):
            dev, _ = _peer(k, x, y, c)
            first.append(_remote(c8_ref, cg_ref.at[me], send_sems.at[0, k], recv_sems.at[0, k], dev))
            first.append(_remote(ln_ref, lng_ref.at[me], send_sems.at[1, k], recv_sems.at[1, k], dev))
        for cp in first:
            cp.start()
        for k in range(1, N_DEV):
            dev, pid = _peer(k, x, y, c)
            _remote(c8_ref, cg_ref.at[pid], send_sems.at[0, k], recv_sems.at[0, k], dev).wait_recv()
            _remote(ln_ref, lng_ref.at[pid], send_sems.at[1, k], recv_sems.at[1, k], dev).wait_recv()
        sc = _silu_bf16(cg_ref[...].reshape(N_DEV * SUBLANES, D_MODEL))
        for t in range(nt):
            mloc_ref[t] = jnp.dot(sc, w_ref[t].astype(BF16), preferred_element_type=F32) + b_ref[t : t + 1, :]

        def group(dev_id):
            return pl.ds(pl.multiple_of(dev_id * SUBLANES, SUBLANES), SUBLANES)

        mrecv_ref[me] = mloc_ref[:, group(me), :]
        second = []
        for k in range(1, N_DEV):
            dev, pid = _peer(k, x, y, c)
            second.append(
                _remote(mloc_ref.at[:, group(pid), :], mrecv_ref.at[me], send_sems.at[2, k], recv_sems.at[2, k], dev)
            )
        for cp in second:
            cp.start()
        for k in range(1, N_DEV):
            dev, pid = _peer(k, x, y, c)
            _remote(
                mloc_ref.at[:, group(pid), :], mrecv_ref.at[pid], send_sems.at[2, k], recv_sems.at[2, k], dev
            ).wait_recv()
        for cp in first + second:
            cp.wait_send()

    return pl.pallas_call(
        body,
        name="ada_exchange",
        in_specs=[VMEM, VMEM, VMEM, VMEM],
        out_specs=[VMEM, VMEM, VMEM],
        out_shape=[
            jax.ShapeDtypeStruct((N_DEV, SUBLANES, D_MODEL), F32),
            jax.ShapeDtypeStruct((N_DEV, SUBLANES, LANES), F32),
            jax.ShapeDtypeStruct((N_DEV, nt, SUBLANES, ncol), F32),
        ],
        scratch_shapes=[
            pltpu.VMEM((nt, N_DEV * SUBLANES, ncol), F32),
            pltpu.SemaphoreType.DMA((3, N_DEV)),
            pltpu.SemaphoreType.DMA((3, N_DEV)),
        ],
        compiler_params=pltpu.CompilerParams(vmem_limit_bytes=VMEM_LIMIT_BYTES),
    )(c8, w4, b4, ln8)


def small_exchange(dmx, flat):
    def body(dmx_ref, flat_ref, dmrecv_ref, red_ref, land_ref, send_sems, recv_sems):
        x, y, c, me = _position()
        dmrecv_ref[me] = dmx_ref[me]
        land_ref[me] = flat_ref[me]
        first = []
        for k in range(1, N_DEV):
            dev, pid = _peer(k, x, y, c)
            first.append(_remote(dmx_ref.at[pid], dmrecv_ref.at[me], send_sems.at[0, k], recv_sems.at[0, k], dev))
            first.append(_remote(flat_ref.at[pid], land_ref.at[me], send_sems.at[1, k], recv_sems.at[1, k], dev))
        for cp in first:
            cp.start()
        for k in range(1, N_DEV):
            dev, pid = _peer(k, x, y, c)
            _remote(dmx_ref.at[pid], dmrecv_ref.at[pid], send_sems.at[0, k], recv_sems.at[0, k], dev).wait_recv()
            _remote(flat_ref.at[pid], land_ref.at[pid], send_sems.at[1, k], recv_sems.at[1, k], dev).wait_recv()
        total = land_ref[0]
        for s in range(1, N_DEV):
            total = total + land_ref[s]
        red_ref[me] = total
        second = []
        for k in range(1, N_DEV):
            dev, _ = _peer(k, x, y, c)
            second.append(_remote(red_ref.at[me], red_ref.at[me], send_sems.at[2, k], recv_sems.at[2, k], dev))
        for cp in second:
            cp.start()
        for k in range(1, N_DEV):
            dev, pid = _peer(k, x, y, c)
            _remote(red_ref.at[pid], red_ref.at[pid], send_sems.at[2, k], recv_sems.at[2, k], dev).wait_recv()
        for cp in first + second:
            cp.wait_send()

    return pl.pallas_call(
        body,
        name="small_exchange",
        in_specs=[VMEM, VMEM],
        out_specs=[VMEM, VMEM],
        out_shape=[jax.ShapeDtypeStruct(dmx.shape, F32), jax.ShapeDtypeStruct(flat.shape, F32)],
        scratch_shapes=[
            pltpu.VMEM(flat.shape, F32),
            pltpu.SemaphoreType.DMA((3, N_DEV)),
            pltpu.SemaphoreType.DMA((3, N_DEV)),
        ],
        compiler_params=pltpu.CompilerParams(vmem_limit_bytes=VMEM_LIMIT_BYTES),
    )(dmx, flat)


HBM = pl.BlockSpec(memory_space=pltpu.HBM)
SEM = pl.BlockSpec(memory_space=pltpu.SEMAPHORE)
EFFECT = pltpu.SideEffectType.DATAFLOW_SIDE_EFFECTING


REGROUP_ROWS = 256


def shards_to_columns(x, *, name):
    p, k, n = x.shape

    def body(x_ref, o_ref):
        for s in range(p):
            o_ref[:, s * n : (s + 1) * n] = x_ref[s]

    return pl.pallas_call(
        body,
        name=name,
        grid=(k // REGROUP_ROWS,),
        in_specs=[pl.BlockSpec((p, REGROUP_ROWS, n), lambda i: (0, i, 0))],
        out_specs=pl.BlockSpec((REGROUP_ROWS, p * n), lambda i: (i, 0)),
        out_shape=jax.ShapeDtypeStruct((k, p * n), x.dtype),
        compiler_params=_params(("parallel",)),
    )(x)


def columns_to_shards(xs, *, name):
    k = xs[0].shape[0]
    widths = [x.shape[1] for x in xs]
    n = sum(widths) // N_DEV
    pieces = []
    for s in range(N_DEV):
        start = 0
        for i, w in enumerate(widths):
            lo, hi = max(start, s * n), min(start + w, (s + 1) * n)
            if lo < hi:
                pieces.append((i, lo - start, s, lo - s * n, hi - lo))
            start += w

    def body(*refs):
        x_refs, o_ref = refs[: len(xs)], refs[-1]
        for i, c0, s, d0, w in pieces:
            o_ref[s, :, d0 : d0 + w] = x_refs[i][:, c0 : c0 + w]

    return pl.pallas_call(
        body,
        name=name,
        grid=(k // REGROUP_ROWS,),
        in_specs=[pl.BlockSpec((REGROUP_ROWS, w), lambda i: (i, 0)) for w in widths],
        out_specs=pl.BlockSpec((N_DEV, REGROUP_ROWS, n), lambda i: (0, i, 0)),
        out_shape=jax.ShapeDtypeStruct((N_DEV, k, n), xs[0].dtype),
        compiler_params=_params(("parallel",)),
    )(*xs)


def _own_slot(me, block):
    land = lax.empty((N_DEV, *block.shape), block.dtype)
    return lax.dynamic_update_slice_in_dim(land, block[None], me, axis=0)


N_CHIP_PEERS = 3


class Gather:
    def __init__(self, shards, lands, after, *, name):
        nt = len(shards)
        self.name = name

        def body(*refs):
            src_refs, land_refs = refs[:nt], refs[nt : 2 * nt]
            send_sems, recv_sems = refs[2 * nt + 1 : 3 * nt + 1], refs[3 * nt + 1 : 4 * nt + 1]
            token = refs[-1]
            x, y, c, me = _position()
            for t in range(nt):
                for k, dev in enumerate(self._targets(x, y, c)):
                    _remote(src_refs[t], land_refs[t].at[me], send_sems[t].at[k], recv_sems[t].at[k], dev).start()
            token[...] = jnp.zeros_like(token)

        outs = pl.pallas_call(
            body,
            name=name + "_start",
            in_specs=[HBM] * (2 * nt) + [ANY],
            out_specs=[SEM] * (2 * nt) + [HBM] * (2 * nt) + [VMEM],
            out_shape=[pltpu.SemaphoreType.DMA((1 + N_CHIP_PEERS,))] * (2 * nt)
            + [pltpu.HBM(a.shape, a.dtype) for a in (*shards, *lands)]
            + [jax.ShapeDtypeStruct((SUBLANES, LANES), F32)],
            input_output_aliases={i: 2 * nt + i for i in range(2 * nt)},
            compiler_params=pltpu.CompilerParams(has_side_effects=EFFECT),
        )(*[pltpu.with_memory_space_constraint(a, pltpu.HBM) for a in (*shards, *lands)], after)
        self.send_sems, self.recv_sems = list(outs[:nt]), list(outs[nt : 2 * nt])
        self.srcs, self.lands = list(outs[2 * nt : 3 * nt]), list(outs[3 * nt : 4 * nt])
        self.token = outs[-1]

    @staticmethod
    def _chips(x, y):
        return [(1 - x, y), (x, 1 - y), (1 - x, 1 - y)]

    @classmethod
    def _targets(cls, x, y, c):
        return [(x, y, 1 - c)] + [(*chip, c) for chip in cls._chips(x, y)]

    def zero(self):
        return self.token[0, 0]

    def wait(self, which, after, *, name):
        n = len(which)

        def slot(px, py, pc):
            return 4 * px + 2 * py + pc

        def pass_body(*refs):
            land_refs, recv_sems = refs[:n], refs[n : 2 * n]
            fwd_send, fwd_recv = refs[3 * n + 1 : 4 * n + 1], refs[4 * n + 1 : 5 * n + 1]
            x, y, c, _ = _position()
            for t in range(n):
                for j, chip in enumerate(self._chips(x, y)):
                    blk = land_refs[t].at[slot(*chip, c)]
                    _remote(blk, blk, fwd_send[t].at[j], recv_sems[t].at[1 + j], (*chip, c)).wait_recv()
                    _remote(blk, blk, fwd_send[t].at[j], fwd_recv[t].at[j], (x, y, 1 - c)).start()

        lands = [self.lands[t] for t in which]
        outs = pl.pallas_call(
            pass_body,
            name=name + "_pass",
            in_specs=[HBM] * n + [SEM] * n + [ANY],
            out_specs=[HBM] * n + [SEM] * (2 * n),
            out_shape=[pltpu.HBM(a.shape, a.dtype) for a in lands] + [pltpu.SemaphoreType.DMA((N_CHIP_PEERS,))] * (2 * n),
            input_output_aliases={i: i for i in range(n)},
            compiler_params=pltpu.CompilerParams(has_side_effects=EFFECT),
        )(*lands, *[self.recv_sems[t] for t in which], after)
        lands, fwd_send, fwd_recv = outs[:n], outs[n : 2 * n], outs[2 * n :]

        def wait_body(*refs):
            src_refs, land_refs = refs[:n], refs[n : 2 * n]
            send_sems, recv_sems = refs[2 * n : 3 * n], refs[3 * n : 4 * n]
            fwd_send, fwd_recv = refs[4 * n : 5 * n], refs[5 * n : 6 * n]
            x, y, c, me = _position()
            sibling = (x, y, 1 - c)
            for t in range(n):
                for k, dev in enumerate(self._targets(x, y, c)):
                    _remote(src_refs[t], land_refs[t].at[me], send_sems[t].at[k], recv_sems[t].at[k], dev).wait_send()
                blk = land_refs[t].at[slot(x, y, 1 - c)]
                _remote(blk, blk, send_sems[t].at[0], recv_sems[t].at[0], sibling).wait_recv()
                for j, chip in enumerate(self._chips(x, y)):
                    sent = land_refs[t].at[slot(*chip, c)]
                    _remote(sent, sent, fwd_send[t].at[j], fwd_recv[t].at[j], sibling).wait_send()
                    got = land_refs[t].at[slot(*chip, 1 - c)]
                    _remote(got, got, fwd_send[t].at[j], fwd_recv[t].at[j], sibling).wait_recv()

        srcs = [self.srcs[t] for t in which]
        outs = pl.pallas_call(
            wait_body,
            name=name,
            in_specs=[HBM] * (2 * n) + [SEM] * (4 * n),
            out_specs=[HBM] * (2 * n),
            out_shape=[pltpu.HBM(a.shape, a.dtype) for a in (*srcs, *lands)],
            input_output_aliases={i: i for i in range(2 * n)},
            compiler_params=pltpu.CompilerParams(has_side_effects=EFFECT),
        )(*srcs, *lands, *[self.send_sems[t] for t in which], *[self.recv_sems[t] for t in which], *fwd_send, *fwd_recv)
        return outs[n:]


class Scatter:
    def __init__(self, srcs, lands, after, *, name):
        self.name = name
        nt = self.nt = len(srcs)
        peers = N_DEV - 1

        def body(*refs):
            src_refs, land_refs = refs[:nt], refs[nt : 2 * nt]
            send_sems, recv_sems = refs[2 * nt + 1 : 3 * nt + 1], refs[3 * nt + 1 : 4 * nt + 1]
            token = refs[-1]
            x, y, c, me = _position()
            for t in range(nt):
                for k in range(1, N_DEV):
                    dev, pid = _peer(k, x, y, c)
                    src = src_refs[t].at[pid]
                    _remote(src, land_refs[t].at[me], send_sems[t].at[k - 1], recv_sems[t].at[k - 1], dev).start()
            token[...] = jnp.zeros_like(token)

        outs = pl.pallas_call(
            body,
            name=name + "_start",
            in_specs=[HBM] * (2 * nt) + [ANY],
            out_specs=[SEM] * (2 * nt) + [HBM] * (2 * nt) + [VMEM],
            out_shape=[pltpu.SemaphoreType.DMA((peers,))] * (2 * nt)
            + [pltpu.HBM(a.shape, a.dtype) for a in (*srcs, *lands)]
            + [jax.ShapeDtypeStruct((SUBLANES, LANES), F32)],
            input_output_aliases={i: 2 * nt + i for i in range(2 * nt)},
            compiler_params=pltpu.CompilerParams(has_side_effects=EFFECT),
        )(*[pltpu.with_memory_space_constraint(a, pltpu.HBM) for a in (*srcs, *lands)], after)
        self.send_sems, self.recv_sems = outs[:nt], outs[nt : 2 * nt]
        self.srcs, self.lands = outs[2 * nt : 3 * nt], outs[3 * nt : 4 * nt]
        self.token = outs[-1]

    def zero(self):
        return self.token[0, 0]

    def wait(self, which, after, *, name):
        n = len(which)

        def body(*refs):
            src_refs, land_refs = refs[:n], refs[n : 2 * n]
            send_sems, recv_sems = refs[2 * n : 3 * n], refs[3 * n : 4 * n]
            x, y, c, _ = _position()
            for t in range(n):
                for k in range(1, N_DEV):
                    dev, pid = _peer(k, x, y, c)
                    src = src_refs[t].at[pid]
                    cp = _remote(src, land_refs[t].at[pid], send_sems[t].at[k - 1], recv_sems[t].at[k - 1], dev)
                    cp.wait_send()
                    cp.wait_recv()

        srcs = [self.srcs[t] for t in which]
        lands = [self.lands[t] for t in which]
        outs = pl.pallas_call(
            body,
            name=name,
            in_specs=[HBM] * (2 * n) + [SEM] * (2 * n) + [ANY],
            out_specs=[HBM] * (2 * n),
            out_shape=[pltpu.HBM(a.shape, a.dtype) for a in (*srcs, *lands)],
            input_output_aliases={i: i for i in range(2 * n)},
            compiler_params=pltpu.CompilerParams(has_side_effects=EFFECT),
        )(*srcs, *lands, *[self.send_sems[t] for t in which], *[self.recv_sems[t] for t in which], after)
        return outs[n:]


def _adam_update(g, w, m, v):
    m2 = ADAM_B1 * m + (1.0 - ADAM_B1) * g
    v2 = ADAM_B2 * v + (1.0 - ADAM_B2) * jnp.square(g)
    m_hat = m2 / (1.0 - ADAM_B1**ADAM_STEP)
    v_hat = v2 / (1.0 - ADAM_B2**ADAM_STEP)
    delta = -ADAM_LR * (m_hat / (jnp.sqrt(v_hat) + ADAM_EPS) + ADAM_WD * w)
    return delta, m2, v2


def adamw(gparts, w, m, v, *, name):
    nl, r, c = w.shape
    p = gparts[0].shape[0]
    tr = r if r <= 256 else (256 if c <= D_MODEL else 128)
    ni = r // tr

    def body(*refs):
        g_refs = refs[:nl]
        w_ref, m_ref, v_ref, go_ref, d_ref, mo_ref, vo_ref = refs[nl:]
        for layer in range(nl):

            @pl.when(pl.program_id(0) == layer)
            def _(g_ref=g_refs[layer]):
                g = g_ref[0].astype(F32)
                for i in range(1, p):
                    g = g + g_ref[i].astype(F32)
                delta, m2, v2 = _adam_update(g, w_ref[...], m_ref[...], v_ref[...])
                go_ref[...] = g
                d_ref[...] = delta
                mo_ref[...] = m2
                vo_ref[...] = v2

    def parts_spec(layer):
        def index(l, i):
            return (0, jnp.where(l == layer, i, jnp.where(l < layer, 0, ni - 1)), 0)

        return pl.BlockSpec((p, tr, c), index)

    blk = pl.BlockSpec((None, tr, c), lambda l, i: (l, i, 0))
    return pl.pallas_call(
        body,
        name=name,
        grid=(nl, ni),
        in_specs=[*[parts_spec(layer) for layer in range(nl)], blk, blk, blk],
        out_specs=[blk] * 4,
        out_shape=[jax.ShapeDtypeStruct((nl, r, c), F32)] * 4,
        compiler_params=_params(("arbitrary", "arbitrary")),
    )(*gparts, w, m, v)


def ada_grad_adamw(cg, dmrecv, w4, m4, v4, *, name):
    nt, k, ncol = w4.shape

    def body(cg_ref, dm_ref, w_ref, m_ref, v_ref, go_ref, d_ref, mo_ref, vo_ref, gb_ref):
        sc = _silu_bf16(cg_ref[...].reshape(N_DEV * SUBLANES, k))
        dm = dm_ref[...].reshape(N_DEV * SUBLANES, ncol)
        g = lax.dot_general(sc, dm.astype(BF16), (((0,), (0,)), ((), ())), preferred_element_type=F32)
        delta, m2, v2 = _adam_update(g, w_ref[...], m_ref[...], v_ref[...])
        go_ref[...] = g
        d_ref[...] = delta
        mo_ref[...] = m2
        vo_ref[...] = v2
        gb_ref[...] = jnp.broadcast_to(jnp.sum(dm, axis=0, keepdims=True), (SUBLANES, ncol))

    wblk = pl.BlockSpec((None, k, ncol), lambda t: (t, 0, 0))
    return pl.pallas_call(
        body,
        name=name,
        grid=(nt,),
        in_specs=[
            pl.BlockSpec((N_DEV, SUBLANES, k), lambda t: (0, 0, 0)),
            pl.BlockSpec((N_DEV, None, SUBLANES, ncol), lambda t: (0, t, 0, 0)),
            wblk,
            wblk,
            wblk,
        ],
        out_specs=[wblk] * 4 + [pl.BlockSpec((None, SUBLANES, ncol), lambda t: (t, 0, 0))],
        out_shape=[jax.ShapeDtypeStruct((nt, k, ncol), F32)] * 4 + [jax.ShapeDtypeStruct((nt, SUBLANES, ncol), F32)],
        compiler_params=_params(("parallel",)),
    )(cg, dmrecv, w4, m4, v4)


def kernel(x, c, ada_w, ada_b, ln_g, ln_b, a_w_in, a_b_in, a_vn_g, a_vn_b, a_w_s, a_b_s, a_w_out, b_w_qkv, b_w_out, mlp_w_up, mlp_w_down, loss_target, m_ada_w, m_ada_b, m_ln_g, m_ln_b, m_a_w_in, m_a_b_in, m_a_vn_g, m_a_vn_b, m_a_w_s, m_a_b_s, m_a_w_out, m_b_w_qkv, m_b_w_out, m_mlp_w_up, m_mlp_w_down, v_ada_w, v_ada_b, v_ln_g, v_ln_b, v_a_w_in, v_a_b_in, v_a_vn_g, v_a_vn_b, v_a_w_s, v_a_b_s, v_a_w_out, v_b_w_qkv, v_b_w_out, v_mlp_w_up, v_mlp_w_down):
    x0 = x[0]
    target = loss_target[0]
    me = 4 * lax.axis_index("x") + 2 * lax.axis_index("y") + lax.axis_index("c")

    ada_w4 = ada_w.reshape(N_SUB, D_MODEL, -1)
    ada_b4 = ada_b.reshape(N_SUB, -1)
    ln8 = jnp.concatenate([ln_g.reshape(N_SUB, -1), ln_b.reshape(N_SUB, -1)], axis=0)
    c8 = jnp.broadcast_to(c, (SUBLANES, D_MODEL))
    cg, lng, mrecv = ada_exchange(c8, ada_w4, ada_b4, ln8)

    W_IN, W_AOUT, W_UP0, W_DN0, W_QKV, W_BOUT, W_UP1, W_DN1 = range(8)
    shards = [
        a_w_in[0].astype(BF16),
        a_w_out[0].astype(BF16),
        mlp_w_up[0].astype(BF16),
        mlp_w_down[0].astype(BF16),
        b_w_qkv[0].astype(BF16),
        b_w_out[0].astype(BF16),
        mlp_w_up[1].astype(BF16),
        mlp_w_down[1].astype(BF16),
    ]
    gather = Gather(shards, [_own_slot(me, s) for s in shards], mrecv, name="gather")

    modv = mrecv[:, :, 0, :].transpose(1, 0, 2).reshape(N_SUB, 3 * D_MODEL) + gather.zero()
    shift = [modv[t : t + 1, :D_MODEL] for t in range(N_SUB)]
    scale = [modv[t : t + 1, D_MODEL : 2 * D_MODEL] for t in range(N_SUB)]
    gate1 = [1.0 + modv[t : t + 1, 2 * D_MODEL :] for t in range(N_SUB)]
    lng_full = [lng[:, t, :].reshape(1, D_MODEL) for t in range(N_SUB)]
    lnb_full = [lng[:, N_SUB + t, :].reshape(1, D_MODEL) for t in range(N_SUB)]

    ident = lambda acc: (acc,)
    def relu2(a):
        r = jnp.maximum(a, jnp.zeros_like(a))
        return r * r
    vn_g, vn_b, w_s = a_vn_g, a_vn_b, a_w_s[0]
    bias_full = jnp.repeat(a_b_s[0].T, A_GROUP_DIM, axis=1)
    w_up3, w_dn3 = [None, None], [None, None]

    def mlp_forward(i, h, after):
        up, dn = gather.wait([W_UP0, W_DN0] if i == 0 else [W_UP1, W_DN1], after, name=f"gather_wait_mlp{i}")
        w_up3[i], w_dn3[i] = up, dn.reshape(1, D_FF, D_MODEL)
        (a,) = mm_nn(h, w_up3[i], name=f"mlp{i}_up", tm=2048, ps=2, tn=512, tk=D_MODEL, epilogue=ident, outs=(BF16,))
        (y,) = mm_nn(
            a, w_dn3[i], name=f"mlp{i}_down", tm=1024, ps=1, tn=512, tk=D_FF, prologue=relu2, epilogue=ident, outs=(BF16,)
        )
        return a, y

    h0 = modulate(x0, scale[0], shift[0], name="modulate0")
    w_in3, w_aout3 = gather.wait([W_IN, W_AOUT], h0, name="gather_wait_a")
    w_aout3 = w_aout3.reshape(1, D_MODEL, D_MODEL)
    (a_pre,) = mm_nn(
        h0, w_in3, name="a_in", tm=2048, ps=4, tn=256, tk=D_MODEL, epilogue=lambda acc, b: (acc + b,),
        extras=[(a_b_in, "row")], outs=(BF16,),
    )
    p_gate = gate_fwd(a_pre, vn_g, vn_b, w_s, bias_full, name="gate_fwd")
    (y0,) = mm_nn(p_gate, w_aout3, name="a_out", tm=2048, ps=1, tn=D_MODEL, tk=D_MODEL, epilogue=ident, outs=(BF16,))
    x1, h1 = residual_ln(x0, y0, gate1[0], lng_full[0], lnb_full[0], scale[1], shift[1], name="res_ln0")
    a1, y1 = mlp_forward(0, h1, y0)
    x2, h2 = residual_ln(x1, y1, gate1[1], lng_full[1], lnb_full[1], scale[2], shift[2], name="res_ln1")
    w_qkv_shards, w_bout3 = gather.wait([W_QKV, W_BOUT], y1, name="gather_wait_b")
    w_bout3 = w_bout3.reshape(1, D_MODEL, D_MODEL)
    w_qkv3 = shards_to_columns(w_qkv_shards, name="w_qkv_columns")[None]
    pat_tiles = 3
    dil = [d for _, d in B_PATTERNS]
    qkv_p, pat_o, pat_lse = [], [], []
    for g in range(N_PAT):
        (qkv_g,) = mm_nn(
            h2, w_qkv3, name=f"b_qkv{g}", tm=2048, ps=1, tn=D_MODEL, tk=D_MODEL, epilogue=ident, outs=(BF16,),
            b_tile0=pat_tiles * g, b_tiles=pat_tiles, out_streams=dil[g],
        )
        o_g, lse_g = attn_fwd(qkv_g, g, name=f"attn_fwd{g}")
        if g > 0:
            o_g = permute_rows(o_g, dil[g], inverse=True, name=f"unperm_o{g}")
            lse_g = permute_rows(lse_g, dil[g], inverse=True, name=f"unperm_lse{g}")
        qkv_p.append(qkv_g)
        pat_o.append(o_g)
        pat_lse.append(lse_g)
    o_b, o_f, lse = attn_combine(pat_o, pat_lse, name="attn_combine")
    (y2,) = mm_nn(o_b, w_bout3, name="b_out", tm=2048, ps=1, tn=D_MODEL, tk=D_MODEL, epilogue=ident, outs=(BF16,))
    x3, h3 = residual_ln(x2, y2, gate1[2], lng_full[2], lnb_full[2], scale[3], shift[3], name="res_ln2")
    a3, y3 = mlp_forward(1, h3, y2)
    dxo, loss_local = residual_ln_loss(x3, y3, gate1[3], lng_full[3], lnb_full[3], target, name="res_ln3_loss")

    def scatter(parts, after, name):
        parts = [p.reshape(N_DEV, -1, p.shape[-1]) for p in parts]
        lands = [_own_slot(me, lax.dynamic_index_in_dim(p, me, 0, keepdims=False)) for p in parts]
        return Scatter(parts, lands, after, name=name)

    def mlp_backward(i, h, a, dy):
        (da,) = mm_nt(
            dy,
            w_dn3[i],
            name=f"mlp{i}_da",
            tm=2048,
            tko=1024,
            ps=1,
            tc=D_MODEL,
            epilogue=lambda acc, act: (acc * (2.0 * jnp.maximum(act.astype(F32), 0.0)),),
            extras=[(a, "full")],
            outs=(BF16,),
        )
        (dh,) = mm_nt(da, w_up3[i], name=f"mlp{i}_dh", tm=1024, tko=512, ps=N_DEV, tc=512, epilogue=ident, outs=(F32,))
        dw_dn = mm_tn(
            a, dy, name=f"mlp{i}_dw_down", p=1, tk=1024, ps=1, tn=D_MODEL, tmc=2048, prologue=relu2, out_dtype=BF16
        )
        dw_up = mm_tn(h, da, name=f"mlp{i}_dw_up", p=N_DEV, tk=1024, ps=2, tn=512, tmc=2048, out_dtype=BF16)
        return scatter([dw_up, dw_dn], dh, f"scatter_mlp{i}"), dh

    dz3, dy3, st3 = residual_ln_bwd(name="res_bwd3", dxo=dxo, this=(x3, y3, gate1[3], lng_full[3], lnb_full[3]))
    rs_mlp1, dh3 = mlp_backward(1, h3, a3, dy3)
    dz2, dy2, st2 = residual_ln_bwd(
        name="res_bwd2", later=(dh3, dz3, scale[3] + rs_mlp1.zero()), this=(x2, y2, gate1[2], lng_full[2], lnb_full[2])
    )
    (d_o,) = mm_nt(dy2, w_bout3, name="b_do", tm=2048, tko=1024, ps=1, tc=D_MODEL, epilogue=ident, outs=(F32,))
    do_b, delta = attn_delta(d_o, o_f, name="attn_delta")
    dh2, dw_pat = None, []
    for g in range(N_PAT):
        do_g, lse_g, delta_g = do_b, lse, delta
        if g > 0:
            do_g = permute_rows(do_b, dil[g], inverse=False, name=f"perm_do{g}")
            lse_g = permute_rows(lse, dil[g], inverse=False, name=f"perm_lse{g}")
            delta_g = permute_rows(delta, dil[g], inverse=False, name=f"perm_delta{g}")
        dqkv_g = attn_bwd(qkv_p[g], do_g, lse_g, delta_g, g, name=f"attn_bwd{g}")
        (dh2,) = mm_nt(
            dqkv_g, w_qkv3, name=f"b_dh{g}", tm=2048, tko=512, ps=1, tc=pat_tiles * D_MODEL, outs=(F32,),
            b_tile0=g, g_streams=dil[g],
            epilogue=ident if g == 0 else (lambda acc, prev, d=dil[g]: (_from_streams(acc, d) + prev,)),
            extras=[] if g == 0 else [(dh2, "full")],
        )
        dw_pat.append(
            mm_tn(
                h2, dqkv_g, name=f"b_dw_qkv{g}", p=1, tk=1024, ps=1, tn=D_MODEL, tmc=2048, out_dtype=BF16,
                g_streams=dil[g],
            )[0]
        )
    dw_bout = mm_tn(o_b, dy2, name="b_dw_out", p=1, tk=1024, ps=1, tn=D_MODEL, tmc=2048, out_dtype=BF16)
    dw_qkv = columns_to_shards(dw_pat, name="dw_qkv_shards")
    rs_b = scatter([dw_qkv, dw_bout], dh2, "scatter_b")
    dz1, dy1, st1 = residual_ln_bwd(
        name="res_bwd1", later=(dh2, dz2, scale[2] + rs_b.zero()), this=(x1, y1, gate1[1], lng_full[1], lnb_full[1])
    )
    rs_mlp0, dh1 = mlp_backward(0, h1, a1, dy1)
    dz0, dy0, st0 = residual_ln_bwd(
        name="res_bwd0", later=(dh1, dz1, scale[1] + rs_mlp0.zero()), this=(x0, y0, gate1[0], lng_full[0], lnb_full[0])
    )
    (dp_gate,) = mm_nt(dy0, w_aout3, name="a_dp", tm=2048, tko=1024, ps=1, tc=D_MODEL, epilogue=ident, outs=(F32,))
    dw_aout = mm_tn(p_gate, dy0, name="a_dw_out", p=1, tk=1024, ps=1, tn=D_MODEL, tmc=2048, out_dtype=BF16)
    rs_aout = scatter([dw_aout], dp_gate, "scatter_a_out")
    da0, d_ws, d_bs, gate_rows = gate_bwd(a_pre, dp_gate, vn_g + rs_aout.zero(), vn_b, w_s, bias_full, name="gate_bwd")
    dw_in = mm_tn(h0, da0, name="a_dw_in", p=N_DEV, tk=1024, ps=4, tn=256, tmc=2048, out_dtype=BF16)
    rs_in = scatter([dw_in], d_ws, "scatter_a_in")
    (dh0,) = mm_nt(da0, w_in3, name="a_dh", tm=2048, tko=1024, ps=N_DEV, tc=256, epilogue=ident, outs=(F32,))
    grad_x, stf = residual_ln_bwd(name="res_bwd_in", later=(dh0, dz0, scale[0] + rs_in.zero()), x_out=x0)

    stats_after = [stf, st0, st1, st2]
    stats_own = [st0, st1, st2, st3]
    dm = jnp.stack(
        [
            jnp.concatenate(
                [stats_after[t][ST_DSHIFT], stats_after[t][ST_DSCALE], stats_own[t][ST_DGATE]], axis=0
            )
            for t in range(N_SUB)
        ]
    )
    ncol = 3 * D_MODEL // N_DEV
    dmx = jnp.pad(
        dm.reshape(N_SUB, N_DEV, ncol).transpose(1, 0, 2)[:, :, None, :], ((0, 0), (0, 0), (0, SUBLANES - 1), (0, 0))
    )
    small = [
        gate_rows[0],
        gate_rows[1],
        d_ws.reshape(-1),
        d_bs[:, :A_GROUPS].T.reshape(-1),
        *[stats_own[t][ST_DG] for t in range(N_SUB)],
        *[stats_own[t][ST_DB] for t in range(N_SUB)],
        jnp.pad(loss_local.reshape(1), (0, LANES - 1)),
    ]
    n_small = sum(s.size for s in small)
    part_rows = -(-n_small // (N_DEV * LANES * SUBLANES)) * SUBLANES
    flat = jnp.concatenate(small + [jnp.zeros((N_DEV * part_rows * LANES - n_small,), F32)])
    dmrecv, reduced = small_exchange(dmx, flat.reshape(N_DEV, part_rows, LANES))
    reduced = reduced.reshape(-1)
    sizes = [2 * D_MODEL, D_MODEL, D_MODEL, A_GROUPS * CHUNK * CHUNK, A_GROUPS * CHUNK, N_SUB * D_MODEL, N_SUB * D_MODEL]
    offs = [sum(sizes[:i]) for i in range(len(sizes) + 1)]
    g_b_in, g_vn_g, g_vn_b, g_ws, g_bs, g_lng, g_lnb = [reduced[offs[i] : offs[i + 1]] for i in range(len(sizes))]
    loss = reduced[offs[-1]]

    results = {}

    def update(wname, gparts, w, m, v):
        shape = w.shape
        layers = len(gparts) if isinstance(gparts, list) else 1
        w3 = w.reshape(layers, -1, shape[-1])
        parts = [g.reshape(g.shape[0], *w3.shape[1:]) for g in (gparts if layers > 1 else [gparts])]
        outs = adamw(parts, w3, m.reshape(w3.shape), v.reshape(w3.shape), name=f"adamw_{wname}")
        results[wname] = [o.reshape(shape) for o in outs]

    ada_outs = ada_grad_adamw(cg, dmrecv, ada_w4, m_ada_w.reshape(ada_w4.shape), v_ada_w.reshape(ada_w4.shape), name="ada_grad_adamw")
    results["ada_w"] = [o.reshape(ada_w.shape) for o in ada_outs[:4]]
    update("ada_b", ada_outs[4][:, 0, :][None], ada_b, m_ada_b, v_ada_b)
    ln_cols = D_MODEL // N_DEV
    my_ln = lambda gfull: lax.dynamic_slice_in_dim(gfull.reshape(N_SUB, N_DEV, ln_cols), me, 1, axis=1)
    update("ln_g", my_ln(g_lng).reshape(1, N_SUB, ln_cols), ln_g, m_ln_g, v_ln_g)
    update("ln_b", my_ln(g_lnb).reshape(1, N_SUB, ln_cols), ln_b, m_ln_b, v_ln_b)
    update("a_b_in", g_b_in[None], a_b_in, m_a_b_in, v_a_b_in)
    update("a_vn_g", g_vn_g[None], a_vn_g, m_a_vn_g, v_a_vn_g)
    update("a_vn_b", g_vn_b[None], a_vn_b, m_a_vn_b, v_a_vn_b)
    update("a_w_s", g_ws[None], a_w_s, m_a_w_s, v_a_w_s)
    update("a_b_s", g_bs[None], a_b_s, m_a_b_s, v_a_b_s)
    g_up1, g_dn1 = rs_mlp1.wait([0, 1], grad_x, name="scatter_wait_mlp1")
    g_qkv, g_bout = rs_b.wait([0, 1], grad_x, name="scatter_wait_b")
    update("b_w_qkv", g_qkv, b_w_qkv, m_b_w_qkv, v_b_w_qkv)
    update("b_w_out", g_bout, b_w_out, m_b_w_out, v_b_w_out)
    g_up0, g_dn0 = rs_mlp0.wait([0, 1], grad_x, name="scatter_wait_mlp0")
    update("mlp_w_up", [g_up0, g_up1], mlp_w_up, m_mlp_w_up, v_mlp_w_up)
    update("mlp_w_down", [g_dn0, g_dn1], mlp_w_down, m_mlp_w_down, v_mlp_w_down)
    (g_aout,) = rs_aout.wait([0], grad_x, name="scatter_wait_a_out")
    (g_in,) = rs_in.wait([0], grad_x, name="scatter_wait_a_in")
    update("a_w_in", g_in, a_w_in, m_a_w_in, v_a_w_in)
    update("a_w_out", g_aout, a_w_out, m_a_w_out, v_a_w_out)

    order = ["ada_w", "ada_b", "ln_g", "ln_b", "a_w_in", "a_b_in", "a_vn_g", "a_vn_b", "a_w_s", "a_b_s", "a_w_out", "b_w_qkv", "b_w_out", "mlp_w_up", "mlp_w_down"]
    return (loss, grad_x[None], *[results[n][0] for n in order], *[results[n][1] for n in order],
            *[results[n][2] for n in order], *[results[n][3] for n in order])
```

```python
import math

import jax
import jax.numpy as jnp
from jax import lax
from jax.experimental import pallas as pl
from jax.experimental.pallas import tpu as pltpu

F32 = jnp.float32
BF16 = jnp.bfloat16
MESH = pl.DeviceIdType.MESH
ANY = pl.BlockSpec(memory_space=pl.ANY)
VMEM = pl.BlockSpec(memory_space=pltpu.VMEM)

N_DEV = 8
D_MODEL = 1024
SEQ = 4096
DEPTH = 2
CHUNK = 128
A_GROUPS = 16
A_GROUP_DIM = D_MODEL // A_GROUPS
B_HEADS = 16
B_HEAD_DIM = 64
B_PATTERNS = ((128, 1), (512, 4), (2048, 16))
N_PAT = len(B_PATTERNS)
SPAN = 128
D_FF = 4 * D_MODEL
D_QKV = N_PAT * 3 * D_MODEL
ALPHA = (2 * DEPTH) ** 0.25
LN_EPS = 1e-5
NEG = -1e30
ADAM_LR = 0.001
ADAM_B1 = 0.9
ADAM_B2 = 0.999
ADAM_EPS = 1e-08
ADAM_WD = 0.01
ADAM_STEP = 10
GELU_C = math.sqrt(2.0 / math.pi)
GELU_A = 0.044715

VMEM_LIMIT_BYTES = 56 * 1024 * 1024
LANES = 128
SUBLANES = 8
ROW_TILE = 512
N_SUB = 2 * DEPTH


def _params(sem):
    return pltpu.CompilerParams(dimension_semantics=sem, vmem_limit_bytes=VMEM_LIMIT_BYTES)


def _lane(shape):
    return lax.broadcasted_iota(jnp.int32, shape, len(shape) - 1)


def _split_bf16(x):
    hi = x.astype(BF16)
    lo = (x - hi.astype(F32)).astype(BF16)
    return hi, lo


def _group_expand_matrix(groups_padded, width):
    per = width // A_GROUPS
    r = lax.broadcasted_iota(jnp.int32, (groups_padded, width), 0)
    c = lax.broadcasted_iota(jnp.int32, (groups_padded, width), 1)
    return (c // per == r).astype(BF16)


def _group_reduce_matrix(width, groups_padded):
    per = width // A_GROUPS
    r = lax.broadcasted_iota(jnp.int32, (width, groups_padded), 0)
    c = lax.broadcasted_iota(jnp.int32, (width, groups_padded), 1)
    return (r // per == c).astype(BF16)


def _expand_groups(w):
    e = _group_expand_matrix(LANES, D_MODEL)
    hi, lo = _split_bf16(w)
    return jnp.dot(hi, e, preferred_element_type=F32) + jnp.dot(lo, e, preferred_element_type=F32)


def _reduce_groups(x):
    e = _group_reduce_matrix(D_MODEL, LANES)
    hi, lo = _split_bf16(x)
    return jnp.dot(hi, e, preferred_element_type=F32) + jnp.dot(lo, e, preferred_element_type=F32)


def _to_streams(x, d):
    rows, w = x.shape
    return jnp.swapaxes(x.reshape(rows // d, d, w), 0, 1).reshape(rows, w)


def _from_streams(x, d):
    rows, w = x.shape
    return jnp.swapaxes(x.reshape(d, rows // d, w), 0, 1).reshape(rows, w)


def _column_tiles(p, n, ps, tn):
    assert (ps == 1 or tn == n) and p % ps == 0 and n % tn == 0
    q = n // tn
    return (p // ps) * q, q


def _extra_specs(extras, tm, width):
    specs = []
    for _, kind in extras:
        if kind == "row":
            specs.append(pl.BlockSpec((1, width), lambda i, j, c: (0, j)))
        else:
            specs.append(pl.BlockSpec((tm, width), lambda i, j, c: (i, j)))
    return specs


def mm_nn(a, b3, *, name, tm, ps, tn, tk, epilogue, extras=(), outs, prologue=None, b_tile0=0, b_tiles=None, out_streams=1):
    m, k = a.shape
    p, _, n = b3.shape
    nj, q = _column_tiles(p, n, ps, tn)
    nj = nj if b_tiles is None else b_tiles
    nk = k // tk
    width = ps * tn
    d = out_streams
    assert d == 1 or not extras

    def body(a_ref, b_ref, *rest):
        ex = rest[: len(extras)]
        out_refs = rest[len(extras) : len(extras) + len(outs)]
        kk = pl.program_id(2)
        av = a_ref[...] if prologue is None else prologue(a_ref[...])
        if d > 1:
            av = _to_streams(av, d)

        def finish(cs, acc):
            res = epilogue(acc, *[e[:, cs] for e in ex])
            for o_ref, r in zip(out_refs, res, strict=True):
                if d > 1:
                    o_ref[:, :, cs] = r.astype(o_ref.dtype).reshape(d, tm // d, tn)
                else:
                    o_ref[:, cs] = r.astype(o_ref.dtype)

        for s in range(ps):
            cs = slice(s * tn, (s + 1) * tn)
            part = jnp.dot(av, b_ref[s], preferred_element_type=F32)
            if nk == 1:
                finish(cs, part)
                continue
            acc_ref = rest[-1]

            @pl.when(kk == 0)
            def _(part=part, cs=cs):
                acc_ref[:, cs] = part

            @pl.when(kk > 0)
            def _(part=part, cs=cs):
                acc_ref[:, cs] += part

        if nk > 1:

            @pl.when(kk == nk - 1)
            def _():
                for s in range(ps):
                    cs = slice(s * tn, (s + 1) * tn)
                    finish(cs, rest[-1][:, cs])

    if d > 1:
        out_spec = pl.BlockSpec((d, tm // d, width), lambda i, j, kk: (0, i, j))
        out_shape = (d, m // d, nj * width)
    else:
        out_spec = pl.BlockSpec((tm, width), lambda i, j, kk: (i, j))
        out_shape = (m, nj * width)
    res = pl.pallas_call(
        body,
        name=name,
        grid=(m // tm, nj, nk),
        in_specs=[
            pl.BlockSpec((tm, tk), lambda i, j, kk: (i, kk)),
            pl.BlockSpec((ps, tk, tn), lambda i, j, kk: ((j + b_tile0) // q, kk, (j + b_tile0) % q)),
            *_extra_specs(extras, tm, width),
        ],
        out_specs=[out_spec for _ in outs],
        out_shape=[jax.ShapeDtypeStruct(out_shape, dt) for dt in outs],
        scratch_shapes=[pltpu.VMEM((tm, width), F32)] if nk > 1 else [],
        compiler_params=_params(("parallel", "parallel", "arbitrary")),
    )(a, b3, *[arr for arr, _ in extras])
    return [r.reshape(m, nj * width) for r in res]


def mm_nt(g, b3, *, name, tm, tko, ps, tc, epilogue, extras=(), outs, b_tile0=0, g_streams=1):
    m, width = g.shape
    p, k, n = b3.shape
    _, q = _column_tiles(p, n, ps, tc)
    nc = width // (ps * tc)
    ds = g_streams

    def body(g_ref, b_ref, *rest):
        ex = rest[: len(extras)]
        out_refs = rest[len(extras) : len(extras) + len(outs)]
        c = pl.program_id(2)
        gv = g_ref[...].reshape(tm, ps * tc) if ds > 1 else g_ref[...]
        part = None
        for s in range(ps):
            d = lax.dot_general(gv[:, s * tc : (s + 1) * tc], b_ref[s], _NT, preferred_element_type=F32)
            part = d if part is None else part + d

        def finish(acc):
            res = epilogue(acc, *[e[...] for e in ex])
            for o_ref, r in zip(out_refs, res, strict=True):
                o_ref[...] = r.astype(o_ref.dtype)

        if nc == 1:
            finish(part)
            return
        acc_ref = rest[-1]

        @pl.when(c == 0)
        def _():
            acc_ref[...] = part

        @pl.when(c > 0)
        def _():
            acc_ref[...] += part

        @pl.when(c == nc - 1)
        def _():
            finish(acc_ref[...])

    return pl.pallas_call(
        body,
        name=name,
        grid=(m // tm, k // tko, nc),
        in_specs=[
            pl.BlockSpec((ds, tm // ds, ps * tc), lambda i, j, c: (0, i, c))
            if ds > 1
            else pl.BlockSpec((tm, ps * tc), lambda i, j, c: (i, c)),
            pl.BlockSpec((ps, tko, tc), lambda i, j, c: ((c + b_tile0) // q, j, (c + b_tile0) % q)),
            *_extra_specs(extras, tm, tko),
        ],
        out_specs=[pl.BlockSpec((tm, tko), lambda i, j, c: (i, j)) for _ in outs],
        out_shape=[jax.ShapeDtypeStruct((m, k), dt) for dt in outs],
        scratch_shapes=[pltpu.VMEM((tm, tko), F32)] if nc > 1 else [],
        compiler_params=_params(("parallel", "parallel", "arbitrary")),
    )(g.reshape(ds, m // ds, width) if ds > 1 else g, b3, *[arr for arr, _ in extras])


def mm_tn(a, g, *, name, p, tk, ps, tn, tmc, out_dtype, prologue=None, g_streams=1):
    m, k = a.shape
    width = g.shape[1]
    n = width // p
    nj, q = _column_tiles(p, n, ps, tn)
    nc = m // tmc
    ds = g_streams

    def body(a_ref, g_ref, o_ref, acc_ref):
        c = pl.program_id(2)
        av = a_ref[...] if prologue is None else prologue(a_ref[...])
        gv = g_ref[...]
        if ds > 1:
            av, gv = _to_streams(av, ds), gv.reshape(tmc, ps * tn)
        part = lax.dot_general(av, gv, (((0,), (0,)), ((), ())), preferred_element_type=F32)

        @pl.when(c == 0)
        def _():
            acc_ref[...] = part

        @pl.when(c > 0)
        def _():
            acc_ref[...] += part

        @pl.when(c == nc - 1)
        def _():
            for s in range(ps):
                o_ref[s] = acc_ref[:, s * tn : (s + 1) * tn].astype(o_ref.dtype)

    return pl.pallas_call(
        body,
        name=name,
        grid=(k // tk, nj, nc),
        in_specs=[
            pl.BlockSpec((tmc, tk), lambda i, j, c: (c, i)),
            pl.BlockSpec((ds, tmc // ds, ps * tn), lambda i, j, c: (0, c, j))
            if ds > 1
            else pl.BlockSpec((tmc, ps * tn), lambda i, j, c: (c, j)),
        ],
        out_specs=pl.BlockSpec((ps, tk, tn), lambda i, j, c: (j // q, i, j % q)),
        out_shape=jax.ShapeDtypeStruct((p, k, n), out_dtype),
        scratch_shapes=[pltpu.VMEM((tk, ps * tn), F32)],
        compiler_params=_params(("parallel", "parallel", "arbitrary")),
    )(a, g.reshape(ds, m // ds, width) if ds > 1 else g)


def _rows(cols):
    return pl.BlockSpec((ROW_TILE, cols), lambda i: (i, 0))


def _vec(cols, rows=1):
    return pl.BlockSpec((rows, cols), lambda i: (0, 0))


def _layer_norm_hat(z):
    mu = jnp.mean(z, axis=-1, keepdims=True)
    zc = z - mu
    var = jnp.mean(zc * zc, axis=-1, keepdims=True)
    rstd = lax.rsqrt(var + LN_EPS)
    return zc * rstd, rstd


def modulate(x, scale, shift, *, name):
    s, d = x.shape

    def body(x_ref, sc_ref, sh_ref, h_ref):
        h_ref[...] = (x_ref[...] * (1.0 + sc_ref[...]) + sh_ref[...]).astype(BF16)

    return pl.pallas_call(
        body,
        name=name,
        grid=(s // ROW_TILE,),
        in_specs=[_rows(d), _vec(d), _vec(d)],
        out_specs=_rows(d),
        out_shape=jax.ShapeDtypeStruct((s, d), BF16),
        compiler_params=_params(("parallel",)),
    )(x, scale, shift)


def residual_ln(x, y, gate1, g, b, nscale, nshift, *, name):
    s, d = x.shape

    def body(x_ref, y_ref, gt_ref, g_ref, b_ref, sc_ref, sh_ref, xn_ref, hn_ref):
        z = ALPHA * x_ref[...] + gt_ref[...] * y_ref[...].astype(F32)
        xhat, _ = _layer_norm_hat(z)
        xn = xhat * g_ref[...] + b_ref[...]
        xn_ref[...] = xn
        hn_ref[...] = (xn * (1.0 + sc_ref[...]) + sh_ref[...]).astype(BF16)

    return pl.pallas_call(
        body,
        name=name,
        grid=(s // ROW_TILE,),
        in_specs=[_rows(d), _rows(d), _vec(d), _vec(d), _vec(d), _vec(d), _vec(d)],
        out_specs=[_rows(d), _rows(d)],
        out_shape=[jax.ShapeDtypeStruct((s, d), F32), jax.ShapeDtypeStruct((s, d), BF16)],
        compiler_params=_params(("parallel",)),
    )(x, y, gate1, g, b, nscale, nshift)


def residual_ln_loss(x, y, gate1, g, b, target, *, name):
    s, d = x.shape

    def body(x_ref, y_ref, gt_ref, g_ref, b_ref, t_ref, dx_ref, loss_ref):
        z = ALPHA * x_ref[...] + gt_ref[...] * y_ref[...].astype(F32)
        xhat, _ = _layer_norm_hat(z)
        err = xhat * g_ref[...] + b_ref[...] - t_ref[...]
        dx_ref[...] = err * (1.0 / d)
        part = jnp.sum(jnp.sum(err * err, axis=1, keepdims=True), axis=0, keepdims=True) * (0.5 / d)

        @pl.when(pl.program_id(0) == 0)
        def _():
            loss_ref[...] = part

        @pl.when(pl.program_id(0) > 0)
        def _():
            loss_ref[...] += part

    return pl.pallas_call(
        body,
        name=name,
        grid=(s // ROW_TILE,),
        in_specs=[_rows(d), _rows(d), _vec(d), _vec(d), _vec(d), _rows(d)],
        out_specs=[_rows(d), pl.BlockSpec((1, 1), lambda i: (0, 0))],
        out_shape=[jax.ShapeDtypeStruct((s, d), F32), jax.ShapeDtypeStruct((1, 1), F32)],
        compiler_params=_params(("arbitrary",)),
    )(x, y, gate1, g, b, target)


ST_DSCALE, ST_DSHIFT, ST_DG, ST_DB, ST_DGATE = 0, 1, 2, 3, 4


def residual_ln_bwd(*, name, later=None, dxo=None, x_out=None, this=None):
    lead = later[0] if later is not None else dxo
    s, d = lead.shape
    has_later, has_ln = later is not None, this is not None

    def body(*refs):
        refs = list(refs)
        if has_later:
            dh_ref, dzl_ref, scl_ref = refs[:3]
            refs = refs[3:]
        else:
            dxo_ref = refs.pop(0)
        if has_ln:
            x_ref, y_ref, gt_ref, g_ref, b_ref = refs[:5]
            refs = refs[5:]
            dz_ref, dy_ref, st_ref = refs
        else:
            xo_ref, dx_ref, st_ref = refs

        @pl.when(pl.program_id(0) == 0)
        def _():
            st_ref[...] = jnp.zeros_like(st_ref)

        def acc(row, val):
            st_ref[row : row + 1, :] += jnp.sum(val, axis=0, keepdims=True)

        if has_ln:
            y = y_ref[...].astype(F32)
            gate1 = gt_ref[...]
            xhat, rstd = _layer_norm_hat(ALPHA * x_ref[...] + gate1 * y)
            x_out_v = xhat * g_ref[...] + b_ref[...]
        else:
            x_out_v = xo_ref[...]
        if has_later:
            dh = dh_ref[...]
            g_out = ALPHA * dzl_ref[...] + dh * (1.0 + scl_ref[...])
            acc(ST_DSCALE, dh * x_out_v)
            acc(ST_DSHIFT, dh)
        else:
            g_out = dxo_ref[...]
        if not has_ln:
            dx_ref[...] = g_out
            return
        acc(ST_DG, g_out * xhat)
        acc(ST_DB, g_out)
        dxh = g_out * g_ref[...]
        m1 = jnp.mean(dxh, axis=-1, keepdims=True)
        m2 = jnp.mean(dxh * xhat, axis=-1, keepdims=True)
        dz = rstd * (dxh - m1 - xhat * m2)
        acc(ST_DGATE, dz * y)
        dz_ref[...] = dz
        dy_ref[...] = (dz * gate1).astype(BF16)

    ins, specs = [], []
    if has_later:
        ins += list(later)
        specs += [_rows(d), _rows(d), _vec(d)]
    else:
        ins += [dxo]
        specs += [_rows(d)]
    if not has_ln:
        ins += [x_out]
        specs += [_rows(d)]
    if has_ln:
        ins += list(this)
        specs += [_rows(d), _rows(d), _vec(d), _vec(d), _vec(d)]
        out_specs = [_rows(d), _rows(d), _vec(d, SUBLANES)]
        out_shape = [
            jax.ShapeDtypeStruct((s, d), F32),
            jax.ShapeDtypeStruct((s, d), BF16),
            jax.ShapeDtypeStruct((SUBLANES, d), F32),
        ]
    else:
        out_specs = [_rows(d), _vec(d, SUBLANES)]
        out_shape = [jax.ShapeDtypeStruct((s, d), F32), jax.ShapeDtypeStruct((SUBLANES, d), F32)]
    return pl.pallas_call(
        body,
        name=name,
        grid=(s // ROW_TILE,),
        in_specs=specs,
        out_specs=out_specs,
        out_shape=out_shape,
        compiler_params=_params(("arbitrary",)),
    )(*ins)


GATE_CHUNKS = 4


def _gelu(x, with_grad=False):
    x2 = x * x
    t = jnp.tanh(GELU_C * (x + GELU_A * x2 * x))
    half = 0.5 * (1.0 + t)
    y = x * half
    if not with_grad:
        return y
    return y, half + 0.5 * x * (1.0 - t * t) * (GELU_C * (1.0 + 3.0 * GELU_A * x2))


def _causal_weights(w_ref, transpose):
    t = lax.broadcasted_iota(jnp.int32, (CHUNK, CHUNK), 0)
    s = lax.broadcasted_iota(jnp.int32, (CHUNK, CHUNK), 1)
    out = []
    for g in range(A_GROUPS):
        w = jnp.where(t >= s, w_ref[g], 0.0)
        out.append((w.T if transpose else w).astype(BF16))
    return out


def _spatial(ws, vn, lo_mask):
    rows = vn.shape[0]
    out_rows = []
    for r in range(rows // CHUNK):
        cols = []
        for j in range(A_GROUPS // 2):
            blk = vn[r * CHUNK : (r + 1) * CHUNK, j * LANES : (j + 1) * LANES]
            za = jnp.dot(ws[2 * j], blk, preferred_element_type=F32)
            zb = jnp.dot(ws[2 * j + 1], blk, preferred_element_type=F32)
            cols.append(jnp.where(lo_mask, za, zb))
        out_rows.append(jnp.concatenate(cols, axis=1))
    return jnp.concatenate(out_rows, axis=0)


def _gate_forward(a, vg, vb, ws, bias, lo_mask, with_grad=False):
    u = _gelu(a[:, :D_MODEL], with_grad)
    v = _gelu(a[:, D_MODEL:], with_grad)
    gu, gv = None, None
    if with_grad:
        (u, gu), (v, gv) = u, v
    vhat, rstd = _layer_norm_hat(v)
    vn = (vhat * vg + vb).astype(BF16)
    z = _spatial(ws, vn, lo_mask) + jnp.concatenate([bias] * (a.shape[0] // CHUNK), axis=0)
    return u, vhat, rstd, vn, z, gu, gv


def gate_fwd(a_pre, vn_g, vn_b, w_s, bias_full, *, name):
    s = a_pre.shape[0]
    tr = GATE_CHUNKS * CHUNK

    def body(a_ref, vg_ref, vb_ref, w_ref, bias_ref, p_ref):
        lo_mask = _lane((CHUNK, LANES)) < A_GROUP_DIM
        ws = _causal_weights(w_ref, transpose=False)
        u, _, _, _, z, _, _ = _gate_forward(a_ref[...].astype(F32), vg_ref[...], vb_ref[...], ws, bias_ref[...], lo_mask)
        p_ref[...] = (u * z).astype(BF16)

    return pl.pallas_call(
        body,
        name=name,
        grid=(s // tr,),
        in_specs=[
            pl.BlockSpec((tr, 2 * D_MODEL), lambda i: (i, 0)),
            _vec(D_MODEL),
            _vec(D_MODEL),
            pl.BlockSpec((A_GROUPS, CHUNK, CHUNK), lambda i: (0, 0, 0)),
            _vec(D_MODEL, CHUNK),
        ],
        out_specs=pl.BlockSpec((tr, D_MODEL), lambda i: (i, 0)),
        out_shape=jax.ShapeDtypeStruct((s, D_MODEL), BF16),
        compiler_params=_params(("parallel",)),
    )(a_pre, vn_g, vn_b, w_s, bias_full)


def gate_bwd(a_pre, dp, vn_g, vn_b, w_s, bias_full, *, name):
    s = a_pre.shape[0]
    tr = GATE_CHUNKS * CHUNK
    nsteps = s // tr

    def body(a_ref, dp_ref, vg_ref, vb_ref, w_ref, bias_ref, da_ref, dw_ref, dbs_ref, rows_ref, dbias_acc):
        step = pl.program_id(0)
        lo_mask = _lane((CHUNK, LANES)) < A_GROUP_DIM

        @pl.when(step == 0)
        def _():
            dw_ref[...] = jnp.zeros_like(dw_ref)
            rows_ref[...] = jnp.zeros_like(rows_ref)
            dbias_acc[...] = jnp.zeros_like(dbias_acc)

        a = a_ref[...].astype(F32)
        vg = vg_ref[...]
        ws = _causal_weights(w_ref, transpose=False)
        wts = _causal_weights(w_ref, transpose=True)
        u, vhat, rstd, vn, z, gelu_du, gelu_dv = _gate_forward(a, vg, vb_ref[...], ws, bias_ref[...], lo_mask, True)
        dp = dp_ref[...]
        du = dp * z
        dzz = dp * u
        dzz_b = dzz.astype(BF16)
        dvn = _spatial(wts, dzz_b, lo_mask)
        dbias = None
        for r in range(GATE_CHUNKS):
            rs = slice(r * CHUNK, (r + 1) * CHUNK)
            dbias = dzz[rs] if dbias is None else dbias + dzz[rs]
            for j in range(A_GROUPS // 2):
                cs = slice(j * LANES, (j + 1) * LANES)
                dblk = dzz[rs, cs]
                vblk = vn[rs, cs]
                for half in range(2):
                    keep = lo_mask if half == 0 else jnp.logical_not(lo_mask)
                    dm = jnp.where(keep, dblk, 0.0).astype(BF16)
                    dw_ref[2 * j + half] += lax.dot_general(
                        dm, vblk, (((1,), (1,)), ((), ())), preferred_element_type=F32
                    )
        dbias_acc[...] += dbias
        rows_ref[1:2, :D_MODEL] += jnp.sum(dvn * vhat, axis=0, keepdims=True)
        rows_ref[1:2, D_MODEL:] += jnp.sum(dvn, axis=0, keepdims=True)
        dvh = dvn * vg
        m1 = jnp.mean(dvh, axis=-1, keepdims=True)
        m2 = jnp.mean(dvh * vhat, axis=-1, keepdims=True)
        dv = rstd * (dvh - m1 - vhat * m2)
        da_u = du * gelu_du
        da_v = dv * gelu_dv
        da_ref[:, :D_MODEL] = da_u.astype(BF16)
        da_ref[:, D_MODEL:] = da_v.astype(BF16)
        rows_ref[0:1, :D_MODEL] += jnp.sum(da_u, axis=0, keepdims=True)
        rows_ref[0:1, D_MODEL:] += jnp.sum(da_v, axis=0, keepdims=True)

        @pl.when(step == nsteps - 1)
        def _():
            t = lax.broadcasted_iota(jnp.int32, (CHUNK, CHUNK), 0)
            sx = lax.broadcasted_iota(jnp.int32, (CHUNK, CHUNK), 1)
            for g in range(A_GROUPS):
                dw_ref[g] = jnp.where(t >= sx, dw_ref[g], 0.0)
            dbs_ref[...] = _reduce_groups(dbias_acc[...])

    return pl.pallas_call(
        body,
        name=name,
        grid=(nsteps,),
        in_specs=[
            pl.BlockSpec((tr, 2 * D_MODEL), lambda i: (i, 0)),
            pl.BlockSpec((tr, D_MODEL), lambda i: (i, 0)),
            _vec(D_MODEL),
            _vec(D_MODEL),
            pl.BlockSpec((A_GROUPS, CHUNK, CHUNK), lambda i: (0, 0, 0)),
            _vec(D_MODEL, CHUNK),
        ],
        out_specs=[
            pl.BlockSpec((tr, 2 * D_MODEL), lambda i: (i, 0)),
            pl.BlockSpec((A_GROUPS, CHUNK, CHUNK), lambda i: (0, 0, 0)),
            _vec(LANES, CHUNK),
            _vec(2 * D_MODEL, SUBLANES),
        ],
        out_shape=[
            jax.ShapeDtypeStruct((s, 2 * D_MODEL), BF16),
            jax.ShapeDtypeStruct((A_GROUPS, CHUNK, CHUNK), F32),
            jax.ShapeDtypeStruct((CHUNK, LANES), F32),
            jax.ShapeDtypeStruct((SUBLANES, 2 * D_MODEL), F32),
        ],
        scratch_shapes=[pltpu.VMEM((CHUNK, D_MODEL), F32)],
        compiler_params=_params(("arbitrary",)),
    )(a_pre, dp, vn_g, vn_b, w_s, bias_full)


def alibi_tables(dilation):
    qi = jnp.arange(SPAN)[:, None]
    ki = jnp.arange(2 * SPAN)[None, :]
    diff = SPAN + qi - ki
    valid = (diff >= 0) & (diff <= SPAN)
    heads = jnp.arange(1, B_HEADS + 1, dtype=F32)
    slopes = jnp.exp2(-8.0 * heads / B_HEADS)
    bias = -slopes[:, None, None] * (dilation * diff).astype(F32)
    bias = jnp.where(valid[None], bias, NEG).reshape(B_HEADS // 2, 2 * SPAN, 2 * SPAN)
    bias = jnp.stack([jnp.where(ki < SPAN, NEG, bias), bias])
    return bias, bias.transpose(0, 1, 3, 2)


def _bias_spec(is_first):
    return pl.BlockSpec(
        (None, B_HEADS // 2, 2 * SPAN, 2 * SPAN), lambda *grid: (jnp.where(is_first(*grid), 0, 1), 0, 0, 0)
    )


def _pair_rows(x, halves):
    return jnp.concatenate([x * halves[0], x * halves[1]], axis=0)


def _pair_column(v, lane, j):
    pick = lambda h: jnp.sum(jnp.where(lane == h, v, 0.0), axis=1, keepdims=True)
    return jnp.concatenate([pick(2 * j), pick(2 * j + 1)], axis=0)


_NT = (((1,), (1,)), ((), ()))


def permute_rows(xs, dilation, *, inverse, name):
    s = xs[0].shape[0]
    tile = SPAN * dilation
    nat = [pl.BlockSpec((tile, x.shape[1]), lambda i: (i, 0)) for x in xs]
    streams = [pl.BlockSpec((dilation, SPAN, x.shape[1]), lambda i: (0, i, 0)) for x in xs]
    stream_shape = [jax.ShapeDtypeStruct((dilation, s // dilation, x.shape[1]), x.dtype) for x in xs]

    def body(*refs):
        for x_ref, o_ref in zip(refs[: len(xs)], refs[len(xs) :], strict=True):
            w = x_ref.shape[-1]
            if inverse:
                o_ref[...] = jnp.swapaxes(x_ref[...], 0, 1).reshape(tile, w)
            else:
                o_ref[...] = jnp.swapaxes(x_ref[...].reshape(SPAN, dilation, w), 0, 1)

    outs = pl.pallas_call(
        body,
        name=name,
        grid=(s // tile,),
        in_specs=streams if inverse else nat,
        out_specs=nat if inverse else streams,
        out_shape=[jax.ShapeDtypeStruct(x.shape, x.dtype) for x in xs] if inverse else stream_shape,
        compiler_params=_params(("parallel",)),
    )(*[x.reshape(dilation, s // dilation, x.shape[1]) if inverse else x for x in xs])
    return [o.reshape(x.shape) for o, x in zip(outs, xs, strict=True)]


def _qkv_specs(block_of):
    def spec(which, prev):
        def index(*grid):
            blk = block_of(*grid)
            return (jnp.maximum(blk - 1, 0) if prev else blk, which)

        return pl.BlockSpec((SPAN, D_MODEL), index)

    return [spec(0, False), spec(1, True), spec(1, False), spec(2, True), spec(2, False)]


def attn_fwd(qkv_p, pat, *, name):
    _, dilation = B_PATTERNS[pat]
    nb = SEQ // dilation // SPAN
    bias, _ = alibi_tables(dilation)

    def body(q_ref, kp_ref, kc_ref, vp_ref, vc_ref, bias_ref, o_ref, lse_ref):
        lane = _lane((SPAN, LANES))
        lo_mask = lane < B_HEAD_DIM
        q = q_ref[...] * jnp.asarray(B_HEAD_DIM**-0.5, BF16)
        kk = jnp.concatenate([kp_ref[...], kc_ref[...]], axis=0)
        vv = jnp.concatenate([vp_ref[...], vc_ref[...]], axis=0)
        halves = (lo_mask.astype(BF16), jnp.logical_not(lo_mask).astype(BF16))
        stats = jnp.zeros((SPAN, LANES), F32)
        for j in range(B_HEADS // 2):
            cs = slice(j * LANES, (j + 1) * LANES)
            sc = lax.dot_general(_pair_rows(q[:, cs], halves), kk[:, cs], _NT, preferred_element_type=F32)
            sc = sc + bias_ref[j]
            m = jnp.max(sc, axis=1, keepdims=True)
            p = jnp.exp(sc - m)
            l = jnp.sum(p, axis=1, keepdims=True)
            acc = jnp.dot(p.astype(BF16), vv[:, cs], preferred_element_type=F32) * (1.0 / l)
            lse_pair = m + jnp.log(l)
            o_ref[:, cs] = jnp.where(lo_mask, acc[:SPAN], acc[SPAN:]).astype(BF16)
            stats = jnp.where(lane == 2 * j, lse_pair[:SPAN], stats)
            stats = jnp.where(lane == 2 * j + 1, lse_pair[SPAN:], stats)
        lse_ref[...] = stats

    return pl.pallas_call(
        body,
        name=name,
        grid=(dilation, nb),
        in_specs=[
            *_qkv_specs(lambda r, n: r * nb + n),
            _bias_spec(lambda r, n: n == 0),
        ],
        out_specs=[
            pl.BlockSpec((SPAN, D_MODEL), lambda r, n: (r * nb + n, 0)),
            pl.BlockSpec((SPAN, LANES), lambda r, n: (r * nb + n, 0)),
        ],
        out_shape=[jax.ShapeDtypeStruct((SEQ, D_MODEL), BF16), jax.ShapeDtypeStruct((SEQ, LANES), F32)],
        compiler_params=_params(("parallel", "arbitrary")),
    )(qkv_p, qkv_p, qkv_p, qkv_p, qkv_p, bias)


def attn_combine(outs, lses, *, name):
    def body(o0, o1, o2, l0, l1, l2, ob_ref, of_ref, lse_ref):
        ls = [l0[...], l1[...], l2[...]]
        m = jnp.maximum(jnp.maximum(ls[0], ls[1]), ls[2])
        tot = jnp.log(jnp.exp(ls[0] - m) + jnp.exp(ls[1] - m) + jnp.exp(ls[2] - m)) + m
        o = None
        for o_ref, l in zip((o0, o1, o2), ls, strict=True):
            term = _expand_groups(jnp.exp(l - tot)) * o_ref[...]
            o = term if o is None else o + term
        ob_ref[...] = o.astype(BF16)
        of_ref[...] = o
        lse_ref[...] = tot

    return pl.pallas_call(
        body,
        name=name,
        grid=(SEQ // ROW_TILE,),
        in_specs=[_rows(D_MODEL)] * 3 + [_rows(LANES)] * 3,
        out_specs=[_rows(D_MODEL), _rows(D_MODEL), _rows(LANES)],
        out_shape=[
            jax.ShapeDtypeStruct((SEQ, D_MODEL), BF16),
            jax.ShapeDtypeStruct((SEQ, D_MODEL), F32),
            jax.ShapeDtypeStruct((SEQ, LANES), F32),
        ],
        compiler_params=_params(("parallel",)),
    )(*outs, *lses)


def attn_delta(do, o, *, name):
    def body(do_ref, o_ref, dob_ref, dl_ref):
        do_v = do_ref[...]
        dob_ref[...] = do_v.astype(BF16)
        dl_ref[...] = _reduce_groups(do_v * o_ref[...])

    return pl.pallas_call(
        body,
        name=name,
        grid=(SEQ // ROW_TILE,),
        in_specs=[_rows(D_MODEL), _rows(D_MODEL)],
        out_specs=[_rows(D_MODEL), _rows(LANES)],
        out_shape=[jax.ShapeDtypeStruct((SEQ, D_MODEL), BF16), jax.ShapeDtypeStruct((SEQ, LANES), F32)],
        compiler_params=_params(("parallel",)),
    )(do, o)


def attn_bwd(qkv_p, do_p, lse_p, delta_p, pat, *, name):
    _, dilation = B_PATTERNS[pat]
    nb = SEQ // dilation // SPAN
    n_blocks = SEQ // SPAN
    bias, bias_t = alibi_tables(dilation)
    last = n_blocks - 1
    q_cols, k_cols, v_cols = (slice(i * D_MODEL, (i + 1) * D_MODEL) for i in range(3))

    def body(q_ref, kp_ref, kc_ref, vp_ref, vc_ref, do_ref, lse_ref, dl_ref, bias_ref, biast_ref, out_ref, cq_ref, ck_ref, cv_ref):
        g = pl.program_id(0)

        @pl.when(g == n_blocks)
        def _():
            out_ref[:, q_cols] = cq_ref[...].astype(BF16)
            out_ref[:, k_cols] = ck_ref[...].astype(BF16)
            out_ref[:, v_cols] = cv_ref[...].astype(BF16)

        @pl.when(g == 0)
        def _():
            cq_ref[...] = jnp.zeros_like(cq_ref)
            ck_ref[...] = jnp.zeros_like(ck_ref)
            cv_ref[...] = jnp.zeros_like(cv_ref)

        @pl.when(g < n_blocks)
        def _():
            lane = _lane((SPAN, LANES))
            lo_mask = lane < B_HEAD_DIM
            q = q_ref[...] * jnp.asarray(B_HEAD_DIM**-0.5, BF16)
            kk = jnp.concatenate([kp_ref[...], kc_ref[...]], axis=0)
            vv = jnp.concatenate([vp_ref[...], vc_ref[...]], axis=0)
            do_v = do_ref[...]
            lse_v = lse_ref[...]
            dl_v = dl_ref[...]
            lse_t = lse_v.T
            dl_t = dl_v.T
            halves = (lo_mask.astype(BF16), jnp.logical_not(lo_mask).astype(BF16))
            for j in range(B_HEADS // 2):
                cs = slice(j * LANES, (j + 1) * LANES)
                kp, vp = kk[:, cs], vv[:, cs]
                q2 = _pair_rows(q[:, cs], halves)
                do2 = _pair_rows(do_v[:, cs], halves)
                lse_c, dl_c = _pair_column(lse_v, lane, j), _pair_column(dl_v, lane, j)
                lse_r = jnp.concatenate([lse_t[2 * j : 2 * j + 1], lse_t[2 * j + 1 : 2 * j + 2]], axis=1)
                dl_r = jnp.concatenate([dl_t[2 * j : 2 * j + 1], dl_t[2 * j + 1 : 2 * j + 2]], axis=1)
                sc = lax.dot_general(q2, kp, _NT, preferred_element_type=F32)
                p = jnp.exp(sc + bias_ref[j] - lse_c)
                dp = lax.dot_general(do2, vp, _NT, preferred_element_type=F32)
                ds = (p * (dp - dl_c)).astype(BF16)
                dq2 = jnp.dot(ds, kp, preferred_element_type=F32)
                sc_t = lax.dot_general(kp, q2, _NT, preferred_element_type=F32)
                p_t = jnp.exp(sc_t + biast_ref[j] - lse_r)
                dp_t = lax.dot_general(vp, do2, _NT, preferred_element_type=F32)
                ds_t = (p_t * (dp_t - dl_r)).astype(BF16)
                dk_pair = jnp.dot(ds_t, q2, preferred_element_type=F32)
                dv_pair = jnp.dot(p_t.astype(BF16), do2, preferred_element_type=F32)
                oq = slice(j * LANES, (j + 1) * LANES)
                ok = slice(D_MODEL + j * LANES, D_MODEL + (j + 1) * LANES)
                ov = slice(2 * D_MODEL + j * LANES, 2 * D_MODEL + (j + 1) * LANES)
                out_ref[:, oq] = cq_ref[:, cs].astype(BF16)
                out_ref[:, ok] = (ck_ref[:, cs] + dk_pair[:SPAN]).astype(BF16)
                out_ref[:, ov] = (cv_ref[:, cs] + dv_pair[:SPAN]).astype(BF16)
                cq_ref[:, cs] = jnp.where(lo_mask, dq2[:SPAN], dq2[SPAN:]) * (B_HEAD_DIM**-0.5)
                ck_ref[:, cs] = dk_pair[SPAN:]
                cv_ref[:, cs] = dv_pair[SPAN:]

    def block_of(g):
        return jnp.minimum(g, last)

    def row_spec(width):
        return pl.BlockSpec((SPAN, width), lambda g: (block_of(g), 0))

    return pl.pallas_call(
        body,
        name=name,
        grid=(n_blocks + 1,),
        in_specs=[
            *_qkv_specs(block_of),
            row_spec(D_MODEL),
            row_spec(LANES),
            row_spec(LANES),
            _bias_spec(lambda g: lax.rem(block_of(g), nb) == 0),
            _bias_spec(lambda g: lax.rem(block_of(g), nb) == 0),
        ],
        out_specs=pl.BlockSpec((SPAN, 3 * D_MODEL), lambda g: (jnp.maximum(g - 1, 0), 0)),
        out_shape=jax.ShapeDtypeStruct((SEQ, 3 * D_MODEL), BF16),
        scratch_shapes=[pltpu.VMEM((SPAN, D_MODEL), F32)] * 3,
        compiler_params=_params(("arbitrary",)),
    )(qkv_p, qkv_p, qkv_p, qkv_p, qkv_p, do_p, lse_p, delta_p, bias, bias_t)


def _position():
    x, y, c = lax.axis_index("x"), lax.axis_index("y"), lax.axis_index("c")
    return x, y, c, 4 * x + 2 * y + c


def _peer(k, x, y, c):
    px = 1 - x if k & 4 else x
    py = 1 - y if k & 2 else y
    pc = 1 - c if k & 1 else c
    return (px, py, pc), 4 * px + 2 * py + pc


def _remote(src, dst, send_sem, recv_sem, device):
    return pltpu.make_async_remote_copy(
        src_ref=src, dst_ref=dst, send_sem=send_sem, recv_sem=recv_sem, device_id=device, device_id_type=MESH
    )


def _silu_bf16(cf):
    return (cf * (1.0 / (1.0 + jnp.exp(-cf)))).astype(BF16)


def ada_exchange(c8, w4, b4, ln8):
    nt, _, ncol = w4.shape

    def body(c8_ref, w_ref, b_ref, ln_ref, cg_ref, lng_ref, mrecv_ref, mloc_ref, send_sems, recv_sems):
        x, y, c, me = _position()
        cg_ref[me] = c8_ref[...]
        lng_ref[me] = ln_ref[...]
        first = []
        for k in range(1, N_DEV):
            dev, _ = _peer(k, x, y, c)
            first.append(_remote(c8_ref, cg_ref.at[me], send_sems.at[0, k], recv_sems.at[0, k], dev))
            first.append(_remote(ln_ref, lng_ref.at[me], send_sems.at[1, k], recv_sems.at[1, k], dev))
        for cp in first:
            cp.start()
        for k in range(1, N_DEV):
            dev, pid = _peer(k, x, y, c)
            _remote(c8_ref, cg_ref.at[pid], send_sems.at[0, k], recv_sems.at[0, k], dev).wait_recv()
            _remote(ln_ref, lng_ref.at[pid], send_sems.at[1, k], recv_sems.at[1, k], dev).wait_recv()
        sc = _silu_bf16(cg_ref[...].reshape(N_DEV * SUBLANES, D_MODEL))
        for t in range(nt):
            mloc_ref[t] = jnp.dot(sc, w_ref[t].astype(BF16), preferred_element_type=F32) + b_ref[t : t + 1, :]

        def group(dev_id):
            return pl.ds(pl.multiple_of(dev_id * SUBLANES, SUBLANES), SUBLANES)

        mrecv_ref[me] = mloc_ref[:, group(me), :]
        second = []
        for k in range(1, N_DEV):
            dev, pid = _peer(k, x, y, c)
            second.append(
                _remote(mloc_ref.at[:, group(pid), :], mrecv_ref.at[me], send_sems.at[2, k], recv_sems.at[2, k], dev)
            )
        for cp in second:
            cp.start()
        for k in range(1, N_DEV):
            dev, pid = _peer(k, x, y, c)
            _remote(
                mloc_ref.at[:, group(pid), :], mrecv_ref.at[pid], send_sems.at[2, k], recv_sems.at[2, k], dev
            ).wait_recv()
        for cp in first + second:
            cp.wait_send()

    return pl.pallas_call(
        body,
        name="ada_exchange",
        in_specs=[VMEM, VMEM, VMEM, VMEM],
        out_specs=[VMEM, VMEM, VMEM],
        out_shape=[
            jax.ShapeDtypeStruct((N_DEV, SUBLANES, D_MODEL), F32),
            jax.ShapeDtypeStruct((N_DEV, SUBLANES, LANES), F32),
            jax.ShapeDtypeStruct((N_DEV, nt, SUBLANES, ncol), F32),
        ],
        scratch_shapes=[
            pltpu.VMEM((nt, N_DEV * SUBLANES, ncol), F32),
            pltpu.SemaphoreType.DMA((3, N_DEV)),
            pltpu.SemaphoreType.DMA((3, N_DEV)),
        ],
        compiler_params=pltpu.CompilerParams(vmem_limit_bytes=VMEM_LIMIT_BYTES),
    )(c8, w4, b4, ln8)


def small_exchange(dmx, flat):
    def body(dmx_ref, flat_ref, dmrecv_ref, red_ref, land_ref, send_sems, recv_sems):
        x, y, c, me = _position()
        dmrecv_ref[me] = dmx_ref[me]
        land_ref[me] = flat_ref[me]
        first = []
        for k in range(1, N_DEV):
            dev, pid = _peer(k, x, y, c)
            first.append(_remote(dmx_ref.at[pid], dmrecv_ref.at[me], send_sems.at[0, k], recv_sems.at[0, k], dev))
            first.append(_remote(flat_ref.at[pid], land_ref.at[me], send_sems.at[1, k], recv_sems.at[1, k], dev))
        for cp in first:
            cp.start()
        for k in range(1, N_DEV):
            dev, pid = _peer(k, x, y, c)
            _remote(dmx_ref.at[pid], dmrecv_ref.at[pid], send_sems.at[0, k], recv_sems.at[0, k], dev).wait_recv()
            _remote(flat_ref.at[pid], land_ref.at[pid], send_sems.at[1, k], recv_sems.at[1, k], dev).wait_recv()
        total = land_ref[0]
        for s in range(1, N_DEV):
            total = total + land_ref[s]
        red_ref[me] = total
        second = []
        for k in range(1, N_DEV):
            dev, _ = _peer(k, x, y, c)
            second.append(_remote(red_ref.at[me], red_ref.at[me], send_sems.at[2, k], recv_sems.at[2, k], dev))
        for cp in second:
            cp.start()
        for k in range(1, N_DEV):
            dev, pid = _peer(k, x, y, c)
            _remote(red_ref.at[pid], red_ref.at[pid], send_sems.at[2, k], recv_sems.at[2, k], dev).wait_recv()
        for cp in first + second:
            cp.wait_send()

    return pl.pallas_call(
        body,
        name="small_exchange",
        in_specs=[VMEM, VMEM],
        out_specs=[VMEM, VMEM],
        out_shape=[jax.ShapeDtypeStruct(dmx.shape, F32), jax.ShapeDtypeStruct(flat.shape, F32)],
        scratch_shapes=[
            pltpu.VMEM(flat.shape, F32),
            pltpu.SemaphoreType.DMA((3, N_DEV)),
            pltpu.SemaphoreType.DMA((3, N_DEV)),
        ],
        compiler_params=pltpu.CompilerParams(vmem_limit_bytes=VMEM_LIMIT_BYTES),
    )(dmx, flat)


HBM = pl.BlockSpec(memory_space=pltpu.HBM)
SEM = pl.BlockSpec(memory_space=pltpu.SEMAPHORE)
EFFECT = pltpu.SideEffectType.DATAFLOW_SIDE_EFFECTING


REGROUP_ROWS = 256


def shards_to_columns(x, *, name):
    p, k, n = x.shape

    def body(x_ref, o_ref):
        for s in range(p):
            o_ref[:, s * n : (s + 1) * n] = x_ref[s]

    return pl.pallas_call(
        body,
        name=name,
        grid=(k // REGROUP_ROWS,),
        in_specs=[pl.BlockSpec((p, REGROUP_ROWS, n), lambda i: (0, i, 0))],
        out_specs=pl.BlockSpec((REGROUP_ROWS, p * n), lambda i: (i, 0)),
        out_shape=jax.ShapeDtypeStruct((k, p * n), x.dtype),
        compiler_params=_params(("parallel",)),
    )(x)


def columns_to_shards(xs, *, name):
    k = xs[0].shape[0]
    widths = [x.shape[1] for x in xs]
    n = sum(widths) // N_DEV
    pieces = []
    for s in range(N_DEV):
        start = 0
        for i, w in enumerate(widths):
            lo, hi = max(start, s * n), min(start + w, (s + 1) * n)
            if lo < hi:
                pieces.append((i, lo - start, s, lo - s * n, hi - lo))
            start += w

    def body(*refs):
        x_refs, o_ref = refs[: len(xs)], refs[-1]
        for i, c0, s, d0, w in pieces:
            o_ref[s, :, d0 : d0 + w] = x_refs[i][:, c0 : c0 + w]

    return pl.pallas_call(
        body,
        name=name,
        grid=(k // REGROUP_ROWS,),
        in_specs=[pl.BlockSpec((REGROUP_ROWS, w), lambda i: (i, 0)) for w in widths],
        out_specs=pl.BlockSpec((N_DEV, REGROUP_ROWS, n), lambda i: (0, i, 0)),
        out_shape=jax.ShapeDtypeStruct((N_DEV, k, n), xs[0].dtype),
        compiler_params=_params(("parallel",)),
    )(*xs)


def _own_slot(me, block):
    land = lax.empty((N_DEV, *block.shape), block.dtype)
    return lax.dynamic_update_slice_in_dim(land, block[None], me, axis=0)


N_CHIP_PEERS = 3


class Gather:
    def __init__(self, shards, lands, after, *, name):
        nt = len(shards)
        self.name = name

        def body(*refs):
            src_refs, land_refs = refs[:nt], refs[nt : 2 * nt]
            send_sems, recv_sems = refs[2 * nt + 1 : 3 * nt + 1], refs[3 * nt + 1 : 4 * nt + 1]
            token = refs[-1]
            x, y, c, me = _position()
            for t in range(nt):
                for k, dev in enumerate(self._targets(x, y, c)):
                    _remote(src_refs[t], land_refs[t].at[me], send_sems[t].at[k], recv_sems[t].at[k], dev).start()
            token[...] = jnp.zeros_like(token)

        outs = pl.pallas_call(
            body,
            name=name + "_start",
            in_specs=[HBM] * (2 * nt) + [ANY],
            out_specs=[SEM] * (2 * nt) + [HBM] * (2 * nt) + [VMEM],
            out_shape=[pltpu.SemaphoreType.DMA((1 + N_CHIP_PEERS,))] * (2 * nt)
            + [pltpu.HBM(a.shape, a.dtype) for a in (*shards, *lands)]
            + [jax.ShapeDtypeStruct((SUBLANES, LANES), F32)],
            input_output_aliases={i: 2 * nt + i for i in range(2 * nt)},
            compiler_params=pltpu.CompilerParams(has_side_effects=EFFECT),
        )(*[pltpu.with_memory_space_constraint(a, pltpu.HBM) for a in (*shards, *lands)], after)
        self.send_sems, self.recv_sems = list(outs[:nt]), list(outs[nt : 2 * nt])
        self.srcs, self.lands = list(outs[2 * nt : 3 * nt]), list(outs[3 * nt : 4 * nt])
        self.token = outs[-1]

    @staticmethod
    def _chips(x, y):
        return [(1 - x, y), (x, 1 - y), (1 - x, 1 - y)]

    @classmethod
    def _targets(cls, x, y, c):
        return [(x, y, 1 - c)] + [(*chip, c) for chip in cls._chips(x, y)]

    def zero(self):
        return self.token[0, 0]

    def wait(self, which, after, *, name):
        n = len(which)

        def slot(px, py, pc):
            return 4 * px + 2 * py + pc

        def pass_body(*refs):
            land_refs, recv_sems = refs[:n], refs[n : 2 * n]
            fwd_send, fwd_recv = refs[3 * n + 1 : 4 * n + 1], refs[4 * n + 1 : 5 * n + 1]
            x, y, c, _ = _position()
            for t in range(n):
                for j, chip in enumerate(self._chips(x, y)):
                    blk = land_refs[t].at[slot(*chip, c)]
                    _remote(blk, blk, fwd_send[t].at[j], recv_sems[t].at[1 + j], (*chip, c)).wait_recv()
                    _remote(blk, blk, fwd_send[t].at[j], fwd_recv[t].at[j], (x, y, 1 - c)).start()

        lands = [self.lands[t] for t in which]
        outs = pl.pallas_call(
            pass_body,
            name=name + "_pass",
            in_specs=[HBM] * n + [SEM] * n + [ANY],
            out_specs=[HBM] * n + [SEM] * (2 * n),
            out_shape=[pltpu.HBM(a.shape, a.dtype) for a in lands] + [pltpu.SemaphoreType.DMA((N_CHIP_PEERS,))] * (2 * n),
            input_output_aliases={i: i for i in range(n)},
            compiler_params=pltpu.CompilerParams(has_side_effects=EFFECT),
        )(*lands, *[self.recv_sems[t] for t in which], after)
        lands, fwd_send, fwd_recv = outs[:n], outs[n : 2 * n], outs[2 * n :]

        def wait_body(*refs):
            src_refs, land_refs = refs[:n], refs[n : 2 * n]
            send_sems, recv_sems = refs[2 * n : 3 * n], refs[3 * n : 4 * n]
            fwd_send, fwd_recv = refs[4 * n : 5 * n], refs[5 * n : 6 * n]
            x, y, c, me = _position()
            sibling = (x, y, 1 - c)
            for t in range(n):
                for k, dev in enumerate(self._targets(x, y, c)):
                    _remote(src_refs[t], land_refs[t].at[me], send_sems[t].at[k], recv_sems[t].at[k], dev).wait_send()
                blk = land_refs[t].at[slot(x, y, 1 - c)]
                _remote(blk, blk, send_sems[t].at[0], recv_sems[t].at[0], sibling).wait_recv()
                for j, chip in enumerate(self._chips(x, y)):
                    sent = land_refs[t].at[slot(*chip, c)]
                    _remote(sent, sent, fwd_send[t].at[j], fwd_recv[t].at[j], sibling).wait_send()
                    got = land_refs[t].at[slot(*chip, 1 - c)]
                    _remote(got, got, fwd_send[t].at[j], fwd_recv[t].at[j], sibling).wait_recv()

        srcs = [self.srcs[t] for t in which]
        outs = pl.pallas_call(
            wait_body,
            name=name,
            in_specs=[HBM] * (2 * n) + [SEM] * (4 * n),
            out_specs=[HBM] * (2 * n),
            out_shape=[pltpu.HBM(a.shape, a.dtype) for a in (*srcs, *lands)],
            input_output_aliases={i: i for i in range(2 * n)},
            compiler_params=pltpu.CompilerParams(has_side_effects=EFFECT),
        )(*srcs, *lands, *[self.send_sems[t] for t in which], *[self.recv_sems[t] for t in which], *fwd_send, *fwd_recv)
        return outs[n:]


class Scatter:
    def __init__(self, srcs, lands, after, *, name):
        self.name = name
        nt = self.nt = len(srcs)
        peers = N_DEV - 1

        def body(*refs):
            src_refs, land_refs = refs[:nt], refs[nt : 2 * nt]
            send_sems, recv_sems = refs[2 * nt + 1 : 3 * nt + 1], refs[3 * nt + 1 : 4 * nt + 1]
            token = refs[-1]
            x, y, c, me = _position()
            for t in range(nt):
                for k in range(1, N_DEV):
                    dev, pid = _peer(k, x, y, c)
                    src = src_refs[t].at[pid]
                    _remote(src, land_refs[t].at[me], send_sems[t].at[k - 1], recv_sems[t].at[k - 1], dev).start()
            token[...] = jnp.zeros_like(token)

        outs = pl.pallas_call(
            body,
            name=name + "_start",
            in_specs=[HBM] * (2 * nt) + [ANY],
            out_specs=[SEM] * (2 * nt) + [HBM] * (2 * nt) + [VMEM],
            out_shape=[pltpu.SemaphoreType.DMA((peers,))] * (2 * nt)
            + [pltpu.HBM(a.shape, a.dtype) for a in (*srcs, *lands)]
            + [jax.ShapeDtypeStruct((SUBLANES, LANES), F32)],
            input_output_aliases={i: 2 * nt + i for i in range(2 * nt)},
            compiler_params=pltpu.CompilerParams(has_side_effects=EFFECT),
        )(*[pltpu.with_memory_space_constraint(a, pltpu.HBM) for a in (*srcs, *lands)], after)
        self.send_sems, self.recv_sems = outs[:nt], outs[nt : 2 * nt]
        self.srcs, self.lands = outs[2 * nt : 3 * nt], outs[3 * nt : 4 * nt]
        self.token = outs[-1]

    def zero(self):
        return self.token[0, 0]

    def wait(self, which, after, *, name):
        n = len(which)

        def body(*refs):
            src_refs, land_refs = refs[:n], refs[n : 2 * n]
            send_sems, recv_sems = refs[2 * n : 3 * n], refs[3 * n : 4 * n]
            x, y, c, _ = _position()
            for t in range(n):
                for k in range(1, N_DEV):
                    dev, pid = _peer(k, x, y, c)
                    src = src_refs[t].at[pid]
                    cp = _remote(src, land_refs[t].at[pid], send_sems[t].at[k - 1], recv_sems[t].at[k - 1], dev)
                    cp.wait_send()
                    cp.wait_recv()

        srcs = [self.srcs[t] for t in which]
        lands = [self.lands[t] for t in which]
        outs = pl.pallas_call(
            body,
            name=name,
            in_specs=[HBM] * (2 * n) + [SEM] * (2 * n) + [ANY],
            out_specs=[HBM] * (2 * n),
            out_shape=[pltpu.HBM(a.shape, a.dtype) for a in (*srcs, *lands)],
            input_output_aliases={i: i for i in range(2 * n)},
            compiler_params=pltpu.CompilerParams(has_side_effects=EFFECT),
        )(*srcs, *lands, *[self.send_sems[t] for t in which], *[self.recv_sems[t] for t in which], after)
        return outs[n:]


def _adam_update(g, w, m, v):
    m2 = ADAM_B1 * m + (1.0 - ADAM_B1) * g
    v2 = ADAM_B2 * v + (1.0 - ADAM_B2) * jnp.square(g)
    m_hat = m2 / (1.0 - ADAM_B1**ADAM_STEP)
    v_hat = v2 / (1.0 - ADAM_B2**ADAM_STEP)
    delta = -ADAM_LR * (m_hat / (jnp.sqrt(v_hat) + ADAM_EPS) + ADAM_WD * w)
    return delta, m2, v2


def adamw(gparts, w, m, v, *, name):
    nl, r, c = w.shape
    p = gparts[0].shape[0]
    tr = r if r <= 256 else (256 if c <= D_MODEL else 128)
    ni = r // tr

    def body(*refs):
        g_refs = refs[:nl]
        w_ref, m_ref, v_ref, go_ref, d_ref, mo_ref, vo_ref = refs[nl:]
        for layer in range(nl):

            @pl.when(pl.program_id(0) == layer)
            def _(g_ref=g_refs[layer]):
                g = g_ref[0].astype(F32)
                for i in range(1, p):
                    g = g + g_ref[i].astype(F32)
                delta, m2, v2 = _adam_update(g, w_ref[...], m_ref[...], v_ref[...])
                go_ref[...] = g
                d_ref[...] = delta
                mo_ref[...] = m2
                vo_ref[...] = v2

    def parts_spec(layer):
        def index(l, i):
            return (0, jnp.where(l == layer, i, jnp.where(l < layer, 0, ni - 1)), 0)

        return pl.BlockSpec((p, tr, c), index)

    blk = pl.BlockSpec((None, tr, c), lambda l, i: (l, i, 0))
    return pl.pallas_call(
        body,
        name=name,
        grid=(nl, ni),
        in_specs=[*[parts_spec(layer) for layer in range(nl)], blk, blk, blk],
        out_specs=[blk] * 4,
        out_shape=[jax.ShapeDtypeStruct((nl, r, c), F32)] * 4,
        compiler_params=_params(("arbitrary", "arbitrary")),
    )(*gparts, w, m, v)


def ada_grad_adamw(cg, dmrecv, w4, m4, v4, *, name):
    nt, k, ncol = w4.shape

    def body(cg_ref, dm_ref, w_ref, m_ref, v_ref, go_ref, d_ref, mo_ref, vo_ref, gb_ref):
        sc = _silu_bf16(cg_ref[...].reshape(N_DEV * SUBLANES, k))
        dm = dm_ref[...].reshape(N_DEV * SUBLANES, ncol)
        g = lax.dot_general(sc, dm.astype(BF16), (((0,), (0,)), ((), ())), preferred_element_type=F32)
        delta, m2, v2 = _adam_update(g, w_ref[...], m_ref[...], v_ref[...])
        go_ref[...] = g
        d_ref[...] = delta
        mo_ref[...] = m2
        vo_ref[...] = v2
        gb_ref[...] = jnp.broadcast_to(jnp.sum(dm, axis=0, keepdims=True), (SUBLANES, ncol))

    wblk = pl.BlockSpec((None, k, ncol), lambda t: (t, 0, 0))
    return pl.pallas_call(
        body,
        name=name,
        grid=(nt,),
        in_specs=[
            pl.BlockSpec((N_DEV, SUBLANES, k), lambda t: (0, 0, 0)),
            pl.BlockSpec((N_DEV, None, SUBLANES, ncol), lambda t: (0, t, 0, 0)),
            wblk,
            wblk,
            wblk,
        ],
        out_specs=[wblk] * 4 + [pl.BlockSpec((None, SUBLANES, ncol), lambda t: (t, 0, 0))],
        out_shape=[jax.ShapeDtypeStruct((nt, k, ncol), F32)] * 4 + [jax.ShapeDtypeStruct((nt, SUBLANES, ncol), F32)],
        compiler_params=_params(("parallel",)),
    )(cg, dmrecv, w4, m4, v4)


def kernel(x, c, ada_w, ada_b, ln_g, ln_b, a_w_in, a_b_in, a_vn_g, a_vn_b, a_w_s, a_b_s, a_w_out, b_w_qkv, b_w_out, mlp_w_up, mlp_w_down, loss_target, m_ada_w, m_ada_b, m_ln_g, m_ln_b, m_a_w_in, m_a_b_in, m_a_vn_g, m_a_vn_b, m_a_w_s, m_a_b_s, m_a_w_out, m_b_w_qkv, m_b_w_out, m_mlp_w_up, m_mlp_w_down, v_ada_w, v_ada_b, v_ln_g, v_ln_b, v_a_w_in, v_a_b_in, v_a_vn_g, v_a_vn_b, v_a_w_s, v_a_b_s, v_a_w_out, v_b_w_qkv, v_b_w_out, v_mlp_w_up, v_mlp_w_down):
    x0 = x[0]
    target = loss_target[0]
    me = 4 * lax.axis_index("x") + 2 * lax.axis_index("y") + lax.axis_index("c")

    ada_w4 = ada_w.reshape(N_SUB, D_MODEL, -1)
    ada_b4 = ada_b.reshape(N_SUB, -1)
    ln8 = jnp.concatenate([ln_g.reshape(N_SUB, -1), ln_b.reshape(N_SUB, -1)], axis=0)
    c8 = jnp.broadcast_to(c, (SUBLANES, D_MODEL))
    cg, lng, mrecv = ada_exchange(c8, ada_w4, ada_b4, ln8)

    W_IN, W_AOUT, W_UP0, W_DN0, W_QKV, W_BOUT, W_UP1, W_DN1 = range(8)
    shards = [
        a_w_in[0].astype(BF16),
        a_w_out[0].astype(BF16),
        mlp_w_up[0].astype(BF16),
        mlp_w_down[0].astype(BF16),
        b_w_qkv[0].astype(BF16),
        b_w_out[0].astype(BF16),
        mlp_w_up[1].astype(BF16),
        mlp_w_down[1].astype(BF16),
    ]
    gather = Gather(shards, [_own_slot(me, s) for s in shards], mrecv, name="gather")

    modv = mrecv[:, :, 0, :].transpose(1, 0, 2).reshape(N_SUB, 3 * D_MODEL) + gather.zero()
    shift = [modv[t : t + 1, :D_MODEL] for t in range(N_SUB)]
    scale = [modv[t : t + 1, D_MODEL : 2 * D_MODEL] for t in range(N_SUB)]
    gate1 = [1.0 + modv[t : t + 1, 2 * D_MODEL :] for t in range(N_SUB)]
    lng_full = [lng[:, t, :].reshape(1, D_MODEL) for t in range(N_SUB)]
    lnb_full = [lng[:, N_SUB + t, :].reshape(1, D_MODEL) for t in range(N_SUB)]

    ident = lambda acc: (acc,)
    def relu2(a):
        r = jnp.maximum(a, jnp.zeros_like(a))
        return r * r
    vn_g, vn_b, w_s = a_vn_g, a_vn_b, a_w_s[0]
    bias_full = jnp.repeat(a_b_s[0].T, A_GROUP_DIM, axis=1)
    w_up3, w_dn3 = [None, None], [None, None]

    def mlp_forward(i, h, after):
        up, dn = gather.wait([W_UP0, W_DN0] if i == 0 else [W_UP1, W_DN1], after, name=f"gather_wait_mlp{i}")
        w_up3[i], w_dn3[i] = up, dn.reshape(1, D_FF, D_MODEL)
        (a,) = mm_nn(h, w_up3[i], name=f"mlp{i}_up", tm=2048, ps=2, tn=512, tk=D_MODEL, epilogue=ident, outs=(BF16,))
        (y,) = mm_nn(
            a, w_dn3[i], name=f"mlp{i}_down", tm=1024, ps=1, tn=512, tk=D_FF, prologue=relu2, epilogue=ident, outs=(BF16,)
        )
        return a, y

    h0 = modulate(x0, scale[0], shift[0], name="modulate0")
    w_in3, w_aout3 = gather.wait([W_IN, W_AOUT], h0, name="gather_wait_a")
    w_aout3 = w_aout3.reshape(1, D_MODEL, D_MODEL)
    (a_pre,) = mm_nn(
        h0, w_in3, name="a_in", tm=2048, ps=4, tn=256, tk=D_MODEL, epilogue=lambda acc, b: (acc + b,),
        extras=[(a_b_in, "row")], outs=(BF16,),
    )
    p_gate = gate_fwd(a_pre, vn_g, vn_b, w_s, bias_full, name="gate_fwd")
    (y0,) = mm_nn(p_gate, w_aout3, name="a_out", tm=2048, ps=1, tn=D_MODEL, tk=D_MODEL, epilogue=ident, outs=(BF16,))
    x1, h1 = residual_ln(x0, y0, gate1[0], lng_full[0], lnb_full[0], scale[1], shift[1], name="res_ln0")
    a1, y1 = mlp_forward(0, h1, y0)
    x2, h2 = residual_ln(x1, y1, gate1[1], lng_full[1], lnb_full[1], scale[2], shift[2], name="res_ln1")
    w_qkv_shards, w_bout3 = gather.wait([W_QKV, W_BOUT], y1, name="gather_wait_b")
    w_bout3 = w_bout3.reshape(1, D_MODEL, D_MODEL)
    w_qkv3 = shards_to_columns(w_qkv_shards, name="w_qkv_columns")[None]
    pat_tiles = 3
    dil = [d for _, d in B_PATTERNS]
    qkv_p, pat_o, pat_lse = [], [], []
    for g in range(N_PAT):
        (qkv_g,) = mm_nn(
            h2, w_qkv3, name=f"b_qkv{g}", tm=2048, ps=1, tn=D_MODEL, tk=D_MODEL, epilogue=ident, outs=(BF16,),
            b_tile0=pat_tiles * g, b_tiles=pat_tiles, out_streams=dil[g],
        )
        o_g, lse_g = attn_fwd(qkv_g, g, name=f"attn_fwd{g}")
        if g > 0:
            o_g, lse_g = permute_rows([o_g, lse_g], dil[g], inverse=True, name=f"unperm_o{g}")
        qkv_p.append(qkv_g)
        pat_o.append(o_g)
        pat_lse.append(lse_g)
    o_b, o_f, lse = attn_combine(pat_o, pat_lse, name="attn_combine")
    (y2,) = mm_nn(o_b, w_bout3, name="b_out", tm=2048, ps=1, tn=D_MODEL, tk=D_MODEL, epilogue=ident, outs=(BF16,))
    x3, h3 = residual_ln(x2, y2, gate1[2], lng_full[2], lnb_full[2], scale[3], shift[3], name="res_ln2")
    a3, y3 = mlp_forward(1, h3, y2)
    dxo, loss_local = residual_ln_loss(x3, y3, gate1[3], lng_full[3], lnb_full[3], target, name="res_ln3_loss")

    def scatter(parts, after, name):
        parts = [p.reshape(N_DEV, -1, p.shape[-1]) for p in parts]
        lands = [_own_slot(me, lax.dynamic_index_in_dim(p, me, 0, keepdims=False)) for p in parts]
        return Scatter(parts, lands, after, name=name)

    def mlp_backward(i, h, a, dy):
        (da,) = mm_nt(
            dy,
            w_dn3[i],
            name=f"mlp{i}_da",
            tm=2048,
            tko=1024,
            ps=1,
            tc=D_MODEL,
            epilogue=lambda acc, act: (acc * (2.0 * jnp.maximum(act.astype(F32), 0.0)),),
            extras=[(a, "full")],
            outs=(BF16,),
        )
        (dh,) = mm_nt(da, w_up3[i], name=f"mlp{i}_dh", tm=1024, tko=512, ps=N_DEV, tc=512, epilogue=ident, outs=(F32,))
        dw_dn = mm_tn(
            a, dy, name=f"mlp{i}_dw_down", p=1, tk=1024, ps=1, tn=D_MODEL, tmc=2048, prologue=relu2, out_dtype=BF16
        )
        dw_up = mm_tn(h, da, name=f"mlp{i}_dw_up", p=N_DEV, tk=1024, ps=2, tn=512, tmc=2048, out_dtype=BF16)
        return scatter([dw_up, dw_dn], dh, f"scatter_mlp{i}"), dh

    dz3, dy3, st3 = residual_ln_bwd(name="res_bwd3", dxo=dxo, this=(x3, y3, gate1[3], lng_full[3], lnb_full[3]))
    rs_mlp1, dh3 = mlp_backward(1, h3, a3, dy3)
    dz2, dy2, st2 = residual_ln_bwd(
        name="res_bwd2", later=(dh3, dz3, scale[3] + rs_mlp1.zero()), this=(x2, y2, gate1[2], lng_full[2], lnb_full[2])
    )
    (d_o,) = mm_nt(dy2, w_bout3, name="b_do", tm=2048, tko=1024, ps=1, tc=D_MODEL, epilogue=ident, outs=(F32,))
    do_b, delta = attn_delta(d_o, o_f, name="attn_delta")
    dh2, dw_pat = None, []
    for g in range(N_PAT):
        do_g, lse_g, delta_g = do_b, lse, delta
        if g > 0:
            do_g, lse_g, delta_g = permute_rows([do_b, lse, delta], dil[g], inverse=False, name=f"perm_do{g}")
        dqkv_g = attn_bwd(qkv_p[g], do_g, lse_g, delta_g, g, name=f"attn_bwd{g}")
        (dh2,) = mm_nt(
            dqkv_g, w_qkv3, name=f"b_dh{g}", tm=2048, tko=512, ps=1, tc=pat_tiles * D_MODEL, outs=(F32,),
            b_tile0=g, g_streams=dil[g],
            epilogue=ident if g == 0 else (lambda acc, prev, d=dil[g]: (_from_streams(acc, d) + prev,)),
            extras=[] if g == 0 else [(dh2, "full")],
        )
        dw_pat.append(
            mm_tn(
                h2, dqkv_g, name=f"b_dw_qkv{g}", p=1, tk=1024, ps=1, tn=D_MODEL, tmc=2048, out_dtype=BF16,
                g_streams=dil[g],
            )[0]
        )
    dw_bout = mm_tn(o_b, dy2, name="b_dw_out", p=1, tk=1024, ps=1, tn=D_MODEL, tmc=2048, out_dtype=BF16)
    dw_qkv = columns_to_shards(dw_pat, name="dw_qkv_shards")
    rs_b = scatter([dw_qkv, dw_bout], dh2, "scatter_b")
    dz1, dy1, st1 = residual_ln_bwd(
        name="res_bwd1", later=(dh2, dz2, scale[2] + rs_b.zero()), this=(x1, y1, gate1[1], lng_full[1], lnb_full[1])
    )
    rs_mlp0, dh1 = mlp_backward(0, h1, a1, dy1)
    dz0, dy0, st0 = residual_ln_bwd(
        name="res_bwd0", later=(dh1, dz1, scale[1] + rs_mlp0.zero()), this=(x0, y0, gate1[0], lng_full[0], lnb_full[0])
    )
    (dp_gate,) = mm_nt(dy0, w_aout3, name="a_dp", tm=2048, tko=1024, ps=1, tc=D_MODEL, epilogue=ident, outs=(F32,))
    dw_aout = mm_tn(p_gate, dy0, name="a_dw_out", p=1, tk=1024, ps=1, tn=D_MODEL, tmc=2048, out_dtype=BF16)
    rs_aout = scatter([dw_aout], dp_gate, "scatter_a_out")
    da0, d_ws, d_bs, gate_rows = gate_bwd(a_pre, dp_gate, vn_g + rs_aout.zero(), vn_b, w_s, bias_full, name="gate_bwd")
    dw_in = mm_tn(h0, da0, name="a_dw_in", p=N_DEV, tk=1024, ps=4, tn=256, tmc=2048, out_dtype=BF16)
    rs_in = scatter([dw_in], d_ws, "scatter_a_in")
    (dh0,) = mm_nt(da0, w_in3, name="a_dh", tm=2048, tko=1024, ps=N_DEV, tc=256, epilogue=ident, outs=(F32,))
    grad_x, stf = residual_ln_bwd(name="res_bwd_in", later=(dh0, dz0, scale[0] + rs_in.zero()), x_out=x0)

    stats_after = [stf, st0, st1, st2]
    stats_own = [st0, st1, st2, st3]
    dm = jnp.stack(
        [
            jnp.concatenate(
                [stats_after[t][ST_DSHIFT], stats_after[t][ST_DSCALE], stats_own[t][ST_DGATE]], axis=0
            )
            for t in range(N_SUB)
        ]
    )
    ncol = 3 * D_MODEL // N_DEV
    dmx = jnp.pad(
        dm.reshape(N_SUB, N_DEV, ncol).transpose(1, 0, 2)[:, :, None, :], ((0, 0), (0, 0), (0, SUBLANES - 1), (0, 0))
    )
    small = [
        gate_rows[0],
        gate_rows[1],
        d_ws.reshape(-1),
        d_bs[:, :A_GROUPS].T.reshape(-1),
        *[stats_own[t][ST_DG] for t in range(N_SUB)],
        *[stats_own[t][ST_DB] for t in range(N_SUB)],
        jnp.pad(loss_local.reshape(1), (0, LANES - 1)),
    ]
    n_small = sum(s.size for s in small)
    part_rows = -(-n_small // (N_DEV * LANES * SUBLANES)) * SUBLANES
    flat = jnp.concatenate(small + [jnp.zeros((N_DEV * part_rows * LANES - n_small,), F32)])
    dmrecv, reduced = small_exchange(dmx, flat.reshape(N_DEV, part_rows, LANES))
    reduced = reduced.reshape(-1)
    sizes = [2 * D_MODEL, D_MODEL, D_MODEL, A_GROUPS * CHUNK * CHUNK, A_GROUPS * CHUNK, N_SUB * D_MODEL, N_SUB * D_MODEL]
    offs = [sum(sizes[:i]) for i in range(len(sizes) + 1)]
    g_b_in, g_vn_g, g_vn_b, g_ws, g_bs, g_lng, g_lnb = [reduced[offs[i] : offs[i + 1]] for i in range(len(sizes))]
    loss = reduced[offs[-1]]

    results = {}

    def update(wname, gparts, w, m, v):
        shape = w.shape
        layers = len(gparts) if isinstance(gparts, list) else 1
        w3 = w.reshape(layers, -1, shape[-1])
        parts = [g.reshape(g.shape[0], *w3.shape[1:]) for g in (gparts if layers > 1 else [gparts])]
        outs = adamw(parts, w3, m.reshape(w3.shape), v.reshape(w3.shape), name=f"adamw_{wname}")
        results[wname] = [o.reshape(shape) for o in outs]

    ada_outs = ada_grad_adamw(cg, dmrecv, ada_w4, m_ada_w.reshape(ada_w4.shape), v_ada_w.reshape(ada_w4.shape), name="ada_grad_adamw")
    results["ada_w"] = [o.reshape(ada_w.shape) for o in ada_outs[:4]]
    update("ada_b", ada_outs[4][:, 0, :][None], ada_b, m_ada_b, v_ada_b)
    ln_cols = D_MODEL // N_DEV
    my_ln = lambda gfull: lax.dynamic_slice_in_dim(gfull.reshape(N_SUB, N_DEV, ln_cols), me, 1, axis=1)
    update("ln_g", my_ln(g_lng).reshape(1, N_SUB, ln_cols), ln_g, m_ln_g, v_ln_g)
    update("ln_b", my_ln(g_lnb).reshape(1, N_SUB, ln_cols), ln_b, m_ln_b, v_ln_b)
    update("a_b_in", g_b_in[None], a_b_in, m_a_b_in, v_a_b_in)
    update("a_vn_g", g_vn_g[None], a_vn_g, m_a_vn_g, v_a_vn_g)
    update("a_vn_b", g_vn_b[None], a_vn_b, m_a_vn_b, v_a_vn_b)
    update("a_w_s", g_ws[None], a_w_s, m_a_w_s, v_a_w_s)
    update("a_b_s", g_bs[None], a_b_s, m_a_b_s, v_a_b_s)
    g_up1, g_dn1 = rs_mlp1.wait([0, 1], grad_x, name="scatter_wait_mlp1")
    g_qkv, g_bout = rs_b.wait([0, 1], grad_x, name="scatter_wait_b")
    update("b_w_qkv", g_qkv, b_w_qkv, m_b_w_qkv, v_b_w_qkv)
    update("b_w_out", g_bout, b_w_out, m_b_w_out, v_b_w_out)
    g_up0, g_dn0 = rs_mlp0.wait([0, 1], grad_x, name="scatter_wait_mlp0")
    update("mlp_w_up", [g_up0, g_up1], mlp_w_up, m_mlp_w_up, v_mlp_w_up)
    update("mlp_w_down", [g_dn0, g_dn1], mlp_w_down, m_mlp_w_down, v_mlp_w_down)
    (g_aout,) = rs_aout.wait([0], grad_x, name="scatter_wait_a_out")
    (g_in,) = rs_in.wait([0], grad_x, name="scatter_wait_a_in")
    update("a_w_in", g_in, a_w_in, m_a_w_in, v_a_w_in)
    update("a_w_out", g_aout, a_w_out, m_a_w_out, v_a_w_out)

    order = ["ada_w", "ada_b", "ln_g", "ln_b", "a_w_in", "a_b_in", "a_vn_g", "a_vn_b", "a_w_s", "a_b_s", "a_w_out", "b_w_qkv", "b_w_out", "mlp_w_up", "mlp_w_down"]
    return (loss, grad_x[None], *[results[n][0] for n in order], *[results[n][1] for n in order],
            *[results[n][2] for n in order], *[results[n][3] for n in order])
```

```python
import math

import jax
import jax.numpy as jnp
from jax import lax
from jax.experimental import pallas as pl
from jax.experimental.pallas import tpu as pltpu

F32 = jnp.float32
BF16 = jnp.bfloat16
MESH = pl.DeviceIdType.MESH
ANY = pl.BlockSpec(memory_space=pl.ANY)
VMEM = pl.BlockSpec(memory_space=pltpu.VMEM)

N_DEV = 8
D_MODEL = 1024
SEQ = 4096
DEPTH = 2
CHUNK = 128
A_GROUPS = 16
A_GROUP_DIM = D_MODEL // A_GROUPS
B_HEADS = 16
B_HEAD_DIM = 64
B_PATTERNS = ((128, 1), (512, 4), (2048, 16))
N_PAT = len(B_PATTERNS)
SPAN = 128
D_FF = 4 * D_MODEL
D_QKV = N_PAT * 3 * D_MODEL
ALPHA = (2 * DEPTH) ** 0.25
LN_EPS = 1e-5
NEG = -1e30
ADAM_LR = 0.001
ADAM_B1 = 0.9
ADAM_B2 = 0.999
ADAM_EPS = 1e-08
ADAM_WD = 0.01
ADAM_STEP = 10
GELU_C = math.sqrt(2.0 / math.pi)
GELU_A = 0.044715

VMEM_LIMIT_BYTES = 56 * 1024 * 1024
LANES = 128
SUBLANES = 8
ROW_TILE = 512
N_SUB = 2 * DEPTH


def _params(sem):
    return pltpu.CompilerParams(dimension_semantics=sem, vmem_limit_bytes=VMEM_LIMIT_BYTES)


def _lane(shape):
    return lax.broadcasted_iota(jnp.int32, shape, len(shape) - 1)


def _split_bf16(x):
    hi = x.astype(BF16)
    lo = (x - hi.astype(F32)).astype(BF16)
    return hi, lo


def _group_expand_matrix(groups_padded, width):
    per = width // A_GROUPS
    r = lax.broadcasted_iota(jnp.int32, (groups_padded, width), 0)
    c = lax.broadcasted_iota(jnp.int32, (groups_padded, width), 1)
    return (c // per == r).astype(BF16)


def _group_reduce_matrix(width, groups_padded):
    per = width // A_GROUPS
    r = lax.broadcasted_iota(jnp.int32, (width, groups_padded), 0)
    c = lax.broadcasted_iota(jnp.int32, (width, groups_padded), 1)
    return (r // per == c).astype(BF16)


def _expand_groups(w):
    e = _group_expand_matrix(LANES, D_MODEL)
    hi, lo = _split_bf16(w)
    return jnp.dot(hi, e, preferred_element_type=F32) + jnp.dot(lo, e, preferred_element_type=F32)


def _reduce_groups(x):
    e = _group_reduce_matrix(D_MODEL, LANES)
    hi, lo = _split_bf16(x)
    return jnp.dot(hi, e, preferred_element_type=F32) + jnp.dot(lo, e, preferred_element_type=F32)


def _to_streams(x, d):
    rows, w = x.shape
    return jnp.swapaxes(x.reshape(rows // d, d, w), 0, 1).reshape(rows, w)


def _from_streams(x, d):
    rows, w = x.shape
    return jnp.swapaxes(x.reshape(d, rows // d, w), 0, 1).reshape(rows, w)


def _column_tiles(p, n, ps, tn):
    assert (ps == 1 or tn == n) and p % ps == 0 and n % tn == 0
    q = n // tn
    return (p // ps) * q, q


def _extra_specs(extras, tm, width):
    specs = []
    for _, kind in extras:
        if kind == "row":
            specs.append(pl.BlockSpec((1, width), lambda i, j, c: (0, j)))
        else:
            specs.append(pl.BlockSpec((tm, width), lambda i, j, c: (i, j)))
    return specs


def mm_nn(a, b3, *, name, tm, ps, tn, tk, epilogue, extras=(), outs, prologue=None, b_tile0=0, b_tiles=None, out_streams=1):
    m, k = a.shape
    p, _, n = b3.shape
    nj, q = _column_tiles(p, n, ps, tn)
    nj = nj if b_tiles is None else b_tiles
    nk = k // tk
    width = ps * tn
    d = out_streams
    assert d == 1 or not extras

    def body(a_ref, b_ref, *rest):
        ex = rest[: len(extras)]
        out_refs = rest[len(extras) : len(extras) + len(outs)]
        kk = pl.program_id(2)
        av = a_ref[...] if prologue is None else prologue(a_ref[...])
        if d > 1:
            av = _to_streams(av, d)

        def finish(cs, acc):
            res = epilogue(acc, *[e[:, cs] for e in ex])
            for o_ref, r in zip(out_refs, res, strict=True):
                if d > 1:
                    o_ref[:, :, cs] = r.astype(o_ref.dtype).reshape(d, tm // d, tn)
                else:
                    o_ref[:, cs] = r.astype(o_ref.dtype)

        for s in range(ps):
            cs = slice(s * tn, (s + 1) * tn)
            part = jnp.dot(av, b_ref[s], preferred_element_type=F32)
            if nk == 1:
                finish(cs, part)
                continue
            acc_ref = rest[-1]

            @pl.when(kk == 0)
            def _(part=part, cs=cs):
                acc_ref[:, cs] = part

            @pl.when(kk > 0)
            def _(part=part, cs=cs):
                acc_ref[:, cs] += part

        if nk > 1:

            @pl.when(kk == nk - 1)
            def _():
                for s in range(ps):
                    cs = slice(s * tn, (s + 1) * tn)
                    finish(cs, rest[-1][:, cs])

    if d > 1:
        out_spec = pl.BlockSpec((d, tm // d, width), lambda i, j, kk: (0, i, j))
        out_shape = (d, m // d, nj * width)
    else:
        out_spec = pl.BlockSpec((tm, width), lambda i, j, kk: (i, j))
        out_shape = (m, nj * width)
    res = pl.pallas_call(
        body,
        name=name,
        grid=(m // tm, nj, nk),
        in_specs=[
            pl.BlockSpec((tm, tk), lambda i, j, kk: (i, kk)),
            pl.BlockSpec((ps, tk, tn), lambda i, j, kk: ((j + b_tile0) // q, kk, (j + b_tile0) % q)),
            *_extra_specs(extras, tm, width),
        ],
        out_specs=[out_spec for _ in outs],
        out_shape=[jax.ShapeDtypeStruct(out_shape, dt) for dt in outs],
        scratch_shapes=[pltpu.VMEM((tm, width), F32)] if nk > 1 else [],
        compiler_params=_params(("parallel", "parallel", "arbitrary")),
    )(a, b3, *[arr for arr, _ in extras])
    return [r.reshape(m, nj * width) for r in res]


def mm_nt(g, b3, *, name, tm, tko, ps, tc, epilogue, extras=(), outs, b_tile0=0, g_streams=1):
    m, width = g.shape
    p, k, n = b3.shape
    _, q = _column_tiles(p, n, ps, tc)
    nc = width // (ps * tc)
    ds = g_streams

    def body(g_ref, b_ref, *rest):
        ex = rest[: len(extras)]
        out_refs = rest[len(extras) : len(extras) + len(outs)]
        c = pl.program_id(2)
        gv = g_ref[...].reshape(tm, ps * tc) if ds > 1 else g_ref[...]
        part = None
        for s in range(ps):
            d = lax.dot_general(gv[:, s * tc : (s + 1) * tc], b_ref[s], _NT, preferred_element_type=F32)
            part = d if part is None else part + d

        def finish(acc):
            res = epilogue(acc, *[e[...] for e in ex])
            for o_ref, r in zip(out_refs, res, strict=True):
                o_ref[...] = r.astype(o_ref.dtype)

        if nc == 1:
            finish(part)
            return
        acc_ref = rest[-1]

        @pl.when(c == 0)
        def _():
            acc_ref[...] = part

        @pl.when(c > 0)
        def _():
            acc_ref[...] += part

        @pl.when(c == nc - 1)
        def _():
            finish(acc_ref[...])

    return pl.pallas_call(
        body,
        name=name,
        grid=(m // tm, k // tko, nc),
        in_specs=[
            pl.BlockSpec((ds, tm // ds, ps * tc), lambda i, j, c: (0, i, c))
            if ds > 1
            else pl.BlockSpec((tm, ps * tc), lambda i, j, c: (i, c)),
            pl.BlockSpec((ps, tko, tc), lambda i, j, c: ((c + b_tile0) // q, j, (c + b_tile0) % q)),
            *_extra_specs(extras, tm, tko),
        ],
        out_specs=[pl.BlockSpec((tm, tko), lambda i, j, c: (i, j)) for _ in outs],
        out_shape=[jax.ShapeDtypeStruct((m, k), dt) for dt in outs],
        scratch_shapes=[pltpu.VMEM((tm, tko), F32)] if nc > 1 else [],
        compiler_params=_params(("parallel", "parallel", "arbitrary")),
    )(g.reshape(ds, m // ds, width) if ds > 1 else g, b3, *[arr for arr, _ in extras])


def mm_tn(a, g, *, name, p, tk, ps, tn, tmc, out_dtype, prologue=None, g_streams=1):
    m, k = a.shape
    width = g.shape[1]
    n = width // p
    nj, q = _column_tiles(p, n, ps, tn)
    nc = m // tmc
    ds = g_streams

    def body(a_ref, g_ref, o_ref, acc_ref):
        c = pl.program_id(2)
        av = a_ref[...] if prologue is None else prologue(a_ref[...])
        gv = g_ref[...]
        if ds > 1:
            av, gv = _to_streams(av, ds), gv.reshape(tmc, ps * tn)
        part = lax.dot_general(av, gv, (((0,), (0,)), ((), ())), preferred_element_type=F32)

        @pl.when(c == 0)
        def _():
            acc_ref[...] = part

        @pl.when(c > 0)
        def _():
            acc_ref[...] += part

        @pl.when(c == nc - 1)
        def _():
            for s in range(ps):
                o_ref[s] = acc_ref[:, s * tn : (s + 1) * tn].astype(o_ref.dtype)

    return pl.pallas_call(
        body,
        name=name,
        grid=(k // tk, nj, nc),
        in_specs=[
            pl.BlockSpec((tmc, tk), lambda i, j, c: (c, i)),
            pl.BlockSpec((ds, tmc // ds, ps * tn), lambda i, j, c: (0, c, j))
            if ds > 1
            else pl.BlockSpec((tmc, ps * tn), lambda i, j, c: (c, j)),
        ],
        out_specs=pl.BlockSpec((ps, tk, tn), lambda i, j, c: (j // q, i, j % q)),
        out_shape=jax.ShapeDtypeStruct((p, k, n), out_dtype),
        scratch_shapes=[pltpu.VMEM((tk, ps * tn), F32)],
        compiler_params=_params(("parallel", "parallel", "arbitrary")),
    )(a, g.reshape(ds, m // ds, width) if ds > 1 else g)


def _rows(cols):
    return pl.BlockSpec((ROW_TILE, cols), lambda i: (i, 0))


def _vec(cols, rows=1):
    return pl.BlockSpec((rows, cols), lambda i: (0, 0))


def _layer_norm_hat(z):
    mu = jnp.mean(z, axis=-1, keepdims=True)
    zc = z - mu
    var = jnp.mean(zc * zc, axis=-1, keepdims=True)
    rstd = lax.rsqrt(var + LN_EPS)
    return zc * rstd, rstd


def modulate(x, scale, shift, *, name):
    s, d = x.shape

    def body(x_ref, sc_ref, sh_ref, h_ref):
        h_ref[...] = (x_ref[...] * (1.0 + sc_ref[...]) + sh_ref[...]).astype(BF16)

    return pl.pallas_call(
        body,
        name=name,
        grid=(s // ROW_TILE,),
        in_specs=[_rows(d), _vec(d), _vec(d)],
        out_specs=_rows(d),
        out_shape=jax.ShapeDtypeStruct((s, d), BF16),
        compiler_params=_params(("parallel",)),
    )(x, scale, shift)


def residual_ln_epilogue(acc, x, gate1, g, b, nscale, nshift):
    y = acc.astype(BF16)
    xhat, _ = _layer_norm_hat(ALPHA * x + gate1 * y.astype(F32))
    xn = xhat * g + b
    return y, xn, xn * (1.0 + nscale) + nshift


def residual_ln_loss(x, y, gate1, g, b, target, *, name):
    s, d = x.shape

    def body(x_ref, y_ref, gt_ref, g_ref, b_ref, t_ref, dx_ref, loss_ref):
        z = ALPHA * x_ref[...] + gt_ref[...] * y_ref[...].astype(F32)
        xhat, _ = _layer_norm_hat(z)
        err = xhat * g_ref[...] + b_ref[...] - t_ref[...]
        dx_ref[...] = err * (1.0 / d)
        part = jnp.sum(jnp.sum(err * err, axis=1, keepdims=True), axis=0, keepdims=True) * (0.5 / d)

        @pl.when(pl.program_id(0) == 0)
        def _():
            loss_ref[...] = part

        @pl.when(pl.program_id(0) > 0)
        def _():
            loss_ref[...] += part

    return pl.pallas_call(
        body,
        name=name,
        grid=(s // ROW_TILE,),
        in_specs=[_rows(d), _rows(d), _vec(d), _vec(d), _vec(d), _rows(d)],
        out_specs=[_rows(d), pl.BlockSpec((1, 1), lambda i: (0, 0))],
        out_shape=[jax.ShapeDtypeStruct((s, d), F32), jax.ShapeDtypeStruct((1, 1), F32)],
        compiler_params=_params(("arbitrary",)),
    )(x, y, gate1, g, b, target)


ST_DSCALE, ST_DSHIFT, ST_DG, ST_DB, ST_DGATE = 0, 1, 2, 3, 4


def residual_ln_bwd(*, name, later=None, dxo=None, x_out=None, this=None):
    lead = later[0] if later is not None else dxo
    s, d = lead.shape
    has_later, has_ln = later is not None, this is not None

    def body(*refs):
        refs = list(refs)
        if has_later:
            dh_ref, dzl_ref, scl_ref = refs[:3]
            refs = refs[3:]
        else:
            dxo_ref = refs.pop(0)
        if has_ln:
            x_ref, y_ref, gt_ref, g_ref, b_ref = refs[:5]
            refs = refs[5:]
            dz_ref, dy_ref, st_ref = refs
        else:
            xo_ref, dx_ref, st_ref = refs

        @pl.when(pl.program_id(0) == 0)
        def _():
            st_ref[...] = jnp.zeros_like(st_ref)

        def acc(row, val):
            st_ref[row : row + 1, :] += jnp.sum(val, axis=0, keepdims=True)

        if has_ln:
            y = y_ref[...].astype(F32)
            gate1 = gt_ref[...]
            xhat, rstd = _layer_norm_hat(ALPHA * x_ref[...] + gate1 * y)
            x_out_v = xhat * g_ref[...] + b_ref[...]
        else:
            x_out_v = xo_ref[...]
        if has_later:
            dh = dh_ref[...]
            g_out = ALPHA * dzl_ref[...] + dh * (1.0 + scl_ref[...])
            acc(ST_DSCALE, dh * x_out_v)
            acc(ST_DSHIFT, dh)
        else:
            g_out = dxo_ref[...]
        if not has_ln:
            dx_ref[...] = g_out
            return
        acc(ST_DG, g_out * xhat)
        acc(ST_DB, g_out)
        dxh = g_out * g_ref[...]
        m1 = jnp.mean(dxh, axis=-1, keepdims=True)
        m2 = jnp.mean(dxh * xhat, axis=-1, keepdims=True)
        dz = rstd * (dxh - m1 - xhat * m2)
        acc(ST_DGATE, dz * y)
        dz_ref[...] = dz
        dy_ref[...] = (dz * gate1).astype(BF16)

    ins, specs = [], []
    if has_later:
        ins += list(later)
        specs += [_rows(d), _rows(d), _vec(d)]
    else:
        ins += [dxo]
        specs += [_rows(d)]
    if not has_ln:
        ins += [x_out]
        specs += [_rows(d)]
    if has_ln:
        ins += list(this)
        specs += [_rows(d), _rows(d), _vec(d), _vec(d), _vec(d)]
        out_specs = [_rows(d), _rows(d), _vec(d, SUBLANES)]
        out_shape = [
            jax.ShapeDtypeStruct((s, d), F32),
            jax.ShapeDtypeStruct((s, d), BF16),
            jax.ShapeDtypeStruct((SUBLANES, d), F32),
        ]
    else:
        out_specs = [_rows(d), _vec(d, SUBLANES)]
        out_shape = [jax.ShapeDtypeStruct((s, d), F32), jax.ShapeDtypeStruct((SUBLANES, d), F32)]
    return pl.pallas_call(
        body,
        name=name,
        grid=(s // ROW_TILE,),
        in_specs=specs,
        out_specs=out_specs,
        out_shape=out_shape,
        compiler_params=_params(("arbitrary",)),
    )(*ins)


GATE_CHUNKS = 4


def _gelu(x, with_grad=False):
    x2 = x * x
    t = jnp.tanh(GELU_C * (x + GELU_A * x2 * x))
    half = 0.5 * (1.0 + t)
    y = x * half
    if not with_grad:
        return y
    return y, half + 0.5 * x * (1.0 - t * t) * (GELU_C * (1.0 + 3.0 * GELU_A * x2))


def _causal_weights(w_ref, transpose):
    t = lax.broadcasted_iota(jnp.int32, (CHUNK, CHUNK), 0)
    s = lax.broadcasted_iota(jnp.int32, (CHUNK, CHUNK), 1)
    out = []
    for g in range(A_GROUPS):
        w = jnp.where(t >= s, w_ref[g], 0.0)
        out.append((w.T if transpose else w).astype(BF16))
    return out


def _spatial(ws, vn, lo_mask):
    rows = vn.shape[0]
    out_rows = []
    for r in range(rows // CHUNK):
        cols = []
        for j in range(A_GROUPS // 2):
            blk = vn[r * CHUNK : (r + 1) * CHUNK, j * LANES : (j + 1) * LANES]
            za = jnp.dot(ws[2 * j], blk, preferred_element_type=F32)
            zb = jnp.dot(ws[2 * j + 1], blk, preferred_element_type=F32)
            cols.append(jnp.where(lo_mask, za, zb))
        out_rows.append(jnp.concatenate(cols, axis=1))
    return jnp.concatenate(out_rows, axis=0)


def _gate_forward(a, vg, vb, ws, bias, lo_mask, with_grad=False):
    u = _gelu(a[:, :D_MODEL], with_grad)
    v = _gelu(a[:, D_MODEL:], with_grad)
    gu, gv = None, None
    if with_grad:
        (u, gu), (v, gv) = u, v
    vhat, rstd = _layer_norm_hat(v)
    vn = (vhat * vg + vb).astype(BF16)
    z = _spatial(ws, vn, lo_mask) + jnp.concatenate([bias] * (a.shape[0] // CHUNK), axis=0)
    return u, vhat, rstd, vn, z, gu, gv


def gate_fwd(a_pre, vn_g, vn_b, w_s, bias_full, *, name):
    s = a_pre.shape[0]
    tr = GATE_CHUNKS * CHUNK

    def body(a_ref, vg_ref, vb_ref, w_ref, bias_ref, p_ref):
        lo_mask = _lane((CHUNK, LANES)) < A_GROUP_DIM
        ws = _causal_weights(w_ref, transpose=False)
        u, _, _, _, z, _, _ = _gate_forward(a_ref[...].astype(F32), vg_ref[...], vb_ref[...], ws, bias_ref[...], lo_mask)
        p_ref[...] = (u * z).astype(BF16)

    return pl.pallas_call(
        body,
        name=name,
        grid=(s // tr,),
        in_specs=[
            pl.BlockSpec((tr, 2 * D_MODEL), lambda i: (i, 0)),
            _vec(D_MODEL),
            _vec(D_MODEL),
            pl.BlockSpec((A_GROUPS, CHUNK, CHUNK), lambda i: (0, 0, 0)),
            _vec(D_MODEL, CHUNK),
        ],
        out_specs=pl.BlockSpec((tr, D_MODEL), lambda i: (i, 0)),
        out_shape=jax.ShapeDtypeStruct((s, D_MODEL), BF16),
        compiler_params=_params(("parallel",)),
    )(a_pre, vn_g, vn_b, w_s, bias_full)


def gate_bwd(a_pre, dp, vn_g, vn_b, w_s, bias_full, *, name):
    s = a_pre.shape[0]
    tr = GATE_CHUNKS * CHUNK
    nsteps = s // tr

    def body(a_ref, dp_ref, vg_ref, vb_ref, w_ref, bias_ref, da_ref, dw_ref, dbs_ref, rows_ref, dbias_acc):
        step = pl.program_id(0)
        lo_mask = _lane((CHUNK, LANES)) < A_GROUP_DIM

        @pl.when(step == 0)
        def _():
            dw_ref[...] = jnp.zeros_like(dw_ref)
            rows_ref[...] = jnp.zeros_like(rows_ref)
            dbias_acc[...] = jnp.zeros_like(dbias_acc)

        a = a_ref[...].astype(F32)
        vg = vg_ref[...]
        ws = _causal_weights(w_ref, transpose=False)
        wts = _causal_weights(w_ref, transpose=True)
        u, vhat, rstd, vn, z, gelu_du, gelu_dv = _gate_forward(a, vg, vb_ref[...], ws, bias_ref[...], lo_mask, True)
        dp = dp_ref[...]
        du = dp * z
        dzz = dp * u
        dzz_b = dzz.astype(BF16)
        dvn = _spatial(wts, dzz_b, lo_mask)
        dbias = None
        for r in range(GATE_CHUNKS):
            rs = slice(r * CHUNK, (r + 1) * CHUNK)
            dbias = dzz[rs] if dbias is None else dbias + dzz[rs]
            for j in range(A_GROUPS // 2):
                cs = slice(j * LANES, (j + 1) * LANES)
                dblk = dzz[rs, cs]
                vblk = vn[rs, cs]
                for half in range(2):
                    keep = lo_mask if half == 0 else jnp.logical_not(lo_mask)
                    dm = jnp.where(keep, dblk, 0.0).astype(BF16)
                    dw_ref[2 * j + half] += lax.dot_general(
                        dm, vblk, (((1,), (1,)), ((), ())), preferred_element_type=F32
                    )
        dbias_acc[...] += dbias
        rows_ref[1:2, :D_MODEL] += jnp.sum(dvn * vhat, axis=0, keepdims=True)
        rows_ref[1:2, D_MODEL:] += jnp.sum(dvn, axis=0, keepdims=True)
        dvh = dvn * vg
        m1 = jnp.mean(dvh, axis=-1, keepdims=True)
        m2 = jnp.mean(dvh * vhat, axis=-1, keepdims=True)
        dv = rstd * (dvh - m1 - vhat * m2)
        da_u = du * gelu_du
        da_v = dv * gelu_dv
        da_ref[:, :D_MODEL] = da_u.astype(BF16)
        da_ref[:, D_MODEL:] = da_v.astype(BF16)
        rows_ref[0:1, :D_MODEL] += jnp.sum(da_u, axis=0, keepdims=True)
        rows_ref[0:1, D_MODEL:] += jnp.sum(da_v, axis=0, keepdims=True)

        @pl.when(step == nsteps - 1)
        def _():
            t = lax.broadcasted_iota(jnp.int32, (CHUNK, CHUNK), 0)
            sx = lax.broadcasted_iota(jnp.int32, (CHUNK, CHUNK), 1)
            for g in range(A_GROUPS):
                dw_ref[g] = jnp.where(t >= sx, dw_ref[g], 0.0)
            dbs_ref[...] = _reduce_groups(dbias_acc[...])

    return pl.pallas_call(
        body,
        name=name,
        grid=(nsteps,),
        in_specs=[
            pl.BlockSpec((tr, 2 * D_MODEL), lambda i: (i, 0)),
            pl.BlockSpec((tr, D_MODEL), lambda i: (i, 0)),
            _vec(D_MODEL),
            _vec(D_MODEL),
            pl.BlockSpec((A_GROUPS, CHUNK, CHUNK), lambda i: (0, 0, 0)),
            _vec(D_MODEL, CHUNK),
        ],
        out_specs=[
            pl.BlockSpec((tr, 2 * D_MODEL), lambda i: (i, 0)),
            pl.BlockSpec((A_GROUPS, CHUNK, CHUNK), lambda i: (0, 0, 0)),
            _vec(LANES, CHUNK),
            _vec(2 * D_MODEL, SUBLANES),
        ],
        out_shape=[
            jax.ShapeDtypeStruct((s, 2 * D_MODEL), BF16),
            jax.ShapeDtypeStruct((A_GROUPS, CHUNK, CHUNK), F32),
            jax.ShapeDtypeStruct((CHUNK, LANES), F32),
            jax.ShapeDtypeStruct((SUBLANES, 2 * D_MODEL), F32),
        ],
        scratch_shapes=[pltpu.VMEM((CHUNK, D_MODEL), F32)],
        compiler_params=_params(("arbitrary",)),
    )(a_pre, dp, vn_g, vn_b, w_s, bias_full)


def alibi_tables(dilation):
    qi = jnp.arange(SPAN)[:, None]
    ki = jnp.arange(2 * SPAN)[None, :]
    diff = SPAN + qi - ki
    valid = (diff >= 0) & (diff <= SPAN)
    heads = jnp.arange(1, B_HEADS + 1, dtype=F32)
    slopes = jnp.exp2(-8.0 * heads / B_HEADS)
    bias = -slopes[:, None, None] * (dilation * diff).astype(F32)
    bias = jnp.where(valid[None], bias, NEG).reshape(B_HEADS // 2, 2 * SPAN, 2 * SPAN)
    return bias, bias.transpose(0, 2, 1)


def _pair_rows(x, halves):
    return jnp.concatenate([x * halves[0], x * halves[1]], axis=0)


def _pair_column(v, lane, j):
    pick = lambda h: jnp.sum(jnp.where(lane == h, v, 0.0), axis=1, keepdims=True)
    return jnp.concatenate([pick(2 * j), pick(2 * j + 1)], axis=0)


_NT = (((1,), (1,)), ((), ()))


def permute_rows(xs, dilation, *, inverse, name):
    s = xs[0].shape[0]
    tile = SPAN * dilation
    nat = [pl.BlockSpec((tile, x.shape[1]), lambda i: (i, 0)) for x in xs]
    streams = [pl.BlockSpec((dilation, SPAN, x.shape[1]), lambda i: (0, i, 0)) for x in xs]
    stream_shape = [jax.ShapeDtypeStruct((dilation, s // dilation, x.shape[1]), x.dtype) for x in xs]

    def body(*refs):
        for x_ref, o_ref in zip(refs[: len(xs)], refs[len(xs) :], strict=True):
            w = x_ref.shape[-1]
            if inverse:
                o_ref[...] = jnp.swapaxes(x_ref[...], 0, 1).reshape(tile, w)
            else:
                o_ref[...] = jnp.swapaxes(x_ref[...].reshape(SPAN, dilation, w), 0, 1)

    outs = pl.pallas_call(
        body,
        name=name,
        grid=(s // tile,),
        in_specs=streams if inverse else nat,
        out_specs=nat if inverse else streams,
        out_shape=[jax.ShapeDtypeStruct(x.shape, x.dtype) for x in xs] if inverse else stream_shape,
        compiler_params=_params(("parallel",)),
    )(*[x.reshape(dilation, s // dilation, x.shape[1]) if inverse else x for x in xs])
    return [o.reshape(x.shape) for o, x in zip(outs, xs, strict=True)]


def _qkv_specs(block_of):
    def spec(which, prev):
        def index(*grid):
            blk = block_of(*grid)
            return (jnp.maximum(blk - 1, 0) if prev else blk, which)

        return pl.BlockSpec((SPAN, D_MODEL), index)

    return [spec(0, False), spec(1, True), spec(1, False), spec(2, True), spec(2, False)]


def attn_fwd(qkv_p, pat, *, name):
    _, dilation = B_PATTERNS[pat]
    nb = SEQ // dilation // SPAN
    bias, _ = alibi_tables(dilation)

    def body(q_ref, kp_ref, kc_ref, vp_ref, vc_ref, bias_ref, o_ref, lse_ref):
        n = pl.program_id(1)
        first_prev = jnp.logical_and(n == 0, _lane((2 * SPAN, 2 * SPAN)) < SPAN)
        lane = _lane((SPAN, LANES))
        lo_mask = lane < B_HEAD_DIM
        q = q_ref[...] * jnp.asarray(B_HEAD_DIM**-0.5, BF16)
        kk = jnp.concatenate([kp_ref[...], kc_ref[...]], axis=0)
        vv = jnp.concatenate([vp_ref[...], vc_ref[...]], axis=0)
        halves = (lo_mask.astype(BF16), jnp.logical_not(lo_mask).astype(BF16))
        stats = jnp.zeros((SPAN, LANES), F32)
        for j in range(B_HEADS // 2):
            cs = slice(j * LANES, (j + 1) * LANES)
            sc = lax.dot_general(_pair_rows(q[:, cs], halves), kk[:, cs], _NT, preferred_element_type=F32)
            sc = jnp.where(first_prev, NEG, sc + bias_ref[j])
            m = jnp.max(sc, axis=1, keepdims=True)
            p = jnp.exp(sc - m)
            l = jnp.sum(p, axis=1, keepdims=True)
            acc = jnp.dot(p.astype(BF16), vv[:, cs], preferred_element_type=F32) * (1.0 / l)
            lse_pair = m + jnp.log(l)
            o_ref[:, cs] = jnp.where(lo_mask, acc[:SPAN], acc[SPAN:]).astype(BF16)
            stats = jnp.where(lane == 2 * j, lse_pair[:SPAN], stats)
            stats = jnp.where(lane == 2 * j + 1, lse_pair[SPAN:], stats)
        lse_ref[...] = stats

    return pl.pallas_call(
        body,
        name=name,
        grid=(dilation, nb),
        in_specs=[
            *_qkv_specs(lambda r, n: r * nb + n),
            pl.BlockSpec((B_HEADS // 2, 2 * SPAN, 2 * SPAN), lambda r, n: (0, 0, 0)),
        ],
        out_specs=[
            pl.BlockSpec((SPAN, D_MODEL), lambda r, n: (r * nb + n, 0)),
            pl.BlockSpec((SPAN, LANES), lambda r, n: (r * nb + n, 0)),
        ],
        out_shape=[jax.ShapeDtypeStruct((SEQ, D_MODEL), BF16), jax.ShapeDtypeStruct((SEQ, LANES), F32)],
        compiler_params=_params(("parallel", "arbitrary")),
    )(qkv_p, qkv_p, qkv_p, qkv_p, qkv_p, bias)


def attn_combine(outs, lses, *, name):
    def body(o0, o1, o2, l0, l1, l2, ob_ref, of_ref, lse_ref):
        ls = [l0[...], l1[...], l2[...]]
        m = jnp.maximum(jnp.maximum(ls[0], ls[1]), ls[2])
        tot = jnp.log(jnp.exp(ls[0] - m) + jnp.exp(ls[1] - m) + jnp.exp(ls[2] - m)) + m
        o = None
        for o_ref, l in zip((o0, o1, o2), ls, strict=True):
            term = _expand_groups(jnp.exp(l - tot)) * o_ref[...]
            o = term if o is None else o + term
        ob_ref[...] = o.astype(BF16)
        of_ref[...] = o
        lse_ref[...] = tot

    return pl.pallas_call(
        body,
        name=name,
        grid=(SEQ // ROW_TILE,),
        in_specs=[_rows(D_MODEL)] * 3 + [_rows(LANES)] * 3,
        out_specs=[_rows(D_MODEL), _rows(D_MODEL), _rows(LANES)],
        out_shape=[
            jax.ShapeDtypeStruct((SEQ, D_MODEL), BF16),
            jax.ShapeDtypeStruct((SEQ, D_MODEL), F32),
            jax.ShapeDtypeStruct((SEQ, LANES), F32),
        ],
        compiler_params=_params(("parallel",)),
    )(*outs, *lses)


def attn_delta(do, o, *, name):
    def body(do_ref, o_ref, dob_ref, dl_ref):
        do_v = do_ref[...]
        dob_ref[...] = do_v.astype(BF16)
        dl_ref[...] = _reduce_groups(do_v * o_ref[...])

    return pl.pallas_call(
        body,
        name=name,
        grid=(SEQ // ROW_TILE,),
        in_specs=[_rows(D_MODEL), _rows(D_MODEL)],
        out_specs=[_rows(D_MODEL), _rows(LANES)],
        out_shape=[jax.ShapeDtypeStruct((SEQ, D_MODEL), BF16), jax.ShapeDtypeStruct((SEQ, LANES), F32)],
        compiler_params=_params(("parallel",)),
    )(do, o)


def attn_bwd(qkv_p, do_p, lse_p, delta_p, pat, *, name):
    _, dilation = B_PATTERNS[pat]
    nb = SEQ // dilation // SPAN
    n_blocks = SEQ // SPAN
    bias, bias_t = alibi_tables(dilation)
    last = n_blocks - 1
    q_cols, k_cols, v_cols = (slice(i * D_MODEL, (i + 1) * D_MODEL) for i in range(3))

    def body(q_ref, kp_ref, kc_ref, vp_ref, vc_ref, do_ref, lse_ref, dl_ref, bias_ref, biast_ref, out_ref, cq_ref, ck_ref, cv_ref):
        g = pl.program_id(0)

        @pl.when(g == n_blocks)
        def _():
            out_ref[:, q_cols] = cq_ref[...].astype(BF16)
            out_ref[:, k_cols] = ck_ref[...].astype(BF16)
            out_ref[:, v_cols] = cv_ref[...].astype(BF16)

        @pl.when(g == 0)
        def _():
            cq_ref[...] = jnp.zeros_like(cq_ref)
            ck_ref[...] = jnp.zeros_like(ck_ref)
            cv_ref[...] = jnp.zeros_like(cv_ref)

        @pl.when(g < n_blocks)
        def _():
            lane = _lane((SPAN, LANES))
            lo_mask = lane < B_HEAD_DIM
            pair = (2 * SPAN, 2 * SPAN)
            first = lax.rem(g, nb) == 0
            prev_key_cols = jnp.logical_and(first, _lane(pair) < SPAN)
            prev_key_rows = jnp.logical_and(first, lax.broadcasted_iota(jnp.int32, pair, 0) < SPAN)
            q = q_ref[...] * jnp.asarray(B_HEAD_DIM**-0.5, BF16)
            kk = jnp.concatenate([kp_ref[...], kc_ref[...]], axis=0)
            vv = jnp.concatenate([vp_ref[...], vc_ref[...]], axis=0)
            do_v = do_ref[...]
            lse_v = lse_ref[...]
            dl_v = dl_ref[...]
            lse_t = lse_v.T
            dl_t = dl_v.T
            halves = (lo_mask.astype(BF16), jnp.logical_not(lo_mask).astype(BF16))
            for j in range(B_HEADS // 2):
                cs = slice(j * LANES, (j + 1) * LANES)
                kp, vp = kk[:, cs], vv[:, cs]
                q2 = _pair_rows(q[:, cs], halves)
                do2 = _pair_rows(do_v[:, cs], halves)
                lse_c, dl_c = _pair_column(lse_v, lane, j), _pair_column(dl_v, lane, j)
                lse_r = jnp.concatenate([lse_t[2 * j : 2 * j + 1], lse_t[2 * j + 1 : 2 * j + 2]], axis=1)
                dl_r = jnp.concatenate([dl_t[2 * j : 2 * j + 1], dl_t[2 * j + 1 : 2 * j + 2]], axis=1)
                sc = lax.dot_general(q2, kp, _NT, preferred_element_type=F32)
                p = jnp.exp(jnp.where(prev_key_cols, NEG, sc + bias_ref[j]) - lse_c)
                dp = lax.dot_general(do2, vp, _NT, preferred_element_type=F32)
                ds = (p * (dp - dl_c)).astype(BF16)
                dq2 = jnp.dot(ds, kp, preferred_element_type=F32)
                sc_t = lax.dot_general(kp, q2, _NT, preferred_element_type=F32)
                p_t = jnp.exp(jnp.where(prev_key_rows, NEG, sc_t + biast_ref[j]) - lse_r)
                dp_t = lax.dot_general(vp, do2, _NT, preferred_element_type=F32)
                ds_t = (p_t * (dp_t - dl_r)).astype(BF16)
                dk_pair = jnp.dot(ds_t, q2, preferred_element_type=F32)
                dv_pair = jnp.dot(p_t.astype(BF16), do2, preferred_element_type=F32)
                oq = slice(j * LANES, (j + 1) * LANES)
                ok = slice(D_MODEL + j * LANES, D_MODEL + (j + 1) * LANES)
                ov = slice(2 * D_MODEL + j * LANES, 2 * D_MODEL + (j + 1) * LANES)
                out_ref[:, oq] = cq_ref[:, cs].astype(BF16)
                out_ref[:, ok] = (ck_ref[:, cs] + dk_pair[:SPAN]).astype(BF16)
                out_ref[:, ov] = (cv_ref[:, cs] + dv_pair[:SPAN]).astype(BF16)
                cq_ref[:, cs] = jnp.where(lo_mask, dq2[:SPAN], dq2[SPAN:]) * (B_HEAD_DIM**-0.5)
                ck_ref[:, cs] = dk_pair[SPAN:]
                cv_ref[:, cs] = dv_pair[SPAN:]

    def block_of(g):
        return jnp.minimum(g, last)

    def row_spec(width):
        return pl.BlockSpec((SPAN, width), lambda g: (block_of(g), 0))

    return pl.pallas_call(
        body,
        name=name,
        grid=(n_blocks + 1,),
        in_specs=[
            *_qkv_specs(block_of),
            row_spec(D_MODEL),
            row_spec(LANES),
            row_spec(LANES),
            pl.BlockSpec((B_HEADS // 2, 2 * SPAN, 2 * SPAN), lambda g: (0, 0, 0)),
            pl.BlockSpec((B_HEADS // 2, 2 * SPAN, 2 * SPAN), lambda g: (0, 0, 0)),
        ],
        out_specs=pl.BlockSpec((SPAN, 3 * D_MODEL), lambda g: (jnp.maximum(g - 1, 0), 0)),
        out_shape=jax.ShapeDtypeStruct((SEQ, 3 * D_MODEL), BF16),
        scratch_shapes=[pltpu.VMEM((SPAN, D_MODEL), F32)] * 3,
        compiler_params=_params(("arbitrary",)),
    )(qkv_p, qkv_p, qkv_p, qkv_p, qkv_p, do_p, lse_p, delta_p, bias, bias_t)


def _position():
    x, y, c = lax.axis_index("x"), lax.axis_index("y"), lax.axis_index("c")
    return x, y, c, 4 * x + 2 * y + c


def _peer(k, x, y, c):
    px = 1 - x if k & 4 else x
    py = 1 - y if k & 2 else y
    pc = 1 - c if k & 1 else c
    return (px, py, pc), 4 * px + 2 * py + pc


def _remote(src, dst, send_sem, recv_sem, device):
    return pltpu.make_async_remote_copy(
        src_ref=src, dst_ref=dst, send_sem=send_sem, recv_sem=recv_sem, device_id=device, device_id_type=MESH
    )


def _silu_bf16(cf):
    return (cf * (1.0 / (1.0 + jnp.exp(-cf)))).astype(BF16)


def ada_exchange(c8, w4, b4, ln8):
    nt, _, ncol = w4.shape

    def body(c8_ref, w_ref, b_ref, ln_ref, cg_ref, lng_ref, mrecv_ref, mloc_ref, send_sems, recv_sems):
        x, y, c, me = _position()
        cg_ref[me] = c8_ref[...]
        lng_ref[me] = ln_ref[...]
        first = []
        for k in range(1, N_DEV):
            dev, _ = _peer(k, x, y, c)
            first.append(_remote(c8_ref, cg_ref.at[me], send_sems.at[0, k], recv_sems.at[0, k], dev))
            first.append(_remote(ln_ref, lng_ref.at[me], send_sems.at[1, k], recv_sems.at[1, k], dev))
        for cp in first:
            cp.start()
        for k in range(1, N_DEV):
            dev, pid = _peer(k, x, y, c)
            _remote(c8_ref, cg_ref.at[pid], send_sems.at[0, k], recv_sems.at[0, k], dev).wait_recv()
            _remote(ln_ref, lng_ref.at[pid], send_sems.at[1, k], recv_sems.at[1, k], dev).wait_recv()
        sc = _silu_bf16(cg_ref[...].reshape(N_DEV * SUBLANES, D_MODEL))
        for t in range(nt):
            mloc_ref[t] = jnp.dot(sc, w_ref[t].astype(BF16), preferred_element_type=F32) + b_ref[t : t + 1, :]

        def group(dev_id):
            return pl.ds(pl.multiple_of(dev_id * SUBLANES, SUBLANES), SUBLANES)

        mrecv_ref[me] = mloc_ref[:, group(me), :]
        second = []
        for k in range(1, N_DEV):
            dev, pid = _peer(k, x, y, c)
            second.append(
                _remote(mloc_ref.at[:, group(pid), :], mrecv_ref.at[me], send_sems.at[2, k], recv_sems.at[2, k], dev)
            )
        for cp in second:
            cp.start()
        for k in range(1, N_DEV):
            dev, pid = _peer(k, x, y, c)
            _remote(
                mloc_ref.at[:, group(pid), :], mrecv_ref.at[pid], send_sems.at[2, k], recv_sems.at[2, k], dev
            ).wait_recv()
        for cp in first + second:
            cp.wait_send()

    return pl.pallas_call(
        body,
        name="ada_exchange",
        in_specs=[VMEM, VMEM, VMEM, VMEM],
        out_specs=[VMEM, VMEM, VMEM],
        out_shape=[
            jax.ShapeDtypeStruct((N_DEV, SUBLANES, D_MODEL), F32),
            jax.ShapeDtypeStruct((N_DEV, SUBLANES, LANES), F32),
            jax.ShapeDtypeStruct((N_DEV, nt, SUBLANES, ncol), F32),
        ],
        scratch_shapes=[
            pltpu.VMEM((nt, N_DEV * SUBLANES, ncol), F32),
            pltpu.SemaphoreType.DMA((3, N_DEV)),
            pltpu.SemaphoreType.DMA((3, N_DEV)),
        ],
        compiler_params=pltpu.CompilerParams(vmem_limit_bytes=VMEM_LIMIT_BYTES),
    )(c8, w4, b4, ln8)


def small_exchange(dmx, flat):
    def body(dmx_ref, flat_ref, dmrecv_ref, red_ref, land_ref, send_sems, recv_sems):
        x, y, c, me = _position()
        dmrecv_ref[me] = dmx_ref[me]
        land_ref[me] = flat_ref[me]
        first = []
        for k in range(1, N_DEV):
            dev, pid = _peer(k, x, y, c)
            first.append(_remote(dmx_ref.at[pid], dmrecv_ref.at[me], send_sems.at[0, k], recv_sems.at[0, k], dev))
            first.append(_remote(flat_ref.at[pid], land_ref.at[me], send_sems.at[1, k], recv_sems.at[1, k], dev))
        for cp in first:
            cp.start()
        for k in range(1, N_DEV):
            dev, pid = _peer(k, x, y, c)
            _remote(dmx_ref.at[pid], dmrecv_ref.at[pid], send_sems.at[0, k], recv_sems.at[0, k], dev).wait_recv()
            _remote(flat_ref.at[pid], land_ref.at[pid], send_sems.at[1, k], recv_sems.at[1, k], dev).wait_recv()
        total = land_ref[0]
        for s in range(1, N_DEV):
            total = total + land_ref[s]
        red_ref[me] = total
        second = []
        for k in range(1, N_DEV):
            dev, _ = _peer(k, x, y, c)
            second.append(_remote(red_ref.at[me], red_ref.at[me], send_sems.at[2, k], recv_sems.at[2, k], dev))
        for cp in second:
            cp.start()
        for k in range(1, N_DEV):
            dev, pid = _peer(k, x, y, c)
            _remote(red_ref.at[pid], red_ref.at[pid], send_sems.at[2, k], recv_sems.at[2, k], dev).wait_recv()
        for cp in first + second:
            cp.wait_send()

    return pl.pallas_call(
        body,
        name="small_exchange",
        in_specs=[VMEM, VMEM],
        out_specs=[VMEM, VMEM],
        out_shape=[jax.ShapeDtypeStruct(dmx.shape, F32), jax.ShapeDtypeStruct(flat.shape, F32)],
        scratch_shapes=[
            pltpu.VMEM(flat.shape, F32),
            pltpu.SemaphoreType.DMA((3, N_DEV)),
            pltpu.SemaphoreType.DMA((3, N_DEV)),
        ],
        compiler_params=pltpu.CompilerParams(vmem_limit_bytes=VMEM_LIMIT_BYTES),
    )(dmx, flat)


HBM = pl.BlockSpec(memory_space=pltpu.HBM)
SEM = pl.BlockSpec(memory_space=pltpu.SEMAPHORE)
EFFECT = pltpu.SideEffectType.DATAFLOW_SIDE_EFFECTING


REGROUP_ROWS = 256


def shards_to_columns(x, *, name):
    p, k, n = x.shape

    def body(x_ref, o_ref):
        for s in range(p):
            o_ref[:, s * n : (s + 1) * n] = x_ref[s]

    return pl.pallas_call(
        body,
        name=name,
        grid=(k // REGROUP_ROWS,),
        in_specs=[pl.BlockSpec((p, REGROUP_ROWS, n), lambda i: (0, i, 0))],
        out_specs=pl.BlockSpec((REGROUP_ROWS, p * n), lambda i: (i, 0)),
        out_shape=jax.ShapeDtypeStruct((k, p * n), x.dtype),
        compiler_params=_params(("parallel",)),
    )(x)


def columns_to_shards(xs, *, name):
    k = xs[0].shape[0]
    widths = [x.shape[1] for x in xs]
    n = sum(widths) // N_DEV
    pieces = []
    for s in range(N_DEV):
        start = 0
        for i, w in enumerate(widths):
            lo, hi = max(start, s * n), min(start + w, (s + 1) * n)
            if lo < hi:
                pieces.append((i, lo - start, s, lo - s * n, hi - lo))
            start += w

    def body(*refs):
        x_refs, o_ref = refs[: len(xs)], refs[-1]
        for i, c0, s, d0, w in pieces:
            o_ref[s, :, d0 : d0 + w] = x_refs[i][:, c0 : c0 + w]

    return pl.pallas_call(
        body,
        name=name,
        grid=(k // REGROUP_ROWS,),
        in_specs=[pl.BlockSpec((REGROUP_ROWS, w), lambda i: (i, 0)) for w in widths],
        out_specs=pl.BlockSpec((N_DEV, REGROUP_ROWS, n), lambda i: (0, i, 0)),
        out_shape=jax.ShapeDtypeStruct((N_DEV, k, n), xs[0].dtype),
        compiler_params=_params(("parallel",)),
    )(*xs)


def _own_slot(me, block):
    land = lax.empty((N_DEV, *block.shape), block.dtype)
    return lax.dynamic_update_slice_in_dim(land, block[None], me, axis=0)


N_CHIP_PEERS = 3


class Gather:
    def __init__(self, shards, lands, after, *, name):
        nt = len(shards)
        self.name = name

        def body(*refs):
            src_refs, land_refs = refs[:nt], refs[nt : 2 * nt]
            send_sems, recv_sems = refs[2 * nt + 1 : 3 * nt + 1], refs[3 * nt + 1 : 4 * nt + 1]
            token = refs[-1]
            x, y, c, me = _position()
            for t in range(nt):
                for k, dev in enumerate(self._targets(x, y, c)):
                    _remote(src_refs[t], land_refs[t].at[me], send_sems[t].at[k], recv_sems[t].at[k], dev).start()
            token[...] = jnp.zeros_like(token)

        outs = pl.pallas_call(
            body,
            name=name + "_start",
            in_specs=[HBM] * (2 * nt) + [ANY],
            out_specs=[SEM] * (2 * nt) + [HBM] * (2 * nt) + [VMEM],
            out_shape=[pltpu.SemaphoreType.DMA((1 + N_CHIP_PEERS,))] * (2 * nt)
            + [pltpu.HBM(a.shape, a.dtype) for a in (*shards, *lands)]
            + [jax.ShapeDtypeStruct((SUBLANES, LANES), F32)],
            input_output_aliases={i: 2 * nt + i for i in range(2 * nt)},
            compiler_params=pltpu.CompilerParams(has_side_effects=EFFECT),
        )(*[pltpu.with_memory_space_constraint(a, pltpu.HBM) for a in (*shards, *lands)], after)
        self.send_sems, self.recv_sems = list(outs[:nt]), list(outs[nt : 2 * nt])
        self.srcs, self.lands = list(outs[2 * nt : 3 * nt]), list(outs[3 * nt : 4 * nt])
        self.token = outs[-1]

    @staticmethod
    def _chips(x, y):
        return [(1 - x, y), (x, 1 - y), (1 - x, 1 - y)]

    @classmethod
    def _targets(cls, x, y, c):
        return [(x, y, 1 - c)] + [(*chip, c) for chip in cls._chips(x, y)]

    def zero(self):
        return self.token[0, 0]

    def wait(self, which, after, *, name):
        n = len(which)

        def slot(px, py, pc):
            return 4 * px + 2 * py + pc

        def pass_body(*refs):
            land_refs, recv_sems = refs[:n], refs[n : 2 * n]
            fwd_send, fwd_recv = refs[3 * n + 1 : 4 * n + 1], refs[4 * n + 1 : 5 * n + 1]
            x, y, c, _ = _position()
            for t in range(n):
                for j, chip in enumerate(self._chips(x, y)):
                    blk = land_refs[t].at[slot(*chip, c)]
                    _remote(blk, blk, fwd_send[t].at[j], recv_sems[t].at[1 + j], (*chip, c)).wait_recv()
                    _remote(blk, blk, fwd_send[t].at[j], fwd_recv[t].at[j], (x, y, 1 - c)).start()

        lands = [self.lands[t] for t in which]
        outs = pl.pallas_call(
            pass_body,
            name=name + "_pass",
            in_specs=[HBM] * n + [SEM] * n + [ANY],
            out_specs=[HBM] * n + [SEM] * (2 * n),
            out_shape=[pltpu.HBM(a.shape, a.dtype) for a in lands] + [pltpu.SemaphoreType.DMA((N_CHIP_PEERS,))] * (2 * n),
            input_output_aliases={i: i for i in range(n)},
            compiler_params=pltpu.CompilerParams(has_side_effects=EFFECT),
        )(*lands, *[self.recv_sems[t] for t in which], after)
        lands, fwd_send, fwd_recv = outs[:n], outs[n : 2 * n], outs[2 * n :]

        def wait_body(*refs):
            src_refs, land_refs = refs[:n], refs[n : 2 * n]
            send_sems, recv_sems = refs[2 * n : 3 * n], refs[3 * n : 4 * n]
            fwd_send, fwd_recv = refs[4 * n : 5 * n], refs[5 * n : 6 * n]
            x, y, c, me = _position()
            sibling = (x, y, 1 - c)
            for t in range(n):
                for k, dev in enumerate(self._targets(x, y, c)):
                    _remote(src_refs[t], land_refs[t].at[me], send_sems[t].at[k], recv_sems[t].at[k], dev).wait_send()
                blk = land_refs[t].at[slot(x, y, 1 - c)]
                _remote(blk, blk, send_sems[t].at[0], recv_sems[t].at[0], sibling).wait_recv()
                for j, chip in enumerate(self._chips(x, y)):
                    sent = land_refs[t].at[slot(*chip, c)]
                    _remote(sent, sent, fwd_send[t].at[j], fwd_recv[t].at[j], sibling).wait_send()
                    got = land_refs[t].at[slot(*chip, 1 - c)]
                    _remote(got, got, fwd_send[t].at[j], fwd_recv[t].at[j], sibling).wait_recv()

        srcs = [self.srcs[t] for t in which]
        outs = pl.pallas_call(
            wait_body,
            name=name,
            in_specs=[HBM] * (2 * n) + [SEM] * (4 * n),
            out_specs=[HBM] * (2 * n),
            out_shape=[pltpu.HBM(a.shape, a.dtype) for a in (*srcs, *lands)],
            input_output_aliases={i: i for i in range(2 * n)},
            compiler_params=pltpu.CompilerParams(has_side_effects=EFFECT),
        )(*srcs, *lands, *[self.send_sems[t] for t in which], *[self.recv_sems[t] for t in which], *fwd_send, *fwd_recv)
        return outs[n:]


class Scatter:
    def __init__(self, srcs, lands, after, *, name):
        self.name = name
        nt = self.nt = len(srcs)
        peers = N_DEV - 1

        def body(*refs):
            src_refs, land_refs = refs[:nt], refs[nt : 2 * nt]
            send_sems, recv_sems = refs[2 * nt + 1 : 3 * nt + 1], refs[3 * nt + 1 : 4 * nt + 1]
            token = refs[-1]
            x, y, c, me = _position()
            for t in range(nt):
                for k in range(1, N_DEV):
                    dev, pid = _peer(k, x, y, c)
                    src = src_refs[t].at[pid]
                    _remote(src, land_refs[t].at[me], send_sems[t].at[k - 1], recv_sems[t].at[k - 1], dev).start()
            token[...] = jnp.zeros_like(token)

        outs = pl.pallas_call(
            body,
            name=name + "_start",
            in_specs=[HBM] * (2 * nt) + [ANY],
            out_specs=[SEM] * (2 * nt) + [HBM] * (2 * nt) + [VMEM],
            out_shape=[pltpu.SemaphoreType.DMA((peers,))] * (2 * nt)
            + [pltpu.HBM(a.shape, a.dtype) for a in (*srcs, *lands)]
            + [jax.ShapeDtypeStruct((SUBLANES, LANES), F32)],
            input_output_aliases={i: 2 * nt + i for i in range(2 * nt)},
            compiler_params=pltpu.CompilerParams(has_side_effects=EFFECT),
        )(*[pltpu.with_memory_space_constraint(a, pltpu.HBM) for a in (*srcs, *lands)], after)
        self.send_sems, self.recv_sems = outs[:nt], outs[nt : 2 * nt]
        self.srcs, self.lands = outs[2 * nt : 3 * nt], outs[3 * nt : 4 * nt]
        self.token = outs[-1]

    def zero(self):
        return self.token[0, 0]

    def wait(self, which, after, *, name):
        n = len(which)

        def body(*refs):
            src_refs, land_refs = refs[:n], refs[n : 2 * n]
            send_sems, recv_sems = refs[2 * n : 3 * n], refs[3 * n : 4 * n]
            x, y, c, _ = _position()
            for t in range(n):
                for k in range(1, N_DEV):
                    dev, pid = _peer(k, x, y, c)
                    src = src_refs[t].at[pid]
                    cp = _remote(src, land_refs[t].at[pid], send_sems[t].at[k - 1], recv_sems[t].at[k - 1], dev)
                    cp.wait_send()
                    cp.wait_recv()

        srcs = [self.srcs[t] for t in which]
        lands = [self.lands[t] for t in which]
        outs = pl.pallas_call(
            body,
            name=name,
            in_specs=[HBM] * (2 * n) + [SEM] * (2 * n) + [ANY],
            out_specs=[HBM] * (2 * n),
            out_shape=[pltpu.HBM(a.shape, a.dtype) for a in (*srcs, *lands)],
            input_output_aliases={i: i for i in range(2 * n)},
            compiler_params=pltpu.CompilerParams(has_side_effects=EFFECT),
        )(*srcs, *lands, *[self.send_sems[t] for t in which], *[self.recv_sems[t] for t in which], after)
        return outs[n:]


def _adam_update(g, w, m, v):
    m2 = ADAM_B1 * m + (1.0 - ADAM_B1) * g
    v2 = ADAM_B2 * v + (1.0 - ADAM_B2) * jnp.square(g)
    m_hat = m2 / (1.0 - ADAM_B1**ADAM_STEP)
    v_hat = v2 / (1.0 - ADAM_B2**ADAM_STEP)
    delta = -ADAM_LR * (m_hat / (jnp.sqrt(v_hat) + ADAM_EPS) + ADAM_WD * w)
    return delta, m2, v2


def adamw(gparts, w, m, v, *, name):
    nl, r, c = w.shape
    p = gparts[0].shape[0]
    tr = r if r <= 256 else (256 if c <= D_MODEL else 128)
    ni = r // tr

    def body(*refs):
        g_refs = refs[:nl]
        w_ref, m_ref, v_ref, go_ref, d_ref, mo_ref, vo_ref = refs[nl:]
        for layer in range(nl):

            @pl.when(pl.program_id(0) == layer)
            def _(g_ref=g_refs[layer]):
                g = g_ref[0].astype(F32)
                for i in range(1, p):
                    g = g + g_ref[i].astype(F32)
                delta, m2, v2 = _adam_update(g, w_ref[...], m_ref[...], v_ref[...])
                go_ref[...] = g
                d_ref[...] = delta
                mo_ref[...] = m2
                vo_ref[...] = v2

    def parts_spec(layer):
        def index(l, i):
            return (0, jnp.where(l == layer, i, jnp.where(l < layer, 0, ni - 1)), 0)

        return pl.BlockSpec((p, tr, c), index)

    blk = pl.BlockSpec((None, tr, c), lambda l, i: (l, i, 0))
    return pl.pallas_call(
        body,
        name=name,
        grid=(nl, ni),
        in_specs=[*[parts_spec(layer) for layer in range(nl)], blk, blk, blk],
        out_specs=[blk] * 4,
        out_shape=[jax.ShapeDtypeStruct((nl, r, c), F32)] * 4,
        compiler_params=_params(("arbitrary", "arbitrary")),
    )(*gparts, w, m, v)


def ada_grad_adamw(cg, dmrecv, w4, m4, v4, *, name):
    nt, k, ncol = w4.shape

    def body(cg_ref, dm_ref, w_ref, m_ref, v_ref, go_ref, d_ref, mo_ref, vo_ref, gb_ref):
        sc = _silu_bf16(cg_ref[...].reshape(N_DEV * SUBLANES, k))
        dm = dm_ref[...].reshape(N_DEV * SUBLANES, ncol)
        g = lax.dot_general(sc, dm.astype(BF16), (((0,), (0,)), ((), ())), preferred_element_type=F32)
        delta, m2, v2 = _adam_update(g, w_ref[...], m_ref[...], v_ref[...])
        go_ref[...] = g
        d_ref[...] = delta
        mo_ref[...] = m2
        vo_ref[...] = v2
        gb_ref[...] = jnp.broadcast_to(jnp.sum(dm, axis=0, keepdims=True), (SUBLANES, ncol))

    wblk = pl.BlockSpec((None, k, ncol), lambda t: (t, 0, 0))
    return pl.pallas_call(
        body,
        name=name,
        grid=(nt,),
        in_specs=[
            pl.BlockSpec((N_DEV, SUBLANES, k), lambda t: (0, 0, 0)),
            pl.BlockSpec((N_DEV, None, SUBLANES, ncol), lambda t: (0, t, 0, 0)),
            wblk,
            wblk,
            wblk,
        ],
        out_specs=[wblk] * 4 + [pl.BlockSpec((None, SUBLANES, ncol), lambda t: (t, 0, 0))],
        out_shape=[jax.ShapeDtypeStruct((nt, k, ncol), F32)] * 4 + [jax.ShapeDtypeStruct((nt, SUBLANES, ncol), F32)],
        compiler_params=_params(("parallel",)),
    )(cg, dmrecv, w4, m4, v4)


def kernel(x, c, ada_w, ada_b, ln_g, ln_b, a_w_in, a_b_in, a_vn_g, a_vn_b, a_w_s, a_b_s, a_w_out, b_w_qkv, b_w_out, mlp_w_up, mlp_w_down, loss_target, m_ada_w, m_ada_b, m_ln_g, m_ln_b, m_a_w_in, m_a_b_in, m_a_vn_g, m_a_vn_b, m_a_w_s, m_a_b_s, m_a_w_out, m_b_w_qkv, m_b_w_out, m_mlp_w_up, m_mlp_w_down, v_ada_w, v_ada_b, v_ln_g, v_ln_b, v_a_w_in, v_a_b_in, v_a_vn_g, v_a_vn_b, v_a_w_s, v_a_b_s, v_a_w_out, v_b_w_qkv, v_b_w_out, v_mlp_w_up, v_mlp_w_down):
    x0 = x[0]
    target = loss_target[0]
    me = 4 * lax.axis_index("x") + 2 * lax.axis_index("y") + lax.axis_index("c")

    ada_w4 = ada_w.reshape(N_SUB, D_MODEL, -1)
    ada_b4 = ada_b.reshape(N_SUB, -1)
    ln8 = jnp.concatenate([ln_g.reshape(N_SUB, -1), ln_b.reshape(N_SUB, -1)], axis=0)
    c8 = jnp.broadcast_to(c, (SUBLANES, D_MODEL))
    cg, lng, mrecv = ada_exchange(c8, ada_w4, ada_b4, ln8)

    W_IN, W_AOUT, W_UP0, W_DN0, W_QKV, W_BOUT, W_UP1, W_DN1 = range(8)
    shards = [
        a_w_in[0].astype(BF16),
        a_w_out[0].astype(BF16),
        mlp_w_up[0].astype(BF16),
        mlp_w_down[0].astype(BF16),
        b_w_qkv[0].astype(BF16),
        b_w_out[0].astype(BF16),
        mlp_w_up[1].astype(BF16),
        mlp_w_down[1].astype(BF16),
    ]
    gather = Gather(shards, [_own_slot(me, s) for s in shards], mrecv, name="gather")

    modv = mrecv[:, :, 0, :].transpose(1, 0, 2).reshape(N_SUB, 3 * D_MODEL) + gather.zero()
    shift = [modv[t : t + 1, :D_MODEL] for t in range(N_SUB)]
    scale = [modv[t : t + 1, D_MODEL : 2 * D_MODEL] for t in range(N_SUB)]
    gate1 = [1.0 + modv[t : t + 1, 2 * D_MODEL :] for t in range(N_SUB)]
    lng_full = [lng[:, t, :].reshape(1, D_MODEL) for t in range(N_SUB)]
    lnb_full = [lng[:, N_SUB + t, :].reshape(1, D_MODEL) for t in range(N_SUB)]

    ident = lambda acc: (acc,)
    def relu2(a):
        r = jnp.maximum(a, jnp.zeros_like(a))
        return r * r
    vn_g, vn_b, w_s = a_vn_g, a_vn_b, a_w_s[0]
    bias_full = jnp.repeat(a_b_s[0].T, A_GROUP_DIM, axis=1)
    w_up3, w_dn3 = [None, None], [None, None]

    def mlp_forward(i, h, after, x_in=None, t_next=None):
        up, dn = gather.wait([W_UP0, W_DN0] if i == 0 else [W_UP1, W_DN1], after, name=f"gather_wait_mlp{i}")
        w_up3[i], w_dn3[i] = up, dn.reshape(1, D_FF, D_MODEL)
        (a,) = mm_nn(h, w_up3[i], name=f"mlp{i}_up", tm=2048, ps=2, tn=512, tk=D_MODEL, epilogue=ident, outs=(BF16,))
        if t_next is None:
            (y,) = mm_nn(
                a, w_dn3[i], name=f"mlp{i}_down", tm=1024, ps=1, tn=512, tk=D_FF, prologue=relu2, epilogue=ident, outs=(BF16,)
            )
            return a, y
        y, xn, hn = mm_nn(
            a, w_dn3[i], name=f"mlp{i}_down", tm=512, ps=1, tn=D_MODEL, tk=D_FF, prologue=relu2,
            epilogue=residual_ln_epilogue, extras=residual_extras(x_in, t_next - 1), outs=(BF16, F32, BF16),
        )
        return a, y, xn, hn

    def residual_extras(x_in, t):
        rows = (gate1[t], lng_full[t], lnb_full[t], scale[t + 1], shift[t + 1])
        return [(x_in, "full")] + [(r, "row") for r in rows]

    h0 = modulate(x0, scale[0], shift[0], name="modulate0")
    w_in3, w_aout3 = gather.wait([W_IN, W_AOUT], h0, name="gather_wait_a")
    w_aout3 = w_aout3.reshape(1, D_MODEL, D_MODEL)
    (a_pre,) = mm_nn(
        h0, w_in3, name="a_in", tm=2048, ps=4, tn=256, tk=D_MODEL, epilogue=lambda acc, b: (acc + b,),
        extras=[(a_b_in, "row")], outs=(BF16,),
    )
    p_gate = gate_fwd(a_pre, vn_g, vn_b, w_s, bias_full, name="gate_fwd")
    y0, x1, h1 = mm_nn(
        p_gate, w_aout3, name="a_out", tm=1024, ps=1, tn=D_MODEL, tk=D_MODEL, epilogue=residual_ln_epilogue,
        extras=residual_extras(x0, 0), outs=(BF16, F32, BF16),
    )
    a1, y1, x2, h2 = mlp_forward(0, h1, y0, x_in=x1, t_next=2)
    w_qkv_shards, w_bout3 = gather.wait([W_QKV, W_BOUT], y1, name="gather_wait_b")
    w_bout3 = w_bout3.reshape(1, D_MODEL, D_MODEL)
    w_qkv3 = shards_to_columns(w_qkv_shards, name="w_qkv_columns")[None]
    pat_tiles = 3
    dil = [d for _, d in B_PATTERNS]
    qkv_p, pat_o, pat_lse = [], [], []
    for g in range(N_PAT):
        (qkv_g,) = mm_nn(
            h2, w_qkv3, name=f"b_qkv{g}", tm=2048, ps=1, tn=D_MODEL, tk=D_MODEL, epilogue=ident, outs=(BF16,),
            b_tile0=pat_tiles * g, b_tiles=pat_tiles, out_streams=dil[g],
        )
        o_g, lse_g = attn_fwd(qkv_g, g, name=f"attn_fwd{g}")
        if g > 0:
            o_g, lse_g = permute_rows([o_g, lse_g], dil[g], inverse=True, name=f"unperm_o{g}")
        qkv_p.append(qkv_g)
        pat_o.append(o_g)
        pat_lse.append(lse_g)
    o_b, o_f, lse = attn_combine(pat_o, pat_lse, name="attn_combine")
    y2, x3, h3 = mm_nn(
        o_b, w_bout3, name="b_out", tm=1024, ps=1, tn=D_MODEL, tk=D_MODEL, epilogue=residual_ln_epilogue,
        extras=residual_extras(x2, 2), outs=(BF16, F32, BF16),
    )
    a3, y3 = mlp_forward(1, h3, y2)
    dxo, loss_local = residual_ln_loss(x3, y3, gate1[3], lng_full[3], lnb_full[3], target, name="res_ln3_loss")

    def scatter(parts, after, name):
        parts = [p.reshape(N_DEV, -1, p.shape[-1]) for p in parts]
        lands = [_own_slot(me, lax.dynamic_index_in_dim(p, me, 0, keepdims=False)) for p in parts]
        return Scatter(parts, lands, after, name=name)

    def mlp_backward(i, h, a, dy):
        (da,) = mm_nt(
            dy,
            w_dn3[i],
            name=f"mlp{i}_da",
            tm=2048,
            tko=1024,
            ps=1,
            tc=D_MODEL,
            epilogue=lambda acc, act: (acc * (2.0 * jnp.maximum(act.astype(F32), 0.0)),),
            extras=[(a, "full")],
            outs=(BF16,),
        )
        (dh,) = mm_nt(da, w_up3[i], name=f"mlp{i}_dh", tm=1024, tko=512, ps=N_DEV, tc=512, epilogue=ident, outs=(F32,))
        dw_dn = mm_tn(
            a, dy, name=f"mlp{i}_dw_down", p=1, tk=1024, ps=1, tn=D_MODEL, tmc=2048, prologue=relu2, out_dtype=BF16
        )
        dw_up = mm_tn(h, da, name=f"mlp{i}_dw_up", p=N_DEV, tk=1024, ps=2, tn=512, tmc=2048, out_dtype=BF16)
        return scatter([dw_up, dw_dn], dh, f"scatter_mlp{i}"), dh

    dz3, dy3, st3 = residual_ln_bwd(name="res_bwd3", dxo=dxo, this=(x3, y3, gate1[3], lng_full[3], lnb_full[3]))
    rs_mlp1, dh3 = mlp_backward(1, h3, a3, dy3)
    dz2, dy2, st2 = residual_ln_bwd(
        name="res_bwd2", later=(dh3, dz3, scale[3] + rs_mlp1.zero()), this=(x2, y2, gate1[2], lng_full[2], lnb_full[2])
    )
    (d_o,) = mm_nt(dy2, w_bout3, name="b_do", tm=2048, tko=1024, ps=1, tc=D_MODEL, epilogue=ident, outs=(F32,))
    do_b, delta = attn_delta(d_o, o_f, name="attn_delta")
    dh2, dw_pat = None, []
    for g in range(N_PAT):
        do_g, lse_g, delta_g = do_b, lse, delta
        if g > 0:
            do_g, lse_g, delta_g = permute_rows([do_b, lse, delta], dil[g], inverse=False, name=f"perm_do{g}")
        dqkv_g = attn_bwd(qkv_p[g], do_g, lse_g, delta_g, g, name=f"attn_bwd{g}")
        (dh2,) = mm_nt(
            dqkv_g, w_qkv3, name=f"b_dh{g}", tm=2048, tko=512, ps=1, tc=pat_tiles * D_MODEL, outs=(F32,),
            b_tile0=g, g_streams=dil[g],
            epilogue=ident if g == 0 else (lambda acc, prev, d=dil[g]: (_from_streams(acc, d) + prev,)),
            extras=[] if g == 0 else [(dh2, "full")],
        )
        dw_pat.append(
            mm_tn(
                h2, dqkv_g, name=f"b_dw_qkv{g}", p=1, tk=1024, ps=1, tn=D_MODEL, tmc=2048, out_dtype=BF16,
                g_streams=dil[g],
            )[0]
        )
    dw_bout = mm_tn(o_b, dy2, name="b_dw_out", p=1, tk=1024, ps=1, tn=D_MODEL, tmc=2048, out_dtype=BF16)
    dw_qkv = columns_to_shards(dw_pat, name="dw_qkv_shards")
    rs_b = scatter([dw_qkv, dw_bout], dh2, "scatter_b")
    dz1, dy1, st1 = residual_ln_bwd(
        name="res_bwd1", later=(dh2, dz2, scale[2] + rs_b.zero()), this=(x1, y1, gate1[1], lng_full[1], lnb_full[1])
    )
    rs_mlp0, dh1 = mlp_backward(0, h1, a1, dy1)
    dz0, dy0, st0 = residual_ln_bwd(
        name="res_bwd0", later=(dh1, dz1, scale[1] + rs_mlp0.zero()), this=(x0, y0, gate1[0], lng_full[0], lnb_full[0])
    )
    (dp_gate,) = mm_nt(dy0, w_aout3, name="a_dp", tm=2048, tko=1024, ps=1, tc=D_MODEL, epilogue=ident, outs=(F32,))
    dw_aout = mm_tn(p_gate, dy0, name="a_dw_out", p=1, tk=1024, ps=1, tn=D_MODEL, tmc=2048, out_dtype=BF16)
    rs_aout = scatter([dw_aout], dp_gate, "scatter_a_out")
    da0, d_ws, d_bs, gate_rows = gate_bwd(a_pre, dp_gate, vn_g + rs_aout.zero(), vn_b, w_s, bias_full, name="gate_bwd")
    dw_in = mm_tn(h0, da0, name="a_dw_in", p=N_DEV, tk=1024, ps=4, tn=256, tmc=2048, out_dtype=BF16)
    rs_in = scatter([dw_in], d_ws, "scatter_a_in")
    (dh0,) = mm_nt(da0, w_in3, name="a_dh", tm=2048, tko=1024, ps=N_DEV, tc=256, epilogue=ident, outs=(F32,))
    grad_x, stf = residual_ln_bwd(name="res_bwd_in", later=(dh0, dz0, scale[0] + rs_in.zero()), x_out=x0)

    stats_after = [stf, st0, st1, st2]
    stats_own = [st0, st1, st2, st3]
    dm = jnp.stack(
        [
            jnp.concatenate(
                [stats_after[t][ST_DSHIFT], stats_after[t][ST_DSCALE], stats_own[t][ST_DGATE]], axis=0
            )
            for t in range(N_SUB)
        ]
    )
    ncol = 3 * D_MODEL // N_DEV
    dmx = jnp.pad(
        dm.reshape(N_SUB, N_DEV, ncol).transpose(1, 0, 2)[:, :, None, :], ((0, 0), (0, 0), (0, SUBLANES - 1), (0, 0))
    )
    small = [
        gate_rows[0],
        gate_rows[1],
        d_ws.reshape(-1),
        d_bs[:, :A_GROUPS].T.reshape(-1),
        *[stats_own[t][ST_DG] for t in range(N_SUB)],
        *[stats_own[t][ST_DB] for t in range(N_SUB)],
        jnp.pad(loss_local.reshape(1), (0, LANES - 1)),
    ]
    n_small = sum(s.size for s in small)
    part_rows = -(-n_small // (N_DEV * LANES * SUBLANES)) * SUBLANES
    flat = jnp.concatenate(small + [jnp.zeros((N_DEV * part_rows * LANES - n_small,), F32)])
    dmrecv, reduced = small_exchange(dmx, flat.reshape(N_DEV, part_rows, LANES))
    reduced = reduced.reshape(-1)
    sizes = [2 * D_MODEL, D_MODEL, D_MODEL, A_GROUPS * CHUNK * CHUNK, A_GROUPS * CHUNK, N_SUB * D_MODEL, N_SUB * D_MODEL]
    offs = [sum(sizes[:i]) for i in range(len(sizes) + 1)]
    g_b_in, g_vn_g, g_vn_b, g_ws, g_bs, g_lng, g_lnb = [reduced[offs[i] : offs[i + 1]] for i in range(len(sizes))]
    loss = reduced[offs[-1]]

    results = {}

    def update(wname, gparts, w, m, v):
        shape = w.shape
        layers = len(gparts) if isinstance(gparts, list) else 1
        w3 = w.reshape(layers, -1, shape[-1])
        parts = [g.reshape(g.shape[0], *w3.shape[1:]) for g in (gparts if layers > 1 else [gparts])]
        outs = adamw(parts, w3, m.reshape(w3.shape), v.reshape(w3.shape), name=f"adamw_{wname}")
        results[wname] = [o.reshape(shape) for o in outs]

    ada_outs = ada_grad_adamw(cg, dmrecv, ada_w4, m_ada_w.reshape(ada_w4.shape), v_ada_w.reshape(ada_w4.shape), name="ada_grad_adamw")
    results["ada_w"] = [o.reshape(ada_w.shape) for o in ada_outs[:4]]
    update("ada_b", ada_outs[4][:, 0, :][None], ada_b, m_ada_b, v_ada_b)
    ln_cols = D_MODEL // N_DEV
    my_ln = lambda gfull: lax.dynamic_slice_in_dim(gfull.reshape(N_SUB, N_DEV, ln_cols), me, 1, axis=1)
    update("ln_g", my_ln(g_lng).reshape(1, N_SUB, ln_cols), ln_g, m_ln_g, v_ln_g)
    update("ln_b", my_ln(g_lnb).reshape(1, N_SUB, ln_cols), ln_b, m_ln_b, v_ln_b)
    update("a_b_in", g_b_in[None], a_b_in, m_a_b_in, v_a_b_in)
    update("a_vn_g", g_vn_g[None], a_vn_g, m_a_vn_g, v_a_vn_g)
    update("a_vn_b", g_vn_b[None], a_vn_b, m_a_vn_b, v_a_vn_b)
    update("a_w_s", g_ws[None], a_w_s, m_a_w_s, v_a_w_s)
    update("a_b_s", g_bs[None], a_b_s, m_a_b_s, v_a_b_s)
    g_up1, g_dn1 = rs_mlp1.wait([0, 1], grad_x, name="scatter_wait_mlp1")
    g_qkv, g_bout = rs_b.wait([0, 1], grad_x, name="scatter_wait_b")
    update("b_w_qkv", g_qkv, b_w_qkv, m_b_w_qkv, v_b_w_qkv)
    update("b_w_out", g_bout, b_w_out, m_b_w_out, v_b_w_out)
    g_up0, g_dn0 = rs_mlp0.wait([0, 1], grad_x, name="scatter_wait_mlp0")
    update("mlp_w_up", [g_up0, g_up1], mlp_w_up, m_mlp_w_up, v_mlp_w_up)
    update("mlp_w_down", [g_dn0, g_dn1], mlp_w_down, m_mlp_w_down, v_mlp_w_down)
    (g_aout,) = rs_aout.wait([0], grad_x, name="scatter_wait_a_out")
    (g_in,) = rs_in.wait([0], grad_x, name="scatter_wait_a_in")
    update("a_w_in", g_in, a_w_in, m_a_w_in, v_a_w_in)
    update("a_w_out", g_aout, a_w_out, m_a_w_out, v_a_w_out)

    order = ["ada_w", "ada_b", "ln_g", "ln_b", "a_w_in", "a_b_in", "a_vn_g", "a_vn_b", "a_w_s", "a_b_s", "a_w_out", "b_w_qkv", "b_w_out", "mlp_w_up", "mlp_w_down"]
    return (loss, grad_x[None], *[results[n][0] for n in order], *[results[n][1] for n in order],
            *[results[n][2] for n in order], *[results[n][3] for n in order])
```

```python
import math

import jax
import jax.numpy as jnp
from jax import lax
from jax.experimental import pallas as pl
from jax.experimental.pallas import tpu as pltpu

F32 = jnp.float32
BF16 = jnp.bfloat16
MESH = pl.DeviceIdType.MESH
ANY = pl.BlockSpec(memory_space=pl.ANY)
VMEM = pl.BlockSpec(memory_space=pltpu.VMEM)

N_DEV = 8
D_MODEL = 1024
SEQ = 4096
DEPTH = 2
CHUNK = 128
A_GROUPS = 16
A_GROUP_DIM = D_MODEL // A_GROUPS
B_HEADS = 16
B_HEAD_DIM = 64
B_PATTERNS = ((128, 1), (512, 4), (2048, 16))
N_PAT = len(B_PATTERNS)
SPAN = 128
D_FF = 4 * D_MODEL
D_QKV = N_PAT * 3 * D_MODEL
ALPHA = (2 * DEPTH) ** 0.25
LN_EPS = 1e-5
NEG = -1e30
ADAM_LR = 0.001
ADAM_B1 = 0.9
ADAM_B2 = 0.999
ADAM_EPS = 1e-08
ADAM_WD = 0.01
ADAM_STEP = 10
GELU_C = math.sqrt(2.0 / math.pi)
GELU_A = 0.044715

VMEM_LIMIT_BYTES = 56 * 1024 * 1024
LANES = 128
SUBLANES = 8
ROW_TILE = 512
N_SUB = 2 * DEPTH


def _params(sem):
    return pltpu.CompilerParams(dimension_semantics=sem, vmem_limit_bytes=VMEM_LIMIT_BYTES)


def _lane(shape):
    return lax.broadcasted_iota(jnp.int32, shape, len(shape) - 1)


def _split_bf16(x):
    hi = x.astype(BF16)
    lo = (x - hi.astype(F32)).astype(BF16)
    return hi, lo


def _group_expand_matrix(groups_padded, width):
    per = width // A_GROUPS
    r = lax.broadcasted_iota(jnp.int32, (groups_padded, width), 0)
    c = lax.broadcasted_iota(jnp.int32, (groups_padded, width), 1)
    return (c // per == r).astype(BF16)


def _group_reduce_matrix(width, groups_padded):
    per = width // A_GROUPS
    r = lax.broadcasted_iota(jnp.int32, (width, groups_padded), 0)
    c = lax.broadcasted_iota(jnp.int32, (width, groups_padded), 1)
    return (r // per == c).astype(BF16)


def _expand_groups(w):
    e = _group_expand_matrix(LANES, D_MODEL)
    hi, lo = _split_bf16(w)
    return jnp.dot(hi, e, preferred_element_type=F32) + jnp.dot(lo, e, preferred_element_type=F32)


def _reduce_groups(x):
    e = _group_reduce_matrix(D_MODEL, LANES)
    hi, lo = _split_bf16(x)
    return jnp.dot(hi, e, preferred_element_type=F32) + jnp.dot(lo, e, preferred_element_type=F32)


def _to_streams(x, d):
    rows, w = x.shape
    return jnp.swapaxes(x.reshape(rows // d, d, w), 0, 1).reshape(rows, w)


def _from_streams(x, d):
    rows, w = x.shape
    return jnp.swapaxes(x.reshape(d, rows // d, w), 0, 1).reshape(rows, w)


def _column_tiles(p, n, ps, tn):
    assert (ps == 1 or tn == n) and p % ps == 0 and n % tn == 0
    q = n // tn
    return (p // ps) * q, q


def _extra_specs(extras, tm, width):
    specs = []
    for _, kind in extras:
        if kind == "row":
            specs.append(pl.BlockSpec((1, width), lambda i, j, c: (0, j)))
        else:
            specs.append(pl.BlockSpec((tm, width), lambda i, j, c: (i, j)))
    return specs


def mm_nn(a, b3, *, name, tm, ps, tn, tk, epilogue, extras=(), outs, prologue=None, b_tile0=0, b_tiles=None, out_streams=1):
    m, k = a.shape
    p, _, n = b3.shape
    nj, q = _column_tiles(p, n, ps, tn)
    nj = nj if b_tiles is None else b_tiles
    nk = k // tk
    width = ps * tn
    d = out_streams
    assert d == 1 or not extras

    def body(a_ref, b_ref, *rest):
        ex = rest[: len(extras)]
        out_refs = rest[len(extras) : len(extras) + len(outs)]
        kk = pl.program_id(2)
        av = a_ref[...] if prologue is None else prologue(a_ref[...])
        if d > 1:
            av = _to_streams(av, d)

        def finish(cs, acc):
            res = epilogue(acc, *[e[:, cs] for e in ex])
            for o_ref, r in zip(out_refs, res, strict=True):
                if d > 1:
                    o_ref[:, :, cs] = r.astype(o_ref.dtype).reshape(d, tm // d, tn)
                else:
                    o_ref[:, cs] = r.astype(o_ref.dtype)

        for s in range(ps):
            cs = slice(s * tn, (s + 1) * tn)
            part = jnp.dot(av, b_ref[s], preferred_element_type=F32)
            if nk == 1:
                finish(cs, part)
                continue
            acc_ref = rest[-1]

            @pl.when(kk == 0)
            def _(part=part, cs=cs):
                acc_ref[:, cs] = part

            @pl.when(kk > 0)
            def _(part=part, cs=cs):
                acc_ref[:, cs] += part

        if nk > 1:

            @pl.when(kk == nk - 1)
            def _():
                for s in range(ps):
                    cs = slice(s * tn, (s + 1) * tn)
                    finish(cs, rest[-1][:, cs])

    if d > 1:
        out_spec = pl.BlockSpec((d, tm // d, width), lambda i, j, kk: (0, i, j))
        out_shape = (d, m // d, nj * width)
    else:
        out_spec = pl.BlockSpec((tm, width), lambda i, j, kk: (i, j))
        out_shape = (m, nj * width)
    res = pl.pallas_call(
        body,
        name=name,
        grid=(m // tm, nj, nk),
        in_specs=[
            pl.BlockSpec((tm, tk), lambda i, j, kk: (i, kk)),
            pl.BlockSpec((ps, tk, tn), lambda i, j, kk: ((j + b_tile0) // q, kk, (j + b_tile0) % q)),
            *_extra_specs(extras, tm, width),
        ],
        out_specs=[out_spec for _ in outs],
        out_shape=[jax.ShapeDtypeStruct(out_shape, dt) for dt in outs],
        scratch_shapes=[pltpu.VMEM((tm, width), F32)] if nk > 1 else [],
        compiler_params=_params(("parallel", "parallel", "arbitrary")),
    )(a, b3, *[arr for arr, _ in extras])
    return [r.reshape(m, nj * width) for r in res]


def mm_nt(g, b3, *, name, tm, tko, ps, tc, epilogue, extras=(), outs, b_tile0=0, g_streams=1, stats=False):
    m, width = g.shape
    p, k, n = b3.shape
    _, q = _column_tiles(p, n, ps, tc)
    nc = width // (ps * tc)
    ds = g_streams
    assert not stats or tko == k

    def body(g_ref, b_ref, *rest):
        ex = rest[: len(extras)]
        out_refs = rest[len(extras) : len(extras) + len(outs)]
        c = pl.program_id(2)
        gv = g_ref[...].reshape(tm, ps * tc) if ds > 1 else g_ref[...]
        part = None
        for s in range(ps):
            d = lax.dot_general(gv[:, s * tc : (s + 1) * tc], b_ref[s], _NT, preferred_element_type=F32)
            part = d if part is None else part + d

        def finish(acc):
            res = epilogue(acc, *[e[...] for e in ex])
            if stats:
                *res, sums = res
                st_ref = rest[len(extras) + len(outs)]

                @pl.when(pl.program_id(0) == 0)
                def _():
                    st_ref[...] = jnp.zeros_like(st_ref)

                for row, val in enumerate(sums):
                    st_ref[row : row + 1, :] += val
            for o_ref, r in zip(out_refs, res, strict=True):
                o_ref[...] = r.astype(o_ref.dtype)

        if nc == 1:
            finish(part)
            return
        acc_ref = rest[-1]

        @pl.when(c == 0)
        def _():
            acc_ref[...] = part

        @pl.when(c > 0)
        def _():
            acc_ref[...] += part

        @pl.when(c == nc - 1)
        def _():
            finish(acc_ref[...])

    return pl.pallas_call(
        body,
        name=name,
        grid=(m // tm, k // tko, nc),
        in_specs=[
            pl.BlockSpec((ds, tm // ds, ps * tc), lambda i, j, c: (0, i, c))
            if ds > 1
            else pl.BlockSpec((tm, ps * tc), lambda i, j, c: (i, c)),
            pl.BlockSpec((ps, tko, tc), lambda i, j, c: ((c + b_tile0) // q, j, (c + b_tile0) % q)),
            *_extra_specs(extras, tm, tko),
        ],
        out_specs=[pl.BlockSpec((tm, tko), lambda i, j, c: (i, j)) for _ in outs]
        + [pl.BlockSpec((SUBLANES, tko), lambda i, j, c: (0, 0))] * stats,
        out_shape=[jax.ShapeDtypeStruct((m, k), dt) for dt in outs]
        + [jax.ShapeDtypeStruct((SUBLANES, k), F32)] * stats,
        scratch_shapes=[pltpu.VMEM((tm, tko), F32)] if nc > 1 else [],
        compiler_params=_params(("arbitrary" if stats else "parallel", "parallel", "arbitrary")),
    )(g.reshape(ds, m // ds, width) if ds > 1 else g, b3, *[arr for arr, _ in extras])


def mm_tn(a, g, *, name, p, tk, ps, tn, tmc, out_dtype, prologue=None, g_streams=1):
    m, k = a.shape
    width = g.shape[1]
    n = width // p
    nj, q = _column_tiles(p, n, ps, tn)
    nc = m // tmc
    ds = g_streams

    def body(a_ref, g_ref, o_ref, acc_ref):
        c = pl.program_id(2)
        av = a_ref[...] if prologue is None else prologue(a_ref[...])
        gv = g_ref[...]
        if ds > 1:
            av, gv = _to_streams(av, ds), gv.reshape(tmc, ps * tn)
        part = lax.dot_general(av, gv, (((0,), (0,)), ((), ())), preferred_element_type=F32)

        @pl.when(c == 0)
        def _():
            acc_ref[...] = part

        @pl.when(c > 0)
        def _():
            acc_ref[...] += part

        @pl.when(c == nc - 1)
        def _():
            for s in range(ps):
                o_ref[s] = acc_ref[:, s * tn : (s + 1) * tn].astype(o_ref.dtype)

    return pl.pallas_call(
        body,
        name=name,
        grid=(k // tk, nj, nc),
        in_specs=[
            pl.BlockSpec((tmc, tk), lambda i, j, c: (c, i)),
            pl.BlockSpec((ds, tmc // ds, ps * tn), lambda i, j, c: (0, c, j))
            if ds > 1
            else pl.BlockSpec((tmc, ps * tn), lambda i, j, c: (c, j)),
        ],
        out_specs=pl.BlockSpec((ps, tk, tn), lambda i, j, c: (j // q, i, j % q)),
        out_shape=jax.ShapeDtypeStruct((p, k, n), out_dtype),
        scratch_shapes=[pltpu.VMEM((tk, ps * tn), F32)],
        compiler_params=_params(("parallel", "parallel", "arbitrary")),
    )(a, g.reshape(ds, m // ds, width) if ds > 1 else g)


def _rows(cols):
    return pl.BlockSpec((ROW_TILE, cols), lambda i: (i, 0))


def _vec(cols, rows=1):
    return pl.BlockSpec((rows, cols), lambda i: (0, 0))


def _layer_norm_hat(z):
    mu = jnp.mean(z, axis=-1, keepdims=True)
    zc = z - mu
    var = jnp.mean(zc * zc, axis=-1, keepdims=True)
    rstd = lax.rsqrt(var + LN_EPS)
    return zc * rstd, rstd


def modulate(x, scale, shift, *, name):
    s, d = x.shape

    def body(x_ref, sc_ref, sh_ref, h_ref):
        h_ref[...] = (x_ref[...] * (1.0 + sc_ref[...]) + sh_ref[...]).astype(BF16)

    return pl.pallas_call(
        body,
        name=name,
        grid=(s // ROW_TILE,),
        in_specs=[_rows(d), _vec(d), _vec(d)],
        out_specs=_rows(d),
        out_shape=jax.ShapeDtypeStruct((s, d), BF16),
        compiler_params=_params(("parallel",)),
    )(x, scale, shift)


def residual_ln_epilogue(acc, x, gate1, g, b, nscale, nshift):
    y = acc.astype(BF16)
    xhat, _ = _layer_norm_hat(ALPHA * x + gate1 * y.astype(F32))
    xn = xhat * g + b
    return y, xn, xn * (1.0 + nscale) + nshift


ST_DSCALE, ST_DSHIFT, ST_DG, ST_DB, ST_DGATE = 0, 1, 2, 3, 4


def _layer_norm_bwd(g_out, xhat, rstd, g):
    dxh = g_out * g
    m1 = jnp.mean(dxh, axis=-1, keepdims=True)
    m2 = jnp.mean(dxh * xhat, axis=-1, keepdims=True)
    return rstd * (dxh - m1 - xhat * m2)


def _column_sums(vals):
    return [jnp.sum(v, axis=0, keepdims=True) for v in vals]


def residual_bwd_epilogue(dh, dzl, scl, x, y, gate1, g, b):
    yf = y.astype(F32)
    xhat, rstd = _layer_norm_hat(ALPHA * x + gate1 * yf)
    g_out = ALPHA * dzl + dh * (1.0 + scl)
    dz = _layer_norm_bwd(g_out, xhat, rstd, g)
    return dz, dz * gate1, _column_sums([dh * (xhat * g + b), dh, g_out * xhat, g_out, dz * yf])


def input_bwd_epilogue(dh, dzl, scl, x):
    return ALPHA * dzl + dh * (1.0 + scl), _column_sums([dh * x, dh])


def residual_ln_loss_bwd(x, y, gate1, g, b, target, *, name):
    s, d = x.shape

    def body(x_ref, y_ref, gt_ref, g_ref, b_ref, t_ref, dz_ref, dy_ref, st_ref, loss_ref):
        y = y_ref[...].astype(F32)
        gate1 = gt_ref[...]
        xhat, rstd = _layer_norm_hat(ALPHA * x_ref[...] + gate1 * y)
        err = xhat * g_ref[...] + b_ref[...] - t_ref[...]
        g_out = err * (1.0 / d)
        dz = _layer_norm_bwd(g_out, xhat, rstd, g_ref[...])
        dz_ref[...] = dz
        dy_ref[...] = (dz * gate1).astype(BF16)
        part = jnp.sum(jnp.sum(err * err, axis=1, keepdims=True), axis=0, keepdims=True) * (0.5 / d)

        @pl.when(pl.program_id(0) == 0)
        def _():
            st_ref[...] = jnp.zeros_like(st_ref)
            loss_ref[...] = jnp.zeros_like(loss_ref)

        loss_ref[...] += part
        for row, val in zip((ST_DG, ST_DB, ST_DGATE), _column_sums([g_out * xhat, g_out, dz * y]), strict=True):
            st_ref[row : row + 1, :] += val

    return pl.pallas_call(
        body,
        name=name,
        grid=(s // ROW_TILE,),
        in_specs=[_rows(d), _rows(d), _vec(d), _vec(d), _vec(d), _rows(d)],
        out_specs=[_rows(d), _rows(d), _vec(d, SUBLANES), pl.BlockSpec((1, 1), lambda i: (0, 0))],
        out_shape=[
            jax.ShapeDtypeStruct((s, d), F32),
            jax.ShapeDtypeStruct((s, d), BF16),
            jax.ShapeDtypeStruct((SUBLANES, d), F32),
            jax.ShapeDtypeStruct((1, 1), F32),
        ],
        compiler_params=_params(("arbitrary",)),
    )(x, y, gate1, g, b, target)


def residual_ln_bwd(*, name, later=None, dxo=None, x_out=None, this=None):
    lead = later[0] if later is not None else dxo
    s, d = lead.shape
    has_later, has_ln = later is not None, this is not None

    def body(*refs):
        refs = list(refs)
        if has_later:
            dh_ref, dzl_ref, scl_ref = refs[:3]
            refs = refs[3:]
        else:
            dxo_ref = refs.pop(0)
        if has_ln:
            x_ref, y_ref, gt_ref, g_ref, b_ref = refs[:5]
            refs = refs[5:]
            dz_ref, dy_ref, st_ref = refs
        else:
            xo_ref, dx_ref, st_ref = refs

        @pl.when(pl.program_id(0) == 0)
        def _():
            st_ref[...] = jnp.zeros_like(st_ref)

        def acc(row, val):
            st_ref[row : row + 1, :] += jnp.sum(val, axis=0, keepdims=True)

        if has_ln:
            y = y_ref[...].astype(F32)
            gate1 = gt_ref[...]
            xhat, rstd = _layer_norm_hat(ALPHA * x_ref[...] + gate1 * y)
            x_out_v = xhat * g_ref[...] + b_ref[...]
        else:
            x_out_v = xo_ref[...]
        if has_later:
            dh = dh_ref[...]
            g_out = ALPHA * dzl_ref[...] + dh * (1.0 + scl_ref[...])
            acc(ST_DSCALE, dh * x_out_v)
            acc(ST_DSHIFT, dh)
        else:
            g_out = dxo_ref[...]
        if not has_ln:
            dx_ref[...] = g_out
            return
        acc(ST_DG, g_out * xhat)
        acc(ST_DB, g_out)
        dxh = g_out * g_ref[...]
        m1 = jnp.mean(dxh, axis=-1, keepdims=True)
        m2 = jnp.mean(dxh * xhat, axis=-1, keepdims=True)
        dz = rstd * (dxh - m1 - xhat * m2)
        acc(ST_DGATE, dz * y)
        dz_ref[...] = dz
        dy_ref[...] = (dz * gate1).astype(BF16)

    ins, specs = [], []
    if has_later:
        ins += list(later)
        specs += [_rows(d), _rows(d), _vec(d)]
    else:
        ins += [dxo]
        specs += [_rows(d)]
    if not has_ln:
        ins += [x_out]
        specs += [_rows(d)]
    if has_ln:
        ins += list(this)
        specs += [_rows(d), _rows(d), _vec(d), _vec(d), _vec(d)]
        out_specs = [_rows(d), _rows(d), _vec(d, SUBLANES)]
        out_shape = [
            jax.ShapeDtypeStruct((s, d), F32),
            jax.ShapeDtypeStruct((s, d), BF16),
            jax.ShapeDtypeStruct((SUBLANES, d), F32),
        ]
    else:
        out_specs = [_rows(d), _vec(d, SUBLANES)]
        out_shape = [jax.ShapeDtypeStruct((s, d), F32), jax.ShapeDtypeStruct((SUBLANES, d), F32)]
    return pl.pallas_call(
        body,
        name=name,
        grid=(s // ROW_TILE,),
        in_specs=specs,
        out_specs=out_specs,
        out_shape=out_shape,
        compiler_params=_params(("arbitrary",)),
    )(*ins)


GATE_CHUNKS = 4


def _gelu(x, with_grad=False):
    x2 = x * x
    t = jnp.tanh(GELU_C * (x + GELU_A * x2 * x))
    half = 0.5 * (1.0 + t)
    y = x * half
    if not with_grad:
        return y
    return y, half + 0.5 * x * (1.0 - t * t) * (GELU_C * (1.0 + 3.0 * GELU_A * x2))


def _causal_weights(w_ref, transpose):
    t = lax.broadcasted_iota(jnp.int32, (CHUNK, CHUNK), 0)
    s = lax.broadcasted_iota(jnp.int32, (CHUNK, CHUNK), 1)
    out = []
    for g in range(A_GROUPS):
        w = jnp.where(t >= s, w_ref[g], 0.0)
        out.append((w.T if transpose else w).astype(BF16))
    return out


def _spatial(ws, vn, lo_mask):
    rows = vn.shape[0]
    out_rows = []
    for r in range(rows // CHUNK):
        cols = []
        for j in range(A_GROUPS // 2):
            blk = vn[r * CHUNK : (r + 1) * CHUNK, j * LANES : (j + 1) * LANES]
            za = jnp.dot(ws[2 * j], blk, preferred_element_type=F32)
            zb = jnp.dot(ws[2 * j + 1], blk, preferred_element_type=F32)
            cols.append(jnp.where(lo_mask, za, zb))
        out_rows.append(jnp.concatenate(cols, axis=1))
    return jnp.concatenate(out_rows, axis=0)


def _gate_forward(a, vg, vb, ws, bias, lo_mask, with_grad=False):
    u = _gelu(a[:, :D_MODEL], with_grad)
    v = _gelu(a[:, D_MODEL:], with_grad)
    gu, gv = None, None
    if with_grad:
        (u, gu), (v, gv) = u, v
    vhat, rstd = _layer_norm_hat(v)
    vn = (vhat * vg + vb).astype(BF16)
    z = _spatial(ws, vn, lo_mask) + jnp.concatenate([bias] * (a.shape[0] // CHUNK), axis=0)
    return u, vhat, rstd, vn, z, gu, gv


def gate_fwd(a_pre, vn_g, vn_b, w_s, bias_full, *, name):
    s = a_pre.shape[0]
    tr = GATE_CHUNKS * CHUNK

    def body(a_ref, vg_ref, vb_ref, w_ref, bias_ref, p_ref):
        lo_mask = _lane((CHUNK, LANES)) < A_GROUP_DIM
        ws = _causal_weights(w_ref, transpose=False)
        u, _, _, _, z, _, _ = _gate_forward(a_ref[...].astype(F32), vg_ref[...], vb_ref[...], ws, bias_ref[...], lo_mask)
        p_ref[...] = (u * z).astype(BF16)

    return pl.pallas_call(
        body,
        name=name,
        grid=(s // tr,),
        in_specs=[
            pl.BlockSpec((tr, 2 * D_MODEL), lambda i: (i, 0)),
            _vec(D_MODEL),
            _vec(D_MODEL),
            pl.BlockSpec((A_GROUPS, CHUNK, CHUNK), lambda i: (0, 0, 0)),
            _vec(D_MODEL, CHUNK),
        ],
        out_specs=pl.BlockSpec((tr, D_MODEL), lambda i: (i, 0)),
        out_shape=jax.ShapeDtypeStruct((s, D_MODEL), BF16),
        compiler_params=_params(("parallel",)),
    )(a_pre, vn_g, vn_b, w_s, bias_full)


def gate_bwd(a_pre, dp, vn_g, vn_b, w_s, bias_full, *, name):
    s = a_pre.shape[0]
    tr = GATE_CHUNKS * CHUNK
    nsteps = s // tr

    def body(a_ref, dp_ref, vg_ref, vb_ref, w_ref, bias_ref, da_ref, dw_ref, dbs_ref, rows_ref, dbias_acc):
        step = pl.program_id(0)
        lo_mask = _lane((CHUNK, LANES)) < A_GROUP_DIM

        @pl.when(step == 0)
        def _():
            dw_ref[...] = jnp.zeros_like(dw_ref)
            rows_ref[...] = jnp.zeros_like(rows_ref)
            dbias_acc[...] = jnp.zeros_like(dbias_acc)

        a = a_ref[...].astype(F32)
        vg = vg_ref[...]
        ws = _causal_weights(w_ref, transpose=False)
        wts = _causal_weights(w_ref, transpose=True)
        u, vhat, rstd, vn, z, gelu_du, gelu_dv = _gate_forward(a, vg, vb_ref[...], ws, bias_ref[...], lo_mask, True)
        dp = dp_ref[...]
        du = dp * z
        dzz = dp * u
        dzz_b = dzz.astype(BF16)
        dvn = _spatial(wts, dzz_b, lo_mask)
        dbias = None
        for r in range(GATE_CHUNKS):
            rs = slice(r * CHUNK, (r + 1) * CHUNK)
            dbias = dzz[rs] if dbias is None else dbias + dzz[rs]
            for j in range(A_GROUPS // 2):
                cs = slice(j * LANES, (j + 1) * LANES)
                dblk = dzz[rs, cs]
                vblk = vn[rs, cs]
                for half in range(2):
                    keep = lo_mask if half == 0 else jnp.logical_not(lo_mask)
                    dm = jnp.where(keep, dblk, 0.0).astype(BF16)
                    dw_ref[2 * j + half] += lax.dot_general(
                        dm, vblk, (((1,), (1,)), ((), ())), preferred_element_type=F32
                    )
        dbias_acc[...] += dbias
        rows_ref[1:2, :D_MODEL] += jnp.sum(dvn * vhat, axis=0, keepdims=True)
        rows_ref[1:2, D_MODEL:] += jnp.sum(dvn, axis=0, keepdims=True)
        dvh = dvn * vg
        m1 = jnp.mean(dvh, axis=-1, keepdims=True)
        m2 = jnp.mean(dvh * vhat, axis=-1, keepdims=True)
        dv = rstd * (dvh - m1 - vhat * m2)
        da_u = du * gelu_du
        da_v = dv * gelu_dv
        da_ref[:, :D_MODEL] = da_u.astype(BF16)
        da_ref[:, D_MODEL:] = da_v.astype(BF16)
        rows_ref[0:1, :D_MODEL] += jnp.sum(da_u, axis=0, keepdims=True)
        rows_ref[0:1, D_MODEL:] += jnp.sum(da_v, axis=0, keepdims=True)

        @pl.when(step == nsteps - 1)
        def _():
            t = lax.broadcasted_iota(jnp.int32, (CHUNK, CHUNK), 0)
            sx = lax.broadcasted_iota(jnp.int32, (CHUNK, CHUNK), 1)
            for g in range(A_GROUPS):
                dw_ref[g] = jnp.where(t >= sx, dw_ref[g], 0.0)
            dbs_ref[...] = _reduce_groups(dbias_acc[...])

    return pl.pallas_call(
        body,
        name=name,
        grid=(nsteps,),
        in_specs=[
            pl.BlockSpec((tr, 2 * D_MODEL), lambda i: (i, 0)),
            pl.BlockSpec((tr, D_MODEL), lambda i: (i, 0)),
            _vec(D_MODEL),
            _vec(D_MODEL),
            pl.BlockSpec((A_GROUPS, CHUNK, CHUNK), lambda i: (0, 0, 0)),
            _vec(D_MODEL, CHUNK),
        ],
        out_specs=[
            pl.BlockSpec((tr, 2 * D_MODEL), lambda i: (i, 0)),
            pl.BlockSpec((A_GROUPS, CHUNK, CHUNK), lambda i: (0, 0, 0)),
            _vec(LANES, CHUNK),
            _vec(2 * D_MODEL, SUBLANES),
        ],
        out_shape=[
            jax.ShapeDtypeStruct((s, 2 * D_MODEL), BF16),
            jax.ShapeDtypeStruct((A_GROUPS, CHUNK, CHUNK), F32),
            jax.ShapeDtypeStruct((CHUNK, LANES), F32),
            jax.ShapeDtypeStruct((SUBLANES, 2 * D_MODEL), F32),
        ],
        scratch_shapes=[pltpu.VMEM((CHUNK, D_MODEL), F32)],
        compiler_params=_params(("arbitrary",)),
    )(a_pre, dp, vn_g, vn_b, w_s, bias_full)


def alibi_tables(dilation):
    qi = jnp.arange(SPAN)[:, None]
    ki = jnp.arange(2 * SPAN)[None, :]
    diff = SPAN + qi - ki
    valid = (diff >= 0) & (diff <= SPAN)
    heads = jnp.arange(1, B_HEADS + 1, dtype=F32)
    slopes = jnp.exp2(-8.0 * heads / B_HEADS)
    bias = -slopes[:, None, None] * (dilation * diff).astype(F32)
    bias = jnp.where(valid[None], bias, NEG).reshape(B_HEADS // 2, 2 * SPAN, 2 * SPAN)
    return bias, bias.transpose(0, 2, 1)


def _pair_rows(x, halves):
    return jnp.concatenate([x * halves[0], x * halves[1]], axis=0)


def _pair_column(v, lane, j):
    pick = lambda h: jnp.sum(jnp.where(lane == h, v, 0.0), axis=1, keepdims=True)
    return jnp.concatenate([pick(2 * j), pick(2 * j + 1)], axis=0)


_NT = (((1,), (1,)), ((), ()))


def permute_rows(xs, dilation, *, inverse, name):
    s = xs[0].shape[0]
    tile = SPAN * dilation
    nat = [pl.BlockSpec((tile, x.shape[1]), lambda i: (i, 0)) for x in xs]
    streams = [pl.BlockSpec((dilation, SPAN, x.shape[1]), lambda i: (0, i, 0)) for x in xs]
    stream_shape = [jax.ShapeDtypeStruct((dilation, s // dilation, x.shape[1]), x.dtype) for x in xs]

    def body(*refs):
        for x_ref, o_ref in zip(refs[: len(xs)], refs[len(xs) :], strict=True):
            w = x_ref.shape[-1]
            if inverse:
                o_ref[...] = jnp.swapaxes(x_ref[...], 0, 1).reshape(tile, w)
            else:
                o_ref[...] = jnp.swapaxes(x_ref[...].reshape(SPAN, dilation, w), 0, 1)

    outs = pl.pallas_call(
        body,
        name=name,
        grid=(s // tile,),
        in_specs=streams if inverse else nat,
        out_specs=nat if inverse else streams,
        out_shape=[jax.ShapeDtypeStruct(x.shape, x.dtype) for x in xs] if inverse else stream_shape,
        compiler_params=_params(("parallel",)),
    )(*[x.reshape(dilation, s // dilation, x.shape[1]) if inverse else x for x in xs])
    return [o.reshape(x.shape) for o, x in zip(outs, xs, strict=True)]


def _qkv_specs(block_of):
    def spec(which, prev):
        def index(*grid):
            blk = block_of(*grid)
            return (jnp.maximum(blk - 1, 0) if prev else blk, which)

        return pl.BlockSpec((SPAN, D_MODEL), index)

    return [spec(0, False), spec(1, True), spec(1, False), spec(2, True), spec(2, False)]


def attn_fwd(qkv_p, pat, *, name):
    _, dilation = B_PATTERNS[pat]
    nb = SEQ // dilation // SPAN
    bias, _ = alibi_tables(dilation)

    def body(q_ref, kp_ref, kc_ref, vp_ref, vc_ref, bias_ref, o_ref, lse_ref):
        n = pl.program_id(1)
        first_prev = jnp.logical_and(n == 0, _lane((2 * SPAN, 2 * SPAN)) < SPAN)
        lane = _lane((SPAN, LANES))
        lo_mask = lane < B_HEAD_DIM
        q = q_ref[...] * jnp.asarray(B_HEAD_DIM**-0.5, BF16)
        kk = jnp.concatenate([kp_ref[...], kc_ref[...]], axis=0)
        vv = jnp.concatenate([vp_ref[...], vc_ref[...]], axis=0)
        halves = (lo_mask.astype(BF16), jnp.logical_not(lo_mask).astype(BF16))
        stats = jnp.zeros((SPAN, LANES), F32)
        for j in range(B_HEADS // 2):
            cs = slice(j * LANES, (j + 1) * LANES)
            sc = lax.dot_general(_pair_rows(q[:, cs], halves), kk[:, cs], _NT, preferred_element_type=F32)
            sc = jnp.where(first_prev, NEG, sc + bias_ref[j])
            m = jnp.max(sc, axis=1, keepdims=True)
            p = jnp.exp(sc - m)
            l = jnp.sum(p, axis=1, keepdims=True)
            acc = jnp.dot(p.astype(BF16), vv[:, cs], preferred_element_type=F32) * (1.0 / l)
            lse_pair = m + jnp.log(l)
            o_ref[:, cs] = jnp.where(lo_mask, acc[:SPAN], acc[SPAN:]).astype(BF16)
            stats = jnp.where(lane == 2 * j, lse_pair[:SPAN], stats)
            stats = jnp.where(lane == 2 * j + 1, lse_pair[SPAN:], stats)
        lse_ref[...] = stats

    return pl.pallas_call(
        body,
        name=name,
        grid=(dilation, nb),
        in_specs=[
            *_qkv_specs(lambda r, n: r * nb + n),
            pl.BlockSpec((B_HEADS // 2, 2 * SPAN, 2 * SPAN), lambda r, n: (0, 0, 0)),
        ],
        out_specs=[
            pl.BlockSpec((SPAN, D_MODEL), lambda r, n: (r * nb + n, 0)),
            pl.BlockSpec((SPAN, LANES), lambda r, n: (r * nb + n, 0)),
        ],
        out_shape=[jax.ShapeDtypeStruct((SEQ, D_MODEL), BF16), jax.ShapeDtypeStruct((SEQ, LANES), F32)],
        compiler_params=_params(("parallel", "arbitrary")),
    )(qkv_p, qkv_p, qkv_p, qkv_p, qkv_p, bias)


def attn_combine(outs, lses, *, name):
    def body(o0, o1, o2, l0, l1, l2, ob_ref, of_ref, lse_ref):
        ls = [l0[...], l1[...], l2[...]]
        m = jnp.maximum(jnp.maximum(ls[0], ls[1]), ls[2])
        tot = jnp.log(jnp.exp(ls[0] - m) + jnp.exp(ls[1] - m) + jnp.exp(ls[2] - m)) + m
        o = None
        for o_ref, l in zip((o0, o1, o2), ls, strict=True):
            term = _expand_groups(jnp.exp(l - tot)) * o_ref[...]
            o = term if o is None else o + term
        ob_ref[...] = o.astype(BF16)
        of_ref[...] = o
        lse_ref[...] = tot

    return pl.pallas_call(
        body,
        name=name,
        grid=(SEQ // ROW_TILE,),
        in_specs=[_rows(D_MODEL)] * 3 + [_rows(LANES)] * 3,
        out_specs=[_rows(D_MODEL), _rows(D_MODEL), _rows(LANES)],
        out_shape=[
            jax.ShapeDtypeStruct((SEQ, D_MODEL), BF16),
            jax.ShapeDtypeStruct((SEQ, D_MODEL), F32),
            jax.ShapeDtypeStruct((SEQ, LANES), F32),
        ],
        compiler_params=_params(("parallel",)),
    )(*outs, *lses)


def attn_delta(do, o, *, name):
    def body(do_ref, o_ref, dob_ref, dl_ref):
        do_v = do_ref[...]
        dob_ref[...] = do_v.astype(BF16)
        dl_ref[...] = _reduce_groups(do_v * o_ref[...])

    return pl.pallas_call(
        body,
        name=name,
        grid=(SEQ // ROW_TILE,),
        in_specs=[_rows(D_MODEL), _rows(D_MODEL)],
        out_specs=[_rows(D_MODEL), _rows(LANES)],
        out_shape=[jax.ShapeDtypeStruct((SEQ, D_MODEL), BF16), jax.ShapeDtypeStruct((SEQ, LANES), F32)],
        compiler_params=_params(("parallel",)),
    )(do, o)


def attn_bwd(qkv_p, do_p, lse_p, delta_p, pat, *, name):
    _, dilation = B_PATTERNS[pat]
    nb = SEQ // dilation // SPAN
    n_blocks = SEQ // SPAN
    bias, bias_t = alibi_tables(dilation)
    last = n_blocks - 1
    q_cols, k_cols, v_cols = (slice(i * D_MODEL, (i + 1) * D_MODEL) for i in range(3))

    def body(q_ref, kp_ref, kc_ref, vp_ref, vc_ref, do_ref, lse_ref, dl_ref, bias_ref, biast_ref, out_ref, cq_ref, ck_ref, cv_ref):
        g = pl.program_id(0)

        @pl.when(g == n_blocks)
        def _():
            out_ref[:, q_cols] = cq_ref[...].astype(BF16)
            out_ref[:, k_cols] = ck_ref[...].astype(BF16)
            out_ref[:, v_cols] = cv_ref[...].astype(BF16)

        @pl.when(g == 0)
        def _():
            cq_ref[...] = jnp.zeros_like(cq_ref)
            ck_ref[...] = jnp.zeros_like(ck_ref)
            cv_ref[...] = jnp.zeros_like(cv_ref)

        @pl.when(g < n_blocks)
        def _():
            lane = _lane((SPAN, LANES))
            lo_mask = lane < B_HEAD_DIM
            pair = (2 * SPAN, 2 * SPAN)
            first = lax.rem(g, nb) == 0
            prev_key_cols = jnp.logical_and(first, _lane(pair) < SPAN)
            prev_key_rows = jnp.logical_and(first, lax.broadcasted_iota(jnp.int32, pair, 0) < SPAN)
            q = q_ref[...] * jnp.asarray(B_HEAD_DIM**-0.5, BF16)
            kk = jnp.concatenate([kp_ref[...], kc_ref[...]], axis=0)
            vv = jnp.concatenate([vp_ref[...], vc_ref[...]], axis=0)
            do_v = do_ref[...]
            lse_v = lse_ref[...]
            dl_v = dl_ref[...]
            lse_t = lse_v.T
            dl_t = dl_v.T
            halves = (lo_mask.astype(BF16), jnp.logical_not(lo_mask).astype(BF16))
            for j in range(B_HEADS // 2):
                cs = slice(j * LANES, (j + 1) * LANES)
                kp, vp = kk[:, cs], vv[:, cs]
                q2 = _pair_rows(q[:, cs], halves)
                do2 = _pair_rows(do_v[:, cs], halves)
                lse_c, dl_c = _pair_column(lse_v, lane, j), _pair_column(dl_v, lane, j)
                lse_r = jnp.concatenate([lse_t[2 * j : 2 * j + 1], lse_t[2 * j + 1 : 2 * j + 2]], axis=1)
                dl_r = jnp.concatenate([dl_t[2 * j : 2 * j + 1], dl_t[2 * j + 1 : 2 * j + 2]], axis=1)
                sc = lax.dot_general(q2, kp, _NT, preferred_element_type=F32)
                p = jnp.exp(jnp.where(prev_key_cols, NEG, sc + bias_ref[j]) - lse_c)
                dp = lax.dot_general(do2, vp, _NT, preferred_element_type=F32)
                ds = (p * (dp - dl_c)).astype(BF16)
                dq2 = jnp.dot(ds, kp, preferred_element_type=F32)
                sc_t = lax.dot_general(kp, q2, _NT, preferred_element_type=F32)
                p_t = jnp.exp(jnp.where(prev_key_rows, NEG, sc_t + biast_ref[j]) - lse_r)
                dp_t = lax.dot_general(vp, do2, _NT, preferred_element_type=F32)
                ds_t = (p_t * (dp_t - dl_r)).astype(BF16)
                dk_pair = jnp.dot(ds_t, q2, preferred_element_type=F32)
                dv_pair = jnp.dot(p_t.astype(BF16), do2, preferred_element_type=F32)
                oq = slice(j * LANES, (j + 1) * LANES)
                ok = slice(D_MODEL + j * LANES, D_MODEL + (j + 1) * LANES)
                ov = slice(2 * D_MODEL + j * LANES, 2 * D_MODEL + (j + 1) * LANES)
                out_ref[:, oq] = cq_ref[:, cs].astype(BF16)
                out_ref[:, ok] = (ck_ref[:, cs] + dk_pair[:SPAN]).astype(BF16)
                out_ref[:, ov] = (cv_ref[:, cs] + dv_pair[:SPAN]).astype(BF16)
                cq_ref[:, cs] = jnp.where(lo_mask, dq2[:SPAN], dq2[SPAN:]) * (B_HEAD_DIM**-0.5)
                ck_ref[:, cs] = dk_pair[SPAN:]
                cv_ref[:, cs] = dv_pair[SPAN:]

    def block_of(g):
        return jnp.minimum(g, last)

    def row_spec(width):
        return pl.BlockSpec((SPAN, width), lambda g: (block_of(g), 0))

    return pl.pallas_call(
        body,
        name=name,
        grid=(n_blocks + 1,),
        in_specs=[
            *_qkv_specs(block_of),
            row_spec(D_MODEL),
            row_spec(LANES),
            row_spec(LANES),
            pl.BlockSpec((B_HEADS // 2, 2 * SPAN, 2 * SPAN), lambda g: (0, 0, 0)),
            pl.BlockSpec((B_HEADS // 2, 2 * SPAN, 2 * SPAN), lambda g: (0, 0, 0)),
        ],
        out_specs=pl.BlockSpec((SPAN, 3 * D_MODEL), lambda g: (jnp.maximum(g - 1, 0), 0)),
        out_shape=jax.ShapeDtypeStruct((SEQ, 3 * D_MODEL), BF16),
        scratch_shapes=[pltpu.VMEM((SPAN, D_MODEL), F32)] * 3,
        compiler_params=_params(("arbitrary",)),
    )(qkv_p, qkv_p, qkv_p, qkv_p, qkv_p, do_p, lse_p, delta_p, bias, bias_t)


def _position():
    x, y, c = lax.axis_index("x"), lax.axis_index("y"), lax.axis_index("c")
    return x, y, c, 4 * x + 2 * y + c


def _peer(k, x, y, c):
    px = 1 - x if k & 4 else x
    py = 1 - y if k & 2 else y
    pc = 1 - c if k & 1 else c
    return (px, py, pc), 4 * px + 2 * py + pc


def _remote(src, dst, send_sem, recv_sem, device):
    return pltpu.make_async_remote_copy(
        src_ref=src, dst_ref=dst, send_sem=send_sem, recv_sem=recv_sem, device_id=device, device_id_type=MESH
    )


def _silu_bf16(cf):
    return (cf * (1.0 / (1.0 + jnp.exp(-cf)))).astype(BF16)


def ada_exchange(c8, w4, b4, ln8):
    nt, _, ncol = w4.shape

    def body(c8_ref, w_ref, b_ref, ln_ref, cg_ref, lng_ref, mrecv_ref, mloc_ref, send_sems, recv_sems):
        x, y, c, me = _position()
        cg_ref[me] = c8_ref[...]
        lng_ref[me] = ln_ref[...]
        first = []
        for k in range(1, N_DEV):
            dev, _ = _peer(k, x, y, c)
            first.append(_remote(c8_ref, cg_ref.at[me], send_sems.at[0, k], recv_sems.at[0, k], dev))
            first.append(_remote(ln_ref, lng_ref.at[me], send_sems.at[1, k], recv_sems.at[1, k], dev))
        for cp in first:
            cp.start()
        for k in range(1, N_DEV):
            dev, pid = _peer(k, x, y, c)
            _remote(c8_ref, cg_ref.at[pid], send_sems.at[0, k], recv_sems.at[0, k], dev).wait_recv()
            _remote(ln_ref, lng_ref.at[pid], send_sems.at[1, k], recv_sems.at[1, k], dev).wait_recv()
        sc = _silu_bf16(cg_ref[...].reshape(N_DEV * SUBLANES, D_MODEL))
        for t in range(nt):
            mloc_ref[t] = jnp.dot(sc, w_ref[t].astype(BF16), preferred_element_type=F32) + b_ref[t : t + 1, :]

        def group(dev_id):
            return pl.ds(pl.multiple_of(dev_id * SUBLANES, SUBLANES), SUBLANES)

        mrecv_ref[me] = mloc_ref[:, group(me), :]
        second = []
        for k in range(1, N_DEV):
            dev, pid = _peer(k, x, y, c)
            second.append(
                _remote(mloc_ref.at[:, group(pid), :], mrecv_ref.at[me], send_sems.at[2, k], recv_sems.at[2, k], dev)
            )
        for cp in second:
            cp.start()
        for k in range(1, N_DEV):
            dev, pid = _peer(k, x, y, c)
            _remote(
                mloc_ref.at[:, group(pid), :], mrecv_ref.at[pid], send_sems.at[2, k], recv_sems.at[2, k], dev
            ).wait_recv()
        for cp in first + second:
            cp.wait_send()

    return pl.pallas_call(
        body,
        name="ada_exchange",
        in_specs=[VMEM, VMEM, VMEM, VMEM],
        out_specs=[VMEM, VMEM, VMEM],
        out_shape=[
            jax.ShapeDtypeStruct((N_DEV, SUBLANES, D_MODEL), F32),
            jax.ShapeDtypeStruct((N_DEV, SUBLANES, LANES), F32),
            jax.ShapeDtypeStruct((N_DEV, nt, SUBLANES, ncol), F32),
        ],
        scratch_shapes=[
            pltpu.VMEM((nt, N_DEV * SUBLANES, ncol), F32),
            pltpu.SemaphoreType.DMA((3, N_DEV)),
            pltpu.SemaphoreType.DMA((3, N_DEV)),
        ],
        compiler_params=pltpu.CompilerParams(vmem_limit_bytes=VMEM_LIMIT_BYTES),
    )(c8, w4, b4, ln8)


def small_exchange(dmx, flat):
    def body(dmx_ref, flat_ref, dmrecv_ref, red_ref, land_ref, send_sems, recv_sems):
        x, y, c, me = _position()
        dmrecv_ref[me] = dmx_ref[me]
        land_ref[me] = flat_ref[me]
        first = []
        for k in range(1, N_DEV):
            dev, pid = _peer(k, x, y, c)
            first.append(_remote(dmx_ref.at[pid], dmrecv_ref.at[me], send_sems.at[0, k], recv_sems.at[0, k], dev))
            first.append(_remote(flat_ref.at[pid], land_ref.at[me], send_sems.at[1, k], recv_sems.at[1, k], dev))
        for cp in first:
            cp.start()
        for k in range(1, N_DEV):
            dev, pid = _peer(k, x, y, c)
            _remote(dmx_ref.at[pid], dmrecv_ref.at[pid], send_sems.at[0, k], recv_sems.at[0, k], dev).wait_recv()
            _remote(flat_ref.at[pid], land_ref.at[pid], send_sems.at[1, k], recv_sems.at[1, k], dev).wait_recv()
        total = land_ref[0]
        for s in range(1, N_DEV):
            total = total + land_ref[s]
        red_ref[me] = total
        second = []
        for k in range(1, N_DEV):
            dev, _ = _peer(k, x, y, c)
            second.append(_remote(red_ref.at[me], red_ref.at[me], send_sems.at[2, k], recv_sems.at[2, k], dev))
        for cp in second:
            cp.start()
        for k in range(1, N_DEV):
            dev, pid = _peer(k, x, y, c)
            _remote(red_ref.at[pid], red_ref.at[pid], send_sems.at[2, k], recv_sems.at[2, k], dev).wait_recv()
        for cp in first + second:
            cp.wait_send()

    return pl.pallas_call(
        body,
        name="small_exchange",
        in_specs=[VMEM, VMEM],
        out_specs=[VMEM, VMEM],
        out_shape=[jax.ShapeDtypeStruct(dmx.shape, F32), jax.ShapeDtypeStruct(flat.shape, F32)],
        scratch_shapes=[
            pltpu.VMEM(flat.shape, F32),
            pltpu.SemaphoreType.DMA((3, N_DEV)),
            pltpu.SemaphoreType.DMA((3, N_DEV)),
        ],
        compiler_params=pltpu.CompilerParams(vmem_limit_bytes=VMEM_LIMIT_BYTES),
    )(dmx, flat)


HBM = pl.BlockSpec(memory_space=pltpu.HBM)
SEM = pl.BlockSpec(memory_space=pltpu.SEMAPHORE)
EFFECT = pltpu.SideEffectType.DATAFLOW_SIDE_EFFECTING


REGROUP_ROWS = 256


def shards_to_columns(x, *, name):
    p, k, n = x.shape

    def body(x_ref, o_ref):
        for s in range(p):
            o_ref[:, s * n : (s + 1) * n] = x_ref[s]

    return pl.pallas_call(
        body,
        name=name,
        grid=(k // REGROUP_ROWS,),
        in_specs=[pl.BlockSpec((p, REGROUP_ROWS, n), lambda i: (0, i, 0))],
        out_specs=pl.BlockSpec((REGROUP_ROWS, p * n), lambda i: (i, 0)),
        out_shape=jax.ShapeDtypeStruct((k, p * n), x.dtype),
        compiler_params=_params(("parallel",)),
    )(x)


def columns_to_shards(xs, *, name):
    k = xs[0].shape[0]
    widths = [x.shape[1] for x in xs]
    n = sum(widths) // N_DEV
    pieces = []
    for s in range(N_DEV):
        start = 0
        for i, w in enumerate(widths):
            lo, hi = max(start, s * n), min(start + w, (s + 1) * n)
            if lo < hi:
                pieces.append((i, lo - start, s, lo - s * n, hi - lo))
            start += w

    def body(*refs):
        x_refs, o_ref = refs[: len(xs)], refs[-1]
        for i, c0, s, d0, w in pieces:
            o_ref[s, :, d0 : d0 + w] = x_refs[i][:, c0 : c0 + w]

    return pl.pallas_call(
        body,
        name=name,
        grid=(k // REGROUP_ROWS,),
        in_specs=[pl.BlockSpec((REGROUP_ROWS, w), lambda i: (i, 0)) for w in widths],
        out_specs=pl.BlockSpec((N_DEV, REGROUP_ROWS, n), lambda i: (0, i, 0)),
        out_shape=jax.ShapeDtypeStruct((N_DEV, k, n), xs[0].dtype),
        compiler_params=_params(("parallel",)),
    )(*xs)


def _own_slot(me, block):
    land = lax.empty((N_DEV, *block.shape), block.dtype)
    return lax.dynamic_update_slice_in_dim(land, block[None], me, axis=0)


N_CHIP_PEERS = 3


class Gather:
    def __init__(self, shards, lands, after, *, name):
        nt = len(shards)
        self.name = name

        def body(*refs):
            src_refs, land_refs = refs[:nt], refs[nt : 2 * nt]
            send_sems, recv_sems = refs[2 * nt + 1 : 3 * nt + 1], refs[3 * nt + 1 : 4 * nt + 1]
            token = refs[-1]
            x, y, c, me = _position()
            for t in range(nt):
                for k, dev in enumerate(self._targets(x, y, c)):
                    _remote(src_refs[t], land_refs[t].at[me], send_sems[t].at[k], recv_sems[t].at[k], dev).start()
            token[...] = jnp.zeros_like(token)

        outs = pl.pallas_call(
            body,
            name=name + "_start",
            in_specs=[HBM] * (2 * nt) + [ANY],
            out_specs=[SEM] * (2 * nt) + [HBM] * (2 * nt) + [VMEM],
            out_shape=[pltpu.SemaphoreType.DMA((1 + N_CHIP_PEERS,))] * (2 * nt)
            + [pltpu.HBM(a.shape, a.dtype) for a in (*shards, *lands)]
            + [jax.ShapeDtypeStruct((SUBLANES, LANES), F32)],
            input_output_aliases={i: 2 * nt + i for i in range(2 * nt)},
            compiler_params=pltpu.CompilerParams(has_side_effects=EFFECT),
        )(*[pltpu.with_memory_space_constraint(a, pltpu.HBM) for a in (*shards, *lands)], after)
        self.send_sems, self.recv_sems = list(outs[:nt]), list(outs[nt : 2 * nt])
        self.srcs, self.lands = list(outs[2 * nt : 3 * nt]), list(outs[3 * nt : 4 * nt])
        self.token = outs[-1]

    @staticmethod
    def _chips(x, y):
        return [(1 - x, y), (x, 1 - y), (1 - x, 1 - y)]

    @classmethod
    def _targets(cls, x, y, c):
        return [(x, y, 1 - c)] + [(*chip, c) for chip in cls._chips(x, y)]

    def zero(self):
        return self.token[0, 0]

    def wait(self, which, after, *, name):
        n = len(which)

        def slot(px, py, pc):
            return 4 * px + 2 * py + pc

        def pass_body(*refs):
            land_refs, recv_sems = refs[:n], refs[n : 2 * n]
            fwd_send, fwd_recv = refs[3 * n + 1 : 4 * n + 1], refs[4 * n + 1 : 5 * n + 1]
            x, y, c, _ = _position()
            for t in range(n):
                for j, chip in enumerate(self._chips(x, y)):
                    blk = land_refs[t].at[slot(*chip, c)]
                    _remote(blk, blk, fwd_send[t].at[j], recv_sems[t].at[1 + j], (*chip, c)).wait_recv()
                    _remote(blk, blk, fwd_send[t].at[j], fwd_recv[t].at[j], (x, y, 1 - c)).start()

        lands = [self.lands[t] for t in which]
        outs = pl.pallas_call(
            pass_body,
            name=name + "_pass",
            in_specs=[HBM] * n + [SEM] * n + [ANY],
            out_specs=[HBM] * n + [SEM] * (2 * n),
            out_shape=[pltpu.HBM(a.shape, a.dtype) for a in lands] + [pltpu.SemaphoreType.DMA((N_CHIP_PEERS,))] * (2 * n),
            input_output_aliases={i: i for i in range(n)},
            compiler_params=pltpu.CompilerParams(has_side_effects=EFFECT),
        )(*lands, *[self.recv_sems[t] for t in which], after)
        lands, fwd_send, fwd_recv = outs[:n], outs[n : 2 * n], outs[2 * n :]

        def wait_body(*refs):
            src_refs, land_refs = refs[:n], refs[n : 2 * n]
            send_sems, recv_sems = refs[2 * n : 3 * n], refs[3 * n : 4 * n]
            fwd_send, fwd_recv = refs[4 * n : 5 * n], refs[5 * n : 6 * n]
            x, y, c, me = _position()
            sibling = (x, y, 1 - c)
            for t in range(n):
                for k, dev in enumerate(self._targets(x, y, c)):
                    _remote(src_refs[t], land_refs[t].at[me], send_sems[t].at[k], recv_sems[t].at[k], dev).wait_send()
                blk = land_refs[t].at[slot(x, y, 1 - c)]
                _remote(blk, blk, send_sems[t].at[0], recv_sems[t].at[0], sibling).wait_recv()
                for j, chip in enumerate(self._chips(x, y)):
                    sent = land_refs[t].at[slot(*chip, c)]
                    _remote(sent, sent, fwd_send[t].at[j], fwd_recv[t].at[j], sibling).wait_send()
                    got = land_refs[t].at[slot(*chip, 1 - c)]
                    _remote(got, got, fwd_send[t].at[j], fwd_recv[t].at[j], sibling).wait_recv()

        srcs = [self.srcs[t] for t in which]
        outs = pl.pallas_call(
            wait_body,
            name=name,
            in_specs=[HBM] * (2 * n) + [SEM] * (4 * n),
            out_specs=[HBM] * (2 * n),
            out_shape=[pltpu.HBM(a.shape, a.dtype) for a in (*srcs, *lands)],
            input_output_aliases={i: i for i in range(2 * n)},
            compiler_params=pltpu.CompilerParams(has_side_effects=EFFECT),
        )(*srcs, *lands, *[self.send_sems[t] for t in which], *[self.recv_sems[t] for t in which], *fwd_send, *fwd_recv)
        return outs[n:]


class Scatter:
    def __init__(self, srcs, lands, after, *, name):
        self.name = name
        nt = self.nt = len(srcs)
        peers = N_DEV - 1

        def body(*refs):
            src_refs, land_refs = refs[:nt], refs[nt : 2 * nt]
            send_sems, recv_sems = refs[2 * nt + 1 : 3 * nt + 1], refs[3 * nt + 1 : 4 * nt + 1]
            token = refs[-1]
            x, y, c, me = _position()
            for t in range(nt):
                for k in range(1, N_DEV):
                    dev, pid = _peer(k, x, y, c)
                    src = src_refs[t].at[pid]
                    _remote(src, land_refs[t].at[me], send_sems[t].at[k - 1], recv_sems[t].at[k - 1], dev).start()
            token[...] = jnp.zeros_like(token)

        outs = pl.pallas_call(
            body,
            name=name + "_start",
            in_specs=[HBM] * (2 * nt) + [ANY],
            out_specs=[SEM] * (2 * nt) + [HBM] * (2 * nt) + [VMEM],
            out_shape=[pltpu.SemaphoreType.DMA((peers,))] * (2 * nt)
            + [pltpu.HBM(a.shape, a.dtype) for a in (*srcs, *lands)]
            + [jax.ShapeDtypeStruct((SUBLANES, LANES), F32)],
            input_output_aliases={i: 2 * nt + i for i in range(2 * nt)},
            compiler_params=pltpu.CompilerParams(has_side_effects=EFFECT),
        )(*[pltpu.with_memory_space_constraint(a, pltpu.HBM) for a in (*srcs, *lands)], after)
        self.send_sems, self.recv_sems = outs[:nt], outs[nt : 2 * nt]
        self.srcs, self.lands = outs[2 * nt : 3 * nt], outs[3 * nt : 4 * nt]
        self.token = outs[-1]

    def zero(self):
        return self.token[0, 0]

    def wait(self, which, after, *, name):
        n = len(which)

        def body(*refs):
            src_refs, land_refs = refs[:n], refs[n : 2 * n]
            send_sems, recv_sems = refs[2 * n : 3 * n], refs[3 * n : 4 * n]
            x, y, c, _ = _position()
            for t in range(n):
                for k in range(1, N_DEV):
                    dev, pid = _peer(k, x, y, c)
                    src = src_refs[t].at[pid]
                    cp = _remote(src, land_refs[t].at[pid], send_sems[t].at[k - 1], recv_sems[t].at[k - 1], dev)
                    cp.wait_send()
                    cp.wait_recv()

        srcs = [self.srcs[t] for t in which]
        lands = [self.lands[t] for t in which]
        outs = pl.pallas_call(
            body,
            name=name,
            in_specs=[HBM] * (2 * n) + [SEM] * (2 * n) + [ANY],
            out_specs=[HBM] * (2 * n),
            out_shape=[pltpu.HBM(a.shape, a.dtype) for a in (*srcs, *lands)],
            input_output_aliases={i: i for i in range(2 * n)},
            compiler_params=pltpu.CompilerParams(has_side_effects=EFFECT),
        )(*srcs, *lands, *[self.send_sems[t] for t in which], *[self.recv_sems[t] for t in which], after)
        return outs[n:]


def _adam_update(g, w, m, v):
    m2 = ADAM_B1 * m + (1.0 - ADAM_B1) * g
    v2 = ADAM_B2 * v + (1.0 - ADAM_B2) * jnp.square(g)
    m_hat = m2 / (1.0 - ADAM_B1**ADAM_STEP)
    v_hat = v2 / (1.0 - ADAM_B2**ADAM_STEP)
    delta = -ADAM_LR * (m_hat / (jnp.sqrt(v_hat) + ADAM_EPS) + ADAM_WD * w)
    return delta, m2, v2


def adamw(gparts, w, m, v, *, name):
    nl, r, c = w.shape
    p = gparts[0].shape[0]
    tr = r if r <= 256 else (256 if c <= D_MODEL else 128)
    ni = r // tr

    def body(*refs):
        g_refs = refs[:nl]
        w_ref, m_ref, v_ref, go_ref, d_ref, mo_ref, vo_ref = refs[nl:]
        for layer in range(nl):

            @pl.when(pl.program_id(0) == layer)
            def _(g_ref=g_refs[layer]):
                g = g_ref[0].astype(F32)
                for i in range(1, p):
                    g = g + g_ref[i].astype(F32)
                delta, m2, v2 = _adam_update(g, w_ref[...], m_ref[...], v_ref[...])
                go_ref[...] = g
                d_ref[...] = delta
                mo_ref[...] = m2
                vo_ref[...] = v2

    def parts_spec(layer):
        def index(l, i):
            return (0, jnp.where(l == layer, i, jnp.where(l < layer, 0, ni - 1)), 0)

        return pl.BlockSpec((p, tr, c), index)

    blk = pl.BlockSpec((None, tr, c), lambda l, i: (l, i, 0))
    return pl.pallas_call(
        body,
        name=name,
        grid=(nl, ni),
        in_specs=[*[parts_spec(layer) for layer in range(nl)], blk, blk, blk],
        out_specs=[blk] * 4,
        out_shape=[jax.ShapeDtypeStruct((nl, r, c), F32)] * 4,
        compiler_params=_params(("arbitrary", "arbitrary")),
    )(*gparts, w, m, v)


def ada_grad_adamw(cg, dmrecv, w4, m4, v4, *, name):
    nt, k, ncol = w4.shape

    def body(cg_ref, dm_ref, w_ref, m_ref, v_ref, go_ref, d_ref, mo_ref, vo_ref, gb_ref):
        sc = _silu_bf16(cg_ref[...].reshape(N_DEV * SUBLANES, k))
        dm = dm_ref[...].reshape(N_DEV * SUBLANES, ncol)
        g = lax.dot_general(sc, dm.astype(BF16), (((0,), (0,)), ((), ())), preferred_element_type=F32)
        delta, m2, v2 = _adam_update(g, w_ref[...], m_ref[...], v_ref[...])
        go_ref[...] = g
        d_ref[...] = delta
        mo_ref[...] = m2
        vo_ref[...] = v2
        gb_ref[...] = jnp.broadcast_to(jnp.sum(dm, axis=0, keepdims=True), (SUBLANES, ncol))

    wblk = pl.BlockSpec((None, k, ncol), lambda t: (t, 0, 0))
    return pl.pallas_call(
        body,
        name=name,
        grid=(nt,),
        in_specs=[
            pl.BlockSpec((N_DEV, SUBLANES, k), lambda t: (0, 0, 0)),
            pl.BlockSpec((N_DEV, None, SUBLANES, ncol), lambda t: (0, t, 0, 0)),
            wblk,
            wblk,
            wblk,
        ],
        out_specs=[wblk] * 4 + [pl.BlockSpec((None, SUBLANES, ncol), lambda t: (t, 0, 0))],
        out_shape=[jax.ShapeDtypeStruct((nt, k, ncol), F32)] * 4 + [jax.ShapeDtypeStruct((nt, SUBLANES, ncol), F32)],
        compiler_params=_params(("parallel",)),
    )(cg, dmrecv, w4, m4, v4)


def kernel(x, c, ada_w, ada_b, ln_g, ln_b, a_w_in, a_b_in, a_vn_g, a_vn_b, a_w_s, a_b_s, a_w_out, b_w_qkv, b_w_out, mlp_w_up, mlp_w_down, loss_target, m_ada_w, m_ada_b, m_ln_g, m_ln_b, m_a_w_in, m_a_b_in, m_a_vn_g, m_a_vn_b, m_a_w_s, m_a_b_s, m_a_w_out, m_b_w_qkv, m_b_w_out, m_mlp_w_up, m_mlp_w_down, v_ada_w, v_ada_b, v_ln_g, v_ln_b, v_a_w_in, v_a_b_in, v_a_vn_g, v_a_vn_b, v_a_w_s, v_a_b_s, v_a_w_out, v_b_w_qkv, v_b_w_out, v_mlp_w_up, v_mlp_w_down):
    x0 = x[0]
    target = loss_target[0]
    me = 4 * lax.axis_index("x") + 2 * lax.axis_index("y") + lax.axis_index("c")

    ada_w4 = ada_w.reshape(N_SUB, D_MODEL, -1)
    ada_b4 = ada_b.reshape(N_SUB, -1)
    ln8 = jnp.concatenate([ln_g.reshape(N_SUB, -1), ln_b.reshape(N_SUB, -1)], axis=0)
    c8 = jnp.broadcast_to(c, (SUBLANES, D_MODEL))
    cg, lng, mrecv = ada_exchange(c8, ada_w4, ada_b4, ln8)

    W_IN, W_AOUT, W_UP0, W_DN0, W_QKV, W_BOUT, W_UP1, W_DN1 = range(8)
    shards = [
        a_w_in[0].astype(BF16),
        a_w_out[0].astype(BF16),
        mlp_w_up[0].astype(BF16),
        mlp_w_down[0].astype(BF16),
        b_w_qkv[0].astype(BF16),
        b_w_out[0].astype(BF16),
        mlp_w_up[1].astype(BF16),
        mlp_w_down[1].astype(BF16),
    ]
    gather = Gather(shards, [_own_slot(me, s) for s in shards], mrecv, name="gather")

    modv = mrecv[:, :, 0, :].transpose(1, 0, 2).reshape(N_SUB, 3 * D_MODEL) + gather.zero()
    shift = [modv[t : t + 1, :D_MODEL] for t in range(N_SUB)]
    scale = [modv[t : t + 1, D_MODEL : 2 * D_MODEL] for t in range(N_SUB)]
    gate1 = [1.0 + modv[t : t + 1, 2 * D_MODEL :] for t in range(N_SUB)]
    lng_full = [lng[:, t, :].reshape(1, D_MODEL) for t in range(N_SUB)]
    lnb_full = [lng[:, N_SUB + t, :].reshape(1, D_MODEL) for t in range(N_SUB)]

    ident = lambda acc: (acc,)
    def relu2(a):
        r = jnp.maximum(a, jnp.zeros_like(a))
        return r * r
    vn_g, vn_b, w_s = a_vn_g, a_vn_b, a_w_s[0]
    bias_full = jnp.repeat(a_b_s[0].T, A_GROUP_DIM, axis=1)
    w_up3, w_dn3 = [None, None], [None, None]

    def mlp_forward(i, h, after, x_in=None, t_next=None):
        up, dn = gather.wait([W_UP0, W_DN0] if i == 0 else [W_UP1, W_DN1], after, name=f"gather_wait_mlp{i}")
        w_up3[i], w_dn3[i] = up, dn.reshape(1, D_FF, D_MODEL)
        (a,) = mm_nn(h, w_up3[i], name=f"mlp{i}_up", tm=2048, ps=2, tn=512, tk=D_MODEL, epilogue=ident, outs=(BF16,))
        if t_next is None:
            (y,) = mm_nn(
                a, w_dn3[i], name=f"mlp{i}_down", tm=1024, ps=1, tn=512, tk=D_FF, prologue=relu2, epilogue=ident, outs=(BF16,)
            )
            return a, y
        y, xn, hn = mm_nn(
            a, w_dn3[i], name=f"mlp{i}_down", tm=512, ps=1, tn=D_MODEL, tk=D_FF, prologue=relu2,
            epilogue=residual_ln_epilogue, extras=residual_extras(x_in, t_next - 1), outs=(BF16, F32, BF16),
        )
        return a, y, xn, hn

    def residual_extras(x_in, t):
        rows = (gate1[t], lng_full[t], lnb_full[t], scale[t + 1], shift[t + 1])
        return [(x_in, "full")] + [(r, "row") for r in rows]

    h0 = modulate(x0, scale[0], shift[0], name="modulate0")
    w_in3, w_aout3 = gather.wait([W_IN, W_AOUT], h0, name="gather_wait_a")
    w_aout3 = w_aout3.reshape(1, D_MODEL, D_MODEL)
    (a_pre,) = mm_nn(
        h0, w_in3, name="a_in", tm=2048, ps=4, tn=256, tk=D_MODEL, epilogue=lambda acc, b: (acc + b,),
        extras=[(a_b_in, "row")], outs=(BF16,),
    )
    p_gate = gate_fwd(a_pre, vn_g, vn_b, w_s, bias_full, name="gate_fwd")
    y0, x1, h1 = mm_nn(
        p_gate, w_aout3, name="a_out", tm=1024, ps=1, tn=D_MODEL, tk=D_MODEL, epilogue=residual_ln_epilogue,
        extras=residual_extras(x0, 0), outs=(BF16, F32, BF16),
    )
    a1, y1, x2, h2 = mlp_forward(0, h1, y0, x_in=x1, t_next=2)
    w_qkv_shards, w_bout3 = gather.wait([W_QKV, W_BOUT], y1, name="gather_wait_b")
    w_bout3 = w_bout3.reshape(1, D_MODEL, D_MODEL)
    w_qkv3 = shards_to_columns(w_qkv_shards, name="w_qkv_columns")[None]
    pat_tiles = 3
    dil = [d for _, d in B_PATTERNS]
    qkv_p, pat_o, pat_lse = [], [], []
    for g in range(N_PAT):
        (qkv_g,) = mm_nn(
            h2, w_qkv3, name=f"b_qkv{g}", tm=2048, ps=1, tn=D_MODEL, tk=D_MODEL, epilogue=ident, outs=(BF16,),
            b_tile0=pat_tiles * g, b_tiles=pat_tiles, out_streams=dil[g],
        )
        o_g, lse_g = attn_fwd(qkv_g, g, name=f"attn_fwd{g}")
        if g > 0:
            o_g, lse_g = permute_rows([o_g, lse_g], dil[g], inverse=True, name=f"unperm_o{g}")
        qkv_p.append(qkv_g)
        pat_o.append(o_g)
        pat_lse.append(lse_g)
    o_b, o_f, lse = attn_combine(pat_o, pat_lse, name="attn_combine")
    y2, x3, h3 = mm_nn(
        o_b, w_bout3, name="b_out", tm=1024, ps=1, tn=D_MODEL, tk=D_MODEL, epilogue=residual_ln_epilogue,
        extras=residual_extras(x2, 2), outs=(BF16, F32, BF16),
    )
    a3, y3 = mlp_forward(1, h3, y2)
    dz3, dy3, st3, loss_local = residual_ln_loss_bwd(
        x3, y3, gate1[3], lng_full[3], lnb_full[3], target, name="res_ln3_loss_bwd"
    )

    def scatter(parts, after, name):
        parts = [p.reshape(N_DEV, -1, p.shape[-1]) for p in parts]
        lands = [_own_slot(me, lax.dynamic_index_in_dim(p, me, 0, keepdims=False)) for p in parts]
        return Scatter(parts, lands, after, name=name)

    xs_in, ys = [x0, x1, x2, x3], [y0, y1, y2, y3]

    def residual_bwd_extras(dz_later, t, zero):
        return [
            (dz_later, "full"), (scale[t + 1] + zero, "row"), (xs_in[t], "full"), (ys[t], "full"),
            (gate1[t], "row"), (lng_full[t], "row"), (lnb_full[t], "row"),
        ]

    def mlp_backward(i, h, a, dy, dz_later, t):
        (da,) = mm_nt(
            dy,
            w_dn3[i],
            name=f"mlp{i}_da",
            tm=2048,
            tko=1024,
            ps=1,
            tc=D_MODEL,
            epilogue=lambda acc, act: (acc * (2.0 * jnp.maximum(act.astype(F32), 0.0)),),
            extras=[(a, "full")],
            outs=(BF16,),
        )
        dw_dn = mm_tn(
            a, dy, name=f"mlp{i}_dw_down", p=1, tk=1024, ps=1, tn=D_MODEL, tmc=2048, prologue=relu2, out_dtype=BF16
        )
        dw_up = mm_tn(h, da, name=f"mlp{i}_dw_up", p=N_DEV, tk=1024, ps=2, tn=512, tmc=2048, out_dtype=BF16)
        rs = scatter([dw_up, dw_dn], da, f"scatter_mlp{i}")
        dz, dy_before, st = mm_nt(
            da, w_up3[i], name=f"mlp{i}_dh", tm=512, tko=D_MODEL, ps=N_DEV, tc=512, epilogue=residual_bwd_epilogue,
            extras=residual_bwd_extras(dz_later, t, rs.zero()), outs=(F32, BF16), stats=True,
        )
        return rs, dz, dy_before, st

    rs_mlp1, dz2, dy2, st2 = mlp_backward(1, h3, a3, dy3, dz3, 2)
    (d_o,) = mm_nt(dy2, w_bout3, name="b_do", tm=2048, tko=1024, ps=1, tc=D_MODEL, epilogue=ident, outs=(F32,))
    do_b, delta = attn_delta(d_o, o_f, name="attn_delta")
    dh2, dw_pat = None, []
    for g in range(N_PAT):
        do_g, lse_g, delta_g = do_b, lse, delta
        if g > 0:
            do_g, lse_g, delta_g = permute_rows([do_b, lse, delta], dil[g], inverse=False, name=f"perm_do{g}")
        dqkv_g = attn_bwd(qkv_p[g], do_g, lse_g, delta_g, g, name=f"attn_bwd{g}")
        dw_pat.append(
            mm_tn(
                h2, dqkv_g, name=f"b_dw_qkv{g}", p=1, tk=1024, ps=1, tn=D_MODEL, tmc=2048, out_dtype=BF16,
                g_streams=dil[g],
            )[0]
        )
        if g == N_PAT - 1:
            break
        (dh2,) = mm_nt(
            dqkv_g, w_qkv3, name=f"b_dh{g}", tm=2048, tko=512, ps=1, tc=pat_tiles * D_MODEL, outs=(F32,),
            b_tile0=g, g_streams=dil[g],
            epilogue=ident if g == 0 else (lambda acc, prev, d=dil[g]: (_from_streams(acc, d) + prev,)),
            extras=[] if g == 0 else [(dh2, "full")],
        )
    dw_bout = mm_tn(o_b, dy2, name="b_dw_out", p=1, tk=1024, ps=1, tn=D_MODEL, tmc=2048, out_dtype=BF16)
    dw_qkv = columns_to_shards(dw_pat, name="dw_qkv_shards")
    rs_b = scatter([dw_qkv, dw_bout], dqkv_g, "scatter_b")
    dz1, dy1, st1 = mm_nt(
        dqkv_g, w_qkv3, name=f"b_dh{N_PAT - 1}", tm=512, tko=D_MODEL, ps=1, tc=pat_tiles * D_MODEL, outs=(F32, BF16),
        b_tile0=N_PAT - 1, g_streams=dil[-1], stats=True,
        epilogue=lambda acc, prev, *rest: residual_bwd_epilogue(_from_streams(acc, dil[-1]) + prev, *rest),
        extras=[(dh2, "full")] + residual_bwd_extras(dz2, 1, rs_b.zero()),
    )
    rs_mlp0, dz0, dy0, st0 = mlp_backward(0, h1, a1, dy1, dz1, 0)
    (dp_gate,) = mm_nt(dy0, w_aout3, name="a_dp", tm=2048, tko=1024, ps=1, tc=D_MODEL, epilogue=ident, outs=(F32,))
    dw_aout = mm_tn(p_gate, dy0, name="a_dw_out", p=1, tk=1024, ps=1, tn=D_MODEL, tmc=2048, out_dtype=BF16)
    rs_aout = scatter([dw_aout], dp_gate, "scatter_a_out")
    da0, d_ws, d_bs, gate_rows = gate_bwd(a_pre, dp_gate, vn_g + rs_aout.zero(), vn_b, w_s, bias_full, name="gate_bwd")
    dw_in = mm_tn(h0, da0, name="a_dw_in", p=N_DEV, tk=1024, ps=4, tn=256, tmc=2048, out_dtype=BF16)
    rs_in = scatter([dw_in], d_ws, "scatter_a_in")
    grad_x, stf = mm_nt(
        da0, w_in3, name="a_dh", tm=1024, tko=D_MODEL, ps=N_DEV, tc=256, epilogue=input_bwd_epilogue,
        extras=[(dz0, "full"), (scale[0] + rs_in.zero(), "row"), (x0, "full")], outs=(F32,), stats=True,
    )

    stats_after = [stf, st0, st1, st2]
    stats_own = [st0, st1, st2, st3]
    dm = jnp.stack(
        [
            jnp.concatenate(
                [stats_after[t][ST_DSHIFT], stats_after[t][ST_DSCALE], stats_own[t][ST_DGATE]], axis=0
            )
            for t in range(N_SUB)
        ]
    )
    ncol = 3 * D_MODEL // N_DEV
    dmx = jnp.pad(
        dm.reshape(N_SUB, N_DEV, ncol).transpose(1, 0, 2)[:, :, None, :], ((0, 0), (0, 0), (0, SUBLANES - 1), (0, 0))
    )
    small = [
        gate_rows[0],
        gate_rows[1],
        d_ws.reshape(-1),
        d_bs[:, :A_GROUPS].T.reshape(-1),
        *[stats_own[t][ST_DG] for t in range(N_SUB)],
        *[stats_own[t][ST_DB] for t in range(N_SUB)],
        jnp.pad(loss_local.reshape(1), (0, LANES - 1)),
    ]
    n_small = sum(s.size for s in small)
    part_rows = -(-n_small // (N_DEV * LANES * SUBLANES)) * SUBLANES
    flat = jnp.concatenate(small + [jnp.zeros((N_DEV * part_rows * LANES - n_small,), F32)])
    dmrecv, reduced = small_exchange(dmx, flat.reshape(N_DEV, part_rows, LANES))
    reduced = reduced.reshape(-1)
    sizes = [2 * D_MODEL, D_MODEL, D_MODEL, A_GROUPS * CHUNK * CHUNK, A_GROUPS * CHUNK, N_SUB * D_MODEL, N_SUB * D_MODEL]
    offs = [sum(sizes[:i]) for i in range(len(sizes) + 1)]
    g_b_in, g_vn_g, g_vn_b, g_ws, g_bs, g_lng, g_lnb = [reduced[offs[i] : offs[i + 1]] for i in range(len(sizes))]
    loss = reduced[offs[-1]]

    results = {}

    def update(wname, gparts, w, m, v):
        shape = w.shape
        layers = len(gparts) if isinstance(gparts, list) else 1
        w3 = w.reshape(layers, -1, shape[-1])
        parts = [g.reshape(g.shape[0], *w3.shape[1:]) for g in (gparts if layers > 1 else [gparts])]
        outs = adamw(parts, w3, m.reshape(w3.shape), v.reshape(w3.shape), name=f"adamw_{wname}")
        results[wname] = [o.reshape(shape) for o in outs]

    ada_outs = ada_grad_adamw(cg, dmrecv, ada_w4, m_ada_w.reshape(ada_w4.shape), v_ada_w.reshape(ada_w4.shape), name="ada_grad_adamw")
    results["ada_w"] = [o.reshape(ada_w.shape) for o in ada_outs[:4]]
    update("ada_b", ada_outs[4][:, 0, :][None], ada_b, m_ada_b, v_ada_b)
    ln_cols = D_MODEL // N_DEV
    my_ln = lambda gfull: lax.dynamic_slice_in_dim(gfull.reshape(N_SUB, N_DEV, ln_cols), me, 1, axis=1)
    update("ln_g", my_ln(g_lng).reshape(1, N_SUB, ln_cols), ln_g, m_ln_g, v_ln_g)
    update("ln_b", my_ln(g_lnb).reshape(1, N_SUB, ln_cols), ln_b, m_ln_b, v_ln_b)
    update("a_b_in", g_b_in[None], a_b_in, m_a_b_in, v_a_b_in)
    update("a_vn_g", g_vn_g[None], a_vn_g, m_a_vn_g, v_a_vn_g)
    update("a_vn_b", g_vn_b[None], a_vn_b, m_a_vn_b, v_a_vn_b)
    update("a_w_s", g_ws[None], a_w_s, m_a_w_s, v_a_w_s)
    update("a_b_s", g_bs[None], a_b_s, m_a_b_s, v_a_b_s)
    g_up1, g_dn1 = rs_mlp1.wait([0, 1], grad_x, name="scatter_wait_mlp1")
    g_qkv, g_bout = rs_b.wait([0, 1], grad_x, name="scatter_wait_b")
    update("b_w_qkv", g_qkv, b_w_qkv, m_b_w_qkv, v_b_w_qkv)
    update("b_w_out", g_bout, b_w_out, m_b_w_out, v_b_w_out)
    g_up0, g_dn0 = rs_mlp0.wait([0, 1], grad_x, name="scatter_wait_mlp0")
    update("mlp_w_up", [g_up0, g_up1], mlp_w_up, m_mlp_w_up, v_mlp_w_up)
    update("mlp_w_down", [g_dn0, g_dn1], mlp_w_down, m_mlp_w_down, v_mlp_w_down)
    (g_aout,) = rs_aout.wait([0], grad_x, name="scatter_wait_a_out")
    (g_in,) = rs_in.wait([0], grad_x, name="scatter_wait_a_in")
    update("a_w_in", g_in, a_w_in, m_a_w_in, v_a_w_in)
    update("a_w_out", g_aout, a_w_out, m_a_w_out, v_a_w_out)

    order = ["ada_w", "ada_b", "ln_g", "ln_b", "a_w_in", "a_b_in", "a_vn_g", "a_vn_b", "a_w_s", "a_b_s", "a_w_out", "b_w_qkv", "b_w_out", "mlp_w_up", "mlp_w_down"]
    return (loss, grad_x[None], *[results[n][0] for n in order], *[results[n][1] for n in order],
            *[results[n][2] for n in order], *[results[n][3] for n in order])
```

```python
import math

import jax
import jax.numpy as jnp
from jax import lax
from jax.experimental import pallas as pl
from jax.experimental.pallas import tpu as pltpu

F32 = jnp.float32
BF16 = jnp.bfloat16
MESH = pl.DeviceIdType.MESH
ANY = pl.BlockSpec(memory_space=pl.ANY)
VMEM = pl.BlockSpec(memory_space=pltpu.VMEM)

N_DEV = 8
D_MODEL = 1024
SEQ = 4096
DEPTH = 2
CHUNK = 128
A_GROUPS = 16
A_GROUP_DIM = D_MODEL // A_GROUPS
B_HEADS = 16
B_HEAD_DIM = 64
B_PATTERNS = ((128, 1), (512, 4), (2048, 16))
N_PAT = len(B_PATTERNS)
SPAN = 128
D_FF = 4 * D_MODEL
D_QKV = N_PAT * 3 * D_MODEL
ALPHA = (2 * DEPTH) ** 0.25
LN_EPS = 1e-5
NEG = -1e30
ADAM_LR = 0.001
ADAM_B1 = 0.9
ADAM_B2 = 0.999
ADAM_EPS = 1e-08
ADAM_WD = 0.01
ADAM_STEP = 10
GELU_C = math.sqrt(2.0 / math.pi)
GELU_A = 0.044715

VMEM_LIMIT_BYTES = 56 * 1024 * 1024
LANES = 128
SUBLANES = 8
ROW_TILE = 512
N_SUB = 2 * DEPTH


def _params(sem):
    return pltpu.CompilerParams(dimension_semantics=sem, vmem_limit_bytes=VMEM_LIMIT_BYTES)


def _lane(shape):
    return lax.broadcasted_iota(jnp.int32, shape, len(shape) - 1)


def _split_bf16(x):
    hi = x.astype(BF16)
    lo = (x - hi.astype(F32)).astype(BF16)
    return hi, lo


def _group_expand_matrix(groups_padded, width):
    per = width // A_GROUPS
    r = lax.broadcasted_iota(jnp.int32, (groups_padded, width), 0)
    c = lax.broadcasted_iota(jnp.int32, (groups_padded, width), 1)
    return (c // per == r).astype(BF16)


def _group_reduce_matrix(width, groups_padded):
    per = width // A_GROUPS
    r = lax.broadcasted_iota(jnp.int32, (width, groups_padded), 0)
    c = lax.broadcasted_iota(jnp.int32, (width, groups_padded), 1)
    return (r // per == c).astype(BF16)


def _expand_groups(w):
    e = _group_expand_matrix(LANES, D_MODEL)
    hi, lo = _split_bf16(w)
    return jnp.dot(hi, e, preferred_element_type=F32) + jnp.dot(lo, e, preferred_element_type=F32)


def _reduce_groups(x):
    e = _group_reduce_matrix(D_MODEL, LANES)
    hi, lo = _split_bf16(x)
    return jnp.dot(hi, e, preferred_element_type=F32) + jnp.dot(lo, e, preferred_element_type=F32)


def _to_streams(x, d):
    rows, w = x.shape
    return jnp.swapaxes(x.reshape(rows // d, d, w), 0, 1).reshape(rows, w)


def _from_streams(x, d):
    rows, w = x.shape
    return jnp.swapaxes(x.reshape(d, rows // d, w), 0, 1).reshape(rows, w)


def _column_tiles(p, n, ps, tn):
    assert (ps == 1 or tn == n) and p % ps == 0 and n % tn == 0
    q = n // tn
    return (p // ps) * q, q


def _extra_specs(extras, tm, width):
    specs = []
    for _, kind in extras:
        if kind == "row":
            specs.append(pl.BlockSpec((1, width), lambda i, j, c: (0, j)))
        else:
            specs.append(pl.BlockSpec((tm, width), lambda i, j, c: (i, j)))
    return specs


def mm_nn(a, b3, *, name, tm, ps, tn, tk, epilogue, extras=(), outs, prologue=None, b_tile0=0, b_tiles=None, out_streams=1):
    m, k = a.shape
    p, _, n = b3.shape
    nj, q = _column_tiles(p, n, ps, tn)
    nj = nj if b_tiles is None else b_tiles
    nk = k // tk
    width = ps * tn
    d = out_streams
    assert d == 1 or not extras

    def body(a_ref, b_ref, *rest):
        ex = rest[: len(extras)]
        out_refs = rest[len(extras) : len(extras) + len(outs)]
        kk = pl.program_id(2)
        av = a_ref[...] if prologue is None else prologue(a_ref[...])
        if d > 1:
            av = _to_streams(av, d)

        def finish(cs, acc):
            res = epilogue(acc, *[e[:, cs] for e in ex])
            for o_ref, r in zip(out_refs, res, strict=True):
                if d > 1:
                    o_ref[:, :, cs] = r.astype(o_ref.dtype).reshape(d, tm // d, tn)
                else:
                    o_ref[:, cs] = r.astype(o_ref.dtype)

        for s in range(ps):
            cs = slice(s * tn, (s + 1) * tn)
            part = jnp.dot(av, b_ref[s], preferred_element_type=F32)
            if nk == 1:
                finish(cs, part)
                continue
            acc_ref = rest[-1]

            @pl.when(kk == 0)
            def _(part=part, cs=cs):
                acc_ref[:, cs] = part

            @pl.when(kk > 0)
            def _(part=part, cs=cs):
                acc_ref[:, cs] += part

        if nk > 1:

            @pl.when(kk == nk - 1)
            def _():
                for s in range(ps):
                    cs = slice(s * tn, (s + 1) * tn)
                    finish(cs, rest[-1][:, cs])

    if d > 1:
        out_spec = pl.BlockSpec((d, tm // d, width), lambda i, j, kk: (0, i, j))
        out_shape = (d, m // d, nj * width)
    else:
        out_spec = pl.BlockSpec((tm, width), lambda i, j, kk: (i, j))
        out_shape = (m, nj * width)
    res = pl.pallas_call(
        body,
        name=name,
        grid=(m // tm, nj, nk),
        in_specs=[
            pl.BlockSpec((tm, tk), lambda i, j, kk: (i, kk)),
            pl.BlockSpec((ps, tk, tn), lambda i, j, kk: ((j + b_tile0) // q, kk, (j + b_tile0) % q)),
            *_extra_specs(extras, tm, width),
        ],
        out_specs=[out_spec for _ in outs],
        out_shape=[jax.ShapeDtypeStruct(out_shape, dt) for dt in outs],
        scratch_shapes=[pltpu.VMEM((tm, width), F32)] if nk > 1 else [],
        compiler_params=_params(("parallel", "parallel", "arbitrary")),
    )(a, b3, *[arr for arr, _ in extras])
    return [r.reshape(m, nj * width) for r in res]


def mm_nt(g, b3, *, name, tm, tko, ps, tc, epilogue, extras=(), outs, b_tile0=0, g_streams=1, stats=False):
    m, width = g.shape
    p, k, n = b3.shape
    _, q = _column_tiles(p, n, ps, tc)
    nc = width // (ps * tc)
    ds = g_streams
    assert not stats or tko == k

    def body(g_ref, b_ref, *rest):
        ex = rest[: len(extras)]
        out_refs = rest[len(extras) : len(extras) + len(outs)]
        c = pl.program_id(2)
        gv = g_ref[...].reshape(tm, ps * tc) if ds > 1 else g_ref[...]
        part = None
        for s in range(ps):
            d = lax.dot_general(gv[:, s * tc : (s + 1) * tc], b_ref[s], _NT, preferred_element_type=F32)
            part = d if part is None else part + d

        def finish(acc):
            res = epilogue(acc, *[e[...] for e in ex])
            if stats:
                *res, sums = res
                st_ref = rest[len(extras) + len(outs)]

                @pl.when(pl.program_id(0) == 0)
                def _():
                    st_ref[...] = jnp.zeros_like(st_ref)

                for row, val in enumerate(sums):
                    st_ref[row : row + 1, :] += val
            for o_ref, r in zip(out_refs, res, strict=True):
                o_ref[...] = r.astype(o_ref.dtype)

        if nc == 1:
            finish(part)
            return
        acc_ref = rest[-1]

        @pl.when(c == 0)
        def _():
            acc_ref[...] = part

        @pl.when(c > 0)
        def _():
            acc_ref[...] += part

        @pl.when(c == nc - 1)
        def _():
            finish(acc_ref[...])

    return pl.pallas_call(
        body,
        name=name,
        grid=(m // tm, k // tko, nc),
        in_specs=[
            pl.BlockSpec((ds, tm // ds, ps * tc), lambda i, j, c: (0, i, c))
            if ds > 1
            else pl.BlockSpec((tm, ps * tc), lambda i, j, c: (i, c)),
            pl.BlockSpec((ps, tko, tc), lambda i, j, c: ((c + b_tile0) // q, j, (c + b_tile0) % q)),
            *_extra_specs(extras, tm, tko),
        ],
        out_specs=[pl.BlockSpec((tm, tko), lambda i, j, c: (i, j)) for _ in outs]
        + [pl.BlockSpec((SUBLANES, tko), lambda i, j, c: (0, 0))] * stats,
        out_shape=[jax.ShapeDtypeStruct((m, k), dt) for dt in outs]
        + [jax.ShapeDtypeStruct((SUBLANES, k), F32)] * stats,
        scratch_shapes=[pltpu.VMEM((tm, tko), F32)] if nc > 1 else [],
        compiler_params=_params(("arbitrary" if stats else "parallel", "parallel", "arbitrary")),
    )(g.reshape(ds, m // ds, width) if ds > 1 else g, b3, *[arr for arr, _ in extras])


def mm_tn(a, g, *, name, p, tk, ps, tn, tmc, out_dtype, prologue=None, g_streams=1):
    m, k = a.shape
    width = g.shape[1]
    n = width // p
    nj, q = _column_tiles(p, n, ps, tn)
    nc = m // tmc
    ds = g_streams

    def body(a_ref, g_ref, o_ref, acc_ref):
        c = pl.program_id(2)
        av = a_ref[...] if prologue is None else prologue(a_ref[...])
        gv = g_ref[...]
        if ds > 1:
            av, gv = _to_streams(av, ds), gv.reshape(tmc, ps * tn)
        part = lax.dot_general(av, gv, (((0,), (0,)), ((), ())), preferred_element_type=F32)

        @pl.when(c == 0)
        def _():
            acc_ref[...] = part

        @pl.when(c > 0)
        def _():
            acc_ref[...] += part

        @pl.when(c == nc - 1)
        def _():
            for s in range(ps):
                o_ref[s] = acc_ref[:, s * tn : (s + 1) * tn].astype(o_ref.dtype)

    return pl.pallas_call(
        body,
        name=name,
        grid=(k // tk, nj, nc),
        in_specs=[
            pl.BlockSpec((tmc, tk), lambda i, j, c: (c, i)),
            pl.BlockSpec((ds, tmc // ds, ps * tn), lambda i, j, c: (0, c, j))
            if ds > 1
            else pl.BlockSpec((tmc, ps * tn), lambda i, j, c: (c, j)),
        ],
        out_specs=pl.BlockSpec((ps, tk, tn), lambda i, j, c: (j // q, i, j % q)),
        out_shape=jax.ShapeDtypeStruct((p, k, n), out_dtype),
        scratch_shapes=[pltpu.VMEM((tk, ps * tn), F32)],
        compiler_params=_params(("parallel", "parallel", "arbitrary")),
    )(a, g.reshape(ds, m // ds, width) if ds > 1 else g)


def _rows(cols):
    return pl.BlockSpec((ROW_TILE, cols), lambda i: (i, 0))


def _vec(cols, rows=1):
    return pl.BlockSpec((rows, cols), lambda i: (0, 0))


def _layer_norm_hat(z):
    mu = jnp.mean(z, axis=-1, keepdims=True)
    zc = z - mu
    var = jnp.mean(zc * zc, axis=-1, keepdims=True)
    rstd = lax.rsqrt(var + LN_EPS)
    return zc * rstd, rstd


def modulate(x, scale, shift, *, name):
    s, d = x.shape

    def body(x_ref, sc_ref, sh_ref, h_ref):
        h_ref[...] = (x_ref[...] * (1.0 + sc_ref[...]) + sh_ref[...]).astype(BF16)

    return pl.pallas_call(
        body,
        name=name,
        grid=(s // ROW_TILE,),
        in_specs=[_rows(d), _vec(d), _vec(d)],
        out_specs=_rows(d),
        out_shape=jax.ShapeDtypeStruct((s, d), BF16),
        compiler_params=_params(("parallel",)),
    )(x, scale, shift)


def residual_ln_epilogue(acc, x, gate1, g, b, nscale, nshift):
    y = acc.astype(BF16)
    xhat, _ = _layer_norm_hat(ALPHA * x + gate1 * y.astype(F32))
    xn = xhat * g + b
    return y, xn, xn * (1.0 + nscale) + nshift


ST_DSCALE, ST_DSHIFT, ST_DG, ST_DB, ST_DGATE = 0, 1, 2, 3, 4


def _layer_norm_bwd(g_out, xhat, rstd, g):
    dxh = g_out * g
    m1 = jnp.mean(dxh, axis=-1, keepdims=True)
    m2 = jnp.mean(dxh * xhat, axis=-1, keepdims=True)
    return rstd * (dxh - m1 - xhat * m2)


def _column_sums(vals):
    return [jnp.sum(v, axis=0, keepdims=True) for v in vals]


def residual_bwd_epilogue(dh, dzl, scl, x, y, gate1, g, b):
    yf = y.astype(F32)
    xhat, rstd = _layer_norm_hat(ALPHA * x + gate1 * yf)
    g_out = ALPHA * dzl + dh * (1.0 + scl)
    dz = _layer_norm_bwd(g_out, xhat, rstd, g)
    return dz, dz * gate1, _column_sums([dh * (xhat * g + b), dh, g_out * xhat, g_out, dz * yf])


def input_bwd_epilogue(dh, dzl, scl, x):
    return ALPHA * dzl + dh * (1.0 + scl), _column_sums([dh * x, dh])


def residual_ln_loss_bwd(x, y, gate1, g, b, target, *, name):
    s, d = x.shape

    def body(x_ref, y_ref, gt_ref, g_ref, b_ref, t_ref, dz_ref, dy_ref, st_ref, loss_ref):
        y = y_ref[...].astype(F32)
        gate1 = gt_ref[...]
        xhat, rstd = _layer_norm_hat(ALPHA * x_ref[...] + gate1 * y)
        err = xhat * g_ref[...] + b_ref[...] - t_ref[...]
        g_out = err * (1.0 / d)
        dz = _layer_norm_bwd(g_out, xhat, rstd, g_ref[...])
        dz_ref[...] = dz
        dy_ref[...] = (dz * gate1).astype(BF16)
        part = jnp.sum(jnp.sum(err * err, axis=1, keepdims=True), axis=0, keepdims=True) * (0.5 / d)

        @pl.when(pl.program_id(0) == 0)
        def _():
            st_ref[...] = jnp.zeros_like(st_ref)
            loss_ref[...] = jnp.zeros_like(loss_ref)

        loss_ref[...] += part
        for row, val in zip((ST_DG, ST_DB, ST_DGATE), _column_sums([g_out * xhat, g_out, dz * y]), strict=True):
            st_ref[row : row + 1, :] += val

    return pl.pallas_call(
        body,
        name=name,
        grid=(s // ROW_TILE,),
        in_specs=[_rows(d), _rows(d), _vec(d), _vec(d), _vec(d), _rows(d)],
        out_specs=[_rows(d), _rows(d), _vec(d, SUBLANES), pl.BlockSpec((1, 1), lambda i: (0, 0))],
        out_shape=[
            jax.ShapeDtypeStruct((s, d), F32),
            jax.ShapeDtypeStruct((s, d), BF16),
            jax.ShapeDtypeStruct((SUBLANES, d), F32),
            jax.ShapeDtypeStruct((1, 1), F32),
        ],
        compiler_params=_params(("arbitrary",)),
    )(x, y, gate1, g, b, target)


def residual_ln_bwd(*, name, later=None, dxo=None, x_out=None, this=None):
    lead = later[0] if later is not None else dxo
    s, d = lead.shape
    has_later, has_ln = later is not None, this is not None

    def body(*refs):
        refs = list(refs)
        if has_later:
            dh_ref, dzl_ref, scl_ref = refs[:3]
            refs = refs[3:]
        else:
            dxo_ref = refs.pop(0)
        if has_ln:
            x_ref, y_ref, gt_ref, g_ref, b_ref = refs[:5]
            refs = refs[5:]
            dz_ref, dy_ref, st_ref = refs
        else:
            xo_ref, dx_ref, st_ref = refs

        @pl.when(pl.program_id(0) == 0)
        def _():
            st_ref[...] = jnp.zeros_like(st_ref)

        def acc(row, val):
            st_ref[row : row + 1, :] += jnp.sum(val, axis=0, keepdims=True)

        if has_ln:
            y = y_ref[...].astype(F32)
            gate1 = gt_ref[...]
            xhat, rstd = _layer_norm_hat(ALPHA * x_ref[...] + gate1 * y)
            x_out_v = xhat * g_ref[...] + b_ref[...]
        else:
            x_out_v = xo_ref[...]
        if has_later:
            dh = dh_ref[...]
            g_out = ALPHA * dzl_ref[...] + dh * (1.0 + scl_ref[...])
            acc(ST_DSCALE, dh * x_out_v)
            acc(ST_DSHIFT, dh)
        else:
            g_out = dxo_ref[...]
        if not has_ln:
            dx_ref[...] = g_out
            return
        acc(ST_DG, g_out * xhat)
        acc(ST_DB, g_out)
        dxh = g_out * g_ref[...]
        m1 = jnp.mean(dxh, axis=-1, keepdims=True)
        m2 = jnp.mean(dxh * xhat, axis=-1, keepdims=True)
        dz = rstd * (dxh - m1 - xhat * m2)
        acc(ST_DGATE, dz * y)
        dz_ref[...] = dz
        dy_ref[...] = (dz * gate1).astype(BF16)

    ins, specs = [], []
    if has_later:
        ins += list(later)
        specs += [_rows(d), _rows(d), _vec(d)]
    else:
        ins += [dxo]
        specs += [_rows(d)]
    if not has_ln:
        ins += [x_out]
        specs += [_rows(d)]
    if has_ln:
        ins += list(this)
        specs += [_rows(d), _rows(d), _vec(d), _vec(d), _vec(d)]
        out_specs = [_rows(d), _rows(d), _vec(d, SUBLANES)]
        out_shape = [
            jax.ShapeDtypeStruct((s, d), F32),
            jax.ShapeDtypeStruct((s, d), BF16),
            jax.ShapeDtypeStruct((SUBLANES, d), F32),
        ]
    else:
        out_specs = [_rows(d), _vec(d, SUBLANES)]
        out_shape = [jax.ShapeDtypeStruct((s, d), F32), jax.ShapeDtypeStruct((SUBLANES, d), F32)]
    return pl.pallas_call(
        body,
        name=name,
        grid=(s // ROW_TILE,),
        in_specs=specs,
        out_specs=out_specs,
        out_shape=out_shape,
        compiler_params=_params(("arbitrary",)),
    )(*ins)


GATE_CHUNKS = 4


def _gelu(x, with_grad=False):
    x2 = x * x
    t = jnp.tanh(GELU_C * (x + GELU_A * x2 * x))
    half = 0.5 * (1.0 + t)
    y = x * half
    if not with_grad:
        return y
    return y, half + 0.5 * x * (1.0 - t * t) * (GELU_C * (1.0 + 3.0 * GELU_A * x2))


def _causal_weights(w_ref, transpose):
    t = lax.broadcasted_iota(jnp.int32, (CHUNK, CHUNK), 0)
    s = lax.broadcasted_iota(jnp.int32, (CHUNK, CHUNK), 1)
    out = []
    for g in range(A_GROUPS):
        w = jnp.where(t >= s, w_ref[g], 0.0)
        out.append((w.T if transpose else w).astype(BF16))
    return out


def _spatial(ws, vn, lo_mask):
    rows = vn.shape[0]
    out_rows = []
    for r in range(rows // CHUNK):
        cols = []
        for j in range(A_GROUPS // 2):
            blk = vn[r * CHUNK : (r + 1) * CHUNK, j * LANES : (j + 1) * LANES]
            za = jnp.dot(ws[2 * j], blk, preferred_element_type=F32)
            zb = jnp.dot(ws[2 * j + 1], blk, preferred_element_type=F32)
            cols.append(jnp.where(lo_mask, za, zb))
        out_rows.append(jnp.concatenate(cols, axis=1))
    return jnp.concatenate(out_rows, axis=0)


def _gate_forward(a, vg, vb, ws, bias, lo_mask, with_grad=False):
    u = _gelu(a[:, :D_MODEL], with_grad)
    v = _gelu(a[:, D_MODEL:], with_grad)
    gu, gv = None, None
    if with_grad:
        (u, gu), (v, gv) = u, v
    vhat, rstd = _layer_norm_hat(v)
    vn = (vhat * vg + vb).astype(BF16)
    z = _spatial(ws, vn, lo_mask) + jnp.concatenate([bias] * (a.shape[0] // CHUNK), axis=0)
    return u, vhat, rstd, vn, z, gu, gv


def gate_fwd(a_pre, vn_g, vn_b, w_s, bias_full, *, name):
    s = a_pre.shape[0]
    tr = GATE_CHUNKS * CHUNK

    def body(a_ref, vg_ref, vb_ref, w_ref, bias_ref, p_ref):
        lo_mask = _lane((CHUNK, LANES)) < A_GROUP_DIM
        ws = _causal_weights(w_ref, transpose=False)
        u, _, _, _, z, _, _ = _gate_forward(a_ref[...].astype(F32), vg_ref[...], vb_ref[...], ws, bias_ref[...], lo_mask)
        p_ref[...] = (u * z).astype(BF16)

    return pl.pallas_call(
        body,
        name=name,
        grid=(s // tr,),
        in_specs=[
            pl.BlockSpec((tr, 2 * D_MODEL), lambda i: (i, 0)),
            _vec(D_MODEL),
            _vec(D_MODEL),
            pl.BlockSpec((A_GROUPS, CHUNK, CHUNK), lambda i: (0, 0, 0)),
            _vec(D_MODEL, CHUNK),
        ],
        out_specs=pl.BlockSpec((tr, D_MODEL), lambda i: (i, 0)),
        out_shape=jax.ShapeDtypeStruct((s, D_MODEL), BF16),
        compiler_params=_params(("parallel",)),
    )(a_pre, vn_g, vn_b, w_s, bias_full)


def gate_bwd(a_pre, dp, vn_g, vn_b, w_s, bias_full, *, name):
    s = a_pre.shape[0]
    tr = GATE_CHUNKS * CHUNK
    nsteps = s // tr

    def body(a_ref, dp_ref, vg_ref, vb_ref, w_ref, bias_ref, da_ref, dw_ref, dbs_ref, rows_ref, dbias_acc):
        step = pl.program_id(0)
        lo_mask = _lane((CHUNK, LANES)) < A_GROUP_DIM

        @pl.when(step == 0)
        def _():
            dw_ref[...] = jnp.zeros_like(dw_ref)
            rows_ref[...] = jnp.zeros_like(rows_ref)
            dbias_acc[...] = jnp.zeros_like(dbias_acc)

        a = a_ref[...].astype(F32)
        vg = vg_ref[...]
        ws = _causal_weights(w_ref, transpose=False)
        wts = _causal_weights(w_ref, transpose=True)
        u, vhat, rstd, vn, z, gelu_du, gelu_dv = _gate_forward(a, vg, vb_ref[...], ws, bias_ref[...], lo_mask, True)
        dp = dp_ref[...]
        du = dp * z
        dzz = dp * u
        dzz_b = dzz.astype(BF16)
        dvn = _spatial(wts, dzz_b, lo_mask)
        dbias = None
        for r in range(GATE_CHUNKS):
            rs = slice(r * CHUNK, (r + 1) * CHUNK)
            dbias = dzz[rs] if dbias is None else dbias + dzz[rs]
            for j in range(A_GROUPS // 2):
                cs = slice(j * LANES, (j + 1) * LANES)
                dblk = dzz[rs, cs]
                vblk = vn[rs, cs]
                for half in range(2):
                    keep = lo_mask if half == 0 else jnp.logical_not(lo_mask)
                    dm = jnp.where(keep, dblk, 0.0).astype(BF16)
                    dw_ref[2 * j + half] += lax.dot_general(
                        dm, vblk, (((1,), (1,)), ((), ())), preferred_element_type=F32
                    )
        dbias_acc[...] += dbias
        rows_ref[1:2, :D_MODEL] += jnp.sum(dvn * vhat, axis=0, keepdims=True)
        rows_ref[1:2, D_MODEL:] += jnp.sum(dvn, axis=0, keepdims=True)
        dvh = dvn * vg
        m1 = jnp.mean(dvh, axis=-1, keepdims=True)
        m2 = jnp.mean(dvh * vhat, axis=-1, keepdims=True)
        dv = rstd * (dvh - m1 - vhat * m2)
        da_u = du * gelu_du
        da_v = dv * gelu_dv
        da_ref[:, :D_MODEL] = da_u.astype(BF16)
        da_ref[:, D_MODEL:] = da_v.astype(BF16)
        rows_ref[0:1, :D_MODEL] += jnp.sum(da_u, axis=0, keepdims=True)
        rows_ref[0:1, D_MODEL:] += jnp.sum(da_v, axis=0, keepdims=True)

        @pl.when(step == nsteps - 1)
        def _():
            t = lax.broadcasted_iota(jnp.int32, (CHUNK, CHUNK), 0)
            sx = lax.broadcasted_iota(jnp.int32, (CHUNK, CHUNK), 1)
            for g in range(A_GROUPS):
                dw_ref[g] = jnp.where(t >= sx, dw_ref[g], 0.0)
            dbs_ref[...] = _reduce_groups(dbias_acc[...])

    return pl.pallas_call(
        body,
        name=name,
        grid=(nsteps,),
        in_specs=[
            pl.BlockSpec((tr, 2 * D_MODEL), lambda i: (i, 0)),
            pl.BlockSpec((tr, D_MODEL), lambda i: (i, 0)),
            _vec(D_MODEL),
            _vec(D_MODEL),
            pl.BlockSpec((A_GROUPS, CHUNK, CHUNK), lambda i: (0, 0, 0)),
            _vec(D_MODEL, CHUNK),
        ],
        out_specs=[
            pl.BlockSpec((tr, 2 * D_MODEL), lambda i: (i, 0)),
            pl.BlockSpec((A_GROUPS, CHUNK, CHUNK), lambda i: (0, 0, 0)),
            _vec(LANES, CHUNK),
            _vec(2 * D_MODEL, SUBLANES),
        ],
        out_shape=[
            jax.ShapeDtypeStruct((s, 2 * D_MODEL), BF16),
            jax.ShapeDtypeStruct((A_GROUPS, CHUNK, CHUNK), F32),
            jax.ShapeDtypeStruct((CHUNK, LANES), F32),
            jax.ShapeDtypeStruct((SUBLANES, 2 * D_MODEL), F32),
        ],
        scratch_shapes=[pltpu.VMEM((CHUNK, D_MODEL), F32)],
        compiler_params=_params(("arbitrary",)),
    )(a_pre, dp, vn_g, vn_b, w_s, bias_full)


def alibi_tables(dilation):
    qi = jnp.arange(SPAN)[:, None]
    ki = jnp.arange(2 * SPAN)[None, :]
    diff = SPAN + qi - ki
    valid = (diff >= 0) & (diff <= SPAN)
    heads = jnp.arange(1, B_HEADS + 1, dtype=F32)
    slopes = jnp.exp2(-8.0 * heads / B_HEADS)
    bias = -slopes[:, None, None] * (dilation * diff).astype(F32)
    bias = jnp.where(valid[None], bias, NEG).reshape(B_HEADS // 2, 2 * SPAN, 2 * SPAN)
    return bias, bias.transpose(0, 2, 1)


def _pair_rows(x, halves):
    return jnp.concatenate([x * halves[0], x * halves[1]], axis=0)


def _pair_column(v, lane, j):
    pick = lambda h: jnp.sum(jnp.where(lane == h, v, 0.0), axis=1, keepdims=True)
    return jnp.concatenate([pick(2 * j), pick(2 * j + 1)], axis=0)


_NT = (((1,), (1,)), ((), ()))


def permute_rows(xs, dilation, *, inverse, name):
    s = xs[0].shape[0]
    tile = SPAN * dilation
    nat = [pl.BlockSpec((tile, x.shape[1]), lambda i: (i, 0)) for x in xs]
    streams = [pl.BlockSpec((dilation, SPAN, x.shape[1]), lambda i: (0, i, 0)) for x in xs]
    stream_shape = [jax.ShapeDtypeStruct((dilation, s // dilation, x.shape[1]), x.dtype) for x in xs]

    def body(*refs):
        for x_ref, o_ref in zip(refs[: len(xs)], refs[len(xs) :], strict=True):
            w = x_ref.shape[-1]
            if inverse:
                o_ref[...] = jnp.swapaxes(x_ref[...], 0, 1).reshape(tile, w)
            else:
                o_ref[...] = jnp.swapaxes(x_ref[...].reshape(SPAN, dilation, w), 0, 1)

    outs = pl.pallas_call(
        body,
        name=name,
        grid=(s // tile,),
        in_specs=streams if inverse else nat,
        out_specs=nat if inverse else streams,
        out_shape=[jax.ShapeDtypeStruct(x.shape, x.dtype) for x in xs] if inverse else stream_shape,
        compiler_params=_params(("parallel",)),
    )(*[x.reshape(dilation, s // dilation, x.shape[1]) if inverse else x for x in xs])
    return [o.reshape(x.shape) for o, x in zip(outs, xs, strict=True)]


def _qkv_specs(block_of):
    def spec(which, prev):
        def index(*grid):
            blk = block_of(*grid)
            return (jnp.maximum(blk - 1, 0) if prev else blk, which)

        return pl.BlockSpec((SPAN, D_MODEL), index)

    return [spec(0, False), spec(1, True), spec(1, False), spec(2, True), spec(2, False)]


def attn_fwd(qkv_p, pat, *, name):
    _, dilation = B_PATTERNS[pat]
    nb = SEQ // dilation // SPAN
    bias, _ = alibi_tables(dilation)

    def body(q_ref, kp_ref, kc_ref, vp_ref, vc_ref, bias_ref, o_ref, lse_ref):
        n = pl.program_id(1)
        first_prev = jnp.logical_and(n == 0, _lane((2 * SPAN, 2 * SPAN)) < SPAN)
        lane = _lane((SPAN, LANES))
        lo_mask = lane < B_HEAD_DIM
        q = q_ref[...] * jnp.asarray(B_HEAD_DIM**-0.5, BF16)
        kk = jnp.concatenate([kp_ref[...], kc_ref[...]], axis=0)
        vv = jnp.concatenate([vp_ref[...], vc_ref[...]], axis=0)
        halves = (lo_mask.astype(BF16), jnp.logical_not(lo_mask).astype(BF16))
        stats = jnp.zeros((SPAN, LANES), F32)
        for j in range(B_HEADS // 2):
            cs = slice(j * LANES, (j + 1) * LANES)
            sc = lax.dot_general(_pair_rows(q[:, cs], halves), kk[:, cs], _NT, preferred_element_type=F32)
            sc = jnp.where(first_prev, NEG, sc + bias_ref[j])
            m = jnp.max(sc, axis=1, keepdims=True)
            p = jnp.exp(sc - m)
            l = jnp.sum(p, axis=1, keepdims=True)
            acc = jnp.dot(p.astype(BF16), vv[:, cs], preferred_element_type=F32) * (1.0 / l)
            lse_pair = m + jnp.log(l)
            o_ref[:, cs] = jnp.where(lo_mask, acc[:SPAN], acc[SPAN:]).astype(BF16)
            stats = jnp.where(lane == 2 * j, lse_pair[:SPAN], stats)
            stats = jnp.where(lane == 2 * j + 1, lse_pair[SPAN:], stats)
        lse_ref[...] = stats

    return pl.pallas_call(
        body,
        name=name,
        grid=(dilation, nb),
        in_specs=[
            *_qkv_specs(lambda r, n: r * nb + n),
            pl.BlockSpec((B_HEADS // 2, 2 * SPAN, 2 * SPAN), lambda r, n: (0, 0, 0)),
        ],
        out_specs=[
            pl.BlockSpec((SPAN, D_MODEL), lambda r, n: (r * nb + n, 0)),
            pl.BlockSpec((SPAN, LANES), lambda r, n: (r * nb + n, 0)),
        ],
        out_shape=[jax.ShapeDtypeStruct((SEQ, D_MODEL), BF16), jax.ShapeDtypeStruct((SEQ, LANES), F32)],
        compiler_params=_params(("parallel", "arbitrary")),
    )(qkv_p, qkv_p, qkv_p, qkv_p, qkv_p, bias)


def attn_combine(outs, lses, *, name):
    def body(o0, o1, o2, l0, l1, l2, ob_ref, of_ref, lse_ref):
        ls = [l0[...], l1[...], l2[...]]
        m = jnp.maximum(jnp.maximum(ls[0], ls[1]), ls[2])
        tot = jnp.log(jnp.exp(ls[0] - m) + jnp.exp(ls[1] - m) + jnp.exp(ls[2] - m)) + m
        o = None
        for o_ref, l in zip((o0, o1, o2), ls, strict=True):
            term = _expand_groups(jnp.exp(l - tot)) * o_ref[...]
            o = term if o is None else o + term
        ob_ref[...] = o.astype(BF16)
        of_ref[...] = o
        lse_ref[...] = tot

    return pl.pallas_call(
        body,
        name=name,
        grid=(SEQ // ROW_TILE,),
        in_specs=[_rows(D_MODEL)] * 3 + [_rows(LANES)] * 3,
        out_specs=[_rows(D_MODEL), _rows(D_MODEL), _rows(LANES)],
        out_shape=[
            jax.ShapeDtypeStruct((SEQ, D_MODEL), BF16),
            jax.ShapeDtypeStruct((SEQ, D_MODEL), F32),
            jax.ShapeDtypeStruct((SEQ, LANES), F32),
        ],
        compiler_params=_params(("parallel",)),
    )(*outs, *lses)


def attn_delta(do, o, *, name):
    def body(do_ref, o_ref, dob_ref, dl_ref):
        do_v = do_ref[...]
        dob_ref[...] = do_v.astype(BF16)
        dl_ref[...] = _reduce_groups(do_v * o_ref[...])

    return pl.pallas_call(
        body,
        name=name,
        grid=(SEQ // ROW_TILE,),
        in_specs=[_rows(D_MODEL), _rows(D_MODEL)],
        out_specs=[_rows(D_MODEL), _rows(LANES)],
        out_shape=[jax.ShapeDtypeStruct((SEQ, D_MODEL), BF16), jax.ShapeDtypeStruct((SEQ, LANES), F32)],
        compiler_params=_params(("parallel",)),
    )(do, o)


def attn_bwd(qkv_p, do_p, lse_p, delta_p, pat, *, name):
    _, dilation = B_PATTERNS[pat]
    nb = SEQ // dilation // SPAN
    n_blocks = SEQ // SPAN
    bias, bias_t = alibi_tables(dilation)
    last = n_blocks - 1
    q_cols, k_cols, v_cols = (slice(i * D_MODEL, (i + 1) * D_MODEL) for i in range(3))

    def body(q_ref, kp_ref, kc_ref, vp_ref, vc_ref, do_ref, lse_ref, dl_ref, bias_ref, biast_ref, out_ref, cq_ref, ck_ref, cv_ref):
        g = pl.program_id(0)

        @pl.when(g == n_blocks)
        def _():
            out_ref[:, q_cols] = cq_ref[...].astype(BF16)
            out_ref[:, k_cols] = ck_ref[...].astype(BF16)
            out_ref[:, v_cols] = cv_ref[...].astype(BF16)

        @pl.when(g == 0)
        def _():
            cq_ref[...] = jnp.zeros_like(cq_ref)
            ck_ref[...] = jnp.zeros_like(ck_ref)
            cv_ref[...] = jnp.zeros_like(cv_ref)

        @pl.when(g < n_blocks)
        def _():
            lane = _lane((SPAN, LANES))
            lo_mask = lane < B_HEAD_DIM
            pair = (2 * SPAN, 2 * SPAN)
            first = lax.rem(g, nb) == 0
            prev_key_cols = jnp.logical_and(first, _lane(pair) < SPAN)
            prev_key_rows = jnp.logical_and(first, lax.broadcasted_iota(jnp.int32, pair, 0) < SPAN)
            q = q_ref[...] * jnp.asarray(B_HEAD_DIM**-0.5, BF16)
            kk = jnp.concatenate([kp_ref[...], kc_ref[...]], axis=0)
            vv = jnp.concatenate([vp_ref[...], vc_ref[...]], axis=0)
            do_v = do_ref[...]
            lse_v = lse_ref[...]
            dl_v = dl_ref[...]
            lse_t = lse_v.T
            dl_t = dl_v.T
            halves = (lo_mask.astype(BF16), jnp.logical_not(lo_mask).astype(BF16))
            for j in range(B_HEADS // 2):
                cs = slice(j * LANES, (j + 1) * LANES)
                kp, vp = kk[:, cs], vv[:, cs]
                q2 = _pair_rows(q[:, cs], halves)
                do2 = _pair_rows(do_v[:, cs], halves)
                lse_c, dl_c = _pair_column(lse_v, lane, j), _pair_column(dl_v, lane, j)
                lse_r = jnp.concatenate([lse_t[2 * j : 2 * j + 1], lse_t[2 * j + 1 : 2 * j + 2]], axis=1)
                dl_r = jnp.concatenate([dl_t[2 * j : 2 * j + 1], dl_t[2 * j + 1 : 2 * j + 2]], axis=1)
                sc = lax.dot_general(q2, kp, _NT, preferred_element_type=F32)
                p = jnp.exp(jnp.where(prev_key_cols, NEG, sc + bias_ref[j]) - lse_c)
                dp = lax.dot_general(do2, vp, _NT, preferred_element_type=F32)
                ds = (p * (dp - dl_c)).astype(BF16)
                dq2 = jnp.dot(ds, kp, preferred_element_type=F32)
                sc_t = lax.dot_general(kp, q2, _NT, preferred_element_type=F32)
                p_t = jnp.exp(jnp.where(prev_key_rows, NEG, sc_t + biast_ref[j]) - lse_r)
                dp_t = lax.dot_general(vp, do2, _NT, preferred_element_type=F32)
                ds_t = (p_t * (dp_t - dl_r)).astype(BF16)
                dk_pair = jnp.dot(ds_t, q2, preferred_element_type=F32)
                dv_pair = jnp.dot(p_t.astype(BF16), do2, preferred_element_type=F32)
                oq = slice(j * LANES, (j + 1) * LANES)
                ok = slice(D_MODEL + j * LANES, D_MODEL + (j + 1) * LANES)
                ov = slice(2 * D_MODEL + j * LANES, 2 * D_MODEL + (j + 1) * LANES)
                out_ref[:, oq] = cq_ref[:, cs].astype(BF16)
                out_ref[:, ok] = (ck_ref[:, cs] + dk_pair[:SPAN]).astype(BF16)
                out_ref[:, ov] = (cv_ref[:, cs] + dv_pair[:SPAN]).astype(BF16)
                cq_ref[:, cs] = jnp.where(lo_mask, dq2[:SPAN], dq2[SPAN:]) * (B_HEAD_DIM**-0.5)
                ck_ref[:, cs] = dk_pair[SPAN:]
                cv_ref[:, cs] = dv_pair[SPAN:]

    def block_of(g):
        return jnp.minimum(g, last)

    def row_spec(width):
        return pl.BlockSpec((SPAN, width), lambda g: (block_of(g), 0))

    return pl.pallas_call(
        body,
        name=name,
        grid=(n_blocks + 1,),
        in_specs=[
            *_qkv_specs(block_of),
            row_spec(D_MODEL),
            row_spec(LANES),
            row_spec(LANES),
            pl.BlockSpec((B_HEADS // 2, 2 * SPAN, 2 * SPAN), lambda g: (0, 0, 0)),
            pl.BlockSpec((B_HEADS // 2, 2 * SPAN, 2 * SPAN), lambda g: (0, 0, 0)),
        ],
        out_specs=pl.BlockSpec((SPAN, 3 * D_MODEL), lambda g: (jnp.maximum(g - 1, 0), 0)),
        out_shape=jax.ShapeDtypeStruct((SEQ, 3 * D_MODEL), BF16),
        scratch_shapes=[pltpu.VMEM((SPAN, D_MODEL), F32)] * 3,
        compiler_params=_params(("arbitrary",)),
    )(qkv_p, qkv_p, qkv_p, qkv_p, qkv_p, do_p, lse_p, delta_p, bias, bias_t)


def _position():
    x, y, c = lax.axis_index("x"), lax.axis_index("y"), lax.axis_index("c")
    return x, y, c, 4 * x + 2 * y + c


def _peer(k, x, y, c):
    px = 1 - x if k & 4 else x
    py = 1 - y if k & 2 else y
    pc = 1 - c if k & 1 else c
    return (px, py, pc), 4 * px + 2 * py + pc


def _remote(src, dst, send_sem, recv_sem, device):
    return pltpu.make_async_remote_copy(
        src_ref=src, dst_ref=dst, send_sem=send_sem, recv_sem=recv_sem, device_id=device, device_id_type=MESH
    )


def _silu_bf16(cf):
    return (cf * (1.0 / (1.0 + jnp.exp(-cf)))).astype(BF16)


def ada_exchange(c8, w4, b4, ln8):
    nt, _, ncol = w4.shape

    def body(c8_ref, w_ref, b_ref, ln_ref, cg_ref, lng_ref, mrecv_ref, mloc_ref, send_sems, recv_sems):
        x, y, c, me = _position()
        cg_ref[me] = c8_ref[...]
        lng_ref[me] = ln_ref[...]
        first = []
        for k in range(1, N_DEV):
            dev, _ = _peer(k, x, y, c)
            first.append(_remote(c8_ref, cg_ref.at[me], send_sems.at[0, k], recv_sems.at[0, k], dev))
            first.append(_remote(ln_ref, lng_ref.at[me], send_sems.at[1, k], recv_sems.at[1, k], dev))
        for cp in first:
            cp.start()
        for k in range(1, N_DEV):
            dev, pid = _peer(k, x, y, c)
            _remote(c8_ref, cg_ref.at[pid], send_sems.at[0, k], recv_sems.at[0, k], dev).wait_recv()
            _remote(ln_ref, lng_ref.at[pid], send_sems.at[1, k], recv_sems.at[1, k], dev).wait_recv()
        sc = _silu_bf16(cg_ref[...].reshape(N_DEV * SUBLANES, D_MODEL))
        for t in range(nt):
            mloc_ref[t] = jnp.dot(sc, w_ref[t].astype(BF16), preferred_element_type=F32) + b_ref[t : t + 1, :]

        def group(dev_id):
            return pl.ds(pl.multiple_of(dev_id * SUBLANES, SUBLANES), SUBLANES)

        mrecv_ref[me] = mloc_ref[:, group(me), :]
        second = []
        for k in range(1, N_DEV):
            dev, pid = _peer(k, x, y, c)
            second.append(
                _remote(mloc_ref.at[:, group(pid), :], mrecv_ref.at[me], send_sems.at[2, k], recv_sems.at[2, k], dev)
            )
        for cp in second:
            cp.start()
        for k in range(1, N_DEV):
            dev, pid = _peer(k, x, y, c)
            _remote(
                mloc_ref.at[:, group(pid), :], mrecv_ref.at[pid], send_sems.at[2, k], recv_sems.at[2, k], dev
            ).wait_recv()
        for cp in first + second:
            cp.wait_send()

    return pl.pallas_call(
        body,
        name="ada_exchange",
        in_specs=[VMEM, VMEM, VMEM, VMEM],
        out_specs=[VMEM, VMEM, VMEM],
        out_shape=[
            jax.ShapeDtypeStruct((N_DEV, SUBLANES, D_MODEL), F32),
            jax.ShapeDtypeStruct((N_DEV, SUBLANES, LANES), F32),
            jax.ShapeDtypeStruct((N_DEV, nt, SUBLANES, ncol), F32),
        ],
        scratch_shapes=[
            pltpu.VMEM((nt, N_DEV * SUBLANES, ncol), F32),
            pltpu.SemaphoreType.DMA((3, N_DEV)),
            pltpu.SemaphoreType.DMA((3, N_DEV)),
        ],
        compiler_params=pltpu.CompilerParams(vmem_limit_bytes=VMEM_LIMIT_BYTES),
    )(c8, w4, b4, ln8)


def small_exchange(dmx, flat):
    def body(dmx_ref, flat_ref, dmrecv_ref, red_ref, land_ref, send_sems, recv_sems):
        x, y, c, me = _position()
        dmrecv_ref[me] = dmx_ref[me]
        land_ref[me] = flat_ref[me]
        first = []
        for k in range(1, N_DEV):
            dev, pid = _peer(k, x, y, c)
            first.append(_remote(dmx_ref.at[pid], dmrecv_ref.at[me], send_sems.at[0, k], recv_sems.at[0, k], dev))
            first.append(_remote(flat_ref.at[pid], land_ref.at[me], send_sems.at[1, k], recv_sems.at[1, k], dev))
        for cp in first:
            cp.start()
        for k in range(1, N_DEV):
            dev, pid = _peer(k, x, y, c)
            _remote(dmx_ref.at[pid], dmrecv_ref.at[pid], send_sems.at[0, k], recv_sems.at[0, k], dev).wait_recv()
            _remote(flat_ref.at[pid], land_ref.at[pid], send_sems.at[1, k], recv_sems.at[1, k], dev).wait_recv()
        total = land_ref[0]
        for s in range(1, N_DEV):
            total = total + land_ref[s]
        red_ref[me] = total
        second = []
        for k in range(1, N_DEV):
            dev, _ = _peer(k, x, y, c)
            second.append(_remote(red_ref.at[me], red_ref.at[me], send_sems.at[2, k], recv_sems.at[2, k], dev))
        for cp in second:
            cp.start()
        for k in range(1, N_DEV):
            dev, pid = _peer(k, x, y, c)
            _remote(red_ref.at[pid], red_ref.at[pid], send_sems.at[2, k], recv_sems.at[2, k], dev).wait_recv()
        for cp in first + second:
            cp.wait_send()

    return pl.pallas_call(
        body,
        name="small_exchange",
        in_specs=[VMEM, VMEM],
        out_specs=[VMEM, VMEM],
        out_shape=[jax.ShapeDtypeStruct(dmx.shape, F32), jax.ShapeDtypeStruct(flat.shape, F32)],
        scratch_shapes=[
            pltpu.VMEM(flat.shape, F32),
            pltpu.SemaphoreType.DMA((3, N_DEV)),
            pltpu.SemaphoreType.DMA((3, N_DEV)),
        ],
        compiler_params=pltpu.CompilerParams(vmem_limit_bytes=VMEM_LIMIT_BYTES),
    )(dmx, flat)


HBM = pl.BlockSpec(memory_space=pltpu.HBM)
SEM = pl.BlockSpec(memory_space=pltpu.SEMAPHORE)
EFFECT = pltpu.SideEffectType.DATAFLOW_SIDE_EFFECTING


REGROUP_ROWS = 256


def shards_to_columns(x, *, name):
    p, k, n = x.shape

    def body(x_ref, o_ref):
        for s in range(p):
            o_ref[:, s * n : (s + 1) * n] = x_ref[s]

    return pl.pallas_call(
        body,
        name=name,
        grid=(k // REGROUP_ROWS,),
        in_specs=[pl.BlockSpec((p, REGROUP_ROWS, n), lambda i: (0, i, 0))],
        out_specs=pl.BlockSpec((REGROUP_ROWS, p * n), lambda i: (i, 0)),
        out_shape=jax.ShapeDtypeStruct((k, p * n), x.dtype),
        compiler_params=_params(("parallel",)),
    )(x)


def columns_to_shards(xs, *, name):
    k = xs[0].shape[0]
    widths = [x.shape[1] for x in xs]
    n = sum(widths) // N_DEV
    pieces = []
    for s in range(N_DEV):
        start = 0
        for i, w in enumerate(widths):
            lo, hi = max(start, s * n), min(start + w, (s + 1) * n)
            if lo < hi:
                pieces.append((i, lo - start, s, lo - s * n, hi - lo))
            start += w

    def body(*refs):
        x_refs, o_ref = refs[: len(xs)], refs[-1]
        for i, c0, s, d0, w in pieces:
            o_ref[s, :, d0 : d0 + w] = x_refs[i][:, c0 : c0 + w]

    return pl.pallas_call(
        body,
        name=name,
        grid=(k // REGROUP_ROWS,),
        in_specs=[pl.BlockSpec((REGROUP_ROWS, w), lambda i: (i, 0)) for w in widths],
        out_specs=pl.BlockSpec((N_DEV, REGROUP_ROWS, n), lambda i: (0, i, 0)),
        out_shape=jax.ShapeDtypeStruct((N_DEV, k, n), xs[0].dtype),
        compiler_params=_params(("parallel",)),
    )(*xs)


def _own_slot(me, block):
    land = lax.empty((N_DEV, *block.shape), block.dtype)
    return lax.dynamic_update_slice_in_dim(land, block[None], me, axis=0)


N_CHIP_PEERS = 3


class Gather:
    def __init__(self, shards, lands, after, *, name):
        nt = len(shards)
        self.name = name

        def body(*refs):
            src_refs, land_refs = refs[:nt], refs[nt : 2 * nt]
            send_sems, recv_sems = refs[2 * nt + 1 : 3 * nt + 1], refs[3 * nt + 1 : 4 * nt + 1]
            token = refs[-1]
            x, y, c, me = _position()
            for t in range(nt):
                for k, dev in enumerate(self._targets(x, y, c)):
                    _remote(src_refs[t], land_refs[t].at[me], send_sems[t].at[k], recv_sems[t].at[k], dev).start()
            token[...] = jnp.zeros_like(token)

        outs = pl.pallas_call(
            body,
            name=name + "_start",
            in_specs=[HBM] * (2 * nt) + [ANY],
            out_specs=[SEM] * (2 * nt) + [HBM] * (2 * nt) + [VMEM],
            out_shape=[pltpu.SemaphoreType.DMA((1 + N_CHIP_PEERS,))] * (2 * nt)
            + [pltpu.HBM(a.shape, a.dtype) for a in (*shards, *lands)]
            + [jax.ShapeDtypeStruct((SUBLANES, LANES), F32)],
            input_output_aliases={i: 2 * nt + i for i in range(2 * nt)},
            compiler_params=pltpu.CompilerParams(has_side_effects=EFFECT),
        )(*[pltpu.with_memory_space_constraint(a, pltpu.HBM) for a in (*shards, *lands)], after)
        self.send_sems, self.recv_sems = list(outs[:nt]), list(outs[nt : 2 * nt])
        self.srcs, self.lands = list(outs[2 * nt : 3 * nt]), list(outs[3 * nt : 4 * nt])
        self.token = outs[-1]

    @staticmethod
    def _chips(x, y):
        return [(1 - x, y), (x, 1 - y), (1 - x, 1 - y)]

    @classmethod
    def _targets(cls, x, y, c):
        return [(x, y, 1 - c)] + [(*chip, c) for chip in cls._chips(x, y)]

    def zero(self):
        return self.token[0, 0]

    @staticmethod
    def _slot(px, py, pc):
        return 4 * px + 2 * py + pc

    def pass_on(self, which, after, *, name):
        n = len(which)

        def pass_body(*refs):
            land_refs, recv_sems = refs[:n], refs[n : 2 * n]
            fwd_send, fwd_recv = refs[3 * n + 1 : 4 * n + 1], refs[4 * n + 1 : 5 * n + 1]
            x, y, c, _ = _position()
            for t in range(n):
                for j, chip in enumerate(self._chips(x, y)):
                    blk = land_refs[t].at[self._slot(*chip, c)]
                    _remote(blk, blk, fwd_send[t].at[j], recv_sems[t].at[1 + j], (*chip, c)).wait_recv()
                    _remote(blk, blk, fwd_send[t].at[j], fwd_recv[t].at[j], (x, y, 1 - c)).start()

        lands = [self.lands[t] for t in which]
        outs = pl.pallas_call(
            pass_body,
            name=name,
            in_specs=[HBM] * n + [SEM] * n + [ANY],
            out_specs=[HBM] * n + [SEM] * (2 * n),
            out_shape=[pltpu.HBM(a.shape, a.dtype) for a in lands] + [pltpu.SemaphoreType.DMA((N_CHIP_PEERS,))] * (2 * n),
            input_output_aliases={i: i for i in range(n)},
            compiler_params=pltpu.CompilerParams(has_side_effects=EFFECT),
        )(*lands, *[self.recv_sems[t] for t in which], after)
        return which, outs[:n], outs[n : 2 * n], outs[2 * n :]

    def wait(self, which, after, *, name):
        return self.finish(self.pass_on(which, after, name=name + "_pass"), after, name=name)

    def finish(self, passed, after, *, name):
        which, lands, fwd_send, fwd_recv = passed
        n = len(which)
        slot = self._slot

        def wait_body(*refs):
            src_refs, land_refs = refs[:n], refs[n : 2 * n]
            send_sems, recv_sems = refs[2 * n : 3 * n], refs[3 * n : 4 * n]
            fwd_send, fwd_recv = refs[4 * n : 5 * n], refs[5 * n : 6 * n]
            x, y, c, me = _position()
            sibling = (x, y, 1 - c)
            for t in range(n):
                for k, dev in enumerate(self._targets(x, y, c)):
                    _remote(src_refs[t], land_refs[t].at[me], send_sems[t].at[k], recv_sems[t].at[k], dev).wait_send()
                blk = land_refs[t].at[slot(x, y, 1 - c)]
                _remote(blk, blk, send_sems[t].at[0], recv_sems[t].at[0], sibling).wait_recv()
                for j, chip in enumerate(self._chips(x, y)):
                    sent = land_refs[t].at[slot(*chip, c)]
                    _remote(sent, sent, fwd_send[t].at[j], fwd_recv[t].at[j], sibling).wait_send()
                    got = land_refs[t].at[slot(*chip, 1 - c)]
                    _remote(got, got, fwd_send[t].at[j], fwd_recv[t].at[j], sibling).wait_recv()

        srcs = [self.srcs[t] for t in which]
        outs = pl.pallas_call(
            wait_body,
            name=name,
            in_specs=[HBM] * (2 * n) + [SEM] * (4 * n) + [ANY],
            out_specs=[HBM] * (2 * n),
            out_shape=[pltpu.HBM(a.shape, a.dtype) for a in (*srcs, *lands)],
            input_output_aliases={i: i for i in range(2 * n)},
            compiler_params=pltpu.CompilerParams(has_side_effects=EFFECT),
        )(
            *srcs, *lands, *[self.send_sems[t] for t in which], *[self.recv_sems[t] for t in which], *fwd_send,
            *fwd_recv, after,
        )
        return outs[n:]


class Scatter:
    def __init__(self, srcs, lands, after, *, name):
        self.name = name
        nt = self.nt = len(srcs)
        peers = N_DEV - 1

        def body(*refs):
            src_refs, land_refs = refs[:nt], refs[nt : 2 * nt]
            send_sems, recv_sems = refs[2 * nt + 1 : 3 * nt + 1], refs[3 * nt + 1 : 4 * nt + 1]
            token = refs[-1]
            x, y, c, me = _position()
            for t in range(nt):
                for k in range(1, N_DEV):
                    dev, pid = _peer(k, x, y, c)
                    src = src_refs[t].at[pid]
                    _remote(src, land_refs[t].at[me], send_sems[t].at[k - 1], recv_sems[t].at[k - 1], dev).start()
            token[...] = jnp.zeros_like(token)

        outs = pl.pallas_call(
            body,
            name=name + "_start",
            in_specs=[HBM] * (2 * nt) + [ANY],
            out_specs=[SEM] * (2 * nt) + [HBM] * (2 * nt) + [VMEM],
            out_shape=[pltpu.SemaphoreType.DMA((peers,))] * (2 * nt)
            + [pltpu.HBM(a.shape, a.dtype) for a in (*srcs, *lands)]
            + [jax.ShapeDtypeStruct((SUBLANES, LANES), F32)],
            input_output_aliases={i: 2 * nt + i for i in range(2 * nt)},
            compiler_params=pltpu.CompilerParams(has_side_effects=EFFECT),
        )(*[pltpu.with_memory_space_constraint(a, pltpu.HBM) for a in (*srcs, *lands)], after)
        self.send_sems, self.recv_sems = outs[:nt], outs[nt : 2 * nt]
        self.srcs, self.lands = outs[2 * nt : 3 * nt], outs[3 * nt : 4 * nt]
        self.token = outs[-1]

    def zero(self):
        return self.token[0, 0]

    def wait(self, which, after, *, name):
        n = len(which)

        def body(*refs):
            src_refs, land_refs = refs[:n], refs[n : 2 * n]
            send_sems, recv_sems = refs[2 * n : 3 * n], refs[3 * n : 4 * n]
            x, y, c, _ = _position()
            for t in range(n):
                for k in range(1, N_DEV):
                    dev, pid = _peer(k, x, y, c)
                    src = src_refs[t].at[pid]
                    cp = _remote(src, land_refs[t].at[pid], send_sems[t].at[k - 1], recv_sems[t].at[k - 1], dev)
                    cp.wait_send()
                    cp.wait_recv()

        srcs = [self.srcs[t] for t in which]
        lands = [self.lands[t] for t in which]
        outs = pl.pallas_call(
            body,
            name=name,
            in_specs=[HBM] * (2 * n) + [SEM] * (2 * n) + [ANY],
            out_specs=[HBM] * (2 * n),
            out_shape=[pltpu.HBM(a.shape, a.dtype) for a in (*srcs, *lands)],
            input_output_aliases={i: i for i in range(2 * n)},
            compiler_params=pltpu.CompilerParams(has_side_effects=EFFECT),
        )(*srcs, *lands, *[self.send_sems[t] for t in which], *[self.recv_sems[t] for t in which], after)
        return outs[n:]


def _adam_update(g, w, m, v):
    m2 = ADAM_B1 * m + (1.0 - ADAM_B1) * g
    v2 = ADAM_B2 * v + (1.0 - ADAM_B2) * jnp.square(g)
    m_hat = m2 / (1.0 - ADAM_B1**ADAM_STEP)
    v_hat = v2 / (1.0 - ADAM_B2**ADAM_STEP)
    delta = -ADAM_LR * (m_hat / (jnp.sqrt(v_hat) + ADAM_EPS) + ADAM_WD * w)
    return delta, m2, v2


def adamw(gparts, w, m, v, *, name):
    nl, r, c = w.shape
    p = gparts[0].shape[0]
    tr = r if r <= 256 else (256 if c <= D_MODEL else 128)
    ni = r // tr

    def body(*refs):
        g_refs = refs[:nl]
        w_ref, m_ref, v_ref, go_ref, d_ref, mo_ref, vo_ref = refs[nl:]
        for layer in range(nl):

            @pl.when(pl.program_id(0) == layer)
            def _(g_ref=g_refs[layer]):
                g = g_ref[0].astype(F32)
                for i in range(1, p):
                    g = g + g_ref[i].astype(F32)
                delta, m2, v2 = _adam_update(g, w_ref[...], m_ref[...], v_ref[...])
                go_ref[...] = g
                d_ref[...] = delta
                mo_ref[...] = m2
                vo_ref[...] = v2

    def parts_spec(layer):
        def index(l, i):
            return (0, jnp.where(l == layer, i, jnp.where(l < layer, 0, ni - 1)), 0)

        return pl.BlockSpec((p, tr, c), index)

    blk = pl.BlockSpec((None, tr, c), lambda l, i: (l, i, 0))
    return pl.pallas_call(
        body,
        name=name,
        grid=(nl, ni),
        in_specs=[*[parts_spec(layer) for layer in range(nl)], blk, blk, blk],
        out_specs=[blk] * 4,
        out_shape=[jax.ShapeDtypeStruct((nl, r, c), F32)] * 4,
        compiler_params=_params(("arbitrary", "arbitrary")),
    )(*gparts, w, m, v)


def adamw_small(items, *, name):
    n = len(items)

    def body(*refs):
        ins, outs = refs[: 4 * n], refs[4 * n :]
        for t in range(n):
            g_ref, w_ref, m_ref, v_ref = ins[4 * t : 4 * t + 4]
            g = g_ref[...]
            delta, m2, v2 = _adam_update(g, w_ref[...], m_ref[...], v_ref[...])
            for o_ref, val in zip(outs[4 * t : 4 * t + 4], (g, delta, m2, v2), strict=True):
                o_ref[...] = val

    outs = pl.pallas_call(
        body,
        name=name,
        out_shape=[jax.ShapeDtypeStruct(item[1].shape, F32) for item in items for _ in range(4)],
    )(*[a for item in items for a in item])
    return [outs[4 * t : 4 * t + 4] for t in range(n)]


def ada_grad_adamw(cg, dmrecv, w4, m4, v4, *, name):
    nt, k, ncol = w4.shape

    def body(cg_ref, dm_ref, w_ref, m_ref, v_ref, go_ref, d_ref, mo_ref, vo_ref, gb_ref):
        sc = _silu_bf16(cg_ref[...].reshape(N_DEV * SUBLANES, k))
        dm = dm_ref[...].reshape(N_DEV * SUBLANES, ncol)
        g = lax.dot_general(sc, dm.astype(BF16), (((0,), (0,)), ((), ())), preferred_element_type=F32)
        delta, m2, v2 = _adam_update(g, w_ref[...], m_ref[...], v_ref[...])
        go_ref[...] = g
        d_ref[...] = delta
        mo_ref[...] = m2
        vo_ref[...] = v2
        gb_ref[...] = jnp.broadcast_to(jnp.sum(dm, axis=0, keepdims=True), (SUBLANES, ncol))

    wblk = pl.BlockSpec((None, k, ncol), lambda t: (t, 0, 0))
    return pl.pallas_call(
        body,
        name=name,
        grid=(nt,),
        in_specs=[
            pl.BlockSpec((N_DEV, SUBLANES, k), lambda t: (0, 0, 0)),
            pl.BlockSpec((N_DEV, None, SUBLANES, ncol), lambda t: (0, t, 0, 0)),
            wblk,
            wblk,
            wblk,
        ],
        out_specs=[wblk] * 4 + [pl.BlockSpec((None, SUBLANES, ncol), lambda t: (t, 0, 0))],
        out_shape=[jax.ShapeDtypeStruct((nt, k, ncol), F32)] * 4 + [jax.ShapeDtypeStruct((nt, SUBLANES, ncol), F32)],
        compiler_params=_params(("parallel",)),
    )(cg, dmrecv, w4, m4, v4)


def kernel(x, c, ada_w, ada_b, ln_g, ln_b, a_w_in, a_b_in, a_vn_g, a_vn_b, a_w_s, a_b_s, a_w_out, b_w_qkv, b_w_out, mlp_w_up, mlp_w_down, loss_target, m_ada_w, m_ada_b, m_ln_g, m_ln_b, m_a_w_in, m_a_b_in, m_a_vn_g, m_a_vn_b, m_a_w_s, m_a_b_s, m_a_w_out, m_b_w_qkv, m_b_w_out, m_mlp_w_up, m_mlp_w_down, v_ada_w, v_ada_b, v_ln_g, v_ln_b, v_a_w_in, v_a_b_in, v_a_vn_g, v_a_vn_b, v_a_w_s, v_a_b_s, v_a_w_out, v_b_w_qkv, v_b_w_out, v_mlp_w_up, v_mlp_w_down):
    x0 = x[0]
    target = loss_target[0]
    me = 4 * lax.axis_index("x") + 2 * lax.axis_index("y") + lax.axis_index("c")

    ada_w4 = ada_w.reshape(N_SUB, D_MODEL, -1)
    ada_b4 = ada_b.reshape(N_SUB, -1)
    ln8 = jnp.concatenate([ln_g.reshape(N_SUB, -1), ln_b.reshape(N_SUB, -1)], axis=0)
    c8 = jnp.broadcast_to(c, (SUBLANES, D_MODEL))
    cg, lng, mrecv = ada_exchange(c8, ada_w4, ada_b4, ln8)

    W_IN, W_AOUT, W_UP0, W_DN0, W_QKV, W_BOUT, W_UP1, W_DN1 = range(8)
    shards = [
        a_w_in[0].astype(BF16),
        a_w_out[0].astype(BF16),
        mlp_w_up[0].astype(BF16),
        mlp_w_down[0].astype(BF16),
        b_w_qkv[0].astype(BF16),
        b_w_out[0].astype(BF16),
        mlp_w_up[1].astype(BF16),
        mlp_w_down[1].astype(BF16),
    ]
    gather = Gather(shards, [_own_slot(me, s) for s in shards], mrecv, name="gather")

    modv = mrecv[:, :, 0, :].transpose(1, 0, 2).reshape(N_SUB, 3 * D_MODEL) + gather.zero()
    shift = [modv[t : t + 1, :D_MODEL] for t in range(N_SUB)]
    scale = [modv[t : t + 1, D_MODEL : 2 * D_MODEL] for t in range(N_SUB)]
    gate1 = [1.0 + modv[t : t + 1, 2 * D_MODEL :] for t in range(N_SUB)]
    lng_full = [lng[:, t, :].reshape(1, D_MODEL) for t in range(N_SUB)]
    lnb_full = [lng[:, N_SUB + t, :].reshape(1, D_MODEL) for t in range(N_SUB)]

    ident = lambda acc: (acc,)
    def relu2(a):
        r = jnp.maximum(a, jnp.zeros_like(a))
        return r * r
    vn_g, vn_b, w_s = a_vn_g, a_vn_b, a_w_s[0]
    bias_full = jnp.repeat(a_b_s[0].T, A_GROUP_DIM, axis=1)
    w_up3, w_dn3 = [None, None], [None, None]

    def mlp_forward(i, h, up, dn, x_in=None, t_next=None):
        w_up3[i], w_dn3[i] = up, dn.reshape(1, D_FF, D_MODEL)
        (a,) = mm_nn(h, w_up3[i], name=f"mlp{i}_up", tm=2048, ps=2, tn=512, tk=D_MODEL, epilogue=ident, outs=(BF16,))
        if t_next is None:
            (y,) = mm_nn(
                a, w_dn3[i], name=f"mlp{i}_down", tm=1024, ps=1, tn=512, tk=D_FF, prologue=relu2, epilogue=ident, outs=(BF16,)
            )
            return a, y
        y, xn, hn = mm_nn(
            a, w_dn3[i], name=f"mlp{i}_down", tm=512, ps=1, tn=D_MODEL, tk=D_FF, prologue=relu2,
            epilogue=residual_ln_epilogue, extras=residual_extras(x_in, t_next - 1), outs=(BF16, F32, BF16),
        )
        return a, y, xn, hn

    def residual_extras(x_in, t):
        rows = (gate1[t], lng_full[t], lnb_full[t], scale[t + 1], shift[t + 1])
        return [(x_in, "full")] + [(r, "row") for r in rows]

    h0 = modulate(x0, scale[0], shift[0], name="modulate0")
    w_in3, w_aout3 = gather.wait([W_IN, W_AOUT], h0, name="gather_wait_a")
    w_aout3 = w_aout3.reshape(1, D_MODEL, D_MODEL)
    (a_pre,) = mm_nn(
        h0, w_in3, name="a_in", tm=2048, ps=4, tn=256, tk=D_MODEL, epilogue=lambda acc, b: (acc + b,),
        extras=[(a_b_in, "row")], outs=(BF16,),
    )
    p_gate = gate_fwd(a_pre, vn_g, vn_b, w_s, bias_full, name="gate_fwd")
    y0, x1, h1 = mm_nn(
        p_gate, w_aout3, name="a_out", tm=1024, ps=1, tn=D_MODEL, tk=D_MODEL, epilogue=residual_ln_epilogue,
        extras=residual_extras(x0, 0), outs=(BF16, F32, BF16),
    )
    w_mlp0 = gather.wait([W_UP0, W_DN0], y0, name="gather_wait_mlp0")
    a1, y1, x2, h2 = mlp_forward(0, h1, *w_mlp0, x_in=x1, t_next=2)
    (w_qkv_shards,) = gather.wait([W_QKV], y1, name="gather_wait_qkv")
    w_qkv3 = shards_to_columns(w_qkv_shards, name="w_qkv_columns")[None]
    pat_tiles = 3
    dil = [d for _, d in B_PATTERNS]
    qkv_p, pat_o, pat_lse = [], [], []
    for g in range(N_PAT):
        (qkv_g,) = mm_nn(
            h2, w_qkv3, name=f"b_qkv{g}", tm=2048, ps=1, tn=D_MODEL, tk=D_MODEL, epilogue=ident, outs=(BF16,),
            b_tile0=pat_tiles * g, b_tiles=pat_tiles, out_streams=dil[g],
        )
        o_g, lse_g = attn_fwd(qkv_g, g, name=f"attn_fwd{g}")
        if g == 0:
            rest_passed = gather.pass_on([W_BOUT, W_UP1, W_DN1], lse_g, name="gather_pass_rest")
        if g > 0:
            o_g, lse_g = permute_rows([o_g, lse_g], dil[g], inverse=True, name=f"unperm_o{g}")
        qkv_p.append(qkv_g)
        pat_o.append(o_g)
        pat_lse.append(lse_g)
    o_b, o_f, lse = attn_combine(pat_o, pat_lse, name="attn_combine")
    w_bout3, *w_mlp1 = gather.finish(rest_passed, lse, name="gather_wait_rest")
    w_bout3 = w_bout3.reshape(1, D_MODEL, D_MODEL)
    y2, x3, h3 = mm_nn(
        o_b, w_bout3, name="b_out", tm=1024, ps=1, tn=D_MODEL, tk=D_MODEL, epilogue=residual_ln_epilogue,
        extras=residual_extras(x2, 2), outs=(BF16, F32, BF16),
    )
    a3, y3 = mlp_forward(1, h3, *w_mlp1)
    dz3, dy3, st3, loss_local = residual_ln_loss_bwd(
        x3, y3, gate1[3], lng_full[3], lnb_full[3], target, name="res_ln3_loss_bwd"
    )

    def scatter(parts, after, name):
        parts = [p.reshape(N_DEV, -1, p.shape[-1]) for p in parts]
        lands = [_own_slot(me, lax.dynamic_index_in_dim(p, me, 0, keepdims=False)) for p in parts]
        return Scatter(parts, lands, after, name=name)

    xs_in, ys = [x0, x1, x2, x3], [y0, y1, y2, y3]

    def residual_bwd_extras(dz_later, t, zero):
        return [
            (dz_later, "full"), (scale[t + 1] + zero, "row"), (xs_in[t], "full"), (ys[t], "full"),
            (gate1[t], "row"), (lng_full[t], "row"), (lnb_full[t], "row"),
        ]

    def mlp_backward(i, h, a, dy, dz_later, t):
        (da,) = mm_nt(
            dy,
            w_dn3[i],
            name=f"mlp{i}_da",
            tm=2048,
            tko=1024,
            ps=1,
            tc=D_MODEL,
            epilogue=lambda acc, act: (acc * (2.0 * jnp.maximum(act.astype(F32), 0.0)),),
            extras=[(a, "full")],
            outs=(BF16,),
        )
        dw_dn = mm_tn(
            a, dy, name=f"mlp{i}_dw_down", p=1, tk=1024, ps=1, tn=D_MODEL, tmc=2048, prologue=relu2, out_dtype=BF16
        )
        dw_up = mm_tn(h, da, name=f"mlp{i}_dw_up", p=N_DEV, tk=1024, ps=2, tn=512, tmc=2048, out_dtype=BF16)
        rs = scatter([dw_up, dw_dn], da, f"scatter_mlp{i}")
        dz, dy_before, st = mm_nt(
            da, w_up3[i], name=f"mlp{i}_dh", tm=512, tko=D_MODEL, ps=N_DEV, tc=512, epilogue=residual_bwd_epilogue,
            extras=residual_bwd_extras(dz_later, t, rs.zero()), outs=(F32, BF16), stats=True,
        )
        return rs, dz, dy_before, st

    rs_mlp1, dz2, dy2, st2 = mlp_backward(1, h3, a3, dy3, dz3, 2)
    (d_o,) = mm_nt(dy2, w_bout3, name="b_do", tm=2048, tko=1024, ps=1, tc=D_MODEL, epilogue=ident, outs=(F32,))
    do_b, delta = attn_delta(d_o, o_f, name="attn_delta")
    dh2, dw_pat = None, []
    for g in range(N_PAT):
        do_g, lse_g, delta_g = do_b, lse, delta
        if g > 0:
            do_g, lse_g, delta_g = permute_rows([do_b, lse, delta], dil[g], inverse=False, name=f"perm_do{g}")
        dqkv_g = attn_bwd(qkv_p[g], do_g, lse_g, delta_g, g, name=f"attn_bwd{g}")
        dw_pat.append(
            mm_tn(
                h2, dqkv_g, name=f"b_dw_qkv{g}", p=1, tk=1024, ps=1, tn=D_MODEL, tmc=2048, out_dtype=BF16,
                g_streams=dil[g],
            )[0]
        )
        if g == N_PAT - 1:
            break
        (dh2,) = mm_nt(
            dqkv_g, w_qkv3, name=f"b_dh{g}", tm=2048, tko=512, ps=1, tc=pat_tiles * D_MODEL, outs=(F32,),
            b_tile0=g, g_streams=dil[g],
            epilogue=ident if g == 0 else (lambda acc, prev, d=dil[g]: (_from_streams(acc, d) + prev,)),
            extras=[] if g == 0 else [(dh2, "full")],
        )
    dw_bout = mm_tn(o_b, dy2, name="b_dw_out", p=1, tk=1024, ps=1, tn=D_MODEL, tmc=2048, out_dtype=BF16)
    dw_qkv = columns_to_shards(dw_pat, name="dw_qkv_shards")
    rs_b = scatter([dw_qkv, dw_bout], dqkv_g, "scatter_b")
    dz1, dy1, st1 = mm_nt(
        dqkv_g, w_qkv3, name=f"b_dh{N_PAT - 1}", tm=512, tko=D_MODEL, ps=1, tc=pat_tiles * D_MODEL, outs=(F32, BF16),
        b_tile0=N_PAT - 1, g_streams=dil[-1], stats=True,
        epilogue=lambda acc, prev, *rest: residual_bwd_epilogue(_from_streams(acc, dil[-1]) + prev, *rest),
        extras=[(dh2, "full")] + residual_bwd_extras(dz2, 1, rs_b.zero()),
    )
    rs_mlp0, dz0, dy0, st0 = mlp_backward(0, h1, a1, dy1, dz1, 0)
    (dp_gate,) = mm_nt(dy0, w_aout3, name="a_dp", tm=2048, tko=1024, ps=1, tc=D_MODEL, epilogue=ident, outs=(F32,))
    dw_aout = mm_tn(p_gate, dy0, name="a_dw_out", p=1, tk=1024, ps=1, tn=D_MODEL, tmc=2048, out_dtype=BF16)
    rs_aout = scatter([dw_aout], dp_gate, "scatter_a_out")
    da0, d_ws, d_bs, gate_rows = gate_bwd(a_pre, dp_gate, vn_g + rs_aout.zero(), vn_b, w_s, bias_full, name="gate_bwd")
    dw_in = mm_tn(h0, da0, name="a_dw_in", p=N_DEV, tk=1024, ps=4, tn=256, tmc=2048, out_dtype=BF16)
    rs_in = scatter([dw_in], d_ws, "scatter_a_in")
    grad_x, stf = mm_nt(
        da0, w_in3, name="a_dh", tm=1024, tko=D_MODEL, ps=N_DEV, tc=256, epilogue=input_bwd_epilogue,
        extras=[(dz0, "full"), (scale[0] + rs_in.zero(), "row"), (x0, "full")], outs=(F32,), stats=True,
    )

    stats_after = [stf, st0, st1, st2]
    stats_own = [st0, st1, st2, st3]
    dm = jnp.stack(
        [
            jnp.concatenate(
                [stats_after[t][ST_DSHIFT], stats_after[t][ST_DSCALE], stats_own[t][ST_DGATE]], axis=0
            )
            for t in range(N_SUB)
        ]
    )
    ncol = 3 * D_MODEL // N_DEV
    dmx = jnp.pad(
        dm.reshape(N_SUB, N_DEV, ncol).transpose(1, 0, 2)[:, :, None, :], ((0, 0), (0, 0), (0, SUBLANES - 1), (0, 0))
    )
    small = [
        gate_rows[0],
        gate_rows[1],
        d_ws.reshape(-1),
        d_bs[:, :A_GROUPS].T.reshape(-1),
        *[stats_own[t][ST_DG] for t in range(N_SUB)],
        *[stats_own[t][ST_DB] for t in range(N_SUB)],
        jnp.pad(loss_local.reshape(1), (0, LANES - 1)),
    ]
    n_small = sum(s.size for s in small)
    part_rows = -(-n_small // (N_DEV * LANES * SUBLANES)) * SUBLANES
    flat = jnp.concatenate(small + [jnp.zeros((N_DEV * part_rows * LANES - n_small,), F32)])
    dmrecv, reduced = small_exchange(dmx, flat.reshape(N_DEV, part_rows, LANES))
    reduced = reduced.reshape(-1)
    sizes = [2 * D_MODEL, D_MODEL, D_MODEL, A_GROUPS * CHUNK * CHUNK, A_GROUPS * CHUNK, N_SUB * D_MODEL, N_SUB * D_MODEL]
    offs = [sum(sizes[:i]) for i in range(len(sizes) + 1)]
    g_b_in, g_vn_g, g_vn_b, g_ws, g_bs, g_lng, g_lnb = [reduced[offs[i] : offs[i + 1]] for i in range(len(sizes))]
    loss = reduced[offs[-1]]

    results = {}

    def update(wname, gparts, w, m, v):
        shape = w.shape
        layers = len(gparts) if isinstance(gparts, list) else 1
        w3 = w.reshape(layers, -1, shape[-1])
        parts = [g.reshape(g.shape[0], *w3.shape[1:]) for g in (gparts if layers > 1 else [gparts])]
        outs = adamw(parts, w3, m.reshape(w3.shape), v.reshape(w3.shape), name=f"adamw_{wname}")
        results[wname] = [o.reshape(shape) for o in outs]

    ada_outs = ada_grad_adamw(cg, dmrecv, ada_w4, m_ada_w.reshape(ada_w4.shape), v_ada_w.reshape(ada_w4.shape), name="ada_grad_adamw")
    results["ada_w"] = [o.reshape(ada_w.shape) for o in ada_outs[:4]]
    ln_cols = D_MODEL // N_DEV
    my_ln = lambda gfull: lax.dynamic_slice_in_dim(gfull.reshape(N_SUB, N_DEV, ln_cols), me, 1, axis=1)
    small_params = [
        ("ada_b", ada_outs[4][:, 0, :], ada_b, m_ada_b, v_ada_b),
        ("ln_g", my_ln(g_lng), ln_g, m_ln_g, v_ln_g),
        ("ln_b", my_ln(g_lnb), ln_b, m_ln_b, v_ln_b),
        ("a_b_in", g_b_in, a_b_in, m_a_b_in, v_a_b_in),
        ("a_vn_g", g_vn_g, a_vn_g, m_a_vn_g, v_a_vn_g),
        ("a_vn_b", g_vn_b, a_vn_b, m_a_vn_b, v_a_vn_b),
        ("a_b_s", g_bs, a_b_s, m_a_b_s, v_a_b_s),
    ]
    small_outs = adamw_small(
        [[a.reshape(-1, w.shape[-1]) for a in (g, w, m, v)] for _, g, w, m, v in small_params], name="adamw_small"
    )
    for (wname, _, w, _, _), outs in zip(small_params, small_outs, strict=True):
        results[wname] = [o.reshape(w.shape) for o in outs]
    update("a_w_s", g_ws[None], a_w_s, m_a_w_s, v_a_w_s)
    g_up1, g_dn1 = rs_mlp1.wait([0, 1], grad_x, name="scatter_wait_mlp1")
    g_qkv, g_bout = rs_b.wait([0, 1], grad_x, name="scatter_wait_b")
    update("b_w_qkv", g_qkv, b_w_qkv, m_b_w_qkv, v_b_w_qkv)
    update("b_w_out", g_bout, b_w_out, m_b_w_out, v_b_w_out)
    g_up0, g_dn0 = rs_mlp0.wait([0, 1], grad_x, name="scatter_wait_mlp0")
    update("mlp_w_up", [g_up0, g_up1], mlp_w_up, m_mlp_w_up, v_mlp_w_up)
    update("mlp_w_down", [g_dn0, g_dn1], mlp_w_down, m_mlp_w_down, v_mlp_w_down)
    (g_aout,) = rs_aout.wait([0], grad_x, name="scatter_wait_a_out")
    (g_in,) = rs_in.wait([0], grad_x, name="scatter_wait_a_in")
    update("a_w_in", g_in, a_w_in, m_a_w_in, v_a_w_in)
    update("a_w_out", g_aout, a_w_out, m_a_w_out, v_a_w_out)

    order = ["ada_w", "ada_b", "ln_g", "ln_b", "a_w_in", "a_b_in", "a_vn_g", "a_vn_b", "a_w_s", "a_b_s", "a_w_out", "b_w_qkv", "b_w_out", "mlp_w_up", "mlp_w_down"]
    return (loss, grad_x[None], *[results[n][0] for n in order], *[results[n][1] for n in order],
            *[results[n][2] for n in order], *[results[n][3] for n in order])
```

```python
import math

import jax
import jax.numpy as jnp
from jax import lax
from jax.experimental import pallas as pl
from jax.experimental.pallas import tpu as pltpu

F32 = jnp.float32
BF16 = jnp.bfloat16
MESH = pl.DeviceIdType.MESH
ANY = pl.BlockSpec(memory_space=pl.ANY)
VMEM = pl.BlockSpec(memory_space=pltpu.VMEM)

N_DEV = 8
D_MODEL = 1024
SEQ = 4096
DEPTH = 2
CHUNK = 128
A_GROUPS = 16
A_GROUP_DIM = D_MODEL // A_GROUPS
B_HEADS = 16
B_HEAD_DIM = 64
B_PATTERNS = ((128, 1), (512, 4), (2048, 16))
N_PAT = len(B_PATTERNS)
SPAN = 128
D_FF = 4 * D_MODEL
D_QKV = N_PAT * 3 * D_MODEL
ALPHA = (2 * DEPTH) ** 0.25
LN_EPS = 1e-5
NEG = -1e30
ADAM_LR = 0.001
ADAM_B1 = 0.9
ADAM_B2 = 0.999
ADAM_EPS = 1e-08
ADAM_WD = 0.01
ADAM_STEP = 10
GELU_C = math.sqrt(2.0 / math.pi)
GELU_A = 0.044715

VMEM_LIMIT_BYTES = 56 * 1024 * 1024
LANES = 128
SUBLANES = 8
ROW_TILE = 512
N_SUB = 2 * DEPTH


def _params(sem):
    return pltpu.CompilerParams(dimension_semantics=sem, vmem_limit_bytes=VMEM_LIMIT_BYTES)


def _lane(shape):
    return lax.broadcasted_iota(jnp.int32, shape, len(shape) - 1)


def _split_bf16(x):
    hi = x.astype(BF16)
    lo = (x - hi.astype(F32)).astype(BF16)
    return hi, lo


def _group_expand_matrix(groups_padded, width):
    per = width // A_GROUPS
    r = lax.broadcasted_iota(jnp.int32, (groups_padded, width), 0)
    c = lax.broadcasted_iota(jnp.int32, (groups_padded, width), 1)
    return (c // per == r).astype(BF16)


def _group_reduce_matrix(width, groups_padded):
    per = width // A_GROUPS
    r = lax.broadcasted_iota(jnp.int32, (width, groups_padded), 0)
    c = lax.broadcasted_iota(jnp.int32, (width, groups_padded), 1)
    return (r // per == c).astype(BF16)


def _expand_groups(w):
    e = _group_expand_matrix(LANES, D_MODEL)
    hi, lo = _split_bf16(w)
    return jnp.dot(hi, e, preferred_element_type=F32) + jnp.dot(lo, e, preferred_element_type=F32)


def _reduce_groups(x):
    e = _group_reduce_matrix(D_MODEL, LANES)
    hi, lo = _split_bf16(x)
    return jnp.dot(hi, e, preferred_element_type=F32) + jnp.dot(lo, e, preferred_element_type=F32)


def _to_streams(x, d):
    rows, w = x.shape
    return jnp.swapaxes(x.reshape(rows // d, d, w), 0, 1).reshape(rows, w)


def _from_streams(x, d):
    rows, w = x.shape
    return jnp.swapaxes(x.reshape(d, rows // d, w), 0, 1).reshape(rows, w)


def _column_tiles(p, n, ps, tn):
    assert (ps == 1 or tn == n) and p % ps == 0 and n % tn == 0
    q = n // tn
    return (p // ps) * q, q


def _extra_specs(extras, tm, width):
    specs = []
    for _, kind in extras:
        if kind == "row":
            specs.append(pl.BlockSpec((1, width), lambda i, j, c: (0, j)))
        else:
            specs.append(pl.BlockSpec((tm, width), lambda i, j, c: (i, j)))
    return specs


def mm_nn(a, b3, *, name, tm, ps, tn, tk, epilogue, extras=(), outs, prologue=None, b_tile0=0, b_tiles=None, out_streams=1):
    m, k = a.shape
    p, _, n = b3.shape
    nj, q = _column_tiles(p, n, ps, tn)
    nj = nj if b_tiles is None else b_tiles
    nk = k // tk
    width = ps * tn
    d = out_streams
    assert d == 1 or not extras

    def body(a_ref, b_ref, *rest):
        ex = rest[: len(extras)]
        out_refs = rest[len(extras) : len(extras) + len(outs)]
        kk = pl.program_id(2)
        av = a_ref[...] if prologue is None else prologue(a_ref[...])
        if d > 1:
            av = _to_streams(av, d)

        def finish(cs, acc):
            res = epilogue(acc, *[e[:, cs] for e in ex])
            for o_ref, r in zip(out_refs, res, strict=True):
                if d > 1:
                    o_ref[:, :, cs] = r.astype(o_ref.dtype).reshape(d, tm // d, tn)
                else:
                    o_ref[:, cs] = r.astype(o_ref.dtype)

        for s in range(ps):
            cs = slice(s * tn, (s + 1) * tn)
            part = jnp.dot(av, b_ref[s], preferred_element_type=F32)
            if nk == 1:
                finish(cs, part)
                continue
            acc_ref = rest[-1]

            @pl.when(kk == 0)
            def _(part=part, cs=cs):
                acc_ref[:, cs] = part

            @pl.when(kk > 0)
            def _(part=part, cs=cs):
                acc_ref[:, cs] += part

        if nk > 1:

            @pl.when(kk == nk - 1)
            def _():
                for s in range(ps):
                    cs = slice(s * tn, (s + 1) * tn)
                    finish(cs, rest[-1][:, cs])

    if d > 1:
        out_spec = pl.BlockSpec((d, tm // d, width), lambda i, j, kk: (0, i, j))
        out_shape = (d, m // d, nj * width)
    else:
        out_spec = pl.BlockSpec((tm, width), lambda i, j, kk: (i, j))
        out_shape = (m, nj * width)
    res = pl.pallas_call(
        body,
        name=name,
        grid=(m // tm, nj, nk),
        in_specs=[
            pl.BlockSpec((tm, tk), lambda i, j, kk: (i, kk)),
            pl.BlockSpec((ps, tk, tn), lambda i, j, kk: ((j + b_tile0) // q, kk, (j + b_tile0) % q)),
            *_extra_specs(extras, tm, width),
        ],
        out_specs=[out_spec for _ in outs],
        out_shape=[jax.ShapeDtypeStruct(out_shape, dt) for dt in outs],
        scratch_shapes=[pltpu.VMEM((tm, width), F32)] if nk > 1 else [],
        compiler_params=_params(("parallel", "parallel", "arbitrary")),
    )(a, b3, *[arr for arr, _ in extras])
    return [r.reshape(m, nj * width) for r in res]


def mm_nt(g, b3, *, name, tm, tko, ps, tc, epilogue, extras=(), outs, out_widths=None, b_tile0=0, g_streams=1, stats=False):
    m, width = g.shape
    p, k, n = b3.shape
    _, q = _column_tiles(p, n, ps, tc)
    nc = width // (ps * tc)
    ds = g_streams
    assert not stats or tko == k
    widths = [tko] * len(outs) if out_widths is None else out_widths

    def body(g_ref, b_ref, *rest):
        ex = rest[: len(extras)]
        out_refs = rest[len(extras) : len(extras) + len(outs)]
        c = pl.program_id(2)
        gv = g_ref[...].reshape(tm, ps * tc) if ds > 1 else g_ref[...]
        part = None
        for s in range(ps):
            d = lax.dot_general(gv[:, s * tc : (s + 1) * tc], b_ref[s], _NT, preferred_element_type=F32)
            part = d if part is None else part + d

        def finish(acc):
            res = epilogue(acc, *[e[...] for e in ex])
            if stats:
                *res, sums = res
                st_ref = rest[len(extras) + len(outs)]

                @pl.when(pl.program_id(0) == 0)
                def _():
                    st_ref[...] = jnp.zeros_like(st_ref)

                for row, val in enumerate(sums):
                    st_ref[row : row + 1, :] += val
            for o_ref, r in zip(out_refs, res, strict=True):
                o_ref[...] = r.astype(o_ref.dtype)

        if nc == 1:
            finish(part)
            return
        acc_ref = rest[-1]

        @pl.when(c == 0)
        def _():
            acc_ref[...] = part

        @pl.when(c > 0)
        def _():
            acc_ref[...] += part

        @pl.when(c == nc - 1)
        def _():
            finish(acc_ref[...])

    return pl.pallas_call(
        body,
        name=name,
        grid=(m // tm, k // tko, nc),
        in_specs=[
            pl.BlockSpec((ds, tm // ds, ps * tc), lambda i, j, c: (0, i, c))
            if ds > 1
            else pl.BlockSpec((tm, ps * tc), lambda i, j, c: (i, c)),
            pl.BlockSpec((ps, tko, tc), lambda i, j, c: ((c + b_tile0) // q, j, (c + b_tile0) % q)),
            *_extra_specs(extras, tm, tko),
        ],
        out_specs=[pl.BlockSpec((tm, w), lambda i, j, c: (i, j)) for w in widths]
        + [pl.BlockSpec((SUBLANES, tko), lambda i, j, c: (0, 0))] * stats,
        out_shape=[jax.ShapeDtypeStruct((m, w * (k // tko)), dt) for dt, w in zip(outs, widths, strict=True)]
        + [jax.ShapeDtypeStruct((SUBLANES, k), F32)] * stats,
        scratch_shapes=[pltpu.VMEM((tm, tko), F32)] if nc > 1 else [],
        compiler_params=_params(("arbitrary" if stats else "parallel", "parallel", "arbitrary")),
    )(g.reshape(ds, m // ds, width) if ds > 1 else g, b3, *[arr for arr, _ in extras])


def mm_tn(a, g, *, name, p, tk, ps, tn, tmc, out_dtype, prologue=None, g_streams=1):
    m, k = a.shape
    width = g.shape[1]
    n = width // p
    nj, q = _column_tiles(p, n, ps, tn)
    nc = m // tmc
    ds = g_streams

    def body(a_ref, g_ref, o_ref, acc_ref):
        c = pl.program_id(2)
        av = a_ref[...] if prologue is None else prologue(a_ref[...])
        gv = g_ref[...]
        if ds > 1:
            av, gv = _to_streams(av, ds), gv.reshape(tmc, ps * tn)
        part = lax.dot_general(av, gv, (((0,), (0,)), ((), ())), preferred_element_type=F32)

        @pl.when(c == 0)
        def _():
            acc_ref[...] = part

        @pl.when(c > 0)
        def _():
            acc_ref[...] += part

        @pl.when(c == nc - 1)
        def _():
            for s in range(ps):
                o_ref[s] = acc_ref[:, s * tn : (s + 1) * tn].astype(o_ref.dtype)

    return pl.pallas_call(
        body,
        name=name,
        grid=(k // tk, nj, nc),
        in_specs=[
            pl.BlockSpec((tmc, tk), lambda i, j, c: (c, i)),
            pl.BlockSpec((ds, tmc // ds, ps * tn), lambda i, j, c: (0, c, j))
            if ds > 1
            else pl.BlockSpec((tmc, ps * tn), lambda i, j, c: (c, j)),
        ],
        out_specs=pl.BlockSpec((ps, tk, tn), lambda i, j, c: (j // q, i, j % q)),
        out_shape=jax.ShapeDtypeStruct((p, k, n), out_dtype),
        scratch_shapes=[pltpu.VMEM((tk, ps * tn), F32)],
        compiler_params=_params(("parallel", "parallel", "arbitrary")),
    )(a, g.reshape(ds, m // ds, width) if ds > 1 else g)


def _rows(cols):
    return pl.BlockSpec((ROW_TILE, cols), lambda i: (i, 0))


def _vec(cols, rows=1):
    return pl.BlockSpec((rows, cols), lambda i: (0, 0))


def _layer_norm_hat(z):
    mu = jnp.mean(z, axis=-1, keepdims=True)
    zc = z - mu
    var = jnp.mean(zc * zc, axis=-1, keepdims=True)
    rstd = lax.rsqrt(var + LN_EPS)
    return zc * rstd, rstd


def modulate(x, scale, shift, *, name):
    s, d = x.shape

    def body(x_ref, sc_ref, sh_ref, h_ref):
        h_ref[...] = (x_ref[...] * (1.0 + sc_ref[...]) + sh_ref[...]).astype(BF16)

    return pl.pallas_call(
        body,
        name=name,
        grid=(s // ROW_TILE,),
        in_specs=[_rows(d), _vec(d), _vec(d)],
        out_specs=_rows(d),
        out_shape=jax.ShapeDtypeStruct((s, d), BF16),
        compiler_params=_params(("parallel",)),
    )(x, scale, shift)


def residual_ln_epilogue(acc, x, gate1, g, b, nscale, nshift):
    y = acc.astype(BF16)
    xhat, _ = _layer_norm_hat(ALPHA * x + gate1 * y.astype(F32))
    xn = xhat * g + b
    return y, xn, xn * (1.0 + nscale) + nshift


ST_DSCALE, ST_DSHIFT, ST_DG, ST_DB, ST_DGATE = 0, 1, 2, 3, 4


def _layer_norm_bwd(g_out, xhat, rstd, g):
    dxh = g_out * g
    m1 = jnp.mean(dxh, axis=-1, keepdims=True)
    m2 = jnp.mean(dxh * xhat, axis=-1, keepdims=True)
    return rstd * (dxh - m1 - xhat * m2)


def _column_sums(vals):
    return [jnp.sum(v, axis=0, keepdims=True) for v in vals]


def residual_bwd_epilogue(dh, dzl, scl, x, y, gate1, g, b):
    yf = y.astype(F32)
    xhat, rstd = _layer_norm_hat(ALPHA * x + gate1 * yf)
    g_out = ALPHA * dzl + dh * (1.0 + scl)
    dz = _layer_norm_bwd(g_out, xhat, rstd, g)
    return dz, dz * gate1, _column_sums([dh * (xhat * g + b), dh, g_out * xhat, g_out, dz * yf])


def input_bwd_epilogue(dh, dzl, scl, x):
    return ALPHA * dzl + dh * (1.0 + scl), _column_sums([dh * x, dh])


def residual_ln_loss_bwd(x, y, gate1, g, b, target, *, name):
    s, d = x.shape

    def body(x_ref, y_ref, gt_ref, g_ref, b_ref, t_ref, dz_ref, dy_ref, st_ref, loss_ref):
        y = y_ref[...].astype(F32)
        gate1 = gt_ref[...]
        xhat, rstd = _layer_norm_hat(ALPHA * x_ref[...] + gate1 * y)
        err = xhat * g_ref[...] + b_ref[...] - t_ref[...]
        g_out = err * (1.0 / d)
        dz = _layer_norm_bwd(g_out, xhat, rstd, g_ref[...])
        dz_ref[...] = dz
        dy_ref[...] = (dz * gate1).astype(BF16)
        part = jnp.sum(jnp.sum(err * err, axis=1, keepdims=True), axis=0, keepdims=True) * (0.5 / d)

        @pl.when(pl.program_id(0) == 0)
        def _():
            st_ref[...] = jnp.zeros_like(st_ref)
            loss_ref[...] = jnp.zeros_like(loss_ref)

        loss_ref[...] += part
        for row, val in zip((ST_DG, ST_DB, ST_DGATE), _column_sums([g_out * xhat, g_out, dz * y]), strict=True):
            st_ref[row : row + 1, :] += val

    return pl.pallas_call(
        body,
        name=name,
        grid=(s // ROW_TILE,),
        in_specs=[_rows(d), _rows(d), _vec(d), _vec(d), _vec(d), _rows(d)],
        out_specs=[_rows(d), _rows(d), _vec(d, SUBLANES), pl.BlockSpec((1, 1), lambda i: (0, 0))],
        out_shape=[
            jax.ShapeDtypeStruct((s, d), F32),
            jax.ShapeDtypeStruct((s, d), BF16),
            jax.ShapeDtypeStruct((SUBLANES, d), F32),
            jax.ShapeDtypeStruct((1, 1), F32),
        ],
        compiler_params=_params(("arbitrary",)),
    )(x, y, gate1, g, b, target)


def residual_ln_bwd(*, name, later=None, dxo=None, x_out=None, this=None):
    lead = later[0] if later is not None else dxo
    s, d = lead.shape
    has_later, has_ln = later is not None, this is not None

    def body(*refs):
        refs = list(refs)
        if has_later:
            dh_ref, dzl_ref, scl_ref = refs[:3]
            refs = refs[3:]
        else:
            dxo_ref = refs.pop(0)
        if has_ln:
            x_ref, y_ref, gt_ref, g_ref, b_ref = refs[:5]
            refs = refs[5:]
            dz_ref, dy_ref, st_ref = refs
        else:
            xo_ref, dx_ref, st_ref = refs

        @pl.when(pl.program_id(0) == 0)
        def _():
            st_ref[...] = jnp.zeros_like(st_ref)

        def acc(row, val):
            st_ref[row : row + 1, :] += jnp.sum(val, axis=0, keepdims=True)

        if has_ln:
            y = y_ref[...].astype(F32)
            gate1 = gt_ref[...]
            xhat, rstd = _layer_norm_hat(ALPHA * x_ref[...] + gate1 * y)
            x_out_v = xhat * g_ref[...] + b_ref[...]
        else:
            x_out_v = xo_ref[...]
        if has_later:
            dh = dh_ref[...]
            g_out = ALPHA * dzl_ref[...] + dh * (1.0 + scl_ref[...])
            acc(ST_DSCALE, dh * x_out_v)
            acc(ST_DSHIFT, dh)
        else:
            g_out = dxo_ref[...]
        if not has_ln:
            dx_ref[...] = g_out
            return
        acc(ST_DG, g_out * xhat)
        acc(ST_DB, g_out)
        dxh = g_out * g_ref[...]
        m1 = jnp.mean(dxh, axis=-1, keepdims=True)
        m2 = jnp.mean(dxh * xhat, axis=-1, keepdims=True)
        dz = rstd * (dxh - m1 - xhat * m2)
        acc(ST_DGATE, dz * y)
        dz_ref[...] = dz
        dy_ref[...] = (dz * gate1).astype(BF16)

    ins, specs = [], []
    if has_later:
        ins += list(later)
        specs += [_rows(d), _rows(d), _vec(d)]
    else:
        ins += [dxo]
        specs += [_rows(d)]
    if not has_ln:
        ins += [x_out]
        specs += [_rows(d)]
    if has_ln:
        ins += list(this)
        specs += [_rows(d), _rows(d), _vec(d), _vec(d), _vec(d)]
        out_specs = [_rows(d), _rows(d), _vec(d, SUBLANES)]
        out_shape = [
            jax.ShapeDtypeStruct((s, d), F32),
            jax.ShapeDtypeStruct((s, d), BF16),
            jax.ShapeDtypeStruct((SUBLANES, d), F32),
        ]
    else:
        out_specs = [_rows(d), _vec(d, SUBLANES)]
        out_shape = [jax.ShapeDtypeStruct((s, d), F32), jax.ShapeDtypeStruct((SUBLANES, d), F32)]
    return pl.pallas_call(
        body,
        name=name,
        grid=(s // ROW_TILE,),
        in_specs=specs,
        out_specs=out_specs,
        out_shape=out_shape,
        compiler_params=_params(("arbitrary",)),
    )(*ins)


GATE_CHUNKS = 4


def _gelu(x, with_grad=False):
    x2 = x * x
    t = jnp.tanh(GELU_C * (x + GELU_A * x2 * x))
    half = 0.5 * (1.0 + t)
    y = x * half
    if not with_grad:
        return y
    return y, half + 0.5 * x * (1.0 - t * t) * (GELU_C * (1.0 + 3.0 * GELU_A * x2))


def _causal_weights(w_ref, transpose):
    t = lax.broadcasted_iota(jnp.int32, (CHUNK, CHUNK), 0)
    s = lax.broadcasted_iota(jnp.int32, (CHUNK, CHUNK), 1)
    out = []
    for g in range(A_GROUPS):
        w = jnp.where(t >= s, w_ref[g], 0.0)
        out.append((w.T if transpose else w).astype(BF16))
    return out


def _spatial(ws, vn, lo_mask):
    rows = vn.shape[0]
    out_rows = []
    for r in range(rows // CHUNK):
        cols = []
        for j in range(A_GROUPS // 2):
            blk = vn[r * CHUNK : (r + 1) * CHUNK, j * LANES : (j + 1) * LANES]
            za = jnp.dot(ws[2 * j], blk, preferred_element_type=F32)
            zb = jnp.dot(ws[2 * j + 1], blk, preferred_element_type=F32)
            cols.append(jnp.where(lo_mask, za, zb))
        out_rows.append(jnp.concatenate(cols, axis=1))
    return jnp.concatenate(out_rows, axis=0)


def _gate_forward(a, vg, vb, ws, bias, lo_mask, with_grad=False):
    u = _gelu(a[:, :D_MODEL], with_grad)
    v = _gelu(a[:, D_MODEL:], with_grad)
    gu, gv = None, None
    if with_grad:
        (u, gu), (v, gv) = u, v
    vhat, rstd = _layer_norm_hat(v)
    vn = (vhat * vg + vb).astype(BF16)
    z = _spatial(ws, vn, lo_mask) + jnp.concatenate([bias] * (a.shape[0] // CHUNK), axis=0)
    return u, vhat, rstd, vn, z, gu, gv


def gate_fwd(a_pre, vn_g, vn_b, w_s, bias_full, *, name):
    s = a_pre.shape[0]
    tr = GATE_CHUNKS * CHUNK

    def body(a_ref, vg_ref, vb_ref, w_ref, bias_ref, p_ref):
        lo_mask = _lane((CHUNK, LANES)) < A_GROUP_DIM
        ws = _causal_weights(w_ref, transpose=False)
        u, _, _, _, z, _, _ = _gate_forward(a_ref[...].astype(F32), vg_ref[...], vb_ref[...], ws, bias_ref[...], lo_mask)
        p_ref[...] = (u * z).astype(BF16)

    return pl.pallas_call(
        body,
        name=name,
        grid=(s // tr,),
        in_specs=[
            pl.BlockSpec((tr, 2 * D_MODEL), lambda i: (i, 0)),
            _vec(D_MODEL),
            _vec(D_MODEL),
            pl.BlockSpec((A_GROUPS, CHUNK, CHUNK), lambda i: (0, 0, 0)),
            _vec(D_MODEL, CHUNK),
        ],
        out_specs=pl.BlockSpec((tr, D_MODEL), lambda i: (i, 0)),
        out_shape=jax.ShapeDtypeStruct((s, D_MODEL), BF16),
        compiler_params=_params(("parallel",)),
    )(a_pre, vn_g, vn_b, w_s, bias_full)


def gate_bwd(a_pre, dp, vn_g, vn_b, w_s, bias_full, *, name):
    s = a_pre.shape[0]
    tr = GATE_CHUNKS * CHUNK
    nsteps = s // tr

    def body(a_ref, dp_ref, vg_ref, vb_ref, w_ref, bias_ref, da_ref, dw_ref, dbs_ref, rows_ref, dbias_acc):
        step = pl.program_id(0)
        lo_mask = _lane((CHUNK, LANES)) < A_GROUP_DIM

        @pl.when(step == 0)
        def _():
            dw_ref[...] = jnp.zeros_like(dw_ref)
            rows_ref[...] = jnp.zeros_like(rows_ref)
            dbias_acc[...] = jnp.zeros_like(dbias_acc)

        a = a_ref[...].astype(F32)
        vg = vg_ref[...]
        ws = _causal_weights(w_ref, transpose=False)
        wts = _causal_weights(w_ref, transpose=True)
        u, vhat, rstd, vn, z, gelu_du, gelu_dv = _gate_forward(a, vg, vb_ref[...], ws, bias_ref[...], lo_mask, True)
        dp = dp_ref[...]
        du = dp * z
        dzz = dp * u
        dzz_b = dzz.astype(BF16)
        dvn = _spatial(wts, dzz_b, lo_mask)
        dbias = None
        for r in range(GATE_CHUNKS):
            rs = slice(r * CHUNK, (r + 1) * CHUNK)
            dbias = dzz[rs] if dbias is None else dbias + dzz[rs]
            for j in range(A_GROUPS // 2):
                cs = slice(j * LANES, (j + 1) * LANES)
                dblk = dzz[rs, cs]
                vblk = vn[rs, cs]
                for half in range(2):
                    keep = lo_mask if half == 0 else jnp.logical_not(lo_mask)
                    dm = jnp.where(keep, dblk, 0.0).astype(BF16)
                    dw_ref[2 * j + half] += lax.dot_general(
                        dm, vblk, (((1,), (1,)), ((), ())), preferred_element_type=F32
                    )
        dbias_acc[...] += dbias
        rows_ref[1:2, :D_MODEL] += jnp.sum(dvn * vhat, axis=0, keepdims=True)
        rows_ref[1:2, D_MODEL:] += jnp.sum(dvn, axis=0, keepdims=True)
        dvh = dvn * vg
        m1 = jnp.mean(dvh, axis=-1, keepdims=True)
        m2 = jnp.mean(dvh * vhat, axis=-1, keepdims=True)
        dv = rstd * (dvh - m1 - vhat * m2)
        da_u = du * gelu_du
        da_v = dv * gelu_dv
        da_ref[:, :D_MODEL] = da_u.astype(BF16)
        da_ref[:, D_MODEL:] = da_v.astype(BF16)
        rows_ref[0:1, :D_MODEL] += jnp.sum(da_u, axis=0, keepdims=True)
        rows_ref[0:1, D_MODEL:] += jnp.sum(da_v, axis=0, keepdims=True)

        @pl.when(step == nsteps - 1)
        def _():
            t = lax.broadcasted_iota(jnp.int32, (CHUNK, CHUNK), 0)
            sx = lax.broadcasted_iota(jnp.int32, (CHUNK, CHUNK), 1)
            for g in range(A_GROUPS):
                dw_ref[g] = jnp.where(t >= sx, dw_ref[g], 0.0)
            dbs_ref[...] = _reduce_groups(dbias_acc[...])

    return pl.pallas_call(
        body,
        name=name,
        grid=(nsteps,),
        in_specs=[
            pl.BlockSpec((tr, 2 * D_MODEL), lambda i: (i, 0)),
            pl.BlockSpec((tr, D_MODEL), lambda i: (i, 0)),
            _vec(D_MODEL),
            _vec(D_MODEL),
            pl.BlockSpec((A_GROUPS, CHUNK, CHUNK), lambda i: (0, 0, 0)),
            _vec(D_MODEL, CHUNK),
        ],
        out_specs=[
            pl.BlockSpec((tr, 2 * D_MODEL), lambda i: (i, 0)),
            pl.BlockSpec((A_GROUPS, CHUNK, CHUNK), lambda i: (0, 0, 0)),
            _vec(LANES, CHUNK),
            _vec(2 * D_MODEL, SUBLANES),
        ],
        out_shape=[
            jax.ShapeDtypeStruct((s, 2 * D_MODEL), BF16),
            jax.ShapeDtypeStruct((A_GROUPS, CHUNK, CHUNK), F32),
            jax.ShapeDtypeStruct((CHUNK, LANES), F32),
            jax.ShapeDtypeStruct((SUBLANES, 2 * D_MODEL), F32),
        ],
        scratch_shapes=[pltpu.VMEM((CHUNK, D_MODEL), F32)],
        compiler_params=_params(("arbitrary",)),
    )(a_pre, dp, vn_g, vn_b, w_s, bias_full)


def alibi_tables(dilation):
    qi = jnp.arange(SPAN)[:, None]
    ki = jnp.arange(2 * SPAN)[None, :]
    diff = SPAN + qi - ki
    valid = (diff >= 0) & (diff <= SPAN)
    heads = jnp.arange(1, B_HEADS + 1, dtype=F32)
    slopes = jnp.exp2(-8.0 * heads / B_HEADS)
    bias = -slopes[:, None, None] * (dilation * diff).astype(F32)
    bias = jnp.where(valid[None], bias, NEG).reshape(B_HEADS // 2, 2 * SPAN, 2 * SPAN)
    return bias, bias.transpose(0, 2, 1)


def _pair_rows(x, halves):
    return jnp.concatenate([x * halves[0], x * halves[1]], axis=0)


def _pair_column(v, lane, j):
    pick = lambda h: jnp.sum(jnp.where(lane == h, v, 0.0), axis=1, keepdims=True)
    return jnp.concatenate([pick(2 * j), pick(2 * j + 1)], axis=0)


_NT = (((1,), (1,)), ((), ()))


def permute_rows(xs, dilation, *, inverse, name):
    s = xs[0].shape[0]
    tile = SPAN * dilation
    nat = [pl.BlockSpec((tile, x.shape[1]), lambda i: (i, 0)) for x in xs]
    streams = [pl.BlockSpec((dilation, SPAN, x.shape[1]), lambda i: (0, i, 0)) for x in xs]
    stream_shape = [jax.ShapeDtypeStruct((dilation, s // dilation, x.shape[1]), x.dtype) for x in xs]

    def body(*refs):
        for x_ref, o_ref in zip(refs[: len(xs)], refs[len(xs) :], strict=True):
            w = x_ref.shape[-1]
            if inverse:
                o_ref[...] = jnp.swapaxes(x_ref[...], 0, 1).reshape(tile, w)
            else:
                o_ref[...] = jnp.swapaxes(x_ref[...].reshape(SPAN, dilation, w), 0, 1)

    outs = pl.pallas_call(
        body,
        name=name,
        grid=(s // tile,),
        in_specs=streams if inverse else nat,
        out_specs=nat if inverse else streams,
        out_shape=[jax.ShapeDtypeStruct(x.shape, x.dtype) for x in xs] if inverse else stream_shape,
        compiler_params=_params(("parallel",)),
    )(*[x.reshape(dilation, s // dilation, x.shape[1]) if inverse else x for x in xs])
    return [o.reshape(x.shape) for o, x in zip(outs, xs, strict=True)]


def _qkv_specs(block_of):
    def spec(which, prev):
        def index(*grid):
            blk = block_of(*grid)
            return (jnp.maximum(blk - 1, 0) if prev else blk, which)

        return pl.BlockSpec((SPAN, D_MODEL), index)

    return [spec(0, False), spec(1, True), spec(1, False), spec(2, True), spec(2, False)]


def attn_fwd(qkv_p, pat, *, name, after=None):
    _, dilation = B_PATTERNS[pat]
    nb = SEQ // dilation // SPAN
    bias, _ = alibi_tables(dilation)
    order = [] if after is None else [after]

    def body(q_ref, kp_ref, kc_ref, vp_ref, vc_ref, bias_ref, *rest):
        o_ref, lse_ref = rest[len(order) :]
        n = pl.program_id(1)
        first_prev = jnp.logical_and(n == 0, _lane((2 * SPAN, 2 * SPAN)) < SPAN)
        lane = _lane((SPAN, LANES))
        lo_mask = lane < B_HEAD_DIM
        q = q_ref[...] * jnp.asarray(B_HEAD_DIM**-0.5, BF16)
        kk = jnp.concatenate([kp_ref[...], kc_ref[...]], axis=0)
        vv = jnp.concatenate([vp_ref[...], vc_ref[...]], axis=0)
        halves = (lo_mask.astype(BF16), jnp.logical_not(lo_mask).astype(BF16))
        stats = jnp.zeros((SPAN, LANES), F32)
        for j in range(B_HEADS // 2):
            cs = slice(j * LANES, (j + 1) * LANES)
            sc = lax.dot_general(_pair_rows(q[:, cs], halves), kk[:, cs], _NT, preferred_element_type=F32)
            sc = jnp.where(first_prev, NEG, sc + bias_ref[j])
            m = jnp.max(sc, axis=1, keepdims=True)
            p = jnp.exp(sc - m)
            l = jnp.sum(p, axis=1, keepdims=True)
            acc = jnp.dot(p.astype(BF16), vv[:, cs], preferred_element_type=F32) * (1.0 / l)
            lse_pair = m + jnp.log(l)
            o_ref[:, cs] = jnp.where(lo_mask, acc[:SPAN], acc[SPAN:]).astype(BF16)
            stats = jnp.where(lane == 2 * j, lse_pair[:SPAN], stats)
            stats = jnp.where(lane == 2 * j + 1, lse_pair[SPAN:], stats)
        lse_ref[...] = stats

    return pl.pallas_call(
        body,
        name=name,
        grid=(dilation, nb),
        in_specs=[
            *_qkv_specs(lambda r, n: r * nb + n),
            pl.BlockSpec((B_HEADS // 2, 2 * SPAN, 2 * SPAN), lambda r, n: (0, 0, 0)),
            *[ANY for _ in order],
        ],
        out_specs=[
            pl.BlockSpec((SPAN, D_MODEL), lambda r, n: (r * nb + n, 0)),
            pl.BlockSpec((SPAN, LANES), lambda r, n: (r * nb + n, 0)),
        ],
        out_shape=[jax.ShapeDtypeStruct((SEQ, D_MODEL), BF16), jax.ShapeDtypeStruct((SEQ, LANES), F32)],
        compiler_params=_params(("parallel", "arbitrary")),
    )(qkv_p, qkv_p, qkv_p, qkv_p, qkv_p, bias, *order)


def attn_combine(outs, lses, *, name):
    dils = [d for _, d in B_PATTERNS]

    def positions(ref, d):
        x = ref[...].astype(F32)
        return x if d == 1 else jnp.swapaxes(x, 0, 1).reshape(ROW_TILE, x.shape[-1])

    def body(o0, o1, o2, l0, l1, l2, ob_ref, of_ref, lse_ref):
        ls = [positions(l, d) for l, d in zip((l0, l1, l2), dils, strict=True)]
        m = jnp.maximum(jnp.maximum(ls[0], ls[1]), ls[2])
        tot = jnp.log(jnp.exp(ls[0] - m) + jnp.exp(ls[1] - m) + jnp.exp(ls[2] - m)) + m
        o = None
        for o_ref, l, d in zip((o0, o1, o2), ls, dils, strict=True):
            term = _expand_groups(jnp.exp(l - tot)) * positions(o_ref, d)
            o = term if o is None else o + term
        ob_ref[...] = o.astype(BF16)
        of_ref[...] = o
        lse_ref[...] = tot

    def stream_rows(cols, d):
        return _rows(cols) if d == 1 else pl.BlockSpec((d, ROW_TILE // d, cols), lambda i: (0, i, 0))

    def streams(x, d):
        return x if d == 1 else x.reshape(d, SEQ // d, x.shape[-1])

    outs = [streams(x, d) for x, d in zip(outs, dils, strict=True)]
    lses = [streams(x, d) for x, d in zip(lses, dils, strict=True)]
    return pl.pallas_call(
        body,
        name=name,
        grid=(SEQ // ROW_TILE,),
        in_specs=[stream_rows(D_MODEL, d) for d in dils] + [stream_rows(LANES, d) for d in dils],
        out_specs=[_rows(D_MODEL), _rows(D_MODEL), _rows(LANES)],
        out_shape=[
            jax.ShapeDtypeStruct((SEQ, D_MODEL), BF16),
            jax.ShapeDtypeStruct((SEQ, D_MODEL), F32),
            jax.ShapeDtypeStruct((SEQ, LANES), F32),
        ],
        compiler_params=_params(("parallel",)),
    )(*outs, *lses)


def attn_delta(do, o, *, name):
    def body(do_ref, o_ref, dob_ref, dl_ref):
        do_v = do_ref[...]
        dob_ref[...] = do_v.astype(BF16)
        dl_ref[...] = _reduce_groups(do_v * o_ref[...])

    return pl.pallas_call(
        body,
        name=name,
        grid=(SEQ // ROW_TILE,),
        in_specs=[_rows(D_MODEL), _rows(D_MODEL)],
        out_specs=[_rows(D_MODEL), _rows(LANES)],
        out_shape=[jax.ShapeDtypeStruct((SEQ, D_MODEL), BF16), jax.ShapeDtypeStruct((SEQ, LANES), F32)],
        compiler_params=_params(("parallel",)),
    )(do, o)


def attn_bwd(qkv_p, do_p, lse_p, delta_p, pat, *, name):
    _, dilation = B_PATTERNS[pat]
    nb = SEQ // dilation // SPAN
    n_blocks = SEQ // SPAN
    bias, bias_t = alibi_tables(dilation)
    last = n_blocks - 1
    q_cols, k_cols, v_cols = (slice(i * D_MODEL, (i + 1) * D_MODEL) for i in range(3))

    def body(q_ref, kp_ref, kc_ref, vp_ref, vc_ref, do_ref, lse_ref, dl_ref, bias_ref, biast_ref, out_ref, cq_ref, ck_ref, cv_ref):
        g = pl.program_id(0)

        @pl.when(g == n_blocks)
        def _():
            out_ref[:, q_cols] = cq_ref[...].astype(BF16)
            out_ref[:, k_cols] = ck_ref[...].astype(BF16)
            out_ref[:, v_cols] = cv_ref[...].astype(BF16)

        @pl.when(g == 0)
        def _():
            cq_ref[...] = jnp.zeros_like(cq_ref)
            ck_ref[...] = jnp.zeros_like(ck_ref)
            cv_ref[...] = jnp.zeros_like(cv_ref)

        @pl.when(g < n_blocks)
        def _():
            lane = _lane((SPAN, LANES))
            lo_mask = lane < B_HEAD_DIM
            pair = (2 * SPAN, 2 * SPAN)
            first = lax.rem(g, nb) == 0
            prev_key_cols = jnp.logical_and(first, _lane(pair) < SPAN)
            prev_key_rows = jnp.logical_and(first, lax.broadcasted_iota(jnp.int32, pair, 0) < SPAN)
            q = q_ref[...] * jnp.asarray(B_HEAD_DIM**-0.5, BF16)
            kk = jnp.concatenate([kp_ref[...], kc_ref[...]], axis=0)
            vv = jnp.concatenate([vp_ref[...], vc_ref[...]], axis=0)
            do_v = do_ref[...]
            lse_v = lse_ref[...]
            dl_v = dl_ref[...]
            lse_t = lse_v.T
            dl_t = dl_v.T
            halves = (lo_mask.astype(BF16), jnp.logical_not(lo_mask).astype(BF16))
            for j in range(B_HEADS // 2):
                cs = slice(j * LANES, (j + 1) * LANES)
                kp, vp = kk[:, cs], vv[:, cs]
                q2 = _pair_rows(q[:, cs], halves)
                do2 = _pair_rows(do_v[:, cs], halves)
                lse_c, dl_c = _pair_column(lse_v, lane, j), _pair_column(dl_v, lane, j)
                lse_r = jnp.concatenate([lse_t[2 * j : 2 * j + 1], lse_t[2 * j + 1 : 2 * j + 2]], axis=1)
                dl_r = jnp.concatenate([dl_t[2 * j : 2 * j + 1], dl_t[2 * j + 1 : 2 * j + 2]], axis=1)
                sc = lax.dot_general(q2, kp, _NT, preferred_element_type=F32)
                p = jnp.exp(jnp.where(prev_key_cols, NEG, sc + bias_ref[j]) - lse_c)
                dp = lax.dot_general(do2, vp, _NT, preferred_element_type=F32)
                ds = (p * (dp - dl_c)).astype(BF16)
                dq2 = jnp.dot(ds, kp, preferred_element_type=F32)
                sc_t = lax.dot_general(kp, q2, _NT, preferred_element_type=F32)
                p_t = jnp.exp(jnp.where(prev_key_rows, NEG, sc_t + biast_ref[j]) - lse_r)
                dp_t = lax.dot_general(vp, do2, _NT, preferred_element_type=F32)
                ds_t = (p_t * (dp_t - dl_r)).astype(BF16)
                dk_pair = jnp.dot(ds_t, q2, preferred_element_type=F32)
                dv_pair = jnp.dot(p_t.astype(BF16), do2, preferred_element_type=F32)
                oq = slice(j * LANES, (j + 1) * LANES)
                ok = slice(D_MODEL + j * LANES, D_MODEL + (j + 1) * LANES)
                ov = slice(2 * D_MODEL + j * LANES, 2 * D_MODEL + (j + 1) * LANES)
                out_ref[:, oq] = cq_ref[:, cs].astype(BF16)
                out_ref[:, ok] = (ck_ref[:, cs] + dk_pair[:SPAN]).astype(BF16)
                out_ref[:, ov] = (cv_ref[:, cs] + dv_pair[:SPAN]).astype(BF16)
                cq_ref[:, cs] = jnp.where(lo_mask, dq2[:SPAN], dq2[SPAN:]) * (B_HEAD_DIM**-0.5)
                ck_ref[:, cs] = dk_pair[SPAN:]
                cv_ref[:, cs] = dv_pair[SPAN:]

    def block_of(g):
        return jnp.minimum(g, last)

    def row_spec(width):
        return pl.BlockSpec((SPAN, width), lambda g: (block_of(g), 0))

    return pl.pallas_call(
        body,
        name=name,
        grid=(n_blocks + 1,),
        in_specs=[
            *_qkv_specs(block_of),
            row_spec(D_MODEL),
            row_spec(LANES),
            row_spec(LANES),
            pl.BlockSpec((B_HEADS // 2, 2 * SPAN, 2 * SPAN), lambda g: (0, 0, 0)),
            pl.BlockSpec((B_HEADS // 2, 2 * SPAN, 2 * SPAN), lambda g: (0, 0, 0)),
        ],
        out_specs=pl.BlockSpec((SPAN, 3 * D_MODEL), lambda g: (jnp.maximum(g - 1, 0), 0)),
        out_shape=jax.ShapeDtypeStruct((SEQ, 3 * D_MODEL), BF16),
        scratch_shapes=[pltpu.VMEM((SPAN, D_MODEL), F32)] * 3,
        compiler_params=_params(("arbitrary",)),
    )(qkv_p, qkv_p, qkv_p, qkv_p, qkv_p, do_p, lse_p, delta_p, bias, bias_t)


def _position():
    x, y, c = lax.axis_index("x"), lax.axis_index("y"), lax.axis_index("c")
    return x, y, c, 4 * x + 2 * y + c


def _peer(k, x, y, c):
    px = 1 - x if k & 4 else x
    py = 1 - y if k & 2 else y
    pc = 1 - c if k & 1 else c
    return (px, py, pc), 4 * px + 2 * py + pc


def _remote(src, dst, send_sem, recv_sem, device):
    return pltpu.make_async_remote_copy(
        src_ref=src, dst_ref=dst, send_sem=send_sem, recv_sem=recv_sem, device_id=device, device_id_type=MESH
    )


def _silu_bf16(cf):
    return (cf * (1.0 / (1.0 + jnp.exp(-cf)))).astype(BF16)


def ada_exchange(c8, w4, b4, ln8):
    nt, _, ncol = w4.shape

    def body(c8_ref, w_ref, b_ref, ln_ref, cg_ref, lng_ref, mrecv_ref, mloc_ref, send_sems, recv_sems):
        x, y, c, me = _position()
        cg_ref[me] = c8_ref[...]
        lng_ref[me] = ln_ref[...]
        first = []
        for k in range(1, N_DEV):
            dev, _ = _peer(k, x, y, c)
            first.append(_remote(c8_ref, cg_ref.at[me], send_sems.at[0, k], recv_sems.at[0, k], dev))
            first.append(_remote(ln_ref, lng_ref.at[me], send_sems.at[1, k], recv_sems.at[1, k], dev))
        for cp in first:
            cp.start()
        for k in range(1, N_DEV):
            dev, pid = _peer(k, x, y, c)
            _remote(c8_ref, cg_ref.at[pid], send_sems.at[0, k], recv_sems.at[0, k], dev).wait_recv()
            _remote(ln_ref, lng_ref.at[pid], send_sems.at[1, k], recv_sems.at[1, k], dev).wait_recv()
        sc = _silu_bf16(cg_ref[...].reshape(N_DEV * SUBLANES, D_MODEL))
        for t in range(nt):
            mloc_ref[t] = jnp.dot(sc, w_ref[t].astype(BF16), preferred_element_type=F32) + b_ref[t : t + 1, :]

        def group(dev_id):
            return pl.ds(pl.multiple_of(dev_id * SUBLANES, SUBLANES), SUBLANES)

        mrecv_ref[me] = mloc_ref[:, group(me), :]
        second = []
        for k in range(1, N_DEV):
            dev, pid = _peer(k, x, y, c)
            second.append(
                _remote(mloc_ref.at[:, group(pid), :], mrecv_ref.at[me], send_sems.at[2, k], recv_sems.at[2, k], dev)
            )
        for cp in second:
            cp.start()
        for k in range(1, N_DEV):
            dev, pid = _peer(k, x, y, c)
            _remote(
                mloc_ref.at[:, group(pid), :], mrecv_ref.at[pid], send_sems.at[2, k], recv_sems.at[2, k], dev
            ).wait_recv()
        for cp in first + second:
            cp.wait_send()

    return pl.pallas_call(
        body,
        name="ada_exchange",
        in_specs=[VMEM, VMEM, VMEM, VMEM],
        out_specs=[VMEM, VMEM, VMEM],
        out_shape=[
            jax.ShapeDtypeStruct((N_DEV, SUBLANES, D_MODEL), F32),
            jax.ShapeDtypeStruct((N_DEV, SUBLANES, LANES), F32),
            jax.ShapeDtypeStruct((N_DEV, nt, SUBLANES, ncol), F32),
        ],
        scratch_shapes=[
            pltpu.VMEM((nt, N_DEV * SUBLANES, ncol), F32),
            pltpu.SemaphoreType.DMA((3, N_DEV)),
            pltpu.SemaphoreType.DMA((3, N_DEV)),
        ],
        compiler_params=pltpu.CompilerParams(vmem_limit_bytes=VMEM_LIMIT_BYTES),
    )(c8, w4, b4, ln8)


def small_exchange(dmx, flat):
    def body(dmx_ref, flat_ref, dmrecv_ref, red_ref, land_ref, send_sems, recv_sems):
        x, y, c, me = _position()
        dmrecv_ref[me] = dmx_ref[me]
        land_ref[me] = flat_ref[me]
        first = []
        for k in range(1, N_DEV):
            dev, pid = _peer(k, x, y, c)
            first.append(_remote(dmx_ref.at[pid], dmrecv_ref.at[me], send_sems.at[0, k], recv_sems.at[0, k], dev))
            first.append(_remote(flat_ref.at[pid], land_ref.at[me], send_sems.at[1, k], recv_sems.at[1, k], dev))
        for cp in first:
            cp.start()
        for k in range(1, N_DEV):
            dev, pid = _peer(k, x, y, c)
            _remote(dmx_ref.at[pid], dmrecv_ref.at[pid], send_sems.at[0, k], recv_sems.at[0, k], dev).wait_recv()
            _remote(flat_ref.at[pid], land_ref.at[pid], send_sems.at[1, k], recv_sems.at[1, k], dev).wait_recv()
        total = land_ref[0]
        for s in range(1, N_DEV):
            total = total + land_ref[s]
        red_ref[me] = total
        second = []
        for k in range(1, N_DEV):
            dev, _ = _peer(k, x, y, c)
            second.append(_remote(red_ref.at[me], red_ref.at[me], send_sems.at[2, k], recv_sems.at[2, k], dev))
        for cp in second:
            cp.start()
        for k in range(1, N_DEV):
            dev, pid = _peer(k, x, y, c)
            _remote(red_ref.at[pid], red_ref.at[pid], send_sems.at[2, k], recv_sems.at[2, k], dev).wait_recv()
        for cp in first + second:
            cp.wait_send()

    return pl.pallas_call(
        body,
        name="small_exchange",
        in_specs=[VMEM, VMEM],
        out_specs=[VMEM, VMEM],
        out_shape=[jax.ShapeDtypeStruct(dmx.shape, F32), jax.ShapeDtypeStruct(flat.shape, F32)],
        scratch_shapes=[
            pltpu.VMEM(flat.shape, F32),
            pltpu.SemaphoreType.DMA((3, N_DEV)),
            pltpu.SemaphoreType.DMA((3, N_DEV)),
        ],
        compiler_params=pltpu.CompilerParams(vmem_limit_bytes=VMEM_LIMIT_BYTES),
    )(dmx, flat)


HBM = pl.BlockSpec(memory_space=pltpu.HBM)
SEM = pl.BlockSpec(memory_space=pltpu.SEMAPHORE)
EFFECT = pltpu.SideEffectType.DATAFLOW_SIDE_EFFECTING


REGROUP_ROWS = 256


def shards_to_columns(x, *, name):
    p, k, n = x.shape

    def body(x_ref, o_ref):
        for s in range(p):
            o_ref[:, s * n : (s + 1) * n] = x_ref[s]

    return pl.pallas_call(
        body,
        name=name,
        grid=(k // REGROUP_ROWS,),
        in_specs=[pl.BlockSpec((p, REGROUP_ROWS, n), lambda i: (0, i, 0))],
        out_specs=pl.BlockSpec((REGROUP_ROWS, p * n), lambda i: (i, 0)),
        out_shape=jax.ShapeDtypeStruct((k, p * n), x.dtype),
        compiler_params=_params(("parallel",)),
    )(x)


def columns_to_shards(xs, *, name):
    k = xs[0].shape[0]
    widths = [x.shape[1] for x in xs]
    n = sum(widths) // N_DEV
    pieces = []
    for s in range(N_DEV):
        start = 0
        for i, w in enumerate(widths):
            lo, hi = max(start, s * n), min(start + w, (s + 1) * n)
            if lo < hi:
                pieces.append((i, lo - start, s, lo - s * n, hi - lo))
            start += w

    def body(*refs):
        x_refs, o_ref = refs[: len(xs)], refs[-1]
        for i, c0, s, d0, w in pieces:
            o_ref[s, :, d0 : d0 + w] = x_refs[i][:, c0 : c0 + w]

    return pl.pallas_call(
        body,
        name=name,
        grid=(k // REGROUP_ROWS,),
        in_specs=[pl.BlockSpec((REGROUP_ROWS, w), lambda i: (i, 0)) for w in widths],
        out_specs=pl.BlockSpec((N_DEV, REGROUP_ROWS, n), lambda i: (0, i, 0)),
        out_shape=jax.ShapeDtypeStruct((N_DEV, k, n), xs[0].dtype),
        compiler_params=_params(("parallel",)),
    )(*xs)


def _own_slot(me, block):
    land = lax.empty((N_DEV, *block.shape), block.dtype)
    return lax.dynamic_update_slice_in_dim(land, block[None], me, axis=0)


N_CHIP_PEERS = 3


class Gather:
    def __init__(self, shards, lands, after, *, name):
        nt = len(shards)
        self.name = name

        def body(*refs):
            src_refs, land_refs = refs[:nt], refs[nt : 2 * nt]
            send_sems, recv_sems = refs[2 * nt + 1 : 3 * nt + 1], refs[3 * nt + 1 : 4 * nt + 1]
            token = refs[-1]
            x, y, c, me = _position()
            for t in range(nt):
                for k, dev in enumerate(self._targets(x, y, c)):
                    _remote(src_refs[t], land_refs[t].at[me], send_sems[t].at[k], recv_sems[t].at[k], dev).start()
            token[...] = jnp.zeros_like(token)

        outs = pl.pallas_call(
            body,
            name=name + "_start",
            in_specs=[HBM] * (2 * nt) + [ANY],
            out_specs=[SEM] * (2 * nt) + [HBM] * (2 * nt) + [VMEM],
            out_shape=[pltpu.SemaphoreType.DMA((1 + N_CHIP_PEERS,))] * (2 * nt)
            + [pltpu.HBM(a.shape, a.dtype) for a in (*shards, *lands)]
            + [jax.ShapeDtypeStruct((SUBLANES, LANES), F32)],
            input_output_aliases={i: 2 * nt + i for i in range(2 * nt)},
            compiler_params=pltpu.CompilerParams(has_side_effects=EFFECT),
        )(*[pltpu.with_memory_space_constraint(a, pltpu.HBM) for a in (*shards, *lands)], after)
        self.send_sems, self.recv_sems = list(outs[:nt]), list(outs[nt : 2 * nt])
        self.srcs, self.lands = list(outs[2 * nt : 3 * nt]), list(outs[3 * nt : 4 * nt])
        self.token = outs[-1]

    @staticmethod
    def _chips(x, y):
        return [(1 - x, y), (x, 1 - y), (1 - x, 1 - y)]

    @classmethod
    def _targets(cls, x, y, c):
        return [(x, y, 1 - c)] + [(*chip, c) for chip in cls._chips(x, y)]

    def zero(self):
        return self.token[0, 0]

    @staticmethod
    def _slot(px, py, pc):
        return 4 * px + 2 * py + pc

    def pass_on(self, which, after, *, name):
        n = len(which)

        def pass_body(*refs):
            land_refs, recv_sems = refs[:n], refs[n : 2 * n]
            fwd_send, fwd_recv = refs[3 * n + 1 : 4 * n + 1], refs[4 * n + 1 : 5 * n + 1]
            token = refs[-1]
            x, y, c, _ = _position()
            for t in range(n):
                for j, chip in enumerate(self._chips(x, y)):
                    blk = land_refs[t].at[self._slot(*chip, c)]
                    _remote(blk, blk, fwd_send[t].at[j], recv_sems[t].at[1 + j], (*chip, c)).wait_recv()
                    _remote(blk, blk, fwd_send[t].at[j], fwd_recv[t].at[j], (x, y, 1 - c)).start()
            token[...] = jnp.zeros_like(token)

        lands = [self.lands[t] for t in which]
        outs = pl.pallas_call(
            pass_body,
            name=name,
            in_specs=[HBM] * n + [SEM] * n + [ANY],
            out_specs=[HBM] * n + [SEM] * (2 * n) + [VMEM],
            out_shape=[pltpu.HBM(a.shape, a.dtype) for a in lands]
            + [pltpu.SemaphoreType.DMA((N_CHIP_PEERS,))] * (2 * n)
            + [jax.ShapeDtypeStruct((SUBLANES, LANES), F32)],
            input_output_aliases={i: i for i in range(n)},
            compiler_params=pltpu.CompilerParams(has_side_effects=EFFECT),
        )(*lands, *[self.recv_sems[t] for t in which], after)
        return (which, outs[:n], outs[n : 2 * n], outs[2 * n : 3 * n]), outs[-1]

    def wait(self, which, after, *, name):
        return self.finish(self.pass_on(which, after, name=name + "_pass")[0], after, name=name)

    def finish(self, passed, after, *, name):
        which, lands, fwd_send, fwd_recv = passed
        n = len(which)
        slot = self._slot

        def wait_body(*refs):
            src_refs, land_refs = refs[:n], refs[n : 2 * n]
            send_sems, recv_sems = refs[2 * n : 3 * n], refs[3 * n : 4 * n]
            fwd_send, fwd_recv = refs[4 * n : 5 * n], refs[5 * n : 6 * n]
            x, y, c, me = _position()
            sibling = (x, y, 1 - c)
            for t in range(n):
                for k, dev in enumerate(self._targets(x, y, c)):
                    _remote(src_refs[t], land_refs[t].at[me], send_sems[t].at[k], recv_sems[t].at[k], dev).wait_send()
                blk = land_refs[t].at[slot(x, y, 1 - c)]
                _remote(blk, blk, send_sems[t].at[0], recv_sems[t].at[0], sibling).wait_recv()
                for j, chip in enumerate(self._chips(x, y)):
                    sent = land_refs[t].at[slot(*chip, c)]
                    _remote(sent, sent, fwd_send[t].at[j], fwd_recv[t].at[j], sibling).wait_send()
                    got = land_refs[t].at[slot(*chip, 1 - c)]
                    _remote(got, got, fwd_send[t].at[j], fwd_recv[t].at[j], sibling).wait_recv()

        srcs = [self.srcs[t] for t in which]
        outs = pl.pallas_call(
            wait_body,
            name=name,
            in_specs=[HBM] * (2 * n) + [SEM] * (4 * n) + [ANY],
            out_specs=[HBM] * (2 * n),
            out_shape=[pltpu.HBM(a.shape, a.dtype) for a in (*srcs, *lands)],
            input_output_aliases={i: i for i in range(2 * n)},
            compiler_params=pltpu.CompilerParams(has_side_effects=EFFECT),
        )(
            *srcs, *lands, *[self.send_sems[t] for t in which], *[self.recv_sems[t] for t in which], *fwd_send,
            *fwd_recv, after,
        )
        return outs[n:]


class Scatter:
    def __init__(self, srcs, lands, after, *, name):
        self.name = name
        nt = self.nt = len(srcs)
        peers = N_DEV - 1

        def body(*refs):
            src_refs, land_refs = refs[:nt], refs[nt : 2 * nt]
            send_sems, recv_sems = refs[2 * nt + 1 : 3 * nt + 1], refs[3 * nt + 1 : 4 * nt + 1]
            token = refs[-1]
            x, y, c, me = _position()
            for t in range(nt):
                for k in range(1, N_DEV):
                    dev, pid = _peer(k, x, y, c)
                    src = src_refs[t].at[pid]
                    _remote(src, land_refs[t].at[me], send_sems[t].at[k - 1], recv_sems[t].at[k - 1], dev).start()
            token[...] = jnp.zeros_like(token)

        outs = pl.pallas_call(
            body,
            name=name + "_start",
            in_specs=[HBM] * (2 * nt) + [ANY],
            out_specs=[SEM] * (2 * nt) + [HBM] * (2 * nt) + [VMEM],
            out_shape=[pltpu.SemaphoreType.DMA((peers,))] * (2 * nt)
            + [pltpu.HBM(a.shape, a.dtype) for a in (*srcs, *lands)]
            + [jax.ShapeDtypeStruct((SUBLANES, LANES), F32)],
            input_output_aliases={i: 2 * nt + i for i in range(2 * nt)},
            compiler_params=pltpu.CompilerParams(has_side_effects=EFFECT),
        )(*[pltpu.with_memory_space_constraint(a, pltpu.HBM) for a in (*srcs, *lands)], after)
        self.send_sems, self.recv_sems = outs[:nt], outs[nt : 2 * nt]
        self.srcs, self.lands = outs[2 * nt : 3 * nt], outs[3 * nt : 4 * nt]
        self.token = outs[-1]

    def zero(self):
        return self.token[0, 0]

    def wait(self, which, after, *, name):
        n = len(which)

        def body(*refs):
            src_refs, land_refs = refs[:n], refs[n : 2 * n]
            send_sems, recv_sems = refs[2 * n : 3 * n], refs[3 * n : 4 * n]
            x, y, c, _ = _position()
            for t in range(n):
                for k in range(1, N_DEV):
                    dev, pid = _peer(k, x, y, c)
                    src = src_refs[t].at[pid]
                    cp = _remote(src, land_refs[t].at[pid], send_sems[t].at[k - 1], recv_sems[t].at[k - 1], dev)
                    cp.wait_send()
                    cp.wait_recv()

        srcs = [self.srcs[t] for t in which]
        lands = [self.lands[t] for t in which]
        outs = pl.pallas_call(
            body,
            name=name,
            in_specs=[HBM] * (2 * n) + [SEM] * (2 * n) + [ANY],
            out_specs=[HBM] * (2 * n),
            out_shape=[pltpu.HBM(a.shape, a.dtype) for a in (*srcs, *lands)],
            input_output_aliases={i: i for i in range(2 * n)},
            compiler_params=pltpu.CompilerParams(has_side_effects=EFFECT),
        )(*srcs, *lands, *[self.send_sems[t] for t in which], *[self.recv_sems[t] for t in which], after)
        return outs[n:]


def _adam_update(g, w, m, v):
    m2 = ADAM_B1 * m + (1.0 - ADAM_B1) * g
    v2 = ADAM_B2 * v + (1.0 - ADAM_B2) * jnp.square(g)
    m_hat = m2 / (1.0 - ADAM_B1**ADAM_STEP)
    v_hat = v2 / (1.0 - ADAM_B2**ADAM_STEP)
    delta = -ADAM_LR * (m_hat / (jnp.sqrt(v_hat) + ADAM_EPS) + ADAM_WD * w)
    return delta, m2, v2


def adamw(gparts, w, m, v, *, name):
    nl, r, c = w.shape
    p = gparts[0].shape[0]
    tr = r if r <= 256 else (256 if c <= D_MODEL else 128)
    ni = r // tr

    def body(*refs):
        g_refs = refs[:nl]
        w_ref, m_ref, v_ref, go_ref, d_ref, mo_ref, vo_ref = refs[nl:]
        for layer in range(nl):

            @pl.when(pl.program_id(0) == layer)
            def _(g_ref=g_refs[layer]):
                g = g_ref[0].astype(F32)
                for i in range(1, p):
                    g = g + g_ref[i].astype(F32)
                delta, m2, v2 = _adam_update(g, w_ref[...], m_ref[...], v_ref[...])
                go_ref[...] = g
                d_ref[...] = delta
                mo_ref[...] = m2
                vo_ref[...] = v2

    def parts_spec(layer):
        def index(l, i):
            return (0, jnp.where(l == layer, i, jnp.where(l < layer, 0, ni - 1)), 0)

        return pl.BlockSpec((p, tr, c), index)

    blk = pl.BlockSpec((None, tr, c), lambda l, i: (l, i, 0))
    return pl.pallas_call(
        body,
        name=name,
        grid=(nl, ni),
        in_specs=[*[parts_spec(layer) for layer in range(nl)], blk, blk, blk],
        out_specs=[blk] * 4,
        out_shape=[jax.ShapeDtypeStruct((nl, r, c), F32)] * 4,
        compiler_params=_params(("arbitrary", "arbitrary")),
    )(*gparts, w, m, v)


def adamw_small(items, *, name):
    n = len(items)

    def body(*refs):
        ins, outs = refs[: 4 * n], refs[4 * n :]
        for t in range(n):
            g_ref, w_ref, m_ref, v_ref = ins[4 * t : 4 * t + 4]
            g = g_ref[...]
            delta, m2, v2 = _adam_update(g, w_ref[...], m_ref[...], v_ref[...])
            for o_ref, val in zip(outs[4 * t : 4 * t + 4], (g, delta, m2, v2), strict=True):
                o_ref[...] = val

    outs = pl.pallas_call(
        body,
        name=name,
        out_shape=[jax.ShapeDtypeStruct(item[1].shape, F32) for item in items for _ in range(4)],
    )(*[a for item in items for a in item])
    return [outs[4 * t : 4 * t + 4] for t in range(n)]


def ada_grad_adamw(cg, dmrecv, w4, m4, v4, *, name):
    nt, k, ncol = w4.shape

    def body(cg_ref, dm_ref, w_ref, m_ref, v_ref, go_ref, d_ref, mo_ref, vo_ref, gb_ref):
        sc = _silu_bf16(cg_ref[...].reshape(N_DEV * SUBLANES, k))
        dm = dm_ref[...].reshape(N_DEV * SUBLANES, ncol)
        g = lax.dot_general(sc, dm.astype(BF16), (((0,), (0,)), ((), ())), preferred_element_type=F32)
        delta, m2, v2 = _adam_update(g, w_ref[...], m_ref[...], v_ref[...])
        go_ref[...] = g
        d_ref[...] = delta
        mo_ref[...] = m2
        vo_ref[...] = v2
        gb_ref[...] = jnp.broadcast_to(jnp.sum(dm, axis=0, keepdims=True), (SUBLANES, ncol))

    wblk = pl.BlockSpec((None, k, ncol), lambda t: (t, 0, 0))
    return pl.pallas_call(
        body,
        name=name,
        grid=(nt,),
        in_specs=[
            pl.BlockSpec((N_DEV, SUBLANES, k), lambda t: (0, 0, 0)),
            pl.BlockSpec((N_DEV, None, SUBLANES, ncol), lambda t: (0, t, 0, 0)),
            wblk,
            wblk,
            wblk,
        ],
        out_specs=[wblk] * 4 + [pl.BlockSpec((None, SUBLANES, ncol), lambda t: (t, 0, 0))],
        out_shape=[jax.ShapeDtypeStruct((nt, k, ncol), F32)] * 4 + [jax.ShapeDtypeStruct((nt, SUBLANES, ncol), F32)],
        compiler_params=_params(("parallel",)),
    )(cg, dmrecv, w4, m4, v4)


def kernel(x, c, ada_w, ada_b, ln_g, ln_b, a_w_in, a_b_in, a_vn_g, a_vn_b, a_w_s, a_b_s, a_w_out, b_w_qkv, b_w_out, mlp_w_up, mlp_w_down, loss_target, m_ada_w, m_ada_b, m_ln_g, m_ln_b, m_a_w_in, m_a_b_in, m_a_vn_g, m_a_vn_b, m_a_w_s, m_a_b_s, m_a_w_out, m_b_w_qkv, m_b_w_out, m_mlp_w_up, m_mlp_w_down, v_ada_w, v_ada_b, v_ln_g, v_ln_b, v_a_w_in, v_a_b_in, v_a_vn_g, v_a_vn_b, v_a_w_s, v_a_b_s, v_a_w_out, v_b_w_qkv, v_b_w_out, v_mlp_w_up, v_mlp_w_down):
    x0 = x[0]
    target = loss_target[0]
    me = 4 * lax.axis_index("x") + 2 * lax.axis_index("y") + lax.axis_index("c")

    ada_w4 = ada_w.reshape(N_SUB, D_MODEL, -1)
    ada_b4 = ada_b.reshape(N_SUB, -1)
    ln8 = jnp.concatenate([ln_g.reshape(N_SUB, -1), ln_b.reshape(N_SUB, -1)], axis=0)
    c8 = jnp.broadcast_to(c, (SUBLANES, D_MODEL))
    cg, lng, mrecv = ada_exchange(c8, ada_w4, ada_b4, ln8)

    W_IN, W_AOUT, W_UP0, W_DN0, W_QKV, W_BOUT, W_UP1, W_DN1 = range(8)
    shards = [
        a_w_in[0].astype(BF16),
        a_w_out[0].astype(BF16),
        mlp_w_up[0].astype(BF16),
        mlp_w_down[0].astype(BF16),
        b_w_qkv[0].astype(BF16),
        b_w_out[0].astype(BF16),
        mlp_w_up[1].astype(BF16),
        mlp_w_down[1].astype(BF16),
    ]
    gather = Gather(shards, [_own_slot(me, s) for s in shards], mrecv, name="gather")

    modv = mrecv[:, :, 0, :].transpose(1, 0, 2).reshape(N_SUB, 3 * D_MODEL) + gather.zero()
    shift = [modv[t : t + 1, :D_MODEL] for t in range(N_SUB)]
    scale = [modv[t : t + 1, D_MODEL : 2 * D_MODEL] for t in range(N_SUB)]
    gate1 = [1.0 + modv[t : t + 1, 2 * D_MODEL :] for t in range(N_SUB)]
    lng_full = [lng[:, t, :].reshape(1, D_MODEL) for t in range(N_SUB)]
    lnb_full = [lng[:, N_SUB + t, :].reshape(1, D_MODEL) for t in range(N_SUB)]

    ident = lambda acc: (acc,)
    def relu2(a):
        r = jnp.maximum(a, jnp.zeros_like(a))
        return r * r
    vn_g, vn_b, w_s = a_vn_g, a_vn_b, a_w_s[0]
    bias_full = jnp.repeat(a_b_s[0].T, A_GROUP_DIM, axis=1)
    w_up3, w_dn3 = [None, None], [None, None]

    def mlp_forward(i, h, up, dn, x_in=None, t_next=None):
        w_up3[i], w_dn3[i] = up, dn.reshape(1, D_FF, D_MODEL)
        (a,) = mm_nn(h, w_up3[i], name=f"mlp{i}_up", tm=2048, ps=2, tn=512, tk=D_MODEL, epilogue=ident, outs=(BF16,))
        if t_next is None:
            (y,) = mm_nn(
                a, w_dn3[i], name=f"mlp{i}_down", tm=1024, ps=1, tn=512, tk=D_FF, prologue=relu2, epilogue=ident, outs=(BF16,)
            )
            return a, y
        y, xn, hn = mm_nn(
            a, w_dn3[i], name=f"mlp{i}_down", tm=512, ps=1, tn=D_MODEL, tk=D_FF, prologue=relu2,
            epilogue=residual_ln_epilogue, extras=residual_extras(x_in, t_next - 1), outs=(BF16, F32, BF16),
        )
        return a, y, xn, hn

    def residual_extras(x_in, t):
        rows = (gate1[t], lng_full[t], lnb_full[t], scale[t + 1], shift[t + 1])
        return [(x_in, "full")] + [(r, "row") for r in rows]

    h0 = modulate(x0, scale[0], shift[0], name="modulate0")
    w_in3, w_aout3 = gather.wait([W_IN, W_AOUT], h0, name="gather_wait_a")
    w_aout3 = w_aout3.reshape(1, D_MODEL, D_MODEL)
    (a_pre,) = mm_nn(
        h0, w_in3, name="a_in", tm=2048, ps=4, tn=256, tk=D_MODEL, epilogue=lambda acc, b: (acc + b,),
        extras=[(a_b_in, "row")], outs=(BF16,),
    )
    p_gate = gate_fwd(a_pre, vn_g, vn_b, w_s, bias_full, name="gate_fwd")
    y0, x1, h1 = mm_nn(
        p_gate, w_aout3, name="a_out", tm=1024, ps=1, tn=D_MODEL, tk=D_MODEL, epilogue=residual_ln_epilogue,
        extras=residual_extras(x0, 0), outs=(BF16, F32, BF16),
    )
    w_mlp0 = gather.wait([W_UP0, W_DN0], y0, name="gather_wait_mlp0")
    a1, y1, x2, h2 = mlp_forward(0, h1, *w_mlp0, x_in=x1, t_next=2)
    (w_qkv_shards,) = gather.wait([W_QKV], y1, name="gather_wait_qkv")
    w_qkv3 = shards_to_columns(w_qkv_shards, name="w_qkv_columns")[None]
    pat_tiles = 3
    dil = [d for _, d in B_PATTERNS]
    qkv_p, pat_o, pat_lse = [], [], []
    for g in range(N_PAT):
        (qkv_g,) = mm_nn(
            h2, w_qkv3, name=f"b_qkv{g}", tm=2048, ps=1, tn=D_MODEL, tk=D_MODEL, epilogue=ident, outs=(BF16,),
            b_tile0=pat_tiles * g, b_tiles=pat_tiles, out_streams=dil[g],
        )
        o_g, lse_g = attn_fwd(qkv_g, g, name=f"attn_fwd{g}", after=rest_token if g == 1 else None)
        if g == 0:
            rest_passed, rest_token = gather.pass_on([W_BOUT, W_UP1, W_DN1], lse_g, name="gather_pass_rest")
        qkv_p.append(qkv_g)
        pat_o.append(o_g)
        pat_lse.append(lse_g)
    o_b, o_f, lse = attn_combine(pat_o, pat_lse, name="attn_combine")
    w_bout3, *w_mlp1 = gather.finish(rest_passed, lse, name="gather_wait_rest")
    w_bout3 = w_bout3.reshape(1, D_MODEL, D_MODEL)
    y2, x3, h3 = mm_nn(
        o_b, w_bout3, name="b_out", tm=1024, ps=1, tn=D_MODEL, tk=D_MODEL, epilogue=residual_ln_epilogue,
        extras=residual_extras(x2, 2), outs=(BF16, F32, BF16),
    )
    a3, y3 = mlp_forward(1, h3, *w_mlp1)
    dz3, dy3, st3, loss_local = residual_ln_loss_bwd(
        x3, y3, gate1[3], lng_full[3], lnb_full[3], target, name="res_ln3_loss_bwd"
    )

    def scatter(parts, after, name):
        parts = [p.reshape(N_DEV, -1, p.shape[-1]) for p in parts]
        lands = [_own_slot(me, lax.dynamic_index_in_dim(p, me, 0, keepdims=False)) for p in parts]
        return Scatter(parts, lands, after, name=name)

    xs_in, ys = [x0, x1, x2, x3], [y0, y1, y2, y3]

    def residual_bwd_extras(dz_later, t, zero):
        return [
            (dz_later, "full"), (scale[t + 1] + zero, "row"), (xs_in[t], "full"), (ys[t], "full"),
            (gate1[t], "row"), (lng_full[t], "row"), (lnb_full[t], "row"),
        ]

    def mlp_backward(i, h, a, dy, dz_later, t):
        (da,) = mm_nt(
            dy,
            w_dn3[i],
            name=f"mlp{i}_da",
            tm=2048,
            tko=1024,
            ps=1,
            tc=D_MODEL,
            epilogue=lambda acc, act: (acc * (2.0 * jnp.maximum(act.astype(F32), 0.0)),),
            extras=[(a, "full")],
            outs=(BF16,),
        )
        dw_dn = mm_tn(
            a, dy, name=f"mlp{i}_dw_down", p=1, tk=1024, ps=1, tn=D_MODEL, tmc=2048, prologue=relu2, out_dtype=BF16
        )
        dw_up = mm_tn(h, da, name=f"mlp{i}_dw_up", p=N_DEV, tk=1024, ps=2, tn=512, tmc=2048, out_dtype=BF16)
        rs = scatter([dw_up, dw_dn], da, f"scatter_mlp{i}")
        dz, dy_before, st = mm_nt(
            da, w_up3[i], name=f"mlp{i}_dh", tm=512, tko=D_MODEL, ps=N_DEV, tc=512, epilogue=residual_bwd_epilogue,
            extras=residual_bwd_extras(dz_later, t, rs.zero()), outs=(F32, BF16), stats=True,
        )
        return rs, dz, dy_before, st

    rs_mlp1, dz2, dy2, st2 = mlp_backward(1, h3, a3, dy3, dz3, 2)
    do_b, delta = mm_nt(
        dy2, w_bout3, name="b_do", tm=2048, tko=D_MODEL, ps=1, tc=D_MODEL, extras=[(o_f, "full")],
        epilogue=lambda acc, o: (acc, _reduce_groups(acc * o)), outs=(BF16, F32), out_widths=[D_MODEL, LANES],
    )
    dh2, dw_pat = None, []
    for g in range(N_PAT):
        do_g, lse_g, delta_g = do_b, lse, delta
        if g > 0:
            do_g, lse_g, delta_g = permute_rows([do_b, lse, delta], dil[g], inverse=False, name=f"perm_do{g}")
        dqkv_g = attn_bwd(qkv_p[g], do_g, lse_g, delta_g, g, name=f"attn_bwd{g}")
        dw_pat.append(
            mm_tn(
                h2, dqkv_g, name=f"b_dw_qkv{g}", p=1, tk=1024, ps=1, tn=D_MODEL, tmc=2048, out_dtype=BF16,
                g_streams=dil[g],
            )[0]
        )
        if g == N_PAT - 1:
            break
        (dh2,) = mm_nt(
            dqkv_g, w_qkv3, name=f"b_dh{g}", tm=2048, tko=512, ps=1, tc=pat_tiles * D_MODEL, outs=(F32,),
            b_tile0=g, g_streams=dil[g],
            epilogue=ident if g == 0 else (lambda acc, prev, d=dil[g]: (_from_streams(acc, d) + prev,)),
            extras=[] if g == 0 else [(dh2, "full")],
        )
    dw_bout = mm_tn(o_b, dy2, name="b_dw_out", p=1, tk=1024, ps=1, tn=D_MODEL, tmc=2048, out_dtype=BF16)
    dw_qkv = columns_to_shards(dw_pat, name="dw_qkv_shards")
    rs_b = scatter([dw_qkv, dw_bout], dqkv_g, "scatter_b")
    dz1, dy1, st1 = mm_nt(
        dqkv_g, w_qkv3, name=f"b_dh{N_PAT - 1}", tm=512, tko=D_MODEL, ps=1, tc=pat_tiles * D_MODEL, outs=(F32, BF16),
        b_tile0=N_PAT - 1, g_streams=dil[-1], stats=True,
        epilogue=lambda acc, prev, *rest: residual_bwd_epilogue(_from_streams(acc, dil[-1]) + prev, *rest),
        extras=[(dh2, "full")] + residual_bwd_extras(dz2, 1, rs_b.zero()),
    )
    rs_mlp0, dz0, dy0, st0 = mlp_backward(0, h1, a1, dy1, dz1, 0)
    (dp_gate,) = mm_nt(dy0, w_aout3, name="a_dp", tm=2048, tko=1024, ps=1, tc=D_MODEL, epilogue=ident, outs=(F32,))
    dw_aout = mm_tn(p_gate, dy0, name="a_dw_out", p=1, tk=1024, ps=1, tn=D_MODEL, tmc=2048, out_dtype=BF16)
    rs_aout = scatter([dw_aout], dp_gate, "scatter_a_out")
    da0, d_ws, d_bs, gate_rows = gate_bwd(a_pre, dp_gate, vn_g + rs_aout.zero(), vn_b, w_s, bias_full, name="gate_bwd")
    dw_in = mm_tn(h0, da0, name="a_dw_in", p=N_DEV, tk=1024, ps=4, tn=256, tmc=2048, out_dtype=BF16)
    rs_in = scatter([dw_in], d_ws, "scatter_a_in")
    grad_x, stf = mm_nt(
        da0, w_in3, name="a_dh", tm=1024, tko=D_MODEL, ps=N_DEV, tc=256, epilogue=input_bwd_epilogue,
        extras=[(dz0, "full"), (scale[0] + rs_in.zero(), "row"), (x0, "full")], outs=(F32,), stats=True,
    )

    stats_after = [stf, st0, st1, st2]
    stats_own = [st0, st1, st2, st3]
    dm = jnp.stack(
        [
            jnp.concatenate(
                [stats_after[t][ST_DSHIFT], stats_after[t][ST_DSCALE], stats_own[t][ST_DGATE]], axis=0
            )
            for t in range(N_SUB)
        ]
    )
    ncol = 3 * D_MODEL // N_DEV
    dmx = jnp.pad(
        dm.reshape(N_SUB, N_DEV, ncol).transpose(1, 0, 2)[:, :, None, :], ((0, 0), (0, 0), (0, SUBLANES - 1), (0, 0))
    )
    small = [
        gate_rows[0],
        gate_rows[1],
        d_ws.reshape(-1),
        d_bs[:, :A_GROUPS].T.reshape(-1),
        *[stats_own[t][ST_DG] for t in range(N_SUB)],
        *[stats_own[t][ST_DB] for t in range(N_SUB)],
        jnp.pad(loss_local.reshape(1), (0, LANES - 1)),
    ]
    n_small = sum(s.size for s in small)
    part_rows = -(-n_small // (N_DEV * LANES * SUBLANES)) * SUBLANES
    flat = jnp.concatenate(small + [jnp.zeros((N_DEV * part_rows * LANES - n_small,), F32)])
    dmrecv, reduced = small_exchange(dmx, flat.reshape(N_DEV, part_rows, LANES))
    reduced = reduced.reshape(-1)
    sizes = [2 * D_MODEL, D_MODEL, D_MODEL, A_GROUPS * CHUNK * CHUNK, A_GROUPS * CHUNK, N_SUB * D_MODEL, N_SUB * D_MODEL]
    offs = [sum(sizes[:i]) for i in range(len(sizes) + 1)]
    g_b_in, g_vn_g, g_vn_b, g_ws, g_bs, g_lng, g_lnb = [reduced[offs[i] : offs[i + 1]] for i in range(len(sizes))]
    loss = reduced[offs[-1]]

    results = {}

    def update(wname, gparts, w, m, v):
        shape = w.shape
        layers = len(gparts) if isinstance(gparts, list) else 1
        w3 = w.reshape(layers, -1, shape[-1])
        parts = [g.reshape(g.shape[0], *w3.shape[1:]) for g in (gparts if layers > 1 else [gparts])]
        outs = adamw(parts, w3, m.reshape(w3.shape), v.reshape(w3.shape), name=f"adamw_{wname}")
        results[wname] = [o.reshape(shape) for o in outs]

    ada_outs = ada_grad_adamw(cg, dmrecv, ada_w4, m_ada_w.reshape(ada_w4.shape), v_ada_w.reshape(ada_w4.shape), name="ada_grad_adamw")
    results["ada_w"] = [o.reshape(ada_w.shape) for o in ada_outs[:4]]
    ln_cols = D_MODEL // N_DEV
    my_ln = lambda gfull: lax.dynamic_slice_in_dim(gfull.reshape(N_SUB, N_DEV, ln_cols), me, 1, axis=1)
    small_params = [
        ("ada_b", ada_outs[4][:, 0, :], ada_b, m_ada_b, v_ada_b),
        ("ln_g", my_ln(g_lng), ln_g, m_ln_g, v_ln_g),
        ("ln_b", my_ln(g_lnb), ln_b, m_ln_b, v_ln_b),
        ("a_b_in", g_b_in, a_b_in, m_a_b_in, v_a_b_in),
        ("a_vn_g", g_vn_g, a_vn_g, m_a_vn_g, v_a_vn_g),
        ("a_vn_b", g_vn_b, a_vn_b, m_a_vn_b, v_a_vn_b),
        ("a_b_s", g_bs, a_b_s, m_a_b_s, v_a_b_s),
    ]
    small_outs = adamw_small(
        [[a.reshape(-1, w.shape[-1]) for a in (g, w, m, v)] for _, g, w, m, v in small_params], name="adamw_small"
    )
    for (wname, _, w, _, _), outs in zip(small_params, small_outs, strict=True):
        results[wname] = [o.reshape(w.shape) for o in outs]
    update("a_w_s", g_ws[None], a_w_s, m_a_w_s, v_a_w_s)
    g_up1, g_dn1 = rs_mlp1.wait([0, 1], grad_x, name="scatter_wait_mlp1")
    g_qkv, g_bout = rs_b.wait([0, 1], grad_x, name="scatter_wait_b")
    update("b_w_qkv", g_qkv, b_w_qkv, m_b_w_qkv, v_b_w_qkv)
    update("b_w_out", g_bout, b_w_out, m_b_w_out, v_b_w_out)
    g_up0, g_dn0 = rs_mlp0.wait([0, 1], grad_x, name="scatter_wait_mlp0")
    update("mlp_w_up", [g_up0, g_up1], mlp_w_up, m_mlp_w_up, v_mlp_w_up)
    update("mlp_w_down", [g_dn0, g_dn1], mlp_w_down, m_mlp_w_down, v_mlp_w_down)
    (g_aout,) = rs_aout.wait([0], grad_x, name="scatter_wait_a_out")
    (g_in,) = rs_in.wait([0], grad_x, name="scatter_wait_a_in")
    update("a_w_in", g_in, a_w_in, m_a_w_in, v_a_w_in)
    update("a_w_out", g_aout, a_w_out, m_a_w_out, v_a_w_out)

    order = ["ada_w", "ada_b", "ln_g", "ln_b", "a_w_in", "a_b_in", "a_vn_g", "a_vn_b", "a_w_s", "a_b_s", "a_w_out", "b_w_qkv", "b_w_out", "mlp_w_up", "mlp_w_down"]
    return (loss, grad_x[None], *[results[n][0] for n in order], *[results[n][1] for n in order],
            *[results[n][2] for n in order], *[results[n][3] for n in order])
```

```python
import math

import jax
import jax.numpy as jnp
from jax import lax
from jax.experimental import pallas as pl
from jax.experimental.pallas import tpu as pltpu

F32 = jnp.float32
BF16 = jnp.bfloat16
MESH = pl.DeviceIdType.MESH
ANY = pl.BlockSpec(memory_space=pl.ANY)
VMEM = pl.BlockSpec(memory_space=pltpu.VMEM)

N_DEV = 8
D_MODEL = 1024
SEQ = 4096
DEPTH = 2
CHUNK = 128
A_GROUPS = 16
A_GROUP_DIM = D_MODEL // A_GROUPS
B_HEADS = 16
B_HEAD_DIM = 64
B_PATTERNS = ((128, 1), (512, 4), (2048, 16))
N_PAT = len(B_PATTERNS)
SPAN = 128
D_FF = 4 * D_MODEL
D_QKV = N_PAT * 3 * D_MODEL
ALPHA = (2 * DEPTH) ** 0.25
LN_EPS = 1e-5
NEG = -1e30
ADAM_LR = 0.001
ADAM_B1 = 0.9
ADAM_B2 = 0.999
ADAM_EPS = 1e-08
ADAM_WD = 0.01
ADAM_STEP = 10
GELU_C = math.sqrt(2.0 / math.pi)
GELU_A = 0.044715

VMEM_LIMIT_BYTES = 56 * 1024 * 1024
LANES = 128
SUBLANES = 8
ROW_TILE = 512
N_SUB = 2 * DEPTH


def _params(sem):
    return pltpu.CompilerParams(dimension_semantics=sem, vmem_limit_bytes=VMEM_LIMIT_BYTES)


def _lane(shape):
    return lax.broadcasted_iota(jnp.int32, shape, len(shape) - 1)


def _split_bf16(x):
    hi = x.astype(BF16)
    lo = (x - hi.astype(F32)).astype(BF16)
    return hi, lo


def _group_expand_matrix(groups_padded, width):
    per = width // A_GROUPS
    r = lax.broadcasted_iota(jnp.int32, (groups_padded, width), 0)
    c = lax.broadcasted_iota(jnp.int32, (groups_padded, width), 1)
    return (c // per == r).astype(BF16)


def _group_reduce_matrix(width, groups_padded):
    per = width // A_GROUPS
    r = lax.broadcasted_iota(jnp.int32, (width, groups_padded), 0)
    c = lax.broadcasted_iota(jnp.int32, (width, groups_padded), 1)
    return (r // per == c).astype(BF16)


def _expand_groups(w):
    e = _group_expand_matrix(LANES, D_MODEL)
    hi, lo = _split_bf16(w)
    return jnp.dot(hi, e, preferred_element_type=F32) + jnp.dot(lo, e, preferred_element_type=F32)


def _reduce_groups(x):
    e = _group_reduce_matrix(D_MODEL, LANES)
    hi, lo = _split_bf16(x)
    return jnp.dot(hi, e, preferred_element_type=F32) + jnp.dot(lo, e, preferred_element_type=F32)


def _to_streams(x, d):
    rows, w = x.shape
    return jnp.swapaxes(x.reshape(rows // d, d, w), 0, 1).reshape(rows, w)


def _from_streams(x, d):
    rows, w = x.shape
    return jnp.swapaxes(x.reshape(d, rows // d, w), 0, 1).reshape(rows, w)


def _column_tiles(p, n, ps, tn):
    assert (ps == 1 or tn == n) and p % ps == 0 and n % tn == 0
    q = n // tn
    return (p // ps) * q, q


def _extra_specs(extras, tm, width):
    specs = []
    for _, kind in extras:
        if kind == "row":
            specs.append(pl.BlockSpec((1, width), lambda i, j, c: (0, j)))
        else:
            specs.append(pl.BlockSpec((tm, width), lambda i, j, c: (i, j)))
    return specs


def mm_nn(a, b3, *, name, tm, ps, tn, tk, epilogue, extras=(), outs, prologue=None, b_tile0=0, b_tiles=None, out_streams=1):
    m, k = a.shape
    p, _, n = b3.shape
    nj, q = _column_tiles(p, n, ps, tn)
    nj = nj if b_tiles is None else b_tiles
    nk = k // tk
    width = ps * tn
    d = out_streams
    assert d == 1 or not extras

    def body(a_ref, b_ref, *rest):
        ex = rest[: len(extras)]
        out_refs = rest[len(extras) : len(extras) + len(outs)]
        kk = pl.program_id(2)
        av = a_ref[...] if prologue is None else prologue(a_ref[...])
        if d > 1:
            av = _to_streams(av, d)

        def finish(cs, acc):
            res = epilogue(acc, *[e[:, cs] for e in ex])
            for o_ref, r in zip(out_refs, res, strict=True):
                if d > 1:
                    o_ref[:, :, cs] = r.astype(o_ref.dtype).reshape(d, tm // d, tn)
                else:
                    o_ref[:, cs] = r.astype(o_ref.dtype)

        for s in range(ps):
            cs = slice(s * tn, (s + 1) * tn)
            part = jnp.dot(av, b_ref[s], preferred_element_type=F32)
            if nk == 1:
                finish(cs, part)
                continue
            acc_ref = rest[-1]

            @pl.when(kk == 0)
            def _(part=part, cs=cs):
                acc_ref[:, cs] = part

            @pl.when(kk > 0)
            def _(part=part, cs=cs):
                acc_ref[:, cs] += part

        if nk > 1:

            @pl.when(kk == nk - 1)
            def _():
                for s in range(ps):
                    cs = slice(s * tn, (s + 1) * tn)
                    finish(cs, rest[-1][:, cs])

    if d > 1:
        out_spec = pl.BlockSpec((d, tm // d, width), lambda i, j, kk: (0, i, j))
        out_shape = (d, m // d, nj * width)
    else:
        out_spec = pl.BlockSpec((tm, width), lambda i, j, kk: (i, j))
        out_shape = (m, nj * width)
    res = pl.pallas_call(
        body,
        name=name,
        grid=(m // tm, nj, nk),
        in_specs=[
            pl.BlockSpec((tm, tk), lambda i, j, kk: (i, kk)),
            pl.BlockSpec((ps, tk, tn), lambda i, j, kk: ((j + b_tile0) // q, kk, (j + b_tile0) % q)),
            *_extra_specs(extras, tm, width),
        ],
        out_specs=[out_spec for _ in outs],
        out_shape=[jax.ShapeDtypeStruct(out_shape, dt) for dt in outs],
        scratch_shapes=[pltpu.VMEM((tm, width), F32)] if nk > 1 else [],
        compiler_params=_params(("parallel", "parallel", "arbitrary")),
    )(a, b3, *[arr for arr, _ in extras])
    return [r.reshape(m, nj * width) for r in res]


def mm_nt(
    g, b3, *, name, tm, tko, ps, tc, epilogue, extras=(), outs, out_widths=None, out_streams=None, b_tile0=0,
    g_streams=1, stats=False,
):
    m, width = g.shape
    p, k, n = b3.shape
    _, q = _column_tiles(p, n, ps, tc)
    nc = width // (ps * tc)
    ds = g_streams
    assert not stats or tko == k
    widths = [tko] * len(outs) if out_widths is None else out_widths
    streams_out = [1] * len(outs) if out_streams is None else out_streams

    def out_spec(w, d):
        if d == 1:
            return pl.BlockSpec((tm, w), lambda i, j, c: (i, j))
        return pl.BlockSpec((d, tm // d, w), lambda i, j, c: (0, i, j))

    def out_struct(dt, w, d):
        cols = w * (k // tko)
        return jax.ShapeDtypeStruct((m, cols) if d == 1 else (d, m // d, cols), dt)

    def body(g_ref, b_ref, *rest):
        ex = rest[: len(extras)]
        out_refs = rest[len(extras) : len(extras) + len(outs)]
        c = pl.program_id(2)
        gv = g_ref[...].reshape(tm, ps * tc) if ds > 1 else g_ref[...]
        part = None
        for s in range(ps):
            d = lax.dot_general(gv[:, s * tc : (s + 1) * tc], b_ref[s], _NT, preferred_element_type=F32)
            part = d if part is None else part + d

        def finish(acc):
            res = epilogue(acc, *[e[...] for e in ex])
            if stats:
                *res, sums = res
                st_ref = rest[len(extras) + len(outs)]

                @pl.when(pl.program_id(0) == 0)
                def _():
                    st_ref[...] = jnp.zeros_like(st_ref)

                for row, val in enumerate(sums):
                    st_ref[row : row + 1, :] += val
            for o_ref, r, d_out in zip(out_refs, res, streams_out, strict=True):
                r = r.astype(o_ref.dtype)
                o_ref[...] = r if d_out == 1 else _to_streams(r, d_out).reshape(o_ref.shape)

        if nc == 1:
            finish(part)
            return
        acc_ref = rest[-1]

        @pl.when(c == 0)
        def _():
            acc_ref[...] = part

        @pl.when(c > 0)
        def _():
            acc_ref[...] += part

        @pl.when(c == nc - 1)
        def _():
            finish(acc_ref[...])

    return pl.pallas_call(
        body,
        name=name,
        grid=(m // tm, k // tko, nc),
        in_specs=[
            pl.BlockSpec((ds, tm // ds, ps * tc), lambda i, j, c: (0, i, c))
            if ds > 1
            else pl.BlockSpec((tm, ps * tc), lambda i, j, c: (i, c)),
            pl.BlockSpec((ps, tko, tc), lambda i, j, c: ((c + b_tile0) // q, j, (c + b_tile0) % q)),
            *_extra_specs(extras, tm, tko),
        ],
        out_specs=[out_spec(w, d) for w, d in zip(widths, streams_out, strict=True)]
        + [pl.BlockSpec((SUBLANES, tko), lambda i, j, c: (0, 0))] * stats,
        out_shape=[out_struct(dt, w, d) for dt, w, d in zip(outs, widths, streams_out, strict=True)]
        + [jax.ShapeDtypeStruct((SUBLANES, k), F32)] * stats,
        scratch_shapes=[pltpu.VMEM((tm, tko), F32)] if nc > 1 else [],
        compiler_params=_params(("arbitrary" if stats else "parallel", "parallel", "arbitrary")),
    )(g.reshape(ds, m // ds, width) if ds > 1 else g, b3, *[arr for arr, _ in extras])


def mm_tn(a, g, *, name, p, tk, ps, tn, tmc, out_dtype, prologue=None, g_streams=1):
    m, k = a.shape
    width = g.shape[1]
    n = width // p
    nj, q = _column_tiles(p, n, ps, tn)
    nc = m // tmc
    ds = g_streams

    def body(a_ref, g_ref, o_ref, acc_ref):
        c = pl.program_id(2)
        av = a_ref[...] if prologue is None else prologue(a_ref[...])
        gv = g_ref[...]
        if ds > 1:
            av, gv = _to_streams(av, ds), gv.reshape(tmc, ps * tn)
        part = lax.dot_general(av, gv, (((0,), (0,)), ((), ())), preferred_element_type=F32)

        @pl.when(c == 0)
        def _():
            acc_ref[...] = part

        @pl.when(c > 0)
        def _():
            acc_ref[...] += part

        @pl.when(c == nc - 1)
        def _():
            for s in range(ps):
                o_ref[s] = acc_ref[:, s * tn : (s + 1) * tn].astype(o_ref.dtype)

    return pl.pallas_call(
        body,
        name=name,
        grid=(k // tk, nj, nc),
        in_specs=[
            pl.BlockSpec((tmc, tk), lambda i, j, c: (c, i)),
            pl.BlockSpec((ds, tmc // ds, ps * tn), lambda i, j, c: (0, c, j))
            if ds > 1
            else pl.BlockSpec((tmc, ps * tn), lambda i, j, c: (c, j)),
        ],
        out_specs=pl.BlockSpec((ps, tk, tn), lambda i, j, c: (j // q, i, j % q)),
        out_shape=jax.ShapeDtypeStruct((p, k, n), out_dtype),
        scratch_shapes=[pltpu.VMEM((tk, ps * tn), F32)],
        compiler_params=_params(("parallel", "parallel", "arbitrary")),
    )(a, g.reshape(ds, m // ds, width) if ds > 1 else g)


def _rows(cols):
    return pl.BlockSpec((ROW_TILE, cols), lambda i: (i, 0))


def _vec(cols, rows=1):
    return pl.BlockSpec((rows, cols), lambda i: (0, 0))


def _layer_norm_hat(z):
    mu = jnp.mean(z, axis=-1, keepdims=True)
    zc = z - mu
    var = jnp.mean(zc * zc, axis=-1, keepdims=True)
    rstd = lax.rsqrt(var + LN_EPS)
    return zc * rstd, rstd


def modulate(x, scale, shift, *, name):
    s, d = x.shape

    def body(x_ref, sc_ref, sh_ref, h_ref):
        h_ref[...] = (x_ref[...] * (1.0 + sc_ref[...]) + sh_ref[...]).astype(BF16)

    return pl.pallas_call(
        body,
        name=name,
        grid=(s // ROW_TILE,),
        in_specs=[_rows(d), _vec(d), _vec(d)],
        out_specs=_rows(d),
        out_shape=jax.ShapeDtypeStruct((s, d), BF16),
        compiler_params=_params(("parallel",)),
    )(x, scale, shift)


def residual_ln_epilogue(acc, x, gate1, g, b, nscale, nshift):
    y = acc.astype(BF16)
    xhat, _ = _layer_norm_hat(ALPHA * x + gate1 * y.astype(F32))
    xn = xhat * g + b
    return y, xn, xn * (1.0 + nscale) + nshift


ST_DSCALE, ST_DSHIFT, ST_DG, ST_DB, ST_DGATE = 0, 1, 2, 3, 4


def _layer_norm_bwd(g_out, xhat, rstd, g):
    dxh = g_out * g
    m1 = jnp.mean(dxh, axis=-1, keepdims=True)
    m2 = jnp.mean(dxh * xhat, axis=-1, keepdims=True)
    return rstd * (dxh - m1 - xhat * m2)


def _column_sums(vals):
    return [jnp.sum(v, axis=0, keepdims=True) for v in vals]


def residual_bwd_epilogue(dh, dzl, scl, x, y, gate1, g, b):
    yf = y.astype(F32)
    xhat, rstd = _layer_norm_hat(ALPHA * x + gate1 * yf)
    g_out = ALPHA * dzl + dh * (1.0 + scl)
    dz = _layer_norm_bwd(g_out, xhat, rstd, g)
    return dz, dz * gate1, _column_sums([dh * (xhat * g + b), dh, g_out * xhat, g_out, dz * yf])


def input_bwd_epilogue(dh, dzl, scl, x):
    return ALPHA * dzl + dh * (1.0 + scl), _column_sums([dh * x, dh])


def residual_ln_loss_bwd(x, y, gate1, g, b, target, *, name):
    s, d = x.shape

    def body(x_ref, y_ref, gt_ref, g_ref, b_ref, t_ref, dz_ref, dy_ref, st_ref, loss_ref):
        y = y_ref[...].astype(F32)
        gate1 = gt_ref[...]
        xhat, rstd = _layer_norm_hat(ALPHA * x_ref[...] + gate1 * y)
        err = xhat * g_ref[...] + b_ref[...] - t_ref[...]
        g_out = err * (1.0 / d)
        dz = _layer_norm_bwd(g_out, xhat, rstd, g_ref[...])
        dz_ref[...] = dz
        dy_ref[...] = (dz * gate1).astype(BF16)
        part = jnp.sum(jnp.sum(err * err, axis=1, keepdims=True), axis=0, keepdims=True) * (0.5 / d)

        @pl.when(pl.program_id(0) == 0)
        def _():
            st_ref[...] = jnp.zeros_like(st_ref)
            loss_ref[...] = jnp.zeros_like(loss_ref)

        loss_ref[...] += part
        for row, val in zip((ST_DG, ST_DB, ST_DGATE), _column_sums([g_out * xhat, g_out, dz * y]), strict=True):
            st_ref[row : row + 1, :] += val

    return pl.pallas_call(
        body,
        name=name,
        grid=(s // ROW_TILE,),
        in_specs=[_rows(d), _rows(d), _vec(d), _vec(d), _vec(d), _rows(d)],
        out_specs=[_rows(d), _rows(d), _vec(d, SUBLANES), pl.BlockSpec((1, 1), lambda i: (0, 0))],
        out_shape=[
            jax.ShapeDtypeStruct((s, d), F32),
            jax.ShapeDtypeStruct((s, d), BF16),
            jax.ShapeDtypeStruct((SUBLANES, d), F32),
            jax.ShapeDtypeStruct((1, 1), F32),
        ],
        compiler_params=_params(("arbitrary",)),
    )(x, y, gate1, g, b, target)


def residual_ln_bwd(*, name, later=None, dxo=None, x_out=None, this=None):
    lead = later[0] if later is not None else dxo
    s, d = lead.shape
    has_later, has_ln = later is not None, this is not None

    def body(*refs):
        refs = list(refs)
        if has_later:
            dh_ref, dzl_ref, scl_ref = refs[:3]
            refs = refs[3:]
        else:
            dxo_ref = refs.pop(0)
        if has_ln:
            x_ref, y_ref, gt_ref, g_ref, b_ref = refs[:5]
            refs = refs[5:]
            dz_ref, dy_ref, st_ref = refs
        else:
            xo_ref, dx_ref, st_ref = refs

        @pl.when(pl.program_id(0) == 0)
        def _():
            st_ref[...] = jnp.zeros_like(st_ref)

        def acc(row, val):
            st_ref[row : row + 1, :] += jnp.sum(val, axis=0, keepdims=True)

        if has_ln:
            y = y_ref[...].astype(F32)
            gate1 = gt_ref[...]
            xhat, rstd = _layer_norm_hat(ALPHA * x_ref[...] + gate1 * y)
            x_out_v = xhat * g_ref[...] + b_ref[...]
        else:
            x_out_v = xo_ref[...]
        if has_later:
            dh = dh_ref[...]
            g_out = ALPHA * dzl_ref[...] + dh * (1.0 + scl_ref[...])
            acc(ST_DSCALE, dh * x_out_v)
            acc(ST_DSHIFT, dh)
        else:
            g_out = dxo_ref[...]
        if not has_ln:
            dx_ref[...] = g_out
            return
        acc(ST_DG, g_out * xhat)
        acc(ST_DB, g_out)
        dxh = g_out * g_ref[...]
        m1 = jnp.mean(dxh, axis=-1, keepdims=True)
        m2 = jnp.mean(dxh * xhat, axis=-1, keepdims=True)
        dz = rstd * (dxh - m1 - xhat * m2)
        acc(ST_DGATE, dz * y)
        dz_ref[...] = dz
        dy_ref[...] = (dz * gate1).astype(BF16)

    ins, specs = [], []
    if has_later:
        ins += list(later)
        specs += [_rows(d), _rows(d), _vec(d)]
    else:
        ins += [dxo]
        specs += [_rows(d)]
    if not has_ln:
        ins += [x_out]
        specs += [_rows(d)]
    if has_ln:
        ins += list(this)
        specs += [_rows(d), _rows(d), _vec(d), _vec(d), _vec(d)]
        out_specs = [_rows(d), _rows(d), _vec(d, SUBLANES)]
        out_shape = [
            jax.ShapeDtypeStruct((s, d), F32),
            jax.ShapeDtypeStruct((s, d), BF16),
            jax.ShapeDtypeStruct((SUBLANES, d), F32),
        ]
    else:
        out_specs = [_rows(d), _vec(d, SUBLANES)]
        out_shape = [jax.ShapeDtypeStruct((s, d), F32), jax.ShapeDtypeStruct((SUBLANES, d), F32)]
    return pl.pallas_call(
        body,
        name=name,
        grid=(s // ROW_TILE,),
        in_specs=specs,
        out_specs=out_specs,
        out_shape=out_shape,
        compiler_params=_params(("arbitrary",)),
    )(*ins)


GATE_CHUNKS = 4


def _gelu(x, with_grad=False):
    x2 = x * x
    t = jnp.tanh(GELU_C * (x + GELU_A * x2 * x))
    half = 0.5 * (1.0 + t)
    y = x * half
    if not with_grad:
        return y
    return y, half + 0.5 * x * (1.0 - t * t) * (GELU_C * (1.0 + 3.0 * GELU_A * x2))


def _causal_weights(w_ref, transpose):
    t = lax.broadcasted_iota(jnp.int32, (CHUNK, CHUNK), 0)
    s = lax.broadcasted_iota(jnp.int32, (CHUNK, CHUNK), 1)
    out = []
    for g in range(A_GROUPS):
        w = jnp.where(t >= s, w_ref[g], 0.0)
        out.append((w.T if transpose else w).astype(BF16))
    return out


def _spatial(ws, vn, lo_mask):
    rows = vn.shape[0]
    out_rows = []
    for r in range(rows // CHUNK):
        cols = []
        for j in range(A_GROUPS // 2):
            blk = vn[r * CHUNK : (r + 1) * CHUNK, j * LANES : (j + 1) * LANES]
            za = jnp.dot(ws[2 * j], blk, preferred_element_type=F32)
            zb = jnp.dot(ws[2 * j + 1], blk, preferred_element_type=F32)
            cols.append(jnp.where(lo_mask, za, zb))
        out_rows.append(jnp.concatenate(cols, axis=1))
    return jnp.concatenate(out_rows, axis=0)


def _gate_forward(a, vg, vb, ws, bias, lo_mask, with_grad=False):
    u = _gelu(a[:, :D_MODEL], with_grad)
    v = _gelu(a[:, D_MODEL:], with_grad)
    gu, gv = None, None
    if with_grad:
        (u, gu), (v, gv) = u, v
    vhat, rstd = _layer_norm_hat(v)
    vn = (vhat * vg + vb).astype(BF16)
    z = _spatial(ws, vn, lo_mask) + jnp.concatenate([bias] * (a.shape[0] // CHUNK), axis=0)
    return u, vhat, rstd, vn, z, gu, gv


def gate_fwd(a_pre, vn_g, vn_b, w_s, bias_full, *, name):
    s = a_pre.shape[0]
    tr = GATE_CHUNKS * CHUNK

    def body(a_ref, vg_ref, vb_ref, w_ref, bias_ref, p_ref):
        lo_mask = _lane((CHUNK, LANES)) < A_GROUP_DIM
        ws = _causal_weights(w_ref, transpose=False)
        u, _, _, _, z, _, _ = _gate_forward(a_ref[...].astype(F32), vg_ref[...], vb_ref[...], ws, bias_ref[...], lo_mask)
        p_ref[...] = (u * z).astype(BF16)

    return pl.pallas_call(
        body,
        name=name,
        grid=(s // tr,),
        in_specs=[
            pl.BlockSpec((tr, 2 * D_MODEL), lambda i: (i, 0)),
            _vec(D_MODEL),
            _vec(D_MODEL),
            pl.BlockSpec((A_GROUPS, CHUNK, CHUNK), lambda i: (0, 0, 0)),
            _vec(D_MODEL, CHUNK),
        ],
        out_specs=pl.BlockSpec((tr, D_MODEL), lambda i: (i, 0)),
        out_shape=jax.ShapeDtypeStruct((s, D_MODEL), BF16),
        compiler_params=_params(("parallel",)),
    )(a_pre, vn_g, vn_b, w_s, bias_full)


def gate_bwd(a_pre, dp, vn_g, vn_b, w_s, bias_full, *, name):
    s = a_pre.shape[0]
    tr = GATE_CHUNKS * CHUNK
    nsteps = s // tr

    def body(a_ref, dp_ref, vg_ref, vb_ref, w_ref, bias_ref, da_ref, dw_ref, dbs_ref, rows_ref, dbias_acc):
        step = pl.program_id(0)
        lo_mask = _lane((CHUNK, LANES)) < A_GROUP_DIM

        @pl.when(step == 0)
        def _():
            dw_ref[...] = jnp.zeros_like(dw_ref)
            rows_ref[...] = jnp.zeros_like(rows_ref)
            dbias_acc[...] = jnp.zeros_like(dbias_acc)

        a = a_ref[...].astype(F32)
        vg = vg_ref[...]
        ws = _causal_weights(w_ref, transpose=False)
        wts = _causal_weights(w_ref, transpose=True)
        u, vhat, rstd, vn, z, gelu_du, gelu_dv = _gate_forward(a, vg, vb_ref[...], ws, bias_ref[...], lo_mask, True)
        dp = dp_ref[...]
        du = dp * z
        dzz = dp * u
        dzz_b = dzz.astype(BF16)
        dvn = _spatial(wts, dzz_b, lo_mask)
        dbias = None
        for r in range(GATE_CHUNKS):
            rs = slice(r * CHUNK, (r + 1) * CHUNK)
            dbias = dzz[rs] if dbias is None else dbias + dzz[rs]
            for j in range(A_GROUPS // 2):
                cs = slice(j * LANES, (j + 1) * LANES)
                dblk = dzz[rs, cs]
                vblk = vn[rs, cs]
                for half in range(2):
                    keep = lo_mask if half == 0 else jnp.logical_not(lo_mask)
                    dm = jnp.where(keep, dblk, 0.0).astype(BF16)
                    dw_ref[2 * j + half] += lax.dot_general(
                        dm, vblk, (((1,), (1,)), ((), ())), preferred_element_type=F32
                    )
        dbias_acc[...] += dbias
        rows_ref[1:2, :D_MODEL] += jnp.sum(dvn * vhat, axis=0, keepdims=True)
        rows_ref[1:2, D_MODEL:] += jnp.sum(dvn, axis=0, keepdims=True)
        dvh = dvn * vg
        m1 = jnp.mean(dvh, axis=-1, keepdims=True)
        m2 = jnp.mean(dvh * vhat, axis=-1, keepdims=True)
        dv = rstd * (dvh - m1 - vhat * m2)
        da_u = du * gelu_du
        da_v = dv * gelu_dv
        da_ref[:, :D_MODEL] = da_u.astype(BF16)
        da_ref[:, D_MODEL:] = da_v.astype(BF16)
        rows_ref[0:1, :D_MODEL] += jnp.sum(da_u, axis=0, keepdims=True)
        rows_ref[0:1, D_MODEL:] += jnp.sum(da_v, axis=0, keepdims=True)

        @pl.when(step == nsteps - 1)
        def _():
            t = lax.broadcasted_iota(jnp.int32, (CHUNK, CHUNK), 0)
            sx = lax.broadcasted_iota(jnp.int32, (CHUNK, CHUNK), 1)
            for g in range(A_GROUPS):
                dw_ref[g] = jnp.where(t >= sx, dw_ref[g], 0.0)
            dbs_ref[...] = _reduce_groups(dbias_acc[...])

    return pl.pallas_call(
        body,
        name=name,
        grid=(nsteps,),
        in_specs=[
            pl.BlockSpec((tr, 2 * D_MODEL), lambda i: (i, 0)),
            pl.BlockSpec((tr, D_MODEL), lambda i: (i, 0)),
            _vec(D_MODEL),
            _vec(D_MODEL),
            pl.BlockSpec((A_GROUPS, CHUNK, CHUNK), lambda i: (0, 0, 0)),
            _vec(D_MODEL, CHUNK),
        ],
        out_specs=[
            pl.BlockSpec((tr, 2 * D_MODEL), lambda i: (i, 0)),
            pl.BlockSpec((A_GROUPS, CHUNK, CHUNK), lambda i: (0, 0, 0)),
            _vec(LANES, CHUNK),
            _vec(2 * D_MODEL, SUBLANES),
        ],
        out_shape=[
            jax.ShapeDtypeStruct((s, 2 * D_MODEL), BF16),
            jax.ShapeDtypeStruct((A_GROUPS, CHUNK, CHUNK), F32),
            jax.ShapeDtypeStruct((CHUNK, LANES), F32),
            jax.ShapeDtypeStruct((SUBLANES, 2 * D_MODEL), F32),
        ],
        scratch_shapes=[pltpu.VMEM((CHUNK, D_MODEL), F32)],
        compiler_params=_params(("arbitrary",)),
    )(a_pre, dp, vn_g, vn_b, w_s, bias_full)


def alibi_tables(dilation):
    qi = jnp.arange(SPAN)[:, None]
    ki = jnp.arange(2 * SPAN)[None, :]
    diff = SPAN + qi - ki
    valid = (diff >= 0) & (diff <= SPAN)
    heads = jnp.arange(1, B_HEADS + 1, dtype=F32)
    slopes = jnp.exp2(-8.0 * heads / B_HEADS)
    bias = -slopes[:, None, None] * (dilation * diff).astype(F32)
    bias = jnp.where(valid[None], bias, NEG).reshape(B_HEADS // 2, 2 * SPAN, 2 * SPAN)
    return bias, bias.transpose(0, 2, 1)


def _pair_rows(x, halves):
    return jnp.concatenate([x * halves[0], x * halves[1]], axis=0)


def _pair_column(v, lane, j):
    pick = lambda h: jnp.sum(jnp.where(lane == h, v, 0.0), axis=1, keepdims=True)
    return jnp.concatenate([pick(2 * j), pick(2 * j + 1)], axis=0)


_NT = (((1,), (1,)), ((), ()))


def permute_rows(xs, dilation, *, inverse, name):
    s = xs[0].shape[0]
    tile = SPAN * dilation
    nat = [pl.BlockSpec((tile, x.shape[1]), lambda i: (i, 0)) for x in xs]
    streams = [pl.BlockSpec((dilation, SPAN, x.shape[1]), lambda i: (0, i, 0)) for x in xs]
    stream_shape = [jax.ShapeDtypeStruct((dilation, s // dilation, x.shape[1]), x.dtype) for x in xs]

    def body(*refs):
        for x_ref, o_ref in zip(refs[: len(xs)], refs[len(xs) :], strict=True):
            w = x_ref.shape[-1]
            if inverse:
                o_ref[...] = jnp.swapaxes(x_ref[...], 0, 1).reshape(tile, w)
            else:
                o_ref[...] = jnp.swapaxes(x_ref[...].reshape(SPAN, dilation, w), 0, 1)

    outs = pl.pallas_call(
        body,
        name=name,
        grid=(s // tile,),
        in_specs=streams if inverse else nat,
        out_specs=nat if inverse else streams,
        out_shape=[jax.ShapeDtypeStruct(x.shape, x.dtype) for x in xs] if inverse else stream_shape,
        compiler_params=_params(("parallel",)),
    )(*[x.reshape(dilation, s // dilation, x.shape[1]) if inverse else x for x in xs])
    return [o.reshape(x.shape) for o, x in zip(outs, xs, strict=True)]


def _qkv_specs(block_of):
    def spec(which, prev):
        def index(*grid):
            blk = block_of(*grid)
            return (jnp.maximum(blk - 1, 0) if prev else blk, which)

        return pl.BlockSpec((SPAN, D_MODEL), index)

    return [spec(0, False), spec(1, True), spec(1, False), spec(2, True), spec(2, False)]


def attn_fwd(qkv_p, pat, *, name, after=None):
    _, dilation = B_PATTERNS[pat]
    nb = SEQ // dilation // SPAN
    bias, _ = alibi_tables(dilation)
    order = [] if after is None else [after]

    def body(q_ref, kp_ref, kc_ref, vp_ref, vc_ref, bias_ref, *rest):
        o_ref, lse_ref = rest[len(order) :]
        n = pl.program_id(1)
        first_prev = jnp.logical_and(n == 0, _lane((2 * SPAN, 2 * SPAN)) < SPAN)
        lane = _lane((SPAN, LANES))
        lo_mask = lane < B_HEAD_DIM
        q = q_ref[...] * jnp.asarray(B_HEAD_DIM**-0.5, BF16)
        kk = jnp.concatenate([kp_ref[...], kc_ref[...]], axis=0)
        vv = jnp.concatenate([vp_ref[...], vc_ref[...]], axis=0)
        halves = (lo_mask.astype(BF16), jnp.logical_not(lo_mask).astype(BF16))
        stats = jnp.zeros((SPAN, LANES), F32)
        for j in range(B_HEADS // 2):
            cs = slice(j * LANES, (j + 1) * LANES)
            sc = lax.dot_general(_pair_rows(q[:, cs], halves), kk[:, cs], _NT, preferred_element_type=F32)
            sc = jnp.where(first_prev, NEG, sc + bias_ref[j])
            m = jnp.max(sc, axis=1, keepdims=True)
            p = jnp.exp(sc - m)
            l = jnp.sum(p, axis=1, keepdims=True)
            acc = jnp.dot(p.astype(BF16), vv[:, cs], preferred_element_type=F32) * (1.0 / l)
            lse_pair = m + jnp.log(l)
            o_ref[:, cs] = jnp.where(lo_mask, acc[:SPAN], acc[SPAN:]).astype(BF16)
            stats = jnp.where(lane == 2 * j, lse_pair[:SPAN], stats)
            stats = jnp.where(lane == 2 * j + 1, lse_pair[SPAN:], stats)
        lse_ref[...] = stats

    return pl.pallas_call(
        body,
        name=name,
        grid=(dilation, nb),
        in_specs=[
            *_qkv_specs(lambda r, n: r * nb + n),
            pl.BlockSpec((B_HEADS // 2, 2 * SPAN, 2 * SPAN), lambda r, n: (0, 0, 0)),
            *[ANY for _ in order],
        ],
        out_specs=[
            pl.BlockSpec((SPAN, D_MODEL), lambda r, n: (r * nb + n, 0)),
            pl.BlockSpec((SPAN, LANES), lambda r, n: (r * nb + n, 0)),
        ],
        out_shape=[jax.ShapeDtypeStruct((SEQ, D_MODEL), BF16), jax.ShapeDtypeStruct((SEQ, LANES), F32)],
        compiler_params=_params(("parallel", "arbitrary")),
    )(qkv_p, qkv_p, qkv_p, qkv_p, qkv_p, bias, *order)


def attn_combine(outs, lses, *, name):
    dils = [d for _, d in B_PATTERNS]

    def positions(ref, d):
        x = ref[...].astype(F32)
        return x if d == 1 else jnp.swapaxes(x, 0, 1).reshape(ROW_TILE, x.shape[-1])

    def body(o0, o1, o2, l0, l1, l2, ob_ref, of_ref, *lse_refs):
        ls = [positions(l, d) for l, d in zip((l0, l1, l2), dils, strict=True)]
        m = jnp.maximum(jnp.maximum(ls[0], ls[1]), ls[2])
        tot = jnp.log(jnp.exp(ls[0] - m) + jnp.exp(ls[1] - m) + jnp.exp(ls[2] - m)) + m
        o = None
        for o_ref, l, d in zip((o0, o1, o2), ls, dils, strict=True):
            term = _expand_groups(jnp.exp(l - tot)) * positions(o_ref, d)
            o = term if o is None else o + term
        ob_ref[...] = o.astype(BF16)
        of_ref[...] = o
        for lse_ref, d in zip(lse_refs, dils, strict=True):
            lse_ref[...] = tot if d == 1 else _to_streams(tot, d).reshape(lse_ref.shape)

    def stream_rows(cols, d):
        return _rows(cols) if d == 1 else pl.BlockSpec((d, ROW_TILE // d, cols), lambda i: (0, i, 0))

    def streams(x, d):
        return x if d == 1 else x.reshape(d, SEQ // d, x.shape[-1])

    outs = [streams(x, d) for x, d in zip(outs, dils, strict=True)]
    lses = [streams(x, d) for x, d in zip(lses, dils, strict=True)]
    return pl.pallas_call(
        body,
        name=name,
        grid=(SEQ // ROW_TILE,),
        in_specs=[stream_rows(D_MODEL, d) for d in dils] + [stream_rows(LANES, d) for d in dils],
        out_specs=[_rows(D_MODEL), _rows(D_MODEL)] + [stream_rows(LANES, d) for d in dils],
        out_shape=[jax.ShapeDtypeStruct((SEQ, D_MODEL), BF16), jax.ShapeDtypeStruct((SEQ, D_MODEL), F32)]
        + [jax.ShapeDtypeStruct(lse.shape, F32) for lse in lses],
        compiler_params=_params(("parallel",)),
    )(*outs, *lses)


def attn_delta(do, o, *, name):
    def body(do_ref, o_ref, dob_ref, dl_ref):
        do_v = do_ref[...]
        dob_ref[...] = do_v.astype(BF16)
        dl_ref[...] = _reduce_groups(do_v * o_ref[...])

    return pl.pallas_call(
        body,
        name=name,
        grid=(SEQ // ROW_TILE,),
        in_specs=[_rows(D_MODEL), _rows(D_MODEL)],
        out_specs=[_rows(D_MODEL), _rows(LANES)],
        out_shape=[jax.ShapeDtypeStruct((SEQ, D_MODEL), BF16), jax.ShapeDtypeStruct((SEQ, LANES), F32)],
        compiler_params=_params(("parallel",)),
    )(do, o)


def attn_bwd(qkv_p, do_p, lse_p, delta_p, pat, *, name):
    _, dilation = B_PATTERNS[pat]
    nb = SEQ // dilation // SPAN
    n_blocks = SEQ // SPAN
    bias, bias_t = alibi_tables(dilation)
    last = n_blocks - 1
    q_cols, k_cols, v_cols = (slice(i * D_MODEL, (i + 1) * D_MODEL) for i in range(3))

    def body(q_ref, kp_ref, kc_ref, vp_ref, vc_ref, do_ref, lse_ref, dl_ref, bias_ref, biast_ref, out_ref, cq_ref, ck_ref, cv_ref):
        g = pl.program_id(0)

        @pl.when(g == n_blocks)
        def _():
            out_ref[:, q_cols] = cq_ref[...].astype(BF16)
            out_ref[:, k_cols] = ck_ref[...].astype(BF16)
            out_ref[:, v_cols] = cv_ref[...].astype(BF16)

        @pl.when(g == 0)
        def _():
            cq_ref[...] = jnp.zeros_like(cq_ref)
            ck_ref[...] = jnp.zeros_like(ck_ref)
            cv_ref[...] = jnp.zeros_like(cv_ref)

        @pl.when(g < n_blocks)
        def _():
            lane = _lane((SPAN, LANES))
            lo_mask = lane < B_HEAD_DIM
            pair = (2 * SPAN, 2 * SPAN)
            first = lax.rem(g, nb) == 0
            prev_key_cols = jnp.logical_and(first, _lane(pair) < SPAN)
            prev_key_rows = jnp.logical_and(first, lax.broadcasted_iota(jnp.int32, pair, 0) < SPAN)
            q = q_ref[...] * jnp.asarray(B_HEAD_DIM**-0.5, BF16)
            kk = jnp.concatenate([kp_ref[...], kc_ref[...]], axis=0)
            vv = jnp.concatenate([vp_ref[...], vc_ref[...]], axis=0)
            do_v = do_ref[...]
            lse_v = lse_ref[...]
            dl_v = dl_ref[...]
            lse_t = lse_v.T
            dl_t = dl_v.T
            halves = (lo_mask.astype(BF16), jnp.logical_not(lo_mask).astype(BF16))
            for j in range(B_HEADS // 2):
                cs = slice(j * LANES, (j + 1) * LANES)
                kp, vp = kk[:, cs], vv[:, cs]
                q2 = _pair_rows(q[:, cs], halves)
                do2 = _pair_rows(do_v[:, cs], halves)
                lse_c, dl_c = _pair_column(lse_v, lane, j), _pair_column(dl_v, lane, j)
                lse_r = jnp.concatenate([lse_t[2 * j : 2 * j + 1], lse_t[2 * j + 1 : 2 * j + 2]], axis=1)
                dl_r = jnp.concatenate([dl_t[2 * j : 2 * j + 1], dl_t[2 * j + 1 : 2 * j + 2]], axis=1)
                sc = lax.dot_general(q2, kp, _NT, preferred_element_type=F32)
                p = jnp.exp(jnp.where(prev_key_cols, NEG, sc + bias_ref[j]) - lse_c)
                dp = lax.dot_general(do2, vp, _NT, preferred_element_type=F32)
                ds = (p * (dp - dl_c)).astype(BF16)
                dq2 = jnp.dot(ds, kp, preferred_element_type=F32)
                sc_t = lax.dot_general(kp, q2, _NT, preferred_element_type=F32)
                p_t = jnp.exp(jnp.where(prev_key_rows, NEG, sc_t + biast_ref[j]) - lse_r)
                dp_t = lax.dot_general(vp, do2, _NT, preferred_element_type=F32)
                ds_t = (p_t * (dp_t - dl_r)).astype(BF16)
                dk_pair = jnp.dot(ds_t, q2, preferred_element_type=F32)
                dv_pair = jnp.dot(p_t.astype(BF16), do2, preferred_element_type=F32)
                oq = slice(j * LANES, (j + 1) * LANES)
                ok = slice(D_MODEL + j * LANES, D_MODEL + (j + 1) * LANES)
                ov = slice(2 * D_MODEL + j * LANES, 2 * D_MODEL + (j + 1) * LANES)
                out_ref[:, oq] = cq_ref[:, cs].astype(BF16)
                out_ref[:, ok] = (ck_ref[:, cs] + dk_pair[:SPAN]).astype(BF16)
                out_ref[:, ov] = (cv_ref[:, cs] + dv_pair[:SPAN]).astype(BF16)
                cq_ref[:, cs] = jnp.where(lo_mask, dq2[:SPAN], dq2[SPAN:]) * (B_HEAD_DIM**-0.5)
                ck_ref[:, cs] = dk_pair[SPAN:]
                cv_ref[:, cs] = dv_pair[SPAN:]

    def block_of(g):
        return jnp.minimum(g, last)

    def row_spec(width):
        return pl.BlockSpec((SPAN, width), lambda g: (block_of(g), 0))

    return pl.pallas_call(
        body,
        name=name,
        grid=(n_blocks + 1,),
        in_specs=[
            *_qkv_specs(block_of),
            row_spec(D_MODEL),
            row_spec(LANES),
            row_spec(LANES),
            pl.BlockSpec((B_HEADS // 2, 2 * SPAN, 2 * SPAN), lambda g: (0, 0, 0)),
            pl.BlockSpec((B_HEADS // 2, 2 * SPAN, 2 * SPAN), lambda g: (0, 0, 0)),
        ],
        out_specs=pl.BlockSpec((SPAN, 3 * D_MODEL), lambda g: (jnp.maximum(g - 1, 0), 0)),
        out_shape=jax.ShapeDtypeStruct((SEQ, 3 * D_MODEL), BF16),
        scratch_shapes=[pltpu.VMEM((SPAN, D_MODEL), F32)] * 3,
        compiler_params=_params(("arbitrary",)),
    )(qkv_p, qkv_p, qkv_p, qkv_p, qkv_p, do_p, lse_p, delta_p, bias, bias_t)


def _position():
    x, y, c = lax.axis_index("x"), lax.axis_index("y"), lax.axis_index("c")
    return x, y, c, 4 * x + 2 * y + c


def _peer(k, x, y, c):
    px = 1 - x if k & 4 else x
    py = 1 - y if k & 2 else y
    pc = 1 - c if k & 1 else c
    return (px, py, pc), 4 * px + 2 * py + pc


def _remote(src, dst, send_sem, recv_sem, device):
    return pltpu.make_async_remote_copy(
        src_ref=src, dst_ref=dst, send_sem=send_sem, recv_sem=recv_sem, device_id=device, device_id_type=MESH
    )


def _silu_bf16(cf):
    return (cf * (1.0 / (1.0 + jnp.exp(-cf)))).astype(BF16)


def ada_exchange(c8, w4, b4, ln8):
    nt, _, ncol = w4.shape

    def body(c8_ref, w_ref, b_ref, ln_ref, cg_ref, lng_ref, mrecv_ref, mloc_ref, send_sems, recv_sems):
        x, y, c, me = _position()
        cg_ref[me] = c8_ref[...]
        lng_ref[me] = ln_ref[...]
        first = []
        for k in range(1, N_DEV):
            dev, _ = _peer(k, x, y, c)
            first.append(_remote(c8_ref, cg_ref.at[me], send_sems.at[0, k], recv_sems.at[0, k], dev))
            first.append(_remote(ln_ref, lng_ref.at[me], send_sems.at[1, k], recv_sems.at[1, k], dev))
        for cp in first:
            cp.start()
        for k in range(1, N_DEV):
            dev, pid = _peer(k, x, y, c)
            _remote(c8_ref, cg_ref.at[pid], send_sems.at[0, k], recv_sems.at[0, k], dev).wait_recv()
            _remote(ln_ref, lng_ref.at[pid], send_sems.at[1, k], recv_sems.at[1, k], dev).wait_recv()
        sc = _silu_bf16(cg_ref[...].reshape(N_DEV * SUBLANES, D_MODEL))
        for t in range(nt):
            mloc_ref[t] = jnp.dot(sc, w_ref[t].astype(BF16), preferred_element_type=F32) + b_ref[t : t + 1, :]

        def group(dev_id):
            return pl.ds(pl.multiple_of(dev_id * SUBLANES, SUBLANES), SUBLANES)

        mrecv_ref[me] = mloc_ref[:, group(me), :]
        second = []
        for k in range(1, N_DEV):
            dev, pid = _peer(k, x, y, c)
            second.append(
                _remote(mloc_ref.at[:, group(pid), :], mrecv_ref.at[me], send_sems.at[2, k], recv_sems.at[2, k], dev)
            )
        for cp in second:
            cp.start()
        for k in range(1, N_DEV):
            dev, pid = _peer(k, x, y, c)
            _remote(
                mloc_ref.at[:, group(pid), :], mrecv_ref.at[pid], send_sems.at[2, k], recv_sems.at[2, k], dev
            ).wait_recv()
        for cp in first + second:
            cp.wait_send()

    return pl.pallas_call(
        body,
        name="ada_exchange",
        in_specs=[VMEM, VMEM, VMEM, VMEM],
        out_specs=[VMEM, VMEM, VMEM],
        out_shape=[
            jax.ShapeDtypeStruct((N_DEV, SUBLANES, D_MODEL), F32),
            jax.ShapeDtypeStruct((N_DEV, SUBLANES, LANES), F32),
            jax.ShapeDtypeStruct((N_DEV, nt, SUBLANES, ncol), F32),
        ],
        scratch_shapes=[
            pltpu.VMEM((nt, N_DEV * SUBLANES, ncol), F32),
            pltpu.SemaphoreType.DMA((3, N_DEV)),
            pltpu.SemaphoreType.DMA((3, N_DEV)),
        ],
        compiler_params=pltpu.CompilerParams(vmem_limit_bytes=VMEM_LIMIT_BYTES),
    )(c8, w4, b4, ln8)


def small_exchange(dmx, flat, after):
    def body(dmx_ref, flat_ref, _, dmrecv_ref, red_ref, land_ref, send_sems, recv_sems):
        x, y, c, me = _position()
        dmrecv_ref[me] = dmx_ref[me]
        land_ref[me] = flat_ref[me]
        first = []
        for k in range(1, N_DEV):
            dev, pid = _peer(k, x, y, c)
            first.append(_remote(dmx_ref.at[pid], dmrecv_ref.at[me], send_sems.at[0, k], recv_sems.at[0, k], dev))
            first.append(_remote(flat_ref.at[pid], land_ref.at[me], send_sems.at[1, k], recv_sems.at[1, k], dev))
        for cp in first:
            cp.start()
        for k in range(1, N_DEV):
            dev, pid = _peer(k, x, y, c)
            _remote(dmx_ref.at[pid], dmrecv_ref.at[pid], send_sems.at[0, k], recv_sems.at[0, k], dev).wait_recv()
            _remote(flat_ref.at[pid], land_ref.at[pid], send_sems.at[1, k], recv_sems.at[1, k], dev).wait_recv()
        total = land_ref[0]
        for s in range(1, N_DEV):
            total = total + land_ref[s]
        red_ref[me] = total
        second = []
        for k in range(1, N_DEV):
            dev, _ = _peer(k, x, y, c)
            second.append(_remote(red_ref.at[me], red_ref.at[me], send_sems.at[2, k], recv_sems.at[2, k], dev))
        for cp in second:
            cp.start()
        for k in range(1, N_DEV):
            dev, pid = _peer(k, x, y, c)
            _remote(red_ref.at[pid], red_ref.at[pid], send_sems.at[2, k], recv_sems.at[2, k], dev).wait_recv()
        for cp in first + second:
            cp.wait_send()

    return pl.pallas_call(
        body,
        name="small_exchange",
        in_specs=[VMEM, VMEM, ANY],
        out_specs=[VMEM, VMEM],
        out_shape=[jax.ShapeDtypeStruct(dmx.shape, F32), jax.ShapeDtypeStruct(flat.shape, F32)],
        scratch_shapes=[
            pltpu.VMEM(flat.shape, F32),
            pltpu.SemaphoreType.DMA((3, N_DEV)),
            pltpu.SemaphoreType.DMA((3, N_DEV)),
        ],
        compiler_params=pltpu.CompilerParams(vmem_limit_bytes=VMEM_LIMIT_BYTES),
    )(dmx, flat, after)


HBM = pl.BlockSpec(memory_space=pltpu.HBM)
SEM = pl.BlockSpec(memory_space=pltpu.SEMAPHORE)
EFFECT = pltpu.SideEffectType.DATAFLOW_SIDE_EFFECTING


REGROUP_ROWS = 256


def shards_to_columns(x, *, name):
    p, k, n = x.shape

    def body(x_ref, o_ref):
        for s in range(p):
            o_ref[:, s * n : (s + 1) * n] = x_ref[s]

    return pl.pallas_call(
        body,
        name=name,
        grid=(k // REGROUP_ROWS,),
        in_specs=[pl.BlockSpec((p, REGROUP_ROWS, n), lambda i: (0, i, 0))],
        out_specs=pl.BlockSpec((REGROUP_ROWS, p * n), lambda i: (i, 0)),
        out_shape=jax.ShapeDtypeStruct((k, p * n), x.dtype),
        compiler_params=_params(("parallel",)),
    )(x)


def columns_to_shards(xs, *, name):
    k = xs[0].shape[0]
    widths = [x.shape[1] for x in xs]
    n = sum(widths) // N_DEV
    pieces = []
    for s in range(N_DEV):
        start = 0
        for i, w in enumerate(widths):
            lo, hi = max(start, s * n), min(start + w, (s + 1) * n)
            if lo < hi:
                pieces.append((i, lo - start, s, lo - s * n, hi - lo))
            start += w

    def body(*refs):
        x_refs, o_ref = refs[: len(xs)], refs[-1]
        for i, c0, s, d0, w in pieces:
            o_ref[s, :, d0 : d0 + w] = x_refs[i][:, c0 : c0 + w]

    return pl.pallas_call(
        body,
        name=name,
        grid=(k // REGROUP_ROWS,),
        in_specs=[pl.BlockSpec((REGROUP_ROWS, w), lambda i: (i, 0)) for w in widths],
        out_specs=pl.BlockSpec((N_DEV, REGROUP_ROWS, n), lambda i: (0, i, 0)),
        out_shape=jax.ShapeDtypeStruct((N_DEV, k, n), xs[0].dtype),
        compiler_params=_params(("parallel",)),
    )(*xs)


def _own_slot(me, block):
    land = lax.empty((N_DEV, *block.shape), block.dtype)
    return lax.dynamic_update_slice_in_dim(land, block[None], me, axis=0)


N_CHIP_PEERS = 3


class Gather:
    def __init__(self, shards, lands, after, *, name):
        nt = len(shards)
        self.name = name

        def body(*refs):
            src_refs, land_refs = refs[:nt], refs[nt : 2 * nt]
            send_sems, recv_sems = refs[2 * nt + 1 : 3 * nt + 1], refs[3 * nt + 1 : 4 * nt + 1]
            token = refs[-1]
            x, y, c, me = _position()
            for t in range(nt):
                for k, dev in enumerate(self._targets(x, y, c)):
                    _remote(src_refs[t], land_refs[t].at[me], send_sems[t].at[k], recv_sems[t].at[k], dev).start()
            token[...] = jnp.zeros_like(token)

        outs = pl.pallas_call(
            body,
            name=name + "_start",
            in_specs=[HBM] * (2 * nt) + [ANY],
            out_specs=[SEM] * (2 * nt) + [HBM] * (2 * nt) + [VMEM],
            out_shape=[pltpu.SemaphoreType.DMA((1 + N_CHIP_PEERS,))] * (2 * nt)
            + [pltpu.HBM(a.shape, a.dtype) for a in (*shards, *lands)]
            + [jax.ShapeDtypeStruct((SUBLANES, LANES), F32)],
            input_output_aliases={i: 2 * nt + i for i in range(2 * nt)},
            compiler_params=pltpu.CompilerParams(has_side_effects=EFFECT),
        )(*[pltpu.with_memory_space_constraint(a, pltpu.HBM) for a in (*shards, *lands)], after)
        self.send_sems, self.recv_sems = list(outs[:nt]), list(outs[nt : 2 * nt])
        self.srcs, self.lands = list(outs[2 * nt : 3 * nt]), list(outs[3 * nt : 4 * nt])
        self.token = outs[-1]

    @staticmethod
    def _chips(x, y):
        return [(1 - x, y), (x, 1 - y), (1 - x, 1 - y)]

    @classmethod
    def _targets(cls, x, y, c):
        return [(x, y, 1 - c)] + [(*chip, c) for chip in cls._chips(x, y)]

    def zero(self):
        return self.token[0, 0]

    @staticmethod
    def _slot(px, py, pc):
        return 4 * px + 2 * py + pc

    def pass_on(self, which, after, *, name):
        n = len(which)

        def pass_body(*refs):
            land_refs, recv_sems = refs[:n], refs[n : 2 * n]
            fwd_send, fwd_recv = refs[3 * n + 1 : 4 * n + 1], refs[4 * n + 1 : 5 * n + 1]
            token = refs[-1]
            x, y, c, _ = _position()
            for t in range(n):
                for j, chip in enumerate(self._chips(x, y)):
                    blk = land_refs[t].at[self._slot(*chip, c)]
                    _remote(blk, blk, fwd_send[t].at[j], recv_sems[t].at[1 + j], (*chip, c)).wait_recv()
                    _remote(blk, blk, fwd_send[t].at[j], fwd_recv[t].at[j], (x, y, 1 - c)).start()
            token[...] = jnp.zeros_like(token)

        lands = [self.lands[t] for t in which]
        outs = pl.pallas_call(
            pass_body,
            name=name,
            in_specs=[HBM] * n + [SEM] * n + [ANY],
            out_specs=[HBM] * n + [SEM] * (2 * n) + [VMEM],
            out_shape=[pltpu.HBM(a.shape, a.dtype) for a in lands]
            + [pltpu.SemaphoreType.DMA((N_CHIP_PEERS,))] * (2 * n)
            + [jax.ShapeDtypeStruct((SUBLANES, LANES), F32)],
            input_output_aliases={i: i for i in range(n)},
            compiler_params=pltpu.CompilerParams(has_side_effects=EFFECT),
        )(*lands, *[self.recv_sems[t] for t in which], after)
        return (which, outs[:n], outs[n : 2 * n], outs[2 * n : 3 * n]), outs[-1]

    def wait(self, which, after, *, name):
        return self.finish(self.pass_on(which, after, name=name + "_pass")[0], after, name=name)

    def finish(self, passed, after, *, name):
        which, lands, fwd_send, fwd_recv = passed
        n = len(which)
        slot = self._slot

        def wait_body(*refs):
            src_refs, land_refs = refs[:n], refs[n : 2 * n]
            send_sems, recv_sems = refs[2 * n : 3 * n], refs[3 * n : 4 * n]
            fwd_send, fwd_recv = refs[4 * n : 5 * n], refs[5 * n : 6 * n]
            x, y, c, me = _position()
            sibling = (x, y, 1 - c)
            for t in range(n):
                for k, dev in enumerate(self._targets(x, y, c)):
                    _remote(src_refs[t], land_refs[t].at[me], send_sems[t].at[k], recv_sems[t].at[k], dev).wait_send()
                blk = land_refs[t].at[slot(x, y, 1 - c)]
                _remote(blk, blk, send_sems[t].at[0], recv_sems[t].at[0], sibling).wait_recv()
                for j, chip in enumerate(self._chips(x, y)):
                    sent = land_refs[t].at[slot(*chip, c)]
                    _remote(sent, sent, fwd_send[t].at[j], fwd_recv[t].at[j], sibling).wait_send()
                    got = land_refs[t].at[slot(*chip, 1 - c)]
                    _remote(got, got, fwd_send[t].at[j], fwd_recv[t].at[j], sibling).wait_recv()

        srcs = [self.srcs[t] for t in which]
        outs = pl.pallas_call(
            wait_body,
            name=name,
            in_specs=[HBM] * (2 * n) + [SEM] * (4 * n) + [ANY],
            out_specs=[HBM] * (2 * n),
            out_shape=[pltpu.HBM(a.shape, a.dtype) for a in (*srcs, *lands)],
            input_output_aliases={i: i for i in range(2 * n)},
            compiler_params=pltpu.CompilerParams(has_side_effects=EFFECT),
        )(
            *srcs, *lands, *[self.send_sems[t] for t in which], *[self.recv_sems[t] for t in which], *fwd_send,
            *fwd_recv, after,
        )
        return outs[n:]


class Scatter:
    def __init__(self, srcs, lands, after, *, name):
        self.name = name
        nt = self.nt = len(srcs)
        peers = N_DEV - 1

        def body(*refs):
            src_refs, land_refs = refs[:nt], refs[nt : 2 * nt]
            send_sems, recv_sems = refs[2 * nt + 1 : 3 * nt + 1], refs[3 * nt + 1 : 4 * nt + 1]
            token = refs[-1]
            x, y, c, me = _position()
            for t in range(nt):
                for k in range(1, N_DEV):
                    dev, pid = _peer(k, x, y, c)
                    src = src_refs[t].at[pid]
                    _remote(src, land_refs[t].at[me], send_sems[t].at[k - 1], recv_sems[t].at[k - 1], dev).start()
            token[...] = jnp.zeros_like(token)

        outs = pl.pallas_call(
            body,
            name=name + "_start",
            in_specs=[HBM] * (2 * nt) + [ANY],
            out_specs=[SEM] * (2 * nt) + [HBM] * (2 * nt) + [VMEM],
            out_shape=[pltpu.SemaphoreType.DMA((peers,))] * (2 * nt)
            + [pltpu.HBM(a.shape, a.dtype) for a in (*srcs, *lands)]
            + [jax.ShapeDtypeStruct((SUBLANES, LANES), F32)],
            input_output_aliases={i: 2 * nt + i for i in range(2 * nt)},
            compiler_params=pltpu.CompilerParams(has_side_effects=EFFECT),
        )(*[pltpu.with_memory_space_constraint(a, pltpu.HBM) for a in (*srcs, *lands)], after)
        self.send_sems, self.recv_sems = outs[:nt], outs[nt : 2 * nt]
        self.srcs, self.lands = outs[2 * nt : 3 * nt], outs[3 * nt : 4 * nt]
        self.token = outs[-1]

    def zero(self):
        return self.token[0, 0]

    def wait(self, which, after, *, name):
        n = len(which)

        def body(*refs):
            src_refs, land_refs = refs[:n], refs[n : 2 * n]
            send_sems, recv_sems = refs[2 * n : 3 * n], refs[3 * n : 4 * n]
            x, y, c, _ = _position()
            for t in range(n):
                for k in range(1, N_DEV):
                    dev, pid = _peer(k, x, y, c)
                    src = src_refs[t].at[pid]
                    cp = _remote(src, land_refs[t].at[pid], send_sems[t].at[k - 1], recv_sems[t].at[k - 1], dev)
                    cp.wait_send()
                    cp.wait_recv()

        srcs = [self.srcs[t] for t in which]
        lands = [self.lands[t] for t in which]
        outs = pl.pallas_call(
            body,
            name=name,
            in_specs=[HBM] * (2 * n) + [SEM] * (2 * n) + [ANY],
            out_specs=[HBM] * (2 * n),
            out_shape=[pltpu.HBM(a.shape, a.dtype) for a in (*srcs, *lands)],
            input_output_aliases={i: i for i in range(2 * n)},
            compiler_params=pltpu.CompilerParams(has_side_effects=EFFECT),
        )(*srcs, *lands, *[self.send_sems[t] for t in which], *[self.recv_sems[t] for t in which], after)
        return outs[n:]


def _adam_update(g, w, m, v):
    m2 = ADAM_B1 * m + (1.0 - ADAM_B1) * g
    v2 = ADAM_B2 * v + (1.0 - ADAM_B2) * jnp.square(g)
    m_hat = m2 / (1.0 - ADAM_B1**ADAM_STEP)
    v_hat = v2 / (1.0 - ADAM_B2**ADAM_STEP)
    delta = -ADAM_LR * (m_hat / (jnp.sqrt(v_hat) + ADAM_EPS) + ADAM_WD * w)
    return delta, m2, v2


def adamw(gparts, w, m, v, *, name):
    nl, r, c = w.shape
    p = gparts[0].shape[0]
    tr = r if r <= 256 else (256 if c <= D_MODEL else 128)
    ni = r // tr

    def body(*refs):
        g_refs = refs[:nl]
        w_ref, m_ref, v_ref, go_ref, d_ref, mo_ref, vo_ref = refs[nl:]
        for layer in range(nl):

            @pl.when(pl.program_id(0) == layer)
            def _(g_ref=g_refs[layer]):
                g = g_ref[0].astype(F32)
                for i in range(1, p):
                    g = g + g_ref[i].astype(F32)
                delta, m2, v2 = _adam_update(g, w_ref[...], m_ref[...], v_ref[...])
                go_ref[...] = g
                d_ref[...] = delta
                mo_ref[...] = m2
                vo_ref[...] = v2

    def parts_spec(layer):
        def index(l, i):
            return (0, jnp.where(l == layer, i, jnp.where(l < layer, 0, ni - 1)), 0)

        return pl.BlockSpec((p, tr, c), index)

    blk = pl.BlockSpec((None, tr, c), lambda l, i: (l, i, 0))
    return pl.pallas_call(
        body,
        name=name,
        grid=(nl, ni),
        in_specs=[*[parts_spec(layer) for layer in range(nl)], blk, blk, blk],
        out_specs=[blk] * 4,
        out_shape=[jax.ShapeDtypeStruct((nl, r, c), F32)] * 4,
        compiler_params=_params(("arbitrary", "arbitrary")),
    )(*gparts, w, m, v)


def adamw_small(items, *, name):
    n = len(items)

    def body(*refs):
        ins, outs = refs[: 4 * n], refs[4 * n :]
        for t in range(n):
            g_ref, w_ref, m_ref, v_ref = ins[4 * t : 4 * t + 4]
            g = g_ref[...]
            delta, m2, v2 = _adam_update(g, w_ref[...], m_ref[...], v_ref[...])
            for o_ref, val in zip(outs[4 * t : 4 * t + 4], (g, delta, m2, v2), strict=True):
                o_ref[...] = val

    outs = pl.pallas_call(
        body,
        name=name,
        out_shape=[jax.ShapeDtypeStruct(item[1].shape, F32) for item in items for _ in range(4)],
    )(*[a for item in items for a in item])
    return [outs[4 * t : 4 * t + 4] for t in range(n)]


def ada_grad_adamw(cg, dmrecv, w4, m4, v4, *, name):
    nt, k, ncol = w4.shape

    def body(cg_ref, dm_ref, w_ref, m_ref, v_ref, go_ref, d_ref, mo_ref, vo_ref, gb_ref):
        sc = _silu_bf16(cg_ref[...].reshape(N_DEV * SUBLANES, k))
        dm = dm_ref[...].reshape(N_DEV * SUBLANES, ncol)
        g = lax.dot_general(sc, dm.astype(BF16), (((0,), (0,)), ((), ())), preferred_element_type=F32)
        delta, m2, v2 = _adam_update(g, w_ref[...], m_ref[...], v_ref[...])
        go_ref[...] = g
        d_ref[...] = delta
        mo_ref[...] = m2
        vo_ref[...] = v2
        gb_ref[...] = jnp.broadcast_to(jnp.sum(dm, axis=0, keepdims=True), (SUBLANES, ncol))

    wblk = pl.BlockSpec((None, k, ncol), lambda t: (t, 0, 0))
    return pl.pallas_call(
        body,
        name=name,
        grid=(nt,),
        in_specs=[
            pl.BlockSpec((N_DEV, SUBLANES, k), lambda t: (0, 0, 0)),
            pl.BlockSpec((N_DEV, None, SUBLANES, ncol), lambda t: (0, t, 0, 0)),
            wblk,
            wblk,
            wblk,
        ],
        out_specs=[wblk] * 4 + [pl.BlockSpec((None, SUBLANES, ncol), lambda t: (t, 0, 0))],
        out_shape=[jax.ShapeDtypeStruct((nt, k, ncol), F32)] * 4 + [jax.ShapeDtypeStruct((nt, SUBLANES, ncol), F32)],
        compiler_params=_params(("parallel",)),
    )(cg, dmrecv, w4, m4, v4)


def kernel(x, c, ada_w, ada_b, ln_g, ln_b, a_w_in, a_b_in, a_vn_g, a_vn_b, a_w_s, a_b_s, a_w_out, b_w_qkv, b_w_out, mlp_w_up, mlp_w_down, loss_target, m_ada_w, m_ada_b, m_ln_g, m_ln_b, m_a_w_in, m_a_b_in, m_a_vn_g, m_a_vn_b, m_a_w_s, m_a_b_s, m_a_w_out, m_b_w_qkv, m_b_w_out, m_mlp_w_up, m_mlp_w_down, v_ada_w, v_ada_b, v_ln_g, v_ln_b, v_a_w_in, v_a_b_in, v_a_vn_g, v_a_vn_b, v_a_w_s, v_a_b_s, v_a_w_out, v_b_w_qkv, v_b_w_out, v_mlp_w_up, v_mlp_w_down):
    x0 = x[0]
    target = loss_target[0]
    me = 4 * lax.axis_index("x") + 2 * lax.axis_index("y") + lax.axis_index("c")

    ada_w4 = ada_w.reshape(N_SUB, D_MODEL, -1)
    ada_b4 = ada_b.reshape(N_SUB, -1)
    ln8 = jnp.concatenate([ln_g.reshape(N_SUB, -1), ln_b.reshape(N_SUB, -1)], axis=0)
    c8 = jnp.broadcast_to(c, (SUBLANES, D_MODEL))
    cg, lng, mrecv = ada_exchange(c8, ada_w4, ada_b4, ln8)

    W_IN, W_AOUT, W_UP0, W_DN0, W_QKV, W_BOUT, W_UP1, W_DN1 = range(8)
    shards = [
        a_w_in[0].astype(BF16),
        a_w_out[0].astype(BF16),
        mlp_w_up[0].astype(BF16),
        mlp_w_down[0].astype(BF16),
        b_w_qkv[0].astype(BF16),
        b_w_out[0].astype(BF16),
        mlp_w_up[1].astype(BF16),
        mlp_w_down[1].astype(BF16),
    ]
    gather = Gather(shards, [_own_slot(me, s) for s in shards], mrecv, name="gather")

    modv = mrecv[:, :, 0, :].transpose(1, 0, 2).reshape(N_SUB, 3 * D_MODEL) + gather.zero()
    shift = [modv[t : t + 1, :D_MODEL] for t in range(N_SUB)]
    scale = [modv[t : t + 1, D_MODEL : 2 * D_MODEL] for t in range(N_SUB)]
    gate1 = [1.0 + modv[t : t + 1, 2 * D_MODEL :] for t in range(N_SUB)]
    lng_full = [lng[:, t, :].reshape(1, D_MODEL) for t in range(N_SUB)]
    lnb_full = [lng[:, N_SUB + t, :].reshape(1, D_MODEL) for t in range(N_SUB)]

    ident = lambda acc: (acc,)
    def relu2(a):
        r = jnp.maximum(a, jnp.zeros_like(a))
        return r * r
    vn_g, vn_b, w_s = a_vn_g, a_vn_b, a_w_s[0]
    bias_full = jnp.repeat(a_b_s[0].T, A_GROUP_DIM, axis=1)
    w_up3, w_dn3 = [None, None], [None, None]

    def mlp_forward(i, h, up, dn, x_in=None, t_next=None):
        w_up3[i], w_dn3[i] = up, dn.reshape(1, D_FF, D_MODEL)
        (a,) = mm_nn(h, w_up3[i], name=f"mlp{i}_up", tm=2048, ps=2, tn=512, tk=D_MODEL, epilogue=ident, outs=(BF16,))
        if t_next is None:
            (y,) = mm_nn(
                a, w_dn3[i], name=f"mlp{i}_down", tm=1024, ps=1, tn=512, tk=D_FF, prologue=relu2, epilogue=ident, outs=(BF16,)
            )
            return a, y
        y, xn, hn = mm_nn(
            a, w_dn3[i], name=f"mlp{i}_down", tm=512, ps=1, tn=D_MODEL, tk=D_FF, prologue=relu2,
            epilogue=residual_ln_epilogue, extras=residual_extras(x_in, t_next - 1), outs=(BF16, F32, BF16),
        )
        return a, y, xn, hn

    def residual_extras(x_in, t):
        rows = (gate1[t], lng_full[t], lnb_full[t], scale[t + 1], shift[t + 1])
        return [(x_in, "full")] + [(r, "row") for r in rows]

    h0 = modulate(x0, scale[0], shift[0], name="modulate0")
    w_in3, w_aout3 = gather.wait([W_IN, W_AOUT], h0, name="gather_wait_a")
    w_aout3 = w_aout3.reshape(1, D_MODEL, D_MODEL)
    (a_pre,) = mm_nn(
        h0, w_in3, name="a_in", tm=2048, ps=4, tn=256, tk=D_MODEL, epilogue=lambda acc, b: (acc + b,),
        extras=[(a_b_in, "row")], outs=(BF16,),
    )
    p_gate = gate_fwd(a_pre, vn_g, vn_b, w_s, bias_full, name="gate_fwd")
    y0, x1, h1 = mm_nn(
        p_gate, w_aout3, name="a_out", tm=1024, ps=1, tn=D_MODEL, tk=D_MODEL, epilogue=residual_ln_epilogue,
        extras=residual_extras(x0, 0), outs=(BF16, F32, BF16),
    )
    w_mlp0 = gather.wait([W_UP0, W_DN0], y0, name="gather_wait_mlp0")
    a1, y1, x2, h2 = mlp_forward(0, h1, *w_mlp0, x_in=x1, t_next=2)
    (w_qkv_shards,) = gather.wait([W_QKV], y1, name="gather_wait_qkv")
    w_qkv3 = shards_to_columns(w_qkv_shards, name="w_qkv_columns")[None]
    pat_tiles = 3
    dil = [d for _, d in B_PATTERNS]
    qkv_p, pat_o, pat_lse = [], [], []
    for g in range(N_PAT):
        (qkv_g,) = mm_nn(
            h2, w_qkv3, name=f"b_qkv{g}", tm=2048, ps=1, tn=D_MODEL, tk=D_MODEL, epilogue=ident, outs=(BF16,),
            b_tile0=pat_tiles * g, b_tiles=pat_tiles, out_streams=dil[g],
        )
        o_g, lse_g = attn_fwd(qkv_g, g, name=f"attn_fwd{g}", after=rest_token if g == 1 else None)
        if g == 0:
            rest_passed, rest_token = gather.pass_on([W_BOUT, W_UP1, W_DN1], lse_g, name="gather_pass_rest")
        qkv_p.append(qkv_g)
        pat_o.append(o_g)
        pat_lse.append(lse_g)
    o_b, o_f, *lse_pat = attn_combine(pat_o, pat_lse, name="attn_combine")
    lse_pat = [x.reshape(SEQ, LANES) for x in lse_pat]
    w_bout3, *w_mlp1 = gather.finish(rest_passed, lse_pat[0], name="gather_wait_rest")
    w_bout3 = w_bout3.reshape(1, D_MODEL, D_MODEL)
    y2, x3, h3 = mm_nn(
        o_b, w_bout3, name="b_out", tm=1024, ps=1, tn=D_MODEL, tk=D_MODEL, epilogue=residual_ln_epilogue,
        extras=residual_extras(x2, 2), outs=(BF16, F32, BF16),
    )
    a3, y3 = mlp_forward(1, h3, *w_mlp1)
    dz3, dy3, st3, loss_local = residual_ln_loss_bwd(
        x3, y3, gate1[3], lng_full[3], lnb_full[3], target, name="res_ln3_loss_bwd"
    )

    def scatter(parts, after, name):
        parts = [p.reshape(N_DEV, -1, p.shape[-1]) for p in parts]
        lands = [_own_slot(me, lax.dynamic_index_in_dim(p, me, 0, keepdims=False)) for p in parts]
        return Scatter(parts, lands, after, name=name)

    xs_in, ys = [x0, x1, x2, x3], [y0, y1, y2, y3]

    def residual_bwd_extras(dz_later, t, zero):
        return [
            (dz_later, "full"), (scale[t + 1] + zero, "row"), (xs_in[t], "full"), (ys[t], "full"),
            (gate1[t], "row"), (lng_full[t], "row"), (lnb_full[t], "row"),
        ]

    def mlp_backward(i, h, a, dy, dz_later, t):
        (da,) = mm_nt(
            dy,
            w_dn3[i],
            name=f"mlp{i}_da",
            tm=2048,
            tko=1024,
            ps=1,
            tc=D_MODEL,
            epilogue=lambda acc, act: (acc * (2.0 * jnp.maximum(act.astype(F32), 0.0)),),
            extras=[(a, "full")],
            outs=(BF16,),
        )
        dw_dn = mm_tn(
            a, dy, name=f"mlp{i}_dw_down", p=1, tk=1024, ps=1, tn=D_MODEL, tmc=2048, prologue=relu2, out_dtype=BF16
        )
        dw_up = mm_tn(h, da, name=f"mlp{i}_dw_up", p=N_DEV, tk=1024, ps=2, tn=512, tmc=2048, out_dtype=BF16)
        rs = scatter([dw_up, dw_dn], da, f"scatter_mlp{i}")
        dz, dy_before, st = mm_nt(
            da, w_up3[i], name=f"mlp{i}_dh", tm=512, tko=D_MODEL, ps=N_DEV, tc=512, epilogue=residual_bwd_epilogue,
            extras=residual_bwd_extras(dz_later, t, rs.zero()), outs=(F32, BF16), stats=True,
        )
        return rs, dz, dy_before, st

    rs_mlp1, dz2, dy2, st2 = mlp_backward(1, h3, a3, dy3, dz3, 2)
    do_delta = mm_nt(
        dy2, w_bout3, name="b_do", tm=1024, tko=D_MODEL, ps=1, tc=D_MODEL, extras=[(o_f, "full")],
        epilogue=lambda acc, o: (acc,) * N_PAT + (_reduce_groups(acc * o),) * N_PAT,
        outs=(BF16,) * N_PAT + (F32,) * N_PAT, out_widths=[D_MODEL] * N_PAT + [LANES] * N_PAT, out_streams=dil + dil,
    )
    do_pat = [x.reshape(SEQ, D_MODEL) for x in do_delta[:N_PAT]]
    delta_pat = [x.reshape(SEQ, LANES) for x in do_delta[N_PAT:]]
    dh2, dw_pat = None, []
    for g in range(N_PAT):
        dqkv_g = attn_bwd(qkv_p[g], do_pat[g], lse_pat[g], delta_pat[g], g, name=f"attn_bwd{g}")
        dw_pat.append(
            mm_tn(
                h2, dqkv_g, name=f"b_dw_qkv{g}", p=1, tk=1024, ps=1, tn=D_MODEL, tmc=2048, out_dtype=BF16,
                g_streams=dil[g],
            )[0]
        )
        if g == N_PAT - 1:
            break
        (dh2,) = mm_nt(
            dqkv_g, w_qkv3, name=f"b_dh{g}", tm=2048, tko=512, ps=1, tc=pat_tiles * D_MODEL, outs=(F32,),
            b_tile0=g, g_streams=dil[g],
            epilogue=ident if g == 0 else (lambda acc, prev, d=dil[g]: (_from_streams(acc, d) + prev,)),
            extras=[] if g == 0 else [(dh2, "full")],
        )
    dw_bout = mm_tn(o_b, dy2, name="b_dw_out", p=1, tk=1024, ps=1, tn=D_MODEL, tmc=2048, out_dtype=BF16)
    dw_qkv = columns_to_shards(dw_pat, name="dw_qkv_shards")
    rs_b = scatter([dw_qkv, dw_bout], dqkv_g, "scatter_b")
    dz1, dy1, st1 = mm_nt(
        dqkv_g, w_qkv3, name=f"b_dh{N_PAT - 1}", tm=512, tko=D_MODEL, ps=1, tc=pat_tiles * D_MODEL, outs=(F32, BF16),
        b_tile0=N_PAT - 1, g_streams=dil[-1], stats=True,
        epilogue=lambda acc, prev, *rest: residual_bwd_epilogue(_from_streams(acc, dil[-1]) + prev, *rest),
        extras=[(dh2, "full")] + residual_bwd_extras(dz2, 1, rs_b.zero()),
    )
    rs_mlp0, dz0, dy0, st0 = mlp_backward(0, h1, a1, dy1, dz1, 0)
    (dp_gate,) = mm_nt(dy0, w_aout3, name="a_dp", tm=2048, tko=1024, ps=1, tc=D_MODEL, epilogue=ident, outs=(F32,))
    dw_aout = mm_tn(p_gate, dy0, name="a_dw_out", p=1, tk=1024, ps=1, tn=D_MODEL, tmc=2048, out_dtype=BF16)
    rs_aout = scatter([dw_aout], dp_gate, "scatter_a_out")
    da0, d_ws, d_bs, gate_rows = gate_bwd(a_pre, dp_gate, vn_g + rs_aout.zero(), vn_b, w_s, bias_full, name="gate_bwd")
    dw_in = mm_tn(h0, da0, name="a_dw_in", p=N_DEV, tk=1024, ps=4, tn=256, tmc=2048, out_dtype=BF16)
    rs_in = scatter([dw_in], d_ws, "scatter_a_in")
    grad_x, stf = mm_nt(
        da0, w_in3, name="a_dh", tm=1024, tko=D_MODEL, ps=N_DEV, tc=256, epilogue=input_bwd_epilogue,
        extras=[(dz0, "full"), (scale[0] + rs_in.zero(), "row"), (x0, "full")], outs=(F32,), stats=True,
    )

    results = {}

    def update(wname, gparts, w, m, v):
        shape = w.shape
        layers = len(gparts) if isinstance(gparts, list) else 1
        w3 = w.reshape(layers, -1, shape[-1])
        parts = [g.reshape(g.shape[0], *w3.shape[1:]) for g in (gparts if layers > 1 else [gparts])]
        outs = adamw(parts, w3, m.reshape(w3.shape), v.reshape(w3.shape), name=f"adamw_{wname}")
        results[wname] = [o.reshape(shape) for o in outs]

    g_up1, g_dn1 = rs_mlp1.wait([0, 1], grad_x, name="scatter_wait_mlp1")
    g_qkv, g_bout = rs_b.wait([0, 1], grad_x, name="scatter_wait_b")
    update("b_w_qkv", g_qkv, b_w_qkv, m_b_w_qkv, v_b_w_qkv)
    update("b_w_out", g_bout, b_w_out, m_b_w_out, v_b_w_out)
    g_up0, g_dn0 = rs_mlp0.wait([0, 1], grad_x, name="scatter_wait_mlp0")
    update("mlp_w_up", [g_up0, g_up1], mlp_w_up, m_mlp_w_up, v_mlp_w_up)
    update("mlp_w_down", [g_dn0, g_dn1], mlp_w_down, m_mlp_w_down, v_mlp_w_down)

    stats_after = [stf, st0, st1, st2]
    stats_own = [st0, st1, st2, st3]
    dm = jnp.stack(
        [
            jnp.concatenate(
                [stats_after[t][ST_DSHIFT], stats_after[t][ST_DSCALE], stats_own[t][ST_DGATE]], axis=0
            )
            for t in range(N_SUB)
        ]
    )
    ncol = 3 * D_MODEL // N_DEV
    dmx = jnp.pad(
        dm.reshape(N_SUB, N_DEV, ncol).transpose(1, 0, 2)[:, :, None, :], ((0, 0), (0, 0), (0, SUBLANES - 1), (0, 0))
    )
    small = [
        gate_rows[0],
        gate_rows[1],
        d_ws.reshape(-1),
        d_bs[:, :A_GROUPS].T.reshape(-1),
        *[stats_own[t][ST_DG] for t in range(N_SUB)],
        *[stats_own[t][ST_DB] for t in range(N_SUB)],
        jnp.pad(loss_local.reshape(1), (0, LANES - 1)),
    ]
    n_small = sum(s.size for s in small)
    part_rows = -(-n_small // (N_DEV * LANES * SUBLANES)) * SUBLANES
    flat = jnp.concatenate(small + [jnp.zeros((N_DEV * part_rows * LANES - n_small,), F32)])
    dmrecv, reduced = small_exchange(dmx, flat.reshape(N_DEV, part_rows, LANES), results["mlp_w_down"][3])
    reduced = reduced.reshape(-1)
    sizes = [2 * D_MODEL, D_MODEL, D_MODEL, A_GROUPS * CHUNK * CHUNK, A_GROUPS * CHUNK, N_SUB * D_MODEL, N_SUB * D_MODEL]
    offs = [sum(sizes[:i]) for i in range(len(sizes) + 1)]
    g_b_in, g_vn_g, g_vn_b, g_ws, g_bs, g_lng, g_lnb = [reduced[offs[i] : offs[i + 1]] for i in range(len(sizes))]
    loss = reduced[offs[-1]]

    ada_outs = ada_grad_adamw(cg, dmrecv, ada_w4, m_ada_w.reshape(ada_w4.shape), v_ada_w.reshape(ada_w4.shape), name="ada_grad_adamw")
    results["ada_w"] = [o.reshape(ada_w.shape) for o in ada_outs[:4]]
    ln_cols = D_MODEL // N_DEV
    my_ln = lambda gfull: lax.dynamic_slice_in_dim(gfull.reshape(N_SUB, N_DEV, ln_cols), me, 1, axis=1)
    small_params = [
        ("ada_b", ada_outs[4][:, 0, :], ada_b, m_ada_b, v_ada_b),
        ("ln_g", my_ln(g_lng), ln_g, m_ln_g, v_ln_g),
        ("ln_b", my_ln(g_lnb), ln_b, m_ln_b, v_ln_b),
        ("a_b_in", g_b_in, a_b_in, m_a_b_in, v_a_b_in),
        ("a_vn_g", g_vn_g, a_vn_g, m_a_vn_g, v_a_vn_g),
        ("a_vn_b", g_vn_b, a_vn_b, m_a_vn_b, v_a_vn_b),
        ("a_b_s", g_bs, a_b_s, m_a_b_s, v_a_b_s),
    ]
    small_outs = adamw_small(
        [[a.reshape(-1, w.shape[-1]) for a in (g, w, m, v)] for _, g, w, m, v in small_params], name="adamw_small"
    )
    for (wname, _, w, _, _), outs in zip(small_params, small_outs, strict=True):
        results[wname] = [o.reshape(w.shape) for o in outs]
    update("a_w_s", g_ws[None], a_w_s, m_a_w_s, v_a_w_s)
    (g_aout,) = rs_aout.wait([0], grad_x, name="scatter_wait_a_out")
    (g_in,) = rs_in.wait([0], grad_x, name="scatter_wait_a_in")
    update("a_w_in", g_in, a_w_in, m_a_w_in, v_a_w_in)
    update("a_w_out", g_aout, a_w_out, m_a_w_out, v_a_w_out)

    order = ["ada_w", "ada_b", "ln_g", "ln_b", "a_w_in", "a_b_in", "a_vn_g", "a_vn_b", "a_w_s", "a_b_s", "a_w_out", "b_w_qkv", "b_w_out", "mlp_w_up", "mlp_w_down"]
    return (loss, grad_x[None], *[results[n][0] for n in order], *[results[n][1] for n in order],
            *[results[n][2] for n in order], *[results[n][3] for n in order])
```

```python
import math

import jax
import jax.numpy as jnp
from jax import lax
from jax.experimental import pallas as pl
from jax.experimental.pallas import tpu as pltpu

F32 = jnp.float32
BF16 = jnp.bfloat16
MESH = pl.DeviceIdType.MESH
ANY = pl.BlockSpec(memory_space=pl.ANY)
VMEM = pl.BlockSpec(memory_space=pltpu.VMEM)

N_DEV = 8
D_MODEL = 1024
SEQ = 4096
DEPTH = 2
CHUNK = 128
A_GROUPS = 16
A_GROUP_DIM = D_MODEL // A_GROUPS
B_HEADS = 16
B_HEAD_DIM = 64
B_PATTERNS = ((128, 1), (512, 4), (2048, 16))
N_PAT = len(B_PATTERNS)
SPAN = 128
D_FF = 4 * D_MODEL
D_QKV = N_PAT * 3 * D_MODEL
ALPHA = (2 * DEPTH) ** 0.25
LN_EPS = 1e-5
NEG = -1e30
ADAM_LR = 0.001
ADAM_B1 = 0.9
ADAM_B2 = 0.999
ADAM_EPS = 1e-08
ADAM_WD = 0.01
ADAM_STEP = 10
GELU_C = math.sqrt(2.0 / math.pi)
GELU_A = 0.044715

VMEM_LIMIT_BYTES = 56 * 1024 * 1024
LANES = 128
SUBLANES = 8
ROW_TILE = 512
N_SUB = 2 * DEPTH


def _params(sem):
    return pltpu.CompilerParams(dimension_semantics=sem, vmem_limit_bytes=VMEM_LIMIT_BYTES)


def _lane(shape):
    return lax.broadcasted_iota(jnp.int32, shape, len(shape) - 1)


def _split_bf16(x):
    hi = x.astype(BF16)
    lo = (x - hi.astype(F32)).astype(BF16)
    return hi, lo


def _group_expand_matrix(groups_padded, width):
    per = width // A_GROUPS
    r = lax.broadcasted_iota(jnp.int32, (groups_padded, width), 0)
    c = lax.broadcasted_iota(jnp.int32, (groups_padded, width), 1)
    return (c // per == r).astype(BF16)


def _group_reduce_matrix(width, groups_padded):
    per = width // A_GROUPS
    r = lax.broadcasted_iota(jnp.int32, (width, groups_padded), 0)
    c = lax.broadcasted_iota(jnp.int32, (width, groups_padded), 1)
    return (r // per == c).astype(BF16)


def _expand_groups(w):
    e = _group_expand_matrix(LANES, D_MODEL)
    hi, lo = _split_bf16(w)
    return jnp.dot(hi, e, preferred_element_type=F32) + jnp.dot(lo, e, preferred_element_type=F32)


def _reduce_groups(x):
    e = _group_reduce_matrix(D_MODEL, LANES)
    hi, lo = _split_bf16(x)
    return jnp.dot(hi, e, preferred_element_type=F32) + jnp.dot(lo, e, preferred_element_type=F32)


def _to_streams(x, d):
    rows, w = x.shape
    return jnp.swapaxes(x.reshape(rows // d, d, w), 0, 1).reshape(rows, w)


def _from_streams(x, d):
    rows, w = x.shape
    return jnp.swapaxes(x.reshape(d, rows // d, w), 0, 1).reshape(rows, w)


def _column_tiles(p, n, ps, tn):
    assert (ps == 1 or tn == n) and p % ps == 0 and n % tn == 0
    q = n // tn
    return (p // ps) * q, q


def _extra_specs(extras, tm, width):
    specs = []
    for _, kind in extras:
        if kind == "row":
            specs.append(pl.BlockSpec((1, width), lambda i, j, c: (0, j)))
        else:
            specs.append(pl.BlockSpec((tm, width), lambda i, j, c: (i, j)))
    return specs


def mm_nn(a, b3, *, name, tm, ps, tn, tk, epilogue, extras=(), outs, prologue=None, b_tile0=0, b_tiles=None, out_streams=1):
    m, k = a.shape
    p, _, n = b3.shape
    nj, q = _column_tiles(p, n, ps, tn)
    nj = nj if b_tiles is None else b_tiles
    nk = k // tk
    width = ps * tn
    d = out_streams
    assert d == 1 or not extras

    def body(a_ref, b_ref, *rest):
        ex = rest[: len(extras)]
        out_refs = rest[len(extras) : len(extras) + len(outs)]
        kk = pl.program_id(2)
        av = a_ref[...] if prologue is None else prologue(a_ref[...])
        if d > 1:
            av = _to_streams(av, d)

        def finish(cs, acc):
            res = epilogue(acc, *[e[:, cs] for e in ex])
            for o_ref, r in zip(out_refs, res, strict=True):
                if d > 1:
                    o_ref[:, :, cs] = r.astype(o_ref.dtype).reshape(d, tm // d, tn)
                else:
                    o_ref[:, cs] = r.astype(o_ref.dtype)

        for s in range(ps):
            cs = slice(s * tn, (s + 1) * tn)
            part = jnp.dot(av, b_ref[s], preferred_element_type=F32)
            if nk == 1:
                finish(cs, part)
                continue
            acc_ref = rest[-1]

            @pl.when(kk == 0)
            def _(part=part, cs=cs):
                acc_ref[:, cs] = part

            @pl.when(kk > 0)
            def _(part=part, cs=cs):
                acc_ref[:, cs] += part

        if nk > 1:

            @pl.when(kk == nk - 1)
            def _():
                for s in range(ps):
                    cs = slice(s * tn, (s + 1) * tn)
                    finish(cs, rest[-1][:, cs])

    if d > 1:
        out_spec = pl.BlockSpec((d, tm // d, width), lambda i, j, kk: (0, i, j))
        out_shape = (d, m // d, nj * width)
    else:
        out_spec = pl.BlockSpec((tm, width), lambda i, j, kk: (i, j))
        out_shape = (m, nj * width)
    res = pl.pallas_call(
        body,
        name=name,
        grid=(m // tm, nj, nk),
        in_specs=[
            pl.BlockSpec((tm, tk), lambda i, j, kk: (i, kk)),
            pl.BlockSpec((ps, tk, tn), lambda i, j, kk: ((j + b_tile0) // q, kk, (j + b_tile0) % q)),
            *_extra_specs(extras, tm, width),
        ],
        out_specs=[out_spec for _ in outs],
        out_shape=[jax.ShapeDtypeStruct(out_shape, dt) for dt in outs],
        scratch_shapes=[pltpu.VMEM((tm, width), F32)] if nk > 1 else [],
        compiler_params=_params(("parallel", "parallel", "arbitrary")),
    )(a, b3, *[arr for arr, _ in extras])
    return [r.reshape(m, nj * width) for r in res]


def mm_nt(
    g, b3, *, name, tm, tko, ps, tc, epilogue, extras=(), outs, out_widths=None, out_streams=None, b_tile0=0,
    g_streams=1, stats=False,
):
    m, width = g.shape
    p, k, n = b3.shape
    _, q = _column_tiles(p, n, ps, tc)
    nc = width // (ps * tc)
    ds = g_streams
    assert not stats or tko == k
    widths = [tko] * len(outs) if out_widths is None else out_widths
    streams_out = [1] * len(outs) if out_streams is None else out_streams

    def out_spec(w, d):
        if d == 1:
            return pl.BlockSpec((tm, w), lambda i, j, c: (i, j))
        return pl.BlockSpec((d, tm // d, w), lambda i, j, c: (0, i, j))

    def out_struct(dt, w, d):
        cols = w * (k // tko)
        return jax.ShapeDtypeStruct((m, cols) if d == 1 else (d, m // d, cols), dt)

    def body(g_ref, b_ref, *rest):
        ex = rest[: len(extras)]
        out_refs = rest[len(extras) : len(extras) + len(outs)]
        c = pl.program_id(2)
        gv = g_ref[...].reshape(tm, ps * tc) if ds > 1 else g_ref[...]
        part = None
        for s in range(ps):
            d = lax.dot_general(gv[:, s * tc : (s + 1) * tc], b_ref[s], _NT, preferred_element_type=F32)
            part = d if part is None else part + d

        def finish(acc):
            res = epilogue(acc, *[e[...] for e in ex])
            if stats:
                *res, sums = res
                st_ref = rest[len(extras) + len(outs)]

                @pl.when(pl.program_id(0) == 0)
                def _():
                    st_ref[...] = jnp.zeros_like(st_ref)

                for row, val in enumerate(sums):
                    st_ref[row : row + 1, :] += val
            for o_ref, r, d_out in zip(out_refs, res, streams_out, strict=True):
                r = r.astype(o_ref.dtype)
                o_ref[...] = r if d_out == 1 else _to_streams(r, d_out).reshape(o_ref.shape)

        if nc == 1:
            finish(part)
            return
        acc_ref = rest[-1]

        @pl.when(c == 0)
        def _():
            acc_ref[...] = part

        @pl.when(c > 0)
        def _():
            acc_ref[...] += part

        @pl.when(c == nc - 1)
        def _():
            finish(acc_ref[...])

    return pl.pallas_call(
        body,
        name=name,
        grid=(m // tm, k // tko, nc),
        in_specs=[
            pl.BlockSpec((ds, tm // ds, ps * tc), lambda i, j, c: (0, i, c))
            if ds > 1
            else pl.BlockSpec((tm, ps * tc), lambda i, j, c: (i, c)),
            pl.BlockSpec((ps, tko, tc), lambda i, j, c: ((c + b_tile0) // q, j, (c + b_tile0) % q)),
            *_extra_specs(extras, tm, tko),
        ],
        out_specs=[out_spec(w, d) for w, d in zip(widths, streams_out, strict=True)]
        + [pl.BlockSpec((SUBLANES, tko), lambda i, j, c: (0, 0))] * stats,
        out_shape=[out_struct(dt, w, d) for dt, w, d in zip(outs, widths, streams_out, strict=True)]
        + [jax.ShapeDtypeStruct((SUBLANES, k), F32)] * stats,
        scratch_shapes=[pltpu.VMEM((tm, tko), F32)] if nc > 1 else [],
        compiler_params=_params(("arbitrary" if stats else "parallel", "parallel", "arbitrary")),
    )(g.reshape(ds, m // ds, width) if ds > 1 else g, b3, *[arr for arr, _ in extras])


def mm_tn(a, g, *, name, p, tk, ps, tn, tmc, out_dtype, prologue=None, g_streams=1):
    m, k = a.shape
    width = g.shape[1]
    n = width // p
    nj, q = _column_tiles(p, n, ps, tn)
    nc = m // tmc
    ds = g_streams

    def body(a_ref, g_ref, o_ref, acc_ref):
        c = pl.program_id(2)
        av = a_ref[...] if prologue is None else prologue(a_ref[...])
        gv = g_ref[...]
        if ds > 1:
            av, gv = _to_streams(av, ds), gv.reshape(tmc, ps * tn)
        part = lax.dot_general(av, gv, (((0,), (0,)), ((), ())), preferred_element_type=F32)

        @pl.when(c == 0)
        def _():
            acc_ref[...] = part

        @pl.when(c > 0)
        def _():
            acc_ref[...] += part

        @pl.when(c == nc - 1)
        def _():
            for s in range(ps):
                o_ref[s] = acc_ref[:, s * tn : (s + 1) * tn].astype(o_ref.dtype)

    return pl.pallas_call(
        body,
        name=name,
        grid=(k // tk, nj, nc),
        in_specs=[
            pl.BlockSpec((tmc, tk), lambda i, j, c: (c, i)),
            pl.BlockSpec((ds, tmc // ds, ps * tn), lambda i, j, c: (0, c, j))
            if ds > 1
            else pl.BlockSpec((tmc, ps * tn), lambda i, j, c: (c, j)),
        ],
        out_specs=pl.BlockSpec((ps, tk, tn), lambda i, j, c: (j // q, i, j % q)),
        out_shape=jax.ShapeDtypeStruct((p, k, n), out_dtype),
        scratch_shapes=[pltpu.VMEM((tk, ps * tn), F32)],
        compiler_params=_params(("parallel", "parallel", "arbitrary")),
    )(a, g.reshape(ds, m // ds, width) if ds > 1 else g)


def _rows(cols):
    return pl.BlockSpec((ROW_TILE, cols), lambda i: (i, 0))


def _vec(cols, rows=1):
    return pl.BlockSpec((rows, cols), lambda i: (0, 0))


def _layer_norm_hat(z):
    mu = jnp.mean(z, axis=-1, keepdims=True)
    zc = z - mu
    var = jnp.mean(zc * zc, axis=-1, keepdims=True)
    rstd = lax.rsqrt(var + LN_EPS)
    return zc * rstd, rstd


def modulate(x, scale, shift, *, name):
    s, d = x.shape

    def body(x_ref, sc_ref, sh_ref, h_ref):
        h_ref[...] = (x_ref[...] * (1.0 + sc_ref[...]) + sh_ref[...]).astype(BF16)

    return pl.pallas_call(
        body,
        name=name,
        grid=(s // ROW_TILE,),
        in_specs=[_rows(d), _vec(d), _vec(d)],
        out_specs=_rows(d),
        out_shape=jax.ShapeDtypeStruct((s, d), BF16),
        compiler_params=_params(("parallel",)),
    )(x, scale, shift)


def residual_ln_epilogue(acc, x, gate1, g, b, nscale, nshift):
    y = acc.astype(BF16)
    xhat, _ = _layer_norm_hat(ALPHA * x + gate1 * y.astype(F32))
    xn = xhat * g + b
    return y, xn, xn * (1.0 + nscale) + nshift


ST_DSCALE, ST_DSHIFT, ST_DG, ST_DB, ST_DGATE = 0, 1, 2, 3, 4


def _layer_norm_bwd(g_out, xhat, rstd, g):
    dxh = g_out * g
    m1 = jnp.mean(dxh, axis=-1, keepdims=True)
    m2 = jnp.mean(dxh * xhat, axis=-1, keepdims=True)
    return rstd * (dxh - m1 - xhat * m2)


def _column_sums(vals):
    return [jnp.sum(v, axis=0, keepdims=True) for v in vals]


def residual_bwd_epilogue(dh, dzl, scl, x, y, gate1, g, b):
    yf = y.astype(F32)
    xhat, rstd = _layer_norm_hat(ALPHA * x + gate1 * yf)
    g_out = ALPHA * dzl + dh * (1.0 + scl)
    dz = _layer_norm_bwd(g_out, xhat, rstd, g)
    return dz, dz * gate1, _column_sums([dh * (xhat * g + b), dh, g_out * xhat, g_out, dz * yf])


def input_bwd_epilogue(dh, dzl, scl, x):
    return ALPHA * dzl + dh * (1.0 + scl), _column_sums([dh * x, dh])


def residual_ln_loss_bwd(x, y, gate1, g, b, target, *, name):
    s, d = x.shape

    def body(x_ref, y_ref, gt_ref, g_ref, b_ref, t_ref, dz_ref, dy_ref, st_ref, loss_ref):
        y = y_ref[...].astype(F32)
        gate1 = gt_ref[...]
        xhat, rstd = _layer_norm_hat(ALPHA * x_ref[...] + gate1 * y)
        err = xhat * g_ref[...] + b_ref[...] - t_ref[...]
        g_out = err * (1.0 / d)
        dz = _layer_norm_bwd(g_out, xhat, rstd, g_ref[...])
        dz_ref[...] = dz
        dy_ref[...] = (dz * gate1).astype(BF16)
        part = jnp.sum(jnp.sum(err * err, axis=1, keepdims=True), axis=0, keepdims=True) * (0.5 / d)

        @pl.when(pl.program_id(0) == 0)
        def _():
            st_ref[...] = jnp.zeros_like(st_ref)
            loss_ref[...] = jnp.zeros_like(loss_ref)

        loss_ref[...] += part
        for row, val in zip((ST_DG, ST_DB, ST_DGATE), _column_sums([g_out * xhat, g_out, dz * y]), strict=True):
            st_ref[row : row + 1, :] += val

    return pl.pallas_call(
        body,
        name=name,
        grid=(s // ROW_TILE,),
        in_specs=[_rows(d), _rows(d), _vec(d), _vec(d), _vec(d), _rows(d)],
        out_specs=[_rows(d), _rows(d), _vec(d, SUBLANES), pl.BlockSpec((1, 1), lambda i: (0, 0))],
        out_shape=[
            jax.ShapeDtypeStruct((s, d), F32),
            jax.ShapeDtypeStruct((s, d), BF16),
            jax.ShapeDtypeStruct((SUBLANES, d), F32),
            jax.ShapeDtypeStruct((1, 1), F32),
        ],
        compiler_params=_params(("arbitrary",)),
    )(x, y, gate1, g, b, target)


def residual_ln_bwd(*, name, later=None, dxo=None, x_out=None, this=None):
    lead = later[0] if later is not None else dxo
    s, d = lead.shape
    has_later, has_ln = later is not None, this is not None

    def body(*refs):
        refs = list(refs)
        if has_later:
            dh_ref, dzl_ref, scl_ref = refs[:3]
            refs = refs[3:]
        else:
            dxo_ref = refs.pop(0)
        if has_ln:
            x_ref, y_ref, gt_ref, g_ref, b_ref = refs[:5]
            refs = refs[5:]
            dz_ref, dy_ref, st_ref = refs
        else:
            xo_ref, dx_ref, st_ref = refs

        @pl.when(pl.program_id(0) == 0)
        def _():
            st_ref[...] = jnp.zeros_like(st_ref)

        def acc(row, val):
            st_ref[row : row + 1, :] += jnp.sum(val, axis=0, keepdims=True)

        if has_ln:
            y = y_ref[...].astype(F32)
            gate1 = gt_ref[...]
            xhat, rstd = _layer_norm_hat(ALPHA * x_ref[...] + gate1 * y)
            x_out_v = xhat * g_ref[...] + b_ref[...]
        else:
            x_out_v = xo_ref[...]
        if has_later:
            dh = dh_ref[...]
            g_out = ALPHA * dzl_ref[...] + dh * (1.0 + scl_ref[...])
            acc(ST_DSCALE, dh * x_out_v)
            acc(ST_DSHIFT, dh)
        else:
            g_out = dxo_ref[...]
        if not has_ln:
            dx_ref[...] = g_out
            return
        acc(ST_DG, g_out * xhat)
        acc(ST_DB, g_out)
        dxh = g_out * g_ref[...]
        m1 = jnp.mean(dxh, axis=-1, keepdims=True)
        m2 = jnp.mean(dxh * xhat, axis=-1, keepdims=True)
        dz = rstd * (dxh - m1 - xhat * m2)
        acc(ST_DGATE, dz * y)
        dz_ref[...] = dz
        dy_ref[...] = (dz * gate1).astype(BF16)

    ins, specs = [], []
    if has_later:
        ins += list(later)
        specs += [_rows(d), _rows(d), _vec(d)]
    else:
        ins += [dxo]
        specs += [_rows(d)]
    if not has_ln:
        ins += [x_out]
        specs += [_rows(d)]
    if has_ln:
        ins += list(this)
        specs += [_rows(d), _rows(d), _vec(d), _vec(d), _vec(d)]
        out_specs = [_rows(d), _rows(d), _vec(d, SUBLANES)]
        out_shape = [
            jax.ShapeDtypeStruct((s, d), F32),
            jax.ShapeDtypeStruct((s, d), BF16),
            jax.ShapeDtypeStruct((SUBLANES, d), F32),
        ]
    else:
        out_specs = [_rows(d), _vec(d, SUBLANES)]
        out_shape = [jax.ShapeDtypeStruct((s, d), F32), jax.ShapeDtypeStruct((SUBLANES, d), F32)]
    return pl.pallas_call(
        body,
        name=name,
        grid=(s // ROW_TILE,),
        in_specs=specs,
        out_specs=out_specs,
        out_shape=out_shape,
        compiler_params=_params(("arbitrary",)),
    )(*ins)


GATE_CHUNKS = 4


def _gelu(x, with_grad=False):
    x2 = x * x
    t = jnp.tanh(GELU_C * (x + GELU_A * x2 * x))
    half = 0.5 * (1.0 + t)
    y = x * half
    if not with_grad:
        return y
    return y, half + 0.5 * x * (1.0 - t * t) * (GELU_C * (1.0 + 3.0 * GELU_A * x2))


def _causal_weights(w_ref, transpose):
    t = lax.broadcasted_iota(jnp.int32, (CHUNK, CHUNK), 0)
    s = lax.broadcasted_iota(jnp.int32, (CHUNK, CHUNK), 1)
    out = []
    for g in range(A_GROUPS):
        w = jnp.where(t >= s, w_ref[g], 0.0)
        out.append((w.T if transpose else w).astype(BF16))
    return out


def _spatial(ws, vn, lo_mask):
    rows = vn.shape[0]
    out_rows = []
    for r in range(rows // CHUNK):
        cols = []
        for j in range(A_GROUPS // 2):
            blk = vn[r * CHUNK : (r + 1) * CHUNK, j * LANES : (j + 1) * LANES]
            za = jnp.dot(ws[2 * j], blk, preferred_element_type=F32)
            zb = jnp.dot(ws[2 * j + 1], blk, preferred_element_type=F32)
            cols.append(jnp.where(lo_mask, za, zb))
        out_rows.append(jnp.concatenate(cols, axis=1))
    return jnp.concatenate(out_rows, axis=0)


def _gate_forward(a, vg, vb, ws, bias, lo_mask, with_grad=False):
    u = _gelu(a[:, :D_MODEL], with_grad)
    v = _gelu(a[:, D_MODEL:], with_grad)
    gu, gv = None, None
    if with_grad:
        (u, gu), (v, gv) = u, v
    vhat, rstd = _layer_norm_hat(v)
    vn = (vhat * vg + vb).astype(BF16)
    z = _spatial(ws, vn, lo_mask) + jnp.concatenate([bias] * (a.shape[0] // CHUNK), axis=0)
    return u, vhat, rstd, vn, z, gu, gv


def gate_fwd(a_pre, vn_g, vn_b, w_s, bias_full, *, name):
    s = a_pre.shape[0]
    tr = GATE_CHUNKS * CHUNK

    def body(a_ref, vg_ref, vb_ref, w_ref, bias_ref, p_ref):
        lo_mask = _lane((CHUNK, LANES)) < A_GROUP_DIM
        ws = _causal_weights(w_ref, transpose=False)
        u, _, _, _, z, _, _ = _gate_forward(a_ref[...].astype(F32), vg_ref[...], vb_ref[...], ws, bias_ref[...], lo_mask)
        p_ref[...] = (u * z).astype(BF16)

    return pl.pallas_call(
        body,
        name=name,
        grid=(s // tr,),
        in_specs=[
            pl.BlockSpec((tr, 2 * D_MODEL), lambda i: (i, 0)),
            _vec(D_MODEL),
            _vec(D_MODEL),
            pl.BlockSpec((A_GROUPS, CHUNK, CHUNK), lambda i: (0, 0, 0)),
            _vec(D_MODEL, CHUNK),
        ],
        out_specs=pl.BlockSpec((tr, D_MODEL), lambda i: (i, 0)),
        out_shape=jax.ShapeDtypeStruct((s, D_MODEL), BF16),
        compiler_params=_params(("parallel",)),
    )(a_pre, vn_g, vn_b, w_s, bias_full)


def gate_bwd(a_pre, dp, vn_g, vn_b, w_s, bias_full, *, name):
    s = a_pre.shape[0]
    tr = GATE_CHUNKS * CHUNK
    nsteps = s // tr

    def body(a_ref, dp_ref, vg_ref, vb_ref, w_ref, bias_ref, da_ref, dw_ref, dbs_ref, rows_ref, dbias_acc):
        step = pl.program_id(0)
        lo_mask = _lane((CHUNK, LANES)) < A_GROUP_DIM

        @pl.when(step == 0)
        def _():
            dw_ref[...] = jnp.zeros_like(dw_ref)
            rows_ref[...] = jnp.zeros_like(rows_ref)
            dbias_acc[...] = jnp.zeros_like(dbias_acc)

        a = a_ref[...].astype(F32)
        vg = vg_ref[...]
        ws = _causal_weights(w_ref, transpose=False)
        wts = _causal_weights(w_ref, transpose=True)
        u, vhat, rstd, vn, z, gelu_du, gelu_dv = _gate_forward(a, vg, vb_ref[...], ws, bias_ref[...], lo_mask, True)
        dp = dp_ref[...]
        du = dp * z
        dzz = dp * u
        dzz_b = dzz.astype(BF16)
        dvn = _spatial(wts, dzz_b, lo_mask)
        dbias = None
        for r in range(GATE_CHUNKS):
            rs = slice(r * CHUNK, (r + 1) * CHUNK)
            dbias = dzz[rs] if dbias is None else dbias + dzz[rs]
            for j in range(A_GROUPS // 2):
                cs = slice(j * LANES, (j + 1) * LANES)
                dblk = dzz[rs, cs]
                vblk = vn[rs, cs]
                for half in range(2):
                    keep = lo_mask if half == 0 else jnp.logical_not(lo_mask)
                    dm = jnp.where(keep, dblk, 0.0).astype(BF16)
                    dw_ref[2 * j + half] += lax.dot_general(
                        dm, vblk, (((1,), (1,)), ((), ())), preferred_element_type=F32
                    )
        dbias_acc[...] += dbias
        rows_ref[1:2, :D_MODEL] += jnp.sum(dvn * vhat, axis=0, keepdims=True)
        rows_ref[1:2, D_MODEL:] += jnp.sum(dvn, axis=0, keepdims=True)
        dvh = dvn * vg
        m1 = jnp.mean(dvh, axis=-1, keepdims=True)
        m2 = jnp.mean(dvh * vhat, axis=-1, keepdims=True)
        dv = rstd * (dvh - m1 - vhat * m2)
        da_u = du * gelu_du
        da_v = dv * gelu_dv
        da_ref[:, :D_MODEL] = da_u.astype(BF16)
        da_ref[:, D_MODEL:] = da_v.astype(BF16)
        rows_ref[0:1, :D_MODEL] += jnp.sum(da_u, axis=0, keepdims=True)
        rows_ref[0:1, D_MODEL:] += jnp.sum(da_v, axis=0, keepdims=True)

        @pl.when(step == nsteps - 1)
        def _():
            t = lax.broadcasted_iota(jnp.int32, (CHUNK, CHUNK), 0)
            sx = lax.broadcasted_iota(jnp.int32, (CHUNK, CHUNK), 1)
            for g in range(A_GROUPS):
                dw_ref[g] = jnp.where(t >= sx, dw_ref[g], 0.0)
            dbs_ref[...] = _reduce_groups(dbias_acc[...])

    return pl.pallas_call(
        body,
        name=name,
        grid=(nsteps,),
        in_specs=[
            pl.BlockSpec((tr, 2 * D_MODEL), lambda i: (i, 0)),
            pl.BlockSpec((tr, D_MODEL), lambda i: (i, 0)),
            _vec(D_MODEL),
            _vec(D_MODEL),
            pl.BlockSpec((A_GROUPS, CHUNK, CHUNK), lambda i: (0, 0, 0)),
            _vec(D_MODEL, CHUNK),
        ],
        out_specs=[
            pl.BlockSpec((tr, 2 * D_MODEL), lambda i: (i, 0)),
            pl.BlockSpec((A_GROUPS, CHUNK, CHUNK), lambda i: (0, 0, 0)),
            _vec(LANES, CHUNK),
            _vec(2 * D_MODEL, SUBLANES),
        ],
        out_shape=[
            jax.ShapeDtypeStruct((s, 2 * D_MODEL), BF16),
            jax.ShapeDtypeStruct((A_GROUPS, CHUNK, CHUNK), F32),
            jax.ShapeDtypeStruct((CHUNK, LANES), F32),
            jax.ShapeDtypeStruct((SUBLANES, 2 * D_MODEL), F32),
        ],
        scratch_shapes=[pltpu.VMEM((CHUNK, D_MODEL), F32)],
        compiler_params=_params(("arbitrary",)),
    )(a_pre, dp, vn_g, vn_b, w_s, bias_full)


def alibi_tables(dilation):
    qi = jnp.arange(SPAN)[:, None]
    ki = jnp.arange(2 * SPAN)[None, :]
    diff = SPAN + qi - ki
    valid = (diff >= 0) & (diff <= SPAN)
    heads = jnp.arange(1, B_HEADS + 1, dtype=F32)
    slopes = jnp.exp2(-8.0 * heads / B_HEADS)
    bias = -slopes[:, None, None] * (dilation * diff).astype(F32)
    bias = jnp.where(valid[None], bias, NEG).reshape(B_HEADS // 2, 2 * SPAN, 2 * SPAN)
    return bias, bias.transpose(0, 2, 1)


def _pair_rows(x, halves):
    return jnp.concatenate([x * halves[0], x * halves[1]], axis=0)


def _pair_column(v, lane, j):
    pick = lambda h: jnp.sum(jnp.where(lane == h, v, 0.0), axis=1, keepdims=True)
    return jnp.concatenate([pick(2 * j), pick(2 * j + 1)], axis=0)


_NT = (((1,), (1,)), ((), ()))


def permute_rows(xs, dilation, *, inverse, name):
    s = xs[0].shape[0]
    tile = SPAN * dilation
    nat = [pl.BlockSpec((tile, x.shape[1]), lambda i: (i, 0)) for x in xs]
    streams = [pl.BlockSpec((dilation, SPAN, x.shape[1]), lambda i: (0, i, 0)) for x in xs]
    stream_shape = [jax.ShapeDtypeStruct((dilation, s // dilation, x.shape[1]), x.dtype) for x in xs]

    def body(*refs):
        for x_ref, o_ref in zip(refs[: len(xs)], refs[len(xs) :], strict=True):
            w = x_ref.shape[-1]
            if inverse:
                o_ref[...] = jnp.swapaxes(x_ref[...], 0, 1).reshape(tile, w)
            else:
                o_ref[...] = jnp.swapaxes(x_ref[...].reshape(SPAN, dilation, w), 0, 1)

    outs = pl.pallas_call(
        body,
        name=name,
        grid=(s // tile,),
        in_specs=streams if inverse else nat,
        out_specs=nat if inverse else streams,
        out_shape=[jax.ShapeDtypeStruct(x.shape, x.dtype) for x in xs] if inverse else stream_shape,
        compiler_params=_params(("parallel",)),
    )(*[x.reshape(dilation, s // dilation, x.shape[1]) if inverse else x for x in xs])
    return [o.reshape(x.shape) for o, x in zip(outs, xs, strict=True)]


def _qkv_specs(block_of):
    def spec(which, prev):
        def index(*grid):
            blk = block_of(*grid)
            return (jnp.maximum(blk - 1, 0) if prev else blk, which)

        return pl.BlockSpec((SPAN, D_MODEL), index)

    return [spec(0, False), spec(1, True), spec(1, False), spec(2, True), spec(2, False)]


def attn_fwd(qkv_p, pat, *, name, after=None):
    _, dilation = B_PATTERNS[pat]
    nb = SEQ // dilation // SPAN
    bias, _ = alibi_tables(dilation)
    order = [] if after is None else [after]

    def body(q_ref, kp_ref, kc_ref, vp_ref, vc_ref, bias_ref, *rest):
        o_ref, lse_ref = rest[len(order) :]
        n = pl.program_id(1)
        first_prev = jnp.logical_and(n == 0, _lane((2 * SPAN, 2 * SPAN)) < SPAN)
        lane = _lane((SPAN, LANES))
        lo_mask = lane < B_HEAD_DIM
        q = q_ref[...] * jnp.asarray(B_HEAD_DIM**-0.5, BF16)
        kk = jnp.concatenate([kp_ref[...], kc_ref[...]], axis=0)
        vv = jnp.concatenate([vp_ref[...], vc_ref[...]], axis=0)
        halves = (lo_mask.astype(BF16), jnp.logical_not(lo_mask).astype(BF16))
        stats = jnp.zeros((SPAN, LANES), F32)
        for j in range(B_HEADS // 2):
            cs = slice(j * LANES, (j + 1) * LANES)
            sc = lax.dot_general(_pair_rows(q[:, cs], halves), kk[:, cs], _NT, preferred_element_type=F32)
            sc = jnp.where(first_prev, NEG, sc + bias_ref[j])
            m = jnp.max(sc, axis=1, keepdims=True)
            p = jnp.exp(sc - m)
            l = jnp.sum(p, axis=1, keepdims=True)
            acc = jnp.dot(p.astype(BF16), vv[:, cs], preferred_element_type=F32) * (1.0 / l)
            lse_pair = m + jnp.log(l)
            o_ref[:, cs] = jnp.where(lo_mask, acc[:SPAN], acc[SPAN:]).astype(BF16)
            stats = jnp.where(lane == 2 * j, lse_pair[:SPAN], stats)
            stats = jnp.where(lane == 2 * j + 1, lse_pair[SPAN:], stats)
        lse_ref[...] = stats

    return pl.pallas_call(
        body,
        name=name,
        grid=(dilation, nb),
        in_specs=[
            *_qkv_specs(lambda r, n: r * nb + n),
            pl.BlockSpec((B_HEADS // 2, 2 * SPAN, 2 * SPAN), lambda r, n: (0, 0, 0)),
            *[ANY for _ in order],
        ],
        out_specs=[
            pl.BlockSpec((SPAN, D_MODEL), lambda r, n: (r * nb + n, 0)),
            pl.BlockSpec((SPAN, LANES), lambda r, n: (r * nb + n, 0)),
        ],
        out_shape=[jax.ShapeDtypeStruct((SEQ, D_MODEL), BF16), jax.ShapeDtypeStruct((SEQ, LANES), F32)],
        compiler_params=_params(("parallel", "arbitrary")),
    )(qkv_p, qkv_p, qkv_p, qkv_p, qkv_p, bias, *order)


def attn_combine(outs, lses, *, name):
    dils = [d for _, d in B_PATTERNS]

    def positions(ref, d):
        x = ref[...].astype(F32)
        return x if d == 1 else jnp.swapaxes(x, 0, 1).reshape(ROW_TILE, x.shape[-1])

    def body(o0, o1, o2, l0, l1, l2, ob_ref, of_ref, *lse_refs):
        ls = [positions(l, d) for l, d in zip((l0, l1, l2), dils, strict=True)]
        m = jnp.maximum(jnp.maximum(ls[0], ls[1]), ls[2])
        tot = jnp.log(jnp.exp(ls[0] - m) + jnp.exp(ls[1] - m) + jnp.exp(ls[2] - m)) + m
        o = None
        for o_ref, l, d in zip((o0, o1, o2), ls, dils, strict=True):
            term = _expand_groups(jnp.exp(l - tot)) * positions(o_ref, d)
            o = term if o is None else o + term
        ob_ref[...] = o.astype(BF16)
        of_ref[...] = o
        for lse_ref, d in zip(lse_refs, dils, strict=True):
            lse_ref[...] = tot if d == 1 else _to_streams(tot, d).reshape(lse_ref.shape)

    def stream_rows(cols, d):
        return _rows(cols) if d == 1 else pl.BlockSpec((d, ROW_TILE // d, cols), lambda i: (0, i, 0))

    def streams(x, d):
        return x if d == 1 else x.reshape(d, SEQ // d, x.shape[-1])

    outs = [streams(x, d) for x, d in zip(outs, dils, strict=True)]
    lses = [streams(x, d) for x, d in zip(lses, dils, strict=True)]
    return pl.pallas_call(
        body,
        name=name,
        grid=(SEQ // ROW_TILE,),
        in_specs=[stream_rows(D_MODEL, d) for d in dils] + [stream_rows(LANES, d) for d in dils],
        out_specs=[_rows(D_MODEL), _rows(D_MODEL)] + [stream_rows(LANES, d) for d in dils],
        out_shape=[jax.ShapeDtypeStruct((SEQ, D_MODEL), BF16), jax.ShapeDtypeStruct((SEQ, D_MODEL), F32)]
        + [jax.ShapeDtypeStruct(lse.shape, F32) for lse in lses],
        compiler_params=_params(("parallel",)),
    )(*outs, *lses)


def attn_delta(do, o, *, name):
    def body(do_ref, o_ref, dob_ref, dl_ref):
        do_v = do_ref[...]
        dob_ref[...] = do_v.astype(BF16)
        dl_ref[...] = _reduce_groups(do_v * o_ref[...])

    return pl.pallas_call(
        body,
        name=name,
        grid=(SEQ // ROW_TILE,),
        in_specs=[_rows(D_MODEL), _rows(D_MODEL)],
        out_specs=[_rows(D_MODEL), _rows(LANES)],
        out_shape=[jax.ShapeDtypeStruct((SEQ, D_MODEL), BF16), jax.ShapeDtypeStruct((SEQ, LANES), F32)],
        compiler_params=_params(("parallel",)),
    )(do, o)


def attn_bwd(qkv_p, do_p, lse_p, delta_p, pat, *, name):
    _, dilation = B_PATTERNS[pat]
    nb = SEQ // dilation // SPAN
    n_blocks = SEQ // SPAN
    bias, bias_t = alibi_tables(dilation)
    last = n_blocks - 1
    q_cols, k_cols, v_cols = (slice(i * D_MODEL, (i + 1) * D_MODEL) for i in range(3))

    def body(q_ref, kp_ref, kc_ref, vp_ref, vc_ref, do_ref, lse_ref, dl_ref, bias_ref, biast_ref, out_ref, cq_ref, ck_ref, cv_ref):
        g = pl.program_id(0)

        @pl.when(g == n_blocks)
        def _():
            out_ref[:, q_cols] = cq_ref[...].astype(BF16)
            out_ref[:, k_cols] = ck_ref[...].astype(BF16)
            out_ref[:, v_cols] = cv_ref[...].astype(BF16)

        @pl.when(g == 0)
        def _():
            cq_ref[...] = jnp.zeros_like(cq_ref)
            ck_ref[...] = jnp.zeros_like(ck_ref)
            cv_ref[...] = jnp.zeros_like(cv_ref)

        @pl.when(g < n_blocks)
        def _():
            lane = _lane((SPAN, LANES))
            lo_mask = lane < B_HEAD_DIM
            pair = (2 * SPAN, 2 * SPAN)
            first = lax.rem(g, nb) == 0
            prev_key_cols = jnp.logical_and(first, _lane(pair) < SPAN)
            prev_key_rows = jnp.logical_and(first, lax.broadcasted_iota(jnp.int32, pair, 0) < SPAN)
            q = q_ref[...] * jnp.asarray(B_HEAD_DIM**-0.5, BF16)
            kk = jnp.concatenate([kp_ref[...], kc_ref[...]], axis=0)
            vv = jnp.concatenate([vp_ref[...], vc_ref[...]], axis=0)
            do_v = do_ref[...]
            lse_v = lse_ref[...]
            dl_v = dl_ref[...]
            lse_t = lse_v.T
            dl_t = dl_v.T
            halves = (lo_mask.astype(BF16), jnp.logical_not(lo_mask).astype(BF16))
            for j in range(B_HEADS // 2):
                cs = slice(j * LANES, (j + 1) * LANES)
                kp, vp = kk[:, cs], vv[:, cs]
                q2 = _pair_rows(q[:, cs], halves)
                do2 = _pair_rows(do_v[:, cs], halves)
                lse_c, dl_c = _pair_column(lse_v, lane, j), _pair_column(dl_v, lane, j)
                lse_r = jnp.concatenate([lse_t[2 * j : 2 * j + 1], lse_t[2 * j + 1 : 2 * j + 2]], axis=1)
                dl_r = jnp.concatenate([dl_t[2 * j : 2 * j + 1], dl_t[2 * j + 1 : 2 * j + 2]], axis=1)
                sc = lax.dot_general(q2, kp, _NT, preferred_element_type=F32)
                p = jnp.exp(jnp.where(prev_key_cols, NEG, sc + bias_ref[j]) - lse_c)
                dp = lax.dot_general(do2, vp, _NT, preferred_element_type=F32)
                ds = (p * (dp - dl_c)).astype(BF16)
                dq2 = jnp.dot(ds, kp, preferred_element_type=F32)
                sc_t = lax.dot_general(kp, q2, _NT, preferred_element_type=F32)
                p_t = jnp.exp(jnp.where(prev_key_rows, NEG, sc_t + biast_ref[j]) - lse_r)
                dp_t = lax.dot_general(vp, do2, _NT, preferred_element_type=F32)
                ds_t = (p_t * (dp_t - dl_r)).astype(BF16)
                dk_pair = jnp.dot(ds_t, q2, preferred_element_type=F32)
                dv_pair = jnp.dot(p_t.astype(BF16), do2, preferred_element_type=F32)
                oq = slice(j * LANES, (j + 1) * LANES)
                ok = slice(D_MODEL + j * LANES, D_MODEL + (j + 1) * LANES)
                ov = slice(2 * D_MODEL + j * LANES, 2 * D_MODEL + (j + 1) * LANES)
                out_ref[:, oq] = cq_ref[:, cs].astype(BF16)
                out_ref[:, ok] = (ck_ref[:, cs] + dk_pair[:SPAN]).astype(BF16)
                out_ref[:, ov] = (cv_ref[:, cs] + dv_pair[:SPAN]).astype(BF16)
                cq_ref[:, cs] = jnp.where(lo_mask, dq2[:SPAN], dq2[SPAN:]) * (B_HEAD_DIM**-0.5)
                ck_ref[:, cs] = dk_pair[SPAN:]
                cv_ref[:, cs] = dv_pair[SPAN:]

    def block_of(g):
        return jnp.minimum(g, last)

    def row_spec(width):
        return pl.BlockSpec((SPAN, width), lambda g: (block_of(g), 0))

    return pl.pallas_call(
        body,
        name=name,
        grid=(n_blocks + 1,),
        in_specs=[
            *_qkv_specs(block_of),
            row_spec(D_MODEL),
            row_spec(LANES),
            row_spec(LANES),
            pl.BlockSpec((B_HEADS // 2, 2 * SPAN, 2 * SPAN), lambda g: (0, 0, 0)),
            pl.BlockSpec((B_HEADS // 2, 2 * SPAN, 2 * SPAN), lambda g: (0, 0, 0)),
        ],
        out_specs=pl.BlockSpec((SPAN, 3 * D_MODEL), lambda g: (jnp.maximum(g - 1, 0), 0)),
        out_shape=jax.ShapeDtypeStruct((SEQ, 3 * D_MODEL), BF16),
        scratch_shapes=[pltpu.VMEM((SPAN, D_MODEL), F32)] * 3,
        compiler_params=_params(("arbitrary",)),
    )(qkv_p, qkv_p, qkv_p, qkv_p, qkv_p, do_p, lse_p, delta_p, bias, bias_t)


def _position():
    x, y, c = lax.axis_index("x"), lax.axis_index("y"), lax.axis_index("c")
    return x, y, c, 4 * x + 2 * y + c


def _peer(k, x, y, c):
    px = 1 - x if k & 4 else x
    py = 1 - y if k & 2 else y
    pc = 1 - c if k & 1 else c
    return (px, py, pc), 4 * px + 2 * py + pc


def _remote(src, dst, send_sem, recv_sem, device):
    return pltpu.make_async_remote_copy(
        src_ref=src, dst_ref=dst, send_sem=send_sem, recv_sem=recv_sem, device_id=device, device_id_type=MESH
    )


def _silu_bf16(cf):
    return (cf * (1.0 / (1.0 + jnp.exp(-cf)))).astype(BF16)


def ada_exchange(c8, w4, b4, ln8):
    nt, _, ncol = w4.shape

    def body(c8_ref, w_ref, b_ref, ln_ref, cg_ref, lng_ref, mrecv_ref, mloc_ref, send_sems, recv_sems):
        x, y, c, me = _position()
        cg_ref[me] = c8_ref[...]
        lng_ref[me] = ln_ref[...]
        first = []
        for k in range(1, N_DEV):
            dev, _ = _peer(k, x, y, c)
            first.append(_remote(c8_ref, cg_ref.at[me], send_sems.at[0, k], recv_sems.at[0, k], dev))
            first.append(_remote(ln_ref, lng_ref.at[me], send_sems.at[1, k], recv_sems.at[1, k], dev))
        for cp in first:
            cp.start()
        for k in range(1, N_DEV):
            dev, pid = _peer(k, x, y, c)
            _remote(c8_ref, cg_ref.at[pid], send_sems.at[0, k], recv_sems.at[0, k], dev).wait_recv()
            _remote(ln_ref, lng_ref.at[pid], send_sems.at[1, k], recv_sems.at[1, k], dev).wait_recv()
        sc = _silu_bf16(cg_ref[...].reshape(N_DEV * SUBLANES, D_MODEL))
        for t in range(nt):
            mloc_ref[t] = jnp.dot(sc, w_ref[t].astype(BF16), preferred_element_type=F32) + b_ref[t : t + 1, :]

        def group(dev_id):
            return pl.ds(pl.multiple_of(dev_id * SUBLANES, SUBLANES), SUBLANES)

        mrecv_ref[me] = mloc_ref[:, group(me), :]
        second = []
        for k in range(1, N_DEV):
            dev, pid = _peer(k, x, y, c)
            second.append(
                _remote(mloc_ref.at[:, group(pid), :], mrecv_ref.at[me], send_sems.at[2, k], recv_sems.at[2, k], dev)
            )
        for cp in second:
            cp.start()
        for k in range(1, N_DEV):
            dev, pid = _peer(k, x, y, c)
            _remote(
                mloc_ref.at[:, group(pid), :], mrecv_ref.at[pid], send_sems.at[2, k], recv_sems.at[2, k], dev
            ).wait_recv()
        for cp in first + second:
            cp.wait_send()

    return pl.pallas_call(
        body,
        name="ada_exchange",
        in_specs=[VMEM, VMEM, VMEM, VMEM],
        out_specs=[VMEM, VMEM, VMEM],
        out_shape=[
            jax.ShapeDtypeStruct((N_DEV, SUBLANES, D_MODEL), F32),
            jax.ShapeDtypeStruct((N_DEV, SUBLANES, LANES), F32),
            jax.ShapeDtypeStruct((N_DEV, nt, SUBLANES, ncol), F32),
        ],
        scratch_shapes=[
            pltpu.VMEM((nt, N_DEV * SUBLANES, ncol), F32),
            pltpu.SemaphoreType.DMA((3, N_DEV)),
            pltpu.SemaphoreType.DMA((3, N_DEV)),
        ],
        compiler_params=pltpu.CompilerParams(vmem_limit_bytes=VMEM_LIMIT_BYTES),
    )(c8, w4, b4, ln8)


def small_exchange(dmx, flat, pre, after):
    rows = flat.shape[1]

    def body(dmx_ref, flat_ref, pre_ref, _, dmrecv_ref, red_ref, land_ref, send_sems, recv_sems):
        x, y, c, me = _position()
        dmrecv_ref[me] = dmx_ref[me]
        land_ref[me] = flat_ref[me]
        first = []
        for k in range(1, N_DEV):
            dev, pid = _peer(k, x, y, c)
            first.append(_remote(dmx_ref.at[pid], dmrecv_ref.at[me], send_sems.at[0, k], recv_sems.at[0, k], dev))
            first.append(_remote(flat_ref.at[pid], land_ref.at[me], send_sems.at[1, k], recv_sems.at[1, k], dev))
        for cp in first:
            cp.start()
        for k in range(1, N_DEV):
            dev, pid = _peer(k, x, y, c)
            _remote(dmx_ref.at[pid], dmrecv_ref.at[pid], send_sems.at[0, k], recv_sems.at[0, k], dev).wait_recv()
            _remote(flat_ref.at[pid], land_ref.at[pid], send_sems.at[1, k], recv_sems.at[1, k], dev).wait_recv()
        total, total_pre = land_ref[0], pre_ref[0]
        for s in range(1, N_DEV):
            total, total_pre = total + land_ref[s], total_pre + pre_ref[s]
        red_ref[me, :rows, :] = total
        red_ref[me, rows:, :] = total_pre
        second = []
        for k in range(1, N_DEV):
            dev, _ = _peer(k, x, y, c)
            second.append(_remote(red_ref.at[me], red_ref.at[me], send_sems.at[2, k], recv_sems.at[2, k], dev))
        for cp in second:
            cp.start()
        for k in range(1, N_DEV):
            dev, pid = _peer(k, x, y, c)
            _remote(red_ref.at[pid], red_ref.at[pid], send_sems.at[2, k], recv_sems.at[2, k], dev).wait_recv()
        for cp in first + second:
            cp.wait_send()

    return pl.pallas_call(
        body,
        name="small_exchange",
        in_specs=[VMEM, VMEM, VMEM, ANY],
        out_specs=[VMEM, VMEM],
        out_shape=[
            jax.ShapeDtypeStruct(dmx.shape, F32),
            jax.ShapeDtypeStruct((N_DEV, rows + pre.shape[1], LANES), F32),
        ],
        scratch_shapes=[
            pltpu.VMEM(flat.shape, F32),
            pltpu.SemaphoreType.DMA((3, N_DEV)),
            pltpu.SemaphoreType.DMA((3, N_DEV)),
        ],
        compiler_params=pltpu.CompilerParams(vmem_limit_bytes=VMEM_LIMIT_BYTES),
    )(dmx, flat, pre, after)


HBM = pl.BlockSpec(memory_space=pltpu.HBM)
SEM = pl.BlockSpec(memory_space=pltpu.SEMAPHORE)
EFFECT = pltpu.SideEffectType.DATAFLOW_SIDE_EFFECTING


REGROUP_ROWS = 256


def shards_to_columns(x, *, name):
    p, k, n = x.shape

    def body(x_ref, o_ref):
        for s in range(p):
            o_ref[:, s * n : (s + 1) * n] = x_ref[s]

    return pl.pallas_call(
        body,
        name=name,
        grid=(k // REGROUP_ROWS,),
        in_specs=[pl.BlockSpec((p, REGROUP_ROWS, n), lambda i: (0, i, 0))],
        out_specs=pl.BlockSpec((REGROUP_ROWS, p * n), lambda i: (i, 0)),
        out_shape=jax.ShapeDtypeStruct((k, p * n), x.dtype),
        compiler_params=_params(("parallel",)),
    )(x)


def columns_to_shards(xs, *, name):
    k = xs[0].shape[0]
    widths = [x.shape[1] for x in xs]
    n = sum(widths) // N_DEV
    pieces = []
    for s in range(N_DEV):
        start = 0
        for i, w in enumerate(widths):
            lo, hi = max(start, s * n), min(start + w, (s + 1) * n)
            if lo < hi:
                pieces.append((i, lo - start, s, lo - s * n, hi - lo))
            start += w

    def body(*refs):
        x_refs, o_ref = refs[: len(xs)], refs[-1]
        for i, c0, s, d0, w in pieces:
            o_ref[s, :, d0 : d0 + w] = x_refs[i][:, c0 : c0 + w]

    return pl.pallas_call(
        body,
        name=name,
        grid=(k // REGROUP_ROWS,),
        in_specs=[pl.BlockSpec((REGROUP_ROWS, w), lambda i: (i, 0)) for w in widths],
        out_specs=pl.BlockSpec((N_DEV, REGROUP_ROWS, n), lambda i: (0, i, 0)),
        out_shape=jax.ShapeDtypeStruct((N_DEV, k, n), xs[0].dtype),
        compiler_params=_params(("parallel",)),
    )(*xs)


def _own_slot(me, block):
    land = lax.empty((N_DEV, *block.shape), block.dtype)
    return lax.dynamic_update_slice_in_dim(land, block[None], me, axis=0)


N_CHIP_PEERS = 3


class Gather:
    def __init__(self, shards, lands, after, *, name):
        nt = len(shards)
        self.name = name

        def body(*refs):
            src_refs, land_refs = refs[:nt], refs[nt : 2 * nt]
            send_sems, recv_sems = refs[2 * nt + 1 : 3 * nt + 1], refs[3 * nt + 1 : 4 * nt + 1]
            token = refs[-1]
            x, y, c, me = _position()
            for t in range(nt):
                for k, dev in enumerate(self._targets(x, y, c)):
                    _remote(src_refs[t], land_refs[t].at[me], send_sems[t].at[k], recv_sems[t].at[k], dev).start()
            token[...] = jnp.zeros_like(token)

        outs = pl.pallas_call(
            body,
            name=name + "_start",
            in_specs=[HBM] * (2 * nt) + [ANY],
            out_specs=[SEM] * (2 * nt) + [HBM] * (2 * nt) + [VMEM],
            out_shape=[pltpu.SemaphoreType.DMA((1 + N_CHIP_PEERS,))] * (2 * nt)
            + [pltpu.HBM(a.shape, a.dtype) for a in (*shards, *lands)]
            + [jax.ShapeDtypeStruct((SUBLANES, LANES), F32)],
            input_output_aliases={i: 2 * nt + i for i in range(2 * nt)},
            compiler_params=pltpu.CompilerParams(has_side_effects=EFFECT),
        )(*[pltpu.with_memory_space_constraint(a, pltpu.HBM) for a in (*shards, *lands)], after)
        self.send_sems, self.recv_sems = list(outs[:nt]), list(outs[nt : 2 * nt])
        self.srcs, self.lands = list(outs[2 * nt : 3 * nt]), list(outs[3 * nt : 4 * nt])
        self.token = outs[-1]

    @staticmethod
    def _chips(x, y):
        return [(1 - x, y), (x, 1 - y), (1 - x, 1 - y)]

    @classmethod
    def _targets(cls, x, y, c):
        return [(x, y, 1 - c)] + [(*chip, c) for chip in cls._chips(x, y)]

    def zero(self):
        return self.token[0, 0]

    @staticmethod
    def _slot(px, py, pc):
        return 4 * px + 2 * py + pc

    def pass_on(self, which, after, *, name):
        n = len(which)

        def pass_body(*refs):
            land_refs, recv_sems = refs[:n], refs[n : 2 * n]
            fwd_send, fwd_recv = refs[3 * n + 1 : 4 * n + 1], refs[4 * n + 1 : 5 * n + 1]
            token = refs[-1]
            x, y, c, _ = _position()
            for t in range(n):
                for j, chip in enumerate(self._chips(x, y)):
                    blk = land_refs[t].at[self._slot(*chip, c)]
                    _remote(blk, blk, fwd_send[t].at[j], recv_sems[t].at[1 + j], (*chip, c)).wait_recv()
                    _remote(blk, blk, fwd_send[t].at[j], fwd_recv[t].at[j], (x, y, 1 - c)).start()
            token[...] = jnp.zeros_like(token)

        lands = [self.lands[t] for t in which]
        outs = pl.pallas_call(
            pass_body,
            name=name,
            in_specs=[HBM] * n + [SEM] * n + [ANY],
            out_specs=[HBM] * n + [SEM] * (2 * n) + [VMEM],
            out_shape=[pltpu.HBM(a.shape, a.dtype) for a in lands]
            + [pltpu.SemaphoreType.DMA((N_CHIP_PEERS,))] * (2 * n)
            + [jax.ShapeDtypeStruct((SUBLANES, LANES), F32)],
            input_output_aliases={i: i for i in range(n)},
            compiler_params=pltpu.CompilerParams(has_side_effects=EFFECT),
        )(*lands, *[self.recv_sems[t] for t in which], after)
        return (which, outs[:n], outs[n : 2 * n], outs[2 * n : 3 * n]), outs[-1]

    def wait(self, which, after, *, name):
        return self.finish(self.pass_on(which, after, name=name + "_pass")[0], after, name=name)

    def finish(self, passed, after, *, name):
        which, lands, fwd_send, fwd_recv = passed
        n = len(which)
        slot = self._slot

        def wait_body(*refs):
            src_refs, land_refs = refs[:n], refs[n : 2 * n]
            send_sems, recv_sems = refs[2 * n : 3 * n], refs[3 * n : 4 * n]
            fwd_send, fwd_recv = refs[4 * n : 5 * n], refs[5 * n : 6 * n]
            x, y, c, me = _position()
            sibling = (x, y, 1 - c)
            for t in range(n):
                for k, dev in enumerate(self._targets(x, y, c)):
                    _remote(src_refs[t], land_refs[t].at[me], send_sems[t].at[k], recv_sems[t].at[k], dev).wait_send()
                blk = land_refs[t].at[slot(x, y, 1 - c)]
                _remote(blk, blk, send_sems[t].at[0], recv_sems[t].at[0], sibling).wait_recv()
                for j, chip in enumerate(self._chips(x, y)):
                    sent = land_refs[t].at[slot(*chip, c)]
                    _remote(sent, sent, fwd_send[t].at[j], fwd_recv[t].at[j], sibling).wait_send()
                    got = land_refs[t].at[slot(*chip, 1 - c)]
                    _remote(got, got, fwd_send[t].at[j], fwd_recv[t].at[j], sibling).wait_recv()

        srcs = [self.srcs[t] for t in which]
        outs = pl.pallas_call(
            wait_body,
            name=name,
            in_specs=[HBM] * (2 * n) + [SEM] * (4 * n) + [ANY],
            out_specs=[HBM] * (2 * n),
            out_shape=[pltpu.HBM(a.shape, a.dtype) for a in (*srcs, *lands)],
            input_output_aliases={i: i for i in range(2 * n)},
            compiler_params=pltpu.CompilerParams(has_side_effects=EFFECT),
        )(
            *srcs, *lands, *[self.send_sems[t] for t in which], *[self.recv_sems[t] for t in which], *fwd_send,
            *fwd_recv, after,
        )
        return outs[n:]


class Scatter:
    def __init__(self, srcs, lands, after, *, name):
        self.name = name
        nt = self.nt = len(srcs)
        peers = N_DEV - 1

        def body(*refs):
            src_refs, land_refs = refs[:nt], refs[nt : 2 * nt]
            send_sems, recv_sems = refs[2 * nt + 1 : 3 * nt + 1], refs[3 * nt + 1 : 4 * nt + 1]
            token = refs[-1]
            x, y, c, me = _position()
            for t in range(nt):
                for k in range(1, N_DEV):
                    dev, pid = _peer(k, x, y, c)
                    src = src_refs[t].at[pid]
                    _remote(src, land_refs[t].at[me], send_sems[t].at[k - 1], recv_sems[t].at[k - 1], dev).start()
            token[...] = jnp.zeros_like(token)

        outs = pl.pallas_call(
            body,
            name=name + "_start",
            in_specs=[HBM] * (2 * nt) + [ANY],
            out_specs=[SEM] * (2 * nt) + [HBM] * (2 * nt) + [VMEM],
            out_shape=[pltpu.SemaphoreType.DMA((peers,))] * (2 * nt)
            + [pltpu.HBM(a.shape, a.dtype) for a in (*srcs, *lands)]
            + [jax.ShapeDtypeStruct((SUBLANES, LANES), F32)],
            input_output_aliases={i: 2 * nt + i for i in range(2 * nt)},
            compiler_params=pltpu.CompilerParams(has_side_effects=EFFECT),
        )(*[pltpu.with_memory_space_constraint(a, pltpu.HBM) for a in (*srcs, *lands)], after)
        self.send_sems, self.recv_sems = outs[:nt], outs[nt : 2 * nt]
        self.srcs, self.lands = outs[2 * nt : 3 * nt], outs[3 * nt : 4 * nt]
        self.token = outs[-1]

    def zero(self):
        return self.token[0, 0]

    def wait(self, which, after, *, name):
        n = len(which)

        def body(*refs):
            src_refs, land_refs = refs[:n], refs[n : 2 * n]
            send_sems, recv_sems = refs[2 * n : 3 * n], refs[3 * n : 4 * n]
            x, y, c, _ = _position()
            for t in range(n):
                for k in range(1, N_DEV):
                    dev, pid = _peer(k, x, y, c)
                    src = src_refs[t].at[pid]
                    cp = _remote(src, land_refs[t].at[pid], send_sems[t].at[k - 1], recv_sems[t].at[k - 1], dev)
                    cp.wait_send()
                    cp.wait_recv()

        srcs = [self.srcs[t] for t in which]
        lands = [self.lands[t] for t in which]
        outs = pl.pallas_call(
            body,
            name=name,
            in_specs=[HBM] * (2 * n) + [SEM] * (2 * n) + [ANY],
            out_specs=[HBM] * (2 * n),
            out_shape=[pltpu.HBM(a.shape, a.dtype) for a in (*srcs, *lands)],
            input_output_aliases={i: i for i in range(2 * n)},
            compiler_params=pltpu.CompilerParams(has_side_effects=EFFECT),
        )(*srcs, *lands, *[self.send_sems[t] for t in which], *[self.recv_sems[t] for t in which], after)
        return outs[n:]


def _adam_update(g, w, m, v):
    m2 = ADAM_B1 * m + (1.0 - ADAM_B1) * g
    v2 = ADAM_B2 * v + (1.0 - ADAM_B2) * jnp.square(g)
    m_hat = m2 / (1.0 - ADAM_B1**ADAM_STEP)
    v_hat = v2 / (1.0 - ADAM_B2**ADAM_STEP)
    delta = -ADAM_LR * (m_hat / (jnp.sqrt(v_hat) + ADAM_EPS) + ADAM_WD * w)
    return delta, m2, v2


def adamw(gparts, w, m, v, *, name):
    nl, r, c = w.shape
    p = gparts[0].shape[0]
    tr = r if r <= 256 else (256 if c <= D_MODEL else 128)
    ni = r // tr

    def body(*refs):
        g_refs = refs[:nl]
        w_ref, m_ref, v_ref, go_ref, d_ref, mo_ref, vo_ref = refs[nl:]
        for layer in range(nl):

            @pl.when(pl.program_id(0) == layer)
            def _(g_ref=g_refs[layer]):
                g = g_ref[0].astype(F32)
                for i in range(1, p):
                    g = g + g_ref[i].astype(F32)
                delta, m2, v2 = _adam_update(g, w_ref[...], m_ref[...], v_ref[...])
                go_ref[...] = g
                d_ref[...] = delta
                mo_ref[...] = m2
                vo_ref[...] = v2

    def parts_spec(layer):
        def index(l, i):
            return (0, jnp.where(l == layer, i, jnp.where(l < layer, 0, ni - 1)), 0)

        return pl.BlockSpec((p, tr, c), index)

    blk = pl.BlockSpec((None, tr, c), lambda l, i: (l, i, 0))
    return pl.pallas_call(
        body,
        name=name,
        grid=(nl, ni),
        in_specs=[*[parts_spec(layer) for layer in range(nl)], blk, blk, blk],
        out_specs=[blk] * 4,
        out_shape=[jax.ShapeDtypeStruct((nl, r, c), F32)] * 4,
        compiler_params=_params(("arbitrary", "arbitrary")),
    )(*gparts, w, m, v)


def adamw_small(items, *, name):
    n = len(items)

    def body(*refs):
        ins, outs = refs[: 4 * n], refs[4 * n :]
        for t in range(n):
            g_ref, w_ref, m_ref, v_ref = ins[4 * t : 4 * t + 4]
            g = g_ref[...]
            delta, m2, v2 = _adam_update(g, w_ref[...], m_ref[...], v_ref[...])
            for o_ref, val in zip(outs[4 * t : 4 * t + 4], (g, delta, m2, v2), strict=True):
                o_ref[...] = val

    outs = pl.pallas_call(
        body,
        name=name,
        out_shape=[jax.ShapeDtypeStruct(item[1].shape, F32) for item in items for _ in range(4)],
    )(*[a for item in items for a in item])
    return [outs[4 * t : 4 * t + 4] for t in range(n)]


def ada_grad_adamw(cg, dmrecv, w4, m4, v4, *, name):
    nt, k, ncol = w4.shape

    def body(cg_ref, dm_ref, w_ref, m_ref, v_ref, go_ref, d_ref, mo_ref, vo_ref, gb_ref):
        sc = _silu_bf16(cg_ref[...].reshape(N_DEV * SUBLANES, k))
        dm = dm_ref[...].reshape(N_DEV * SUBLANES, ncol)
        g = lax.dot_general(sc, dm.astype(BF16), (((0,), (0,)), ((), ())), preferred_element_type=F32)
        delta, m2, v2 = _adam_update(g, w_ref[...], m_ref[...], v_ref[...])
        go_ref[...] = g
        d_ref[...] = delta
        mo_ref[...] = m2
        vo_ref[...] = v2
        gb_ref[...] = jnp.broadcast_to(jnp.sum(dm, axis=0, keepdims=True), (SUBLANES, ncol))

    wblk = pl.BlockSpec((None, k, ncol), lambda t: (t, 0, 0))
    return pl.pallas_call(
        body,
        name=name,
        grid=(nt,),
        in_specs=[
            pl.BlockSpec((N_DEV, SUBLANES, k), lambda t: (0, 0, 0)),
            pl.BlockSpec((N_DEV, None, SUBLANES, ncol), lambda t: (0, t, 0, 0)),
            wblk,
            wblk,
            wblk,
        ],
        out_specs=[wblk] * 4 + [pl.BlockSpec((None, SUBLANES, ncol), lambda t: (t, 0, 0))],
        out_shape=[jax.ShapeDtypeStruct((nt, k, ncol), F32)] * 4 + [jax.ShapeDtypeStruct((nt, SUBLANES, ncol), F32)],
        compiler_params=_params(("parallel",)),
    )(cg, dmrecv, w4, m4, v4)


def kernel(x, c, ada_w, ada_b, ln_g, ln_b, a_w_in, a_b_in, a_vn_g, a_vn_b, a_w_s, a_b_s, a_w_out, b_w_qkv, b_w_out, mlp_w_up, mlp_w_down, loss_target, m_ada_w, m_ada_b, m_ln_g, m_ln_b, m_a_w_in, m_a_b_in, m_a_vn_g, m_a_vn_b, m_a_w_s, m_a_b_s, m_a_w_out, m_b_w_qkv, m_b_w_out, m_mlp_w_up, m_mlp_w_down, v_ada_w, v_ada_b, v_ln_g, v_ln_b, v_a_w_in, v_a_b_in, v_a_vn_g, v_a_vn_b, v_a_w_s, v_a_b_s, v_a_w_out, v_b_w_qkv, v_b_w_out, v_mlp_w_up, v_mlp_w_down):
    x0 = x[0]
    target = loss_target[0]
    me = 4 * lax.axis_index("x") + 2 * lax.axis_index("y") + lax.axis_index("c")

    ada_w4 = ada_w.reshape(N_SUB, D_MODEL, -1)
    ada_b4 = ada_b.reshape(N_SUB, -1)
    ln8 = jnp.concatenate([ln_g.reshape(N_SUB, -1), ln_b.reshape(N_SUB, -1)], axis=0)
    c8 = jnp.broadcast_to(c, (SUBLANES, D_MODEL))
    cg, lng, mrecv = ada_exchange(c8, ada_w4, ada_b4, ln8)

    W_IN, W_AOUT, W_UP0, W_DN0, W_QKV, W_BOUT, W_UP1, W_DN1 = range(8)
    shards = [
        a_w_in[0].astype(BF16),
        a_w_out[0].astype(BF16),
        mlp_w_up[0].astype(BF16),
        mlp_w_down[0].astype(BF16),
        b_w_qkv[0].astype(BF16),
        b_w_out[0].astype(BF16),
        mlp_w_up[1].astype(BF16),
        mlp_w_down[1].astype(BF16),
    ]
    gather = Gather(shards, [_own_slot(me, s) for s in shards], mrecv, name="gather")

    modv = mrecv[:, :, 0, :].transpose(1, 0, 2).reshape(N_SUB, 3 * D_MODEL) + gather.zero()
    shift = [modv[t : t + 1, :D_MODEL] for t in range(N_SUB)]
    scale = [modv[t : t + 1, D_MODEL : 2 * D_MODEL] for t in range(N_SUB)]
    gate1 = [1.0 + modv[t : t + 1, 2 * D_MODEL :] for t in range(N_SUB)]
    lng_full = [lng[:, t, :].reshape(1, D_MODEL) for t in range(N_SUB)]
    lnb_full = [lng[:, N_SUB + t, :].reshape(1, D_MODEL) for t in range(N_SUB)]

    ident = lambda acc: (acc,)
    def relu2(a):
        r = jnp.maximum(a, jnp.zeros_like(a))
        return r * r
    vn_g, vn_b, w_s = a_vn_g, a_vn_b, a_w_s[0]
    bias_full = jnp.repeat(a_b_s[0].T, A_GROUP_DIM, axis=1)
    w_up3, w_dn3 = [None, None], [None, None]

    def mlp_forward(i, h, up, dn, x_in=None, t_next=None):
        w_up3[i], w_dn3[i] = up, dn.reshape(1, D_FF, D_MODEL)
        (a,) = mm_nn(h, w_up3[i], name=f"mlp{i}_up", tm=2048, ps=2, tn=512, tk=D_MODEL, epilogue=ident, outs=(BF16,))
        if t_next is None:
            (y,) = mm_nn(
                a, w_dn3[i], name=f"mlp{i}_down", tm=1024, ps=1, tn=512, tk=D_FF, prologue=relu2, epilogue=ident, outs=(BF16,)
            )
            return a, y
        y, xn, hn = mm_nn(
            a, w_dn3[i], name=f"mlp{i}_down", tm=512, ps=1, tn=D_MODEL, tk=D_FF, prologue=relu2,
            epilogue=residual_ln_epilogue, extras=residual_extras(x_in, t_next - 1), outs=(BF16, F32, BF16),
        )
        return a, y, xn, hn

    def residual_extras(x_in, t):
        rows = (gate1[t], lng_full[t], lnb_full[t], scale[t + 1], shift[t + 1])
        return [(x_in, "full")] + [(r, "row") for r in rows]

    h0 = modulate(x0, scale[0], shift[0], name="modulate0")
    w_in3, w_aout3 = gather.wait([W_IN, W_AOUT], h0, name="gather_wait_a")
    w_aout3 = w_aout3.reshape(1, D_MODEL, D_MODEL)
    (a_pre,) = mm_nn(
        h0, w_in3, name="a_in", tm=2048, ps=4, tn=256, tk=D_MODEL, epilogue=lambda acc, b: (acc + b,),
        extras=[(a_b_in, "row")], outs=(BF16,),
    )
    p_gate = gate_fwd(a_pre, vn_g, vn_b, w_s, bias_full, name="gate_fwd")
    y0, x1, h1 = mm_nn(
        p_gate, w_aout3, name="a_out", tm=1024, ps=1, tn=D_MODEL, tk=D_MODEL, epilogue=residual_ln_epilogue,
        extras=residual_extras(x0, 0), outs=(BF16, F32, BF16),
    )
    w_mlp0 = gather.wait([W_UP0, W_DN0], y0, name="gather_wait_mlp0")
    a1, y1, x2, h2 = mlp_forward(0, h1, *w_mlp0, x_in=x1, t_next=2)
    (w_qkv_shards,) = gather.wait([W_QKV], y1, name="gather_wait_qkv")
    w_qkv3 = shards_to_columns(w_qkv_shards, name="w_qkv_columns")[None]
    pat_tiles = 3
    dil = [d for _, d in B_PATTERNS]
    qkv_p, pat_o, pat_lse = [], [], []
    for g in range(N_PAT):
        (qkv_g,) = mm_nn(
            h2, w_qkv3, name=f"b_qkv{g}", tm=2048, ps=1, tn=D_MODEL, tk=D_MODEL, epilogue=ident, outs=(BF16,),
            b_tile0=pat_tiles * g, b_tiles=pat_tiles, out_streams=dil[g],
        )
        o_g, lse_g = attn_fwd(qkv_g, g, name=f"attn_fwd{g}", after=rest_token if g == 1 else None)
        if g == 0:
            rest_passed, rest_token = gather.pass_on([W_BOUT, W_UP1, W_DN1], lse_g, name="gather_pass_rest")
        qkv_p.append(qkv_g)
        pat_o.append(o_g)
        pat_lse.append(lse_g)
    o_b, o_f, *lse_pat = attn_combine(pat_o, pat_lse, name="attn_combine")
    lse_pat = [x.reshape(SEQ, LANES) for x in lse_pat]
    w_bout3, *w_mlp1 = gather.finish(rest_passed, lse_pat[0], name="gather_wait_rest")
    w_bout3 = w_bout3.reshape(1, D_MODEL, D_MODEL)
    y2, x3, h3 = mm_nn(
        o_b, w_bout3, name="b_out", tm=1024, ps=1, tn=D_MODEL, tk=D_MODEL, epilogue=residual_ln_epilogue,
        extras=residual_extras(x2, 2), outs=(BF16, F32, BF16),
    )
    a3, y3 = mlp_forward(1, h3, *w_mlp1)
    dz3, dy3, st3, loss_local = residual_ln_loss_bwd(
        x3, y3, gate1[3], lng_full[3], lnb_full[3], target, name="res_ln3_loss_bwd"
    )

    def scatter(parts, after, name):
        parts = [p.reshape(N_DEV, -1, p.shape[-1]) for p in parts]
        lands = [_own_slot(me, lax.dynamic_index_in_dim(p, me, 0, keepdims=False)) for p in parts]
        return Scatter(parts, lands, after, name=name)

    xs_in, ys = [x0, x1, x2, x3], [y0, y1, y2, y3]

    def residual_bwd_extras(dz_later, t, zero):
        return [
            (dz_later, "full"), (scale[t + 1] + zero, "row"), (xs_in[t], "full"), (ys[t], "full"),
            (gate1[t], "row"), (lng_full[t], "row"), (lnb_full[t], "row"),
        ]

    def mlp_backward(i, h, a, dy, dz_later, t):
        (da,) = mm_nt(
            dy,
            w_dn3[i],
            name=f"mlp{i}_da",
            tm=2048,
            tko=1024,
            ps=1,
            tc=D_MODEL,
            epilogue=lambda acc, act: (acc * (2.0 * jnp.maximum(act.astype(F32), 0.0)),),
            extras=[(a, "full")],
            outs=(BF16,),
        )
        dw_dn = mm_tn(
            a, dy, name=f"mlp{i}_dw_down", p=1, tk=1024, ps=1, tn=D_MODEL, tmc=2048, prologue=relu2, out_dtype=BF16
        )
        dw_up = mm_tn(h, da, name=f"mlp{i}_dw_up", p=N_DEV, tk=1024, ps=2, tn=512, tmc=2048, out_dtype=BF16)
        rs = scatter([dw_up, dw_dn], da, f"scatter_mlp{i}")
        dz, dy_before, st = mm_nt(
            da, w_up3[i], name=f"mlp{i}_dh", tm=512, tko=D_MODEL, ps=N_DEV, tc=512, epilogue=residual_bwd_epilogue,
            extras=residual_bwd_extras(dz_later, t, rs.zero()), outs=(F32, BF16), stats=True,
        )
        return rs, dz, dy_before, st

    rs_mlp1, dz2, dy2, st2 = mlp_backward(1, h3, a3, dy3, dz3, 2)
    do_delta = mm_nt(
        dy2, w_bout3, name="b_do", tm=1024, tko=D_MODEL, ps=1, tc=D_MODEL, extras=[(o_f, "full")],
        epilogue=lambda acc, o: (acc,) * N_PAT + (_reduce_groups(acc * o),) * N_PAT,
        outs=(BF16,) * N_PAT + (F32,) * N_PAT, out_widths=[D_MODEL] * N_PAT + [LANES] * N_PAT, out_streams=dil + dil,
    )
    do_pat = [x.reshape(SEQ, D_MODEL) for x in do_delta[:N_PAT]]
    delta_pat = [x.reshape(SEQ, LANES) for x in do_delta[N_PAT:]]
    dh2, dw_pat = None, []
    for g in range(N_PAT):
        dqkv_g = attn_bwd(qkv_p[g], do_pat[g], lse_pat[g], delta_pat[g], g, name=f"attn_bwd{g}")
        dw_pat.append(
            mm_tn(
                h2, dqkv_g, name=f"b_dw_qkv{g}", p=1, tk=1024, ps=1, tn=D_MODEL, tmc=2048, out_dtype=BF16,
                g_streams=dil[g],
            )[0]
        )
        if g == N_PAT - 1:
            break
        (dh2,) = mm_nt(
            dqkv_g, w_qkv3, name=f"b_dh{g}", tm=2048, tko=512, ps=1, tc=pat_tiles * D_MODEL, outs=(F32,),
            b_tile0=g, g_streams=dil[g],
            epilogue=ident if g == 0 else (lambda acc, prev, d=dil[g]: (_from_streams(acc, d) + prev,)),
            extras=[] if g == 0 else [(dh2, "full")],
        )
    dw_bout = mm_tn(o_b, dy2, name="b_dw_out", p=1, tk=1024, ps=1, tn=D_MODEL, tmc=2048, out_dtype=BF16)
    dw_qkv = columns_to_shards(dw_pat, name="dw_qkv_shards")
    rs_b = scatter([dw_qkv, dw_bout], dqkv_g, "scatter_b")
    dz1, dy1, st1 = mm_nt(
        dqkv_g, w_qkv3, name=f"b_dh{N_PAT - 1}", tm=512, tko=D_MODEL, ps=1, tc=pat_tiles * D_MODEL, outs=(F32, BF16),
        b_tile0=N_PAT - 1, g_streams=dil[-1], stats=True,
        epilogue=lambda acc, prev, *rest: residual_bwd_epilogue(_from_streams(acc, dil[-1]) + prev, *rest),
        extras=[(dh2, "full")] + residual_bwd_extras(dz2, 1, rs_b.zero()),
    )
    rs_mlp0, dz0, dy0, st0 = mlp_backward(0, h1, a1, dy1, dz1, 0)
    (dp_gate,) = mm_nt(dy0, w_aout3, name="a_dp", tm=2048, tko=1024, ps=1, tc=D_MODEL, epilogue=ident, outs=(F32,))
    dw_aout = mm_tn(p_gate, dy0, name="a_dw_out", p=1, tk=1024, ps=1, tn=D_MODEL, tmc=2048, out_dtype=BF16)
    rs_aout = scatter([dw_aout], dp_gate, "scatter_a_out")
    da0, d_ws, d_bs, gate_rows = gate_bwd(a_pre, dp_gate, vn_g + rs_aout.zero(), vn_b, w_s, bias_full, name="gate_bwd")
    rs_ws = scatter([d_ws], gate_rows, "scatter_a_w_s")
    dw_in = mm_tn(h0, da0, name="a_dw_in", p=N_DEV, tk=1024, ps=4, tn=256, tmc=2048, out_dtype=BF16)
    rs_in = scatter([dw_in], rs_ws.token, "scatter_a_in")
    grad_x, stf = mm_nt(
        da0, w_in3, name="a_dh", tm=1024, tko=D_MODEL, ps=N_DEV, tc=256, epilogue=input_bwd_epilogue,
        extras=[(dz0, "full"), (scale[0] + rs_in.zero(), "row"), (x0, "full")], outs=(F32,), stats=True,
    )

    results = {}

    def update(wname, gparts, w, m, v):
        shape = w.shape
        layers = len(gparts) if isinstance(gparts, list) else 1
        w3 = w.reshape(layers, -1, shape[-1])
        parts = [g.reshape(g.shape[0], *w3.shape[1:]) for g in (gparts if layers > 1 else [gparts])]
        outs = adamw(parts, w3, m.reshape(w3.shape), v.reshape(w3.shape), name=f"adamw_{wname}")
        results[wname] = [o.reshape(shape) for o in outs]

    g_up1, g_dn1 = rs_mlp1.wait([0, 1], grad_x, name="scatter_wait_mlp1")
    g_qkv, g_bout = rs_b.wait([0, 1], grad_x, name="scatter_wait_b")
    update("b_w_qkv", g_qkv, b_w_qkv, m_b_w_qkv, v_b_w_qkv)
    update("b_w_out", g_bout, b_w_out, m_b_w_out, v_b_w_out)
    g_up0, g_dn0 = rs_mlp0.wait([0, 1], grad_x, name="scatter_wait_mlp0")
    update("mlp_w_up", [g_up0, g_up1], mlp_w_up, m_mlp_w_up, v_mlp_w_up)
    update("mlp_w_down", [g_dn0, g_dn1], mlp_w_down, m_mlp_w_down, v_mlp_w_down)

    stats_after = [stf, st0, st1, st2]
    stats_own = [st0, st1, st2, st3]
    dm = jnp.stack(
        [
            jnp.concatenate(
                [stats_after[t][ST_DSHIFT], stats_after[t][ST_DSCALE], stats_own[t][ST_DGATE]], axis=0
            )
            for t in range(N_SUB)
        ]
    )
    ncol = 3 * D_MODEL // N_DEV
    dmx = jnp.pad(
        dm.reshape(N_SUB, N_DEV, ncol).transpose(1, 0, 2)[:, :, None, :], ((0, 0), (0, 0), (0, SUBLANES - 1), (0, 0))
    )
    small = [
        gate_rows[0],
        gate_rows[1],
        d_bs[:, :A_GROUPS].T.reshape(-1),
        *[stats_own[t][ST_DG] for t in range(N_SUB)],
        *[stats_own[t][ST_DB] for t in range(N_SUB)],
        jnp.pad(loss_local.reshape(1), (0, LANES - 1)),
    ]
    n_small = sum(s.size for s in small)
    part_rows = -(-n_small // (N_DEV * LANES * SUBLANES)) * SUBLANES
    flat = jnp.concatenate(small + [jnp.zeros((N_DEV * part_rows * LANES - n_small,), F32)])
    (ws_parts,) = rs_ws.wait([0], results["mlp_w_down"][3], name="scatter_wait_a_w_s")
    dmrecv, reduced = small_exchange(dmx, flat.reshape(N_DEV, part_rows, LANES), ws_parts, results["mlp_w_down"][3])
    g_ws = reduced[:, part_rows:, :].reshape(-1)
    reduced = reduced[:, :part_rows, :].reshape(-1)
    sizes = [2 * D_MODEL, D_MODEL, D_MODEL, A_GROUPS * CHUNK, N_SUB * D_MODEL, N_SUB * D_MODEL]
    offs = [sum(sizes[:i]) for i in range(len(sizes) + 1)]
    g_b_in, g_vn_g, g_vn_b, g_bs, g_lng, g_lnb = [reduced[offs[i] : offs[i + 1]] for i in range(len(sizes))]
    loss = reduced[offs[-1]]

    ada_outs = ada_grad_adamw(cg, dmrecv, ada_w4, m_ada_w.reshape(ada_w4.shape), v_ada_w.reshape(ada_w4.shape), name="ada_grad_adamw")
    results["ada_w"] = [o.reshape(ada_w.shape) for o in ada_outs[:4]]
    ln_cols = D_MODEL // N_DEV
    my_ln = lambda gfull: lax.dynamic_slice_in_dim(gfull.reshape(N_SUB, N_DEV, ln_cols), me, 1, axis=1)
    small_params = [
        ("ada_b", ada_outs[4][:, 0, :], ada_b, m_ada_b, v_ada_b),
        ("ln_g", my_ln(g_lng), ln_g, m_ln_g, v_ln_g),
        ("ln_b", my_ln(g_lnb), ln_b, m_ln_b, v_ln_b),
        ("a_b_in", g_b_in, a_b_in, m_a_b_in, v_a_b_in),
        ("a_vn_g", g_vn_g, a_vn_g, m_a_vn_g, v_a_vn_g),
        ("a_vn_b", g_vn_b, a_vn_b, m_a_vn_b, v_a_vn_b),
        ("a_b_s", g_bs, a_b_s, m_a_b_s, v_a_b_s),
    ]
    small_outs = adamw_small(
        [[a.reshape(-1, w.shape[-1]) for a in (g, w, m, v)] for _, g, w, m, v in small_params], name="adamw_small"
    )
    for (wname, _, w, _, _), outs in zip(small_params, small_outs, strict=True):
        results[wname] = [o.reshape(w.shape) for o in outs]
    update("a_w_s", g_ws[None], a_w_s, m_a_w_s, v_a_w_s)
    (g_aout,) = rs_aout.wait([0], grad_x, name="scatter_wait_a_out")
    (g_in,) = rs_in.wait([0], grad_x, name="scatter_wait_a_in")
    update("a_w_in", g_in, a_w_in, m_a_w_in, v_a_w_in)
    update("a_w_out", g_aout, a_w_out, m_a_w_out, v_a_w_out)

    order = ["ada_w", "ada_b", "ln_g", "ln_b", "a_w_in", "a_b_in", "a_vn_g", "a_vn_b", "a_w_s", "a_b_s", "a_w_out", "b_w_qkv", "b_w_out", "mlp_w_up", "mlp_w_down"]
    return (loss, grad_x[None], *[results[n][0] for n in order], *[results[n][1] for n in order],
            *[results[n][2] for n in order], *[results[n][3] for n in order])
```

```python
import math

import jax
import jax.numpy as jnp
from jax import lax
from jax.experimental import pallas as pl
from jax.experimental.pallas import tpu as pltpu

F32 = jnp.float32
BF16 = jnp.bfloat16
MESH = pl.DeviceIdType.MESH
ANY = pl.BlockSpec(memory_space=pl.ANY)
VMEM = pl.BlockSpec(memory_space=pltpu.VMEM)

N_DEV = 8
D_MODEL = 1024
SEQ = 4096
DEPTH = 2
CHUNK = 128
A_GROUPS = 16
A_GROUP_DIM = D_MODEL // A_GROUPS
B_HEADS = 16
B_HEAD_DIM = 64
B_PATTERNS = ((128, 1), (512, 4), (2048, 16))
N_PAT = len(B_PATTERNS)
SPAN = 128
D_FF = 4 * D_MODEL
D_QKV = N_PAT * 3 * D_MODEL
ALPHA = (2 * DEPTH) ** 0.25
LN_EPS = 1e-5
NEG = -1e30
ADAM_LR = 0.001
ADAM_B1 = 0.9
ADAM_B2 = 0.999
ADAM_EPS = 1e-08
ADAM_WD = 0.01
ADAM_STEP = 10
GELU_C = math.sqrt(2.0 / math.pi)
GELU_A = 0.044715

VMEM_LIMIT_BYTES = 56 * 1024 * 1024
LANES = 128
SUBLANES = 8
ROW_TILE = 512
N_SUB = 2 * DEPTH


def _params(sem):
    return pltpu.CompilerParams(dimension_semantics=sem, vmem_limit_bytes=VMEM_LIMIT_BYTES)


def _lane(shape):
    return lax.broadcasted_iota(jnp.int32, shape, len(shape) - 1)


def _split_bf16(x):
    hi = x.astype(BF16)
    lo = (x - hi.astype(F32)).astype(BF16)
    return hi, lo


def _group_expand_matrix(groups_padded, width):
    per = width // A_GROUPS
    r = lax.broadcasted_iota(jnp.int32, (groups_padded, width), 0)
    c = lax.broadcasted_iota(jnp.int32, (groups_padded, width), 1)
    return (c // per == r).astype(BF16)


def _group_reduce_matrix(width, groups_padded):
    per = width // A_GROUPS
    r = lax.broadcasted_iota(jnp.int32, (width, groups_padded), 0)
    c = lax.broadcasted_iota(jnp.int32, (width, groups_padded), 1)
    return (r // per == c).astype(BF16)


def _expand_groups(w):
    e = _group_expand_matrix(LANES, D_MODEL)
    hi, lo = _split_bf16(w)
    return jnp.dot(hi, e, preferred_element_type=F32) + jnp.dot(lo, e, preferred_element_type=F32)


def _reduce_groups(x):
    e = _group_reduce_matrix(D_MODEL, LANES)
    hi, lo = _split_bf16(x)
    return jnp.dot(hi, e, preferred_element_type=F32) + jnp.dot(lo, e, preferred_element_type=F32)


def _to_streams(x, d):
    rows, w = x.shape
    return jnp.swapaxes(x.reshape(rows // d, d, w), 0, 1).reshape(rows, w)


def _from_streams(x, d):
    rows, w = x.shape
    return jnp.swapaxes(x.reshape(d, rows // d, w), 0, 1).reshape(rows, w)


def _column_tiles(p, n, ps, tn):
    assert (ps == 1 or tn == n) and p % ps == 0 and n % tn == 0
    q = n // tn
    return (p // ps) * q, q


def _extra_specs(extras, tm, width):
    specs = []
    for _, kind in extras:
        if kind == "row":
            specs.append(pl.BlockSpec((1, width), lambda i, j, c: (0, j)))
        else:
            specs.append(pl.BlockSpec((tm, width), lambda i, j, c: (i, j)))
    return specs


def mm_nn(a, b3, *, name, tm, ps, tn, tk, epilogue, extras=(), outs, prologue=None, b_tile0=0, b_tiles=None, out_streams=1):
    m, k = a.shape
    p, _, n = b3.shape
    nj, q = _column_tiles(p, n, ps, tn)
    nj = nj if b_tiles is None else b_tiles
    nk = k // tk
    width = ps * tn
    d = out_streams
    assert d == 1 or not extras

    def body(a_ref, b_ref, *rest):
        ex = rest[: len(extras)]
        out_refs = rest[len(extras) : len(extras) + len(outs)]
        kk = pl.program_id(2)
        av = a_ref[...] if prologue is None else prologue(a_ref[...])
        if d > 1:
            av = _to_streams(av, d)

        def finish(cs, acc):
            res = epilogue(acc, *[e[:, cs] for e in ex])
            for o_ref, r in zip(out_refs, res, strict=True):
                if d > 1:
                    o_ref[:, :, cs] = r.astype(o_ref.dtype).reshape(d, tm // d, tn)
                else:
                    o_ref[:, cs] = r.astype(o_ref.dtype)

        for s in range(ps):
            cs = slice(s * tn, (s + 1) * tn)
            part = jnp.dot(av, b_ref[s], preferred_element_type=F32)
            if nk == 1:
                finish(cs, part)
                continue
            acc_ref = rest[-1]

            @pl.when(kk == 0)
            def _(part=part, cs=cs):
                acc_ref[:, cs] = part

            @pl.when(kk > 0)
            def _(part=part, cs=cs):
                acc_ref[:, cs] += part

        if nk > 1:

            @pl.when(kk == nk - 1)
            def _():
                for s in range(ps):
                    cs = slice(s * tn, (s + 1) * tn)
                    finish(cs, rest[-1][:, cs])

    if d > 1:
        out_spec = pl.BlockSpec((d, tm // d, width), lambda i, j, kk: (0, i, j))
        out_shape = (d, m // d, nj * width)
    else:
        out_spec = pl.BlockSpec((tm, width), lambda i, j, kk: (i, j))
        out_shape = (m, nj * width)
    res = pl.pallas_call(
        body,
        name=name,
        grid=(m // tm, nj, nk),
        in_specs=[
            pl.BlockSpec((tm, tk), lambda i, j, kk: (i, kk)),
            pl.BlockSpec((ps, tk, tn), lambda i, j, kk: ((j + b_tile0) // q, kk, (j + b_tile0) % q)),
            *_extra_specs(extras, tm, width),
        ],
        out_specs=[out_spec for _ in outs],
        out_shape=[jax.ShapeDtypeStruct(out_shape, dt) for dt in outs],
        scratch_shapes=[pltpu.VMEM((tm, width), F32)] if nk > 1 else [],
        compiler_params=_params(("parallel", "parallel", "arbitrary")),
    )(a, b3, *[arr for arr, _ in extras])
    return [r.reshape(m, nj * width) for r in res]


def mm_nt(
    g, b3, *, name, tm, tko, ps, tc, epilogue, extras=(), outs, out_widths=None, out_streams=None, b_tile0=0,
    g_streams=1, stats=False,
):
    m, width = g.shape
    p, k, n = b3.shape
    _, q = _column_tiles(p, n, ps, tc)
    nc = width // (ps * tc)
    ds = g_streams
    assert not stats or tko == k
    widths = [tko] * len(outs) if out_widths is None else out_widths
    streams_out = [1] * len(outs) if out_streams is None else out_streams

    def out_spec(w, d):
        if d == 1:
            return pl.BlockSpec((tm, w), lambda i, j, c: (i, j))
        return pl.BlockSpec((d, tm // d, w), lambda i, j, c: (0, i, j))

    def out_struct(dt, w, d):
        cols = w * (k // tko)
        return jax.ShapeDtypeStruct((m, cols) if d == 1 else (d, m // d, cols), dt)

    def body(g_ref, b_ref, *rest):
        ex = rest[: len(extras)]
        out_refs = rest[len(extras) : len(extras) + len(outs)]
        c = pl.program_id(2)
        gv = g_ref[...].reshape(tm, ps * tc) if ds > 1 else g_ref[...]
        part = None
        for s in range(ps):
            d = lax.dot_general(gv[:, s * tc : (s + 1) * tc], b_ref[s], _NT, preferred_element_type=F32)
            part = d if part is None else part + d

        def finish(acc):
            res = epilogue(acc, *[e[...] for e in ex])
            if stats:
                *res, sums = res
                st_ref = rest[len(extras) + len(outs)]

                @pl.when(pl.program_id(0) == 0)
                def _():
                    st_ref[...] = jnp.zeros_like(st_ref)

                for row, val in enumerate(sums):
                    st_ref[row : row + 1, :] += val
            for o_ref, r, d_out in zip(out_refs, res, streams_out, strict=True):
                r = r.astype(o_ref.dtype)
                o_ref[...] = r if d_out == 1 else _to_streams(r, d_out).reshape(o_ref.shape)

        if nc == 1:
            finish(part)
            return
        acc_ref = rest[-1]

        @pl.when(c == 0)
        def _():
            acc_ref[...] = part

        @pl.when(c > 0)
        def _():
            acc_ref[...] += part

        @pl.when(c == nc - 1)
        def _():
            finish(acc_ref[...])

    return pl.pallas_call(
        body,
        name=name,
        grid=(m // tm, k // tko, nc),
        in_specs=[
            pl.BlockSpec((ds, tm // ds, ps * tc), lambda i, j, c: (0, i, c))
            if ds > 1
            else pl.BlockSpec((tm, ps * tc), lambda i, j, c: (i, c)),
            pl.BlockSpec((ps, tko, tc), lambda i, j, c: ((c + b_tile0) // q, j, (c + b_tile0) % q)),
            *_extra_specs(extras, tm, tko),
        ],
        out_specs=[out_spec(w, d) for w, d in zip(widths, streams_out, strict=True)]
        + [pl.BlockSpec((SUBLANES, tko), lambda i, j, c: (0, 0))] * stats,
        out_shape=[out_struct(dt, w, d) for dt, w, d in zip(outs, widths, streams_out, strict=True)]
        + [jax.ShapeDtypeStruct((SUBLANES, k), F32)] * stats,
        scratch_shapes=[pltpu.VMEM((tm, tko), F32)] if nc > 1 else [],
        compiler_params=_params(("arbitrary" if stats else "parallel", "parallel", "arbitrary")),
    )(g.reshape(ds, m // ds, width) if ds > 1 else g, b3, *[arr for arr, _ in extras])


def mm_tn(a, g, *, name, p, tk, ps, tn, tmc, out_dtype, prologue=None, g_streams=1):
    m, k = a.shape
    width = g.shape[1]
    n = width // p
    nj, q = _column_tiles(p, n, ps, tn)
    nc = m // tmc
    ds = g_streams

    def body(a_ref, g_ref, o_ref, acc_ref):
        c = pl.program_id(2)
        av = a_ref[...] if prologue is None else prologue(a_ref[...])
        gv = g_ref[...]
        if ds > 1:
            av, gv = _to_streams(av, ds), gv.reshape(tmc, ps * tn)
        part = lax.dot_general(av, gv, (((0,), (0,)), ((), ())), preferred_element_type=F32)

        @pl.when(c == 0)
        def _():
            acc_ref[...] = part

        @pl.when(c > 0)
        def _():
            acc_ref[...] += part

        @pl.when(c == nc - 1)
        def _():
            for s in range(ps):
                o_ref[s] = acc_ref[:, s * tn : (s + 1) * tn].astype(o_ref.dtype)

    return pl.pallas_call(
        body,
        name=name,
        grid=(k // tk, nj, nc),
        in_specs=[
            pl.BlockSpec((tmc, tk), lambda i, j, c: (c, i)),
            pl.BlockSpec((ds, tmc // ds, ps * tn), lambda i, j, c: (0, c, j))
            if ds > 1
            else pl.BlockSpec((tmc, ps * tn), lambda i, j, c: (c, j)),
        ],
        out_specs=pl.BlockSpec((ps, tk, tn), lambda i, j, c: (j // q, i, j % q)),
        out_shape=jax.ShapeDtypeStruct((p, k, n), out_dtype),
        scratch_shapes=[pltpu.VMEM((tk, ps * tn), F32)],
        compiler_params=_params(("parallel", "parallel", "arbitrary")),
    )(a, g.reshape(ds, m // ds, width) if ds > 1 else g)


def _rows(cols):
    return pl.BlockSpec((ROW_TILE, cols), lambda i: (i, 0))


def _vec(cols, rows=1):
    return pl.BlockSpec((rows, cols), lambda i: (0, 0))


def _layer_norm_hat(z):
    mu = jnp.mean(z, axis=-1, keepdims=True)
    zc = z - mu
    var = jnp.mean(zc * zc, axis=-1, keepdims=True)
    rstd = lax.rsqrt(var + LN_EPS)
    return zc * rstd, rstd


def modulate(x, scale, shift, *, name):
    s, d = x.shape

    def body(x_ref, sc_ref, sh_ref, h_ref):
        h_ref[...] = (x_ref[...] * (1.0 + sc_ref[...]) + sh_ref[...]).astype(BF16)

    return pl.pallas_call(
        body,
        name=name,
        grid=(s // ROW_TILE,),
        in_specs=[_rows(d), _vec(d), _vec(d)],
        out_specs=_rows(d),
        out_shape=jax.ShapeDtypeStruct((s, d), BF16),
        compiler_params=_params(("parallel",)),
    )(x, scale, shift)


def residual_ln_epilogue(acc, x, gate1, g, b, nscale, nshift):
    y = acc.astype(BF16)
    xhat, _ = _layer_norm_hat(ALPHA * x + gate1 * y.astype(F32))
    xn = xhat * g + b
    return y, xn, xn * (1.0 + nscale) + nshift


ST_DSCALE, ST_DSHIFT, ST_DG, ST_DB, ST_DGATE = 0, 1, 2, 3, 4


def _layer_norm_bwd(g_out, xhat, rstd, g):
    dxh = g_out * g
    m1 = jnp.mean(dxh, axis=-1, keepdims=True)
    m2 = jnp.mean(dxh * xhat, axis=-1, keepdims=True)
    return rstd * (dxh - m1 - xhat * m2)


def _column_sums(vals):
    return [jnp.sum(v, axis=0, keepdims=True) for v in vals]


def residual_bwd_epilogue(dh, dzl, scl, x, y, gate1, g, b):
    yf = y.astype(F32)
    xhat, rstd = _layer_norm_hat(ALPHA * x + gate1 * yf)
    g_out = ALPHA * dzl + dh * (1.0 + scl)
    dz = _layer_norm_bwd(g_out, xhat, rstd, g)
    return dz, dz * gate1, _column_sums([dh * (xhat * g + b), dh, g_out * xhat, g_out, dz * yf])


def input_bwd_epilogue(dh, dzl, scl, x):
    return ALPHA * dzl + dh * (1.0 + scl), _column_sums([dh * x, dh])


def residual_ln_loss_bwd(x, y, gate1, g, b, target, *, name):
    s, d = x.shape

    def body(x_ref, y_ref, gt_ref, g_ref, b_ref, t_ref, dz_ref, dy_ref, st_ref, loss_ref):
        y = y_ref[...].astype(F32)
        gate1 = gt_ref[...]
        xhat, rstd = _layer_norm_hat(ALPHA * x_ref[...] + gate1 * y)
        err = xhat * g_ref[...] + b_ref[...] - t_ref[...]
        g_out = err * (1.0 / d)
        dz = _layer_norm_bwd(g_out, xhat, rstd, g_ref[...])
        dz_ref[...] = dz
        dy_ref[...] = (dz * gate1).astype(BF16)
        part = jnp.sum(jnp.sum(err * err, axis=1, keepdims=True), axis=0, keepdims=True) * (0.5 / d)

        @pl.when(pl.program_id(0) == 0)
        def _():
            st_ref[...] = jnp.zeros_like(st_ref)
            loss_ref[...] = jnp.zeros_like(loss_ref)

        loss_ref[...] += part
        for row, val in zip((ST_DG, ST_DB, ST_DGATE), _column_sums([g_out * xhat, g_out, dz * y]), strict=True):
            st_ref[row : row + 1, :] += val

    return pl.pallas_call(
        body,
        name=name,
        grid=(s // ROW_TILE,),
        in_specs=[_rows(d), _rows(d), _vec(d), _vec(d), _vec(d), _rows(d)],
        out_specs=[_rows(d), _rows(d), _vec(d, SUBLANES), pl.BlockSpec((1, 1), lambda i: (0, 0))],
        out_shape=[
            jax.ShapeDtypeStruct((s, d), F32),
            jax.ShapeDtypeStruct((s, d), BF16),
            jax.ShapeDtypeStruct((SUBLANES, d), F32),
            jax.ShapeDtypeStruct((1, 1), F32),
        ],
        compiler_params=_params(("arbitrary",)),
    )(x, y, gate1, g, b, target)


GATE_CHUNKS = 4


def _gelu(x, with_grad=False):
    x2 = x * x
    t = jnp.tanh(GELU_C * (x + GELU_A * x2 * x))
    half = 0.5 * (1.0 + t)
    y = x * half
    if not with_grad:
        return y
    return y, half + 0.5 * x * (1.0 - t * t) * (GELU_C * (1.0 + 3.0 * GELU_A * x2))


def _causal_weights(w_ref, transpose):
    t = lax.broadcasted_iota(jnp.int32, (CHUNK, CHUNK), 0)
    s = lax.broadcasted_iota(jnp.int32, (CHUNK, CHUNK), 1)
    out = []
    for g in range(A_GROUPS):
        w = jnp.where(t >= s, w_ref[g], 0.0)
        out.append((w.T if transpose else w).astype(BF16))
    return out


def _spatial(ws, vn, lo_mask):
    rows = vn.shape[0]
    out_rows = []
    for r in range(rows // CHUNK):
        cols = []
        for j in range(A_GROUPS // 2):
            blk = vn[r * CHUNK : (r + 1) * CHUNK, j * LANES : (j + 1) * LANES]
            za = jnp.dot(ws[2 * j], blk, preferred_element_type=F32)
            zb = jnp.dot(ws[2 * j + 1], blk, preferred_element_type=F32)
            cols.append(jnp.where(lo_mask, za, zb))
        out_rows.append(jnp.concatenate(cols, axis=1))
    return jnp.concatenate(out_rows, axis=0)


def _gate_forward(a, vg, vb, ws, bias, lo_mask, with_grad=False):
    u = _gelu(a[:, :D_MODEL], with_grad)
    v = _gelu(a[:, D_MODEL:], with_grad)
    gu, gv = None, None
    if with_grad:
        (u, gu), (v, gv) = u, v
    vhat, rstd = _layer_norm_hat(v)
    vn = (vhat * vg + vb).astype(BF16)
    z = _spatial(ws, vn, lo_mask) + jnp.concatenate([bias] * (a.shape[0] // CHUNK), axis=0)
    return u, vhat, rstd, vn, z, gu, gv


def gate_fwd(a_pre, vn_g, vn_b, w_s, bias_full, *, name):
    s = a_pre.shape[0]
    tr = GATE_CHUNKS * CHUNK

    def body(a_ref, vg_ref, vb_ref, w_ref, bias_ref, p_ref):
        lo_mask = _lane((CHUNK, LANES)) < A_GROUP_DIM
        ws = _causal_weights(w_ref, transpose=False)
        u, _, _, _, z, _, _ = _gate_forward(a_ref[...].astype(F32), vg_ref[...], vb_ref[...], ws, bias_ref[...], lo_mask)
        p_ref[...] = (u * z).astype(BF16)

    return pl.pallas_call(
        body,
        name=name,
        grid=(s // tr,),
        in_specs=[
            pl.BlockSpec((tr, 2 * D_MODEL), lambda i: (i, 0)),
            _vec(D_MODEL),
            _vec(D_MODEL),
            pl.BlockSpec((A_GROUPS, CHUNK, CHUNK), lambda i: (0, 0, 0)),
            _vec(D_MODEL, CHUNK),
        ],
        out_specs=pl.BlockSpec((tr, D_MODEL), lambda i: (i, 0)),
        out_shape=jax.ShapeDtypeStruct((s, D_MODEL), BF16),
        compiler_params=_params(("parallel",)),
    )(a_pre, vn_g, vn_b, w_s, bias_full)


def gate_bwd(a_pre, dp, vn_g, vn_b, w_s, bias_full, *, name):
    s = a_pre.shape[0]
    tr = GATE_CHUNKS * CHUNK
    nsteps = s // tr

    def body(a_ref, dp_ref, vg_ref, vb_ref, w_ref, bias_ref, da_ref, dw_ref, dbs_ref, rows_ref, dbias_acc):
        step = pl.program_id(0)
        lo_mask = _lane((CHUNK, LANES)) < A_GROUP_DIM

        @pl.when(step == 0)
        def _():
            dw_ref[...] = jnp.zeros_like(dw_ref)
            rows_ref[...] = jnp.zeros_like(rows_ref)
            dbias_acc[...] = jnp.zeros_like(dbias_acc)

        a = a_ref[...].astype(F32)
        vg = vg_ref[...]
        ws = _causal_weights(w_ref, transpose=False)
        wts = _causal_weights(w_ref, transpose=True)
        u, vhat, rstd, vn, z, gelu_du, gelu_dv = _gate_forward(a, vg, vb_ref[...], ws, bias_ref[...], lo_mask, True)
        dp = dp_ref[...]
        du = dp * z
        dzz = dp * u
        dzz_b = dzz.astype(BF16)
        dvn = _spatial(wts, dzz_b, lo_mask)
        dbias = None
        for r in range(GATE_CHUNKS):
            rs = slice(r * CHUNK, (r + 1) * CHUNK)
            dbias = dzz[rs] if dbias is None else dbias + dzz[rs]
            for j in range(A_GROUPS // 2):
                cs = slice(j * LANES, (j + 1) * LANES)
                dblk = dzz[rs, cs]
                vblk = vn[rs, cs]
                for half in range(2):
                    keep = lo_mask if half == 0 else jnp.logical_not(lo_mask)
                    dm = jnp.where(keep, dblk, 0.0).astype(BF16)
                    dw_ref[2 * j + half] += lax.dot_general(
                        dm, vblk, (((1,), (1,)), ((), ())), preferred_element_type=F32
                    )
        dbias_acc[...] += dbias
        rows_ref[1:2, :D_MODEL] += jnp.sum(dvn * vhat, axis=0, keepdims=True)
        rows_ref[1:2, D_MODEL:] += jnp.sum(dvn, axis=0, keepdims=True)
        dvh = dvn * vg
        m1 = jnp.mean(dvh, axis=-1, keepdims=True)
        m2 = jnp.mean(dvh * vhat, axis=-1, keepdims=True)
        dv = rstd * (dvh - m1 - vhat * m2)
        da_u = du * gelu_du
        da_v = dv * gelu_dv
        da_ref[:, :D_MODEL] = da_u.astype(BF16)
        da_ref[:, D_MODEL:] = da_v.astype(BF16)
        rows_ref[0:1, :D_MODEL] += jnp.sum(da_u, axis=0, keepdims=True)
        rows_ref[0:1, D_MODEL:] += jnp.sum(da_v, axis=0, keepdims=True)

        @pl.when(step == nsteps - 1)
        def _():
            t = lax.broadcasted_iota(jnp.int32, (CHUNK, CHUNK), 0)
            sx = lax.broadcasted_iota(jnp.int32, (CHUNK, CHUNK), 1)
            for g in range(A_GROUPS):
                dw_ref[g] = jnp.where(t >= sx, dw_ref[g], 0.0)
            dbs_ref[...] = _reduce_groups(dbias_acc[...])

    return pl.pallas_call(
        body,
        name=name,
        grid=(nsteps,),
        in_specs=[
            pl.BlockSpec((tr, 2 * D_MODEL), lambda i: (i, 0)),
            pl.BlockSpec((tr, D_MODEL), lambda i: (i, 0)),
            _vec(D_MODEL),
            _vec(D_MODEL),
            pl.BlockSpec((A_GROUPS, CHUNK, CHUNK), lambda i: (0, 0, 0)),
            _vec(D_MODEL, CHUNK),
        ],
        out_specs=[
            pl.BlockSpec((tr, 2 * D_MODEL), lambda i: (i, 0)),
            pl.BlockSpec((A_GROUPS, CHUNK, CHUNK), lambda i: (0, 0, 0)),
            _vec(LANES, CHUNK),
            _vec(2 * D_MODEL, SUBLANES),
        ],
        out_shape=[
            jax.ShapeDtypeStruct((s, 2 * D_MODEL), BF16),
            jax.ShapeDtypeStruct((A_GROUPS, CHUNK, CHUNK), F32),
            jax.ShapeDtypeStruct((CHUNK, LANES), F32),
            jax.ShapeDtypeStruct((SUBLANES, 2 * D_MODEL), F32),
        ],
        scratch_shapes=[pltpu.VMEM((CHUNK, D_MODEL), F32)],
        compiler_params=_params(("arbitrary",)),
    )(a_pre, dp, vn_g, vn_b, w_s, bias_full)


def alibi_tables(dilation):
    qi = jnp.arange(SPAN)[:, None]
    ki = jnp.arange(2 * SPAN)[None, :]
    diff = SPAN + qi - ki
    valid = (diff >= 0) & (diff <= SPAN)
    heads = jnp.arange(1, B_HEADS + 1, dtype=F32)
    slopes = jnp.exp2(-8.0 * heads / B_HEADS)
    bias = -slopes[:, None, None] * (dilation * diff).astype(F32)
    bias = jnp.where(valid[None], bias, NEG).reshape(B_HEADS // 2, 2 * SPAN, 2 * SPAN)
    return bias, bias.transpose(0, 2, 1)


def _pair_rows(x, halves):
    return jnp.concatenate([x * halves[0], x * halves[1]], axis=0)


def _pair_column(v, lane, j):
    pick = lambda h: jnp.sum(jnp.where(lane == h, v, 0.0), axis=1, keepdims=True)
    return jnp.concatenate([pick(2 * j), pick(2 * j + 1)], axis=0)


_NT = (((1,), (1,)), ((), ()))


def _qkv_specs(block_of):
    def spec(which, prev):
        def index(*grid):
            blk = block_of(*grid)
            return (jnp.maximum(blk - 1, 0) if prev else blk, which)

        return pl.BlockSpec((SPAN, D_MODEL), index)

    return [spec(0, False), spec(1, True), spec(1, False), spec(2, True), spec(2, False)]


def attn_fwd(qkv_p, pat, *, name, after=None):
    _, dilation = B_PATTERNS[pat]
    nb = SEQ // dilation // SPAN
    bias, _ = alibi_tables(dilation)
    order = [] if after is None else [after]

    def body(q_ref, kp_ref, kc_ref, vp_ref, vc_ref, bias_ref, *rest):
        o_ref, lse_ref = rest[len(order) :]
        n = pl.program_id(1)
        first_prev = jnp.logical_and(n == 0, _lane((2 * SPAN, 2 * SPAN)) < SPAN)
        lane = _lane((SPAN, LANES))
        lo_mask = lane < B_HEAD_DIM
        q = q_ref[...] * jnp.asarray(B_HEAD_DIM**-0.5, BF16)
        kk = jnp.concatenate([kp_ref[...], kc_ref[...]], axis=0)
        vv = jnp.concatenate([vp_ref[...], vc_ref[...]], axis=0)
        halves = (lo_mask.astype(BF16), jnp.logical_not(lo_mask).astype(BF16))
        stats = jnp.zeros((SPAN, LANES), F32)
        for j in range(B_HEADS // 2):
            cs = slice(j * LANES, (j + 1) * LANES)
            sc = lax.dot_general(_pair_rows(q[:, cs], halves), kk[:, cs], _NT, preferred_element_type=F32)
            sc = jnp.where(first_prev, NEG, sc + bias_ref[j])
            m = jnp.max(sc, axis=1, keepdims=True)
            p = jnp.exp(sc - m)
            l = jnp.sum(p, axis=1, keepdims=True)
            acc = jnp.dot(p.astype(BF16), vv[:, cs], preferred_element_type=F32) * (1.0 / l)
            lse_pair = m + jnp.log(l)
            o_ref[:, cs] = jnp.where(lo_mask, acc[:SPAN], acc[SPAN:]).astype(BF16)
            stats = jnp.where(lane == 2 * j, lse_pair[:SPAN], stats)
            stats = jnp.where(lane == 2 * j + 1, lse_pair[SPAN:], stats)
        lse_ref[...] = stats

    return pl.pallas_call(
        body,
        name=name,
        grid=(dilation, nb),
        in_specs=[
            *_qkv_specs(lambda r, n: r * nb + n),
            pl.BlockSpec((B_HEADS // 2, 2 * SPAN, 2 * SPAN), lambda r, n: (0, 0, 0)),
            *[ANY for _ in order],
        ],
        out_specs=[
            pl.BlockSpec((SPAN, D_MODEL), lambda r, n: (r * nb + n, 0)),
            pl.BlockSpec((SPAN, LANES), lambda r, n: (r * nb + n, 0)),
        ],
        out_shape=[jax.ShapeDtypeStruct((SEQ, D_MODEL), BF16), jax.ShapeDtypeStruct((SEQ, LANES), F32)],
        compiler_params=_params(("parallel", "arbitrary")),
    )(qkv_p, qkv_p, qkv_p, qkv_p, qkv_p, bias, *order)


def attn_combine(outs, lses, *, name):
    dils = [d for _, d in B_PATTERNS]

    def positions(ref, d):
        x = ref[...].astype(F32)
        return x if d == 1 else jnp.swapaxes(x, 0, 1).reshape(ROW_TILE, x.shape[-1])

    def body(o0, o1, o2, l0, l1, l2, ob_ref, of_ref, *lse_refs):
        ls = [positions(l, d) for l, d in zip((l0, l1, l2), dils, strict=True)]
        m = jnp.maximum(jnp.maximum(ls[0], ls[1]), ls[2])
        tot = jnp.log(jnp.exp(ls[0] - m) + jnp.exp(ls[1] - m) + jnp.exp(ls[2] - m)) + m
        o = None
        for o_ref, l, d in zip((o0, o1, o2), ls, dils, strict=True):
            term = _expand_groups(jnp.exp(l - tot)) * positions(o_ref, d)
            o = term if o is None else o + term
        ob_ref[...] = o.astype(BF16)
        of_ref[...] = o
        for lse_ref, d in zip(lse_refs, dils, strict=True):
            lse_ref[...] = tot if d == 1 else _to_streams(tot, d).reshape(lse_ref.shape)

    def stream_rows(cols, d):
        return _rows(cols) if d == 1 else pl.BlockSpec((d, ROW_TILE // d, cols), lambda i: (0, i, 0))

    def streams(x, d):
        return x if d == 1 else x.reshape(d, SEQ // d, x.shape[-1])

    outs = [streams(x, d) for x, d in zip(outs, dils, strict=True)]
    lses = [streams(x, d) for x, d in zip(lses, dils, strict=True)]
    return pl.pallas_call(
        body,
        name=name,
        grid=(SEQ // ROW_TILE,),
        in_specs=[stream_rows(D_MODEL, d) for d in dils] + [stream_rows(LANES, d) for d in dils],
        out_specs=[_rows(D_MODEL), _rows(D_MODEL)] + [stream_rows(LANES, d) for d in dils],
        out_shape=[jax.ShapeDtypeStruct((SEQ, D_MODEL), BF16), jax.ShapeDtypeStruct((SEQ, D_MODEL), F32)]
        + [jax.ShapeDtypeStruct(lse.shape, F32) for lse in lses],
        compiler_params=_params(("parallel",)),
    )(*outs, *lses)


def attn_bwd(qkv_p, do_p, lse_p, delta_p, pat, *, name):
    _, dilation = B_PATTERNS[pat]
    nb = SEQ // dilation // SPAN
    n_blocks = SEQ // SPAN
    bias, bias_t = alibi_tables(dilation)
    last = n_blocks - 1
    q_cols, k_cols, v_cols = (slice(i * D_MODEL, (i + 1) * D_MODEL) for i in range(3))

    def body(q_ref, kp_ref, kc_ref, vp_ref, vc_ref, do_ref, lse_ref, dl_ref, bias_ref, biast_ref, out_ref, cq_ref, ck_ref, cv_ref):
        g = pl.program_id(0)

        @pl.when(g == n_blocks)
        def _():
            out_ref[:, q_cols] = cq_ref[...].astype(BF16)
            out_ref[:, k_cols] = ck_ref[...].astype(BF16)
            out_ref[:, v_cols] = cv_ref[...].astype(BF16)

        @pl.when(g == 0)
        def _():
            cq_ref[...] = jnp.zeros_like(cq_ref)
            ck_ref[...] = jnp.zeros_like(ck_ref)
            cv_ref[...] = jnp.zeros_like(cv_ref)

        @pl.when(g < n_blocks)
        def _():
            lane = _lane((SPAN, LANES))
            lo_mask = lane < B_HEAD_DIM
            pair = (2 * SPAN, 2 * SPAN)
            first = lax.rem(g, nb) == 0
            prev_key_cols = jnp.logical_and(first, _lane(pair) < SPAN)
            prev_key_rows = jnp.logical_and(first, lax.broadcasted_iota(jnp.int32, pair, 0) < SPAN)
            q = q_ref[...] * jnp.asarray(B_HEAD_DIM**-0.5, BF16)
            kk = jnp.concatenate([kp_ref[...], kc_ref[...]], axis=0)
            vv = jnp.concatenate([vp_ref[...], vc_ref[...]], axis=0)
            do_v = do_ref[...]
            lse_v = lse_ref[...]
            dl_v = dl_ref[...]
            lse_t = lse_v.T
            dl_t = dl_v.T
            halves = (lo_mask.astype(BF16), jnp.logical_not(lo_mask).astype(BF16))
            for j in range(B_HEADS // 2):
                cs = slice(j * LANES, (j + 1) * LANES)
                kp, vp = kk[:, cs], vv[:, cs]
                q2 = _pair_rows(q[:, cs], halves)
                do2 = _pair_rows(do_v[:, cs], halves)
                lse_c, dl_c = _pair_column(lse_v, lane, j), _pair_column(dl_v, lane, j)
                lse_r = jnp.concatenate([lse_t[2 * j : 2 * j + 1], lse_t[2 * j + 1 : 2 * j + 2]], axis=1)
                dl_r = jnp.concatenate([dl_t[2 * j : 2 * j + 1], dl_t[2 * j + 1 : 2 * j + 2]], axis=1)
                sc = lax.dot_general(q2, kp, _NT, preferred_element_type=F32)
                p = jnp.exp(jnp.where(prev_key_cols, NEG, sc + bias_ref[j]) - lse_c)
                dp = lax.dot_general(do2, vp, _NT, preferred_element_type=F32)
                ds = (p * (dp - dl_c)).astype(BF16)
                dq2 = jnp.dot(ds, kp, preferred_element_type=F32)
                sc_t = lax.dot_general(kp, q2, _NT, preferred_element_type=F32)
                p_t = jnp.exp(jnp.where(prev_key_rows, NEG, sc_t + biast_ref[j]) - lse_r)
                dp_t = lax.dot_general(vp, do2, _NT, preferred_element_type=F32)
                ds_t = (p_t * (dp_t - dl_r)).astype(BF16)
                dk_pair = jnp.dot(ds_t, q2, preferred_element_type=F32)
                dv_pair = jnp.dot(p_t.astype(BF16), do2, preferred_element_type=F32)
                oq = slice(j * LANES, (j + 1) * LANES)
                ok = slice(D_MODEL + j * LANES, D_MODEL + (j + 1) * LANES)
                ov = slice(2 * D_MODEL + j * LANES, 2 * D_MODEL + (j + 1) * LANES)
                out_ref[:, oq] = cq_ref[:, cs].astype(BF16)
                out_ref[:, ok] = (ck_ref[:, cs] + dk_pair[:SPAN]).astype(BF16)
                out_ref[:, ov] = (cv_ref[:, cs] + dv_pair[:SPAN]).astype(BF16)
                cq_ref[:, cs] = jnp.where(lo_mask, dq2[:SPAN], dq2[SPAN:]) * (B_HEAD_DIM**-0.5)
                ck_ref[:, cs] = dk_pair[SPAN:]
                cv_ref[:, cs] = dv_pair[SPAN:]

    def block_of(g):
        return jnp.minimum(g, last)

    def row_spec(width):
        return pl.BlockSpec((SPAN, width), lambda g: (block_of(g), 0))

    return pl.pallas_call(
        body,
        name=name,
        grid=(n_blocks + 1,),
        in_specs=[
            *_qkv_specs(block_of),
            row_spec(D_MODEL),
            row_spec(LANES),
            row_spec(LANES),
            pl.BlockSpec((B_HEADS // 2, 2 * SPAN, 2 * SPAN), lambda g: (0, 0, 0)),
            pl.BlockSpec((B_HEADS // 2, 2 * SPAN, 2 * SPAN), lambda g: (0, 0, 0)),
        ],
        out_specs=pl.BlockSpec((SPAN, 3 * D_MODEL), lambda g: (jnp.maximum(g - 1, 0), 0)),
        out_shape=jax.ShapeDtypeStruct((SEQ, 3 * D_MODEL), BF16),
        scratch_shapes=[pltpu.VMEM((SPAN, D_MODEL), F32)] * 3,
        compiler_params=_params(("arbitrary",)),
    )(qkv_p, qkv_p, qkv_p, qkv_p, qkv_p, do_p, lse_p, delta_p, bias, bias_t)


def _position():
    x, y, c = lax.axis_index("x"), lax.axis_index("y"), lax.axis_index("c")
    return x, y, c, 4 * x + 2 * y + c


def _peer(k, x, y, c):
    px = 1 - x if k & 4 else x
    py = 1 - y if k & 2 else y
    pc = 1 - c if k & 1 else c
    return (px, py, pc), 4 * px + 2 * py + pc


def _remote(src, dst, send_sem, recv_sem, device):
    return pltpu.make_async_remote_copy(
        src_ref=src, dst_ref=dst, send_sem=send_sem, recv_sem=recv_sem, device_id=device, device_id_type=MESH
    )


def _silu_bf16(cf):
    return (cf * (1.0 / (1.0 + jnp.exp(-cf)))).astype(BF16)


def ada_exchange(c8, w4, b4, ln8):
    nt, _, ncol = w4.shape

    def body(c8_ref, w_ref, b_ref, ln_ref, cg_ref, lng_ref, mrecv_ref, mloc_ref, send_sems, recv_sems):
        x, y, c, me = _position()
        cg_ref[me] = c8_ref[...]
        lng_ref[me] = ln_ref[...]
        first = []
        for k in range(1, N_DEV):
            dev, _ = _peer(k, x, y, c)
            first.append(_remote(c8_ref, cg_ref.at[me], send_sems.at[0, k], recv_sems.at[0, k], dev))
            first.append(_remote(ln_ref, lng_ref.at[me], send_sems.at[1, k], recv_sems.at[1, k], dev))
        for cp in first:
            cp.start()
        for k in range(1, N_DEV):
            dev, pid = _peer(k, x, y, c)
            _remote(c8_ref, cg_ref.at[pid], send_sems.at[0, k], recv_sems.at[0, k], dev).wait_recv()
            _remote(ln_ref, lng_ref.at[pid], send_sems.at[1, k], recv_sems.at[1, k], dev).wait_recv()
        sc = _silu_bf16(cg_ref[...].reshape(N_DEV * SUBLANES, D_MODEL))
        for t in range(nt):
            mloc_ref[t] = jnp.dot(sc, w_ref[t].astype(BF16), preferred_element_type=F32) + b_ref[t : t + 1, :]

        def group(dev_id):
            return pl.ds(pl.multiple_of(dev_id * SUBLANES, SUBLANES), SUBLANES)

        mrecv_ref[me] = mloc_ref[:, group(me), :]
        second = []
        for k in range(1, N_DEV):
            dev, pid = _peer(k, x, y, c)
            second.append(
                _remote(mloc_ref.at[:, group(pid), :], mrecv_ref.at[me], send_sems.at[2, k], recv_sems.at[2, k], dev)
            )
        for cp in second:
            cp.start()
        for k in range(1, N_DEV):
            dev, pid = _peer(k, x, y, c)
            _remote(
                mloc_ref.at[:, group(pid), :], mrecv_ref.at[pid], send_sems.at[2, k], recv_sems.at[2, k], dev
            ).wait_recv()
        for cp in first + second:
            cp.wait_send()

    return pl.pallas_call(
        body,
        name="ada_exchange",
        in_specs=[VMEM, VMEM, VMEM, VMEM],
        out_specs=[VMEM, VMEM, VMEM],
        out_shape=[
            jax.ShapeDtypeStruct((N_DEV, SUBLANES, D_MODEL), F32),
            jax.ShapeDtypeStruct((N_DEV, SUBLANES, LANES), F32),
            jax.ShapeDtypeStruct((N_DEV, nt, SUBLANES, ncol), F32),
        ],
        scratch_shapes=[
            pltpu.VMEM((nt, N_DEV * SUBLANES, ncol), F32),
            pltpu.SemaphoreType.DMA((3, N_DEV)),
            pltpu.SemaphoreType.DMA((3, N_DEV)),
        ],
        compiler_params=pltpu.CompilerParams(vmem_limit_bytes=VMEM_LIMIT_BYTES),
    )(c8, w4, b4, ln8)


def small_exchange(dmx, flat, pre, after):
    rows = flat.shape[1]

    def body(dmx_ref, flat_ref, pre_ref, _, dmrecv_ref, red_ref, land_ref, send_sems, recv_sems):
        x, y, c, me = _position()
        dmrecv_ref[me] = dmx_ref[me]
        land_ref[me] = flat_ref[me]
        first = []
        for k in range(1, N_DEV):
            dev, pid = _peer(k, x, y, c)
            first.append(_remote(dmx_ref.at[pid], dmrecv_ref.at[me], send_sems.at[0, k], recv_sems.at[0, k], dev))
            first.append(_remote(flat_ref.at[pid], land_ref.at[me], send_sems.at[1, k], recv_sems.at[1, k], dev))
        for cp in first:
            cp.start()
        for k in range(1, N_DEV):
            dev, pid = _peer(k, x, y, c)
            _remote(dmx_ref.at[pid], dmrecv_ref.at[pid], send_sems.at[0, k], recv_sems.at[0, k], dev).wait_recv()
            _remote(flat_ref.at[pid], land_ref.at[pid], send_sems.at[1, k], recv_sems.at[1, k], dev).wait_recv()
        total, total_pre = land_ref[0], pre_ref[0]
        for s in range(1, N_DEV):
            total, total_pre = total + land_ref[s], total_pre + pre_ref[s]
        red_ref[me, :rows, :] = total
        red_ref[me, rows:, :] = total_pre
        second = []
        for k in range(1, N_DEV):
            dev, _ = _peer(k, x, y, c)
            second.append(_remote(red_ref.at[me], red_ref.at[me], send_sems.at[2, k], recv_sems.at[2, k], dev))
        for cp in second:
            cp.start()
        for k in range(1, N_DEV):
            dev, pid = _peer(k, x, y, c)
            _remote(red_ref.at[pid], red_ref.at[pid], send_sems.at[2, k], recv_sems.at[2, k], dev).wait_recv()
        for cp in first + second:
            cp.wait_send()

    return pl.pallas_call(
        body,
        name="small_exchange",
        in_specs=[VMEM, VMEM, VMEM, ANY],
        out_specs=[VMEM, VMEM],
        out_shape=[
            jax.ShapeDtypeStruct(dmx.shape, F32),
            jax.ShapeDtypeStruct((N_DEV, rows + pre.shape[1], LANES), F32),
        ],
        scratch_shapes=[
            pltpu.VMEM(flat.shape, F32),
            pltpu.SemaphoreType.DMA((3, N_DEV)),
            pltpu.SemaphoreType.DMA((3, N_DEV)),
        ],
        compiler_params=pltpu.CompilerParams(vmem_limit_bytes=VMEM_LIMIT_BYTES),
    )(dmx, flat, pre, after)


HBM = pl.BlockSpec(memory_space=pltpu.HBM)
SEM = pl.BlockSpec(memory_space=pltpu.SEMAPHORE)
EFFECT = pltpu.SideEffectType.DATAFLOW_SIDE_EFFECTING


REGROUP_ROWS = 256


def shards_to_columns(x, *, name):
    p, k, n = x.shape

    def body(x_ref, o_ref):
        for s in range(p):
            o_ref[:, s * n : (s + 1) * n] = x_ref[s]

    return pl.pallas_call(
        body,
        name=name,
        grid=(k // REGROUP_ROWS,),
        in_specs=[pl.BlockSpec((p, REGROUP_ROWS, n), lambda i: (0, i, 0))],
        out_specs=pl.BlockSpec((REGROUP_ROWS, p * n), lambda i: (i, 0)),
        out_shape=jax.ShapeDtypeStruct((k, p * n), x.dtype),
        compiler_params=_params(("parallel",)),
    )(x)


def columns_to_shards(xs, *, name):
    k = xs[0].shape[0]
    widths = [x.shape[1] for x in xs]
    n = sum(widths) // N_DEV
    pieces = []
    for s in range(N_DEV):
        start = 0
        for i, w in enumerate(widths):
            lo, hi = max(start, s * n), min(start + w, (s + 1) * n)
            if lo < hi:
                pieces.append((i, lo - start, s, lo - s * n, hi - lo))
            start += w

    def body(*refs):
        x_refs, o_ref = refs[: len(xs)], refs[-1]
        for i, c0, s, d0, w in pieces:
            o_ref[s, :, d0 : d0 + w] = x_refs[i][:, c0 : c0 + w]

    return pl.pallas_call(
        body,
        name=name,
        grid=(k // REGROUP_ROWS,),
        in_specs=[pl.BlockSpec((REGROUP_ROWS, w), lambda i: (i, 0)) for w in widths],
        out_specs=pl.BlockSpec((N_DEV, REGROUP_ROWS, n), lambda i: (0, i, 0)),
        out_shape=jax.ShapeDtypeStruct((N_DEV, k, n), xs[0].dtype),
        compiler_params=_params(("parallel",)),
    )(*xs)


def _own_slot(me, block):
    land = lax.empty((N_DEV, *block.shape), block.dtype)
    return lax.dynamic_update_slice_in_dim(land, block[None], me, axis=0)


N_CHIP_PEERS = 3


class Gather:
    def __init__(self, shards, lands, after, *, name):
        nt = len(shards)
        self.name = name

        def body(*refs):
            src_refs, land_refs = refs[:nt], refs[nt : 2 * nt]
            send_sems, recv_sems = refs[2 * nt + 1 : 3 * nt + 1], refs[3 * nt + 1 : 4 * nt + 1]
            token = refs[-1]
            x, y, c, me = _position()
            for t in range(nt):
                for k, dev in enumerate(self._targets(x, y, c)):
                    _remote(src_refs[t], land_refs[t].at[me], send_sems[t].at[k], recv_sems[t].at[k], dev).start()
            token[...] = jnp.zeros_like(token)

        outs = pl.pallas_call(
            body,
            name=name + "_start",
            in_specs=[HBM] * (2 * nt) + [ANY],
            out_specs=[SEM] * (2 * nt) + [HBM] * (2 * nt) + [VMEM],
            out_shape=[pltpu.SemaphoreType.DMA((1 + N_CHIP_PEERS,))] * (2 * nt)
            + [pltpu.HBM(a.shape, a.dtype) for a in (*shards, *lands)]
            + [jax.ShapeDtypeStruct((SUBLANES, LANES), F32)],
            input_output_aliases={i: 2 * nt + i for i in range(2 * nt)},
            compiler_params=pltpu.CompilerParams(has_side_effects=EFFECT),
        )(*[pltpu.with_memory_space_constraint(a, pltpu.HBM) for a in (*shards, *lands)], after)
        self.send_sems, self.recv_sems = list(outs[:nt]), list(outs[nt : 2 * nt])
        self.srcs, self.lands = list(outs[2 * nt : 3 * nt]), list(outs[3 * nt : 4 * nt])
        self.token = outs[-1]

    @staticmethod
    def _chips(x, y):
        return [(1 - x, y), (x, 1 - y), (1 - x, 1 - y)]

    @classmethod
    def _targets(cls, x, y, c):
        return [(x, y, 1 - c)] + [(*chip, c) for chip in cls._chips(x, y)]

    def zero(self):
        return self.token[0, 0]

    @staticmethod
    def _slot(px, py, pc):
        return 4 * px + 2 * py + pc

    def pass_on(self, which, after, *, name):
        n = len(which)

        def pass_body(*refs):
            land_refs, recv_sems = refs[:n], refs[n : 2 * n]
            fwd_send, fwd_recv = refs[3 * n + 1 : 4 * n + 1], refs[4 * n + 1 : 5 * n + 1]
            token = refs[-1]
            x, y, c, _ = _position()
            for t in range(n):
                for j, chip in enumerate(self._chips(x, y)):
                    blk = land_refs[t].at[self._slot(*chip, c)]
                    _remote(blk, blk, fwd_send[t].at[j], recv_sems[t].at[1 + j], (*chip, c)).wait_recv()
                    _remote(blk, blk, fwd_send[t].at[j], fwd_recv[t].at[j], (x, y, 1 - c)).start()
            token[...] = jnp.zeros_like(token)

        lands = [self.lands[t] for t in which]
        outs = pl.pallas_call(
            pass_body,
            name=name,
            in_specs=[HBM] * n + [SEM] * n + [ANY],
            out_specs=[HBM] * n + [SEM] * (2 * n) + [VMEM],
            out_shape=[pltpu.HBM(a.shape, a.dtype) for a in lands]
            + [pltpu.SemaphoreType.DMA((N_CHIP_PEERS,))] * (2 * n)
            + [jax.ShapeDtypeStruct((SUBLANES, LANES), F32)],
            input_output_aliases={i: i for i in range(n)},
            compiler_params=pltpu.CompilerParams(has_side_effects=EFFECT),
        )(*lands, *[self.recv_sems[t] for t in which], after)
        return (which, outs[:n], outs[n : 2 * n], outs[2 * n : 3 * n]), outs[-1]

    def wait(self, which, after, *, name):
        return self.finish(self.pass_on(which, after, name=name + "_pass")[0], after, name=name)

    def finish(self, passed, after, *, name):
        which, lands, fwd_send, fwd_recv = passed
        n = len(which)
        slot = self._slot

        def wait_body(*refs):
            src_refs, land_refs = refs[:n], refs[n : 2 * n]
            send_sems, recv_sems = refs[2 * n : 3 * n], refs[3 * n : 4 * n]
            fwd_send, fwd_recv = refs[4 * n : 5 * n], refs[5 * n : 6 * n]
            x, y, c, me = _position()
            sibling = (x, y, 1 - c)
            for t in range(n):
                for k, dev in enumerate(self._targets(x, y, c)):
                    _remote(src_refs[t], land_refs[t].at[me], send_sems[t].at[k], recv_sems[t].at[k], dev).wait_send()
                blk = land_refs[t].at[slot(x, y, 1 - c)]
                _remote(blk, blk, send_sems[t].at[0], recv_sems[t].at[0], sibling).wait_recv()
                for j, chip in enumerate(self._chips(x, y)):
                    sent = land_refs[t].at[slot(*chip, c)]
                    _remote(sent, sent, fwd_send[t].at[j], fwd_recv[t].at[j], sibling).wait_send()
                    got = land_refs[t].at[slot(*chip, 1 - c)]
                    _remote(got, got, fwd_send[t].at[j], fwd_recv[t].at[j], sibling).wait_recv()

        srcs = [self.srcs[t] for t in which]
        outs = pl.pallas_call(
            wait_body,
            name=name,
            in_specs=[HBM] * (2 * n) + [SEM] * (4 * n) + [ANY],
            out_specs=[HBM] * (2 * n),
            out_shape=[pltpu.HBM(a.shape, a.dtype) for a in (*srcs, *lands)],
            input_output_aliases={i: i for i in range(2 * n)},
            compiler_params=pltpu.CompilerParams(has_side_effects=EFFECT),
        )(
            *srcs, *lands, *[self.send_sems[t] for t in which], *[self.recv_sems[t] for t in which], *fwd_send,
            *fwd_recv, after,
        )
        return outs[n:]


class Scatter:
    def __init__(self, srcs, after, *, name):
        self.name = name
        nt = self.nt = len(srcs)
        peers = N_DEV - 1
        lands = [lax.empty(a.shape, a.dtype) for a in srcs]

        def body(*refs):
            src_refs, land_refs = refs[:nt], refs[nt : 2 * nt]
            send_sems, recv_sems = refs[2 * nt + 1 : 3 * nt + 1], refs[3 * nt + 1 : 4 * nt + 1]
            token = refs[-1]
            x, y, c, me = _position()
            for t in range(nt):
                for k in range(1, N_DEV):
                    dev, pid = _peer(k, x, y, c)
                    src = src_refs[t].at[pid]
                    _remote(src, land_refs[t].at[me], send_sems[t].at[k - 1], recv_sems[t].at[k - 1], dev).start()
                self._own(src_refs[t], land_refs[t], send_sems[t], me).start()
            token[...] = jnp.zeros_like(token)

        outs = pl.pallas_call(
            body,
            name=name + "_start",
            in_specs=[HBM] * (2 * nt) + [ANY],
            out_specs=[SEM] * (2 * nt) + [HBM] * (2 * nt) + [VMEM],
            out_shape=[pltpu.SemaphoreType.DMA((peers + 1,))] * (2 * nt)
            + [pltpu.HBM(a.shape, a.dtype) for a in (*srcs, *lands)]
            + [jax.ShapeDtypeStruct((SUBLANES, LANES), F32)],
            input_output_aliases={i: 2 * nt + i for i in range(2 * nt)},
            compiler_params=pltpu.CompilerParams(has_side_effects=EFFECT),
        )(*[pltpu.with_memory_space_constraint(a, pltpu.HBM) for a in (*srcs, *lands)], after)
        self.send_sems, self.recv_sems = outs[:nt], outs[nt : 2 * nt]
        self.srcs, self.lands = outs[2 * nt : 3 * nt], outs[3 * nt : 4 * nt]
        self.token = outs[-1]

    @staticmethod
    def _own(src_ref, land_ref, sems, me):
        return pltpu.make_async_copy(src_ref.at[me], land_ref.at[me], sems.at[N_DEV - 1])

    def zero(self):
        return self.token[0, 0]

    def wait(self, which, after, *, name):
        n = len(which)

        def body(*refs):
            src_refs, land_refs = refs[:n], refs[n : 2 * n]
            send_sems, recv_sems = refs[2 * n : 3 * n], refs[3 * n : 4 * n]
            x, y, c, me = _position()
            for t in range(n):
                for k in range(1, N_DEV):
                    dev, pid = _peer(k, x, y, c)
                    src = src_refs[t].at[pid]
                    cp = _remote(src, land_refs[t].at[pid], send_sems[t].at[k - 1], recv_sems[t].at[k - 1], dev)
                    cp.wait_send()
                    cp.wait_recv()
                self._own(src_refs[t], land_refs[t], send_sems[t], me).wait()

        srcs = [self.srcs[t] for t in which]
        lands = [self.lands[t] for t in which]
        outs = pl.pallas_call(
            body,
            name=name,
            in_specs=[HBM] * (2 * n) + [SEM] * (2 * n) + [ANY],
            out_specs=[HBM] * (2 * n),
            out_shape=[pltpu.HBM(a.shape, a.dtype) for a in (*srcs, *lands)],
            input_output_aliases={i: i for i in range(2 * n)},
            compiler_params=pltpu.CompilerParams(has_side_effects=EFFECT),
        )(*srcs, *lands, *[self.send_sems[t] for t in which], *[self.recv_sems[t] for t in which], after)
        return outs[n:]


def _adam_update(g, w, m, v):
    m2 = ADAM_B1 * m + (1.0 - ADAM_B1) * g
    v2 = ADAM_B2 * v + (1.0 - ADAM_B2) * jnp.square(g)
    m_hat = m2 / (1.0 - ADAM_B1**ADAM_STEP)
    v_hat = v2 / (1.0 - ADAM_B2**ADAM_STEP)
    delta = -ADAM_LR * (m_hat / (jnp.sqrt(v_hat) + ADAM_EPS) + ADAM_WD * w)
    return delta, m2, v2


def adamw(gparts, w, m, v, *, name):
    nl, r, c = w.shape
    p = gparts[0].shape[0]
    tr = r if r <= 256 else (256 if c <= D_MODEL else 128)
    ni = r // tr

    def body(*refs):
        g_refs = refs[:nl]
        w_ref, m_ref, v_ref, go_ref, d_ref, mo_ref, vo_ref = refs[nl:]
        for layer in range(nl):

            @pl.when(pl.program_id(0) == layer)
            def _(g_ref=g_refs[layer]):
                g = g_ref[0].astype(F32)
                for i in range(1, p):
                    g = g + g_ref[i].astype(F32)
                delta, m2, v2 = _adam_update(g, w_ref[...], m_ref[...], v_ref[...])
                go_ref[...] = g
                d_ref[...] = delta
                mo_ref[...] = m2
                vo_ref[...] = v2

    def parts_spec(layer):
        def index(l, i):
            return (0, jnp.where(l == layer, i, jnp.where(l < layer, 0, ni - 1)), 0)

        return pl.BlockSpec((p, tr, c), index)

    blk = pl.BlockSpec((None, tr, c), lambda l, i: (l, i, 0))
    return pl.pallas_call(
        body,
        name=name,
        grid=(nl, ni),
        in_specs=[*[parts_spec(layer) for layer in range(nl)], blk, blk, blk],
        out_specs=[blk] * 4,
        out_shape=[jax.ShapeDtypeStruct((nl, r, c), F32)] * 4,
        compiler_params=_params(("arbitrary", "arbitrary")),
    )(*gparts, w, m, v)


def adamw_small(items, *, name):
    n = len(items)

    def body(*refs):
        ins, outs = refs[: 4 * n], refs[4 * n :]
        for t in range(n):
            g_ref, w_ref, m_ref, v_ref = ins[4 * t : 4 * t + 4]
            g = g_ref[...]
            delta, m2, v2 = _adam_update(g, w_ref[...], m_ref[...], v_ref[...])
            for o_ref, val in zip(outs[4 * t : 4 * t + 4], (g, delta, m2, v2), strict=True):
                o_ref[...] = val

    outs = pl.pallas_call(
        body,
        name=name,
        out_shape=[jax.ShapeDtypeStruct(item[1].shape, F32) for item in items for _ in range(4)],
    )(*[a for item in items for a in item])
    return [outs[4 * t : 4 * t + 4] for t in range(n)]


def ada_grad_adamw(cg, dmrecv, w4, m4, v4, *, name):
    nt, k, ncol = w4.shape

    def body(cg_ref, dm_ref, w_ref, m_ref, v_ref, go_ref, d_ref, mo_ref, vo_ref, gb_ref):
        sc = _silu_bf16(cg_ref[...].reshape(N_DEV * SUBLANES, k))
        dm = dm_ref[...].reshape(N_DEV * SUBLANES, ncol)
        g = lax.dot_general(sc, dm.astype(BF16), (((0,), (0,)), ((), ())), preferred_element_type=F32)
        delta, m2, v2 = _adam_update(g, w_ref[...], m_ref[...], v_ref[...])
        go_ref[...] = g
        d_ref[...] = delta
        mo_ref[...] = m2
        vo_ref[...] = v2
        gb_ref[...] = jnp.broadcast_to(jnp.sum(dm, axis=0, keepdims=True), (SUBLANES, ncol))

    wblk = pl.BlockSpec((None, k, ncol), lambda t: (t, 0, 0))
    return pl.pallas_call(
        body,
        name=name,
        grid=(nt,),
        in_specs=[
            pl.BlockSpec((N_DEV, SUBLANES, k), lambda t: (0, 0, 0)),
            pl.BlockSpec((N_DEV, None, SUBLANES, ncol), lambda t: (0, t, 0, 0)),
            wblk,
            wblk,
            wblk,
        ],
        out_specs=[wblk] * 4 + [pl.BlockSpec((None, SUBLANES, ncol), lambda t: (t, 0, 0))],
        out_shape=[jax.ShapeDtypeStruct((nt, k, ncol), F32)] * 4 + [jax.ShapeDtypeStruct((nt, SUBLANES, ncol), F32)],
        compiler_params=_params(("parallel",)),
    )(cg, dmrecv, w4, m4, v4)


def kernel(x, c, ada_w, ada_b, ln_g, ln_b, a_w_in, a_b_in, a_vn_g, a_vn_b, a_w_s, a_b_s, a_w_out, b_w_qkv, b_w_out, mlp_w_up, mlp_w_down, loss_target, m_ada_w, m_ada_b, m_ln_g, m_ln_b, m_a_w_in, m_a_b_in, m_a_vn_g, m_a_vn_b, m_a_w_s, m_a_b_s, m_a_w_out, m_b_w_qkv, m_b_w_out, m_mlp_w_up, m_mlp_w_down, v_ada_w, v_ada_b, v_ln_g, v_ln_b, v_a_w_in, v_a_b_in, v_a_vn_g, v_a_vn_b, v_a_w_s, v_a_b_s, v_a_w_out, v_b_w_qkv, v_b_w_out, v_mlp_w_up, v_mlp_w_down):
    x0 = x[0]
    target = loss_target[0]
    me = 4 * lax.axis_index("x") + 2 * lax.axis_index("y") + lax.axis_index("c")

    ada_w4 = ada_w.reshape(N_SUB, D_MODEL, -1)
    ada_b4 = ada_b.reshape(N_SUB, -1)
    ln8 = jnp.concatenate([ln_g.reshape(N_SUB, -1), ln_b.reshape(N_SUB, -1)], axis=0)
    c8 = jnp.broadcast_to(c, (SUBLANES, D_MODEL))
    cg, lng, mrecv = ada_exchange(c8, ada_w4, ada_b4, ln8)

    W_IN, W_AOUT, W_UP0, W_DN0, W_QKV, W_BOUT, W_UP1, W_DN1 = range(8)
    shards = [
        a_w_in[0].astype(BF16),
        a_w_out[0].astype(BF16),
        mlp_w_up[0].astype(BF16),
        mlp_w_down[0].astype(BF16),
        b_w_qkv[0].astype(BF16),
        b_w_out[0].astype(BF16),
        mlp_w_up[1].astype(BF16),
        mlp_w_down[1].astype(BF16),
    ]
    gather = Gather(shards, [_own_slot(me, s) for s in shards], mrecv, name="gather")

    modv = mrecv[:, :, 0, :].transpose(1, 0, 2).reshape(N_SUB, 3 * D_MODEL) + gather.zero()
    shift = [modv[t : t + 1, :D_MODEL] for t in range(N_SUB)]
    scale = [modv[t : t + 1, D_MODEL : 2 * D_MODEL] for t in range(N_SUB)]
    gate1 = [1.0 + modv[t : t + 1, 2 * D_MODEL :] for t in range(N_SUB)]
    lng_full = [lng[:, t, :].reshape(1, D_MODEL) for t in range(N_SUB)]
    lnb_full = [lng[:, N_SUB + t, :].reshape(1, D_MODEL) for t in range(N_SUB)]

    ident = lambda acc: (acc,)
    def relu2(a):
        r = jnp.maximum(a, jnp.zeros_like(a))
        return r * r
    vn_g, vn_b, w_s = a_vn_g, a_vn_b, a_w_s[0]
    bias_full = jnp.repeat(a_b_s[0].T, A_GROUP_DIM, axis=1)
    w_up3, w_dn3 = [None, None], [None, None]

    def mlp_forward(i, h, up, dn, x_in=None, t_next=None):
        w_up3[i], w_dn3[i] = up, dn.reshape(1, D_FF, D_MODEL)
        (a,) = mm_nn(h, w_up3[i], name=f"mlp{i}_up", tm=2048, ps=2, tn=512, tk=D_MODEL, epilogue=ident, outs=(BF16,))
        if t_next is None:
            (y,) = mm_nn(
                a, w_dn3[i], name=f"mlp{i}_down", tm=1024, ps=1, tn=512, tk=D_FF, prologue=relu2, epilogue=ident, outs=(BF16,)
            )
            return a, y
        y, xn, hn = mm_nn(
            a, w_dn3[i], name=f"mlp{i}_down", tm=512, ps=1, tn=D_MODEL, tk=D_FF, prologue=relu2,
            epilogue=residual_ln_epilogue, extras=residual_extras(x_in, t_next - 1), outs=(BF16, F32, BF16),
        )
        return a, y, xn, hn

    def residual_extras(x_in, t):
        rows = (gate1[t], lng_full[t], lnb_full[t], scale[t + 1], shift[t + 1])
        return [(x_in, "full")] + [(r, "row") for r in rows]

    h0 = modulate(x0, scale[0], shift[0], name="modulate0")
    w_in3, w_aout3 = gather.wait([W_IN, W_AOUT], h0, name="gather_wait_a")
    w_aout3 = w_aout3.reshape(1, D_MODEL, D_MODEL)
    (a_pre,) = mm_nn(
        h0, w_in3, name="a_in", tm=2048, ps=4, tn=256, tk=D_MODEL, epilogue=lambda acc, b: (acc + b,),
        extras=[(a_b_in, "row")], outs=(BF16,),
    )
    p_gate = gate_fwd(a_pre, vn_g, vn_b, w_s, bias_full, name="gate_fwd")
    y0, x1, h1 = mm_nn(
        p_gate, w_aout3, name="a_out", tm=1024, ps=1, tn=D_MODEL, tk=D_MODEL, epilogue=residual_ln_epilogue,
        extras=residual_extras(x0, 0), outs=(BF16, F32, BF16),
    )
    w_mlp0 = gather.wait([W_UP0, W_DN0], y0, name="gather_wait_mlp0")
    a1, y1, x2, h2 = mlp_forward(0, h1, *w_mlp0, x_in=x1, t_next=2)
    (w_qkv_shards,) = gather.wait([W_QKV], y1, name="gather_wait_qkv")
    w_qkv3 = shards_to_columns(w_qkv_shards, name="w_qkv_columns")[None]
    pat_tiles = 3
    dil = [d for _, d in B_PATTERNS]
    qkv_p, pat_o, pat_lse = [], [], []
    for g in range(N_PAT):
        (qkv_g,) = mm_nn(
            h2, w_qkv3, name=f"b_qkv{g}", tm=2048, ps=1, tn=D_MODEL, tk=D_MODEL, epilogue=ident, outs=(BF16,),
            b_tile0=pat_tiles * g, b_tiles=pat_tiles, out_streams=dil[g],
        )
        o_g, lse_g = attn_fwd(qkv_g, g, name=f"attn_fwd{g}", after=rest_token if g == 1 else None)
        if g == 0:
            rest_passed, rest_token = gather.pass_on([W_BOUT, W_UP1, W_DN1], lse_g, name="gather_pass_rest")
        qkv_p.append(qkv_g)
        pat_o.append(o_g)
        pat_lse.append(lse_g)
    o_b, o_f, *lse_pat = attn_combine(pat_o, pat_lse, name="attn_combine")
    lse_pat = [x.reshape(SEQ, LANES) for x in lse_pat]
    w_bout3, *w_mlp1 = gather.finish(rest_passed, lse_pat[0], name="gather_wait_rest")
    w_bout3 = w_bout3.reshape(1, D_MODEL, D_MODEL)
    y2, x3, h3 = mm_nn(
        o_b, w_bout3, name="b_out", tm=1024, ps=1, tn=D_MODEL, tk=D_MODEL, epilogue=residual_ln_epilogue,
        extras=residual_extras(x2, 2), outs=(BF16, F32, BF16),
    )
    a3, y3 = mlp_forward(1, h3, *w_mlp1)
    dz3, dy3, st3, loss_local = residual_ln_loss_bwd(
        x3, y3, gate1[3], lng_full[3], lnb_full[3], target, name="res_ln3_loss_bwd"
    )

    def scatter(parts, after, name):
        return Scatter([p.reshape(N_DEV, -1, p.shape[-1]) for p in parts], after, name=name)

    xs_in, ys = [x0, x1, x2, x3], [y0, y1, y2, y3]

    def residual_bwd_extras(dz_later, t, zero):
        return [
            (dz_later, "full"), (scale[t + 1] + zero, "row"), (xs_in[t], "full"), (ys[t], "full"),
            (gate1[t], "row"), (lng_full[t], "row"), (lnb_full[t], "row"),
        ]

    def mlp_backward(i, h, a, dy, dz_later, t):
        (da,) = mm_nt(
            dy,
            w_dn3[i],
            name=f"mlp{i}_da",
            tm=2048,
            tko=1024,
            ps=1,
            tc=D_MODEL,
            epilogue=lambda acc, act: (acc * (2.0 * jnp.maximum(act.astype(F32), 0.0)),),
            extras=[(a, "full")],
            outs=(BF16,),
        )
        dw_dn = mm_tn(
            a, dy, name=f"mlp{i}_dw_down", p=1, tk=1024, ps=1, tn=D_MODEL, tmc=2048, prologue=relu2, out_dtype=BF16
        )
        dw_up = mm_tn(h, da, name=f"mlp{i}_dw_up", p=N_DEV, tk=1024, ps=2, tn=512, tmc=2048, out_dtype=BF16)
        rs = scatter([dw_up, dw_dn], da, f"scatter_mlp{i}")
        dz, dy_before, st = mm_nt(
            da, w_up3[i], name=f"mlp{i}_dh", tm=512, tko=D_MODEL, ps=N_DEV, tc=512, epilogue=residual_bwd_epilogue,
            extras=residual_bwd_extras(dz_later, t, rs.zero()), outs=(F32, BF16), stats=True,
        )
        return rs, dz, dy_before, st

    rs_mlp1, dz2, dy2, st2 = mlp_backward(1, h3, a3, dy3, dz3, 2)
    do_delta = mm_nt(
        dy2, w_bout3, name="b_do", tm=1024, tko=D_MODEL, ps=1, tc=D_MODEL, extras=[(o_f, "full")],
        epilogue=lambda acc, o: (acc,) * N_PAT + (_reduce_groups(acc * o),) * N_PAT,
        outs=(BF16,) * N_PAT + (F32,) * N_PAT, out_widths=[D_MODEL] * N_PAT + [LANES] * N_PAT, out_streams=dil + dil,
    )
    do_pat = [x.reshape(SEQ, D_MODEL) for x in do_delta[:N_PAT]]
    delta_pat = [x.reshape(SEQ, LANES) for x in do_delta[N_PAT:]]
    dh2, dw_pat = None, []
    for g in range(N_PAT):
        dqkv_g = attn_bwd(qkv_p[g], do_pat[g], lse_pat[g], delta_pat[g], g, name=f"attn_bwd{g}")
        dw_pat.append(
            mm_tn(
                h2, dqkv_g, name=f"b_dw_qkv{g}", p=1, tk=1024, ps=1, tn=D_MODEL, tmc=2048, out_dtype=BF16,
                g_streams=dil[g],
            )[0]
        )
        if g == N_PAT - 1:
            break
        (dh2,) = mm_nt(
            dqkv_g, w_qkv3, name=f"b_dh{g}", tm=2048, tko=512, ps=1, tc=pat_tiles * D_MODEL, outs=(F32,),
            b_tile0=g, g_streams=dil[g],
            epilogue=ident if g == 0 else (lambda acc, prev, d=dil[g]: (_from_streams(acc, d) + prev,)),
            extras=[] if g == 0 else [(dh2, "full")],
        )
    dw_bout = mm_tn(o_b, dy2, name="b_dw_out", p=1, tk=1024, ps=1, tn=D_MODEL, tmc=2048, out_dtype=BF16)
    dw_qkv = columns_to_shards(dw_pat, name="dw_qkv_shards")
    rs_b = scatter([dw_qkv, dw_bout], dqkv_g, "scatter_b")
    dz1, dy1, st1 = mm_nt(
        dqkv_g, w_qkv3, name=f"b_dh{N_PAT - 1}", tm=512, tko=D_MODEL, ps=1, tc=pat_tiles * D_MODEL, outs=(F32, BF16),
        b_tile0=N_PAT - 1, g_streams=dil[-1], stats=True,
        epilogue=lambda acc, prev, *rest: residual_bwd_epilogue(_from_streams(acc, dil[-1]) + prev, *rest),
        extras=[(dh2, "full")] + residual_bwd_extras(dz2, 1, rs_b.zero()),
    )
    rs_mlp0, dz0, dy0, st0 = mlp_backward(0, h1, a1, dy1, dz1, 0)
    (dp_gate,) = mm_nt(dy0, w_aout3, name="a_dp", tm=2048, tko=1024, ps=1, tc=D_MODEL, epilogue=ident, outs=(F32,))
    dw_aout = mm_tn(p_gate, dy0, name="a_dw_out", p=1, tk=1024, ps=1, tn=D_MODEL, tmc=2048, out_dtype=BF16)
    rs_aout = scatter([dw_aout], dp_gate, "scatter_a_out")
    da0, d_ws, d_bs, gate_rows = gate_bwd(a_pre, dp_gate, vn_g + rs_aout.zero(), vn_b, w_s, bias_full, name="gate_bwd")
    rs_ws = scatter([d_ws], gate_rows, "scatter_a_w_s")
    dw_in = mm_tn(h0, da0, name="a_dw_in", p=N_DEV, tk=1024, ps=4, tn=256, tmc=2048, out_dtype=BF16)
    rs_in = scatter([dw_in], rs_ws.token, "scatter_a_in")
    grad_x, stf = mm_nt(
        da0, w_in3, name="a_dh", tm=1024, tko=D_MODEL, ps=N_DEV, tc=256, epilogue=input_bwd_epilogue,
        extras=[(dz0, "full"), (scale[0] + rs_in.zero(), "row"), (x0, "full")], outs=(F32,), stats=True,
    )

    results = {}

    def update(wname, gparts, w, m, v):
        shape = w.shape
        layers = len(gparts) if isinstance(gparts, list) else 1
        w3 = w.reshape(layers, -1, shape[-1])
        parts = [g.reshape(g.shape[0], *w3.shape[1:]) for g in (gparts if layers > 1 else [gparts])]
        outs = adamw(parts, w3, m.reshape(w3.shape), v.reshape(w3.shape), name=f"adamw_{wname}")
        results[wname] = [o.reshape(shape) for o in outs]

    g_up1, g_dn1 = rs_mlp1.wait([0, 1], grad_x, name="scatter_wait_mlp1")
    g_qkv, g_bout = rs_b.wait([0, 1], grad_x, name="scatter_wait_b")
    update("b_w_qkv", g_qkv, b_w_qkv, m_b_w_qkv, v_b_w_qkv)
    update("b_w_out", g_bout, b_w_out, m_b_w_out, v_b_w_out)
    g_up0, g_dn0 = rs_mlp0.wait([0, 1], grad_x, name="scatter_wait_mlp0")
    update("mlp_w_up", [g_up0, g_up1], mlp_w_up, m_mlp_w_up, v_mlp_w_up)
    update("mlp_w_down", [g_dn0, g_dn1], mlp_w_down, m_mlp_w_down, v_mlp_w_down)

    stats_after = [stf, st0, st1, st2]
    stats_own = [st0, st1, st2, st3]
    dm = jnp.stack(
        [
            jnp.concatenate(
                [stats_after[t][ST_DSHIFT], stats_after[t][ST_DSCALE], stats_own[t][ST_DGATE]], axis=0
            )
            for t in range(N_SUB)
        ]
    )
    ncol = 3 * D_MODEL // N_DEV
    dmx = jnp.pad(
        dm.reshape(N_SUB, N_DEV, ncol).transpose(1, 0, 2)[:, :, None, :], ((0, 0), (0, 0), (0, SUBLANES - 1), (0, 0))
    )
    small = [
        gate_rows[0],
        gate_rows[1],
        d_bs[:, :A_GROUPS].T.reshape(-1),
        *[stats_own[t][ST_DG] for t in range(N_SUB)],
        *[stats_own[t][ST_DB] for t in range(N_SUB)],
        jnp.pad(loss_local.reshape(1), (0, LANES - 1)),
    ]
    n_small = sum(s.size for s in small)
    part_rows = -(-n_small // (N_DEV * LANES * SUBLANES)) * SUBLANES
    flat = jnp.concatenate(small + [jnp.zeros((N_DEV * part_rows * LANES - n_small,), F32)])
    (ws_parts,) = rs_ws.wait([0], results["mlp_w_down"][3], name="scatter_wait_a_w_s")
    dmrecv, reduced = small_exchange(dmx, flat.reshape(N_DEV, part_rows, LANES), ws_parts, results["mlp_w_down"][3])
    g_ws = reduced[:, part_rows:, :].reshape(-1)
    reduced = reduced[:, :part_rows, :].reshape(-1)
    sizes = [2 * D_MODEL, D_MODEL, D_MODEL, A_GROUPS * CHUNK, N_SUB * D_MODEL, N_SUB * D_MODEL]
    offs = [sum(sizes[:i]) for i in range(len(sizes) + 1)]
    g_b_in, g_vn_g, g_vn_b, g_bs, g_lng, g_lnb = [reduced[offs[i] : offs[i + 1]] for i in range(len(sizes))]
    loss = reduced[offs[-1]]

    ada_outs = ada_grad_adamw(cg, dmrecv, ada_w4, m_ada_w.reshape(ada_w4.shape), v_ada_w.reshape(ada_w4.shape), name="ada_grad_adamw")
    results["ada_w"] = [o.reshape(ada_w.shape) for o in ada_outs[:4]]
    ln_cols = D_MODEL // N_DEV
    my_ln = lambda gfull: lax.dynamic_slice_in_dim(gfull.reshape(N_SUB, N_DEV, ln_cols), me, 1, axis=1)
    small_params = [
        ("ada_b", ada_outs[4][:, 0, :], ada_b, m_ada_b, v_ada_b),
        ("ln_g", my_ln(g_lng), ln_g, m_ln_g, v_ln_g),
        ("ln_b", my_ln(g_lnb), ln_b, m_ln_b, v_ln_b),
        ("a_b_in", g_b_in, a_b_in, m_a_b_in, v_a_b_in),
        ("a_vn_g", g_vn_g, a_vn_g, m_a_vn_g, v_a_vn_g),
        ("a_vn_b", g_vn_b, a_vn_b, m_a_vn_b, v_a_vn_b),
        ("a_b_s", g_bs, a_b_s, m_a_b_s, v_a_b_s),
    ]
    small_outs = adamw_small(
        [[a.reshape(-1, w.shape[-1]) for a in (g, w, m, v)] for _, g, w, m, v in small_params], name="adamw_small"
    )
    for (wname, _, w, _, _), outs in zip(small_params, small_outs, strict=True):
        results[wname] = [o.reshape(w.shape) for o in outs]
    update("a_w_s", g_ws[None], a_w_s, m_a_w_s, v_a_w_s)
    (g_aout,) = rs_aout.wait([0], grad_x, name="scatter_wait_a_out")
    (g_in,) = rs_in.wait([0], grad_x, name="scatter_wait_a_in")
    update("a_w_in", g_in, a_w_in, m_a_w_in, v_a_w_in)
    update("a_w_out", g_aout, a_w_out, m_a_w_out, v_a_w_out)

    order = ["ada_w", "ada_b", "ln_g", "ln_b", "a_w_in", "a_b_in", "a_vn_g", "a_vn_b", "a_w_s", "a_b_s", "a_w_out", "b_w_qkv", "b_w_out", "mlp_w_up", "mlp_w_down"]
    return (loss, grad_x[None], *[results[n][0] for n in order], *[results[n][1] for n in order],
            *[results[n][2] for n in order], *[results[n][3] for n in order])
```

```python
import math

import jax
import jax.numpy as jnp
import numpy as np
from jax import lax
from jax.experimental import pallas as pl
from jax.experimental.pallas import tpu as pltpu

F32 = jnp.float32
BF16 = jnp.bfloat16
MESH = pl.DeviceIdType.MESH
ANY = pl.BlockSpec(memory_space=pl.ANY)
VMEM = pl.BlockSpec(memory_space=pltpu.VMEM)

N_DEV = 8
D_MODEL = 1024
SEQ = 4096
DEPTH = 2
CHUNK = 128
A_GROUPS = 16
A_GROUP_DIM = D_MODEL // A_GROUPS
B_HEADS = 16
B_HEAD_DIM = 64
B_PATTERNS = ((128, 1), (512, 4), (2048, 16))
N_PAT = len(B_PATTERNS)
SPAN = 128
D_FF = 4 * D_MODEL
D_QKV = N_PAT * 3 * D_MODEL
ALPHA = (2 * DEPTH) ** 0.25
LN_EPS = 1e-5
NEG = -1e30
ADAM_LR = 0.001
ADAM_B1 = 0.9
ADAM_B2 = 0.999
ADAM_EPS = 1e-08
ADAM_WD = 0.01
ADAM_STEP = 10
GELU_C = math.sqrt(2.0 / math.pi)
GELU_A = 0.044715

VMEM_LIMIT_BYTES = 56 * 1024 * 1024
LANES = 128
SUBLANES = 8
ROW_TILE = 512
N_SUB = 2 * DEPTH


def _params(sem):
    return pltpu.CompilerParams(dimension_semantics=sem, vmem_limit_bytes=VMEM_LIMIT_BYTES)


def _lane(shape):
    return lax.broadcasted_iota(jnp.int32, shape, len(shape) - 1)


def _split_bf16(x):
    hi = x.astype(BF16)
    lo = (x - hi.astype(F32)).astype(BF16)
    return hi, lo


def _group_expand_matrix(groups_padded, width):
    per = width // A_GROUPS
    r = lax.broadcasted_iota(jnp.int32, (groups_padded, width), 0)
    c = lax.broadcasted_iota(jnp.int32, (groups_padded, width), 1)
    return (c // per == r).astype(BF16)


def _group_reduce_matrix(width, groups_padded):
    per = width // A_GROUPS
    r = lax.broadcasted_iota(jnp.int32, (width, groups_padded), 0)
    c = lax.broadcasted_iota(jnp.int32, (width, groups_padded), 1)
    return (r // per == c).astype(BF16)


def _expand_groups(w):
    e = _group_expand_matrix(LANES, D_MODEL)
    hi, lo = _split_bf16(w)
    return jnp.dot(hi, e, preferred_element_type=F32) + jnp.dot(lo, e, preferred_element_type=F32)


def _reduce_groups(x):
    e = _group_reduce_matrix(D_MODEL, LANES)
    hi, lo = _split_bf16(x)
    return jnp.dot(hi, e, preferred_element_type=F32) + jnp.dot(lo, e, preferred_element_type=F32)


def _to_streams(x, d):
    rows, w = x.shape
    return jnp.swapaxes(x.reshape(rows // d, d, w), 0, 1).reshape(rows, w)


def _from_streams(x, d):
    rows, w = x.shape
    return jnp.swapaxes(x.reshape(d, rows // d, w), 0, 1).reshape(rows, w)


def _column_tiles(p, n, ps, tn):
    assert (ps == 1 or tn == n) and p % ps == 0 and n % tn == 0
    q = n // tn
    return (p // ps) * q, q


def _extra_specs(extras, tm, width):
    specs = []
    for _, kind in extras:
        if kind == "row":
            specs.append(pl.BlockSpec((1, width), lambda i, j, c: (0, j)))
        else:
            specs.append(pl.BlockSpec((tm, width), lambda i, j, c: (i, j)))
    return specs


def mm_nn(a, b3, *, name, tm, ps, tn, tk, epilogue, extras=(), outs, prologue=None, b_tile0=0, b_tiles=None, out_streams=1):
    m, k = a.shape
    p, _, n = b3.shape
    nj, q = _column_tiles(p, n, ps, tn)
    nj = nj if b_tiles is None else b_tiles
    nk = k // tk
    width = ps * tn
    d = out_streams
    assert d == 1 or not extras

    def body(a_ref, b_ref, *rest):
        ex = rest[: len(extras)]
        out_refs = rest[len(extras) : len(extras) + len(outs)]
        kk = pl.program_id(2)
        av = a_ref[...] if prologue is None else prologue(a_ref[...])
        if d > 1:
            av = _to_streams(av, d)

        def finish(cs, acc):
            res = epilogue(acc, *[e[:, cs] for e in ex])
            for o_ref, r in zip(out_refs, res, strict=True):
                if d > 1:
                    o_ref[:, :, cs] = r.astype(o_ref.dtype).reshape(d, tm // d, tn)
                else:
                    o_ref[:, cs] = r.astype(o_ref.dtype)

        for s in range(ps):
            cs = slice(s * tn, (s + 1) * tn)
            part = jnp.dot(av, b_ref[s], preferred_element_type=F32)
            if nk == 1:
                finish(cs, part)
                continue
            acc_ref = rest[-1]

            @pl.when(kk == 0)
            def _(part=part, cs=cs):
                acc_ref[:, cs] = part

            @pl.when(kk > 0)
            def _(part=part, cs=cs):
                acc_ref[:, cs] += part

        if nk > 1:

            @pl.when(kk == nk - 1)
            def _():
                for s in range(ps):
                    cs = slice(s * tn, (s + 1) * tn)
                    finish(cs, rest[-1][:, cs])

    if d > 1:
        out_spec = pl.BlockSpec((d, tm // d, width), lambda i, j, kk: (0, i, j))
        out_shape = (d, m // d, nj * width)
    else:
        out_spec = pl.BlockSpec((tm, width), lambda i, j, kk: (i, j))
        out_shape = (m, nj * width)
    res = pl.pallas_call(
        body,
        name=name,
        grid=(m // tm, nj, nk),
        in_specs=[
            pl.BlockSpec((tm, tk), lambda i, j, kk: (i, kk)),
            pl.BlockSpec((ps, tk, tn), lambda i, j, kk: ((j + b_tile0) // q, kk, (j + b_tile0) % q)),
            *_extra_specs(extras, tm, width),
        ],
        out_specs=[out_spec for _ in outs],
        out_shape=[jax.ShapeDtypeStruct(out_shape, dt) for dt in outs],
        scratch_shapes=[pltpu.VMEM((tm, width), F32)] if nk > 1 else [],
        compiler_params=_params(("parallel", "parallel", "arbitrary")),
    )(a, b3, *[arr for arr, _ in extras])
    return [r.reshape(m, nj * width) for r in res]


def mm_nt(
    g, b3, *, name, tm, tko, ps, tc, epilogue, extras=(), outs, out_widths=None, out_streams=None, b_tile0=0,
    g_streams=1, stats=False,
):
    m, width = g.shape
    p, k, n = b3.shape
    _, q = _column_tiles(p, n, ps, tc)
    nc = width // (ps * tc)
    ds = g_streams
    assert not stats or tko == k
    widths = [tko] * len(outs) if out_widths is None else out_widths
    streams_out = [1] * len(outs) if out_streams is None else out_streams

    def out_spec(w, d):
        if d == 1:
            return pl.BlockSpec((tm, w), lambda i, j, c: (i, j))
        return pl.BlockSpec((d, tm // d, w), lambda i, j, c: (0, i, j))

    def out_struct(dt, w, d):
        cols = w * (k // tko)
        return jax.ShapeDtypeStruct((m, cols) if d == 1 else (d, m // d, cols), dt)

    def body(g_ref, b_ref, *rest):
        ex = rest[: len(extras)]
        out_refs = rest[len(extras) : len(extras) + len(outs)]
        c = pl.program_id(2)
        gv = g_ref[...].reshape(tm, ps * tc) if ds > 1 else g_ref[...]
        part = None
        for s in range(ps):
            d = lax.dot_general(gv[:, s * tc : (s + 1) * tc], b_ref[s], _NT, preferred_element_type=F32)
            part = d if part is None else part + d

        def finish(acc):
            res = epilogue(acc, *[e[...] for e in ex])
            if stats:
                *res, sums = res
                st_ref = rest[len(extras) + len(outs)]

                @pl.when(pl.program_id(0) == 0)
                def _():
                    st_ref[...] = jnp.zeros_like(st_ref)

                for row, val in enumerate(sums):
                    st_ref[row : row + 1, :] += val
            for o_ref, r, d_out in zip(out_refs, res, streams_out, strict=True):
                r = r.astype(o_ref.dtype)
                o_ref[...] = r if d_out == 1 else _to_streams(r, d_out).reshape(o_ref.shape)

        if nc == 1:
            finish(part)
            return
        acc_ref = rest[-1]

        @pl.when(c == 0)
        def _():
            acc_ref[...] = part

        @pl.when(c > 0)
        def _():
            acc_ref[...] += part

        @pl.when(c == nc - 1)
        def _():
            finish(acc_ref[...])

    return pl.pallas_call(
        body,
        name=name,
        grid=(m // tm, k // tko, nc),
        in_specs=[
            pl.BlockSpec((ds, tm // ds, ps * tc), lambda i, j, c: (0, i, c))
            if ds > 1
            else pl.BlockSpec((tm, ps * tc), lambda i, j, c: (i, c)),
            pl.BlockSpec((ps, tko, tc), lambda i, j, c: ((c + b_tile0) // q, j, (c + b_tile0) % q)),
            *_extra_specs(extras, tm, tko),
        ],
        out_specs=[out_spec(w, d) for w, d in zip(widths, streams_out, strict=True)]
        + [pl.BlockSpec((SUBLANES, tko), lambda i, j, c: (0, 0))] * stats,
        out_shape=[out_struct(dt, w, d) for dt, w, d in zip(outs, widths, streams_out, strict=True)]
        + [jax.ShapeDtypeStruct((SUBLANES, k), F32)] * stats,
        scratch_shapes=[pltpu.VMEM((tm, tko), F32)] if nc > 1 else [],
        compiler_params=_params(("arbitrary" if stats else "parallel", "parallel", "arbitrary")),
    )(g.reshape(ds, m // ds, width) if ds > 1 else g, b3, *[arr for arr, _ in extras])


def mm_tn(a, g, *, name, p, tk, ps, tn, tmc, out_dtype, prologue=None, g_streams=1):
    m, k = a.shape
    width = g.shape[1]
    n = width // p
    nj, q = _column_tiles(p, n, ps, tn)
    nc = m // tmc
    ds = g_streams

    def body(a_ref, g_ref, o_ref, acc_ref):
        c = pl.program_id(2)
        av = a_ref[...] if prologue is None else prologue(a_ref[...])
        gv = g_ref[...]
        if ds > 1:
            av, gv = _to_streams(av, ds), gv.reshape(tmc, ps * tn)
        part = lax.dot_general(av, gv, (((0,), (0,)), ((), ())), preferred_element_type=F32)

        @pl.when(c == 0)
        def _():
            acc_ref[...] = part

        @pl.when(c > 0)
        def _():
            acc_ref[...] += part

        @pl.when(c == nc - 1)
        def _():
            for s in range(ps):
                o_ref[s] = acc_ref[:, s * tn : (s + 1) * tn].astype(o_ref.dtype)

    return pl.pallas_call(
        body,
        name=name,
        grid=(k // tk, nj, nc),
        in_specs=[
            pl.BlockSpec((tmc, tk), lambda i, j, c: (c, i)),
            pl.BlockSpec((ds, tmc // ds, ps * tn), lambda i, j, c: (0, c, j))
            if ds > 1
            else pl.BlockSpec((tmc, ps * tn), lambda i, j, c: (c, j)),
        ],
        out_specs=pl.BlockSpec((ps, tk, tn), lambda i, j, c: (j // q, i, j % q)),
        out_shape=jax.ShapeDtypeStruct((p, k, n), out_dtype),
        scratch_shapes=[pltpu.VMEM((tk, ps * tn), F32)],
        compiler_params=_params(("parallel", "parallel", "arbitrary")),
    )(a, g.reshape(ds, m // ds, width) if ds > 1 else g)


def _rows(cols):
    return pl.BlockSpec((ROW_TILE, cols), lambda i: (i, 0))


def _vec(cols, rows=1):
    return pl.BlockSpec((rows, cols), lambda i: (0, 0))


def _layer_norm_hat(z):
    mu = jnp.mean(z, axis=-1, keepdims=True)
    zc = z - mu
    var = jnp.mean(zc * zc, axis=-1, keepdims=True)
    rstd = lax.rsqrt(var + LN_EPS)
    return zc * rstd, rstd


def modulate(x, scale, shift, *, name):
    s, d = x.shape

    def body(x_ref, sc_ref, sh_ref, h_ref):
        h_ref[...] = (x_ref[...] * (1.0 + sc_ref[...]) + sh_ref[...]).astype(BF16)

    return pl.pallas_call(
        body,
        name=name,
        grid=(s // ROW_TILE,),
        in_specs=[_rows(d), _vec(d), _vec(d)],
        out_specs=_rows(d),
        out_shape=jax.ShapeDtypeStruct((s, d), BF16),
        compiler_params=_params(("parallel",)),
    )(x, scale, shift)


def residual_ln_epilogue(acc, x, gate1, g, b, nscale, nshift):
    y = acc.astype(BF16)
    xhat, _ = _layer_norm_hat(ALPHA * x + gate1 * y.astype(F32))
    xn = xhat * g + b
    return y, xn, xn * (1.0 + nscale) + nshift


ST_DSCALE, ST_DSHIFT, ST_DG, ST_DB, ST_DGATE = 0, 1, 2, 3, 4


def _layer_norm_bwd(g_out, xhat, rstd, g):
    dxh = g_out * g
    m1 = jnp.mean(dxh, axis=-1, keepdims=True)
    m2 = jnp.mean(dxh * xhat, axis=-1, keepdims=True)
    return rstd * (dxh - m1 - xhat * m2)


def _column_sums(vals):
    return [jnp.sum(v, axis=0, keepdims=True) for v in vals]


def residual_bwd_epilogue(dh, dzl, scl, x, y, gate1, g, b):
    yf = y.astype(F32)
    xhat, rstd = _layer_norm_hat(ALPHA * x + gate1 * yf)
    g_out = ALPHA * dzl + dh * (1.0 + scl)
    dz = _layer_norm_bwd(g_out, xhat, rstd, g)
    return dz, dz * gate1, _column_sums([dh * (xhat * g + b), dh, g_out * xhat, g_out, dz * yf])


def input_bwd_epilogue(dh, dzl, scl, x):
    return ALPHA * dzl + dh * (1.0 + scl), _column_sums([dh * x, dh])


def residual_ln_loss_bwd(x, y, gate1, g, b, target, *, name):
    s, d = x.shape

    def body(x_ref, y_ref, gt_ref, g_ref, b_ref, t_ref, dz_ref, dy_ref, st_ref, loss_ref):
        y = y_ref[...].astype(F32)
        gate1 = gt_ref[...]
        xhat, rstd = _layer_norm_hat(ALPHA * x_ref[...] + gate1 * y)
        err = xhat * g_ref[...] + b_ref[...] - t_ref[...]
        g_out = err * (1.0 / d)
        dz = _layer_norm_bwd(g_out, xhat, rstd, g_ref[...])
        dz_ref[...] = dz
        dy_ref[...] = (dz * gate1).astype(BF16)
        part = jnp.sum(jnp.sum(err * err, axis=1, keepdims=True), axis=0, keepdims=True) * (0.5 / d)

        @pl.when(pl.program_id(0) == 0)
        def _():
            st_ref[...] = jnp.zeros_like(st_ref)
            loss_ref[...] = jnp.zeros_like(loss_ref)

        loss_ref[...] += part
        for row, val in zip((ST_DG, ST_DB, ST_DGATE), _column_sums([g_out * xhat, g_out, dz * y]), strict=True):
            st_ref[row : row + 1, :] += val

    return pl.pallas_call(
        body,
        name=name,
        grid=(s // ROW_TILE,),
        in_specs=[_rows(d), _rows(d), _vec(d), _vec(d), _vec(d), _rows(d)],
        out_specs=[_rows(d), _rows(d), _vec(d, SUBLANES), pl.BlockSpec((1, 1), lambda i: (0, 0))],
        out_shape=[
            jax.ShapeDtypeStruct((s, d), F32),
            jax.ShapeDtypeStruct((s, d), BF16),
            jax.ShapeDtypeStruct((SUBLANES, d), F32),
            jax.ShapeDtypeStruct((1, 1), F32),
        ],
        compiler_params=_params(("arbitrary",)),
    )(x, y, gate1, g, b, target)


GATE_CHUNKS = 4


def _gelu(x, with_grad=False):
    x2 = x * x
    t = jnp.tanh(GELU_C * (x + GELU_A * x2 * x))
    half = 0.5 * (1.0 + t)
    y = x * half
    if not with_grad:
        return y
    return y, half + 0.5 * x * (1.0 - t * t) * (GELU_C * (1.0 + 3.0 * GELU_A * x2))


def _causal_weights(w_ref, transpose):
    t = lax.broadcasted_iota(jnp.int32, (CHUNK, CHUNK), 0)
    s = lax.broadcasted_iota(jnp.int32, (CHUNK, CHUNK), 1)
    out = []
    for g in range(A_GROUPS):
        w = jnp.where(t >= s, w_ref[g], 0.0)
        out.append((w.T if transpose else w).astype(BF16))
    return out


def _spatial(ws, vn, lo_mask):
    rows = vn.shape[0]
    out_rows = []
    for r in range(rows // CHUNK):
        cols = []
        for j in range(A_GROUPS // 2):
            blk = vn[r * CHUNK : (r + 1) * CHUNK, j * LANES : (j + 1) * LANES]
            za = jnp.dot(ws[2 * j], blk, preferred_element_type=F32)
            zb = jnp.dot(ws[2 * j + 1], blk, preferred_element_type=F32)
            cols.append(jnp.where(lo_mask, za, zb))
        out_rows.append(jnp.concatenate(cols, axis=1))
    return jnp.concatenate(out_rows, axis=0)


def _gate_forward(a, vg, vb, ws, bias, lo_mask, with_grad=False):
    u = _gelu(a[:, :D_MODEL], with_grad)
    v = _gelu(a[:, D_MODEL:], with_grad)
    gu, gv = None, None
    if with_grad:
        (u, gu), (v, gv) = u, v
    vhat, rstd = _layer_norm_hat(v)
    vn = (vhat * vg + vb).astype(BF16)
    z = _spatial(ws, vn, lo_mask) + jnp.concatenate([bias] * (a.shape[0] // CHUNK), axis=0)
    return u, vhat, rstd, vn, z, gu, gv


def gate_fwd(a_pre, vn_g, vn_b, w_s, bias_full, *, name):
    s = a_pre.shape[0]
    tr = GATE_CHUNKS * CHUNK

    def body(a_ref, vg_ref, vb_ref, w_ref, bias_ref, p_ref):
        lo_mask = _lane((CHUNK, LANES)) < A_GROUP_DIM
        ws = _causal_weights(w_ref, transpose=False)
        u, _, _, _, z, _, _ = _gate_forward(a_ref[...].astype(F32), vg_ref[...], vb_ref[...], ws, bias_ref[...], lo_mask)
        p_ref[...] = (u * z).astype(BF16)

    return pl.pallas_call(
        body,
        name=name,
        grid=(s // tr,),
        in_specs=[
            pl.BlockSpec((tr, 2 * D_MODEL), lambda i: (i, 0)),
            _vec(D_MODEL),
            _vec(D_MODEL),
            pl.BlockSpec((A_GROUPS, CHUNK, CHUNK), lambda i: (0, 0, 0)),
            _vec(D_MODEL, CHUNK),
        ],
        out_specs=pl.BlockSpec((tr, D_MODEL), lambda i: (i, 0)),
        out_shape=jax.ShapeDtypeStruct((s, D_MODEL), BF16),
        compiler_params=_params(("parallel",)),
    )(a_pre, vn_g, vn_b, w_s, bias_full)


def gate_bwd(a_pre, dp, vn_g, vn_b, w_s, bias_full, *, name):
    s = a_pre.shape[0]
    tr = GATE_CHUNKS * CHUNK
    nsteps = s // tr

    def body(a_ref, dp_ref, vg_ref, vb_ref, w_ref, bias_ref, da_ref, dw_ref, dbs_ref, rows_ref, dbias_acc):
        step = pl.program_id(0)
        lo_mask = _lane((CHUNK, LANES)) < A_GROUP_DIM

        @pl.when(step == 0)
        def _():
            dw_ref[...] = jnp.zeros_like(dw_ref)
            rows_ref[...] = jnp.zeros_like(rows_ref)
            dbias_acc[...] = jnp.zeros_like(dbias_acc)

        a = a_ref[...].astype(F32)
        vg = vg_ref[...]
        ws = _causal_weights(w_ref, transpose=False)
        wts = _causal_weights(w_ref, transpose=True)
        u, vhat, rstd, vn, z, gelu_du, gelu_dv = _gate_forward(a, vg, vb_ref[...], ws, bias_ref[...], lo_mask, True)
        dp = dp_ref[...]
        du = dp * z
        dzz = dp * u
        dzz_b = dzz.astype(BF16)
        dvn = _spatial(wts, dzz_b, lo_mask)
        dbias = None
        for r in range(GATE_CHUNKS):
            rs = slice(r * CHUNK, (r + 1) * CHUNK)
            dbias = dzz[rs] if dbias is None else dbias + dzz[rs]
            for j in range(A_GROUPS // 2):
                cs = slice(j * LANES, (j + 1) * LANES)
                dblk = dzz[rs, cs]
                vblk = vn[rs, cs]
                for half in range(2):
                    keep = lo_mask if half == 0 else jnp.logical_not(lo_mask)
                    dm = jnp.where(keep, dblk, 0.0).astype(BF16)
                    dw_ref[2 * j + half] += lax.dot_general(
                        dm, vblk, (((1,), (1,)), ((), ())), preferred_element_type=F32
                    )
        dbias_acc[...] += dbias
        rows_ref[1:2, :D_MODEL] += jnp.sum(dvn * vhat, axis=0, keepdims=True)
        rows_ref[1:2, D_MODEL:] += jnp.sum(dvn, axis=0, keepdims=True)
        dvh = dvn * vg
        m1 = jnp.mean(dvh, axis=-1, keepdims=True)
        m2 = jnp.mean(dvh * vhat, axis=-1, keepdims=True)
        dv = rstd * (dvh - m1 - vhat * m2)
        da_u = du * gelu_du
        da_v = dv * gelu_dv
        da_ref[:, :D_MODEL] = da_u.astype(BF16)
        da_ref[:, D_MODEL:] = da_v.astype(BF16)
        rows_ref[0:1, :D_MODEL] += jnp.sum(da_u, axis=0, keepdims=True)
        rows_ref[0:1, D_MODEL:] += jnp.sum(da_v, axis=0, keepdims=True)

        @pl.when(step == nsteps - 1)
        def _():
            t = lax.broadcasted_iota(jnp.int32, (CHUNK, CHUNK), 0)
            sx = lax.broadcasted_iota(jnp.int32, (CHUNK, CHUNK), 1)
            for g in range(A_GROUPS):
                dw_ref[g] = jnp.where(t >= sx, dw_ref[g], 0.0)
            dbs_ref[...] = _reduce_groups(dbias_acc[...])

    return pl.pallas_call(
        body,
        name=name,
        grid=(nsteps,),
        in_specs=[
            pl.BlockSpec((tr, 2 * D_MODEL), lambda i: (i, 0)),
            pl.BlockSpec((tr, D_MODEL), lambda i: (i, 0)),
            _vec(D_MODEL),
            _vec(D_MODEL),
            pl.BlockSpec((A_GROUPS, CHUNK, CHUNK), lambda i: (0, 0, 0)),
            _vec(D_MODEL, CHUNK),
        ],
        out_specs=[
            pl.BlockSpec((tr, 2 * D_MODEL), lambda i: (i, 0)),
            pl.BlockSpec((A_GROUPS, CHUNK, CHUNK), lambda i: (0, 0, 0)),
            _vec(LANES, CHUNK),
            _vec(2 * D_MODEL, SUBLANES),
        ],
        out_shape=[
            jax.ShapeDtypeStruct((s, 2 * D_MODEL), BF16),
            jax.ShapeDtypeStruct((A_GROUPS, CHUNK, CHUNK), F32),
            jax.ShapeDtypeStruct((CHUNK, LANES), F32),
            jax.ShapeDtypeStruct((SUBLANES, 2 * D_MODEL), F32),
        ],
        scratch_shapes=[pltpu.VMEM((CHUNK, D_MODEL), F32)],
        compiler_params=_params(("arbitrary",)),
    )(a_pre, dp, vn_g, vn_b, w_s, bias_full)


def alibi_tables(dilation):
    qi = np.arange(SPAN)[:, None]
    ki = np.arange(2 * SPAN)[None, :]
    diff = SPAN + qi - ki
    valid = (diff >= 0) & (diff <= SPAN)
    heads = np.arange(1, B_HEADS + 1, dtype=np.float32)
    slopes = np.exp2(np.float32(-8.0) * heads / np.float32(B_HEADS)).astype(np.float32)
    bias = -slopes[:, None, None] * (dilation * diff).astype(np.float32)
    bias = np.where(valid[None], bias, np.float32(NEG)).reshape(B_HEADS // 2, 2 * SPAN, 2 * SPAN)
    return jnp.asarray(bias), jnp.asarray(np.ascontiguousarray(bias.transpose(0, 2, 1)))


def _pair_rows(x, halves):
    return jnp.concatenate([x * halves[0], x * halves[1]], axis=0)


def _pair_column(v, lane, j):
    pick = lambda h: jnp.sum(jnp.where(lane == h, v, 0.0), axis=1, keepdims=True)
    return jnp.concatenate([pick(2 * j), pick(2 * j + 1)], axis=0)


_NT = (((1,), (1,)), ((), ()))


def _qkv_specs(block_of):
    def spec(which, prev):
        def index(*grid):
            blk = block_of(*grid)
            return (jnp.maximum(blk - 1, 0) if prev else blk, which)

        return pl.BlockSpec((SPAN, D_MODEL), index)

    return [spec(0, False), spec(1, True), spec(1, False), spec(2, True), spec(2, False)]


def attn_fwd(qkv_p, pat, *, name, after=None):
    _, dilation = B_PATTERNS[pat]
    nb = SEQ // dilation // SPAN
    bias, _ = alibi_tables(dilation)
    order = [] if after is None else [after]

    def body(q_ref, kp_ref, kc_ref, vp_ref, vc_ref, bias_ref, *rest):
        o_ref, lse_ref = rest[len(order) :]
        n = pl.program_id(1)
        first_prev = jnp.logical_and(n == 0, _lane((2 * SPAN, 2 * SPAN)) < SPAN)
        lane = _lane((SPAN, LANES))
        lo_mask = lane < B_HEAD_DIM
        q = q_ref[...] * jnp.asarray(B_HEAD_DIM**-0.5, BF16)
        kk = jnp.concatenate([kp_ref[...], kc_ref[...]], axis=0)
        vv = jnp.concatenate([vp_ref[...], vc_ref[...]], axis=0)
        halves = (lo_mask.astype(BF16), jnp.logical_not(lo_mask).astype(BF16))
        stats = jnp.zeros((SPAN, LANES), F32)
        for j in range(B_HEADS // 2):
            cs = slice(j * LANES, (j + 1) * LANES)
            sc = lax.dot_general(_pair_rows(q[:, cs], halves), kk[:, cs], _NT, preferred_element_type=F32)
            sc = jnp.where(first_prev, NEG, sc + bias_ref[j])
            m = jnp.max(sc, axis=1, keepdims=True)
            p = jnp.exp(sc - m)
            l = jnp.sum(p, axis=1, keepdims=True)
            acc = jnp.dot(p.astype(BF16), vv[:, cs], preferred_element_type=F32) * (1.0 / l)
            lse_pair = m + jnp.log(l)
            o_ref[:, cs] = jnp.where(lo_mask, acc[:SPAN], acc[SPAN:]).astype(BF16)
            stats = jnp.where(lane == 2 * j, lse_pair[:SPAN], stats)
            stats = jnp.where(lane == 2 * j + 1, lse_pair[SPAN:], stats)
        lse_ref[...] = stats

    return pl.pallas_call(
        body,
        name=name,
        grid=(dilation, nb),
        in_specs=[
            *_qkv_specs(lambda r, n: r * nb + n),
            pl.BlockSpec((B_HEADS // 2, 2 * SPAN, 2 * SPAN), lambda r, n: (0, 0, 0)),
            *[ANY for _ in order],
        ],
        out_specs=[
            pl.BlockSpec((SPAN, D_MODEL), lambda r, n: (r * nb + n, 0)),
            pl.BlockSpec((SPAN, LANES), lambda r, n: (r * nb + n, 0)),
        ],
        out_shape=[jax.ShapeDtypeStruct((SEQ, D_MODEL), BF16), jax.ShapeDtypeStruct((SEQ, LANES), F32)],
        compiler_params=_params(("parallel", "arbitrary")),
    )(qkv_p, qkv_p, qkv_p, qkv_p, qkv_p, bias, *order)


def attn_combine(outs, lses, *, name):
    dils = [d for _, d in B_PATTERNS]

    def positions(ref, d):
        x = ref[...].astype(F32)
        return x if d == 1 else jnp.swapaxes(x, 0, 1).reshape(ROW_TILE, x.shape[-1])

    def body(o0, o1, o2, l0, l1, l2, ob_ref, of_ref, *lse_refs):
        ls = [positions(l, d) for l, d in zip((l0, l1, l2), dils, strict=True)]
        m = jnp.maximum(jnp.maximum(ls[0], ls[1]), ls[2])
        tot = jnp.log(jnp.exp(ls[0] - m) + jnp.exp(ls[1] - m) + jnp.exp(ls[2] - m)) + m
        o = None
        for o_ref, l, d in zip((o0, o1, o2), ls, dils, strict=True):
            term = _expand_groups(jnp.exp(l - tot)) * positions(o_ref, d)
            o = term if o is None else o + term
        ob_ref[...] = o.astype(BF16)
        of_ref[...] = o
        for lse_ref, d in zip(lse_refs, dils, strict=True):
            lse_ref[...] = tot if d == 1 else _to_streams(tot, d).reshape(lse_ref.shape)

    def stream_rows(cols, d):
        return _rows(cols) if d == 1 else pl.BlockSpec((d, ROW_TILE // d, cols), lambda i: (0, i, 0))

    def streams(x, d):
        return x if d == 1 else x.reshape(d, SEQ // d, x.shape[-1])

    outs = [streams(x, d) for x, d in zip(outs, dils, strict=True)]
    lses = [streams(x, d) for x, d in zip(lses, dils, strict=True)]
    return pl.pallas_call(
        body,
        name=name,
        grid=(SEQ // ROW_TILE,),
        in_specs=[stream_rows(D_MODEL, d) for d in dils] + [stream_rows(LANES, d) for d in dils],
        out_specs=[_rows(D_MODEL), _rows(D_MODEL)] + [stream_rows(LANES, d) for d in dils],
        out_shape=[jax.ShapeDtypeStruct((SEQ, D_MODEL), BF16), jax.ShapeDtypeStruct((SEQ, D_MODEL), F32)]
        + [jax.ShapeDtypeStruct(lse.shape, F32) for lse in lses],
        compiler_params=_params(("parallel",)),
    )(*outs, *lses)


def attn_bwd(qkv_p, do_p, lse_p, delta_p, pat, *, name):
    _, dilation = B_PATTERNS[pat]
    nb = SEQ // dilation // SPAN
    n_blocks = SEQ // SPAN
    bias, bias_t = alibi_tables(dilation)
    last = n_blocks - 1
    q_cols, k_cols, v_cols = (slice(i * D_MODEL, (i + 1) * D_MODEL) for i in range(3))

    def body(q_ref, kp_ref, kc_ref, vp_ref, vc_ref, do_ref, lse_ref, dl_ref, bias_ref, biast_ref, out_ref, cq_ref, ck_ref, cv_ref):
        g = pl.program_id(0)

        @pl.when(g == n_blocks)
        def _():
            out_ref[:, q_cols] = cq_ref[...].astype(BF16)
            out_ref[:, k_cols] = ck_ref[...].astype(BF16)
            out_ref[:, v_cols] = cv_ref[...].astype(BF16)

        @pl.when(g == 0)
        def _():
            cq_ref[...] = jnp.zeros_like(cq_ref)
            ck_ref[...] = jnp.zeros_like(ck_ref)
            cv_ref[...] = jnp.zeros_like(cv_ref)

        @pl.when(g < n_blocks)
        def _():
            lane = _lane((SPAN, LANES))
            lo_mask = lane < B_HEAD_DIM
            pair = (2 * SPAN, 2 * SPAN)
            first = lax.rem(g, nb) == 0
            prev_key_cols = jnp.logical_and(first, _lane(pair) < SPAN)
            prev_key_rows = jnp.logical_and(first, lax.broadcasted_iota(jnp.int32, pair, 0) < SPAN)
            q = q_ref[...] * jnp.asarray(B_HEAD_DIM**-0.5, BF16)
            kk = jnp.concatenate([kp_ref[...], kc_ref[...]], axis=0)
            vv = jnp.concatenate([vp_ref[...], vc_ref[...]], axis=0)
            do_v = do_ref[...]
            lse_v = lse_ref[...]
            dl_v = dl_ref[...]
            lse_t = lse_v.T
            dl_t = dl_v.T
            halves = (lo_mask.astype(BF16), jnp.logical_not(lo_mask).astype(BF16))
            for j in range(B_HEADS // 2):
                cs = slice(j * LANES, (j + 1) * LANES)
                kp, vp = kk[:, cs], vv[:, cs]
                q2 = _pair_rows(q[:, cs], halves)
                do2 = _pair_rows(do_v[:, cs], halves)
                lse_c, dl_c = _pair_column(lse_v, lane, j), _pair_column(dl_v, lane, j)
                lse_r = jnp.concatenate([lse_t[2 * j : 2 * j + 1], lse_t[2 * j + 1 : 2 * j + 2]], axis=1)
                dl_r = jnp.concatenate([dl_t[2 * j : 2 * j + 1], dl_t[2 * j + 1 : 2 * j + 2]], axis=1)
                sc = lax.dot_general(q2, kp, _NT, preferred_element_type=F32)
                p = jnp.exp(jnp.where(prev_key_cols, NEG, sc + bias_ref[j]) - lse_c)
                dp = lax.dot_general(do2, vp, _NT, preferred_element_type=F32)
                ds = (p * (dp - dl_c)).astype(BF16)
                dq2 = jnp.dot(ds, kp, preferred_element_type=F32)
                sc_t = lax.dot_general(kp, q2, _NT, preferred_element_type=F32)
                p_t = jnp.exp(jnp.where(prev_key_rows, NEG, sc_t + biast_ref[j]) - lse_r)
                dp_t = lax.dot_general(vp, do2, _NT, preferred_element_type=F32)
                ds_t = (p_t * (dp_t - dl_r)).astype(BF16)
                dk_pair = jnp.dot(ds_t, q2, preferred_element_type=F32)
                dv_pair = jnp.dot(p_t.astype(BF16), do2, preferred_element_type=F32)
                oq = slice(j * LANES, (j + 1) * LANES)
                ok = slice(D_MODEL + j * LANES, D_MODEL + (j + 1) * LANES)
                ov = slice(2 * D_MODEL + j * LANES, 2 * D_MODEL + (j + 1) * LANES)
                out_ref[:, oq] = cq_ref[:, cs].astype(BF16)
                out_ref[:, ok] = (ck_ref[:, cs] + dk_pair[:SPAN]).astype(BF16)
                out_ref[:, ov] = (cv_ref[:, cs] + dv_pair[:SPAN]).astype(BF16)
                cq_ref[:, cs] = jnp.where(lo_mask, dq2[:SPAN], dq2[SPAN:]) * (B_HEAD_DIM**-0.5)
                ck_ref[:, cs] = dk_pair[SPAN:]
                cv_ref[:, cs] = dv_pair[SPAN:]

    def block_of(g):
        return jnp.minimum(g, last)

    def row_spec(width):
        return pl.BlockSpec((SPAN, width), lambda g: (block_of(g), 0))

    return pl.pallas_call(
        body,
        name=name,
        grid=(n_blocks + 1,),
        in_specs=[
            *_qkv_specs(block_of),
            row_spec(D_MODEL),
            row_spec(LANES),
            row_spec(LANES),
            pl.BlockSpec((B_HEADS // 2, 2 * SPAN, 2 * SPAN), lambda g: (0, 0, 0)),
            pl.BlockSpec((B_HEADS // 2, 2 * SPAN, 2 * SPAN), lambda g: (0, 0, 0)),
        ],
        out_specs=pl.BlockSpec((SPAN, 3 * D_MODEL), lambda g: (jnp.maximum(g - 1, 0), 0)),
        out_shape=jax.ShapeDtypeStruct((SEQ, 3 * D_MODEL), BF16),
        scratch_shapes=[pltpu.VMEM((SPAN, D_MODEL), F32)] * 3,
        compiler_params=_params(("arbitrary",)),
    )(qkv_p, qkv_p, qkv_p, qkv_p, qkv_p, do_p, lse_p, delta_p, bias, bias_t)


def _position():
    x, y, c = lax.axis_index("x"), lax.axis_index("y"), lax.axis_index("c")
    return x, y, c, 4 * x + 2 * y + c


def _peer(k, x, y, c):
    px = 1 - x if k & 4 else x
    py = 1 - y if k & 2 else y
    pc = 1 - c if k & 1 else c
    return (px, py, pc), 4 * px + 2 * py + pc


def _remote(src, dst, send_sem, recv_sem, device):
    return pltpu.make_async_remote_copy(
        src_ref=src, dst_ref=dst, send_sem=send_sem, recv_sem=recv_sem, device_id=device, device_id_type=MESH
    )


def _silu_bf16(cf):
    return (cf * (1.0 / (1.0 + jnp.exp(-cf)))).astype(BF16)


def ada_exchange(c8, w4, b4, ln8):
    nt, _, ncol = w4.shape

    def body(c8_ref, w_ref, b_ref, ln_ref, cg_ref, lng_ref, mrecv_ref, mloc_ref, send_sems, recv_sems):
        x, y, c, me = _position()
        cg_ref[me] = c8_ref[...]
        lng_ref[me] = ln_ref[...]
        first = []
        for k in range(1, N_DEV):
            dev, _ = _peer(k, x, y, c)
            first.append(_remote(c8_ref, cg_ref.at[me], send_sems.at[0, k], recv_sems.at[0, k], dev))
            first.append(_remote(ln_ref, lng_ref.at[me], send_sems.at[1, k], recv_sems.at[1, k], dev))
        for cp in first:
            cp.start()
        for k in range(1, N_DEV):
            dev, pid = _peer(k, x, y, c)
            _remote(c8_ref, cg_ref.at[pid], send_sems.at[0, k], recv_sems.at[0, k], dev).wait_recv()
            _remote(ln_ref, lng_ref.at[pid], send_sems.at[1, k], recv_sems.at[1, k], dev).wait_recv()
        sc = _silu_bf16(cg_ref[...].reshape(N_DEV * SUBLANES, D_MODEL))
        for t in range(nt):
            mloc_ref[t] = jnp.dot(sc, w_ref[t].astype(BF16), preferred_element_type=F32) + b_ref[t : t + 1, :]

        def group(dev_id):
            return pl.ds(pl.multiple_of(dev_id * SUBLANES, SUBLANES), SUBLANES)

        mrecv_ref[me] = mloc_ref[:, group(me), :]
        second = []
        for k in range(1, N_DEV):
            dev, pid = _peer(k, x, y, c)
            second.append(
                _remote(mloc_ref.at[:, group(pid), :], mrecv_ref.at[me], send_sems.at[2, k], recv_sems.at[2, k], dev)
            )
        for cp in second:
            cp.start()
        for k in range(1, N_DEV):
            dev, pid = _peer(k, x, y, c)
            _remote(
                mloc_ref.at[:, group(pid), :], mrecv_ref.at[pid], send_sems.at[2, k], recv_sems.at[2, k], dev
            ).wait_recv()
        for cp in first + second:
            cp.wait_send()

    return pl.pallas_call(
        body,
        name="ada_exchange",
        in_specs=[VMEM, VMEM, VMEM, VMEM],
        out_specs=[VMEM, VMEM, VMEM],
        out_shape=[
            jax.ShapeDtypeStruct((N_DEV, SUBLANES, D_MODEL), F32),
            jax.ShapeDtypeStruct((N_DEV, SUBLANES, LANES), F32),
            jax.ShapeDtypeStruct((N_DEV, nt, SUBLANES, ncol), F32),
        ],
        scratch_shapes=[
            pltpu.VMEM((nt, N_DEV * SUBLANES, ncol), F32),
            pltpu.SemaphoreType.DMA((3, N_DEV)),
            pltpu.SemaphoreType.DMA((3, N_DEV)),
        ],
        compiler_params=pltpu.CompilerParams(vmem_limit_bytes=VMEM_LIMIT_BYTES),
    )(c8, w4, b4, ln8)


def small_exchange(dmx, flat, pre, after):
    rows = flat.shape[1]

    def body(dmx_ref, flat_ref, pre_ref, _, dmrecv_ref, red_ref, land_ref, send_sems, recv_sems):
        x, y, c, me = _position()
        dmrecv_ref[me] = dmx_ref[me]
        land_ref[me] = flat_ref[me]
        first = []
        for k in range(1, N_DEV):
            dev, pid = _peer(k, x, y, c)
            first.append(_remote(dmx_ref.at[pid], dmrecv_ref.at[me], send_sems.at[0, k], recv_sems.at[0, k], dev))
            first.append(_remote(flat_ref.at[pid], land_ref.at[me], send_sems.at[1, k], recv_sems.at[1, k], dev))
        for cp in first:
            cp.start()
        for k in range(1, N_DEV):
            dev, pid = _peer(k, x, y, c)
            _remote(dmx_ref.at[pid], dmrecv_ref.at[pid], send_sems.at[0, k], recv_sems.at[0, k], dev).wait_recv()
            _remote(flat_ref.at[pid], land_ref.at[pid], send_sems.at[1, k], recv_sems.at[1, k], dev).wait_recv()
        total, total_pre = land_ref[0], pre_ref[0]
        for s in range(1, N_DEV):
            total, total_pre = total + land_ref[s], total_pre + pre_ref[s]
        red_ref[me, :rows, :] = total
        red_ref[me, rows:, :] = total_pre
        second = []
        for k in range(1, N_DEV):
            dev, _ = _peer(k, x, y, c)
            second.append(_remote(red_ref.at[me], red_ref.at[me], send_sems.at[2, k], recv_sems.at[2, k], dev))
        for cp in second:
            cp.start()
        for k in range(1, N_DEV):
            dev, pid = _peer(k, x, y, c)
            _remote(red_ref.at[pid], red_ref.at[pid], send_sems.at[2, k], recv_sems.at[2, k], dev).wait_recv()
        for cp in first + second:
            cp.wait_send()

    return pl.pallas_call(
        body,
        name="small_exchange",
        in_specs=[VMEM, VMEM, VMEM, ANY],
        out_specs=[VMEM, VMEM],
        out_shape=[
            jax.ShapeDtypeStruct(dmx.shape, F32),
            jax.ShapeDtypeStruct((N_DEV, rows + pre.shape[1], LANES), F32),
        ],
        scratch_shapes=[
            pltpu.VMEM(flat.shape, F32),
            pltpu.SemaphoreType.DMA((3, N_DEV)),
            pltpu.SemaphoreType.DMA((3, N_DEV)),
        ],
        compiler_params=pltpu.CompilerParams(vmem_limit_bytes=VMEM_LIMIT_BYTES),
    )(dmx, flat, pre, after)


HBM = pl.BlockSpec(memory_space=pltpu.HBM)
SEM = pl.BlockSpec(memory_space=pltpu.SEMAPHORE)
EFFECT = pltpu.SideEffectType.DATAFLOW_SIDE_EFFECTING


REGROUP_ROWS = 256


def shards_to_columns(x, *, name):
    p, k, n = x.shape

    def body(x_ref, o_ref):
        for s in range(p):
            o_ref[:, s * n : (s + 1) * n] = x_ref[s]

    return pl.pallas_call(
        body,
        name=name,
        grid=(k // REGROUP_ROWS,),
        in_specs=[pl.BlockSpec((p, REGROUP_ROWS, n), lambda i: (0, i, 0))],
        out_specs=pl.BlockSpec((REGROUP_ROWS, p * n), lambda i: (i, 0)),
        out_shape=jax.ShapeDtypeStruct((k, p * n), x.dtype),
        compiler_params=_params(("parallel",)),
    )(x)


def columns_to_shards(xs, *, name):
    k = xs[0].shape[0]
    widths = [x.shape[1] for x in xs]
    n = sum(widths) // N_DEV
    pieces = []
    for s in range(N_DEV):
        start = 0
        for i, w in enumerate(widths):
            lo, hi = max(start, s * n), min(start + w, (s + 1) * n)
            if lo < hi:
                pieces.append((i, lo - start, s, lo - s * n, hi - lo))
            start += w

    def body(*refs):
        x_refs, o_ref = refs[: len(xs)], refs[-1]
        for i, c0, s, d0, w in pieces:
            o_ref[s, :, d0 : d0 + w] = x_refs[i][:, c0 : c0 + w]

    return pl.pallas_call(
        body,
        name=name,
        grid=(k // REGROUP_ROWS,),
        in_specs=[pl.BlockSpec((REGROUP_ROWS, w), lambda i: (i, 0)) for w in widths],
        out_specs=pl.BlockSpec((N_DEV, REGROUP_ROWS, n), lambda i: (0, i, 0)),
        out_shape=jax.ShapeDtypeStruct((N_DEV, k, n), xs[0].dtype),
        compiler_params=_params(("parallel",)),
    )(*xs)


def _own_slot(me, block):
    land = lax.empty((N_DEV, *block.shape), block.dtype)
    return lax.dynamic_update_slice_in_dim(land, block[None], me, axis=0)


N_CHIP_PEERS = 3


class Gather:
    def __init__(self, shards, lands, after, *, name):
        nt = len(shards)
        self.name = name

        def body(*refs):
            src_refs, land_refs = refs[:nt], refs[nt : 2 * nt]
            send_sems, recv_sems = refs[2 * nt + 1 : 3 * nt + 1], refs[3 * nt + 1 : 4 * nt + 1]
            token = refs[-1]
            x, y, c, me = _position()
            for t in range(nt):
                for k, dev in enumerate(self._targets(x, y, c)):
                    _remote(src_refs[t], land_refs[t].at[me], send_sems[t].at[k], recv_sems[t].at[k], dev).start()
            token[...] = jnp.zeros_like(token)

        outs = pl.pallas_call(
            body,
            name=name + "_start",
            in_specs=[HBM] * (2 * nt) + [ANY],
            out_specs=[SEM] * (2 * nt) + [HBM] * (2 * nt) + [VMEM],
            out_shape=[pltpu.SemaphoreType.DMA((1 + N_CHIP_PEERS,))] * (2 * nt)
            + [pltpu.HBM(a.shape, a.dtype) for a in (*shards, *lands)]
            + [jax.ShapeDtypeStruct((SUBLANES, LANES), F32)],
            input_output_aliases={i: 2 * nt + i for i in range(2 * nt)},
            compiler_params=pltpu.CompilerParams(has_side_effects=EFFECT),
        )(*[pltpu.with_memory_space_constraint(a, pltpu.HBM) for a in (*shards, *lands)], after)
        self.send_sems, self.recv_sems = list(outs[:nt]), list(outs[nt : 2 * nt])
        self.srcs, self.lands = list(outs[2 * nt : 3 * nt]), list(outs[3 * nt : 4 * nt])
        self.token = outs[-1]

    @staticmethod
    def _chips(x, y):
        return [(1 - x, y), (x, 1 - y), (1 - x, 1 - y)]

    @classmethod
    def _targets(cls, x, y, c):
        return [(x, y, 1 - c)] + [(*chip, c) for chip in cls._chips(x, y)]

    def zero(self):
        return self.token[0, 0]

    @staticmethod
    def _slot(px, py, pc):
        return 4 * px + 2 * py + pc

    def pass_on(self, which, after, *, name):
        n = len(which)

        def pass_body(*refs):
            land_refs, recv_sems = refs[:n], refs[n : 2 * n]
            fwd_send, fwd_recv = refs[3 * n + 1 : 4 * n + 1], refs[4 * n + 1 : 5 * n + 1]
            token = refs[-1]
            x, y, c, _ = _position()
            for t in range(n):
                for j, chip in enumerate(self._chips(x, y)):
                    blk = land_refs[t].at[self._slot(*chip, c)]
                    _remote(blk, blk, fwd_send[t].at[j], recv_sems[t].at[1 + j], (*chip, c)).wait_recv()
                    _remote(blk, blk, fwd_send[t].at[j], fwd_recv[t].at[j], (x, y, 1 - c)).start()
            token[...] = jnp.zeros_like(token)

        lands = [self.lands[t] for t in which]
        outs = pl.pallas_call(
            pass_body,
            name=name,
            in_specs=[HBM] * n + [SEM] * n + [ANY],
            out_specs=[HBM] * n + [SEM] * (2 * n) + [VMEM],
            out_shape=[pltpu.HBM(a.shape, a.dtype) for a in lands]
            + [pltpu.SemaphoreType.DMA((N_CHIP_PEERS,))] * (2 * n)
            + [jax.ShapeDtypeStruct((SUBLANES, LANES), F32)],
            input_output_aliases={i: i for i in range(n)},
            compiler_params=pltpu.CompilerParams(has_side_effects=EFFECT),
        )(*lands, *[self.recv_sems[t] for t in which], after)
        return (which, outs[:n], outs[n : 2 * n], outs[2 * n : 3 * n]), outs[-1]

    def wait(self, which, after, *, name):
        return self.finish(self.pass_on(which, after, name=name + "_pass")[0], after, name=name)

    def finish(self, passed, after, *, name):
        which, lands, fwd_send, fwd_recv = passed
        n = len(which)
        slot = self._slot

        def wait_body(*refs):
            src_refs, land_refs = refs[:n], refs[n : 2 * n]
            send_sems, recv_sems = refs[2 * n : 3 * n], refs[3 * n : 4 * n]
            fwd_send, fwd_recv = refs[4 * n : 5 * n], refs[5 * n : 6 * n]
            x, y, c, me = _position()
            sibling = (x, y, 1 - c)
            for t in range(n):
                for k, dev in enumerate(self._targets(x, y, c)):
                    _remote(src_refs[t], land_refs[t].at[me], send_sems[t].at[k], recv_sems[t].at[k], dev).wait_send()
                blk = land_refs[t].at[slot(x, y, 1 - c)]
                _remote(blk, blk, send_sems[t].at[0], recv_sems[t].at[0], sibling).wait_recv()
                for j, chip in enumerate(self._chips(x, y)):
                    sent = land_refs[t].at[slot(*chip, c)]
                    _remote(sent, sent, fwd_send[t].at[j], fwd_recv[t].at[j], sibling).wait_send()
                    got = land_refs[t].at[slot(*chip, 1 - c)]
                    _remote(got, got, fwd_send[t].at[j], fwd_recv[t].at[j], sibling).wait_recv()

        srcs = [self.srcs[t] for t in which]
        outs = pl.pallas_call(
            wait_body,
            name=name,
            in_specs=[HBM] * (2 * n) + [SEM] * (4 * n) + [ANY],
            out_specs=[HBM] * (2 * n),
            out_shape=[pltpu.HBM(a.shape, a.dtype) for a in (*srcs, *lands)],
            input_output_aliases={i: i for i in range(2 * n)},
            compiler_params=pltpu.CompilerParams(has_side_effects=EFFECT),
        )(
            *srcs, *lands, *[self.send_sems[t] for t in which], *[self.recv_sems[t] for t in which], *fwd_send,
            *fwd_recv, after,
        )
        return outs[n:]


class Scatter:
    def __init__(self, srcs, after, *, name):
        self.name = name
        nt = self.nt = len(srcs)
        peers = N_DEV - 1
        lands = [lax.empty(a.shape, a.dtype) for a in srcs]

        def body(*refs):
            src_refs, land_refs = refs[:nt], refs[nt : 2 * nt]
            send_sems, recv_sems = refs[2 * nt + 1 : 3 * nt + 1], refs[3 * nt + 1 : 4 * nt + 1]
            token = refs[-1]
            x, y, c, me = _position()
            for t in range(nt):
                for k in range(1, N_DEV):
                    dev, pid = _peer(k, x, y, c)
                    src = src_refs[t].at[pid]
                    _remote(src, land_refs[t].at[me], send_sems[t].at[k - 1], recv_sems[t].at[k - 1], dev).start()
                self._own(src_refs[t], land_refs[t], send_sems[t], me).start()
            token[...] = jnp.zeros_like(token)

        outs = pl.pallas_call(
            body,
            name=name + "_start",
            in_specs=[HBM] * (2 * nt) + [ANY],
            out_specs=[SEM] * (2 * nt) + [HBM] * (2 * nt) + [VMEM],
            out_shape=[pltpu.SemaphoreType.DMA((peers + 1,))] * (2 * nt)
            + [pltpu.HBM(a.shape, a.dtype) for a in (*srcs, *lands)]
            + [jax.ShapeDtypeStruct((SUBLANES, LANES), F32)],
            input_output_aliases={i: 2 * nt + i for i in range(2 * nt)},
            compiler_params=pltpu.CompilerParams(has_side_effects=EFFECT),
        )(*[pltpu.with_memory_space_constraint(a, pltpu.HBM) for a in (*srcs, *lands)], after)
        self.send_sems, self.recv_sems = outs[:nt], outs[nt : 2 * nt]
        self.srcs, self.lands = outs[2 * nt : 3 * nt], outs[3 * nt : 4 * nt]
        self.token = outs[-1]

    @staticmethod
    def _own(src_ref, land_ref, sems, me):
        return pltpu.make_async_copy(src_ref.at[me], land_ref.at[me], sems.at[N_DEV - 1])

    def zero(self):
        return self.token[0, 0]

    def wait(self, which, after, *, name):
        n = len(which)

        def body(*refs):
            src_refs, land_refs = refs[:n], refs[n : 2 * n]
            send_sems, recv_sems = refs[2 * n : 3 * n], refs[3 * n : 4 * n]
            x, y, c, me = _position()
            for t in range(n):
                for k in range(1, N_DEV):
                    dev, pid = _peer(k, x, y, c)
                    src = src_refs[t].at[pid]
                    cp = _remote(src, land_refs[t].at[pid], send_sems[t].at[k - 1], recv_sems[t].at[k - 1], dev)
                    cp.wait_send()
                    cp.wait_recv()
                self._own(src_refs[t], land_refs[t], send_sems[t], me).wait()

        srcs = [self.srcs[t] for t in which]
        lands = [self.lands[t] for t in which]
        outs = pl.pallas_call(
            body,
            name=name,
            in_specs=[HBM] * (2 * n) + [SEM] * (2 * n) + [ANY],
            out_specs=[HBM] * (2 * n),
            out_shape=[pltpu.HBM(a.shape, a.dtype) for a in (*srcs, *lands)],
            input_output_aliases={i: i for i in range(2 * n)},
            compiler_params=pltpu.CompilerParams(has_side_effects=EFFECT),
        )(*srcs, *lands, *[self.send_sems[t] for t in which], *[self.recv_sems[t] for t in which], after)
        return outs[n:]


def _adam_update(g, w, m, v):
    m2 = ADAM_B1 * m + (1.0 - ADAM_B1) * g
    v2 = ADAM_B2 * v + (1.0 - ADAM_B2) * jnp.square(g)
    m_hat = m2 / (1.0 - ADAM_B1**ADAM_STEP)
    v_hat = v2 / (1.0 - ADAM_B2**ADAM_STEP)
    delta = -ADAM_LR * (m_hat / (jnp.sqrt(v_hat) + ADAM_EPS) + ADAM_WD * w)
    return delta, m2, v2


def adamw(gparts, w, m, v, *, name):
    nl, r, c = w.shape
    p = gparts[0].shape[0]
    tr = r if r <= 256 else (256 if c <= D_MODEL else 128)
    ni = r // tr

    def body(*refs):
        g_refs = refs[:nl]
        w_ref, m_ref, v_ref, go_ref, d_ref, mo_ref, vo_ref = refs[nl:]
        for layer in range(nl):

            @pl.when(pl.program_id(0) == layer)
            def _(g_ref=g_refs[layer]):
                g = g_ref[0].astype(F32)
                for i in range(1, p):
                    g = g + g_ref[i].astype(F32)
                delta, m2, v2 = _adam_update(g, w_ref[...], m_ref[...], v_ref[...])
                go_ref[...] = g
                d_ref[...] = delta
                mo_ref[...] = m2
                vo_ref[...] = v2

    def parts_spec(layer):
        def index(l, i):
            return (0, jnp.where(l == layer, i, jnp.where(l < layer, 0, ni - 1)), 0)

        return pl.BlockSpec((p, tr, c), index)

    blk = pl.BlockSpec((None, tr, c), lambda l, i: (l, i, 0))
    return pl.pallas_call(
        body,
        name=name,
        grid=(nl, ni),
        in_specs=[*[parts_spec(layer) for layer in range(nl)], blk, blk, blk],
        out_specs=[blk] * 4,
        out_shape=[jax.ShapeDtypeStruct((nl, r, c), F32)] * 4,
        compiler_params=_params(("arbitrary", "arbitrary")),
    )(*gparts, w, m, v)


def adamw_small(items, *, name):
    n = len(items)

    def body(*refs):
        ins, outs = refs[: 4 * n], refs[4 * n :]
        for t in range(n):
            g_ref, w_ref, m_ref, v_ref = ins[4 * t : 4 * t + 4]
            g = g_ref[...]
            delta, m2, v2 = _adam_update(g, w_ref[...], m_ref[...], v_ref[...])
            for o_ref, val in zip(outs[4 * t : 4 * t + 4], (g, delta, m2, v2), strict=True):
                o_ref[...] = val

    outs = pl.pallas_call(
        body,
        name=name,
        out_shape=[jax.ShapeDtypeStruct(item[1].shape, F32) for item in items for _ in range(4)],
    )(*[a for item in items for a in item])
    return [outs[4 * t : 4 * t + 4] for t in range(n)]


def ada_grad_adamw(cg, dmrecv, w4, m4, v4, *, name):
    nt, k, ncol = w4.shape

    def body(cg_ref, dm_ref, w_ref, m_ref, v_ref, go_ref, d_ref, mo_ref, vo_ref, gb_ref):
        sc = _silu_bf16(cg_ref[...].reshape(N_DEV * SUBLANES, k))
        dm = dm_ref[...].reshape(N_DEV * SUBLANES, ncol)
        g = lax.dot_general(sc, dm.astype(BF16), (((0,), (0,)), ((), ())), preferred_element_type=F32)
        delta, m2, v2 = _adam_update(g, w_ref[...], m_ref[...], v_ref[...])
        go_ref[...] = g
        d_ref[...] = delta
        mo_ref[...] = m2
        vo_ref[...] = v2
        gb_ref[...] = jnp.broadcast_to(jnp.sum(dm, axis=0, keepdims=True), (SUBLANES, ncol))

    wblk = pl.BlockSpec((None, k, ncol), lambda t: (t, 0, 0))
    return pl.pallas_call(
        body,
        name=name,
        grid=(nt,),
        in_specs=[
            pl.BlockSpec((N_DEV, SUBLANES, k), lambda t: (0, 0, 0)),
            pl.BlockSpec((N_DEV, None, SUBLANES, ncol), lambda t: (0, t, 0, 0)),
            wblk,
            wblk,
            wblk,
        ],
        out_specs=[wblk] * 4 + [pl.BlockSpec((None, SUBLANES, ncol), lambda t: (t, 0, 0))],
        out_shape=[jax.ShapeDtypeStruct((nt, k, ncol), F32)] * 4 + [jax.ShapeDtypeStruct((nt, SUBLANES, ncol), F32)],
        compiler_params=_params(("parallel",)),
    )(cg, dmrecv, w4, m4, v4)


def kernel(x, c, ada_w, ada_b, ln_g, ln_b, a_w_in, a_b_in, a_vn_g, a_vn_b, a_w_s, a_b_s, a_w_out, b_w_qkv, b_w_out, mlp_w_up, mlp_w_down, loss_target, m_ada_w, m_ada_b, m_ln_g, m_ln_b, m_a_w_in, m_a_b_in, m_a_vn_g, m_a_vn_b, m_a_w_s, m_a_b_s, m_a_w_out, m_b_w_qkv, m_b_w_out, m_mlp_w_up, m_mlp_w_down, v_ada_w, v_ada_b, v_ln_g, v_ln_b, v_a_w_in, v_a_b_in, v_a_vn_g, v_a_vn_b, v_a_w_s, v_a_b_s, v_a_w_out, v_b_w_qkv, v_b_w_out, v_mlp_w_up, v_mlp_w_down):
    x0 = x[0]
    target = loss_target[0]
    me = 4 * lax.axis_index("x") + 2 * lax.axis_index("y") + lax.axis_index("c")

    ada_w4 = ada_w.reshape(N_SUB, D_MODEL, -1)
    ada_b4 = ada_b.reshape(N_SUB, -1)
    ln8 = jnp.concatenate([ln_g.reshape(N_SUB, -1), ln_b.reshape(N_SUB, -1)], axis=0)
    c8 = jnp.broadcast_to(c, (SUBLANES, D_MODEL))
    cg, lng, mrecv = ada_exchange(c8, ada_w4, ada_b4, ln8)

    W_IN, W_AOUT, W_UP0, W_DN0, W_QKV, W_BOUT, W_UP1, W_DN1 = range(8)
    shards = [
        a_w_in[0].astype(BF16),
        a_w_out[0].astype(BF16),
        mlp_w_up[0].astype(BF16),
        mlp_w_down[0].astype(BF16),
        b_w_qkv[0].astype(BF16),
        b_w_out[0].astype(BF16),
        mlp_w_up[1].astype(BF16),
        mlp_w_down[1].astype(BF16),
    ]
    gather = Gather(shards, [_own_slot(me, s) for s in shards], mrecv, name="gather")

    modv = mrecv[:, :, 0, :].transpose(1, 0, 2).reshape(N_SUB, 3 * D_MODEL) + gather.zero()
    shift = [modv[t : t + 1, :D_MODEL] for t in range(N_SUB)]
    scale = [modv[t : t + 1, D_MODEL : 2 * D_MODEL] for t in range(N_SUB)]
    gate1 = [1.0 + modv[t : t + 1, 2 * D_MODEL :] for t in range(N_SUB)]
    lng_full = [lng[:, t, :].reshape(1, D_MODEL) for t in range(N_SUB)]
    lnb_full = [lng[:, N_SUB + t, :].reshape(1, D_MODEL) for t in range(N_SUB)]

    ident = lambda acc: (acc,)
    def relu2(a):
        r = jnp.maximum(a, jnp.zeros_like(a))
        return r * r
    vn_g, vn_b, w_s = a_vn_g, a_vn_b, a_w_s[0]
    bias_full = jnp.repeat(a_b_s[0].T, A_GROUP_DIM, axis=1)
    w_up3, w_dn3 = [None, None], [None, None]

    def mlp_forward(i, h, up, dn, x_in=None, t_next=None):
        w_up3[i], w_dn3[i] = up, dn.reshape(1, D_FF, D_MODEL)
        (a,) = mm_nn(h, w_up3[i], name=f"mlp{i}_up", tm=2048, ps=2, tn=512, tk=D_MODEL, epilogue=ident, outs=(BF16,))
        if t_next is None:
            (y,) = mm_nn(
                a, w_dn3[i], name=f"mlp{i}_down", tm=1024, ps=1, tn=512, tk=D_FF, prologue=relu2, epilogue=ident, outs=(BF16,)
            )
            return a, y
        y, xn, hn = mm_nn(
            a, w_dn3[i], name=f"mlp{i}_down", tm=512, ps=1, tn=D_MODEL, tk=D_FF, prologue=relu2,
            epilogue=residual_ln_epilogue, extras=residual_extras(x_in, t_next - 1), outs=(BF16, F32, BF16),
        )
        return a, y, xn, hn

    def residual_extras(x_in, t):
        rows = (gate1[t], lng_full[t], lnb_full[t], scale[t + 1], shift[t + 1])
        return [(x_in, "full")] + [(r, "row") for r in rows]

    h0 = modulate(x0, scale[0], shift[0], name="modulate0")
    w_in3, w_aout3 = gather.wait([W_IN, W_AOUT], h0, name="gather_wait_a")
    w_aout3 = w_aout3.reshape(1, D_MODEL, D_MODEL)
    (a_pre,) = mm_nn(
        h0, w_in3, name="a_in", tm=2048, ps=4, tn=256, tk=D_MODEL, epilogue=lambda acc, b: (acc + b,),
        extras=[(a_b_in, "row")], outs=(BF16,),
    )
    p_gate = gate_fwd(a_pre, vn_g, vn_b, w_s, bias_full, name="gate_fwd")
    y0, x1, h1 = mm_nn(
        p_gate, w_aout3, name="a_out", tm=1024, ps=1, tn=D_MODEL, tk=D_MODEL, epilogue=residual_ln_epilogue,
        extras=residual_extras(x0, 0), outs=(BF16, F32, BF16),
    )
    w_mlp0 = gather.wait([W_UP0, W_DN0], y0, name="gather_wait_mlp0")
    a1, y1, x2, h2 = mlp_forward(0, h1, *w_mlp0, x_in=x1, t_next=2)
    (w_qkv_shards,) = gather.wait([W_QKV], y1, name="gather_wait_qkv")
    w_qkv3 = shards_to_columns(w_qkv_shards, name="w_qkv_columns")[None]
    pat_tiles = 3
    dil = [d for _, d in B_PATTERNS]
    qkv_p, pat_o, pat_lse = [], [], []
    for g in range(N_PAT):
        (qkv_g,) = mm_nn(
            h2, w_qkv3, name=f"b_qkv{g}", tm=2048, ps=1, tn=D_MODEL, tk=D_MODEL, epilogue=ident, outs=(BF16,),
            b_tile0=pat_tiles * g, b_tiles=pat_tiles, out_streams=dil[g],
        )
        o_g, lse_g = attn_fwd(qkv_g, g, name=f"attn_fwd{g}", after=rest_token if g == 1 else None)
        if g == 0:
            rest_passed, rest_token = gather.pass_on([W_BOUT, W_UP1, W_DN1], lse_g, name="gather_pass_rest")
        qkv_p.append(qkv_g)
        pat_o.append(o_g)
        pat_lse.append(lse_g)
    o_b, o_f, *lse_pat = attn_combine(pat_o, pat_lse, name="attn_combine")
    lse_pat = [x.reshape(SEQ, LANES) for x in lse_pat]
    w_bout3, *w_mlp1 = gather.finish(rest_passed, lse_pat[0], name="gather_wait_rest")
    w_bout3 = w_bout3.reshape(1, D_MODEL, D_MODEL)
    y2, x3, h3 = mm_nn(
        o_b, w_bout3, name="b_out", tm=1024, ps=1, tn=D_MODEL, tk=D_MODEL, epilogue=residual_ln_epilogue,
        extras=residual_extras(x2, 2), outs=(BF16, F32, BF16),
    )
    a3, y3 = mlp_forward(1, h3, *w_mlp1)
    dz3, dy3, st3, loss_local = residual_ln_loss_bwd(
        x3, y3, gate1[3], lng_full[3], lnb_full[3], target, name="res_ln3_loss_bwd"
    )

    def scatter(parts, after, name):
        return Scatter([p.reshape(N_DEV, -1, p.shape[-1]) for p in parts], after, name=name)

    xs_in, ys = [x0, x1, x2, x3], [y0, y1, y2, y3]

    def residual_bwd_extras(dz_later, t, zero):
        return [
            (dz_later, "full"), (scale[t + 1] + zero, "row"), (xs_in[t], "full"), (ys[t], "full"),
            (gate1[t], "row"), (lng_full[t], "row"), (lnb_full[t], "row"),
        ]

    def mlp_backward(i, h, a, dy, dz_later, t):
        (da,) = mm_nt(
            dy,
            w_dn3[i],
            name=f"mlp{i}_da",
            tm=2048,
            tko=1024,
            ps=1,
            tc=D_MODEL,
            epilogue=lambda acc, act: (acc * (2.0 * jnp.maximum(act.astype(F32), 0.0)),),
            extras=[(a, "full")],
            outs=(BF16,),
        )
        dw_dn = mm_tn(
            a, dy, name=f"mlp{i}_dw_down", p=1, tk=1024, ps=1, tn=D_MODEL, tmc=2048, prologue=relu2, out_dtype=BF16
        )
        dw_up = mm_tn(h, da, name=f"mlp{i}_dw_up", p=N_DEV, tk=1024, ps=2, tn=512, tmc=2048, out_dtype=BF16)
        rs = scatter([dw_up, dw_dn], da, f"scatter_mlp{i}")
        dz, dy_before, st = mm_nt(
            da, w_up3[i], name=f"mlp{i}_dh", tm=512, tko=D_MODEL, ps=N_DEV, tc=512, epilogue=residual_bwd_epilogue,
            extras=residual_bwd_extras(dz_later, t, rs.zero()), outs=(F32, BF16), stats=True,
        )
        return rs, dz, dy_before, st

    rs_mlp1, dz2, dy2, st2 = mlp_backward(1, h3, a3, dy3, dz3, 2)
    do_delta = mm_nt(
        dy2, w_bout3, name="b_do", tm=1024, tko=D_MODEL, ps=1, tc=D_MODEL, extras=[(o_f, "full")],
        epilogue=lambda acc, o: (acc,) * N_PAT + (_reduce_groups(acc * o),) * N_PAT,
        outs=(BF16,) * N_PAT + (F32,) * N_PAT, out_widths=[D_MODEL] * N_PAT + [LANES] * N_PAT, out_streams=dil + dil,
    )
    do_pat = [x.reshape(SEQ, D_MODEL) for x in do_delta[:N_PAT]]
    delta_pat = [x.reshape(SEQ, LANES) for x in do_delta[N_PAT:]]
    dh2, dw_pat = None, []
    for g in range(N_PAT):
        dqkv_g = attn_bwd(qkv_p[g], do_pat[g], lse_pat[g], delta_pat[g], g, name=f"attn_bwd{g}")
        dw_pat.append(
            mm_tn(
                h2, dqkv_g, name=f"b_dw_qkv{g}", p=1, tk=1024, ps=1, tn=D_MODEL, tmc=2048, out_dtype=BF16,
                g_streams=dil[g],
            )[0]
        )
        if g == N_PAT - 1:
            break
        (dh2,) = mm_nt(
            dqkv_g, w_qkv3, name=f"b_dh{g}", tm=2048, tko=512, ps=1, tc=pat_tiles * D_MODEL, outs=(F32,),
            b_tile0=g, g_streams=dil[g],
            epilogue=ident if g == 0 else (lambda acc, prev, d=dil[g]: (_from_streams(acc, d) + prev,)),
            extras=[] if g == 0 else [(dh2, "full")],
        )
    dw_bout = mm_tn(o_b, dy2, name="b_dw_out", p=1, tk=1024, ps=1, tn=D_MODEL, tmc=2048, out_dtype=BF16)
    dw_qkv = columns_to_shards(dw_pat, name="dw_qkv_shards")
    rs_b = scatter([dw_qkv, dw_bout], dqkv_g, "scatter_b")
    dz1, dy1, st1 = mm_nt(
        dqkv_g, w_qkv3, name=f"b_dh{N_PAT - 1}", tm=512, tko=D_MODEL, ps=1, tc=pat_tiles * D_MODEL, outs=(F32, BF16),
        b_tile0=N_PAT - 1, g_streams=dil[-1], stats=True,
        epilogue=lambda acc, prev, *rest: residual_bwd_epilogue(_from_streams(acc, dil[-1]) + prev, *rest),
        extras=[(dh2, "full")] + residual_bwd_extras(dz2, 1, rs_b.zero()),
    )
    rs_mlp0, dz0, dy0, st0 = mlp_backward(0, h1, a1, dy1, dz1, 0)
    (dp_gate,) = mm_nt(dy0, w_aout3, name="a_dp", tm=2048, tko=1024, ps=1, tc=D_MODEL, epilogue=ident, outs=(F32,))
    dw_aout = mm_tn(p_gate, dy0, name="a_dw_out", p=1, tk=1024, ps=1, tn=D_MODEL, tmc=2048, out_dtype=BF16)
    rs_aout = scatter([dw_aout], dp_gate, "scatter_a_out")
    da0, d_ws, d_bs, gate_rows = gate_bwd(a_pre, dp_gate, vn_g + rs_aout.zero(), vn_b, w_s, bias_full, name="gate_bwd")
    rs_ws = scatter([d_ws], gate_rows, "scatter_a_w_s")
    dw_in = mm_tn(h0, da0, name="a_dw_in", p=N_DEV, tk=1024, ps=4, tn=256, tmc=2048, out_dtype=BF16)
    rs_in = scatter([dw_in], rs_ws.token, "scatter_a_in")
    grad_x, stf = mm_nt(
        da0, w_in3, name="a_dh", tm=1024, tko=D_MODEL, ps=N_DEV, tc=256, epilogue=input_bwd_epilogue,
        extras=[(dz0, "full"), (scale[0] + rs_in.zero(), "row"), (x0, "full")], outs=(F32,), stats=True,
    )

    results = {}

    def update(wname, gparts, w, m, v):
        shape = w.shape
        layers = len(gparts) if isinstance(gparts, list) else 1
        w3 = w.reshape(layers, -1, shape[-1])
        parts = [g.reshape(g.shape[0], *w3.shape[1:]) for g in (gparts if layers > 1 else [gparts])]
        outs = adamw(parts, w3, m.reshape(w3.shape), v.reshape(w3.shape), name=f"adamw_{wname}")
        results[wname] = [o.reshape(shape) for o in outs]

    g_up1, g_dn1 = rs_mlp1.wait([0, 1], grad_x, name="scatter_wait_mlp1")
    g_qkv, g_bout = rs_b.wait([0, 1], grad_x, name="scatter_wait_b")
    update("b_w_qkv", g_qkv, b_w_qkv, m_b_w_qkv, v_b_w_qkv)
    update("b_w_out", g_bout, b_w_out, m_b_w_out, v_b_w_out)
    g_up0, g_dn0 = rs_mlp0.wait([0, 1], grad_x, name="scatter_wait_mlp0")
    update("mlp_w_up", [g_up0, g_up1], mlp_w_up, m_mlp_w_up, v_mlp_w_up)
    update("mlp_w_down", [g_dn0, g_dn1], mlp_w_down, m_mlp_w_down, v_mlp_w_down)

    stats_after = [stf, st0, st1, st2]
    stats_own = [st0, st1, st2, st3]
    dm = jnp.stack(
        [
            jnp.concatenate(
                [stats_after[t][ST_DSHIFT], stats_after[t][ST_DSCALE], stats_own[t][ST_DGATE]], axis=0
            )
            for t in range(N_SUB)
        ]
    )
    ncol = 3 * D_MODEL // N_DEV
    dmx = jnp.pad(
        dm.reshape(N_SUB, N_DEV, ncol).transpose(1, 0, 2)[:, :, None, :], ((0, 0), (0, 0), (0, SUBLANES - 1), (0, 0))
    )
    small = [
        gate_rows[0],
        gate_rows[1],
        d_bs[:, :A_GROUPS].T.reshape(-1),
        *[stats_own[t][ST_DG] for t in range(N_SUB)],
        *[stats_own[t][ST_DB] for t in range(N_SUB)],
        jnp.pad(loss_local.reshape(1), (0, LANES - 1)),
    ]
    n_small = sum(s.size for s in small)
    part_rows = -(-n_small // (N_DEV * LANES * SUBLANES)) * SUBLANES
    flat = jnp.concatenate(small + [jnp.zeros((N_DEV * part_rows * LANES - n_small,), F32)])
    (ws_parts,) = rs_ws.wait([0], results["mlp_w_down"][3], name="scatter_wait_a_w_s")
    dmrecv, reduced = small_exchange(dmx, flat.reshape(N_DEV, part_rows, LANES), ws_parts, results["mlp_w_down"][3])
    g_ws = reduced[:, part_rows:, :].reshape(-1)
    reduced = reduced[:, :part_rows, :].reshape(-1)
    sizes = [2 * D_MODEL, D_MODEL, D_MODEL, A_GROUPS * CHUNK, N_SUB * D_MODEL, N_SUB * D_MODEL]
    offs = [sum(sizes[:i]) for i in range(len(sizes) + 1)]
    g_b_in, g_vn_g, g_vn_b, g_bs, g_lng, g_lnb = [reduced[offs[i] : offs[i + 1]] for i in range(len(sizes))]
    loss = reduced[offs[-1]]

    ada_outs = ada_grad_adamw(cg, dmrecv, ada_w4, m_ada_w.reshape(ada_w4.shape), v_ada_w.reshape(ada_w4.shape), name="ada_grad_adamw")
    results["ada_w"] = [o.reshape(ada_w.shape) for o in ada_outs[:4]]
    ln_cols = D_MODEL // N_DEV
    my_ln = lambda gfull: lax.dynamic_slice_in_dim(gfull.reshape(N_SUB, N_DEV, ln_cols), me, 1, axis=1)
    small_params = [
        ("ada_b", ada_outs[4][:, 0, :], ada_b, m_ada_b, v_ada_b),
        ("ln_g", my_ln(g_lng), ln_g, m_ln_g, v_ln_g),
        ("ln_b", my_ln(g_lnb), ln_b, m_ln_b, v_ln_b),
        ("a_b_in", g_b_in, a_b_in, m_a_b_in, v_a_b_in),
        ("a_vn_g", g_vn_g, a_vn_g, m_a_vn_g, v_a_vn_g),
        ("a_vn_b", g_vn_b, a_vn_b, m_a_vn_b, v_a_vn_b),
        ("a_b_s", g_bs, a_b_s, m_a_b_s, v_a_b_s),
    ]
    small_outs = adamw_small(
        [[a.reshape(-1, w.shape[-1]) for a in (g, w, m, v)] for _, g, w, m, v in small_params], name="adamw_small"
    )
    for (wname, _, w, _, _), outs in zip(small_params, small_outs, strict=True):
        results[wname] = [o.reshape(w.shape) for o in outs]
    update("a_w_s", g_ws[None], a_w_s, m_a_w_s, v_a_w_s)
    (g_aout,) = rs_aout.wait([0], grad_x, name="scatter_wait_a_out")
    (g_in,) = rs_in.wait([0], grad_x, name="scatter_wait_a_in")
    update("a_w_in", g_in, a_w_in, m_a_w_in, v_a_w_in)
    update("a_w_out", g_aout, a_w_out, m_a_w_out, v_a_w_out)

    order = ["ada_w", "ada_b", "ln_g", "ln_b", "a_w_in", "a_b_in", "a_vn_g", "a_vn_b", "a_w_s", "a_b_s", "a_w_out", "b_w_qkv", "b_w_out", "mlp_w_up", "mlp_w_down"]
    return (loss, grad_x[None], *[results[n][0] for n in order], *[results[n][1] for n in order],
            *[results[n][2] for n in order], *[results[n][3] for n in order])
```

```python
import math

import jax
import jax.numpy as jnp
import numpy as np
from jax import lax
from jax.experimental import pallas as pl
from jax.experimental.pallas import tpu as pltpu

F32 = jnp.float32
BF16 = jnp.bfloat16
MESH = pl.DeviceIdType.MESH
ANY = pl.BlockSpec(memory_space=pl.ANY)
VMEM = pl.BlockSpec(memory_space=pltpu.VMEM)

N_DEV = 8
D_MODEL = 1024
SEQ = 4096
DEPTH = 2
CHUNK = 128
A_GROUPS = 16
A_GROUP_DIM = D_MODEL // A_GROUPS
B_HEADS = 16
B_HEAD_DIM = 64
B_PATTERNS = ((128, 1), (512, 4), (2048, 16))
N_PAT = len(B_PATTERNS)
SPAN = 128
D_FF = 4 * D_MODEL
D_QKV = N_PAT * 3 * D_MODEL
ALPHA = (2 * DEPTH) ** 0.25
LN_EPS = 1e-5
NEG = -1e30
ADAM_LR = 0.001
ADAM_B1 = 0.9
ADAM_B2 = 0.999
ADAM_EPS = 1e-08
ADAM_WD = 0.01
ADAM_STEP = 10
GELU_C = math.sqrt(2.0 / math.pi)
GELU_A = 0.044715

VMEM_LIMIT_BYTES = 56 * 1024 * 1024
LANES = 128
SUBLANES = 8
ROW_TILE = 512
N_SUB = 2 * DEPTH


def _params(sem):
    return pltpu.CompilerParams(dimension_semantics=sem, vmem_limit_bytes=VMEM_LIMIT_BYTES)


def _lane(shape):
    return lax.broadcasted_iota(jnp.int32, shape, len(shape) - 1)


def _split_bf16(x):
    hi = x.astype(BF16)
    lo = (x - hi.astype(F32)).astype(BF16)
    return hi, lo


def _group_expand_matrix(groups_padded, width):
    per = width // A_GROUPS
    r = lax.broadcasted_iota(jnp.int32, (groups_padded, width), 0)
    c = lax.broadcasted_iota(jnp.int32, (groups_padded, width), 1)
    return (c // per == r).astype(BF16)


def _group_reduce_matrix(width, groups_padded):
    per = width // A_GROUPS
    r = lax.broadcasted_iota(jnp.int32, (width, groups_padded), 0)
    c = lax.broadcasted_iota(jnp.int32, (width, groups_padded), 1)
    return (r // per == c).astype(BF16)


def _expand_groups(w):
    e = _group_expand_matrix(LANES, D_MODEL)
    hi, lo = _split_bf16(w)
    return jnp.dot(hi, e, preferred_element_type=F32) + jnp.dot(lo, e, preferred_element_type=F32)


def _reduce_groups(x):
    e = _group_reduce_matrix(D_MODEL, LANES)
    hi, lo = _split_bf16(x)
    return jnp.dot(hi, e, preferred_element_type=F32) + jnp.dot(lo, e, preferred_element_type=F32)


def _to_streams(x, d):
    rows, w = x.shape
    return jnp.swapaxes(x.reshape(rows // d, d, w), 0, 1).reshape(rows, w)


def _from_streams(x, d):
    rows, w = x.shape
    return jnp.swapaxes(x.reshape(d, rows // d, w), 0, 1).reshape(rows, w)


def _column_tiles(p, n, ps, tn):
    assert (ps == 1 or tn == n) and p % ps == 0 and n % tn == 0
    q = n // tn
    return (p // ps) * q, q


def _extra_specs(extras, tm, width):
    specs = []
    for _, kind in extras:
        if kind == "row":
            specs.append(pl.BlockSpec((1, width), lambda i, j, c: (0, j)))
        else:
            specs.append(pl.BlockSpec((tm, width), lambda i, j, c: (i, j)))
    return specs


def mm_nn(a, b3, *, name, tm, ps, tn, tk, epilogue, extras=(), outs, prologue=None, b_tile0=0, b_tiles=None, out_streams=1):
    m, k = a.shape
    p, _, n = b3.shape
    nj, q = _column_tiles(p, n, ps, tn)
    nj = nj if b_tiles is None else b_tiles
    nk = k // tk
    width = ps * tn
    d = out_streams
    assert d == 1 or not extras

    def body(a_ref, b_ref, *rest):
        ex = rest[: len(extras)]
        out_refs = rest[len(extras) : len(extras) + len(outs)]
        kk = pl.program_id(2)
        av = a_ref[...] if prologue is None else prologue(a_ref[...])
        if d > 1:
            av = _to_streams(av, d)

        def finish(cs, acc):
            res = epilogue(acc, *[e[:, cs] for e in ex])
            for o_ref, r in zip(out_refs, res, strict=True):
                if d > 1:
                    o_ref[:, :, cs] = r.astype(o_ref.dtype).reshape(d, tm // d, tn)
                else:
                    o_ref[:, cs] = r.astype(o_ref.dtype)

        for s in range(ps):
            cs = slice(s * tn, (s + 1) * tn)
            part = jnp.dot(av, b_ref[s], preferred_element_type=F32)
            if nk == 1:
                finish(cs, part)
                continue
            acc_ref = rest[-1]

            @pl.when(kk == 0)
            def _(part=part, cs=cs):
                acc_ref[:, cs] = part

            @pl.when(kk > 0)
            def _(part=part, cs=cs):
                acc_ref[:, cs] += part

        if nk > 1:

            @pl.when(kk == nk - 1)
            def _():
                for s in range(ps):
                    cs = slice(s * tn, (s + 1) * tn)
                    finish(cs, rest[-1][:, cs])

    if d > 1:
        out_spec = pl.BlockSpec((d, tm // d, width), lambda i, j, kk: (0, i, j))
        out_shape = (d, m // d, nj * width)
    else:
        out_spec = pl.BlockSpec((tm, width), lambda i, j, kk: (i, j))
        out_shape = (m, nj * width)
    res = pl.pallas_call(
        body,
        name=name,
        grid=(m // tm, nj, nk),
        in_specs=[
            pl.BlockSpec((tm, tk), lambda i, j, kk: (i, kk)),
            pl.BlockSpec((ps, tk, tn), lambda i, j, kk: ((j + b_tile0) // q, kk, (j + b_tile0) % q)),
            *_extra_specs(extras, tm, width),
        ],
        out_specs=[out_spec for _ in outs],
        out_shape=[jax.ShapeDtypeStruct(out_shape, dt) for dt in outs],
        scratch_shapes=[pltpu.VMEM((tm, width), F32)] if nk > 1 else [],
        compiler_params=_params(("parallel", "parallel", "arbitrary")),
    )(a, b3, *[arr for arr, _ in extras])
    return [r.reshape(m, nj * width) for r in res]


def mm_nt(
    g, b3, *, name, tm, tko, ps, tc, epilogue, extras=(), outs, out_widths=None, out_streams=None, b_tile0=0,
    g_streams=1, stats=False, after=None,
):
    m, width = g.shape
    p, k, n = b3.shape
    _, q = _column_tiles(p, n, ps, tc)
    nc = width // (ps * tc)
    ds = g_streams
    assert not stats or tko == k
    widths = [tko] * len(outs) if out_widths is None else out_widths
    streams_out = [1] * len(outs) if out_streams is None else out_streams

    def out_spec(w, d):
        if d == 1:
            return pl.BlockSpec((tm, w), lambda i, j, c: (i, j))
        return pl.BlockSpec((d, tm // d, w), lambda i, j, c: (0, i, j))

    def out_struct(dt, w, d):
        cols = w * (k // tko)
        return jax.ShapeDtypeStruct((m, cols) if d == 1 else (d, m // d, cols), dt)

    order = [] if after is None else [after]
    n_in = len(extras) + len(order)

    def body(g_ref, b_ref, *rest):
        ex = rest[: len(extras)]
        out_refs = rest[n_in : n_in + len(outs)]
        c = pl.program_id(2)
        gv = g_ref[...].reshape(tm, ps * tc) if ds > 1 else g_ref[...]
        part = None
        for s in range(ps):
            d = lax.dot_general(gv[:, s * tc : (s + 1) * tc], b_ref[s], _NT, preferred_element_type=F32)
            part = d if part is None else part + d

        def finish(acc):
            res = epilogue(acc, *[e[...] for e in ex])
            if stats:
                *res, sums = res
                st_ref = rest[n_in + len(outs)]

                @pl.when(pl.program_id(0) == 0)
                def _():
                    st_ref[...] = jnp.zeros_like(st_ref)

                for row, val in enumerate(sums):
                    st_ref[row : row + 1, :] += val
            for o_ref, r, d_out in zip(out_refs, res, streams_out, strict=True):
                r = r.astype(o_ref.dtype)
                o_ref[...] = r if d_out == 1 else _to_streams(r, d_out).reshape(o_ref.shape)

        if nc == 1:
            finish(part)
            return
        acc_ref = rest[-1]

        @pl.when(c == 0)
        def _():
            acc_ref[...] = part

        @pl.when(c > 0)
        def _():
            acc_ref[...] += part

        @pl.when(c == nc - 1)
        def _():
            finish(acc_ref[...])

    return pl.pallas_call(
        body,
        name=name,
        grid=(m // tm, k // tko, nc),
        in_specs=[
            pl.BlockSpec((ds, tm // ds, ps * tc), lambda i, j, c: (0, i, c))
            if ds > 1
            else pl.BlockSpec((tm, ps * tc), lambda i, j, c: (i, c)),
            pl.BlockSpec((ps, tko, tc), lambda i, j, c: ((c + b_tile0) // q, j, (c + b_tile0) % q)),
            *_extra_specs(extras, tm, tko),
            *[ANY for _ in order],
        ],
        out_specs=[out_spec(w, d) for w, d in zip(widths, streams_out, strict=True)]
        + [pl.BlockSpec((SUBLANES, tko), lambda i, j, c: (0, 0))] * stats,
        out_shape=[out_struct(dt, w, d) for dt, w, d in zip(outs, widths, streams_out, strict=True)]
        + [jax.ShapeDtypeStruct((SUBLANES, k), F32)] * stats,
        scratch_shapes=[pltpu.VMEM((tm, tko), F32)] if nc > 1 else [],
        compiler_params=_params(("arbitrary" if stats else "parallel", "parallel", "arbitrary")),
    )(g.reshape(ds, m // ds, width) if ds > 1 else g, b3, *[arr for arr, _ in extras], *order)


def mm_tn(a, g, *, name, p, tk, ps, tn, tmc, out_dtype, prologue=None, g_streams=1):
    m, k = a.shape
    width = g.shape[1]
    n = width // p
    nj, q = _column_tiles(p, n, ps, tn)
    nc = m // tmc
    ds = g_streams

    def body(a_ref, g_ref, o_ref, acc_ref):
        c = pl.program_id(2)
        av = a_ref[...] if prologue is None else prologue(a_ref[...])
        gv = g_ref[...]
        if ds > 1:
            av, gv = _to_streams(av, ds), gv.reshape(tmc, ps * tn)
        part = lax.dot_general(av, gv, (((0,), (0,)), ((), ())), preferred_element_type=F32)

        @pl.when(c == 0)
        def _():
            acc_ref[...] = part

        @pl.when(c > 0)
        def _():
            acc_ref[...] += part

        @pl.when(c == nc - 1)
        def _():
            for s in range(ps):
                o_ref[s] = acc_ref[:, s * tn : (s + 1) * tn].astype(o_ref.dtype)

    return pl.pallas_call(
        body,
        name=name,
        grid=(k // tk, nj, nc),
        in_specs=[
            pl.BlockSpec((tmc, tk), lambda i, j, c: (c, i)),
            pl.BlockSpec((ds, tmc // ds, ps * tn), lambda i, j, c: (0, c, j))
            if ds > 1
            else pl.BlockSpec((tmc, ps * tn), lambda i, j, c: (c, j)),
        ],
        out_specs=pl.BlockSpec((ps, tk, tn), lambda i, j, c: (j // q, i, j % q)),
        out_shape=jax.ShapeDtypeStruct((p, k, n), out_dtype),
        scratch_shapes=[pltpu.VMEM((tk, ps * tn), F32)],
        compiler_params=_params(("parallel", "parallel", "arbitrary")),
    )(a, g.reshape(ds, m // ds, width) if ds > 1 else g)


def _rows(cols):
    return pl.BlockSpec((ROW_TILE, cols), lambda i: (i, 0))


def _vec(cols, rows=1):
    return pl.BlockSpec((rows, cols), lambda i: (0, 0))


def _layer_norm_hat(z):
    mu = jnp.mean(z, axis=-1, keepdims=True)
    zc = z - mu
    var = jnp.mean(zc * zc, axis=-1, keepdims=True)
    rstd = lax.rsqrt(var + LN_EPS)
    return zc * rstd, rstd


def modulate(x, scale, shift, *, name):
    s, d = x.shape

    def body(x_ref, sc_ref, sh_ref, h_ref):
        h_ref[...] = (x_ref[...] * (1.0 + sc_ref[...]) + sh_ref[...]).astype(BF16)

    return pl.pallas_call(
        body,
        name=name,
        grid=(s // ROW_TILE,),
        in_specs=[_rows(d), _vec(d), _vec(d)],
        out_specs=_rows(d),
        out_shape=jax.ShapeDtypeStruct((s, d), BF16),
        compiler_params=_params(("parallel",)),
    )(x, scale, shift)


def residual_ln_epilogue(acc, x, gate1, g, b, nscale, nshift):
    y = acc.astype(BF16)
    xhat, _ = _layer_norm_hat(ALPHA * x + gate1 * y.astype(F32))
    xn = xhat * g + b
    return y, xn, xn * (1.0 + nscale) + nshift


ST_DSCALE, ST_DSHIFT, ST_DG, ST_DB, ST_DGATE = 0, 1, 2, 3, 4


def _layer_norm_bwd(g_out, xhat, rstd, g):
    dxh = g_out * g
    m1 = jnp.mean(dxh, axis=-1, keepdims=True)
    m2 = jnp.mean(dxh * xhat, axis=-1, keepdims=True)
    return rstd * (dxh - m1 - xhat * m2)


def _column_sums(vals):
    return [jnp.sum(v, axis=0, keepdims=True) for v in vals]


def residual_bwd_epilogue(dh, dzl, scl, x, y, gate1, g, b):
    yf = y.astype(F32)
    xhat, rstd = _layer_norm_hat(ALPHA * x + gate1 * yf)
    g_out = ALPHA * dzl + dh * (1.0 + scl)
    dz = _layer_norm_bwd(g_out, xhat, rstd, g)
    return dz, dz * gate1, _column_sums([dh * (xhat * g + b), dh, g_out * xhat, g_out, dz * yf])


def input_bwd_epilogue(dh, dzl, scl, x):
    return ALPHA * dzl + dh * (1.0 + scl), _column_sums([dh * x, dh])


def residual_ln_loss_bwd(x, y, gate1, g, b, target, *, name):
    s, d = x.shape

    def body(x_ref, y_ref, gt_ref, g_ref, b_ref, t_ref, dz_ref, dy_ref, st_ref, loss_ref):
        y = y_ref[...].astype(F32)
        gate1 = gt_ref[...]
        xhat, rstd = _layer_norm_hat(ALPHA * x_ref[...] + gate1 * y)
        err = xhat * g_ref[...] + b_ref[...] - t_ref[...]
        g_out = err * (1.0 / d)
        dz = _layer_norm_bwd(g_out, xhat, rstd, g_ref[...])
        dz_ref[...] = dz
        dy_ref[...] = (dz * gate1).astype(BF16)
        part = jnp.sum(jnp.sum(err * err, axis=1, keepdims=True), axis=0, keepdims=True) * (0.5 / d)

        @pl.when(pl.program_id(0) == 0)
        def _():
            st_ref[...] = jnp.zeros_like(st_ref)
            loss_ref[...] = jnp.zeros_like(loss_ref)

        loss_ref[...] += part
        for row, val in zip((ST_DG, ST_DB, ST_DGATE), _column_sums([g_out * xhat, g_out, dz * y]), strict=True):
            st_ref[row : row + 1, :] += val

    return pl.pallas_call(
        body,
        name=name,
        grid=(s // ROW_TILE,),
        in_specs=[_rows(d), _rows(d), _vec(d), _vec(d), _vec(d), _rows(d)],
        out_specs=[_rows(d), _rows(d), _vec(d, SUBLANES), pl.BlockSpec((1, 1), lambda i: (0, 0))],
        out_shape=[
            jax.ShapeDtypeStruct((s, d), F32),
            jax.ShapeDtypeStruct((s, d), BF16),
            jax.ShapeDtypeStruct((SUBLANES, d), F32),
            jax.ShapeDtypeStruct((1, 1), F32),
        ],
        compiler_params=_params(("arbitrary",)),
    )(x, y, gate1, g, b, target)


GATE_CHUNKS = 4


def _gelu(x, with_grad=False):
    x2 = x * x
    t = jnp.tanh(GELU_C * (x + GELU_A * x2 * x))
    half = 0.5 * (1.0 + t)
    y = x * half
    if not with_grad:
        return y
    return y, half + 0.5 * x * (1.0 - t * t) * (GELU_C * (1.0 + 3.0 * GELU_A * x2))


def _causal_weights(w_ref, transpose):
    t = lax.broadcasted_iota(jnp.int32, (CHUNK, CHUNK), 0)
    s = lax.broadcasted_iota(jnp.int32, (CHUNK, CHUNK), 1)
    out = []
    for g in range(A_GROUPS):
        w = jnp.where(t >= s, w_ref[g], 0.0)
        out.append((w.T if transpose else w).astype(BF16))
    return out


def _spatial(ws, vn, lo_mask):
    rows = vn.shape[0]
    out_rows = []
    for r in range(rows // CHUNK):
        cols = []
        for j in range(A_GROUPS // 2):
            blk = vn[r * CHUNK : (r + 1) * CHUNK, j * LANES : (j + 1) * LANES]
            za = jnp.dot(ws[2 * j], blk, preferred_element_type=F32)
            zb = jnp.dot(ws[2 * j + 1], blk, preferred_element_type=F32)
            cols.append(jnp.where(lo_mask, za, zb))
        out_rows.append(jnp.concatenate(cols, axis=1))
    return jnp.concatenate(out_rows, axis=0)


def _gate_forward(a, vg, vb, ws, bias, lo_mask, with_grad=False):
    u = _gelu(a[:, :D_MODEL], with_grad)
    v = _gelu(a[:, D_MODEL:], with_grad)
    gu, gv = None, None
    if with_grad:
        (u, gu), (v, gv) = u, v
    vhat, rstd = _layer_norm_hat(v)
    vn = (vhat * vg + vb).astype(BF16)
    z = _spatial(ws, vn, lo_mask) + jnp.concatenate([bias] * (a.shape[0] // CHUNK), axis=0)
    return u, vhat, rstd, vn, z, gu, gv


def gate_fwd(a_pre, vn_g, vn_b, w_s, bias_full, *, name):
    s = a_pre.shape[0]
    tr = GATE_CHUNKS * CHUNK

    def body(a_ref, vg_ref, vb_ref, w_ref, bias_ref, p_ref):
        lo_mask = _lane((CHUNK, LANES)) < A_GROUP_DIM
        ws = _causal_weights(w_ref, transpose=False)
        u, _, _, _, z, _, _ = _gate_forward(a_ref[...].astype(F32), vg_ref[...], vb_ref[...], ws, bias_ref[...], lo_mask)
        p_ref[...] = (u * z).astype(BF16)

    return pl.pallas_call(
        body,
        name=name,
        grid=(s // tr,),
        in_specs=[
            pl.BlockSpec((tr, 2 * D_MODEL), lambda i: (i, 0)),
            _vec(D_MODEL),
            _vec(D_MODEL),
            pl.BlockSpec((A_GROUPS, CHUNK, CHUNK), lambda i: (0, 0, 0)),
            _vec(D_MODEL, CHUNK),
        ],
        out_specs=pl.BlockSpec((tr, D_MODEL), lambda i: (i, 0)),
        out_shape=jax.ShapeDtypeStruct((s, D_MODEL), BF16),
        compiler_params=_params(("parallel",)),
    )(a_pre, vn_g, vn_b, w_s, bias_full)


def gate_bwd(a_pre, dp, vn_g, vn_b, w_s, bias_full, *, name):
    s = a_pre.shape[0]
    tr = GATE_CHUNKS * CHUNK
    nsteps = s // tr

    def body(a_ref, dp_ref, vg_ref, vb_ref, w_ref, bias_ref, da_ref, dw_ref, dbs_ref, rows_ref, dbias_acc):
        step = pl.program_id(0)
        lo_mask = _lane((CHUNK, LANES)) < A_GROUP_DIM

        @pl.when(step == 0)
        def _():
            dw_ref[...] = jnp.zeros_like(dw_ref)
            rows_ref[...] = jnp.zeros_like(rows_ref)
            dbias_acc[...] = jnp.zeros_like(dbias_acc)

        a = a_ref[...].astype(F32)
        vg = vg_ref[...]
        ws = _causal_weights(w_ref, transpose=False)
        wts = _causal_weights(w_ref, transpose=True)
        u, vhat, rstd, vn, z, gelu_du, gelu_dv = _gate_forward(a, vg, vb_ref[...], ws, bias_ref[...], lo_mask, True)
        dp = dp_ref[...]
        du = dp * z
        dzz = dp * u
        dzz_b = dzz.astype(BF16)
        dvn = _spatial(wts, dzz_b, lo_mask)
        dbias = None
        for r in range(GATE_CHUNKS):
            rs = slice(r * CHUNK, (r + 1) * CHUNK)
            dbias = dzz[rs] if dbias is None else dbias + dzz[rs]
            for j in range(A_GROUPS // 2):
                cs = slice(j * LANES, (j + 1) * LANES)
                dblk = dzz[rs, cs]
                vblk = vn[rs, cs]
                for half in range(2):
                    keep = lo_mask if half == 0 else jnp.logical_not(lo_mask)
                    dm = jnp.where(keep, dblk, 0.0).astype(BF16)
                    dw_ref[2 * j + half] += lax.dot_general(
                        dm, vblk, (((1,), (1,)), ((), ())), preferred_element_type=F32
                    )
        dbias_acc[...] += dbias
        rows_ref[1:2, :D_MODEL] += jnp.sum(dvn * vhat, axis=0, keepdims=True)
        rows_ref[1:2, D_MODEL:] += jnp.sum(dvn, axis=0, keepdims=True)
        dvh = dvn * vg
        m1 = jnp.mean(dvh, axis=-1, keepdims=True)
        m2 = jnp.mean(dvh * vhat, axis=-1, keepdims=True)
        dv = rstd * (dvh - m1 - vhat * m2)
        da_u = du * gelu_du
        da_v = dv * gelu_dv
        da_ref[:, :D_MODEL] = da_u.astype(BF16)
        da_ref[:, D_MODEL:] = da_v.astype(BF16)
        rows_ref[0:1, :D_MODEL] += jnp.sum(da_u, axis=0, keepdims=True)
        rows_ref[0:1, D_MODEL:] += jnp.sum(da_v, axis=0, keepdims=True)

        @pl.when(step == nsteps - 1)
        def _():
            t = lax.broadcasted_iota(jnp.int32, (CHUNK, CHUNK), 0)
            sx = lax.broadcasted_iota(jnp.int32, (CHUNK, CHUNK), 1)
            for g in range(A_GROUPS):
                dw_ref[g] = jnp.where(t >= sx, dw_ref[g], 0.0)
            dbs_ref[...] = _reduce_groups(dbias_acc[...])

    return pl.pallas_call(
        body,
        name=name,
        grid=(nsteps,),
        in_specs=[
            pl.BlockSpec((tr, 2 * D_MODEL), lambda i: (i, 0)),
            pl.BlockSpec((tr, D_MODEL), lambda i: (i, 0)),
            _vec(D_MODEL),
            _vec(D_MODEL),
            pl.BlockSpec((A_GROUPS, CHUNK, CHUNK), lambda i: (0, 0, 0)),
            _vec(D_MODEL, CHUNK),
        ],
        out_specs=[
            pl.BlockSpec((tr, 2 * D_MODEL), lambda i: (i, 0)),
            pl.BlockSpec((A_GROUPS, CHUNK, CHUNK), lambda i: (0, 0, 0)),
            _vec(LANES, CHUNK),
            _vec(2 * D_MODEL, SUBLANES),
        ],
        out_shape=[
            jax.ShapeDtypeStruct((s, 2 * D_MODEL), BF16),
            jax.ShapeDtypeStruct((A_GROUPS, CHUNK, CHUNK), F32),
            jax.ShapeDtypeStruct((CHUNK, LANES), F32),
            jax.ShapeDtypeStruct((SUBLANES, 2 * D_MODEL), F32),
        ],
        scratch_shapes=[pltpu.VMEM((CHUNK, D_MODEL), F32)],
        compiler_params=_params(("arbitrary",)),
    )(a_pre, dp, vn_g, vn_b, w_s, bias_full)


def alibi_tables(dilation):
    qi = np.arange(SPAN)[:, None]
    ki = np.arange(2 * SPAN)[None, :]
    diff = SPAN + qi - ki
    valid = (diff >= 0) & (diff <= SPAN)
    heads = np.arange(1, B_HEADS + 1, dtype=np.float32)
    slopes = np.exp2(np.float32(-8.0) * heads / np.float32(B_HEADS)).astype(np.float32)
    bias = -slopes[:, None, None] * (dilation * diff).astype(np.float32)
    bias = np.where(valid[None], bias, np.float32(NEG)).reshape(B_HEADS // 2, 2 * SPAN, 2 * SPAN)
    return jnp.asarray(bias), jnp.asarray(np.ascontiguousarray(bias.transpose(0, 2, 1)))


def _pair_rows(x, halves):
    return jnp.concatenate([x * halves[0], x * halves[1]], axis=0)


def _pair_column(v, lane, j):
    pick = lambda h: jnp.sum(jnp.where(lane == h, v, 0.0), axis=1, keepdims=True)
    return jnp.concatenate([pick(2 * j), pick(2 * j + 1)], axis=0)


_NT = (((1,), (1,)), ((), ()))


def _qkv_specs(block_of):
    def spec(which, prev):
        def index(*grid):
            blk = block_of(*grid)
            return (jnp.maximum(blk - 1, 0) if prev else blk, which)

        return pl.BlockSpec((SPAN, D_MODEL), index)

    return [spec(0, False), spec(1, True), spec(1, False), spec(2, True), spec(2, False)]


def attn_fwd(qkv_p, pat, *, name, after=None):
    _, dilation = B_PATTERNS[pat]
    nb = SEQ // dilation // SPAN
    bias, _ = alibi_tables(dilation)
    order = [] if after is None else [after]

    def body(q_ref, kp_ref, kc_ref, vp_ref, vc_ref, bias_ref, *rest):
        o_ref, lse_ref = rest[len(order) :]
        n = pl.program_id(1)
        first_prev = jnp.logical_and(n == 0, _lane((2 * SPAN, 2 * SPAN)) < SPAN)
        lane = _lane((SPAN, LANES))
        lo_mask = lane < B_HEAD_DIM
        q = q_ref[...] * jnp.asarray(B_HEAD_DIM**-0.5, BF16)
        kk = jnp.concatenate([kp_ref[...], kc_ref[...]], axis=0)
        vv = jnp.concatenate([vp_ref[...], vc_ref[...]], axis=0)
        halves = (lo_mask.astype(BF16), jnp.logical_not(lo_mask).astype(BF16))
        stats = jnp.zeros((SPAN, LANES), F32)
        for j in range(B_HEADS // 2):
            cs = slice(j * LANES, (j + 1) * LANES)
            sc = lax.dot_general(_pair_rows(q[:, cs], halves), kk[:, cs], _NT, preferred_element_type=F32)
            sc = jnp.where(first_prev, NEG, sc + bias_ref[j])
            m = jnp.max(sc, axis=1, keepdims=True)
            p = jnp.exp(sc - m)
            l = jnp.sum(p, axis=1, keepdims=True)
            acc = jnp.dot(p.astype(BF16), vv[:, cs], preferred_element_type=F32) * (1.0 / l)
            lse_pair = m + jnp.log(l)
            o_ref[:, cs] = jnp.where(lo_mask, acc[:SPAN], acc[SPAN:]).astype(BF16)
            stats = jnp.where(lane == 2 * j, lse_pair[:SPAN], stats)
            stats = jnp.where(lane == 2 * j + 1, lse_pair[SPAN:], stats)
        lse_ref[...] = stats

    return pl.pallas_call(
        body,
        name=name,
        grid=(dilation, nb),
        in_specs=[
            *_qkv_specs(lambda r, n: r * nb + n),
            pl.BlockSpec((B_HEADS // 2, 2 * SPAN, 2 * SPAN), lambda r, n: (0, 0, 0)),
            *[ANY for _ in order],
        ],
        out_specs=[
            pl.BlockSpec((SPAN, D_MODEL), lambda r, n: (r * nb + n, 0)),
            pl.BlockSpec((SPAN, LANES), lambda r, n: (r * nb + n, 0)),
        ],
        out_shape=[jax.ShapeDtypeStruct((SEQ, D_MODEL), BF16), jax.ShapeDtypeStruct((SEQ, LANES), F32)],
        compiler_params=_params(("parallel", "arbitrary")),
    )(qkv_p, qkv_p, qkv_p, qkv_p, qkv_p, bias, *order)


def attn_combine(outs, lses, *, name):
    dils = [d for _, d in B_PATTERNS]

    def positions(ref, d):
        x = ref[...].astype(F32)
        return x if d == 1 else jnp.swapaxes(x, 0, 1).reshape(ROW_TILE, x.shape[-1])

    def body(o0, o1, o2, l0, l1, l2, ob_ref, of_ref, *lse_refs):
        ls = [positions(l, d) for l, d in zip((l0, l1, l2), dils, strict=True)]
        m = jnp.maximum(jnp.maximum(ls[0], ls[1]), ls[2])
        tot = jnp.log(jnp.exp(ls[0] - m) + jnp.exp(ls[1] - m) + jnp.exp(ls[2] - m)) + m
        o = None
        for o_ref, l, d in zip((o0, o1, o2), ls, dils, strict=True):
            term = _expand_groups(jnp.exp(l - tot)) * positions(o_ref, d)
            o = term if o is None else o + term
        ob_ref[...] = o.astype(BF16)
        of_ref[...] = o
        for lse_ref, d in zip(lse_refs, dils, strict=True):
            lse_ref[...] = tot if d == 1 else _to_streams(tot, d).reshape(lse_ref.shape)

    def stream_rows(cols, d):
        return _rows(cols) if d == 1 else pl.BlockSpec((d, ROW_TILE // d, cols), lambda i: (0, i, 0))

    def streams(x, d):
        return x if d == 1 else x.reshape(d, SEQ // d, x.shape[-1])

    outs = [streams(x, d) for x, d in zip(outs, dils, strict=True)]
    lses = [streams(x, d) for x, d in zip(lses, dils, strict=True)]
    return pl.pallas_call(
        body,
        name=name,
        grid=(SEQ // ROW_TILE,),
        in_specs=[stream_rows(D_MODEL, d) for d in dils] + [stream_rows(LANES, d) for d in dils],
        out_specs=[_rows(D_MODEL), _rows(D_MODEL)] + [stream_rows(LANES, d) for d in dils],
        out_shape=[jax.ShapeDtypeStruct((SEQ, D_MODEL), BF16), jax.ShapeDtypeStruct((SEQ, D_MODEL), F32)]
        + [jax.ShapeDtypeStruct(lse.shape, F32) for lse in lses],
        compiler_params=_params(("parallel",)),
    )(*outs, *lses)


def attn_bwd(qkv_p, do_p, lse_p, delta_p, pat, *, name):
    _, dilation = B_PATTERNS[pat]
    nb = SEQ // dilation // SPAN
    n_blocks = SEQ // SPAN
    bias, bias_t = alibi_tables(dilation)
    last = n_blocks - 1
    q_cols, k_cols, v_cols = (slice(i * D_MODEL, (i + 1) * D_MODEL) for i in range(3))

    def body(q_ref, kp_ref, kc_ref, vp_ref, vc_ref, do_ref, lse_ref, dl_ref, bias_ref, biast_ref, out_ref, cq_ref, ck_ref, cv_ref):
        g = pl.program_id(0)

        @pl.when(g == n_blocks)
        def _():
            out_ref[:, q_cols] = cq_ref[...].astype(BF16)
            out_ref[:, k_cols] = ck_ref[...].astype(BF16)
            out_ref[:, v_cols] = cv_ref[...].astype(BF16)

        @pl.when(g == 0)
        def _():
            cq_ref[...] = jnp.zeros_like(cq_ref)
            ck_ref[...] = jnp.zeros_like(ck_ref)
            cv_ref[...] = jnp.zeros_like(cv_ref)

        @pl.when(g < n_blocks)
        def _():
            lane = _lane((SPAN, LANES))
            lo_mask = lane < B_HEAD_DIM
            pair = (2 * SPAN, 2 * SPAN)
            first = lax.rem(g, nb) == 0
            prev_key_cols = jnp.logical_and(first, _lane(pair) < SPAN)
            prev_key_rows = jnp.logical_and(first, lax.broadcasted_iota(jnp.int32, pair, 0) < SPAN)
            q = q_ref[...] * jnp.asarray(B_HEAD_DIM**-0.5, BF16)
            kk = jnp.concatenate([kp_ref[...], kc_ref[...]], axis=0)
            vv = jnp.concatenate([vp_ref[...], vc_ref[...]], axis=0)
            do_v = do_ref[...]
            lse_v = lse_ref[...]
            dl_v = dl_ref[...]
            lse_t = lse_v.T
            dl_t = dl_v.T
            halves = (lo_mask.astype(BF16), jnp.logical_not(lo_mask).astype(BF16))
            for j in range(B_HEADS // 2):
                cs = slice(j * LANES, (j + 1) * LANES)
                kp, vp = kk[:, cs], vv[:, cs]
                q2 = _pair_rows(q[:, cs], halves)
                do2 = _pair_rows(do_v[:, cs], halves)
                lse_c, dl_c = _pair_column(lse_v, lane, j), _pair_column(dl_v, lane, j)
                lse_r = jnp.concatenate([lse_t[2 * j : 2 * j + 1], lse_t[2 * j + 1 : 2 * j + 2]], axis=1)
                dl_r = jnp.concatenate([dl_t[2 * j : 2 * j + 1], dl_t[2 * j + 1 : 2 * j + 2]], axis=1)
                sc = lax.dot_general(q2, kp, _NT, preferred_element_type=F32)
                p = jnp.exp(jnp.where(prev_key_cols, NEG, sc + bias_ref[j]) - lse_c)
                dp = lax.dot_general(do2, vp, _NT, preferred_element_type=F32)
                ds = (p * (dp - dl_c)).astype(BF16)
                dq2 = jnp.dot(ds, kp, preferred_element_type=F32)
                sc_t = lax.dot_general(kp, q2, _NT, preferred_element_type=F32)
                p_t = jnp.exp(jnp.where(prev_key_rows, NEG, sc_t + biast_ref[j]) - lse_r)
                dp_t = lax.dot_general(vp, do2, _NT, preferred_element_type=F32)
                ds_t = (p_t * (dp_t - dl_r)).astype(BF16)
                dk_pair = jnp.dot(ds_t, q2, preferred_element_type=F32)
                dv_pair = jnp.dot(p_t.astype(BF16), do2, preferred_element_type=F32)
                oq = slice(j * LANES, (j + 1) * LANES)
                ok = slice(D_MODEL + j * LANES, D_MODEL + (j + 1) * LANES)
                ov = slice(2 * D_MODEL + j * LANES, 2 * D_MODEL + (j + 1) * LANES)
                out_ref[:, oq] = cq_ref[:, cs].astype(BF16)
                out_ref[:, ok] = (ck_ref[:, cs] + dk_pair[:SPAN]).astype(BF16)
                out_ref[:, ov] = (cv_ref[:, cs] + dv_pair[:SPAN]).astype(BF16)
                cq_ref[:, cs] = jnp.where(lo_mask, dq2[:SPAN], dq2[SPAN:]) * (B_HEAD_DIM**-0.5)
                ck_ref[:, cs] = dk_pair[SPAN:]
                cv_ref[:, cs] = dv_pair[SPAN:]

    def block_of(g):
        return jnp.minimum(g, last)

    def row_spec(width):
        return pl.BlockSpec((SPAN, width), lambda g: (block_of(g), 0))

    return pl.pallas_call(
        body,
        name=name,
        grid=(n_blocks + 1,),
        in_specs=[
            *_qkv_specs(block_of),
            row_spec(D_MODEL),
            row_spec(LANES),
            row_spec(LANES),
            pl.BlockSpec((B_HEADS // 2, 2 * SPAN, 2 * SPAN), lambda g: (0, 0, 0)),
            pl.BlockSpec((B_HEADS // 2, 2 * SPAN, 2 * SPAN), lambda g: (0, 0, 0)),
        ],
        out_specs=pl.BlockSpec((SPAN, 3 * D_MODEL), lambda g: (jnp.maximum(g - 1, 0), 0)),
        out_shape=jax.ShapeDtypeStruct((SEQ, 3 * D_MODEL), BF16),
        scratch_shapes=[pltpu.VMEM((SPAN, D_MODEL), F32)] * 3,
        compiler_params=_params(("arbitrary",)),
    )(qkv_p, qkv_p, qkv_p, qkv_p, qkv_p, do_p, lse_p, delta_p, bias, bias_t)


def _position():
    x, y, c = lax.axis_index("x"), lax.axis_index("y"), lax.axis_index("c")
    return x, y, c, 4 * x + 2 * y + c


def _peer(k, x, y, c):
    px = 1 - x if k & 4 else x
    py = 1 - y if k & 2 else y
    pc = 1 - c if k & 1 else c
    return (px, py, pc), 4 * px + 2 * py + pc


def _remote(src, dst, send_sem, recv_sem, device):
    return pltpu.make_async_remote_copy(
        src_ref=src, dst_ref=dst, send_sem=send_sem, recv_sem=recv_sem, device_id=device, device_id_type=MESH
    )


def _silu_bf16(cf):
    return (cf * (1.0 / (1.0 + jnp.exp(-cf)))).astype(BF16)


def ada_exchange(c8, w4, b4, ln8):
    nt, _, ncol = w4.shape

    def body(c8_ref, w_ref, b_ref, ln_ref, cg_ref, lng_ref, mrecv_ref, mloc_ref, send_sems, recv_sems):
        x, y, c, me = _position()
        cg_ref[me] = c8_ref[...]
        lng_ref[me] = ln_ref[...]
        first = []
        for k in range(1, N_DEV):
            dev, _ = _peer(k, x, y, c)
            first.append(_remote(c8_ref, cg_ref.at[me], send_sems.at[0, k], recv_sems.at[0, k], dev))
            first.append(_remote(ln_ref, lng_ref.at[me], send_sems.at[1, k], recv_sems.at[1, k], dev))
        for cp in first:
            cp.start()
        for k in range(1, N_DEV):
            dev, pid = _peer(k, x, y, c)
            _remote(c8_ref, cg_ref.at[pid], send_sems.at[0, k], recv_sems.at[0, k], dev).wait_recv()
            _remote(ln_ref, lng_ref.at[pid], send_sems.at[1, k], recv_sems.at[1, k], dev).wait_recv()
        sc = _silu_bf16(cg_ref[...].reshape(N_DEV * SUBLANES, D_MODEL))
        for t in range(nt):
            mloc_ref[t] = jnp.dot(sc, w_ref[t].astype(BF16), preferred_element_type=F32) + b_ref[t : t + 1, :]

        def group(dev_id):
            return pl.ds(pl.multiple_of(dev_id * SUBLANES, SUBLANES), SUBLANES)

        mrecv_ref[me] = mloc_ref[:, group(me), :]
        second = []
        for k in range(1, N_DEV):
            dev, pid = _peer(k, x, y, c)
            second.append(
                _remote(mloc_ref.at[:, group(pid), :], mrecv_ref.at[me], send_sems.at[2, k], recv_sems.at[2, k], dev)
            )
        for cp in second:
            cp.start()
        for k in range(1, N_DEV):
            dev, pid = _peer(k, x, y, c)
            _remote(
                mloc_ref.at[:, group(pid), :], mrecv_ref.at[pid], send_sems.at[2, k], recv_sems.at[2, k], dev
            ).wait_recv()
        for cp in first + second:
            cp.wait_send()

    return pl.pallas_call(
        body,
        name="ada_exchange",
        in_specs=[VMEM, VMEM, VMEM, VMEM],
        out_specs=[VMEM, VMEM, VMEM],
        out_shape=[
            jax.ShapeDtypeStruct((N_DEV, SUBLANES, D_MODEL), F32),
            jax.ShapeDtypeStruct((N_DEV, SUBLANES, LANES), F32),
            jax.ShapeDtypeStruct((N_DEV, nt, SUBLANES, ncol), F32),
        ],
        scratch_shapes=[
            pltpu.VMEM((nt, N_DEV * SUBLANES, ncol), F32),
            pltpu.SemaphoreType.DMA((3, N_DEV)),
            pltpu.SemaphoreType.DMA((3, N_DEV)),
        ],
        compiler_params=pltpu.CompilerParams(vmem_limit_bytes=VMEM_LIMIT_BYTES),
    )(c8, w4, b4, ln8)


def small_exchange(dmx, flat, pre, after):
    rows = flat.shape[1]

    def body(dmx_ref, flat_ref, pre_ref, _, dmrecv_ref, red_ref, land_ref, send_sems, recv_sems):
        x, y, c, me = _position()
        dmrecv_ref[me] = dmx_ref[me]
        land_ref[me] = flat_ref[me]
        first = []
        for k in range(1, N_DEV):
            dev, pid = _peer(k, x, y, c)
            first.append(_remote(dmx_ref.at[pid], dmrecv_ref.at[me], send_sems.at[0, k], recv_sems.at[0, k], dev))
            first.append(_remote(flat_ref.at[pid], land_ref.at[me], send_sems.at[1, k], recv_sems.at[1, k], dev))
        for cp in first:
            cp.start()
        for k in range(1, N_DEV):
            dev, pid = _peer(k, x, y, c)
            _remote(dmx_ref.at[pid], dmrecv_ref.at[pid], send_sems.at[0, k], recv_sems.at[0, k], dev).wait_recv()
            _remote(flat_ref.at[pid], land_ref.at[pid], send_sems.at[1, k], recv_sems.at[1, k], dev).wait_recv()
        total, total_pre = land_ref[0], pre_ref[0]
        for s in range(1, N_DEV):
            total, total_pre = total + land_ref[s], total_pre + pre_ref[s]
        red_ref[me, :rows, :] = total
        red_ref[me, rows:, :] = total_pre
        second = []
        for k in range(1, N_DEV):
            dev, _ = _peer(k, x, y, c)
            second.append(_remote(red_ref.at[me], red_ref.at[me], send_sems.at[2, k], recv_sems.at[2, k], dev))
        for cp in second:
            cp.start()
        for k in range(1, N_DEV):
            dev, pid = _peer(k, x, y, c)
            _remote(red_ref.at[pid], red_ref.at[pid], send_sems.at[2, k], recv_sems.at[2, k], dev).wait_recv()
        for cp in first + second:
            cp.wait_send()

    return pl.pallas_call(
        body,
        name="small_exchange",
        in_specs=[VMEM, VMEM, VMEM, ANY],
        out_specs=[VMEM, VMEM],
        out_shape=[
            jax.ShapeDtypeStruct(dmx.shape, F32),
            jax.ShapeDtypeStruct((N_DEV, rows + pre.shape[1], LANES), F32),
        ],
        scratch_shapes=[
            pltpu.VMEM(flat.shape, F32),
            pltpu.SemaphoreType.DMA((3, N_DEV)),
            pltpu.SemaphoreType.DMA((3, N_DEV)),
        ],
        compiler_params=pltpu.CompilerParams(vmem_limit_bytes=VMEM_LIMIT_BYTES),
    )(dmx, flat, pre, after)


HBM = pl.BlockSpec(memory_space=pltpu.HBM)
SEM = pl.BlockSpec(memory_space=pltpu.SEMAPHORE)
EFFECT = pltpu.SideEffectType.DATAFLOW_SIDE_EFFECTING


REGROUP_ROWS = 256


def shards_to_columns(x, *, name):
    p, k, n = x.shape

    def body(x_ref, o_ref):
        for s in range(p):
            o_ref[:, s * n : (s + 1) * n] = x_ref[s]

    return pl.pallas_call(
        body,
        name=name,
        grid=(k // REGROUP_ROWS,),
        in_specs=[pl.BlockSpec((p, REGROUP_ROWS, n), lambda i: (0, i, 0))],
        out_specs=pl.BlockSpec((REGROUP_ROWS, p * n), lambda i: (i, 0)),
        out_shape=jax.ShapeDtypeStruct((k, p * n), x.dtype),
        compiler_params=_params(("parallel",)),
    )(x)


def columns_to_shards(xs, *, name):
    k = xs[0].shape[0]
    widths = [x.shape[1] for x in xs]
    n = sum(widths) // N_DEV
    pieces = []
    for s in range(N_DEV):
        start = 0
        for i, w in enumerate(widths):
            lo, hi = max(start, s * n), min(start + w, (s + 1) * n)
            if lo < hi:
                pieces.append((i, lo - start, s, lo - s * n, hi - lo))
            start += w

    def body(*refs):
        x_refs, o_ref = refs[: len(xs)], refs[-1]
        for i, c0, s, d0, w in pieces:
            o_ref[s, :, d0 : d0 + w] = x_refs[i][:, c0 : c0 + w]

    return pl.pallas_call(
        body,
        name=name,
        grid=(k // REGROUP_ROWS,),
        in_specs=[pl.BlockSpec((REGROUP_ROWS, w), lambda i: (i, 0)) for w in widths],
        out_specs=pl.BlockSpec((N_DEV, REGROUP_ROWS, n), lambda i: (0, i, 0)),
        out_shape=jax.ShapeDtypeStruct((N_DEV, k, n), xs[0].dtype),
        compiler_params=_params(("parallel",)),
    )(*xs)


def _own_slot(me, block):
    land = lax.empty((N_DEV, *block.shape), block.dtype)
    return lax.dynamic_update_slice_in_dim(land, block[None], me, axis=0)


N_CHIP_PEERS = 3


class Gather:
    def __init__(self, shards, lands, after, *, name):
        nt = len(shards)
        self.name = name

        def body(*refs):
            src_refs, land_refs = refs[:nt], refs[nt : 2 * nt]
            send_sems, recv_sems = refs[2 * nt + 1 : 3 * nt + 1], refs[3 * nt + 1 : 4 * nt + 1]
            token = refs[-1]
            x, y, c, me = _position()
            for t in range(nt):
                for k, dev in enumerate(self._targets(x, y, c)):
                    _remote(src_refs[t], land_refs[t].at[me], send_sems[t].at[k], recv_sems[t].at[k], dev).start()
            token[...] = jnp.zeros_like(token)

        outs = pl.pallas_call(
            body,
            name=name + "_start",
            in_specs=[HBM] * (2 * nt) + [ANY],
            out_specs=[SEM] * (2 * nt) + [HBM] * (2 * nt) + [VMEM],
            out_shape=[pltpu.SemaphoreType.DMA((1 + N_CHIP_PEERS,))] * (2 * nt)
            + [pltpu.HBM(a.shape, a.dtype) for a in (*shards, *lands)]
            + [jax.ShapeDtypeStruct((SUBLANES, LANES), F32)],
            input_output_aliases={i: 2 * nt + i for i in range(2 * nt)},
            compiler_params=pltpu.CompilerParams(has_side_effects=EFFECT),
        )(*[pltpu.with_memory_space_constraint(a, pltpu.HBM) for a in (*shards, *lands)], after)
        self.send_sems, self.recv_sems = list(outs[:nt]), list(outs[nt : 2 * nt])
        self.srcs, self.lands = list(outs[2 * nt : 3 * nt]), list(outs[3 * nt : 4 * nt])
        self.token = outs[-1]

    @staticmethod
    def _chips(x, y):
        return [(1 - x, y), (x, 1 - y), (1 - x, 1 - y)]

    @classmethod
    def _targets(cls, x, y, c):
        return [(x, y, 1 - c)] + [(*chip, c) for chip in cls._chips(x, y)]

    def zero(self):
        return self.token[0, 0]

    @staticmethod
    def _slot(px, py, pc):
        return 4 * px + 2 * py + pc

    def pass_on(self, which, after, *, name):
        n = len(which)

        def pass_body(*refs):
            land_refs, recv_sems = refs[:n], refs[n : 2 * n]
            fwd_send, fwd_recv = refs[3 * n + 1 : 4 * n + 1], refs[4 * n + 1 : 5 * n + 1]
            token = refs[-1]
            x, y, c, _ = _position()
            for t in range(n):
                for j, chip in enumerate(self._chips(x, y)):
                    blk = land_refs[t].at[self._slot(*chip, c)]
                    _remote(blk, blk, fwd_send[t].at[j], recv_sems[t].at[1 + j], (*chip, c)).wait_recv()
                    _remote(blk, blk, fwd_send[t].at[j], fwd_recv[t].at[j], (x, y, 1 - c)).start()
            token[...] = jnp.zeros_like(token)

        lands = [self.lands[t] for t in which]
        outs = pl.pallas_call(
            pass_body,
            name=name,
            in_specs=[HBM] * n + [SEM] * n + [ANY],
            out_specs=[HBM] * n + [SEM] * (2 * n) + [VMEM],
            out_shape=[pltpu.HBM(a.shape, a.dtype) for a in lands]
            + [pltpu.SemaphoreType.DMA((N_CHIP_PEERS,))] * (2 * n)
            + [jax.ShapeDtypeStruct((SUBLANES, LANES), F32)],
            input_output_aliases={i: i for i in range(n)},
            compiler_params=pltpu.CompilerParams(has_side_effects=EFFECT),
        )(*lands, *[self.recv_sems[t] for t in which], after)
        return (which, outs[:n], outs[n : 2 * n], outs[2 * n : 3 * n]), outs[-1]

    def wait(self, which, after, *, name):
        return self.finish(self.pass_on(which, after, name=name + "_pass")[0], after, name=name)

    def finish(self, passed, after, *, name):
        which, lands, fwd_send, fwd_recv = passed
        n = len(which)
        slot = self._slot

        def wait_body(*refs):
            src_refs, land_refs = refs[:n], refs[n : 2 * n]
            send_sems, recv_sems = refs[2 * n : 3 * n], refs[3 * n : 4 * n]
            fwd_send, fwd_recv = refs[4 * n : 5 * n], refs[5 * n : 6 * n]
            x, y, c, me = _position()
            sibling = (x, y, 1 - c)
            for t in range(n):
                for k, dev in enumerate(self._targets(x, y, c)):
                    _remote(src_refs[t], land_refs[t].at[me], send_sems[t].at[k], recv_sems[t].at[k], dev).wait_send()
                blk = land_refs[t].at[slot(x, y, 1 - c)]
                _remote(blk, blk, send_sems[t].at[0], recv_sems[t].at[0], sibling).wait_recv()
                for j, chip in enumerate(self._chips(x, y)):
                    sent = land_refs[t].at[slot(*chip, c)]
                    _remote(sent, sent, fwd_send[t].at[j], fwd_recv[t].at[j], sibling).wait_send()
                    got = land_refs[t].at[slot(*chip, 1 - c)]
                    _remote(got, got, fwd_send[t].at[j], fwd_recv[t].at[j], sibling).wait_recv()

        srcs = [self.srcs[t] for t in which]
        outs = pl.pallas_call(
            wait_body,
            name=name,
            in_specs=[HBM] * (2 * n) + [SEM] * (4 * n) + [ANY],
            out_specs=[HBM] * (2 * n),
            out_shape=[pltpu.HBM(a.shape, a.dtype) for a in (*srcs, *lands)],
            input_output_aliases={i: i for i in range(2 * n)},
            compiler_params=pltpu.CompilerParams(has_side_effects=EFFECT),
        )(
            *srcs, *lands, *[self.send_sems[t] for t in which], *[self.recv_sems[t] for t in which], *fwd_send,
            *fwd_recv, after,
        )
        return outs[n:]


class Scatter:
    def __init__(self, srcs, after, *, name):
        self.name = name
        nt = self.nt = len(srcs)
        peers = N_DEV - 1
        lands = [lax.empty(a.shape, a.dtype) for a in srcs]

        def body(*refs):
            src_refs, land_refs = refs[:nt], refs[nt : 2 * nt]
            send_sems, recv_sems = refs[2 * nt + 1 : 3 * nt + 1], refs[3 * nt + 1 : 4 * nt + 1]
            token = refs[-1]
            x, y, c, me = _position()
            for t in range(nt):
                for k in range(1, N_DEV):
                    dev, pid = _peer(k, x, y, c)
                    src = src_refs[t].at[pid]
                    _remote(src, land_refs[t].at[me], send_sems[t].at[k - 1], recv_sems[t].at[k - 1], dev).start()
                self._own(src_refs[t], land_refs[t], send_sems[t], me).start()
            token[...] = jnp.zeros_like(token)

        outs = pl.pallas_call(
            body,
            name=name + "_start",
            in_specs=[HBM] * (2 * nt) + [ANY],
            out_specs=[SEM] * (2 * nt) + [HBM] * (2 * nt) + [VMEM],
            out_shape=[pltpu.SemaphoreType.DMA((peers + 1,))] * (2 * nt)
            + [pltpu.HBM(a.shape, a.dtype) for a in (*srcs, *lands)]
            + [jax.ShapeDtypeStruct((SUBLANES, LANES), F32)],
            input_output_aliases={i: 2 * nt + i for i in range(2 * nt)},
            compiler_params=pltpu.CompilerParams(has_side_effects=EFFECT),
        )(*[pltpu.with_memory_space_constraint(a, pltpu.HBM) for a in (*srcs, *lands)], after)
        self.send_sems, self.recv_sems = outs[:nt], outs[nt : 2 * nt]
        self.srcs, self.lands = outs[2 * nt : 3 * nt], outs[3 * nt : 4 * nt]
        self.token = outs[-1]

    @staticmethod
    def _own(src_ref, land_ref, sems, me):
        return pltpu.make_async_copy(src_ref.at[me], land_ref.at[me], sems.at[N_DEV - 1])

    def zero(self):
        return self.token[0, 0]

    def wait(self, which, after, *, name):
        n = len(which)

        def body(*refs):
            src_refs, land_refs = refs[:n], refs[n : 2 * n]
            send_sems, recv_sems = refs[2 * n : 3 * n], refs[3 * n : 4 * n]
            x, y, c, me = _position()
            for t in range(n):
                for k in range(1, N_DEV):
                    dev, pid = _peer(k, x, y, c)
                    src = src_refs[t].at[pid]
                    cp = _remote(src, land_refs[t].at[pid], send_sems[t].at[k - 1], recv_sems[t].at[k - 1], dev)
                    cp.wait_send()
                    cp.wait_recv()
                self._own(src_refs[t], land_refs[t], send_sems[t], me).wait()

        srcs = [self.srcs[t] for t in which]
        lands = [self.lands[t] for t in which]
        outs = pl.pallas_call(
            body,
            name=name,
            in_specs=[HBM] * (2 * n) + [SEM] * (2 * n) + [ANY],
            out_specs=[HBM] * (2 * n),
            out_shape=[pltpu.HBM(a.shape, a.dtype) for a in (*srcs, *lands)],
            input_output_aliases={i: i for i in range(2 * n)},
            compiler_params=pltpu.CompilerParams(has_side_effects=EFFECT),
        )(*srcs, *lands, *[self.send_sems[t] for t in which], *[self.recv_sems[t] for t in which], after)
        return outs[n:]


def _adam_update(g, w, m, v):
    m2 = ADAM_B1 * m + (1.0 - ADAM_B1) * g
    v2 = ADAM_B2 * v + (1.0 - ADAM_B2) * jnp.square(g)
    m_hat = m2 / (1.0 - ADAM_B1**ADAM_STEP)
    v_hat = v2 / (1.0 - ADAM_B2**ADAM_STEP)
    delta = -ADAM_LR * (m_hat / (jnp.sqrt(v_hat) + ADAM_EPS) + ADAM_WD * w)
    return delta, m2, v2


def adamw(gparts, w, m, v, *, name):
    nl, r, c = w.shape
    p = gparts[0].shape[0]
    tr = r if r <= 256 else (256 if c <= D_MODEL else 128)
    ni = r // tr

    def body(*refs):
        g_refs = refs[:nl]
        w_ref, m_ref, v_ref, go_ref, d_ref, mo_ref, vo_ref = refs[nl:]
        for layer in range(nl):

            @pl.when(pl.program_id(0) == layer)
            def _(g_ref=g_refs[layer]):
                g = g_ref[0].astype(F32)
                for i in range(1, p):
                    g = g + g_ref[i].astype(F32)
                delta, m2, v2 = _adam_update(g, w_ref[...], m_ref[...], v_ref[...])
                go_ref[...] = g
                d_ref[...] = delta
                mo_ref[...] = m2
                vo_ref[...] = v2

    def parts_spec(layer):
        def index(l, i):
            return (0, jnp.where(l == layer, i, jnp.where(l < layer, 0, ni - 1)), 0)

        return pl.BlockSpec((p, tr, c), index)

    blk = pl.BlockSpec((None, tr, c), lambda l, i: (l, i, 0))
    return pl.pallas_call(
        body,
        name=name,
        grid=(nl, ni),
        in_specs=[*[parts_spec(layer) for layer in range(nl)], blk, blk, blk],
        out_specs=[blk] * 4,
        out_shape=[jax.ShapeDtypeStruct((nl, r, c), F32)] * 4,
        compiler_params=_params(("arbitrary", "arbitrary")),
    )(*gparts, w, m, v)


def adamw_small(items, *, name):
    n = len(items)

    def body(*refs):
        ins, outs = refs[: 4 * n], refs[4 * n :]
        for t in range(n):
            g_ref, w_ref, m_ref, v_ref = ins[4 * t : 4 * t + 4]
            g = g_ref[...]
            delta, m2, v2 = _adam_update(g, w_ref[...], m_ref[...], v_ref[...])
            for o_ref, val in zip(outs[4 * t : 4 * t + 4], (g, delta, m2, v2), strict=True):
                o_ref[...] = val

    outs = pl.pallas_call(
        body,
        name=name,
        out_shape=[jax.ShapeDtypeStruct(item[1].shape, F32) for item in items for _ in range(4)],
    )(*[a for item in items for a in item])
    return [outs[4 * t : 4 * t + 4] for t in range(n)]


def ada_grad_adamw(cg, dmrecv, w4, m4, v4, *, name):
    nt, k, ncol = w4.shape

    def body(cg_ref, dm_ref, w_ref, m_ref, v_ref, go_ref, d_ref, mo_ref, vo_ref, gb_ref):
        sc = _silu_bf16(cg_ref[...].reshape(N_DEV * SUBLANES, k))
        dm = dm_ref[...].reshape(N_DEV * SUBLANES, ncol)
        g = lax.dot_general(sc, dm.astype(BF16), (((0,), (0,)), ((), ())), preferred_element_type=F32)
        delta, m2, v2 = _adam_update(g, w_ref[...], m_ref[...], v_ref[...])
        go_ref[...] = g
        d_ref[...] = delta
        mo_ref[...] = m2
        vo_ref[...] = v2
        gb_ref[...] = jnp.broadcast_to(jnp.sum(dm, axis=0, keepdims=True), (SUBLANES, ncol))

    wblk = pl.BlockSpec((None, k, ncol), lambda t: (t, 0, 0))
    return pl.pallas_call(
        body,
        name=name,
        grid=(nt,),
        in_specs=[
            pl.BlockSpec((N_DEV, SUBLANES, k), lambda t: (0, 0, 0)),
            pl.BlockSpec((N_DEV, None, SUBLANES, ncol), lambda t: (0, t, 0, 0)),
            wblk,
            wblk,
            wblk,
        ],
        out_specs=[wblk] * 4 + [pl.BlockSpec((None, SUBLANES, ncol), lambda t: (t, 0, 0))],
        out_shape=[jax.ShapeDtypeStruct((nt, k, ncol), F32)] * 4 + [jax.ShapeDtypeStruct((nt, SUBLANES, ncol), F32)],
        compiler_params=_params(("parallel",)),
    )(cg, dmrecv, w4, m4, v4)


def kernel(x, c, ada_w, ada_b, ln_g, ln_b, a_w_in, a_b_in, a_vn_g, a_vn_b, a_w_s, a_b_s, a_w_out, b_w_qkv, b_w_out, mlp_w_up, mlp_w_down, loss_target, m_ada_w, m_ada_b, m_ln_g, m_ln_b, m_a_w_in, m_a_b_in, m_a_vn_g, m_a_vn_b, m_a_w_s, m_a_b_s, m_a_w_out, m_b_w_qkv, m_b_w_out, m_mlp_w_up, m_mlp_w_down, v_ada_w, v_ada_b, v_ln_g, v_ln_b, v_a_w_in, v_a_b_in, v_a_vn_g, v_a_vn_b, v_a_w_s, v_a_b_s, v_a_w_out, v_b_w_qkv, v_b_w_out, v_mlp_w_up, v_mlp_w_down):
    x0 = x[0]
    target = loss_target[0]
    me = 4 * lax.axis_index("x") + 2 * lax.axis_index("y") + lax.axis_index("c")

    ada_w4 = ada_w.reshape(N_SUB, D_MODEL, -1)
    ada_b4 = ada_b.reshape(N_SUB, -1)
    ln8 = jnp.concatenate([ln_g.reshape(N_SUB, -1), ln_b.reshape(N_SUB, -1)], axis=0)
    c8 = jnp.broadcast_to(c, (SUBLANES, D_MODEL))
    cg, lng, mrecv = ada_exchange(c8, ada_w4, ada_b4, ln8)

    W_IN, W_AOUT, W_UP0, W_DN0, W_QKV, W_BOUT, W_UP1, W_DN1 = range(8)
    shards = [
        a_w_in[0].astype(BF16),
        a_w_out[0].astype(BF16),
        mlp_w_up[0].astype(BF16),
        mlp_w_down[0].astype(BF16),
        b_w_qkv[0].astype(BF16),
        b_w_out[0].astype(BF16),
        mlp_w_up[1].astype(BF16),
        mlp_w_down[1].astype(BF16),
    ]
    gather = Gather(shards, [_own_slot(me, s) for s in shards], mrecv, name="gather")

    modv = mrecv[:, :, 0, :].transpose(1, 0, 2).reshape(N_SUB, 3 * D_MODEL) + gather.zero()
    shift = [modv[t : t + 1, :D_MODEL] for t in range(N_SUB)]
    scale = [modv[t : t + 1, D_MODEL : 2 * D_MODEL] for t in range(N_SUB)]
    gate1 = [1.0 + modv[t : t + 1, 2 * D_MODEL :] for t in range(N_SUB)]
    lng_full = [lng[:, t, :].reshape(1, D_MODEL) for t in range(N_SUB)]
    lnb_full = [lng[:, N_SUB + t, :].reshape(1, D_MODEL) for t in range(N_SUB)]

    ident = lambda acc: (acc,)
    def relu2(a):
        r = jnp.maximum(a, jnp.zeros_like(a))
        return r * r
    vn_g, vn_b, w_s = a_vn_g, a_vn_b, a_w_s[0]
    bias_full = jnp.repeat(a_b_s[0].T, A_GROUP_DIM, axis=1)
    w_up3, w_dn3 = [None, None], [None, None]

    def mlp_forward(i, h, up, dn, x_in=None, t_next=None):
        w_up3[i], w_dn3[i] = up, dn.reshape(1, D_FF, D_MODEL)
        (a,) = mm_nn(h, w_up3[i], name=f"mlp{i}_up", tm=2048, ps=2, tn=512, tk=D_MODEL, epilogue=ident, outs=(BF16,))
        if t_next is None:
            (y,) = mm_nn(
                a, w_dn3[i], name=f"mlp{i}_down", tm=1024, ps=1, tn=512, tk=D_FF, prologue=relu2, epilogue=ident, outs=(BF16,)
            )
            return a, y
        y, xn, hn = mm_nn(
            a, w_dn3[i], name=f"mlp{i}_down", tm=512, ps=1, tn=D_MODEL, tk=D_FF, prologue=relu2,
            epilogue=residual_ln_epilogue, extras=residual_extras(x_in, t_next - 1), outs=(BF16, F32, BF16),
        )
        return a, y, xn, hn

    def residual_extras(x_in, t):
        rows = (gate1[t], lng_full[t], lnb_full[t], scale[t + 1], shift[t + 1])
        return [(x_in, "full")] + [(r, "row") for r in rows]

    h0 = modulate(x0, scale[0], shift[0], name="modulate0")
    w_in3, w_aout3 = gather.wait([W_IN, W_AOUT], h0, name="gather_wait_a")
    w_aout3 = w_aout3.reshape(1, D_MODEL, D_MODEL)
    (a_pre,) = mm_nn(
        h0, w_in3, name="a_in", tm=2048, ps=4, tn=256, tk=D_MODEL, epilogue=lambda acc, b: (acc + b,),
        extras=[(a_b_in, "row")], outs=(BF16,),
    )
    p_gate = gate_fwd(a_pre, vn_g, vn_b, w_s, bias_full, name="gate_fwd")
    y0, x1, h1 = mm_nn(
        p_gate, w_aout3, name="a_out", tm=1024, ps=1, tn=D_MODEL, tk=D_MODEL, epilogue=residual_ln_epilogue,
        extras=residual_extras(x0, 0), outs=(BF16, F32, BF16),
    )
    w_mlp0 = gather.wait([W_UP0, W_DN0], y0, name="gather_wait_mlp0")
    a1, y1, x2, h2 = mlp_forward(0, h1, *w_mlp0, x_in=x1, t_next=2)
    (w_qkv_shards,) = gather.wait([W_QKV], y1, name="gather_wait_qkv")
    w_qkv3 = shards_to_columns(w_qkv_shards, name="w_qkv_columns")[None]
    pat_tiles = 3
    dil = [d for _, d in B_PATTERNS]
    qkv_p, pat_o, pat_lse = [], [], []
    for g in range(N_PAT):
        (qkv_g,) = mm_nn(
            h2, w_qkv3, name=f"b_qkv{g}", tm=2048, ps=1, tn=D_MODEL, tk=D_MODEL, epilogue=ident, outs=(BF16,),
            b_tile0=pat_tiles * g, b_tiles=pat_tiles, out_streams=dil[g],
        )
        o_g, lse_g = attn_fwd(qkv_g, g, name=f"attn_fwd{g}", after=rest_token if g == 1 else None)
        if g == 0:
            rest_passed, rest_token = gather.pass_on([W_BOUT, W_UP1, W_DN1], lse_g, name="gather_pass_rest")
        qkv_p.append(qkv_g)
        pat_o.append(o_g)
        pat_lse.append(lse_g)
    o_b, o_f, *lse_pat = attn_combine(pat_o, pat_lse, name="attn_combine")
    lse_pat = [x.reshape(SEQ, LANES) for x in lse_pat]
    w_bout3, *w_mlp1 = gather.finish(rest_passed, lse_pat[0], name="gather_wait_rest")
    w_bout3 = w_bout3.reshape(1, D_MODEL, D_MODEL)
    y2, x3, h3 = mm_nn(
        o_b, w_bout3, name="b_out", tm=1024, ps=1, tn=D_MODEL, tk=D_MODEL, epilogue=residual_ln_epilogue,
        extras=residual_extras(x2, 2), outs=(BF16, F32, BF16),
    )
    a3, y3 = mlp_forward(1, h3, *w_mlp1)
    dz3, dy3, st3, loss_local = residual_ln_loss_bwd(
        x3, y3, gate1[3], lng_full[3], lnb_full[3], target, name="res_ln3_loss_bwd"
    )

    def scatter(parts, after, name):
        return Scatter([p.reshape(N_DEV, -1, p.shape[-1]) for p in parts], after, name=name)

    xs_in, ys = [x0, x1, x2, x3], [y0, y1, y2, y3]

    def residual_bwd_extras(dz_later, t):
        return [
            (dz_later, "full"), (scale[t + 1], "row"), (xs_in[t], "full"), (ys[t], "full"),
            (gate1[t], "row"), (lng_full[t], "row"), (lnb_full[t], "row"),
        ]

    def mlp_backward(i, h, a, dy, dz_later, t):
        (da,) = mm_nt(
            dy,
            w_dn3[i],
            name=f"mlp{i}_da",
            tm=2048,
            tko=1024,
            ps=1,
            tc=D_MODEL,
            epilogue=lambda acc, act: (acc * (2.0 * jnp.maximum(act.astype(F32), 0.0)),),
            extras=[(a, "full")],
            outs=(BF16,),
        )
        dw_dn = mm_tn(
            a, dy, name=f"mlp{i}_dw_down", p=1, tk=1024, ps=1, tn=D_MODEL, tmc=2048, prologue=relu2, out_dtype=BF16
        )
        dw_up = mm_tn(h, da, name=f"mlp{i}_dw_up", p=N_DEV, tk=1024, ps=2, tn=512, tmc=2048, out_dtype=BF16)
        rs = scatter([dw_up, dw_dn], da, f"scatter_mlp{i}")
        dz, dy_before, st = mm_nt(
            da, w_up3[i], name=f"mlp{i}_dh", tm=512, tko=D_MODEL, ps=N_DEV, tc=512, epilogue=residual_bwd_epilogue,
            extras=residual_bwd_extras(dz_later, t), outs=(F32, BF16), stats=True, after=rs.token,
        )
        return rs, dz, dy_before, st

    rs_mlp1, dz2, dy2, st2 = mlp_backward(1, h3, a3, dy3, dz3, 2)
    do_delta = mm_nt(
        dy2, w_bout3, name="b_do", tm=1024, tko=D_MODEL, ps=1, tc=D_MODEL, extras=[(o_f, "full")],
        epilogue=lambda acc, o: (acc,) * N_PAT + (_reduce_groups(acc * o),) * N_PAT,
        outs=(BF16,) * N_PAT + (F32,) * N_PAT, out_widths=[D_MODEL] * N_PAT + [LANES] * N_PAT, out_streams=dil + dil,
    )
    do_pat = [x.reshape(SEQ, D_MODEL) for x in do_delta[:N_PAT]]
    delta_pat = [x.reshape(SEQ, LANES) for x in do_delta[N_PAT:]]
    dh2, dw_pat = None, []
    for g in range(N_PAT):
        dqkv_g = attn_bwd(qkv_p[g], do_pat[g], lse_pat[g], delta_pat[g], g, name=f"attn_bwd{g}")
        dw_pat.append(
            mm_tn(
                h2, dqkv_g, name=f"b_dw_qkv{g}", p=1, tk=1024, ps=1, tn=D_MODEL, tmc=2048, out_dtype=BF16,
                g_streams=dil[g],
            )[0]
        )
        if g == N_PAT - 1:
            break
        (dh2,) = mm_nt(
            dqkv_g, w_qkv3, name=f"b_dh{g}", tm=2048, tko=512, ps=1, tc=pat_tiles * D_MODEL, outs=(F32,),
            b_tile0=g, g_streams=dil[g],
            epilogue=ident if g == 0 else (lambda acc, prev, d=dil[g]: (_from_streams(acc, d) + prev,)),
            extras=[] if g == 0 else [(dh2, "full")],
        )
    dw_bout = mm_tn(o_b, dy2, name="b_dw_out", p=1, tk=1024, ps=1, tn=D_MODEL, tmc=2048, out_dtype=BF16)
    dw_qkv = columns_to_shards(dw_pat, name="dw_qkv_shards")
    rs_b = scatter([dw_qkv, dw_bout], dqkv_g, "scatter_b")
    dz1, dy1, st1 = mm_nt(
        dqkv_g, w_qkv3, name=f"b_dh{N_PAT - 1}", tm=512, tko=D_MODEL, ps=1, tc=pat_tiles * D_MODEL, outs=(F32, BF16),
        b_tile0=N_PAT - 1, g_streams=dil[-1], stats=True,
        epilogue=lambda acc, prev, *rest: residual_bwd_epilogue(_from_streams(acc, dil[-1]) + prev, *rest),
        extras=[(dh2, "full")] + residual_bwd_extras(dz2, 1), after=rs_b.token,
    )
    rs_mlp0, dz0, dy0, st0 = mlp_backward(0, h1, a1, dy1, dz1, 0)
    (dp_gate,) = mm_nt(dy0, w_aout3, name="a_dp", tm=2048, tko=1024, ps=1, tc=D_MODEL, epilogue=ident, outs=(F32,))
    dw_aout = mm_tn(p_gate, dy0, name="a_dw_out", p=1, tk=1024, ps=1, tn=D_MODEL, tmc=2048, out_dtype=BF16)
    rs_aout = scatter([dw_aout], dp_gate, "scatter_a_out")
    da0, d_ws, d_bs, gate_rows = gate_bwd(a_pre, dp_gate, vn_g + rs_aout.zero(), vn_b, w_s, bias_full, name="gate_bwd")
    rs_ws = scatter([d_ws], gate_rows, "scatter_a_w_s")
    dw_in = mm_tn(h0, da0, name="a_dw_in", p=N_DEV, tk=1024, ps=4, tn=256, tmc=2048, out_dtype=BF16)
    rs_in = scatter([dw_in], rs_ws.token, "scatter_a_in")
    grad_x, stf = mm_nt(
        da0, w_in3, name="a_dh", tm=1024, tko=D_MODEL, ps=N_DEV, tc=256, epilogue=input_bwd_epilogue,
        extras=[(dz0, "full"), (scale[0], "row"), (x0, "full")], outs=(F32,), stats=True, after=rs_in.token,
    )

    results = {}

    def update(wname, gparts, w, m, v):
        shape = w.shape
        layers = len(gparts) if isinstance(gparts, list) else 1
        w3 = w.reshape(layers, -1, shape[-1])
        parts = [g.reshape(g.shape[0], *w3.shape[1:]) for g in (gparts if layers > 1 else [gparts])]
        outs = adamw(parts, w3, m.reshape(w3.shape), v.reshape(w3.shape), name=f"adamw_{wname}")
        results[wname] = [o.reshape(shape) for o in outs]

    g_up1, g_dn1 = rs_mlp1.wait([0, 1], grad_x, name="scatter_wait_mlp1")
    g_qkv, g_bout = rs_b.wait([0, 1], grad_x, name="scatter_wait_b")
    update("b_w_qkv", g_qkv, b_w_qkv, m_b_w_qkv, v_b_w_qkv)
    update("b_w_out", g_bout, b_w_out, m_b_w_out, v_b_w_out)
    g_up0, g_dn0 = rs_mlp0.wait([0, 1], grad_x, name="scatter_wait_mlp0")
    update("mlp_w_up", [g_up0, g_up1], mlp_w_up, m_mlp_w_up, v_mlp_w_up)
    update("mlp_w_down", [g_dn0, g_dn1], mlp_w_down, m_mlp_w_down, v_mlp_w_down)

    stats_after = [stf, st0, st1, st2]
    stats_own = [st0, st1, st2, st3]
    dm = jnp.stack(
        [
            jnp.concatenate(
                [stats_after[t][ST_DSHIFT], stats_after[t][ST_DSCALE], stats_own[t][ST_DGATE]], axis=0
            )
            for t in range(N_SUB)
        ]
    )
    ncol = 3 * D_MODEL // N_DEV
    dmx = jnp.pad(
        dm.reshape(N_SUB, N_DEV, ncol).transpose(1, 0, 2)[:, :, None, :], ((0, 0), (0, 0), (0, SUBLANES - 1), (0, 0))
    )
    small = [
        gate_rows[0],
        gate_rows[1],
        d_bs[:, :A_GROUPS].T.reshape(-1),
        *[stats_own[t][ST_DG] for t in range(N_SUB)],
        *[stats_own[t][ST_DB] for t in range(N_SUB)],
        jnp.pad(loss_local.reshape(1), (0, LANES - 1)),
    ]
    n_small = sum(s.size for s in small)
    part_rows = -(-n_small // (N_DEV * LANES * SUBLANES)) * SUBLANES
    flat = jnp.concatenate(small + [jnp.zeros((N_DEV * part_rows * LANES - n_small,), F32)])
    (ws_parts,) = rs_ws.wait([0], results["mlp_w_down"][3], name="scatter_wait_a_w_s")
    dmrecv, reduced = small_exchange(dmx, flat.reshape(N_DEV, part_rows, LANES), ws_parts, results["mlp_w_down"][3])
    g_ws = reduced[:, part_rows:, :].reshape(-1)
    reduced = reduced[:, :part_rows, :].reshape(-1)
    sizes = [2 * D_MODEL, D_MODEL, D_MODEL, A_GROUPS * CHUNK, N_SUB * D_MODEL, N_SUB * D_MODEL]
    offs = [sum(sizes[:i]) for i in range(len(sizes) + 1)]
    g_b_in, g_vn_g, g_vn_b, g_bs, g_lng, g_lnb = [reduced[offs[i] : offs[i + 1]] for i in range(len(sizes))]
    loss = reduced[offs[-1]]

    ada_outs = ada_grad_adamw(cg, dmrecv, ada_w4, m_ada_w.reshape(ada_w4.shape), v_ada_w.reshape(ada_w4.shape), name="ada_grad_adamw")
    results["ada_w"] = [o.reshape(ada_w.shape) for o in ada_outs[:4]]
    ln_cols = D_MODEL // N_DEV
    my_ln = lambda gfull: lax.dynamic_slice_in_dim(gfull.reshape(N_SUB, N_DEV, ln_cols), me, 1, axis=1)
    small_params = [
        ("ada_b", ada_outs[4][:, 0, :], ada_b, m_ada_b, v_ada_b),
        ("ln_g", my_ln(g_lng), ln_g, m_ln_g, v_ln_g),
        ("ln_b", my_ln(g_lnb), ln_b, m_ln_b, v_ln_b),
        ("a_b_in", g_b_in, a_b_in, m_a_b_in, v_a_b_in),
        ("a_vn_g", g_vn_g, a_vn_g, m_a_vn_g, v_a_vn_g),
        ("a_vn_b", g_vn_b, a_vn_b, m_a_vn_b, v_a_vn_b),
        ("a_b_s", g_bs, a_b_s, m_a_b_s, v_a_b_s),
    ]
    small_outs = adamw_small(
        [[a.reshape(-1, w.shape[-1]) for a in (g, w, m, v)] for _, g, w, m, v in small_params], name="adamw_small"
    )
    for (wname, _, w, _, _), outs in zip(small_params, small_outs, strict=True):
        results[wname] = [o.reshape(w.shape) for o in outs]
    update("a_w_s", g_ws[None], a_w_s, m_a_w_s, v_a_w_s)
    (g_aout,) = rs_aout.wait([0], grad_x, name="scatter_wait_a_out")
    (g_in,) = rs_in.wait([0], grad_x, name="scatter_wait_a_in")
    update("a_w_in", g_in, a_w_in, m_a_w_in, v_a_w_in)
    update("a_w_out", g_aout, a_w_out, m_a_w_out, v_a_w_out)

    order = ["ada_w", "ada_b", "ln_g", "ln_b", "a_w_in", "a_b_in", "a_vn_g", "a_vn_b", "a_w_s", "a_b_s", "a_w_out", "b_w_qkv", "b_w_out", "mlp_w_up", "mlp_w_down"]
    return (loss, grad_x[None], *[results[n][0] for n in order], *[results[n][1] for n in order],
            *[results[n][2] for n in order], *[results[n][3] for n in order])
```

```python
import math

import jax
import jax.numpy as jnp
import numpy as np
from jax import lax
from jax.experimental import pallas as pl
from jax.experimental.pallas import tpu as pltpu

F32 = jnp.float32
BF16 = jnp.bfloat16
MESH = pl.DeviceIdType.MESH
ANY = pl.BlockSpec(memory_space=pl.ANY)
VMEM = pl.BlockSpec(memory_space=pltpu.VMEM)

N_DEV = 8
D_MODEL = 1024
SEQ = 4096
DEPTH = 2
CHUNK = 128
A_GROUPS = 16
A_GROUP_DIM = D_MODEL // A_GROUPS
B_HEADS = 16
B_HEAD_DIM = 64
B_PATTERNS = ((128, 1), (512, 4), (2048, 16))
N_PAT = len(B_PATTERNS)
SPAN = 128
D_FF = 4 * D_MODEL
D_QKV = N_PAT * 3 * D_MODEL
ALPHA = (2 * DEPTH) ** 0.25
LN_EPS = 1e-5
NEG = -1e30
ADAM_LR = 0.001
ADAM_B1 = 0.9
ADAM_B2 = 0.999
ADAM_EPS = 1e-08
ADAM_WD = 0.01
ADAM_STEP = 10
GELU_C = math.sqrt(2.0 / math.pi)
GELU_A = 0.044715

VMEM_LIMIT_BYTES = 56 * 1024 * 1024
LANES = 128
SUBLANES = 8
ROW_TILE = 512
N_SUB = 2 * DEPTH


def _params(sem):
    return pltpu.CompilerParams(dimension_semantics=sem, vmem_limit_bytes=VMEM_LIMIT_BYTES)


def _lane(shape):
    return lax.broadcasted_iota(jnp.int32, shape, len(shape) - 1)


def _split_bf16(x):
    hi = x.astype(BF16)
    lo = (x - hi.astype(F32)).astype(BF16)
    return hi, lo


def _group_expand_matrix(groups_padded, width):
    per = width // A_GROUPS
    r = lax.broadcasted_iota(jnp.int32, (groups_padded, width), 0)
    c = lax.broadcasted_iota(jnp.int32, (groups_padded, width), 1)
    return (c // per == r).astype(BF16)


def _group_reduce_matrix(width, groups_padded):
    per = width // A_GROUPS
    r = lax.broadcasted_iota(jnp.int32, (width, groups_padded), 0)
    c = lax.broadcasted_iota(jnp.int32, (width, groups_padded), 1)
    return (r // per == c).astype(BF16)


def _expand_groups(w):
    e = _group_expand_matrix(LANES, D_MODEL)
    hi, lo = _split_bf16(w)
    return jnp.dot(hi, e, preferred_element_type=F32) + jnp.dot(lo, e, preferred_element_type=F32)


def _reduce_groups(x):
    e = _group_reduce_matrix(D_MODEL, LANES)
    hi, lo = _split_bf16(x)
    return jnp.dot(hi, e, preferred_element_type=F32) + jnp.dot(lo, e, preferred_element_type=F32)


def _to_streams(x, d):
    rows, w = x.shape
    return jnp.swapaxes(x.reshape(rows // d, d, w), 0, 1).reshape(rows, w)


def _from_streams(x, d):
    rows, w = x.shape
    return jnp.swapaxes(x.reshape(d, rows // d, w), 0, 1).reshape(rows, w)


def _column_tiles(p, n, ps, tn):
    assert (ps == 1 or tn == n) and p % ps == 0 and n % tn == 0
    q = n // tn
    return (p // ps) * q, q


def _extra_specs(extras, tm, width):
    specs = []
    for _, kind in extras:
        if kind == "row":
            specs.append(pl.BlockSpec((1, width), lambda i, j, c: (0, j)))
        else:
            specs.append(pl.BlockSpec((tm, width), lambda i, j, c: (i, j)))
    return specs


def mm_nn(a, b3, *, name, tm, ps, tn, tk, epilogue, extras=(), outs, prologue=None, b_tile0=0, b_tiles=None, out_streams=1):
    m, k = a.shape
    p, _, n = b3.shape
    nj, q = _column_tiles(p, n, ps, tn)
    nj = nj if b_tiles is None else b_tiles
    nk = k // tk
    width = ps * tn
    d = out_streams
    assert d == 1 or not extras

    def body(a_ref, b_ref, *rest):
        ex = rest[: len(extras)]
        out_refs = rest[len(extras) : len(extras) + len(outs)]
        kk = pl.program_id(2)
        av = a_ref[...] if prologue is None else prologue(a_ref[...])
        if d > 1:
            av = _to_streams(av, d)

        def finish(cs, acc):
            res = epilogue(acc, *[e[:, cs] for e in ex])
            for o_ref, r in zip(out_refs, res, strict=True):
                if d > 1:
                    o_ref[:, :, cs] = r.astype(o_ref.dtype).reshape(d, tm // d, tn)
                else:
                    o_ref[:, cs] = r.astype(o_ref.dtype)

        for s in range(ps):
            cs = slice(s * tn, (s + 1) * tn)
            part = jnp.dot(av, b_ref[s], preferred_element_type=F32)
            if nk == 1:
                finish(cs, part)
                continue
            acc_ref = rest[-1]

            @pl.when(kk == 0)
            def _(part=part, cs=cs):
                acc_ref[:, cs] = part

            @pl.when(kk > 0)
            def _(part=part, cs=cs):
                acc_ref[:, cs] += part

        if nk > 1:

            @pl.when(kk == nk - 1)
            def _():
                for s in range(ps):
                    cs = slice(s * tn, (s + 1) * tn)
                    finish(cs, rest[-1][:, cs])

    if d > 1:
        out_spec = pl.BlockSpec((d, tm // d, width), lambda i, j, kk: (0, i, j))
        out_shape = (d, m // d, nj * width)
    else:
        out_spec = pl.BlockSpec((tm, width), lambda i, j, kk: (i, j))
        out_shape = (m, nj * width)
    res = pl.pallas_call(
        body,
        name=name,
        grid=(m // tm, nj, nk),
        in_specs=[
            pl.BlockSpec((tm, tk), lambda i, j, kk: (i, kk)),
            pl.BlockSpec((ps, tk, tn), lambda i, j, kk: ((j + b_tile0) // q, kk, (j + b_tile0) % q)),
            *_extra_specs(extras, tm, width),
        ],
        out_specs=[out_spec for _ in outs],
        out_shape=[jax.ShapeDtypeStruct(out_shape, dt) for dt in outs],
        scratch_shapes=[pltpu.VMEM((tm, width), F32)] if nk > 1 else [],
        compiler_params=_params(("parallel", "parallel", "arbitrary")),
    )(a, b3, *[arr for arr, _ in extras])
    return [r.reshape(m, nj * width) for r in res]


def mm_nt(
    g, b3, *, name, tm, tko, ps, tc, epilogue, extras=(), outs, out_widths=None, out_streams=None, b_tile0=0,
    g_streams=1, stats=False, after=None,
):
    m, width = g.shape
    p, k, n = b3.shape
    _, q = _column_tiles(p, n, ps, tc)
    nc = width // (ps * tc)
    ds = g_streams
    assert not stats or tko == k
    widths = [tko] * len(outs) if out_widths is None else out_widths
    streams_out = [1] * len(outs) if out_streams is None else out_streams

    def out_spec(w, d):
        if d == 1:
            return pl.BlockSpec((tm, w), lambda i, j, c: (i, j))
        return pl.BlockSpec((d, tm // d, w), lambda i, j, c: (0, i, j))

    def out_struct(dt, w, d):
        cols = w * (k // tko)
        return jax.ShapeDtypeStruct((m, cols) if d == 1 else (d, m // d, cols), dt)

    order = [] if after is None else [after]
    n_in = len(extras) + len(order)

    def body(g_ref, b_ref, *rest):
        ex = rest[: len(extras)]
        out_refs = rest[n_in : n_in + len(outs)]
        c = pl.program_id(2)
        gv = g_ref[...].reshape(tm, ps * tc) if ds > 1 else g_ref[...]
        part = None
        for s in range(ps):
            d = lax.dot_general(gv[:, s * tc : (s + 1) * tc], b_ref[s], _NT, preferred_element_type=F32)
            part = d if part is None else part + d

        def finish(acc):
            res = epilogue(acc, *[e[...] for e in ex])
            if stats:
                *res, sums = res
                st_ref = rest[n_in + len(outs)]

                @pl.when(pl.program_id(0) == 0)
                def _():
                    st_ref[...] = jnp.zeros_like(st_ref)

                for row, val in enumerate(sums):
                    st_ref[row : row + 1, :] += val
            for o_ref, r, d_out in zip(out_refs, res, streams_out, strict=True):
                r = r.astype(o_ref.dtype)
                o_ref[...] = r if d_out == 1 else _to_streams(r, d_out).reshape(o_ref.shape)

        if nc == 1:
            finish(part)
            return
        acc_ref = rest[-1]

        @pl.when(c == 0)
        def _():
            acc_ref[...] = part

        @pl.when(c > 0)
        def _():
            acc_ref[...] += part

        @pl.when(c == nc - 1)
        def _():
            finish(acc_ref[...])

    return pl.pallas_call(
        body,
        name=name,
        grid=(m // tm, k // tko, nc),
        in_specs=[
            pl.BlockSpec((ds, tm // ds, ps * tc), lambda i, j, c: (0, i, c))
            if ds > 1
            else pl.BlockSpec((tm, ps * tc), lambda i, j, c: (i, c)),
            pl.BlockSpec((ps, tko, tc), lambda i, j, c: ((c + b_tile0) // q, j, (c + b_tile0) % q)),
            *_extra_specs(extras, tm, tko),
            *[ANY for _ in order],
        ],
        out_specs=[out_spec(w, d) for w, d in zip(widths, streams_out, strict=True)]
        + [pl.BlockSpec((SUBLANES, tko), lambda i, j, c: (0, 0))] * stats,
        out_shape=[out_struct(dt, w, d) for dt, w, d in zip(outs, widths, streams_out, strict=True)]
        + [jax.ShapeDtypeStruct((SUBLANES, k), F32)] * stats,
        scratch_shapes=[pltpu.VMEM((tm, tko), F32)] if nc > 1 else [],
        compiler_params=_params(("arbitrary" if stats else "parallel", "parallel", "arbitrary")),
    )(g.reshape(ds, m // ds, width) if ds > 1 else g, b3, *[arr for arr, _ in extras], *order)


def mm_tn(a, g, *, name, p, tk, ps, tn, tmc, out_dtype, prologue=None, g_streams=1):
    m, k = a.shape
    width = g.shape[1]
    n = width // p
    nj, q = _column_tiles(p, n, ps, tn)
    nc = m // tmc
    ds = g_streams

    def body(a_ref, g_ref, o_ref, acc_ref):
        c = pl.program_id(2)
        av = a_ref[...] if prologue is None else prologue(a_ref[...])
        gv = g_ref[...]
        if ds > 1:
            av, gv = _to_streams(av, ds), gv.reshape(tmc, ps * tn)
        part = lax.dot_general(av, gv, (((0,), (0,)), ((), ())), preferred_element_type=F32)

        @pl.when(c == 0)
        def _():
            acc_ref[...] = part

        @pl.when(c > 0)
        def _():
            acc_ref[...] += part

        @pl.when(c == nc - 1)
        def _():
            for s in range(ps):
                o_ref[s] = acc_ref[:, s * tn : (s + 1) * tn].astype(o_ref.dtype)

    return pl.pallas_call(
        body,
        name=name,
        grid=(k // tk, nj, nc),
        in_specs=[
            pl.BlockSpec((tmc, tk), lambda i, j, c: (c, i)),
            pl.BlockSpec((ds, tmc // ds, ps * tn), lambda i, j, c: (0, c, j))
            if ds > 1
            else pl.BlockSpec((tmc, ps * tn), lambda i, j, c: (c, j)),
        ],
        out_specs=pl.BlockSpec((ps, tk, tn), lambda i, j, c: (j // q, i, j % q)),
        out_shape=jax.ShapeDtypeStruct((p, k, n), out_dtype),
        scratch_shapes=[pltpu.VMEM((tk, ps * tn), F32)],
        compiler_params=_params(("parallel", "parallel", "arbitrary")),
    )(a, g.reshape(ds, m // ds, width) if ds > 1 else g)


def _rows(cols):
    return pl.BlockSpec((ROW_TILE, cols), lambda i: (i, 0))


def _vec(cols, rows=1):
    return pl.BlockSpec((rows, cols), lambda i: (0, 0))


def _layer_norm_hat(z):
    mu = jnp.mean(z, axis=-1, keepdims=True)
    zc = z - mu
    var = jnp.mean(zc * zc, axis=-1, keepdims=True)
    rstd = lax.rsqrt(var + LN_EPS)
    return zc * rstd, rstd


def modulate(x, scale, shift, *, name):
    s, d = x.shape

    def body(x_ref, sc_ref, sh_ref, h_ref):
        h_ref[...] = (x_ref[...] * (1.0 + sc_ref[...]) + sh_ref[...]).astype(BF16)

    return pl.pallas_call(
        body,
        name=name,
        grid=(s // ROW_TILE,),
        in_specs=[_rows(d), _vec(d), _vec(d)],
        out_specs=_rows(d),
        out_shape=jax.ShapeDtypeStruct((s, d), BF16),
        compiler_params=_params(("parallel",)),
    )(x, scale, shift)


def residual_ln_epilogue(acc, x, gate1, g, b, nscale, nshift):
    y = acc.astype(BF16)
    xhat, _ = _layer_norm_hat(ALPHA * x + gate1 * y.astype(F32))
    xn = xhat * g + b
    return y, xn, xn * (1.0 + nscale) + nshift


ST_DSCALE, ST_DSHIFT, ST_DG, ST_DB, ST_DGATE = 0, 1, 2, 3, 4


def _layer_norm_bwd(g_out, xhat, rstd, g):
    dxh = g_out * g
    m1 = jnp.mean(dxh, axis=-1, keepdims=True)
    m2 = jnp.mean(dxh * xhat, axis=-1, keepdims=True)
    return rstd * (dxh - m1 - xhat * m2)


def _column_sums(vals):
    return [jnp.sum(v, axis=0, keepdims=True) for v in vals]


def residual_bwd_epilogue(dh, dzl, scl, x, y, gate1, g, b):
    yf = y.astype(F32)
    xhat, rstd = _layer_norm_hat(ALPHA * x + gate1 * yf)
    g_out = ALPHA * dzl + dh * (1.0 + scl)
    dz = _layer_norm_bwd(g_out, xhat, rstd, g)
    return dz, dz * gate1, _column_sums([dh * (xhat * g + b), dh, g_out * xhat, g_out, dz * yf])


def input_bwd_epilogue(dh, dzl, scl, x):
    return ALPHA * dzl + dh * (1.0 + scl), _column_sums([dh * x, dh])


def residual_ln_loss_bwd(x, y, gate1, g, b, target, *, name):
    s, d = x.shape

    def body(x_ref, y_ref, gt_ref, g_ref, b_ref, t_ref, dz_ref, dy_ref, st_ref, loss_ref):
        y = y_ref[...].astype(F32)
        gate1 = gt_ref[...]
        xhat, rstd = _layer_norm_hat(ALPHA * x_ref[...] + gate1 * y)
        err = xhat * g_ref[...] + b_ref[...] - t_ref[...]
        g_out = err * (1.0 / d)
        dz = _layer_norm_bwd(g_out, xhat, rstd, g_ref[...])
        dz_ref[...] = dz
        dy_ref[...] = (dz * gate1).astype(BF16)
        part = jnp.sum(jnp.sum(err * err, axis=1, keepdims=True), axis=0, keepdims=True) * (0.5 / d)

        @pl.when(pl.program_id(0) == 0)
        def _():
            st_ref[...] = jnp.zeros_like(st_ref)
            loss_ref[...] = jnp.zeros_like(loss_ref)

        loss_ref[...] += part
        for row, val in zip((ST_DG, ST_DB, ST_DGATE), _column_sums([g_out * xhat, g_out, dz * y]), strict=True):
            st_ref[row : row + 1, :] += val

    return pl.pallas_call(
        body,
        name=name,
        grid=(s // ROW_TILE,),
        in_specs=[_rows(d), _rows(d), _vec(d), _vec(d), _vec(d), _rows(d)],
        out_specs=[_rows(d), _rows(d), _vec(d, SUBLANES), pl.BlockSpec((1, 1), lambda i: (0, 0))],
        out_shape=[
            jax.ShapeDtypeStruct((s, d), F32),
            jax.ShapeDtypeStruct((s, d), BF16),
            jax.ShapeDtypeStruct((SUBLANES, d), F32),
            jax.ShapeDtypeStruct((1, 1), F32),
        ],
        compiler_params=_params(("arbitrary",)),
    )(x, y, gate1, g, b, target)


GATE_CHUNKS = 4


def _gelu(x, with_grad=False):
    x2 = x * x
    t = jnp.tanh(GELU_C * (x + GELU_A * x2 * x))
    half = 0.5 * (1.0 + t)
    y = x * half
    if not with_grad:
        return y
    return y, half + 0.5 * x * (1.0 - t * t) * (GELU_C * (1.0 + 3.0 * GELU_A * x2))


def _causal_weights(w_ref, transpose):
    t = lax.broadcasted_iota(jnp.int32, (CHUNK, CHUNK), 0)
    s = lax.broadcasted_iota(jnp.int32, (CHUNK, CHUNK), 1)
    out = []
    for g in range(A_GROUPS):
        w = jnp.where(t >= s, w_ref[g], 0.0)
        out.append((w.T if transpose else w).astype(BF16))
    return out


def _spatial(ws, vn, lo_mask):
    rows = vn.shape[0]
    out_rows = []
    for r in range(rows // CHUNK):
        cols = []
        for j in range(A_GROUPS // 2):
            blk = vn[r * CHUNK : (r + 1) * CHUNK, j * LANES : (j + 1) * LANES]
            za = jnp.dot(ws[2 * j], blk, preferred_element_type=F32)
            zb = jnp.dot(ws[2 * j + 1], blk, preferred_element_type=F32)
            cols.append(jnp.where(lo_mask, za, zb))
        out_rows.append(jnp.concatenate(cols, axis=1))
    return jnp.concatenate(out_rows, axis=0)


def _gate_forward(a, vg, vb, ws, bias, lo_mask, with_grad=False):
    u = _gelu(a[:, :D_MODEL], with_grad)
    v = _gelu(a[:, D_MODEL:], with_grad)
    gu, gv = None, None
    if with_grad:
        (u, gu), (v, gv) = u, v
    vhat, rstd = _layer_norm_hat(v)
    vn = (vhat * vg + vb).astype(BF16)
    z = _spatial(ws, vn, lo_mask) + jnp.concatenate([bias] * (a.shape[0] // CHUNK), axis=0)
    return u, vhat, rstd, vn, z, gu, gv


def gate_fwd(a_pre, vn_g, vn_b, w_s, bias_full, *, name):
    s = a_pre.shape[0]
    tr = GATE_CHUNKS * CHUNK

    def body(a_ref, vg_ref, vb_ref, w_ref, bias_ref, p_ref):
        lo_mask = _lane((CHUNK, LANES)) < A_GROUP_DIM
        ws = _causal_weights(w_ref, transpose=False)
        u, _, _, _, z, _, _ = _gate_forward(a_ref[...].astype(F32), vg_ref[...], vb_ref[...], ws, bias_ref[...], lo_mask)
        p_ref[...] = (u * z).astype(BF16)

    return pl.pallas_call(
        body,
        name=name,
        grid=(s // tr,),
        in_specs=[
            pl.BlockSpec((tr, 2 * D_MODEL), lambda i: (i, 0)),
            _vec(D_MODEL),
            _vec(D_MODEL),
            pl.BlockSpec((A_GROUPS, CHUNK, CHUNK), lambda i: (0, 0, 0)),
            _vec(D_MODEL, CHUNK),
        ],
        out_specs=pl.BlockSpec((tr, D_MODEL), lambda i: (i, 0)),
        out_shape=jax.ShapeDtypeStruct((s, D_MODEL), BF16),
        compiler_params=_params(("parallel",)),
    )(a_pre, vn_g, vn_b, w_s, bias_full)


def gate_bwd(a_pre, dp, vn_g, vn_b, w_s, bias_full, *, name):
    s = a_pre.shape[0]
    tr = GATE_CHUNKS * CHUNK
    nsteps = s // tr

    def body(a_ref, dp_ref, vg_ref, vb_ref, w_ref, bias_ref, da_ref, dw_ref, dbs_ref, rows_ref, dbias_acc):
        step = pl.program_id(0)
        lo_mask = _lane((CHUNK, LANES)) < A_GROUP_DIM

        @pl.when(step == 0)
        def _():
            dw_ref[...] = jnp.zeros_like(dw_ref)
            rows_ref[...] = jnp.zeros_like(rows_ref)
            dbias_acc[...] = jnp.zeros_like(dbias_acc)

        a = a_ref[...].astype(F32)
        vg = vg_ref[...]
        ws = _causal_weights(w_ref, transpose=False)
        wts = _causal_weights(w_ref, transpose=True)
        u, vhat, rstd, vn, z, gelu_du, gelu_dv = _gate_forward(a, vg, vb_ref[...], ws, bias_ref[...], lo_mask, True)
        dp = dp_ref[...]
        du = dp * z
        dzz = dp * u
        dzz_b = dzz.astype(BF16)
        dvn = _spatial(wts, dzz_b, lo_mask)
        dbias = None
        for r in range(GATE_CHUNKS):
            rs = slice(r * CHUNK, (r + 1) * CHUNK)
            dbias = dzz[rs] if dbias is None else dbias + dzz[rs]
            for j in range(A_GROUPS // 2):
                cs = slice(j * LANES, (j + 1) * LANES)
                dblk = dzz[rs, cs]
                vblk = vn[rs, cs]
                for half in range(2):
                    keep = lo_mask if half == 0 else jnp.logical_not(lo_mask)
                    dm = jnp.where(keep, dblk, 0.0).astype(BF16)
                    dw_ref[2 * j + half] += lax.dot_general(
                        dm, vblk, (((1,), (1,)), ((), ())), preferred_element_type=F32
                    )
        dbias_acc[...] += dbias
        rows_ref[1:2, :D_MODEL] += jnp.sum(dvn * vhat, axis=0, keepdims=True)
        rows_ref[1:2, D_MODEL:] += jnp.sum(dvn, axis=0, keepdims=True)
        dvh = dvn * vg
        m1 = jnp.mean(dvh, axis=-1, keepdims=True)
        m2 = jnp.mean(dvh * vhat, axis=-1, keepdims=True)
        dv = rstd * (dvh - m1 - vhat * m2)
        da_u = du * gelu_du
        da_v = dv * gelu_dv
        da_ref[:, :D_MODEL] = da_u.astype(BF16)
        da_ref[:, D_MODEL:] = da_v.astype(BF16)
        rows_ref[0:1, :D_MODEL] += jnp.sum(da_u, axis=0, keepdims=True)
        rows_ref[0:1, D_MODEL:] += jnp.sum(da_v, axis=0, keepdims=True)

        @pl.when(step == nsteps - 1)
        def _():
            t = lax.broadcasted_iota(jnp.int32, (CHUNK, CHUNK), 0)
            sx = lax.broadcasted_iota(jnp.int32, (CHUNK, CHUNK), 1)
            for g in range(A_GROUPS):
                dw_ref[g] = jnp.where(t >= sx, dw_ref[g], 0.0)
            dbs_ref[...] = _reduce_groups(dbias_acc[...])

    return pl.pallas_call(
        body,
        name=name,
        grid=(nsteps,),
        in_specs=[
            pl.BlockSpec((tr, 2 * D_MODEL), lambda i: (i, 0)),
            pl.BlockSpec((tr, D_MODEL), lambda i: (i, 0)),
            _vec(D_MODEL),
            _vec(D_MODEL),
            pl.BlockSpec((A_GROUPS, CHUNK, CHUNK), lambda i: (0, 0, 0)),
            _vec(D_MODEL, CHUNK),
        ],
        out_specs=[
            pl.BlockSpec((tr, 2 * D_MODEL), lambda i: (i, 0)),
            pl.BlockSpec((A_GROUPS, CHUNK, CHUNK), lambda i: (0, 0, 0)),
            _vec(LANES, CHUNK),
            _vec(2 * D_MODEL, SUBLANES),
        ],
        out_shape=[
            jax.ShapeDtypeStruct((s, 2 * D_MODEL), BF16),
            jax.ShapeDtypeStruct((A_GROUPS, CHUNK, CHUNK), F32),
            jax.ShapeDtypeStruct((CHUNK, LANES), F32),
            jax.ShapeDtypeStruct((SUBLANES, 2 * D_MODEL), F32),
        ],
        scratch_shapes=[pltpu.VMEM((CHUNK, D_MODEL), F32)],
        compiler_params=_params(("arbitrary",)),
    )(a_pre, dp, vn_g, vn_b, w_s, bias_full)


def alibi_tables(dilation):
    qi = np.arange(SPAN)[:, None]
    ki = np.arange(2 * SPAN)[None, :]
    diff = SPAN + qi - ki
    valid = (diff >= 0) & (diff <= SPAN)
    heads = np.arange(1, B_HEADS + 1, dtype=np.float32)
    slopes = np.exp2(np.float32(-8.0) * heads / np.float32(B_HEADS)).astype(np.float32)
    bias = -slopes[:, None, None] * (dilation * diff).astype(np.float32)
    bias = np.where(valid[None], bias, np.float32(NEG)).reshape(B_HEADS // 2, 2 * SPAN, 2 * SPAN)
    return jnp.asarray(bias), jnp.asarray(np.ascontiguousarray(bias.transpose(0, 2, 1)))


def _pair_rows(x, halves):
    return jnp.concatenate([x * halves[0], x * halves[1]], axis=0)


def _pair_column(v, lane, j):
    pick = lambda h: jnp.sum(jnp.where(lane == h, v, 0.0), axis=1, keepdims=True)
    return jnp.concatenate([pick(2 * j), pick(2 * j + 1)], axis=0)


_NT = (((1,), (1,)), ((), ()))


def _qkv_specs(block_of):
    def spec(which, prev):
        def index(*grid):
            blk = block_of(*grid)
            return (jnp.maximum(blk - 1, 0) if prev else blk, which)

        return pl.BlockSpec((SPAN, D_MODEL), index)

    return [spec(0, False), spec(1, True), spec(1, False), spec(2, True), spec(2, False)]


def attn_fwd(qkv_p, pat, *, name, after=None):
    _, dilation = B_PATTERNS[pat]
    nb = SEQ // dilation // SPAN
    bias, _ = alibi_tables(dilation)
    order = [] if after is None else [after]

    def body(q_ref, kp_ref, kc_ref, vp_ref, vc_ref, bias_ref, *rest):
        o_ref, lse_ref = rest[len(order) :]
        n = pl.program_id(1)
        first_prev = jnp.logical_and(n == 0, _lane((2 * SPAN, 2 * SPAN)) < SPAN)
        lane = _lane((SPAN, LANES))
        lo_mask = lane < B_HEAD_DIM
        q = q_ref[...] * jnp.asarray(B_HEAD_DIM**-0.5, BF16)
        kk = jnp.concatenate([kp_ref[...], kc_ref[...]], axis=0)
        vv = jnp.concatenate([vp_ref[...], vc_ref[...]], axis=0)
        halves = (lo_mask.astype(BF16), jnp.logical_not(lo_mask).astype(BF16))
        stats = jnp.zeros((SPAN, LANES), F32)
        for j in range(B_HEADS // 2):
            cs = slice(j * LANES, (j + 1) * LANES)
            sc = lax.dot_general(_pair_rows(q[:, cs], halves), kk[:, cs], _NT, preferred_element_type=F32)
            sc = jnp.where(first_prev, NEG, sc + bias_ref[j])
            m = jnp.max(sc, axis=1, keepdims=True)
            p = jnp.exp(sc - m)
            l = jnp.sum(p, axis=1, keepdims=True)
            acc = jnp.dot(p.astype(BF16), vv[:, cs], preferred_element_type=F32) * (1.0 / l)
            lse_pair = m + jnp.log(l)
            o_ref[:, cs] = jnp.where(lo_mask, acc[:SPAN], acc[SPAN:]).astype(BF16)
            stats = jnp.where(lane == 2 * j, lse_pair[:SPAN], stats)
            stats = jnp.where(lane == 2 * j + 1, lse_pair[SPAN:], stats)
        lse_ref[...] = stats

    return pl.pallas_call(
        body,
        name=name,
        grid=(dilation, nb),
        in_specs=[
            *_qkv_specs(lambda r, n: r * nb + n),
            pl.BlockSpec((B_HEADS // 2, 2 * SPAN, 2 * SPAN), lambda r, n: (0, 0, 0)),
            *[ANY for _ in order],
        ],
        out_specs=[
            pl.BlockSpec((SPAN, D_MODEL), lambda r, n: (r * nb + n, 0)),
            pl.BlockSpec((SPAN, LANES), lambda r, n: (r * nb + n, 0)),
        ],
        out_shape=[jax.ShapeDtypeStruct((SEQ, D_MODEL), BF16), jax.ShapeDtypeStruct((SEQ, LANES), F32)],
        compiler_params=_params(("parallel", "arbitrary")),
    )(qkv_p, qkv_p, qkv_p, qkv_p, qkv_p, bias, *order)


def attn_combine(outs, lses, *, name):
    dils = [d for _, d in B_PATTERNS]

    def positions(ref, d):
        x = ref[...].astype(F32)
        return x if d == 1 else jnp.swapaxes(x, 0, 1).reshape(ROW_TILE, x.shape[-1])

    def body(o0, o1, o2, l0, l1, l2, ob_ref, of_ref, *lse_refs):
        ls = [positions(l, d) for l, d in zip((l0, l1, l2), dils, strict=True)]
        m = jnp.maximum(jnp.maximum(ls[0], ls[1]), ls[2])
        tot = jnp.log(jnp.exp(ls[0] - m) + jnp.exp(ls[1] - m) + jnp.exp(ls[2] - m)) + m
        o = None
        for o_ref, l, d in zip((o0, o1, o2), ls, dils, strict=True):
            term = _expand_groups(jnp.exp(l - tot)) * positions(o_ref, d)
            o = term if o is None else o + term
        ob_ref[...] = o.astype(BF16)
        of_ref[...] = o
        for lse_ref, d in zip(lse_refs, dils, strict=True):
            lse_ref[...] = tot if d == 1 else _to_streams(tot, d).reshape(lse_ref.shape)

    def stream_rows(cols, d):
        return _rows(cols) if d == 1 else pl.BlockSpec((d, ROW_TILE // d, cols), lambda i: (0, i, 0))

    def streams(x, d):
        return x if d == 1 else x.reshape(d, SEQ // d, x.shape[-1])

    outs = [streams(x, d) for x, d in zip(outs, dils, strict=True)]
    lses = [streams(x, d) for x, d in zip(lses, dils, strict=True)]
    return pl.pallas_call(
        body,
        name=name,
        grid=(SEQ // ROW_TILE,),
        in_specs=[stream_rows(D_MODEL, d) for d in dils] + [stream_rows(LANES, d) for d in dils],
        out_specs=[_rows(D_MODEL), _rows(D_MODEL)] + [stream_rows(LANES, d) for d in dils],
        out_shape=[jax.ShapeDtypeStruct((SEQ, D_MODEL), BF16), jax.ShapeDtypeStruct((SEQ, D_MODEL), F32)]
        + [jax.ShapeDtypeStruct(lse.shape, F32) for lse in lses],
        compiler_params=_params(("parallel",)),
    )(*outs, *lses)


def attn_bwd(qkv_p, do_p, lse_p, delta_p, pat, *, name):
    _, dilation = B_PATTERNS[pat]
    nb = SEQ // dilation // SPAN
    n_blocks = SEQ // SPAN
    bias, bias_t = alibi_tables(dilation)
    last = n_blocks - 1
    q_cols, k_cols, v_cols = (slice(i * D_MODEL, (i + 1) * D_MODEL) for i in range(3))

    def body(q_ref, kp_ref, kc_ref, vp_ref, vc_ref, do_ref, lse_ref, dl_ref, bias_ref, biast_ref, out_ref, cq_ref, ck_ref, cv_ref):
        g = pl.program_id(0)

        @pl.when(g == n_blocks)
        def _():
            out_ref[:, q_cols] = cq_ref[...].astype(BF16)
            out_ref[:, k_cols] = ck_ref[...].astype(BF16)
            out_ref[:, v_cols] = cv_ref[...].astype(BF16)

        @pl.when(g == 0)
        def _():
            cq_ref[...] = jnp.zeros_like(cq_ref)
            ck_ref[...] = jnp.zeros_like(ck_ref)
            cv_ref[...] = jnp.zeros_like(cv_ref)

        @pl.when(g < n_blocks)
        def _():
            lane = _lane((SPAN, LANES))
            lo_mask = lane < B_HEAD_DIM
            pair = (2 * SPAN, 2 * SPAN)
            first = lax.rem(g, nb) == 0
            prev_key_cols = jnp.logical_and(first, _lane(pair) < SPAN)
            prev_key_rows = jnp.logical_and(first, lax.broadcasted_iota(jnp.int32, pair, 0) < SPAN)
            q = q_ref[...] * jnp.asarray(B_HEAD_DIM**-0.5, BF16)
            kk = jnp.concatenate([kp_ref[...], kc_ref[...]], axis=0)
            vv = jnp.concatenate([vp_ref[...], vc_ref[...]], axis=0)
            do_v = do_ref[...]
            lse_v = lse_ref[...]
            dl_v = dl_ref[...]
            lse_t = lse_v.T
            dl_t = dl_v.T
            halves = (lo_mask.astype(BF16), jnp.logical_not(lo_mask).astype(BF16))
            for j in range(B_HEADS // 2):
                cs = slice(j * LANES, (j + 1) * LANES)
                kp, vp = kk[:, cs], vv[:, cs]
                q2 = _pair_rows(q[:, cs], halves)
                do2 = _pair_rows(do_v[:, cs], halves)
                lse_c, dl_c = _pair_column(lse_v, lane, j), _pair_column(dl_v, lane, j)
                lse_r = jnp.concatenate([lse_t[2 * j : 2 * j + 1], lse_t[2 * j + 1 : 2 * j + 2]], axis=1)
                dl_r = jnp.concatenate([dl_t[2 * j : 2 * j + 1], dl_t[2 * j + 1 : 2 * j + 2]], axis=1)
                sc = lax.dot_general(q2, kp, _NT, preferred_element_type=F32)
                p = jnp.exp(jnp.where(prev_key_cols, NEG, sc + bias_ref[j]) - lse_c)
                dp = lax.dot_general(do2, vp, _NT, preferred_element_type=F32)
                ds = (p * (dp - dl_c)).astype(BF16)
                dq2 = jnp.dot(ds, kp, preferred_element_type=F32)
                sc_t = lax.dot_general(kp, q2, _NT, preferred_element_type=F32)
                p_t = jnp.exp(jnp.where(prev_key_rows, NEG, sc_t + biast_ref[j]) - lse_r)
                dp_t = lax.dot_general(vp, do2, _NT, preferred_element_type=F32)
                ds_t = (p_t * (dp_t - dl_r)).astype(BF16)
                dk_pair = jnp.dot(ds_t, q2, preferred_element_type=F32)
                dv_pair = jnp.dot(p_t.astype(BF16), do2, preferred_element_type=F32)
                oq = slice(j * LANES, (j + 1) * LANES)
                ok = slice(D_MODEL + j * LANES, D_MODEL + (j + 1) * LANES)
                ov = slice(2 * D_MODEL + j * LANES, 2 * D_MODEL + (j + 1) * LANES)
                out_ref[:, oq] = cq_ref[:, cs].astype(BF16)
                out_ref[:, ok] = (ck_ref[:, cs] + dk_pair[:SPAN]).astype(BF16)
                out_ref[:, ov] = (cv_ref[:, cs] + dv_pair[:SPAN]).astype(BF16)
                cq_ref[:, cs] = jnp.where(lo_mask, dq2[:SPAN], dq2[SPAN:]) * (B_HEAD_DIM**-0.5)
                ck_ref[:, cs] = dk_pair[SPAN:]
                cv_ref[:, cs] = dv_pair[SPAN:]

    def block_of(g):
        return jnp.minimum(g, last)

    def row_spec(width):
        return pl.BlockSpec((SPAN, width), lambda g: (block_of(g), 0))

    return pl.pallas_call(
        body,
        name=name,
        grid=(n_blocks + 1,),
        in_specs=[
            *_qkv_specs(block_of),
            row_spec(D_MODEL),
            row_spec(LANES),
            row_spec(LANES),
            pl.BlockSpec((B_HEADS // 2, 2 * SPAN, 2 * SPAN), lambda g: (0, 0, 0)),
            pl.BlockSpec((B_HEADS // 2, 2 * SPAN, 2 * SPAN), lambda g: (0, 0, 0)),
        ],
        out_specs=pl.BlockSpec((SPAN, 3 * D_MODEL), lambda g: (jnp.maximum(g - 1, 0), 0)),
        out_shape=jax.ShapeDtypeStruct((SEQ, 3 * D_MODEL), BF16),
        scratch_shapes=[pltpu.VMEM((SPAN, D_MODEL), F32)] * 3,
        compiler_params=_params(("arbitrary",)),
    )(qkv_p, qkv_p, qkv_p, qkv_p, qkv_p, do_p, lse_p, delta_p, bias, bias_t)


def _position():
    x, y, c = lax.axis_index("x"), lax.axis_index("y"), lax.axis_index("c")
    return x, y, c, 4 * x + 2 * y + c


def _peer(k, x, y, c):
    px = 1 - x if k & 4 else x
    py = 1 - y if k & 2 else y
    pc = 1 - c if k & 1 else c
    return (px, py, pc), 4 * px + 2 * py + pc


def _remote(src, dst, send_sem, recv_sem, device):
    return pltpu.make_async_remote_copy(
        src_ref=src, dst_ref=dst, send_sem=send_sem, recv_sem=recv_sem, device_id=device, device_id_type=MESH
    )


def _silu_bf16(cf):
    return (cf * (1.0 / (1.0 + jnp.exp(-cf)))).astype(BF16)


def ada_exchange(c8, w4, b4, ln8):
    nt, _, ncol = w4.shape

    def body(c8_ref, w_ref, b_ref, ln_ref, cg_ref, lng_ref, mrecv_ref, mloc_ref, send_sems, recv_sems):
        x, y, c, me = _position()
        cg_ref[me] = c8_ref[...]
        lng_ref[me] = ln_ref[...]
        first = []
        for k in range(1, N_DEV):
            dev, _ = _peer(k, x, y, c)
            first.append(_remote(c8_ref, cg_ref.at[me], send_sems.at[0, k], recv_sems.at[0, k], dev))
            first.append(_remote(ln_ref, lng_ref.at[me], send_sems.at[1, k], recv_sems.at[1, k], dev))
        for cp in first:
            cp.start()
        for k in range(1, N_DEV):
            dev, pid = _peer(k, x, y, c)
            _remote(c8_ref, cg_ref.at[pid], send_sems.at[0, k], recv_sems.at[0, k], dev).wait_recv()
            _remote(ln_ref, lng_ref.at[pid], send_sems.at[1, k], recv_sems.at[1, k], dev).wait_recv()
        sc = _silu_bf16(cg_ref[...].reshape(N_DEV * SUBLANES, D_MODEL))
        for t in range(nt):
            mloc_ref[t] = jnp.dot(sc, w_ref[t].astype(BF16), preferred_element_type=F32) + b_ref[t : t + 1, :]

        def group(dev_id):
            return pl.ds(pl.multiple_of(dev_id * SUBLANES, SUBLANES), SUBLANES)

        mrecv_ref[me] = mloc_ref[:, group(me), :]
        second = []
        for k in range(1, N_DEV):
            dev, pid = _peer(k, x, y, c)
            second.append(
                _remote(mloc_ref.at[:, group(pid), :], mrecv_ref.at[me], send_sems.at[2, k], recv_sems.at[2, k], dev)
            )
        for cp in second:
            cp.start()
        for k in range(1, N_DEV):
            dev, pid = _peer(k, x, y, c)
            _remote(
                mloc_ref.at[:, group(pid), :], mrecv_ref.at[pid], send_sems.at[2, k], recv_sems.at[2, k], dev
            ).wait_recv()
        for cp in first + second:
            cp.wait_send()

    return pl.pallas_call(
        body,
        name="ada_exchange",
        in_specs=[VMEM, VMEM, VMEM, VMEM],
        out_specs=[VMEM, VMEM, VMEM],
        out_shape=[
            jax.ShapeDtypeStruct((N_DEV, SUBLANES, D_MODEL), F32),
            jax.ShapeDtypeStruct((N_DEV, SUBLANES, LANES), F32),
            jax.ShapeDtypeStruct((N_DEV, nt, SUBLANES, ncol), F32),
        ],
        scratch_shapes=[
            pltpu.VMEM((nt, N_DEV * SUBLANES, ncol), F32),
            pltpu.SemaphoreType.DMA((3, N_DEV)),
            pltpu.SemaphoreType.DMA((3, N_DEV)),
        ],
        compiler_params=pltpu.CompilerParams(vmem_limit_bytes=VMEM_LIMIT_BYTES),
    )(c8, w4, b4, ln8)


def small_exchange(dmx, flat, pre, after):
    rows = flat.shape[1]

    def body(dmx_ref, flat_ref, pre_ref, *rest):
        dmrecv_ref, red_ref, land_ref, send_sems, recv_sems = rest[len(after) :]
        x, y, c, me = _position()
        dmrecv_ref[me] = dmx_ref[me]
        land_ref[me] = flat_ref[me]
        first = []
        for k in range(1, N_DEV):
            dev, pid = _peer(k, x, y, c)
            first.append(_remote(dmx_ref.at[pid], dmrecv_ref.at[me], send_sems.at[0, k], recv_sems.at[0, k], dev))
            first.append(_remote(flat_ref.at[pid], land_ref.at[me], send_sems.at[1, k], recv_sems.at[1, k], dev))
        for cp in first:
            cp.start()
        for k in range(1, N_DEV):
            dev, pid = _peer(k, x, y, c)
            _remote(dmx_ref.at[pid], dmrecv_ref.at[pid], send_sems.at[0, k], recv_sems.at[0, k], dev).wait_recv()
            _remote(flat_ref.at[pid], land_ref.at[pid], send_sems.at[1, k], recv_sems.at[1, k], dev).wait_recv()
        total, total_pre = land_ref[0], pre_ref[0]
        for s in range(1, N_DEV):
            total, total_pre = total + land_ref[s], total_pre + pre_ref[s]
        red_ref[me, :rows, :] = total
        red_ref[me, rows:, :] = total_pre
        second = []
        for k in range(1, N_DEV):
            dev, _ = _peer(k, x, y, c)
            second.append(_remote(red_ref.at[me], red_ref.at[me], send_sems.at[2, k], recv_sems.at[2, k], dev))
        for cp in second:
            cp.start()
        for k in range(1, N_DEV):
            dev, pid = _peer(k, x, y, c)
            _remote(red_ref.at[pid], red_ref.at[pid], send_sems.at[2, k], recv_sems.at[2, k], dev).wait_recv()
        for cp in first + second:
            cp.wait_send()

    return pl.pallas_call(
        body,
        name="small_exchange",
        in_specs=[VMEM, VMEM, VMEM, *[ANY for _ in after]],
        out_specs=[VMEM, VMEM],
        out_shape=[
            jax.ShapeDtypeStruct(dmx.shape, F32),
            jax.ShapeDtypeStruct((N_DEV, rows + pre.shape[1], LANES), F32),
        ],
        scratch_shapes=[
            pltpu.VMEM(flat.shape, F32),
            pltpu.SemaphoreType.DMA((3, N_DEV)),
            pltpu.SemaphoreType.DMA((3, N_DEV)),
        ],
        compiler_params=pltpu.CompilerParams(vmem_limit_bytes=VMEM_LIMIT_BYTES),
    )(dmx, flat, pre, *after)


HBM = pl.BlockSpec(memory_space=pltpu.HBM)
SEM = pl.BlockSpec(memory_space=pltpu.SEMAPHORE)
EFFECT = pltpu.SideEffectType.DATAFLOW_SIDE_EFFECTING


REGROUP_ROWS = 256


def shards_to_columns(x, *, name):
    p, k, n = x.shape

    def body(x_ref, o_ref):
        for s in range(p):
            o_ref[:, s * n : (s + 1) * n] = x_ref[s]

    return pl.pallas_call(
        body,
        name=name,
        grid=(k // REGROUP_ROWS,),
        in_specs=[pl.BlockSpec((p, REGROUP_ROWS, n), lambda i: (0, i, 0))],
        out_specs=pl.BlockSpec((REGROUP_ROWS, p * n), lambda i: (i, 0)),
        out_shape=jax.ShapeDtypeStruct((k, p * n), x.dtype),
        compiler_params=_params(("parallel",)),
    )(x)


def columns_to_shards(xs, *, name):
    k = xs[0].shape[0]
    widths = [x.shape[1] for x in xs]
    n = sum(widths) // N_DEV
    pieces = []
    for s in range(N_DEV):
        start = 0
        for i, w in enumerate(widths):
            lo, hi = max(start, s * n), min(start + w, (s + 1) * n)
            if lo < hi:
                pieces.append((i, lo - start, s, lo - s * n, hi - lo))
            start += w

    def body(*refs):
        x_refs, o_ref = refs[: len(xs)], refs[-1]
        for i, c0, s, d0, w in pieces:
            o_ref[s, :, d0 : d0 + w] = x_refs[i][:, c0 : c0 + w]

    return pl.pallas_call(
        body,
        name=name,
        grid=(k // REGROUP_ROWS,),
        in_specs=[pl.BlockSpec((REGROUP_ROWS, w), lambda i: (i, 0)) for w in widths],
        out_specs=pl.BlockSpec((N_DEV, REGROUP_ROWS, n), lambda i: (0, i, 0)),
        out_shape=jax.ShapeDtypeStruct((N_DEV, k, n), xs[0].dtype),
        compiler_params=_params(("parallel",)),
    )(*xs)


def _own_slot(me, block):
    land = lax.empty((N_DEV, *block.shape), block.dtype)
    return lax.dynamic_update_slice_in_dim(land, block[None], me, axis=0)


N_CHIP_PEERS = 3


class Gather:
    def __init__(self, shards, lands, after, *, name):
        nt = len(shards)
        self.name = name

        def body(*refs):
            src_refs, land_refs = refs[:nt], refs[nt : 2 * nt]
            send_sems, recv_sems = refs[2 * nt + 1 : 3 * nt + 1], refs[3 * nt + 1 : 4 * nt + 1]
            token = refs[-1]
            x, y, c, me = _position()
            for t in range(nt):
                for k, dev in enumerate(self._targets(x, y, c)):
                    _remote(src_refs[t], land_refs[t].at[me], send_sems[t].at[k], recv_sems[t].at[k], dev).start()
            token[...] = jnp.zeros_like(token)

        outs = pl.pallas_call(
            body,
            name=name + "_start",
            in_specs=[HBM] * (2 * nt) + [ANY],
            out_specs=[SEM] * (2 * nt) + [HBM] * (2 * nt) + [VMEM],
            out_shape=[pltpu.SemaphoreType.DMA((1 + N_CHIP_PEERS,))] * (2 * nt)
            + [pltpu.HBM(a.shape, a.dtype) for a in (*shards, *lands)]
            + [jax.ShapeDtypeStruct((SUBLANES, LANES), F32)],
            input_output_aliases={i: 2 * nt + i for i in range(2 * nt)},
            compiler_params=pltpu.CompilerParams(has_side_effects=EFFECT),
        )(*[pltpu.with_memory_space_constraint(a, pltpu.HBM) for a in (*shards, *lands)], after)
        self.send_sems, self.recv_sems = list(outs[:nt]), list(outs[nt : 2 * nt])
        self.srcs, self.lands = list(outs[2 * nt : 3 * nt]), list(outs[3 * nt : 4 * nt])
        self.token = outs[-1]

    @staticmethod
    def _chips(x, y):
        return [(1 - x, y), (x, 1 - y), (1 - x, 1 - y)]

    @classmethod
    def _targets(cls, x, y, c):
        return [(x, y, 1 - c)] + [(*chip, c) for chip in cls._chips(x, y)]

    def zero(self):
        return self.token[0, 0]

    @staticmethod
    def _slot(px, py, pc):
        return 4 * px + 2 * py + pc

    def pass_on(self, which, after, *, name):
        n = len(which)

        def pass_body(*refs):
            land_refs, recv_sems = refs[:n], refs[n : 2 * n]
            fwd_send, fwd_recv = refs[3 * n + 1 : 4 * n + 1], refs[4 * n + 1 : 5 * n + 1]
            token = refs[-1]
            x, y, c, _ = _position()
            for t in range(n):
                for j, chip in enumerate(self._chips(x, y)):
                    blk = land_refs[t].at[self._slot(*chip, c)]
                    _remote(blk, blk, fwd_send[t].at[j], recv_sems[t].at[1 + j], (*chip, c)).wait_recv()
                    _remote(blk, blk, fwd_send[t].at[j], fwd_recv[t].at[j], (x, y, 1 - c)).start()
            token[...] = jnp.zeros_like(token)

        lands = [self.lands[t] for t in which]
        outs = pl.pallas_call(
            pass_body,
            name=name,
            in_specs=[HBM] * n + [SEM] * n + [ANY],
            out_specs=[HBM] * n + [SEM] * (2 * n) + [VMEM],
            out_shape=[pltpu.HBM(a.shape, a.dtype) for a in lands]
            + [pltpu.SemaphoreType.DMA((N_CHIP_PEERS,))] * (2 * n)
            + [jax.ShapeDtypeStruct((SUBLANES, LANES), F32)],
            input_output_aliases={i: i for i in range(n)},
            compiler_params=pltpu.CompilerParams(has_side_effects=EFFECT),
        )(*lands, *[self.recv_sems[t] for t in which], after)
        return (which, outs[:n], outs[n : 2 * n], outs[2 * n : 3 * n]), outs[-1]

    def wait(self, which, after, *, name):
        return self.finish(self.pass_on(which, after, name=name + "_pass")[0], after, name=name)

    def finish(self, passed, after, *, name):
        which, lands, fwd_send, fwd_recv = passed
        n = len(which)
        slot = self._slot

        def wait_body(*refs):
            src_refs, land_refs = refs[:n], refs[n : 2 * n]
            send_sems, recv_sems = refs[2 * n : 3 * n], refs[3 * n : 4 * n]
            fwd_send, fwd_recv = refs[4 * n : 5 * n], refs[5 * n : 6 * n]
            x, y, c, me = _position()
            sibling = (x, y, 1 - c)
            for t in range(n):
                for k, dev in enumerate(self._targets(x, y, c)):
                    _remote(src_refs[t], land_refs[t].at[me], send_sems[t].at[k], recv_sems[t].at[k], dev).wait_send()
                blk = land_refs[t].at[slot(x, y, 1 - c)]
                _remote(blk, blk, send_sems[t].at[0], recv_sems[t].at[0], sibling).wait_recv()
                for j, chip in enumerate(self._chips(x, y)):
                    sent = land_refs[t].at[slot(*chip, c)]
                    _remote(sent, sent, fwd_send[t].at[j], fwd_recv[t].at[j], sibling).wait_send()
                    got = land_refs[t].at[slot(*chip, 1 - c)]
                    _remote(got, got, fwd_send[t].at[j], fwd_recv[t].at[j], sibling).wait_recv()

        srcs = [self.srcs[t] for t in which]
        outs = pl.pallas_call(
            wait_body,
            name=name,
            in_specs=[HBM] * (2 * n) + [SEM] * (4 * n) + [ANY],
            out_specs=[HBM] * (2 * n),
            out_shape=[pltpu.HBM(a.shape, a.dtype) for a in (*srcs, *lands)],
            input_output_aliases={i: i for i in range(2 * n)},
            compiler_params=pltpu.CompilerParams(has_side_effects=EFFECT),
        )(
            *srcs, *lands, *[self.send_sems[t] for t in which], *[self.recv_sems[t] for t in which], *fwd_send,
            *fwd_recv, after,
        )
        return outs[n:]


class Scatter:
    def __init__(self, srcs, after, *, name):
        self.name = name
        nt = self.nt = len(srcs)
        peers = N_DEV - 1
        lands = [lax.empty(a.shape, a.dtype) for a in srcs]

        def body(*refs):
            src_refs, land_refs = refs[:nt], refs[nt : 2 * nt]
            send_sems, recv_sems = refs[2 * nt + 1 : 3 * nt + 1], refs[3 * nt + 1 : 4 * nt + 1]
            token = refs[-1]
            x, y, c, me = _position()
            for t in range(nt):
                for k in range(1, N_DEV):
                    dev, pid = _peer(k, x, y, c)
                    src = src_refs[t].at[pid]
                    _remote(src, land_refs[t].at[me], send_sems[t].at[k - 1], recv_sems[t].at[k - 1], dev).start()
                self._own(src_refs[t], land_refs[t], send_sems[t], me).start()
            token[...] = jnp.zeros_like(token)

        outs = pl.pallas_call(
            body,
            name=name + "_start",
            in_specs=[HBM] * (2 * nt) + [ANY],
            out_specs=[SEM] * (2 * nt) + [HBM] * (2 * nt) + [VMEM],
            out_shape=[pltpu.SemaphoreType.DMA((peers + 1,))] * (2 * nt)
            + [pltpu.HBM(a.shape, a.dtype) for a in (*srcs, *lands)]
            + [jax.ShapeDtypeStruct((SUBLANES, LANES), F32)],
            input_output_aliases={i: 2 * nt + i for i in range(2 * nt)},
            compiler_params=pltpu.CompilerParams(has_side_effects=EFFECT),
        )(*[pltpu.with_memory_space_constraint(a, pltpu.HBM) for a in (*srcs, *lands)], after)
        self.send_sems, self.recv_sems = outs[:nt], outs[nt : 2 * nt]
        self.srcs, self.lands = outs[2 * nt : 3 * nt], outs[3 * nt : 4 * nt]
        self.token = outs[-1]

    @staticmethod
    def _own(src_ref, land_ref, sems, me):
        return pltpu.make_async_copy(src_ref.at[me], land_ref.at[me], sems.at[N_DEV - 1])

    def zero(self):
        return self.token[0, 0]

    def wait(self, which, after, *, name):
        n = len(which)

        def body(*refs):
            src_refs, land_refs = refs[:n], refs[n : 2 * n]
            send_sems, recv_sems = refs[2 * n : 3 * n], refs[3 * n : 4 * n]
            x, y, c, me = _position()
            for t in range(n):
                for k in range(1, N_DEV):
                    dev, pid = _peer(k, x, y, c)
                    src = src_refs[t].at[pid]
                    cp = _remote(src, land_refs[t].at[pid], send_sems[t].at[k - 1], recv_sems[t].at[k - 1], dev)
                    cp.wait_send()
                    cp.wait_recv()
                self._own(src_refs[t], land_refs[t], send_sems[t], me).wait()

        srcs = [self.srcs[t] for t in which]
        lands = [self.lands[t] for t in which]
        outs = pl.pallas_call(
            body,
            name=name,
            in_specs=[HBM] * (2 * n) + [SEM] * (2 * n) + [ANY],
            out_specs=[HBM] * (2 * n),
            out_shape=[pltpu.HBM(a.shape, a.dtype) for a in (*srcs, *lands)],
            input_output_aliases={i: i for i in range(2 * n)},
            compiler_params=pltpu.CompilerParams(has_side_effects=EFFECT),
        )(*srcs, *lands, *[self.send_sems[t] for t in which], *[self.recv_sems[t] for t in which], after)
        return outs[n:]


def _adam_update(g, w, m, v):
    m2 = ADAM_B1 * m + (1.0 - ADAM_B1) * g
    v2 = ADAM_B2 * v + (1.0 - ADAM_B2) * jnp.square(g)
    m_hat = m2 / (1.0 - ADAM_B1**ADAM_STEP)
    v_hat = v2 / (1.0 - ADAM_B2**ADAM_STEP)
    delta = -ADAM_LR * (m_hat / (jnp.sqrt(v_hat) + ADAM_EPS) + ADAM_WD * w)
    return delta, m2, v2


def adamw(gparts, w, m, v, *, name):
    nl, r, c = w.shape
    p = gparts[0].shape[0]
    tr = r if r <= 256 else (256 if c <= D_MODEL else 128)
    ni = r // tr

    def body(*refs):
        g_refs = refs[:nl]
        w_ref, m_ref, v_ref, go_ref, d_ref, mo_ref, vo_ref = refs[nl:]
        for layer in range(nl):

            @pl.when(pl.program_id(0) == layer)
            def _(g_ref=g_refs[layer]):
                g = g_ref[0].astype(F32)
                for i in range(1, p):
                    g = g + g_ref[i].astype(F32)
                delta, m2, v2 = _adam_update(g, w_ref[...], m_ref[...], v_ref[...])
                go_ref[...] = g
                d_ref[...] = delta
                mo_ref[...] = m2
                vo_ref[...] = v2

    def parts_spec(layer):
        def index(l, i):
            return (0, jnp.where(l == layer, i, jnp.where(l < layer, 0, ni - 1)), 0)

        return pl.BlockSpec((p, tr, c), index)

    blk = pl.BlockSpec((None, tr, c), lambda l, i: (l, i, 0))
    return pl.pallas_call(
        body,
        name=name,
        grid=(nl, ni),
        in_specs=[*[parts_spec(layer) for layer in range(nl)], blk, blk, blk],
        out_specs=[blk] * 4,
        out_shape=[jax.ShapeDtypeStruct((nl, r, c), F32)] * 4,
        compiler_params=_params(("arbitrary", "arbitrary")),
    )(*gparts, w, m, v)


def adamw_small(items, *, name):
    n = len(items)

    def body(*refs):
        ins, outs = refs[: 4 * n], refs[4 * n :]
        for t in range(n):
            g_ref, w_ref, m_ref, v_ref = ins[4 * t : 4 * t + 4]
            g = g_ref[...]
            delta, m2, v2 = _adam_update(g, w_ref[...], m_ref[...], v_ref[...])
            for o_ref, val in zip(outs[4 * t : 4 * t + 4], (g, delta, m2, v2), strict=True):
                o_ref[...] = val

    outs = pl.pallas_call(
        body,
        name=name,
        out_shape=[jax.ShapeDtypeStruct(item[1].shape, F32) for item in items for _ in range(4)],
    )(*[a for item in items for a in item])
    return [outs[4 * t : 4 * t + 4] for t in range(n)]


def ada_grad_adamw(cg, dmrecv, w4, m4, v4, *, name):
    nt, k, ncol = w4.shape

    def body(cg_ref, dm_ref, w_ref, m_ref, v_ref, go_ref, d_ref, mo_ref, vo_ref, gb_ref):
        sc = _silu_bf16(cg_ref[...].reshape(N_DEV * SUBLANES, k))
        dm = dm_ref[...].reshape(N_DEV * SUBLANES, ncol)
        g = lax.dot_general(sc, dm.astype(BF16), (((0,), (0,)), ((), ())), preferred_element_type=F32)
        delta, m2, v2 = _adam_update(g, w_ref[...], m_ref[...], v_ref[...])
        go_ref[...] = g
        d_ref[...] = delta
        mo_ref[...] = m2
        vo_ref[...] = v2
        gb_ref[...] = jnp.broadcast_to(jnp.sum(dm, axis=0, keepdims=True), (SUBLANES, ncol))

    wblk = pl.BlockSpec((None, k, ncol), lambda t: (t, 0, 0))
    return pl.pallas_call(
        body,
        name=name,
        grid=(nt,),
        in_specs=[
            pl.BlockSpec((N_DEV, SUBLANES, k), lambda t: (0, 0, 0)),
            pl.BlockSpec((N_DEV, None, SUBLANES, ncol), lambda t: (0, t, 0, 0)),
            wblk,
            wblk,
            wblk,
        ],
        out_specs=[wblk] * 4 + [pl.BlockSpec((None, SUBLANES, ncol), lambda t: (t, 0, 0))],
        out_shape=[jax.ShapeDtypeStruct((nt, k, ncol), F32)] * 4 + [jax.ShapeDtypeStruct((nt, SUBLANES, ncol), F32)],
        compiler_params=_params(("parallel",)),
    )(cg, dmrecv, w4, m4, v4)


def kernel(x, c, ada_w, ada_b, ln_g, ln_b, a_w_in, a_b_in, a_vn_g, a_vn_b, a_w_s, a_b_s, a_w_out, b_w_qkv, b_w_out, mlp_w_up, mlp_w_down, loss_target, m_ada_w, m_ada_b, m_ln_g, m_ln_b, m_a_w_in, m_a_b_in, m_a_vn_g, m_a_vn_b, m_a_w_s, m_a_b_s, m_a_w_out, m_b_w_qkv, m_b_w_out, m_mlp_w_up, m_mlp_w_down, v_ada_w, v_ada_b, v_ln_g, v_ln_b, v_a_w_in, v_a_b_in, v_a_vn_g, v_a_vn_b, v_a_w_s, v_a_b_s, v_a_w_out, v_b_w_qkv, v_b_w_out, v_mlp_w_up, v_mlp_w_down):
    x0 = x[0]
    target = loss_target[0]
    me = 4 * lax.axis_index("x") + 2 * lax.axis_index("y") + lax.axis_index("c")

    ada_w4 = ada_w.reshape(N_SUB, D_MODEL, -1)
    ada_b4 = ada_b.reshape(N_SUB, -1)
    ln8 = jnp.concatenate([ln_g.reshape(N_SUB, -1), ln_b.reshape(N_SUB, -1)], axis=0)
    c8 = jnp.broadcast_to(c, (SUBLANES, D_MODEL))
    cg, lng, mrecv = ada_exchange(c8, ada_w4, ada_b4, ln8)

    W_IN, W_AOUT, W_UP0, W_DN0, W_QKV, W_BOUT, W_UP1, W_DN1 = range(8)
    shards = [
        a_w_in[0].astype(BF16),
        a_w_out[0].astype(BF16),
        mlp_w_up[0].astype(BF16),
        mlp_w_down[0].astype(BF16),
        b_w_qkv[0].astype(BF16),
        b_w_out[0].astype(BF16),
        mlp_w_up[1].astype(BF16),
        mlp_w_down[1].astype(BF16),
    ]
    gather = Gather(shards, [_own_slot(me, s) for s in shards], mrecv, name="gather")

    modv = mrecv[:, :, 0, :].transpose(1, 0, 2).reshape(N_SUB, 3 * D_MODEL) + gather.zero()
    shift = [modv[t : t + 1, :D_MODEL] for t in range(N_SUB)]
    scale = [modv[t : t + 1, D_MODEL : 2 * D_MODEL] for t in range(N_SUB)]
    gate1 = [1.0 + modv[t : t + 1, 2 * D_MODEL :] for t in range(N_SUB)]
    lng_full = [lng[:, t, :].reshape(1, D_MODEL) for t in range(N_SUB)]
    lnb_full = [lng[:, N_SUB + t, :].reshape(1, D_MODEL) for t in range(N_SUB)]

    ident = lambda acc: (acc,)
    def relu2(a):
        r = jnp.maximum(a, jnp.zeros_like(a))
        return r * r
    vn_g, vn_b, w_s = a_vn_g, a_vn_b, a_w_s[0]
    bias_full = jnp.repeat(a_b_s[0].T, A_GROUP_DIM, axis=1)
    w_up3, w_dn3 = [None, None], [None, None]

    def mlp_forward(i, h, up, dn, x_in=None, t_next=None):
        w_up3[i], w_dn3[i] = up, dn.reshape(1, D_FF, D_MODEL)
        (a,) = mm_nn(h, w_up3[i], name=f"mlp{i}_up", tm=2048, ps=2, tn=512, tk=D_MODEL, epilogue=ident, outs=(BF16,))
        if t_next is None:
            (y,) = mm_nn(
                a, w_dn3[i], name=f"mlp{i}_down", tm=1024, ps=1, tn=512, tk=D_FF, prologue=relu2, epilogue=ident, outs=(BF16,)
            )
            return a, y
        y, xn, hn = mm_nn(
            a, w_dn3[i], name=f"mlp{i}_down", tm=512, ps=1, tn=D_MODEL, tk=D_FF, prologue=relu2,
            epilogue=residual_ln_epilogue, extras=residual_extras(x_in, t_next - 1), outs=(BF16, F32, BF16),
        )
        return a, y, xn, hn

    def residual_extras(x_in, t):
        rows = (gate1[t], lng_full[t], lnb_full[t], scale[t + 1], shift[t + 1])
        return [(x_in, "full")] + [(r, "row") for r in rows]

    h0 = modulate(x0, scale[0], shift[0], name="modulate0")
    w_in3, w_aout3 = gather.wait([W_IN, W_AOUT], h0, name="gather_wait_a")
    w_aout3 = w_aout3.reshape(1, D_MODEL, D_MODEL)
    (a_pre,) = mm_nn(
        h0, w_in3, name="a_in", tm=2048, ps=4, tn=256, tk=D_MODEL, epilogue=lambda acc, b: (acc + b,),
        extras=[(a_b_in, "row")], outs=(BF16,),
    )
    p_gate = gate_fwd(a_pre, vn_g, vn_b, w_s, bias_full, name="gate_fwd")
    y0, x1, h1 = mm_nn(
        p_gate, w_aout3, name="a_out", tm=1024, ps=1, tn=D_MODEL, tk=D_MODEL, epilogue=residual_ln_epilogue,
        extras=residual_extras(x0, 0), outs=(BF16, F32, BF16),
    )
    w_mlp0 = gather.wait([W_UP0, W_DN0], y0, name="gather_wait_mlp0")
    a1, y1, x2, h2 = mlp_forward(0, h1, *w_mlp0, x_in=x1, t_next=2)
    (w_qkv_shards,) = gather.wait([W_QKV], y1, name="gather_wait_qkv")
    w_qkv3 = shards_to_columns(w_qkv_shards, name="w_qkv_columns")[None]
    pat_tiles = 3
    dil = [d for _, d in B_PATTERNS]
    qkv_p, pat_o, pat_lse = [], [], []
    for g in range(N_PAT):
        (qkv_g,) = mm_nn(
            h2, w_qkv3, name=f"b_qkv{g}", tm=2048, ps=1, tn=D_MODEL, tk=D_MODEL, epilogue=ident, outs=(BF16,),
            b_tile0=pat_tiles * g, b_tiles=pat_tiles, out_streams=dil[g],
        )
        o_g, lse_g = attn_fwd(qkv_g, g, name=f"attn_fwd{g}", after=rest_token if g == 1 else None)
        if g == 0:
            rest_passed, rest_token = gather.pass_on([W_BOUT, W_UP1, W_DN1], lse_g, name="gather_pass_rest")
        qkv_p.append(qkv_g)
        pat_o.append(o_g)
        pat_lse.append(lse_g)
    o_b, o_f, *lse_pat = attn_combine(pat_o, pat_lse, name="attn_combine")
    lse_pat = [x.reshape(SEQ, LANES) for x in lse_pat]
    w_bout3, *w_mlp1 = gather.finish(rest_passed, lse_pat[0], name="gather_wait_rest")
    w_bout3 = w_bout3.reshape(1, D_MODEL, D_MODEL)
    y2, x3, h3 = mm_nn(
        o_b, w_bout3, name="b_out", tm=1024, ps=1, tn=D_MODEL, tk=D_MODEL, epilogue=residual_ln_epilogue,
        extras=residual_extras(x2, 2), outs=(BF16, F32, BF16),
    )
    a3, y3 = mlp_forward(1, h3, *w_mlp1)
    dz3, dy3, st3, loss_local = residual_ln_loss_bwd(
        x3, y3, gate1[3], lng_full[3], lnb_full[3], target, name="res_ln3_loss_bwd"
    )

    def scatter(parts, after, name):
        return Scatter([p.reshape(N_DEV, -1, p.shape[-1]) for p in parts], after, name=name)

    xs_in, ys = [x0, x1, x2, x3], [y0, y1, y2, y3]

    def residual_bwd_extras(dz_later, t):
        return [
            (dz_later, "full"), (scale[t + 1], "row"), (xs_in[t], "full"), (ys[t], "full"),
            (gate1[t], "row"), (lng_full[t], "row"), (lnb_full[t], "row"),
        ]

    def mlp_backward(i, h, a, dy, dz_later, t):
        (da,) = mm_nt(
            dy,
            w_dn3[i],
            name=f"mlp{i}_da",
            tm=2048,
            tko=1024,
            ps=1,
            tc=D_MODEL,
            epilogue=lambda acc, act: (acc * (2.0 * jnp.maximum(act.astype(F32), 0.0)),),
            extras=[(a, "full")],
            outs=(BF16,),
        )
        dw_dn = mm_tn(
            a, dy, name=f"mlp{i}_dw_down", p=1, tk=1024, ps=1, tn=D_MODEL, tmc=2048, prologue=relu2, out_dtype=BF16
        )
        dw_up = mm_tn(h, da, name=f"mlp{i}_dw_up", p=N_DEV, tk=1024, ps=2, tn=512, tmc=2048, out_dtype=BF16)
        rs = scatter([dw_up, dw_dn], da, f"scatter_mlp{i}")
        dz, dy_before, st = mm_nt(
            da, w_up3[i], name=f"mlp{i}_dh", tm=512, tko=D_MODEL, ps=N_DEV, tc=512, epilogue=residual_bwd_epilogue,
            extras=residual_bwd_extras(dz_later, t), outs=(F32, BF16), stats=True, after=rs.token,
        )
        return rs, dz, dy_before, st

    rs_mlp1, dz2, dy2, st2 = mlp_backward(1, h3, a3, dy3, dz3, 2)
    do_delta = mm_nt(
        dy2, w_bout3, name="b_do", tm=1024, tko=D_MODEL, ps=1, tc=D_MODEL, extras=[(o_f, "full")],
        epilogue=lambda acc, o: (acc,) * N_PAT + (_reduce_groups(acc * o),) * N_PAT,
        outs=(BF16,) * N_PAT + (F32,) * N_PAT, out_widths=[D_MODEL] * N_PAT + [LANES] * N_PAT, out_streams=dil + dil,
    )
    do_pat = [x.reshape(SEQ, D_MODEL) for x in do_delta[:N_PAT]]
    delta_pat = [x.reshape(SEQ, LANES) for x in do_delta[N_PAT:]]
    dh2, dw_pat = None, []
    for g in range(N_PAT):
        dqkv_g = attn_bwd(qkv_p[g], do_pat[g], lse_pat[g], delta_pat[g], g, name=f"attn_bwd{g}")
        dw_pat.append(
            mm_tn(
                h2, dqkv_g, name=f"b_dw_qkv{g}", p=1, tk=1024, ps=1, tn=D_MODEL, tmc=2048, out_dtype=BF16,
                g_streams=dil[g],
            )[0]
        )
        if g == N_PAT - 1:
            break
        (dh2,) = mm_nt(
            dqkv_g, w_qkv3, name=f"b_dh{g}", tm=2048, tko=512, ps=1, tc=pat_tiles * D_MODEL, outs=(F32,),
            b_tile0=g, g_streams=dil[g],
            epilogue=ident if g == 0 else (lambda acc, prev, d=dil[g]: (_from_streams(acc, d) + prev,)),
            extras=[] if g == 0 else [(dh2, "full")],
        )
    dw_bout = mm_tn(o_b, dy2, name="b_dw_out", p=1, tk=1024, ps=1, tn=D_MODEL, tmc=2048, out_dtype=BF16)
    dw_qkv = columns_to_shards(dw_pat, name="dw_qkv_shards")
    rs_b = scatter([dw_qkv, dw_bout], dqkv_g, "scatter_b")
    dz1, dy1, st1 = mm_nt(
        dqkv_g, w_qkv3, name=f"b_dh{N_PAT - 1}", tm=512, tko=D_MODEL, ps=1, tc=pat_tiles * D_MODEL, outs=(F32, BF16),
        b_tile0=N_PAT - 1, g_streams=dil[-1], stats=True,
        epilogue=lambda acc, prev, *rest: residual_bwd_epilogue(_from_streams(acc, dil[-1]) + prev, *rest),
        extras=[(dh2, "full")] + residual_bwd_extras(dz2, 1), after=rs_b.token,
    )
    rs_mlp0, dz0, dy0, st0 = mlp_backward(0, h1, a1, dy1, dz1, 0)
    (dp_gate,) = mm_nt(dy0, w_aout3, name="a_dp", tm=2048, tko=1024, ps=1, tc=D_MODEL, epilogue=ident, outs=(F32,))
    dw_aout = mm_tn(p_gate, dy0, name="a_dw_out", p=1, tk=1024, ps=1, tn=D_MODEL, tmc=2048, out_dtype=BF16)
    rs_aout = scatter([dw_aout], dp_gate, "scatter_a_out")
    da0, d_ws, d_bs, gate_rows = gate_bwd(a_pre, dp_gate, vn_g + rs_aout.zero(), vn_b, w_s, bias_full, name="gate_bwd")
    rs_ws = scatter([d_ws], gate_rows, "scatter_a_w_s")
    dw_in = mm_tn(h0, da0, name="a_dw_in", p=N_DEV, tk=1024, ps=4, tn=256, tmc=2048, out_dtype=BF16)
    rs_in = scatter([dw_in], rs_ws.token, "scatter_a_in")
    grad_x, stf = mm_nt(
        da0, w_in3, name="a_dh", tm=1024, tko=D_MODEL, ps=N_DEV, tc=256, epilogue=input_bwd_epilogue,
        extras=[(dz0, "full"), (scale[0], "row"), (x0, "full")], outs=(F32,), stats=True, after=rs_in.token,
    )

    results = {}

    def update(wname, gparts, w, m, v):
        shape = w.shape
        layers = len(gparts) if isinstance(gparts, list) else 1
        w3 = w.reshape(layers, -1, shape[-1])
        parts = [g.reshape(g.shape[0], *w3.shape[1:]) for g in (gparts if layers > 1 else [gparts])]
        outs = adamw(parts, w3, m.reshape(w3.shape), v.reshape(w3.shape), name=f"adamw_{wname}")
        results[wname] = [o.reshape(shape) for o in outs]

    g_up1, g_dn1 = rs_mlp1.wait([0, 1], grad_x, name="scatter_wait_mlp1")
    g_qkv, g_bout = rs_b.wait([0, 1], grad_x, name="scatter_wait_b")
    update("b_w_qkv", g_qkv, b_w_qkv, m_b_w_qkv, v_b_w_qkv)
    update("b_w_out", g_bout, b_w_out, m_b_w_out, v_b_w_out)
    g_up0, g_dn0 = rs_mlp0.wait([0, 1], grad_x, name="scatter_wait_mlp0")
    update("mlp_w_up", [g_up0, g_up1], mlp_w_up, m_mlp_w_up, v_mlp_w_up)
    update("mlp_w_down", [g_dn0, g_dn1], mlp_w_down, m_mlp_w_down, v_mlp_w_down)

    stats_after = [stf, st0, st1, st2]
    stats_own = [st0, st1, st2, st3]
    dm = jnp.stack(
        [
            jnp.concatenate(
                [stats_after[t][ST_DSHIFT], stats_after[t][ST_DSCALE], stats_own[t][ST_DGATE]], axis=0
            )
            for t in range(N_SUB)
        ]
    )
    ncol = 3 * D_MODEL // N_DEV
    dmx = jnp.pad(
        dm.reshape(N_SUB, N_DEV, ncol).transpose(1, 0, 2)[:, :, None, :], ((0, 0), (0, 0), (0, SUBLANES - 1), (0, 0))
    )
    small = [
        gate_rows[0],
        gate_rows[1],
        d_bs[:, :A_GROUPS].T.reshape(-1),
        *[stats_own[t][ST_DG] for t in range(N_SUB)],
        *[stats_own[t][ST_DB] for t in range(N_SUB)],
        jnp.pad(loss_local.reshape(1), (0, LANES - 1)),
    ]
    n_small = sum(s.size for s in small)
    part_rows = -(-n_small // (N_DEV * LANES * SUBLANES)) * SUBLANES
    flat = jnp.concatenate(small + [jnp.zeros((N_DEV * part_rows * LANES - n_small,), F32)])
    (ws_parts,) = rs_ws.wait([0], results["mlp_w_down"][3], name="scatter_wait_a_w_s")
    early = [results[n][3] for n in ("b_w_qkv", "b_w_out", "mlp_w_up", "mlp_w_down")]
    dmrecv, reduced = small_exchange(dmx, flat.reshape(N_DEV, part_rows, LANES), ws_parts, early)
    g_ws = reduced[:, part_rows:, :].reshape(-1)
    reduced = reduced[:, :part_rows, :].reshape(-1)
    sizes = [2 * D_MODEL, D_MODEL, D_MODEL, A_GROUPS * CHUNK, N_SUB * D_MODEL, N_SUB * D_MODEL]
    offs = [sum(sizes[:i]) for i in range(len(sizes) + 1)]
    g_b_in, g_vn_g, g_vn_b, g_bs, g_lng, g_lnb = [reduced[offs[i] : offs[i + 1]] for i in range(len(sizes))]
    loss = reduced[offs[-1]]

    ada_outs = ada_grad_adamw(cg, dmrecv, ada_w4, m_ada_w.reshape(ada_w4.shape), v_ada_w.reshape(ada_w4.shape), name="ada_grad_adamw")
    results["ada_w"] = [o.reshape(ada_w.shape) for o in ada_outs[:4]]
    ln_cols = D_MODEL // N_DEV
    my_ln = lambda gfull: lax.dynamic_slice_in_dim(gfull.reshape(N_SUB, N_DEV, ln_cols), me, 1, axis=1)
    small_params = [
        ("ada_b", ada_outs[4][:, 0, :], ada_b, m_ada_b, v_ada_b),
        ("ln_g", my_ln(g_lng), ln_g, m_ln_g, v_ln_g),
        ("ln_b", my_ln(g_lnb), ln_b, m_ln_b, v_ln_b),
        ("a_b_in", g_b_in, a_b_in, m_a_b_in, v_a_b_in),
        ("a_vn_g", g_vn_g, a_vn_g, m_a_vn_g, v_a_vn_g),
        ("a_vn_b", g_vn_b, a_vn_b, m_a_vn_b, v_a_vn_b),
        ("a_b_s", g_bs, a_b_s, m_a_b_s, v_a_b_s),
    ]
    small_outs = adamw_small(
        [[a.reshape(-1, w.shape[-1]) for a in (g, w, m, v)] for _, g, w, m, v in small_params], name="adamw_small"
    )
    for (wname, _, w, _, _), outs in zip(small_params, small_outs, strict=True):
        results[wname] = [o.reshape(w.shape) for o in outs]
    update("a_w_s", g_ws[None], a_w_s, m_a_w_s, v_a_w_s)
    (g_aout,) = rs_aout.wait([0], grad_x, name="scatter_wait_a_out")
    (g_in,) = rs_in.wait([0], grad_x, name="scatter_wait_a_in")
    update("a_w_in", g_in, a_w_in, m_a_w_in, v_a_w_in)
    update("a_w_out", g_aout, a_w_out, m_a_w_out, v_a_w_out)

    order = ["ada_w", "ada_b", "ln_g", "ln_b", "a_w_in", "a_b_in", "a_vn_g", "a_vn_b", "a_w_s", "a_b_s", "a_w_out", "b_w_qkv", "b_w_out", "mlp_w_up", "mlp_w_down"]
    return (loss, grad_x[None], *[results[n][0] for n in order], *[results[n][1] for n in order],
            *[results[n][2] for n in order], *[results[n][3] for n in order])
```

```python
import math

import jax
import jax.numpy as jnp
import numpy as np
from jax import lax
from jax.experimental import pallas as pl
from jax.experimental.pallas import tpu as pltpu

F32 = jnp.float32
BF16 = jnp.bfloat16
MESH = pl.DeviceIdType.MESH
ANY = pl.BlockSpec(memory_space=pl.ANY)
VMEM = pl.BlockSpec(memory_space=pltpu.VMEM)

N_DEV = 8
D_MODEL = 1024
SEQ = 4096
DEPTH = 2
CHUNK = 128
A_GROUPS = 16
A_GROUP_DIM = D_MODEL // A_GROUPS
B_HEADS = 16
B_HEAD_DIM = 64
B_PATTERNS = ((128, 1), (512, 4), (2048, 16))
N_PAT = len(B_PATTERNS)
SPAN = 128
D_FF = 4 * D_MODEL
D_QKV = N_PAT * 3 * D_MODEL
ALPHA = (2 * DEPTH) ** 0.25
LN_EPS = 1e-5
NEG = -1e30
ADAM_LR = 0.001
ADAM_B1 = 0.9
ADAM_B2 = 0.999
ADAM_EPS = 1e-08
ADAM_WD = 0.01
ADAM_STEP = 10
GELU_C = math.sqrt(2.0 / math.pi)
GELU_A = 0.044715

VMEM_LIMIT_BYTES = 56 * 1024 * 1024
LANES = 128
SUBLANES = 8
ROW_TILE = 512
N_SUB = 2 * DEPTH


def _params(sem):
    return pltpu.CompilerParams(dimension_semantics=sem, vmem_limit_bytes=VMEM_LIMIT_BYTES)


def _lane(shape):
    return lax.broadcasted_iota(jnp.int32, shape, len(shape) - 1)


def _split_bf16(x):
    hi = x.astype(BF16)
    lo = (x - hi.astype(F32)).astype(BF16)
    return hi, lo


def _group_expand_matrix(groups_padded, width):
    per = width // A_GROUPS
    r = lax.broadcasted_iota(jnp.int32, (groups_padded, width), 0)
    c = lax.broadcasted_iota(jnp.int32, (groups_padded, width), 1)
    return (c // per == r).astype(BF16)


def _group_reduce_matrix(width, groups_padded):
    per = width // A_GROUPS
    r = lax.broadcasted_iota(jnp.int32, (width, groups_padded), 0)
    c = lax.broadcasted_iota(jnp.int32, (width, groups_padded), 1)
    return (r // per == c).astype(BF16)


def _expand_groups(w):
    e = _group_expand_matrix(LANES, D_MODEL)
    hi, lo = _split_bf16(w)
    return jnp.dot(hi, e, preferred_element_type=F32) + jnp.dot(lo, e, preferred_element_type=F32)


def _reduce_groups(x):
    e = _group_reduce_matrix(D_MODEL, LANES)
    hi, lo = _split_bf16(x)
    return jnp.dot(hi, e, preferred_element_type=F32) + jnp.dot(lo, e, preferred_element_type=F32)


def _to_streams(x, d):
    rows, w = x.shape
    return jnp.swapaxes(x.reshape(rows // d, d, w), 0, 1).reshape(rows, w)


def _from_streams(x, d):
    rows, w = x.shape
    return jnp.swapaxes(x.reshape(d, rows // d, w), 0, 1).reshape(rows, w)


def _column_tiles(p, n, ps, tn):
    assert (ps == 1 or tn == n) and p % ps == 0 and n % tn == 0
    q = n // tn
    return (p // ps) * q, q


def _extra_specs(extras, tm, width):
    specs = []
    for _, kind in extras:
        if kind == "row":
            specs.append(pl.BlockSpec((1, width), lambda i, j, c: (0, j)))
        else:
            specs.append(pl.BlockSpec((tm, width), lambda i, j, c: (i, j)))
    return specs


def mm_nn(a, b3, *, name, tm, ps, tn, tk, epilogue, extras=(), outs, prologue=None, b_tile0=0, b_tiles=None, out_streams=1):
    m, k = a.shape
    p, _, n = b3.shape
    nj, q = _column_tiles(p, n, ps, tn)
    nj = nj if b_tiles is None else b_tiles
    nk = k // tk
    width = ps * tn
    d = out_streams
    assert d == 1 or not extras

    def body(a_ref, b_ref, *rest):
        ex = rest[: len(extras)]
        out_refs = rest[len(extras) : len(extras) + len(outs)]
        kk = pl.program_id(2)
        av = a_ref[...] if prologue is None else prologue(a_ref[...])
        if d > 1:
            av = _to_streams(av, d)

        def finish(cs, acc):
            res = epilogue(acc, *[e[:, cs] for e in ex])
            for o_ref, r in zip(out_refs, res, strict=True):
                if d > 1:
                    o_ref[:, :, cs] = r.astype(o_ref.dtype).reshape(d, tm // d, tn)
                else:
                    o_ref[:, cs] = r.astype(o_ref.dtype)

        for s in range(ps):
            cs = slice(s * tn, (s + 1) * tn)
            part = jnp.dot(av, b_ref[s], preferred_element_type=F32)
            if nk == 1:
                finish(cs, part)
                continue
            acc_ref = rest[-1]

            @pl.when(kk == 0)
            def _(part=part, cs=cs):
                acc_ref[:, cs] = part

            @pl.when(kk > 0)
            def _(part=part, cs=cs):
                acc_ref[:, cs] += part

        if nk > 1:

            @pl.when(kk == nk - 1)
            def _():
                for s in range(ps):
                    cs = slice(s * tn, (s + 1) * tn)
                    finish(cs, rest[-1][:, cs])

    if d > 1:
        out_spec = pl.BlockSpec((d, tm // d, width), lambda i, j, kk: (0, i, j))
        out_shape = (d, m // d, nj * width)
    else:
        out_spec = pl.BlockSpec((tm, width), lambda i, j, kk: (i, j))
        out_shape = (m, nj * width)
    res = pl.pallas_call(
        body,
        name=name,
        grid=(m // tm, nj, nk),
        in_specs=[
            pl.BlockSpec((tm, tk), lambda i, j, kk: (i, kk)),
            pl.BlockSpec((ps, tk, tn), lambda i, j, kk: ((j + b_tile0) // q, kk, (j + b_tile0) % q)),
            *_extra_specs(extras, tm, width),
        ],
        out_specs=[out_spec for _ in outs],
        out_shape=[jax.ShapeDtypeStruct(out_shape, dt) for dt in outs],
        scratch_shapes=[pltpu.VMEM((tm, width), F32)] if nk > 1 else [],
        compiler_params=_params(("parallel", "parallel", "arbitrary")),
    )(a, b3, *[arr for arr, _ in extras])
    return [r.reshape(m, nj * width) for r in res]


def mm_nt(
    g, b3, *, name, tm, tko, ps, tc, epilogue, extras=(), outs, out_widths=None, out_streams=None, b_tile0=0,
    g_streams=1, stats=False, after=None,
):
    m, width = g.shape
    p, k, n = b3.shape
    _, q = _column_tiles(p, n, ps, tc)
    nc = width // (ps * tc)
    ds = g_streams
    assert not stats or tko == k
    widths = [tko] * len(outs) if out_widths is None else out_widths
    streams_out = [1] * len(outs) if out_streams is None else out_streams

    def out_spec(w, d):
        if d == 1:
            return pl.BlockSpec((tm, w), lambda i, j, c: (i, j))
        return pl.BlockSpec((d, tm // d, w), lambda i, j, c: (0, i, j))

    def out_struct(dt, w, d):
        cols = w * (k // tko)
        return jax.ShapeDtypeStruct((m, cols) if d == 1 else (d, m // d, cols), dt)

    order = [] if after is None else [after]
    n_in = len(extras) + len(order)

    def body(g_ref, b_ref, *rest):
        ex = rest[: len(extras)]
        out_refs = rest[n_in : n_in + len(outs)]
        c = pl.program_id(2)
        gv = g_ref[...].reshape(tm, ps * tc) if ds > 1 else g_ref[...]
        part = None
        for s in range(ps):
            d = lax.dot_general(gv[:, s * tc : (s + 1) * tc], b_ref[s], _NT, preferred_element_type=F32)
            part = d if part is None else part + d

        def finish(acc):
            res = epilogue(acc, *[e[...] for e in ex])
            if stats:
                *res, sums = res
                st_ref = rest[n_in + len(outs)]

                @pl.when(pl.program_id(0) == 0)
                def _():
                    st_ref[...] = jnp.zeros_like(st_ref)

                for row, val in enumerate(sums):
                    st_ref[row : row + 1, :] += val
            for o_ref, r, d_out in zip(out_refs, res, streams_out, strict=True):
                r = r.astype(o_ref.dtype)
                o_ref[...] = r if d_out == 1 else _to_streams(r, d_out).reshape(o_ref.shape)

        if nc == 1:
            finish(part)
            return
        acc_ref = rest[-1]

        @pl.when(c == 0)
        def _():
            acc_ref[...] = part

        @pl.when(c > 0)
        def _():
            acc_ref[...] += part

        @pl.when(c == nc - 1)
        def _():
            finish(acc_ref[...])

    return pl.pallas_call(
        body,
        name=name,
        grid=(m // tm, k // tko, nc),
        in_specs=[
            pl.BlockSpec((ds, tm // ds, ps * tc), lambda i, j, c: (0, i, c))
            if ds > 1
            else pl.BlockSpec((tm, ps * tc), lambda i, j, c: (i, c)),
            pl.BlockSpec((ps, tko, tc), lambda i, j, c: ((c + b_tile0) // q, j, (c + b_tile0) % q)),
            *_extra_specs(extras, tm, tko),
            *[ANY for _ in order],
        ],
        out_specs=[out_spec(w, d) for w, d in zip(widths, streams_out, strict=True)]
        + [pl.BlockSpec((SUBLANES, tko), lambda i, j, c: (0, 0))] * stats,
        out_shape=[out_struct(dt, w, d) for dt, w, d in zip(outs, widths, streams_out, strict=True)]
        + [jax.ShapeDtypeStruct((SUBLANES, k), F32)] * stats,
        scratch_shapes=[pltpu.VMEM((tm, tko), F32)] if nc > 1 else [],
        compiler_params=_params(("arbitrary" if stats else "parallel", "parallel", "arbitrary")),
    )(g.reshape(ds, m // ds, width) if ds > 1 else g, b3, *[arr for arr, _ in extras], *order)


def mm_tn(a, g, *, name, p, tk, ps, tn, tmc, out_dtype, prologue=None, g_streams=1):
    m, k = a.shape
    width = g.shape[1]
    n = width // p
    nj, q = _column_tiles(p, n, ps, tn)
    nc = m // tmc
    ds = g_streams

    def body(a_ref, g_ref, o_ref, acc_ref):
        c = pl.program_id(2)
        av = a_ref[...] if prologue is None else prologue(a_ref[...])
        gv = g_ref[...]
        if ds > 1:
            av, gv = _to_streams(av, ds), gv.reshape(tmc, ps * tn)
        part = lax.dot_general(av, gv, (((0,), (0,)), ((), ())), preferred_element_type=F32)

        @pl.when(c == 0)
        def _():
            acc_ref[...] = part

        @pl.when(c > 0)
        def _():
            acc_ref[...] += part

        @pl.when(c == nc - 1)
        def _():
            for s in range(ps):
                o_ref[s] = acc_ref[:, s * tn : (s + 1) * tn].astype(o_ref.dtype)

    return pl.pallas_call(
        body,
        name=name,
        grid=(k // tk, nj, nc),
        in_specs=[
            pl.BlockSpec((tmc, tk), lambda i, j, c: (c, i)),
            pl.BlockSpec((ds, tmc // ds, ps * tn), lambda i, j, c: (0, c, j))
            if ds > 1
            else pl.BlockSpec((tmc, ps * tn), lambda i, j, c: (c, j)),
        ],
        out_specs=pl.BlockSpec((ps, tk, tn), lambda i, j, c: (j // q, i, j % q)),
        out_shape=jax.ShapeDtypeStruct((p, k, n), out_dtype),
        scratch_shapes=[pltpu.VMEM((tk, ps * tn), F32)],
        compiler_params=_params(("parallel", "parallel", "arbitrary")),
    )(a, g.reshape(ds, m // ds, width) if ds > 1 else g)


def _rows(cols):
    return pl.BlockSpec((ROW_TILE, cols), lambda i: (i, 0))


def _vec(cols, rows=1):
    return pl.BlockSpec((rows, cols), lambda i: (0, 0))


def _layer_norm_hat(z):
    mu = jnp.mean(z, axis=-1, keepdims=True)
    zc = z - mu
    var = jnp.mean(zc * zc, axis=-1, keepdims=True)
    rstd = lax.rsqrt(var + LN_EPS)
    return zc * rstd, rstd


def modulate(x, scale, shift, *, name):
    s, d = x.shape

    def body(x_ref, sc_ref, sh_ref, h_ref):
        h_ref[...] = (x_ref[...] * (1.0 + sc_ref[...]) + sh_ref[...]).astype(BF16)

    return pl.pallas_call(
        body,
        name=name,
        grid=(s // ROW_TILE,),
        in_specs=[_rows(d), _vec(d), _vec(d)],
        out_specs=_rows(d),
        out_shape=jax.ShapeDtypeStruct((s, d), BF16),
        compiler_params=_params(("parallel",)),
    )(x, scale, shift)


def residual_ln_epilogue(acc, x, gate1, g, b, nscale, nshift):
    y = acc.astype(BF16)
    xhat, _ = _layer_norm_hat(ALPHA * x + gate1 * y.astype(F32))
    xn = xhat * g + b
    return y, xn, xn * (1.0 + nscale) + nshift


ST_DSCALE, ST_DSHIFT, ST_DG, ST_DB, ST_DGATE = 0, 1, 2, 3, 4


def _layer_norm_bwd(g_out, xhat, rstd, g):
    dxh = g_out * g
    m1 = jnp.mean(dxh, axis=-1, keepdims=True)
    m2 = jnp.mean(dxh * xhat, axis=-1, keepdims=True)
    return rstd * (dxh - m1 - xhat * m2)


def _column_sums(vals):
    return [jnp.sum(v, axis=0, keepdims=True) for v in vals]


def residual_bwd_epilogue(dh, dzl, scl, x, y, gate1, g, b):
    yf = y.astype(F32)
    xhat, rstd = _layer_norm_hat(ALPHA * x + gate1 * yf)
    g_out = ALPHA * dzl + dh * (1.0 + scl)
    dz = _layer_norm_bwd(g_out, xhat, rstd, g)
    return dz, dz * gate1, _column_sums([dh * (xhat * g + b), dh, g_out * xhat, g_out, dz * yf])


def input_bwd_epilogue(dh, dzl, scl, x):
    return ALPHA * dzl + dh * (1.0 + scl), _column_sums([dh * x, dh])


def residual_ln_loss_bwd(x, y, gate1, g, b, target, *, name):
    s, d = x.shape

    def body(x_ref, y_ref, gt_ref, g_ref, b_ref, t_ref, dz_ref, dy_ref, st_ref, loss_ref):
        y = y_ref[...].astype(F32)
        gate1 = gt_ref[...]
        xhat, rstd = _layer_norm_hat(ALPHA * x_ref[...] + gate1 * y)
        err = xhat * g_ref[...] + b_ref[...] - t_ref[...]
        g_out = err * (1.0 / d)
        dz = _layer_norm_bwd(g_out, xhat, rstd, g_ref[...])
        dz_ref[...] = dz
        dy_ref[...] = (dz * gate1).astype(BF16)
        part = jnp.sum(jnp.sum(err * err, axis=1, keepdims=True), axis=0, keepdims=True) * (0.5 / d)

        @pl.when(pl.program_id(0) == 0)
        def _():
            st_ref[...] = jnp.zeros_like(st_ref)
            loss_ref[...] = jnp.zeros_like(loss_ref)

        loss_ref[...] += part
        for row, val in zip((ST_DG, ST_DB, ST_DGATE), _column_sums([g_out * xhat, g_out, dz * y]), strict=True):
            st_ref[row : row + 1, :] += val

    return pl.pallas_call(
        body,
        name=name,
        grid=(s // ROW_TILE,),
        in_specs=[_rows(d), _rows(d), _vec(d), _vec(d), _vec(d), _rows(d)],
        out_specs=[_rows(d), _rows(d), _vec(d, SUBLANES), pl.BlockSpec((1, 1), lambda i: (0, 0))],
        out_shape=[
            jax.ShapeDtypeStruct((s, d), F32),
            jax.ShapeDtypeStruct((s, d), BF16),
            jax.ShapeDtypeStruct((SUBLANES, d), F32),
            jax.ShapeDtypeStruct((1, 1), F32),
        ],
        compiler_params=_params(("arbitrary",)),
    )(x, y, gate1, g, b, target)


GATE_CHUNKS = 4


def _gelu(x, with_grad=False):
    x2 = x * x
    t = jnp.tanh(GELU_C * (x + GELU_A * x2 * x))
    half = 0.5 * (1.0 + t)
    y = x * half
    if not with_grad:
        return y
    return y, half + 0.5 * x * (1.0 - t * t) * (GELU_C * (1.0 + 3.0 * GELU_A * x2))


def _causal_weights(w_ref, transpose):
    t = lax.broadcasted_iota(jnp.int32, (CHUNK, CHUNK), 0)
    s = lax.broadcasted_iota(jnp.int32, (CHUNK, CHUNK), 1)
    out = []
    for g in range(A_GROUPS):
        w = jnp.where(t >= s, w_ref[g], 0.0)
        out.append((w.T if transpose else w).astype(BF16))
    return out


def _spatial(ws, vn, lo_mask):
    rows = vn.shape[0]
    out_rows = []
    for r in range(rows // CHUNK):
        cols = []
        for j in range(A_GROUPS // 2):
            blk = vn[r * CHUNK : (r + 1) * CHUNK, j * LANES : (j + 1) * LANES]
            za = jnp.dot(ws[2 * j], blk, preferred_element_type=F32)
            zb = jnp.dot(ws[2 * j + 1], blk, preferred_element_type=F32)
            cols.append(jnp.where(lo_mask, za, zb))
        out_rows.append(jnp.concatenate(cols, axis=1))
    return jnp.concatenate(out_rows, axis=0)


def _gate_forward(a, vg, vb, ws, bias, lo_mask, with_grad=False):
    u = _gelu(a[:, :D_MODEL], with_grad)
    v = _gelu(a[:, D_MODEL:], with_grad)
    gu, gv = None, None
    if with_grad:
        (u, gu), (v, gv) = u, v
    vhat, rstd = _layer_norm_hat(v)
    vn = (vhat * vg + vb).astype(BF16)
    z = _spatial(ws, vn, lo_mask) + jnp.concatenate([bias] * (a.shape[0] // CHUNK), axis=0)
    return u, vhat, rstd, vn, z, gu, gv


def gate_fwd(a_pre, vn_g, vn_b, w_s, bias_full, *, name):
    s = a_pre.shape[0]
    tr = GATE_CHUNKS * CHUNK

    def body(a_ref, vg_ref, vb_ref, w_ref, bias_ref, p_ref):
        lo_mask = _lane((CHUNK, LANES)) < A_GROUP_DIM
        ws = _causal_weights(w_ref, transpose=False)
        u, _, _, _, z, _, _ = _gate_forward(a_ref[...].astype(F32), vg_ref[...], vb_ref[...], ws, bias_ref[...], lo_mask)
        p_ref[...] = (u * z).astype(BF16)

    return pl.pallas_call(
        body,
        name=name,
        grid=(s // tr,),
        in_specs=[
            pl.BlockSpec((tr, 2 * D_MODEL), lambda i: (i, 0)),
            _vec(D_MODEL),
            _vec(D_MODEL),
            pl.BlockSpec((A_GROUPS, CHUNK, CHUNK), lambda i: (0, 0, 0)),
            _vec(D_MODEL, CHUNK),
        ],
        out_specs=pl.BlockSpec((tr, D_MODEL), lambda i: (i, 0)),
        out_shape=jax.ShapeDtypeStruct((s, D_MODEL), BF16),
        compiler_params=_params(("parallel",)),
    )(a_pre, vn_g, vn_b, w_s, bias_full)


def gate_bwd(a_pre, dp, vn_g, vn_b, w_s, bias_full, *, name):
    s = a_pre.shape[0]
    tr = GATE_CHUNKS * CHUNK
    nsteps = s // tr

    def body(a_ref, dp_ref, vg_ref, vb_ref, w_ref, bias_ref, da_ref, dw_ref, dbs_ref, rows_ref, dbias_acc):
        step = pl.program_id(0)
        lo_mask = _lane((CHUNK, LANES)) < A_GROUP_DIM

        @pl.when(step == 0)
        def _():
            dw_ref[...] = jnp.zeros_like(dw_ref)
            rows_ref[...] = jnp.zeros_like(rows_ref)
            dbias_acc[...] = jnp.zeros_like(dbias_acc)

        a = a_ref[...].astype(F32)
        vg = vg_ref[...]
        ws = _causal_weights(w_ref, transpose=False)
        wts = _causal_weights(w_ref, transpose=True)
        u, vhat, rstd, vn, z, gelu_du, gelu_dv = _gate_forward(a, vg, vb_ref[...], ws, bias_ref[...], lo_mask, True)
        dp = dp_ref[...]
        du = dp * z
        dzz = dp * u
        dzz_b = dzz.astype(BF16)
        dvn = _spatial(wts, dzz_b, lo_mask)
        dbias = None
        for r in range(GATE_CHUNKS):
            rs = slice(r * CHUNK, (r + 1) * CHUNK)
            dbias = dzz[rs] if dbias is None else dbias + dzz[rs]
            for j in range(A_GROUPS // 2):
                cs = slice(j * LANES, (j + 1) * LANES)
                dblk = dzz[rs, cs]
                vblk = vn[rs, cs]
                for half in range(2):
                    keep = lo_mask if half == 0 else jnp.logical_not(lo_mask)
                    dm = jnp.where(keep, dblk, 0.0).astype(BF16)
                    dw_ref[2 * j + half] += lax.dot_general(
                        dm, vblk, (((1,), (1,)), ((), ())), preferred_element_type=F32
                    )
        dbias_acc[...] += dbias
        rows_ref[1:2, :D_MODEL] += jnp.sum(dvn * vhat, axis=0, keepdims=True)
        rows_ref[1:2, D_MODEL:] += jnp.sum(dvn, axis=0, keepdims=True)
        dvh = dvn * vg
        m1 = jnp.mean(dvh, axis=-1, keepdims=True)
        m2 = jnp.mean(dvh * vhat, axis=-1, keepdims=True)
        dv = rstd * (dvh - m1 - vhat * m2)
        da_u = du * gelu_du
        da_v = dv * gelu_dv
        da_ref[:, :D_MODEL] = da_u.astype(BF16)
        da_ref[:, D_MODEL:] = da_v.astype(BF16)
        rows_ref[0:1, :D_MODEL] += jnp.sum(da_u, axis=0, keepdims=True)
        rows_ref[0:1, D_MODEL:] += jnp.sum(da_v, axis=0, keepdims=True)

        @pl.when(step == nsteps - 1)
        def _():
            t = lax.broadcasted_iota(jnp.int32, (CHUNK, CHUNK), 0)
            sx = lax.broadcasted_iota(jnp.int32, (CHUNK, CHUNK), 1)
            for g in range(A_GROUPS):
                dw_ref[g] = jnp.where(t >= sx, dw_ref[g], 0.0)
            dbs_ref[...] = _reduce_groups(dbias_acc[...])

    return pl.pallas_call(
        body,
        name=name,
        grid=(nsteps,),
        in_specs=[
            pl.BlockSpec((tr, 2 * D_MODEL), lambda i: (i, 0)),
            pl.BlockSpec((tr, D_MODEL), lambda i: (i, 0)),
            _vec(D_MODEL),
            _vec(D_MODEL),
            pl.BlockSpec((A_GROUPS, CHUNK, CHUNK), lambda i: (0, 0, 0)),
            _vec(D_MODEL, CHUNK),
        ],
        out_specs=[
            pl.BlockSpec((tr, 2 * D_MODEL), lambda i: (i, 0)),
            pl.BlockSpec((A_GROUPS, CHUNK, CHUNK), lambda i: (0, 0, 0)),
            _vec(LANES, CHUNK),
            _vec(2 * D_MODEL, SUBLANES),
        ],
        out_shape=[
            jax.ShapeDtypeStruct((s, 2 * D_MODEL), BF16),
            jax.ShapeDtypeStruct((A_GROUPS, CHUNK, CHUNK), F32),
            jax.ShapeDtypeStruct((CHUNK, LANES), F32),
            jax.ShapeDtypeStruct((SUBLANES, 2 * D_MODEL), F32),
        ],
        scratch_shapes=[pltpu.VMEM((CHUNK, D_MODEL), F32)],
        compiler_params=_params(("arbitrary",)),
    )(a_pre, dp, vn_g, vn_b, w_s, bias_full)


def alibi_tables(dilation):
    qi = np.arange(SPAN)[:, None]
    ki = np.arange(2 * SPAN)[None, :]
    diff = SPAN + qi - ki
    valid = (diff >= 0) & (diff <= SPAN)
    heads = np.arange(1, B_HEADS + 1, dtype=np.float32)
    slopes = np.exp2(np.float32(-8.0) * heads / np.float32(B_HEADS)).astype(np.float32)
    bias = -slopes[:, None, None] * (dilation * diff).astype(np.float32)
    bias = np.where(valid[None], bias, np.float32(NEG)).reshape(B_HEADS // 2, 2 * SPAN, 2 * SPAN)
    return jnp.asarray(bias), jnp.asarray(np.ascontiguousarray(bias.transpose(0, 2, 1)))


def _pair_rows(x, halves):
    return jnp.concatenate([x * halves[0], x * halves[1]], axis=0)


def _pair_column(v, lane, j):
    pick = lambda h: jnp.sum(jnp.where(lane == h, v, 0.0), axis=1, keepdims=True)
    return jnp.concatenate([pick(2 * j), pick(2 * j + 1)], axis=0)


_NT = (((1,), (1,)), ((), ()))


def _qkv_specs(block_of):
    def spec(which, prev):
        def index(*grid):
            blk = block_of(*grid)
            return (jnp.maximum(blk - 1, 0) if prev else blk, which)

        return pl.BlockSpec((SPAN, D_MODEL), index)

    return [spec(0, False), spec(1, True), spec(1, False), spec(2, True), spec(2, False)]


def attn_fwd(qkv_p, pat, *, name, after=None):
    _, dilation = B_PATTERNS[pat]
    nb = SEQ // dilation // SPAN
    bias, _ = alibi_tables(dilation)
    order = [] if after is None else [after]

    def body(q_ref, kp_ref, kc_ref, vp_ref, vc_ref, bias_ref, *rest):
        o_ref, lse_ref = rest[len(order) :]
        n = pl.program_id(1)
        first_prev = jnp.logical_and(n == 0, _lane((2 * SPAN, 2 * SPAN)) < SPAN)
        lane = _lane((SPAN, LANES))
        lo_mask = lane < B_HEAD_DIM
        q = q_ref[...] * jnp.asarray(B_HEAD_DIM**-0.5, BF16)
        kk = jnp.concatenate([kp_ref[...], kc_ref[...]], axis=0)
        vv = jnp.concatenate([vp_ref[...], vc_ref[...]], axis=0)
        halves = (lo_mask.astype(BF16), jnp.logical_not(lo_mask).astype(BF16))
        stats = jnp.zeros((SPAN, LANES), F32)
        for j in range(B_HEADS // 2):
            cs = slice(j * LANES, (j + 1) * LANES)
            sc = lax.dot_general(_pair_rows(q[:, cs], halves), kk[:, cs], _NT, preferred_element_type=F32)
            sc = jnp.where(first_prev, NEG, sc + bias_ref[j])
            m = jnp.max(sc, axis=1, keepdims=True)
            p = jnp.exp(sc - m)
            l = jnp.sum(p, axis=1, keepdims=True)
            acc = jnp.dot(p.astype(BF16), vv[:, cs], preferred_element_type=F32) * (1.0 / l)
            lse_pair = m + jnp.log(l)
            o_ref[:, cs] = jnp.where(lo_mask, acc[:SPAN], acc[SPAN:]).astype(BF16)
            stats = jnp.where(lane == 2 * j, lse_pair[:SPAN], stats)
            stats = jnp.where(lane == 2 * j + 1, lse_pair[SPAN:], stats)
        lse_ref[...] = stats

    return pl.pallas_call(
        body,
        name=name,
        grid=(dilation, nb),
        in_specs=[
            *_qkv_specs(lambda r, n: r * nb + n),
            pl.BlockSpec((B_HEADS // 2, 2 * SPAN, 2 * SPAN), lambda r, n: (0, 0, 0)),
            *[ANY for _ in order],
        ],
        out_specs=[
            pl.BlockSpec((SPAN, D_MODEL), lambda r, n: (r * nb + n, 0)),
            pl.BlockSpec((SPAN, LANES), lambda r, n: (r * nb + n, 0)),
        ],
        out_shape=[jax.ShapeDtypeStruct((SEQ, D_MODEL), BF16), jax.ShapeDtypeStruct((SEQ, LANES), F32)],
        compiler_params=_params(("parallel", "arbitrary")),
    )(qkv_p, qkv_p, qkv_p, qkv_p, qkv_p, bias, *order)


def attn_combine(outs, lses, *, name):
    dils = [d for _, d in B_PATTERNS]

    def positions(ref, d):
        x = ref[...].astype(F32)
        return x if d == 1 else jnp.swapaxes(x, 0, 1).reshape(ROW_TILE, x.shape[-1])

    def body(o0, o1, o2, l0, l1, l2, ob_ref, of_ref, *lse_refs):
        ls = [positions(l, d) for l, d in zip((l0, l1, l2), dils, strict=True)]
        m = jnp.maximum(jnp.maximum(ls[0], ls[1]), ls[2])
        tot = jnp.log(jnp.exp(ls[0] - m) + jnp.exp(ls[1] - m) + jnp.exp(ls[2] - m)) + m
        o = None
        for o_ref, l, d in zip((o0, o1, o2), ls, dils, strict=True):
            term = _expand_groups(jnp.exp(l - tot)) * positions(o_ref, d)
            o = term if o is None else o + term
        ob_ref[...] = o.astype(BF16)
        of_ref[...] = o
        for lse_ref, d in zip(lse_refs, dils, strict=True):
            lse_ref[...] = tot if d == 1 else _to_streams(tot, d).reshape(lse_ref.shape)

    def stream_rows(cols, d):
        return _rows(cols) if d == 1 else pl.BlockSpec((d, ROW_TILE // d, cols), lambda i: (0, i, 0))

    def streams(x, d):
        return x if d == 1 else x.reshape(d, SEQ // d, x.shape[-1])

    outs = [streams(x, d) for x, d in zip(outs, dils, strict=True)]
    lses = [streams(x, d) for x, d in zip(lses, dils, strict=True)]
    return pl.pallas_call(
        body,
        name=name,
        grid=(SEQ // ROW_TILE,),
        in_specs=[stream_rows(D_MODEL, d) for d in dils] + [stream_rows(LANES, d) for d in dils],
        out_specs=[_rows(D_MODEL), _rows(D_MODEL)] + [stream_rows(LANES, d) for d in dils],
        out_shape=[jax.ShapeDtypeStruct((SEQ, D_MODEL), BF16), jax.ShapeDtypeStruct((SEQ, D_MODEL), F32)]
        + [jax.ShapeDtypeStruct(lse.shape, F32) for lse in lses],
        compiler_params=_params(("parallel",)),
    )(*outs, *lses)


def attn_bwd(qkv_p, do_p, lse_p, delta_p, pat, *, name):
    _, dilation = B_PATTERNS[pat]
    nb = SEQ // dilation // SPAN
    n_blocks = SEQ // SPAN
    bias, bias_t = alibi_tables(dilation)
    last = n_blocks - 1
    q_cols, k_cols, v_cols = (slice(i * D_MODEL, (i + 1) * D_MODEL) for i in range(3))

    def body(q_ref, kp_ref, kc_ref, vp_ref, vc_ref, do_ref, lse_ref, dl_ref, bias_ref, biast_ref, out_ref, cq_ref, ck_ref, cv_ref):
        g = pl.program_id(0)

        @pl.when(g == n_blocks)
        def _():
            out_ref[:, q_cols] = cq_ref[...].astype(BF16)
            out_ref[:, k_cols] = ck_ref[...].astype(BF16)
            out_ref[:, v_cols] = cv_ref[...].astype(BF16)

        @pl.when(g == 0)
        def _():
            cq_ref[...] = jnp.zeros_like(cq_ref)
            ck_ref[...] = jnp.zeros_like(ck_ref)
            cv_ref[...] = jnp.zeros_like(cv_ref)

        @pl.when(g < n_blocks)
        def _():
            lane = _lane((SPAN, LANES))
            lo_mask = lane < B_HEAD_DIM
            pair = (2 * SPAN, 2 * SPAN)
            first = lax.rem(g, nb) == 0
            prev_key_cols = jnp.logical_and(first, _lane(pair) < SPAN)
            prev_key_rows = jnp.logical_and(first, lax.broadcasted_iota(jnp.int32, pair, 0) < SPAN)
            q = q_ref[...] * jnp.asarray(B_HEAD_DIM**-0.5, BF16)
            kk = jnp.concatenate([kp_ref[...], kc_ref[...]], axis=0)
            vv = jnp.concatenate([vp_ref[...], vc_ref[...]], axis=0)
            do_v = do_ref[...]
            lse_v = lse_ref[...]
            dl_v = dl_ref[...]
            lse_t = lse_v.T
            dl_t = dl_v.T
            halves = (lo_mask.astype(BF16), jnp.logical_not(lo_mask).astype(BF16))
            for j in range(B_HEADS // 2):
                cs = slice(j * LANES, (j + 1) * LANES)
                kp, vp = kk[:, cs], vv[:, cs]
                q2 = _pair_rows(q[:, cs], halves)
                do2 = _pair_rows(do_v[:, cs], halves)
                lse_c, dl_c = _pair_column(lse_v, lane, j), _pair_column(dl_v, lane, j)
                lse_r = jnp.concatenate([lse_t[2 * j : 2 * j + 1], lse_t[2 * j + 1 : 2 * j + 2]], axis=1)
                dl_r = jnp.concatenate([dl_t[2 * j : 2 * j + 1], dl_t[2 * j + 1 : 2 * j + 2]], axis=1)
                sc = lax.dot_general(q2, kp, _NT, preferred_element_type=F32)
                p = jnp.exp(jnp.where(prev_key_cols, NEG, sc + bias_ref[j]) - lse_c)
                dp = lax.dot_general(do2, vp, _NT, preferred_element_type=F32)
                ds = (p * (dp - dl_c)).astype(BF16)
                dq2 = jnp.dot(ds, kp, preferred_element_type=F32)
                sc_t = lax.dot_general(kp, q2, _NT, preferred_element_type=F32)
                p_t = jnp.exp(jnp.where(prev_key_rows, NEG, sc_t + biast_ref[j]) - lse_r)
                dp_t = lax.dot_general(vp, do2, _NT, preferred_element_type=F32)
                ds_t = (p_t * (dp_t - dl_r)).astype(BF16)
                dk_pair = jnp.dot(ds_t, q2, preferred_element_type=F32)
                dv_pair = jnp.dot(p_t.astype(BF16), do2, preferred_element_type=F32)
                oq = slice(j * LANES, (j + 1) * LANES)
                ok = slice(D_MODEL + j * LANES, D_MODEL + (j + 1) * LANES)
                ov = slice(2 * D_MODEL + j * LANES, 2 * D_MODEL + (j + 1) * LANES)
                out_ref[:, oq] = cq_ref[:, cs].astype(BF16)
                out_ref[:, ok] = (ck_ref[:, cs] + dk_pair[:SPAN]).astype(BF16)
                out_ref[:, ov] = (cv_ref[:, cs] + dv_pair[:SPAN]).astype(BF16)
                cq_ref[:, cs] = jnp.where(lo_mask, dq2[:SPAN], dq2[SPAN:]) * (B_HEAD_DIM**-0.5)
                ck_ref[:, cs] = dk_pair[SPAN:]
                cv_ref[:, cs] = dv_pair[SPAN:]

    def block_of(g):
        return jnp.minimum(g, last)

    def row_spec(width):
        return pl.BlockSpec((SPAN, width), lambda g: (block_of(g), 0))

    return pl.pallas_call(
        body,
        name=name,
        grid=(n_blocks + 1,),
        in_specs=[
            *_qkv_specs(block_of),
            row_spec(D_MODEL),
            row_spec(LANES),
            row_spec(LANES),
            pl.BlockSpec((B_HEADS // 2, 2 * SPAN, 2 * SPAN), lambda g: (0, 0, 0)),
            pl.BlockSpec((B_HEADS // 2, 2 * SPAN, 2 * SPAN), lambda g: (0, 0, 0)),
        ],
        out_specs=pl.BlockSpec((SPAN, 3 * D_MODEL), lambda g: (jnp.maximum(g - 1, 0), 0)),
        out_shape=jax.ShapeDtypeStruct((SEQ, 3 * D_MODEL), BF16),
        scratch_shapes=[pltpu.VMEM((SPAN, D_MODEL), F32)] * 3,
        compiler_params=_params(("arbitrary",)),
    )(qkv_p, qkv_p, qkv_p, qkv_p, qkv_p, do_p, lse_p, delta_p, bias, bias_t)


def _position():
    x, y, c = lax.axis_index("x"), lax.axis_index("y"), lax.axis_index("c")
    return x, y, c, 4 * x + 2 * y + c


def _peer(k, x, y, c):
    px = 1 - x if k & 4 else x
    py = 1 - y if k & 2 else y
    pc = 1 - c if k & 1 else c
    return (px, py, pc), 4 * px + 2 * py + pc


def _remote(src, dst, send_sem, recv_sem, device):
    return pltpu.make_async_remote_copy(
        src_ref=src, dst_ref=dst, send_sem=send_sem, recv_sem=recv_sem, device_id=device, device_id_type=MESH
    )


def _silu_bf16(cf):
    return (cf * (1.0 / (1.0 + jnp.exp(-cf)))).astype(BF16)


def ada_exchange(c8, w4, b4, ln8):
    nt, _, ncol = w4.shape

    def body(c8_ref, w_ref, b_ref, ln_ref, cg_ref, lng_ref, mrecv_ref, mloc_ref, send_sems, recv_sems):
        x, y, c, me = _position()
        cg_ref[me] = c8_ref[...]
        lng_ref[me] = ln_ref[...]
        first = []
        for k in range(1, N_DEV):
            dev, _ = _peer(k, x, y, c)
            first.append(_remote(c8_ref, cg_ref.at[me], send_sems.at[0, k], recv_sems.at[0, k], dev))
            first.append(_remote(ln_ref, lng_ref.at[me], send_sems.at[1, k], recv_sems.at[1, k], dev))
        for cp in first:
            cp.start()
        for k in range(1, N_DEV):
            dev, pid = _peer(k, x, y, c)
            _remote(c8_ref, cg_ref.at[pid], send_sems.at[0, k], recv_sems.at[0, k], dev).wait_recv()
            _remote(ln_ref, lng_ref.at[pid], send_sems.at[1, k], recv_sems.at[1, k], dev).wait_recv()
        sc = _silu_bf16(cg_ref[...].reshape(N_DEV * SUBLANES, D_MODEL))
        for t in range(nt):
            mloc_ref[t] = jnp.dot(sc, w_ref[t].astype(BF16), preferred_element_type=F32) + b_ref[t : t + 1, :]

        def group(dev_id):
            return pl.ds(pl.multiple_of(dev_id * SUBLANES, SUBLANES), SUBLANES)

        mrecv_ref[me] = mloc_ref[:, group(me), :]
        second = []
        for k in range(1, N_DEV):
            dev, pid = _peer(k, x, y, c)
            second.append(
                _remote(mloc_ref.at[:, group(pid), :], mrecv_ref.at[me], send_sems.at[2, k], recv_sems.at[2, k], dev)
            )
        for cp in second:
            cp.start()
        for k in range(1, N_DEV):
            dev, pid = _peer(k, x, y, c)
            _remote(
                mloc_ref.at[:, group(pid), :], mrecv_ref.at[pid], send_sems.at[2, k], recv_sems.at[2, k], dev
            ).wait_recv()
        for cp in first + second:
            cp.wait_send()

    return pl.pallas_call(
        body,
        name="ada_exchange",
        in_specs=[VMEM, VMEM, VMEM, VMEM],
        out_specs=[VMEM, VMEM, VMEM],
        out_shape=[
            jax.ShapeDtypeStruct((N_DEV, SUBLANES, D_MODEL), F32),
            jax.ShapeDtypeStruct((N_DEV, SUBLANES, LANES), F32),
            jax.ShapeDtypeStruct((N_DEV, nt, SUBLANES, ncol), F32),
        ],
        scratch_shapes=[
            pltpu.VMEM((nt, N_DEV * SUBLANES, ncol), F32),
            pltpu.SemaphoreType.DMA((3, N_DEV)),
            pltpu.SemaphoreType.DMA((3, N_DEV)),
        ],
        compiler_params=pltpu.CompilerParams(vmem_limit_bytes=VMEM_LIMIT_BYTES),
    )(c8, w4, b4, ln8)


def small_exchange(dmx, flat, pre, after):
    rows = flat.shape[1]

    def body(dmx_ref, flat_ref, pre_ref, *rest):
        dmrecv_ref, red_ref, land_ref, send_sems, recv_sems = rest[len(after) :]
        x, y, c, me = _position()
        dmrecv_ref[me] = dmx_ref[me]
        land_ref[me] = flat_ref[me]
        first = []
        for k in range(1, N_DEV):
            dev, pid = _peer(k, x, y, c)
            first.append(_remote(dmx_ref.at[pid], dmrecv_ref.at[me], send_sems.at[0, k], recv_sems.at[0, k], dev))
            first.append(_remote(flat_ref.at[pid], land_ref.at[me], send_sems.at[1, k], recv_sems.at[1, k], dev))
        for cp in first:
            cp.start()
        for k in range(1, N_DEV):
            dev, pid = _peer(k, x, y, c)
            _remote(dmx_ref.at[pid], dmrecv_ref.at[pid], send_sems.at[0, k], recv_sems.at[0, k], dev).wait_recv()
            _remote(flat_ref.at[pid], land_ref.at[pid], send_sems.at[1, k], recv_sems.at[1, k], dev).wait_recv()
        total, total_pre = land_ref[0], pre_ref[0]
        for s in range(1, N_DEV):
            total, total_pre = total + land_ref[s], total_pre + pre_ref[s]
        red_ref[me, :rows, :] = total
        red_ref[me, rows:, :] = total_pre
        second = []
        for k in range(1, N_DEV):
            dev, _ = _peer(k, x, y, c)
            second.append(_remote(red_ref.at[me], red_ref.at[me], send_sems.at[2, k], recv_sems.at[2, k], dev))
        for cp in second:
            cp.start()
        for k in range(1, N_DEV):
            dev, pid = _peer(k, x, y, c)
            _remote(red_ref.at[pid], red_ref.at[pid], send_sems.at[2, k], recv_sems.at[2, k], dev).wait_recv()
        for cp in first + second:
            cp.wait_send()

    return pl.pallas_call(
        body,
        name="small_exchange",
        in_specs=[VMEM, VMEM, VMEM, *[ANY for _ in after]],
        out_specs=[VMEM, VMEM],
        out_shape=[
            jax.ShapeDtypeStruct(dmx.shape, F32),
            jax.ShapeDtypeStruct((N_DEV, rows + pre.shape[1], LANES), F32),
        ],
        scratch_shapes=[
            pltpu.VMEM(flat.shape, F32),
            pltpu.SemaphoreType.DMA((3, N_DEV)),
            pltpu.SemaphoreType.DMA((3, N_DEV)),
        ],
        compiler_params=pltpu.CompilerParams(vmem_limit_bytes=VMEM_LIMIT_BYTES),
    )(dmx, flat, pre, *after)


HBM = pl.BlockSpec(memory_space=pltpu.HBM)
SEM = pl.BlockSpec(memory_space=pltpu.SEMAPHORE)
EFFECT = pltpu.SideEffectType.DATAFLOW_SIDE_EFFECTING


REGROUP_ROWS = 256


def shards_to_columns(x, *, name):
    p, k, n = x.shape

    def body(x_ref, o_ref):
        for s in range(p):
            o_ref[:, s * n : (s + 1) * n] = x_ref[s]

    return pl.pallas_call(
        body,
        name=name,
        grid=(k // REGROUP_ROWS,),
        in_specs=[pl.BlockSpec((p, REGROUP_ROWS, n), lambda i: (0, i, 0))],
        out_specs=pl.BlockSpec((REGROUP_ROWS, p * n), lambda i: (i, 0)),
        out_shape=jax.ShapeDtypeStruct((k, p * n), x.dtype),
        compiler_params=_params(("parallel",)),
    )(x)


def columns_to_shards(xs, *, name):
    k = xs[0].shape[0]
    widths = [x.shape[1] for x in xs]
    n = sum(widths) // N_DEV
    pieces = []
    for s in range(N_DEV):
        start = 0
        for i, w in enumerate(widths):
            lo, hi = max(start, s * n), min(start + w, (s + 1) * n)
            if lo < hi:
                pieces.append((i, lo - start, s, lo - s * n, hi - lo))
            start += w

    def body(*refs):
        x_refs, o_ref = refs[: len(xs)], refs[-1]
        for i, c0, s, d0, w in pieces:
            o_ref[s, :, d0 : d0 + w] = x_refs[i][:, c0 : c0 + w]

    return pl.pallas_call(
        body,
        name=name,
        grid=(k // REGROUP_ROWS,),
        in_specs=[pl.BlockSpec((REGROUP_ROWS, w), lambda i: (i, 0)) for w in widths],
        out_specs=pl.BlockSpec((N_DEV, REGROUP_ROWS, n), lambda i: (0, i, 0)),
        out_shape=jax.ShapeDtypeStruct((N_DEV, k, n), xs[0].dtype),
        compiler_params=_params(("parallel",)),
    )(*xs)


def _own_slot(me, block):
    land = lax.empty((N_DEV, *block.shape), block.dtype)
    return lax.dynamic_update_slice_in_dim(land, block[None], me, axis=0)


N_CHIP_PEERS = 3


class Gather:
    def __init__(self, shards, lands, after, *, name):
        nt = len(shards)
        self.name = name

        def body(*refs):
            src_refs, land_refs = refs[:nt], refs[nt : 2 * nt]
            send_sems, recv_sems = refs[2 * nt + 1 : 3 * nt + 1], refs[3 * nt + 1 : 4 * nt + 1]
            token = refs[-1]
            x, y, c, me = _position()
            for t in range(nt):
                for k, dev in enumerate(self._targets(x, y, c)):
                    _remote(src_refs[t], land_refs[t].at[me], send_sems[t].at[k], recv_sems[t].at[k], dev).start()
            token[...] = jnp.zeros_like(token)

        outs = pl.pallas_call(
            body,
            name=name + "_start",
            in_specs=[HBM] * (2 * nt) + [ANY],
            out_specs=[SEM] * (2 * nt) + [HBM] * (2 * nt) + [VMEM],
            out_shape=[pltpu.SemaphoreType.DMA((1 + N_CHIP_PEERS,))] * (2 * nt)
            + [pltpu.HBM(a.shape, a.dtype) for a in (*shards, *lands)]
            + [jax.ShapeDtypeStruct((SUBLANES, LANES), F32)],
            input_output_aliases={i: 2 * nt + i for i in range(2 * nt)},
            compiler_params=pltpu.CompilerParams(has_side_effects=EFFECT),
        )(*[pltpu.with_memory_space_constraint(a, pltpu.HBM) for a in (*shards, *lands)], after)
        self.send_sems, self.recv_sems = list(outs[:nt]), list(outs[nt : 2 * nt])
        self.srcs, self.lands = list(outs[2 * nt : 3 * nt]), list(outs[3 * nt : 4 * nt])
        self.token = outs[-1]

    @staticmethod
    def _chips(x, y):
        return [(1 - x, y), (x, 1 - y), (1 - x, 1 - y)]

    @classmethod
    def _targets(cls, x, y, c):
        return [(x, y, 1 - c)] + [(*chip, c) for chip in cls._chips(x, y)]

    def zero(self):
        return self.token[0, 0]

    @staticmethod
    def _slot(px, py, pc):
        return 4 * px + 2 * py + pc

    def pass_on(self, which, after, *, name):
        n = len(which)

        def pass_body(*refs):
            land_refs, recv_sems = refs[:n], refs[n : 2 * n]
            fwd_send, fwd_recv = refs[3 * n + 1 : 4 * n + 1], refs[4 * n + 1 : 5 * n + 1]
            token = refs[-1]
            x, y, c, _ = _position()
            for t in range(n):
                for j, chip in enumerate(self._chips(x, y)):
                    blk = land_refs[t].at[self._slot(*chip, c)]
                    _remote(blk, blk, fwd_send[t].at[j], recv_sems[t].at[1 + j], (*chip, c)).wait_recv()
                    _remote(blk, blk, fwd_send[t].at[j], fwd_recv[t].at[j], (x, y, 1 - c)).start()
            token[...] = jnp.zeros_like(token)

        lands = [self.lands[t] for t in which]
        outs = pl.pallas_call(
            pass_body,
            name=name,
            in_specs=[HBM] * n + [SEM] * n + [ANY],
            out_specs=[HBM] * n + [SEM] * (2 * n) + [VMEM],
            out_shape=[pltpu.HBM(a.shape, a.dtype) for a in lands]
            + [pltpu.SemaphoreType.DMA((N_CHIP_PEERS,))] * (2 * n)
            + [jax.ShapeDtypeStruct((SUBLANES, LANES), F32)],
            input_output_aliases={i: i for i in range(n)},
            compiler_params=pltpu.CompilerParams(has_side_effects=EFFECT),
        )(*lands, *[self.recv_sems[t] for t in which], after)
        return (which, outs[:n], outs[n : 2 * n], outs[2 * n : 3 * n]), outs[-1]

    def wait(self, which, after, *, name):
        return self.finish(self.pass_on(which, after, name=name + "_pass")[0], after, name=name)

    def finish(self, passed, after, *, name):
        which, lands, fwd_send, fwd_recv = passed
        n = len(which)
        slot = self._slot

        def wait_body(*refs):
            src_refs, land_refs = refs[:n], refs[n : 2 * n]
            send_sems, recv_sems = refs[2 * n : 3 * n], refs[3 * n : 4 * n]
            fwd_send, fwd_recv = refs[4 * n : 5 * n], refs[5 * n : 6 * n]
            x, y, c, me = _position()
            sibling = (x, y, 1 - c)
            for t in range(n):
                for k, dev in enumerate(self._targets(x, y, c)):
                    _remote(src_refs[t], land_refs[t].at[me], send_sems[t].at[k], recv_sems[t].at[k], dev).wait_send()
                blk = land_refs[t].at[slot(x, y, 1 - c)]
                _remote(blk, blk, send_sems[t].at[0], recv_sems[t].at[0], sibling).wait_recv()
                for j, chip in enumerate(self._chips(x, y)):
                    sent = land_refs[t].at[slot(*chip, c)]
                    _remote(sent, sent, fwd_send[t].at[j], fwd_recv[t].at[j], sibling).wait_send()
                    got = land_refs[t].at[slot(*chip, 1 - c)]
                    _remote(got, got, fwd_send[t].at[j], fwd_recv[t].at[j], sibling).wait_recv()

        srcs = [self.srcs[t] for t in which]
        outs = pl.pallas_call(
            wait_body,
            name=name,
            in_specs=[HBM] * (2 * n) + [SEM] * (4 * n) + [ANY],
            out_specs=[HBM] * (2 * n),
            out_shape=[pltpu.HBM(a.shape, a.dtype) for a in (*srcs, *lands)],
            input_output_aliases={i: i for i in range(2 * n)},
            compiler_params=pltpu.CompilerParams(has_side_effects=EFFECT),
        )(
            *srcs, *lands, *[self.send_sems[t] for t in which], *[self.recv_sems[t] for t in which], *fwd_send,
            *fwd_recv, after,
        )
        return outs[n:]


class Scatter:
    def __init__(self, srcs, after, *, name):
        self.name = name
        nt = self.nt = len(srcs)
        peers = N_DEV - 1
        lands = [lax.empty(a.shape, a.dtype) for a in srcs]

        def body(*refs):
            src_refs, land_refs = refs[:nt], refs[nt : 2 * nt]
            send_sems, recv_sems = refs[2 * nt + 1 : 3 * nt + 1], refs[3 * nt + 1 : 4 * nt + 1]
            token = refs[-1]
            x, y, c, me = _position()
            for t in range(nt):
                for k in range(1, N_DEV):
                    dev, pid = _peer(k, x, y, c)
                    src = src_refs[t].at[pid]
                    _remote(src, land_refs[t].at[me], send_sems[t].at[k - 1], recv_sems[t].at[k - 1], dev).start()
                self._own(src_refs[t], land_refs[t], send_sems[t], me).start()
            token[...] = jnp.zeros_like(token)

        outs = pl.pallas_call(
            body,
            name=name + "_start",
            in_specs=[HBM] * (2 * nt) + [ANY],
            out_specs=[SEM] * (2 * nt) + [HBM] * (2 * nt) + [VMEM],
            out_shape=[pltpu.SemaphoreType.DMA((peers + 1,))] * (2 * nt)
            + [pltpu.HBM(a.shape, a.dtype) for a in (*srcs, *lands)]
            + [jax.ShapeDtypeStruct((SUBLANES, LANES), F32)],
            input_output_aliases={i: 2 * nt + i for i in range(2 * nt)},
            compiler_params=pltpu.CompilerParams(has_side_effects=EFFECT),
        )(*[pltpu.with_memory_space_constraint(a, pltpu.HBM) for a in (*srcs, *lands)], after)
        self.send_sems, self.recv_sems = outs[:nt], outs[nt : 2 * nt]
        self.srcs, self.lands = outs[2 * nt : 3 * nt], outs[3 * nt : 4 * nt]
        self.token = outs[-1]

    @staticmethod
    def _own(src_ref, land_ref, sems, me):
        return pltpu.make_async_copy(src_ref.at[me], land_ref.at[me], sems.at[N_DEV - 1])

    def zero(self):
        return self.token[0, 0]

    def wait(self, which, after, *, name):
        n = len(which)

        def body(*refs):
            src_refs, land_refs = refs[:n], refs[n : 2 * n]
            send_sems, recv_sems = refs[2 * n : 3 * n], refs[3 * n : 4 * n]
            x, y, c, me = _position()
            for t in range(n):
                for k in range(1, N_DEV):
                    dev, pid = _peer(k, x, y, c)
                    src = src_refs[t].at[pid]
                    cp = _remote(src, land_refs[t].at[pid], send_sems[t].at[k - 1], recv_sems[t].at[k - 1], dev)
                    cp.wait_send()
                    cp.wait_recv()
                self._own(src_refs[t], land_refs[t], send_sems[t], me).wait()

        srcs = [self.srcs[t] for t in which]
        lands = [self.lands[t] for t in which]
        outs = pl.pallas_call(
            body,
            name=name,
            in_specs=[HBM] * (2 * n) + [SEM] * (2 * n) + [ANY],
            out_specs=[HBM] * (2 * n),
            out_shape=[pltpu.HBM(a.shape, a.dtype) for a in (*srcs, *lands)],
            input_output_aliases={i: i for i in range(2 * n)},
            compiler_params=pltpu.CompilerParams(has_side_effects=EFFECT),
        )(*srcs, *lands, *[self.send_sems[t] for t in which], *[self.recv_sems[t] for t in which], after)
        return outs[n:]


def _adam_update(g, w, m, v):
    m2 = ADAM_B1 * m + (1.0 - ADAM_B1) * g
    v2 = ADAM_B2 * v + (1.0 - ADAM_B2) * jnp.square(g)
    m_hat = m2 / (1.0 - ADAM_B1**ADAM_STEP)
    v_hat = v2 / (1.0 - ADAM_B2**ADAM_STEP)
    delta = -ADAM_LR * (m_hat / (jnp.sqrt(v_hat) + ADAM_EPS) + ADAM_WD * w)
    return delta, m2, v2


def adamw(gparts, w, m, v, *, name):
    nl, r, c = w.shape
    p = gparts[0].shape[0]
    tr = r if r <= 256 else (256 if c <= D_MODEL else 128)
    ni = r // tr

    def body(*refs):
        g_refs = refs[:nl]
        w_ref, m_ref, v_ref, go_ref, d_ref, mo_ref, vo_ref = refs[nl:]
        for layer in range(nl):

            @pl.when(pl.program_id(0) == layer)
            def _(g_ref=g_refs[layer]):
                g = g_ref[0].astype(F32)
                for i in range(1, p):
                    g = g + g_ref[i].astype(F32)
                delta, m2, v2 = _adam_update(g, w_ref[...], m_ref[...], v_ref[...])
                go_ref[...] = g
                d_ref[...] = delta
                mo_ref[...] = m2
                vo_ref[...] = v2

    def parts_spec(layer):
        def index(l, i):
            return (0, jnp.where(l == layer, i, jnp.where(l < layer, 0, ni - 1)), 0)

        return pl.BlockSpec((p, tr, c), index)

    blk = pl.BlockSpec((None, tr, c), lambda l, i: (l, i, 0))
    return pl.pallas_call(
        body,
        name=name,
        grid=(nl, ni),
        in_specs=[*[parts_spec(layer) for layer in range(nl)], blk, blk, blk],
        out_specs=[blk] * 4,
        out_shape=[jax.ShapeDtypeStruct((nl, r, c), F32)] * 4,
        compiler_params=_params(("arbitrary", "arbitrary")),
    )(*gparts, w, m, v)


def adamw_small(items, *, name):
    n = len(items)

    def body(*refs):
        ins, outs = refs[: 4 * n], refs[4 * n :]
        for t in range(n):
            g_ref, w_ref, m_ref, v_ref = ins[4 * t : 4 * t + 4]
            g = g_ref[...]
            delta, m2, v2 = _adam_update(g, w_ref[...], m_ref[...], v_ref[...])
            for o_ref, val in zip(outs[4 * t : 4 * t + 4], (g, delta, m2, v2), strict=True):
                o_ref[...] = val

    outs = pl.pallas_call(
        body,
        name=name,
        out_shape=[jax.ShapeDtypeStruct(item[1].shape, F32) for item in items for _ in range(4)],
    )(*[a for item in items for a in item])
    return [outs[4 * t : 4 * t + 4] for t in range(n)]


def ada_grad_adamw(cg, dmrecv, w4, m4, v4, *, name):
    nt, k, ncol = w4.shape

    def body(cg_ref, dm_ref, w_ref, m_ref, v_ref, go_ref, d_ref, mo_ref, vo_ref, gb_ref):
        sc = _silu_bf16(cg_ref[...].reshape(N_DEV * SUBLANES, k))
        dm = dm_ref[...].reshape(N_DEV * SUBLANES, ncol)
        g = lax.dot_general(sc, dm.astype(BF16), (((0,), (0,)), ((), ())), preferred_element_type=F32)
        delta, m2, v2 = _adam_update(g, w_ref[...], m_ref[...], v_ref[...])
        go_ref[...] = g
        d_ref[...] = delta
        mo_ref[...] = m2
        vo_ref[...] = v2
        gb_ref[...] = jnp.broadcast_to(jnp.sum(dm, axis=0, keepdims=True), (SUBLANES, ncol))

    wblk = pl.BlockSpec((None, k, ncol), lambda t: (t, 0, 0))
    return pl.pallas_call(
        body,
        name=name,
        grid=(nt,),
        in_specs=[
            pl.BlockSpec((N_DEV, SUBLANES, k), lambda t: (0, 0, 0)),
            pl.BlockSpec((N_DEV, None, SUBLANES, ncol), lambda t: (0, t, 0, 0)),
            wblk,
            wblk,
            wblk,
        ],
        out_specs=[wblk] * 4 + [pl.BlockSpec((None, SUBLANES, ncol), lambda t: (t, 0, 0))],
        out_shape=[jax.ShapeDtypeStruct((nt, k, ncol), F32)] * 4 + [jax.ShapeDtypeStruct((nt, SUBLANES, ncol), F32)],
        compiler_params=_params(("parallel",)),
    )(cg, dmrecv, w4, m4, v4)


def kernel(x, c, ada_w, ada_b, ln_g, ln_b, a_w_in, a_b_in, a_vn_g, a_vn_b, a_w_s, a_b_s, a_w_out, b_w_qkv, b_w_out, mlp_w_up, mlp_w_down, loss_target, m_ada_w, m_ada_b, m_ln_g, m_ln_b, m_a_w_in, m_a_b_in, m_a_vn_g, m_a_vn_b, m_a_w_s, m_a_b_s, m_a_w_out, m_b_w_qkv, m_b_w_out, m_mlp_w_up, m_mlp_w_down, v_ada_w, v_ada_b, v_ln_g, v_ln_b, v_a_w_in, v_a_b_in, v_a_vn_g, v_a_vn_b, v_a_w_s, v_a_b_s, v_a_w_out, v_b_w_qkv, v_b_w_out, v_mlp_w_up, v_mlp_w_down):
    x0 = x[0]
    target = loss_target[0]
    me = 4 * lax.axis_index("x") + 2 * lax.axis_index("y") + lax.axis_index("c")

    ada_w4 = ada_w.reshape(N_SUB, D_MODEL, -1)
    ada_b4 = ada_b.reshape(N_SUB, -1)
    ln8 = jnp.concatenate([ln_g.reshape(N_SUB, -1), ln_b.reshape(N_SUB, -1)], axis=0)
    c8 = jnp.broadcast_to(c, (SUBLANES, D_MODEL))
    cg, lng, mrecv = ada_exchange(c8, ada_w4, ada_b4, ln8)

    W_IN, W_AOUT = range(2)
    W_UP0, W_DN0, W_QKV, W_BOUT, W_UP1, W_DN1 = range(6)
    shards = [
        a_w_in[0].astype(BF16),
        a_w_out[0].astype(BF16),
        mlp_w_up[0].astype(BF16),
        mlp_w_down[0].astype(BF16),
        b_w_qkv[0].astype(BF16),
        b_w_out[0].astype(BF16),
        mlp_w_up[1].astype(BF16),
        mlp_w_down[1].astype(BF16),
    ]
    gather_a = Gather(shards[:2], [_own_slot(me, s) for s in shards[:2]], mrecv, name="gather_a")
    gather = Gather(shards[2:], [_own_slot(me, s) for s in shards[2:]], gather_a.token, name="gather")

    modv = mrecv[:, :, 0, :].transpose(1, 0, 2).reshape(N_SUB, 3 * D_MODEL) + gather.zero()
    shift = [modv[t : t + 1, :D_MODEL] for t in range(N_SUB)]
    scale = [modv[t : t + 1, D_MODEL : 2 * D_MODEL] for t in range(N_SUB)]
    gate1 = [1.0 + modv[t : t + 1, 2 * D_MODEL :] for t in range(N_SUB)]
    lng_full = [lng[:, t, :].reshape(1, D_MODEL) for t in range(N_SUB)]
    lnb_full = [lng[:, N_SUB + t, :].reshape(1, D_MODEL) for t in range(N_SUB)]

    ident = lambda acc: (acc,)
    def relu2(a):
        r = jnp.maximum(a, jnp.zeros_like(a))
        return r * r
    vn_g, vn_b, w_s = a_vn_g, a_vn_b, a_w_s[0]
    bias_full = jnp.repeat(a_b_s[0].T, A_GROUP_DIM, axis=1)
    w_up3, w_dn3 = [None, None], [None, None]

    def mlp_forward(i, h, up, dn, x_in=None, t_next=None):
        w_up3[i], w_dn3[i] = up, dn.reshape(1, D_FF, D_MODEL)
        (a,) = mm_nn(h, w_up3[i], name=f"mlp{i}_up", tm=2048, ps=2, tn=512, tk=D_MODEL, epilogue=ident, outs=(BF16,))
        if t_next is None:
            (y,) = mm_nn(
                a, w_dn3[i], name=f"mlp{i}_down", tm=1024, ps=1, tn=512, tk=D_FF, prologue=relu2, epilogue=ident, outs=(BF16,)
            )
            return a, y
        y, xn, hn = mm_nn(
            a, w_dn3[i], name=f"mlp{i}_down", tm=512, ps=1, tn=D_MODEL, tk=D_FF, prologue=relu2,
            epilogue=residual_ln_epilogue, extras=residual_extras(x_in, t_next - 1), outs=(BF16, F32, BF16),
        )
        return a, y, xn, hn

    def residual_extras(x_in, t):
        rows = (gate1[t], lng_full[t], lnb_full[t], scale[t + 1], shift[t + 1])
        return [(x_in, "full")] + [(r, "row") for r in rows]

    h0 = modulate(x0, scale[0], shift[0], name="modulate0")
    w_in3, w_aout3 = gather_a.wait([W_IN, W_AOUT], h0, name="gather_wait_a")
    w_aout3 = w_aout3.reshape(1, D_MODEL, D_MODEL)
    (a_pre,) = mm_nn(
        h0, w_in3, name="a_in", tm=2048, ps=4, tn=256, tk=D_MODEL, epilogue=lambda acc, b: (acc + b,),
        extras=[(a_b_in, "row")], outs=(BF16,),
    )
    p_gate = gate_fwd(a_pre, vn_g, vn_b, w_s, bias_full, name="gate_fwd")
    y0, x1, h1 = mm_nn(
        p_gate, w_aout3, name="a_out", tm=1024, ps=1, tn=D_MODEL, tk=D_MODEL, epilogue=residual_ln_epilogue,
        extras=residual_extras(x0, 0), outs=(BF16, F32, BF16),
    )
    w_mlp0 = gather.wait([W_UP0, W_DN0], y0, name="gather_wait_mlp0")
    a1, y1, x2, h2 = mlp_forward(0, h1, *w_mlp0, x_in=x1, t_next=2)
    (w_qkv_shards,) = gather.wait([W_QKV], y1, name="gather_wait_qkv")
    w_qkv3 = shards_to_columns(w_qkv_shards, name="w_qkv_columns")[None]
    pat_tiles = 3
    dil = [d for _, d in B_PATTERNS]
    qkv_p, pat_o, pat_lse = [], [], []
    for g in range(N_PAT):
        (qkv_g,) = mm_nn(
            h2, w_qkv3, name=f"b_qkv{g}", tm=2048, ps=1, tn=D_MODEL, tk=D_MODEL, epilogue=ident, outs=(BF16,),
            b_tile0=pat_tiles * g, b_tiles=pat_tiles, out_streams=dil[g],
        )
        o_g, lse_g = attn_fwd(qkv_g, g, name=f"attn_fwd{g}", after=rest_token if g == 1 else None)
        if g == 0:
            rest_passed, rest_token = gather.pass_on([W_BOUT, W_UP1, W_DN1], lse_g, name="gather_pass_rest")
        qkv_p.append(qkv_g)
        pat_o.append(o_g)
        pat_lse.append(lse_g)
    o_b, o_f, *lse_pat = attn_combine(pat_o, pat_lse, name="attn_combine")
    lse_pat = [x.reshape(SEQ, LANES) for x in lse_pat]
    w_bout3, *w_mlp1 = gather.finish(rest_passed, lse_pat[0], name="gather_wait_rest")
    w_bout3 = w_bout3.reshape(1, D_MODEL, D_MODEL)
    y2, x3, h3 = mm_nn(
        o_b, w_bout3, name="b_out", tm=1024, ps=1, tn=D_MODEL, tk=D_MODEL, epilogue=residual_ln_epilogue,
        extras=residual_extras(x2, 2), outs=(BF16, F32, BF16),
    )
    a3, y3 = mlp_forward(1, h3, *w_mlp1)
    dz3, dy3, st3, loss_local = residual_ln_loss_bwd(
        x3, y3, gate1[3], lng_full[3], lnb_full[3], target, name="res_ln3_loss_bwd"
    )

    def scatter(parts, after, name):
        return Scatter([p.reshape(N_DEV, -1, p.shape[-1]) for p in parts], after, name=name)

    xs_in, ys = [x0, x1, x2, x3], [y0, y1, y2, y3]

    def residual_bwd_extras(dz_later, t):
        return [
            (dz_later, "full"), (scale[t + 1], "row"), (xs_in[t], "full"), (ys[t], "full"),
            (gate1[t], "row"), (lng_full[t], "row"), (lnb_full[t], "row"),
        ]

    def mlp_backward(i, h, a, dy, dz_later, t):
        (da,) = mm_nt(
            dy,
            w_dn3[i],
            name=f"mlp{i}_da",
            tm=2048,
            tko=1024,
            ps=1,
            tc=D_MODEL,
            epilogue=lambda acc, act: (acc * (2.0 * jnp.maximum(act.astype(F32), 0.0)),),
            extras=[(a, "full")],
            outs=(BF16,),
        )
        dw_dn = mm_tn(
            a, dy, name=f"mlp{i}_dw_down", p=1, tk=1024, ps=1, tn=D_MODEL, tmc=2048, prologue=relu2, out_dtype=BF16
        )
        dw_up = mm_tn(h, da, name=f"mlp{i}_dw_up", p=N_DEV, tk=1024, ps=2, tn=512, tmc=2048, out_dtype=BF16)
        rs = scatter([dw_up, dw_dn], da, f"scatter_mlp{i}")
        dz, dy_before, st = mm_nt(
            da, w_up3[i], name=f"mlp{i}_dh", tm=512, tko=D_MODEL, ps=N_DEV, tc=512, epilogue=residual_bwd_epilogue,
            extras=residual_bwd_extras(dz_later, t), outs=(F32, BF16), stats=True, after=rs.token,
        )
        return rs, dz, dy_before, st

    rs_mlp1, dz2, dy2, st2 = mlp_backward(1, h3, a3, dy3, dz3, 2)
    do_delta = mm_nt(
        dy2, w_bout3, name="b_do", tm=1024, tko=D_MODEL, ps=1, tc=D_MODEL, extras=[(o_f, "full")],
        epilogue=lambda acc, o: (acc,) * N_PAT + (_reduce_groups(acc * o),) * N_PAT,
        outs=(BF16,) * N_PAT + (F32,) * N_PAT, out_widths=[D_MODEL] * N_PAT + [LANES] * N_PAT, out_streams=dil + dil,
    )
    do_pat = [x.reshape(SEQ, D_MODEL) for x in do_delta[:N_PAT]]
    delta_pat = [x.reshape(SEQ, LANES) for x in do_delta[N_PAT:]]
    dh2, dw_pat = None, []
    for g in range(N_PAT):
        dqkv_g = attn_bwd(qkv_p[g], do_pat[g], lse_pat[g], delta_pat[g], g, name=f"attn_bwd{g}")
        dw_pat.append(
            mm_tn(
                h2, dqkv_g, name=f"b_dw_qkv{g}", p=1, tk=1024, ps=1, tn=D_MODEL, tmc=2048, out_dtype=BF16,
                g_streams=dil[g],
            )[0]
        )
        if g == N_PAT - 1:
            break
        (dh2,) = mm_nt(
            dqkv_g, w_qkv3, name=f"b_dh{g}", tm=2048, tko=512, ps=1, tc=pat_tiles * D_MODEL, outs=(F32,),
            b_tile0=g, g_streams=dil[g],
            epilogue=ident if g == 0 else (lambda acc, prev, d=dil[g]: (_from_streams(acc, d) + prev,)),
            extras=[] if g == 0 else [(dh2, "full")],
        )
    dw_bout = mm_tn(o_b, dy2, name="b_dw_out", p=1, tk=1024, ps=1, tn=D_MODEL, tmc=2048, out_dtype=BF16)
    dw_qkv = columns_to_shards(dw_pat, name="dw_qkv_shards")
    rs_b = scatter([dw_qkv, dw_bout], dqkv_g, "scatter_b")
    dz1, dy1, st1 = mm_nt(
        dqkv_g, w_qkv3, name=f"b_dh{N_PAT - 1}", tm=512, tko=D_MODEL, ps=1, tc=pat_tiles * D_MODEL, outs=(F32, BF16),
        b_tile0=N_PAT - 1, g_streams=dil[-1], stats=True,
        epilogue=lambda acc, prev, *rest: residual_bwd_epilogue(_from_streams(acc, dil[-1]) + prev, *rest),
        extras=[(dh2, "full")] + residual_bwd_extras(dz2, 1), after=rs_b.token,
    )
    rs_mlp0, dz0, dy0, st0 = mlp_backward(0, h1, a1, dy1, dz1, 0)
    (dp_gate,) = mm_nt(dy0, w_aout3, name="a_dp", tm=2048, tko=1024, ps=1, tc=D_MODEL, epilogue=ident, outs=(F32,))
    dw_aout = mm_tn(p_gate, dy0, name="a_dw_out", p=1, tk=1024, ps=1, tn=D_MODEL, tmc=2048, out_dtype=BF16)
    rs_aout = scatter([dw_aout], dp_gate, "scatter_a_out")
    da0, d_ws, d_bs, gate_rows = gate_bwd(a_pre, dp_gate, vn_g + rs_aout.zero(), vn_b, w_s, bias_full, name="gate_bwd")
    rs_ws = scatter([d_ws], gate_rows, "scatter_a_w_s")
    dw_in = mm_tn(h0, da0, name="a_dw_in", p=N_DEV, tk=1024, ps=4, tn=256, tmc=2048, out_dtype=BF16)
    rs_in = scatter([dw_in], rs_ws.token, "scatter_a_in")
    grad_x, stf = mm_nt(
        da0, w_in3, name="a_dh", tm=1024, tko=D_MODEL, ps=N_DEV, tc=256, epilogue=input_bwd_epilogue,
        extras=[(dz0, "full"), (scale[0], "row"), (x0, "full")], outs=(F32,), stats=True, after=rs_in.token,
    )

    results = {}

    def update(wname, gparts, w, m, v):
        shape = w.shape
        layers = len(gparts) if isinstance(gparts, list) else 1
        w3 = w.reshape(layers, -1, shape[-1])
        parts = [g.reshape(g.shape[0], *w3.shape[1:]) for g in (gparts if layers > 1 else [gparts])]
        outs = adamw(parts, w3, m.reshape(w3.shape), v.reshape(w3.shape), name=f"adamw_{wname}")
        results[wname] = [o.reshape(shape) for o in outs]

    g_up1, g_dn1 = rs_mlp1.wait([0, 1], grad_x, name="scatter_wait_mlp1")
    g_qkv, g_bout = rs_b.wait([0, 1], grad_x, name="scatter_wait_b")
    update("b_w_qkv", g_qkv, b_w_qkv, m_b_w_qkv, v_b_w_qkv)
    update("b_w_out", g_bout, b_w_out, m_b_w_out, v_b_w_out)
    g_up0, g_dn0 = rs_mlp0.wait([0, 1], grad_x, name="scatter_wait_mlp0")
    update("mlp_w_up", [g_up0, g_up1], mlp_w_up, m_mlp_w_up, v_mlp_w_up)
    update("mlp_w_down", [g_dn0, g_dn1], mlp_w_down, m_mlp_w_down, v_mlp_w_down)

    stats_after = [stf, st0, st1, st2]
    stats_own = [st0, st1, st2, st3]
    dm = jnp.stack(
        [
            jnp.concatenate(
                [stats_after[t][ST_DSHIFT], stats_after[t][ST_DSCALE], stats_own[t][ST_DGATE]], axis=0
            )
            for t in range(N_SUB)
        ]
    )
    ncol = 3 * D_MODEL // N_DEV
    dmx = jnp.pad(
        dm.reshape(N_SUB, N_DEV, ncol).transpose(1, 0, 2)[:, :, None, :], ((0, 0), (0, 0), (0, SUBLANES - 1), (0, 0))
    )
    small = [
        gate_rows[0],
        gate_rows[1],
        d_bs[:, :A_GROUPS].T.reshape(-1),
        *[stats_own[t][ST_DG] for t in range(N_SUB)],
        *[stats_own[t][ST_DB] for t in range(N_SUB)],
        jnp.pad(loss_local.reshape(1), (0, LANES - 1)),
    ]
    n_small = sum(s.size for s in small)
    part_rows = -(-n_small // (N_DEV * LANES * SUBLANES)) * SUBLANES
    flat = jnp.concatenate(small + [jnp.zeros((N_DEV * part_rows * LANES - n_small,), F32)])
    (ws_parts,) = rs_ws.wait([0], results["mlp_w_down"][3], name="scatter_wait_a_w_s")
    early = [results[n][3] for n in ("b_w_qkv", "b_w_out", "mlp_w_up", "mlp_w_down")]
    dmrecv, reduced = small_exchange(dmx, flat.reshape(N_DEV, part_rows, LANES), ws_parts, early)
    g_ws = reduced[:, part_rows:, :].reshape(-1)
    reduced = reduced[:, :part_rows, :].reshape(-1)
    sizes = [2 * D_MODEL, D_MODEL, D_MODEL, A_GROUPS * CHUNK, N_SUB * D_MODEL, N_SUB * D_MODEL]
    offs = [sum(sizes[:i]) for i in range(len(sizes) + 1)]
    g_b_in, g_vn_g, g_vn_b, g_bs, g_lng, g_lnb = [reduced[offs[i] : offs[i + 1]] for i in range(len(sizes))]
    loss = reduced[offs[-1]]

    ada_outs = ada_grad_adamw(cg, dmrecv, ada_w4, m_ada_w.reshape(ada_w4.shape), v_ada_w.reshape(ada_w4.shape), name="ada_grad_adamw")
    results["ada_w"] = [o.reshape(ada_w.shape) for o in ada_outs[:4]]
    ln_cols = D_MODEL // N_DEV
    my_ln = lambda gfull: lax.dynamic_slice_in_dim(gfull.reshape(N_SUB, N_DEV, ln_cols), me, 1, axis=1)
    small_params = [
        ("ada_b", ada_outs[4][:, 0, :], ada_b, m_ada_b, v_ada_b),
        ("ln_g", my_ln(g_lng), ln_g, m_ln_g, v_ln_g),
        ("ln_b", my_ln(g_lnb), ln_b, m_ln_b, v_ln_b),
        ("a_b_in", g_b_in, a_b_in, m_a_b_in, v_a_b_in),
        ("a_vn_g", g_vn_g, a_vn_g, m_a_vn_g, v_a_vn_g),
        ("a_vn_b", g_vn_b, a_vn_b, m_a_vn_b, v_a_vn_b),
        ("a_b_s", g_bs, a_b_s, m_a_b_s, v_a_b_s),
    ]
    small_outs = adamw_small(
        [[a.reshape(-1, w.shape[-1]) for a in (g, w, m, v)] for _, g, w, m, v in small_params], name="adamw_small"
    )
    for (wname, _, w, _, _), outs in zip(small_params, small_outs, strict=True):
        results[wname] = [o.reshape(w.shape) for o in outs]
    update("a_w_s", g_ws[None], a_w_s, m_a_w_s, v_a_w_s)
    (g_aout,) = rs_aout.wait([0], grad_x, name="scatter_wait_a_out")
    (g_in,) = rs_in.wait([0], grad_x, name="scatter_wait_a_in")
    update("a_w_in", g_in, a_w_in, m_a_w_in, v_a_w_in)
    update("a_w_out", g_aout, a_w_out, m_a_w_out, v_a_w_out)

    order = ["ada_w", "ada_b", "ln_g", "ln_b", "a_w_in", "a_b_in", "a_vn_g", "a_vn_b", "a_w_s", "a_b_s", "a_w_out", "b_w_qkv", "b_w_out", "mlp_w_up", "mlp_w_down"]
    return (loss, grad_x[None], *[results[n][0] for n in order], *[results[n][1] for n in order],
            *[results[n][2] for n in order], *[results[n][3] for n in order])
```

```python
import math

import jax
import jax.numpy as jnp
import numpy as np
from jax import lax
from jax.experimental import pallas as pl
from jax.experimental.pallas import tpu as pltpu

F32 = jnp.float32
BF16 = jnp.bfloat16
MESH = pl.DeviceIdType.MESH
ANY = pl.BlockSpec(memory_space=pl.ANY)
VMEM = pl.BlockSpec(memory_space=pltpu.VMEM)

N_DEV = 8
D_MODEL = 1024
SEQ = 4096
DEPTH = 2
CHUNK = 128
A_GROUPS = 16
A_GROUP_DIM = D_MODEL // A_GROUPS
B_HEADS = 16
B_HEAD_DIM = 64
B_PATTERNS = ((128, 1), (512, 4), (2048, 16))
N_PAT = len(B_PATTERNS)
SPAN = 128
D_FF = 4 * D_MODEL
D_QKV = N_PAT * 3 * D_MODEL
ALPHA = (2 * DEPTH) ** 0.25
LN_EPS = 1e-5
NEG = -1e30
ADAM_LR = 0.001
ADAM_B1 = 0.9
ADAM_B2 = 0.999
ADAM_EPS = 1e-08
ADAM_WD = 0.01
ADAM_STEP = 10
GELU_C = math.sqrt(2.0 / math.pi)
GELU_A = 0.044715

VMEM_LIMIT_BYTES = 56 * 1024 * 1024
LANES = 128
SUBLANES = 8
ROW_TILE = 512
N_SUB = 2 * DEPTH


def _params(sem):
    return pltpu.CompilerParams(dimension_semantics=sem, vmem_limit_bytes=VMEM_LIMIT_BYTES)


def _lane(shape):
    return lax.broadcasted_iota(jnp.int32, shape, len(shape) - 1)


def _split_bf16(x):
    hi = x.astype(BF16)
    lo = (x - hi.astype(F32)).astype(BF16)
    return hi, lo


def _group_expand_matrix(groups_padded, width):
    per = width // A_GROUPS
    r = lax.broadcasted_iota(jnp.int32, (groups_padded, width), 0)
    c = lax.broadcasted_iota(jnp.int32, (groups_padded, width), 1)
    return (c // per == r).astype(BF16)


def _group_reduce_matrix(width, groups_padded):
    per = width // A_GROUPS
    r = lax.broadcasted_iota(jnp.int32, (width, groups_padded), 0)
    c = lax.broadcasted_iota(jnp.int32, (width, groups_padded), 1)
    return (r // per == c).astype(BF16)


def _expand_groups(w):
    e = _group_expand_matrix(LANES, D_MODEL)
    hi, lo = _split_bf16(w)
    return jnp.dot(hi, e, preferred_element_type=F32) + jnp.dot(lo, e, preferred_element_type=F32)


def _reduce_groups(x):
    e = _group_reduce_matrix(D_MODEL, LANES)
    hi, lo = _split_bf16(x)
    return jnp.dot(hi, e, preferred_element_type=F32) + jnp.dot(lo, e, preferred_element_type=F32)


def _to_streams(x, d):
    rows, w = x.shape
    return jnp.swapaxes(x.reshape(rows // d, d, w), 0, 1).reshape(rows, w)


def _from_streams(x, d):
    rows, w = x.shape
    return jnp.swapaxes(x.reshape(d, rows // d, w), 0, 1).reshape(rows, w)


def _column_tiles(p, n, ps, tn):
    assert (ps == 1 or tn == n) and p % ps == 0 and n % tn == 0
    q = n // tn
    return (p // ps) * q, q


def _extra_specs(extras, tm, width):
    specs = []
    for _, kind in extras:
        if kind == "row":
            specs.append(pl.BlockSpec((1, width), lambda i, j, c: (0, j)))
        else:
            specs.append(pl.BlockSpec((tm, width), lambda i, j, c: (i, j)))
    return specs


def mm_nn(a, b3, *, name, tm, ps, tn, tk, epilogue, extras=(), outs, prologue=None, b_tile0=0, b_tiles=None, out_streams=1):
    m, k = a.shape
    p, _, n = b3.shape
    nj, q = _column_tiles(p, n, ps, tn)
    nj = nj if b_tiles is None else b_tiles
    nk = k // tk
    width = ps * tn
    d = out_streams
    assert d == 1 or not extras

    def body(a_ref, b_ref, *rest):
        ex = rest[: len(extras)]
        out_refs = rest[len(extras) : len(extras) + len(outs)]
        kk = pl.program_id(2)
        av = a_ref[...] if prologue is None else prologue(a_ref[...])
        if d > 1:
            av = _to_streams(av, d)

        def finish(cs, acc):
            res = epilogue(acc, *[e[:, cs] for e in ex])
            for o_ref, r in zip(out_refs, res, strict=True):
                if d > 1:
                    o_ref[:, :, cs] = r.astype(o_ref.dtype).reshape(d, tm // d, tn)
                else:
                    o_ref[:, cs] = r.astype(o_ref.dtype)

        for s in range(ps):
            cs = slice(s * tn, (s + 1) * tn)
            part = jnp.dot(av, b_ref[s], preferred_element_type=F32)
            if nk == 1:
                finish(cs, part)
                continue
            acc_ref = rest[-1]

            @pl.when(kk == 0)
            def _(part=part, cs=cs):
                acc_ref[:, cs] = part

            @pl.when(kk > 0)
            def _(part=part, cs=cs):
                acc_ref[:, cs] += part

        if nk > 1:

            @pl.when(kk == nk - 1)
            def _():
                for s in range(ps):
                    cs = slice(s * tn, (s + 1) * tn)
                    finish(cs, rest[-1][:, cs])

    if d > 1:
        out_spec = pl.BlockSpec((d, tm // d, width), lambda i, j, kk: (0, i, j))
        out_shape = (d, m // d, nj * width)
    else:
        out_spec = pl.BlockSpec((tm, width), lambda i, j, kk: (i, j))
        out_shape = (m, nj * width)
    res = pl.pallas_call(
        body,
        name=name,
        grid=(m // tm, nj, nk),
        in_specs=[
            pl.BlockSpec((tm, tk), lambda i, j, kk: (i, kk)),
            pl.BlockSpec((ps, tk, tn), lambda i, j, kk: ((j + b_tile0) // q, kk, (j + b_tile0) % q)),
            *_extra_specs(extras, tm, width),
        ],
        out_specs=[out_spec for _ in outs],
        out_shape=[jax.ShapeDtypeStruct(out_shape, dt) for dt in outs],
        scratch_shapes=[pltpu.VMEM((tm, width), F32)] if nk > 1 else [],
        compiler_params=_params(("parallel", "parallel", "arbitrary")),
    )(a, b3, *[arr for arr, _ in extras])
    return [r.reshape(m, nj * width) for r in res]


def mm_nt(
    g, b3, *, name, tm, tko, ps, tc, epilogue, extras=(), outs, out_widths=None, out_streams=None, b_tile0=0,
    g_streams=1, stats=False, after=None,
):
    m, width = g.shape
    p, k, n = b3.shape
    _, q = _column_tiles(p, n, ps, tc)
    nc = width // (ps * tc)
    ds = g_streams
    assert not stats or tko == k
    widths = [tko] * len(outs) if out_widths is None else out_widths
    streams_out = [1] * len(outs) if out_streams is None else out_streams

    def out_spec(w, d):
        if d == 1:
            return pl.BlockSpec((tm, w), lambda i, j, c: (i, j))
        return pl.BlockSpec((d, tm // d, w), lambda i, j, c: (0, i, j))

    def out_struct(dt, w, d):
        cols = w * (k // tko)
        return jax.ShapeDtypeStruct((m, cols) if d == 1 else (d, m // d, cols), dt)

    order = [] if after is None else [after]
    n_in = len(extras) + len(order)

    def body(g_ref, b_ref, *rest):
        ex = rest[: len(extras)]
        out_refs = rest[n_in : n_in + len(outs)]
        c = pl.program_id(2)
        gv = g_ref[...].reshape(tm, ps * tc) if ds > 1 else g_ref[...]
        part = None
        for s in range(ps):
            d = lax.dot_general(gv[:, s * tc : (s + 1) * tc], b_ref[s], _NT, preferred_element_type=F32)
            part = d if part is None else part + d

        def finish(acc):
            res = epilogue(acc, *[e[...] for e in ex])
            if stats:
                *res, sums = res
                st_ref = rest[n_in + len(outs)]

                @pl.when(pl.program_id(0) == 0)
                def _():
                    st_ref[...] = jnp.zeros_like(st_ref)

                for row, val in enumerate(sums):
                    st_ref[row : row + 1, :] += val
            for o_ref, r, d_out in zip(out_refs, res, streams_out, strict=True):
                r = r.astype(o_ref.dtype)
                o_ref[...] = r if d_out == 1 else _to_streams(r, d_out).reshape(o_ref.shape)

        if nc == 1:
            finish(part)
            return
        acc_ref = rest[-1]

        @pl.when(c == 0)
        def _():
            acc_ref[...] = part

        @pl.when(c > 0)
        def _():
            acc_ref[...] += part

        @pl.when(c == nc - 1)
        def _():
            finish(acc_ref[...])

    return pl.pallas_call(
        body,
        name=name,
        grid=(m // tm, k // tko, nc),
        in_specs=[
            pl.BlockSpec((ds, tm // ds, ps * tc), lambda i, j, c: (0, i, c))
            if ds > 1
            else pl.BlockSpec((tm, ps * tc), lambda i, j, c: (i, c)),
            pl.BlockSpec((ps, tko, tc), lambda i, j, c: ((c + b_tile0) // q, j, (c + b_tile0) % q)),
            *_extra_specs(extras, tm, tko),
            *[ANY for _ in order],
        ],
        out_specs=[out_spec(w, d) for w, d in zip(widths, streams_out, strict=True)]
        + [pl.BlockSpec((SUBLANES, tko), lambda i, j, c: (0, 0))] * stats,
        out_shape=[out_struct(dt, w, d) for dt, w, d in zip(outs, widths, streams_out, strict=True)]
        + [jax.ShapeDtypeStruct((SUBLANES, k), F32)] * stats,
        scratch_shapes=[pltpu.VMEM((tm, tko), F32)] if nc > 1 else [],
        compiler_params=_params(("arbitrary" if stats else "parallel", "parallel", "arbitrary")),
    )(g.reshape(ds, m // ds, width) if ds > 1 else g, b3, *[arr for arr, _ in extras], *order)


def mm_tn(a, g, *, name, p, tk, ps, tn, tmc, out_dtype, prologue=None, g_streams=1):
    m, k = a.shape
    width = g.shape[1]
    n = width // p
    nj, q = _column_tiles(p, n, ps, tn)
    nc = m // tmc
    ds = g_streams

    def body(a_ref, g_ref, o_ref, acc_ref):
        c = pl.program_id(2)
        av = a_ref[...] if prologue is None else prologue(a_ref[...])
        gv = g_ref[...]
        if ds > 1:
            av, gv = _to_streams(av, ds), gv.reshape(tmc, ps * tn)
        part = lax.dot_general(av, gv, (((0,), (0,)), ((), ())), preferred_element_type=F32)

        @pl.when(c == 0)
        def _():
            acc_ref[...] = part

        @pl.when(c > 0)
        def _():
            acc_ref[...] += part

        @pl.when(c == nc - 1)
        def _():
            for s in range(ps):
                o_ref[s] = acc_ref[:, s * tn : (s + 1) * tn].astype(o_ref.dtype)

    return pl.pallas_call(
        body,
        name=name,
        grid=(k // tk, nj, nc),
        in_specs=[
            pl.BlockSpec((tmc, tk), lambda i, j, c: (c, i)),
            pl.BlockSpec((ds, tmc // ds, ps * tn), lambda i, j, c: (0, c, j))
            if ds > 1
            else pl.BlockSpec((tmc, ps * tn), lambda i, j, c: (c, j)),
        ],
        out_specs=pl.BlockSpec((ps, tk, tn), lambda i, j, c: (j // q, i, j % q)),
        out_shape=jax.ShapeDtypeStruct((p, k, n), out_dtype),
        scratch_shapes=[pltpu.VMEM((tk, ps * tn), F32)],
        compiler_params=_params(("parallel", "parallel", "arbitrary")),
    )(a, g.reshape(ds, m // ds, width) if ds > 1 else g)


def _rows(cols):
    return pl.BlockSpec((ROW_TILE, cols), lambda i: (i, 0))


def _vec(cols, rows=1):
    return pl.BlockSpec((rows, cols), lambda i: (0, 0))


def _layer_norm_hat(z):
    mu = jnp.mean(z, axis=-1, keepdims=True)
    zc = z - mu
    var = jnp.mean(zc * zc, axis=-1, keepdims=True)
    rstd = lax.rsqrt(var + LN_EPS)
    return zc * rstd, rstd


def modulate(x, scale, shift, *, name):
    s, d = x.shape

    def body(x_ref, sc_ref, sh_ref, h_ref):
        h_ref[...] = (x_ref[...] * (1.0 + sc_ref[...]) + sh_ref[...]).astype(BF16)

    return pl.pallas_call(
        body,
        name=name,
        grid=(s // ROW_TILE,),
        in_specs=[_rows(d), _vec(d), _vec(d)],
        out_specs=_rows(d),
        out_shape=jax.ShapeDtypeStruct((s, d), BF16),
        compiler_params=_params(("parallel",)),
    )(x, scale, shift)


def residual_ln_epilogue(acc, x, gate1, g, b, nscale, nshift):
    y = acc.astype(BF16)
    xhat, _ = _layer_norm_hat(ALPHA * x + gate1 * y.astype(F32))
    xn = xhat * g + b
    return y, xn, xn * (1.0 + nscale) + nshift


ST_DSCALE, ST_DSHIFT, ST_DG, ST_DB, ST_DGATE = 0, 1, 2, 3, 4


def _layer_norm_bwd(g_out, xhat, rstd, g):
    dxh = g_out * g
    m1 = jnp.mean(dxh, axis=-1, keepdims=True)
    m2 = jnp.mean(dxh * xhat, axis=-1, keepdims=True)
    return rstd * (dxh - m1 - xhat * m2)


def _column_sums(vals):
    return [jnp.sum(v, axis=0, keepdims=True) for v in vals]


def residual_bwd_epilogue(dh, dzl, scl, x, y, gate1, g, b):
    yf = y.astype(F32)
    xhat, rstd = _layer_norm_hat(ALPHA * x + gate1 * yf)
    g_out = ALPHA * dzl + dh * (1.0 + scl)
    dz = _layer_norm_bwd(g_out, xhat, rstd, g)
    return dz, dz * gate1, _column_sums([dh * (xhat * g + b), dh, g_out * xhat, g_out, dz * yf])


def input_bwd_epilogue(dh, dzl, scl, x):
    return ALPHA * dzl + dh * (1.0 + scl), _column_sums([dh * x, dh])


def residual_ln_loss_bwd(x, y, gate1, g, b, target, *, name):
    s, d = x.shape

    def body(x_ref, y_ref, gt_ref, g_ref, b_ref, t_ref, dz_ref, dy_ref, st_ref, loss_ref):
        y = y_ref[...].astype(F32)
        gate1 = gt_ref[...]
        xhat, rstd = _layer_norm_hat(ALPHA * x_ref[...] + gate1 * y)
        err = xhat * g_ref[...] + b_ref[...] - t_ref[...]
        g_out = err * (1.0 / d)
        dz = _layer_norm_bwd(g_out, xhat, rstd, g_ref[...])
        dz_ref[...] = dz
        dy_ref[...] = (dz * gate1).astype(BF16)
        part = jnp.sum(jnp.sum(err * err, axis=1, keepdims=True), axis=0, keepdims=True) * (0.5 / d)

        @pl.when(pl.program_id(0) == 0)
        def _():
            st_ref[...] = jnp.zeros_like(st_ref)
            loss_ref[...] = jnp.zeros_like(loss_ref)

        loss_ref[...] += part
        for row, val in zip((ST_DG, ST_DB, ST_DGATE), _column_sums([g_out * xhat, g_out, dz * y]), strict=True):
            st_ref[row : row + 1, :] += val

    return pl.pallas_call(
        body,
        name=name,
        grid=(s // ROW_TILE,),
        in_specs=[_rows(d), _rows(d), _vec(d), _vec(d), _vec(d), _rows(d)],
        out_specs=[_rows(d), _rows(d), _vec(d, SUBLANES), pl.BlockSpec((1, 1), lambda i: (0, 0))],
        out_shape=[
            jax.ShapeDtypeStruct((s, d), F32),
            jax.ShapeDtypeStruct((s, d), BF16),
            jax.ShapeDtypeStruct((SUBLANES, d), F32),
            jax.ShapeDtypeStruct((1, 1), F32),
        ],
        compiler_params=_params(("arbitrary",)),
    )(x, y, gate1, g, b, target)


GATE_CHUNKS = 4


def _gelu(x, with_grad=False):
    x2 = x * x
    t = jnp.tanh(GELU_C * (x + GELU_A * x2 * x))
    half = 0.5 * (1.0 + t)
    y = x * half
    if not with_grad:
        return y
    return y, half + 0.5 * x * (1.0 - t * t) * (GELU_C * (1.0 + 3.0 * GELU_A * x2))


def _causal_weights(w_ref, transpose):
    t = lax.broadcasted_iota(jnp.int32, (CHUNK, CHUNK), 0)
    s = lax.broadcasted_iota(jnp.int32, (CHUNK, CHUNK), 1)
    out = []
    for g in range(A_GROUPS):
        w = jnp.where(t >= s, w_ref[g], 0.0)
        out.append((w.T if transpose else w).astype(BF16))
    return out


def _spatial(ws, vn, lo_mask):
    rows = vn.shape[0]
    out_rows = []
    for r in range(rows // CHUNK):
        cols = []
        for j in range(A_GROUPS // 2):
            blk = vn[r * CHUNK : (r + 1) * CHUNK, j * LANES : (j + 1) * LANES]
            za = jnp.dot(ws[2 * j], blk, preferred_element_type=F32)
            zb = jnp.dot(ws[2 * j + 1], blk, preferred_element_type=F32)
            cols.append(jnp.where(lo_mask, za, zb))
        out_rows.append(jnp.concatenate(cols, axis=1))
    return jnp.concatenate(out_rows, axis=0)


def _gate_forward(a, vg, vb, ws, bias, lo_mask, with_grad=False):
    u = _gelu(a[:, :D_MODEL], with_grad)
    v = _gelu(a[:, D_MODEL:], with_grad)
    gu, gv = None, None
    if with_grad:
        (u, gu), (v, gv) = u, v
    vhat, rstd = _layer_norm_hat(v)
    vn = (vhat * vg + vb).astype(BF16)
    z = _spatial(ws, vn, lo_mask) + jnp.concatenate([bias] * (a.shape[0] // CHUNK), axis=0)
    return u, vhat, rstd, vn, z, gu, gv


def gate_fwd(a_pre, vn_g, vn_b, w_s, bias_full, *, name):
    s = a_pre.shape[0]
    tr = GATE_CHUNKS * CHUNK

    def body(a_ref, vg_ref, vb_ref, w_ref, bias_ref, p_ref):
        lo_mask = _lane((CHUNK, LANES)) < A_GROUP_DIM
        ws = _causal_weights(w_ref, transpose=False)
        u, _, _, _, z, _, _ = _gate_forward(a_ref[...].astype(F32), vg_ref[...], vb_ref[...], ws, bias_ref[...], lo_mask)
        p_ref[...] = (u * z).astype(BF16)

    return pl.pallas_call(
        body,
        name=name,
        grid=(s // tr,),
        in_specs=[
            pl.BlockSpec((tr, 2 * D_MODEL), lambda i: (i, 0)),
            _vec(D_MODEL),
            _vec(D_MODEL),
            pl.BlockSpec((A_GROUPS, CHUNK, CHUNK), lambda i: (0, 0, 0)),
            _vec(D_MODEL, CHUNK),
        ],
        out_specs=pl.BlockSpec((tr, D_MODEL), lambda i: (i, 0)),
        out_shape=jax.ShapeDtypeStruct((s, D_MODEL), BF16),
        compiler_params=_params(("parallel",)),
    )(a_pre, vn_g, vn_b, w_s, bias_full)


def gate_bwd(a_pre, dp, vn_g, vn_b, w_s, bias_full, *, name):
    s = a_pre.shape[0]
    tr = GATE_CHUNKS * CHUNK
    nsteps = s // tr

    def body(a_ref, dp_ref, vg_ref, vb_ref, w_ref, bias_ref, da_ref, dw_ref, dbs_ref, rows_ref, dbias_acc):
        step = pl.program_id(0)
        lo_mask = _lane((CHUNK, LANES)) < A_GROUP_DIM

        @pl.when(step == 0)
        def _():
            dw_ref[...] = jnp.zeros_like(dw_ref)
            rows_ref[...] = jnp.zeros_like(rows_ref)
            dbias_acc[...] = jnp.zeros_like(dbias_acc)

        a = a_ref[...].astype(F32)
        vg = vg_ref[...]
        ws = _causal_weights(w_ref, transpose=False)
        wts = _causal_weights(w_ref, transpose=True)
        u, vhat, rstd, vn, z, gelu_du, gelu_dv = _gate_forward(a, vg, vb_ref[...], ws, bias_ref[...], lo_mask, True)
        dp = dp_ref[...]
        du = dp * z
        dzz = dp * u
        dzz_b = dzz.astype(BF16)
        dvn = _spatial(wts, dzz_b, lo_mask)
        dbias = None
        for r in range(GATE_CHUNKS):
            rs = slice(r * CHUNK, (r + 1) * CHUNK)
            dbias = dzz[rs] if dbias is None else dbias + dzz[rs]
            for j in range(A_GROUPS // 2):
                cs = slice(j * LANES, (j + 1) * LANES)
                dblk = dzz[rs, cs]
                vblk = vn[rs, cs]
                for half in range(2):
                    keep = lo_mask if half == 0 else jnp.logical_not(lo_mask)
                    dm = jnp.where(keep, dblk, 0.0).astype(BF16)
                    dw_ref[2 * j + half] += lax.dot_general(
                        dm, vblk, (((1,), (1,)), ((), ())), preferred_element_type=F32
                    )
        dbias_acc[...] += dbias
        rows_ref[1:2, :D_MODEL] += jnp.sum(dvn * vhat, axis=0, keepdims=True)
        rows_ref[1:2, D_MODEL:] += jnp.sum(dvn, axis=0, keepdims=True)
        dvh = dvn * vg
        m1 = jnp.mean(dvh, axis=-1, keepdims=True)
        m2 = jnp.mean(dvh * vhat, axis=-1, keepdims=True)
        dv = rstd * (dvh - m1 - vhat * m2)
        da_u = du * gelu_du
        da_v = dv * gelu_dv
        da_ref[:, :D_MODEL] = da_u.astype(BF16)
        da_ref[:, D_MODEL:] = da_v.astype(BF16)
        rows_ref[0:1, :D_MODEL] += jnp.sum(da_u, axis=0, keepdims=True)
        rows_ref[0:1, D_MODEL:] += jnp.sum(da_v, axis=0, keepdims=True)

        @pl.when(step == nsteps - 1)
        def _():
            t = lax.broadcasted_iota(jnp.int32, (CHUNK, CHUNK), 0)
            sx = lax.broadcasted_iota(jnp.int32, (CHUNK, CHUNK), 1)
            for g in range(A_GROUPS):
                dw_ref[g] = jnp.where(t >= sx, dw_ref[g], 0.0)
            dbs_ref[...] = _reduce_groups(dbias_acc[...])

    return pl.pallas_call(
        body,
        name=name,
        grid=(nsteps,),
        in_specs=[
            pl.BlockSpec((tr, 2 * D_MODEL), lambda i: (i, 0)),
            pl.BlockSpec((tr, D_MODEL), lambda i: (i, 0)),
            _vec(D_MODEL),
            _vec(D_MODEL),
            pl.BlockSpec((A_GROUPS, CHUNK, CHUNK), lambda i: (0, 0, 0)),
            _vec(D_MODEL, CHUNK),
        ],
        out_specs=[
            pl.BlockSpec((tr, 2 * D_MODEL), lambda i: (i, 0)),
            pl.BlockSpec((A_GROUPS, CHUNK, CHUNK), lambda i: (0, 0, 0)),
            _vec(LANES, CHUNK),
            _vec(2 * D_MODEL, SUBLANES),
        ],
        out_shape=[
            jax.ShapeDtypeStruct((s, 2 * D_MODEL), BF16),
            jax.ShapeDtypeStruct((A_GROUPS, CHUNK, CHUNK), F32),
            jax.ShapeDtypeStruct((CHUNK, LANES), F32),
            jax.ShapeDtypeStruct((SUBLANES, 2 * D_MODEL), F32),
        ],
        scratch_shapes=[pltpu.VMEM((CHUNK, D_MODEL), F32)],
        compiler_params=_params(("arbitrary",)),
    )(a_pre, dp, vn_g, vn_b, w_s, bias_full)


def alibi_tables(dilation):
    qi = np.arange(SPAN)[:, None]
    ki = np.arange(2 * SPAN)[None, :]
    diff = SPAN + qi - ki
    valid = (diff >= 0) & (diff <= SPAN)
    heads = np.arange(1, B_HEADS + 1, dtype=np.float32)
    slopes = np.exp2(np.float32(-8.0) * heads / np.float32(B_HEADS)).astype(np.float32)
    bias = -slopes[:, None, None] * (dilation * diff).astype(np.float32)
    bias = np.where(valid[None], bias, np.float32(NEG)).reshape(B_HEADS // 2, 2 * SPAN, 2 * SPAN)
    return jnp.asarray(bias), jnp.asarray(np.ascontiguousarray(bias.transpose(0, 2, 1)))


def _pair_rows(x, halves):
    return jnp.concatenate([x * halves[0], x * halves[1]], axis=0)


def _pair_column(v, lane, j):
    pick = lambda h: jnp.sum(jnp.where(lane == h, v, 0.0), axis=1, keepdims=True)
    return jnp.concatenate([pick(2 * j), pick(2 * j + 1)], axis=0)


_NT = (((1,), (1,)), ((), ()))


def _qkv_specs(block_of):
    def spec(which, prev):
        def index(*grid):
            blk = block_of(*grid)
            return (jnp.maximum(blk - 1, 0) if prev else blk, which)

        return pl.BlockSpec((SPAN, D_MODEL), index)

    return [spec(0, False), spec(1, True), spec(1, False), spec(2, True), spec(2, False)]


def attn_fwd(qkv_p, pat, *, name, after=None):
    _, dilation = B_PATTERNS[pat]
    nb = SEQ // dilation // SPAN
    bias, _ = alibi_tables(dilation)
    order = [] if after is None else [after]

    def body(q_ref, kp_ref, kc_ref, vp_ref, vc_ref, bias_ref, *rest):
        o_ref, lse_ref = rest[len(order) :]
        n = pl.program_id(1)
        first_prev = jnp.logical_and(n == 0, _lane((2 * SPAN, 2 * SPAN)) < SPAN)
        lane = _lane((SPAN, LANES))
        lo_mask = lane < B_HEAD_DIM
        halves = (lo_mask.astype(BF16), jnp.logical_not(lo_mask).astype(BF16))
        stats = jnp.zeros((SPAN, LANES), F32)
        for j in range(B_HEADS // 2):
            cs = slice(j * LANES, (j + 1) * LANES)
            q = q_ref[:, cs] * jnp.asarray(B_HEAD_DIM**-0.5, BF16)
            kp = jnp.concatenate([kp_ref[:, cs], kc_ref[:, cs]], axis=0)
            vp = jnp.concatenate([vp_ref[:, cs], vc_ref[:, cs]], axis=0)
            sc = lax.dot_general(_pair_rows(q, halves), kp, _NT, preferred_element_type=F32)
            sc = jnp.where(first_prev, NEG, sc + bias_ref[j])
            m = jnp.max(sc, axis=1, keepdims=True)
            p = jnp.exp(sc - m)
            l = jnp.sum(p, axis=1, keepdims=True)
            acc = jnp.dot(p.astype(BF16), vp, preferred_element_type=F32) * (1.0 / l)
            lse_pair = m + jnp.log(l)
            o_ref[:, cs] = jnp.where(lo_mask, acc[:SPAN], acc[SPAN:]).astype(BF16)
            stats = jnp.where(lane == 2 * j, lse_pair[:SPAN], stats)
            stats = jnp.where(lane == 2 * j + 1, lse_pair[SPAN:], stats)
        lse_ref[...] = stats

    return pl.pallas_call(
        body,
        name=name,
        grid=(dilation, nb),
        in_specs=[
            *_qkv_specs(lambda r, n: r * nb + n),
            pl.BlockSpec((B_HEADS // 2, 2 * SPAN, 2 * SPAN), lambda r, n: (0, 0, 0)),
            *[ANY for _ in order],
        ],
        out_specs=[
            pl.BlockSpec((SPAN, D_MODEL), lambda r, n: (r * nb + n, 0)),
            pl.BlockSpec((SPAN, LANES), lambda r, n: (r * nb + n, 0)),
        ],
        out_shape=[jax.ShapeDtypeStruct((SEQ, D_MODEL), BF16), jax.ShapeDtypeStruct((SEQ, LANES), F32)],
        compiler_params=_params(("parallel", "arbitrary")),
    )(qkv_p, qkv_p, qkv_p, qkv_p, qkv_p, bias, *order)


def attn_combine(outs, lses, *, name):
    dils = [d for _, d in B_PATTERNS]

    def positions(ref, d):
        x = ref[...].astype(F32)
        return x if d == 1 else jnp.swapaxes(x, 0, 1).reshape(ROW_TILE, x.shape[-1])

    def body(o0, o1, o2, l0, l1, l2, ob_ref, of_ref, *lse_refs):
        ls = [positions(l, d) for l, d in zip((l0, l1, l2), dils, strict=True)]
        m = jnp.maximum(jnp.maximum(ls[0], ls[1]), ls[2])
        tot = jnp.log(jnp.exp(ls[0] - m) + jnp.exp(ls[1] - m) + jnp.exp(ls[2] - m)) + m
        o = None
        for o_ref, l, d in zip((o0, o1, o2), ls, dils, strict=True):
            term = _expand_groups(jnp.exp(l - tot)) * positions(o_ref, d)
            o = term if o is None else o + term
        ob_ref[...] = o.astype(BF16)
        of_ref[...] = o
        for lse_ref, d in zip(lse_refs, dils, strict=True):
            lse_ref[...] = tot if d == 1 else _to_streams(tot, d).reshape(lse_ref.shape)

    def stream_rows(cols, d):
        return _rows(cols) if d == 1 else pl.BlockSpec((d, ROW_TILE // d, cols), lambda i: (0, i, 0))

    def streams(x, d):
        return x if d == 1 else x.reshape(d, SEQ // d, x.shape[-1])

    outs = [streams(x, d) for x, d in zip(outs, dils, strict=True)]
    lses = [streams(x, d) for x, d in zip(lses, dils, strict=True)]
    return pl.pallas_call(
        body,
        name=name,
        grid=(SEQ // ROW_TILE,),
        in_specs=[stream_rows(D_MODEL, d) for d in dils] + [stream_rows(LANES, d) for d in dils],
        out_specs=[_rows(D_MODEL), _rows(D_MODEL)] + [stream_rows(LANES, d) for d in dils],
        out_shape=[jax.ShapeDtypeStruct((SEQ, D_MODEL), BF16), jax.ShapeDtypeStruct((SEQ, D_MODEL), F32)]
        + [jax.ShapeDtypeStruct(lse.shape, F32) for lse in lses],
        compiler_params=_params(("parallel",)),
    )(*outs, *lses)


def attn_bwd(qkv_p, do_p, lse_p, delta_p, pat, *, name):
    _, dilation = B_PATTERNS[pat]
    nb = SEQ // dilation // SPAN
    n_blocks = SEQ // SPAN
    bias, bias_t = alibi_tables(dilation)
    last = n_blocks - 1
    q_cols, k_cols, v_cols = (slice(i * D_MODEL, (i + 1) * D_MODEL) for i in range(3))

    def body(q_ref, kp_ref, kc_ref, vp_ref, vc_ref, do_ref, lse_ref, dl_ref, bias_ref, biast_ref, out_ref, cq_ref, ck_ref, cv_ref):
        g = pl.program_id(0)

        @pl.when(g == n_blocks)
        def _():
            out_ref[:, q_cols] = cq_ref[...].astype(BF16)
            out_ref[:, k_cols] = ck_ref[...].astype(BF16)
            out_ref[:, v_cols] = cv_ref[...].astype(BF16)

        @pl.when(g == 0)
        def _():
            cq_ref[...] = jnp.zeros_like(cq_ref)
            ck_ref[...] = jnp.zeros_like(ck_ref)
            cv_ref[...] = jnp.zeros_like(cv_ref)

        @pl.when(g < n_blocks)
        def _():
            lane = _lane((SPAN, LANES))
            lo_mask = lane < B_HEAD_DIM
            pair = (2 * SPAN, 2 * SPAN)
            first = lax.rem(g, nb) == 0
            prev_key_cols = jnp.logical_and(first, _lane(pair) < SPAN)
            prev_key_rows = jnp.logical_and(first, lax.broadcasted_iota(jnp.int32, pair, 0) < SPAN)
            lse_v = lse_ref[...]
            dl_v = dl_ref[...]
            lse_t = lse_v.T
            dl_t = dl_v.T
            halves = (lo_mask.astype(BF16), jnp.logical_not(lo_mask).astype(BF16))
            for j in range(B_HEADS // 2):
                cs = slice(j * LANES, (j + 1) * LANES)
                kp = jnp.concatenate([kp_ref[:, cs], kc_ref[:, cs]], axis=0)
                vp = jnp.concatenate([vp_ref[:, cs], vc_ref[:, cs]], axis=0)
                q2 = _pair_rows(q_ref[:, cs] * jnp.asarray(B_HEAD_DIM**-0.5, BF16), halves)
                do2 = _pair_rows(do_ref[:, cs], halves)
                lse_c, dl_c = _pair_column(lse_v, lane, j), _pair_column(dl_v, lane, j)
                lse_r = jnp.concatenate([lse_t[2 * j : 2 * j + 1], lse_t[2 * j + 1 : 2 * j + 2]], axis=1)
                dl_r = jnp.concatenate([dl_t[2 * j : 2 * j + 1], dl_t[2 * j + 1 : 2 * j + 2]], axis=1)
                sc = lax.dot_general(q2, kp, _NT, preferred_element_type=F32)
                p = jnp.exp(jnp.where(prev_key_cols, NEG, sc + bias_ref[j]) - lse_c)
                dp = lax.dot_general(do2, vp, _NT, preferred_element_type=F32)
                ds = (p * (dp - dl_c)).astype(BF16)
                dq2 = jnp.dot(ds, kp, preferred_element_type=F32)
                sc_t = lax.dot_general(kp, q2, _NT, preferred_element_type=F32)
                p_t = jnp.exp(jnp.where(prev_key_rows, NEG, sc_t + biast_ref[j]) - lse_r)
                dp_t = lax.dot_general(vp, do2, _NT, preferred_element_type=F32)
                ds_t = (p_t * (dp_t - dl_r)).astype(BF16)
                dk_pair = jnp.dot(ds_t, q2, preferred_element_type=F32)
                dv_pair = jnp.dot(p_t.astype(BF16), do2, preferred_element_type=F32)
                oq = slice(j * LANES, (j + 1) * LANES)
                ok = slice(D_MODEL + j * LANES, D_MODEL + (j + 1) * LANES)
                ov = slice(2 * D_MODEL + j * LANES, 2 * D_MODEL + (j + 1) * LANES)
                out_ref[:, oq] = cq_ref[:, cs].astype(BF16)
                out_ref[:, ok] = (ck_ref[:, cs] + dk_pair[:SPAN]).astype(BF16)
                out_ref[:, ov] = (cv_ref[:, cs] + dv_pair[:SPAN]).astype(BF16)
                cq_ref[:, cs] = jnp.where(lo_mask, dq2[:SPAN], dq2[SPAN:]) * (B_HEAD_DIM**-0.5)
                ck_ref[:, cs] = dk_pair[SPAN:]
                cv_ref[:, cs] = dv_pair[SPAN:]

    def block_of(g):
        return jnp.minimum(g, last)

    def row_spec(width):
        return pl.BlockSpec((SPAN, width), lambda g: (block_of(g), 0))

    return pl.pallas_call(
        body,
        name=name,
        grid=(n_blocks + 1,),
        in_specs=[
            *_qkv_specs(block_of),
            row_spec(D_MODEL),
            row_spec(LANES),
            row_spec(LANES),
            pl.BlockSpec((B_HEADS // 2, 2 * SPAN, 2 * SPAN), lambda g: (0, 0, 0)),
            pl.BlockSpec((B_HEADS // 2, 2 * SPAN, 2 * SPAN), lambda g: (0, 0, 0)),
        ],
        out_specs=pl.BlockSpec((SPAN, 3 * D_MODEL), lambda g: (jnp.maximum(g - 1, 0), 0)),
        out_shape=jax.ShapeDtypeStruct((SEQ, 3 * D_MODEL), BF16),
        scratch_shapes=[pltpu.VMEM((SPAN, D_MODEL), F32)] * 3,
        compiler_params=_params(("arbitrary",)),
    )(qkv_p, qkv_p, qkv_p, qkv_p, qkv_p, do_p, lse_p, delta_p, bias, bias_t)


def _position():
    x, y, c = lax.axis_index("x"), lax.axis_index("y"), lax.axis_index("c")
    return x, y, c, 4 * x + 2 * y + c


def _peer(k, x, y, c):
    px = 1 - x if k & 4 else x
    py = 1 - y if k & 2 else y
    pc = 1 - c if k & 1 else c
    return (px, py, pc), 4 * px + 2 * py + pc


def _remote(src, dst, send_sem, recv_sem, device):
    return pltpu.make_async_remote_copy(
        src_ref=src, dst_ref=dst, send_sem=send_sem, recv_sem=recv_sem, device_id=device, device_id_type=MESH
    )


def _silu_bf16(cf):
    return (cf * (1.0 / (1.0 + jnp.exp(-cf)))).astype(BF16)


def ada_exchange(c8, w4, b4, ln8):
    nt, _, ncol = w4.shape

    def body(c8_ref, w_ref, b_ref, ln_ref, cg_ref, lng_ref, mrecv_ref, mloc_ref, send_sems, recv_sems):
        x, y, c, me = _position()
        cg_ref[me] = c8_ref[...]
        lng_ref[me] = ln_ref[...]
        first = []
        for k in range(1, N_DEV):
            dev, _ = _peer(k, x, y, c)
            first.append(_remote(c8_ref, cg_ref.at[me], send_sems.at[0, k], recv_sems.at[0, k], dev))
            first.append(_remote(ln_ref, lng_ref.at[me], send_sems.at[1, k], recv_sems.at[1, k], dev))
        for cp in first:
            cp.start()
        for k in range(1, N_DEV):
            dev, pid = _peer(k, x, y, c)
            _remote(c8_ref, cg_ref.at[pid], send_sems.at[0, k], recv_sems.at[0, k], dev).wait_recv()
            _remote(ln_ref, lng_ref.at[pid], send_sems.at[1, k], recv_sems.at[1, k], dev).wait_recv()
        sc = _silu_bf16(cg_ref[...].reshape(N_DEV * SUBLANES, D_MODEL))
        for t in range(nt):
            mloc_ref[t] = jnp.dot(sc, w_ref[t].astype(BF16), preferred_element_type=F32) + b_ref[t : t + 1, :]

        def group(dev_id):
            return pl.ds(pl.multiple_of(dev_id * SUBLANES, SUBLANES), SUBLANES)

        mrecv_ref[me] = mloc_ref[:, group(me), :]
        second = []
        for k in range(1, N_DEV):
            dev, pid = _peer(k, x, y, c)
            second.append(
                _remote(mloc_ref.at[:, group(pid), :], mrecv_ref.at[me], send_sems.at[2, k], recv_sems.at[2, k], dev)
            )
        for cp in second:
            cp.start()
        for k in range(1, N_DEV):
            dev, pid = _peer(k, x, y, c)
            _remote(
                mloc_ref.at[:, group(pid), :], mrecv_ref.at[pid], send_sems.at[2, k], recv_sems.at[2, k], dev
            ).wait_recv()
        for cp in first + second:
            cp.wait_send()

    return pl.pallas_call(
        body,
        name="ada_exchange",
        in_specs=[VMEM, VMEM, VMEM, VMEM],
        out_specs=[VMEM, VMEM, VMEM],
        out_shape=[
            jax.ShapeDtypeStruct((N_DEV, SUBLANES, D_MODEL), F32),
            jax.ShapeDtypeStruct((N_DEV, SUBLANES, LANES), F32),
            jax.ShapeDtypeStruct((N_DEV, nt, SUBLANES, ncol), F32),
        ],
        scratch_shapes=[
            pltpu.VMEM((nt, N_DEV * SUBLANES, ncol), F32),
            pltpu.SemaphoreType.DMA((3, N_DEV)),
            pltpu.SemaphoreType.DMA((3, N_DEV)),
        ],
        compiler_params=pltpu.CompilerParams(vmem_limit_bytes=VMEM_LIMIT_BYTES),
    )(c8, w4, b4, ln8)


def small_exchange(dmx, flat, pre, after):
    rows = flat.shape[1]

    def body(dmx_ref, flat_ref, pre_ref, *rest):
        dmrecv_ref, red_ref, land_ref, send_sems, recv_sems = rest[len(after) :]
        x, y, c, me = _position()
        dmrecv_ref[me] = dmx_ref[me]
        land_ref[me] = flat_ref[me]
        first = []
        for k in range(1, N_DEV):
            dev, pid = _peer(k, x, y, c)
            first.append(_remote(dmx_ref.at[pid], dmrecv_ref.at[me], send_sems.at[0, k], recv_sems.at[0, k], dev))
            first.append(_remote(flat_ref.at[pid], land_ref.at[me], send_sems.at[1, k], recv_sems.at[1, k], dev))
        for cp in first:
            cp.start()
        for k in range(1, N_DEV):
            dev, pid = _peer(k, x, y, c)
            _remote(dmx_ref.at[pid], dmrecv_ref.at[pid], send_sems.at[0, k], recv_sems.at[0, k], dev).wait_recv()
            _remote(flat_ref.at[pid], land_ref.at[pid], send_sems.at[1, k], recv_sems.at[1, k], dev).wait_recv()
        total, total_pre = land_ref[0], pre_ref[0]
        for s in range(1, N_DEV):
            total, total_pre = total + land_ref[s], total_pre + pre_ref[s]
        red_ref[me, :rows, :] = total
        red_ref[me, rows:, :] = total_pre
        second = []
        for k in range(1, N_DEV):
            dev, _ = _peer(k, x, y, c)
            second.append(_remote(red_ref.at[me], red_ref.at[me], send_sems.at[2, k], recv_sems.at[2, k], dev))
        for cp in second:
            cp.start()
        for k in range(1, N_DEV):
            dev, pid = _peer(k, x, y, c)
            _remote(red_ref.at[pid], red_ref.at[pid], send_sems.at[2, k], recv_sems.at[2, k], dev).wait_recv()
        for cp in first + second:
            cp.wait_send()

    return pl.pallas_call(
        body,
        name="small_exchange",
        in_specs=[VMEM, VMEM, VMEM, *[ANY for _ in after]],
        out_specs=[VMEM, VMEM],
        out_shape=[
            jax.ShapeDtypeStruct(dmx.shape, F32),
            jax.ShapeDtypeStruct((N_DEV, rows + pre.shape[1], LANES), F32),
        ],
        scratch_shapes=[
            pltpu.VMEM(flat.shape, F32),
            pltpu.SemaphoreType.DMA((3, N_DEV)),
            pltpu.SemaphoreType.DMA((3, N_DEV)),
        ],
        compiler_params=pltpu.CompilerParams(vmem_limit_bytes=VMEM_LIMIT_BYTES),
    )(dmx, flat, pre, *after)


HBM = pl.BlockSpec(memory_space=pltpu.HBM)
SEM = pl.BlockSpec(memory_space=pltpu.SEMAPHORE)
EFFECT = pltpu.SideEffectType.DATAFLOW_SIDE_EFFECTING


REGROUP_ROWS = 256


def shards_to_columns(x, *, name):
    p, k, n = x.shape

    def body(x_ref, o_ref):
        for s in range(p):
            o_ref[:, s * n : (s + 1) * n] = x_ref[s]

    return pl.pallas_call(
        body,
        name=name,
        grid=(k // REGROUP_ROWS,),
        in_specs=[pl.BlockSpec((p, REGROUP_ROWS, n), lambda i: (0, i, 0))],
        out_specs=pl.BlockSpec((REGROUP_ROWS, p * n), lambda i: (i, 0)),
        out_shape=jax.ShapeDtypeStruct((k, p * n), x.dtype),
        compiler_params=_params(("parallel",)),
    )(x)


def columns_to_shards(xs, *, name):
    k = xs[0].shape[0]
    widths = [x.shape[1] for x in xs]
    n = sum(widths) // N_DEV
    pieces = []
    for s in range(N_DEV):
        start = 0
        for i, w in enumerate(widths):
            lo, hi = max(start, s * n), min(start + w, (s + 1) * n)
            if lo < hi:
                pieces.append((i, lo - start, s, lo - s * n, hi - lo))
            start += w

    def body(*refs):
        x_refs, o_ref = refs[: len(xs)], refs[-1]
        for i, c0, s, d0, w in pieces:
            o_ref[s, :, d0 : d0 + w] = x_refs[i][:, c0 : c0 + w]

    return pl.pallas_call(
        body,
        name=name,
        grid=(k // REGROUP_ROWS,),
        in_specs=[pl.BlockSpec((REGROUP_ROWS, w), lambda i: (i, 0)) for w in widths],
        out_specs=pl.BlockSpec((N_DEV, REGROUP_ROWS, n), lambda i: (0, i, 0)),
        out_shape=jax.ShapeDtypeStruct((N_DEV, k, n), xs[0].dtype),
        compiler_params=_params(("parallel",)),
    )(*xs)


def _own_slot(me, block):
    land = lax.empty((N_DEV, *block.shape), block.dtype)
    return lax.dynamic_update_slice_in_dim(land, block[None], me, axis=0)


N_CHIP_PEERS = 3


class Gather:
    def __init__(self, shards, lands, after, *, name):
        nt = len(shards)
        self.name = name

        def body(*refs):
            src_refs, land_refs = refs[:nt], refs[nt : 2 * nt]
            send_sems, recv_sems = refs[2 * nt + 1 : 3 * nt + 1], refs[3 * nt + 1 : 4 * nt + 1]
            token = refs[-1]
            x, y, c, me = _position()
            for t in range(nt):
                for k, dev in enumerate(self._targets(x, y, c)):
                    _remote(src_refs[t], land_refs[t].at[me], send_sems[t].at[k], recv_sems[t].at[k], dev).start()
            token[...] = jnp.zeros_like(token)

        outs = pl.pallas_call(
            body,
            name=name + "_start",
            in_specs=[HBM] * (2 * nt) + [ANY],
            out_specs=[SEM] * (2 * nt) + [HBM] * (2 * nt) + [VMEM],
            out_shape=[pltpu.SemaphoreType.DMA((1 + N_CHIP_PEERS,))] * (2 * nt)
            + [pltpu.HBM(a.shape, a.dtype) for a in (*shards, *lands)]
            + [jax.ShapeDtypeStruct((SUBLANES, LANES), F32)],
            input_output_aliases={i: 2 * nt + i for i in range(2 * nt)},
            compiler_params=pltpu.CompilerParams(has_side_effects=EFFECT),
        )(*[pltpu.with_memory_space_constraint(a, pltpu.HBM) for a in (*shards, *lands)], after)
        self.send_sems, self.recv_sems = list(outs[:nt]), list(outs[nt : 2 * nt])
        self.srcs, self.lands = list(outs[2 * nt : 3 * nt]), list(outs[3 * nt : 4 * nt])
        self.token = outs[-1]

    @staticmethod
    def _chips(x, y):
        return [(1 - x, y), (x, 1 - y), (1 - x, 1 - y)]

    @classmethod
    def _targets(cls, x, y, c):
        return [(x, y, 1 - c)] + [(*chip, c) for chip in cls._chips(x, y)]

    def zero(self):
        return self.token[0, 0]

    @staticmethod
    def _slot(px, py, pc):
        return 4 * px + 2 * py + pc

    def pass_on(self, which, after, *, name):
        n = len(which)

        def pass_body(*refs):
            land_refs, recv_sems = refs[:n], refs[n : 2 * n]
            fwd_send, fwd_recv = refs[3 * n + 1 : 4 * n + 1], refs[4 * n + 1 : 5 * n + 1]
            token = refs[-1]
            x, y, c, _ = _position()
            for t in range(n):
                for j, chip in enumerate(self._chips(x, y)):
                    blk = land_refs[t].at[self._slot(*chip, c)]
                    _remote(blk, blk, fwd_send[t].at[j], recv_sems[t].at[1 + j], (*chip, c)).wait_recv()
                    _remote(blk, blk, fwd_send[t].at[j], fwd_recv[t].at[j], (x, y, 1 - c)).start()
            token[...] = jnp.zeros_like(token)

        lands = [self.lands[t] for t in which]
        outs = pl.pallas_call(
            pass_body,
            name=name,
            in_specs=[HBM] * n + [SEM] * n + [ANY],
            out_specs=[HBM] * n + [SEM] * (2 * n) + [VMEM],
            out_shape=[pltpu.HBM(a.shape, a.dtype) for a in lands]
            + [pltpu.SemaphoreType.DMA((N_CHIP_PEERS,))] * (2 * n)
            + [jax.ShapeDtypeStruct((SUBLANES, LANES), F32)],
            input_output_aliases={i: i for i in range(n)},
            compiler_params=pltpu.CompilerParams(has_side_effects=EFFECT),
        )(*lands, *[self.recv_sems[t] for t in which], after)
        return (which, outs[:n], outs[n : 2 * n], outs[2 * n : 3 * n]), outs[-1]

    def wait(self, which, after, *, name):
        return self.finish(self.pass_on(which, after, name=name + "_pass")[0], after, name=name)

    def finish(self, passed, after, *, name):
        which, lands, fwd_send, fwd_recv = passed
        n = len(which)
        slot = self._slot

        def wait_body(*refs):
            src_refs, land_refs = refs[:n], refs[n : 2 * n]
            send_sems, recv_sems = refs[2 * n : 3 * n], refs[3 * n : 4 * n]
            fwd_send, fwd_recv = refs[4 * n : 5 * n], refs[5 * n : 6 * n]
            x, y, c, me = _position()
            sibling = (x, y, 1 - c)
            for t in range(n):
                for k, dev in enumerate(self._targets(x, y, c)):
                    _remote(src_refs[t], land_refs[t].at[me], send_sems[t].at[k], recv_sems[t].at[k], dev).wait_send()
                blk = land_refs[t].at[slot(x, y, 1 - c)]
                _remote(blk, blk, send_sems[t].at[0], recv_sems[t].at[0], sibling).wait_recv()
                for j, chip in enumerate(self._chips(x, y)):
                    sent = land_refs[t].at[slot(*chip, c)]
                    _remote(sent, sent, fwd_send[t].at[j], fwd_recv[t].at[j], sibling).wait_send()
                    got = land_refs[t].at[slot(*chip, 1 - c)]
                    _remote(got, got, fwd_send[t].at[j], fwd_recv[t].at[j], sibling).wait_recv()

        srcs = [self.srcs[t] for t in which]
        outs = pl.pallas_call(
            wait_body,
            name=name,
            in_specs=[HBM] * (2 * n) + [SEM] * (4 * n) + [ANY],
            out_specs=[HBM] * (2 * n),
            out_shape=[pltpu.HBM(a.shape, a.dtype) for a in (*srcs, *lands)],
            input_output_aliases={i: i for i in range(2 * n)},
            compiler_params=pltpu.CompilerParams(has_side_effects=EFFECT),
        )(
            *srcs, *lands, *[self.send_sems[t] for t in which], *[self.recv_sems[t] for t in which], *fwd_send,
            *fwd_recv, after,
        )
        return outs[n:]


class Scatter:
    def __init__(self, srcs, after, *, name):
        self.name = name
        nt = self.nt = len(srcs)
        peers = N_DEV - 1
        lands = [lax.empty(a.shape, a.dtype) for a in srcs]

        def body(*refs):
            src_refs, land_refs = refs[:nt], refs[nt : 2 * nt]
            send_sems, recv_sems = refs[2 * nt + 1 : 3 * nt + 1], refs[3 * nt + 1 : 4 * nt + 1]
            token = refs[-1]
            x, y, c, me = _position()
            for t in range(nt):
                for k in range(1, N_DEV):
                    dev, pid = _peer(k, x, y, c)
                    src = src_refs[t].at[pid]
                    _remote(src, land_refs[t].at[me], send_sems[t].at[k - 1], recv_sems[t].at[k - 1], dev).start()
                self._own(src_refs[t], land_refs[t], send_sems[t], me).start()
            token[...] = jnp.zeros_like(token)

        outs = pl.pallas_call(
            body,
            name=name + "_start",
            in_specs=[HBM] * (2 * nt) + [ANY],
            out_specs=[SEM] * (2 * nt) + [HBM] * (2 * nt) + [VMEM],
            out_shape=[pltpu.SemaphoreType.DMA((peers + 1,))] * (2 * nt)
            + [pltpu.HBM(a.shape, a.dtype) for a in (*srcs, *lands)]
            + [jax.ShapeDtypeStruct((SUBLANES, LANES), F32)],
            input_output_aliases={i: 2 * nt + i for i in range(2 * nt)},
            compiler_params=pltpu.CompilerParams(has_side_effects=EFFECT),
        )(*[pltpu.with_memory_space_constraint(a, pltpu.HBM) for a in (*srcs, *lands)], after)
        self.send_sems, self.recv_sems = outs[:nt], outs[nt : 2 * nt]
        self.srcs, self.lands = outs[2 * nt : 3 * nt], outs[3 * nt : 4 * nt]
        self.token = outs[-1]

    @staticmethod
    def _own(src_ref, land_ref, sems, me):
        return pltpu.make_async_copy(src_ref.at[me], land_ref.at[me], sems.at[N_DEV - 1])

    def zero(self):
        return self.token[0, 0]

    def wait(self, which, after, *, name):
        n = len(which)

        def body(*refs):
            src_refs, land_refs = refs[:n], refs[n : 2 * n]
            send_sems, recv_sems = refs[2 * n : 3 * n], refs[3 * n : 4 * n]
            x, y, c, me = _position()
            for t in range(n):
                for k in range(1, N_DEV):
                    dev, pid = _peer(k, x, y, c)
                    src = src_refs[t].at[pid]
                    cp = _remote(src, land_refs[t].at[pid], send_sems[t].at[k - 1], recv_sems[t].at[k - 1], dev)
                    cp.wait_send()
                    cp.wait_recv()
                self._own(src_refs[t], land_refs[t], send_sems[t], me).wait()

        srcs = [self.srcs[t] for t in which]
        lands = [self.lands[t] for t in which]
        outs = pl.pallas_call(
            body,
            name=name,
            in_specs=[HBM] * (2 * n) + [SEM] * (2 * n) + [ANY],
            out_specs=[HBM] * (2 * n),
            out_shape=[pltpu.HBM(a.shape, a.dtype) for a in (*srcs, *lands)],
            input_output_aliases={i: i for i in range(2 * n)},
            compiler_params=pltpu.CompilerParams(has_side_effects=EFFECT),
        )(*srcs, *lands, *[self.send_sems[t] for t in which], *[self.recv_sems[t] for t in which], after)
        return outs[n:]


def _adam_update(g, w, m, v):
    m2 = ADAM_B1 * m + (1.0 - ADAM_B1) * g
    v2 = ADAM_B2 * v + (1.0 - ADAM_B2) * jnp.square(g)
    m_hat = m2 / (1.0 - ADAM_B1**ADAM_STEP)
    v_hat = v2 / (1.0 - ADAM_B2**ADAM_STEP)
    delta = -ADAM_LR * (m_hat / (jnp.sqrt(v_hat) + ADAM_EPS) + ADAM_WD * w)
    return delta, m2, v2


def adamw(gparts, w, m, v, *, name):
    nl, r, c = w.shape
    p = gparts[0].shape[0]
    tr = r if r <= 256 else (256 if c <= D_MODEL else 128)
    ni = r // tr

    def body(*refs):
        g_refs = refs[:nl]
        w_ref, m_ref, v_ref, go_ref, d_ref, mo_ref, vo_ref = refs[nl:]
        for layer in range(nl):

            @pl.when(pl.program_id(0) == layer)
            def _(g_ref=g_refs[layer]):
                g = g_ref[0].astype(F32)
                for i in range(1, p):
                    g = g + g_ref[i].astype(F32)
                delta, m2, v2 = _adam_update(g, w_ref[...], m_ref[...], v_ref[...])
                go_ref[...] = g
                d_ref[...] = delta
                mo_ref[...] = m2
                vo_ref[...] = v2

    def parts_spec(layer):
        def index(l, i):
            return (0, jnp.where(l == layer, i, jnp.where(l < layer, 0, ni - 1)), 0)

        return pl.BlockSpec((p, tr, c), index)

    blk = pl.BlockSpec((None, tr, c), lambda l, i: (l, i, 0))
    return pl.pallas_call(
        body,
        name=name,
        grid=(nl, ni),
        in_specs=[*[parts_spec(layer) for layer in range(nl)], blk, blk, blk],
        out_specs=[blk] * 4,
        out_shape=[jax.ShapeDtypeStruct((nl, r, c), F32)] * 4,
        compiler_params=_params(("arbitrary", "arbitrary")),
    )(*gparts, w, m, v)


def adamw_small(items, *, name):
    n = len(items)

    def body(*refs):
        ins, outs = refs[: 4 * n], refs[4 * n :]
        for t in range(n):
            g_ref, w_ref, m_ref, v_ref = ins[4 * t : 4 * t + 4]
            g = g_ref[...]
            delta, m2, v2 = _adam_update(g, w_ref[...], m_ref[...], v_ref[...])
            for o_ref, val in zip(outs[4 * t : 4 * t + 4], (g, delta, m2, v2), strict=True):
                o_ref[...] = val

    outs = pl.pallas_call(
        body,
        name=name,
        out_shape=[jax.ShapeDtypeStruct(item[1].shape, F32) for item in items for _ in range(4)],
    )(*[a for item in items for a in item])
    return [outs[4 * t : 4 * t + 4] for t in range(n)]


def ada_grad_adamw(cg, dmrecv, w4, m4, v4, *, name):
    nt, k, ncol = w4.shape

    def body(cg_ref, dm_ref, w_ref, m_ref, v_ref, go_ref, d_ref, mo_ref, vo_ref, gb_ref):
        sc = _silu_bf16(cg_ref[...].reshape(N_DEV * SUBLANES, k))
        dm = dm_ref[...].reshape(N_DEV * SUBLANES, ncol)
        g = lax.dot_general(sc, dm.astype(BF16), (((0,), (0,)), ((), ())), preferred_element_type=F32)
        delta, m2, v2 = _adam_update(g, w_ref[...], m_ref[...], v_ref[...])
        go_ref[...] = g
        d_ref[...] = delta
        mo_ref[...] = m2
        vo_ref[...] = v2
        gb_ref[...] = jnp.broadcast_to(jnp.sum(dm, axis=0, keepdims=True), (SUBLANES, ncol))

    wblk = pl.BlockSpec((None, k, ncol), lambda t: (t, 0, 0))
    return pl.pallas_call(
        body,
        name=name,
        grid=(nt,),
        in_specs=[
            pl.BlockSpec((N_DEV, SUBLANES, k), lambda t: (0, 0, 0)),
            pl.BlockSpec((N_DEV, None, SUBLANES, ncol), lambda t: (0, t, 0, 0)),
            wblk,
            wblk,
            wblk,
        ],
        out_specs=[wblk] * 4 + [pl.BlockSpec((None, SUBLANES, ncol), lambda t: (t, 0, 0))],
        out_shape=[jax.ShapeDtypeStruct((nt, k, ncol), F32)] * 4 + [jax.ShapeDtypeStruct((nt, SUBLANES, ncol), F32)],
        compiler_params=_params(("parallel",)),
    )(cg, dmrecv, w4, m4, v4)


def kernel(x, c, ada_w, ada_b, ln_g, ln_b, a_w_in, a_b_in, a_vn_g, a_vn_b, a_w_s, a_b_s, a_w_out, b_w_qkv, b_w_out, mlp_w_up, mlp_w_down, loss_target, m_ada_w, m_ada_b, m_ln_g, m_ln_b, m_a_w_in, m_a_b_in, m_a_vn_g, m_a_vn_b, m_a_w_s, m_a_b_s, m_a_w_out, m_b_w_qkv, m_b_w_out, m_mlp_w_up, m_mlp_w_down, v_ada_w, v_ada_b, v_ln_g, v_ln_b, v_a_w_in, v_a_b_in, v_a_vn_g, v_a_vn_b, v_a_w_s, v_a_b_s, v_a_w_out, v_b_w_qkv, v_b_w_out, v_mlp_w_up, v_mlp_w_down):
    x0 = x[0]
    target = loss_target[0]
    me = 4 * lax.axis_index("x") + 2 * lax.axis_index("y") + lax.axis_index("c")

    ada_w4 = ada_w.reshape(N_SUB, D_MODEL, -1)
    ada_b4 = ada_b.reshape(N_SUB, -1)
    ln8 = jnp.concatenate([ln_g.reshape(N_SUB, -1), ln_b.reshape(N_SUB, -1)], axis=0)
    c8 = jnp.broadcast_to(c, (SUBLANES, D_MODEL))
    cg, lng, mrecv = ada_exchange(c8, ada_w4, ada_b4, ln8)

    W_IN, W_AOUT, W_UP0, W_DN0, W_QKV, W_BOUT, W_UP1, W_DN1 = range(8)
    shards = [
        a_w_in[0].astype(BF16),
        a_w_out[0].astype(BF16),
        mlp_w_up[0].astype(BF16),
        mlp_w_down[0].astype(BF16),
        b_w_qkv[0].astype(BF16),
        b_w_out[0].astype(BF16),
        mlp_w_up[1].astype(BF16),
        mlp_w_down[1].astype(BF16),
    ]
    gather = Gather(shards, [_own_slot(me, s) for s in shards], mrecv, name="gather")

    modv = mrecv[:, :, 0, :].transpose(1, 0, 2).reshape(N_SUB, 3 * D_MODEL) + gather.zero()
    shift = [modv[t : t + 1, :D_MODEL] for t in range(N_SUB)]
    scale = [modv[t : t + 1, D_MODEL : 2 * D_MODEL] for t in range(N_SUB)]
    gate1 = [1.0 + modv[t : t + 1, 2 * D_MODEL :] for t in range(N_SUB)]
    lng_full = [lng[:, t, :].reshape(1, D_MODEL) for t in range(N_SUB)]
    lnb_full = [lng[:, N_SUB + t, :].reshape(1, D_MODEL) for t in range(N_SUB)]

    ident = lambda acc: (acc,)
    def relu2(a):
        r = jnp.maximum(a, jnp.zeros_like(a))
        return r * r
    vn_g, vn_b, w_s = a_vn_g, a_vn_b, a_w_s[0]
    bias_full = jnp.repeat(a_b_s[0].T, A_GROUP_DIM, axis=1)
    w_up3, w_dn3 = [None, None], [None, None]

    def mlp_forward(i, h, up, dn, x_in=None, t_next=None):
        w_up3[i], w_dn3[i] = up, dn.reshape(1, D_FF, D_MODEL)
        (a,) = mm_nn(h, w_up3[i], name=f"mlp{i}_up", tm=2048, ps=2, tn=512, tk=D_MODEL, epilogue=ident, outs=(BF16,))
        if t_next is None:
            (y,) = mm_nn(
                a, w_dn3[i], name=f"mlp{i}_down", tm=1024, ps=1, tn=512, tk=D_FF, prologue=relu2, epilogue=ident, outs=(BF16,)
            )
            return a, y
        y, xn, hn = mm_nn(
            a, w_dn3[i], name=f"mlp{i}_down", tm=512, ps=1, tn=D_MODEL, tk=D_FF, prologue=relu2,
            epilogue=residual_ln_epilogue, extras=residual_extras(x_in, t_next - 1), outs=(BF16, F32, BF16),
        )
        return a, y, xn, hn

    def residual_extras(x_in, t):
        rows = (gate1[t], lng_full[t], lnb_full[t], scale[t + 1], shift[t + 1])
        return [(x_in, "full")] + [(r, "row") for r in rows]

    h0 = modulate(x0, scale[0], shift[0], name="modulate0")
    w_in3, w_aout3 = gather.wait([W_IN, W_AOUT], h0, name="gather_wait_a")
    w_aout3 = w_aout3.reshape(1, D_MODEL, D_MODEL)
    (a_pre,) = mm_nn(
        h0, w_in3, name="a_in", tm=2048, ps=4, tn=256, tk=D_MODEL, epilogue=lambda acc, b: (acc + b,),
        extras=[(a_b_in, "row")], outs=(BF16,),
    )
    p_gate = gate_fwd(a_pre, vn_g, vn_b, w_s, bias_full, name="gate_fwd")
    y0, x1, h1 = mm_nn(
        p_gate, w_aout3, name="a_out", tm=1024, ps=1, tn=D_MODEL, tk=D_MODEL, epilogue=residual_ln_epilogue,
        extras=residual_extras(x0, 0), outs=(BF16, F32, BF16),
    )
    w_mlp0 = gather.wait([W_UP0, W_DN0], y0, name="gather_wait_mlp0")
    a1, y1, x2, h2 = mlp_forward(0, h1, *w_mlp0, x_in=x1, t_next=2)
    (w_qkv_shards,) = gather.wait([W_QKV], y1, name="gather_wait_qkv")
    w_qkv3 = shards_to_columns(w_qkv_shards, name="w_qkv_columns")[None]
    pat_tiles = 3
    dil = [d for _, d in B_PATTERNS]
    qkv_p, pat_o, pat_lse = [], [], []
    for g in range(N_PAT):
        (qkv_g,) = mm_nn(
            h2, w_qkv3, name=f"b_qkv{g}", tm=2048, ps=1, tn=D_MODEL, tk=D_MODEL, epilogue=ident, outs=(BF16,),
            b_tile0=pat_tiles * g, b_tiles=pat_tiles, out_streams=dil[g],
        )
        o_g, lse_g = attn_fwd(qkv_g, g, name=f"attn_fwd{g}", after=rest_token if g == 1 else None)
        if g == 0:
            rest_passed, rest_token = gather.pass_on([W_BOUT, W_UP1, W_DN1], lse_g, name="gather_pass_rest")
        qkv_p.append(qkv_g)
        pat_o.append(o_g)
        pat_lse.append(lse_g)
    o_b, o_f, *lse_pat = attn_combine(pat_o, pat_lse, name="attn_combine")
    lse_pat = [x.reshape(SEQ, LANES) for x in lse_pat]
    w_bout3, *w_mlp1 = gather.finish(rest_passed, lse_pat[0], name="gather_wait_rest")
    w_bout3 = w_bout3.reshape(1, D_MODEL, D_MODEL)
    y2, x3, h3 = mm_nn(
        o_b, w_bout3, name="b_out", tm=1024, ps=1, tn=D_MODEL, tk=D_MODEL, epilogue=residual_ln_epilogue,
        extras=residual_extras(x2, 2), outs=(BF16, F32, BF16),
    )
    a3, y3 = mlp_forward(1, h3, *w_mlp1)
    dz3, dy3, st3, loss_local = residual_ln_loss_bwd(
        x3, y3, gate1[3], lng_full[3], lnb_full[3], target, name="res_ln3_loss_bwd"
    )

    def scatter(parts, after, name):
        return Scatter([p.reshape(N_DEV, -1, p.shape[-1]) for p in parts], after, name=name)

    xs_in, ys = [x0, x1, x2, x3], [y0, y1, y2, y3]

    def residual_bwd_extras(dz_later, t):
        return [
            (dz_later, "full"), (scale[t + 1], "row"), (xs_in[t], "full"), (ys[t], "full"),
            (gate1[t], "row"), (lng_full[t], "row"), (lnb_full[t], "row"),
        ]

    def mlp_backward(i, h, a, dy, dz_later, t):
        (da,) = mm_nt(
            dy,
            w_dn3[i],
            name=f"mlp{i}_da",
            tm=2048,
            tko=1024,
            ps=1,
            tc=D_MODEL,
            epilogue=lambda acc, act: (acc * (2.0 * jnp.maximum(act.astype(F32), 0.0)),),
            extras=[(a, "full")],
            outs=(BF16,),
        )
        dw_dn = mm_tn(
            a, dy, name=f"mlp{i}_dw_down", p=1, tk=1024, ps=1, tn=D_MODEL, tmc=2048, prologue=relu2, out_dtype=BF16
        )
        dw_up = mm_tn(h, da, name=f"mlp{i}_dw_up", p=N_DEV, tk=1024, ps=2, tn=512, tmc=2048, out_dtype=BF16)
        rs = scatter([dw_up, dw_dn], da, f"scatter_mlp{i}")
        dz, dy_before, st = mm_nt(
            da, w_up3[i], name=f"mlp{i}_dh", tm=512, tko=D_MODEL, ps=N_DEV, tc=512, epilogue=residual_bwd_epilogue,
            extras=residual_bwd_extras(dz_later, t), outs=(F32, BF16), stats=True, after=rs.token,
        )
        return rs, dz, dy_before, st

    rs_mlp1, dz2, dy2, st2 = mlp_backward(1, h3, a3, dy3, dz3, 2)
    do_delta = mm_nt(
        dy2, w_bout3, name="b_do", tm=1024, tko=D_MODEL, ps=1, tc=D_MODEL, extras=[(o_f, "full")],
        epilogue=lambda acc, o: (acc,) * N_PAT + (_reduce_groups(acc * o),) * N_PAT,
        outs=(BF16,) * N_PAT + (F32,) * N_PAT, out_widths=[D_MODEL] * N_PAT + [LANES] * N_PAT, out_streams=dil + dil,
    )
    do_pat = [x.reshape(SEQ, D_MODEL) for x in do_delta[:N_PAT]]
    delta_pat = [x.reshape(SEQ, LANES) for x in do_delta[N_PAT:]]
    dh2, dw_pat = None, []
    for g in range(N_PAT):
        dqkv_g = attn_bwd(qkv_p[g], do_pat[g], lse_pat[g], delta_pat[g], g, name=f"attn_bwd{g}")
        dw_pat.append(
            mm_tn(
                h2, dqkv_g, name=f"b_dw_qkv{g}", p=1, tk=1024, ps=1, tn=D_MODEL, tmc=2048, out_dtype=BF16,
                g_streams=dil[g],
            )[0]
        )
        if g == N_PAT - 1:
            break
        (dh2,) = mm_nt(
            dqkv_g, w_qkv3, name=f"b_dh{g}", tm=2048, tko=512, ps=1, tc=pat_tiles * D_MODEL, outs=(F32,),
            b_tile0=g, g_streams=dil[g],
            epilogue=ident if g == 0 else (lambda acc, prev, d=dil[g]: (_from_streams(acc, d) + prev,)),
            extras=[] if g == 0 else [(dh2, "full")],
        )
    dw_bout = mm_tn(o_b, dy2, name="b_dw_out", p=1, tk=1024, ps=1, tn=D_MODEL, tmc=2048, out_dtype=BF16)
    dw_qkv = columns_to_shards(dw_pat, name="dw_qkv_shards")
    rs_b = scatter([dw_qkv, dw_bout], dqkv_g, "scatter_b")
    dz1, dy1, st1 = mm_nt(
        dqkv_g, w_qkv3, name=f"b_dh{N_PAT - 1}", tm=512, tko=D_MODEL, ps=1, tc=pat_tiles * D_MODEL, outs=(F32, BF16),
        b_tile0=N_PAT - 1, g_streams=dil[-1], stats=True,
        epilogue=lambda acc, prev, *rest: residual_bwd_epilogue(_from_streams(acc, dil[-1]) + prev, *rest),
        extras=[(dh2, "full")] + residual_bwd_extras(dz2, 1), after=rs_b.token,
    )
    rs_mlp0, dz0, dy0, st0 = mlp_backward(0, h1, a1, dy1, dz1, 0)
    (dp_gate,) = mm_nt(dy0, w_aout3, name="a_dp", tm=2048, tko=1024, ps=1, tc=D_MODEL, epilogue=ident, outs=(F32,))
    dw_aout = mm_tn(p_gate, dy0, name="a_dw_out", p=1, tk=1024, ps=1, tn=D_MODEL, tmc=2048, out_dtype=BF16)
    rs_aout = scatter([dw_aout], dp_gate, "scatter_a_out")
    da0, d_ws, d_bs, gate_rows = gate_bwd(a_pre, dp_gate, vn_g + rs_aout.zero(), vn_b, w_s, bias_full, name="gate_bwd")
    rs_ws = scatter([d_ws], gate_rows, "scatter_a_w_s")
    dw_in = mm_tn(h0, da0, name="a_dw_in", p=N_DEV, tk=1024, ps=4, tn=256, tmc=2048, out_dtype=BF16)
    rs_in = scatter([dw_in], rs_ws.token, "scatter_a_in")
    grad_x, stf = mm_nt(
        da0, w_in3, name="a_dh", tm=1024, tko=D_MODEL, ps=N_DEV, tc=256, epilogue=input_bwd_epilogue,
        extras=[(dz0, "full"), (scale[0], "row"), (x0, "full")], outs=(F32,), stats=True, after=rs_in.token,
    )

    results = {}

    def update(wname, gparts, w, m, v):
        shape = w.shape
        layers = len(gparts) if isinstance(gparts, list) else 1
        w3 = w.reshape(layers, -1, shape[-1])
        parts = [g.reshape(g.shape[0], *w3.shape[1:]) for g in (gparts if layers > 1 else [gparts])]
        outs = adamw(parts, w3, m.reshape(w3.shape), v.reshape(w3.shape), name=f"adamw_{wname}")
        results[wname] = [o.reshape(shape) for o in outs]

    g_up1, g_dn1 = rs_mlp1.wait([0, 1], grad_x, name="scatter_wait_mlp1")
    g_qkv, g_bout = rs_b.wait([0, 1], grad_x, name="scatter_wait_b")
    update("b_w_qkv", g_qkv, b_w_qkv, m_b_w_qkv, v_b_w_qkv)
    update("b_w_out", g_bout, b_w_out, m_b_w_out, v_b_w_out)
    g_up0, g_dn0 = rs_mlp0.wait([0, 1], grad_x, name="scatter_wait_mlp0")
    update("mlp_w_up", [g_up0, g_up1], mlp_w_up, m_mlp_w_up, v_mlp_w_up)
    update("mlp_w_down", [g_dn0, g_dn1], mlp_w_down, m_mlp_w_down, v_mlp_w_down)

    stats_after = [stf, st0, st1, st2]
    stats_own = [st0, st1, st2, st3]
    dm = jnp.stack(
        [
            jnp.concatenate(
                [stats_after[t][ST_DSHIFT], stats_after[t][ST_DSCALE], stats_own[t][ST_DGATE]], axis=0
            )
            for t in range(N_SUB)
        ]
    )
    ncol = 3 * D_MODEL // N_DEV
    dmx = jnp.pad(
        dm.reshape(N_SUB, N_DEV, ncol).transpose(1, 0, 2)[:, :, None, :], ((0, 0), (0, 0), (0, SUBLANES - 1), (0, 0))
    )
    small = [
        gate_rows[0],
        gate_rows[1],
        d_bs[:, :A_GROUPS].T.reshape(-1),
        *[stats_own[t][ST_DG] for t in range(N_SUB)],
        *[stats_own[t][ST_DB] for t in range(N_SUB)],
        jnp.pad(loss_local.reshape(1), (0, LANES - 1)),
    ]
    n_small = sum(s.size for s in small)
    part_rows = -(-n_small // (N_DEV * LANES * SUBLANES)) * SUBLANES
    flat = jnp.concatenate(small + [jnp.zeros((N_DEV * part_rows * LANES - n_small,), F32)])
    (ws_parts,) = rs_ws.wait([0], results["mlp_w_down"][3], name="scatter_wait_a_w_s")
    early = [results[n][3] for n in ("b_w_qkv", "b_w_out", "mlp_w_up", "mlp_w_down")]
    dmrecv, reduced = small_exchange(dmx, flat.reshape(N_DEV, part_rows, LANES), ws_parts, early)
    g_ws = reduced[:, part_rows:, :].reshape(-1)
    reduced = reduced[:, :part_rows, :].reshape(-1)
    sizes = [2 * D_MODEL, D_MODEL, D_MODEL, A_GROUPS * CHUNK, N_SUB * D_MODEL, N_SUB * D_MODEL]
    offs = [sum(sizes[:i]) for i in range(len(sizes) + 1)]
    g_b_in, g_vn_g, g_vn_b, g_bs, g_lng, g_lnb = [reduced[offs[i] : offs[i + 1]] for i in range(len(sizes))]
    loss = reduced[offs[-1]]

    ada_outs = ada_grad_adamw(cg, dmrecv, ada_w4, m_ada_w.reshape(ada_w4.shape), v_ada_w.reshape(ada_w4.shape), name="ada_grad_adamw")
    results["ada_w"] = [o.reshape(ada_w.shape) for o in ada_outs[:4]]
    ln_cols = D_MODEL // N_DEV
    my_ln = lambda gfull: lax.dynamic_slice_in_dim(gfull.reshape(N_SUB, N_DEV, ln_cols), me, 1, axis=1)
    small_params = [
        ("ada_b", ada_outs[4][:, 0, :], ada_b, m_ada_b, v_ada_b),
        ("ln_g", my_ln(g_lng), ln_g, m_ln_g, v_ln_g),
        ("ln_b", my_ln(g_lnb), ln_b, m_ln_b, v_ln_b),
        ("a_b_in", g_b_in, a_b_in, m_a_b_in, v_a_b_in),
        ("a_vn_g", g_vn_g, a_vn_g, m_a_vn_g, v_a_vn_g),
        ("a_vn_b", g_vn_b, a_vn_b, m_a_vn_b, v_a_vn_b),
        ("a_b_s", g_bs, a_b_s, m_a_b_s, v_a_b_s),
    ]
    small_outs = adamw_small(
        [[a.reshape(-1, w.shape[-1]) for a in (g, w, m, v)] for _, g, w, m, v in small_params], name="adamw_small"
    )
    for (wname, _, w, _, _), outs in zip(small_params, small_outs, strict=True):
        results[wname] = [o.reshape(w.shape) for o in outs]
    update("a_w_s", g_ws[None], a_w_s, m_a_w_s, v_a_w_s)
    (g_aout,) = rs_aout.wait([0], grad_x, name="scatter_wait_a_out")
    (g_in,) = rs_in.wait([0], grad_x, name="scatter_wait_a_in")
    update("a_w_in", g_in, a_w_in, m_a_w_in, v_a_w_in)
    update("a_w_out", g_aout, a_w_out, m_a_w_out, v_a_w_out)

    order = ["ada_w", "ada_b", "ln_g", "ln_b", "a_w_in", "a_b_in", "a_vn_g", "a_vn_b", "a_w_s", "a_b_s", "a_w_out", "b_w_qkv", "b_w_out", "mlp_w_up", "mlp_w_down"]
    return (loss, grad_x[None], *[results[n][0] for n in order], *[results[n][1] for n in order],
            *[results[n][2] for n in order], *[results[n][3] for n in order])
```
